```python
import math
import jax, jax.numpy as jnp
from jax import lax
import numpy as np

D_MODEL = 1024
BATCH = 8
SEQ = 4096
DEPTH = 1

ATTN_HEAD_DIM = 64
ATTN_WIDTH = D_MODEL // 2
ATTN_HEADS = ATTN_WIDTH // ATTN_HEAD_DIM
DILATED_PATTERNS = ((128, 1), (512, 4), (2048, 16))
ATTN_BLOCK = 128
HGRN_HEAD_DIM = 128
HGRN_WIDTH = D_MODEL - ATTN_WIDTH
HGRN_HEADS = HGRN_WIDTH // HGRN_HEAD_DIM
HGRN_CHUNK = 64
MIX_WIDTH = ATTN_WIDTH + HGRN_WIDTH
IN_PROJ_WIDTH = 3 * ATTN_WIDTH + 4 * HGRN_WIDTH
D_FF = 4 * D_MODEL
RMS_EPS = 1e-6

kernel_name = "hymba_dilated_attn_hgrn2_sqrelu"


def rmsnorm(x, gain):
    xf = x.astype(jnp.float32)
    y = xf * lax.rsqrt(jnp.mean(xf * xf, axis=-1, keepdims=True) + RMS_EPS)
    return (y * gain.astype(jnp.float32)).astype(x.dtype)


def alibi_slopes(n_heads):
    return jnp.exp2(-8.0 * jnp.arange(1, n_heads + 1, dtype=jnp.float32) / n_heads)


def dilated_branch(q, k, v, slopes, window, dilation):
    B, S, H, Dh = q.shape
    span = window // dilation
    L = S // dilation
    nb = -(-L // ATTN_BLOCK)
    Lp = nb * ATTN_BLOCK
    Bd = B * dilation

    def to_sub(t):
        t = t.reshape(B, L, dilation, H, Dh).transpose(0, 2, 1, 3, 4).reshape(Bd, L, H, Dh)
        t = jnp.pad(t, ((0, 0), (0, Lp - L), (0, 0), (0, 0)))
        return t.reshape(Bd, nb, ATTN_BLOCK, H, Dh)

    def with_prev(t):
        prev = jnp.pad(t[:, :-1], ((0, 0), (1, 0), (0, 0), (0, 0), (0, 0)))
        return jnp.concatenate([prev, t], axis=2)

    qb = to_sub(q)
    kk = with_prev(to_sub(k))
    vv = with_prev(to_sub(v))
    s = jnp.einsum('bnqhd,bnkhd->bnhqk', qb, kk).astype(jnp.float32) * (Dh ** -0.5)
    qi = jnp.arange(ATTN_BLOCK)[:, None] + ATTN_BLOCK
    kj = jnp.arange(2 * ATTN_BLOCK)[None, :]
    dist = qi - kj
    blk = jnp.arange(nb)[:, None, None]
    valid = (dist >= 0) & (dist <= span) & (blk * ATTN_BLOCK + kj - ATTN_BLOCK >= 0)
    bias = -slopes[:, None, None] * (dist * dilation).astype(jnp.float32)
    s = jnp.where(valid[None, :, None], s + bias[None, None], -jnp.inf)
    m = jnp.max(s, axis=-1, keepdims=True)
    p = jnp.exp(s - m)
    den = jnp.sum(p, axis=-1)
    o = jnp.einsum('bnhqk,bnkhd->bnqhd', p, vv.astype(jnp.float32))
    o = o / jnp.transpose(den, (0, 1, 3, 2))[..., None]
    lse = jnp.transpose(m[..., 0] + jnp.log(den), (0, 1, 3, 2))

    def from_sub(t):
        t = t.reshape((Bd, Lp) + t.shape[3:])[:, :L]
        t = t.reshape((B, dilation, L) + t.shape[2:])
        t = jnp.swapaxes(t, 1, 2)
        return t.reshape((B, S) + t.shape[3:])

    return from_sub(o), from_sub(lse)


def dilated_mixture_attention(q, k, v, slopes):
    outs, lses = [], []
    for window, dilation in DILATED_PATTERNS:
        o, lse = dilated_branch(q, k, v, slopes, window, dilation)
        outs.append(o)
        lses.append(lse)
    w = jax.nn.softmax(jnp.stack(lses, axis=0), axis=0)
    o = jnp.sum(w[..., None] * jnp.stack(outs, axis=0), axis=0)
    return o.astype(q.dtype)


def hgrn2_recurrence(q, f_pre, i, gate, lower_bound, out_gain):
    B, S, _ = q.shape
    H, D, C = HGRN_HEADS, HGRN_HEAD_DIM, HGRN_CHUNK
    shape = (B, S, H, D)
    lb = lower_bound.reshape(H, D)
    qf = jax.nn.silu(q.astype(jnp.float32)).reshape(shape)
    log_f = jnp.logaddexp(jnp.log(lb), jnp.log1p(-lb) + jax.nn.log_sigmoid(f_pre.astype(jnp.float32).reshape(shape)))
    kf = -jnp.expm1(log_f)
    vf = i.astype(jnp.float32).reshape(shape)
    nC = S // C

    def to_chunks(t):
        return t.reshape(B, nC, C, H, D).transpose(1, 0, 3, 2, 4)

    causal = jnp.tril(jnp.ones((C, C), dtype=bool))[:, :, None]

    def step(state, xs):
        qc, kc, vc, gc = xs
        b = jnp.cumsum(gc, axis=2)
        o_inter = jnp.einsum('bhtk,bhkv->bhtv', qc * jnp.exp(b), state)
        diff = b[:, :, :, None, :] - b[:, :, None, :, :]
        decay = jnp.exp(jnp.where(causal, diff, -jnp.inf))
        a = jnp.einsum('bhtk,bhsk,bhtsk->bhts', qc, kc, decay)
        o = o_inter + jnp.einsum('bhts,bhsv->bhtv', a, vc)
        b_last = b[:, :, -1:, :]
        new_state = jnp.exp(b_last[:, :, 0, :, None]) * state + jnp.einsum(
            'bhsk,bhsv->bhkv', kc * jnp.exp(b_last - b), vc)
        return new_state, o

    state0 = jnp.zeros((B, H, D, D), jnp.float32)
    _, o = lax.scan(step, state0, (to_chunks(qf), to_chunks(kf), to_chunks(vf), to_chunks(log_f)))
    o = o.transpose(1, 0, 3, 2, 4).reshape(shape)
    o = rmsnorm(o, out_gain.reshape(H, D)).reshape(B, S, H * D)
    return (o * jax.nn.silu(gate.astype(jnp.float32))).astype(q.dtype)


def _fwd_setup_inputs(seed: int = 0) -> dict:
    key = jax.random.key(seed)
    ks = jax.random.split(key, 12)
    f32 = jnp.float32

    def gain(k, shape):
        return 1.0 + 0.02 * jax.random.normal(k, shape, f32)

    return {
        "x": jax.random.normal(ks[0], (BATCH, SEQ, D_MODEL), f32),
        "mix_pre_norm": gain(ks[1], (DEPTH, D_MODEL)),
        "w_in": jax.random.normal(ks[2], (DEPTH, D_MODEL, IN_PROJ_WIDTH), f32) * D_MODEL ** -0.5,
        "attn_out_norm": gain(ks[3], (DEPTH, ATTN_WIDTH)),
        "hgrn_lb_logits": 0.5 * jax.random.normal(ks[4], (DEPTH + 1, HGRN_WIDTH), f32),
        "hgrn_out_norm": gain(ks[5], (DEPTH, HGRN_WIDTH)),
        "w_out": jax.random.normal(ks[6], (DEPTH, MIX_WIDTH, D_MODEL), f32) * MIX_WIDTH ** -0.5,
        "mix_post_norm": gain(ks[7], (DEPTH, D_MODEL)),
        "mlp_pre_norm": gain(ks[8], (DEPTH, D_MODEL)),
        "w_ff1": jax.random.normal(ks[9], (DEPTH, D_MODEL, D_FF), f32) * D_MODEL ** -0.5,
        "w_ff2": jax.random.normal(ks[10], (DEPTH, D_FF, D_MODEL), f32) * D_FF ** -0.5,
        "mlp_post_norm": gain(ks[11], (DEPTH, D_MODEL)),
    }


def _fwd_reference(x, mix_pre_norm, w_in, attn_out_norm, hgrn_lb_logits, hgrn_out_norm, w_out,
              mix_post_norm, mlp_pre_norm, w_ff1, w_ff2, mlp_post_norm):
    B, S, _ = x.shape
    slopes = alibi_slopes(ATTN_HEADS)
    lower_bounds = jnp.cumsum(jax.nn.softmax(hgrn_lb_logits.astype(jnp.float32), axis=0), axis=0)
    aw, hw = ATTN_WIDTH, HGRN_WIDTH
    splits = [aw, 2 * aw, 3 * aw, 3 * aw + hw, 3 * aw + 2 * hw, 3 * aw + 3 * hw]
    for layer in range(DEPTH):
        h = rmsnorm(x, mix_pre_norm[layer])
        proj = h @ w_in[layer]
        q_a, k_a, v_a, q_h, f_h, i_h, g_h = jnp.split(proj, splits, axis=-1)
        ahead = (B, S, ATTN_HEADS, ATTN_HEAD_DIM)
        attn = dilated_mixture_attention(q_a.reshape(ahead), k_a.reshape(ahead), v_a.reshape(ahead), slopes)
        attn = rmsnorm(attn.reshape(B, S, ATTN_WIDTH), attn_out_norm[layer])
        rec = hgrn2_recurrence(q_h, f_h, i_h, g_h, lower_bounds[layer], hgrn_out_norm[layer])
        mixed = jnp.concatenate([attn, rec], axis=-1) @ w_out[layer]
        x = x + rmsnorm(mixed, mix_post_norm[layer])
        h = rmsnorm(x, mlp_pre_norm[layer])
        ff = jnp.square(jax.nn.relu(h @ w_ff1[layer])) @ w_ff2[layer]
        x = x + rmsnorm(ff, mlp_post_norm[layer])
    return x


import jax as _jax
import jax.numpy as _jnp

TWIN_FORMAT = 'train_step'
FWD_PARAMS = ['x', 'mix_pre_norm', 'w_in', 'attn_out_norm', 'hgrn_lb_logits', 'hgrn_out_norm', 'w_out', 'mix_post_norm', 'mlp_pre_norm', 'w_ff1', 'w_ff2', 'mlp_post_norm']
TWIN_WEIGHTS = ['mix_pre_norm', 'w_in', 'attn_out_norm', 'hgrn_lb_logits', 'hgrn_out_norm', 'w_out', 'mix_post_norm', 'mlp_pre_norm', 'w_ff1', 'w_ff2', 'mlp_post_norm']
TWIN_DIFF_INPUT = 'x'
TWIN_INPUTS = ['x', 'mix_pre_norm', 'w_in', 'attn_out_norm', 'hgrn_lb_logits', 'hgrn_out_norm', 'w_out', 'mix_post_norm', 'mlp_pre_norm', 'w_ff1', 'w_ff2', 'mlp_post_norm', 'loss_target', 'm_mix_pre_norm', 'm_w_in', 'm_attn_out_norm', 'm_hgrn_lb_logits', 'm_hgrn_out_norm', 'm_w_out', 'm_mix_post_norm', 'm_mlp_pre_norm', 'm_w_ff1', 'm_w_ff2', 'm_mlp_post_norm', 'v_mix_pre_norm', 'v_w_in', 'v_attn_out_norm', 'v_hgrn_lb_logits', 'v_hgrn_out_norm', 'v_w_out', 'v_mix_post_norm', 'v_mlp_pre_norm', 'v_w_ff1', 'v_w_ff2', 'v_mlp_post_norm']
TWIN_OUTPUTS = ['loss', 'grad_x', 'grad_mix_pre_norm', 'grad_w_in', 'grad_attn_out_norm', 'grad_hgrn_lb_logits', 'grad_hgrn_out_norm', 'grad_w_out', 'grad_mix_post_norm', 'grad_mlp_pre_norm', 'grad_w_ff1', 'grad_w_ff2', 'grad_mlp_post_norm', 'delta_mix_pre_norm', 'delta_w_in', 'delta_attn_out_norm', 'delta_hgrn_lb_logits', 'delta_hgrn_out_norm', 'delta_w_out', 'delta_mix_post_norm', 'delta_mlp_pre_norm', 'delta_w_ff1', 'delta_w_ff2', 'delta_mlp_post_norm', 'new_m_mix_pre_norm', 'new_m_w_in', 'new_m_attn_out_norm', 'new_m_hgrn_lb_logits', 'new_m_hgrn_out_norm', 'new_m_w_out', 'new_m_mix_post_norm', 'new_m_mlp_pre_norm', 'new_m_w_ff1', 'new_m_w_ff2', 'new_m_mlp_post_norm', 'new_v_mix_pre_norm', 'new_v_w_in', 'new_v_attn_out_norm', 'new_v_hgrn_lb_logits', 'new_v_hgrn_out_norm', 'new_v_w_out', 'new_v_mix_post_norm', 'new_v_mlp_pre_norm', 'new_v_w_ff1', 'new_v_w_ff2', 'new_v_mlp_post_norm']
TWIN_LEAF_KINDS = {'loss': 'loss', 'grad_x': 'grad_x', 'grad_mix_pre_norm': 'grad_w', 'grad_w_in': 'grad_w', 'grad_attn_out_norm': 'grad_w', 'grad_hgrn_lb_logits': 'grad_w', 'grad_hgrn_out_norm': 'grad_w', 'grad_w_out': 'grad_w', 'grad_mix_post_norm': 'grad_w', 'grad_mlp_pre_norm': 'grad_w', 'grad_w_ff1': 'grad_w', 'grad_w_ff2': 'grad_w', 'grad_mlp_post_norm': 'grad_w', 'delta_mix_pre_norm': 'delta_w', 'delta_w_in': 'delta_w', 'delta_attn_out_norm': 'delta_w', 'delta_hgrn_lb_logits': 'delta_w', 'delta_hgrn_out_norm': 'delta_w', 'delta_w_out': 'delta_w', 'delta_mix_post_norm': 'delta_w', 'delta_mlp_pre_norm': 'delta_w', 'delta_w_ff1': 'delta_w', 'delta_w_ff2': 'delta_w', 'delta_mlp_post_norm': 'delta_w', 'new_m_mix_pre_norm': 'new_m', 'new_m_w_in': 'new_m', 'new_m_attn_out_norm': 'new_m', 'new_m_hgrn_lb_logits': 'new_m', 'new_m_hgrn_out_norm': 'new_m', 'new_m_w_out': 'new_m', 'new_m_mix_post_norm': 'new_m', 'new_m_mlp_pre_norm': 'new_m', 'new_m_w_ff1': 'new_m', 'new_m_w_ff2': 'new_m', 'new_m_mlp_post_norm': 'new_m', 'new_v_mix_pre_norm': 'new_v', 'new_v_w_in': 'new_v', 'new_v_attn_out_norm': 'new_v', 'new_v_hgrn_lb_logits': 'new_v', 'new_v_hgrn_out_norm': 'new_v', 'new_v_w_out': 'new_v', 'new_v_mix_post_norm': 'new_v', 'new_v_mlp_pre_norm': 'new_v', 'new_v_w_ff1': 'new_v', 'new_v_w_ff2': 'new_v', 'new_v_mlp_post_norm': 'new_v'}


def _forward(args):
    return _fwd_reference(*[args[k] for k in FWD_PARAMS])


def _output_shape():
    out = _jax.eval_shape(lambda: _forward(_fwd_setup_inputs(0)))
    return out.shape, out.dtype

N_MICROBATCH = 1
ADAM_LR = 0.001
ADAM_B1 = 0.9
ADAM_B2 = 0.999
ADAM_EPS = 1e-08
ADAM_WD = 0.01
ADAM_STEP = 10
PER_EXAMPLE_BATCH_AXIS = {'x': 0, 'loss_target': 0}
SHARED_INPUTS = []
_WEIGHT_DTYPES = {'mix_pre_norm': _jnp.float32, 'w_in': _jnp.float32, 'attn_out_norm': _jnp.float32, 'hgrn_lb_logits': _jnp.float32, 'hgrn_out_norm': _jnp.float32, 'w_out': _jnp.float32, 'mix_post_norm': _jnp.float32, 'mlp_pre_norm': _jnp.float32, 'w_ff1': _jnp.float32, 'w_ff2': _jnp.float32, 'mlp_post_norm': _jnp.float32}
MOMENT_SCALE = {'mix_pre_norm': 8.214749e-01, 'w_in': 3.903404e-01, 'attn_out_norm': 1.476125e+00, 'hgrn_lb_logits': 2.449489e-02, 'hgrn_out_norm': 4.466819e-01, 'w_out': 9.089856e-01, 'mix_post_norm': 3.198259e+01, 'mlp_pre_norm': 8.836302e-01, 'w_ff1': 4.485693e-01, 'w_ff2': 9.126706e-01, 'mlp_post_norm': 3.283096e+01}


def _to_microbatches(a, axis):
    t = _jnp.moveaxis(a, axis, 0)
    t = t.reshape((N_MICROBATCH, t.shape[0] // N_MICROBATCH) + t.shape[1:])
    return _jnp.moveaxis(t, 1, axis + 1)


def setup_inputs(seed: int = 0) -> dict:
    inp = _fwd_setup_inputs(seed)
    key = _jax.random.fold_in(_jax.random.key(seed), 7919)
    shape, _ = _output_shape()
    out = dict(inp)
    out["loss_target"] = _jax.random.normal(_jax.random.fold_in(key, 0), shape, _jnp.float32)
    for i, name in enumerate(TWIN_WEIGHTS):
        w = inp[name].astype(_jnp.float32)
        if MOMENT_SCALE is None:
            s = _jnp.sqrt(_jnp.mean(_jnp.square(w)) + 1e-30)
        else:
            s = MOMENT_SCALE[name]
        km, kv = _jax.random.split(_jax.random.fold_in(key, i + 1))
        out[name] = w
        out["m_" + name] = s * _jax.random.normal(km, w.shape, _jnp.float32)
        out["v_" + name] = (s * s) * _jax.random.uniform(kv, w.shape, _jnp.float32, 0.5, 1.5)
    if N_MICROBATCH > 1:
        for name, axis in PER_EXAMPLE_BATCH_AXIS.items():
            out[name] = _to_microbatches(out[name], axis)
    return {'x': out['x'], 'mix_pre_norm': out['mix_pre_norm'], 'w_in': out['w_in'], 'attn_out_norm': out['attn_out_norm'], 'hgrn_lb_logits': out['hgrn_lb_logits'], 'hgrn_out_norm': out['hgrn_out_norm'], 'w_out': out['w_out'], 'mix_post_norm': out['mix_post_norm'], 'mlp_pre_norm': out['mlp_pre_norm'], 'w_ff1': out['w_ff1'], 'w_ff2': out['w_ff2'], 'mlp_post_norm': out['mlp_post_norm'], 'loss_target': out['loss_target'], 'm_mix_pre_norm': out['m_mix_pre_norm'], 'm_w_in': out['m_w_in'], 'm_attn_out_norm': out['m_attn_out_norm'], 'm_hgrn_lb_logits': out['m_hgrn_lb_logits'], 'm_hgrn_out_norm': out['m_hgrn_out_norm'], 'm_w_out': out['m_w_out'], 'm_mix_post_norm': out['m_mix_post_norm'], 'm_mlp_pre_norm': out['m_mlp_pre_norm'], 'm_w_ff1': out['m_w_ff1'], 'm_w_ff2': out['m_w_ff2'], 'm_mlp_post_norm': out['m_mlp_post_norm'], 'v_mix_pre_norm': out['v_mix_pre_norm'], 'v_w_in': out['v_w_in'], 'v_attn_out_norm': out['v_attn_out_norm'], 'v_hgrn_lb_logits': out['v_hgrn_lb_logits'], 'v_hgrn_out_norm': out['v_hgrn_out_norm'], 'v_w_out': out['v_w_out'], 'v_mix_post_norm': out['v_mix_post_norm'], 'v_mlp_pre_norm': out['v_mlp_pre_norm'], 'v_w_ff1': out['v_w_ff1'], 'v_w_ff2': out['v_w_ff2'], 'v_mlp_post_norm': out['v_mlp_post_norm']}


def _loss(weights, diff, rest, loss_target):
    with _jax.named_scope("forward"):
        args = {**rest, TWIN_DIFF_INPUT: diff, **{k: w.astype(_WEIGHT_DTYPES[k]) for k, w in weights.items()}}
        y = _forward(args)
    with _jax.named_scope("loss_head"):
        err = _jnp.square(y.astype(_jnp.float32) - loss_target)
        return 0.5 * _jnp.sum(_jnp.mean(err, axis=-1)) if err.ndim else 0.5 * err


def _adamw(w, g, m, v):
    m = ADAM_B1 * m + (1.0 - ADAM_B1) * g
    v = ADAM_B2 * v + (1.0 - ADAM_B2) * _jnp.square(g)
    m_hat = m / (1.0 - ADAM_B1 ** ADAM_STEP)
    v_hat = v / (1.0 - ADAM_B2 ** ADAM_STEP)
    delta = -ADAM_LR * (m_hat / (_jnp.sqrt(v_hat) + ADAM_EPS) + ADAM_WD * w)
    return delta, m, v


def reference(x, mix_pre_norm, w_in, attn_out_norm, hgrn_lb_logits, hgrn_out_norm, w_out, mix_post_norm, mlp_pre_norm, w_ff1, w_ff2, mlp_post_norm, loss_target, m_mix_pre_norm, m_w_in, m_attn_out_norm, m_hgrn_lb_logits, m_hgrn_out_norm, m_w_out, m_mix_post_norm, m_mlp_pre_norm, m_w_ff1, m_w_ff2, m_mlp_post_norm, v_mix_pre_norm, v_w_in, v_attn_out_norm, v_hgrn_lb_logits, v_hgrn_out_norm, v_w_out, v_mix_post_norm, v_mlp_pre_norm, v_w_ff1, v_w_ff2, v_mlp_post_norm):
    given = dict(x=x, mix_pre_norm=mix_pre_norm, w_in=w_in, attn_out_norm=attn_out_norm, hgrn_lb_logits=hgrn_lb_logits, hgrn_out_norm=hgrn_out_norm, w_out=w_out, mix_post_norm=mix_post_norm, mlp_pre_norm=mlp_pre_norm, w_ff1=w_ff1, w_ff2=w_ff2, mlp_post_norm=mlp_post_norm, loss_target=loss_target, m_mix_pre_norm=m_mix_pre_norm, m_w_in=m_w_in, m_attn_out_norm=m_attn_out_norm, m_hgrn_lb_logits=m_hgrn_lb_logits, m_hgrn_out_norm=m_hgrn_out_norm, m_w_out=m_w_out, m_mix_post_norm=m_mix_post_norm, m_mlp_pre_norm=m_mlp_pre_norm, m_w_ff1=m_w_ff1, m_w_ff2=m_w_ff2, m_mlp_post_norm=m_mlp_post_norm, v_mix_pre_norm=v_mix_pre_norm, v_w_in=v_w_in, v_attn_out_norm=v_attn_out_norm, v_hgrn_lb_logits=v_hgrn_lb_logits, v_hgrn_out_norm=v_hgrn_out_norm, v_w_out=v_w_out, v_mix_post_norm=v_mix_post_norm, v_mlp_pre_norm=v_mlp_pre_norm, v_w_ff1=v_w_ff1, v_w_ff2=v_w_ff2, v_mlp_post_norm=v_mlp_post_norm)
    weights = {n: given[n] for n in TWIN_WEIGHTS}
    shared = {n: given[n] for n in SHARED_INPUTS}
    per_example = {n: given[n] for n in ['x']}
    grad_fn = _jax.value_and_grad(_loss, argnums=(0, 1))

    def one_microbatch(ex, loss_target):
        ex = dict(ex)
        diff = ex.pop(TWIN_DIFF_INPUT)
        return grad_fn(weights, diff, {**shared, **ex}, loss_target)

    if N_MICROBATCH == 1:
        loss, (grad_w, grad_x) = one_microbatch(per_example, given["loss_target"])
    else:
        def body(carry, xs):
            loss_sum, grad_sum = carry
            l_k, (gw_k, gx_k) = one_microbatch(xs[0], xs[1])
            with _jax.named_scope("update"):
                return (loss_sum + l_k, _jax.tree.map(_jnp.add, grad_sum, gw_k)), gx_k

        init = (_jnp.zeros((), _jnp.float32), _jax.tree.map(_jnp.zeros_like, weights))
        (loss, grad_w), grad_x = _jax.lax.scan(body, init, (per_example, given["loss_target"]))
    with _jax.named_scope("update"):
        delta_w, new_m, new_v = {}, {}, {}
        for n in TWIN_WEIGHTS:
            delta_w[n], new_m[n], new_v[n] = _adamw(weights[n], grad_w[n], given["m_" + n], given["v_" + n])
    return (loss, grad_x, *[grad_w[n] for n in TWIN_WEIGHTS], *[delta_w[n] for n in TWIN_WEIGHTS],
            *[new_m[n] for n in TWIN_WEIGHTS], *[new_v[n] for n in TWIN_WEIGHTS])
```

```python
import functools
import math

import jax
import jax.numpy as jnp
from jax import lax
from jax.experimental import pallas as pl
from jax.experimental.pallas import tpu as pltpu

F32 = jnp.float32
BF16 = jnp.bfloat16

D_MODEL = 1024
SEQ = 4096
ATTN_WIDTH = 512
ATTN_HEAD_DIM = 64
ATTN_HEADS = 8
ATTN_BLOCK = 128
DILATIONS = (1, 4, 16)
HGRN_WIDTH = 512
HGRN_HEADS = 4
HGRN_HEAD_DIM = 128
HGRN_CHUNK = 64
IN_PROJ_WIDTH = 3584
D_FF = 4096
RMS_EPS = 1e-6
N_DEV = 8
ADAM_LR = 0.001
ADAM_B1 = 0.9
ADAM_B2 = 0.999
ADAM_EPS = 1e-08
ADAM_WD = 0.01
ADAM_STEP = 10

TOKEN_TILE = 256
VMEM_LIMIT = 56 * 1024 * 1024
NEG_BIG = -1e30
MESH = pl.DeviceIdType.MESH


def _params(**kw):
    return pltpu.CompilerParams(vmem_limit_bytes=VMEM_LIMIT, **kw)


def _vmem_spec():
    return pl.BlockSpec(memory_space=pltpu.VMEM)


def _dot(a, b):
    return jnp.dot(a, b, preferred_element_type=F32)


def _dot_nt(a, b):
    return lax.dot_general(a, b, (((1,), (1,)), ((), ())), preferred_element_type=F32)


def _dot_tn(a, b):
    return lax.dot_general(a, b, (((0,), (0,)), ((), ())), preferred_element_type=F32)


def _sigmoid(x):
    return 1.0 / (1.0 + jnp.exp(-x))


def _rms_fwd(x, gain, width):
    r = lax.rsqrt(jnp.sum(x * x, axis=-1, keepdims=True) * (1.0 / width) + RMS_EPS)
    return x * r * gain


def _rms_bwd(dy, x, gain, width):
    r = lax.rsqrt(jnp.sum(x * x, axis=-1, keepdims=True) * (1.0 / width) + RMS_EPS)
    xhat = x * r
    dxhat = dy * gain
    dx = r * (dxhat - xhat * (jnp.sum(dxhat * xhat, axis=-1, keepdims=True) * (1.0 / width)))
    return dx, dy * xhat


def _split3(x):
    hi = x.astype(BF16)
    r1 = x - hi.astype(F32)
    mid = r1.astype(BF16)
    lo = (r1 - mid.astype(F32)).astype(BF16)
    return hi, mid, lo


def _tri_sum(tri_bf16, x):
    hi, mid, lo = _split3(x)
    return _dot(tri_bf16, hi) + _dot(tri_bf16, mid) + _dot(tri_bf16, lo)


def in_proj_fwd(x, g1, w_in_b):
    s = x.shape[0]
    tm = TOKEN_TILE

    def body(x_ref, g_ref, w_ref, proj_ref, h_ref):
        h = _rms_fwd(x_ref[...], g_ref[...], D_MODEL).astype(BF16)
        h_ref[...] = h
        proj_ref[...] = _dot(h, w_ref[...])

    return pl.pallas_call(
        body,
        name="in_proj_fwd",
        grid=(s // tm,),
        in_specs=[
            pl.BlockSpec((tm, D_MODEL), lambda i: (i, 0)),
            pl.BlockSpec((1, D_MODEL), lambda i: (0, 0)),
            _vmem_spec(),
        ],
        out_specs=[
            pl.BlockSpec((tm, IN_PROJ_WIDTH), lambda i: (i, 0)),
            pl.BlockSpec((tm, D_MODEL), lambda i: (i, 0)),
        ],
        out_shape=[jax.ShapeDtypeStruct((s, IN_PROJ_WIDTH), F32), jax.ShapeDtypeStruct((s, D_MODEL), BF16)],
        compiler_params=_params(dimension_semantics=("arbitrary",)),
    )(x, g1, w_in_b)


def _attn_scores(qm, kcat, head, dilation, first_block):
    s = _dot_nt(qm, kcat) * (ATTN_HEAD_DIM ** -0.5)
    qi = lax.broadcasted_iota(jnp.int32, (ATTN_BLOCK, 2 * ATTN_BLOCK), 0)
    kj = lax.broadcasted_iota(jnp.int32, (ATTN_BLOCK, 2 * ATTN_BLOCK), 1)
    dist = qi + ATTN_BLOCK - kj
    valid = (dist >= 0) & (dist <= ATTN_BLOCK) & ((kj >= ATTN_BLOCK) | jnp.logical_not(first_block))
    slope = 2.0 ** (-8.0 * (head + 1) / ATTN_HEADS)
    bias = dist.astype(F32) * (-slope * dilation)
    return jnp.where(valid, s + bias, NEG_BIG)


def _lane_half(shape, sub):
    lane = lax.broadcasted_iota(jnp.int32, shape, 1)
    return (lane < ATTN_HEAD_DIM) if sub == 0 else (lane >= ATTN_HEAD_DIM)


def attn_fwd(proj, dilation):
    s = proj.shape[0]
    d = dilation
    length = s // d
    nb = length // ATTN_BLOCK
    nq = IN_PROJ_WIDTH // ATTN_WIDTH
    pv = proj.reshape(length, d * IN_PROJ_WIDTH)

    def body(q_ref, kc_ref, kp_ref, vc_ref, vp_ref, o_ref, lse_ref):
        first = pl.program_id(1) == 0
        for pair in range(ATTN_HEADS // 2):
            lanes = slice(pair * 128, (pair + 1) * 128)
            q2 = q_ref[:, lanes]
            kcat = jnp.concatenate([kp_ref[:, lanes], kc_ref[:, lanes]], axis=0).astype(BF16)
            vcat = jnp.concatenate([vp_ref[:, lanes], vc_ref[:, lanes]], axis=0).astype(BF16)
            o_pair = jnp.zeros((ATTN_BLOCK, 128), F32)
            lse_pair = jnp.zeros((ATTN_BLOCK, 128), F32)
            for sub in range(2):
                keep = _lane_half((ATTN_BLOCK, 128), sub)
                qm = jnp.where(keep, q2, 0.0).astype(BF16)
                sc = _attn_scores(qm, kcat, 2 * pair + sub, d, first)
                m = jnp.max(sc, axis=-1, keepdims=True)
                p = jnp.exp(sc - m)
                den = jnp.sum(p, axis=-1, keepdims=True)
                o = _dot(p.astype(BF16), vcat) / den
                o_pair = jnp.where(keep, o, o_pair)
                lse_pair = jnp.where(keep, m + jnp.log(den), lse_pair)
            o_ref[:, lanes] = o_pair
            lse_ref[:, lanes] = lse_pair

    blk = (ATTN_BLOCK, ATTN_WIDTH)
    out = pl.pallas_call(
        body,
        name=f"attn_fwd_d{d}",
        grid=(d, nb),
        in_specs=[
            pl.BlockSpec(blk, lambda r, n: (n, r * nq)),
            pl.BlockSpec(blk, lambda r, n: (n, r * nq + 1)),
            pl.BlockSpec(blk, lambda r, n: (jnp.maximum(n - 1, 0), r * nq + 1)),
            pl.BlockSpec(blk, lambda r, n: (n, r * nq + 2)),
            pl.BlockSpec(blk, lambda r, n: (jnp.maximum(n - 1, 0), r * nq + 2)),
        ],
        out_specs=[pl.BlockSpec(blk, lambda r, n: (n, r)), pl.BlockSpec(blk, lambda r, n: (n, r))],
        out_shape=[jax.ShapeDtypeStruct((length, d * ATTN_WIDTH), F32)] * 2,
        compiler_params=_params(dimension_semantics=("arbitrary", "arbitrary")),
    )(pv, pv, pv, pv, pv)
    return out[0].reshape(s, ATTN_WIDTH), out[1].reshape(s, ATTN_WIDTH)


def attn_bwd(proj, d_out, lse, delta, dilation):
    s = proj.shape[0]
    d = dilation
    length = s // d
    nb = length // ATTN_BLOCK
    nq = IN_PROJ_WIDTH // ATTN_WIDTH
    pv = proj.reshape(length, d * IN_PROJ_WIDTH)
    view = lambda a: a.reshape(length, d * ATTN_WIDTH)

    def body(q_ref, kc_ref, kp_ref, vc_ref, vp_ref, do_ref, lse_ref, dl_ref, dq_ref, dk_ref, dv_ref, ck_ref, cv_ref):
        n = pl.program_id(1)

        @pl.when(n == 0)
        def _():
            ck_ref[...] = jnp.zeros_like(ck_ref)
            cv_ref[...] = jnp.zeros_like(cv_ref)

        @pl.when(n < nb)
        def _():
            first = n == 0
            for pair in range(ATTN_HEADS // 2):
                lanes = slice(pair * 128, (pair + 1) * 128)
                q2 = q_ref[:, lanes]
                do2 = do_ref[:, lanes]
                kcat = jnp.concatenate([kp_ref[:, lanes], kc_ref[:, lanes]], axis=0).astype(BF16)
                vcat = jnp.concatenate([vp_ref[:, lanes], vc_ref[:, lanes]], axis=0).astype(BF16)
                dq_pair = jnp.zeros((ATTN_BLOCK, 128), F32)
                dk_cat = jnp.zeros((2 * ATTN_BLOCK, 128), F32)
                dv_cat = jnp.zeros((2 * ATTN_BLOCK, 128), F32)
                for sub in range(2):
                    keep = _lane_half((ATTN_BLOCK, 128), sub)
                    col = pair * 128 + sub * ATTN_HEAD_DIM
                    qm = jnp.where(keep, q2, 0.0).astype(BF16)
                    dom = jnp.where(keep, do2, 0.0).astype(BF16)
                    sc = _attn_scores(qm, kcat, 2 * pair + sub, d, first)
                    p = jnp.exp(sc - lse_ref[:, col:col + 1])
                    dp = _dot_nt(dom, vcat)
                    ds = (p * (dp - dl_ref[:, col:col + 1]) * (ATTN_HEAD_DIM ** -0.5)).astype(BF16)
                    dq_pair = jnp.where(keep, _dot(ds, kcat), dq_pair)
                    dk_cat = dk_cat + _dot_tn(ds, qm)
                    dv_cat = dv_cat + _dot_tn(p.astype(BF16), dom)
                dq_ref[:, lanes] = dq_pair
                dk_ref[:, lanes] = ck_ref[:, lanes] + dk_cat[:ATTN_BLOCK]
                dv_ref[:, lanes] = cv_ref[:, lanes] + dv_cat[:ATTN_BLOCK]
                ck_ref[:, lanes] = dk_cat[ATTN_BLOCK:]
                cv_ref[:, lanes] = dv_cat[ATTN_BLOCK:]

        @pl.when(n == nb)
        def _():
            dk_ref[...] = ck_ref[...]
            dv_ref[...] = cv_ref[...]

    blk = (ATTN_BLOCK, ATTN_WIDTH)
    cur = lambda n: jnp.minimum(n, nb - 1)
    prev = lambda n: jnp.maximum(jnp.minimum(n, nb - 1) - 1, 0)
    out = pl.pallas_call(
        body,
        name=f"attn_bwd_d{d}",
        grid=(d, nb + 1),
        in_specs=[
            pl.BlockSpec(blk, lambda r, n: (cur(n), r * nq)),
            pl.BlockSpec(blk, lambda r, n: (cur(n), r * nq + 1)),
            pl.BlockSpec(blk, lambda r, n: (prev(n), r * nq + 1)),
            pl.BlockSpec(blk, lambda r, n: (cur(n), r * nq + 2)),
            pl.BlockSpec(blk, lambda r, n: (prev(n), r * nq + 2)),
            pl.BlockSpec(blk, lambda r, n: (cur(n), r)),
            pl.BlockSpec(blk, lambda r, n: (cur(n), r)),
            pl.BlockSpec(blk, lambda r, n: (cur(n), r)),
        ],
        out_specs=[
            pl.BlockSpec(blk, lambda r, n: (cur(n), r)),
            pl.BlockSpec(blk, lambda r, n: (jnp.maximum(n - 1, 0), r)),
            pl.BlockSpec(blk, lambda r, n: (jnp.maximum(n - 1, 0), r)),
        ],
        out_shape=[jax.ShapeDtypeStruct((length, d * ATTN_WIDTH), F32)] * 3,
        scratch_shapes=[pltpu.VMEM(blk, F32), pltpu.VMEM(blk, F32)],
        compiler_params=_params(dimension_semantics=("arbitrary", "arbitrary")),
    )(pv, pv, pv, pv, pv, view(d_out), view(lse), view(delta))
    return tuple(o.reshape(s, ATTN_WIDTH) for o in out)


def _lower_bound(logits):
    return _sigmoid(logits[0:1, :] - logits[1:2, :])


def _hgrn_gates(q, fp, lb):
    sq = _sigmoid(q)
    qf = q * sq
    sig = _sigmoid(fp)
    f = lb + (1.0 - lb) * sig
    kf = (1.0 - lb) * _sigmoid(-fp)
    return sq, qf, sig, f, kf


def _tril_bf16(n, upper=False):
    r = lax.broadcasted_iota(jnp.int32, (n, n), 0)
    c = lax.broadcasted_iota(jnp.int32, (n, n), 1)
    keep = (c >= r) if upper else (c <= r)
    return jnp.where(keep, 1.0, 0.0).astype(BF16)


def hgrn_fwd(proj, lb):
    s = proj.shape[0]
    c_len, nh, hd = HGRN_CHUNK, HGRN_HEADS, HGRN_HEAD_DIM
    n_chunks = s // c_len
    col0 = (3 * ATTN_WIDTH) // HGRN_WIDTH

    def body(q_ref, f_ref, i_ref, lb_ref, o_ref, st_out_ref, a_out_ref, st_ref, b_ref, qf_ref, kf_ref, a_ref):
        @pl.when(pl.program_id(0) == 0)
        def _():
            st_ref[...] = jnp.zeros_like(st_ref)

        lbv = _lower_bound(lb_ref[...])
        _, qf, _, f, kf = _hgrn_gates(q_ref[...], f_ref[...], lbv)
        b = _tri_sum(_tril_bf16(c_len), jnp.log(f))
        b_ref[...] = b
        qf_ref[...] = qf
        kf_ref[...] = kf
        a_ref[...] = jnp.zeros_like(a_ref)
        t_idx = lax.broadcasted_iota(jnp.int32, (c_len, nh * hd), 0)
        lane = lax.broadcasted_iota(jnp.int32, (c_len, hd), 1)

        def column(j, carry):
            bj = b_ref[pl.ds(j, 1), :]
            kj = kf_ref[pl.ds(j, 1), :]
            e = jnp.exp(jnp.where(t_idx >= j, b_ref[...] - bj, NEG_BIG))
            prod = qf_ref[...] * kj * e
            for h in range(nh):
                col = jnp.sum(prod[:, h * hd:(h + 1) * hd], axis=-1, keepdims=True)
                a_ref[h] = jnp.where(lane == j, col, a_ref[h])
            return carry

        lax.fori_loop(0, c_len, column, 0)

        b_last = b[c_len - 1:c_len, :]
        qb = (qf * jnp.exp(b)).astype(BF16)
        kb2 = (kf * jnp.exp(b_last - b)).astype(BF16)
        vf = i_ref[...].astype(BF16)
        for h in range(nh):
            hs = slice(h * hd, (h + 1) * hd)
            st = st_ref[h]
            st_out_ref[0, h] = st
            a_h = a_ref[h]
            a_out_ref[:, hs] = a_h
            o_ref[:, hs] = _dot_nt(qb[:, hs], st.astype(BF16)) + _dot(a_h[:, :c_len].astype(BF16), vf[:, hs])
            st_ref[h] = st * jnp.exp(b_last[:, hs]) + _dot_tn(vf[:, hs], kb2[:, hs])

    blk = (c_len, HGRN_WIDTH)
    return pl.pallas_call(
        body,
        name="hgrn_fwd",
        grid=(n_chunks,),
        in_specs=[
            pl.BlockSpec(blk, lambda c: (c, col0)),
            pl.BlockSpec(blk, lambda c: (c, col0 + 1)),
            pl.BlockSpec(blk, lambda c: (c, col0 + 2)),
            pl.BlockSpec((2, HGRN_WIDTH), lambda c: (0, 0)),
        ],
        out_specs=[
            pl.BlockSpec(blk, lambda c: (c, 0)),
            pl.BlockSpec((1, nh, hd, hd), lambda c: (c, 0, 0, 0)),
            pl.BlockSpec(blk, lambda c: (c, 0)),
        ],
        out_shape=[
            jax.ShapeDtypeStruct((s, HGRN_WIDTH), F32),
            jax.ShapeDtypeStruct((n_chunks, nh, hd, hd), F32),
            jax.ShapeDtypeStruct((s, nh * hd), F32),
        ],
        scratch_shapes=[
            pltpu.VMEM((nh, hd, hd), F32),
            pltpu.VMEM(blk, F32),
            pltpu.VMEM(blk, F32),
            pltpu.VMEM(blk, F32),
            pltpu.VMEM((nh, c_len, hd), F32),
        ],
        compiler_params=_params(dimension_semantics=("arbitrary",)),
    )(proj, proj, proj, lb)


def hgrn_bwd(proj, lb, d_o, states, a_mat):
    s = proj.shape[0]
    c_len, nh, hd = HGRN_CHUNK, HGRN_HEADS, HGRN_HEAD_DIM
    n_chunks = s // c_len
    col0 = (3 * ATTN_WIDTH) // HGRN_WIDTH
    last = n_chunks - 1

    def body(q_ref, f_ref, i_ref, lb_ref, do_ref, st_in_ref, a_in_ref, dq_ref, df_ref, di_ref, dlb_ref,
             dst_ref, b_ref, qf_ref, kf_ref, da_ref, dqi_ref, dki_ref):
        @pl.when(pl.program_id(0) == 0)
        def _():
            dst_ref[...] = jnp.zeros_like(dst_ref)
            dlb_ref[...] = jnp.zeros_like(dlb_ref)

        lbv = _lower_bound(lb_ref[...])
        q = q_ref[...]
        sq, qf, sig, f, kf = _hgrn_gates(q, f_ref[...], lbv)
        b = _tri_sum(_tril_bf16(c_len), jnp.log(f))
        b_ref[...] = b
        qf_ref[...] = qf
        kf_ref[...] = kf
        b_last = b[c_len - 1:c_len, :]
        eb = jnp.exp(b)
        ebl = jnp.exp(b_last - b)
        qb = qf * eb
        kb2 = kf * ebl
        vf = i_ref[...]
        d_o = do_ref[...]
        qb_b, kb2_b, vf_b, do_b = qb.astype(BF16), kb2.astype(BF16), vf.astype(BF16), d_o.astype(BF16)
        tq = lax.broadcasted_iota(jnp.int32, (c_len, hd), 0)
        lane = lax.broadcasted_iota(jnp.int32, (c_len, hd), 1)

        dqb_parts, dvf_parts, dkb2_parts, dbl_parts = [], [], [], []
        for h in range(nh):
            hs = slice(h * hd, (h + 1) * hd)
            st = st_in_ref[0, h]
            dst = dst_ref[h]
            st_b, dst_b = st.astype(BF16), dst.astype(BF16)
            a_h = a_in_ref[:, hs][:, :c_len].astype(BF16)
            dqb_parts.append(_dot(do_b[:, hs], st_b))
            dvf_parts.append(_dot_tn(a_h, do_b[:, hs]) + _dot_nt(kb2_b[:, hs], dst_b))
            dkb2_parts.append(_dot(vf_b[:, hs], dst_b))
            da = _dot_nt(do_b[:, hs], vf_b[:, hs])
            da = jnp.concatenate([da, jnp.zeros((c_len, hd - c_len), F32)], axis=1)
            da_ref[h] = jnp.where(tq >= lane, da, 0.0)
            dbl_parts.append(jnp.sum(dst * st, axis=0, keepdims=True) * jnp.exp(b_last[:, hs]))
            dst_ref[h] = dst * jnp.exp(b_last[:, hs]) + _dot_tn(do_b[:, hs], qb_b[:, hs])
        dqb = jnp.concatenate(dqb_parts, axis=1)
        dvf = jnp.concatenate(dvf_parts, axis=1)
        dkb2 = jnp.concatenate(dkb2_parts, axis=1)
        dbl = jnp.concatenate(dbl_parts, axis=1) + jnp.sum(dkb2 * kb2, axis=0, keepdims=True)

        dqi_ref[...] = jnp.zeros_like(dqi_ref)
        t_idx = lax.broadcasted_iota(jnp.int32, (c_len, nh * hd), 0)

        def column(j, carry):
            bj = b_ref[pl.ds(j, 1), :]
            kj = kf_ref[pl.ds(j, 1), :]
            e = jnp.exp(jnp.where(t_idx >= j, b_ref[...] - bj, NEG_BIG))
            cols = [jnp.sum(jnp.where(lane == j, da_ref[h], 0.0), axis=-1, keepdims=True) for h in range(nh)]
            w = e * jnp.concatenate([jnp.broadcast_to(cc, (c_len, hd)) for cc in cols], axis=1)
            dqi_ref[...] += w * kj
            dki_ref[pl.ds(j, 1), :] = jnp.sum(w * qf_ref[...], axis=0, keepdims=True)
            return carry

        lax.fori_loop(0, c_len, column, 0)
        dq_intra = dqi_ref[...]
        dk_intra = dki_ref[...]

        db = dqb * qb + qf * dq_intra - kf * dk_intra - dkb2 * kb2
        db = db + jnp.where(t_idx == c_len - 1, dbl, 0.0)
        dg = _tri_sum(_tril_bf16(c_len, upper=True), db)
        dqf = dqb * eb + dq_intra
        dkf = dkb2 * ebl + dk_intra
        dq_ref[...] = dqf * (sq * (1.0 + q * (1.0 - sq)))
        dfv = dg / f - dkf
        df_ref[...] = dfv * (1.0 - lbv) * sig * (1.0 - sig)
        di_ref[...] = dvf
        dlb_ref[...] += jnp.sum(dfv * (1.0 - sig), axis=0, keepdims=True)

    blk = (c_len, HGRN_WIDTH)
    rev = lambda c: last - c
    return pl.pallas_call(
        body,
        name="hgrn_bwd",
        grid=(n_chunks,),
        in_specs=[
            pl.BlockSpec(blk, lambda c: (rev(c), col0)),
            pl.BlockSpec(blk, lambda c: (rev(c), col0 + 1)),
            pl.BlockSpec(blk, lambda c: (rev(c), col0 + 2)),
            pl.BlockSpec((2, HGRN_WIDTH), lambda c: (0, 0)),
            pl.BlockSpec(blk, lambda c: (rev(c), 0)),
            pl.BlockSpec((1, nh, hd, hd), lambda c: (rev(c), 0, 0, 0)),
            pl.BlockSpec(blk, lambda c: (rev(c), 0)),
        ],
        out_specs=[
            pl.BlockSpec(blk, lambda c: (rev(c), 0)),
            pl.BlockSpec(blk, lambda c: (rev(c), 0)),
            pl.BlockSpec(blk, lambda c: (rev(c), 0)),
            pl.BlockSpec((1, HGRN_WIDTH), lambda c: (0, 0)),
        ],
        out_shape=[jax.ShapeDtypeStruct((s, HGRN_WIDTH), F32)] * 3 + [jax.ShapeDtypeStruct((1, HGRN_WIDTH), F32)],
        scratch_shapes=[
            pltpu.VMEM((nh, hd, hd), F32),
            pltpu.VMEM(blk, F32),
            pltpu.VMEM(blk, F32),
            pltpu.VMEM(blk, F32),
            pltpu.VMEM((nh, c_len, hd), F32),
            pltpu.VMEM(blk, F32),
            pltpu.VMEM(blk, F32),
        ],
        compiler_params=_params(dimension_semantics=("arbitrary",)),
    )(proj, proj, proj, lb, d_o, states, a_mat)


def _row_spec(tm, width, col=0):
    return pl.BlockSpec((tm, width), lambda i: (i, col))


def _const_spec(width):
    return pl.BlockSpec((1, width), lambda i: (0, 0))


def _acc_rows(ref, value):
    @pl.when(pl.program_id(0) == 0)
    def _():
        ref[...] = jnp.zeros_like(ref)

    ref[...] += jnp.sum(value, axis=0, keepdims=True)


def mix_fwd(attn_parts, o_h, proj, an, hn, w_out_b, gp, x):
    s = x.shape[0]
    tm = TOKEN_TILE
    gate_col = IN_PROJ_WIDTH // HGRN_WIDTH - 1
    hd = HGRN_HEAD_DIM

    def body(o1, o2, o3, l1, l2, l3, oh_ref, gate_ref, an_ref, hn_ref, w_ref, gp_ref, x_ref,
             x1_ref, cat_ref, mixed_ref, attn_ref, lse_ref):
        ls = [l1[...], l2[...], l3[...]]
        m = jnp.maximum(jnp.maximum(ls[0], ls[1]), ls[2])
        es = [jnp.exp(l - m) for l in ls]
        den = es[0] + es[1] + es[2]
        attn = (es[0] * o1[...] + es[1] * o2[...] + es[2] * o3[...]) / den
        attn_ref[...] = attn
        lse_ref[...] = m + jnp.log(den)
        cat_ref[:, :ATTN_WIDTH] = _rms_fwd(attn, an_ref[...], ATTN_WIDTH).astype(BF16)
        gate = gate_ref[...]
        silu_g = gate * _sigmoid(gate)
        for h in range(HGRN_HEADS):
            hs = slice(h * hd, (h + 1) * hd)
            rec = _rms_fwd(oh_ref[:, hs], hn_ref[:, hs], hd) * silu_g[:, hs]
            cat_ref[:, ATTN_WIDTH + h * hd:ATTN_WIDTH + (h + 1) * hd] = rec.astype(BF16)
        mixed = _dot(cat_ref[...], w_ref[...])
        mixed_ref[...] = mixed
        x1_ref[...] = x_ref[...] + _rms_fwd(mixed, gp_ref[...], D_MODEL)

    aw = ATTN_WIDTH
    return pl.pallas_call(
        body,
        name="mix_fwd",
        grid=(s // tm,),
        in_specs=[_row_spec(tm, aw)] * 7 + [_row_spec(tm, aw, gate_col), _const_spec(aw), _const_spec(aw), _vmem_spec(),
                                            _const_spec(D_MODEL), _row_spec(tm, D_MODEL)],
        out_specs=[_row_spec(tm, D_MODEL), _row_spec(tm, D_MODEL), _row_spec(tm, D_MODEL), _row_spec(tm, aw),
                   _row_spec(tm, aw)],
        out_shape=[
            jax.ShapeDtypeStruct((s, D_MODEL), F32),
            jax.ShapeDtypeStruct((s, D_MODEL), BF16),
            jax.ShapeDtypeStruct((s, D_MODEL), F32),
            jax.ShapeDtypeStruct((s, aw), F32),
            jax.ShapeDtypeStruct((s, aw), F32),
        ],
        compiler_params=_params(dimension_semantics=("arbitrary",)),
    )(*[p[0] for p in attn_parts], *[p[1] for p in attn_parts], o_h, proj, an, hn, w_out_b, gp, x)


def mix_bwd(dx1, mixed, gp, w_out_b, attn, an, o_h, proj, hn):
    s = dx1.shape[0]
    tm = TOKEN_TILE
    gate_col = IN_PROJ_WIDTH // HGRN_WIDTH - 1
    hd = HGRN_HEAD_DIM
    aw = ATTN_WIDTH

    def body(dx1_ref, mixed_ref, gp_ref, w_ref, attn_ref, an_ref, oh_ref, gate_ref, hn_ref,
             dmix_ref, do_ref, delta_ref, doh_ref, dgate_ref, dgp_ref, dan_ref, dhn_ref):
        dmixed, gp_c = _rms_bwd(dx1_ref[...], mixed_ref[...], gp_ref[...], D_MODEL)
        _acc_rows(dgp_ref, gp_c)
        dmixed_b = dmixed.astype(BF16)
        dmix_ref[...] = dmixed_b
        dcat = _dot_nt(dmixed_b, w_ref[...])
        attn = attn_ref[...]
        d_o, an_c = _rms_bwd(dcat[:, :aw], attn, an_ref[...], aw)
        _acc_rows(dan_ref, an_c)
        do_ref[...] = d_o
        prod = d_o * attn
        for pair in range(ATTN_HEADS // 2):
            lanes = slice(pair * 128, (pair + 1) * 128)
            pp = prod[:, lanes]
            low = _lane_half((tm, 128), 0)
            lo = jnp.sum(jnp.where(low, pp, 0.0), axis=-1, keepdims=True)
            hi = jnp.sum(jnp.where(low, 0.0, pp), axis=-1, keepdims=True)
            delta_ref[:, lanes] = jnp.where(low, lo, hi)
        gate = gate_ref[...]
        sg = _sigmoid(gate)
        silu_g = gate * sg
        drec = dcat[:, aw:]
        hn_parts = []
        for h in range(HGRN_HEADS):
            hs = slice(h * hd, (h + 1) * hd)
            oh = oh_ref[:, hs]
            on = _rms_fwd(oh, hn_ref[:, hs], hd)
            dgate_ref[:, hs] = drec[:, hs] * on * (sg[:, hs] * (1.0 + gate[:, hs] * (1.0 - sg[:, hs])))
            d_oh, hn_c = _rms_bwd(drec[:, hs] * silu_g[:, hs], oh, hn_ref[:, hs], hd)
            doh_ref[:, hs] = d_oh
            hn_parts.append(hn_c)
        _acc_rows(dhn_ref, jnp.concatenate(hn_parts, axis=1))

    return pl.pallas_call(
        body,
        name="mix_bwd",
        grid=(s // tm,),
        in_specs=[_row_spec(tm, D_MODEL), _row_spec(tm, D_MODEL), _const_spec(D_MODEL), _vmem_spec(), _row_spec(tm, aw),
                  _const_spec(aw), _row_spec(tm, aw), _row_spec(tm, aw, gate_col), _const_spec(aw)],
        out_specs=[_row_spec(tm, D_MODEL)] + [_row_spec(tm, aw)] * 4 + [_const_spec(D_MODEL), _const_spec(aw),
                                                                        _const_spec(aw)],
        out_shape=[jax.ShapeDtypeStruct((s, D_MODEL), BF16)] + [jax.ShapeDtypeStruct((s, aw), F32)] * 4 + [
            jax.ShapeDtypeStruct((1, D_MODEL), F32), jax.ShapeDtypeStruct((1, aw), F32),
            jax.ShapeDtypeStruct((1, aw), F32)],
        compiler_params=_params(dimension_semantics=("arbitrary",)),
    )(dx1, mixed, gp, w_out_b, attn, an, o_h, proj, hn)


def mlp_fwd_bwd(x1, g_pre, w1_blocks, w2_b, g_post, target):
    s = x1.shape[0]
    tm = TOKEN_TILE
    nblk, _, fb = w1_blocks.shape

    def body(x1_ref, gpre_ref, w1_ref, w2_ref, gpost_ref, t_ref,
             dx1_ref, h2_ref, a_ref, du_ref, dff_ref, loss_ref, dgpre_ref, dgpost_ref, u_ref):
        x1v = x1_ref[...]
        h2 = _rms_fwd(x1v, gpre_ref[...], D_MODEL).astype(BF16)
        h2_ref[...] = h2
        ff = jnp.zeros((tm, D_MODEL), F32)
        for j in range(nblk):
            cols = slice(j * fb, (j + 1) * fb)
            ru = jnp.maximum(_dot(h2, w1_ref[j]), 0.0)
            u_ref[:, cols] = ru
            a = (ru * ru).astype(BF16)
            a_ref[:, cols] = a
            ff = ff + _dot(a, w2_ref[cols, :])
        diff = x1v + _rms_fwd(ff, gpost_ref[...], D_MODEL) - t_ref[...]
        _acc_rows(loss_ref, diff * diff)
        dy = diff * (1.0 / D_MODEL)
        dff, gpost_c = _rms_bwd(dy, ff, gpost_ref[...], D_MODEL)
        _acc_rows(dgpost_ref, gpost_c)
        dff_b = dff.astype(BF16)
        dff_ref[...] = dff_b
        dh2 = jnp.zeros((tm, D_MODEL), F32)
        for j in range(nblk):
            cols = slice(j * fb, (j + 1) * fb)
            du = (_dot_nt(dff_b, w2_ref[cols, :]) * (2.0 * u_ref[:, cols])).astype(BF16)
            du_ref[:, cols] = du
            dh2 = dh2 + _dot_nt(du, w1_ref[j])
        dxa, gpre_c = _rms_bwd(dh2, x1v, gpre_ref[...], D_MODEL)
        _acc_rows(dgpre_ref, gpre_c)
        dx1_ref[...] = dy + dxa

    dm = D_MODEL
    return pl.pallas_call(
        body,
        name="mlp_fwd_bwd",
        grid=(s // tm,),
        in_specs=[_row_spec(tm, dm), _const_spec(dm), _vmem_spec(), _vmem_spec(), _const_spec(dm), _row_spec(tm, dm)],
        out_specs=[_row_spec(tm, dm), _row_spec(tm, dm), _row_spec(tm, D_FF), _row_spec(tm, D_FF), _row_spec(tm, dm),
                   _const_spec(dm), _const_spec(dm), _const_spec(dm)],
        out_shape=[
            jax.ShapeDtypeStruct((s, dm), F32),
            jax.ShapeDtypeStruct((s, dm), BF16),
            jax.ShapeDtypeStruct((s, D_FF), BF16),
            jax.ShapeDtypeStruct((s, D_FF), BF16),
            jax.ShapeDtypeStruct((s, dm), BF16),
            jax.ShapeDtypeStruct((1, dm), F32),
            jax.ShapeDtypeStruct((1, dm), F32),
            jax.ShapeDtypeStruct((1, dm), F32),
        ],
        scratch_shapes=[pltpu.VMEM((tm, D_FF), F32)],
        compiler_params=_params(dimension_semantics=("arbitrary",)),
    )(x1, g_pre, w1_blocks, w2_b, g_post, target)


def in_proj_bwd(attn_grads, hgrn_grads, dgate, w_in_b, x, g1, dx1):
    s = x.shape[0]
    tm = TOKEN_TILE
    aw = ATTN_WIDTH
    n_attn = len(attn_grads)
    flat = [g[k] for k in range(3) for g in attn_grads] + list(hgrn_grads) + [dgate]

    def body(*refs):
        parts = refs[:len(flat)]
        w_ref, x_ref, g_ref, dx1_ref, dx_ref, dproj_ref, dg_ref = refs[len(flat):]
        groups = []
        for k in range(3):
            acc = parts[k * n_attn][...]
            for p in parts[k * n_attn + 1:(k + 1) * n_attn]:
                acc = acc + p[...]
            groups.append(acc)
        groups += [p[...] for p in parts[3 * n_attn:]]
        dh = jnp.zeros((tm, D_MODEL), F32)
        for gi, grp in enumerate(groups):
            cols = slice(gi * aw, (gi + 1) * aw)
            gb = grp.astype(BF16)
            dproj_ref[:, cols] = gb
            dh = dh + _dot_nt(gb, w_ref[:, cols])
        dxa, g_c = _rms_bwd(dh, x_ref[...], g_ref[...], D_MODEL)
        _acc_rows(dg_ref, g_c)
        dx_ref[...] = dx1_ref[...] + dxa

    dm = D_MODEL
    return pl.pallas_call(
        body,
        name="in_proj_bwd",
        grid=(s // tm,),
        in_specs=[_row_spec(tm, aw)] * len(flat) + [_vmem_spec(), _row_spec(tm, dm), _const_spec(dm), _row_spec(tm, dm)],
        out_specs=[_row_spec(tm, dm), _row_spec(tm, IN_PROJ_WIDTH), _const_spec(dm)],
        out_shape=[jax.ShapeDtypeStruct((s, dm), F32), jax.ShapeDtypeStruct((s, IN_PROJ_WIDTH), BF16),
                   jax.ShapeDtypeStruct((1, dm), F32)],
        compiler_params=_params(dimension_semantics=("arbitrary",)),
    )(*flat, w_in_b, x, g1, dx1)


def wgrad(a_b, b_b, tn, name):
    s, k = a_b.shape
    n = b_b.shape[1]
    ts = 512

    def body(a_ref, b_ref, o_ref):
        @pl.when(pl.program_id(1) == 0)
        def _():
            o_ref[...] = jnp.zeros_like(o_ref)

        o_ref[0] += _dot_tn(a_ref[...], b_ref[...])

    return pl.pallas_call(
        body,
        name=name,
        grid=(n // tn, s // ts),
        in_specs=[pl.BlockSpec((ts, k), lambda j, i: (i, 0)), pl.BlockSpec((ts, tn), lambda j, i: (i, j))],
        out_specs=pl.BlockSpec((1, k, tn), lambda j, i: (j, 0, 0)),
        out_shape=jax.ShapeDtypeStruct((n // tn, k, tn), F32),
        compiler_params=_params(dimension_semantics=("arbitrary", "arbitrary")),
    )(a_b, b_b)


def local_step(x, target, g1, w_in_b, an, lb, hn, w_out_b, gp, g_pre, w1_blocks, w2_b, g_post):
    proj, h_b = in_proj_fwd(x, g1, w_in_b)
    attn_parts = [attn_fwd(proj, d) for d in DILATIONS]
    o_h, states, a_mat = hgrn_fwd(proj, lb)
    x1, cat_b, mixed, attn, lse = mix_fwd(attn_parts, o_h, proj, an, hn, w_out_b, gp, x)
    dx1, h2_b, a_b, du_b, dff_b, loss_vec, dg_pre, dg_post = mlp_fwd_bwd(x1, g_pre, w1_blocks, w2_b, g_post, target)
    dw2 = wgrad(a_b, dff_b, D_MODEL, "wgrad_ff2")
    dw1 = wgrad(h2_b, du_b, D_FF // N_DEV, "wgrad_ff1")
    dmix_b, d_o, delta, d_oh, dgate, dgp, dan, dhn = mix_bwd(dx1, mixed, gp, w_out_b, attn, an, o_h, proj, hn)
    dwout = wgrad(cat_b, dmix_b, D_MODEL, "wgrad_out")
    attn_grads = [attn_bwd(proj, d_o, lse, delta, d) for d in DILATIONS]
    hgrn_grads = hgrn_bwd(proj, lb, d_oh, states, a_mat)
    dlb = hgrn_grads[3]
    dx, dproj_b, dg1 = in_proj_bwd(attn_grads, hgrn_grads[:3], dgate, w_in_b, x, g1, dx1)
    dwin = wgrad(h_b, dproj_b, 2 * IN_PROJ_WIDTH // N_DEV, "wgrad_in")
    small = dict(dg1=dg1, dan=dan, dlb=dlb, dhn=dhn, dgp=dgp, dg_pre=dg_pre, dg_post=dg_post, loss_vec=loss_vec)
    return dx, dwin, dwout, dw1, dw2, small


def _position():
    x, y, c = lax.axis_index("x"), lax.axis_index("y"), lax.axis_index("c")
    other_chips = [(1 - x, y), (x, 1 - y), (1 - x, 1 - y)]
    return x, y, c, other_chips


def _any_spec():
    return pl.BlockSpec(memory_space=pl.ANY)


def all_gather(shards):
    n = len(shards)

    def body(*refs):
        ins, outs = refs[:n], refs[n:2 * n]
        send_sems, recv_sems, local_sems = refs[2 * n:]
        x, y, c, chips = _position()
        me, sibling = (x, y, c), (x, y, 1 - c)

        def slot(a, px, py, pc):
            return outs[a].at[4 * px + 2 * py + pc]

        def copy(a, k, block, to, src=None):
            return pltpu.make_async_remote_copy(
                src_ref=slot(a, *block) if src is None else src, dst_ref=slot(a, *block),
                send_sem=send_sems.at[a, k], recv_sem=recv_sems.at[a, k], device_id=to, device_id_type=MESH)

        local = [pltpu.make_async_copy(ins[a], slot(a, *me), local_sems.at[a]) for a in range(n)]
        for cp in local:
            cp.start()
        first = []
        for a in range(n):
            first.append(copy(a, 0, me, sibling, src=ins[a]))
            first += [copy(a, 1 + j, me, (*chip, c), src=ins[a]) for j, chip in enumerate(chips)]
        for cp in first:
            cp.start()
        passed = []
        for j, chip in enumerate(chips):
            for a in range(n):
                copy(a, 1 + j, (*chip, c), me).wait_recv()
                fwd = copy(a, 4 + j, (*chip, c), sibling)
                fwd.start()
                passed.append(fwd)
        for a in range(n):
            copy(a, 0, sibling, me).wait_recv()
            for j, chip in enumerate(chips):
                copy(a, 4 + j, (*chip, 1 - c), me).wait_recv()
        for cp in first + passed:
            cp.wait_send()
        for cp in local:
            cp.wait()

    return pl.pallas_call(
        body,
        name="all_gather_weights",
        in_specs=[_any_spec()] * n,
        out_specs=[_any_spec()] * n,
        out_shape=[jax.ShapeDtypeStruct((N_DEV,) + sh.shape, sh.dtype) for sh in shards],
        scratch_shapes=[pltpu.SemaphoreType.DMA((n, 7)), pltpu.SemaphoreType.DMA((n, 7)), pltpu.SemaphoreType.DMA((n,))],
    )(*shards)


def reduce_to_core(grads):
    n = len(grads)

    def body(*refs):
        ins, outs = refs[:n], refs[n:2 * n]
        send_sems, recv_sems = refs[2 * n:]
        x, y, c, _ = _position()
        sibling = (x, y, 1 - c)
        copies = []
        for a in range(n):
            for q in range(4):
                copies.append(pltpu.make_async_remote_copy(
                    src_ref=ins[a].at[2 * q + (1 - c)], dst_ref=outs[a].at[q],
                    send_sem=send_sems.at[a, q], recv_sem=recv_sems.at[a, q], device_id=sibling, device_id_type=MESH))
        for cp in copies:
            cp.start()
        for cp in copies:
            cp.wait()

    return pl.pallas_call(
        body,
        name="reduce_to_core",
        in_specs=[_any_spec()] * n,
        out_specs=[_any_spec()] * n,
        out_shape=[jax.ShapeDtypeStruct((4,) + g.shape[1:], g.dtype) for g in grads],
        scratch_shapes=[pltpu.SemaphoreType.DMA((n, 4)), pltpu.SemaphoreType.DMA((n, 4))],
    )(*grads)


def pair_sum(grad, from_sibling, name):
    _, r, cdim = grad.shape
    tr = min(r, 256)
    c_idx = lax.axis_index("c").astype(jnp.int32).reshape(1)

    def body(c_ref, g_ref, s_ref, o_ref):
        o_ref[...] = g_ref[...] + s_ref[...]

    return pl.pallas_call(
        body,
        name=name,
        grid_spec=pltpu.PrefetchScalarGridSpec(
            num_scalar_prefetch=1,
            grid=(4, r // tr),
            in_specs=[pl.BlockSpec((1, tr, cdim), lambda q, i, cr: (2 * q + cr[0], i, 0)),
                      pl.BlockSpec((1, tr, cdim), lambda q, i, cr: (q, i, 0))],
            out_specs=pl.BlockSpec((1, tr, cdim), lambda q, i, cr: (q, i, 0)),
        ),
        out_shape=jax.ShapeDtypeStruct((4, r, cdim), grad.dtype),
        compiler_params=_params(dimension_semantics=("arbitrary", "arbitrary")),
    )(c_idx, grad, from_sibling)


def reduce_to_chip(pairs):
    n = len(pairs)

    def body(*refs):
        ins, outs = refs[:n], refs[n:2 * n]
        send_sems, recv_sems = refs[2 * n:]
        x, y, c, chips = _position()
        copies = []
        for a in range(n):
            for j, (px, py) in enumerate(chips):
                copies.append(pltpu.make_async_remote_copy(
                    src_ref=ins[a].at[2 * px + py], dst_ref=outs[a].at[j],
                    send_sem=send_sems.at[a, j], recv_sem=recv_sems.at[a, j], device_id=(px, py, c), device_id_type=MESH))
        for cp in copies:
            cp.start()
        for cp in copies:
            cp.wait()

    return pl.pallas_call(
        body,
        name="reduce_to_chip",
        in_specs=[_any_spec()] * n,
        out_specs=[_any_spec()] * n,
        out_shape=[jax.ShapeDtypeStruct((3,) + p.shape[1:], p.dtype) for p in pairs],
        scratch_shapes=[pltpu.SemaphoreType.DMA((n, 3)), pltpu.SemaphoreType.DMA((n, 3))],
    )(*pairs)


def _adamw(w, g, m, v):
    m = ADAM_B1 * m + (1.0 - ADAM_B1) * g
    v = ADAM_B2 * v + (1.0 - ADAM_B2) * (g * g)
    m_hat = m / (1.0 - ADAM_B1 ** ADAM_STEP)
    v_hat = v / (1.0 - ADAM_B2 ** ADAM_STEP)
    delta = -ADAM_LR * (m_hat / (jnp.sqrt(v_hat) + ADAM_EPS) + ADAM_WD * w)
    return delta, m, v


def sum_adamw(pairs, others, w, m, v, name):
    r, cdim = w.shape
    tr = min(r, 256)
    chip_idx = (2 * lax.axis_index("x") + lax.axis_index("y")).astype(jnp.int32).reshape(1)

    def body(q_ref, p_ref, o_ref, w_ref, m_ref, v_ref, g_out, d_out, m_out, v_out):
        g = p_ref[0] + o_ref[0] + o_ref[1] + o_ref[2]
        g_out[...] = g
        d_out[...], m_out[...], v_out[...] = _adamw(w_ref[...], g, m_ref[...], v_ref[...])

    tile = lambda: pl.BlockSpec((tr, cdim), lambda i, qr: (i, 0))
    return pl.pallas_call(
        body,
        name=name,
        grid_spec=pltpu.PrefetchScalarGridSpec(
            num_scalar_prefetch=1,
            grid=(r // tr,),
            in_specs=[pl.BlockSpec((1, tr, cdim), lambda i, qr: (qr[0], i, 0)),
                      pl.BlockSpec((3, tr, cdim), lambda i, qr: (0, i, 0)), tile(), tile(), tile()],
            out_specs=[tile(), tile(), tile(), tile()],
        ),
        out_shape=[jax.ShapeDtypeStruct((r, cdim), F32)] * 4,
        compiler_params=_params(dimension_semantics=("arbitrary",)),
    )(chip_idx, pairs, others, w, m, v)


SMALL_ROWS = 8


def small_all_reduce(packed):
    shape = packed.shape

    def body(in_ref, out_ref, recv_ref, send_sems, recv_sems):
        x, y, c, _ = _position()
        my_id = 4 * x + 2 * y + c
        recv_ref[my_id] = in_ref[...]
        copies = []
        for rel in range(1, N_DEV):
            fx, fy, fc = (rel >> 2) & 1, (rel >> 1) & 1, rel & 1
            px = 1 - x if fx else x
            py = 1 - y if fy else y
            pc = 1 - c if fc else c
            cp = pltpu.make_async_remote_copy(
                src_ref=in_ref, dst_ref=recv_ref.at[my_id], send_sem=send_sems.at[rel - 1],
                recv_sem=recv_sems.at[rel - 1], device_id=(px, py, pc), device_id_type=MESH)
            cp.start()
            copies.append((cp, pltpu.make_async_remote_copy(
                src_ref=in_ref, dst_ref=recv_ref.at[4 * px + 2 * py + pc], send_sem=send_sems.at[rel - 1],
                recv_sem=recv_sems.at[rel - 1], device_id=(px, py, pc), device_id_type=MESH)))
        for cp, landing in copies:
            landing.wait_recv()
        for cp, landing in copies:
            cp.wait_send()
        total = recv_ref[0]
        for k in range(1, N_DEV):
            total = total + recv_ref[k]
        out_ref[...] = total

    return pl.pallas_call(
        body,
        name="small_all_reduce",
        in_specs=[_vmem_spec()],
        out_specs=_vmem_spec(),
        out_shape=jax.ShapeDtypeStruct(shape, F32),
        scratch_shapes=[pltpu.VMEM((N_DEV,) + shape, F32), pltpu.SemaphoreType.DMA((N_DEV - 1,)),
                        pltpu.SemaphoreType.DMA((N_DEV - 1,))],
    )(packed)


def small_adamw(reduced, w, m, v):
    def body(r_ref, w_ref, m_ref, v_ref, g_out, d_out, m_out, v_out, loss_out):
        red = r_ref[...]
        wv = w_ref[...]
        lb = _lower_bound(jnp.concatenate([wv[5:6, :HGRN_WIDTH], wv[5:6, HGRN_WIDTH:]], axis=0))
        t = red[5:6, :HGRN_WIDTH] * lb * (1.0 - lb)
        row = lax.broadcasted_iota(jnp.int32, red.shape, 0)
        g = jnp.where(row == 5, jnp.concatenate([t, -t], axis=1), jnp.where(row >= 6, 0.0, red))
        g_out[...] = g
        d_out[...], m_out[...], v_out[...] = _adamw(wv, g, m_ref[...], v_ref[...])
        loss = jnp.sum(red[6:7, :], axis=-1, keepdims=True) * (0.5 / D_MODEL)
        loss_out[...] = jnp.broadcast_to(loss, loss_out.shape)

    return pl.pallas_call(
        body,
        name="small_adamw",
        in_specs=[_vmem_spec()] * 4,
        out_specs=[_vmem_spec()] * 5,
        out_shape=[jax.ShapeDtypeStruct(reduced.shape, F32)] * 4 + [jax.ShapeDtypeStruct((8, 128), F32)],
    )(reduced, w, m, v)


def _pack_small(g1, gp, g_pre, g_post, an, hn, logits_or_dlb, extra=None):
    row5 = logits_or_dlb.reshape(1, -1)
    row5 = jnp.pad(row5, ((0, 0), (0, D_MODEL - row5.shape[1])))
    row6 = jnp.zeros((1, D_MODEL), F32) if extra is None else extra
    return jnp.concatenate([g1, gp, g_pre, g_post, jnp.concatenate([an, hn], axis=1), row5, row6,
                            jnp.zeros((1, D_MODEL), F32)], axis=0)


def _unpack_small(p):
    return dict(mix_pre_norm=p[0:1], mix_post_norm=p[1:2], mlp_pre_norm=p[2:3], mlp_post_norm=p[3:4],
                attn_out_norm=p[4:5, :ATTN_WIDTH], hgrn_out_norm=p[4:5, ATTN_WIDTH:],
                hgrn_lb_logits=p[5].reshape(2, HGRN_WIDTH))


BIG = ("w_in", "w_out", "w_ff1", "w_ff2")
ORDER = ("mix_pre_norm", "w_in", "attn_out_norm", "hgrn_lb_logits", "hgrn_out_norm", "w_out", "mix_post_norm",
         "mlp_pre_norm", "w_ff1", "w_ff2", "mlp_post_norm")


def kernel(x, mix_pre_norm, w_in, attn_out_norm, hgrn_lb_logits, hgrn_out_norm, w_out, mix_post_norm, mlp_pre_norm, w_ff1, w_ff2, mlp_post_norm, loss_target, m_mix_pre_norm, m_w_in, m_attn_out_norm, m_hgrn_lb_logits, m_hgrn_out_norm, m_w_out, m_mix_post_norm, m_mlp_pre_norm, m_w_ff1, m_w_ff2, m_mlp_post_norm, v_mix_pre_norm, v_w_in, v_attn_out_norm, v_hgrn_lb_logits, v_hgrn_out_norm, v_w_out, v_mix_post_norm, v_mlp_pre_norm, v_w_ff1, v_w_ff2, v_mlp_post_norm):
    w = dict(w_in=w_in[0], w_out=w_out[0], w_ff1=w_ff1[0], w_ff2=w_ff2[0])
    m = dict(w_in=m_w_in[0], w_out=m_w_out[0], w_ff1=m_w_ff1[0], w_ff2=m_w_ff2[0])
    v = dict(w_in=v_w_in[0], w_out=v_w_out[0], w_ff1=v_w_ff1[0], w_ff2=v_w_ff2[0])

    gathered = all_gather([w[k].astype(BF16) for k in BIG])
    w_in_b = gathered[0].transpose(1, 0, 2).reshape(D_MODEL, IN_PROJ_WIDTH)
    w_out_b = gathered[1].reshape(D_MODEL, D_MODEL)
    w1_blocks = gathered[2]
    w2_b = gathered[3].reshape(D_FF, D_MODEL)

    dx, dwin, dwout, dw1, dw2, small = local_step(
        x[0], loss_target[0], mix_pre_norm, w_in_b, attn_out_norm, hgrn_lb_logits, hgrn_out_norm, w_out_b,
        mix_post_norm, mlp_pre_norm, w1_blocks, w2_b, mlp_post_norm)

    shard_w = IN_PROJ_WIDTH // N_DEV
    grads = [
        dwin.reshape(N_DEV // 2, D_MODEL, 2, shard_w).transpose(0, 2, 1, 3).reshape(N_DEV, D_MODEL, shard_w),
        dwout.reshape(N_DEV, D_MODEL // N_DEV, D_MODEL),
        dw1,
        dw2.reshape(N_DEV, D_FF // N_DEV, D_MODEL),
    ]
    from_sibling = reduce_to_core(grads)
    pairs = [pair_sum(g, s, f"pair_sum_{k}") for g, s, k in zip(grads, from_sibling, BIG)]
    others = reduce_to_chip(pairs)
    big = {k: sum_adamw(p, o, w[k], m[k], v[k], f"sum_adamw_{k}") for k, p, o in zip(BIG, pairs, others)}

    packed_g = _pack_small(small["dg1"], small["dgp"], small["dg_pre"], small["dg_post"], small["dan"], small["dhn"],
                           small["dlb"], small["loss_vec"])
    reduced = small_all_reduce(packed_g)
    pack = lambda a, b, c2, d, e, f, g: _pack_small(a, b, c2, d, e, f, g)
    w_s = pack(mix_pre_norm, mix_post_norm, mlp_pre_norm, mlp_post_norm, attn_out_norm, hgrn_out_norm, hgrn_lb_logits)
    m_s = pack(m_mix_pre_norm, m_mix_post_norm, m_mlp_pre_norm, m_mlp_post_norm, m_attn_out_norm, m_hgrn_out_norm,
               m_hgrn_lb_logits)
    v_s = pack(v_mix_pre_norm, v_mix_post_norm, v_mlp_pre_norm, v_mlp_post_norm, v_attn_out_norm, v_hgrn_out_norm,
               v_hgrn_lb_logits)
    g_s, d_s, nm_s, nv_s, loss = small_adamw(reduced, w_s, m_s, v_s)
    small_out = [_unpack_small(t) for t in (g_s, d_s, nm_s, nv_s)]

    outs = [loss[0, 0], dx[None]]
    for kind in range(4):
        for name in ORDER:
            outs.append(big[name][kind][None] if name in BIG else small_out[kind][name])
    return tuple(outs)
```

```python
import functools
import math

import jax
import jax.numpy as jnp
from jax import lax
from jax.experimental import pallas as pl
from jax.experimental.pallas import tpu as pltpu

F32 = jnp.float32
BF16 = jnp.bfloat16

D_MODEL = 1024
SEQ = 4096
ATTN_WIDTH = 512
ATTN_HEAD_DIM = 64
ATTN_HEADS = 8
ATTN_BLOCK = 128
DILATIONS = (1, 4, 16)
HGRN_WIDTH = 512
HGRN_HEADS = 4
HGRN_HEAD_DIM = 128
HGRN_CHUNK = 64
IN_PROJ_WIDTH = 3584
D_FF = 4096
RMS_EPS = 1e-6
N_DEV = 8
ADAM_LR = 0.001
ADAM_B1 = 0.9
ADAM_B2 = 0.999
ADAM_EPS = 1e-08
ADAM_WD = 0.01
ADAM_STEP = 10

SUBLANES = 8
COLUMN_UNROLL = 4
TOKEN_TILE = 256
VMEM_LIMIT = 56 * 1024 * 1024
NEG_BIG = -1e30
MESH = pl.DeviceIdType.MESH


def _params(**kw):
    return pltpu.CompilerParams(vmem_limit_bytes=VMEM_LIMIT, **kw)


def _vmem_spec():
    return pl.BlockSpec(memory_space=pltpu.VMEM)


def _dot(a, b):
    return jnp.dot(a, b, preferred_element_type=F32)


def _dot_nt(a, b):
    return lax.dot_general(a, b, (((1,), (1,)), ((), ())), preferred_element_type=F32)


def _dot_tn(a, b):
    return lax.dot_general(a, b, (((0,), (0,)), ((), ())), preferred_element_type=F32)


def _sigmoid(x):
    return 1.0 / (1.0 + jnp.exp(-x))


def _rms_fwd(x, gain, width):
    r = lax.rsqrt(jnp.sum(x * x, axis=-1, keepdims=True) * (1.0 / width) + RMS_EPS)
    return x * r * gain


def _rms_bwd(dy, x, gain, width):
    r = lax.rsqrt(jnp.sum(x * x, axis=-1, keepdims=True) * (1.0 / width) + RMS_EPS)
    xhat = x * r
    dxhat = dy * gain
    dx = r * (dxhat - xhat * (jnp.sum(dxhat * xhat, axis=-1, keepdims=True) * (1.0 / width)))
    return dx, dy * xhat


def _split3(x):
    hi = x.astype(BF16)
    r1 = x - hi.astype(F32)
    mid = r1.astype(BF16)
    lo = (r1 - mid.astype(F32)).astype(BF16)
    return hi, mid, lo


def _tri_sum(tri_bf16, x):
    hi, mid, lo = _split3(x)
    return _dot(tri_bf16, hi) + _dot(tri_bf16, mid) + _dot(tri_bf16, lo)


def in_proj_fwd(x, g1, w_in_b):
    s = x.shape[0]
    tm = TOKEN_TILE

    def body(x_ref, g_ref, w_ref, proj_ref, h_ref):
        h = _rms_fwd(x_ref[...], g_ref[...], D_MODEL).astype(BF16)
        h_ref[...] = h
        proj_ref[...] = _dot(h, w_ref[...])

    return pl.pallas_call(
        body,
        name="in_proj_fwd",
        grid=(s // tm,),
        in_specs=[
            pl.BlockSpec((tm, D_MODEL), lambda i: (i, 0)),
            pl.BlockSpec((1, D_MODEL), lambda i: (0, 0)),
            _vmem_spec(),
        ],
        out_specs=[
            pl.BlockSpec((tm, IN_PROJ_WIDTH), lambda i: (i, 0)),
            pl.BlockSpec((tm, D_MODEL), lambda i: (i, 0)),
        ],
        out_shape=[jax.ShapeDtypeStruct((s, IN_PROJ_WIDTH), F32), jax.ShapeDtypeStruct((s, D_MODEL), BF16)],
        compiler_params=_params(dimension_semantics=("arbitrary",)),
    )(x, g1, w_in_b)


def _attn_scores(qm, kcat, head, dilation, first_block):
    s = _dot_nt(qm, kcat) * (ATTN_HEAD_DIM ** -0.5)
    qi = lax.broadcasted_iota(jnp.int32, (ATTN_BLOCK, 2 * ATTN_BLOCK), 0)
    kj = lax.broadcasted_iota(jnp.int32, (ATTN_BLOCK, 2 * ATTN_BLOCK), 1)
    dist = qi + ATTN_BLOCK - kj
    valid = (dist >= 0) & (dist <= ATTN_BLOCK) & ((kj >= ATTN_BLOCK) | jnp.logical_not(first_block))
    slope = 2.0 ** (-8.0 * (head + 1) / ATTN_HEADS)
    bias = dist.astype(F32) * (-slope * dilation)
    return jnp.where(valid, s + bias, NEG_BIG)


def _lane_half(shape, sub):
    lane = lax.broadcasted_iota(jnp.int32, shape, 1)
    return (lane < ATTN_HEAD_DIM) if sub == 0 else (lane >= ATTN_HEAD_DIM)


def attn_fwd(proj, dilation):
    s = proj.shape[0]
    d = dilation
    length = s // d
    nb = length // ATTN_BLOCK
    nq = IN_PROJ_WIDTH // ATTN_WIDTH
    pv = proj.reshape(length, d * IN_PROJ_WIDTH)

    def body(q_ref, kc_ref, kp_ref, vc_ref, vp_ref, o_ref, lse_ref):
        first = pl.program_id(1) == 0
        for pair in range(ATTN_HEADS // 2):
            lanes = slice(pair * 128, (pair + 1) * 128)
            q2 = q_ref[:, lanes]
            kcat = jnp.concatenate([kp_ref[:, lanes], kc_ref[:, lanes]], axis=0).astype(BF16)
            vcat = jnp.concatenate([vp_ref[:, lanes], vc_ref[:, lanes]], axis=0).astype(BF16)
            o_pair = jnp.zeros((ATTN_BLOCK, 128), F32)
            lse_pair = jnp.zeros((ATTN_BLOCK, 128), F32)
            for sub in range(2):
                keep = _lane_half((ATTN_BLOCK, 128), sub)
                qm = jnp.where(keep, q2, 0.0).astype(BF16)
                sc = _attn_scores(qm, kcat, 2 * pair + sub, d, first)
                m = jnp.max(sc, axis=-1, keepdims=True)
                p = jnp.exp(sc - m)
                den = jnp.sum(p, axis=-1, keepdims=True)
                o = _dot(p.astype(BF16), vcat) / den
                o_pair = jnp.where(keep, o, o_pair)
                lse_pair = jnp.where(keep, m + jnp.log(den), lse_pair)
            o_ref[:, lanes] = o_pair
            lse_ref[:, lanes] = lse_pair

    blk = (ATTN_BLOCK, ATTN_WIDTH)
    out = pl.pallas_call(
        body,
        name=f"attn_fwd_d{d}",
        grid=(d, nb),
        in_specs=[
            pl.BlockSpec(blk, lambda r, n: (n, r * nq)),
            pl.BlockSpec(blk, lambda r, n: (n, r * nq + 1)),
            pl.BlockSpec(blk, lambda r, n: (jnp.maximum(n - 1, 0), r * nq + 1)),
            pl.BlockSpec(blk, lambda r, n: (n, r * nq + 2)),
            pl.BlockSpec(blk, lambda r, n: (jnp.maximum(n - 1, 0), r * nq + 2)),
        ],
        out_specs=[pl.BlockSpec(blk, lambda r, n: (n, r)), pl.BlockSpec(blk, lambda r, n: (n, r))],
        out_shape=[jax.ShapeDtypeStruct((length, d * ATTN_WIDTH), F32)] * 2,
        compiler_params=_params(dimension_semantics=("arbitrary", "arbitrary")),
    )(pv, pv, pv, pv, pv)
    return out[0].reshape(s, ATTN_WIDTH), out[1].reshape(s, ATTN_WIDTH)


def attn_bwd(proj, d_out, lse, delta, dilation):
    s = proj.shape[0]
    d = dilation
    length = s // d
    nb = length // ATTN_BLOCK
    nq = IN_PROJ_WIDTH // ATTN_WIDTH
    pv = proj.reshape(length, d * IN_PROJ_WIDTH)
    view = lambda a: a.reshape(length, d * ATTN_WIDTH)

    def body(q_ref, kc_ref, kp_ref, vc_ref, vp_ref, do_ref, lse_ref, dl_ref, dq_ref, dk_ref, dv_ref, ck_ref, cv_ref):
        n = pl.program_id(1)

        @pl.when(n == 0)
        def _():
            ck_ref[...] = jnp.zeros_like(ck_ref)
            cv_ref[...] = jnp.zeros_like(cv_ref)

        @pl.when(n < nb)
        def _():
            first = n == 0
            for pair in range(ATTN_HEADS // 2):
                lanes = slice(pair * 128, (pair + 1) * 128)
                q2 = q_ref[:, lanes]
                do2 = do_ref[:, lanes]
                kcat = jnp.concatenate([kp_ref[:, lanes], kc_ref[:, lanes]], axis=0).astype(BF16)
                vcat = jnp.concatenate([vp_ref[:, lanes], vc_ref[:, lanes]], axis=0).astype(BF16)
                dq_pair = jnp.zeros((ATTN_BLOCK, 128), F32)
                dk_cat = jnp.zeros((2 * ATTN_BLOCK, 128), F32)
                dv_cat = jnp.zeros((2 * ATTN_BLOCK, 128), F32)
                for sub in range(2):
                    keep = _lane_half((ATTN_BLOCK, 128), sub)
                    col = pair * 128 + sub * ATTN_HEAD_DIM
                    qm = jnp.where(keep, q2, 0.0).astype(BF16)
                    dom = jnp.where(keep, do2, 0.0).astype(BF16)
                    sc = _attn_scores(qm, kcat, 2 * pair + sub, d, first)
                    p = jnp.exp(sc - lse_ref[:, col:col + 1])
                    dp = _dot_nt(dom, vcat)
                    ds = (p * (dp - dl_ref[:, col:col + 1]) * (ATTN_HEAD_DIM ** -0.5)).astype(BF16)
                    dq_pair = jnp.where(keep, _dot(ds, kcat), dq_pair)
                    dk_cat = dk_cat + _dot_tn(ds, qm)
                    dv_cat = dv_cat + _dot_tn(p.astype(BF16), dom)
                dq_ref[:, lanes] = dq_pair
                dk_ref[:, lanes] = ck_ref[:, lanes] + dk_cat[:ATTN_BLOCK]
                dv_ref[:, lanes] = cv_ref[:, lanes] + dv_cat[:ATTN_BLOCK]
                ck_ref[:, lanes] = dk_cat[ATTN_BLOCK:]
                cv_ref[:, lanes] = dv_cat[ATTN_BLOCK:]

        @pl.when(n == nb)
        def _():
            dk_ref[...] = ck_ref[...]
            dv_ref[...] = cv_ref[...]

    blk = (ATTN_BLOCK, ATTN_WIDTH)
    cur = lambda n: jnp.minimum(n, nb - 1)
    prev = lambda n: jnp.maximum(jnp.minimum(n, nb - 1) - 1, 0)
    out = pl.pallas_call(
        body,
        name=f"attn_bwd_d{d}",
        grid=(d, nb + 1),
        in_specs=[
            pl.BlockSpec(blk, lambda r, n: (cur(n), r * nq)),
            pl.BlockSpec(blk, lambda r, n: (cur(n), r * nq + 1)),
            pl.BlockSpec(blk, lambda r, n: (prev(n), r * nq + 1)),
            pl.BlockSpec(blk, lambda r, n: (cur(n), r * nq + 2)),
            pl.BlockSpec(blk, lambda r, n: (prev(n), r * nq + 2)),
            pl.BlockSpec(blk, lambda r, n: (cur(n), r)),
            pl.BlockSpec(blk, lambda r, n: (cur(n), r)),
            pl.BlockSpec(blk, lambda r, n: (cur(n), r)),
        ],
        out_specs=[
            pl.BlockSpec(blk, lambda r, n: (cur(n), r)),
            pl.BlockSpec(blk, lambda r, n: (jnp.maximum(n - 1, 0), r)),
            pl.BlockSpec(blk, lambda r, n: (jnp.maximum(n - 1, 0), r)),
        ],
        out_shape=[jax.ShapeDtypeStruct((length, d * ATTN_WIDTH), F32)] * 3,
        scratch_shapes=[pltpu.VMEM(blk, F32), pltpu.VMEM(blk, F32)],
        compiler_params=_params(dimension_semantics=("arbitrary", "arbitrary")),
    )(pv, pv, pv, pv, pv, view(d_out), view(lse), view(delta))
    return tuple(o.reshape(s, ATTN_WIDTH) for o in out)


def _lower_bound(logits):
    return _sigmoid(logits[0:1, :] - logits[1:2, :])


def _hgrn_gates(q, fp, lb):
    sq = _sigmoid(q)
    qf = q * sq
    sig = _sigmoid(fp)
    f = lb + (1.0 - lb) * sig
    kf = (1.0 - lb) * _sigmoid(-fp)
    return sq, qf, sig, f, kf


def _tril_bf16(n, upper=False):
    r = lax.broadcasted_iota(jnp.int32, (n, n), 0)
    c = lax.broadcasted_iota(jnp.int32, (n, n), 1)
    keep = (c >= r) if upper else (c <= r)
    return jnp.where(keep, 1.0, 0.0).astype(BF16)


def hgrn_fwd(proj, lb):
    s = proj.shape[0]
    c_len, nh, hd = HGRN_CHUNK, HGRN_HEADS, HGRN_HEAD_DIM
    n_chunks = s // c_len
    col0 = (3 * ATTN_WIDTH) // HGRN_WIDTH

    def body(q_ref, f_ref, i_ref, lb_ref, o_ref, st_out_ref, a_out_ref, st_ref, b_ref, qf_ref, kf_ref, a_ref):
        @pl.when(pl.program_id(0) == 0)
        def _():
            st_ref[...] = jnp.zeros_like(st_ref)

        lbv = _lower_bound(lb_ref[...])
        _, qf, _, f, kf = _hgrn_gates(q_ref[...], f_ref[...], lbv)
        b = _tri_sum(_tril_bf16(c_len), jnp.log(f))
        b_ref[...] = b
        qf_ref[...] = qf
        kf_ref[...] = kf
        a_ref[...] = jnp.zeros_like(a_ref)
        t_idx = lax.broadcasted_iota(jnp.int32, (c_len, nh * hd), 0)
        lane = lax.broadcasted_iota(jnp.int32, (c_len, hd), 1)

        for r0 in range(0, c_len, SUBLANES):
            rows = slice(r0, c_len)

            def column(jj, carry, r0=r0, rows=rows):
                j = r0 + jj
                bj = b_ref[pl.ds(j, 1), :]
                kj = kf_ref[pl.ds(j, 1), :]
                e = jnp.exp(jnp.where(t_idx[rows] >= j, b_ref[rows, :] - bj, NEG_BIG))
                prod = qf_ref[rows, :] * kj * e
                for h in range(nh):
                    col = jnp.sum(prod[:, h * hd:(h + 1) * hd], axis=-1, keepdims=True)
                    a_ref[h, rows, :] = jnp.where(lane[rows] == j, col, a_ref[h, rows, :])
                return carry

            lax.fori_loop(0, SUBLANES, column, 0, unroll=COLUMN_UNROLL)

        b_last = b[c_len - 1:c_len, :]
        qb = (qf * jnp.exp(b)).astype(BF16)
        kb2 = (kf * jnp.exp(b_last - b)).astype(BF16)
        vf = i_ref[...].astype(BF16)
        for h in range(nh):
            hs = slice(h * hd, (h + 1) * hd)
            st = st_ref[h]
            st_out_ref[0, h] = st
            a_h = a_ref[h]
            a_out_ref[:, hs] = a_h
            o_ref[:, hs] = _dot_nt(qb[:, hs], st.astype(BF16)) + _dot(a_h[:, :c_len].astype(BF16), vf[:, hs])
            st_ref[h] = st * jnp.exp(b_last[:, hs]) + _dot_tn(vf[:, hs], kb2[:, hs])

    blk = (c_len, HGRN_WIDTH)
    return pl.pallas_call(
        body,
        name="hgrn_fwd",
        grid=(n_chunks,),
        in_specs=[
            pl.BlockSpec(blk, lambda c: (c, col0)),
            pl.BlockSpec(blk, lambda c: (c, col0 + 1)),
            pl.BlockSpec(blk, lambda c: (c, col0 + 2)),
            pl.BlockSpec((2, HGRN_WIDTH), lambda c: (0, 0)),
        ],
        out_specs=[
            pl.BlockSpec(blk, lambda c: (c, 0)),
            pl.BlockSpec((1, nh, hd, hd), lambda c: (c, 0, 0, 0)),
            pl.BlockSpec(blk, lambda c: (c, 0)),
        ],
        out_shape=[
            jax.ShapeDtypeStruct((s, HGRN_WIDTH), F32),
            jax.ShapeDtypeStruct((n_chunks, nh, hd, hd), F32),
            jax.ShapeDtypeStruct((s, nh * hd), F32),
        ],
        scratch_shapes=[
            pltpu.VMEM((nh, hd, hd), F32),
            pltpu.VMEM(blk, F32),
            pltpu.VMEM(blk, F32),
            pltpu.VMEM(blk, F32),
            pltpu.VMEM((nh, c_len, hd), F32),
        ],
        compiler_params=_params(dimension_semantics=("arbitrary",)),
    )(proj, proj, proj, lb)


def hgrn_bwd(proj, lb, d_o, states, a_mat):
    s = proj.shape[0]
    c_len, nh, hd = HGRN_CHUNK, HGRN_HEADS, HGRN_HEAD_DIM
    n_chunks = s // c_len
    col0 = (3 * ATTN_WIDTH) // HGRN_WIDTH
    last = n_chunks - 1

    def body(q_ref, f_ref, i_ref, lb_ref, do_ref, st_in_ref, a_in_ref, dq_ref, df_ref, di_ref, dlb_ref,
             dst_ref, b_ref, qf_ref, kf_ref, da_ref, dqi_ref, dki_ref):
        @pl.when(pl.program_id(0) == 0)
        def _():
            dst_ref[...] = jnp.zeros_like(dst_ref)
            dlb_ref[...] = jnp.zeros_like(dlb_ref)

        lbv = _lower_bound(lb_ref[...])
        q = q_ref[...]
        sq, qf, sig, f, kf = _hgrn_gates(q, f_ref[...], lbv)
        b = _tri_sum(_tril_bf16(c_len), jnp.log(f))
        b_ref[...] = b
        qf_ref[...] = qf
        kf_ref[...] = kf
        b_last = b[c_len - 1:c_len, :]
        eb = jnp.exp(b)
        ebl = jnp.exp(b_last - b)
        qb = qf * eb
        kb2 = kf * ebl
        vf = i_ref[...]
        d_o = do_ref[...]
        qb_b, kb2_b, vf_b, do_b = qb.astype(BF16), kb2.astype(BF16), vf.astype(BF16), d_o.astype(BF16)
        tq = lax.broadcasted_iota(jnp.int32, (c_len, hd), 0)
        lane = lax.broadcasted_iota(jnp.int32, (c_len, hd), 1)

        dqb_parts, dvf_parts, dkb2_parts, dbl_parts = [], [], [], []
        for h in range(nh):
            hs = slice(h * hd, (h + 1) * hd)
            st = st_in_ref[0, h]
            dst = dst_ref[h]
            st_b, dst_b = st.astype(BF16), dst.astype(BF16)
            a_h = a_in_ref[:, hs][:, :c_len].astype(BF16)
            dqb_parts.append(_dot(do_b[:, hs], st_b))
            dvf_parts.append(_dot_tn(a_h, do_b[:, hs]) + _dot_nt(kb2_b[:, hs], dst_b))
            dkb2_parts.append(_dot(vf_b[:, hs], dst_b))
            da = _dot_nt(do_b[:, hs], vf_b[:, hs])
            da = jnp.concatenate([da, jnp.zeros((c_len, hd - c_len), F32)], axis=1)
            da_ref[h] = jnp.where(tq >= lane, da, 0.0)
            dbl_parts.append(jnp.sum(dst * st, axis=0, keepdims=True) * jnp.exp(b_last[:, hs]))
            dst_ref[h] = dst * jnp.exp(b_last[:, hs]) + _dot_tn(do_b[:, hs], qb_b[:, hs])
        dqb = jnp.concatenate(dqb_parts, axis=1)
        dvf = jnp.concatenate(dvf_parts, axis=1)
        dkb2 = jnp.concatenate(dkb2_parts, axis=1)
        dbl = jnp.concatenate(dbl_parts, axis=1) + jnp.sum(dkb2 * kb2, axis=0, keepdims=True)

        dqi_ref[...] = jnp.zeros_like(dqi_ref)
        t_idx = lax.broadcasted_iota(jnp.int32, (c_len, nh * hd), 0)

        for r0 in range(0, c_len, SUBLANES):
            rows = slice(r0, c_len)
            nrow = c_len - r0

            def column(jj, carry, r0=r0, rows=rows, nrow=nrow):
                j = r0 + jj
                bj = b_ref[pl.ds(j, 1), :]
                kj = kf_ref[pl.ds(j, 1), :]
                e = jnp.exp(jnp.where(t_idx[rows] >= j, b_ref[rows, :] - bj, NEG_BIG))
                cols = [jnp.sum(jnp.where(lane[rows] == j, da_ref[h, rows, :], 0.0), axis=-1, keepdims=True)
                        for h in range(nh)]
                w = e * jnp.concatenate([jnp.broadcast_to(cc, (nrow, hd)) for cc in cols], axis=1)
                dqi_ref[rows, :] += w * kj
                dki_ref[pl.ds(j, 1), :] = jnp.sum(w * qf_ref[rows, :], axis=0, keepdims=True)
                return carry

            lax.fori_loop(0, SUBLANES, column, 0, unroll=COLUMN_UNROLL)
        dq_intra = dqi_ref[...]
        dk_intra = dki_ref[...]

        db = dqb * qb + qf * dq_intra - kf * dk_intra - dkb2 * kb2
        db = db + jnp.where(t_idx == c_len - 1, dbl, 0.0)
        dg = _tri_sum(_tril_bf16(c_len, upper=True), db)
        dqf = dqb * eb + dq_intra
        dkf = dkb2 * ebl + dk_intra
        dq_ref[...] = dqf * (sq * (1.0 + q * (1.0 - sq)))
        dfv = dg / f - dkf
        df_ref[...] = dfv * (1.0 - lbv) * sig * (1.0 - sig)
        di_ref[...] = dvf
        dlb_ref[...] += jnp.sum(dfv * (1.0 - sig), axis=0, keepdims=True)

    blk = (c_len, HGRN_WIDTH)
    rev = lambda c: last - c
    return pl.pallas_call(
        body,
        name="hgrn_bwd",
        grid=(n_chunks,),
        in_specs=[
            pl.BlockSpec(blk, lambda c: (rev(c), col0)),
            pl.BlockSpec(blk, lambda c: (rev(c), col0 + 1)),
            pl.BlockSpec(blk, lambda c: (rev(c), col0 + 2)),
            pl.BlockSpec((2, HGRN_WIDTH), lambda c: (0, 0)),
            pl.BlockSpec(blk, lambda c: (rev(c), 0)),
            pl.BlockSpec((1, nh, hd, hd), lambda c: (rev(c), 0, 0, 0)),
            pl.BlockSpec(blk, lambda c: (rev(c), 0)),
        ],
        out_specs=[
            pl.BlockSpec(blk, lambda c: (rev(c), 0)),
            pl.BlockSpec(blk, lambda c: (rev(c), 0)),
            pl.BlockSpec(blk, lambda c: (rev(c), 0)),
            pl.BlockSpec((1, HGRN_WIDTH), lambda c: (0, 0)),
        ],
        out_shape=[jax.ShapeDtypeStruct((s, HGRN_WIDTH), F32)] * 3 + [jax.ShapeDtypeStruct((1, HGRN_WIDTH), F32)],
        scratch_shapes=[
            pltpu.VMEM((nh, hd, hd), F32),
            pltpu.VMEM(blk, F32),
            pltpu.VMEM(blk, F32),
            pltpu.VMEM(blk, F32),
            pltpu.VMEM((nh, c_len, hd), F32),
            pltpu.VMEM(blk, F32),
            pltpu.VMEM(blk, F32),
        ],
        compiler_params=_params(dimension_semantics=("arbitrary",)),
    )(proj, proj, proj, lb, d_o, states, a_mat)


def _row_spec(tm, width, col=0):
    return pl.BlockSpec((tm, width), lambda i: (i, col))


def _const_spec(width):
    return pl.BlockSpec((1, width), lambda i: (0, 0))


def _acc_rows(ref, value):
    @pl.when(pl.program_id(0) == 0)
    def _():
        ref[...] = jnp.zeros_like(ref)

    ref[...] += jnp.sum(value, axis=0, keepdims=True)


def mix_fwd(attn_parts, o_h, proj, an, hn, w_out_b, gp, x):
    s = x.shape[0]
    tm = TOKEN_TILE
    gate_col = IN_PROJ_WIDTH // HGRN_WIDTH - 1
    hd = HGRN_HEAD_DIM

    def body(o1, o2, o3, l1, l2, l3, oh_ref, gate_ref, an_ref, hn_ref, w_ref, gp_ref, x_ref,
             x1_ref, cat_ref, mixed_ref, attn_ref, lse_ref):
        ls = [l1[...], l2[...], l3[...]]
        m = jnp.maximum(jnp.maximum(ls[0], ls[1]), ls[2])
        es = [jnp.exp(l - m) for l in ls]
        den = es[0] + es[1] + es[2]
        attn = (es[0] * o1[...] + es[1] * o2[...] + es[2] * o3[...]) / den
        attn_ref[...] = attn
        lse_ref[...] = m + jnp.log(den)
        cat_ref[:, :ATTN_WIDTH] = _rms_fwd(attn, an_ref[...], ATTN_WIDTH).astype(BF16)
        gate = gate_ref[...]
        silu_g = gate * _sigmoid(gate)
        for h in range(HGRN_HEADS):
            hs = slice(h * hd, (h + 1) * hd)
            rec = _rms_fwd(oh_ref[:, hs], hn_ref[:, hs], hd) * silu_g[:, hs]
            cat_ref[:, ATTN_WIDTH + h * hd:ATTN_WIDTH + (h + 1) * hd] = rec.astype(BF16)
        mixed = _dot(cat_ref[...], w_ref[...])
        mixed_ref[...] = mixed
        x1_ref[...] = x_ref[...] + _rms_fwd(mixed, gp_ref[...], D_MODEL)

    aw = ATTN_WIDTH
    return pl.pallas_call(
        body,
        name="mix_fwd",
        grid=(s // tm,),
        in_specs=[_row_spec(tm, aw)] * 7 + [_row_spec(tm, aw, gate_col), _const_spec(aw), _const_spec(aw), _vmem_spec(),
                                            _const_spec(D_MODEL), _row_spec(tm, D_MODEL)],
        out_specs=[_row_spec(tm, D_MODEL), _row_spec(tm, D_MODEL), _row_spec(tm, D_MODEL), _row_spec(tm, aw),
                   _row_spec(tm, aw)],
        out_shape=[
            jax.ShapeDtypeStruct((s, D_MODEL), F32),
            jax.ShapeDtypeStruct((s, D_MODEL), BF16),
            jax.ShapeDtypeStruct((s, D_MODEL), F32),
            jax.ShapeDtypeStruct((s, aw), F32),
            jax.ShapeDtypeStruct((s, aw), F32),
        ],
        compiler_params=_params(dimension_semantics=("arbitrary",)),
    )(*[p[0] for p in attn_parts], *[p[1] for p in attn_parts], o_h, proj, an, hn, w_out_b, gp, x)


def mix_bwd(dx1, mixed, gp, w_out_b, attn, an, o_h, proj, hn):
    s = dx1.shape[0]
    tm = TOKEN_TILE
    gate_col = IN_PROJ_WIDTH // HGRN_WIDTH - 1
    hd = HGRN_HEAD_DIM
    aw = ATTN_WIDTH

    def body(dx1_ref, mixed_ref, gp_ref, w_ref, attn_ref, an_ref, oh_ref, gate_ref, hn_ref,
             dmix_ref, do_ref, delta_ref, doh_ref, dgate_ref, dgp_ref, dan_ref, dhn_ref):
        dmixed, gp_c = _rms_bwd(dx1_ref[...], mixed_ref[...], gp_ref[...], D_MODEL)
        _acc_rows(dgp_ref, gp_c)
        dmixed_b = dmixed.astype(BF16)
        dmix_ref[...] = dmixed_b
        dcat = _dot_nt(dmixed_b, w_ref[...])
        attn = attn_ref[...]
        d_o, an_c = _rms_bwd(dcat[:, :aw], attn, an_ref[...], aw)
        _acc_rows(dan_ref, an_c)
        do_ref[...] = d_o
        prod = d_o * attn
        for pair in range(ATTN_HEADS // 2):
            lanes = slice(pair * 128, (pair + 1) * 128)
            pp = prod[:, lanes]
            low = _lane_half((tm, 128), 0)
            lo = jnp.sum(jnp.where(low, pp, 0.0), axis=-1, keepdims=True)
            hi = jnp.sum(jnp.where(low, 0.0, pp), axis=-1, keepdims=True)
            delta_ref[:, lanes] = jnp.where(low, lo, hi)
        gate = gate_ref[...]
        sg = _sigmoid(gate)
        silu_g = gate * sg
        drec = dcat[:, aw:]
        hn_parts = []
        for h in range(HGRN_HEADS):
            hs = slice(h * hd, (h + 1) * hd)
            oh = oh_ref[:, hs]
            on = _rms_fwd(oh, hn_ref[:, hs], hd)
            dgate_ref[:, hs] = drec[:, hs] * on * (sg[:, hs] * (1.0 + gate[:, hs] * (1.0 - sg[:, hs])))
            d_oh, hn_c = _rms_bwd(drec[:, hs] * silu_g[:, hs], oh, hn_ref[:, hs], hd)
            doh_ref[:, hs] = d_oh
            hn_parts.append(hn_c)
        _acc_rows(dhn_ref, jnp.concatenate(hn_parts, axis=1))

    return pl.pallas_call(
        body,
        name="mix_bwd",
        grid=(s // tm,),
        in_specs=[_row_spec(tm, D_MODEL), _row_spec(tm, D_MODEL), _const_spec(D_MODEL), _vmem_spec(), _row_spec(tm, aw),
                  _const_spec(aw), _row_spec(tm, aw), _row_spec(tm, aw, gate_col), _const_spec(aw)],
        out_specs=[_row_spec(tm, D_MODEL)] + [_row_spec(tm, aw)] * 4 + [_const_spec(D_MODEL), _const_spec(aw),
                                                                        _const_spec(aw)],
        out_shape=[jax.ShapeDtypeStruct((s, D_MODEL), BF16)] + [jax.ShapeDtypeStruct((s, aw), F32)] * 4 + [
            jax.ShapeDtypeStruct((1, D_MODEL), F32), jax.ShapeDtypeStruct((1, aw), F32),
            jax.ShapeDtypeStruct((1, aw), F32)],
        compiler_params=_params(dimension_semantics=("arbitrary",)),
    )(dx1, mixed, gp, w_out_b, attn, an, o_h, proj, hn)


def mlp_fwd_bwd(x1, g_pre, w1_blocks, w2_b, g_post, target):
    s = x1.shape[0]
    tm = TOKEN_TILE
    nblk, _, fb = w1_blocks.shape

    def body(x1_ref, gpre_ref, w1_ref, w2_ref, gpost_ref, t_ref,
             dx1_ref, h2_ref, a_ref, du_ref, dff_ref, loss_ref, dgpre_ref, dgpost_ref, u_ref):
        x1v = x1_ref[...]
        h2 = _rms_fwd(x1v, gpre_ref[...], D_MODEL).astype(BF16)
        h2_ref[...] = h2
        ff = jnp.zeros((tm, D_MODEL), F32)
        for j in range(nblk):
            cols = slice(j * fb, (j + 1) * fb)
            ru = jnp.maximum(_dot(h2, w1_ref[j]), 0.0)
            u_ref[:, cols] = ru
            a = (ru * ru).astype(BF16)
            a_ref[:, cols] = a
            ff = ff + _dot(a, w2_ref[cols, :])
        diff = x1v + _rms_fwd(ff, gpost_ref[...], D_MODEL) - t_ref[...]
        _acc_rows(loss_ref, diff * diff)
        dy = diff * (1.0 / D_MODEL)
        dff, gpost_c = _rms_bwd(dy, ff, gpost_ref[...], D_MODEL)
        _acc_rows(dgpost_ref, gpost_c)
        dff_b = dff.astype(BF16)
        dff_ref[...] = dff_b
        dh2 = jnp.zeros((tm, D_MODEL), F32)
        for j in range(nblk):
            cols = slice(j * fb, (j + 1) * fb)
            du = (_dot_nt(dff_b, w2_ref[cols, :]) * (2.0 * u_ref[:, cols])).astype(BF16)
            du_ref[:, cols] = du
            dh2 = dh2 + _dot_nt(du, w1_ref[j])
        dxa, gpre_c = _rms_bwd(dh2, x1v, gpre_ref[...], D_MODEL)
        _acc_rows(dgpre_ref, gpre_c)
        dx1_ref[...] = dy + dxa

    dm = D_MODEL
    return pl.pallas_call(
        body,
        name="mlp_fwd_bwd",
        grid=(s // tm,),
        in_specs=[_row_spec(tm, dm), _const_spec(dm), _vmem_spec(), _vmem_spec(), _const_spec(dm), _row_spec(tm, dm)],
        out_specs=[_row_spec(tm, dm), _row_spec(tm, dm), _row_spec(tm, D_FF), _row_spec(tm, D_FF), _row_spec(tm, dm),
                   _const_spec(dm), _const_spec(dm), _const_spec(dm)],
        out_shape=[
            jax.ShapeDtypeStruct((s, dm), F32),
            jax.ShapeDtypeStruct((s, dm), BF16),
            jax.ShapeDtypeStruct((s, D_FF), BF16),
            jax.ShapeDtypeStruct((s, D_FF), BF16),
            jax.ShapeDtypeStruct((s, dm), BF16),
            jax.ShapeDtypeStruct((1, dm), F32),
            jax.ShapeDtypeStruct((1, dm), F32),
            jax.ShapeDtypeStruct((1, dm), F32),
        ],
        scratch_shapes=[pltpu.VMEM((tm, D_FF), F32)],
        compiler_params=_params(dimension_semantics=("arbitrary",)),
    )(x1, g_pre, w1_blocks, w2_b, g_post, target)


def in_proj_bwd(attn_grads, hgrn_grads, dgate, w_in_b, x, g1, dx1):
    s = x.shape[0]
    tm = TOKEN_TILE
    aw = ATTN_WIDTH
    n_attn = len(attn_grads)
    flat = [g[k] for k in range(3) for g in attn_grads] + list(hgrn_grads) + [dgate]

    def body(*refs):
        parts = refs[:len(flat)]
        w_ref, x_ref, g_ref, dx1_ref, dx_ref, dproj_ref, dg_ref = refs[len(flat):]
        groups = []
        for k in range(3):
            acc = parts[k * n_attn][...]
            for p in parts[k * n_attn + 1:(k + 1) * n_attn]:
                acc = acc + p[...]
            groups.append(acc)
        groups += [p[...] for p in parts[3 * n_attn:]]
        dh = jnp.zeros((tm, D_MODEL), F32)
        for gi, grp in enumerate(groups):
            cols = slice(gi * aw, (gi + 1) * aw)
            gb = grp.astype(BF16)
            dproj_ref[:, cols] = gb
            dh = dh + _dot_nt(gb, w_ref[:, cols])
        dxa, g_c = _rms_bwd(dh, x_ref[...], g_ref[...], D_MODEL)
        _acc_rows(dg_ref, g_c)
        dx_ref[...] = dx1_ref[...] + dxa

    dm = D_MODEL
    return pl.pallas_call(
        body,
        name="in_proj_bwd",
        grid=(s // tm,),
        in_specs=[_row_spec(tm, aw)] * len(flat) + [_vmem_spec(), _row_spec(tm, dm), _const_spec(dm), _row_spec(tm, dm)],
        out_specs=[_row_spec(tm, dm), _row_spec(tm, IN_PROJ_WIDTH), _const_spec(dm)],
        out_shape=[jax.ShapeDtypeStruct((s, dm), F32), jax.ShapeDtypeStruct((s, IN_PROJ_WIDTH), BF16),
                   jax.ShapeDtypeStruct((1, dm), F32)],
        compiler_params=_params(dimension_semantics=("arbitrary",)),
    )(*flat, w_in_b, x, g1, dx1)


def wgrad(a_b, b_b, tn, name):
    s, k = a_b.shape
    n = b_b.shape[1]
    ts = 512

    def body(a_ref, b_ref, o_ref):
        @pl.when(pl.program_id(1) == 0)
        def _():
            o_ref[...] = jnp.zeros_like(o_ref)

        o_ref[0] += _dot_tn(a_ref[...], b_ref[...])

    return pl.pallas_call(
        body,
        name=name,
        grid=(n // tn, s // ts),
        in_specs=[pl.BlockSpec((ts, k), lambda j, i: (i, 0)), pl.BlockSpec((ts, tn), lambda j, i: (i, j))],
        out_specs=pl.BlockSpec((1, k, tn), lambda j, i: (j, 0, 0)),
        out_shape=jax.ShapeDtypeStruct((n // tn, k, tn), F32),
        compiler_params=_params(dimension_semantics=("arbitrary", "arbitrary")),
    )(a_b, b_b)


def local_step(x, target, g1, w_in_b, an, lb, hn, w_out_b, gp, g_pre, w1_blocks, w2_b, g_post):
    proj, h_b = in_proj_fwd(x, g1, w_in_b)
    attn_parts = [attn_fwd(proj, d) for d in DILATIONS]
    o_h, states, a_mat = hgrn_fwd(proj, lb)
    x1, cat_b, mixed, attn, lse = mix_fwd(attn_parts, o_h, proj, an, hn, w_out_b, gp, x)
    dx1, h2_b, a_b, du_b, dff_b, loss_vec, dg_pre, dg_post = mlp_fwd_bwd(x1, g_pre, w1_blocks, w2_b, g_post, target)
    dw2 = wgrad(a_b, dff_b, D_MODEL, "wgrad_ff2")
    dw1 = wgrad(h2_b, du_b, D_FF // N_DEV, "wgrad_ff1")
    dmix_b, d_o, delta, d_oh, dgate, dgp, dan, dhn = mix_bwd(dx1, mixed, gp, w_out_b, attn, an, o_h, proj, hn)
    dwout = wgrad(cat_b, dmix_b, D_MODEL, "wgrad_out")
    attn_grads = [attn_bwd(proj, d_o, lse, delta, d) for d in DILATIONS]
    hgrn_grads = hgrn_bwd(proj, lb, d_oh, states, a_mat)
    dlb = hgrn_grads[3]
    dx, dproj_b, dg1 = in_proj_bwd(attn_grads, hgrn_grads[:3], dgate, w_in_b, x, g1, dx1)
    dwin = wgrad(h_b, dproj_b, 2 * IN_PROJ_WIDTH // N_DEV, "wgrad_in")
    small = dict(dg1=dg1, dan=dan, dlb=dlb, dhn=dhn, dgp=dgp, dg_pre=dg_pre, dg_post=dg_post, loss_vec=loss_vec)
    return dx, dwin, dwout, dw1, dw2, small


def _position():
    x, y, c = lax.axis_index("x"), lax.axis_index("y"), lax.axis_index("c")
    other_chips = [(1 - x, y), (x, 1 - y), (1 - x, 1 - y)]
    return x, y, c, other_chips


def _any_spec():
    return pl.BlockSpec(memory_space=pl.ANY)


def all_gather(shards):
    n = len(shards)

    def body(*refs):
        ins, outs = refs[:n], refs[n:2 * n]
        send_sems, recv_sems, local_sems = refs[2 * n:]
        x, y, c, chips = _position()
        me, sibling = (x, y, c), (x, y, 1 - c)

        def slot(a, px, py, pc):
            return outs[a].at[4 * px + 2 * py + pc]

        def copy(a, k, block, to, src=None):
            return pltpu.make_async_remote_copy(
                src_ref=slot(a, *block) if src is None else src, dst_ref=slot(a, *block),
                send_sem=send_sems.at[a, k], recv_sem=recv_sems.at[a, k], device_id=to, device_id_type=MESH)

        local = [pltpu.make_async_copy(ins[a], slot(a, *me), local_sems.at[a]) for a in range(n)]
        for cp in local:
            cp.start()
        first = []
        for a in range(n):
            first.append(copy(a, 0, me, sibling, src=ins[a]))
            first += [copy(a, 1 + j, me, (*chip, c), src=ins[a]) for j, chip in enumerate(chips)]
        for cp in first:
            cp.start()
        passed = []
        for j, chip in enumerate(chips):
            for a in range(n):
                copy(a, 1 + j, (*chip, c), me).wait_recv()
                fwd = copy(a, 4 + j, (*chip, c), sibling)
                fwd.start()
                passed.append(fwd)
        for a in range(n):
            copy(a, 0, sibling, me).wait_recv()
            for j, chip in enumerate(chips):
                copy(a, 4 + j, (*chip, 1 - c), me).wait_recv()
        for cp in first + passed:
            cp.wait_send()
        for cp in local:
            cp.wait()

    return pl.pallas_call(
        body,
        name="all_gather_weights",
        in_specs=[_any_spec()] * n,
        out_specs=[_any_spec()] * n,
        out_shape=[jax.ShapeDtypeStruct((N_DEV,) + sh.shape, sh.dtype) for sh in shards],
        scratch_shapes=[pltpu.SemaphoreType.DMA((n, 7)), pltpu.SemaphoreType.DMA((n, 7)), pltpu.SemaphoreType.DMA((n,))],
    )(*shards)


def reduce_to_core(grads):
    n = len(grads)

    def body(*refs):
        ins, outs = refs[:n], refs[n:2 * n]
        send_sems, recv_sems = refs[2 * n:]
        x, y, c, _ = _position()
        sibling = (x, y, 1 - c)
        copies = []
        for a in range(n):
            for q in range(4):
                copies.append(pltpu.make_async_remote_copy(
                    src_ref=ins[a].at[2 * q + (1 - c)], dst_ref=outs[a].at[q],
                    send_sem=send_sems.at[a, q], recv_sem=recv_sems.at[a, q], device_id=sibling, device_id_type=MESH))
        for cp in copies:
            cp.start()
        for cp in copies:
            cp.wait()

    return pl.pallas_call(
        body,
        name="reduce_to_core",
        in_specs=[_any_spec()] * n,
        out_specs=[_any_spec()] * n,
        out_shape=[jax.ShapeDtypeStruct((4,) + g.shape[1:], g.dtype) for g in grads],
        scratch_shapes=[pltpu.SemaphoreType.DMA((n, 4)), pltpu.SemaphoreType.DMA((n, 4))],
    )(*grads)


def pair_sum(grad, from_sibling, name):
    _, r, cdim = grad.shape
    tr = min(r, 256)
    c_idx = lax.axis_index("c").astype(jnp.int32).reshape(1)

    def body(c_ref, g_ref, s_ref, o_ref):
        o_ref[...] = g_ref[...] + s_ref[...]

    return pl.pallas_call(
        body,
        name=name,
        grid_spec=pltpu.PrefetchScalarGridSpec(
            num_scalar_prefetch=1,
            grid=(4, r // tr),
            in_specs=[pl.BlockSpec((1, tr, cdim), lambda q, i, cr: (2 * q + cr[0], i, 0)),
                      pl.BlockSpec((1, tr, cdim), lambda q, i, cr: (q, i, 0))],
            out_specs=pl.BlockSpec((1, tr, cdim), lambda q, i, cr: (q, i, 0)),
        ),
        out_shape=jax.ShapeDtypeStruct((4, r, cdim), grad.dtype),
        compiler_params=_params(dimension_semantics=("arbitrary", "arbitrary")),
    )(c_idx, grad, from_sibling)


def reduce_to_chip(pairs):
    n = len(pairs)

    def body(*refs):
        ins, outs = refs[:n], refs[n:2 * n]
        send_sems, recv_sems = refs[2 * n:]
        x, y, c, chips = _position()
        copies = []
        for a in range(n):
            for j, (px, py) in enumerate(chips):
                copies.append(pltpu.make_async_remote_copy(
                    src_ref=ins[a].at[2 * px + py], dst_ref=outs[a].at[j],
                    send_sem=send_sems.at[a, j], recv_sem=recv_sems.at[a, j], device_id=(px, py, c), device_id_type=MESH))
        for cp in copies:
            cp.start()
        for cp in copies:
            cp.wait()

    return pl.pallas_call(
        body,
        name="reduce_to_chip",
        in_specs=[_any_spec()] * n,
        out_specs=[_any_spec()] * n,
        out_shape=[jax.ShapeDtypeStruct((3,) + p.shape[1:], p.dtype) for p in pairs],
        scratch_shapes=[pltpu.SemaphoreType.DMA((n, 3)), pltpu.SemaphoreType.DMA((n, 3))],
    )(*pairs)


def _adamw(w, g, m, v):
    m = ADAM_B1 * m + (1.0 - ADAM_B1) * g
    v = ADAM_B2 * v + (1.0 - ADAM_B2) * (g * g)
    m_hat = m / (1.0 - ADAM_B1 ** ADAM_STEP)
    v_hat = v / (1.0 - ADAM_B2 ** ADAM_STEP)
    delta = -ADAM_LR * (m_hat / (jnp.sqrt(v_hat) + ADAM_EPS) + ADAM_WD * w)
    return delta, m, v


def sum_adamw(pairs, others, w, m, v, name):
    r, cdim = w.shape
    tr = min(r, 256)
    chip_idx = (2 * lax.axis_index("x") + lax.axis_index("y")).astype(jnp.int32).reshape(1)

    def body(q_ref, p_ref, o_ref, w_ref, m_ref, v_ref, g_out, d_out, m_out, v_out):
        g = p_ref[0] + o_ref[0] + o_ref[1] + o_ref[2]
        g_out[...] = g
        d_out[...], m_out[...], v_out[...] = _adamw(w_ref[...], g, m_ref[...], v_ref[...])

    tile = lambda: pl.BlockSpec((tr, cdim), lambda i, qr: (i, 0))
    return pl.pallas_call(
        body,
        name=name,
        grid_spec=pltpu.PrefetchScalarGridSpec(
            num_scalar_prefetch=1,
            grid=(r // tr,),
            in_specs=[pl.BlockSpec((1, tr, cdim), lambda i, qr: (qr[0], i, 0)),
                      pl.BlockSpec((3, tr, cdim), lambda i, qr: (0, i, 0)), tile(), tile(), tile()],
            out_specs=[tile(), tile(), tile(), tile()],
        ),
        out_shape=[jax.ShapeDtypeStruct((r, cdim), F32)] * 4,
        compiler_params=_params(dimension_semantics=("arbitrary",)),
    )(chip_idx, pairs, others, w, m, v)


SMALL_ROWS = 8


def small_all_reduce(packed):
    shape = packed.shape

    def body(in_ref, out_ref, recv_ref, send_sems, recv_sems):
        x, y, c, _ = _position()
        my_id = 4 * x + 2 * y + c
        recv_ref[my_id] = in_ref[...]
        copies = []
        for rel in range(1, N_DEV):
            fx, fy, fc = (rel >> 2) & 1, (rel >> 1) & 1, rel & 1
            px = 1 - x if fx else x
            py = 1 - y if fy else y
            pc = 1 - c if fc else c
            cp = pltpu.make_async_remote_copy(
                src_ref=in_ref, dst_ref=recv_ref.at[my_id], send_sem=send_sems.at[rel - 1],
                recv_sem=recv_sems.at[rel - 1], device_id=(px, py, pc), device_id_type=MESH)
            cp.start()
            copies.append((cp, pltpu.make_async_remote_copy(
                src_ref=in_ref, dst_ref=recv_ref.at[4 * px + 2 * py + pc], send_sem=send_sems.at[rel - 1],
                recv_sem=recv_sems.at[rel - 1], device_id=(px, py, pc), device_id_type=MESH)))
        for cp, landing in copies:
            landing.wait_recv()
        for cp, landing in copies:
            cp.wait_send()
        total = recv_ref[0]
        for k in range(1, N_DEV):
            total = total + recv_ref[k]
        out_ref[...] = total

    return pl.pallas_call(
        body,
        name="small_all_reduce",
        in_specs=[_vmem_spec()],
        out_specs=_vmem_spec(),
        out_shape=jax.ShapeDtypeStruct(shape, F32),
        scratch_shapes=[pltpu.VMEM((N_DEV,) + shape, F32), pltpu.SemaphoreType.DMA((N_DEV - 1,)),
                        pltpu.SemaphoreType.DMA((N_DEV - 1,))],
    )(packed)


def small_adamw(reduced, w, m, v):
    def body(r_ref, w_ref, m_ref, v_ref, g_out, d_out, m_out, v_out, loss_out):
        red = r_ref[...]
        wv = w_ref[...]
        lb = _lower_bound(jnp.concatenate([wv[5:6, :HGRN_WIDTH], wv[5:6, HGRN_WIDTH:]], axis=0))
        t = red[5:6, :HGRN_WIDTH] * lb * (1.0 - lb)
        row = lax.broadcasted_iota(jnp.int32, red.shape, 0)
        g = jnp.where(row == 5, jnp.concatenate([t, -t], axis=1), jnp.where(row >= 6, 0.0, red))
        g_out[...] = g
        d_out[...], m_out[...], v_out[...] = _adamw(wv, g, m_ref[...], v_ref[...])
        loss = jnp.sum(red[6:7, :], axis=-1, keepdims=True) * (0.5 / D_MODEL)
        loss_out[...] = jnp.broadcast_to(loss, loss_out.shape)

    return pl.pallas_call(
        body,
        name="small_adamw",
        in_specs=[_vmem_spec()] * 4,
        out_specs=[_vmem_spec()] * 5,
        out_shape=[jax.ShapeDtypeStruct(reduced.shape, F32)] * 4 + [jax.ShapeDtypeStruct((8, 128), F32)],
    )(reduced, w, m, v)


def _pack_small(g1, gp, g_pre, g_post, an, hn, logits_or_dlb, extra=None):
    row5 = logits_or_dlb.reshape(1, -1)
    row5 = jnp.pad(row5, ((0, 0), (0, D_MODEL - row5.shape[1])))
    row6 = jnp.zeros((1, D_MODEL), F32) if extra is None else extra
    return jnp.concatenate([g1, gp, g_pre, g_post, jnp.concatenate([an, hn], axis=1), row5, row6,
                            jnp.zeros((1, D_MODEL), F32)], axis=0)


def _unpack_small(p):
    return dict(mix_pre_norm=p[0:1], mix_post_norm=p[1:2], mlp_pre_norm=p[2:3], mlp_post_norm=p[3:4],
                attn_out_norm=p[4:5, :ATTN_WIDTH], hgrn_out_norm=p[4:5, ATTN_WIDTH:],
                hgrn_lb_logits=p[5].reshape(2, HGRN_WIDTH))


BIG = ("w_in", "w_out", "w_ff1", "w_ff2")
ORDER = ("mix_pre_norm", "w_in", "attn_out_norm", "hgrn_lb_logits", "hgrn_out_norm", "w_out", "mix_post_norm",
         "mlp_pre_norm", "w_ff1", "w_ff2", "mlp_post_norm")


def kernel(x, mix_pre_norm, w_in, attn_out_norm, hgrn_lb_logits, hgrn_out_norm, w_out, mix_post_norm, mlp_pre_norm, w_ff1, w_ff2, mlp_post_norm, loss_target, m_mix_pre_norm, m_w_in, m_attn_out_norm, m_hgrn_lb_logits, m_hgrn_out_norm, m_w_out, m_mix_post_norm, m_mlp_pre_norm, m_w_ff1, m_w_ff2, m_mlp_post_norm, v_mix_pre_norm, v_w_in, v_attn_out_norm, v_hgrn_lb_logits, v_hgrn_out_norm, v_w_out, v_mix_post_norm, v_mlp_pre_norm, v_w_ff1, v_w_ff2, v_mlp_post_norm):
    w = dict(w_in=w_in[0], w_out=w_out[0], w_ff1=w_ff1[0], w_ff2=w_ff2[0])
    m = dict(w_in=m_w_in[0], w_out=m_w_out[0], w_ff1=m_w_ff1[0], w_ff2=m_w_ff2[0])
    v = dict(w_in=v_w_in[0], w_out=v_w_out[0], w_ff1=v_w_ff1[0], w_ff2=v_w_ff2[0])

    gathered = all_gather([w[k].astype(BF16) for k in BIG])
    w_in_b = gathered[0].transpose(1, 0, 2).reshape(D_MODEL, IN_PROJ_WIDTH)
    w_out_b = gathered[1].reshape(D_MODEL, D_MODEL)
    w1_blocks = gathered[2]
    w2_b = gathered[3].reshape(D_FF, D_MODEL)

    dx, dwin, dwout, dw1, dw2, small = local_step(
        x[0], loss_target[0], mix_pre_norm, w_in_b, attn_out_norm, hgrn_lb_logits, hgrn_out_norm, w_out_b,
        mix_post_norm, mlp_pre_norm, w1_blocks, w2_b, mlp_post_norm)

    shard_w = IN_PROJ_WIDTH // N_DEV
    grads = [
        dwin.reshape(N_DEV // 2, D_MODEL, 2, shard_w).transpose(0, 2, 1, 3).reshape(N_DEV, D_MODEL, shard_w),
        dwout.reshape(N_DEV, D_MODEL // N_DEV, D_MODEL),
        dw1,
        dw2.reshape(N_DEV, D_FF // N_DEV, D_MODEL),
    ]
    from_sibling = reduce_to_core(grads)
    pairs = [pair_sum(g, s, f"pair_sum_{k}") for g, s, k in zip(grads, from_sibling, BIG)]
    others = reduce_to_chip(pairs)
    big = {k: sum_adamw(p, o, w[k], m[k], v[k], f"sum_adamw_{k}") for k, p, o in zip(BIG, pairs, others)}

    packed_g = _pack_small(small["dg1"], small["dgp"], small["dg_pre"], small["dg_post"], small["dan"], small["dhn"],
                           small["dlb"], small["loss_vec"])
    reduced = small_all_reduce(packed_g)
    pack = lambda a, b, c2, d, e, f, g: _pack_small(a, b, c2, d, e, f, g)
    w_s = pack(mix_pre_norm, mix_post_norm, mlp_pre_norm, mlp_post_norm, attn_out_norm, hgrn_out_norm, hgrn_lb_logits)
    m_s = pack(m_mix_pre_norm, m_mix_post_norm, m_mlp_pre_norm, m_mlp_post_norm, m_attn_out_norm, m_hgrn_out_norm,
               m_hgrn_lb_logits)
    v_s = pack(v_mix_pre_norm, v_mix_post_norm, v_mlp_pre_norm, v_mlp_post_norm, v_attn_out_norm, v_hgrn_out_norm,
               v_hgrn_lb_logits)
    g_s, d_s, nm_s, nv_s, loss = small_adamw(reduced, w_s, m_s, v_s)
    small_out = [_unpack_small(t) for t in (g_s, d_s, nm_s, nv_s)]

    outs = [loss[0, 0], dx[None]]
    for kind in range(4):
        for name in ORDER:
            outs.append(big[name][kind][None] if name in BIG else small_out[kind][name])
    return tuple(outs)
```

```python
import functools
import math

import jax
import jax.numpy as jnp
from jax import lax
from jax.experimental import pallas as pl
from jax.experimental.pallas import tpu as pltpu

F32 = jnp.float32
BF16 = jnp.bfloat16

D_MODEL = 1024
SEQ = 4096
ATTN_WIDTH = 512
ATTN_HEAD_DIM = 64
ATTN_HEADS = 8
ATTN_BLOCK = 128
DILATIONS = (1, 4, 16)
HGRN_WIDTH = 512
HGRN_HEADS = 4
HGRN_HEAD_DIM = 128
HGRN_CHUNK = 64
IN_PROJ_WIDTH = 3584
D_FF = 4096
RMS_EPS = 1e-6
N_DEV = 8
ADAM_LR = 0.001
ADAM_B1 = 0.9
ADAM_B2 = 0.999
ADAM_EPS = 1e-08
ADAM_WD = 0.01
ADAM_STEP = 10

SUBLANES = 8
LANES = 128
COLUMN_UNROLL = 4
TOKEN_TILE = 256
VMEM_LIMIT = 56 * 1024 * 1024
NEG_BIG = -1e30
MESH = pl.DeviceIdType.MESH


def _params(**kw):
    return pltpu.CompilerParams(vmem_limit_bytes=VMEM_LIMIT, **kw)


def _vmem_spec():
    return pl.BlockSpec(memory_space=pltpu.VMEM)


def _dot(a, b):
    return jnp.dot(a, b, preferred_element_type=F32)


def _dot_nt(a, b):
    return lax.dot_general(a, b, (((1,), (1,)), ((), ())), preferred_element_type=F32)


def _dot_tn(a, b):
    return lax.dot_general(a, b, (((0,), (0,)), ((), ())), preferred_element_type=F32)


def _sigmoid(x):
    return 1.0 / (1.0 + jnp.exp(-x))


def _rms_fwd(x, gain, width):
    r = lax.rsqrt(jnp.sum(x * x, axis=-1, keepdims=True) * (1.0 / width) + RMS_EPS)
    return x * r * gain


def _rms_bwd(dy, x, gain, width):
    r = lax.rsqrt(jnp.sum(x * x, axis=-1, keepdims=True) * (1.0 / width) + RMS_EPS)
    xhat = x * r
    dxhat = dy * gain
    dx = r * (dxhat - xhat * (jnp.sum(dxhat * xhat, axis=-1, keepdims=True) * (1.0 / width)))
    return dx, dy * xhat


def _split3(x):
    hi = x.astype(BF16)
    r1 = x - hi.astype(F32)
    mid = r1.astype(BF16)
    lo = (r1 - mid.astype(F32)).astype(BF16)
    return hi, mid, lo


def _tri_sum(tri_bf16, x):
    hi, mid, lo = _split3(x)
    return _dot(tri_bf16, hi) + _dot(tri_bf16, mid) + _dot(tri_bf16, lo)


def _dilated_spec(d, tm, width):
    return pl.BlockSpec((d, tm // d, width), lambda i: (0, i, 0))


def _lane_blocks(ref, value):
    for c in range(ref.shape[0]):
        ref[c] = value[:, c * LANES:(c + 1) * LANES]


def _to_dilated(src_ref, dst_ref, d, tm, cast=None):
    for r in range(d):
        for c in range(src_ref.shape[0]):
            v = src_ref[c] if d == 1 else src_ref[c, pl.ds(r, tm // d, stride=d), :]
            dst_ref[r, :, c * LANES:(c + 1) * LANES] = v if cast is None else v.astype(cast)


def _from_dilated(src_ref, scratch_ref, d, tm):
    if d == 1:
        return src_ref[0]
    nblk = scratch_ref.shape[0]
    for r in range(d):
        for c in range(nblk):
            scratch_ref[c, pl.ds(r, tm // d, stride=d), :] = src_ref[r, :, c * LANES:(c + 1) * LANES]
    return jnp.concatenate([scratch_ref[c] for c in range(nblk)], axis=1)


def in_proj_fwd(x, g1, w_in_b):
    s = x.shape[0]
    tm = TOKEN_TILE
    qkv_w = 3 * ATTN_WIDTH
    hg_w = IN_PROJ_WIDTH - qkv_w

    def body(x_ref, g_ref, w_ref, hg_ref, h_ref, *rest):
        qkv_refs, qkv_scr = rest[:len(DILATIONS)], rest[len(DILATIONS)]
        h = _rms_fwd(x_ref[...], g_ref[...], D_MODEL).astype(BF16)
        h_ref[...] = h
        proj = _dot(h, w_ref[...])
        hg_ref[...] = proj[:, qkv_w:]
        _lane_blocks(qkv_scr, proj[:, :qkv_w])
        for d, ref in zip(DILATIONS, qkv_refs):
            _to_dilated(qkv_scr, ref, d, tm, cast=BF16)

    return pl.pallas_call(
        body,
        name="in_proj_fwd",
        grid=(s // tm,),
        in_specs=[
            pl.BlockSpec((tm, D_MODEL), lambda i: (i, 0)),
            pl.BlockSpec((1, D_MODEL), lambda i: (0, 0)),
            _vmem_spec(),
        ],
        out_specs=[
            pl.BlockSpec((tm, hg_w), lambda i: (i, 0)),
            pl.BlockSpec((tm, D_MODEL), lambda i: (i, 0)),
        ] + [_dilated_spec(d, tm, qkv_w) for d in DILATIONS],
        out_shape=[jax.ShapeDtypeStruct((s, hg_w), F32), jax.ShapeDtypeStruct((s, D_MODEL), BF16)] + [
            jax.ShapeDtypeStruct((d, s // d, qkv_w), BF16) for d in DILATIONS],
        scratch_shapes=[pltpu.VMEM((qkv_w // LANES, tm, LANES), F32)],
        compiler_params=_params(dimension_semantics=("arbitrary",)),
    )(x, g1, w_in_b)


def _attn_scores(qm, kcat, head, dilation, first_block):
    s = _dot_nt(qm, kcat) * (ATTN_HEAD_DIM ** -0.5)
    qi = lax.broadcasted_iota(jnp.int32, (ATTN_BLOCK, 2 * ATTN_BLOCK), 0)
    kj = lax.broadcasted_iota(jnp.int32, (ATTN_BLOCK, 2 * ATTN_BLOCK), 1)
    dist = qi + ATTN_BLOCK - kj
    valid = (dist >= 0) & (dist <= ATTN_BLOCK) & ((kj >= ATTN_BLOCK) | jnp.logical_not(first_block))
    slope = 2.0 ** (-8.0 * (head + 1) / ATTN_HEADS)
    bias = dist.astype(F32) * (-slope * dilation)
    return jnp.where(valid, s + bias, NEG_BIG)


def _lane_half(shape, sub):
    lane = lax.broadcasted_iota(jnp.int32, shape, 1)
    return (lane < ATTN_HEAD_DIM) if sub == 0 else (lane >= ATTN_HEAD_DIM)


def _sub_block(col, row):
    return pl.BlockSpec((None, ATTN_BLOCK, ATTN_WIDTH), lambda r, n: (r, row(n), col))


def attn_fwd(qkv, dilation):
    d, length, _ = qkv.shape
    assert d == dilation
    nb = length // ATTN_BLOCK

    def body(q_ref, kc_ref, kp_ref, vc_ref, vp_ref, o_ref, lse_ref):
        first = pl.program_id(1) == 0
        for pair in range(ATTN_HEADS // 2):
            lanes = slice(pair * 128, (pair + 1) * 128)
            q2 = q_ref[:, lanes]
            kcat = jnp.concatenate([kp_ref[:, lanes], kc_ref[:, lanes]], axis=0)
            vcat = jnp.concatenate([vp_ref[:, lanes], vc_ref[:, lanes]], axis=0)
            o_pair = jnp.zeros((ATTN_BLOCK, 128), F32)
            lse_pair = jnp.zeros((ATTN_BLOCK, 128), F32)
            for sub in range(2):
                keep = _lane_half((ATTN_BLOCK, 128), sub)
                qm = jnp.where(keep, q2, jnp.zeros_like(q2))
                sc = _attn_scores(qm, kcat, 2 * pair + sub, d, first)
                m = jnp.max(sc, axis=-1, keepdims=True)
                p = jnp.exp(sc - m)
                den = jnp.sum(p, axis=-1, keepdims=True)
                o = _dot(p.astype(BF16), vcat) / den
                o_pair = jnp.where(keep, o, o_pair)
                lse_pair = jnp.where(keep, m + jnp.log(den), lse_pair)
            o_ref[:, lanes] = o_pair
            lse_ref[:, lanes] = lse_pair

    cur = lambda n: n
    prev = lambda n: jnp.maximum(n - 1, 0)
    return pl.pallas_call(
        body,
        name=f"attn_fwd_d{d}",
        grid=(d, nb),
        in_specs=[_sub_block(0, cur), _sub_block(1, cur), _sub_block(1, prev), _sub_block(2, cur), _sub_block(2, prev)],
        out_specs=[_sub_block(0, cur), _sub_block(0, cur)],
        out_shape=[jax.ShapeDtypeStruct((d, length, ATTN_WIDTH), F32)] * 2,
        compiler_params=_params(dimension_semantics=("arbitrary", "arbitrary")),
    )(qkv, qkv, qkv, qkv, qkv)


def attn_bwd(qkv, d_out, lse, delta, dilation):
    d, length, _ = qkv.shape
    assert d == dilation
    nb = length // ATTN_BLOCK

    def body(q_ref, kc_ref, kp_ref, vc_ref, vp_ref, do_ref, lse_ref, dl_ref, dq_ref, dk_ref, dv_ref, ck_ref, cv_ref):
        n = pl.program_id(1)

        @pl.when(n == 0)
        def _():
            ck_ref[...] = jnp.zeros_like(ck_ref)
            cv_ref[...] = jnp.zeros_like(cv_ref)

        @pl.when(n < nb)
        def _():
            first = n == 0
            for pair in range(ATTN_HEADS // 2):
                lanes = slice(pair * 128, (pair + 1) * 128)
                q2 = q_ref[:, lanes]
                do2 = do_ref[:, lanes]
                kcat = jnp.concatenate([kp_ref[:, lanes], kc_ref[:, lanes]], axis=0)
                vcat = jnp.concatenate([vp_ref[:, lanes], vc_ref[:, lanes]], axis=0)
                dq_pair = jnp.zeros((ATTN_BLOCK, 128), F32)
                dk_cat = jnp.zeros((2 * ATTN_BLOCK, 128), F32)
                dv_cat = jnp.zeros((2 * ATTN_BLOCK, 128), F32)
                for sub in range(2):
                    keep = _lane_half((ATTN_BLOCK, 128), sub)
                    col = pair * 128 + sub * ATTN_HEAD_DIM
                    qm = jnp.where(keep, q2, jnp.zeros_like(q2))
                    dom = jnp.where(keep, do2, 0.0).astype(BF16)
                    sc = _attn_scores(qm, kcat, 2 * pair + sub, d, first)
                    p = jnp.exp(sc - lse_ref[:, col:col + 1])
                    dp = _dot_nt(dom, vcat)
                    ds = (p * (dp - dl_ref[:, col:col + 1]) * (ATTN_HEAD_DIM ** -0.5)).astype(BF16)
                    dq_pair = jnp.where(keep, _dot(ds, kcat), dq_pair)
                    dk_cat = dk_cat + _dot_tn(ds, qm)
                    dv_cat = dv_cat + _dot_tn(p.astype(BF16), dom)
                dq_ref[:, lanes] = dq_pair
                dk_ref[:, lanes] = ck_ref[:, lanes] + dk_cat[:ATTN_BLOCK]
                dv_ref[:, lanes] = cv_ref[:, lanes] + dv_cat[:ATTN_BLOCK]
                ck_ref[:, lanes] = dk_cat[ATTN_BLOCK:]
                cv_ref[:, lanes] = dv_cat[ATTN_BLOCK:]

        @pl.when(n == nb)
        def _():
            dk_ref[...] = ck_ref[...]
            dv_ref[...] = cv_ref[...]

    blk = (ATTN_BLOCK, ATTN_WIDTH)
    cur = lambda n: jnp.minimum(n, nb - 1)
    prev = lambda n: jnp.maximum(jnp.minimum(n, nb - 1) - 1, 0)
    done = lambda n: jnp.maximum(n - 1, 0)
    return pl.pallas_call(
        body,
        name=f"attn_bwd_d{d}",
        grid=(d, nb + 1),
        in_specs=[_sub_block(0, cur), _sub_block(1, cur), _sub_block(1, prev), _sub_block(2, cur), _sub_block(2, prev),
                  _sub_block(0, cur), _sub_block(0, cur), _sub_block(0, cur)],
        out_specs=[_sub_block(0, cur), _sub_block(0, done), _sub_block(0, done)],
        out_shape=[jax.ShapeDtypeStruct((d, length, ATTN_WIDTH), F32)] * 3,
        scratch_shapes=[pltpu.VMEM(blk, F32), pltpu.VMEM(blk, F32)],
        compiler_params=_params(dimension_semantics=("arbitrary", "arbitrary")),
    )(qkv, qkv, qkv, qkv, qkv, d_out, lse, delta)


def _lower_bound(logits):
    return _sigmoid(logits[0:1, :] - logits[1:2, :])


def _hgrn_gates(q, fp, lb):
    sq = _sigmoid(q)
    qf = q * sq
    sig = _sigmoid(fp)
    f = lb + (1.0 - lb) * sig
    kf = (1.0 - lb) * _sigmoid(-fp)
    return sq, qf, sig, f, kf


def _tril_bf16(n, upper=False):
    r = lax.broadcasted_iota(jnp.int32, (n, n), 0)
    c = lax.broadcasted_iota(jnp.int32, (n, n), 1)
    keep = (c >= r) if upper else (c <= r)
    return jnp.where(keep, 1.0, 0.0).astype(BF16)


def hgrn_fwd(proj, lb):
    s = proj.shape[0]
    c_len, nh, hd = HGRN_CHUNK, HGRN_HEADS, HGRN_HEAD_DIM
    n_chunks = s // c_len
    col0 = 0

    def body(q_ref, f_ref, i_ref, lb_ref, o_ref, st_out_ref, a_out_ref, st_ref, b_ref, qf_ref, kf_ref, a_ref):
        @pl.when(pl.program_id(0) == 0)
        def _():
            st_ref[...] = jnp.zeros_like(st_ref)

        lbv = _lower_bound(lb_ref[...])
        _, qf, _, f, kf = _hgrn_gates(q_ref[...], f_ref[...], lbv)
        b = _tri_sum(_tril_bf16(c_len), jnp.log(f))
        b_ref[...] = b
        qf_ref[...] = qf
        kf_ref[...] = kf
        a_ref[...] = jnp.zeros_like(a_ref)
        t_idx = lax.broadcasted_iota(jnp.int32, (c_len, nh * hd), 0)
        lane = lax.broadcasted_iota(jnp.int32, (c_len, hd), 1)

        for r0 in range(0, c_len, SUBLANES):
            rows = slice(r0, c_len)

            def column(jj, carry, r0=r0, rows=rows):
                j = r0 + jj
                bj = b_ref[pl.ds(j, 1), :]
                kj = kf_ref[pl.ds(j, 1), :]
                e = jnp.exp(jnp.where(t_idx[rows] >= j, b_ref[rows, :] - bj, NEG_BIG))
                prod = qf_ref[rows, :] * kj * e
                for h in range(nh):
                    col = jnp.sum(prod[:, h * hd:(h + 1) * hd], axis=-1, keepdims=True)
                    a_ref[h, rows, :] = jnp.where(lane[rows] == j, col, a_ref[h, rows, :])
                return carry

            lax.fori_loop(0, SUBLANES, column, 0, unroll=COLUMN_UNROLL)

        b_last = b[c_len - 1:c_len, :]
        qb = (qf * jnp.exp(b)).astype(BF16)
        kb2 = (kf * jnp.exp(b_last - b)).astype(BF16)
        vf = i_ref[...].astype(BF16)
        for h in range(nh):
            hs = slice(h * hd, (h + 1) * hd)
            st = st_ref[h]
            st_out_ref[0, h] = st
            a_h = a_ref[h]
            a_out_ref[:, hs] = a_h
            o_ref[:, hs] = _dot_nt(qb[:, hs], st.astype(BF16)) + _dot(a_h[:, :c_len].astype(BF16), vf[:, hs])
            st_ref[h] = st * jnp.exp(b_last[:, hs]) + _dot_tn(vf[:, hs], kb2[:, hs])

    blk = (c_len, HGRN_WIDTH)
    return pl.pallas_call(
        body,
        name="hgrn_fwd",
        grid=(n_chunks,),
        in_specs=[
            pl.BlockSpec(blk, lambda c: (c, col0)),
            pl.BlockSpec(blk, lambda c: (c, col0 + 1)),
            pl.BlockSpec(blk, lambda c: (c, col0 + 2)),
            pl.BlockSpec((2, HGRN_WIDTH), lambda c: (0, 0)),
        ],
        out_specs=[
            pl.BlockSpec(blk, lambda c: (c, 0)),
            pl.BlockSpec((1, nh, hd, hd), lambda c: (c, 0, 0, 0)),
            pl.BlockSpec(blk, lambda c: (c, 0)),
        ],
        out_shape=[
            jax.ShapeDtypeStruct((s, HGRN_WIDTH), F32),
            jax.ShapeDtypeStruct((n_chunks, nh, hd, hd), F32),
            jax.ShapeDtypeStruct((s, nh * hd), F32),
        ],
        scratch_shapes=[
            pltpu.VMEM((nh, hd, hd), F32),
            pltpu.VMEM(blk, F32),
            pltpu.VMEM(blk, F32),
            pltpu.VMEM(blk, F32),
            pltpu.VMEM((nh, c_len, hd), F32),
        ],
        compiler_params=_params(dimension_semantics=("arbitrary",)),
    )(proj, proj, proj, lb)


def hgrn_bwd(proj, lb, d_o, states, a_mat):
    s = proj.shape[0]
    c_len, nh, hd = HGRN_CHUNK, HGRN_HEADS, HGRN_HEAD_DIM
    n_chunks = s // c_len
    col0 = 0
    last = n_chunks - 1

    def body(q_ref, f_ref, i_ref, lb_ref, do_ref, st_in_ref, a_in_ref, dq_ref, df_ref, di_ref, dlb_ref,
             dst_ref, b_ref, qf_ref, kf_ref, da_ref, dqi_ref, dki_ref):
        @pl.when(pl.program_id(0) == 0)
        def _():
            dst_ref[...] = jnp.zeros_like(dst_ref)
            dlb_ref[...] = jnp.zeros_like(dlb_ref)

        lbv = _lower_bound(lb_ref[...])
        q = q_ref[...]
        sq, qf, sig, f, kf = _hgrn_gates(q, f_ref[...], lbv)
        b = _tri_sum(_tril_bf16(c_len), jnp.log(f))
        b_ref[...] = b
        qf_ref[...] = qf
        kf_ref[...] = kf
        b_last = b[c_len - 1:c_len, :]
        eb = jnp.exp(b)
        ebl = jnp.exp(b_last - b)
        qb = qf * eb
        kb2 = kf * ebl
        vf = i_ref[...]
        d_o = do_ref[...]
        qb_b, kb2_b, vf_b, do_b = qb.astype(BF16), kb2.astype(BF16), vf.astype(BF16), d_o.astype(BF16)
        tq = lax.broadcasted_iota(jnp.int32, (c_len, hd), 0)
        lane = lax.broadcasted_iota(jnp.int32, (c_len, hd), 1)

        dqb_parts, dvf_parts, dkb2_parts, dbl_parts = [], [], [], []
        for h in range(nh):
            hs = slice(h * hd, (h + 1) * hd)
            st = st_in_ref[0, h]
            dst = dst_ref[h]
            st_b, dst_b = st.astype(BF16), dst.astype(BF16)
            a_h = a_in_ref[:, hs][:, :c_len].astype(BF16)
            dqb_parts.append(_dot(do_b[:, hs], st_b))
            dvf_parts.append(_dot_tn(a_h, do_b[:, hs]) + _dot_nt(kb2_b[:, hs], dst_b))
            dkb2_parts.append(_dot(vf_b[:, hs], dst_b))
            da = _dot_nt(do_b[:, hs], vf_b[:, hs])
            da = jnp.concatenate([da, jnp.zeros((c_len, hd - c_len), F32)], axis=1)
            da_ref[h] = jnp.where(tq >= lane, da, 0.0)
            dbl_parts.append(jnp.sum(dst * st, axis=0, keepdims=True) * jnp.exp(b_last[:, hs]))
            dst_ref[h] = dst * jnp.exp(b_last[:, hs]) + _dot_tn(do_b[:, hs], qb_b[:, hs])
        dqb = jnp.concatenate(dqb_parts, axis=1)
        dvf = jnp.concatenate(dvf_parts, axis=1)
        dkb2 = jnp.concatenate(dkb2_parts, axis=1)
        dbl = jnp.concatenate(dbl_parts, axis=1) + jnp.sum(dkb2 * kb2, axis=0, keepdims=True)

        dqi_ref[...] = jnp.zeros_like(dqi_ref)
        t_idx = lax.broadcasted_iota(jnp.int32, (c_len, nh * hd), 0)

        for r0 in range(0, c_len, SUBLANES):
            rows = slice(r0, c_len)
            nrow = c_len - r0

            def column(jj, carry, r0=r0, rows=rows, nrow=nrow):
                j = r0 + jj
                bj = b_ref[pl.ds(j, 1), :]
                kj = kf_ref[pl.ds(j, 1), :]
                e = jnp.exp(jnp.where(t_idx[rows] >= j, b_ref[rows, :] - bj, NEG_BIG))
                cols = [jnp.sum(jnp.where(lane[rows] == j, da_ref[h, rows, :], 0.0), axis=-1, keepdims=True)
                        for h in range(nh)]
                w = e * jnp.concatenate([jnp.broadcast_to(cc, (nrow, hd)) for cc in cols], axis=1)
                dqi_ref[rows, :] += w * kj
                dki_ref[pl.ds(j, 1), :] = jnp.sum(w * qf_ref[rows, :], axis=0, keepdims=True)
                return carry

            lax.fori_loop(0, SUBLANES, column, 0, unroll=COLUMN_UNROLL)
        dq_intra = dqi_ref[...]
        dk_intra = dki_ref[...]

        db = dqb * qb + qf * dq_intra - kf * dk_intra - dkb2 * kb2
        db = db + jnp.where(t_idx == c_len - 1, dbl, 0.0)
        dg = _tri_sum(_tril_bf16(c_len, upper=True), db)
        dqf = dqb * eb + dq_intra
        dkf = dkb2 * ebl + dk_intra
        dq_ref[...] = dqf * (sq * (1.0 + q * (1.0 - sq)))
        dfv = dg / f - dkf
        df_ref[...] = dfv * (1.0 - lbv) * sig * (1.0 - sig)
        di_ref[...] = dvf
        dlb_ref[...] += jnp.sum(dfv * (1.0 - sig), axis=0, keepdims=True)

    blk = (c_len, HGRN_WIDTH)
    rev = lambda c: last - c
    return pl.pallas_call(
        body,
        name="hgrn_bwd",
        grid=(n_chunks,),
        in_specs=[
            pl.BlockSpec(blk, lambda c: (rev(c), col0)),
            pl.BlockSpec(blk, lambda c: (rev(c), col0 + 1)),
            pl.BlockSpec(blk, lambda c: (rev(c), col0 + 2)),
            pl.BlockSpec((2, HGRN_WIDTH), lambda c: (0, 0)),
            pl.BlockSpec(blk, lambda c: (rev(c), 0)),
            pl.BlockSpec((1, nh, hd, hd), lambda c: (rev(c), 0, 0, 0)),
            pl.BlockSpec(blk, lambda c: (rev(c), 0)),
        ],
        out_specs=[
            pl.BlockSpec(blk, lambda c: (rev(c), 0)),
            pl.BlockSpec(blk, lambda c: (rev(c), 0)),
            pl.BlockSpec(blk, lambda c: (rev(c), 0)),
            pl.BlockSpec((1, HGRN_WIDTH), lambda c: (0, 0)),
        ],
        out_shape=[jax.ShapeDtypeStruct((s, HGRN_WIDTH), F32)] * 3 + [jax.ShapeDtypeStruct((1, HGRN_WIDTH), F32)],
        scratch_shapes=[
            pltpu.VMEM((nh, hd, hd), F32),
            pltpu.VMEM(blk, F32),
            pltpu.VMEM(blk, F32),
            pltpu.VMEM(blk, F32),
            pltpu.VMEM((nh, c_len, hd), F32),
            pltpu.VMEM(blk, F32),
            pltpu.VMEM(blk, F32),
        ],
        compiler_params=_params(dimension_semantics=("arbitrary",)),
    )(proj, proj, proj, lb, d_o, states, a_mat)


def _row_spec(tm, width, col=0):
    return pl.BlockSpec((tm, width), lambda i: (i, col))


def _const_spec(width):
    return pl.BlockSpec((1, width), lambda i: (0, 0))


def _acc_rows(ref, value):
    @pl.when(pl.program_id(0) == 0)
    def _():
        ref[...] = jnp.zeros_like(ref)

    ref[...] += jnp.sum(value, axis=0, keepdims=True)


def mix_fwd(attn_parts, o_h, proj, an, hn, w_out_b, gp, x):
    s = x.shape[0]
    tm = TOKEN_TILE
    gate_col = 3
    hd = HGRN_HEAD_DIM
    nd = len(DILATIONS)

    def body(*refs):
        o_refs, l_refs = refs[:nd], refs[nd:2 * nd]
        oh_ref, gate_ref, an_ref, hn_ref, w_ref, gp_ref, x_ref = refs[2 * nd:2 * nd + 7]
        x1_ref, cat_ref, mixed_ref, attn_ref = refs[2 * nd + 7:2 * nd + 11]
        lse_refs = refs[2 * nd + 11:3 * nd + 11]
        o_scr, l_scr, lse_scr = refs[3 * nd + 11:]
        os_ = [_from_dilated(r, o_scr.at[k], d, tm) for k, (r, d) in enumerate(zip(o_refs, DILATIONS))]
        ls = [_from_dilated(r, l_scr.at[k], d, tm) for k, (r, d) in enumerate(zip(l_refs, DILATIONS))]
        m = jnp.maximum(jnp.maximum(ls[0], ls[1]), ls[2])
        es = [jnp.exp(l - m) for l in ls]
        den = es[0] + es[1] + es[2]
        attn = (es[0] * os_[0] + es[1] * os_[1] + es[2] * os_[2]) / den
        attn_ref[...] = attn
        _lane_blocks(lse_scr, m + jnp.log(den))
        for d, ref in zip(DILATIONS, lse_refs):
            _to_dilated(lse_scr, ref, d, tm)
        cat_ref[:, :ATTN_WIDTH] = _rms_fwd(attn, an_ref[...], ATTN_WIDTH).astype(BF16)
        gate = gate_ref[...]
        silu_g = gate * _sigmoid(gate)
        for h in range(HGRN_HEADS):
            hs = slice(h * hd, (h + 1) * hd)
            rec = _rms_fwd(oh_ref[:, hs], hn_ref[:, hs], hd) * silu_g[:, hs]
            cat_ref[:, ATTN_WIDTH + h * hd:ATTN_WIDTH + (h + 1) * hd] = rec.astype(BF16)
        mixed = _dot(cat_ref[...], w_ref[...])
        mixed_ref[...] = mixed
        x1_ref[...] = x_ref[...] + _rms_fwd(mixed, gp_ref[...], D_MODEL)

    aw = ATTN_WIDTH
    return pl.pallas_call(
        body,
        name="mix_fwd",
        grid=(s // tm,),
        in_specs=[_dilated_spec(d, tm, aw) for d in DILATIONS] * 2 + [
            _row_spec(tm, aw), _row_spec(tm, aw, gate_col), _const_spec(aw), _const_spec(aw), _vmem_spec(),
            _const_spec(D_MODEL), _row_spec(tm, D_MODEL)],
        out_specs=[_row_spec(tm, D_MODEL), _row_spec(tm, D_MODEL), _row_spec(tm, D_MODEL), _row_spec(tm, aw)] + [
            _dilated_spec(d, tm, aw) for d in DILATIONS],
        out_shape=[
            jax.ShapeDtypeStruct((s, D_MODEL), F32),
            jax.ShapeDtypeStruct((s, D_MODEL), BF16),
            jax.ShapeDtypeStruct((s, D_MODEL), F32),
            jax.ShapeDtypeStruct((s, aw), F32),
        ] + [jax.ShapeDtypeStruct((d, s // d, aw), F32) for d in DILATIONS],
        scratch_shapes=[pltpu.VMEM((nd, aw // LANES, tm, LANES), F32), pltpu.VMEM((nd, aw // LANES, tm, LANES), F32),
                        pltpu.VMEM((aw // LANES, tm, LANES), F32)],
        compiler_params=_params(dimension_semantics=("arbitrary",)),
    )(*[p[0] for p in attn_parts], *[p[1] for p in attn_parts], o_h, proj, an, hn, w_out_b, gp, x)


def mix_bwd(dx1, mixed, gp, w_out_b, attn, an, o_h, proj, hn):
    s = dx1.shape[0]
    tm = TOKEN_TILE
    gate_col = 3
    hd = HGRN_HEAD_DIM
    aw = ATTN_WIDTH

    nd = len(DILATIONS)

    def body(*refs):
        dx1_ref, mixed_ref, gp_ref, w_ref, attn_ref, an_ref, oh_ref, gate_ref, hn_ref, dmix_ref = refs[:10]
        do_refs, delta_refs = refs[10:10 + nd], refs[10 + nd:10 + 2 * nd]
        doh_ref, dgate_ref, dgp_ref, dan_ref, dhn_ref, do_ref, delta_ref = refs[10 + 2 * nd:]
        dmixed, gp_c = _rms_bwd(dx1_ref[...], mixed_ref[...], gp_ref[...], D_MODEL)
        _acc_rows(dgp_ref, gp_c)
        dmixed_b = dmixed.astype(BF16)
        dmix_ref[...] = dmixed_b
        dcat = _dot_nt(dmixed_b, w_ref[...])
        attn = attn_ref[...]
        d_o, an_c = _rms_bwd(dcat[:, :aw], attn, an_ref[...], aw)
        _acc_rows(dan_ref, an_c)
        _lane_blocks(do_ref, d_o)
        prod = d_o * attn
        for pair in range(ATTN_HEADS // 2):
            pp = prod[:, pair * LANES:(pair + 1) * LANES]
            low = _lane_half((tm, LANES), 0)
            lo = jnp.sum(jnp.where(low, pp, 0.0), axis=-1, keepdims=True)
            hi = jnp.sum(jnp.where(low, 0.0, pp), axis=-1, keepdims=True)
            delta_ref[pair] = jnp.where(low, lo, hi)
        for d, o_ref, l_ref in zip(DILATIONS, do_refs, delta_refs):
            _to_dilated(do_ref, o_ref, d, tm)
            _to_dilated(delta_ref, l_ref, d, tm)
        gate = gate_ref[...]
        sg = _sigmoid(gate)
        silu_g = gate * sg
        drec = dcat[:, aw:]
        hn_parts = []
        for h in range(HGRN_HEADS):
            hs = slice(h * hd, (h + 1) * hd)
            oh = oh_ref[:, hs]
            on = _rms_fwd(oh, hn_ref[:, hs], hd)
            dgate_ref[:, hs] = drec[:, hs] * on * (sg[:, hs] * (1.0 + gate[:, hs] * (1.0 - sg[:, hs])))
            d_oh, hn_c = _rms_bwd(drec[:, hs] * silu_g[:, hs], oh, hn_ref[:, hs], hd)
            doh_ref[:, hs] = d_oh
            hn_parts.append(hn_c)
        _acc_rows(dhn_ref, jnp.concatenate(hn_parts, axis=1))

    return pl.pallas_call(
        body,
        name="mix_bwd",
        grid=(s // tm,),
        in_specs=[_row_spec(tm, D_MODEL), _row_spec(tm, D_MODEL), _const_spec(D_MODEL), _vmem_spec(), _row_spec(tm, aw),
                  _const_spec(aw), _row_spec(tm, aw), _row_spec(tm, aw, gate_col), _const_spec(aw)],
        out_specs=[_row_spec(tm, D_MODEL)] + [_dilated_spec(d, tm, aw) for d in DILATIONS] * 2 + [_row_spec(tm, aw)] * 2 + [
            _const_spec(D_MODEL), _const_spec(aw), _const_spec(aw)],
        out_shape=[jax.ShapeDtypeStruct((s, D_MODEL), BF16)] + [
            jax.ShapeDtypeStruct((d, s // d, aw), F32) for d in DILATIONS] * 2 + [jax.ShapeDtypeStruct((s, aw), F32)] * 2 + [
            jax.ShapeDtypeStruct((1, D_MODEL), F32), jax.ShapeDtypeStruct((1, aw), F32),
            jax.ShapeDtypeStruct((1, aw), F32)],
        scratch_shapes=[pltpu.VMEM((aw // LANES, tm, LANES), F32), pltpu.VMEM((aw // LANES, tm, LANES), F32)],
        compiler_params=_params(dimension_semantics=("arbitrary",)),
    )(dx1, mixed, gp, w_out_b, attn, an, o_h, proj, hn)


def mlp_fwd_bwd(x1, g_pre, w1_blocks, w2_b, g_post, target):
    s = x1.shape[0]
    tm = TOKEN_TILE
    nblk, _, fb = w1_blocks.shape

    def body(x1_ref, gpre_ref, w1_ref, w2_ref, gpost_ref, t_ref,
             dx1_ref, h2_ref, a_ref, du_ref, dff_ref, loss_ref, dgpre_ref, dgpost_ref, u_ref):
        x1v = x1_ref[...]
        h2 = _rms_fwd(x1v, gpre_ref[...], D_MODEL).astype(BF16)
        h2_ref[...] = h2
        ff = jnp.zeros((tm, D_MODEL), F32)
        for j in range(nblk):
            cols = slice(j * fb, (j + 1) * fb)
            ru = jnp.maximum(_dot(h2, w1_ref[j]), 0.0)
            u_ref[:, cols] = ru
            a = (ru * ru).astype(BF16)
            a_ref[:, cols] = a
            ff = ff + _dot(a, w2_ref[cols, :])
        diff = x1v + _rms_fwd(ff, gpost_ref[...], D_MODEL) - t_ref[...]
        _acc_rows(loss_ref, diff * diff)
        dy = diff * (1.0 / D_MODEL)
        dff, gpost_c = _rms_bwd(dy, ff, gpost_ref[...], D_MODEL)
        _acc_rows(dgpost_ref, gpost_c)
        dff_b = dff.astype(BF16)
        dff_ref[...] = dff_b
        dh2 = jnp.zeros((tm, D_MODEL), F32)
        for j in range(nblk):
            cols = slice(j * fb, (j + 1) * fb)
            du = (_dot_nt(dff_b, w2_ref[cols, :]) * (2.0 * u_ref[:, cols])).astype(BF16)
            du_ref[:, cols] = du
            dh2 = dh2 + _dot_nt(du, w1_ref[j])
        dxa, gpre_c = _rms_bwd(dh2, x1v, gpre_ref[...], D_MODEL)
        _acc_rows(dgpre_ref, gpre_c)
        dx1_ref[...] = dy + dxa

    dm = D_MODEL
    return pl.pallas_call(
        body,
        name="mlp_fwd_bwd",
        grid=(s // tm,),
        in_specs=[_row_spec(tm, dm), _const_spec(dm), _vmem_spec(), _vmem_spec(), _const_spec(dm), _row_spec(tm, dm)],
        out_specs=[_row_spec(tm, dm), _row_spec(tm, dm), _row_spec(tm, D_FF), _row_spec(tm, D_FF), _row_spec(tm, dm),
                   _const_spec(dm), _const_spec(dm), _const_spec(dm)],
        out_shape=[
            jax.ShapeDtypeStruct((s, dm), F32),
            jax.ShapeDtypeStruct((s, dm), BF16),
            jax.ShapeDtypeStruct((s, D_FF), BF16),
            jax.ShapeDtypeStruct((s, D_FF), BF16),
            jax.ShapeDtypeStruct((s, dm), BF16),
            jax.ShapeDtypeStruct((1, dm), F32),
            jax.ShapeDtypeStruct((1, dm), F32),
            jax.ShapeDtypeStruct((1, dm), F32),
        ],
        scratch_shapes=[pltpu.VMEM((tm, D_FF), F32)],
        compiler_params=_params(dimension_semantics=("arbitrary",)),
    )(x1, g_pre, w1_blocks, w2_b, g_post, target)


def in_proj_bwd(attn_grads, hgrn_grads, dgate, w_in_b, x, g1, dx1):
    s = x.shape[0]
    tm = TOKEN_TILE
    aw = ATTN_WIDTH
    n_attn = len(attn_grads)
    flat = [g[k] for k in range(3) for g in attn_grads] + list(hgrn_grads) + [dgate]

    def body(*refs):
        parts = refs[:len(flat)]
        w_ref, x_ref, g_ref, dx1_ref, dx_ref, dproj_ref, dg_ref, scr = refs[len(flat):]
        groups = []
        for k in range(3):
            acc = None
            for p, d in zip(parts[k * n_attn:(k + 1) * n_attn], DILATIONS):
                v = _from_dilated(p, scr, d, tm)
                acc = v if acc is None else acc + v
            groups.append(acc)
        groups += [p[...] for p in parts[3 * n_attn:]]
        dh = jnp.zeros((tm, D_MODEL), F32)
        for gi, grp in enumerate(groups):
            cols = slice(gi * aw, (gi + 1) * aw)
            gb = grp.astype(BF16)
            dproj_ref[:, cols] = gb
            dh = dh + _dot_nt(gb, w_ref[:, cols])
        dxa, g_c = _rms_bwd(dh, x_ref[...], g_ref[...], D_MODEL)
        _acc_rows(dg_ref, g_c)
        dx_ref[...] = dx1_ref[...] + dxa

    dm = D_MODEL
    return pl.pallas_call(
        body,
        name="in_proj_bwd",
        grid=(s // tm,),
        in_specs=[_dilated_spec(d, tm, aw) for d in DILATIONS] * 3 + [_row_spec(tm, aw)] * 4 + [
            _vmem_spec(), _row_spec(tm, dm), _const_spec(dm), _row_spec(tm, dm)],
        out_specs=[_row_spec(tm, dm), _row_spec(tm, IN_PROJ_WIDTH), _const_spec(dm)],
        out_shape=[jax.ShapeDtypeStruct((s, dm), F32), jax.ShapeDtypeStruct((s, IN_PROJ_WIDTH), BF16),
                   jax.ShapeDtypeStruct((1, dm), F32)],
        scratch_shapes=[pltpu.VMEM((aw // LANES, tm, LANES), F32)],
        compiler_params=_params(dimension_semantics=("arbitrary",)),
    )(*flat, w_in_b, x, g1, dx1)


def wgrad(a_b, b_b, tn, name):
    s, k = a_b.shape
    n = b_b.shape[1]
    ts = 512

    def body(a_ref, b_ref, o_ref):
        @pl.when(pl.program_id(1) == 0)
        def _():
            o_ref[...] = jnp.zeros_like(o_ref)

        o_ref[0] += _dot_tn(a_ref[...], b_ref[...])

    return pl.pallas_call(
        body,
        name=name,
        grid=(n // tn, s // ts),
        in_specs=[pl.BlockSpec((ts, k), lambda j, i: (i, 0)), pl.BlockSpec((ts, tn), lambda j, i: (i, j))],
        out_specs=pl.BlockSpec((1, k, tn), lambda j, i: (j, 0, 0)),
        out_shape=jax.ShapeDtypeStruct((n // tn, k, tn), F32),
        compiler_params=_params(dimension_semantics=("arbitrary", "arbitrary")),
    )(a_b, b_b)


def local_step(x, target, g1, w_in_b, an, lb, hn, w_out_b, gp, g_pre, w1_blocks, w2_b, g_post):
    nd = len(DILATIONS)
    proj, h_b, *qkvs = in_proj_fwd(x, g1, w_in_b)
    attn_parts = [attn_fwd(qkv, d) for qkv, d in zip(qkvs, DILATIONS)]
    o_h, states, a_mat = hgrn_fwd(proj, lb)
    x1, cat_b, mixed, attn, *lses = mix_fwd(attn_parts, o_h, proj, an, hn, w_out_b, gp, x)
    dx1, h2_b, a_b, du_b, dff_b, loss_vec, dg_pre, dg_post = mlp_fwd_bwd(x1, g_pre, w1_blocks, w2_b, g_post, target)
    dw2 = wgrad(a_b, dff_b, D_MODEL, "wgrad_ff2")
    dw1 = wgrad(h2_b, du_b, D_FF // N_DEV, "wgrad_ff1")
    dmix_b, *rest = mix_bwd(dx1, mixed, gp, w_out_b, attn, an, o_h, proj, hn)
    d_os, deltas = rest[:nd], rest[nd:2 * nd]
    d_oh, dgate, dgp, dan, dhn = rest[2 * nd:]
    dwout = wgrad(cat_b, dmix_b, D_MODEL, "wgrad_out")
    attn_grads = [attn_bwd(qkv, d_o, lse, delta, d) for qkv, d_o, lse, delta, d in zip(qkvs, d_os, lses, deltas, DILATIONS)]
    hgrn_grads = hgrn_bwd(proj, lb, d_oh, states, a_mat)
    dlb = hgrn_grads[3]
    dx, dproj_b, dg1 = in_proj_bwd(attn_grads, hgrn_grads[:3], dgate, w_in_b, x, g1, dx1)
    dwin = wgrad(h_b, dproj_b, 2 * IN_PROJ_WIDTH // N_DEV, "wgrad_in")
    small = dict(dg1=dg1, dan=dan, dlb=dlb, dhn=dhn, dgp=dgp, dg_pre=dg_pre, dg_post=dg_post, loss_vec=loss_vec)
    return dx, dwin, dwout, dw1, dw2, small


def _position():
    x, y, c = lax.axis_index("x"), lax.axis_index("y"), lax.axis_index("c")
    other_chips = [(1 - x, y), (x, 1 - y), (1 - x, 1 - y)]
    return x, y, c, other_chips


def _any_spec():
    return pl.BlockSpec(memory_space=pl.ANY)


def all_gather(shards):
    n = len(shards)

    def body(*refs):
        ins, outs = refs[:n], refs[n:2 * n]
        send_sems, recv_sems, local_sems = refs[2 * n:]
        x, y, c, chips = _position()
        me, sibling = (x, y, c), (x, y, 1 - c)

        def slot(a, px, py, pc):
            return outs[a].at[4 * px + 2 * py + pc]

        def copy(a, k, block, to, src=None):
            return pltpu.make_async_remote_copy(
                src_ref=slot(a, *block) if src is None else src, dst_ref=slot(a, *block),
                send_sem=send_sems.at[a, k], recv_sem=recv_sems.at[a, k], device_id=to, device_id_type=MESH)

        local = [pltpu.make_async_copy(ins[a], slot(a, *me), local_sems.at[a]) for a in range(n)]
        for cp in local:
            cp.start()
        first = []
        for a in range(n):
            first.append(copy(a, 0, me, sibling, src=ins[a]))
            first += [copy(a, 1 + j, me, (*chip, c), src=ins[a]) for j, chip in enumerate(chips)]
        for cp in first:
            cp.start()
        passed = []
        for j, chip in enumerate(chips):
            for a in range(n):
                copy(a, 1 + j, (*chip, c), me).wait_recv()
                fwd = copy(a, 4 + j, (*chip, c), sibling)
                fwd.start()
                passed.append(fwd)
        for a in range(n):
            copy(a, 0, sibling, me).wait_recv()
            for j, chip in enumerate(chips):
                copy(a, 4 + j, (*chip, 1 - c), me).wait_recv()
        for cp in first + passed:
            cp.wait_send()
        for cp in local:
            cp.wait()

    return pl.pallas_call(
        body,
        name="all_gather_weights",
        in_specs=[_any_spec()] * n,
        out_specs=[_any_spec()] * n,
        out_shape=[jax.ShapeDtypeStruct((N_DEV,) + sh.shape, sh.dtype) for sh in shards],
        scratch_shapes=[pltpu.SemaphoreType.DMA((n, 7)), pltpu.SemaphoreType.DMA((n, 7)), pltpu.SemaphoreType.DMA((n,))],
    )(*shards)


def reduce_to_core(grads):
    n = len(grads)

    def body(*refs):
        ins, outs = refs[:n], refs[n:2 * n]
        send_sems, recv_sems = refs[2 * n:]
        x, y, c, _ = _position()
        sibling = (x, y, 1 - c)
        copies = []
        for a in range(n):
            for q in range(4):
                copies.append(pltpu.make_async_remote_copy(
                    src_ref=ins[a].at[2 * q + (1 - c)], dst_ref=outs[a].at[q],
                    send_sem=send_sems.at[a, q], recv_sem=recv_sems.at[a, q], device_id=sibling, device_id_type=MESH))
        for cp in copies:
            cp.start()
        for cp in copies:
            cp.wait()

    return pl.pallas_call(
        body,
        name="reduce_to_core",
        in_specs=[_any_spec()] * n,
        out_specs=[_any_spec()] * n,
        out_shape=[jax.ShapeDtypeStruct((4,) + g.shape[1:], g.dtype) for g in grads],
        scratch_shapes=[pltpu.SemaphoreType.DMA((n, 4)), pltpu.SemaphoreType.DMA((n, 4))],
    )(*grads)


def pair_sum(grad, from_sibling, name):
    _, r, cdim = grad.shape
    tr = min(r, 256)
    c_idx = lax.axis_index("c").astype(jnp.int32).reshape(1)

    def body(c_ref, g_ref, s_ref, o_ref):
        o_ref[...] = g_ref[...] + s_ref[...]

    return pl.pallas_call(
        body,
        name=name,
        grid_spec=pltpu.PrefetchScalarGridSpec(
            num_scalar_prefetch=1,
            grid=(4, r // tr),
            in_specs=[pl.BlockSpec((1, tr, cdim), lambda q, i, cr: (2 * q + cr[0], i, 0)),
                      pl.BlockSpec((1, tr, cdim), lambda q, i, cr: (q, i, 0))],
            out_specs=pl.BlockSpec((1, tr, cdim), lambda q, i, cr: (q, i, 0)),
        ),
        out_shape=jax.ShapeDtypeStruct((4, r, cdim), grad.dtype),
        compiler_params=_params(dimension_semantics=("arbitrary", "arbitrary")),
    )(c_idx, grad, from_sibling)


def reduce_to_chip(pairs):
    n = len(pairs)

    def body(*refs):
        ins, outs = refs[:n], refs[n:2 * n]
        send_sems, recv_sems = refs[2 * n:]
        x, y, c, chips = _position()
        copies = []
        for a in range(n):
            for j, (px, py) in enumerate(chips):
                copies.append(pltpu.make_async_remote_copy(
                    src_ref=ins[a].at[2 * px + py], dst_ref=outs[a].at[j],
                    send_sem=send_sems.at[a, j], recv_sem=recv_sems.at[a, j], device_id=(px, py, c), device_id_type=MESH))
        for cp in copies:
            cp.start()
        for cp in copies:
            cp.wait()

    return pl.pallas_call(
        body,
        name="reduce_to_chip",
        in_specs=[_any_spec()] * n,
        out_specs=[_any_spec()] * n,
        out_shape=[jax.ShapeDtypeStruct((3,) + p.shape[1:], p.dtype) for p in pairs],
        scratch_shapes=[pltpu.SemaphoreType.DMA((n, 3)), pltpu.SemaphoreType.DMA((n, 3))],
    )(*pairs)


def _adamw(w, g, m, v):
    m = ADAM_B1 * m + (1.0 - ADAM_B1) * g
    v = ADAM_B2 * v + (1.0 - ADAM_B2) * (g * g)
    m_hat = m / (1.0 - ADAM_B1 ** ADAM_STEP)
    v_hat = v / (1.0 - ADAM_B2 ** ADAM_STEP)
    delta = -ADAM_LR * (m_hat / (jnp.sqrt(v_hat) + ADAM_EPS) + ADAM_WD * w)
    return delta, m, v


def sum_adamw(pairs, others, w, m, v, name):
    r, cdim = w.shape
    tr = min(r, 256)
    chip_idx = (2 * lax.axis_index("x") + lax.axis_index("y")).astype(jnp.int32).reshape(1)

    def body(q_ref, p_ref, o_ref, w_ref, m_ref, v_ref, g_out, d_out, m_out, v_out):
        g = p_ref[0] + o_ref[0] + o_ref[1] + o_ref[2]
        g_out[...] = g
        d_out[...], m_out[...], v_out[...] = _adamw(w_ref[...], g, m_ref[...], v_ref[...])

    tile = lambda: pl.BlockSpec((tr, cdim), lambda i, qr: (i, 0))
    return pl.pallas_call(
        body,
        name=name,
        grid_spec=pltpu.PrefetchScalarGridSpec(
            num_scalar_prefetch=1,
            grid=(r // tr,),
            in_specs=[pl.BlockSpec((1, tr, cdim), lambda i, qr: (qr[0], i, 0)),
                      pl.BlockSpec((3, tr, cdim), lambda i, qr: (0, i, 0)), tile(), tile(), tile()],
            out_specs=[tile(), tile(), tile(), tile()],
        ),
        out_shape=[jax.ShapeDtypeStruct((r, cdim), F32)] * 4,
        compiler_params=_params(dimension_semantics=("arbitrary",)),
    )(chip_idx, pairs, others, w, m, v)


SMALL_ROWS = 8


def small_all_reduce(packed):
    shape = packed.shape

    def body(in_ref, out_ref, recv_ref, send_sems, recv_sems):
        x, y, c, _ = _position()
        my_id = 4 * x + 2 * y + c
        recv_ref[my_id] = in_ref[...]
        copies = []
        for rel in range(1, N_DEV):
            fx, fy, fc = (rel >> 2) & 1, (rel >> 1) & 1, rel & 1
            px = 1 - x if fx else x
            py = 1 - y if fy else y
            pc = 1 - c if fc else c
            cp = pltpu.make_async_remote_copy(
                src_ref=in_ref, dst_ref=recv_ref.at[my_id], send_sem=send_sems.at[rel - 1],
                recv_sem=recv_sems.at[rel - 1], device_id=(px, py, pc), device_id_type=MESH)
            cp.start()
            copies.append((cp, pltpu.make_async_remote_copy(
                src_ref=in_ref, dst_ref=recv_ref.at[4 * px + 2 * py + pc], send_sem=send_sems.at[rel - 1],
                recv_sem=recv_sems.at[rel - 1], device_id=(px, py, pc), device_id_type=MESH)))
        for cp, landing in copies:
            landing.wait_recv()
        for cp, landing in copies:
            cp.wait_send()
        total = recv_ref[0]
        for k in range(1, N_DEV):
            total = total + recv_ref[k]
        out_ref[...] = total

    return pl.pallas_call(
        body,
        name="small_all_reduce",
        in_specs=[_vmem_spec()],
        out_specs=_vmem_spec(),
        out_shape=jax.ShapeDtypeStruct(shape, F32),
        scratch_shapes=[pltpu.VMEM((N_DEV,) + shape, F32), pltpu.SemaphoreType.DMA((N_DEV - 1,)),
                        pltpu.SemaphoreType.DMA((N_DEV - 1,))],
    )(packed)


def small_adamw(reduced, w, m, v):
    def body(r_ref, w_ref, m_ref, v_ref, g_out, d_out, m_out, v_out, loss_out):
        red = r_ref[...]
        wv = w_ref[...]
        lb = _lower_bound(jnp.concatenate([wv[5:6, :HGRN_WIDTH], wv[5:6, HGRN_WIDTH:]], axis=0))
        t = red[5:6, :HGRN_WIDTH] * lb * (1.0 - lb)
        row = lax.broadcasted_iota(jnp.int32, red.shape, 0)
        g = jnp.where(row == 5, jnp.concatenate([t, -t], axis=1), jnp.where(row >= 6, 0.0, red))
        g_out[...] = g
        d_out[...], m_out[...], v_out[...] = _adamw(wv, g, m_ref[...], v_ref[...])
        loss = jnp.sum(red[6:7, :], axis=-1, keepdims=True) * (0.5 / D_MODEL)
        loss_out[...] = jnp.broadcast_to(loss, loss_out.shape)

    return pl.pallas_call(
        body,
        name="small_adamw",
        in_specs=[_vmem_spec()] * 4,
        out_specs=[_vmem_spec()] * 5,
        out_shape=[jax.ShapeDtypeStruct(reduced.shape, F32)] * 4 + [jax.ShapeDtypeStruct((8, 128), F32)],
    )(reduced, w, m, v)


def _pack_small(g1, gp, g_pre, g_post, an, hn, logits_or_dlb, extra=None):
    row5 = logits_or_dlb.reshape(1, -1)
    row5 = jnp.pad(row5, ((0, 0), (0, D_MODEL - row5.shape[1])))
    row6 = jnp.zeros((1, D_MODEL), F32) if extra is None else extra
    return jnp.concatenate([g1, gp, g_pre, g_post, jnp.concatenate([an, hn], axis=1), row5, row6,
                            jnp.zeros((1, D_MODEL), F32)], axis=0)


def _unpack_small(p):
    return dict(mix_pre_norm=p[0:1], mix_post_norm=p[1:2], mlp_pre_norm=p[2:3], mlp_post_norm=p[3:4],
                attn_out_norm=p[4:5, :ATTN_WIDTH], hgrn_out_norm=p[4:5, ATTN_WIDTH:],
                hgrn_lb_logits=p[5].reshape(2, HGRN_WIDTH))


BIG = ("w_in", "w_out", "w_ff1", "w_ff2")
ORDER = ("mix_pre_norm", "w_in", "attn_out_norm", "hgrn_lb_logits", "hgrn_out_norm", "w_out", "mix_post_norm",
         "mlp_pre_norm", "w_ff1", "w_ff2", "mlp_post_norm")


def kernel(x, mix_pre_norm, w_in, attn_out_norm, hgrn_lb_logits, hgrn_out_norm, w_out, mix_post_norm, mlp_pre_norm, w_ff1, w_ff2, mlp_post_norm, loss_target, m_mix_pre_norm, m_w_in, m_attn_out_norm, m_hgrn_lb_logits, m_hgrn_out_norm, m_w_out, m_mix_post_norm, m_mlp_pre_norm, m_w_ff1, m_w_ff2, m_mlp_post_norm, v_mix_pre_norm, v_w_in, v_attn_out_norm, v_hgrn_lb_logits, v_hgrn_out_norm, v_w_out, v_mix_post_norm, v_mlp_pre_norm, v_w_ff1, v_w_ff2, v_mlp_post_norm):
    w = dict(w_in=w_in[0], w_out=w_out[0], w_ff1=w_ff1[0], w_ff2=w_ff2[0])
    m = dict(w_in=m_w_in[0], w_out=m_w_out[0], w_ff1=m_w_ff1[0], w_ff2=m_w_ff2[0])
    v = dict(w_in=v_w_in[0], w_out=v_w_out[0], w_ff1=v_w_ff1[0], w_ff2=v_w_ff2[0])

    gathered = all_gather([w[k].astype(BF16) for k in BIG])
    w_in_b = gathered[0].transpose(1, 0, 2).reshape(D_MODEL, IN_PROJ_WIDTH)
    w_out_b = gathered[1].reshape(D_MODEL, D_MODEL)
    w1_blocks = gathered[2]
    w2_b = gathered[3].reshape(D_FF, D_MODEL)

    dx, dwin, dwout, dw1, dw2, small = local_step(
        x[0], loss_target[0], mix_pre_norm, w_in_b, attn_out_norm, hgrn_lb_logits, hgrn_out_norm, w_out_b,
        mix_post_norm, mlp_pre_norm, w1_blocks, w2_b, mlp_post_norm)

    shard_w = IN_PROJ_WIDTH // N_DEV
    grads = [
        dwin.reshape(N_DEV // 2, D_MODEL, 2, shard_w).transpose(0, 2, 1, 3).reshape(N_DEV, D_MODEL, shard_w),
        dwout.reshape(N_DEV, D_MODEL // N_DEV, D_MODEL),
        dw1,
        dw2.reshape(N_DEV, D_FF // N_DEV, D_MODEL),
    ]
    from_sibling = reduce_to_core(grads)
    pairs = [pair_sum(g, s, f"pair_sum_{k}") for g, s, k in zip(grads, from_sibling, BIG)]
    others = reduce_to_chip(pairs)
    big = {k: sum_adamw(p, o, w[k], m[k], v[k], f"sum_adamw_{k}") for k, p, o in zip(BIG, pairs, others)}

    packed_g = _pack_small(small["dg1"], small["dgp"], small["dg_pre"], small["dg_post"], small["dan"], small["dhn"],
                           small["dlb"], small["loss_vec"])
    reduced = small_all_reduce(packed_g)
    pack = lambda a, b, c2, d, e, f, g: _pack_small(a, b, c2, d, e, f, g)
    w_s = pack(mix_pre_norm, mix_post_norm, mlp_pre_norm, mlp_post_norm, attn_out_norm, hgrn_out_norm, hgrn_lb_logits)
    m_s = pack(m_mix_pre_norm, m_mix_post_norm, m_mlp_pre_norm, m_mlp_post_norm, m_attn_out_norm, m_hgrn_out_norm,
               m_hgrn_lb_logits)
    v_s = pack(v_mix_pre_norm, v_mix_post_norm, v_mlp_pre_norm, v_mlp_post_norm, v_attn_out_norm, v_hgrn_out_norm,
               v_hgrn_lb_logits)
    g_s, d_s, nm_s, nv_s, loss = small_adamw(reduced, w_s, m_s, v_s)
    small_out = [_unpack_small(t) for t in (g_s, d_s, nm_s, nv_s)]

    outs = [loss[0, 0], dx[None]]
    for kind in range(4):
        for name in ORDER:
            outs.append(big[name][kind][None] if name in BIG else small_out[kind][name])
    return tuple(outs)
```

```python
import functools
import math

import jax
import jax.numpy as jnp
from jax import lax
from jax.experimental import pallas as pl
from jax.experimental.pallas import tpu as pltpu

F32 = jnp.float32
BF16 = jnp.bfloat16

D_MODEL = 1024
SEQ = 4096
ATTN_WIDTH = 512
ATTN_HEAD_DIM = 64
ATTN_HEADS = 8
ATTN_BLOCK = 128
DILATIONS = (1, 4, 16)
HGRN_WIDTH = 512
HGRN_HEADS = 4
HGRN_HEAD_DIM = 128
HGRN_CHUNK = 64
IN_PROJ_WIDTH = 3584
D_FF = 4096
RMS_EPS = 1e-6
N_DEV = 8
ADAM_LR = 0.001
ADAM_B1 = 0.9
ADAM_B2 = 0.999
ADAM_EPS = 1e-08
ADAM_WD = 0.01
ADAM_STEP = 10

SUBLANES = 8
LANES = 128
COLUMN_UNROLL = 4
TOKEN_TILE = 256
VMEM_LIMIT = 56 * 1024 * 1024
NEG_BIG = -1e30
MESH = pl.DeviceIdType.MESH


def _params(**kw):
    return pltpu.CompilerParams(vmem_limit_bytes=VMEM_LIMIT, **kw)


def _vmem_spec():
    return pl.BlockSpec(memory_space=pltpu.VMEM)


def _dot(a, b):
    return jnp.dot(a, b, preferred_element_type=F32)


def _dot_nt(a, b):
    return lax.dot_general(a, b, (((1,), (1,)), ((), ())), preferred_element_type=F32)


def _dot_tn(a, b):
    return lax.dot_general(a, b, (((0,), (0,)), ((), ())), preferred_element_type=F32)


def _sigmoid(x):
    return 1.0 / (1.0 + jnp.exp(-x))


def _rms_fwd(x, gain, width):
    r = lax.rsqrt(jnp.sum(x * x, axis=-1, keepdims=True) * (1.0 / width) + RMS_EPS)
    return x * r * gain


def _rms_bwd(dy, x, gain, width):
    r = lax.rsqrt(jnp.sum(x * x, axis=-1, keepdims=True) * (1.0 / width) + RMS_EPS)
    xhat = x * r
    dxhat = dy * gain
    dx = r * (dxhat - xhat * (jnp.sum(dxhat * xhat, axis=-1, keepdims=True) * (1.0 / width)))
    return dx, dy * xhat


def _split3(x):
    hi = x.astype(BF16)
    r1 = x - hi.astype(F32)
    mid = r1.astype(BF16)
    lo = (r1 - mid.astype(F32)).astype(BF16)
    return hi, mid, lo


def _tri_sum(tri_bf16, x):
    hi, mid, lo = _split3(x)
    return _dot(tri_bf16, hi) + _dot(tri_bf16, mid) + _dot(tri_bf16, lo)


def _dilated_spec(d, tm, width):
    return pl.BlockSpec((d, tm // d, width), lambda i: (0, i, 0))


def _lane_blocks(ref, value):
    for c in range(ref.shape[0]):
        ref[c] = value[:, c * LANES:(c + 1) * LANES]


def _to_dilated(src_ref, dst_ref, d, tm, cast=None):
    for r in range(d):
        for c in range(src_ref.shape[0]):
            v = src_ref[c] if d == 1 else src_ref[c, pl.ds(r, tm // d, stride=d), :]
            dst_ref[r, :, c * LANES:(c + 1) * LANES] = v if cast is None else v.astype(cast)


def _from_dilated(src_ref, scratch_ref, d, tm):
    if d == 1:
        return src_ref[0]
    nblk = scratch_ref.shape[0]
    for r in range(d):
        for c in range(nblk):
            scratch_ref[c, pl.ds(r, tm // d, stride=d), :] = src_ref[r, :, c * LANES:(c + 1) * LANES]
    return jnp.concatenate([scratch_ref[c] for c in range(nblk)], axis=1)


def in_proj_fwd(x, g1, w_in_b):
    s = x.shape[0]
    tm = TOKEN_TILE
    qkv_w = 3 * ATTN_WIDTH
    hg_w = IN_PROJ_WIDTH - qkv_w

    def body(x_ref, g_ref, w_ref, hg_ref, h_ref, *rest):
        qkv_refs, qkv_scr = rest[:len(DILATIONS)], rest[len(DILATIONS)]
        h = _rms_fwd(x_ref[...], g_ref[...], D_MODEL).astype(BF16)
        h_ref[...] = h
        proj = _dot(h, w_ref[...])
        hg_ref[...] = proj[:, qkv_w:]
        _lane_blocks(qkv_scr, proj[:, :qkv_w])
        for d, ref in zip(DILATIONS, qkv_refs):
            _to_dilated(qkv_scr, ref, d, tm, cast=BF16)

    return pl.pallas_call(
        body,
        name="in_proj_fwd",
        grid=(s // tm,),
        in_specs=[
            pl.BlockSpec((tm, D_MODEL), lambda i: (i, 0)),
            pl.BlockSpec((1, D_MODEL), lambda i: (0, 0)),
            _vmem_spec(),
        ],
        out_specs=[
            pl.BlockSpec((tm, hg_w), lambda i: (i, 0)),
            pl.BlockSpec((tm, D_MODEL), lambda i: (i, 0)),
        ] + [_dilated_spec(d, tm, qkv_w) for d in DILATIONS],
        out_shape=[jax.ShapeDtypeStruct((s, hg_w), F32), jax.ShapeDtypeStruct((s, D_MODEL), BF16)] + [
            jax.ShapeDtypeStruct((d, s // d, qkv_w), BF16) for d in DILATIONS],
        scratch_shapes=[pltpu.VMEM((qkv_w // LANES, tm, LANES), F32)],
        compiler_params=_params(dimension_semantics=("arbitrary",)),
    )(x, g1, w_in_b)


def _attn_scores(qm, kcat, head, dilation, first_block):
    s = _dot_nt(qm, kcat) * (ATTN_HEAD_DIM ** -0.5)
    qi = lax.broadcasted_iota(jnp.int32, (ATTN_BLOCK, 2 * ATTN_BLOCK), 0)
    kj = lax.broadcasted_iota(jnp.int32, (ATTN_BLOCK, 2 * ATTN_BLOCK), 1)
    dist = qi + ATTN_BLOCK - kj
    valid = (dist >= 0) & (dist <= ATTN_BLOCK) & ((kj >= ATTN_BLOCK) | jnp.logical_not(first_block))
    slope = 2.0 ** (-8.0 * (head + 1) / ATTN_HEADS)
    bias = dist.astype(F32) * (-slope * dilation)
    return jnp.where(valid, s + bias, NEG_BIG)


def _lane_half(shape, sub):
    lane = lax.broadcasted_iota(jnp.int32, shape, 1)
    return (lane < ATTN_HEAD_DIM) if sub == 0 else (lane >= ATTN_HEAD_DIM)


def _sub_block(col, row):
    return pl.BlockSpec((None, ATTN_BLOCK, ATTN_WIDTH), lambda r, n: (r, row(n), col))


def attn_fwd(qkv, dilation):
    d, length, _ = qkv.shape
    assert d == dilation
    nb = length // ATTN_BLOCK

    def body(q_ref, kc_ref, kp_ref, vc_ref, vp_ref, o_ref, lse_ref):
        first = pl.program_id(1) == 0
        for pair in range(ATTN_HEADS // 2):
            lanes = slice(pair * 128, (pair + 1) * 128)
            q2 = q_ref[:, lanes]
            kcat = jnp.concatenate([kp_ref[:, lanes], kc_ref[:, lanes]], axis=0)
            vcat = jnp.concatenate([vp_ref[:, lanes], vc_ref[:, lanes]], axis=0)
            o_pair = jnp.zeros((ATTN_BLOCK, 128), F32)
            lse_pair = jnp.zeros((ATTN_BLOCK, 128), F32)
            for sub in range(2):
                keep = _lane_half((ATTN_BLOCK, 128), sub)
                qm = jnp.where(keep, q2, jnp.zeros_like(q2))
                sc = _attn_scores(qm, kcat, 2 * pair + sub, d, first)
                m = jnp.max(sc, axis=-1, keepdims=True)
                p = jnp.exp(sc - m)
                den = jnp.sum(p, axis=-1, keepdims=True)
                o = _dot(p.astype(BF16), vcat) / den
                o_pair = jnp.where(keep, o, o_pair)
                lse_pair = jnp.where(keep, m + jnp.log(den), lse_pair)
            o_ref[:, lanes] = o_pair
            lse_ref[:, lanes] = lse_pair

    cur = lambda n: n
    prev = lambda n: jnp.maximum(n - 1, 0)
    return pl.pallas_call(
        body,
        name=f"attn_fwd_d{d}",
        grid=(d, nb),
        in_specs=[_sub_block(0, cur), _sub_block(1, cur), _sub_block(1, prev), _sub_block(2, cur), _sub_block(2, prev)],
        out_specs=[_sub_block(0, cur), _sub_block(0, cur)],
        out_shape=[jax.ShapeDtypeStruct((d, length, ATTN_WIDTH), F32)] * 2,
        compiler_params=_params(dimension_semantics=("arbitrary", "arbitrary")),
    )(qkv, qkv, qkv, qkv, qkv)


def attn_bwd(qkv, d_out, lse, delta, dilation, ride=None):
    d, length, _ = qkv.shape
    assert d == dilation
    nb = length // ATTN_BLOCK

    def body(q_ref, kc_ref, kp_ref, vc_ref, vp_ref, do_ref, lse_ref, dl_ref, dq_ref, dk_ref, dv_ref, ck_ref, cv_ref):
        n = pl.program_id(1)

        @pl.when(n == 0)
        def _():
            ck_ref[...] = jnp.zeros_like(ck_ref)
            cv_ref[...] = jnp.zeros_like(cv_ref)

        @pl.when(n < nb)
        def _():
            first = n == 0
            for pair in range(ATTN_HEADS // 2):
                lanes = slice(pair * 128, (pair + 1) * 128)
                q2 = q_ref[:, lanes]
                do2 = do_ref[:, lanes]
                kcat = jnp.concatenate([kp_ref[:, lanes], kc_ref[:, lanes]], axis=0)
                vcat = jnp.concatenate([vp_ref[:, lanes], vc_ref[:, lanes]], axis=0)
                dq_pair = jnp.zeros((ATTN_BLOCK, 128), F32)
                dk_cat = jnp.zeros((2 * ATTN_BLOCK, 128), F32)
                dv_cat = jnp.zeros((2 * ATTN_BLOCK, 128), F32)
                for sub in range(2):
                    keep = _lane_half((ATTN_BLOCK, 128), sub)
                    col = pair * 128 + sub * ATTN_HEAD_DIM
                    qm = jnp.where(keep, q2, jnp.zeros_like(q2))
                    dom = jnp.where(keep, do2, 0.0).astype(BF16)
                    sc = _attn_scores(qm, kcat, 2 * pair + sub, d, first)
                    p = jnp.exp(sc - lse_ref[:, col:col + 1])
                    dp = _dot_nt(dom, vcat)
                    ds = (p * (dp - dl_ref[:, col:col + 1]) * (ATTN_HEAD_DIM ** -0.5)).astype(BF16)
                    dq_pair = jnp.where(keep, _dot(ds, kcat), dq_pair)
                    dk_cat = dk_cat + _dot_tn(ds, qm)
                    dv_cat = dv_cat + _dot_tn(p.astype(BF16), dom)
                dq_ref[:, lanes] = dq_pair
                dk_ref[:, lanes] = ck_ref[:, lanes] + dk_cat[:ATTN_BLOCK]
                dv_ref[:, lanes] = cv_ref[:, lanes] + dv_cat[:ATTN_BLOCK]
                ck_ref[:, lanes] = dk_cat[ATTN_BLOCK:]
                cv_ref[:, lanes] = dv_cat[ATTN_BLOCK:]

        @pl.when(n == nb)
        def _():
            dk_ref[...] = ck_ref[...]
            dv_ref[...] = cv_ref[...]

    blk = (ATTN_BLOCK, ATTN_WIDTH)
    cur = lambda n: jnp.minimum(n, nb - 1)
    prev = lambda n: jnp.maximum(jnp.minimum(n, nb - 1) - 1, 0)
    done = lambda n: jnp.maximum(n - 1, 0)
    step = lambda r, n: (lambda: (pl.program_id(0) == r) & (pl.program_id(1) == n))
    e_in, e_out, e_shape, e_scr, e_args = _ride_specs(ride)
    return pl.pallas_call(
        _riding(body, 8, 3, 2, ride, step(0, 0), step(d // 2, nb // 2), step(d - 1, nb)),
        name=f"attn_bwd_d{d}",
        grid=(d, nb + 1),
        in_specs=[_sub_block(0, cur), _sub_block(1, cur), _sub_block(1, prev), _sub_block(2, cur), _sub_block(2, prev),
                  _sub_block(0, cur), _sub_block(0, cur), _sub_block(0, cur)] + e_in,
        out_specs=[_sub_block(0, cur), _sub_block(0, done), _sub_block(0, done)] + e_out,
        out_shape=[jax.ShapeDtypeStruct((d, length, ATTN_WIDTH), F32)] * 3 + e_shape,
        scratch_shapes=[pltpu.VMEM(blk, F32), pltpu.VMEM(blk, F32)] + e_scr,
        compiler_params=_params(dimension_semantics=("arbitrary", "arbitrary")),
    )(qkv, qkv, qkv, qkv, qkv, d_out, lse, delta, *e_args)


def _lower_bound(logits):
    return _sigmoid(logits[0:1, :] - logits[1:2, :])


def _hgrn_gates(q, fp, lb):
    sq = _sigmoid(q)
    qf = q * sq
    sig = _sigmoid(fp)
    f = lb + (1.0 - lb) * sig
    kf = (1.0 - lb) * _sigmoid(-fp)
    return sq, qf, sig, f, kf


def _tril_bf16(n, upper=False):
    r = lax.broadcasted_iota(jnp.int32, (n, n), 0)
    c = lax.broadcasted_iota(jnp.int32, (n, n), 1)
    keep = (c >= r) if upper else (c <= r)
    return jnp.where(keep, 1.0, 0.0).astype(BF16)


def hgrn_fwd(proj, lb, ride=None):
    s = proj.shape[0]
    c_len, nh, hd = HGRN_CHUNK, HGRN_HEADS, HGRN_HEAD_DIM
    n_chunks = s // c_len
    col0 = 0

    def body(q_ref, f_ref, i_ref, lb_ref, o_ref, st_out_ref, a_out_ref, st_ref, b_ref, qf_ref, kf_ref, a_ref):
        @pl.when(pl.program_id(0) == 0)
        def _():
            st_ref[...] = jnp.zeros_like(st_ref)

        lbv = _lower_bound(lb_ref[...])
        _, qf, _, f, kf = _hgrn_gates(q_ref[...], f_ref[...], lbv)
        b = _tri_sum(_tril_bf16(c_len), jnp.log(f))
        b_ref[...] = b
        qf_ref[...] = qf
        kf_ref[...] = kf
        a_ref[...] = jnp.zeros_like(a_ref)
        t_idx = lax.broadcasted_iota(jnp.int32, (c_len, nh * hd), 0)
        lane = lax.broadcasted_iota(jnp.int32, (c_len, hd), 1)

        for r0 in range(0, c_len, SUBLANES):
            rows = slice(r0, c_len)

            def column(jj, carry, r0=r0, rows=rows):
                j = r0 + jj
                bj = b_ref[pl.ds(j, 1), :]
                kj = kf_ref[pl.ds(j, 1), :]
                e = jnp.exp(jnp.where(t_idx[rows] >= j, b_ref[rows, :] - bj, NEG_BIG))
                prod = qf_ref[rows, :] * kj * e
                for h in range(nh):
                    col = jnp.sum(prod[:, h * hd:(h + 1) * hd], axis=-1, keepdims=True)
                    a_ref[h, rows, :] = jnp.where(lane[rows] == j, col, a_ref[h, rows, :])
                return carry

            lax.fori_loop(0, SUBLANES, column, 0, unroll=COLUMN_UNROLL)

        b_last = b[c_len - 1:c_len, :]
        qb = (qf * jnp.exp(b)).astype(BF16)
        kb2 = (kf * jnp.exp(b_last - b)).astype(BF16)
        vf = i_ref[...].astype(BF16)
        for h in range(nh):
            hs = slice(h * hd, (h + 1) * hd)
            st = st_ref[h]
            st_out_ref[0, h] = st
            a_h = a_ref[h]
            a_out_ref[:, hs] = a_h
            o_ref[:, hs] = _dot_nt(qb[:, hs], st.astype(BF16)) + _dot(a_h[:, :c_len].astype(BF16), vf[:, hs])
            st_ref[h] = st * jnp.exp(b_last[:, hs]) + _dot_tn(vf[:, hs], kb2[:, hs])

    blk = (c_len, HGRN_WIDTH)
    step = lambda k: (lambda: pl.program_id(0) == k)
    e_in, e_out, e_shape, e_scr, e_args = _ride_specs(ride)
    return pl.pallas_call(
        _riding(body, 4, 3, 5, ride, step(0), step(n_chunks // 2), step(n_chunks - 1)),
        name="hgrn_fwd",
        grid=(n_chunks,),
        in_specs=[
            pl.BlockSpec(blk, lambda c: (c, col0)),
            pl.BlockSpec(blk, lambda c: (c, col0 + 1)),
            pl.BlockSpec(blk, lambda c: (c, col0 + 2)),
            pl.BlockSpec((2, HGRN_WIDTH), lambda c: (0, 0)),
        ] + e_in,
        out_specs=[
            pl.BlockSpec(blk, lambda c: (c, 0)),
            pl.BlockSpec((1, nh, hd, hd), lambda c: (c, 0, 0, 0)),
            pl.BlockSpec(blk, lambda c: (c, 0)),
        ] + e_out,
        out_shape=[
            jax.ShapeDtypeStruct((s, HGRN_WIDTH), F32),
            jax.ShapeDtypeStruct((n_chunks, nh, hd, hd), F32),
            jax.ShapeDtypeStruct((s, nh * hd), F32),
        ] + e_shape,
        scratch_shapes=[
            pltpu.VMEM((nh, hd, hd), F32),
            pltpu.VMEM(blk, F32),
            pltpu.VMEM(blk, F32),
            pltpu.VMEM(blk, F32),
            pltpu.VMEM((nh, c_len, hd), F32),
        ] + e_scr,
        compiler_params=_params(dimension_semantics=("arbitrary",)),
    )(proj, proj, proj, lb, *e_args)


def hgrn_bwd(proj, lb, d_o, states, a_mat, ride=None):
    s = proj.shape[0]
    c_len, nh, hd = HGRN_CHUNK, HGRN_HEADS, HGRN_HEAD_DIM
    n_chunks = s // c_len
    col0 = 0
    last = n_chunks - 1

    def body(q_ref, f_ref, i_ref, lb_ref, do_ref, st_in_ref, a_in_ref, dq_ref, df_ref, di_ref, dlb_ref,
             dst_ref, b_ref, qf_ref, kf_ref, da_ref, dqi_ref, dki_ref):
        @pl.when(pl.program_id(0) == 0)
        def _():
            dst_ref[...] = jnp.zeros_like(dst_ref)
            dlb_ref[...] = jnp.zeros_like(dlb_ref)

        lbv = _lower_bound(lb_ref[...])
        q = q_ref[...]
        sq, qf, sig, f, kf = _hgrn_gates(q, f_ref[...], lbv)
        b = _tri_sum(_tril_bf16(c_len), jnp.log(f))
        b_ref[...] = b
        qf_ref[...] = qf
        kf_ref[...] = kf
        b_last = b[c_len - 1:c_len, :]
        eb = jnp.exp(b)
        ebl = jnp.exp(b_last - b)
        qb = qf * eb
        kb2 = kf * ebl
        vf = i_ref[...]
        d_o = do_ref[...]
        qb_b, kb2_b, vf_b, do_b = qb.astype(BF16), kb2.astype(BF16), vf.astype(BF16), d_o.astype(BF16)
        tq = lax.broadcasted_iota(jnp.int32, (c_len, hd), 0)
        lane = lax.broadcasted_iota(jnp.int32, (c_len, hd), 1)

        dqb_parts, dvf_parts, dkb2_parts, dbl_parts = [], [], [], []
        for h in range(nh):
            hs = slice(h * hd, (h + 1) * hd)
            st = st_in_ref[0, h]
            dst = dst_ref[h]
            st_b, dst_b = st.astype(BF16), dst.astype(BF16)
            a_h = a_in_ref[:, hs][:, :c_len].astype(BF16)
            dqb_parts.append(_dot(do_b[:, hs], st_b))
            dvf_parts.append(_dot_tn(a_h, do_b[:, hs]) + _dot_nt(kb2_b[:, hs], dst_b))
            dkb2_parts.append(_dot(vf_b[:, hs], dst_b))
            da = _dot_nt(do_b[:, hs], vf_b[:, hs])
            da = jnp.concatenate([da, jnp.zeros((c_len, hd - c_len), F32)], axis=1)
            da_ref[h] = jnp.where(tq >= lane, da, 0.0)
            dbl_parts.append(jnp.sum(dst * st, axis=0, keepdims=True) * jnp.exp(b_last[:, hs]))
            dst_ref[h] = dst * jnp.exp(b_last[:, hs]) + _dot_tn(do_b[:, hs], qb_b[:, hs])
        dqb = jnp.concatenate(dqb_parts, axis=1)
        dvf = jnp.concatenate(dvf_parts, axis=1)
        dkb2 = jnp.concatenate(dkb2_parts, axis=1)
        dbl = jnp.concatenate(dbl_parts, axis=1) + jnp.sum(dkb2 * kb2, axis=0, keepdims=True)

        dqi_ref[...] = jnp.zeros_like(dqi_ref)
        t_idx = lax.broadcasted_iota(jnp.int32, (c_len, nh * hd), 0)

        for r0 in range(0, c_len, SUBLANES):
            rows = slice(r0, c_len)
            nrow = c_len - r0

            def column(jj, carry, r0=r0, rows=rows, nrow=nrow):
                j = r0 + jj
                bj = b_ref[pl.ds(j, 1), :]
                kj = kf_ref[pl.ds(j, 1), :]
                e = jnp.exp(jnp.where(t_idx[rows] >= j, b_ref[rows, :] - bj, NEG_BIG))
                cols = [jnp.sum(jnp.where(lane[rows] == j, da_ref[h, rows, :], 0.0), axis=-1, keepdims=True)
                        for h in range(nh)]
                w = e * jnp.concatenate([jnp.broadcast_to(cc, (nrow, hd)) for cc in cols], axis=1)
                dqi_ref[rows, :] += w * kj
                dki_ref[pl.ds(j, 1), :] = jnp.sum(w * qf_ref[rows, :], axis=0, keepdims=True)
                return carry

            lax.fori_loop(0, SUBLANES, column, 0, unroll=COLUMN_UNROLL)
        dq_intra = dqi_ref[...]
        dk_intra = dki_ref[...]

        db = dqb * qb + qf * dq_intra - kf * dk_intra - dkb2 * kb2
        db = db + jnp.where(t_idx == c_len - 1, dbl, 0.0)
        dg = _tri_sum(_tril_bf16(c_len, upper=True), db)
        dqf = dqb * eb + dq_intra
        dkf = dkb2 * ebl + dk_intra
        dq_ref[...] = dqf * (sq * (1.0 + q * (1.0 - sq)))
        dfv = dg / f - dkf
        df_ref[...] = dfv * (1.0 - lbv) * sig * (1.0 - sig)
        di_ref[...] = dvf
        dlb_ref[...] += jnp.sum(dfv * (1.0 - sig), axis=0, keepdims=True)

    blk = (c_len, HGRN_WIDTH)
    rev = lambda c: last - c
    step = lambda k: (lambda: pl.program_id(0) == k)
    e_in, e_out, e_shape, e_scr, e_args = _ride_specs(ride)
    return pl.pallas_call(
        _riding(body, 7, 4, 7, ride, step(0), step(n_chunks // 2), step(last)),
        name="hgrn_bwd",
        grid=(n_chunks,),
        in_specs=[
            pl.BlockSpec(blk, lambda c: (rev(c), col0)),
            pl.BlockSpec(blk, lambda c: (rev(c), col0 + 1)),
            pl.BlockSpec(blk, lambda c: (rev(c), col0 + 2)),
            pl.BlockSpec((2, HGRN_WIDTH), lambda c: (0, 0)),
            pl.BlockSpec(blk, lambda c: (rev(c), 0)),
            pl.BlockSpec((1, nh, hd, hd), lambda c: (rev(c), 0, 0, 0)),
            pl.BlockSpec(blk, lambda c: (rev(c), 0)),
        ] + e_in,
        out_specs=[
            pl.BlockSpec(blk, lambda c: (rev(c), 0)),
            pl.BlockSpec(blk, lambda c: (rev(c), 0)),
            pl.BlockSpec(blk, lambda c: (rev(c), 0)),
            pl.BlockSpec((1, HGRN_WIDTH), lambda c: (0, 0)),
        ] + e_out,
        out_shape=[jax.ShapeDtypeStruct((s, HGRN_WIDTH), F32)] * 3 + [jax.ShapeDtypeStruct((1, HGRN_WIDTH), F32)] + e_shape,
        scratch_shapes=[
            pltpu.VMEM((nh, hd, hd), F32),
            pltpu.VMEM(blk, F32),
            pltpu.VMEM(blk, F32),
            pltpu.VMEM(blk, F32),
            pltpu.VMEM((nh, c_len, hd), F32),
            pltpu.VMEM(blk, F32),
            pltpu.VMEM(blk, F32),
        ] + e_scr,
        compiler_params=_params(dimension_semantics=("arbitrary",)),
    )(proj, proj, proj, lb, d_o, states, a_mat, *e_args)


def _row_spec(tm, width, col=0):
    return pl.BlockSpec((tm, width), lambda i: (i, col))


def _const_spec(width):
    return pl.BlockSpec((1, width), lambda i: (0, 0))


def _acc_rows(ref, value):
    @pl.when(pl.program_id(0) == 0)
    def _():
        ref[...] = jnp.zeros_like(ref)

    ref[...] += jnp.sum(value, axis=0, keepdims=True)


def mix_fwd(attn_parts, o_h, proj, an, hn, w_out_b, gp, x):
    s = x.shape[0]
    tm = TOKEN_TILE
    gate_col = 3
    hd = HGRN_HEAD_DIM
    nd = len(DILATIONS)

    def body(*refs):
        o_refs, l_refs = refs[:nd], refs[nd:2 * nd]
        oh_ref, gate_ref, an_ref, hn_ref, w_ref, gp_ref, x_ref = refs[2 * nd:2 * nd + 7]
        x1_ref, cat_ref, mixed_ref, attn_ref = refs[2 * nd + 7:2 * nd + 11]
        lse_refs = refs[2 * nd + 11:3 * nd + 11]
        o_scr, l_scr, lse_scr = refs[3 * nd + 11:]
        os_ = [_from_dilated(r, o_scr.at[k], d, tm) for k, (r, d) in enumerate(zip(o_refs, DILATIONS))]
        ls = [_from_dilated(r, l_scr.at[k], d, tm) for k, (r, d) in enumerate(zip(l_refs, DILATIONS))]
        m = jnp.maximum(jnp.maximum(ls[0], ls[1]), ls[2])
        es = [jnp.exp(l - m) for l in ls]
        den = es[0] + es[1] + es[2]
        attn = (es[0] * os_[0] + es[1] * os_[1] + es[2] * os_[2]) / den
        attn_ref[...] = attn
        _lane_blocks(lse_scr, m + jnp.log(den))
        for d, ref in zip(DILATIONS, lse_refs):
            _to_dilated(lse_scr, ref, d, tm)
        cat_ref[:, :ATTN_WIDTH] = _rms_fwd(attn, an_ref[...], ATTN_WIDTH).astype(BF16)
        gate = gate_ref[...]
        silu_g = gate * _sigmoid(gate)
        for h in range(HGRN_HEADS):
            hs = slice(h * hd, (h + 1) * hd)
            rec = _rms_fwd(oh_ref[:, hs], hn_ref[:, hs], hd) * silu_g[:, hs]
            cat_ref[:, ATTN_WIDTH + h * hd:ATTN_WIDTH + (h + 1) * hd] = rec.astype(BF16)
        mixed = _dot(cat_ref[...], w_ref[...])
        mixed_ref[...] = mixed
        x1_ref[...] = x_ref[...] + _rms_fwd(mixed, gp_ref[...], D_MODEL)

    aw = ATTN_WIDTH
    return pl.pallas_call(
        body,
        name="mix_fwd",
        grid=(s // tm,),
        in_specs=[_dilated_spec(d, tm, aw) for d in DILATIONS] * 2 + [
            _row_spec(tm, aw), _row_spec(tm, aw, gate_col), _const_spec(aw), _const_spec(aw), _vmem_spec(),
            _const_spec(D_MODEL), _row_spec(tm, D_MODEL)],
        out_specs=[_row_spec(tm, D_MODEL), _row_spec(tm, D_MODEL), _row_spec(tm, D_MODEL), _row_spec(tm, aw)] + [
            _dilated_spec(d, tm, aw) for d in DILATIONS],
        out_shape=[
            jax.ShapeDtypeStruct((s, D_MODEL), F32),
            jax.ShapeDtypeStruct((s, D_MODEL), BF16),
            jax.ShapeDtypeStruct((s, D_MODEL), F32),
            jax.ShapeDtypeStruct((s, aw), F32),
        ] + [jax.ShapeDtypeStruct((d, s // d, aw), F32) for d in DILATIONS],
        scratch_shapes=[pltpu.VMEM((nd, aw // LANES, tm, LANES), F32), pltpu.VMEM((nd, aw // LANES, tm, LANES), F32),
                        pltpu.VMEM((aw // LANES, tm, LANES), F32)],
        compiler_params=_params(dimension_semantics=("arbitrary",)),
    )(*[p[0] for p in attn_parts], *[p[1] for p in attn_parts], o_h, proj, an, hn, w_out_b, gp, x)


def mix_bwd(dx1, mixed, gp, w_out_b, attn, an, o_h, proj, hn):
    s = dx1.shape[0]
    tm = TOKEN_TILE
    gate_col = 3
    hd = HGRN_HEAD_DIM
    aw = ATTN_WIDTH

    nd = len(DILATIONS)

    def body(*refs):
        dx1_ref, mixed_ref, gp_ref, w_ref, attn_ref, an_ref, oh_ref, gate_ref, hn_ref, dmix_ref = refs[:10]
        do_refs, delta_refs = refs[10:10 + nd], refs[10 + nd:10 + 2 * nd]
        doh_ref, dgate_ref, dgp_ref, dan_ref, dhn_ref, do_ref, delta_ref = refs[10 + 2 * nd:]
        dmixed, gp_c = _rms_bwd(dx1_ref[...], mixed_ref[...], gp_ref[...], D_MODEL)
        _acc_rows(dgp_ref, gp_c)
        dmixed_b = dmixed.astype(BF16)
        dmix_ref[...] = dmixed_b
        dcat = _dot_nt(dmixed_b, w_ref[...])
        attn = attn_ref[...]
        d_o, an_c = _rms_bwd(dcat[:, :aw], attn, an_ref[...], aw)
        _acc_rows(dan_ref, an_c)
        _lane_blocks(do_ref, d_o)
        prod = d_o * attn
        for pair in range(ATTN_HEADS // 2):
            pp = prod[:, pair * LANES:(pair + 1) * LANES]
            low = _lane_half((tm, LANES), 0)
            lo = jnp.sum(jnp.where(low, pp, 0.0), axis=-1, keepdims=True)
            hi = jnp.sum(jnp.where(low, 0.0, pp), axis=-1, keepdims=True)
            delta_ref[pair] = jnp.where(low, lo, hi)
        for d, o_ref, l_ref in zip(DILATIONS, do_refs, delta_refs):
            _to_dilated(do_ref, o_ref, d, tm)
            _to_dilated(delta_ref, l_ref, d, tm)
        gate = gate_ref[...]
        sg = _sigmoid(gate)
        silu_g = gate * sg
        drec = dcat[:, aw:]
        hn_parts = []
        for h in range(HGRN_HEADS):
            hs = slice(h * hd, (h + 1) * hd)
            oh = oh_ref[:, hs]
            on = _rms_fwd(oh, hn_ref[:, hs], hd)
            dgate_ref[:, hs] = drec[:, hs] * on * (sg[:, hs] * (1.0 + gate[:, hs] * (1.0 - sg[:, hs])))
            d_oh, hn_c = _rms_bwd(drec[:, hs] * silu_g[:, hs], oh, hn_ref[:, hs], hd)
            doh_ref[:, hs] = d_oh
            hn_parts.append(hn_c)
        _acc_rows(dhn_ref, jnp.concatenate(hn_parts, axis=1))

    return pl.pallas_call(
        body,
        name="mix_bwd",
        grid=(s // tm,),
        in_specs=[_row_spec(tm, D_MODEL), _row_spec(tm, D_MODEL), _const_spec(D_MODEL), _vmem_spec(), _row_spec(tm, aw),
                  _const_spec(aw), _row_spec(tm, aw), _row_spec(tm, aw, gate_col), _const_spec(aw)],
        out_specs=[_row_spec(tm, D_MODEL)] + [_dilated_spec(d, tm, aw) for d in DILATIONS] * 2 + [_row_spec(tm, aw)] * 2 + [
            _const_spec(D_MODEL), _const_spec(aw), _const_spec(aw)],
        out_shape=[jax.ShapeDtypeStruct((s, D_MODEL), BF16)] + [
            jax.ShapeDtypeStruct((d, s // d, aw), F32) for d in DILATIONS] * 2 + [jax.ShapeDtypeStruct((s, aw), F32)] * 2 + [
            jax.ShapeDtypeStruct((1, D_MODEL), F32), jax.ShapeDtypeStruct((1, aw), F32),
            jax.ShapeDtypeStruct((1, aw), F32)],
        scratch_shapes=[pltpu.VMEM((aw // LANES, tm, LANES), F32), pltpu.VMEM((aw // LANES, tm, LANES), F32)],
        compiler_params=_params(dimension_semantics=("arbitrary",)),
    )(dx1, mixed, gp, w_out_b, attn, an, o_h, proj, hn)


def mlp_fwd_bwd(x1, g_pre, w1_blocks, w2_b, g_post, target):
    s = x1.shape[0]
    tm = TOKEN_TILE
    nblk, _, fb = w1_blocks.shape

    def body(x1_ref, gpre_ref, w1_ref, w2_ref, gpost_ref, t_ref,
             dx1_ref, h2_ref, a_ref, du_ref, dff_ref, loss_ref, dgpre_ref, dgpost_ref, u_ref):
        x1v = x1_ref[...]
        h2 = _rms_fwd(x1v, gpre_ref[...], D_MODEL).astype(BF16)
        h2_ref[...] = h2
        ff = jnp.zeros((tm, D_MODEL), F32)
        for j in range(nblk):
            cols = slice(j * fb, (j + 1) * fb)
            ru = jnp.maximum(_dot(h2, w1_ref[j]), 0.0)
            u_ref[:, cols] = ru
            a = (ru * ru).astype(BF16)
            a_ref[:, cols] = a
            ff = ff + _dot(a, w2_ref[cols, :])
        diff = x1v + _rms_fwd(ff, gpost_ref[...], D_MODEL) - t_ref[...]
        _acc_rows(loss_ref, diff * diff)
        dy = diff * (1.0 / D_MODEL)
        dff, gpost_c = _rms_bwd(dy, ff, gpost_ref[...], D_MODEL)
        _acc_rows(dgpost_ref, gpost_c)
        dff_b = dff.astype(BF16)
        dff_ref[...] = dff_b
        dh2 = jnp.zeros((tm, D_MODEL), F32)
        for j in range(nblk):
            cols = slice(j * fb, (j + 1) * fb)
            du = (_dot_nt(dff_b, w2_ref[cols, :]) * (2.0 * u_ref[:, cols])).astype(BF16)
            du_ref[:, cols] = du
            dh2 = dh2 + _dot_nt(du, w1_ref[j])
        dxa, gpre_c = _rms_bwd(dh2, x1v, gpre_ref[...], D_MODEL)
        _acc_rows(dgpre_ref, gpre_c)
        dx1_ref[...] = dy + dxa

    dm = D_MODEL
    return pl.pallas_call(
        body,
        name="mlp_fwd_bwd",
        grid=(s // tm,),
        in_specs=[_row_spec(tm, dm), _const_spec(dm), _vmem_spec(), _vmem_spec(), _const_spec(dm), _row_spec(tm, dm)],
        out_specs=[_row_spec(tm, dm), _row_spec(tm, dm), _row_spec(tm, D_FF), _row_spec(tm, D_FF), _row_spec(tm, dm),
                   _const_spec(dm), _const_spec(dm), _const_spec(dm)],
        out_shape=[
            jax.ShapeDtypeStruct((s, dm), F32),
            jax.ShapeDtypeStruct((s, dm), BF16),
            jax.ShapeDtypeStruct((s, D_FF), BF16),
            jax.ShapeDtypeStruct((s, D_FF), BF16),
            jax.ShapeDtypeStruct((s, dm), BF16),
            jax.ShapeDtypeStruct((1, dm), F32),
            jax.ShapeDtypeStruct((1, dm), F32),
            jax.ShapeDtypeStruct((1, dm), F32),
        ],
        scratch_shapes=[pltpu.VMEM((tm, D_FF), F32)],
        compiler_params=_params(dimension_semantics=("arbitrary",)),
    )(x1, g_pre, w1_blocks, w2_b, g_post, target)


def in_proj_bwd(attn_grads, hgrn_grads, dgate, w_in_b, x, g1, dx1):
    s = x.shape[0]
    tm = TOKEN_TILE
    aw = ATTN_WIDTH
    n_attn = len(attn_grads)
    flat = [g[k] for k in range(3) for g in attn_grads] + list(hgrn_grads) + [dgate]

    def body(*refs):
        parts = refs[:len(flat)]
        w_ref, x_ref, g_ref, dx1_ref, dx_ref, dproj_ref, dg_ref, scr = refs[len(flat):]
        groups = []
        for k in range(3):
            acc = None
            for p, d in zip(parts[k * n_attn:(k + 1) * n_attn], DILATIONS):
                v = _from_dilated(p, scr, d, tm)
                acc = v if acc is None else acc + v
            groups.append(acc)
        groups += [p[...] for p in parts[3 * n_attn:]]
        dh = jnp.zeros((tm, D_MODEL), F32)
        for gi, grp in enumerate(groups):
            cols = slice(gi * aw, (gi + 1) * aw)
            gb = grp.astype(BF16)
            dproj_ref[:, cols] = gb
            dh = dh + _dot_nt(gb, w_ref[:, cols])
        dxa, g_c = _rms_bwd(dh, x_ref[...], g_ref[...], D_MODEL)
        _acc_rows(dg_ref, g_c)
        dx_ref[...] = dx1_ref[...] + dxa

    dm = D_MODEL
    return pl.pallas_call(
        body,
        name="in_proj_bwd",
        grid=(s // tm,),
        in_specs=[_dilated_spec(d, tm, aw) for d in DILATIONS] * 3 + [_row_spec(tm, aw)] * 4 + [
            _vmem_spec(), _row_spec(tm, dm), _const_spec(dm), _row_spec(tm, dm)],
        out_specs=[_row_spec(tm, dm), _row_spec(tm, IN_PROJ_WIDTH), _const_spec(dm)],
        out_shape=[jax.ShapeDtypeStruct((s, dm), F32), jax.ShapeDtypeStruct((s, IN_PROJ_WIDTH), BF16),
                   jax.ShapeDtypeStruct((1, dm), F32)],
        scratch_shapes=[pltpu.VMEM((aw // LANES, tm, LANES), F32)],
        compiler_params=_params(dimension_semantics=("arbitrary",)),
    )(*flat, w_in_b, x, g1, dx1)


def wgrad(a_b, b_b, tn, name):
    s, k = a_b.shape
    n = b_b.shape[1]
    ts = 512

    def body(a_ref, b_ref, o_ref):
        @pl.when(pl.program_id(1) == 0)
        def _():
            o_ref[...] = jnp.zeros_like(o_ref)

        o_ref[0] += _dot_tn(a_ref[...], b_ref[...])

    return pl.pallas_call(
        body,
        name=name,
        grid=(n // tn, s // ts),
        in_specs=[pl.BlockSpec((ts, k), lambda j, i: (i, 0)), pl.BlockSpec((ts, tn), lambda j, i: (i, j))],
        out_specs=pl.BlockSpec((1, k, tn), lambda j, i: (j, 0, 0)),
        out_shape=jax.ShapeDtypeStruct((n // tn, k, tn), F32),
        compiler_params=_params(dimension_semantics=("arbitrary", "arbitrary")),
    )(a_b, b_b)


def train_step(x, target, g1, an, logits, hn, gp, g_pre, g_post, w, m, v):
    nd = len(DILATIONS)
    shard_b = {k: w[k].astype(BF16) for k in BIG}
    (w_in_g,) = run_exchange(gather_exchange([shard_b["w_in"]]), "gather_w_in")
    w_in_b = w_in_g.transpose(1, 0, 2).reshape(D_MODEL, IN_PROJ_WIDTH)

    proj, h_b, *qkvs = in_proj_fwd(x, g1, w_in_b)
    attn_parts = [attn_fwd(qkv, d) for qkv, d in zip(qkvs, DILATIONS)]
    o_h, states, a_mat, w_out_g, w1_blocks, w2_g = hgrn_fwd(
        proj, logits, ride=gather_exchange([shard_b["w_out"], shard_b["w_ff1"], shard_b["w_ff2"]]))
    w_out_b = w_out_g.reshape(D_MODEL, D_MODEL)
    w2_b = w2_g.reshape(D_FF, D_MODEL)
    x1, cat_b, mixed, attn, *lses = mix_fwd(attn_parts, o_h, proj, an, hn, w_out_b, gp, x)
    dx1, h2_b, a_b, du_b, dff_b, loss_vec, dg_pre, dg_post = mlp_fwd_bwd(x1, g_pre, w1_blocks, w2_b, g_post, target)
    dw2 = wgrad(a_b, dff_b, D_MODEL, "wgrad_ff2")
    dw1 = wgrad(h2_b, du_b, D_FF // N_DEV, "wgrad_ff1")
    dmix_b, *rest = mix_bwd(dx1, mixed, gp, w_out_b, attn, an, o_h, proj, hn)
    d_os, deltas = rest[:nd], rest[nd:2 * nd]
    d_oh, dgate, dgp, dan, dhn = rest[2 * nd:]
    dwout = wgrad(cat_b, dmix_b, D_MODEL, "wgrad_out")

    early = ("w_out", "w_ff1", "w_ff2")
    early_grads = [dwout.reshape(N_DEV, D_MODEL // N_DEV, D_MODEL), dw1, dw2.reshape(N_DEV, D_FF // N_DEV, D_MODEL)]
    attn_grads = []
    for k, d in enumerate(DILATIONS):
        ride = to_core_exchange(early_grads) if k == 0 else None
        res = attn_bwd(qkvs[k], d_os[k], lses[k], deltas[k], d, ride=ride)
        attn_grads.append(res[:3])
        if k == 0:
            pairs = [pair_sum(g, s, f"pair_sum_{name}") for g, s, name in zip(early_grads, res[3:], early)]
    dq_h, df_h, di_h, dlb, *others = hgrn_bwd(proj, logits, d_oh, states, a_mat, ride=to_chip_exchange(pairs))
    dx, dproj_b, dg1 = in_proj_bwd(attn_grads, (dq_h, df_h, di_h), dgate, w_in_b, x, g1, dx1)
    dwin = wgrad(h_b, dproj_b, 2 * IN_PROJ_WIDTH // N_DEV, "wgrad_in")
    big = {name: sum_adamw(p, o, w[name], m[name], v[name], f"sum_adamw_{name}")
           for name, p, o in zip(early, pairs, others)}

    shard_w = IN_PROJ_WIDTH // N_DEV
    dwin_blocks = dwin.reshape(N_DEV // 2, D_MODEL, 2, shard_w).transpose(0, 2, 1, 3).reshape(N_DEV, D_MODEL, shard_w)
    (from_sibling,) = run_exchange(to_core_exchange([dwin_blocks]), "reduce_w_in_to_core")
    pair_in = pair_sum(dwin_blocks, from_sibling, "pair_sum_w_in")
    (others_in,) = run_exchange(to_chip_exchange([pair_in]), "reduce_w_in_to_chip")
    big["w_in"] = sum_adamw(pair_in, others_in, w["w_in"], m["w_in"], v["w_in"], "sum_adamw_w_in")
    small = dict(dg1=dg1, dan=dan, dlb=dlb, dhn=dhn, dgp=dgp, dg_pre=dg_pre, dg_post=dg_post, loss_vec=loss_vec)
    return dx, big, small


def _position():
    x, y, c = lax.axis_index("x"), lax.axis_index("y"), lax.axis_index("c")
    other_chips = [(1 - x, y), (x, 1 - y), (1 - x, 1 - y)]
    return x, y, c, other_chips


def _any_spec():
    return pl.BlockSpec(memory_space=pl.ANY)


class Exchange:
    def __init__(self, arrays, out_shape, sems, stages):
        self.arrays, self.out_shape, self.sems, self.stages = list(arrays), list(out_shape), list(sems), stages


def gather_exchange(shards):
    n = len(shards)

    def stages(ins, outs, sems):
        send_sems, recv_sems, local_sems = sems

        def parts():
            x, y, c, chips = _position()
            me, sibling = (x, y, c), (x, y, 1 - c)

            def slot(a, px, py, pc):
                return outs[a].at[4 * px + 2 * py + pc]

            def copy(a, k, block, to, src=None):
                return pltpu.make_async_remote_copy(
                    src_ref=slot(a, *block) if src is None else src, dst_ref=slot(a, *block),
                    send_sem=send_sems.at[a, k], recv_sem=recv_sems.at[a, k], device_id=to, device_id_type=MESH)

            local = [pltpu.make_async_copy(ins[a], slot(a, *me), local_sems.at[a]) for a in range(n)]
            first = []
            for a in range(n):
                first.append(copy(a, 0, me, sibling, src=ins[a]))
                first += [copy(a, 1 + j, me, (*chip, c), src=ins[a]) for j, chip in enumerate(chips)]
            passed = [copy(a, 4 + j, (*chip, c), sibling) for j, chip in enumerate(chips) for a in range(n)]
            return c, chips, me, sibling, copy, local, first, passed

        def begin():
            _, _, _, _, _, local, first, _ = parts()
            for cp in local + first:
                cp.start()

        def middle():
            c, chips, me, _, copy, _, _, passed = parts()
            k = 0
            for j, chip in enumerate(chips):
                for a in range(n):
                    copy(a, 1 + j, (*chip, c), me).wait_recv()
                    passed[k].start()
                    k += 1

        def end():
            c, chips, me, sibling, copy, local, first, passed = parts()
            for a in range(n):
                copy(a, 0, sibling, me).wait_recv()
                for j, chip in enumerate(chips):
                    copy(a, 4 + j, (*chip, 1 - c), me).wait_recv()
            for cp in first + passed:
                cp.wait_send()
            for cp in local:
                cp.wait()

        return begin, middle, end

    return Exchange(
        shards, [jax.ShapeDtypeStruct((N_DEV,) + sh.shape, sh.dtype) for sh in shards],
        [pltpu.SemaphoreType.DMA((n, 7)), pltpu.SemaphoreType.DMA((n, 7)), pltpu.SemaphoreType.DMA((n,))], stages)


def to_core_exchange(grads):
    n = len(grads)

    def stages(ins, outs, sems):
        send_sems, recv_sems = sems

        def copies():
            x, y, c, _ = _position()
            return [pltpu.make_async_remote_copy(
                src_ref=ins[a].at[2 * q + (1 - c)], dst_ref=outs[a].at[q], send_sem=send_sems.at[a, q],
                recv_sem=recv_sems.at[a, q], device_id=(x, y, 1 - c), device_id_type=MESH)
                for a in range(n) for q in range(4)]

        def begin():
            for cp in copies():
                cp.start()

        def end():
            for cp in copies():
                cp.wait()

        return begin, None, end

    return Exchange(grads, [jax.ShapeDtypeStruct((4,) + g.shape[1:], g.dtype) for g in grads],
                    [pltpu.SemaphoreType.DMA((n, 4)), pltpu.SemaphoreType.DMA((n, 4))], stages)


def pair_sum(grad, from_sibling, name):
    _, r, cdim = grad.shape
    tr = min(r, 256)
    c_idx = lax.axis_index("c").astype(jnp.int32).reshape(1)

    def body(c_ref, g_ref, s_ref, o_ref):
        o_ref[...] = g_ref[...] + s_ref[...]

    return pl.pallas_call(
        body,
        name=name,
        grid_spec=pltpu.PrefetchScalarGridSpec(
            num_scalar_prefetch=1,
            grid=(4, r // tr),
            in_specs=[pl.BlockSpec((1, tr, cdim), lambda q, i, cr: (2 * q + cr[0], i, 0)),
                      pl.BlockSpec((1, tr, cdim), lambda q, i, cr: (q, i, 0))],
            out_specs=pl.BlockSpec((1, tr, cdim), lambda q, i, cr: (q, i, 0)),
        ),
        out_shape=jax.ShapeDtypeStruct((4, r, cdim), grad.dtype),
        compiler_params=_params(dimension_semantics=("arbitrary", "arbitrary")),
    )(c_idx, grad, from_sibling)


def to_chip_exchange(pairs):
    n = len(pairs)

    def stages(ins, outs, sems):
        send_sems, recv_sems = sems

        def copies():
            x, y, c, chips = _position()
            return [pltpu.make_async_remote_copy(
                src_ref=ins[a].at[2 * px + py], dst_ref=outs[a].at[j], send_sem=send_sems.at[a, j],
                recv_sem=recv_sems.at[a, j], device_id=(px, py, c), device_id_type=MESH)
                for a in range(n) for j, (px, py) in enumerate(chips)]

        def begin():
            for cp in copies():
                cp.start()

        def end():
            for cp in copies():
                cp.wait()

        return begin, None, end

    return Exchange(pairs, [jax.ShapeDtypeStruct((3,) + p.shape[1:], p.dtype) for p in pairs],
                    [pltpu.SemaphoreType.DMA((n, 3)), pltpu.SemaphoreType.DMA((n, 3))], stages)


def run_exchange(ex, name):
    n_in, n_out = len(ex.arrays), len(ex.out_shape)

    def body(*refs):
        begin, middle, end = ex.stages(refs[:n_in], refs[n_in:n_in + n_out], refs[n_in + n_out:])
        begin()
        if middle is not None:
            middle()
        end()

    return pl.pallas_call(
        body,
        name=name,
        in_specs=[_any_spec()] * n_in,
        out_specs=[_any_spec()] * n_out,
        out_shape=ex.out_shape,
        scratch_shapes=ex.sems,
    )(*ex.arrays)


def _riding(body, n_in, n_out, n_scratch, ex, first, middle, last):
    if ex is None:
        return body
    r_in, r_out = len(ex.arrays), len(ex.out_shape)

    def wrapped(*refs):
        k_in, refs = refs[:n_in], refs[n_in:]
        e_in, refs = refs[:r_in], refs[r_in:]
        k_out, refs = refs[:n_out], refs[n_out:]
        e_out, refs = refs[:r_out], refs[r_out:]
        k_scr, e_sems = refs[:n_scratch], refs[n_scratch:]
        begin, mid, end = ex.stages(e_in, e_out, e_sems)
        pl.when(first())(begin)
        body(*k_in, *k_out, *k_scr)
        if mid is not None:
            pl.when(middle())(mid)
        pl.when(last())(end)

    return wrapped


def _ride_specs(ex):
    if ex is None:
        return [], [], [], [], []
    return [_any_spec()] * len(ex.arrays), [_any_spec()] * len(ex.out_shape), ex.out_shape, ex.sems, ex.arrays


def _adamw(w, g, m, v):
    m = ADAM_B1 * m + (1.0 - ADAM_B1) * g
    v = ADAM_B2 * v + (1.0 - ADAM_B2) * (g * g)
    m_hat = m / (1.0 - ADAM_B1 ** ADAM_STEP)
    v_hat = v / (1.0 - ADAM_B2 ** ADAM_STEP)
    delta = -ADAM_LR * (m_hat / (jnp.sqrt(v_hat) + ADAM_EPS) + ADAM_WD * w)
    return delta, m, v


def sum_adamw(pairs, others, w, m, v, name):
    r, cdim = w.shape
    tr = min(r, 256)
    chip_idx = (2 * lax.axis_index("x") + lax.axis_index("y")).astype(jnp.int32).reshape(1)

    def body(q_ref, p_ref, o_ref, w_ref, m_ref, v_ref, g_out, d_out, m_out, v_out):
        g = p_ref[0] + o_ref[0] + o_ref[1] + o_ref[2]
        g_out[...] = g
        d_out[...], m_out[...], v_out[...] = _adamw(w_ref[...], g, m_ref[...], v_ref[...])

    tile = lambda: pl.BlockSpec((tr, cdim), lambda i, qr: (i, 0))
    return pl.pallas_call(
        body,
        name=name,
        grid_spec=pltpu.PrefetchScalarGridSpec(
            num_scalar_prefetch=1,
            grid=(r // tr,),
            in_specs=[pl.BlockSpec((1, tr, cdim), lambda i, qr: (qr[0], i, 0)),
                      pl.BlockSpec((3, tr, cdim), lambda i, qr: (0, i, 0)), tile(), tile(), tile()],
            out_specs=[tile(), tile(), tile(), tile()],
        ),
        out_shape=[jax.ShapeDtypeStruct((r, cdim), F32)] * 4,
        compiler_params=_params(dimension_semantics=("arbitrary",)),
    )(chip_idx, pairs, others, w, m, v)


SMALL_ROWS = 8


def small_all_reduce(packed):
    shape = packed.shape

    def body(in_ref, out_ref, recv_ref, send_sems, recv_sems):
        x, y, c, _ = _position()
        my_id = 4 * x + 2 * y + c
        recv_ref[my_id] = in_ref[...]
        copies = []
        for rel in range(1, N_DEV):
            fx, fy, fc = (rel >> 2) & 1, (rel >> 1) & 1, rel & 1
            px = 1 - x if fx else x
            py = 1 - y if fy else y
            pc = 1 - c if fc else c
            cp = pltpu.make_async_remote_copy(
                src_ref=in_ref, dst_ref=recv_ref.at[my_id], send_sem=send_sems.at[rel - 1],
                recv_sem=recv_sems.at[rel - 1], device_id=(px, py, pc), device_id_type=MESH)
            cp.start()
            copies.append((cp, pltpu.make_async_remote_copy(
                src_ref=in_ref, dst_ref=recv_ref.at[4 * px + 2 * py + pc], send_sem=send_sems.at[rel - 1],
                recv_sem=recv_sems.at[rel - 1], device_id=(px, py, pc), device_id_type=MESH)))
        for cp, landing in copies:
            landing.wait_recv()
        for cp, landing in copies:
            cp.wait_send()
        total = recv_ref[0]
        for k in range(1, N_DEV):
            total = total + recv_ref[k]
        out_ref[...] = total

    return pl.pallas_call(
        body,
        name="small_all_reduce",
        in_specs=[_vmem_spec()],
        out_specs=_vmem_spec(),
        out_shape=jax.ShapeDtypeStruct(shape, F32),
        scratch_shapes=[pltpu.VMEM((N_DEV,) + shape, F32), pltpu.SemaphoreType.DMA((N_DEV - 1,)),
                        pltpu.SemaphoreType.DMA((N_DEV - 1,))],
    )(packed)


def small_adamw(reduced, w, m, v):
    def body(r_ref, w_ref, m_ref, v_ref, g_out, d_out, m_out, v_out, loss_out):
        red = r_ref[...]
        wv = w_ref[...]
        lb = _lower_bound(jnp.concatenate([wv[5:6, :HGRN_WIDTH], wv[5:6, HGRN_WIDTH:]], axis=0))
        t = red[5:6, :HGRN_WIDTH] * lb * (1.0 - lb)
        row = lax.broadcasted_iota(jnp.int32, red.shape, 0)
        g = jnp.where(row == 5, jnp.concatenate([t, -t], axis=1), jnp.where(row >= 6, 0.0, red))
        g_out[...] = g
        d_out[...], m_out[...], v_out[...] = _adamw(wv, g, m_ref[...], v_ref[...])
        loss = jnp.sum(red[6:7, :], axis=-1, keepdims=True) * (0.5 / D_MODEL)
        loss_out[...] = jnp.broadcast_to(loss, loss_out.shape)

    return pl.pallas_call(
        body,
        name="small_adamw",
        in_specs=[_vmem_spec()] * 4,
        out_specs=[_vmem_spec()] * 5,
        out_shape=[jax.ShapeDtypeStruct(reduced.shape, F32)] * 4 + [jax.ShapeDtypeStruct((8, 128), F32)],
    )(reduced, w, m, v)


def _pack_small(g1, gp, g_pre, g_post, an, hn, logits_or_dlb, extra=None):
    row5 = logits_or_dlb.reshape(1, -1)
    row5 = jnp.pad(row5, ((0, 0), (0, D_MODEL - row5.shape[1])))
    row6 = jnp.zeros((1, D_MODEL), F32) if extra is None else extra
    return jnp.concatenate([g1, gp, g_pre, g_post, jnp.concatenate([an, hn], axis=1), row5, row6,
                            jnp.zeros((1, D_MODEL), F32)], axis=0)


def _unpack_small(p):
    return dict(mix_pre_norm=p[0:1], mix_post_norm=p[1:2], mlp_pre_norm=p[2:3], mlp_post_norm=p[3:4],
                attn_out_norm=p[4:5, :ATTN_WIDTH], hgrn_out_norm=p[4:5, ATTN_WIDTH:],
                hgrn_lb_logits=p[5].reshape(2, HGRN_WIDTH))


BIG = ("w_in", "w_out", "w_ff1", "w_ff2")
ORDER = ("mix_pre_norm", "w_in", "attn_out_norm", "hgrn_lb_logits", "hgrn_out_norm", "w_out", "mix_post_norm",
         "mlp_pre_norm", "w_ff1", "w_ff2", "mlp_post_norm")


def kernel(x, mix_pre_norm, w_in, attn_out_norm, hgrn_lb_logits, hgrn_out_norm, w_out, mix_post_norm, mlp_pre_norm, w_ff1, w_ff2, mlp_post_norm, loss_target, m_mix_pre_norm, m_w_in, m_attn_out_norm, m_hgrn_lb_logits, m_hgrn_out_norm, m_w_out, m_mix_post_norm, m_mlp_pre_norm, m_w_ff1, m_w_ff2, m_mlp_post_norm, v_mix_pre_norm, v_w_in, v_attn_out_norm, v_hgrn_lb_logits, v_hgrn_out_norm, v_w_out, v_mix_post_norm, v_mlp_pre_norm, v_w_ff1, v_w_ff2, v_mlp_post_norm):
    w = dict(w_in=w_in[0], w_out=w_out[0], w_ff1=w_ff1[0], w_ff2=w_ff2[0])
    m = dict(w_in=m_w_in[0], w_out=m_w_out[0], w_ff1=m_w_ff1[0], w_ff2=m_w_ff2[0])
    v = dict(w_in=v_w_in[0], w_out=v_w_out[0], w_ff1=v_w_ff1[0], w_ff2=v_w_ff2[0])

    dx, big, small = train_step(x[0], loss_target[0], mix_pre_norm, attn_out_norm, hgrn_lb_logits, hgrn_out_norm,
                                mix_post_norm, mlp_pre_norm, mlp_post_norm, w, m, v)

    packed_g = _pack_small(small["dg1"], small["dgp"], small["dg_pre"], small["dg_post"], small["dan"], small["dhn"],
                           small["dlb"], small["loss_vec"])
    reduced = small_all_reduce(packed_g)
    pack = lambda a, b, c2, d, e, f, g: _pack_small(a, b, c2, d, e, f, g)
    w_s = pack(mix_pre_norm, mix_post_norm, mlp_pre_norm, mlp_post_norm, attn_out_norm, hgrn_out_norm, hgrn_lb_logits)
    m_s = pack(m_mix_pre_norm, m_mix_post_norm, m_mlp_pre_norm, m_mlp_post_norm, m_attn_out_norm, m_hgrn_out_norm,
               m_hgrn_lb_logits)
    v_s = pack(v_mix_pre_norm, v_mix_post_norm, v_mlp_pre_norm, v_mlp_post_norm, v_attn_out_norm, v_hgrn_out_norm,
               v_hgrn_lb_logits)
    g_s, d_s, nm_s, nv_s, loss = small_adamw(reduced, w_s, m_s, v_s)
    small_out = [_unpack_small(t) for t in (g_s, d_s, nm_s, nv_s)]

    outs = [loss[0, 0], dx[None]]
    for kind in range(4):
        for name in ORDER:
            outs.append(big[name][kind][None] if name in BIG else small_out[kind][name])
    return tuple(outs)
```

```python
import functools
import math

import jax
import jax.numpy as jnp
from jax import lax
from jax.experimental import pallas as pl
from jax.experimental.pallas import tpu as pltpu

F32 = jnp.float32
BF16 = jnp.bfloat16

D_MODEL = 1024
SEQ = 4096
ATTN_WIDTH = 512
ATTN_HEAD_DIM = 64
ATTN_HEADS = 8
ATTN_BLOCK = 128
DILATIONS = (1, 4, 16)
HGRN_WIDTH = 512
HGRN_HEADS = 4
HGRN_HEAD_DIM = 128
HGRN_CHUNK = 64
IN_PROJ_WIDTH = 3584
D_FF = 4096
RMS_EPS = 1e-6
N_DEV = 8
ADAM_LR = 0.001
ADAM_B1 = 0.9
ADAM_B2 = 0.999
ADAM_EPS = 1e-08
ADAM_WD = 0.01
ADAM_STEP = 10

SUBLANES = 8
LANES = 128
COLUMN_UNROLL = 4
TOKEN_TILE = 256
VMEM_LIMIT = 56 * 1024 * 1024
NEG_BIG = -1e30
MESH = pl.DeviceIdType.MESH


def _params(**kw):
    return pltpu.CompilerParams(vmem_limit_bytes=VMEM_LIMIT, **kw)


def _vmem_spec():
    return pl.BlockSpec(memory_space=pltpu.VMEM)


def _dot(a, b):
    return jnp.dot(a, b, preferred_element_type=F32)


def _dot_nt(a, b):
    return lax.dot_general(a, b, (((1,), (1,)), ((), ())), preferred_element_type=F32)


def _dot_tn(a, b):
    return lax.dot_general(a, b, (((0,), (0,)), ((), ())), preferred_element_type=F32)


def _sigmoid(x):
    return 1.0 / (1.0 + jnp.exp(-x))


def _rms_fwd(x, gain, width):
    r = lax.rsqrt(jnp.sum(x * x, axis=-1, keepdims=True) * (1.0 / width) + RMS_EPS)
    return x * r * gain


def _rms_bwd(dy, x, gain, width):
    r = lax.rsqrt(jnp.sum(x * x, axis=-1, keepdims=True) * (1.0 / width) + RMS_EPS)
    xhat = x * r
    dxhat = dy * gain
    dx = r * (dxhat - xhat * (jnp.sum(dxhat * xhat, axis=-1, keepdims=True) * (1.0 / width)))
    return dx, dy * xhat


def _split3(x):
    hi = x.astype(BF16)
    r1 = x - hi.astype(F32)
    mid = r1.astype(BF16)
    lo = (r1 - mid.astype(F32)).astype(BF16)
    return hi, mid, lo


def _tri_sum(tri_bf16, x):
    hi, mid, lo = _split3(x)
    return _dot(tri_bf16, hi) + _dot(tri_bf16, mid) + _dot(tri_bf16, lo)


def _dilated_spec(d, tm, width):
    return pl.BlockSpec((d, tm // d, width), lambda i: (0, i, 0))


def _lane_blocks(ref, value):
    for c in range(ref.shape[0]):
        ref[c] = value[:, c * LANES:(c + 1) * LANES]


def _to_dilated(src_ref, dst_ref, d, tm, cast=None):
    for r in range(d):
        for c in range(src_ref.shape[0]):
            v = src_ref[c] if d == 1 else src_ref[c, pl.ds(r, tm // d, stride=d), :]
            dst_ref[r, :, c * LANES:(c + 1) * LANES] = v if cast is None else v.astype(cast)


def _from_dilated(src_ref, scratch_ref, d, tm):
    if d == 1:
        return src_ref[0]
    nblk = scratch_ref.shape[0]
    for r in range(d):
        for c in range(nblk):
            scratch_ref[c, pl.ds(r, tm // d, stride=d), :] = src_ref[r, :, c * LANES:(c + 1) * LANES]
    return jnp.concatenate([scratch_ref[c] for c in range(nblk)], axis=1)


def in_proj_fwd(x, g1, w_in_b):
    s = x.shape[0]
    tm = TOKEN_TILE
    qkv_w = 3 * ATTN_WIDTH
    hg_w = IN_PROJ_WIDTH - qkv_w

    def body(x_ref, g_ref, w_ref, hg_ref, h_ref, *rest):
        qkv_refs, qkv_scr = rest[:len(DILATIONS)], rest[len(DILATIONS)]
        h = _rms_fwd(x_ref[...], g_ref[...], D_MODEL).astype(BF16)
        h_ref[...] = h
        proj = _dot(h, w_ref[...])
        hg_ref[...] = proj[:, qkv_w:]
        _lane_blocks(qkv_scr, proj[:, :qkv_w])
        for d, ref in zip(DILATIONS, qkv_refs):
            _to_dilated(qkv_scr, ref, d, tm, cast=BF16)

    return pl.pallas_call(
        body,
        name="in_proj_fwd",
        grid=(s // tm,),
        in_specs=[
            pl.BlockSpec((tm, D_MODEL), lambda i: (i, 0)),
            pl.BlockSpec((1, D_MODEL), lambda i: (0, 0)),
            _vmem_spec(),
        ],
        out_specs=[
            pl.BlockSpec((tm, hg_w), lambda i: (i, 0)),
            pl.BlockSpec((tm, D_MODEL), lambda i: (i, 0)),
        ] + [_dilated_spec(d, tm, qkv_w) for d in DILATIONS],
        out_shape=[jax.ShapeDtypeStruct((s, hg_w), F32), jax.ShapeDtypeStruct((s, D_MODEL), BF16)] + [
            jax.ShapeDtypeStruct((d, s // d, qkv_w), BF16) for d in DILATIONS],
        scratch_shapes=[pltpu.VMEM((qkv_w // LANES, tm, LANES), F32)],
        compiler_params=_params(dimension_semantics=("arbitrary",)),
    )(x, g1, w_in_b)


def _attn_scores(qm, kcat, head, dilation, first_block):
    s = _dot_nt(qm, kcat) * (ATTN_HEAD_DIM ** -0.5)
    qi = lax.broadcasted_iota(jnp.int32, (ATTN_BLOCK, 2 * ATTN_BLOCK), 0)
    kj = lax.broadcasted_iota(jnp.int32, (ATTN_BLOCK, 2 * ATTN_BLOCK), 1)
    dist = qi + ATTN_BLOCK - kj
    valid = (dist >= 0) & (dist <= ATTN_BLOCK) & ((kj >= ATTN_BLOCK) | jnp.logical_not(first_block))
    slope = 2.0 ** (-8.0 * (head + 1) / ATTN_HEADS)
    bias = dist.astype(F32) * (-slope * dilation)
    return jnp.where(valid, s + bias, NEG_BIG)


def _lane_half(shape, sub):
    lane = lax.broadcasted_iota(jnp.int32, shape, 1)
    return (lane < ATTN_HEAD_DIM) if sub == 0 else (lane >= ATTN_HEAD_DIM)


def _sub_block(col, row):
    return pl.BlockSpec((None, ATTN_BLOCK, ATTN_WIDTH), lambda r, n: (r, row(n), col))


def attn_fwd(qkv, dilation):
    d, length, _ = qkv.shape
    assert d == dilation
    nb = length // ATTN_BLOCK

    def body(q_ref, kc_ref, kp_ref, vc_ref, vp_ref, o_ref, lse_ref):
        first = pl.program_id(1) == 0
        for pair in range(ATTN_HEADS // 2):
            lanes = slice(pair * 128, (pair + 1) * 128)
            q2 = q_ref[:, lanes]
            kcat = jnp.concatenate([kp_ref[:, lanes], kc_ref[:, lanes]], axis=0)
            vcat = jnp.concatenate([vp_ref[:, lanes], vc_ref[:, lanes]], axis=0)
            o_pair = jnp.zeros((ATTN_BLOCK, 128), F32)
            lse_pair = jnp.zeros((ATTN_BLOCK, 128), F32)
            for sub in range(2):
                keep = _lane_half((ATTN_BLOCK, 128), sub)
                qm = jnp.where(keep, q2, jnp.zeros_like(q2))
                sc = _attn_scores(qm, kcat, 2 * pair + sub, d, first)
                m = jnp.max(sc, axis=-1, keepdims=True)
                p = jnp.exp(sc - m)
                den = jnp.sum(p, axis=-1, keepdims=True)
                o = _dot(p.astype(BF16), vcat) / den
                o_pair = jnp.where(keep, o, o_pair)
                lse_pair = jnp.where(keep, m + jnp.log(den), lse_pair)
            o_ref[:, lanes] = o_pair
            lse_ref[:, lanes] = lse_pair

    cur = lambda n: n
    prev = lambda n: jnp.maximum(n - 1, 0)
    return pl.pallas_call(
        body,
        name=f"attn_fwd_d{d}",
        grid=(d, nb),
        in_specs=[_sub_block(0, cur), _sub_block(1, cur), _sub_block(1, prev), _sub_block(2, cur), _sub_block(2, prev)],
        out_specs=[_sub_block(0, cur), _sub_block(0, cur)],
        out_shape=[jax.ShapeDtypeStruct((d, length, ATTN_WIDTH), F32)] * 2,
        compiler_params=_params(dimension_semantics=("arbitrary", "arbitrary")),
    )(qkv, qkv, qkv, qkv, qkv)


def attn_bwd(qkv, d_out, lse, delta, dilation, ride=None):
    d, length, _ = qkv.shape
    assert d == dilation
    nb = length // ATTN_BLOCK

    steps = d * nb + 1

    def body(q_ref, kc_ref, kp_ref, vc_ref, vp_ref, do_ref, lse_ref, dl_ref, dq_ref, dk_ref, dv_ref, ck_ref, cv_ref):
        t = pl.program_id(0)

        @pl.when(t == 0)
        def _():
            ck_ref[...] = jnp.zeros_like(ck_ref)
            cv_ref[...] = jnp.zeros_like(cv_ref)

        @pl.when(t < steps - 1)
        def _():
            first = t % nb == 0
            for pair in range(ATTN_HEADS // 2):
                lanes = slice(pair * 128, (pair + 1) * 128)
                q2 = q_ref[:, lanes]
                do2 = do_ref[:, lanes]
                kcat = jnp.concatenate([kp_ref[:, lanes], kc_ref[:, lanes]], axis=0)
                vcat = jnp.concatenate([vp_ref[:, lanes], vc_ref[:, lanes]], axis=0)
                dq_pair = jnp.zeros((ATTN_BLOCK, 128), F32)
                dk_cat = jnp.zeros((2 * ATTN_BLOCK, 128), F32)
                dv_cat = jnp.zeros((2 * ATTN_BLOCK, 128), F32)
                for sub in range(2):
                    keep = _lane_half((ATTN_BLOCK, 128), sub)
                    col = pair * 128 + sub * ATTN_HEAD_DIM
                    qm = jnp.where(keep, q2, jnp.zeros_like(q2))
                    dom = jnp.where(keep, do2, 0.0).astype(BF16)
                    sc = _attn_scores(qm, kcat, 2 * pair + sub, d, first)
                    p = jnp.exp(sc - lse_ref[:, col:col + 1])
                    dp = _dot_nt(dom, vcat)
                    ds = (p * (dp - dl_ref[:, col:col + 1]) * (ATTN_HEAD_DIM ** -0.5)).astype(BF16)
                    dq_pair = jnp.where(keep, _dot(ds, kcat), dq_pair)
                    dk_cat = dk_cat + _dot_tn(ds, qm)
                    dv_cat = dv_cat + _dot_tn(p.astype(BF16), dom)
                dq_ref[:, lanes] = dq_pair
                dk_ref[:, lanes] = ck_ref[:, lanes] + dk_cat[:ATTN_BLOCK]
                dv_ref[:, lanes] = cv_ref[:, lanes] + dv_cat[:ATTN_BLOCK]
                ck_ref[:, lanes] = dk_cat[ATTN_BLOCK:]
                cv_ref[:, lanes] = dv_cat[ATTN_BLOCK:]

        @pl.when(t == steps - 1)
        def _():
            dk_ref[...] = ck_ref[...]
            dv_ref[...] = cv_ref[...]

    blk = (ATTN_BLOCK, ATTN_WIDTH)

    def spec(col, shift):
        def index(t):
            f = jnp.minimum(t, steps - 2) if shift > -2 else jnp.maximum(t - 1, 0)
            r, n = f // nb, f % nb
            return (r, jnp.maximum(n - 1, 0) if shift == -1 else n, col)
        return pl.BlockSpec((None, ATTN_BLOCK, ATTN_WIDTH), index)

    step = lambda k: (lambda: pl.program_id(0) == k)
    e_in, e_out, e_shape, e_scr, e_args = _ride_specs(ride)
    return pl.pallas_call(
        _riding(body, 8, 3, 2, ride, step(0), step(steps // 2), step(steps - 1)),
        name=f"attn_bwd_d{d}",
        grid=(steps,),
        in_specs=[spec(0, 0), spec(1, 0), spec(1, -1), spec(2, 0), spec(2, -1), spec(0, 0), spec(0, 0), spec(0, 0)] + e_in,
        out_specs=[spec(0, 0), spec(0, -2), spec(0, -2)] + e_out,
        out_shape=[jax.ShapeDtypeStruct((d, length, ATTN_WIDTH), F32)] * 3 + e_shape,
        scratch_shapes=[pltpu.VMEM(blk, F32), pltpu.VMEM(blk, F32)] + e_scr,
        compiler_params=_params(dimension_semantics=("arbitrary",)),
    )(qkv, qkv, qkv, qkv, qkv, d_out, lse, delta, *e_args)


def _lower_bound(logits):
    return _sigmoid(logits[0:1, :] - logits[1:2, :])


def _hgrn_gates(q, fp, lb):
    sq = _sigmoid(q)
    qf = q * sq
    sig = _sigmoid(fp)
    f = lb + (1.0 - lb) * sig
    kf = (1.0 - lb) * _sigmoid(-fp)
    return sq, qf, sig, f, kf


def _tril_bf16(n, upper=False):
    r = lax.broadcasted_iota(jnp.int32, (n, n), 0)
    c = lax.broadcasted_iota(jnp.int32, (n, n), 1)
    keep = (c >= r) if upper else (c <= r)
    return jnp.where(keep, 1.0, 0.0).astype(BF16)


def hgrn_fwd(proj, lb, ride=None):
    s = proj.shape[0]
    c_len, nh, hd = HGRN_CHUNK, HGRN_HEADS, HGRN_HEAD_DIM
    n_chunks = s // c_len
    col0 = 0

    def body(q_ref, f_ref, i_ref, lb_ref, o_ref, st_out_ref, a_out_ref, st_ref, b_ref, qf_ref, kf_ref, a_ref):
        @pl.when(pl.program_id(0) == 0)
        def _():
            st_ref[...] = jnp.zeros_like(st_ref)

        lbv = _lower_bound(lb_ref[...])
        _, qf, _, f, kf = _hgrn_gates(q_ref[...], f_ref[...], lbv)
        b = _tri_sum(_tril_bf16(c_len), jnp.log(f))
        b_ref[...] = b
        qf_ref[...] = qf
        kf_ref[...] = kf
        a_ref[...] = jnp.zeros_like(a_ref)
        t_idx = lax.broadcasted_iota(jnp.int32, (c_len, nh * hd), 0)
        lane = lax.broadcasted_iota(jnp.int32, (c_len, hd), 1)

        for r0 in range(0, c_len, SUBLANES):
            rows = slice(r0, c_len)

            def column(jj, carry, r0=r0, rows=rows):
                j = r0 + jj
                bj = b_ref[pl.ds(j, 1), :]
                kj = kf_ref[pl.ds(j, 1), :]
                e = jnp.exp(jnp.where(t_idx[rows] >= j, b_ref[rows, :] - bj, NEG_BIG))
                prod = qf_ref[rows, :] * kj * e
                for h in range(nh):
                    col = jnp.sum(prod[:, h * hd:(h + 1) * hd], axis=-1, keepdims=True)
                    a_ref[h, rows, :] = jnp.where(lane[rows] == j, col, a_ref[h, rows, :])
                return carry

            lax.fori_loop(0, SUBLANES, column, 0, unroll=COLUMN_UNROLL)

        b_last = b[c_len - 1:c_len, :]
        qb = (qf * jnp.exp(b)).astype(BF16)
        kb2 = (kf * jnp.exp(b_last - b)).astype(BF16)
        vf = i_ref[...].astype(BF16)
        for h in range(nh):
            hs = slice(h * hd, (h + 1) * hd)
            st = st_ref[h]
            st_out_ref[0, h] = st
            a_h = a_ref[h]
            a_out_ref[:, hs] = a_h
            o_ref[:, hs] = _dot_nt(qb[:, hs], st.astype(BF16)) + _dot(a_h[:, :c_len].astype(BF16), vf[:, hs])
            st_ref[h] = st * jnp.exp(b_last[:, hs]) + _dot_tn(vf[:, hs], kb2[:, hs])

    blk = (c_len, HGRN_WIDTH)
    step = lambda k: (lambda: pl.program_id(0) == k)
    e_in, e_out, e_shape, e_scr, e_args = _ride_specs(ride)
    return pl.pallas_call(
        _riding(body, 4, 3, 5, ride, step(0), step(n_chunks // 2), step(n_chunks - 1)),
        name="hgrn_fwd",
        grid=(n_chunks,),
        in_specs=[
            pl.BlockSpec(blk, lambda c: (c, col0)),
            pl.BlockSpec(blk, lambda c: (c, col0 + 1)),
            pl.BlockSpec(blk, lambda c: (c, col0 + 2)),
            pl.BlockSpec((2, HGRN_WIDTH), lambda c: (0, 0)),
        ] + e_in,
        out_specs=[
            pl.BlockSpec(blk, lambda c: (c, 0)),
            pl.BlockSpec((1, nh, hd, hd), lambda c: (c, 0, 0, 0)),
            pl.BlockSpec(blk, lambda c: (c, 0)),
        ] + e_out,
        out_shape=[
            jax.ShapeDtypeStruct((s, HGRN_WIDTH), F32),
            jax.ShapeDtypeStruct((n_chunks, nh, hd, hd), F32),
            jax.ShapeDtypeStruct((s, nh * hd), F32),
        ] + e_shape,
        scratch_shapes=[
            pltpu.VMEM((nh, hd, hd), F32),
            pltpu.VMEM(blk, F32),
            pltpu.VMEM(blk, F32),
            pltpu.VMEM(blk, F32),
            pltpu.VMEM((nh, c_len, hd), F32),
        ] + e_scr,
        compiler_params=_params(dimension_semantics=("arbitrary",)),
    )(proj, proj, proj, lb, *e_args)


def hgrn_bwd(proj, lb, d_o, states, a_mat, ride=None):
    s = proj.shape[0]
    c_len, nh, hd = HGRN_CHUNK, HGRN_HEADS, HGRN_HEAD_DIM
    n_chunks = s // c_len
    col0 = 0
    last = n_chunks - 1

    def body(q_ref, f_ref, i_ref, lb_ref, do_ref, st_in_ref, a_in_ref, dq_ref, df_ref, di_ref, dlb_ref,
             dst_ref, b_ref, qf_ref, kf_ref, da_ref, dqi_ref, dki_ref):
        @pl.when(pl.program_id(0) == 0)
        def _():
            dst_ref[...] = jnp.zeros_like(dst_ref)
            dlb_ref[...] = jnp.zeros_like(dlb_ref)

        lbv = _lower_bound(lb_ref[...])
        q = q_ref[...]
        sq, qf, sig, f, kf = _hgrn_gates(q, f_ref[...], lbv)
        b = _tri_sum(_tril_bf16(c_len), jnp.log(f))
        b_ref[...] = b
        qf_ref[...] = qf
        kf_ref[...] = kf
        b_last = b[c_len - 1:c_len, :]
        eb = jnp.exp(b)
        ebl = jnp.exp(b_last - b)
        qb = qf * eb
        kb2 = kf * ebl
        vf = i_ref[...]
        d_o = do_ref[...]
        qb_b, kb2_b, vf_b, do_b = qb.astype(BF16), kb2.astype(BF16), vf.astype(BF16), d_o.astype(BF16)
        tq = lax.broadcasted_iota(jnp.int32, (c_len, hd), 0)
        lane = lax.broadcasted_iota(jnp.int32, (c_len, hd), 1)

        dqb_parts, dvf_parts, dkb2_parts, dbl_parts = [], [], [], []
        for h in range(nh):
            hs = slice(h * hd, (h + 1) * hd)
            st = st_in_ref[0, h]
            dst = dst_ref[h]
            st_b, dst_b = st.astype(BF16), dst.astype(BF16)
            a_h = a_in_ref[:, hs][:, :c_len].astype(BF16)
            dqb_parts.append(_dot(do_b[:, hs], st_b))
            dvf_parts.append(_dot_tn(a_h, do_b[:, hs]) + _dot_nt(kb2_b[:, hs], dst_b))
            dkb2_parts.append(_dot(vf_b[:, hs], dst_b))
            da = _dot_nt(do_b[:, hs], vf_b[:, hs])
            da = jnp.concatenate([da, jnp.zeros((c_len, hd - c_len), F32)], axis=1)
            da_ref[h] = jnp.where(tq >= lane, da, 0.0)
            dbl_parts.append(jnp.sum(dst * st, axis=0, keepdims=True) * jnp.exp(b_last[:, hs]))
            dst_ref[h] = dst * jnp.exp(b_last[:, hs]) + _dot_tn(do_b[:, hs], qb_b[:, hs])
        dqb = jnp.concatenate(dqb_parts, axis=1)
        dvf = jnp.concatenate(dvf_parts, axis=1)
        dkb2 = jnp.concatenate(dkb2_parts, axis=1)
        dbl = jnp.concatenate(dbl_parts, axis=1) + jnp.sum(dkb2 * kb2, axis=0, keepdims=True)

        dqi_ref[...] = jnp.zeros_like(dqi_ref)
        t_idx = lax.broadcasted_iota(jnp.int32, (c_len, nh * hd), 0)

        for r0 in range(0, c_len, SUBLANES):
            rows = slice(r0, c_len)
            nrow = c_len - r0

            def column(jj, carry, r0=r0, rows=rows, nrow=nrow):
                j = r0 + jj
                bj = b_ref[pl.ds(j, 1), :]
                kj = kf_ref[pl.ds(j, 1), :]
                e = jnp.exp(jnp.where(t_idx[rows] >= j, b_ref[rows, :] - bj, NEG_BIG))
                cols = [jnp.sum(jnp.where(lane[rows] == j, da_ref[h, rows, :], 0.0), axis=-1, keepdims=True)
                        for h in range(nh)]
                w = e * jnp.concatenate([jnp.broadcast_to(cc, (nrow, hd)) for cc in cols], axis=1)
                dqi_ref[rows, :] += w * kj
                dki_ref[pl.ds(j, 1), :] = jnp.sum(w * qf_ref[rows, :], axis=0, keepdims=True)
                return carry

            lax.fori_loop(0, SUBLANES, column, 0, unroll=COLUMN_UNROLL)
        dq_intra = dqi_ref[...]
        dk_intra = dki_ref[...]

        db = dqb * qb + qf * dq_intra - kf * dk_intra - dkb2 * kb2
        db = db + jnp.where(t_idx == c_len - 1, dbl, 0.0)
        dg = _tri_sum(_tril_bf16(c_len, upper=True), db)
        dqf = dqb * eb + dq_intra
        dkf = dkb2 * ebl + dk_intra
        dq_ref[...] = dqf * (sq * (1.0 + q * (1.0 - sq)))
        dfv = dg / f - dkf
        df_ref[...] = dfv * (1.0 - lbv) * sig * (1.0 - sig)
        di_ref[...] = dvf
        dlb_ref[...] += jnp.sum(dfv * (1.0 - sig), axis=0, keepdims=True)

    blk = (c_len, HGRN_WIDTH)
    rev = lambda c: last - c
    step = lambda k: (lambda: pl.program_id(0) == k)
    e_in, e_out, e_shape, e_scr, e_args = _ride_specs(ride)
    return pl.pallas_call(
        _riding(body, 7, 4, 7, ride, step(0), step(n_chunks // 2), step(last)),
        name="hgrn_bwd",
        grid=(n_chunks,),
        in_specs=[
            pl.BlockSpec(blk, lambda c: (rev(c), col0)),
            pl.BlockSpec(blk, lambda c: (rev(c), col0 + 1)),
            pl.BlockSpec(blk, lambda c: (rev(c), col0 + 2)),
            pl.BlockSpec((2, HGRN_WIDTH), lambda c: (0, 0)),
            pl.BlockSpec(blk, lambda c: (rev(c), 0)),
            pl.BlockSpec((1, nh, hd, hd), lambda c: (rev(c), 0, 0, 0)),
            pl.BlockSpec(blk, lambda c: (rev(c), 0)),
        ] + e_in,
        out_specs=[
            pl.BlockSpec(blk, lambda c: (rev(c), 0)),
            pl.BlockSpec(blk, lambda c: (rev(c), 0)),
            pl.BlockSpec(blk, lambda c: (rev(c), 0)),
            pl.BlockSpec((1, HGRN_WIDTH), lambda c: (0, 0)),
        ] + e_out,
        out_shape=[jax.ShapeDtypeStruct((s, HGRN_WIDTH), F32)] * 3 + [jax.ShapeDtypeStruct((1, HGRN_WIDTH), F32)] + e_shape,
        scratch_shapes=[
            pltpu.VMEM((nh, hd, hd), F32),
            pltpu.VMEM(blk, F32),
            pltpu.VMEM(blk, F32),
            pltpu.VMEM(blk, F32),
            pltpu.VMEM((nh, c_len, hd), F32),
            pltpu.VMEM(blk, F32),
            pltpu.VMEM(blk, F32),
        ] + e_scr,
        compiler_params=_params(dimension_semantics=("arbitrary",)),
    )(proj, proj, proj, lb, d_o, states, a_mat, *e_args)


def _row_spec(tm, width, col=0):
    return pl.BlockSpec((tm, width), lambda i: (i, col))


def _const_spec(width):
    return pl.BlockSpec((1, width), lambda i: (0, 0))


def _acc_rows(ref, value):
    @pl.when(pl.program_id(0) == 0)
    def _():
        ref[...] = jnp.zeros_like(ref)

    ref[...] += jnp.sum(value, axis=0, keepdims=True)


def mix_fwd(attn_parts, o_h, proj, an, hn, w_out_b, gp, x):
    s = x.shape[0]
    tm = TOKEN_TILE
    gate_col = 3
    hd = HGRN_HEAD_DIM
    nd = len(DILATIONS)

    def body(*refs):
        o_refs, l_refs = refs[:nd], refs[nd:2 * nd]
        oh_ref, gate_ref, an_ref, hn_ref, w_ref, gp_ref, x_ref = refs[2 * nd:2 * nd + 7]
        x1_ref, cat_ref, mixed_ref, attn_ref = refs[2 * nd + 7:2 * nd + 11]
        lse_refs = refs[2 * nd + 11:3 * nd + 11]
        o_scr, l_scr, lse_scr = refs[3 * nd + 11:]
        os_ = [_from_dilated(r, o_scr.at[k], d, tm) for k, (r, d) in enumerate(zip(o_refs, DILATIONS))]
        ls = [_from_dilated(r, l_scr.at[k], d, tm) for k, (r, d) in enumerate(zip(l_refs, DILATIONS))]
        m = jnp.maximum(jnp.maximum(ls[0], ls[1]), ls[2])
        es = [jnp.exp(l - m) for l in ls]
        den = es[0] + es[1] + es[2]
        attn = (es[0] * os_[0] + es[1] * os_[1] + es[2] * os_[2]) / den
        attn_ref[...] = attn
        _lane_blocks(lse_scr, m + jnp.log(den))
        for d, ref in zip(DILATIONS, lse_refs):
            _to_dilated(lse_scr, ref, d, tm)
        cat_ref[:, :ATTN_WIDTH] = _rms_fwd(attn, an_ref[...], ATTN_WIDTH).astype(BF16)
        gate = gate_ref[...]
        silu_g = gate * _sigmoid(gate)
        for h in range(HGRN_HEADS):
            hs = slice(h * hd, (h + 1) * hd)
            rec = _rms_fwd(oh_ref[:, hs], hn_ref[:, hs], hd) * silu_g[:, hs]
            cat_ref[:, ATTN_WIDTH + h * hd:ATTN_WIDTH + (h + 1) * hd] = rec.astype(BF16)
        mixed = _dot(cat_ref[...], w_ref[...])
        mixed_ref[...] = mixed
        x1_ref[...] = x_ref[...] + _rms_fwd(mixed, gp_ref[...], D_MODEL)

    aw = ATTN_WIDTH
    return pl.pallas_call(
        body,
        name="mix_fwd",
        grid=(s // tm,),
        in_specs=[_dilated_spec(d, tm, aw) for d in DILATIONS] * 2 + [
            _row_spec(tm, aw), _row_spec(tm, aw, gate_col), _const_spec(aw), _const_spec(aw), _vmem_spec(),
            _const_spec(D_MODEL), _row_spec(tm, D_MODEL)],
        out_specs=[_row_spec(tm, D_MODEL), _row_spec(tm, D_MODEL), _row_spec(tm, D_MODEL), _row_spec(tm, aw)] + [
            _dilated_spec(d, tm, aw) for d in DILATIONS],
        out_shape=[
            jax.ShapeDtypeStruct((s, D_MODEL), F32),
            jax.ShapeDtypeStruct((s, D_MODEL), BF16),
            jax.ShapeDtypeStruct((s, D_MODEL), F32),
            jax.ShapeDtypeStruct((s, aw), F32),
        ] + [jax.ShapeDtypeStruct((d, s // d, aw), F32) for d in DILATIONS],
        scratch_shapes=[pltpu.VMEM((nd, aw // LANES, tm, LANES), F32), pltpu.VMEM((nd, aw // LANES, tm, LANES), F32),
                        pltpu.VMEM((aw // LANES, tm, LANES), F32)],
        compiler_params=_params(dimension_semantics=("arbitrary",)),
    )(*[p[0] for p in attn_parts], *[p[1] for p in attn_parts], o_h, proj, an, hn, w_out_b, gp, x)


def mix_bwd(dx1, mixed, gp, w_out_b, attn, an, o_h, proj, hn):
    s = dx1.shape[0]
    tm = TOKEN_TILE
    gate_col = 3
    hd = HGRN_HEAD_DIM
    aw = ATTN_WIDTH

    nd = len(DILATIONS)

    def body(*refs):
        dx1_ref, mixed_ref, gp_ref, w_ref, attn_ref, an_ref, oh_ref, gate_ref, hn_ref, dmix_ref = refs[:10]
        do_refs, delta_refs = refs[10:10 + nd], refs[10 + nd:10 + 2 * nd]
        doh_ref, dgate_ref, dgp_ref, dan_ref, dhn_ref, do_ref, delta_ref = refs[10 + 2 * nd:]
        dmixed, gp_c = _rms_bwd(dx1_ref[...], mixed_ref[...], gp_ref[...], D_MODEL)
        _acc_rows(dgp_ref, gp_c)
        dmixed_b = dmixed.astype(BF16)
        dmix_ref[...] = dmixed_b
        dcat = _dot_nt(dmixed_b, w_ref[...])
        attn = attn_ref[...]
        d_o, an_c = _rms_bwd(dcat[:, :aw], attn, an_ref[...], aw)
        _acc_rows(dan_ref, an_c)
        _lane_blocks(do_ref, d_o)
        prod = d_o * attn
        for pair in range(ATTN_HEADS // 2):
            pp = prod[:, pair * LANES:(pair + 1) * LANES]
            low = _lane_half((tm, LANES), 0)
            lo = jnp.sum(jnp.where(low, pp, 0.0), axis=-1, keepdims=True)
            hi = jnp.sum(jnp.where(low, 0.0, pp), axis=-1, keepdims=True)
            delta_ref[pair] = jnp.where(low, lo, hi)
        for d, o_ref, l_ref in zip(DILATIONS, do_refs, delta_refs):
            _to_dilated(do_ref, o_ref, d, tm)
            _to_dilated(delta_ref, l_ref, d, tm)
        gate = gate_ref[...]
        sg = _sigmoid(gate)
        silu_g = gate * sg
        drec = dcat[:, aw:]
        hn_parts = []
        for h in range(HGRN_HEADS):
            hs = slice(h * hd, (h + 1) * hd)
            oh = oh_ref[:, hs]
            on = _rms_fwd(oh, hn_ref[:, hs], hd)
            dgate_ref[:, hs] = drec[:, hs] * on * (sg[:, hs] * (1.0 + gate[:, hs] * (1.0 - sg[:, hs])))
            d_oh, hn_c = _rms_bwd(drec[:, hs] * silu_g[:, hs], oh, hn_ref[:, hs], hd)
            doh_ref[:, hs] = d_oh
            hn_parts.append(hn_c)
        _acc_rows(dhn_ref, jnp.concatenate(hn_parts, axis=1))

    return pl.pallas_call(
        body,
        name="mix_bwd",
        grid=(s // tm,),
        in_specs=[_row_spec(tm, D_MODEL), _row_spec(tm, D_MODEL), _const_spec(D_MODEL), _vmem_spec(), _row_spec(tm, aw),
                  _const_spec(aw), _row_spec(tm, aw), _row_spec(tm, aw, gate_col), _const_spec(aw)],
        out_specs=[_row_spec(tm, D_MODEL)] + [_dilated_spec(d, tm, aw) for d in DILATIONS] * 2 + [_row_spec(tm, aw)] * 2 + [
            _const_spec(D_MODEL), _const_spec(aw), _const_spec(aw)],
        out_shape=[jax.ShapeDtypeStruct((s, D_MODEL), BF16)] + [
            jax.ShapeDtypeStruct((d, s // d, aw), F32) for d in DILATIONS] * 2 + [jax.ShapeDtypeStruct((s, aw), F32)] * 2 + [
            jax.ShapeDtypeStruct((1, D_MODEL), F32), jax.ShapeDtypeStruct((1, aw), F32),
            jax.ShapeDtypeStruct((1, aw), F32)],
        scratch_shapes=[pltpu.VMEM((aw // LANES, tm, LANES), F32), pltpu.VMEM((aw // LANES, tm, LANES), F32)],
        compiler_params=_params(dimension_semantics=("arbitrary",)),
    )(dx1, mixed, gp, w_out_b, attn, an, o_h, proj, hn)


def mlp_fwd_bwd(x1, g_pre, w1_blocks, w2_b, g_post, target):
    s = x1.shape[0]
    tm = TOKEN_TILE
    nblk, _, fb = w1_blocks.shape

    def body(x1_ref, gpre_ref, w1_ref, w2_ref, gpost_ref, t_ref,
             dx1_ref, h2_ref, a_ref, du_ref, dff_ref, loss_ref, dgpre_ref, dgpost_ref, u_ref):
        x1v = x1_ref[...]
        h2 = _rms_fwd(x1v, gpre_ref[...], D_MODEL).astype(BF16)
        h2_ref[...] = h2
        ff = jnp.zeros((tm, D_MODEL), F32)
        for j in range(nblk):
            cols = slice(j * fb, (j + 1) * fb)
            ru = jnp.maximum(_dot(h2, w1_ref[j]), 0.0)
            u_ref[:, cols] = ru
            a = (ru * ru).astype(BF16)
            a_ref[:, cols] = a
            ff = ff + _dot(a, w2_ref[cols, :])
        diff = x1v + _rms_fwd(ff, gpost_ref[...], D_MODEL) - t_ref[...]
        _acc_rows(loss_ref, diff * diff)
        dy = diff * (1.0 / D_MODEL)
        dff, gpost_c = _rms_bwd(dy, ff, gpost_ref[...], D_MODEL)
        _acc_rows(dgpost_ref, gpost_c)
        dff_b = dff.astype(BF16)
        dff_ref[...] = dff_b
        dh2 = jnp.zeros((tm, D_MODEL), F32)
        for j in range(nblk):
            cols = slice(j * fb, (j + 1) * fb)
            du = (_dot_nt(dff_b, w2_ref[cols, :]) * (2.0 * u_ref[:, cols])).astype(BF16)
            du_ref[:, cols] = du
            dh2 = dh2 + _dot_nt(du, w1_ref[j])
        dxa, gpre_c = _rms_bwd(dh2, x1v, gpre_ref[...], D_MODEL)
        _acc_rows(dgpre_ref, gpre_c)
        dx1_ref[...] = dy + dxa

    dm = D_MODEL
    return pl.pallas_call(
        body,
        name="mlp_fwd_bwd",
        grid=(s // tm,),
        in_specs=[_row_spec(tm, dm), _const_spec(dm), _vmem_spec(), _vmem_spec(), _const_spec(dm), _row_spec(tm, dm)],
        out_specs=[_row_spec(tm, dm), _row_spec(tm, dm), _row_spec(tm, D_FF), _row_spec(tm, D_FF), _row_spec(tm, dm),
                   _const_spec(dm), _const_spec(dm), _const_spec(dm)],
        out_shape=[
            jax.ShapeDtypeStruct((s, dm), F32),
            jax.ShapeDtypeStruct((s, dm), BF16),
            jax.ShapeDtypeStruct((s, D_FF), BF16),
            jax.ShapeDtypeStruct((s, D_FF), BF16),
            jax.ShapeDtypeStruct((s, dm), BF16),
            jax.ShapeDtypeStruct((1, dm), F32),
            jax.ShapeDtypeStruct((1, dm), F32),
            jax.ShapeDtypeStruct((1, dm), F32),
        ],
        scratch_shapes=[pltpu.VMEM((tm, D_FF), F32)],
        compiler_params=_params(dimension_semantics=("arbitrary",)),
    )(x1, g_pre, w1_blocks, w2_b, g_post, target)


def in_proj_bwd(attn_grads, hgrn_grads, dgate, w_in_b, x, g1, dx1):
    s = x.shape[0]
    tm = TOKEN_TILE
    aw = ATTN_WIDTH
    n_attn = len(attn_grads)
    flat = [g[k] for k in range(3) for g in attn_grads] + list(hgrn_grads) + [dgate]

    def body(*refs):
        parts = refs[:len(flat)]
        w_ref, x_ref, g_ref, dx1_ref, dx_ref, dproj_ref, dg_ref, scr = refs[len(flat):]
        groups = []
        for k in range(3):
            acc = None
            for p, d in zip(parts[k * n_attn:(k + 1) * n_attn], DILATIONS):
                v = _from_dilated(p, scr, d, tm)
                acc = v if acc is None else acc + v
            groups.append(acc)
        groups += [p[...] for p in parts[3 * n_attn:]]
        dh = jnp.zeros((tm, D_MODEL), F32)
        for gi, grp in enumerate(groups):
            cols = slice(gi * aw, (gi + 1) * aw)
            gb = grp.astype(BF16)
            dproj_ref[:, cols] = gb
            dh = dh + _dot_nt(gb, w_ref[:, cols])
        dxa, g_c = _rms_bwd(dh, x_ref[...], g_ref[...], D_MODEL)
        _acc_rows(dg_ref, g_c)
        dx_ref[...] = dx1_ref[...] + dxa

    dm = D_MODEL
    return pl.pallas_call(
        body,
        name="in_proj_bwd",
        grid=(s // tm,),
        in_specs=[_dilated_spec(d, tm, aw) for d in DILATIONS] * 3 + [_row_spec(tm, aw)] * 4 + [
            _vmem_spec(), _row_spec(tm, dm), _const_spec(dm), _row_spec(tm, dm)],
        out_specs=[_row_spec(tm, dm), _row_spec(tm, IN_PROJ_WIDTH), _const_spec(dm)],
        out_shape=[jax.ShapeDtypeStruct((s, dm), F32), jax.ShapeDtypeStruct((s, IN_PROJ_WIDTH), BF16),
                   jax.ShapeDtypeStruct((1, dm), F32)],
        scratch_shapes=[pltpu.VMEM((aw // LANES, tm, LANES), F32)],
        compiler_params=_params(dimension_semantics=("arbitrary",)),
    )(*flat, w_in_b, x, g1, dx1)


def wgrad(a_b, b_b, tn, name, ts=1024, per_step=1):
    s, k = a_b.shape
    n = b_b.shape[1]

    def body(a_ref, b_ref, o_ref):
        @pl.when(pl.program_id(1) == 0)
        def _():
            o_ref[...] = jnp.zeros_like(o_ref)

        a = a_ref[...]
        for jj in range(per_step):
            o_ref[jj] += _dot_tn(a, b_ref[:, jj * tn:(jj + 1) * tn])

    wide = tn * per_step
    return pl.pallas_call(
        body,
        name=name,
        grid=(n // wide, s // ts),
        in_specs=[pl.BlockSpec((ts, k), lambda j, i: (i, 0)), pl.BlockSpec((ts, wide), lambda j, i: (i, j))],
        out_specs=pl.BlockSpec((per_step, k, tn), lambda j, i: (j, 0, 0)),
        out_shape=jax.ShapeDtypeStruct((n // tn, k, tn), F32),
        compiler_params=_params(dimension_semantics=("arbitrary", "arbitrary")),
    )(a_b, b_b)


def train_step(x, target, g1, an, logits, hn, gp, g_pre, g_post, w, m, v):
    nd = len(DILATIONS)
    shard_b = {k: w[k].astype(BF16) for k in BIG}
    (w_in_g,) = run_exchange(gather_exchange([shard_b["w_in"]]), "gather_w_in")
    w_in_b = w_in_g.transpose(1, 0, 2).reshape(D_MODEL, IN_PROJ_WIDTH)

    proj, h_b, *qkvs = in_proj_fwd(x, g1, w_in_b)
    attn_parts = [attn_fwd(qkv, d) for qkv, d in zip(qkvs, DILATIONS)]
    o_h, states, a_mat, w_out_g, w1_blocks, w2_g = hgrn_fwd(
        proj, logits, ride=gather_exchange([shard_b["w_out"], shard_b["w_ff1"], shard_b["w_ff2"]]))
    w_out_b = w_out_g.reshape(D_MODEL, D_MODEL)
    w2_b = w2_g.reshape(D_FF, D_MODEL)
    x1, cat_b, mixed, attn, *lses = mix_fwd(attn_parts, o_h, proj, an, hn, w_out_b, gp, x)
    dx1, h2_b, a_b, du_b, dff_b, loss_vec, dg_pre, dg_post = mlp_fwd_bwd(x1, g_pre, w1_blocks, w2_b, g_post, target)
    dw2 = wgrad(a_b, dff_b, D_MODEL, "wgrad_ff2", ts=512)
    dw1 = wgrad(h2_b, du_b, D_FF // N_DEV, "wgrad_ff1", per_step=2)
    dmix_b, *rest = mix_bwd(dx1, mixed, gp, w_out_b, attn, an, o_h, proj, hn)
    d_os, deltas = rest[:nd], rest[nd:2 * nd]
    d_oh, dgate, dgp, dan, dhn = rest[2 * nd:]
    dwout = wgrad(cat_b, dmix_b, D_MODEL, "wgrad_out")

    early = ("w_out", "w_ff1", "w_ff2")
    early_grads = [dwout.reshape(N_DEV, D_MODEL // N_DEV, D_MODEL), dw1, dw2.reshape(N_DEV, D_FF // N_DEV, D_MODEL)]
    attn_grads = []
    for k, d in enumerate(DILATIONS):
        ride = to_core_exchange(early_grads) if k == 0 else None
        res = attn_bwd(qkvs[k], d_os[k], lses[k], deltas[k], d, ride=ride)
        attn_grads.append(res[:3])
        if k == 0:
            pairs = [pair_sum(g, s, f"pair_sum_{name}") for g, s, name in zip(early_grads, res[3:], early)]
    dq_h, df_h, di_h, dlb, *others = hgrn_bwd(proj, logits, d_oh, states, a_mat, ride=to_chip_exchange(pairs))
    dx, dproj_b, dg1 = in_proj_bwd(attn_grads, (dq_h, df_h, di_h), dgate, w_in_b, x, g1, dx1)
    dwin = wgrad(h_b, dproj_b, 2 * IN_PROJ_WIDTH // N_DEV, "wgrad_in")
    big = {name: sum_adamw(p, o, w[name], m[name], v[name], f"sum_adamw_{name}")
           for name, p, o in zip(early, pairs, others)}

    shard_w = IN_PROJ_WIDTH // N_DEV
    dwin_blocks = dwin.reshape(N_DEV // 2, D_MODEL, 2, shard_w).transpose(0, 2, 1, 3).reshape(N_DEV, D_MODEL, shard_w)
    (from_sibling,) = run_exchange(to_core_exchange([dwin_blocks]), "reduce_w_in_to_core")
    pair_in = pair_sum(dwin_blocks, from_sibling, "pair_sum_w_in")
    (others_in,) = run_exchange(to_chip_exchange([pair_in]), "reduce_w_in_to_chip")
    big["w_in"] = sum_adamw(pair_in, others_in, w["w_in"], m["w_in"], v["w_in"], "sum_adamw_w_in")
    small = dict(dg1=dg1, dan=dan, dlb=dlb, dhn=dhn, dgp=dgp, dg_pre=dg_pre, dg_post=dg_post, loss_vec=loss_vec)
    return dx, big, small


def _position():
    x, y, c = lax.axis_index("x"), lax.axis_index("y"), lax.axis_index("c")
    other_chips = [(1 - x, y), (x, 1 - y), (1 - x, 1 - y)]
    return x, y, c, other_chips


def _any_spec():
    return pl.BlockSpec(memory_space=pl.ANY)


class Exchange:
    def __init__(self, arrays, out_shape, sems, stages):
        self.arrays, self.out_shape, self.sems, self.stages = list(arrays), list(out_shape), list(sems), stages


def gather_exchange(shards):
    n = len(shards)

    def stages(ins, outs, sems):
        send_sems, recv_sems, local_sems = sems

        def parts():
            x, y, c, chips = _position()
            me, sibling = (x, y, c), (x, y, 1 - c)

            def slot(a, px, py, pc):
                return outs[a].at[4 * px + 2 * py + pc]

            def copy(a, k, block, to, src=None):
                return pltpu.make_async_remote_copy(
                    src_ref=slot(a, *block) if src is None else src, dst_ref=slot(a, *block),
                    send_sem=send_sems.at[a, k], recv_sem=recv_sems.at[a, k], device_id=to, device_id_type=MESH)

            local = [pltpu.make_async_copy(ins[a], slot(a, *me), local_sems.at[a]) for a in range(n)]
            first = []
            for a in range(n):
                first.append(copy(a, 0, me, sibling, src=ins[a]))
                first += [copy(a, 1 + j, me, (*chip, c), src=ins[a]) for j, chip in enumerate(chips)]
            passed = [copy(a, 4 + j, (*chip, c), sibling) for j, chip in enumerate(chips) for a in range(n)]
            return c, chips, me, sibling, copy, local, first, passed

        def begin():
            _, _, _, _, _, local, first, _ = parts()
            for cp in local + first:
                cp.start()

        def middle():
            c, chips, me, _, copy, _, _, passed = parts()
            k = 0
            for j, chip in enumerate(chips):
                for a in range(n):
                    copy(a, 1 + j, (*chip, c), me).wait_recv()
                    passed[k].start()
                    k += 1

        def end():
            c, chips, me, sibling, copy, local, first, passed = parts()
            for a in range(n):
                copy(a, 0, sibling, me).wait_recv()
                for j, chip in enumerate(chips):
                    copy(a, 4 + j, (*chip, 1 - c), me).wait_recv()
            for cp in first + passed:
                cp.wait_send()
            for cp in local:
                cp.wait()

        return begin, middle, end

    return Exchange(
        shards, [jax.ShapeDtypeStruct((N_DEV,) + sh.shape, sh.dtype) for sh in shards],
        [pltpu.SemaphoreType.DMA((n, 7)), pltpu.SemaphoreType.DMA((n, 7)), pltpu.SemaphoreType.DMA((n,))], stages)


def to_core_exchange(grads):
    n = len(grads)

    def stages(ins, outs, sems):
        send_sems, recv_sems = sems

        def copies():
            x, y, c, _ = _position()
            return [pltpu.make_async_remote_copy(
                src_ref=ins[a].at[2 * q + (1 - c)], dst_ref=outs[a].at[q], send_sem=send_sems.at[a, q],
                recv_sem=recv_sems.at[a, q], device_id=(x, y, 1 - c), device_id_type=MESH)
                for a in range(n) for q in range(4)]

        def begin():
            for cp in copies():
                cp.start()

        def end():
            for cp in copies():
                cp.wait()

        return begin, None, end

    return Exchange(grads, [jax.ShapeDtypeStruct((4,) + g.shape[1:], g.dtype) for g in grads],
                    [pltpu.SemaphoreType.DMA((n, 4)), pltpu.SemaphoreType.DMA((n, 4))], stages)


def pair_sum(grad, from_sibling, name):
    _, r, cdim = grad.shape
    tr = min(r, 256)
    c_idx = lax.axis_index("c").astype(jnp.int32).reshape(1)

    def body(c_ref, g_ref, s_ref, o_ref):
        o_ref[...] = g_ref[...] + s_ref[...]

    return pl.pallas_call(
        body,
        name=name,
        grid_spec=pltpu.PrefetchScalarGridSpec(
            num_scalar_prefetch=1,
            grid=(4, r // tr),
            in_specs=[pl.BlockSpec((1, tr, cdim), lambda q, i, cr: (2 * q + cr[0], i, 0)),
                      pl.BlockSpec((1, tr, cdim), lambda q, i, cr: (q, i, 0))],
            out_specs=pl.BlockSpec((1, tr, cdim), lambda q, i, cr: (q, i, 0)),
        ),
        out_shape=jax.ShapeDtypeStruct((4, r, cdim), grad.dtype),
        compiler_params=_params(dimension_semantics=("arbitrary", "arbitrary")),
    )(c_idx, grad, from_sibling)


def to_chip_exchange(pairs):
    n = len(pairs)

    def stages(ins, outs, sems):
        send_sems, recv_sems = sems

        def copies():
            x, y, c, chips = _position()
            return [pltpu.make_async_remote_copy(
                src_ref=ins[a].at[2 * px + py], dst_ref=outs[a].at[j], send_sem=send_sems.at[a, j],
                recv_sem=recv_sems.at[a, j], device_id=(px, py, c), device_id_type=MESH)
                for a in range(n) for j, (px, py) in enumerate(chips)]

        def begin():
            for cp in copies():
                cp.start()

        def end():
            for cp in copies():
                cp.wait()

        return begin, None, end

    return Exchange(pairs, [jax.ShapeDtypeStruct((3,) + p.shape[1:], p.dtype) for p in pairs],
                    [pltpu.SemaphoreType.DMA((n, 3)), pltpu.SemaphoreType.DMA((n, 3))], stages)


def run_exchange(ex, name):
    n_in, n_out = len(ex.arrays), len(ex.out_shape)

    def body(*refs):
        begin, middle, end = ex.stages(refs[:n_in], refs[n_in:n_in + n_out], refs[n_in + n_out:])
        begin()
        if middle is not None:
            middle()
        end()

    return pl.pallas_call(
        body,
        name=name,
        in_specs=[_any_spec()] * n_in,
        out_specs=[_any_spec()] * n_out,
        out_shape=ex.out_shape,
        scratch_shapes=ex.sems,
    )(*ex.arrays)


def _riding(body, n_in, n_out, n_scratch, ex, first, middle, last):
    if ex is None:
        return body
    r_in, r_out = len(ex.arrays), len(ex.out_shape)

    def wrapped(*refs):
        k_in, refs = refs[:n_in], refs[n_in:]
        e_in, refs = refs[:r_in], refs[r_in:]
        k_out, refs = refs[:n_out], refs[n_out:]
        e_out, refs = refs[:r_out], refs[r_out:]
        k_scr, e_sems = refs[:n_scratch], refs[n_scratch:]
        begin, mid, end = ex.stages(e_in, e_out, e_sems)
        pl.when(first())(begin)
        body(*k_in, *k_out, *k_scr)
        if mid is not None:
            pl.when(middle())(mid)
        pl.when(last())(end)

    return wrapped


def _ride_specs(ex):
    if ex is None:
        return [], [], [], [], []
    return [_any_spec()] * len(ex.arrays), [_any_spec()] * len(ex.out_shape), ex.out_shape, ex.sems, ex.arrays


def _adamw(w, g, m, v):
    m = ADAM_B1 * m + (1.0 - ADAM_B1) * g
    v = ADAM_B2 * v + (1.0 - ADAM_B2) * (g * g)
    m_hat = m / (1.0 - ADAM_B1 ** ADAM_STEP)
    v_hat = v / (1.0 - ADAM_B2 ** ADAM_STEP)
    delta = -ADAM_LR * (m_hat / (jnp.sqrt(v_hat) + ADAM_EPS) + ADAM_WD * w)
    return delta, m, v


def sum_adamw(pairs, others, w, m, v, name):
    r, cdim = w.shape
    tr = min(r, 256)
    chip_idx = (2 * lax.axis_index("x") + lax.axis_index("y")).astype(jnp.int32).reshape(1)

    def body(q_ref, p_ref, o_ref, w_ref, m_ref, v_ref, g_out, d_out, m_out, v_out):
        g = p_ref[0] + o_ref[0] + o_ref[1] + o_ref[2]
        g_out[...] = g
        d_out[...], m_out[...], v_out[...] = _adamw(w_ref[...], g, m_ref[...], v_ref[...])

    tile = lambda: pl.BlockSpec((tr, cdim), lambda i, qr: (i, 0))
    return pl.pallas_call(
        body,
        name=name,
        grid_spec=pltpu.PrefetchScalarGridSpec(
            num_scalar_prefetch=1,
            grid=(r // tr,),
            in_specs=[pl.BlockSpec((1, tr, cdim), lambda i, qr: (qr[0], i, 0)),
                      pl.BlockSpec((3, tr, cdim), lambda i, qr: (0, i, 0)), tile(), tile(), tile()],
            out_specs=[tile(), tile(), tile(), tile()],
        ),
        out_shape=[jax.ShapeDtypeStruct((r, cdim), F32)] * 4,
        compiler_params=_params(dimension_semantics=("arbitrary",)),
    )(chip_idx, pairs, others, w, m, v)


SMALL_ROWS = 8


def small_all_reduce(packed):
    shape = packed.shape

    def body(in_ref, out_ref, recv_ref, send_sems, recv_sems):
        x, y, c, _ = _position()
        my_id = 4 * x + 2 * y + c
        recv_ref[my_id] = in_ref[...]
        copies = []
        for rel in range(1, N_DEV):
            fx, fy, fc = (rel >> 2) & 1, (rel >> 1) & 1, rel & 1
            px = 1 - x if fx else x
            py = 1 - y if fy else y
            pc = 1 - c if fc else c
            cp = pltpu.make_async_remote_copy(
                src_ref=in_ref, dst_ref=recv_ref.at[my_id], send_sem=send_sems.at[rel - 1],
                recv_sem=recv_sems.at[rel - 1], device_id=(px, py, pc), device_id_type=MESH)
            cp.start()
            copies.append((cp, pltpu.make_async_remote_copy(
                src_ref=in_ref, dst_ref=recv_ref.at[4 * px + 2 * py + pc], send_sem=send_sems.at[rel - 1],
                recv_sem=recv_sems.at[rel - 1], device_id=(px, py, pc), device_id_type=MESH)))
        for cp, landing in copies:
            landing.wait_recv()
        for cp, landing in copies:
            cp.wait_send()
        total = recv_ref[0]
        for k in range(1, N_DEV):
            total = total + recv_ref[k]
        out_ref[...] = total

    return pl.pallas_call(
        body,
        name="small_all_reduce",
        in_specs=[_vmem_spec()],
        out_specs=_vmem_spec(),
        out_shape=jax.ShapeDtypeStruct(shape, F32),
        scratch_shapes=[pltpu.VMEM((N_DEV,) + shape, F32), pltpu.SemaphoreType.DMA((N_DEV - 1,)),
                        pltpu.SemaphoreType.DMA((N_DEV - 1,))],
    )(packed)


def small_adamw(reduced, w, m, v):
    def body(r_ref, w_ref, m_ref, v_ref, g_out, d_out, m_out, v_out, loss_out):
        red = r_ref[...]
        wv = w_ref[...]
        lb = _lower_bound(jnp.concatenate([wv[5:6, :HGRN_WIDTH], wv[5:6, HGRN_WIDTH:]], axis=0))
        t = red[5:6, :HGRN_WIDTH] * lb * (1.0 - lb)
        row = lax.broadcasted_iota(jnp.int32, red.shape, 0)
        g = jnp.where(row == 5, jnp.concatenate([t, -t], axis=1), jnp.where(row >= 6, 0.0, red))
        g_out[...] = g
        d_out[...], m_out[...], v_out[...] = _adamw(wv, g, m_ref[...], v_ref[...])
        loss = jnp.sum(red[6:7, :], axis=-1, keepdims=True) * (0.5 / D_MODEL)
        loss_out[...] = jnp.broadcast_to(loss, loss_out.shape)

    return pl.pallas_call(
        body,
        name="small_adamw",
        in_specs=[_vmem_spec()] * 4,
        out_specs=[_vmem_spec()] * 5,
        out_shape=[jax.ShapeDtypeStruct(reduced.shape, F32)] * 4 + [jax.ShapeDtypeStruct((8, 128), F32)],
    )(reduced, w, m, v)


def _pack_small(g1, gp, g_pre, g_post, an, hn, logits_or_dlb, extra=None):
    row5 = logits_or_dlb.reshape(1, -1)
    row5 = jnp.pad(row5, ((0, 0), (0, D_MODEL - row5.shape[1])))
    row6 = jnp.zeros((1, D_MODEL), F32) if extra is None else extra
    return jnp.concatenate([g1, gp, g_pre, g_post, jnp.concatenate([an, hn], axis=1), row5, row6,
                            jnp.zeros((1, D_MODEL), F32)], axis=0)


def _unpack_small(p):
    return dict(mix_pre_norm=p[0:1], mix_post_norm=p[1:2], mlp_pre_norm=p[2:3], mlp_post_norm=p[3:4],
                attn_out_norm=p[4:5, :ATTN_WIDTH], hgrn_out_norm=p[4:5, ATTN_WIDTH:],
                hgrn_lb_logits=p[5].reshape(2, HGRN_WIDTH))


BIG = ("w_in", "w_out", "w_ff1", "w_ff2")
ORDER = ("mix_pre_norm", "w_in", "attn_out_norm", "hgrn_lb_logits", "hgrn_out_norm", "w_out", "mix_post_norm",
         "mlp_pre_norm", "w_ff1", "w_ff2", "mlp_post_norm")


def kernel(x, mix_pre_norm, w_in, attn_out_norm, hgrn_lb_logits, hgrn_out_norm, w_out, mix_post_norm, mlp_pre_norm, w_ff1, w_ff2, mlp_post_norm, loss_target, m_mix_pre_norm, m_w_in, m_attn_out_norm, m_hgrn_lb_logits, m_hgrn_out_norm, m_w_out, m_mix_post_norm, m_mlp_pre_norm, m_w_ff1, m_w_ff2, m_mlp_post_norm, v_mix_pre_norm, v_w_in, v_attn_out_norm, v_hgrn_lb_logits, v_hgrn_out_norm, v_w_out, v_mix_post_norm, v_mlp_pre_norm, v_w_ff1, v_w_ff2, v_mlp_post_norm):
    w = dict(w_in=w_in[0], w_out=w_out[0], w_ff1=w_ff1[0], w_ff2=w_ff2[0])
    m = dict(w_in=m_w_in[0], w_out=m_w_out[0], w_ff1=m_w_ff1[0], w_ff2=m_w_ff2[0])
    v = dict(w_in=v_w_in[0], w_out=v_w_out[0], w_ff1=v_w_ff1[0], w_ff2=v_w_ff2[0])

    dx, big, small = train_step(x[0], loss_target[0], mix_pre_norm, attn_out_norm, hgrn_lb_logits, hgrn_out_norm,
                                mix_post_norm, mlp_pre_norm, mlp_post_norm, w, m, v)

    packed_g = _pack_small(small["dg1"], small["dgp"], small["dg_pre"], small["dg_post"], small["dan"], small["dhn"],
                           small["dlb"], small["loss_vec"])
    reduced = small_all_reduce(packed_g)
    pack = lambda a, b, c2, d, e, f, g: _pack_small(a, b, c2, d, e, f, g)
    w_s = pack(mix_pre_norm, mix_post_norm, mlp_pre_norm, mlp_post_norm, attn_out_norm, hgrn_out_norm, hgrn_lb_logits)
    m_s = pack(m_mix_pre_norm, m_mix_post_norm, m_mlp_pre_norm, m_mlp_post_norm, m_attn_out_norm, m_hgrn_out_norm,
               m_hgrn_lb_logits)
    v_s = pack(v_mix_pre_norm, v_mix_post_norm, v_mlp_pre_norm, v_mlp_post_norm, v_attn_out_norm, v_hgrn_out_norm,
               v_hgrn_lb_logits)
    g_s, d_s, nm_s, nv_s, loss = small_adamw(reduced, w_s, m_s, v_s)
    small_out = [_unpack_small(t) for t in (g_s, d_s, nm_s, nv_s)]

    outs = [loss[0, 0], dx[None]]
    for kind in range(4):
        for name in ORDER:
            outs.append(big[name][kind][None] if name in BIG else small_out[kind][name])
    return tuple(outs)
```

```python
import functools
import math

import jax
import jax.numpy as jnp
from jax import lax
from jax.experimental import pallas as pl
from jax.experimental.pallas import tpu as pltpu

F32 = jnp.float32
BF16 = jnp.bfloat16

D_MODEL = 1024
SEQ = 4096
ATTN_WIDTH = 512
ATTN_HEAD_DIM = 64
ATTN_HEADS = 8
ATTN_BLOCK = 128
DILATIONS = (1, 4, 16)
HGRN_WIDTH = 512
HGRN_HEADS = 4
HGRN_HEAD_DIM = 128
HGRN_CHUNK = 64
IN_PROJ_WIDTH = 3584
D_FF = 4096
RMS_EPS = 1e-6
N_DEV = 8
ADAM_LR = 0.001
ADAM_B1 = 0.9
ADAM_B2 = 0.999
ADAM_EPS = 1e-08
ADAM_WD = 0.01
ADAM_STEP = 10

SUBLANES = 8
LANES = 128
COLUMN_UNROLL = 4
SUB_BLOCK = 16
TOKEN_TILE = 256
VMEM_LIMIT = 56 * 1024 * 1024
NEG_BIG = -1e30
MESH = pl.DeviceIdType.MESH


def _params(**kw):
    return pltpu.CompilerParams(vmem_limit_bytes=VMEM_LIMIT, **kw)


def _vmem_spec():
    return pl.BlockSpec(memory_space=pltpu.VMEM)


def _dot(a, b):
    return jnp.dot(a, b, preferred_element_type=F32)


def _dot_nt(a, b):
    return lax.dot_general(a, b, (((1,), (1,)), ((), ())), preferred_element_type=F32)


def _dot_tn(a, b):
    return lax.dot_general(a, b, (((0,), (0,)), ((), ())), preferred_element_type=F32)


def _sigmoid(x):
    return 1.0 / (1.0 + jnp.exp(-x))


def _rms_fwd(x, gain, width):
    r = lax.rsqrt(jnp.sum(x * x, axis=-1, keepdims=True) * (1.0 / width) + RMS_EPS)
    return x * r * gain


def _rms_bwd(dy, x, gain, width):
    r = lax.rsqrt(jnp.sum(x * x, axis=-1, keepdims=True) * (1.0 / width) + RMS_EPS)
    xhat = x * r
    dxhat = dy * gain
    dx = r * (dxhat - xhat * (jnp.sum(dxhat * xhat, axis=-1, keepdims=True) * (1.0 / width)))
    return dx, dy * xhat


def _split3(x):
    hi = x.astype(BF16)
    r1 = x - hi.astype(F32)
    mid = r1.astype(BF16)
    lo = (r1 - mid.astype(F32)).astype(BF16)
    return hi, mid, lo


def _tri_sum(tri_bf16, x):
    hi, mid, lo = _split3(x)
    return _dot(tri_bf16, hi) + _dot(tri_bf16, mid) + _dot(tri_bf16, lo)


def _dilated_spec(d, tm, width):
    return pl.BlockSpec((d, tm // d, width), lambda i: (0, i, 0))


def _lane_blocks(ref, value):
    for c in range(ref.shape[0]):
        ref[c] = value[:, c * LANES:(c + 1) * LANES]


def _to_dilated(src_ref, dst_ref, d, tm, cast=None):
    for r in range(d):
        for c in range(src_ref.shape[0]):
            v = src_ref[c] if d == 1 else src_ref[c, pl.ds(r, tm // d, stride=d), :]
            dst_ref[r, :, c * LANES:(c + 1) * LANES] = v if cast is None else v.astype(cast)


def _from_dilated(src_ref, scratch_ref, d, tm):
    if d == 1:
        return src_ref[0]
    nblk = scratch_ref.shape[0]
    for r in range(d):
        for c in range(nblk):
            scratch_ref[c, pl.ds(r, tm // d, stride=d), :] = src_ref[r, :, c * LANES:(c + 1) * LANES]
    return jnp.concatenate([scratch_ref[c] for c in range(nblk)], axis=1)


def in_proj_fwd(x, g1, w_in_b):
    s = x.shape[0]
    tm = TOKEN_TILE
    qkv_w = 3 * ATTN_WIDTH
    hg_w = IN_PROJ_WIDTH - qkv_w

    def body(x_ref, g_ref, w_ref, hg_ref, h_ref, *rest):
        qkv_refs, qkv_scr = rest[:len(DILATIONS)], rest[len(DILATIONS)]
        h = _rms_fwd(x_ref[...], g_ref[...], D_MODEL).astype(BF16)
        h_ref[...] = h
        proj = _dot(h, w_ref[...])
        hg_ref[...] = proj[:, qkv_w:]
        _lane_blocks(qkv_scr, proj[:, :qkv_w])
        for d, ref in zip(DILATIONS, qkv_refs):
            _to_dilated(qkv_scr, ref, d, tm, cast=BF16)

    return pl.pallas_call(
        body,
        name="in_proj_fwd",
        grid=(s // tm,),
        in_specs=[
            pl.BlockSpec((tm, D_MODEL), lambda i: (i, 0)),
            pl.BlockSpec((1, D_MODEL), lambda i: (0, 0)),
            _vmem_spec(),
        ],
        out_specs=[
            pl.BlockSpec((tm, hg_w), lambda i: (i, 0)),
            pl.BlockSpec((tm, D_MODEL), lambda i: (i, 0)),
        ] + [_dilated_spec(d, tm, qkv_w) for d in DILATIONS],
        out_shape=[jax.ShapeDtypeStruct((s, hg_w), F32), jax.ShapeDtypeStruct((s, D_MODEL), BF16)] + [
            jax.ShapeDtypeStruct((d, s // d, qkv_w), BF16) for d in DILATIONS],
        scratch_shapes=[pltpu.VMEM((qkv_w // LANES, tm, LANES), F32)],
        compiler_params=_params(dimension_semantics=("arbitrary",)),
    )(x, g1, w_in_b)


def _attn_scores(qm, kcat, head, dilation, first_block):
    s = _dot_nt(qm, kcat) * (ATTN_HEAD_DIM ** -0.5)
    qi = lax.broadcasted_iota(jnp.int32, (ATTN_BLOCK, 2 * ATTN_BLOCK), 0)
    kj = lax.broadcasted_iota(jnp.int32, (ATTN_BLOCK, 2 * ATTN_BLOCK), 1)
    dist = qi + ATTN_BLOCK - kj
    valid = (dist >= 0) & (dist <= ATTN_BLOCK) & ((kj >= ATTN_BLOCK) | jnp.logical_not(first_block))
    slope = 2.0 ** (-8.0 * (head + 1) / ATTN_HEADS)
    bias = dist.astype(F32) * (-slope * dilation)
    return jnp.where(valid, s + bias, NEG_BIG)


def _lane_half(shape, sub):
    lane = lax.broadcasted_iota(jnp.int32, shape, 1)
    return (lane < ATTN_HEAD_DIM) if sub == 0 else (lane >= ATTN_HEAD_DIM)


def _sub_block(col, row):
    return pl.BlockSpec((None, ATTN_BLOCK, ATTN_WIDTH), lambda r, n: (r, row(n), col))


def attn_fwd(qkv, dilation):
    d, length, _ = qkv.shape
    assert d == dilation
    nb = length // ATTN_BLOCK

    def body(q_ref, kc_ref, kp_ref, vc_ref, vp_ref, o_ref, lse_ref):
        first = pl.program_id(1) == 0
        for pair in range(ATTN_HEADS // 2):
            lanes = slice(pair * 128, (pair + 1) * 128)
            q2 = q_ref[:, lanes]
            kcat = jnp.concatenate([kp_ref[:, lanes], kc_ref[:, lanes]], axis=0)
            vcat = jnp.concatenate([vp_ref[:, lanes], vc_ref[:, lanes]], axis=0)
            o_pair = jnp.zeros((ATTN_BLOCK, 128), F32)
            lse_pair = jnp.zeros((ATTN_BLOCK, 128), F32)
            for sub in range(2):
                keep = _lane_half((ATTN_BLOCK, 128), sub)
                qm = jnp.where(keep, q2, jnp.zeros_like(q2))
                sc = _attn_scores(qm, kcat, 2 * pair + sub, d, first)
                m = jnp.max(sc, axis=-1, keepdims=True)
                p = jnp.exp(sc - m)
                den = jnp.sum(p, axis=-1, keepdims=True)
                o = _dot(p.astype(BF16), vcat) / den
                o_pair = jnp.where(keep, o, o_pair)
                lse_pair = jnp.where(keep, m + jnp.log(den), lse_pair)
            o_ref[:, lanes] = o_pair
            lse_ref[:, lanes] = lse_pair

    cur = lambda n: n
    prev = lambda n: jnp.maximum(n - 1, 0)
    return pl.pallas_call(
        body,
        name=f"attn_fwd_d{d}",
        grid=(d, nb),
        in_specs=[_sub_block(0, cur), _sub_block(1, cur), _sub_block(1, prev), _sub_block(2, cur), _sub_block(2, prev)],
        out_specs=[_sub_block(0, cur), _sub_block(0, cur)],
        out_shape=[jax.ShapeDtypeStruct((d, length, ATTN_WIDTH), F32)] * 2,
        compiler_params=_params(dimension_semantics=("arbitrary", "arbitrary")),
    )(qkv, qkv, qkv, qkv, qkv)


def attn_bwd(qkv, d_out, lse, delta, dilation, ride=None):
    d, length, _ = qkv.shape
    assert d == dilation
    nb = length // ATTN_BLOCK

    steps = d * nb + 1

    def body(q_ref, kc_ref, kp_ref, vc_ref, vp_ref, do_ref, lse_ref, dl_ref, dq_ref, dk_ref, dv_ref, ck_ref, cv_ref):
        t = pl.program_id(0)

        @pl.when(t == 0)
        def _():
            ck_ref[...] = jnp.zeros_like(ck_ref)
            cv_ref[...] = jnp.zeros_like(cv_ref)

        @pl.when(t < steps - 1)
        def _():
            first = t % nb == 0
            for pair in range(ATTN_HEADS // 2):
                lanes = slice(pair * 128, (pair + 1) * 128)
                q2 = q_ref[:, lanes]
                do2 = do_ref[:, lanes]
                kcat = jnp.concatenate([kp_ref[:, lanes], kc_ref[:, lanes]], axis=0)
                vcat = jnp.concatenate([vp_ref[:, lanes], vc_ref[:, lanes]], axis=0)
                dq_pair = jnp.zeros((ATTN_BLOCK, 128), F32)
                dk_cat = jnp.zeros((2 * ATTN_BLOCK, 128), F32)
                dv_cat = jnp.zeros((2 * ATTN_BLOCK, 128), F32)
                for sub in range(2):
                    keep = _lane_half((ATTN_BLOCK, 128), sub)
                    col = pair * 128 + sub * ATTN_HEAD_DIM
                    qm = jnp.where(keep, q2, jnp.zeros_like(q2))
                    dom = jnp.where(keep, do2, 0.0).astype(BF16)
                    sc = _attn_scores(qm, kcat, 2 * pair + sub, d, first)
                    p = jnp.exp(sc - lse_ref[:, col:col + 1])
                    dp = _dot_nt(dom, vcat)
                    ds = (p * (dp - dl_ref[:, col:col + 1]) * (ATTN_HEAD_DIM ** -0.5)).astype(BF16)
                    dq_pair = jnp.where(keep, _dot(ds, kcat), dq_pair)
                    dk_cat = dk_cat + _dot_tn(ds, qm)
                    dv_cat = dv_cat + _dot_tn(p.astype(BF16), dom)
                dq_ref[:, lanes] = dq_pair
                dk_ref[:, lanes] = ck_ref[:, lanes] + dk_cat[:ATTN_BLOCK]
                dv_ref[:, lanes] = cv_ref[:, lanes] + dv_cat[:ATTN_BLOCK]
                ck_ref[:, lanes] = dk_cat[ATTN_BLOCK:]
                cv_ref[:, lanes] = dv_cat[ATTN_BLOCK:]

        @pl.when(t == steps - 1)
        def _():
            dk_ref[...] = ck_ref[...]
            dv_ref[...] = cv_ref[...]

    blk = (ATTN_BLOCK, ATTN_WIDTH)

    def spec(col, shift):
        def index(t):
            f = jnp.minimum(t, steps - 2) if shift > -2 else jnp.maximum(t - 1, 0)
            r, n = f // nb, f % nb
            return (r, jnp.maximum(n - 1, 0) if shift == -1 else n, col)
        return pl.BlockSpec((None, ATTN_BLOCK, ATTN_WIDTH), index)

    step = lambda k: (lambda: pl.program_id(0) == k)
    e_in, e_out, e_shape, e_scr, e_args = _ride_specs(ride)
    return pl.pallas_call(
        _riding(body, 8, 3, 2, ride, step(0), step(steps // 2), step(steps - 1)),
        name=f"attn_bwd_d{d}",
        grid=(steps,),
        in_specs=[spec(0, 0), spec(1, 0), spec(1, -1), spec(2, 0), spec(2, -1), spec(0, 0), spec(0, 0), spec(0, 0)] + e_in,
        out_specs=[spec(0, 0), spec(0, -2), spec(0, -2)] + e_out,
        out_shape=[jax.ShapeDtypeStruct((d, length, ATTN_WIDTH), F32)] * 3 + e_shape,
        scratch_shapes=[pltpu.VMEM(blk, F32), pltpu.VMEM(blk, F32)] + e_scr,
        compiler_params=_params(dimension_semantics=("arbitrary",)),
    )(qkv, qkv, qkv, qkv, qkv, d_out, lse, delta, *e_args)


def _lower_bound(logits):
    return _sigmoid(logits[0:1, :] - logits[1:2, :])


def _hgrn_gates(q, fp, lb):
    sq = _sigmoid(q)
    qf = q * sq
    sig = _sigmoid(fp)
    f = lb + (1.0 - lb) * sig
    kf = (1.0 - lb) * _sigmoid(-fp)
    return sq, qf, sig, f, kf


def _tril_bf16(n, upper=False):
    r = lax.broadcasted_iota(jnp.int32, (n, n), 0)
    c = lax.broadcasted_iota(jnp.int32, (n, n), 1)
    keep = (c >= r) if upper else (c <= r)
    return jnp.where(keep, 1.0, 0.0).astype(BF16)


def _hgrn_diagonal_loops(c_len, diagonal):
    for half in range(SUB_BLOCK // SUBLANES):
        def step(jj, carry, half=half):
            j = half * SUBLANES + jj
            for i in range(c_len // SUB_BLOCK):
                diagonal(slice(i * SUB_BLOCK + half * SUBLANES, (i + 1) * SUB_BLOCK), j, i * SUB_BLOCK + j)
            return carry

        lax.fori_loop(0, SUBLANES, step, 0, unroll=COLUMN_UNROLL)


def _hgrn_off_diagonal(b, qf, kf):
    c_len, width = b.shape
    edges = [b[0:1, :]] + [b[i * SUB_BLOCK - 1:i * SUB_BLOCK, :] for i in range(1, c_len // SUB_BLOCK)]
    eq = jnp.exp(b - jnp.concatenate([jnp.broadcast_to(e, (SUB_BLOCK, width)) for e in edges], axis=0))
    q_til = qf * eq
    k_til, ek = [], []
    for i in range(1, c_len // SUB_BLOCK):
        n = i * SUB_BLOCK
        e = jnp.exp(edges[i] - b[:n, :])
        ek.append(e)
        k_til.append(jnp.concatenate([kf[:n, :] * e, jnp.zeros((2 * c_len - n, width), F32)], axis=0))
    return q_til, k_til, eq, ek


def _split2(x):
    hi = x.astype(BF16)
    return hi, (x - hi.astype(F32)).astype(BF16)


def hgrn_fwd(proj, lb, ride=None):
    s = proj.shape[0]
    c_len, nh, hd = HGRN_CHUNK, HGRN_HEADS, HGRN_HEAD_DIM
    n_chunks = s // c_len
    col0 = 0

    def body(q_ref, f_ref, i_ref, lb_ref, o_ref, st_out_ref, a_out_ref, st_ref, b_ref, qf_ref, kf_ref, a_ref):
        @pl.when(pl.program_id(0) == 0)
        def _():
            st_ref[...] = jnp.zeros_like(st_ref)

        lbv = _lower_bound(lb_ref[...])
        _, qf, _, f, kf = _hgrn_gates(q_ref[...], f_ref[...], lbv)
        b = _tri_sum(_tril_bf16(c_len), jnp.log(f))
        b_ref[...] = b
        qf_ref[...] = qf
        kf_ref[...] = kf
        a_ref[...] = jnp.zeros_like(a_ref)

        def diagonal(rows, j, key):
            bj = b_ref[pl.ds(key, 1), :]
            kj = kf_ref[pl.ds(key, 1), :]
            nrow = rows.stop - rows.start
            t_loc = lax.broadcasted_iota(jnp.int32, (nrow, nh * hd), 0) + (rows.start % SUB_BLOCK)
            e = jnp.exp(jnp.where(t_loc >= j, b_ref[rows, :] - bj, NEG_BIG))
            prod = qf_ref[rows, :] * kj * e
            lane = lax.broadcasted_iota(jnp.int32, (nrow, hd), 1)
            for h in range(nh):
                col = jnp.sum(prod[:, h * hd:(h + 1) * hd], axis=-1, keepdims=True)
                a_ref[h, rows, :] = jnp.where(lane == key, col, a_ref[h, rows, :])

        _hgrn_diagonal_loops(c_len, diagonal)
        q_til, k_til, _, _ = _hgrn_off_diagonal(b, qf, kf)
        q_til = q_til.astype(BF16)
        k_til = [k.astype(BF16) for k in k_til]

        b_last = b[c_len - 1:c_len, :]
        qb = (qf * jnp.exp(b)).astype(BF16)
        kb2 = (kf * jnp.exp(b_last - b)).astype(BF16)
        vf = i_ref[...].astype(BF16)
        for h in range(nh):
            hs = slice(h * hd, (h + 1) * hd)
            st = st_ref[h]
            st_out_ref[0, h] = st
            off = [jnp.zeros((SUB_BLOCK, hd), F32)]
            for i in range(1, c_len // SUB_BLOCK):
                off.append(_dot_nt(q_til[i * SUB_BLOCK:(i + 1) * SUB_BLOCK, hs], k_til[i - 1][:, hs]))
            a_h = a_ref[h] + jnp.concatenate(off, axis=0)
            a_out_ref[:, hs] = a_h
            o_ref[:, hs] = _dot_nt(qb[:, hs], st.astype(BF16)) + _dot(a_h[:, :c_len].astype(BF16), vf[:, hs])
            st_ref[h] = st * jnp.exp(b_last[:, hs]) + _dot_tn(vf[:, hs], kb2[:, hs])

    blk = (c_len, HGRN_WIDTH)
    step = lambda k: (lambda: pl.program_id(0) == k)
    e_in, e_out, e_shape, e_scr, e_args = _ride_specs(ride)
    return pl.pallas_call(
        _riding(body, 4, 3, 5, ride, step(0), step(n_chunks // 2), step(n_chunks - 1)),
        name="hgrn_fwd",
        grid=(n_chunks,),
        in_specs=[
            pl.BlockSpec(blk, lambda c: (c, col0)),
            pl.BlockSpec(blk, lambda c: (c, col0 + 1)),
            pl.BlockSpec(blk, lambda c: (c, col0 + 2)),
            pl.BlockSpec((2, HGRN_WIDTH), lambda c: (0, 0)),
        ] + e_in,
        out_specs=[
            pl.BlockSpec(blk, lambda c: (c, 0)),
            pl.BlockSpec((1, nh, hd, hd), lambda c: (c, 0, 0, 0)),
            pl.BlockSpec(blk, lambda c: (c, 0)),
        ] + e_out,
        out_shape=[
            jax.ShapeDtypeStruct((s, HGRN_WIDTH), F32),
            jax.ShapeDtypeStruct((n_chunks, nh, hd, hd), F32),
            jax.ShapeDtypeStruct((s, nh * hd), F32),
        ] + e_shape,
        scratch_shapes=[
            pltpu.VMEM((nh, hd, hd), F32),
            pltpu.VMEM(blk, F32),
            pltpu.VMEM(blk, F32),
            pltpu.VMEM(blk, F32),
            pltpu.VMEM((nh, c_len, hd), F32),
        ] + e_scr,
        compiler_params=_params(dimension_semantics=("arbitrary",)),
    )(proj, proj, proj, lb, *e_args)


def hgrn_bwd(proj, lb, d_o, states, a_mat, ride=None):
    s = proj.shape[0]
    c_len, nh, hd = HGRN_CHUNK, HGRN_HEADS, HGRN_HEAD_DIM
    n_chunks = s // c_len
    col0 = 0
    last = n_chunks - 1

    def body(q_ref, f_ref, i_ref, lb_ref, do_ref, st_in_ref, a_in_ref, dq_ref, df_ref, di_ref, dlb_ref,
             dst_ref, b_ref, qf_ref, kf_ref, da_ref, dqi_ref, dki_ref):
        @pl.when(pl.program_id(0) == 0)
        def _():
            dst_ref[...] = jnp.zeros_like(dst_ref)
            dlb_ref[...] = jnp.zeros_like(dlb_ref)

        lbv = _lower_bound(lb_ref[...])
        q = q_ref[...]
        sq, qf, sig, f, kf = _hgrn_gates(q, f_ref[...], lbv)
        b = _tri_sum(_tril_bf16(c_len), jnp.log(f))
        b_ref[...] = b
        qf_ref[...] = qf
        kf_ref[...] = kf
        b_last = b[c_len - 1:c_len, :]
        eb = jnp.exp(b)
        ebl = jnp.exp(b_last - b)
        qb = qf * eb
        kb2 = kf * ebl
        vf = i_ref[...]
        d_o = do_ref[...]
        qb_b, kb2_b, vf_b, do_b = qb.astype(BF16), kb2.astype(BF16), vf.astype(BF16), d_o.astype(BF16)
        tq = lax.broadcasted_iota(jnp.int32, (c_len, hd), 0)
        lane = lax.broadcasted_iota(jnp.int32, (c_len, hd), 1)

        dqb_parts, dvf_parts, dkb2_parts, dbl_parts = [], [], [], []
        for h in range(nh):
            hs = slice(h * hd, (h + 1) * hd)
            st = st_in_ref[0, h]
            dst = dst_ref[h]
            st_b, dst_b = st.astype(BF16), dst.astype(BF16)
            a_h = a_in_ref[:, hs][:, :c_len].astype(BF16)
            dqb_parts.append(_dot(do_b[:, hs], st_b))
            dvf_parts.append(_dot_tn(a_h, do_b[:, hs]) + _dot_nt(kb2_b[:, hs], dst_b))
            dkb2_parts.append(_dot(vf_b[:, hs], dst_b))
            da = _dot_nt(do_b[:, hs], vf_b[:, hs])
            da = jnp.concatenate([da, jnp.zeros((c_len, hd - c_len), F32)], axis=1)
            da_ref[h] = jnp.where(tq >= lane, da, 0.0)
            dbl_parts.append(jnp.sum(dst * st, axis=0, keepdims=True) * jnp.exp(b_last[:, hs]))
            dst_ref[h] = dst * jnp.exp(b_last[:, hs]) + _dot_tn(do_b[:, hs], qb_b[:, hs])
        dqb = jnp.concatenate(dqb_parts, axis=1)
        dvf = jnp.concatenate(dvf_parts, axis=1)
        dkb2 = jnp.concatenate(dkb2_parts, axis=1)
        dbl = jnp.concatenate(dbl_parts, axis=1) + jnp.sum(dkb2 * kb2, axis=0, keepdims=True)

        dqi_ref[...] = jnp.zeros_like(dqi_ref)
        t_idx = lax.broadcasted_iota(jnp.int32, (c_len, nh * hd), 0)

        def diagonal(rows, j, key):
            bj = b_ref[pl.ds(key, 1), :]
            kj = kf_ref[pl.ds(key, 1), :]
            nrow = rows.stop - rows.start
            t_loc = lax.broadcasted_iota(jnp.int32, (nrow, nh * hd), 0) + (rows.start % SUB_BLOCK)
            e = jnp.exp(jnp.where(t_loc >= j, b_ref[rows, :] - bj, NEG_BIG))
            lane_r = lax.broadcasted_iota(jnp.int32, (nrow, hd), 1)
            cols = [jnp.sum(jnp.where(lane_r == key, da_ref[h, rows, :], 0.0), axis=-1, keepdims=True)
                    for h in range(nh)]
            w = e * jnp.concatenate([jnp.broadcast_to(cc, (nrow, hd)) for cc in cols], axis=1)
            dqi_ref[rows, :] += w * kj
            dki_ref[pl.ds(key, 1), :] = jnp.sum(w * qf_ref[rows, :], axis=0, keepdims=True)

        _hgrn_diagonal_loops(c_len, diagonal)

        q_til, k_til, eq, ek = _hgrn_off_diagonal(b, qf, kf)
        q_hi, q_lo = _split2(q_til)
        k_pairs = [_split2(k) for k in k_til]
        n_sub = c_len // SUB_BLOCK
        dq_heads, dk_heads = [], []
        for h in range(nh):
            hs = slice(h * hd, (h + 1) * hd)
            dq_rows = [jnp.zeros((SUB_BLOCK, hd), F32)]
            dk_h = jnp.zeros((c_len, hd), F32)
            for i in range(1, n_sub):
                rows = slice(i * SUB_BLOCK, (i + 1) * SUB_BLOCK)
                n = i * SUB_BLOCK
                da_i = da_ref[h, rows, :].astype(BF16)
                k_hi, k_lo = k_pairs[i - 1]
                dq_rows.append((_dot(da_i, k_hi[:, hs]) + _dot(da_i, k_lo[:, hs])) * eq[rows, hs])
                dk_t = (_dot_tn(da_i, q_hi[rows, hs]) + _dot_tn(da_i, q_lo[rows, hs]))[:n, :] * ek[i - 1][:, hs]
                dk_h = dk_h + jnp.concatenate([dk_t, jnp.zeros((c_len - n, hd), F32)], axis=0)
            dq_heads.append(jnp.concatenate(dq_rows, axis=0))
            dk_heads.append(dk_h)
        dq_intra = dqi_ref[...] + jnp.concatenate(dq_heads, axis=1)
        dk_intra = dki_ref[...] + jnp.concatenate(dk_heads, axis=1)

        db = dqb * qb + qf * dq_intra - kf * dk_intra - dkb2 * kb2
        db = db + jnp.where(t_idx == c_len - 1, dbl, 0.0)
        dg = _tri_sum(_tril_bf16(c_len, upper=True), db)
        dqf = dqb * eb + dq_intra
        dkf = dkb2 * ebl + dk_intra
        dq_ref[...] = dqf * (sq * (1.0 + q * (1.0 - sq)))
        dfv = dg / f - dkf
        df_ref[...] = dfv * (1.0 - lbv) * sig * (1.0 - sig)
        di_ref[...] = dvf
        dlb_ref[...] += jnp.sum(dfv * (1.0 - sig), axis=0, keepdims=True)

    blk = (c_len, HGRN_WIDTH)
    rev = lambda c: last - c
    step = lambda k: (lambda: pl.program_id(0) == k)
    e_in, e_out, e_shape, e_scr, e_args = _ride_specs(ride)
    return pl.pallas_call(
        _riding(body, 7, 4, 7, ride, step(0), step(n_chunks // 2), step(last)),
        name="hgrn_bwd",
        grid=(n_chunks,),
        in_specs=[
            pl.BlockSpec(blk, lambda c: (rev(c), col0)),
            pl.BlockSpec(blk, lambda c: (rev(c), col0 + 1)),
            pl.BlockSpec(blk, lambda c: (rev(c), col0 + 2)),
            pl.BlockSpec((2, HGRN_WIDTH), lambda c: (0, 0)),
            pl.BlockSpec(blk, lambda c: (rev(c), 0)),
            pl.BlockSpec((1, nh, hd, hd), lambda c: (rev(c), 0, 0, 0)),
            pl.BlockSpec(blk, lambda c: (rev(c), 0)),
        ] + e_in,
        out_specs=[
            pl.BlockSpec(blk, lambda c: (rev(c), 0)),
            pl.BlockSpec(blk, lambda c: (rev(c), 0)),
            pl.BlockSpec(blk, lambda c: (rev(c), 0)),
            pl.BlockSpec((1, HGRN_WIDTH), lambda c: (0, 0)),
        ] + e_out,
        out_shape=[jax.ShapeDtypeStruct((s, HGRN_WIDTH), F32)] * 3 + [jax.ShapeDtypeStruct((1, HGRN_WIDTH), F32)] + e_shape,
        scratch_shapes=[
            pltpu.VMEM((nh, hd, hd), F32),
            pltpu.VMEM(blk, F32),
            pltpu.VMEM(blk, F32),
            pltpu.VMEM(blk, F32),
            pltpu.VMEM((nh, c_len, hd), F32),
            pltpu.VMEM(blk, F32),
            pltpu.VMEM(blk, F32),
        ] + e_scr,
        compiler_params=_params(dimension_semantics=("arbitrary",)),
    )(proj, proj, proj, lb, d_o, states, a_mat, *e_args)


def _row_spec(tm, width, col=0):
    return pl.BlockSpec((tm, width), lambda i: (i, col))


def _const_spec(width):
    return pl.BlockSpec((1, width), lambda i: (0, 0))


def _acc_rows(ref, value):
    @pl.when(pl.program_id(0) == 0)
    def _():
        ref[...] = jnp.zeros_like(ref)

    ref[...] += jnp.sum(value, axis=0, keepdims=True)


def mix_fwd(attn_parts, o_h, proj, an, hn, w_out_b, gp, x):
    s = x.shape[0]
    tm = TOKEN_TILE
    gate_col = 3
    hd = HGRN_HEAD_DIM
    nd = len(DILATIONS)

    def body(*refs):
        o_refs, l_refs = refs[:nd], refs[nd:2 * nd]
        oh_ref, gate_ref, an_ref, hn_ref, w_ref, gp_ref, x_ref = refs[2 * nd:2 * nd + 7]
        x1_ref, cat_ref, mixed_ref, attn_ref = refs[2 * nd + 7:2 * nd + 11]
        lse_refs = refs[2 * nd + 11:3 * nd + 11]
        o_scr, l_scr, lse_scr = refs[3 * nd + 11:]
        os_ = [_from_dilated(r, o_scr.at[k], d, tm) for k, (r, d) in enumerate(zip(o_refs, DILATIONS))]
        ls = [_from_dilated(r, l_scr.at[k], d, tm) for k, (r, d) in enumerate(zip(l_refs, DILATIONS))]
        m = jnp.maximum(jnp.maximum(ls[0], ls[1]), ls[2])
        es = [jnp.exp(l - m) for l in ls]
        den = es[0] + es[1] + es[2]
        attn = (es[0] * os_[0] + es[1] * os_[1] + es[2] * os_[2]) / den
        attn_ref[...] = attn
        _lane_blocks(lse_scr, m + jnp.log(den))
        for d, ref in zip(DILATIONS, lse_refs):
            _to_dilated(lse_scr, ref, d, tm)
        cat_ref[:, :ATTN_WIDTH] = _rms_fwd(attn, an_ref[...], ATTN_WIDTH).astype(BF16)
        gate = gate_ref[...]
        silu_g = gate * _sigmoid(gate)
        for h in range(HGRN_HEADS):
            hs = slice(h * hd, (h + 1) * hd)
            rec = _rms_fwd(oh_ref[:, hs], hn_ref[:, hs], hd) * silu_g[:, hs]
            cat_ref[:, ATTN_WIDTH + h * hd:ATTN_WIDTH + (h + 1) * hd] = rec.astype(BF16)
        mixed = _dot(cat_ref[...], w_ref[...])
        mixed_ref[...] = mixed
        x1_ref[...] = x_ref[...] + _rms_fwd(mixed, gp_ref[...], D_MODEL)

    aw = ATTN_WIDTH
    return pl.pallas_call(
        body,
        name="mix_fwd",
        grid=(s // tm,),
        in_specs=[_dilated_spec(d, tm, aw) for d in DILATIONS] * 2 + [
            _row_spec(tm, aw), _row_spec(tm, aw, gate_col), _const_spec(aw), _const_spec(aw), _vmem_spec(),
            _const_spec(D_MODEL), _row_spec(tm, D_MODEL)],
        out_specs=[_row_spec(tm, D_MODEL), _row_spec(tm, D_MODEL), _row_spec(tm, D_MODEL), _row_spec(tm, aw)] + [
            _dilated_spec(d, tm, aw) for d in DILATIONS],
        out_shape=[
            jax.ShapeDtypeStruct((s, D_MODEL), F32),
            jax.ShapeDtypeStruct((s, D_MODEL), BF16),
            jax.ShapeDtypeStruct((s, D_MODEL), F32),
            jax.ShapeDtypeStruct((s, aw), F32),
        ] + [jax.ShapeDtypeStruct((d, s // d, aw), F32) for d in DILATIONS],
        scratch_shapes=[pltpu.VMEM((nd, aw // LANES, tm, LANES), F32), pltpu.VMEM((nd, aw // LANES, tm, LANES), F32),
                        pltpu.VMEM((aw // LANES, tm, LANES), F32)],
        compiler_params=_params(dimension_semantics=("arbitrary",)),
    )(*[p[0] for p in attn_parts], *[p[1] for p in attn_parts], o_h, proj, an, hn, w_out_b, gp, x)


def mix_bwd(dx1, mixed, gp, w_out_b, attn, an, o_h, proj, hn):
    s = dx1.shape[0]
    tm = TOKEN_TILE
    gate_col = 3
    hd = HGRN_HEAD_DIM
    aw = ATTN_WIDTH

    nd = len(DILATIONS)

    def body(*refs):
        dx1_ref, mixed_ref, gp_ref, w_ref, attn_ref, an_ref, oh_ref, gate_ref, hn_ref, dmix_ref = refs[:10]
        do_refs, delta_refs = refs[10:10 + nd], refs[10 + nd:10 + 2 * nd]
        doh_ref, dgate_ref, dgp_ref, dan_ref, dhn_ref, do_ref, delta_ref = refs[10 + 2 * nd:]
        dmixed, gp_c = _rms_bwd(dx1_ref[...], mixed_ref[...], gp_ref[...], D_MODEL)
        _acc_rows(dgp_ref, gp_c)
        dmixed_b = dmixed.astype(BF16)
        dmix_ref[...] = dmixed_b
        dcat = _dot_nt(dmixed_b, w_ref[...])
        attn = attn_ref[...]
        d_o, an_c = _rms_bwd(dcat[:, :aw], attn, an_ref[...], aw)
        _acc_rows(dan_ref, an_c)
        _lane_blocks(do_ref, d_o)
        prod = d_o * attn
        for pair in range(ATTN_HEADS // 2):
            pp = prod[:, pair * LANES:(pair + 1) * LANES]
            low = _lane_half((tm, LANES), 0)
            lo = jnp.sum(jnp.where(low, pp, 0.0), axis=-1, keepdims=True)
            hi = jnp.sum(jnp.where(low, 0.0, pp), axis=-1, keepdims=True)
            delta_ref[pair] = jnp.where(low, lo, hi)
        for d, o_ref, l_ref in zip(DILATIONS, do_refs, delta_refs):
            _to_dilated(do_ref, o_ref, d, tm)
            _to_dilated(delta_ref, l_ref, d, tm)
        gate = gate_ref[...]
        sg = _sigmoid(gate)
        silu_g = gate * sg
        drec = dcat[:, aw:]
        hn_parts = []
        for h in range(HGRN_HEADS):
            hs = slice(h * hd, (h + 1) * hd)
            oh = oh_ref[:, hs]
            on = _rms_fwd(oh, hn_ref[:, hs], hd)
            dgate_ref[:, hs] = drec[:, hs] * on * (sg[:, hs] * (1.0 + gate[:, hs] * (1.0 - sg[:, hs])))
            d_oh, hn_c = _rms_bwd(drec[:, hs] * silu_g[:, hs], oh, hn_ref[:, hs], hd)
            doh_ref[:, hs] = d_oh
            hn_parts.append(hn_c)
        _acc_rows(dhn_ref, jnp.concatenate(hn_parts, axis=1))

    return pl.pallas_call(
        body,
        name="mix_bwd",
        grid=(s // tm,),
        in_specs=[_row_spec(tm, D_MODEL), _row_spec(tm, D_MODEL), _const_spec(D_MODEL), _vmem_spec(), _row_spec(tm, aw),
                  _const_spec(aw), _row_spec(tm, aw), _row_spec(tm, aw, gate_col), _const_spec(aw)],
        out_specs=[_row_spec(tm, D_MODEL)] + [_dilated_spec(d, tm, aw) for d in DILATIONS] * 2 + [_row_spec(tm, aw)] * 2 + [
            _const_spec(D_MODEL), _const_spec(aw), _const_spec(aw)],
        out_shape=[jax.ShapeDtypeStruct((s, D_MODEL), BF16)] + [
            jax.ShapeDtypeStruct((d, s // d, aw), F32) for d in DILATIONS] * 2 + [jax.ShapeDtypeStruct((s, aw), F32)] * 2 + [
            jax.ShapeDtypeStruct((1, D_MODEL), F32), jax.ShapeDtypeStruct((1, aw), F32),
            jax.ShapeDtypeStruct((1, aw), F32)],
        scratch_shapes=[pltpu.VMEM((aw // LANES, tm, LANES), F32), pltpu.VMEM((aw // LANES, tm, LANES), F32)],
        compiler_params=_params(dimension_semantics=("arbitrary",)),
    )(dx1, mixed, gp, w_out_b, attn, an, o_h, proj, hn)


def mlp_fwd_bwd(x1, g_pre, w1_blocks, w2_b, g_post, target):
    s = x1.shape[0]
    tm = TOKEN_TILE
    nblk, _, fb = w1_blocks.shape

    def body(x1_ref, gpre_ref, w1_ref, w2_ref, gpost_ref, t_ref,
             dx1_ref, h2_ref, a_ref, du_ref, dff_ref, loss_ref, dgpre_ref, dgpost_ref, u_ref):
        x1v = x1_ref[...]
        h2 = _rms_fwd(x1v, gpre_ref[...], D_MODEL).astype(BF16)
        h2_ref[...] = h2
        ff = jnp.zeros((tm, D_MODEL), F32)
        for j in range(nblk):
            cols = slice(j * fb, (j + 1) * fb)
            ru = jnp.maximum(_dot(h2, w1_ref[j]), 0.0)
            u_ref[:, cols] = ru
            a = (ru * ru).astype(BF16)
            a_ref[:, cols] = a
            ff = ff + _dot(a, w2_ref[cols, :])
        diff = x1v + _rms_fwd(ff, gpost_ref[...], D_MODEL) - t_ref[...]
        _acc_rows(loss_ref, diff * diff)
        dy = diff * (1.0 / D_MODEL)
        dff, gpost_c = _rms_bwd(dy, ff, gpost_ref[...], D_MODEL)
        _acc_rows(dgpost_ref, gpost_c)
        dff_b = dff.astype(BF16)
        dff_ref[...] = dff_b
        dh2 = jnp.zeros((tm, D_MODEL), F32)
        for j in range(nblk):
            cols = slice(j * fb, (j + 1) * fb)
            du = (_dot_nt(dff_b, w2_ref[cols, :]) * (2.0 * u_ref[:, cols])).astype(BF16)
            du_ref[:, cols] = du
            dh2 = dh2 + _dot_nt(du, w1_ref[j])
        dxa, gpre_c = _rms_bwd(dh2, x1v, gpre_ref[...], D_MODEL)
        _acc_rows(dgpre_ref, gpre_c)
        dx1_ref[...] = dy + dxa

    dm = D_MODEL
    return pl.pallas_call(
        body,
        name="mlp_fwd_bwd",
        grid=(s // tm,),
        in_specs=[_row_spec(tm, dm), _const_spec(dm), _vmem_spec(), _vmem_spec(), _const_spec(dm), _row_spec(tm, dm)],
        out_specs=[_row_spec(tm, dm), _row_spec(tm, dm), _row_spec(tm, D_FF), _row_spec(tm, D_FF), _row_spec(tm, dm),
                   _const_spec(dm), _const_spec(dm), _const_spec(dm)],
        out_shape=[
            jax.ShapeDtypeStruct((s, dm), F32),
            jax.ShapeDtypeStruct((s, dm), BF16),
            jax.ShapeDtypeStruct((s, D_FF), BF16),
            jax.ShapeDtypeStruct((s, D_FF), BF16),
            jax.ShapeDtypeStruct((s, dm), BF16),
            jax.ShapeDtypeStruct((1, dm), F32),
            jax.ShapeDtypeStruct((1, dm), F32),
            jax.ShapeDtypeStruct((1, dm), F32),
        ],
        scratch_shapes=[pltpu.VMEM((tm, D_FF), F32)],
        compiler_params=_params(dimension_semantics=("arbitrary",)),
    )(x1, g_pre, w1_blocks, w2_b, g_post, target)


def in_proj_bwd(attn_grads, hgrn_grads, dgate, w_in_b, x, g1, dx1):
    s = x.shape[0]
    tm = TOKEN_TILE
    aw = ATTN_WIDTH
    n_attn = len(attn_grads)
    flat = [g[k] for k in range(3) for g in attn_grads] + list(hgrn_grads) + [dgate]

    def body(*refs):
        parts = refs[:len(flat)]
        w_ref, x_ref, g_ref, dx1_ref, dx_ref, dproj_ref, dg_ref, scr = refs[len(flat):]
        groups = []
        for k in range(3):
            acc = None
            for p, d in zip(parts[k * n_attn:(k + 1) * n_attn], DILATIONS):
                v = _from_dilated(p, scr, d, tm)
                acc = v if acc is None else acc + v
            groups.append(acc)
        groups += [p[...] for p in parts[3 * n_attn:]]
        dh = jnp.zeros((tm, D_MODEL), F32)
        for gi, grp in enumerate(groups):
            cols = slice(gi * aw, (gi + 1) * aw)
            gb = grp.astype(BF16)
            dproj_ref[:, cols] = gb
            dh = dh + _dot_nt(gb, w_ref[:, cols])
        dxa, g_c = _rms_bwd(dh, x_ref[...], g_ref[...], D_MODEL)
        _acc_rows(dg_ref, g_c)
        dx_ref[...] = dx1_ref[...] + dxa

    dm = D_MODEL
    return pl.pallas_call(
        body,
        name="in_proj_bwd",
        grid=(s // tm,),
        in_specs=[_dilated_spec(d, tm, aw) for d in DILATIONS] * 3 + [_row_spec(tm, aw)] * 4 + [
            _vmem_spec(), _row_spec(tm, dm), _const_spec(dm), _row_spec(tm, dm)],
        out_specs=[_row_spec(tm, dm), _row_spec(tm, IN_PROJ_WIDTH), _const_spec(dm)],
        out_shape=[jax.ShapeDtypeStruct((s, dm), F32), jax.ShapeDtypeStruct((s, IN_PROJ_WIDTH), BF16),
                   jax.ShapeDtypeStruct((1, dm), F32)],
        scratch_shapes=[pltpu.VMEM((aw // LANES, tm, LANES), F32)],
        compiler_params=_params(dimension_semantics=("arbitrary",)),
    )(*flat, w_in_b, x, g1, dx1)


def wgrad(a_b, b_b, tn, name, ts=1024, per_step=1):
    s, k = a_b.shape
    n = b_b.shape[1]

    def body(a_ref, b_ref, o_ref):
        @pl.when(pl.program_id(1) == 0)
        def _():
            o_ref[...] = jnp.zeros_like(o_ref)

        a = a_ref[...]
        for jj in range(per_step):
            o_ref[jj] += _dot_tn(a, b_ref[:, jj * tn:(jj + 1) * tn])

    wide = tn * per_step
    return pl.pallas_call(
        body,
        name=name,
        grid=(n // wide, s // ts),
        in_specs=[pl.BlockSpec((ts, k), lambda j, i: (i, 0)), pl.BlockSpec((ts, wide), lambda j, i: (i, j))],
        out_specs=pl.BlockSpec((per_step, k, tn), lambda j, i: (j, 0, 0)),
        out_shape=jax.ShapeDtypeStruct((n // tn, k, tn), F32),
        compiler_params=_params(dimension_semantics=("arbitrary", "arbitrary")),
    )(a_b, b_b)


def train_step(x, target, g1, an, logits, hn, gp, g_pre, g_post, w, m, v):
    nd = len(DILATIONS)
    shard_b = {k: w[k].astype(BF16) for k in BIG}
    (w_in_g,) = run_exchange(gather_exchange([shard_b["w_in"]]), "gather_w_in")
    w_in_b = w_in_g.transpose(1, 0, 2).reshape(D_MODEL, IN_PROJ_WIDTH)

    proj, h_b, *qkvs = in_proj_fwd(x, g1, w_in_b)
    attn_parts = [attn_fwd(qkv, d) for qkv, d in zip(qkvs, DILATIONS)]
    o_h, states, a_mat, w_out_g, w1_blocks, w2_g = hgrn_fwd(
        proj, logits, ride=gather_exchange([shard_b["w_out"], shard_b["w_ff1"], shard_b["w_ff2"]]))
    w_out_b = w_out_g.reshape(D_MODEL, D_MODEL)
    w2_b = w2_g.reshape(D_FF, D_MODEL)
    x1, cat_b, mixed, attn, *lses = mix_fwd(attn_parts, o_h, proj, an, hn, w_out_b, gp, x)
    dx1, h2_b, a_b, du_b, dff_b, loss_vec, dg_pre, dg_post = mlp_fwd_bwd(x1, g_pre, w1_blocks, w2_b, g_post, target)
    dw2 = wgrad(a_b, dff_b, D_MODEL, "wgrad_ff2", ts=512)
    dw1 = wgrad(h2_b, du_b, D_FF // N_DEV, "wgrad_ff1", per_step=2)
    dmix_b, *rest = mix_bwd(dx1, mixed, gp, w_out_b, attn, an, o_h, proj, hn)
    d_os, deltas = rest[:nd], rest[nd:2 * nd]
    d_oh, dgate, dgp, dan, dhn = rest[2 * nd:]
    dwout = wgrad(cat_b, dmix_b, D_MODEL, "wgrad_out")

    early = ("w_out", "w_ff1", "w_ff2")
    early_grads = [dwout.reshape(N_DEV, D_MODEL // N_DEV, D_MODEL), dw1, dw2.reshape(N_DEV, D_FF // N_DEV, D_MODEL)]
    attn_grads = []
    for k, d in enumerate(DILATIONS):
        ride = to_core_exchange(early_grads) if k == 0 else None
        res = attn_bwd(qkvs[k], d_os[k], lses[k], deltas[k], d, ride=ride)
        attn_grads.append(res[:3])
        if k == 0:
            pairs = [pair_sum(g, s, f"pair_sum_{name}") for g, s, name in zip(early_grads, res[3:], early)]
    dq_h, df_h, di_h, dlb, *others = hgrn_bwd(proj, logits, d_oh, states, a_mat, ride=to_chip_exchange(pairs))
    dx, dproj_b, dg1 = in_proj_bwd(attn_grads, (dq_h, df_h, di_h), dgate, w_in_b, x, g1, dx1)
    dwin = wgrad(h_b, dproj_b, 2 * IN_PROJ_WIDTH // N_DEV, "wgrad_in")
    big = {name: sum_adamw(p, o, w[name], m[name], v[name], f"sum_adamw_{name}")
           for name, p, o in zip(early, pairs, others)}

    shard_w = IN_PROJ_WIDTH // N_DEV
    dwin_blocks = dwin.reshape(N_DEV // 2, D_MODEL, 2, shard_w).transpose(0, 2, 1, 3).reshape(N_DEV, D_MODEL, shard_w)
    (from_sibling,) = run_exchange(to_core_exchange([dwin_blocks]), "reduce_w_in_to_core")
    pair_in = pair_sum(dwin_blocks, from_sibling, "pair_sum_w_in")
    (others_in,) = run_exchange(to_chip_exchange([pair_in]), "reduce_w_in_to_chip")
    big["w_in"] = sum_adamw(pair_in, others_in, w["w_in"], m["w_in"], v["w_in"], "sum_adamw_w_in")
    small = dict(dg1=dg1, dan=dan, dlb=dlb, dhn=dhn, dgp=dgp, dg_pre=dg_pre, dg_post=dg_post, loss_vec=loss_vec)
    return dx, big, small


def _position():
    x, y, c = lax.axis_index("x"), lax.axis_index("y"), lax.axis_index("c")
    other_chips = [(1 - x, y), (x, 1 - y), (1 - x, 1 - y)]
    return x, y, c, other_chips


def _any_spec():
    return pl.BlockSpec(memory_space=pl.ANY)


class Exchange:
    def __init__(self, arrays, out_shape, sems, stages):
        self.arrays, self.out_shape, self.sems, self.stages = list(arrays), list(out_shape), list(sems), stages


def gather_exchange(shards):
    n = len(shards)

    def stages(ins, outs, sems):
        send_sems, recv_sems, local_sems = sems

        def parts():
            x, y, c, chips = _position()
            me, sibling = (x, y, c), (x, y, 1 - c)

            def slot(a, px, py, pc):
                return outs[a].at[4 * px + 2 * py + pc]

            def copy(a, k, block, to, src=None):
                return pltpu.make_async_remote_copy(
                    src_ref=slot(a, *block) if src is None else src, dst_ref=slot(a, *block),
                    send_sem=send_sems.at[a, k], recv_sem=recv_sems.at[a, k], device_id=to, device_id_type=MESH)

            local = [pltpu.make_async_copy(ins[a], slot(a, *me), local_sems.at[a]) for a in range(n)]
            first = []
            for a in range(n):
                first.append(copy(a, 0, me, sibling, src=ins[a]))
                first += [copy(a, 1 + j, me, (*chip, c), src=ins[a]) for j, chip in enumerate(chips)]
            passed = [copy(a, 4 + j, (*chip, c), sibling) for j, chip in enumerate(chips) for a in range(n)]
            return c, chips, me, sibling, copy, local, first, passed

        def begin():
            _, _, _, _, _, local, first, _ = parts()
            for cp in local + first:
                cp.start()

        def middle():
            c, chips, me, _, copy, _, _, passed = parts()
            k = 0
            for j, chip in enumerate(chips):
                for a in range(n):
                    copy(a, 1 + j, (*chip, c), me).wait_recv()
                    passed[k].start()
                    k += 1

        def end():
            c, chips, me, sibling, copy, local, first, passed = parts()
            for a in range(n):
                copy(a, 0, sibling, me).wait_recv()
                for j, chip in enumerate(chips):
                    copy(a, 4 + j, (*chip, 1 - c), me).wait_recv()
            for cp in first + passed:
                cp.wait_send()
            for cp in local:
                cp.wait()

        return begin, middle, end

    return Exchange(
        shards, [jax.ShapeDtypeStruct((N_DEV,) + sh.shape, sh.dtype) for sh in shards],
        [pltpu.SemaphoreType.DMA((n, 7)), pltpu.SemaphoreType.DMA((n, 7)), pltpu.SemaphoreType.DMA((n,))], stages)


def to_core_exchange(grads):
    n = len(grads)

    def stages(ins, outs, sems):
        send_sems, recv_sems = sems

        def copies():
            x, y, c, _ = _position()
            return [pltpu.make_async_remote_copy(
                src_ref=ins[a].at[2 * q + (1 - c)], dst_ref=outs[a].at[q], send_sem=send_sems.at[a, q],
                recv_sem=recv_sems.at[a, q], device_id=(x, y, 1 - c), device_id_type=MESH)
                for a in range(n) for q in range(4)]

        def begin():
            for cp in copies():
                cp.start()

        def end():
            for cp in copies():
                cp.wait()

        return begin, None, end

    return Exchange(grads, [jax.ShapeDtypeStruct((4,) + g.shape[1:], g.dtype) for g in grads],
                    [pltpu.SemaphoreType.DMA((n, 4)), pltpu.SemaphoreType.DMA((n, 4))], stages)


def pair_sum(grad, from_sibling, name):
    _, r, cdim = grad.shape
    tr = min(r, 256)
    c_idx = lax.axis_index("c").astype(jnp.int32).reshape(1)

    def body(c_ref, g_ref, s_ref, o_ref):
        o_ref[...] = g_ref[...] + s_ref[...]

    return pl.pallas_call(
        body,
        name=name,
        grid_spec=pltpu.PrefetchScalarGridSpec(
            num_scalar_prefetch=1,
            grid=(4, r // tr),
            in_specs=[pl.BlockSpec((1, tr, cdim), lambda q, i, cr: (2 * q + cr[0], i, 0)),
                      pl.BlockSpec((1, tr, cdim), lambda q, i, cr: (q, i, 0))],
            out_specs=pl.BlockSpec((1, tr, cdim), lambda q, i, cr: (q, i, 0)),
        ),
        out_shape=jax.ShapeDtypeStruct((4, r, cdim), grad.dtype),
        compiler_params=_params(dimension_semantics=("arbitrary", "arbitrary")),
    )(c_idx, grad, from_sibling)


def to_chip_exchange(pairs):
    n = len(pairs)

    def stages(ins, outs, sems):
        send_sems, recv_sems = sems

        def copies():
            x, y, c, chips = _position()
            return [pltpu.make_async_remote_copy(
                src_ref=ins[a].at[2 * px + py], dst_ref=outs[a].at[j], send_sem=send_sems.at[a, j],
                recv_sem=recv_sems.at[a, j], device_id=(px, py, c), device_id_type=MESH)
                for a in range(n) for j, (px, py) in enumerate(chips)]

        def begin():
            for cp in copies():
                cp.start()

        def end():
            for cp in copies():
                cp.wait()

        return begin, None, end

    return Exchange(pairs, [jax.ShapeDtypeStruct((3,) + p.shape[1:], p.dtype) for p in pairs],
                    [pltpu.SemaphoreType.DMA((n, 3)), pltpu.SemaphoreType.DMA((n, 3))], stages)


def run_exchange(ex, name):
    n_in, n_out = len(ex.arrays), len(ex.out_shape)

    def body(*refs):
        begin, middle, end = ex.stages(refs[:n_in], refs[n_in:n_in + n_out], refs[n_in + n_out:])
        begin()
        if middle is not None:
            middle()
        end()

    return pl.pallas_call(
        body,
        name=name,
        in_specs=[_any_spec()] * n_in,
        out_specs=[_any_spec()] * n_out,
        out_shape=ex.out_shape,
        scratch_shapes=ex.sems,
    )(*ex.arrays)


def _riding(body, n_in, n_out, n_scratch, ex, first, middle, last):
    if ex is None:
        return body
    r_in, r_out = len(ex.arrays), len(ex.out_shape)

    def wrapped(*refs):
        k_in, refs = refs[:n_in], refs[n_in:]
        e_in, refs = refs[:r_in], refs[r_in:]
        k_out, refs = refs[:n_out], refs[n_out:]
        e_out, refs = refs[:r_out], refs[r_out:]
        k_scr, e_sems = refs[:n_scratch], refs[n_scratch:]
        begin, mid, end = ex.stages(e_in, e_out, e_sems)
        pl.when(first())(begin)
        body(*k_in, *k_out, *k_scr)
        if mid is not None:
            pl.when(middle())(mid)
        pl.when(last())(end)

    return wrapped


def _ride_specs(ex):
    if ex is None:
        return [], [], [], [], []
    return [_any_spec()] * len(ex.arrays), [_any_spec()] * len(ex.out_shape), ex.out_shape, ex.sems, ex.arrays


def _adamw(w, g, m, v):
    m = ADAM_B1 * m + (1.0 - ADAM_B1) * g
    v = ADAM_B2 * v + (1.0 - ADAM_B2) * (g * g)
    m_hat = m / (1.0 - ADAM_B1 ** ADAM_STEP)
    v_hat = v / (1.0 - ADAM_B2 ** ADAM_STEP)
    delta = -ADAM_LR * (m_hat / (jnp.sqrt(v_hat) + ADAM_EPS) + ADAM_WD * w)
    return delta, m, v


def sum_adamw(pairs, others, w, m, v, name):
    r, cdim = w.shape
    tr = min(r, 256)
    chip_idx = (2 * lax.axis_index("x") + lax.axis_index("y")).astype(jnp.int32).reshape(1)

    def body(q_ref, p_ref, o_ref, w_ref, m_ref, v_ref, g_out, d_out, m_out, v_out):
        g = p_ref[0] + o_ref[0] + o_ref[1] + o_ref[2]
        g_out[...] = g
        d_out[...], m_out[...], v_out[...] = _adamw(w_ref[...], g, m_ref[...], v_ref[...])

    tile = lambda: pl.BlockSpec((tr, cdim), lambda i, qr: (i, 0))
    return pl.pallas_call(
        body,
        name=name,
        grid_spec=pltpu.PrefetchScalarGridSpec(
            num_scalar_prefetch=1,
            grid=(r // tr,),
            in_specs=[pl.BlockSpec((1, tr, cdim), lambda i, qr: (qr[0], i, 0)),
                      pl.BlockSpec((3, tr, cdim), lambda i, qr: (0, i, 0)), tile(), tile(), tile()],
            out_specs=[tile(), tile(), tile(), tile()],
        ),
        out_shape=[jax.ShapeDtypeStruct((r, cdim), F32)] * 4,
        compiler_params=_params(dimension_semantics=("arbitrary",)),
    )(chip_idx, pairs, others, w, m, v)


SMALL_ROWS = 8


def small_all_reduce(packed):
    shape = packed.shape

    def body(in_ref, out_ref, recv_ref, send_sems, recv_sems):
        x, y, c, _ = _position()
        my_id = 4 * x + 2 * y + c
        recv_ref[my_id] = in_ref[...]
        copies = []
        for rel in range(1, N_DEV):
            fx, fy, fc = (rel >> 2) & 1, (rel >> 1) & 1, rel & 1
            px = 1 - x if fx else x
            py = 1 - y if fy else y
            pc = 1 - c if fc else c
            cp = pltpu.make_async_remote_copy(
                src_ref=in_ref, dst_ref=recv_ref.at[my_id], send_sem=send_sems.at[rel - 1],
                recv_sem=recv_sems.at[rel - 1], device_id=(px, py, pc), device_id_type=MESH)
            cp.start()
            copies.append((cp, pltpu.make_async_remote_copy(
                src_ref=in_ref, dst_ref=recv_ref.at[4 * px + 2 * py + pc], send_sem=send_sems.at[rel - 1],
                recv_sem=recv_sems.at[rel - 1], device_id=(px, py, pc), device_id_type=MESH)))
        for cp, landing in copies:
            landing.wait_recv()
        for cp, landing in copies:
            cp.wait_send()
        total = recv_ref[0]
        for k in range(1, N_DEV):
            total = total + recv_ref[k]
        out_ref[...] = total

    return pl.pallas_call(
        body,
        name="small_all_reduce",
        in_specs=[_vmem_spec()],
        out_specs=_vmem_spec(),
        out_shape=jax.ShapeDtypeStruct(shape, F32),
        scratch_shapes=[pltpu.VMEM((N_DEV,) + shape, F32), pltpu.SemaphoreType.DMA((N_DEV - 1,)),
                        pltpu.SemaphoreType.DMA((N_DEV - 1,))],
    )(packed)


def small_adamw(reduced, w, m, v):
    def body(r_ref, w_ref, m_ref, v_ref, g_out, d_out, m_out, v_out, loss_out):
        red = r_ref[...]
        wv = w_ref[...]
        lb = _lower_bound(jnp.concatenate([wv[5:6, :HGRN_WIDTH], wv[5:6, HGRN_WIDTH:]], axis=0))
        t = red[5:6, :HGRN_WIDTH] * lb * (1.0 - lb)
        row = lax.broadcasted_iota(jnp.int32, red.shape, 0)
        g = jnp.where(row == 5, jnp.concatenate([t, -t], axis=1), jnp.where(row >= 6, 0.0, red))
        g_out[...] = g
        d_out[...], m_out[...], v_out[...] = _adamw(wv, g, m_ref[...], v_ref[...])
        loss = jnp.sum(red[6:7, :], axis=-1, keepdims=True) * (0.5 / D_MODEL)
        loss_out[...] = jnp.broadcast_to(loss, loss_out.shape)

    return pl.pallas_call(
        body,
        name="small_adamw",
        in_specs=[_vmem_spec()] * 4,
        out_specs=[_vmem_spec()] * 5,
        out_shape=[jax.ShapeDtypeStruct(reduced.shape, F32)] * 4 + [jax.ShapeDtypeStruct((8, 128), F32)],
    )(reduced, w, m, v)


def _pack_small(g1, gp, g_pre, g_post, an, hn, logits_or_dlb, extra=None):
    row5 = logits_or_dlb.reshape(1, -1)
    row5 = jnp.pad(row5, ((0, 0), (0, D_MODEL - row5.shape[1])))
    row6 = jnp.zeros((1, D_MODEL), F32) if extra is None else extra
    return jnp.concatenate([g1, gp, g_pre, g_post, jnp.concatenate([an, hn], axis=1), row5, row6,
                            jnp.zeros((1, D_MODEL), F32)], axis=0)


def _unpack_small(p):
    return dict(mix_pre_norm=p[0:1], mix_post_norm=p[1:2], mlp_pre_norm=p[2:3], mlp_post_norm=p[3:4],
                attn_out_norm=p[4:5, :ATTN_WIDTH], hgrn_out_norm=p[4:5, ATTN_WIDTH:],
                hgrn_lb_logits=p[5].reshape(2, HGRN_WIDTH))


BIG = ("w_in", "w_out", "w_ff1", "w_ff2")
ORDER = ("mix_pre_norm", "w_in", "attn_out_norm", "hgrn_lb_logits", "hgrn_out_norm", "w_out", "mix_post_norm",
         "mlp_pre_norm", "w_ff1", "w_ff2", "mlp_post_norm")


def kernel(x, mix_pre_norm, w_in, attn_out_norm, hgrn_lb_logits, hgrn_out_norm, w_out, mix_post_norm, mlp_pre_norm, w_ff1, w_ff2, mlp_post_norm, loss_target, m_mix_pre_norm, m_w_in, m_attn_out_norm, m_hgrn_lb_logits, m_hgrn_out_norm, m_w_out, m_mix_post_norm, m_mlp_pre_norm, m_w_ff1, m_w_ff2, m_mlp_post_norm, v_mix_pre_norm, v_w_in, v_attn_out_norm, v_hgrn_lb_logits, v_hgrn_out_norm, v_w_out, v_mix_post_norm, v_mlp_pre_norm, v_w_ff1, v_w_ff2, v_mlp_post_norm):
    w = dict(w_in=w_in[0], w_out=w_out[0], w_ff1=w_ff1[0], w_ff2=w_ff2[0])
    m = dict(w_in=m_w_in[0], w_out=m_w_out[0], w_ff1=m_w_ff1[0], w_ff2=m_w_ff2[0])
    v = dict(w_in=v_w_in[0], w_out=v_w_out[0], w_ff1=v_w_ff1[0], w_ff2=v_w_ff2[0])

    dx, big, small = train_step(x[0], loss_target[0], mix_pre_norm, attn_out_norm, hgrn_lb_logits, hgrn_out_norm,
                                mix_post_norm, mlp_pre_norm, mlp_post_norm, w, m, v)

    packed_g = _pack_small(small["dg1"], small["dgp"], small["dg_pre"], small["dg_post"], small["dan"], small["dhn"],
                           small["dlb"], small["loss_vec"])
    reduced = small_all_reduce(packed_g)
    pack = lambda a, b, c2, d, e, f, g: _pack_small(a, b, c2, d, e, f, g)
    w_s = pack(mix_pre_norm, mix_post_norm, mlp_pre_norm, mlp_post_norm, attn_out_norm, hgrn_out_norm, hgrn_lb_logits)
    m_s = pack(m_mix_pre_norm, m_mix_post_norm, m_mlp_pre_norm, m_mlp_post_norm, m_attn_out_norm, m_hgrn_out_norm,
               m_hgrn_lb_logits)
    v_s = pack(v_mix_pre_norm, v_mix_post_norm, v_mlp_pre_norm, v_mlp_post_norm, v_attn_out_norm, v_hgrn_out_norm,
               v_hgrn_lb_logits)
    g_s, d_s, nm_s, nv_s, loss = small_adamw(reduced, w_s, m_s, v_s)
    small_out = [_unpack_small(t) for t in (g_s, d_s, nm_s, nv_s)]

    outs = [loss[0, 0], dx[None]]
    for kind in range(4):
        for name in ORDER:
            outs.append(big[name][kind][None] if name in BIG else small_out[kind][name])
    return tuple(outs)
```

```python
import functools
import math

import jax
import jax.numpy as jnp
from jax import lax
from jax.experimental import pallas as pl
from jax.experimental.pallas import tpu as pltpu

F32 = jnp.float32
BF16 = jnp.bfloat16

D_MODEL = 1024
SEQ = 4096
ATTN_WIDTH = 512
ATTN_HEAD_DIM = 64
ATTN_HEADS = 8
ATTN_BLOCK = 128
DILATIONS = (1, 4, 16)
HGRN_WIDTH = 512
HGRN_HEADS = 4
HGRN_HEAD_DIM = 128
HGRN_CHUNK = 64
IN_PROJ_WIDTH = 3584
D_FF = 4096
RMS_EPS = 1e-6
N_DEV = 8
ADAM_LR = 0.001
ADAM_B1 = 0.9
ADAM_B2 = 0.999
ADAM_EPS = 1e-08
ADAM_WD = 0.01
ADAM_STEP = 10

SUBLANES = 8
LANES = 128
COLUMN_UNROLL = 4
SUB_BLOCK = 16
TOKEN_TILE = 256
VMEM_LIMIT = 56 * 1024 * 1024
NEG_BIG = -1e30
MESH = pl.DeviceIdType.MESH


def _params(**kw):
    return pltpu.CompilerParams(vmem_limit_bytes=VMEM_LIMIT, **kw)


def _vmem_spec():
    return pl.BlockSpec(memory_space=pltpu.VMEM)


def _dot(a, b):
    return jnp.dot(a, b, preferred_element_type=F32)


def _dot_nt(a, b):
    return lax.dot_general(a, b, (((1,), (1,)), ((), ())), preferred_element_type=F32)


def _dot_tn(a, b):
    return lax.dot_general(a, b, (((0,), (0,)), ((), ())), preferred_element_type=F32)


def _sigmoid(x):
    return 1.0 / (1.0 + jnp.exp(-x))


def _rms_fwd(x, gain, width):
    r = lax.rsqrt(jnp.sum(x * x, axis=-1, keepdims=True) * (1.0 / width) + RMS_EPS)
    return x * r * gain


def _rms_bwd(dy, x, gain, width):
    r = lax.rsqrt(jnp.sum(x * x, axis=-1, keepdims=True) * (1.0 / width) + RMS_EPS)
    xhat = x * r
    dxhat = dy * gain
    dx = r * (dxhat - xhat * (jnp.sum(dxhat * xhat, axis=-1, keepdims=True) * (1.0 / width)))
    return dx, dy * xhat


def _split3(x):
    hi = x.astype(BF16)
    r1 = x - hi.astype(F32)
    mid = r1.astype(BF16)
    lo = (r1 - mid.astype(F32)).astype(BF16)
    return hi, mid, lo


def _tri_sum(tri_bf16, x):
    hi, mid, lo = _split3(x)
    return _dot(tri_bf16, hi) + _dot(tri_bf16, mid) + _dot(tri_bf16, lo)


def _dilated_spec(d, tm, width):
    return pl.BlockSpec((d, tm // d, width), lambda i: (0, i, 0))


def _lane_blocks(ref, value):
    for c in range(ref.shape[0]):
        ref[c] = value[:, c * LANES:(c + 1) * LANES]


def _to_dilated(src_ref, dst_ref, d, tm, cast=None):
    for r in range(d):
        for c in range(src_ref.shape[0]):
            v = src_ref[c] if d == 1 else src_ref[c, pl.ds(r, tm // d, stride=d), :]
            dst_ref[r, :, c * LANES:(c + 1) * LANES] = v if cast is None else v.astype(cast)


def _from_dilated(src_ref, scratch_ref, d, tm):
    if d == 1:
        return src_ref[0]
    nblk = scratch_ref.shape[0]
    for r in range(d):
        for c in range(nblk):
            scratch_ref[c, pl.ds(r, tm // d, stride=d), :] = src_ref[r, :, c * LANES:(c + 1) * LANES]
    return jnp.concatenate([scratch_ref[c] for c in range(nblk)], axis=1)


def in_proj_fwd(x, g1, w_in_b):
    s = x.shape[0]
    tm = TOKEN_TILE
    qkv_w = 3 * ATTN_WIDTH
    hg_w = IN_PROJ_WIDTH - qkv_w

    def body(x_ref, g_ref, w_ref, hg_ref, h_ref, *rest):
        qkv_refs, qkv_scr = rest[:len(DILATIONS)], rest[len(DILATIONS)]
        h = _rms_fwd(x_ref[...], g_ref[...], D_MODEL).astype(BF16)
        h_ref[...] = h
        proj = _dot(h, w_ref[...])
        hg_ref[...] = proj[:, qkv_w:]
        _lane_blocks(qkv_scr, proj[:, :qkv_w])
        for d, ref in zip(DILATIONS, qkv_refs):
            _to_dilated(qkv_scr, ref, d, tm, cast=BF16)

    return pl.pallas_call(
        body,
        name="in_proj_fwd",
        grid=(s // tm,),
        in_specs=[
            pl.BlockSpec((tm, D_MODEL), lambda i: (i, 0)),
            pl.BlockSpec((1, D_MODEL), lambda i: (0, 0)),
            _vmem_spec(),
        ],
        out_specs=[
            pl.BlockSpec((tm, hg_w), lambda i: (i, 0)),
            pl.BlockSpec((tm, D_MODEL), lambda i: (i, 0)),
        ] + [_dilated_spec(d, tm, qkv_w) for d in DILATIONS],
        out_shape=[jax.ShapeDtypeStruct((s, hg_w), F32), jax.ShapeDtypeStruct((s, D_MODEL), BF16)] + [
            jax.ShapeDtypeStruct((d, s // d, qkv_w), BF16) for d in DILATIONS],
        scratch_shapes=[pltpu.VMEM((qkv_w // LANES, tm, LANES), F32)],
        compiler_params=_params(dimension_semantics=("arbitrary",)),
    )(x, g1, w_in_b)


def _attn_scores(qm, kcat, head, dilation, first_block):
    s = _dot_nt(qm, kcat) * (ATTN_HEAD_DIM ** -0.5)
    qi = lax.broadcasted_iota(jnp.int32, (ATTN_BLOCK, 2 * ATTN_BLOCK), 0)
    kj = lax.broadcasted_iota(jnp.int32, (ATTN_BLOCK, 2 * ATTN_BLOCK), 1)
    dist = qi + ATTN_BLOCK - kj
    valid = (dist >= 0) & (dist <= ATTN_BLOCK) & ((kj >= ATTN_BLOCK) | jnp.logical_not(first_block))
    slope = 2.0 ** (-8.0 * (head + 1) / ATTN_HEADS)
    bias = dist.astype(F32) * (-slope * dilation)
    return jnp.where(valid, s + bias, NEG_BIG)


def _lane_half(shape, sub):
    lane = lax.broadcasted_iota(jnp.int32, shape, 1)
    return (lane < ATTN_HEAD_DIM) if sub == 0 else (lane >= ATTN_HEAD_DIM)


def _sub_block(col, row):
    return pl.BlockSpec((None, ATTN_BLOCK, ATTN_WIDTH), lambda r, n: (r, row(n), col))


def attn_fwd(qkv, dilation):
    d, length, _ = qkv.shape
    assert d == dilation
    nb = length // ATTN_BLOCK

    def body(q_ref, kc_ref, kp_ref, vc_ref, vp_ref, o_ref, lse_ref):
        first = pl.program_id(1) == 0
        for pair in range(ATTN_HEADS // 2):
            lanes = slice(pair * 128, (pair + 1) * 128)
            q2 = q_ref[:, lanes]
            kcat = jnp.concatenate([kp_ref[:, lanes], kc_ref[:, lanes]], axis=0)
            vcat = jnp.concatenate([vp_ref[:, lanes], vc_ref[:, lanes]], axis=0)
            o_pair = jnp.zeros((ATTN_BLOCK, 128), F32)
            lse_pair = jnp.zeros((ATTN_BLOCK, 128), F32)
            for sub in range(2):
                keep = _lane_half((ATTN_BLOCK, 128), sub)
                qm = jnp.where(keep, q2, jnp.zeros_like(q2))
                sc = _attn_scores(qm, kcat, 2 * pair + sub, d, first)
                m = jnp.max(sc, axis=-1, keepdims=True)
                p = jnp.exp(sc - m)
                den = jnp.sum(p, axis=-1, keepdims=True)
                o = _dot(p.astype(BF16), vcat) / den
                o_pair = jnp.where(keep, o, o_pair)
                lse_pair = jnp.where(keep, m + jnp.log(den), lse_pair)
            o_ref[:, lanes] = o_pair
            lse_ref[:, lanes] = lse_pair

    cur = lambda n: n
    prev = lambda n: jnp.maximum(n - 1, 0)
    return pl.pallas_call(
        body,
        name=f"attn_fwd_d{d}",
        grid=(d, nb),
        in_specs=[_sub_block(0, cur), _sub_block(1, cur), _sub_block(1, prev), _sub_block(2, cur), _sub_block(2, prev)],
        out_specs=[_sub_block(0, cur), _sub_block(0, cur)],
        out_shape=[jax.ShapeDtypeStruct((d, length, ATTN_WIDTH), F32)] * 2,
        compiler_params=_params(dimension_semantics=("arbitrary", "arbitrary")),
    )(qkv, qkv, qkv, qkv, qkv)


def attn_bwd(qkv, d_out, lse, delta, dilation, ride=None):
    d, length, _ = qkv.shape
    assert d == dilation
    nb = length // ATTN_BLOCK

    steps = d * nb + 1

    def body(q_ref, kc_ref, kp_ref, vc_ref, vp_ref, do_ref, lse_ref, dl_ref, dq_ref, dk_ref, dv_ref, ck_ref, cv_ref):
        t = pl.program_id(0)

        @pl.when(t == 0)
        def _():
            ck_ref[...] = jnp.zeros_like(ck_ref)
            cv_ref[...] = jnp.zeros_like(cv_ref)

        @pl.when(t < steps - 1)
        def _():
            first = t % nb == 0
            for pair in range(ATTN_HEADS // 2):
                lanes = slice(pair * 128, (pair + 1) * 128)
                q2 = q_ref[:, lanes]
                do2 = do_ref[:, lanes]
                kcat = jnp.concatenate([kp_ref[:, lanes], kc_ref[:, lanes]], axis=0)
                vcat = jnp.concatenate([vp_ref[:, lanes], vc_ref[:, lanes]], axis=0)
                dq_pair = jnp.zeros((ATTN_BLOCK, 128), F32)
                dk_cat = jnp.zeros((2 * ATTN_BLOCK, 128), F32)
                dv_cat = jnp.zeros((2 * ATTN_BLOCK, 128), F32)
                for sub in range(2):
                    keep = _lane_half((ATTN_BLOCK, 128), sub)
                    col = pair * 128 + sub * ATTN_HEAD_DIM
                    qm = jnp.where(keep, q2, jnp.zeros_like(q2))
                    dom = jnp.where(keep, do2, 0.0).astype(BF16)
                    sc = _attn_scores(qm, kcat, 2 * pair + sub, d, first)
                    p = jnp.exp(sc - lse_ref[:, col:col + 1])
                    dp = _dot_nt(dom, vcat)
                    ds = (p * (dp - dl_ref[:, col:col + 1]) * (ATTN_HEAD_DIM ** -0.5)).astype(BF16)
                    dq_pair = jnp.where(keep, _dot(ds, kcat), dq_pair)
                    dk_cat = dk_cat + _dot_tn(ds, qm)
                    dv_cat = dv_cat + _dot_tn(p.astype(BF16), dom)
                dq_ref[:, lanes] = dq_pair
                dk_ref[:, lanes] = ck_ref[:, lanes] + dk_cat[:ATTN_BLOCK]
                dv_ref[:, lanes] = cv_ref[:, lanes] + dv_cat[:ATTN_BLOCK]
                ck_ref[:, lanes] = dk_cat[ATTN_BLOCK:]
                cv_ref[:, lanes] = dv_cat[ATTN_BLOCK:]

        @pl.when(t == steps - 1)
        def _():
            dk_ref[...] = ck_ref[...]
            dv_ref[...] = cv_ref[...]

    blk = (ATTN_BLOCK, ATTN_WIDTH)

    def spec(col, shift):
        def index(t):
            f = jnp.minimum(t, steps - 2) if shift > -2 else jnp.maximum(t - 1, 0)
            r, n = f // nb, f % nb
            return (r, jnp.maximum(n - 1, 0) if shift == -1 else n, col)
        return pl.BlockSpec((None, ATTN_BLOCK, ATTN_WIDTH), index)

    step = lambda k: (lambda: pl.program_id(0) == k)
    e_in, e_out, e_shape, e_scr, e_args = _ride_specs(ride)
    return pl.pallas_call(
        _riding(body, 8, 3, 2, ride, step(0), step(steps // 2), step(steps - 1)),
        name=f"attn_bwd_d{d}",
        grid=(steps,),
        in_specs=[spec(0, 0), spec(1, 0), spec(1, -1), spec(2, 0), spec(2, -1), spec(0, 0), spec(0, 0), spec(0, 0)] + e_in,
        out_specs=[spec(0, 0), spec(0, -2), spec(0, -2)] + e_out,
        out_shape=[jax.ShapeDtypeStruct((d, length, ATTN_WIDTH), F32)] * 3 + e_shape,
        scratch_shapes=[pltpu.VMEM(blk, F32), pltpu.VMEM(blk, F32)] + e_scr,
        compiler_params=_params(dimension_semantics=("arbitrary",)),
    )(qkv, qkv, qkv, qkv, qkv, d_out, lse, delta, *e_args)


def _lower_bound(logits):
    return _sigmoid(logits[0:1, :] - logits[1:2, :])


def _hgrn_gates(q, fp, lb):
    sq = _sigmoid(q)
    qf = q * sq
    sig = _sigmoid(fp)
    f = lb + (1.0 - lb) * sig
    kf = (1.0 - lb) * _sigmoid(-fp)
    return sq, qf, sig, f, kf


def _tril_bf16(n, upper=False):
    r = lax.broadcasted_iota(jnp.int32, (n, n), 0)
    c = lax.broadcasted_iota(jnp.int32, (n, n), 1)
    keep = (c >= r) if upper else (c <= r)
    return jnp.where(keep, 1.0, 0.0).astype(BF16)


def _hgrn_diagonal_loops(c_len, diagonal):
    for half in range(SUB_BLOCK // SUBLANES):
        def step(jj, carry, half=half):
            j = half * SUBLANES + jj
            for i in range(c_len // SUB_BLOCK):
                diagonal(slice(i * SUB_BLOCK + half * SUBLANES, (i + 1) * SUB_BLOCK), j, i * SUB_BLOCK + j)
            return carry

        lax.fori_loop(0, SUBLANES, step, 0, unroll=COLUMN_UNROLL)


def _hgrn_off_diagonal(b, qf, kf):
    c_len, width = b.shape
    edges = [b[0:1, :]] + [b[i * SUB_BLOCK - 1:i * SUB_BLOCK, :] for i in range(1, c_len // SUB_BLOCK)]
    eq = jnp.exp(b - jnp.concatenate([jnp.broadcast_to(e, (SUB_BLOCK, width)) for e in edges], axis=0))
    q_til = qf * eq
    k_til, ek = [], []
    for i in range(1, c_len // SUB_BLOCK):
        n = i * SUB_BLOCK
        e = jnp.exp(edges[i] - b[:n, :])
        ek.append(e)
        k_til.append(jnp.concatenate([kf[:n, :] * e, jnp.zeros((2 * c_len - n, width), F32)], axis=0))
    return q_til, k_til, eq, ek


def _split2(x):
    hi = x.astype(BF16)
    return hi, (x - hi.astype(F32)).astype(BF16)


def hgrn_fwd(proj, lb, ride=None):
    s = proj.shape[0]
    c_len, nh, hd = HGRN_CHUNK, HGRN_HEADS, HGRN_HEAD_DIM
    n_chunks = s // c_len
    col0 = 0

    def body(q_ref, f_ref, i_ref, lb_ref, o_ref, st_out_ref, a_out_ref, st_ref, b_ref, qf_ref, kf_ref, a_ref):
        @pl.when(pl.program_id(0) == 0)
        def _():
            st_ref[...] = jnp.zeros_like(st_ref)

        lbv = _lower_bound(lb_ref[...])
        _, qf, _, f, kf = _hgrn_gates(q_ref[...], f_ref[...], lbv)
        b = _tri_sum(_tril_bf16(c_len), jnp.log(f))
        b_ref[...] = b
        qf_ref[...] = qf
        kf_ref[...] = kf
        a_ref[...] = jnp.zeros_like(a_ref)

        def diagonal(rows, j, key):
            bj = b_ref[pl.ds(key, 1), :]
            kj = kf_ref[pl.ds(key, 1), :]
            nrow = rows.stop - rows.start
            t_loc = lax.broadcasted_iota(jnp.int32, (nrow, nh * hd), 0) + (rows.start % SUB_BLOCK)
            e = jnp.exp(jnp.where(t_loc >= j, b_ref[rows, :] - bj, NEG_BIG))
            prod = qf_ref[rows, :] * kj * e
            lane = lax.broadcasted_iota(jnp.int32, (nrow, hd), 1)
            for h in range(nh):
                col = jnp.sum(prod[:, h * hd:(h + 1) * hd], axis=-1, keepdims=True)
                a_ref[h, rows, :] = jnp.where(lane == key, col, a_ref[h, rows, :])

        _hgrn_diagonal_loops(c_len, diagonal)
        q_til, k_til, _, _ = _hgrn_off_diagonal(b, qf, kf)
        q_til = q_til.astype(BF16)
        k_til = [k.astype(BF16) for k in k_til]

        b_last = b[c_len - 1:c_len, :]
        qb = (qf * jnp.exp(b)).astype(BF16)
        kb2 = (kf * jnp.exp(b_last - b)).astype(BF16)
        vf = i_ref[...].astype(BF16)
        for h in range(nh):
            hs = slice(h * hd, (h + 1) * hd)
            st = st_ref[h]
            st_out_ref[0, h] = st
            off = [jnp.zeros((SUB_BLOCK, hd), F32)]
            for i in range(1, c_len // SUB_BLOCK):
                off.append(_dot_nt(q_til[i * SUB_BLOCK:(i + 1) * SUB_BLOCK, hs], k_til[i - 1][:, hs]))
            a_h = a_ref[h] + jnp.concatenate(off, axis=0)
            a_out_ref[:, hs] = a_h
            o_ref[:, hs] = _dot_nt(qb[:, hs], st.astype(BF16)) + _dot(a_h[:, :c_len].astype(BF16), vf[:, hs])
            st_ref[h] = st * jnp.exp(b_last[:, hs]) + _dot_tn(vf[:, hs], kb2[:, hs])

    blk = (c_len, HGRN_WIDTH)
    step = lambda k: (lambda: pl.program_id(0) == k)
    e_in, e_out, e_shape, e_scr, e_args = _ride_specs(ride)
    return pl.pallas_call(
        _riding(body, 4, 3, 5, ride, step(0), step((7 * n_chunks) // 8), step(n_chunks - 1)),
        name="hgrn_fwd",
        grid=(n_chunks,),
        in_specs=[
            pl.BlockSpec(blk, lambda c: (c, col0)),
            pl.BlockSpec(blk, lambda c: (c, col0 + 1)),
            pl.BlockSpec(blk, lambda c: (c, col0 + 2)),
            pl.BlockSpec((2, HGRN_WIDTH), lambda c: (0, 0)),
        ] + e_in,
        out_specs=[
            pl.BlockSpec(blk, lambda c: (c, 0)),
            pl.BlockSpec((1, nh, hd, hd), lambda c: (c, 0, 0, 0)),
            pl.BlockSpec(blk, lambda c: (c, 0)),
        ] + e_out,
        out_shape=[
            jax.ShapeDtypeStruct((s, HGRN_WIDTH), F32),
            jax.ShapeDtypeStruct((n_chunks, nh, hd, hd), F32),
            jax.ShapeDtypeStruct((s, nh * hd), F32),
        ] + e_shape,
        scratch_shapes=[
            pltpu.VMEM((nh, hd, hd), F32),
            pltpu.VMEM(blk, F32),
            pltpu.VMEM(blk, F32),
            pltpu.VMEM(blk, F32),
            pltpu.VMEM((nh, c_len, hd), F32),
        ] + e_scr,
        compiler_params=_params(dimension_semantics=("arbitrary",)),
    )(proj, proj, proj, lb, *e_args)


def hgrn_bwd(proj, lb, d_o, states, a_mat, ride=None):
    s = proj.shape[0]
    c_len, nh, hd = HGRN_CHUNK, HGRN_HEADS, HGRN_HEAD_DIM
    n_chunks = s // c_len
    col0 = 0
    last = n_chunks - 1

    def body(q_ref, f_ref, i_ref, lb_ref, do_ref, st_in_ref, a_in_ref, dq_ref, df_ref, di_ref, dlb_ref,
             dst_ref, b_ref, qf_ref, kf_ref, da_ref, dqi_ref, dki_ref):
        @pl.when(pl.program_id(0) == 0)
        def _():
            dst_ref[...] = jnp.zeros_like(dst_ref)
            dlb_ref[...] = jnp.zeros_like(dlb_ref)

        lbv = _lower_bound(lb_ref[...])
        q = q_ref[...]
        sq, qf, sig, f, kf = _hgrn_gates(q, f_ref[...], lbv)
        b = _tri_sum(_tril_bf16(c_len), jnp.log(f))
        b_ref[...] = b
        qf_ref[...] = qf
        kf_ref[...] = kf
        b_last = b[c_len - 1:c_len, :]
        eb = jnp.exp(b)
        ebl = jnp.exp(b_last - b)
        qb = qf * eb
        kb2 = kf * ebl
        vf = i_ref[...]
        d_o = do_ref[...]
        qb_b, kb2_b, vf_b, do_b = qb.astype(BF16), kb2.astype(BF16), vf.astype(BF16), d_o.astype(BF16)
        tq = lax.broadcasted_iota(jnp.int32, (c_len, hd), 0)
        lane = lax.broadcasted_iota(jnp.int32, (c_len, hd), 1)

        dqb_parts, dvf_parts, dkb2_parts, dbl_parts = [], [], [], []
        for h in range(nh):
            hs = slice(h * hd, (h + 1) * hd)
            st = st_in_ref[0, h]
            dst = dst_ref[h]
            st_b, dst_b = st.astype(BF16), dst.astype(BF16)
            a_h = a_in_ref[:, hs][:, :c_len].astype(BF16)
            dqb_parts.append(_dot(do_b[:, hs], st_b))
            dvf_parts.append(_dot_tn(a_h, do_b[:, hs]) + _dot_nt(kb2_b[:, hs], dst_b))
            dkb2_parts.append(_dot(vf_b[:, hs], dst_b))
            da = _dot_nt(do_b[:, hs], vf_b[:, hs])
            da = jnp.concatenate([da, jnp.zeros((c_len, hd - c_len), F32)], axis=1)
            da_ref[h] = jnp.where(tq >= lane, da, 0.0)
            dbl_parts.append(jnp.sum(dst * st, axis=0, keepdims=True) * jnp.exp(b_last[:, hs]))
            dst_ref[h] = dst * jnp.exp(b_last[:, hs]) + _dot_tn(do_b[:, hs], qb_b[:, hs])
        dqb = jnp.concatenate(dqb_parts, axis=1)
        dvf = jnp.concatenate(dvf_parts, axis=1)
        dkb2 = jnp.concatenate(dkb2_parts, axis=1)
        dbl = jnp.concatenate(dbl_parts, axis=1) + jnp.sum(dkb2 * kb2, axis=0, keepdims=True)

        dqi_ref[...] = jnp.zeros_like(dqi_ref)
        t_idx = lax.broadcasted_iota(jnp.int32, (c_len, nh * hd), 0)

        def diagonal(rows, j, key):
            bj = b_ref[pl.ds(key, 1), :]
            kj = kf_ref[pl.ds(key, 1), :]
            nrow = rows.stop - rows.start
            t_loc = lax.broadcasted_iota(jnp.int32, (nrow, nh * hd), 0) + (rows.start % SUB_BLOCK)
            e = jnp.exp(jnp.where(t_loc >= j, b_ref[rows, :] - bj, NEG_BIG))
            lane_r = lax.broadcasted_iota(jnp.int32, (nrow, hd), 1)
            cols = [jnp.sum(jnp.where(lane_r == key, da_ref[h, rows, :], 0.0), axis=-1, keepdims=True)
                    for h in range(nh)]
            w = e * jnp.concatenate([jnp.broadcast_to(cc, (nrow, hd)) for cc in cols], axis=1)
            dqi_ref[rows, :] += w * kj
            dki_ref[pl.ds(key, 1), :] = jnp.sum(w * qf_ref[rows, :], axis=0, keepdims=True)

        _hgrn_diagonal_loops(c_len, diagonal)

        q_til, k_til, eq, ek = _hgrn_off_diagonal(b, qf, kf)
        q_hi, q_lo = _split2(q_til)
        k_pairs = [_split2(k) for k in k_til]
        n_sub = c_len // SUB_BLOCK
        dq_heads, dk_heads = [], []
        for h in range(nh):
            hs = slice(h * hd, (h + 1) * hd)
            dq_rows = [jnp.zeros((SUB_BLOCK, hd), F32)]
            dk_h = jnp.zeros((c_len, hd), F32)
            for i in range(1, n_sub):
                rows = slice(i * SUB_BLOCK, (i + 1) * SUB_BLOCK)
                n = i * SUB_BLOCK
                da_i = da_ref[h, rows, :].astype(BF16)
                k_hi, k_lo = k_pairs[i - 1]
                dq_rows.append((_dot(da_i, k_hi[:, hs]) + _dot(da_i, k_lo[:, hs])) * eq[rows, hs])
                dk_t = (_dot_tn(da_i, q_hi[rows, hs]) + _dot_tn(da_i, q_lo[rows, hs]))[:n, :] * ek[i - 1][:, hs]
                dk_h = dk_h + jnp.concatenate([dk_t, jnp.zeros((c_len - n, hd), F32)], axis=0)
            dq_heads.append(jnp.concatenate(dq_rows, axis=0))
            dk_heads.append(dk_h)
        dq_intra = dqi_ref[...] + jnp.concatenate(dq_heads, axis=1)
        dk_intra = dki_ref[...] + jnp.concatenate(dk_heads, axis=1)

        db = dqb * qb + qf * dq_intra - kf * dk_intra - dkb2 * kb2
        db = db + jnp.where(t_idx == c_len - 1, dbl, 0.0)
        dg = _tri_sum(_tril_bf16(c_len, upper=True), db)
        dqf = dqb * eb + dq_intra
        dkf = dkb2 * ebl + dk_intra
        dq_ref[...] = dqf * (sq * (1.0 + q * (1.0 - sq)))
        dfv = dg / f - dkf
        df_ref[...] = dfv * (1.0 - lbv) * sig * (1.0 - sig)
        di_ref[...] = dvf
        dlb_ref[...] += jnp.sum(dfv * (1.0 - sig), axis=0, keepdims=True)

    blk = (c_len, HGRN_WIDTH)
    rev = lambda c: last - c
    step = lambda k: (lambda: pl.program_id(0) == k)
    e_in, e_out, e_shape, e_scr, e_args = _ride_specs(ride)
    return pl.pallas_call(
        _riding(body, 7, 4, 7, ride, step(0), step(n_chunks // 2), step(last)),
        name="hgrn_bwd",
        grid=(n_chunks,),
        in_specs=[
            pl.BlockSpec(blk, lambda c: (rev(c), col0)),
            pl.BlockSpec(blk, lambda c: (rev(c), col0 + 1)),
            pl.BlockSpec(blk, lambda c: (rev(c), col0 + 2)),
            pl.BlockSpec((2, HGRN_WIDTH), lambda c: (0, 0)),
            pl.BlockSpec(blk, lambda c: (rev(c), 0)),
            pl.BlockSpec((1, nh, hd, hd), lambda c: (rev(c), 0, 0, 0)),
            pl.BlockSpec(blk, lambda c: (rev(c), 0)),
        ] + e_in,
        out_specs=[
            pl.BlockSpec(blk, lambda c: (rev(c), 0)),
            pl.BlockSpec(blk, lambda c: (rev(c), 0)),
            pl.BlockSpec(blk, lambda c: (rev(c), 0)),
            pl.BlockSpec((1, HGRN_WIDTH), lambda c: (0, 0)),
        ] + e_out,
        out_shape=[jax.ShapeDtypeStruct((s, HGRN_WIDTH), F32)] * 3 + [jax.ShapeDtypeStruct((1, HGRN_WIDTH), F32)] + e_shape,
        scratch_shapes=[
            pltpu.VMEM((nh, hd, hd), F32),
            pltpu.VMEM(blk, F32),
            pltpu.VMEM(blk, F32),
            pltpu.VMEM(blk, F32),
            pltpu.VMEM((nh, c_len, hd), F32),
            pltpu.VMEM(blk, F32),
            pltpu.VMEM(blk, F32),
        ] + e_scr,
        compiler_params=_params(dimension_semantics=("arbitrary",)),
    )(proj, proj, proj, lb, d_o, states, a_mat, *e_args)


def _row_spec(tm, width, col=0):
    return pl.BlockSpec((tm, width), lambda i: (i, col))


def _const_spec(width):
    return pl.BlockSpec((1, width), lambda i: (0, 0))


def _acc_rows(ref, value):
    @pl.when(pl.program_id(0) == 0)
    def _():
        ref[...] = jnp.zeros_like(ref)

    ref[...] += jnp.sum(value, axis=0, keepdims=True)


def mix_fwd(attn_parts, o_h, proj, an, hn, w_out_b, gp, x):
    s = x.shape[0]
    tm = TOKEN_TILE
    gate_col = 3
    hd = HGRN_HEAD_DIM
    nd = len(DILATIONS)

    def body(*refs):
        o_refs, l_refs = refs[:nd], refs[nd:2 * nd]
        oh_ref, gate_ref, an_ref, hn_ref, w_ref, gp_ref, x_ref = refs[2 * nd:2 * nd + 7]
        x1_ref, cat_ref, mixed_ref, attn_ref = refs[2 * nd + 7:2 * nd + 11]
        lse_refs = refs[2 * nd + 11:3 * nd + 11]
        o_scr, l_scr, lse_scr = refs[3 * nd + 11:]
        os_ = [_from_dilated(r, o_scr.at[k], d, tm) for k, (r, d) in enumerate(zip(o_refs, DILATIONS))]
        ls = [_from_dilated(r, l_scr.at[k], d, tm) for k, (r, d) in enumerate(zip(l_refs, DILATIONS))]
        m = jnp.maximum(jnp.maximum(ls[0], ls[1]), ls[2])
        es = [jnp.exp(l - m) for l in ls]
        den = es[0] + es[1] + es[2]
        attn = (es[0] * os_[0] + es[1] * os_[1] + es[2] * os_[2]) / den
        attn_ref[...] = attn
        _lane_blocks(lse_scr, m + jnp.log(den))
        for d, ref in zip(DILATIONS, lse_refs):
            _to_dilated(lse_scr, ref, d, tm)
        cat_ref[:, :ATTN_WIDTH] = _rms_fwd(attn, an_ref[...], ATTN_WIDTH).astype(BF16)
        gate = gate_ref[...]
        silu_g = gate * _sigmoid(gate)
        for h in range(HGRN_HEADS):
            hs = slice(h * hd, (h + 1) * hd)
            rec = _rms_fwd(oh_ref[:, hs], hn_ref[:, hs], hd) * silu_g[:, hs]
            cat_ref[:, ATTN_WIDTH + h * hd:ATTN_WIDTH + (h + 1) * hd] = rec.astype(BF16)
        mixed = _dot(cat_ref[...], w_ref[...])
        mixed_ref[...] = mixed
        x1_ref[...] = x_ref[...] + _rms_fwd(mixed, gp_ref[...], D_MODEL)

    aw = ATTN_WIDTH
    return pl.pallas_call(
        body,
        name="mix_fwd",
        grid=(s // tm,),
        in_specs=[_dilated_spec(d, tm, aw) for d in DILATIONS] * 2 + [
            _row_spec(tm, aw), _row_spec(tm, aw, gate_col), _const_spec(aw), _const_spec(aw), _vmem_spec(),
            _const_spec(D_MODEL), _row_spec(tm, D_MODEL)],
        out_specs=[_row_spec(tm, D_MODEL), _row_spec(tm, D_MODEL), _row_spec(tm, D_MODEL), _row_spec(tm, aw)] + [
            _dilated_spec(d, tm, aw) for d in DILATIONS],
        out_shape=[
            jax.ShapeDtypeStruct((s, D_MODEL), F32),
            jax.ShapeDtypeStruct((s, D_MODEL), BF16),
            jax.ShapeDtypeStruct((s, D_MODEL), F32),
            jax.ShapeDtypeStruct((s, aw), F32),
        ] + [jax.ShapeDtypeStruct((d, s // d, aw), F32) for d in DILATIONS],
        scratch_shapes=[pltpu.VMEM((nd, aw // LANES, tm, LANES), F32), pltpu.VMEM((nd, aw // LANES, tm, LANES), F32),
                        pltpu.VMEM((aw // LANES, tm, LANES), F32)],
        compiler_params=_params(dimension_semantics=("arbitrary",)),
    )(*[p[0] for p in attn_parts], *[p[1] for p in attn_parts], o_h, proj, an, hn, w_out_b, gp, x)


def mix_bwd(dx1, mixed, gp, w_out_b, attn, an, o_h, proj, hn):
    s = dx1.shape[0]
    tm = TOKEN_TILE
    gate_col = 3
    hd = HGRN_HEAD_DIM
    aw = ATTN_WIDTH

    nd = len(DILATIONS)

    def body(*refs):
        dx1_ref, mixed_ref, gp_ref, w_ref, attn_ref, an_ref, oh_ref, gate_ref, hn_ref, dmix_ref = refs[:10]
        do_refs, delta_refs = refs[10:10 + nd], refs[10 + nd:10 + 2 * nd]
        doh_ref, dgate_ref, dgp_ref, dan_ref, dhn_ref, do_ref, delta_ref = refs[10 + 2 * nd:]
        dmixed, gp_c = _rms_bwd(dx1_ref[...], mixed_ref[...], gp_ref[...], D_MODEL)
        _acc_rows(dgp_ref, gp_c)
        dmixed_b = dmixed.astype(BF16)
        dmix_ref[...] = dmixed_b
        dcat = _dot_nt(dmixed_b, w_ref[...])
        attn = attn_ref[...]
        d_o, an_c = _rms_bwd(dcat[:, :aw], attn, an_ref[...], aw)
        _acc_rows(dan_ref, an_c)
        _lane_blocks(do_ref, d_o)
        prod = d_o * attn
        for pair in range(ATTN_HEADS // 2):
            pp = prod[:, pair * LANES:(pair + 1) * LANES]
            low = _lane_half((tm, LANES), 0)
            lo = jnp.sum(jnp.where(low, pp, 0.0), axis=-1, keepdims=True)
            hi = jnp.sum(jnp.where(low, 0.0, pp), axis=-1, keepdims=True)
            delta_ref[pair] = jnp.where(low, lo, hi)
        for d, o_ref, l_ref in zip(DILATIONS, do_refs, delta_refs):
            _to_dilated(do_ref, o_ref, d, tm)
            _to_dilated(delta_ref, l_ref, d, tm)
        gate = gate_ref[...]
        sg = _sigmoid(gate)
        silu_g = gate * sg
        drec = dcat[:, aw:]
        hn_parts = []
        for h in range(HGRN_HEADS):
            hs = slice(h * hd, (h + 1) * hd)
            oh = oh_ref[:, hs]
            on = _rms_fwd(oh, hn_ref[:, hs], hd)
            dgate_ref[:, hs] = drec[:, hs] * on * (sg[:, hs] * (1.0 + gate[:, hs] * (1.0 - sg[:, hs])))
            d_oh, hn_c = _rms_bwd(drec[:, hs] * silu_g[:, hs], oh, hn_ref[:, hs], hd)
            doh_ref[:, hs] = d_oh
            hn_parts.append(hn_c)
        _acc_rows(dhn_ref, jnp.concatenate(hn_parts, axis=1))

    return pl.pallas_call(
        body,
        name="mix_bwd",
        grid=(s // tm,),
        in_specs=[_row_spec(tm, D_MODEL), _row_spec(tm, D_MODEL), _const_spec(D_MODEL), _vmem_spec(), _row_spec(tm, aw),
                  _const_spec(aw), _row_spec(tm, aw), _row_spec(tm, aw, gate_col), _const_spec(aw)],
        out_specs=[_row_spec(tm, D_MODEL)] + [_dilated_spec(d, tm, aw) for d in DILATIONS] * 2 + [_row_spec(tm, aw)] * 2 + [
            _const_spec(D_MODEL), _const_spec(aw), _const_spec(aw)],
        out_shape=[jax.ShapeDtypeStruct((s, D_MODEL), BF16)] + [
            jax.ShapeDtypeStruct((d, s // d, aw), F32) for d in DILATIONS] * 2 + [jax.ShapeDtypeStruct((s, aw), F32)] * 2 + [
            jax.ShapeDtypeStruct((1, D_MODEL), F32), jax.ShapeDtypeStruct((1, aw), F32),
            jax.ShapeDtypeStruct((1, aw), F32)],
        scratch_shapes=[pltpu.VMEM((aw // LANES, tm, LANES), F32), pltpu.VMEM((aw // LANES, tm, LANES), F32)],
        compiler_params=_params(dimension_semantics=("arbitrary",)),
    )(dx1, mixed, gp, w_out_b, attn, an, o_h, proj, hn)


def mlp_fwd_bwd(x1, g_pre, w1_blocks, w2_b, g_post, target):
    s = x1.shape[0]
    tm = TOKEN_TILE
    nblk, _, fb = w1_blocks.shape

    def body(x1_ref, gpre_ref, w1_ref, w2_ref, gpost_ref, t_ref,
             dx1_ref, h2_ref, a_ref, du_ref, dff_ref, loss_ref, dgpre_ref, dgpost_ref, u_ref):
        x1v = x1_ref[...]
        h2 = _rms_fwd(x1v, gpre_ref[...], D_MODEL).astype(BF16)
        h2_ref[...] = h2
        ff = jnp.zeros((tm, D_MODEL), F32)
        for j in range(nblk):
            cols = slice(j * fb, (j + 1) * fb)
            ru = jnp.maximum(_dot(h2, w1_ref[j]), 0.0)
            u_ref[:, cols] = ru
            a = (ru * ru).astype(BF16)
            a_ref[:, cols] = a
            ff = ff + _dot(a, w2_ref[cols, :])
        diff = x1v + _rms_fwd(ff, gpost_ref[...], D_MODEL) - t_ref[...]
        _acc_rows(loss_ref, diff * diff)
        dy = diff * (1.0 / D_MODEL)
        dff, gpost_c = _rms_bwd(dy, ff, gpost_ref[...], D_MODEL)
        _acc_rows(dgpost_ref, gpost_c)
        dff_b = dff.astype(BF16)
        dff_ref[...] = dff_b
        dh2 = jnp.zeros((tm, D_MODEL), F32)
        for j in range(nblk):
            cols = slice(j * fb, (j + 1) * fb)
            du = (_dot_nt(dff_b, w2_ref[cols, :]) * (2.0 * u_ref[:, cols])).astype(BF16)
            du_ref[:, cols] = du
            dh2 = dh2 + _dot_nt(du, w1_ref[j])
        dxa, gpre_c = _rms_bwd(dh2, x1v, gpre_ref[...], D_MODEL)
        _acc_rows(dgpre_ref, gpre_c)
        dx1_ref[...] = dy + dxa

    dm = D_MODEL
    return pl.pallas_call(
        body,
        name="mlp_fwd_bwd",
        grid=(s // tm,),
        in_specs=[_row_spec(tm, dm), _const_spec(dm), _vmem_spec(), _vmem_spec(), _const_spec(dm), _row_spec(tm, dm)],
        out_specs=[_row_spec(tm, dm), _row_spec(tm, dm), _row_spec(tm, D_FF), _row_spec(tm, D_FF), _row_spec(tm, dm),
                   _const_spec(dm), _const_spec(dm), _const_spec(dm)],
        out_shape=[
            jax.ShapeDtypeStruct((s, dm), F32),
            jax.ShapeDtypeStruct((s, dm), BF16),
            jax.ShapeDtypeStruct((s, D_FF), BF16),
            jax.ShapeDtypeStruct((s, D_FF), BF16),
            jax.ShapeDtypeStruct((s, dm), BF16),
            jax.ShapeDtypeStruct((1, dm), F32),
            jax.ShapeDtypeStruct((1, dm), F32),
            jax.ShapeDtypeStruct((1, dm), F32),
        ],
        scratch_shapes=[pltpu.VMEM((tm, D_FF), F32)],
        compiler_params=_params(dimension_semantics=("arbitrary",)),
    )(x1, g_pre, w1_blocks, w2_b, g_post, target)


def in_proj_bwd(attn_grads, hgrn_grads, dgate, w_in_b, x, g1, dx1):
    s = x.shape[0]
    tm = TOKEN_TILE
    aw = ATTN_WIDTH
    n_attn = len(attn_grads)
    flat = [g[k] for k in range(3) for g in attn_grads] + list(hgrn_grads) + [dgate]

    def body(*refs):
        parts = refs[:len(flat)]
        w_ref, x_ref, g_ref, dx1_ref, dx_ref, dproj_ref, dg_ref, scr = refs[len(flat):]
        groups = []
        for k in range(3):
            acc = None
            for p, d in zip(parts[k * n_attn:(k + 1) * n_attn], DILATIONS):
                v = _from_dilated(p, scr, d, tm)
                acc = v if acc is None else acc + v
            groups.append(acc)
        groups += [p[...] for p in parts[3 * n_attn:]]
        dh = jnp.zeros((tm, D_MODEL), F32)
        for gi, grp in enumerate(groups):
            cols = slice(gi * aw, (gi + 1) * aw)
            gb = grp.astype(BF16)
            dproj_ref[:, cols] = gb
            dh = dh + _dot_nt(gb, w_ref[:, cols])
        dxa, g_c = _rms_bwd(dh, x_ref[...], g_ref[...], D_MODEL)
        _acc_rows(dg_ref, g_c)
        dx_ref[...] = dx1_ref[...] + dxa

    dm = D_MODEL
    return pl.pallas_call(
        body,
        name="in_proj_bwd",
        grid=(s // tm,),
        in_specs=[_dilated_spec(d, tm, aw) for d in DILATIONS] * 3 + [_row_spec(tm, aw)] * 4 + [
            _vmem_spec(), _row_spec(tm, dm), _const_spec(dm), _row_spec(tm, dm)],
        out_specs=[_row_spec(tm, dm), _row_spec(tm, IN_PROJ_WIDTH), _const_spec(dm)],
        out_shape=[jax.ShapeDtypeStruct((s, dm), F32), jax.ShapeDtypeStruct((s, IN_PROJ_WIDTH), BF16),
                   jax.ShapeDtypeStruct((1, dm), F32)],
        scratch_shapes=[pltpu.VMEM((aw // LANES, tm, LANES), F32)],
        compiler_params=_params(dimension_semantics=("arbitrary",)),
    )(*flat, w_in_b, x, g1, dx1)


def wgrad(a_b, b_b, tn, name, ts=1024, per_step=1):
    s, k = a_b.shape
    n = b_b.shape[1]

    def body(a_ref, b_ref, o_ref):
        @pl.when(pl.program_id(1) == 0)
        def _():
            o_ref[...] = jnp.zeros_like(o_ref)

        a = a_ref[...]
        for jj in range(per_step):
            o_ref[jj] += _dot_tn(a, b_ref[:, jj * tn:(jj + 1) * tn])

    wide = tn * per_step
    return pl.pallas_call(
        body,
        name=name,
        grid=(n // wide, s // ts),
        in_specs=[pl.BlockSpec((ts, k), lambda j, i: (i, 0)), pl.BlockSpec((ts, wide), lambda j, i: (i, j))],
        out_specs=pl.BlockSpec((per_step, k, tn), lambda j, i: (j, 0, 0)),
        out_shape=jax.ShapeDtypeStruct((n // tn, k, tn), F32),
        compiler_params=_params(dimension_semantics=("arbitrary", "arbitrary")),
    )(a_b, b_b)


def train_step(x, target, g1, an, logits, hn, gp, g_pre, g_post, w, m, v):
    nd = len(DILATIONS)
    shard_b = {k: w[k].astype(BF16) for k in BIG}
    (w_in_g,) = run_exchange(gather_exchange([shard_b["w_in"]]), "gather_w_in")
    w_in_b = w_in_g.transpose(1, 0, 2).reshape(D_MODEL, IN_PROJ_WIDTH)

    proj, h_b, *qkvs = in_proj_fwd(x, g1, w_in_b)
    attn_parts = [attn_fwd(qkv, d) for qkv, d in zip(qkvs, DILATIONS)]
    o_h, states, a_mat, w_out_g, w1_blocks, w2_g = hgrn_fwd(
        proj, logits, ride=gather_exchange([shard_b["w_out"], shard_b["w_ff1"], shard_b["w_ff2"]]))
    w_out_b = w_out_g.reshape(D_MODEL, D_MODEL)
    w2_b = w2_g.reshape(D_FF, D_MODEL)
    x1, cat_b, mixed, attn, *lses = mix_fwd(attn_parts, o_h, proj, an, hn, w_out_b, gp, x)
    dx1, h2_b, a_b, du_b, dff_b, loss_vec, dg_pre, dg_post = mlp_fwd_bwd(x1, g_pre, w1_blocks, w2_b, g_post, target)
    dw2 = wgrad(a_b, dff_b, D_MODEL, "wgrad_ff2", ts=512)
    dw1 = wgrad(h2_b, du_b, D_FF // N_DEV, "wgrad_ff1", per_step=2)
    dmix_b, *rest = mix_bwd(dx1, mixed, gp, w_out_b, attn, an, o_h, proj, hn)
    d_os, deltas = rest[:nd], rest[nd:2 * nd]
    d_oh, dgate, dgp, dan, dhn = rest[2 * nd:]
    dwout = wgrad(cat_b, dmix_b, D_MODEL, "wgrad_out")

    early = ("w_out", "w_ff1", "w_ff2")
    early_grads = [dwout.reshape(N_DEV, D_MODEL // N_DEV, D_MODEL), dw1, dw2.reshape(N_DEV, D_FF // N_DEV, D_MODEL)]
    attn_grads = []
    for k, d in enumerate(DILATIONS):
        ride = to_core_exchange(early_grads) if k == 0 else None
        res = attn_bwd(qkvs[k], d_os[k], lses[k], deltas[k], d, ride=ride)
        attn_grads.append(res[:3])
        if k == 0:
            pairs = [pair_sum(g, s, f"pair_sum_{name}") for g, s, name in zip(early_grads, res[3:], early)]
    dq_h, df_h, di_h, dlb, *others = hgrn_bwd(proj, logits, d_oh, states, a_mat,
                                              ride=to_chip_exchange([p[1] for p in pairs]))
    dx, dproj_b, dg1 = in_proj_bwd(attn_grads, (dq_h, df_h, di_h), dgate, w_in_b, x, g1, dx1)
    dwin = wgrad(h_b, dproj_b, 2 * IN_PROJ_WIDTH // N_DEV, "wgrad_in")
    big = {name: sum_adamw(p[0], o, w[name], m[name], v[name], f"sum_adamw_{name}")
           for name, p, o in zip(early, pairs, others)}

    shard_w = IN_PROJ_WIDTH // N_DEV
    dwin_blocks = dwin.reshape(N_DEV // 2, D_MODEL, 2, shard_w).transpose(0, 2, 1, 3).reshape(N_DEV, D_MODEL, shard_w)
    (from_sibling,) = run_exchange(to_core_exchange([dwin_blocks]), "reduce_w_in_to_core")
    pair_in, pair_in_b = pair_sum(dwin_blocks, from_sibling, "pair_sum_w_in")
    (others_in,) = run_exchange(to_chip_exchange([pair_in_b]), "reduce_w_in_to_chip")
    big["w_in"] = sum_adamw(pair_in, others_in, w["w_in"], m["w_in"], v["w_in"], "sum_adamw_w_in")
    small = dict(dg1=dg1, dan=dan, dlb=dlb, dhn=dhn, dgp=dgp, dg_pre=dg_pre, dg_post=dg_post, loss_vec=loss_vec)
    return dx, big, small


def _position():
    x, y, c = lax.axis_index("x"), lax.axis_index("y"), lax.axis_index("c")
    other_chips = [(1 - x, y), (x, 1 - y), (1 - x, 1 - y)]
    return x, y, c, other_chips


def _any_spec():
    return pl.BlockSpec(memory_space=pl.ANY)


class Exchange:
    def __init__(self, arrays, out_shape, sems, stages):
        self.arrays, self.out_shape, self.sems, self.stages = list(arrays), list(out_shape), list(sems), stages


def gather_exchange(shards):
    n = len(shards)

    def stages(ins, outs, sems):
        send_sems, recv_sems, local_sems = sems

        def parts():
            x, y, c, chips = _position()
            me, sibling = (x, y, c), (x, y, 1 - c)

            def slot(a, px, py, pc):
                return outs[a].at[4 * px + 2 * py + pc]

            def copy(a, k, block, to, src=None):
                return pltpu.make_async_remote_copy(
                    src_ref=slot(a, *block) if src is None else src, dst_ref=slot(a, *block),
                    send_sem=send_sems.at[a, k], recv_sem=recv_sems.at[a, k], device_id=to, device_id_type=MESH)

            local = [pltpu.make_async_copy(ins[a], slot(a, *me), local_sems.at[a]) for a in range(n)]
            first = []
            for a in range(n):
                first.append(copy(a, 0, me, sibling, src=ins[a]))
                first += [copy(a, 1 + j, me, (*chip, c), src=ins[a]) for j, chip in enumerate(chips)]
            passed = [copy(a, 4 + j, (*chip, c), sibling) for j, chip in enumerate(chips) for a in range(n)]
            return c, chips, me, sibling, copy, local, first, passed

        def begin():
            _, _, _, _, _, local, first, _ = parts()
            for cp in local + first:
                cp.start()

        def middle():
            c, chips, me, _, copy, _, _, passed = parts()
            k = 0
            for j, chip in enumerate(chips):
                for a in range(n):
                    copy(a, 1 + j, (*chip, c), me).wait_recv()
                    passed[k].start()
                    k += 1

        def end():
            c, chips, me, sibling, copy, local, first, passed = parts()
            for a in range(n):
                copy(a, 0, sibling, me).wait_recv()
                for j, chip in enumerate(chips):
                    copy(a, 4 + j, (*chip, 1 - c), me).wait_recv()
            for cp in first + passed:
                cp.wait_send()
            for cp in local:
                cp.wait()

        return begin, middle, end

    return Exchange(
        shards, [jax.ShapeDtypeStruct((N_DEV,) + sh.shape, sh.dtype) for sh in shards],
        [pltpu.SemaphoreType.DMA((n, 7)), pltpu.SemaphoreType.DMA((n, 7)), pltpu.SemaphoreType.DMA((n,))], stages)


def to_core_exchange(grads):
    n = len(grads)

    def stages(ins, outs, sems):
        send_sems, recv_sems = sems

        def copies():
            x, y, c, _ = _position()
            return [pltpu.make_async_remote_copy(
                src_ref=ins[a].at[2 * q + (1 - c)], dst_ref=outs[a].at[q], send_sem=send_sems.at[a, q],
                recv_sem=recv_sems.at[a, q], device_id=(x, y, 1 - c), device_id_type=MESH)
                for a in range(n) for q in range(4)]

        def begin():
            for cp in copies():
                cp.start()

        def end():
            for cp in copies():
                cp.wait()

        return begin, None, end

    return Exchange(grads, [jax.ShapeDtypeStruct((4,) + g.shape[1:], g.dtype) for g in grads],
                    [pltpu.SemaphoreType.DMA((n, 4)), pltpu.SemaphoreType.DMA((n, 4))], stages)


def pair_sum(grad, from_sibling, name):
    _, r, cdim = grad.shape
    tr = min(r, 256)
    c_idx = lax.axis_index("c").astype(jnp.int32).reshape(1)

    def body(c_ref, g_ref, s_ref, o_ref, ob_ref):
        total = g_ref[...] + s_ref[...]
        o_ref[...] = total
        ob_ref[...] = total.astype(BF16)

    blk = lambda: pl.BlockSpec((1, tr, cdim), lambda q, i, cr: (q, i, 0))
    return pl.pallas_call(
        body,
        name=name,
        grid_spec=pltpu.PrefetchScalarGridSpec(
            num_scalar_prefetch=1,
            grid=(4, r // tr),
            in_specs=[pl.BlockSpec((1, tr, cdim), lambda q, i, cr: (2 * q + cr[0], i, 0)), blk()],
            out_specs=[blk(), blk()],
        ),
        out_shape=[jax.ShapeDtypeStruct((4, r, cdim), F32), jax.ShapeDtypeStruct((4, r, cdim), BF16)],
        compiler_params=_params(dimension_semantics=("arbitrary", "arbitrary")),
    )(c_idx, grad, from_sibling)


def to_chip_exchange(pairs):
    n = len(pairs)

    def stages(ins, outs, sems):
        send_sems, recv_sems = sems

        def copies():
            x, y, c, chips = _position()
            return [pltpu.make_async_remote_copy(
                src_ref=ins[a].at[2 * px + py], dst_ref=outs[a].at[j], send_sem=send_sems.at[a, j],
                recv_sem=recv_sems.at[a, j], device_id=(px, py, c), device_id_type=MESH)
                for a in range(n) for j, (px, py) in enumerate(chips)]

        def begin():
            for cp in copies():
                cp.start()

        def end():
            for cp in copies():
                cp.wait()

        return begin, None, end

    return Exchange(pairs, [jax.ShapeDtypeStruct((3,) + p.shape[1:], p.dtype) for p in pairs],
                    [pltpu.SemaphoreType.DMA((n, 3)), pltpu.SemaphoreType.DMA((n, 3))], stages)


def run_exchange(ex, name):
    n_in, n_out = len(ex.arrays), len(ex.out_shape)

    def body(*refs):
        begin, middle, end = ex.stages(refs[:n_in], refs[n_in:n_in + n_out], refs[n_in + n_out:])
        begin()
        if middle is not None:
            middle()
        end()

    return pl.pallas_call(
        body,
        name=name,
        in_specs=[_any_spec()] * n_in,
        out_specs=[_any_spec()] * n_out,
        out_shape=ex.out_shape,
        scratch_shapes=ex.sems,
    )(*ex.arrays)


def _riding(body, n_in, n_out, n_scratch, ex, first, middle, last):
    if ex is None:
        return body
    r_in, r_out = len(ex.arrays), len(ex.out_shape)

    def wrapped(*refs):
        k_in, refs = refs[:n_in], refs[n_in:]
        e_in, refs = refs[:r_in], refs[r_in:]
        k_out, refs = refs[:n_out], refs[n_out:]
        e_out, refs = refs[:r_out], refs[r_out:]
        k_scr, e_sems = refs[:n_scratch], refs[n_scratch:]
        begin, mid, end = ex.stages(e_in, e_out, e_sems)
        pl.when(first())(begin)
        body(*k_in, *k_out, *k_scr)
        if mid is not None:
            pl.when(middle())(mid)
        pl.when(last())(end)

    return wrapped


def _ride_specs(ex):
    if ex is None:
        return [], [], [], [], []
    return [_any_spec()] * len(ex.arrays), [_any_spec()] * len(ex.out_shape), ex.out_shape, ex.sems, ex.arrays


def _adamw(w, g, m, v):
    m = ADAM_B1 * m + (1.0 - ADAM_B1) * g
    v = ADAM_B2 * v + (1.0 - ADAM_B2) * (g * g)
    m_hat = m / (1.0 - ADAM_B1 ** ADAM_STEP)
    v_hat = v / (1.0 - ADAM_B2 ** ADAM_STEP)
    delta = -ADAM_LR * (m_hat / (jnp.sqrt(v_hat) + ADAM_EPS) + ADAM_WD * w)
    return delta, m, v


def sum_adamw(pairs, others, w, m, v, name):
    r, cdim = w.shape
    tr = min(r, 256)
    chip_idx = (2 * lax.axis_index("x") + lax.axis_index("y")).astype(jnp.int32).reshape(1)

    def body(q_ref, p_ref, o_ref, w_ref, m_ref, v_ref, g_out, d_out, m_out, v_out):
        g = p_ref[0] + o_ref[0].astype(F32) + o_ref[1].astype(F32) + o_ref[2].astype(F32)
        g_out[...] = g
        d_out[...], m_out[...], v_out[...] = _adamw(w_ref[...], g, m_ref[...], v_ref[...])

    tile = lambda: pl.BlockSpec((tr, cdim), lambda i, qr: (i, 0))
    return pl.pallas_call(
        body,
        name=name,
        grid_spec=pltpu.PrefetchScalarGridSpec(
            num_scalar_prefetch=1,
            grid=(r // tr,),
            in_specs=[pl.BlockSpec((1, tr, cdim), lambda i, qr: (qr[0], i, 0)),
                      pl.BlockSpec((3, tr, cdim), lambda i, qr: (0, i, 0)), tile(), tile(), tile()],
            out_specs=[tile(), tile(), tile(), tile()],
        ),
        out_shape=[jax.ShapeDtypeStruct((r, cdim), F32)] * 4,
        compiler_params=_params(dimension_semantics=("arbitrary",)),
    )(chip_idx, pairs, others, w, m, v)


SMALL_ROWS = 8


def small_all_reduce(packed):
    shape = packed.shape

    def body(in_ref, out_ref, recv_ref, send_sems, recv_sems):
        x, y, c, _ = _position()
        my_id = 4 * x + 2 * y + c
        recv_ref[my_id] = in_ref[...]
        copies = []
        for rel in range(1, N_DEV):
            fx, fy, fc = (rel >> 2) & 1, (rel >> 1) & 1, rel & 1
            px = 1 - x if fx else x
            py = 1 - y if fy else y
            pc = 1 - c if fc else c
            cp = pltpu.make_async_remote_copy(
                src_ref=in_ref, dst_ref=recv_ref.at[my_id], send_sem=send_sems.at[rel - 1],
                recv_sem=recv_sems.at[rel - 1], device_id=(px, py, pc), device_id_type=MESH)
            cp.start()
            copies.append((cp, pltpu.make_async_remote_copy(
                src_ref=in_ref, dst_ref=recv_ref.at[4 * px + 2 * py + pc], send_sem=send_sems.at[rel - 1],
                recv_sem=recv_sems.at[rel - 1], device_id=(px, py, pc), device_id_type=MESH)))
        for cp, landing in copies:
            landing.wait_recv()
        for cp, landing in copies:
            cp.wait_send()
        total = recv_ref[0]
        for k in range(1, N_DEV):
            total = total + recv_ref[k]
        out_ref[...] = total

    return pl.pallas_call(
        body,
        name="small_all_reduce",
        in_specs=[_vmem_spec()],
        out_specs=_vmem_spec(),
        out_shape=jax.ShapeDtypeStruct(shape, F32),
        scratch_shapes=[pltpu.VMEM((N_DEV,) + shape, F32), pltpu.SemaphoreType.DMA((N_DEV - 1,)),
                        pltpu.SemaphoreType.DMA((N_DEV - 1,))],
    )(packed)


def small_adamw(reduced, w, m, v):
    def body(r_ref, w_ref, m_ref, v_ref, g_out, d_out, m_out, v_out, loss_out):
        red = r_ref[...]
        wv = w_ref[...]
        lb = _lower_bound(jnp.concatenate([wv[5:6, :HGRN_WIDTH], wv[5:6, HGRN_WIDTH:]], axis=0))
        t = red[5:6, :HGRN_WIDTH] * lb * (1.0 - lb)
        row = lax.broadcasted_iota(jnp.int32, red.shape, 0)
        g = jnp.where(row == 5, jnp.concatenate([t, -t], axis=1), jnp.where(row >= 6, 0.0, red))
        g_out[...] = g
        d_out[...], m_out[...], v_out[...] = _adamw(wv, g, m_ref[...], v_ref[...])
        loss = jnp.sum(red[6:7, :], axis=-1, keepdims=True) * (0.5 / D_MODEL)
        loss_out[...] = jnp.broadcast_to(loss, loss_out.shape)

    return pl.pallas_call(
        body,
        name="small_adamw",
        in_specs=[_vmem_spec()] * 4,
        out_specs=[_vmem_spec()] * 5,
        out_shape=[jax.ShapeDtypeStruct(reduced.shape, F32)] * 4 + [jax.ShapeDtypeStruct((8, 128), F32)],
    )(reduced, w, m, v)


def _pack_small(g1, gp, g_pre, g_post, an, hn, logits_or_dlb, extra=None):
    row5 = logits_or_dlb.reshape(1, -1)
    row5 = jnp.pad(row5, ((0, 0), (0, D_MODEL - row5.shape[1])))
    row6 = jnp.zeros((1, D_MODEL), F32) if extra is None else extra
    return jnp.concatenate([g1, gp, g_pre, g_post, jnp.concatenate([an, hn], axis=1), row5, row6,
                            jnp.zeros((1, D_MODEL), F32)], axis=0)


def _unpack_small(p):
    return dict(mix_pre_norm=p[0:1], mix_post_norm=p[1:2], mlp_pre_norm=p[2:3], mlp_post_norm=p[3:4],
                attn_out_norm=p[4:5, :ATTN_WIDTH], hgrn_out_norm=p[4:5, ATTN_WIDTH:],
                hgrn_lb_logits=p[5].reshape(2, HGRN_WIDTH))


BIG = ("w_in", "w_out", "w_ff1", "w_ff2")
ORDER = ("mix_pre_norm", "w_in", "attn_out_norm", "hgrn_lb_logits", "hgrn_out_norm", "w_out", "mix_post_norm",
         "mlp_pre_norm", "w_ff1", "w_ff2", "mlp_post_norm")


def kernel(x, mix_pre_norm, w_in, attn_out_norm, hgrn_lb_logits, hgrn_out_norm, w_out, mix_post_norm, mlp_pre_norm, w_ff1, w_ff2, mlp_post_norm, loss_target, m_mix_pre_norm, m_w_in, m_attn_out_norm, m_hgrn_lb_logits, m_hgrn_out_norm, m_w_out, m_mix_post_norm, m_mlp_pre_norm, m_w_ff1, m_w_ff2, m_mlp_post_norm, v_mix_pre_norm, v_w_in, v_attn_out_norm, v_hgrn_lb_logits, v_hgrn_out_norm, v_w_out, v_mix_post_norm, v_mlp_pre_norm, v_w_ff1, v_w_ff2, v_mlp_post_norm):
    w = dict(w_in=w_in[0], w_out=w_out[0], w_ff1=w_ff1[0], w_ff2=w_ff2[0])
    m = dict(w_in=m_w_in[0], w_out=m_w_out[0], w_ff1=m_w_ff1[0], w_ff2=m_w_ff2[0])
    v = dict(w_in=v_w_in[0], w_out=v_w_out[0], w_ff1=v_w_ff1[0], w_ff2=v_w_ff2[0])

    dx, big, small = train_step(x[0], loss_target[0], mix_pre_norm, attn_out_norm, hgrn_lb_logits, hgrn_out_norm,
                                mix_post_norm, mlp_pre_norm, mlp_post_norm, w, m, v)

    packed_g = _pack_small(small["dg1"], small["dgp"], small["dg_pre"], small["dg_post"], small["dan"], small["dhn"],
                           small["dlb"], small["loss_vec"])
    reduced = small_all_reduce(packed_g)
    pack = lambda a, b, c2, d, e, f, g: _pack_small(a, b, c2, d, e, f, g)
    w_s = pack(mix_pre_norm, mix_post_norm, mlp_pre_norm, mlp_post_norm, attn_out_norm, hgrn_out_norm, hgrn_lb_logits)
    m_s = pack(m_mix_pre_norm, m_mix_post_norm, m_mlp_pre_norm, m_mlp_post_norm, m_attn_out_norm, m_hgrn_out_norm,
               m_hgrn_lb_logits)
    v_s = pack(v_mix_pre_norm, v_mix_post_norm, v_mlp_pre_norm, v_mlp_post_norm, v_attn_out_norm, v_hgrn_out_norm,
               v_hgrn_lb_logits)
    g_s, d_s, nm_s, nv_s, loss = small_adamw(reduced, w_s, m_s, v_s)
    small_out = [_unpack_small(t) for t in (g_s, d_s, nm_s, nv_s)]

    outs = [loss[0, 0], dx[None]]
    for kind in range(4):
        for name in ORDER:
            outs.append(big[name][kind][None] if name in BIG else small_out[kind][name])
    return tuple(outs)
```

```python
import functools
import math

import jax
import jax.numpy as jnp
from jax import lax
from jax.experimental import pallas as pl
from jax.experimental.pallas import tpu as pltpu

F32 = jnp.float32
BF16 = jnp.bfloat16

D_MODEL = 1024
SEQ = 4096
ATTN_WIDTH = 512
ATTN_HEAD_DIM = 64
ATTN_HEADS = 8
ATTN_BLOCK = 128
DILATIONS = (1, 4, 16)
HGRN_WIDTH = 512
HGRN_HEADS = 4
HGRN_HEAD_DIM = 128
HGRN_CHUNK = 64
IN_PROJ_WIDTH = 3584
D_FF = 4096
RMS_EPS = 1e-6
N_DEV = 8
ADAM_LR = 0.001
ADAM_B1 = 0.9
ADAM_B2 = 0.999
ADAM_EPS = 1e-08
ADAM_WD = 0.01
ADAM_STEP = 10

SUBLANES = 8
LANES = 128
COLUMN_UNROLL = 4
SUB_BLOCK = 16
TOKEN_TILE = 256
PROJ_TILE = 512
VMEM_LIMIT = 56 * 1024 * 1024
NEG_BIG = -1e30
MESH = pl.DeviceIdType.MESH


def _params(**kw):
    return pltpu.CompilerParams(vmem_limit_bytes=VMEM_LIMIT, **kw)


def _vmem_spec():
    return pl.BlockSpec(memory_space=pltpu.VMEM)


def _dot(a, b):
    return jnp.dot(a, b, preferred_element_type=F32)


def _dot_nt(a, b):
    return lax.dot_general(a, b, (((1,), (1,)), ((), ())), preferred_element_type=F32)


def _dot_tn(a, b):
    return lax.dot_general(a, b, (((0,), (0,)), ((), ())), preferred_element_type=F32)


def _sigmoid(x):
    return 1.0 / (1.0 + jnp.exp(-x))


def _rms_fwd(x, gain, width):
    r = lax.rsqrt(jnp.sum(x * x, axis=-1, keepdims=True) * (1.0 / width) + RMS_EPS)
    return x * r * gain


def _rms_bwd(dy, x, gain, width):
    r = lax.rsqrt(jnp.sum(x * x, axis=-1, keepdims=True) * (1.0 / width) + RMS_EPS)
    xhat = x * r
    dxhat = dy * gain
    dx = r * (dxhat - xhat * (jnp.sum(dxhat * xhat, axis=-1, keepdims=True) * (1.0 / width)))
    return dx, dy * xhat


def _split3(x):
    hi = x.astype(BF16)
    r1 = x - hi.astype(F32)
    mid = r1.astype(BF16)
    lo = (r1 - mid.astype(F32)).astype(BF16)
    return hi, mid, lo


def _tri_sum(tri_bf16, x):
    hi, mid, lo = _split3(x)
    return _dot(tri_bf16, hi) + _dot(tri_bf16, mid) + _dot(tri_bf16, lo)


def _dilated_spec(d, tm, width):
    return pl.BlockSpec((d, tm // d, width), lambda i: (0, i, 0))


def _lane_blocks(ref, value):
    for c in range(ref.shape[0]):
        ref[c] = value[:, c * LANES:(c + 1) * LANES]


def _to_dilated(src_ref, dst_ref, d, tm, cast=None):
    for r in range(d):
        for c in range(src_ref.shape[0]):
            v = src_ref[c] if d == 1 else src_ref[c, pl.ds(r, tm // d, stride=d), :]
            dst_ref[r, :, c * LANES:(c + 1) * LANES] = v if cast is None else v.astype(cast)


def _from_dilated(src_ref, scratch_ref, d, tm):
    if d == 1:
        return src_ref[0].astype(F32)
    nblk = scratch_ref.shape[0]
    for r in range(d):
        for c in range(nblk):
            scratch_ref[c, pl.ds(r, tm // d, stride=d), :] = src_ref[r, :, c * LANES:(c + 1) * LANES].astype(F32)
    return jnp.concatenate([scratch_ref[c] for c in range(nblk)], axis=1)


def in_proj_fwd(x, g1, w_in_b):
    s = x.shape[0]
    tm = PROJ_TILE
    qkv_w = 3 * ATTN_WIDTH
    hg_w = IN_PROJ_WIDTH - qkv_w

    def body(x_ref, g_ref, w_ref, hg_ref, h_ref, *rest):
        qkv_refs, qkv_scr = rest[:len(DILATIONS)], rest[len(DILATIONS)]
        h = _rms_fwd(x_ref[...], g_ref[...], D_MODEL).astype(BF16)
        h_ref[...] = h
        proj = _dot(h, w_ref[...])
        hg_ref[...] = proj[:, qkv_w:]
        _lane_blocks(qkv_scr, proj[:, :qkv_w])
        for d, ref in zip(DILATIONS, qkv_refs):
            _to_dilated(qkv_scr, ref, d, tm, cast=BF16)

    return pl.pallas_call(
        body,
        name="in_proj_fwd",
        grid=(s // tm,),
        in_specs=[
            pl.BlockSpec((tm, D_MODEL), lambda i: (i, 0)),
            pl.BlockSpec((1, D_MODEL), lambda i: (0, 0)),
            _vmem_spec(),
        ],
        out_specs=[
            pl.BlockSpec((tm, hg_w), lambda i: (i, 0)),
            pl.BlockSpec((tm, D_MODEL), lambda i: (i, 0)),
        ] + [_dilated_spec(d, tm, qkv_w) for d in DILATIONS],
        out_shape=[jax.ShapeDtypeStruct((s, hg_w), F32), jax.ShapeDtypeStruct((s, D_MODEL), BF16)] + [
            jax.ShapeDtypeStruct((d, s // d, qkv_w), BF16) for d in DILATIONS],
        scratch_shapes=[pltpu.VMEM((qkv_w // LANES, tm, LANES), F32)],
        compiler_params=_params(dimension_semantics=("arbitrary",)),
    )(x, g1, w_in_b)


ATTN_SCALE = ATTN_HEAD_DIM ** -0.5


def _fill_attn_bias(bias_ref, dilation):
    qi = lax.broadcasted_iota(jnp.int32, (ATTN_BLOCK, 2 * ATTN_BLOCK), 0)
    kj = lax.broadcasted_iota(jnp.int32, (ATTN_BLOCK, 2 * ATTN_BLOCK), 1)
    dist = qi + ATTN_BLOCK - kj
    valid = (dist >= 0) & (dist <= ATTN_BLOCK)
    for head in range(ATTN_HEADS):
        slope = 2.0 ** (-8.0 * (head + 1) / ATTN_HEADS)
        bias = jnp.where(valid, dist.astype(F32) * (-slope * dilation), NEG_BIG)
        bias_ref[0, head] = bias
        bias_ref[1, head] = jnp.where(kj >= ATTN_BLOCK, bias, NEG_BIG)


def _attn_scores(qm, kcat, bias_ref, head, first_block):
    return _dot_nt(qm, kcat) + bias_ref[first_block.astype(jnp.int32), head]


def _lane_half(shape, sub):
    lane = lax.broadcasted_iota(jnp.int32, shape, 1)
    return (lane < ATTN_HEAD_DIM) if sub == 0 else (lane >= ATTN_HEAD_DIM)


def _sub_block(col, row):
    return pl.BlockSpec((None, ATTN_BLOCK, ATTN_WIDTH), lambda r, n: (r, row(n), col))


def attn_fwd(qkv, dilation):
    d, length, _ = qkv.shape
    assert d == dilation
    nb = length // ATTN_BLOCK

    def body(q_ref, kc_ref, kp_ref, vc_ref, vp_ref, o_ref, lse_ref, bias_ref):
        @pl.when((pl.program_id(0) == 0) & (pl.program_id(1) == 0))
        def _():
            _fill_attn_bias(bias_ref, d)

        first = pl.program_id(1) == 0
        for pair in range(ATTN_HEADS // 2):
            lanes = slice(pair * 128, (pair + 1) * 128)
            q2 = q_ref[:, lanes] * ATTN_SCALE
            kcat = jnp.concatenate([kp_ref[:, lanes], kc_ref[:, lanes]], axis=0)
            vcat = jnp.concatenate([vp_ref[:, lanes], vc_ref[:, lanes]], axis=0)
            o_pair = jnp.zeros((ATTN_BLOCK, 128), F32)
            lse_pair = jnp.zeros((ATTN_BLOCK, 128), F32)
            for sub in range(2):
                keep = _lane_half((ATTN_BLOCK, 128), sub)
                qm = jnp.where(keep, q2, jnp.zeros_like(q2))
                sc = _attn_scores(qm, kcat, bias_ref, 2 * pair + sub, first)
                m = jnp.max(sc, axis=-1, keepdims=True)
                p = jnp.exp(sc - m)
                den = jnp.sum(p, axis=-1, keepdims=True)
                o = _dot(p.astype(BF16), vcat) / den
                o_pair = jnp.where(keep, o, o_pair)
                lse_pair = jnp.where(keep, m + jnp.log(den), lse_pair)
            o_ref[:, lanes] = o_pair.astype(BF16)
            lse_ref[:, lanes] = lse_pair

    cur = lambda n: n
    prev = lambda n: jnp.maximum(n - 1, 0)
    return pl.pallas_call(
        body,
        name=f"attn_fwd_d{d}",
        grid=(d, nb),
        in_specs=[_sub_block(0, cur), _sub_block(1, cur), _sub_block(1, prev), _sub_block(2, cur), _sub_block(2, prev)],
        out_specs=[_sub_block(0, cur), _sub_block(0, cur)],
        out_shape=[jax.ShapeDtypeStruct((d, length, ATTN_WIDTH), BF16), jax.ShapeDtypeStruct((d, length, ATTN_WIDTH), F32)],
        scratch_shapes=[pltpu.VMEM((2, ATTN_HEADS, ATTN_BLOCK, 2 * ATTN_BLOCK), F32)],
        compiler_params=_params(dimension_semantics=("arbitrary", "arbitrary")),
    )(qkv, qkv, qkv, qkv, qkv)


def attn_bwd(qkv, d_out, lse, delta, dilation, ride=None):
    d, length, _ = qkv.shape
    assert d == dilation
    nb = length // ATTN_BLOCK

    steps = d * nb + 1

    def body(q_ref, kc_ref, kp_ref, vc_ref, vp_ref, do_ref, lse_ref, dl_ref, dq_ref, dk_ref, dv_ref, ck_ref, cv_ref,
             bias_ref):
        t = pl.program_id(0)

        @pl.when(t == 0)
        def _():
            ck_ref[...] = jnp.zeros_like(ck_ref)
            cv_ref[...] = jnp.zeros_like(cv_ref)
            _fill_attn_bias(bias_ref, d)

        @pl.when(t < steps - 1)
        def _():
            first = t % nb == 0
            for pair in range(ATTN_HEADS // 2):
                lanes = slice(pair * 128, (pair + 1) * 128)
                q2 = q_ref[:, lanes] * ATTN_SCALE
                do2 = do_ref[:, lanes]
                kcat = jnp.concatenate([kp_ref[:, lanes], kc_ref[:, lanes]], axis=0)
                vcat = jnp.concatenate([vp_ref[:, lanes], vc_ref[:, lanes]], axis=0)
                dq_pair = jnp.zeros((ATTN_BLOCK, 128), F32)
                dk_cat = jnp.zeros((2 * ATTN_BLOCK, 128), F32)
                dv_cat = jnp.zeros((2 * ATTN_BLOCK, 128), F32)
                for sub in range(2):
                    keep = _lane_half((ATTN_BLOCK, 128), sub)
                    col = pair * 128 + sub * ATTN_HEAD_DIM
                    qm = jnp.where(keep, q2, jnp.zeros_like(q2))
                    dom = jnp.where(keep, do2, jnp.zeros_like(do2))
                    sc = _attn_scores(qm, kcat, bias_ref, 2 * pair + sub, first)
                    p = jnp.exp(sc - lse_ref[:, col:col + 1])
                    dp = _dot_nt(dom, vcat)
                    ds = (p * (dp - dl_ref[:, col:col + 1])).astype(BF16)
                    dq_pair = jnp.where(keep, _dot(ds, kcat), dq_pair)
                    dk_cat = dk_cat + _dot_tn(ds, qm)
                    dv_cat = dv_cat + _dot_tn(p.astype(BF16), dom)
                dq_ref[:, lanes] = (dq_pair * ATTN_SCALE).astype(BF16)
                dk_ref[:, lanes] = (ck_ref[:, lanes] + dk_cat[:ATTN_BLOCK]).astype(BF16)
                dv_ref[:, lanes] = (cv_ref[:, lanes] + dv_cat[:ATTN_BLOCK]).astype(BF16)
                ck_ref[:, lanes] = dk_cat[ATTN_BLOCK:]
                cv_ref[:, lanes] = dv_cat[ATTN_BLOCK:]

        @pl.when(t == steps - 1)
        def _():
            dk_ref[...] = ck_ref[...].astype(BF16)
            dv_ref[...] = cv_ref[...].astype(BF16)

    blk = (ATTN_BLOCK, ATTN_WIDTH)

    def spec(col, shift):
        def index(t):
            f = jnp.minimum(t, steps - 2) if shift > -2 else jnp.maximum(t - 1, 0)
            r, n = f // nb, f % nb
            return (r, jnp.maximum(n - 1, 0) if shift == -1 else n, col)
        return pl.BlockSpec((None, ATTN_BLOCK, ATTN_WIDTH), index)

    step = lambda k: (lambda: pl.program_id(0) == k)
    e_in, e_out, e_shape, e_scr, e_args = _ride_specs(ride)
    return pl.pallas_call(
        _riding(body, 8, 3, 3, ride, step(0), step(steps // 2), step(steps - 1)),
        name=f"attn_bwd_d{d}",
        grid=(steps,),
        in_specs=[spec(0, 0), spec(1, 0), spec(1, -1), spec(2, 0), spec(2, -1), spec(0, 0), spec(0, 0), spec(0, 0)] + e_in,
        out_specs=[spec(0, 0), spec(0, -2), spec(0, -2)] + e_out,
        out_shape=[jax.ShapeDtypeStruct((d, length, ATTN_WIDTH), BF16)] * 3 + e_shape,
        scratch_shapes=[pltpu.VMEM(blk, F32), pltpu.VMEM(blk, F32),
                        pltpu.VMEM((2, ATTN_HEADS, ATTN_BLOCK, 2 * ATTN_BLOCK), F32)] + e_scr,
        compiler_params=_params(dimension_semantics=("arbitrary",)),
    )(qkv, qkv, qkv, qkv, qkv, d_out, lse, delta, *e_args)


def _lower_bound(logits):
    return _sigmoid(logits[0:1, :] - logits[1:2, :])


def _hgrn_gates(q, fp, lb):
    sq = _sigmoid(q)
    qf = q * sq
    sig = _sigmoid(fp)
    f = lb + (1.0 - lb) * sig
    kf = (1.0 - lb) * _sigmoid(-fp)
    return sq, qf, sig, f, kf


def _tril_bf16(n, upper=False):
    r = lax.broadcasted_iota(jnp.int32, (n, n), 0)
    c = lax.broadcasted_iota(jnp.int32, (n, n), 1)
    keep = (c >= r) if upper else (c <= r)
    return jnp.where(keep, 1.0, 0.0).astype(BF16)


def _hgrn_diagonal_loops(c_len, diagonal):
    for half in range(SUB_BLOCK // SUBLANES):
        def step(jj, carry, half=half):
            j = half * SUBLANES + jj
            for i in range(c_len // SUB_BLOCK):
                diagonal(slice(i * SUB_BLOCK + half * SUBLANES, (i + 1) * SUB_BLOCK), j, i * SUB_BLOCK + j)
            return carry

        lax.fori_loop(0, SUBLANES, step, 0, unroll=COLUMN_UNROLL)


def _hgrn_off_diagonal(b, qf, kf):
    c_len, width = b.shape
    edges = [b[0:1, :]] + [b[i * SUB_BLOCK - 1:i * SUB_BLOCK, :] for i in range(1, c_len // SUB_BLOCK)]
    eq = jnp.exp(b - jnp.concatenate([jnp.broadcast_to(e, (SUB_BLOCK, width)) for e in edges], axis=0))
    q_til = qf * eq
    k_til, ek = [], []
    for i in range(1, c_len // SUB_BLOCK):
        n = i * SUB_BLOCK
        e = jnp.exp(edges[i] - b[:n, :])
        ek.append(e)
        k_til.append(jnp.concatenate([kf[:n, :] * e, jnp.zeros((2 * c_len - n, width), F32)], axis=0))
    return q_til, k_til, eq, ek


def _split2(x):
    hi = x.astype(BF16)
    return hi, (x - hi.astype(F32)).astype(BF16)


def hgrn_fwd(proj, lb, ride=None):
    s = proj.shape[0]
    c_len, nh, hd = HGRN_CHUNK, HGRN_HEADS, HGRN_HEAD_DIM
    n_chunks = s // c_len
    col0 = 0

    def body(q_ref, f_ref, i_ref, lb_ref, o_ref, st_out_ref, a_out_ref, st_ref, b_ref, qf_ref, kf_ref, a_ref):
        @pl.when(pl.program_id(0) == 0)
        def _():
            st_ref[...] = jnp.zeros_like(st_ref)

        lbv = _lower_bound(lb_ref[...])
        _, qf, _, f, kf = _hgrn_gates(q_ref[...], f_ref[...], lbv)
        b = _tri_sum(_tril_bf16(c_len), jnp.log(f))
        b_ref[...] = b
        qf_ref[...] = qf
        kf_ref[...] = kf
        a_ref[...] = jnp.zeros_like(a_ref)

        def diagonal(rows, j, key):
            bj = b_ref[pl.ds(key, 1), :]
            kj = kf_ref[pl.ds(key, 1), :]
            nrow = rows.stop - rows.start
            t_loc = lax.broadcasted_iota(jnp.int32, (nrow, nh * hd), 0) + (rows.start % SUB_BLOCK)
            e = jnp.exp(jnp.where(t_loc >= j, b_ref[rows, :] - bj, NEG_BIG))
            prod = qf_ref[rows, :] * kj * e
            lane = lax.broadcasted_iota(jnp.int32, (nrow, hd), 1)
            for h in range(nh):
                col = jnp.sum(prod[:, h * hd:(h + 1) * hd], axis=-1, keepdims=True)
                a_ref[h, rows, :] = jnp.where(lane == key, col, a_ref[h, rows, :])

        _hgrn_diagonal_loops(c_len, diagonal)
        q_til, k_til, _, _ = _hgrn_off_diagonal(b, qf, kf)
        q_til = q_til.astype(BF16)
        k_til = [k.astype(BF16) for k in k_til]

        b_last = b[c_len - 1:c_len, :]
        qb = (qf * jnp.exp(b)).astype(BF16)
        kb2 = (kf * jnp.exp(b_last - b)).astype(BF16)
        vf = i_ref[...].astype(BF16)
        for h in range(nh):
            hs = slice(h * hd, (h + 1) * hd)
            st = st_ref[h]
            st_out_ref[0, h] = st
            off = [jnp.zeros((SUB_BLOCK, hd), F32)]
            for i in range(1, c_len // SUB_BLOCK):
                off.append(_dot_nt(q_til[i * SUB_BLOCK:(i + 1) * SUB_BLOCK, hs], k_til[i - 1][:, hs]))
            a_h = a_ref[h] + jnp.concatenate(off, axis=0)
            a_out_ref[:, hs] = a_h
            o_ref[:, hs] = _dot_nt(qb[:, hs], st.astype(BF16)) + _dot(a_h[:, :c_len].astype(BF16), vf[:, hs])
            st_ref[h] = st * jnp.exp(b_last[:, hs]) + _dot_tn(vf[:, hs], kb2[:, hs])

    blk = (c_len, HGRN_WIDTH)
    step = lambda k: (lambda: pl.program_id(0) == k)
    e_in, e_out, e_shape, e_scr, e_args = _ride_specs(ride)
    return pl.pallas_call(
        _riding(body, 4, 3, 5, ride, step(0), step((7 * n_chunks) // 8), step(n_chunks - 1)),
        name="hgrn_fwd",
        grid=(n_chunks,),
        in_specs=[
            pl.BlockSpec(blk, lambda c: (c, col0)),
            pl.BlockSpec(blk, lambda c: (c, col0 + 1)),
            pl.BlockSpec(blk, lambda c: (c, col0 + 2)),
            pl.BlockSpec((2, HGRN_WIDTH), lambda c: (0, 0)),
        ] + e_in,
        out_specs=[
            pl.BlockSpec(blk, lambda c: (c, 0)),
            pl.BlockSpec((1, nh, hd, hd), lambda c: (c, 0, 0, 0)),
            pl.BlockSpec(blk, lambda c: (c, 0)),
        ] + e_out,
        out_shape=[
            jax.ShapeDtypeStruct((s, HGRN_WIDTH), F32),
            jax.ShapeDtypeStruct((n_chunks, nh, hd, hd), F32),
            jax.ShapeDtypeStruct((s, nh * hd), F32),
        ] + e_shape,
        scratch_shapes=[
            pltpu.VMEM((nh, hd, hd), F32),
            pltpu.VMEM(blk, F32),
            pltpu.VMEM(blk, F32),
            pltpu.VMEM(blk, F32),
            pltpu.VMEM((nh, c_len, hd), F32),
        ] + e_scr,
        compiler_params=_params(dimension_semantics=("arbitrary",)),
    )(proj, proj, proj, lb, *e_args)


def hgrn_bwd(proj, lb, d_o, states, a_mat, ride=None):
    s = proj.shape[0]
    c_len, nh, hd = HGRN_CHUNK, HGRN_HEADS, HGRN_HEAD_DIM
    n_chunks = s // c_len
    col0 = 0
    last = n_chunks - 1

    def body(q_ref, f_ref, i_ref, lb_ref, do_ref, st_in_ref, a_in_ref, dq_ref, df_ref, di_ref, dlb_ref,
             dst_ref, b_ref, qf_ref, kf_ref, da_ref, dqi_ref, dki_ref):
        @pl.when(pl.program_id(0) == 0)
        def _():
            dst_ref[...] = jnp.zeros_like(dst_ref)
            dlb_ref[...] = jnp.zeros_like(dlb_ref)

        lbv = _lower_bound(lb_ref[...])
        q = q_ref[...]
        sq, qf, sig, f, kf = _hgrn_gates(q, f_ref[...], lbv)
        b = _tri_sum(_tril_bf16(c_len), jnp.log(f))
        b_ref[...] = b
        qf_ref[...] = qf
        kf_ref[...] = kf
        b_last = b[c_len - 1:c_len, :]
        eb = jnp.exp(b)
        ebl = jnp.exp(b_last - b)
        qb = qf * eb
        kb2 = kf * ebl
        vf = i_ref[...]
        d_o = do_ref[...]
        qb_b, kb2_b, vf_b, do_b = qb.astype(BF16), kb2.astype(BF16), vf.astype(BF16), d_o.astype(BF16)
        tq = lax.broadcasted_iota(jnp.int32, (c_len, hd), 0)
        lane = lax.broadcasted_iota(jnp.int32, (c_len, hd), 1)

        dqb_parts, dvf_parts, dkb2_parts, dbl_parts = [], [], [], []
        for h in range(nh):
            hs = slice(h * hd, (h + 1) * hd)
            st = st_in_ref[0, h]
            dst = dst_ref[h]
            st_b, dst_b = st.astype(BF16), dst.astype(BF16)
            a_h = a_in_ref[:, hs][:, :c_len].astype(BF16)
            dqb_parts.append(_dot(do_b[:, hs], st_b))
            dvf_parts.append(_dot_tn(a_h, do_b[:, hs]) + _dot_nt(kb2_b[:, hs], dst_b))
            dkb2_parts.append(_dot(vf_b[:, hs], dst_b))
            da = _dot_nt(do_b[:, hs], vf_b[:, hs])
            da = jnp.concatenate([da, jnp.zeros((c_len, hd - c_len), F32)], axis=1)
            da_ref[h] = jnp.where(tq >= lane, da, 0.0)
            dbl_parts.append(jnp.sum(dst * st, axis=0, keepdims=True) * jnp.exp(b_last[:, hs]))
            dst_ref[h] = dst * jnp.exp(b_last[:, hs]) + _dot_tn(do_b[:, hs], qb_b[:, hs])
        dqb = jnp.concatenate(dqb_parts, axis=1)
        dvf = jnp.concatenate(dvf_parts, axis=1)
        dkb2 = jnp.concatenate(dkb2_parts, axis=1)
        dbl = jnp.concatenate(dbl_parts, axis=1) + jnp.sum(dkb2 * kb2, axis=0, keepdims=True)

        dqi_ref[...] = jnp.zeros_like(dqi_ref)
        t_idx = lax.broadcasted_iota(jnp.int32, (c_len, nh * hd), 0)

        def diagonal(rows, j, key):
            bj = b_ref[pl.ds(key, 1), :]
            kj = kf_ref[pl.ds(key, 1), :]
            nrow = rows.stop - rows.start
            t_loc = lax.broadcasted_iota(jnp.int32, (nrow, nh * hd), 0) + (rows.start % SUB_BLOCK)
            e = jnp.exp(jnp.where(t_loc >= j, b_ref[rows, :] - bj, NEG_BIG))
            lane_r = lax.broadcasted_iota(jnp.int32, (nrow, hd), 1)
            cols = [jnp.sum(jnp.where(lane_r == key, da_ref[h, rows, :], 0.0), axis=-1, keepdims=True)
                    for h in range(nh)]
            w = e * jnp.concatenate([jnp.broadcast_to(cc, (nrow, hd)) for cc in cols], axis=1)
            dqi_ref[rows, :] += w * kj
            dki_ref[pl.ds(key, 1), :] = jnp.sum(w * qf_ref[rows, :], axis=0, keepdims=True)

        _hgrn_diagonal_loops(c_len, diagonal)

        q_til, k_til, eq, ek = _hgrn_off_diagonal(b, qf, kf)
        q_hi, q_lo = _split2(q_til)
        k_pairs = [_split2(k) for k in k_til]
        n_sub = c_len // SUB_BLOCK
        dq_heads, dk_heads = [], []
        for h in range(nh):
            hs = slice(h * hd, (h + 1) * hd)
            dq_rows = [jnp.zeros((SUB_BLOCK, hd), F32)]
            dk_h = jnp.zeros((c_len, hd), F32)
            for i in range(1, n_sub):
                rows = slice(i * SUB_BLOCK, (i + 1) * SUB_BLOCK)
                n = i * SUB_BLOCK
                da_i = da_ref[h, rows, :].astype(BF16)
                k_hi, k_lo = k_pairs[i - 1]
                dq_rows.append((_dot(da_i, k_hi[:, hs]) + _dot(da_i, k_lo[:, hs])) * eq[rows, hs])
                dk_t = (_dot_tn(da_i, q_hi[rows, hs]) + _dot_tn(da_i, q_lo[rows, hs]))[:n, :] * ek[i - 1][:, hs]
                dk_h = dk_h + jnp.concatenate([dk_t, jnp.zeros((c_len - n, hd), F32)], axis=0)
            dq_heads.append(jnp.concatenate(dq_rows, axis=0))
            dk_heads.append(dk_h)
        dq_intra = dqi_ref[...] + jnp.concatenate(dq_heads, axis=1)
        dk_intra = dki_ref[...] + jnp.concatenate(dk_heads, axis=1)

        db = dqb * qb + qf * dq_intra - kf * dk_intra - dkb2 * kb2
        db = db + jnp.where(t_idx == c_len - 1, dbl, 0.0)
        dg = _tri_sum(_tril_bf16(c_len, upper=True), db)
        dqf = dqb * eb + dq_intra
        dkf = dkb2 * ebl + dk_intra
        dq_ref[...] = (dqf * (sq * (1.0 + q * (1.0 - sq)))).astype(BF16)
        dfv = dg / f - dkf
        df_ref[...] = (dfv * (1.0 - lbv) * sig * (1.0 - sig)).astype(BF16)
        di_ref[...] = dvf.astype(BF16)
        dlb_ref[...] += jnp.sum(dfv * (1.0 - sig), axis=0, keepdims=True)

    blk = (c_len, HGRN_WIDTH)
    rev = lambda c: last - c
    step = lambda k: (lambda: pl.program_id(0) == k)
    e_in, e_out, e_shape, e_scr, e_args = _ride_specs(ride)
    return pl.pallas_call(
        _riding(body, 7, 4, 7, ride, step(0), step(n_chunks // 2), step(last)),
        name="hgrn_bwd",
        grid=(n_chunks,),
        in_specs=[
            pl.BlockSpec(blk, lambda c: (rev(c), col0)),
            pl.BlockSpec(blk, lambda c: (rev(c), col0 + 1)),
            pl.BlockSpec(blk, lambda c: (rev(c), col0 + 2)),
            pl.BlockSpec((2, HGRN_WIDTH), lambda c: (0, 0)),
            pl.BlockSpec(blk, lambda c: (rev(c), 0)),
            pl.BlockSpec((1, nh, hd, hd), lambda c: (rev(c), 0, 0, 0)),
            pl.BlockSpec(blk, lambda c: (rev(c), 0)),
        ] + e_in,
        out_specs=[
            pl.BlockSpec(blk, lambda c: (rev(c), 0)),
            pl.BlockSpec(blk, lambda c: (rev(c), 0)),
            pl.BlockSpec(blk, lambda c: (rev(c), 0)),
            pl.BlockSpec((1, HGRN_WIDTH), lambda c: (0, 0)),
        ] + e_out,
        out_shape=[jax.ShapeDtypeStruct((s, HGRN_WIDTH), BF16)] * 3 + [jax.ShapeDtypeStruct((1, HGRN_WIDTH), F32)] + e_shape,
        scratch_shapes=[
            pltpu.VMEM((nh, hd, hd), F32),
            pltpu.VMEM(blk, F32),
            pltpu.VMEM(blk, F32),
            pltpu.VMEM(blk, F32),
            pltpu.VMEM((nh, c_len, hd), F32),
            pltpu.VMEM(blk, F32),
            pltpu.VMEM(blk, F32),
        ] + e_scr,
        compiler_params=_params(dimension_semantics=("arbitrary",)),
    )(proj, proj, proj, lb, d_o, states, a_mat, *e_args)


def _row_spec(tm, width, col=0):
    return pl.BlockSpec((tm, width), lambda i: (i, col))


def _const_spec(width):
    return pl.BlockSpec((1, width), lambda i: (0, 0))


def _acc_rows(ref, value):
    @pl.when(pl.program_id(0) == 0)
    def _():
        ref[...] = jnp.zeros_like(ref)

    ref[...] += jnp.sum(value, axis=0, keepdims=True)


def mix_fwd(attn_parts, o_h, proj, an, hn, w_out_b, gp, x):
    s = x.shape[0]
    tm = TOKEN_TILE
    gate_col = 3
    hd = HGRN_HEAD_DIM
    nd = len(DILATIONS)

    def body(*refs):
        o_refs, l_refs = refs[:nd], refs[nd:2 * nd]
        oh_ref, gate_ref, an_ref, hn_ref, w_ref, gp_ref, x_ref = refs[2 * nd:2 * nd + 7]
        x1_ref, cat_ref, mixed_ref, attn_ref = refs[2 * nd + 7:2 * nd + 11]
        lse_refs = refs[2 * nd + 11:3 * nd + 11]
        o_scr, l_scr, lse_scr = refs[3 * nd + 11:]
        os_ = [_from_dilated(r, o_scr.at[k], d, tm) for k, (r, d) in enumerate(zip(o_refs, DILATIONS))]
        ls = [_from_dilated(r, l_scr.at[k], d, tm) for k, (r, d) in enumerate(zip(l_refs, DILATIONS))]
        m = jnp.maximum(jnp.maximum(ls[0], ls[1]), ls[2])
        es = [jnp.exp(l - m) for l in ls]
        den = es[0] + es[1] + es[2]
        attn = (es[0] * os_[0] + es[1] * os_[1] + es[2] * os_[2]) / den
        attn_ref[...] = attn
        _lane_blocks(lse_scr, m + jnp.log(den))
        for d, ref in zip(DILATIONS, lse_refs):
            _to_dilated(lse_scr, ref, d, tm)
        cat_ref[:, :ATTN_WIDTH] = _rms_fwd(attn, an_ref[...], ATTN_WIDTH).astype(BF16)
        gate = gate_ref[...]
        silu_g = gate * _sigmoid(gate)
        for h in range(HGRN_HEADS):
            hs = slice(h * hd, (h + 1) * hd)
            rec = _rms_fwd(oh_ref[:, hs], hn_ref[:, hs], hd) * silu_g[:, hs]
            cat_ref[:, ATTN_WIDTH + h * hd:ATTN_WIDTH + (h + 1) * hd] = rec.astype(BF16)
        mixed = _dot(cat_ref[...], w_ref[...])
        mixed_ref[...] = mixed
        x1_ref[...] = x_ref[...] + _rms_fwd(mixed, gp_ref[...], D_MODEL)

    aw = ATTN_WIDTH
    return pl.pallas_call(
        body,
        name="mix_fwd",
        grid=(s // tm,),
        in_specs=[_dilated_spec(d, tm, aw) for d in DILATIONS] * 2 + [
            _row_spec(tm, aw), _row_spec(tm, aw, gate_col), _const_spec(aw), _const_spec(aw), _vmem_spec(),
            _const_spec(D_MODEL), _row_spec(tm, D_MODEL)],
        out_specs=[_row_spec(tm, D_MODEL), _row_spec(tm, D_MODEL), _row_spec(tm, D_MODEL), _row_spec(tm, aw)] + [
            _dilated_spec(d, tm, aw) for d in DILATIONS],
        out_shape=[
            jax.ShapeDtypeStruct((s, D_MODEL), F32),
            jax.ShapeDtypeStruct((s, D_MODEL), BF16),
            jax.ShapeDtypeStruct((s, D_MODEL), F32),
            jax.ShapeDtypeStruct((s, aw), F32),
        ] + [jax.ShapeDtypeStruct((d, s // d, aw), F32) for d in DILATIONS],
        scratch_shapes=[pltpu.VMEM((nd, aw // LANES, tm, LANES), F32), pltpu.VMEM((nd, aw // LANES, tm, LANES), F32),
                        pltpu.VMEM((aw // LANES, tm, LANES), F32)],
        compiler_params=_params(dimension_semantics=("arbitrary",)),
    )(*[p[0] for p in attn_parts], *[p[1] for p in attn_parts], o_h, proj, an, hn, w_out_b, gp, x)


def mix_bwd(dx1, mixed, gp, w_out_b, attn, an, o_h, proj, hn):
    s = dx1.shape[0]
    tm = TOKEN_TILE
    gate_col = 3
    hd = HGRN_HEAD_DIM
    aw = ATTN_WIDTH

    nd = len(DILATIONS)

    def body(*refs):
        dx1_ref, mixed_ref, gp_ref, w_ref, attn_ref, an_ref, oh_ref, gate_ref, hn_ref, dmix_ref = refs[:10]
        do_refs, delta_refs = refs[10:10 + nd], refs[10 + nd:10 + 2 * nd]
        doh_ref, dgate_ref, dgp_ref, dan_ref, dhn_ref, do_ref, delta_ref = refs[10 + 2 * nd:]
        dmixed, gp_c = _rms_bwd(dx1_ref[...], mixed_ref[...], gp_ref[...], D_MODEL)
        _acc_rows(dgp_ref, gp_c)
        dmixed_b = dmixed.astype(BF16)
        dmix_ref[...] = dmixed_b
        dcat = _dot_nt(dmixed_b, w_ref[...])
        attn = attn_ref[...]
        d_o, an_c = _rms_bwd(dcat[:, :aw], attn, an_ref[...], aw)
        _acc_rows(dan_ref, an_c)
        _lane_blocks(do_ref, d_o)
        prod = d_o * attn
        for pair in range(ATTN_HEADS // 2):
            pp = prod[:, pair * LANES:(pair + 1) * LANES]
            low = _lane_half((tm, LANES), 0)
            lo = jnp.sum(jnp.where(low, pp, 0.0), axis=-1, keepdims=True)
            hi = jnp.sum(jnp.where(low, 0.0, pp), axis=-1, keepdims=True)
            delta_ref[pair] = jnp.where(low, lo, hi)
        for d, o_ref, l_ref in zip(DILATIONS, do_refs, delta_refs):
            _to_dilated(do_ref, o_ref, d, tm, cast=BF16)
            _to_dilated(delta_ref, l_ref, d, tm)
        gate = gate_ref[...]
        sg = _sigmoid(gate)
        silu_g = gate * sg
        drec = dcat[:, aw:]
        hn_parts = []
        for h in range(HGRN_HEADS):
            hs = slice(h * hd, (h + 1) * hd)
            oh = oh_ref[:, hs]
            on = _rms_fwd(oh, hn_ref[:, hs], hd)
            dgate_ref[:, hs] = (drec[:, hs] * on * (sg[:, hs] * (1.0 + gate[:, hs] * (1.0 - sg[:, hs])))).astype(BF16)
            d_oh, hn_c = _rms_bwd(drec[:, hs] * silu_g[:, hs], oh, hn_ref[:, hs], hd)
            doh_ref[:, hs] = d_oh
            hn_parts.append(hn_c)
        _acc_rows(dhn_ref, jnp.concatenate(hn_parts, axis=1))

    return pl.pallas_call(
        body,
        name="mix_bwd",
        grid=(s // tm,),
        in_specs=[_row_spec(tm, D_MODEL), _row_spec(tm, D_MODEL), _const_spec(D_MODEL), _vmem_spec(), _row_spec(tm, aw),
                  _const_spec(aw), _row_spec(tm, aw), _row_spec(tm, aw, gate_col), _const_spec(aw)],
        out_specs=[_row_spec(tm, D_MODEL)] + [_dilated_spec(d, tm, aw) for d in DILATIONS] * 2 + [_row_spec(tm, aw)] * 2 + [
            _const_spec(D_MODEL), _const_spec(aw), _const_spec(aw)],
        out_shape=[jax.ShapeDtypeStruct((s, D_MODEL), BF16)] + [
            jax.ShapeDtypeStruct((d, s // d, aw), BF16) for d in DILATIONS] + [
            jax.ShapeDtypeStruct((d, s // d, aw), F32) for d in DILATIONS] + [
            jax.ShapeDtypeStruct((s, aw), F32), jax.ShapeDtypeStruct((s, aw), BF16),
            jax.ShapeDtypeStruct((1, D_MODEL), F32), jax.ShapeDtypeStruct((1, aw), F32),
            jax.ShapeDtypeStruct((1, aw), F32)],
        scratch_shapes=[pltpu.VMEM((aw // LANES, tm, LANES), F32), pltpu.VMEM((aw // LANES, tm, LANES), F32)],
        compiler_params=_params(dimension_semantics=("arbitrary",)),
    )(dx1, mixed, gp, w_out_b, attn, an, o_h, proj, hn)


def mlp_fwd_bwd(x1, g_pre, w1_blocks, w2_b, g_post, target):
    s = x1.shape[0]
    tm = TOKEN_TILE
    nblk, _, fb = w1_blocks.shape

    def body(x1_ref, gpre_ref, w1_ref, w2_ref, gpost_ref, t_ref,
             dx1_ref, h2_ref, a_ref, du_ref, dff_ref, loss_ref, dgpre_ref, dgpost_ref, u_ref):
        x1v = x1_ref[...]
        h2 = _rms_fwd(x1v, gpre_ref[...], D_MODEL).astype(BF16)
        h2_ref[...] = h2
        ff = jnp.zeros((tm, D_MODEL), F32)
        for j in range(nblk):
            cols = slice(j * fb, (j + 1) * fb)
            ru = jnp.maximum(_dot(h2, w1_ref[j]), 0.0)
            u_ref[:, cols] = ru
            a = (ru * ru).astype(BF16)
            a_ref[:, cols] = a
            ff = ff + _dot(a, w2_ref[cols, :])
        diff = x1v + _rms_fwd(ff, gpost_ref[...], D_MODEL) - t_ref[...]
        _acc_rows(loss_ref, diff * diff)
        dy = diff * (1.0 / D_MODEL)
        dff, gpost_c = _rms_bwd(dy, ff, gpost_ref[...], D_MODEL)
        _acc_rows(dgpost_ref, gpost_c)
        dff_b = dff.astype(BF16)
        dff_ref[...] = dff_b
        dh2 = jnp.zeros((tm, D_MODEL), F32)
        for j in range(nblk):
            cols = slice(j * fb, (j + 1) * fb)
            du = (_dot_nt(dff_b, w2_ref[cols, :]) * (2.0 * u_ref[:, cols])).astype(BF16)
            du_ref[:, cols] = du
            dh2 = dh2 + _dot_nt(du, w1_ref[j])
        dxa, gpre_c = _rms_bwd(dh2, x1v, gpre_ref[...], D_MODEL)
        _acc_rows(dgpre_ref, gpre_c)
        dx1_ref[...] = dy + dxa

    dm = D_MODEL
    return pl.pallas_call(
        body,
        name="mlp_fwd_bwd",
        grid=(s // tm,),
        in_specs=[_row_spec(tm, dm), _const_spec(dm), _vmem_spec(), _vmem_spec(), _const_spec(dm), _row_spec(tm, dm)],
        out_specs=[_row_spec(tm, dm), _row_spec(tm, dm), _row_spec(tm, D_FF), _row_spec(tm, D_FF), _row_spec(tm, dm),
                   _const_spec(dm), _const_spec(dm), _const_spec(dm)],
        out_shape=[
            jax.ShapeDtypeStruct((s, dm), F32),
            jax.ShapeDtypeStruct((s, dm), BF16),
            jax.ShapeDtypeStruct((s, D_FF), BF16),
            jax.ShapeDtypeStruct((s, D_FF), BF16),
            jax.ShapeDtypeStruct((s, dm), BF16),
            jax.ShapeDtypeStruct((1, dm), F32),
            jax.ShapeDtypeStruct((1, dm), F32),
            jax.ShapeDtypeStruct((1, dm), F32),
        ],
        scratch_shapes=[pltpu.VMEM((tm, D_FF), F32)],
        compiler_params=_params(dimension_semantics=("arbitrary",)),
    )(x1, g_pre, w1_blocks, w2_b, g_post, target)


def in_proj_bwd(attn_grads, hgrn_grads, dgate, w_in_b, x, g1, dx1):
    s = x.shape[0]
    tm = PROJ_TILE
    aw = ATTN_WIDTH
    n_attn = len(attn_grads)
    flat = [g[k] for k in range(3) for g in attn_grads] + list(hgrn_grads) + [dgate]

    def body(*refs):
        parts = refs[:len(flat)]
        w_ref, x_ref, g_ref, dx1_ref, dx_ref, dproj_ref, dg_ref, scr = refs[len(flat):]
        groups = []
        for k in range(3):
            acc = None
            for p, d in zip(parts[k * n_attn:(k + 1) * n_attn], DILATIONS):
                v = _from_dilated(p, scr, d, tm)
                acc = v if acc is None else acc + v
            groups.append(acc)
        groups += [p[...] for p in parts[3 * n_attn:]]
        dh = jnp.zeros((tm, D_MODEL), F32)
        for gi, grp in enumerate(groups):
            cols = slice(gi * aw, (gi + 1) * aw)
            gb = grp.astype(BF16)
            dproj_ref[:, cols] = gb
            dh = dh + _dot_nt(gb, w_ref[:, cols])
        dxa, g_c = _rms_bwd(dh, x_ref[...], g_ref[...], D_MODEL)
        _acc_rows(dg_ref, g_c)
        dx_ref[...] = dx1_ref[...] + dxa

    dm = D_MODEL
    return pl.pallas_call(
        body,
        name="in_proj_bwd",
        grid=(s // tm,),
        in_specs=[_dilated_spec(d, tm, aw) for d in DILATIONS] * 3 + [_row_spec(tm, aw)] * 4 + [
            _vmem_spec(), _row_spec(tm, dm), _const_spec(dm), _row_spec(tm, dm)],
        out_specs=[_row_spec(tm, dm), _row_spec(tm, IN_PROJ_WIDTH), _const_spec(dm)],
        out_shape=[jax.ShapeDtypeStruct((s, dm), F32), jax.ShapeDtypeStruct((s, IN_PROJ_WIDTH), BF16),
                   jax.ShapeDtypeStruct((1, dm), F32)],
        scratch_shapes=[pltpu.VMEM((aw // LANES, tm, LANES), F32)],
        compiler_params=_params(dimension_semantics=("arbitrary",)),
    )(*flat, w_in_b, x, g1, dx1)


def wgrad(a_b, b_b, tn, name, ts=1024, per_step=1):
    s, k = a_b.shape
    n = b_b.shape[1]

    def body(a_ref, b_ref, o_ref):
        @pl.when(pl.program_id(1) == 0)
        def _():
            o_ref[...] = jnp.zeros_like(o_ref)

        a = a_ref[...]
        for jj in range(per_step):
            o_ref[jj] += _dot_tn(a, b_ref[:, jj * tn:(jj + 1) * tn])

    wide = tn * per_step
    return pl.pallas_call(
        body,
        name=name,
        grid=(n // wide, s // ts),
        in_specs=[pl.BlockSpec((ts, k), lambda j, i: (i, 0)), pl.BlockSpec((ts, wide), lambda j, i: (i, j))],
        out_specs=pl.BlockSpec((per_step, k, tn), lambda j, i: (j, 0, 0)),
        out_shape=jax.ShapeDtypeStruct((n // tn, k, tn), F32),
        compiler_params=_params(dimension_semantics=("arbitrary", "arbitrary")),
    )(a_b, b_b)


def train_step(x, target, g1, an, logits, hn, gp, g_pre, g_post, w, m, v):
    nd = len(DILATIONS)
    shard_b = {k: w[k].astype(BF16) for k in BIG}
    (w_in_g,) = run_exchange(gather_exchange([shard_b["w_in"]]), "gather_w_in")
    w_in_b = w_in_g.transpose(1, 0, 2).reshape(D_MODEL, IN_PROJ_WIDTH)

    proj, h_b, *qkvs = in_proj_fwd(x, g1, w_in_b)
    attn_parts = [attn_fwd(qkv, d) for qkv, d in zip(qkvs, DILATIONS)]
    o_h, states, a_mat, w_out_g, w1_blocks, w2_g = hgrn_fwd(
        proj, logits, ride=gather_exchange([shard_b["w_out"], shard_b["w_ff1"], shard_b["w_ff2"]]))
    w_out_b = w_out_g.reshape(D_MODEL, D_MODEL)
    w2_b = w2_g.reshape(D_FF, D_MODEL)
    x1, cat_b, mixed, attn, *lses = mix_fwd(attn_parts, o_h, proj, an, hn, w_out_b, gp, x)
    dx1, h2_b, a_b, du_b, dff_b, loss_vec, dg_pre, dg_post = mlp_fwd_bwd(x1, g_pre, w1_blocks, w2_b, g_post, target)
    dw2 = wgrad(a_b, dff_b, D_MODEL, "wgrad_ff2", ts=512)
    dw1 = wgrad(h2_b, du_b, D_FF // N_DEV, "wgrad_ff1", per_step=2)
    dmix_b, *rest = mix_bwd(dx1, mixed, gp, w_out_b, attn, an, o_h, proj, hn)
    d_os, deltas = rest[:nd], rest[nd:2 * nd]
    d_oh, dgate, dgp, dan, dhn = rest[2 * nd:]
    dwout = wgrad(cat_b, dmix_b, D_MODEL, "wgrad_out")

    early = ("w_out", "w_ff1", "w_ff2")
    early_grads = [dwout.reshape(N_DEV, D_MODEL // N_DEV, D_MODEL), dw1, dw2.reshape(N_DEV, D_FF // N_DEV, D_MODEL)]
    attn_grads = []
    for k, d in enumerate(DILATIONS):
        ride = to_core_exchange(early_grads) if k == 0 else None
        res = attn_bwd(qkvs[k], d_os[k], lses[k], deltas[k], d, ride=ride)
        attn_grads.append(res[:3])
        if k == 0:
            pairs = [pair_sum(g, s, f"pair_sum_{name}") for g, s, name in zip(early_grads, res[3:], early)]
    dq_h, df_h, di_h, dlb, *others = hgrn_bwd(proj, logits, d_oh, states, a_mat,
                                              ride=to_chip_exchange([p[1] for p in pairs]))
    dx, dproj_b, dg1 = in_proj_bwd(attn_grads, (dq_h, df_h, di_h), dgate, w_in_b, x, g1, dx1)
    dwin = wgrad(h_b, dproj_b, 2 * IN_PROJ_WIDTH // N_DEV, "wgrad_in")
    big = {name: sum_adamw(p[0], o, w[name], m[name], v[name], f"sum_adamw_{name}")
           for name, p, o in zip(early, pairs, others)}

    shard_w = IN_PROJ_WIDTH // N_DEV
    dwin_blocks = dwin.reshape(N_DEV // 2, D_MODEL, 2, shard_w).transpose(0, 2, 1, 3).reshape(N_DEV, D_MODEL, shard_w)
    (from_sibling,) = run_exchange(to_core_exchange([dwin_blocks]), "reduce_w_in_to_core")
    pair_in, pair_in_b = pair_sum(dwin_blocks, from_sibling, "pair_sum_w_in")
    (others_in,) = run_exchange(to_chip_exchange([pair_in_b]), "reduce_w_in_to_chip")
    big["w_in"] = sum_adamw(pair_in, others_in, w["w_in"], m["w_in"], v["w_in"], "sum_adamw_w_in")
    small = dict(dg1=dg1, dan=dan, dlb=dlb, dhn=dhn, dgp=dgp, dg_pre=dg_pre, dg_post=dg_post, loss_vec=loss_vec)
    return dx, big, small


def _position():
    x, y, c = lax.axis_index("x"), lax.axis_index("y"), lax.axis_index("c")
    other_chips = [(1 - x, y), (x, 1 - y), (1 - x, 1 - y)]
    return x, y, c, other_chips


def _any_spec():
    return pl.BlockSpec(memory_space=pl.ANY)


class Exchange:
    def __init__(self, arrays, out_shape, sems, stages):
        self.arrays, self.out_shape, self.sems, self.stages = list(arrays), list(out_shape), list(sems), stages


def gather_exchange(shards):
    n = len(shards)

    def stages(ins, outs, sems):
        send_sems, recv_sems, local_sems = sems

        def parts():
            x, y, c, chips = _position()
            me, sibling = (x, y, c), (x, y, 1 - c)

            def slot(a, px, py, pc):
                return outs[a].at[4 * px + 2 * py + pc]

            def copy(a, k, block, to, src=None):
                return pltpu.make_async_remote_copy(
                    src_ref=slot(a, *block) if src is None else src, dst_ref=slot(a, *block),
                    send_sem=send_sems.at[a, k], recv_sem=recv_sems.at[a, k], device_id=to, device_id_type=MESH)

            local = [pltpu.make_async_copy(ins[a], slot(a, *me), local_sems.at[a]) for a in range(n)]
            first = []
            for a in range(n):
                first.append(copy(a, 0, me, sibling, src=ins[a]))
                first += [copy(a, 1 + j, me, (*chip, c), src=ins[a]) for j, chip in enumerate(chips)]
            passed = [copy(a, 4 + j, (*chip, c), sibling) for j, chip in enumerate(chips) for a in range(n)]
            return c, chips, me, sibling, copy, local, first, passed

        def begin():
            _, _, _, _, _, local, first, _ = parts()
            for cp in local + first:
                cp.start()

        def middle():
            c, chips, me, _, copy, _, _, passed = parts()
            k = 0
            for j, chip in enumerate(chips):
                for a in range(n):
                    copy(a, 1 + j, (*chip, c), me).wait_recv()
                    passed[k].start()
                    k += 1

        def end():
            c, chips, me, sibling, copy, local, first, passed = parts()
            for a in range(n):
                copy(a, 0, sibling, me).wait_recv()
                for j, chip in enumerate(chips):
                    copy(a, 4 + j, (*chip, 1 - c), me).wait_recv()
            for cp in first + passed:
                cp.wait_send()
            for cp in local:
                cp.wait()

        return begin, middle, end

    return Exchange(
        shards, [jax.ShapeDtypeStruct((N_DEV,) + sh.shape, sh.dtype) for sh in shards],
        [pltpu.SemaphoreType.DMA((n, 7)), pltpu.SemaphoreType.DMA((n, 7)), pltpu.SemaphoreType.DMA((n,))], stages)


def to_core_exchange(grads):
    n = len(grads)

    def stages(ins, outs, sems):
        send_sems, recv_sems = sems

        def copies():
            x, y, c, _ = _position()
            return [pltpu.make_async_remote_copy(
                src_ref=ins[a].at[2 * q + (1 - c)], dst_ref=outs[a].at[q], send_sem=send_sems.at[a, q],
                recv_sem=recv_sems.at[a, q], device_id=(x, y, 1 - c), device_id_type=MESH)
                for a in range(n) for q in range(4)]

        def begin():
            for cp in copies():
                cp.start()

        def end():
            for cp in copies():
                cp.wait()

        return begin, None, end

    return Exchange(grads, [jax.ShapeDtypeStruct((4,) + g.shape[1:], g.dtype) for g in grads],
                    [pltpu.SemaphoreType.DMA((n, 4)), pltpu.SemaphoreType.DMA((n, 4))], stages)


def pair_sum(grad, from_sibling, name):
    _, r, cdim = grad.shape
    tr = min(r, 256)
    c_idx = lax.axis_index("c").astype(jnp.int32).reshape(1)

    def body(c_ref, g_ref, s_ref, o_ref, ob_ref):
        total = g_ref[...] + s_ref[...]
        o_ref[...] = total
        ob_ref[...] = total.astype(BF16)

    blk = lambda: pl.BlockSpec((1, tr, cdim), lambda q, i, cr: (q, i, 0))
    return pl.pallas_call(
        body,
        name=name,
        grid_spec=pltpu.PrefetchScalarGridSpec(
            num_scalar_prefetch=1,
            grid=(4, r // tr),
            in_specs=[pl.BlockSpec((1, tr, cdim), lambda q, i, cr: (2 * q + cr[0], i, 0)), blk()],
            out_specs=[blk(), blk()],
        ),
        out_shape=[jax.ShapeDtypeStruct((4, r, cdim), F32), jax.ShapeDtypeStruct((4, r, cdim), BF16)],
        compiler_params=_params(dimension_semantics=("arbitrary", "arbitrary")),
    )(c_idx, grad, from_sibling)


def to_chip_exchange(pairs):
    n = len(pairs)

    def stages(ins, outs, sems):
        send_sems, recv_sems = sems

        def copies():
            x, y, c, chips = _position()
            return [pltpu.make_async_remote_copy(
                src_ref=ins[a].at[2 * px + py], dst_ref=outs[a].at[j], send_sem=send_sems.at[a, j],
                recv_sem=recv_sems.at[a, j], device_id=(px, py, c), device_id_type=MESH)
                for a in range(n) for j, (px, py) in enumerate(chips)]

        def begin():
            for cp in copies():
                cp.start()

        def end():
            for cp in copies():
                cp.wait()

        return begin, None, end

    return Exchange(pairs, [jax.ShapeDtypeStruct((3,) + p.shape[1:], p.dtype) for p in pairs],
                    [pltpu.SemaphoreType.DMA((n, 3)), pltpu.SemaphoreType.DMA((n, 3))], stages)


def run_exchange(ex, name):
    n_in, n_out = len(ex.arrays), len(ex.out_shape)

    def body(*refs):
        begin, middle, end = ex.stages(refs[:n_in], refs[n_in:n_in + n_out], refs[n_in + n_out:])
        begin()
        if middle is not None:
            middle()
        end()

    return pl.pallas_call(
        body,
        name=name,
        in_specs=[_any_spec()] * n_in,
        out_specs=[_any_spec()] * n_out,
        out_shape=ex.out_shape,
        scratch_shapes=ex.sems,
    )(*ex.arrays)


def _riding(body, n_in, n_out, n_scratch, ex, first, middle, last):
    if ex is None:
        return body
    r_in, r_out = len(ex.arrays), len(ex.out_shape)

    def wrapped(*refs):
        k_in, refs = refs[:n_in], refs[n_in:]
        e_in, refs = refs[:r_in], refs[r_in:]
        k_out, refs = refs[:n_out], refs[n_out:]
        e_out, refs = refs[:r_out], refs[r_out:]
        k_scr, e_sems = refs[:n_scratch], refs[n_scratch:]
        begin, mid, end = ex.stages(e_in, e_out, e_sems)
        pl.when(first())(begin)
        body(*k_in, *k_out, *k_scr)
        if mid is not None:
            pl.when(middle())(mid)
        pl.when(last())(end)

    return wrapped


def _ride_specs(ex):
    if ex is None:
        return [], [], [], [], []
    return [_any_spec()] * len(ex.arrays), [_any_spec()] * len(ex.out_shape), ex.out_shape, ex.sems, ex.arrays


def _adamw(w, g, m, v):
    m = ADAM_B1 * m + (1.0 - ADAM_B1) * g
    v = ADAM_B2 * v + (1.0 - ADAM_B2) * (g * g)
    m_hat = m / (1.0 - ADAM_B1 ** ADAM_STEP)
    v_hat = v / (1.0 - ADAM_B2 ** ADAM_STEP)
    delta = -ADAM_LR * (m_hat / (jnp.sqrt(v_hat) + ADAM_EPS) + ADAM_WD * w)
    return delta, m, v


def sum_adamw(pairs, others, w, m, v, name):
    r, cdim = w.shape
    tr = min(r, 256)
    chip_idx = (2 * lax.axis_index("x") + lax.axis_index("y")).astype(jnp.int32).reshape(1)

    def body(q_ref, p_ref, o_ref, w_ref, m_ref, v_ref, g_out, d_out, m_out, v_out):
        g = p_ref[0] + o_ref[0].astype(F32) + o_ref[1].astype(F32) + o_ref[2].astype(F32)
        g_out[...] = g
        d_out[...], m_out[...], v_out[...] = _adamw(w_ref[...], g, m_ref[...], v_ref[...])

    tile = lambda: pl.BlockSpec((tr, cdim), lambda i, qr: (i, 0))
    return pl.pallas_call(
        body,
        name=name,
        grid_spec=pltpu.PrefetchScalarGridSpec(
            num_scalar_prefetch=1,
            grid=(r // tr,),
            in_specs=[pl.BlockSpec((1, tr, cdim), lambda i, qr: (qr[0], i, 0)),
                      pl.BlockSpec((3, tr, cdim), lambda i, qr: (0, i, 0)), tile(), tile(), tile()],
            out_specs=[tile(), tile(), tile(), tile()],
        ),
        out_shape=[jax.ShapeDtypeStruct((r, cdim), F32)] * 4,
        compiler_params=_params(dimension_semantics=("arbitrary",)),
    )(chip_idx, pairs, others, w, m, v)


SMALL_ROWS = 8


def small_all_reduce(packed):
    shape = packed.shape

    def body(in_ref, out_ref, recv_ref, send_sems, recv_sems):
        x, y, c, _ = _position()
        my_id = 4 * x + 2 * y + c
        recv_ref[my_id] = in_ref[...]
        copies = []
        for rel in range(1, N_DEV):
            fx, fy, fc = (rel >> 2) & 1, (rel >> 1) & 1, rel & 1
            px = 1 - x if fx else x
            py = 1 - y if fy else y
            pc = 1 - c if fc else c
            cp = pltpu.make_async_remote_copy(
                src_ref=in_ref, dst_ref=recv_ref.at[my_id], send_sem=send_sems.at[rel - 1],
                recv_sem=recv_sems.at[rel - 1], device_id=(px, py, pc), device_id_type=MESH)
            cp.start()
            copies.append((cp, pltpu.make_async_remote_copy(
                src_ref=in_ref, dst_ref=recv_ref.at[4 * px + 2 * py + pc], send_sem=send_sems.at[rel - 1],
                recv_sem=recv_sems.at[rel - 1], device_id=(px, py, pc), device_id_type=MESH)))
        for cp, landing in copies:
            landing.wait_recv()
        for cp, landing in copies:
            cp.wait_send()
        total = recv_ref[0]
        for k in range(1, N_DEV):
            total = total + recv_ref[k]
        out_ref[...] = total

    return pl.pallas_call(
        body,
        name="small_all_reduce",
        in_specs=[_vmem_spec()],
        out_specs=_vmem_spec(),
        out_shape=jax.ShapeDtypeStruct(shape, F32),
        scratch_shapes=[pltpu.VMEM((N_DEV,) + shape, F32), pltpu.SemaphoreType.DMA((N_DEV - 1,)),
                        pltpu.SemaphoreType.DMA((N_DEV - 1,))],
    )(packed)


def small_adamw(reduced, w, m, v):
    def body(r_ref, w_ref, m_ref, v_ref, g_out, d_out, m_out, v_out, loss_out):
        red = r_ref[...]
        wv = w_ref[...]
        lb = _lower_bound(jnp.concatenate([wv[5:6, :HGRN_WIDTH], wv[5:6, HGRN_WIDTH:]], axis=0))
        t = red[5:6, :HGRN_WIDTH] * lb * (1.0 - lb)
        row = lax.broadcasted_iota(jnp.int32, red.shape, 0)
        g = jnp.where(row == 5, jnp.concatenate([t, -t], axis=1), jnp.where(row >= 6, 0.0, red))
        g_out[...] = g
        d_out[...], m_out[...], v_out[...] = _adamw(wv, g, m_ref[...], v_ref[...])
        loss = jnp.sum(red[6:7, :], axis=-1, keepdims=True) * (0.5 / D_MODEL)
        loss_out[...] = jnp.broadcast_to(loss, loss_out.shape)

    return pl.pallas_call(
        body,
        name="small_adamw",
        in_specs=[_vmem_spec()] * 4,
        out_specs=[_vmem_spec()] * 5,
        out_shape=[jax.ShapeDtypeStruct(reduced.shape, F32)] * 4 + [jax.ShapeDtypeStruct((8, 128), F32)],
    )(reduced, w, m, v)


def _pack_small(g1, gp, g_pre, g_post, an, hn, logits_or_dlb, extra=None):
    row5 = logits_or_dlb.reshape(1, -1)
    row5 = jnp.pad(row5, ((0, 0), (0, D_MODEL - row5.shape[1])))
    row6 = jnp.zeros((1, D_MODEL), F32) if extra is None else extra
    return jnp.concatenate([g1, gp, g_pre, g_post, jnp.concatenate([an, hn], axis=1), row5, row6,
                            jnp.zeros((1, D_MODEL), F32)], axis=0)


def _unpack_small(p):
    return dict(mix_pre_norm=p[0:1], mix_post_norm=p[1:2], mlp_pre_norm=p[2:3], mlp_post_norm=p[3:4],
                attn_out_norm=p[4:5, :ATTN_WIDTH], hgrn_out_norm=p[4:5, ATTN_WIDTH:],
                hgrn_lb_logits=p[5].reshape(2, HGRN_WIDTH))


BIG = ("w_in", "w_out", "w_ff1", "w_ff2")
ORDER = ("mix_pre_norm", "w_in", "attn_out_norm", "hgrn_lb_logits", "hgrn_out_norm", "w_out", "mix_post_norm",
         "mlp_pre_norm", "w_ff1", "w_ff2", "mlp_post_norm")


def kernel(x, mix_pre_norm, w_in, attn_out_norm, hgrn_lb_logits, hgrn_out_norm, w_out, mix_post_norm, mlp_pre_norm, w_ff1, w_ff2, mlp_post_norm, loss_target, m_mix_pre_norm, m_w_in, m_attn_out_norm, m_hgrn_lb_logits, m_hgrn_out_norm, m_w_out, m_mix_post_norm, m_mlp_pre_norm, m_w_ff1, m_w_ff2, m_mlp_post_norm, v_mix_pre_norm, v_w_in, v_attn_out_norm, v_hgrn_lb_logits, v_hgrn_out_norm, v_w_out, v_mix_post_norm, v_mlp_pre_norm, v_w_ff1, v_w_ff2, v_mlp_post_norm):
    w = dict(w_in=w_in[0], w_out=w_out[0], w_ff1=w_ff1[0], w_ff2=w_ff2[0])
    m = dict(w_in=m_w_in[0], w_out=m_w_out[0], w_ff1=m_w_ff1[0], w_ff2=m_w_ff2[0])
    v = dict(w_in=v_w_in[0], w_out=v_w_out[0], w_ff1=v_w_ff1[0], w_ff2=v_w_ff2[0])

    dx, big, small = train_step(x[0], loss_target[0], mix_pre_norm, attn_out_norm, hgrn_lb_logits, hgrn_out_norm,
                                mix_post_norm, mlp_pre_norm, mlp_post_norm, w, m, v)

    packed_g = _pack_small(small["dg1"], small["dgp"], small["dg_pre"], small["dg_post"], small["dan"], small["dhn"],
                           small["dlb"], small["loss_vec"])
    reduced = small_all_reduce(packed_g)
    pack = lambda a, b, c2, d, e, f, g: _pack_small(a, b, c2, d, e, f, g)
    w_s = pack(mix_pre_norm, mix_post_norm, mlp_pre_norm, mlp_post_norm, attn_out_norm, hgrn_out_norm, hgrn_lb_logits)
    m_s = pack(m_mix_pre_norm, m_mix_post_norm, m_mlp_pre_norm, m_mlp_post_norm, m_attn_out_norm, m_hgrn_out_norm,
               m_hgrn_lb_logits)
    v_s = pack(v_mix_pre_norm, v_mix_post_norm, v_mlp_pre_norm, v_mlp_post_norm, v_attn_out_norm, v_hgrn_out_norm,
               v_hgrn_lb_logits)
    g_s, d_s, nm_s, nv_s, loss = small_adamw(reduced, w_s, m_s, v_s)
    small_out = [_unpack_small(t) for t in (g_s, d_s, nm_s, nv_s)]

    outs = [loss[0, 0], dx[None]]
    for kind in range(4):
        for name in ORDER:
            outs.append(big[name][kind][None] if name in BIG else small_out[kind][name])
    return tuple(outs)
```

```python
import functools
import math

import jax
import jax.numpy as jnp
from jax import lax
from jax.experimental import pallas as pl
from jax.experimental.pallas import tpu as pltpu

F32 = jnp.float32
BF16 = jnp.bfloat16

D_MODEL = 1024
SEQ = 4096
ATTN_WIDTH = 512
ATTN_HEAD_DIM = 64
ATTN_HEADS = 8
ATTN_BLOCK = 128
DILATIONS = (1, 4, 16)
HGRN_WIDTH = 512
HGRN_HEADS = 4
HGRN_HEAD_DIM = 128
HGRN_CHUNK = 64
IN_PROJ_WIDTH = 3584
D_FF = 4096
RMS_EPS = 1e-6
N_DEV = 8
ADAM_LR = 0.001
ADAM_B1 = 0.9
ADAM_B2 = 0.999
ADAM_EPS = 1e-08
ADAM_WD = 0.01
ADAM_STEP = 10

SUBLANES = 8
LANES = 128
COLUMN_UNROLL = 8
SUB_BLOCK = 16
TOKEN_TILE = 256
PROJ_TILE = 512
VMEM_LIMIT = 56 * 1024 * 1024
NEG_BIG = -1e30
MESH = pl.DeviceIdType.MESH


def _params(**kw):
    return pltpu.CompilerParams(vmem_limit_bytes=VMEM_LIMIT, **kw)


def _vmem_spec():
    return pl.BlockSpec(memory_space=pltpu.VMEM)


def _dot(a, b):
    return jnp.dot(a, b, preferred_element_type=F32)


def _dot_nt(a, b):
    return lax.dot_general(a, b, (((1,), (1,)), ((), ())), preferred_element_type=F32)


def _dot_tn(a, b):
    return lax.dot_general(a, b, (((0,), (0,)), ((), ())), preferred_element_type=F32)


def _sigmoid(x):
    return 1.0 / (1.0 + jnp.exp(-x))


def _rms_fwd(x, gain, width):
    r = lax.rsqrt(jnp.sum(x * x, axis=-1, keepdims=True) * (1.0 / width) + RMS_EPS)
    return x * r * gain


def _rms_bwd(dy, x, gain, width):
    r = lax.rsqrt(jnp.sum(x * x, axis=-1, keepdims=True) * (1.0 / width) + RMS_EPS)
    xhat = x * r
    dxhat = dy * gain
    dx = r * (dxhat - xhat * (jnp.sum(dxhat * xhat, axis=-1, keepdims=True) * (1.0 / width)))
    return dx, dy * xhat


def _split3(x):
    hi = x.astype(BF16)
    r1 = x - hi.astype(F32)
    mid = r1.astype(BF16)
    lo = (r1 - mid.astype(F32)).astype(BF16)
    return hi, mid, lo


def _tri_sum(tri_bf16, x):
    hi, mid, lo = _split3(x)
    return _dot(tri_bf16, hi) + _dot(tri_bf16, mid) + _dot(tri_bf16, lo)


def _dilated_spec(d, tm, width):
    return pl.BlockSpec((d, tm // d, width), lambda i: (0, i, 0))


def _lane_blocks(ref, value):
    for c in range(ref.shape[0]):
        ref[c] = value[:, c * LANES:(c + 1) * LANES]


def _to_dilated(src_ref, dst_ref, d, tm, cast=None):
    for r in range(d):
        for c in range(src_ref.shape[0]):
            v = src_ref[c] if d == 1 else src_ref[c, pl.ds(r, tm // d, stride=d), :]
            dst_ref[r, :, c * LANES:(c + 1) * LANES] = v if cast is None else v.astype(cast)


def _from_dilated(src_ref, scratch_ref, d, tm):
    if d == 1:
        return src_ref[0].astype(F32)
    nblk = scratch_ref.shape[0]
    for r in range(d):
        for c in range(nblk):
            scratch_ref[c, pl.ds(r, tm // d, stride=d), :] = src_ref[r, :, c * LANES:(c + 1) * LANES].astype(F32)
    return jnp.concatenate([scratch_ref[c] for c in range(nblk)], axis=1)


def in_proj_fwd(x, g1, w_in_b):
    s = x.shape[0]
    tm = PROJ_TILE
    qkv_w = 3 * ATTN_WIDTH
    hg_w = IN_PROJ_WIDTH - qkv_w

    def body(x_ref, g_ref, w_ref, hg_ref, h_ref, *rest):
        qkv_refs, qkv_scr = rest[:len(DILATIONS)], rest[len(DILATIONS)]
        h = _rms_fwd(x_ref[...], g_ref[...], D_MODEL).astype(BF16)
        h_ref[...] = h
        proj = _dot(h, w_ref[...])
        hg_ref[...] = proj[:, qkv_w:]
        _lane_blocks(qkv_scr, proj[:, :qkv_w])
        for d, ref in zip(DILATIONS, qkv_refs):
            _to_dilated(qkv_scr, ref, d, tm, cast=BF16)

    return pl.pallas_call(
        body,
        name="in_proj_fwd",
        grid=(s // tm,),
        in_specs=[
            pl.BlockSpec((tm, D_MODEL), lambda i: (i, 0)),
            pl.BlockSpec((1, D_MODEL), lambda i: (0, 0)),
            _vmem_spec(),
        ],
        out_specs=[
            pl.BlockSpec((tm, hg_w), lambda i: (i, 0)),
            pl.BlockSpec((tm, D_MODEL), lambda i: (i, 0)),
        ] + [_dilated_spec(d, tm, qkv_w) for d in DILATIONS],
        out_shape=[jax.ShapeDtypeStruct((s, hg_w), F32), jax.ShapeDtypeStruct((s, D_MODEL), BF16)] + [
            jax.ShapeDtypeStruct((d, s // d, qkv_w), BF16) for d in DILATIONS],
        scratch_shapes=[pltpu.VMEM((qkv_w // LANES, tm, LANES), F32)],
        compiler_params=_params(dimension_semantics=("arbitrary",)),
    )(x, g1, w_in_b)


ATTN_SCALE = ATTN_HEAD_DIM ** -0.5


def _fill_attn_bias(bias_ref, dilation):
    qi = lax.broadcasted_iota(jnp.int32, (ATTN_BLOCK, 2 * ATTN_BLOCK), 0)
    kj = lax.broadcasted_iota(jnp.int32, (ATTN_BLOCK, 2 * ATTN_BLOCK), 1)
    dist = qi + ATTN_BLOCK - kj
    valid = (dist >= 0) & (dist <= ATTN_BLOCK)
    for head in range(ATTN_HEADS):
        slope = 2.0 ** (-8.0 * (head + 1) / ATTN_HEADS)
        bias = jnp.where(valid, dist.astype(F32) * (-slope * dilation), NEG_BIG)
        bias_ref[0, head] = bias
        bias_ref[1, head] = jnp.where(kj >= ATTN_BLOCK, bias, NEG_BIG)


def _attn_scores(qm, kcat, bias_ref, head, first_block):
    return _dot_nt(qm, kcat) + bias_ref[first_block.astype(jnp.int32), head]


def _lane_half(shape, sub):
    lane = lax.broadcasted_iota(jnp.int32, shape, 1)
    return (lane < ATTN_HEAD_DIM) if sub == 0 else (lane >= ATTN_HEAD_DIM)


def _sub_block(col, row):
    return pl.BlockSpec((None, ATTN_BLOCK, ATTN_WIDTH), lambda r, n: (r, row(n), col))


def attn_fwd(qkv, dilation):
    d, length, _ = qkv.shape
    assert d == dilation
    nb = length // ATTN_BLOCK

    def body(q_ref, kc_ref, kp_ref, vc_ref, vp_ref, o_ref, lse_ref, bias_ref):
        @pl.when((pl.program_id(0) == 0) & (pl.program_id(1) == 0))
        def _():
            _fill_attn_bias(bias_ref, d)

        first = pl.program_id(1) == 0
        for pair in range(ATTN_HEADS // 2):
            lanes = slice(pair * 128, (pair + 1) * 128)
            q2 = q_ref[:, lanes] * ATTN_SCALE
            kcat = jnp.concatenate([kp_ref[:, lanes], kc_ref[:, lanes]], axis=0)
            vcat = jnp.concatenate([vp_ref[:, lanes], vc_ref[:, lanes]], axis=0)
            o_pair = jnp.zeros((ATTN_BLOCK, 128), F32)
            lse_pair = jnp.zeros((ATTN_BLOCK, 128), F32)
            for sub in range(2):
                keep = _lane_half((ATTN_BLOCK, 128), sub)
                qm = jnp.where(keep, q2, jnp.zeros_like(q2))
                sc = _attn_scores(qm, kcat, bias_ref, 2 * pair + sub, first)
                m = jnp.max(sc, axis=-1, keepdims=True)
                p = jnp.exp(sc - m)
                den = jnp.sum(p, axis=-1, keepdims=True)
                o = _dot(p.astype(BF16), vcat) / den
                o_pair = jnp.where(keep, o, o_pair)
                lse_pair = jnp.where(keep, m + jnp.log(den), lse_pair)
            o_ref[:, lanes] = o_pair.astype(BF16)
            lse_ref[:, lanes] = lse_pair

    cur = lambda n: n
    prev = lambda n: jnp.maximum(n - 1, 0)
    return pl.pallas_call(
        body,
        name=f"attn_fwd_d{d}",
        grid=(d, nb),
        in_specs=[_sub_block(0, cur), _sub_block(1, cur), _sub_block(1, prev), _sub_block(2, cur), _sub_block(2, prev)],
        out_specs=[_sub_block(0, cur), _sub_block(0, cur)],
        out_shape=[jax.ShapeDtypeStruct((d, length, ATTN_WIDTH), BF16), jax.ShapeDtypeStruct((d, length, ATTN_WIDTH), F32)],
        scratch_shapes=[pltpu.VMEM((2, ATTN_HEADS, ATTN_BLOCK, 2 * ATTN_BLOCK), F32)],
        compiler_params=_params(dimension_semantics=("arbitrary", "arbitrary")),
    )(qkv, qkv, qkv, qkv, qkv)


def attn_bwd(qkv, d_out, lse, delta, dilation, ride=None):
    d, length, _ = qkv.shape
    assert d == dilation
    nb = length // ATTN_BLOCK

    steps = d * nb + 1

    def body(q_ref, kc_ref, kp_ref, vc_ref, vp_ref, do_ref, lse_ref, dl_ref, dq_ref, dk_ref, dv_ref, ck_ref, cv_ref,
             bias_ref):
        t = pl.program_id(0)

        @pl.when(t == 0)
        def _():
            ck_ref[...] = jnp.zeros_like(ck_ref)
            cv_ref[...] = jnp.zeros_like(cv_ref)
            _fill_attn_bias(bias_ref, d)

        @pl.when(t < steps - 1)
        def _():
            first = t % nb == 0
            for pair in range(ATTN_HEADS // 2):
                lanes = slice(pair * 128, (pair + 1) * 128)
                q2 = q_ref[:, lanes] * ATTN_SCALE
                do2 = do_ref[:, lanes]
                kcat = jnp.concatenate([kp_ref[:, lanes], kc_ref[:, lanes]], axis=0)
                vcat = jnp.concatenate([vp_ref[:, lanes], vc_ref[:, lanes]], axis=0)
                dq_pair = jnp.zeros((ATTN_BLOCK, 128), F32)
                dk_cat = jnp.zeros((2 * ATTN_BLOCK, 128), F32)
                dv_cat = jnp.zeros((2 * ATTN_BLOCK, 128), F32)
                for sub in range(2):
                    keep = _lane_half((ATTN_BLOCK, 128), sub)
                    col = pair * 128 + sub * ATTN_HEAD_DIM
                    qm = jnp.where(keep, q2, jnp.zeros_like(q2))
                    dom = jnp.where(keep, do2, jnp.zeros_like(do2))
                    sc = _attn_scores(qm, kcat, bias_ref, 2 * pair + sub, first)
                    p = jnp.exp(sc - lse_ref[:, col:col + 1])
                    dp = _dot_nt(dom, vcat)
                    ds = (p * (dp - dl_ref[:, col:col + 1])).astype(BF16)
                    dq_pair = jnp.where(keep, _dot(ds, kcat), dq_pair)
                    dk_cat = dk_cat + _dot_tn(ds, qm)
                    dv_cat = dv_cat + _dot_tn(p.astype(BF16), dom)
                dq_ref[:, lanes] = (dq_pair * ATTN_SCALE).astype(BF16)
                dk_ref[:, lanes] = (ck_ref[:, lanes] + dk_cat[:ATTN_BLOCK]).astype(BF16)
                dv_ref[:, lanes] = (cv_ref[:, lanes] + dv_cat[:ATTN_BLOCK]).astype(BF16)
                ck_ref[:, lanes] = dk_cat[ATTN_BLOCK:]
                cv_ref[:, lanes] = dv_cat[ATTN_BLOCK:]

        @pl.when(t == steps - 1)
        def _():
            dk_ref[...] = ck_ref[...].astype(BF16)
            dv_ref[...] = cv_ref[...].astype(BF16)

    blk = (ATTN_BLOCK, ATTN_WIDTH)

    def spec(col, shift):
        def index(t):
            f = jnp.minimum(t, steps - 2) if shift > -2 else jnp.maximum(t - 1, 0)
            r, n = f // nb, f % nb
            return (r, jnp.maximum(n - 1, 0) if shift == -1 else n, col)
        return pl.BlockSpec((None, ATTN_BLOCK, ATTN_WIDTH), index)

    step = lambda k: (lambda: pl.program_id(0) == k)
    e_in, e_out, e_shape, e_scr, e_args = _ride_specs(ride)
    return pl.pallas_call(
        _riding(body, 8, 3, 3, ride, step(0), step(steps // 2), step(steps - 1)),
        name=f"attn_bwd_d{d}",
        grid=(steps,),
        in_specs=[spec(0, 0), spec(1, 0), spec(1, -1), spec(2, 0), spec(2, -1), spec(0, 0), spec(0, 0), spec(0, 0)] + e_in,
        out_specs=[spec(0, 0), spec(0, -2), spec(0, -2)] + e_out,
        out_shape=[jax.ShapeDtypeStruct((d, length, ATTN_WIDTH), BF16)] * 3 + e_shape,
        scratch_shapes=[pltpu.VMEM(blk, F32), pltpu.VMEM(blk, F32),
                        pltpu.VMEM((2, ATTN_HEADS, ATTN_BLOCK, 2 * ATTN_BLOCK), F32)] + e_scr,
        compiler_params=_params(dimension_semantics=("arbitrary",)),
    )(qkv, qkv, qkv, qkv, qkv, d_out, lse, delta, *e_args)


def _lower_bound(logits):
    return _sigmoid(logits[0:1, :] - logits[1:2, :])


def _hgrn_gates(q, fp, lb):
    sq = _sigmoid(q)
    qf = q * sq
    sig = _sigmoid(fp)
    f = lb + (1.0 - lb) * sig
    kf = (1.0 - lb) * _sigmoid(-fp)
    return sq, qf, sig, f, kf


def _tril_bf16(n, upper=False):
    r = lax.broadcasted_iota(jnp.int32, (n, n), 0)
    c = lax.broadcasted_iota(jnp.int32, (n, n), 1)
    keep = (c >= r) if upper else (c <= r)
    return jnp.where(keep, 1.0, 0.0).astype(BF16)


def _hgrn_diagonal_loops(c_len, diagonal):
    for half in range(SUB_BLOCK // SUBLANES):
        def step(jj, carry, half=half):
            j = half * SUBLANES + jj
            for i in range(c_len // SUB_BLOCK):
                diagonal(slice(i * SUB_BLOCK + half * SUBLANES, (i + 1) * SUB_BLOCK), j, i * SUB_BLOCK + j)
            return carry

        lax.fori_loop(0, SUBLANES, step, 0, unroll=COLUMN_UNROLL)


def _hgrn_off_diagonal(b, qf, kf):
    c_len, width = b.shape
    edges = [b[0:1, :]] + [b[i * SUB_BLOCK - 1:i * SUB_BLOCK, :] for i in range(1, c_len // SUB_BLOCK)]
    eq = jnp.exp(b - jnp.concatenate([jnp.broadcast_to(e, (SUB_BLOCK, width)) for e in edges], axis=0))
    q_til = qf * eq
    k_til, ek = [], []
    for i in range(1, c_len // SUB_BLOCK):
        n = i * SUB_BLOCK
        e = jnp.exp(edges[i] - b[:n, :])
        ek.append(e)
        k_til.append(jnp.concatenate([kf[:n, :] * e, jnp.zeros((2 * c_len - n, width), F32)], axis=0))
    return q_til, k_til, eq, ek


def _split2(x):
    hi = x.astype(BF16)
    return hi, (x - hi.astype(F32)).astype(BF16)


def hgrn_fwd(proj, lb, ride=None):
    s = proj.shape[0]
    c_len, nh, hd = HGRN_CHUNK, HGRN_HEADS, HGRN_HEAD_DIM
    n_chunks = s // c_len
    col0 = 0

    def body(q_ref, f_ref, i_ref, lb_ref, o_ref, st_out_ref, a_out_ref, st_ref, b_ref, qf_ref, kf_ref, a_ref):
        @pl.when(pl.program_id(0) == 0)
        def _():
            st_ref[...] = jnp.zeros_like(st_ref)

        lbv = _lower_bound(lb_ref[...])
        _, qf, _, f, kf = _hgrn_gates(q_ref[...], f_ref[...], lbv)
        b = _tri_sum(_tril_bf16(c_len), jnp.log(f))
        b_ref[...] = b
        qf_ref[...] = qf
        kf_ref[...] = kf
        a_ref[...] = jnp.zeros_like(a_ref)

        def diagonal(rows, j, key):
            bj = b_ref[pl.ds(key, 1), :]
            kj = kf_ref[pl.ds(key, 1), :]
            nrow = rows.stop - rows.start
            t_loc = lax.broadcasted_iota(jnp.int32, (nrow, nh * hd), 0) + (rows.start % SUB_BLOCK)
            e = jnp.exp(jnp.where(t_loc >= j, b_ref[rows, :] - bj, NEG_BIG))
            prod = qf_ref[rows, :] * kj * e
            lane = lax.broadcasted_iota(jnp.int32, (nrow, hd), 1)
            for h in range(nh):
                col = jnp.sum(prod[:, h * hd:(h + 1) * hd], axis=-1, keepdims=True)
                a_ref[h, rows, :] = jnp.where(lane == key, col, a_ref[h, rows, :])

        _hgrn_diagonal_loops(c_len, diagonal)
        q_til, k_til, _, _ = _hgrn_off_diagonal(b, qf, kf)
        q_til = q_til.astype(BF16)
        k_til = [k.astype(BF16) for k in k_til]

        b_last = b[c_len - 1:c_len, :]
        qb = (qf * jnp.exp(b)).astype(BF16)
        kb2 = (kf * jnp.exp(b_last - b)).astype(BF16)
        vf = i_ref[...].astype(BF16)
        for h in range(nh):
            hs = slice(h * hd, (h + 1) * hd)
            st = st_ref[h]
            st_out_ref[0, h] = st
            off = [jnp.zeros((SUB_BLOCK, hd), F32)]
            for i in range(1, c_len // SUB_BLOCK):
                off.append(_dot_nt(q_til[i * SUB_BLOCK:(i + 1) * SUB_BLOCK, hs], k_til[i - 1][:, hs]))
            a_h = a_ref[h] + jnp.concatenate(off, axis=0)
            a_out_ref[:, hs] = a_h
            o_ref[:, hs] = _dot_nt(qb[:, hs], st.astype(BF16)) + _dot(a_h[:, :c_len].astype(BF16), vf[:, hs])
            st_ref[h] = st * jnp.exp(b_last[:, hs]) + _dot_tn(vf[:, hs], kb2[:, hs])

    blk = (c_len, HGRN_WIDTH)
    step = lambda k: (lambda: pl.program_id(0) == k)
    e_in, e_out, e_shape, e_scr, e_args = _ride_specs(ride)
    return pl.pallas_call(
        _riding(body, 4, 3, 5, ride, step(0), step((7 * n_chunks) // 8), step(n_chunks - 1)),
        name="hgrn_fwd",
        grid=(n_chunks,),
        in_specs=[
            pl.BlockSpec(blk, lambda c: (c, col0)),
            pl.BlockSpec(blk, lambda c: (c, col0 + 1)),
            pl.BlockSpec(blk, lambda c: (c, col0 + 2)),
            pl.BlockSpec((2, HGRN_WIDTH), lambda c: (0, 0)),
        ] + e_in,
        out_specs=[
            pl.BlockSpec(blk, lambda c: (c, 0)),
            pl.BlockSpec((1, nh, hd, hd), lambda c: (c, 0, 0, 0)),
            pl.BlockSpec(blk, lambda c: (c, 0)),
        ] + e_out,
        out_shape=[
            jax.ShapeDtypeStruct((s, HGRN_WIDTH), F32),
            jax.ShapeDtypeStruct((n_chunks, nh, hd, hd), F32),
            jax.ShapeDtypeStruct((s, nh * hd), F32),
        ] + e_shape,
        scratch_shapes=[
            pltpu.VMEM((nh, hd, hd), F32),
            pltpu.VMEM(blk, F32),
            pltpu.VMEM(blk, F32),
            pltpu.VMEM(blk, F32),
            pltpu.VMEM((nh, c_len, hd), F32),
        ] + e_scr,
        compiler_params=_params(dimension_semantics=("arbitrary",)),
    )(proj, proj, proj, lb, *e_args)


def hgrn_bwd(proj, lb, d_o, states, a_mat, ride=None):
    s = proj.shape[0]
    c_len, nh, hd = HGRN_CHUNK, HGRN_HEADS, HGRN_HEAD_DIM
    n_chunks = s // c_len
    col0 = 0
    last = n_chunks - 1

    def body(q_ref, f_ref, i_ref, lb_ref, do_ref, st_in_ref, a_in_ref, dq_ref, df_ref, di_ref, dlb_ref,
             dst_ref, b_ref, qf_ref, kf_ref, da_ref, dqi_ref, dki_ref):
        @pl.when(pl.program_id(0) == 0)
        def _():
            dst_ref[...] = jnp.zeros_like(dst_ref)
            dlb_ref[...] = jnp.zeros_like(dlb_ref)

        lbv = _lower_bound(lb_ref[...])
        q = q_ref[...]
        sq, qf, sig, f, kf = _hgrn_gates(q, f_ref[...], lbv)
        b = _tri_sum(_tril_bf16(c_len), jnp.log(f))
        b_ref[...] = b
        qf_ref[...] = qf
        kf_ref[...] = kf
        b_last = b[c_len - 1:c_len, :]
        eb = jnp.exp(b)
        ebl = jnp.exp(b_last - b)
        qb = qf * eb
        kb2 = kf * ebl
        vf = i_ref[...]
        d_o = do_ref[...]
        qb_b, kb2_b, vf_b, do_b = qb.astype(BF16), kb2.astype(BF16), vf.astype(BF16), d_o.astype(BF16)
        tq = lax.broadcasted_iota(jnp.int32, (c_len, hd), 0)
        lane = lax.broadcasted_iota(jnp.int32, (c_len, hd), 1)

        dqb_parts, dvf_parts, dkb2_parts, dbl_parts = [], [], [], []
        for h in range(nh):
            hs = slice(h * hd, (h + 1) * hd)
            st = st_in_ref[0, h]
            dst = dst_ref[h]
            st_b, dst_b = st.astype(BF16), dst.astype(BF16)
            a_h = a_in_ref[:, hs][:, :c_len].astype(BF16)
            dqb_parts.append(_dot(do_b[:, hs], st_b))
            dvf_parts.append(_dot_tn(a_h, do_b[:, hs]) + _dot_nt(kb2_b[:, hs], dst_b))
            dkb2_parts.append(_dot(vf_b[:, hs], dst_b))
            da = _dot_nt(do_b[:, hs], vf_b[:, hs])
            da = jnp.concatenate([da, jnp.zeros((c_len, hd - c_len), F32)], axis=1)
            da_ref[h] = jnp.where(tq >= lane, da, 0.0)
            dbl_parts.append(jnp.sum(dst * st, axis=0, keepdims=True) * jnp.exp(b_last[:, hs]))
            dst_ref[h] = dst * jnp.exp(b_last[:, hs]) + _dot_tn(do_b[:, hs], qb_b[:, hs])
        dqb = jnp.concatenate(dqb_parts, axis=1)
        dvf = jnp.concatenate(dvf_parts, axis=1)
        dkb2 = jnp.concatenate(dkb2_parts, axis=1)
        dbl = jnp.concatenate(dbl_parts, axis=1) + jnp.sum(dkb2 * kb2, axis=0, keepdims=True)

        dqi_ref[...] = jnp.zeros_like(dqi_ref)
        t_idx = lax.broadcasted_iota(jnp.int32, (c_len, nh * hd), 0)

        def diagonal(rows, j, key):
            bj = b_ref[pl.ds(key, 1), :]
            kj = kf_ref[pl.ds(key, 1), :]
            nrow = rows.stop - rows.start
            t_loc = lax.broadcasted_iota(jnp.int32, (nrow, nh * hd), 0) + (rows.start % SUB_BLOCK)
            e = jnp.exp(jnp.where(t_loc >= j, b_ref[rows, :] - bj, NEG_BIG))
            lane_r = lax.broadcasted_iota(jnp.int32, (nrow, hd), 1)
            cols = [jnp.sum(jnp.where(lane_r == key, da_ref[h, rows, :], 0.0), axis=-1, keepdims=True)
                    for h in range(nh)]
            w = e * jnp.concatenate([jnp.broadcast_to(cc, (nrow, hd)) for cc in cols], axis=1)
            dqi_ref[rows, :] += w * kj
            dki_ref[pl.ds(key, 1), :] = jnp.sum(w * qf_ref[rows, :], axis=0, keepdims=True)

        _hgrn_diagonal_loops(c_len, diagonal)

        q_til, k_til, eq, ek = _hgrn_off_diagonal(b, qf, kf)
        q_hi, q_lo = _split2(q_til)
        k_pairs = [_split2(k) for k in k_til]
        n_sub = c_len // SUB_BLOCK
        dq_heads, dk_heads = [], []
        for h in range(nh):
            hs = slice(h * hd, (h + 1) * hd)
            dq_rows = [jnp.zeros((SUB_BLOCK, hd), F32)]
            dk_h = jnp.zeros((c_len, hd), F32)
            for i in range(1, n_sub):
                rows = slice(i * SUB_BLOCK, (i + 1) * SUB_BLOCK)
                n = i * SUB_BLOCK
                da_i = da_ref[h, rows, :].astype(BF16)
                k_hi, k_lo = k_pairs[i - 1]
                dq_rows.append((_dot(da_i, k_hi[:, hs]) + _dot(da_i, k_lo[:, hs])) * eq[rows, hs])
                dk_t = (_dot_tn(da_i, q_hi[rows, hs]) + _dot_tn(da_i, q_lo[rows, hs]))[:n, :] * ek[i - 1][:, hs]
                dk_h = dk_h + jnp.concatenate([dk_t, jnp.zeros((c_len - n, hd), F32)], axis=0)
            dq_heads.append(jnp.concatenate(dq_rows, axis=0))
            dk_heads.append(dk_h)
        dq_intra = dqi_ref[...] + jnp.concatenate(dq_heads, axis=1)
        dk_intra = dki_ref[...] + jnp.concatenate(dk_heads, axis=1)

        db = dqb * qb + qf * dq_intra - kf * dk_intra - dkb2 * kb2
        db = db + jnp.where(t_idx == c_len - 1, dbl, 0.0)
        dg = _tri_sum(_tril_bf16(c_len, upper=True), db)
        dqf = dqb * eb + dq_intra
        dkf = dkb2 * ebl + dk_intra
        dq_ref[...] = (dqf * (sq * (1.0 + q * (1.0 - sq)))).astype(BF16)
        dfv = dg / f - dkf
        df_ref[...] = (dfv * (1.0 - lbv) * sig * (1.0 - sig)).astype(BF16)
        di_ref[...] = dvf.astype(BF16)
        dlb_ref[...] += jnp.sum(dfv * (1.0 - sig), axis=0, keepdims=True)

    blk = (c_len, HGRN_WIDTH)
    rev = lambda c: last - c
    step = lambda k: (lambda: pl.program_id(0) == k)
    e_in, e_out, e_shape, e_scr, e_args = _ride_specs(ride)
    return pl.pallas_call(
        _riding(body, 7, 4, 7, ride, step(0), step(n_chunks // 2), step(last)),
        name="hgrn_bwd",
        grid=(n_chunks,),
        in_specs=[
            pl.BlockSpec(blk, lambda c: (rev(c), col0)),
            pl.BlockSpec(blk, lambda c: (rev(c), col0 + 1)),
            pl.BlockSpec(blk, lambda c: (rev(c), col0 + 2)),
            pl.BlockSpec((2, HGRN_WIDTH), lambda c: (0, 0)),
            pl.BlockSpec(blk, lambda c: (rev(c), 0)),
            pl.BlockSpec((1, nh, hd, hd), lambda c: (rev(c), 0, 0, 0)),
            pl.BlockSpec(blk, lambda c: (rev(c), 0)),
        ] + e_in,
        out_specs=[
            pl.BlockSpec(blk, lambda c: (rev(c), 0)),
            pl.BlockSpec(blk, lambda c: (rev(c), 0)),
            pl.BlockSpec(blk, lambda c: (rev(c), 0)),
            pl.BlockSpec((1, HGRN_WIDTH), lambda c: (0, 0)),
        ] + e_out,
        out_shape=[jax.ShapeDtypeStruct((s, HGRN_WIDTH), BF16)] * 3 + [jax.ShapeDtypeStruct((1, HGRN_WIDTH), F32)] + e_shape,
        scratch_shapes=[
            pltpu.VMEM((nh, hd, hd), F32),
            pltpu.VMEM(blk, F32),
            pltpu.VMEM(blk, F32),
            pltpu.VMEM(blk, F32),
            pltpu.VMEM((nh, c_len, hd), F32),
            pltpu.VMEM(blk, F32),
            pltpu.VMEM(blk, F32),
        ] + e_scr,
        compiler_params=_params(dimension_semantics=("arbitrary",)),
    )(proj, proj, proj, lb, d_o, states, a_mat, *e_args)


def _row_spec(tm, width, col=0):
    return pl.BlockSpec((tm, width), lambda i: (i, col))


def _const_spec(width):
    return pl.BlockSpec((1, width), lambda i: (0, 0))


def _acc_rows(ref, value):
    @pl.when(pl.program_id(0) == 0)
    def _():
        ref[...] = jnp.zeros_like(ref)

    ref[...] += jnp.sum(value, axis=0, keepdims=True)


def mix_fwd(attn_parts, o_h, proj, an, hn, w_out_b, gp, x):
    s = x.shape[0]
    tm = TOKEN_TILE
    gate_col = 3
    hd = HGRN_HEAD_DIM
    nd = len(DILATIONS)

    def body(*refs):
        o_refs, l_refs = refs[:nd], refs[nd:2 * nd]
        oh_ref, gate_ref, an_ref, hn_ref, w_ref, gp_ref, x_ref = refs[2 * nd:2 * nd + 7]
        x1_ref, cat_ref, mixed_ref, attn_ref = refs[2 * nd + 7:2 * nd + 11]
        lse_refs = refs[2 * nd + 11:3 * nd + 11]
        o_scr, l_scr, lse_scr = refs[3 * nd + 11:]
        os_ = [_from_dilated(r, o_scr.at[k], d, tm) for k, (r, d) in enumerate(zip(o_refs, DILATIONS))]
        ls = [_from_dilated(r, l_scr.at[k], d, tm) for k, (r, d) in enumerate(zip(l_refs, DILATIONS))]
        m = jnp.maximum(jnp.maximum(ls[0], ls[1]), ls[2])
        es = [jnp.exp(l - m) for l in ls]
        den = es[0] + es[1] + es[2]
        attn = (es[0] * os_[0] + es[1] * os_[1] + es[2] * os_[2]) / den
        attn_ref[...] = attn
        _lane_blocks(lse_scr, m + jnp.log(den))
        for d, ref in zip(DILATIONS, lse_refs):
            _to_dilated(lse_scr, ref, d, tm)
        cat_ref[:, :ATTN_WIDTH] = _rms_fwd(attn, an_ref[...], ATTN_WIDTH).astype(BF16)
        gate = gate_ref[...]
        silu_g = gate * _sigmoid(gate)
        for h in range(HGRN_HEADS):
            hs = slice(h * hd, (h + 1) * hd)
            rec = _rms_fwd(oh_ref[:, hs], hn_ref[:, hs], hd) * silu_g[:, hs]
            cat_ref[:, ATTN_WIDTH + h * hd:ATTN_WIDTH + (h + 1) * hd] = rec.astype(BF16)
        mixed = _dot(cat_ref[...], w_ref[...])
        mixed_ref[...] = mixed
        x1_ref[...] = x_ref[...] + _rms_fwd(mixed, gp_ref[...], D_MODEL)

    aw = ATTN_WIDTH
    return pl.pallas_call(
        body,
        name="mix_fwd",
        grid=(s // tm,),
        in_specs=[_dilated_spec(d, tm, aw) for d in DILATIONS] * 2 + [
            _row_spec(tm, aw), _row_spec(tm, aw, gate_col), _const_spec(aw), _const_spec(aw), _vmem_spec(),
            _const_spec(D_MODEL), _row_spec(tm, D_MODEL)],
        out_specs=[_row_spec(tm, D_MODEL), _row_spec(tm, D_MODEL), _row_spec(tm, D_MODEL), _row_spec(tm, aw)] + [
            _dilated_spec(d, tm, aw) for d in DILATIONS],
        out_shape=[
            jax.ShapeDtypeStruct((s, D_MODEL), F32),
            jax.ShapeDtypeStruct((s, D_MODEL), BF16),
            jax.ShapeDtypeStruct((s, D_MODEL), F32),
            jax.ShapeDtypeStruct((s, aw), F32),
        ] + [jax.ShapeDtypeStruct((d, s // d, aw), F32) for d in DILATIONS],
        scratch_shapes=[pltpu.VMEM((nd, aw // LANES, tm, LANES), F32), pltpu.VMEM((nd, aw // LANES, tm, LANES), F32),
                        pltpu.VMEM((aw // LANES, tm, LANES), F32)],
        compiler_params=_params(dimension_semantics=("arbitrary",)),
    )(*[p[0] for p in attn_parts], *[p[1] for p in attn_parts], o_h, proj, an, hn, w_out_b, gp, x)


def mix_bwd(dx1, mixed, gp, w_out_b, attn, an, o_h, proj, hn):
    s = dx1.shape[0]
    tm = TOKEN_TILE
    gate_col = 3
    hd = HGRN_HEAD_DIM
    aw = ATTN_WIDTH

    nd = len(DILATIONS)

    def body(*refs):
        dx1_ref, mixed_ref, gp_ref, w_ref, attn_ref, an_ref, oh_ref, gate_ref, hn_ref, dmix_ref = refs[:10]
        do_refs, delta_refs = refs[10:10 + nd], refs[10 + nd:10 + 2 * nd]
        doh_ref, dgate_ref, dgp_ref, dan_ref, dhn_ref, do_ref, delta_ref = refs[10 + 2 * nd:]
        dmixed, gp_c = _rms_bwd(dx1_ref[...], mixed_ref[...], gp_ref[...], D_MODEL)
        _acc_rows(dgp_ref, gp_c)
        dmixed_b = dmixed.astype(BF16)
        dmix_ref[...] = dmixed_b
        dcat = _dot_nt(dmixed_b, w_ref[...])
        attn = attn_ref[...]
        d_o, an_c = _rms_bwd(dcat[:, :aw], attn, an_ref[...], aw)
        _acc_rows(dan_ref, an_c)
        _lane_blocks(do_ref, d_o)
        prod = d_o * attn
        for pair in range(ATTN_HEADS // 2):
            pp = prod[:, pair * LANES:(pair + 1) * LANES]
            low = _lane_half((tm, LANES), 0)
            lo = jnp.sum(jnp.where(low, pp, 0.0), axis=-1, keepdims=True)
            hi = jnp.sum(jnp.where(low, 0.0, pp), axis=-1, keepdims=True)
            delta_ref[pair] = jnp.where(low, lo, hi)
        for d, o_ref, l_ref in zip(DILATIONS, do_refs, delta_refs):
            _to_dilated(do_ref, o_ref, d, tm, cast=BF16)
            _to_dilated(delta_ref, l_ref, d, tm)
        gate = gate_ref[...]
        sg = _sigmoid(gate)
        silu_g = gate * sg
        drec = dcat[:, aw:]
        hn_parts = []
        for h in range(HGRN_HEADS):
            hs = slice(h * hd, (h + 1) * hd)
            oh = oh_ref[:, hs]
            on = _rms_fwd(oh, hn_ref[:, hs], hd)
            dgate_ref[:, hs] = (drec[:, hs] * on * (sg[:, hs] * (1.0 + gate[:, hs] * (1.0 - sg[:, hs])))).astype(BF16)
            d_oh, hn_c = _rms_bwd(drec[:, hs] * silu_g[:, hs], oh, hn_ref[:, hs], hd)
            doh_ref[:, hs] = d_oh
            hn_parts.append(hn_c)
        _acc_rows(dhn_ref, jnp.concatenate(hn_parts, axis=1))

    return pl.pallas_call(
        body,
        name="mix_bwd",
        grid=(s // tm,),
        in_specs=[_row_spec(tm, D_MODEL), _row_spec(tm, D_MODEL), _const_spec(D_MODEL), _vmem_spec(), _row_spec(tm, aw),
                  _const_spec(aw), _row_spec(tm, aw), _row_spec(tm, aw, gate_col), _const_spec(aw)],
        out_specs=[_row_spec(tm, D_MODEL)] + [_dilated_spec(d, tm, aw) for d in DILATIONS] * 2 + [_row_spec(tm, aw)] * 2 + [
            _const_spec(D_MODEL), _const_spec(aw), _const_spec(aw)],
        out_shape=[jax.ShapeDtypeStruct((s, D_MODEL), BF16)] + [
            jax.ShapeDtypeStruct((d, s // d, aw), BF16) for d in DILATIONS] + [
            jax.ShapeDtypeStruct((d, s // d, aw), F32) for d in DILATIONS] + [
            jax.ShapeDtypeStruct((s, aw), F32), jax.ShapeDtypeStruct((s, aw), BF16),
            jax.ShapeDtypeStruct((1, D_MODEL), F32), jax.ShapeDtypeStruct((1, aw), F32),
            jax.ShapeDtypeStruct((1, aw), F32)],
        scratch_shapes=[pltpu.VMEM((aw // LANES, tm, LANES), F32), pltpu.VMEM((aw // LANES, tm, LANES), F32)],
        compiler_params=_params(dimension_semantics=("arbitrary",)),
    )(dx1, mixed, gp, w_out_b, attn, an, o_h, proj, hn)


def mlp_fwd_bwd(x1, g_pre, w1_blocks, w2_b, g_post, target):
    s = x1.shape[0]
    tm = TOKEN_TILE
    nblk, _, fb = w1_blocks.shape

    def body(x1_ref, gpre_ref, w1_ref, w2_ref, gpost_ref, t_ref,
             dx1_ref, h2_ref, a_ref, du_ref, dff_ref, loss_ref, dgpre_ref, dgpost_ref, u_ref):
        x1v = x1_ref[...]
        h2 = _rms_fwd(x1v, gpre_ref[...], D_MODEL).astype(BF16)
        h2_ref[...] = h2
        ff = jnp.zeros((tm, D_MODEL), F32)
        for j in range(nblk):
            cols = slice(j * fb, (j + 1) * fb)
            ru = jnp.maximum(_dot(h2, w1_ref[j]), 0.0)
            u_ref[:, cols] = ru
            a = (ru * ru).astype(BF16)
            a_ref[:, cols] = a
            ff = ff + _dot(a, w2_ref[cols, :])
        diff = x1v + _rms_fwd(ff, gpost_ref[...], D_MODEL) - t_ref[...]
        _acc_rows(loss_ref, diff * diff)
        dy = diff * (1.0 / D_MODEL)
        dff, gpost_c = _rms_bwd(dy, ff, gpost_ref[...], D_MODEL)
        _acc_rows(dgpost_ref, gpost_c)
        dff_b = dff.astype(BF16)
        dff_ref[...] = dff_b
        dh2 = jnp.zeros((tm, D_MODEL), F32)
        for j in range(nblk):
            cols = slice(j * fb, (j + 1) * fb)
            du = (_dot_nt(dff_b, w2_ref[cols, :]) * (2.0 * u_ref[:, cols])).astype(BF16)
            du_ref[:, cols] = du
            dh2 = dh2 + _dot_nt(du, w1_ref[j])
        dxa, gpre_c = _rms_bwd(dh2, x1v, gpre_ref[...], D_MODEL)
        _acc_rows(dgpre_ref, gpre_c)
        dx1_ref[...] = dy + dxa

    dm = D_MODEL
    return pl.pallas_call(
        body,
        name="mlp_fwd_bwd",
        grid=(s // tm,),
        in_specs=[_row_spec(tm, dm), _const_spec(dm), _vmem_spec(), _vmem_spec(), _const_spec(dm), _row_spec(tm, dm)],
        out_specs=[_row_spec(tm, dm), _row_spec(tm, dm), _row_spec(tm, D_FF), _row_spec(tm, D_FF), _row_spec(tm, dm),
                   _const_spec(dm), _const_spec(dm), _const_spec(dm)],
        out_shape=[
            jax.ShapeDtypeStruct((s, dm), F32),
            jax.ShapeDtypeStruct((s, dm), BF16),
            jax.ShapeDtypeStruct((s, D_FF), BF16),
            jax.ShapeDtypeStruct((s, D_FF), BF16),
            jax.ShapeDtypeStruct((s, dm), BF16),
            jax.ShapeDtypeStruct((1, dm), F32),
            jax.ShapeDtypeStruct((1, dm), F32),
            jax.ShapeDtypeStruct((1, dm), F32),
        ],
        scratch_shapes=[pltpu.VMEM((tm, D_FF), F32)],
        compiler_params=_params(dimension_semantics=("arbitrary",)),
    )(x1, g_pre, w1_blocks, w2_b, g_post, target)


def in_proj_bwd(attn_grads, hgrn_grads, dgate, w_in_b, x, g1, dx1):
    s = x.shape[0]
    tm = PROJ_TILE
    aw = ATTN_WIDTH
    n_attn = len(attn_grads)
    flat = [g[k] for k in range(3) for g in attn_grads] + list(hgrn_grads) + [dgate]

    def body(*refs):
        parts = refs[:len(flat)]
        w_ref, x_ref, g_ref, dx1_ref, dx_ref, dproj_ref, dg_ref, scr = refs[len(flat):]
        groups = []
        for k in range(3):
            acc = None
            for p, d in zip(parts[k * n_attn:(k + 1) * n_attn], DILATIONS):
                v = _from_dilated(p, scr, d, tm)
                acc = v if acc is None else acc + v
            groups.append(acc)
        groups += [p[...] for p in parts[3 * n_attn:]]
        dh = jnp.zeros((tm, D_MODEL), F32)
        for gi, grp in enumerate(groups):
            cols = slice(gi * aw, (gi + 1) * aw)
            gb = grp.astype(BF16)
            dproj_ref[:, cols] = gb
            dh = dh + _dot_nt(gb, w_ref[:, cols])
        dxa, g_c = _rms_bwd(dh, x_ref[...], g_ref[...], D_MODEL)
        _acc_rows(dg_ref, g_c)
        dx_ref[...] = dx1_ref[...] + dxa

    dm = D_MODEL
    return pl.pallas_call(
        body,
        name="in_proj_bwd",
        grid=(s // tm,),
        in_specs=[_dilated_spec(d, tm, aw) for d in DILATIONS] * 3 + [_row_spec(tm, aw)] * 4 + [
            _vmem_spec(), _row_spec(tm, dm), _const_spec(dm), _row_spec(tm, dm)],
        out_specs=[_row_spec(tm, dm), _row_spec(tm, IN_PROJ_WIDTH), _const_spec(dm)],
        out_shape=[jax.ShapeDtypeStruct((s, dm), F32), jax.ShapeDtypeStruct((s, IN_PROJ_WIDTH), BF16),
                   jax.ShapeDtypeStruct((1, dm), F32)],
        scratch_shapes=[pltpu.VMEM((aw // LANES, tm, LANES), F32)],
        compiler_params=_params(dimension_semantics=("arbitrary",)),
    )(*flat, w_in_b, x, g1, dx1)


def wgrad(a_b, b_b, tn, name, ts=1024, per_step=1):
    s, k = a_b.shape
    n = b_b.shape[1]

    def body(a_ref, b_ref, o_ref):
        @pl.when(pl.program_id(1) == 0)
        def _():
            o_ref[...] = jnp.zeros_like(o_ref)

        a = a_ref[...]
        for jj in range(per_step):
            o_ref[jj] += _dot_tn(a, b_ref[:, jj * tn:(jj + 1) * tn])

    wide = tn * per_step
    return pl.pallas_call(
        body,
        name=name,
        grid=(n // wide, s // ts),
        in_specs=[pl.BlockSpec((ts, k), lambda j, i: (i, 0)), pl.BlockSpec((ts, wide), lambda j, i: (i, j))],
        out_specs=pl.BlockSpec((per_step, k, tn), lambda j, i: (j, 0, 0)),
        out_shape=jax.ShapeDtypeStruct((n // tn, k, tn), F32),
        compiler_params=_params(dimension_semantics=("arbitrary", "arbitrary")),
    )(a_b, b_b)


def train_step(x, target, g1, an, logits, hn, gp, g_pre, g_post, w, m, v):
    nd = len(DILATIONS)
    shard_b = {k: w[k].astype(BF16) for k in BIG}
    (w_in_g,) = run_exchange(gather_exchange([shard_b["w_in"]]), "gather_w_in")
    w_in_b = w_in_g.transpose(1, 0, 2).reshape(D_MODEL, IN_PROJ_WIDTH)

    proj, h_b, *qkvs = in_proj_fwd(x, g1, w_in_b)
    attn_parts = [attn_fwd(qkv, d) for qkv, d in zip(qkvs, DILATIONS)]
    o_h, states, a_mat, w_out_g, w1_blocks, w2_g = hgrn_fwd(
        proj, logits, ride=gather_exchange([shard_b["w_out"], shard_b["w_ff1"], shard_b["w_ff2"]]))
    w_out_b = w_out_g.reshape(D_MODEL, D_MODEL)
    w2_b = w2_g.reshape(D_FF, D_MODEL)
    x1, cat_b, mixed, attn, *lses = mix_fwd(attn_parts, o_h, proj, an, hn, w_out_b, gp, x)
    dx1, h2_b, a_b, du_b, dff_b, loss_vec, dg_pre, dg_post = mlp_fwd_bwd(x1, g_pre, w1_blocks, w2_b, g_post, target)
    dw2 = wgrad(a_b, dff_b, D_MODEL, "wgrad_ff2", ts=512)
    dw1 = wgrad(h2_b, du_b, D_FF // N_DEV, "wgrad_ff1", per_step=2)
    dmix_b, *rest = mix_bwd(dx1, mixed, gp, w_out_b, attn, an, o_h, proj, hn)
    d_os, deltas = rest[:nd], rest[nd:2 * nd]
    d_oh, dgate, dgp, dan, dhn = rest[2 * nd:]
    dwout = wgrad(cat_b, dmix_b, D_MODEL, "wgrad_out")

    early = ("w_out", "w_ff1", "w_ff2")
    early_grads = [dwout.reshape(N_DEV, D_MODEL // N_DEV, D_MODEL), dw1, dw2.reshape(N_DEV, D_FF // N_DEV, D_MODEL)]
    attn_grads = []
    for k, d in enumerate(DILATIONS):
        ride = to_core_exchange(early_grads) if k == 0 else None
        res = attn_bwd(qkvs[k], d_os[k], lses[k], deltas[k], d, ride=ride)
        attn_grads.append(res[:3])
        if k == 0:
            pairs = [pair_sum(g, s, f"pair_sum_{name}") for g, s, name in zip(early_grads, res[3:], early)]
    dq_h, df_h, di_h, dlb, *others = hgrn_bwd(proj, logits, d_oh, states, a_mat,
                                              ride=to_chip_exchange([p[1] for p in pairs]))
    dx, dproj_b, dg1 = in_proj_bwd(attn_grads, (dq_h, df_h, di_h), dgate, w_in_b, x, g1, dx1)
    dwin = wgrad(h_b, dproj_b, 2 * IN_PROJ_WIDTH // N_DEV, "wgrad_in")
    big = {name: sum_adamw(p[0], o, w[name], m[name], v[name], f"sum_adamw_{name}")
           for name, p, o in zip(early, pairs, others)}

    shard_w = IN_PROJ_WIDTH // N_DEV
    dwin_blocks = dwin.reshape(N_DEV // 2, D_MODEL, 2, shard_w).transpose(0, 2, 1, 3).reshape(N_DEV, D_MODEL, shard_w)
    (from_sibling,) = run_exchange(to_core_exchange([dwin_blocks]), "reduce_w_in_to_core")
    pair_in, pair_in_b = pair_sum(dwin_blocks, from_sibling, "pair_sum_w_in")
    (others_in,) = run_exchange(to_chip_exchange([pair_in_b]), "reduce_w_in_to_chip")
    big["w_in"] = sum_adamw(pair_in, others_in, w["w_in"], m["w_in"], v["w_in"], "sum_adamw_w_in")
    small = dict(dg1=dg1, dan=dan, dlb=dlb, dhn=dhn, dgp=dgp, dg_pre=dg_pre, dg_post=dg_post, loss_vec=loss_vec)
    return dx, big, small


def _position():
    x, y, c = lax.axis_index("x"), lax.axis_index("y"), lax.axis_index("c")
    other_chips = [(1 - x, y), (x, 1 - y), (1 - x, 1 - y)]
    return x, y, c, other_chips


def _any_spec():
    return pl.BlockSpec(memory_space=pl.ANY)


class Exchange:
    def __init__(self, arrays, out_shape, sems, stages):
        self.arrays, self.out_shape, self.sems, self.stages = list(arrays), list(out_shape), list(sems), stages


def gather_exchange(shards):
    n = len(shards)

    def stages(ins, outs, sems):
        send_sems, recv_sems, local_sems = sems

        def parts():
            x, y, c, chips = _position()
            me, sibling = (x, y, c), (x, y, 1 - c)

            def slot(a, px, py, pc):
                return outs[a].at[4 * px + 2 * py + pc]

            def copy(a, k, block, to, src=None):
                return pltpu.make_async_remote_copy(
                    src_ref=slot(a, *block) if src is None else src, dst_ref=slot(a, *block),
                    send_sem=send_sems.at[a, k], recv_sem=recv_sems.at[a, k], device_id=to, device_id_type=MESH)

            local = [pltpu.make_async_copy(ins[a], slot(a, *me), local_sems.at[a]) for a in range(n)]
            first = []
            for a in range(n):
                first.append(copy(a, 0, me, sibling, src=ins[a]))
                first += [copy(a, 1 + j, me, (*chip, c), src=ins[a]) for j, chip in enumerate(chips)]
            passed = [copy(a, 4 + j, (*chip, c), sibling) for j, chip in enumerate(chips) for a in range(n)]
            return c, chips, me, sibling, copy, local, first, passed

        def begin():
            _, _, _, _, _, local, first, _ = parts()
            for cp in local + first:
                cp.start()

        def middle():
            c, chips, me, _, copy, _, _, passed = parts()
            k = 0
            for j, chip in enumerate(chips):
                for a in range(n):
                    copy(a, 1 + j, (*chip, c), me).wait_recv()
                    passed[k].start()
                    k += 1

        def end():
            c, chips, me, sibling, copy, local, first, passed = parts()
            for a in range(n):
                copy(a, 0, sibling, me).wait_recv()
                for j, chip in enumerate(chips):
                    copy(a, 4 + j, (*chip, 1 - c), me).wait_recv()
            for cp in first + passed:
                cp.wait_send()
            for cp in local:
                cp.wait()

        return begin, middle, end

    return Exchange(
        shards, [jax.ShapeDtypeStruct((N_DEV,) + sh.shape, sh.dtype) for sh in shards],
        [pltpu.SemaphoreType.DMA((n, 7)), pltpu.SemaphoreType.DMA((n, 7)), pltpu.SemaphoreType.DMA((n,))], stages)


def to_core_exchange(grads):
    n = len(grads)

    def stages(ins, outs, sems):
        send_sems, recv_sems = sems

        def copies():
            x, y, c, _ = _position()
            return [pltpu.make_async_remote_copy(
                src_ref=ins[a].at[2 * q + (1 - c)], dst_ref=outs[a].at[q], send_sem=send_sems.at[a, q],
                recv_sem=recv_sems.at[a, q], device_id=(x, y, 1 - c), device_id_type=MESH)
                for a in range(n) for q in range(4)]

        def begin():
            for cp in copies():
                cp.start()

        def end():
            for cp in copies():
                cp.wait()

        return begin, None, end

    return Exchange(grads, [jax.ShapeDtypeStruct((4,) + g.shape[1:], g.dtype) for g in grads],
                    [pltpu.SemaphoreType.DMA((n, 4)), pltpu.SemaphoreType.DMA((n, 4))], stages)


def pair_sum(grad, from_sibling, name):
    _, r, cdim = grad.shape
    tr = min(r, 256)
    c_idx = lax.axis_index("c").astype(jnp.int32).reshape(1)

    def body(c_ref, g_ref, s_ref, o_ref, ob_ref):
        total = g_ref[...] + s_ref[...]
        o_ref[...] = total
        ob_ref[...] = total.astype(BF16)

    blk = lambda: pl.BlockSpec((1, tr, cdim), lambda q, i, cr: (q, i, 0))
    return pl.pallas_call(
        body,
        name=name,
        grid_spec=pltpu.PrefetchScalarGridSpec(
            num_scalar_prefetch=1,
            grid=(4, r // tr),
            in_specs=[pl.BlockSpec((1, tr, cdim), lambda q, i, cr: (2 * q + cr[0], i, 0)), blk()],
            out_specs=[blk(), blk()],
        ),
        out_shape=[jax.ShapeDtypeStruct((4, r, cdim), F32), jax.ShapeDtypeStruct((4, r, cdim), BF16)],
        compiler_params=_params(dimension_semantics=("arbitrary", "arbitrary")),
    )(c_idx, grad, from_sibling)


def to_chip_exchange(pairs):
    n = len(pairs)

    def stages(ins, outs, sems):
        send_sems, recv_sems = sems

        def copies():
            x, y, c, chips = _position()
            return [pltpu.make_async_remote_copy(
                src_ref=ins[a].at[2 * px + py], dst_ref=outs[a].at[j], send_sem=send_sems.at[a, j],
                recv_sem=recv_sems.at[a, j], device_id=(px, py, c), device_id_type=MESH)
                for a in range(n) for j, (px, py) in enumerate(chips)]

        def begin():
            for cp in copies():
                cp.start()

        def end():
            for cp in copies():
                cp.wait()

        return begin, None, end

    return Exchange(pairs, [jax.ShapeDtypeStruct((3,) + p.shape[1:], p.dtype) for p in pairs],
                    [pltpu.SemaphoreType.DMA((n, 3)), pltpu.SemaphoreType.DMA((n, 3))], stages)


def run_exchange(ex, name):
    n_in, n_out = len(ex.arrays), len(ex.out_shape)

    def body(*refs):
        begin, middle, end = ex.stages(refs[:n_in], refs[n_in:n_in + n_out], refs[n_in + n_out:])
        begin()
        if middle is not None:
            middle()
        end()

    return pl.pallas_call(
        body,
        name=name,
        in_specs=[_any_spec()] * n_in,
        out_specs=[_any_spec()] * n_out,
        out_shape=ex.out_shape,
        scratch_shapes=ex.sems,
    )(*ex.arrays)


def _riding(body, n_in, n_out, n_scratch, ex, first, middle, last):
    if ex is None:
        return body
    r_in, r_out = len(ex.arrays), len(ex.out_shape)

    def wrapped(*refs):
        k_in, refs = refs[:n_in], refs[n_in:]
        e_in, refs = refs[:r_in], refs[r_in:]
        k_out, refs = refs[:n_out], refs[n_out:]
        e_out, refs = refs[:r_out], refs[r_out:]
        k_scr, e_sems = refs[:n_scratch], refs[n_scratch:]
        begin, mid, end = ex.stages(e_in, e_out, e_sems)
        pl.when(first())(begin)
        body(*k_in, *k_out, *k_scr)
        if mid is not None:
            pl.when(middle())(mid)
        pl.when(last())(end)

    return wrapped


def _ride_specs(ex):
    if ex is None:
        return [], [], [], [], []
    return [_any_spec()] * len(ex.arrays), [_any_spec()] * len(ex.out_shape), ex.out_shape, ex.sems, ex.arrays


def _adamw(w, g, m, v):
    m = ADAM_B1 * m + (1.0 - ADAM_B1) * g
    v = ADAM_B2 * v + (1.0 - ADAM_B2) * (g * g)
    m_hat = m / (1.0 - ADAM_B1 ** ADAM_STEP)
    v_hat = v / (1.0 - ADAM_B2 ** ADAM_STEP)
    delta = -ADAM_LR * (m_hat / (jnp.sqrt(v_hat) + ADAM_EPS) + ADAM_WD * w)
    return delta, m, v


def sum_adamw(pairs, others, w, m, v, name):
    r, cdim = w.shape
    tr = min(r, 256)
    chip_idx = (2 * lax.axis_index("x") + lax.axis_index("y")).astype(jnp.int32).reshape(1)

    def body(q_ref, p_ref, o_ref, w_ref, m_ref, v_ref, g_out, d_out, m_out, v_out):
        g = p_ref[0] + o_ref[0].astype(F32) + o_ref[1].astype(F32) + o_ref[2].astype(F32)
        g_out[...] = g
        d_out[...], m_out[...], v_out[...] = _adamw(w_ref[...], g, m_ref[...], v_ref[...])

    tile = lambda: pl.BlockSpec((tr, cdim), lambda i, qr: (i, 0))
    return pl.pallas_call(
        body,
        name=name,
        grid_spec=pltpu.PrefetchScalarGridSpec(
            num_scalar_prefetch=1,
            grid=(r // tr,),
            in_specs=[pl.BlockSpec((1, tr, cdim), lambda i, qr: (qr[0], i, 0)),
                      pl.BlockSpec((3, tr, cdim), lambda i, qr: (0, i, 0)), tile(), tile(), tile()],
            out_specs=[tile(), tile(), tile(), tile()],
        ),
        out_shape=[jax.ShapeDtypeStruct((r, cdim), F32)] * 4,
        compiler_params=_params(dimension_semantics=("arbitrary",)),
    )(chip_idx, pairs, others, w, m, v)


SMALL_ROWS = 8


def small_all_reduce(packed):
    shape = packed.shape

    def body(in_ref, out_ref, recv_ref, send_sems, recv_sems):
        x, y, c, _ = _position()
        my_id = 4 * x + 2 * y + c
        recv_ref[my_id] = in_ref[...]
        copies = []
        for rel in range(1, N_DEV):
            fx, fy, fc = (rel >> 2) & 1, (rel >> 1) & 1, rel & 1
            px = 1 - x if fx else x
            py = 1 - y if fy else y
            pc = 1 - c if fc else c
            cp = pltpu.make_async_remote_copy(
                src_ref=in_ref, dst_ref=recv_ref.at[my_id], send_sem=send_sems.at[rel - 1],
                recv_sem=recv_sems.at[rel - 1], device_id=(px, py, pc), device_id_type=MESH)
            cp.start()
            copies.append((cp, pltpu.make_async_remote_copy(
                src_ref=in_ref, dst_ref=recv_ref.at[4 * px + 2 * py + pc], send_sem=send_sems.at[rel - 1],
                recv_sem=recv_sems.at[rel - 1], device_id=(px, py, pc), device_id_type=MESH)))
        for cp, landing in copies:
            landing.wait_recv()
        for cp, landing in copies:
            cp.wait_send()
        total = recv_ref[0]
        for k in range(1, N_DEV):
            total = total + recv_ref[k]
        out_ref[...] = total

    return pl.pallas_call(
        body,
        name="small_all_reduce",
        in_specs=[_vmem_spec()],
        out_specs=_vmem_spec(),
        out_shape=jax.ShapeDtypeStruct(shape, F32),
        scratch_shapes=[pltpu.VMEM((N_DEV,) + shape, F32), pltpu.SemaphoreType.DMA((N_DEV - 1,)),
                        pltpu.SemaphoreType.DMA((N_DEV - 1,))],
    )(packed)


def small_adamw(reduced, w, m, v):
    def body(r_ref, w_ref, m_ref, v_ref, g_out, d_out, m_out, v_out, loss_out):
        red = r_ref[...]
        wv = w_ref[...]
        lb = _lower_bound(jnp.concatenate([wv[5:6, :HGRN_WIDTH], wv[5:6, HGRN_WIDTH:]], axis=0))
        t = red[5:6, :HGRN_WIDTH] * lb * (1.0 - lb)
        row = lax.broadcasted_iota(jnp.int32, red.shape, 0)
        g = jnp.where(row == 5, jnp.concatenate([t, -t], axis=1), jnp.where(row >= 6, 0.0, red))
        g_out[...] = g
        d_out[...], m_out[...], v_out[...] = _adamw(wv, g, m_ref[...], v_ref[...])
        loss = jnp.sum(red[6:7, :], axis=-1, keepdims=True) * (0.5 / D_MODEL)
        loss_out[...] = jnp.broadcast_to(loss, loss_out.shape)

    return pl.pallas_call(
        body,
        name="small_adamw",
        in_specs=[_vmem_spec()] * 4,
        out_specs=[_vmem_spec()] * 5,
        out_shape=[jax.ShapeDtypeStruct(reduced.shape, F32)] * 4 + [jax.ShapeDtypeStruct((8, 128), F32)],
    )(reduced, w, m, v)


def _pack_small(g1, gp, g_pre, g_post, an, hn, logits_or_dlb, extra=None):
    row5 = logits_or_dlb.reshape(1, -1)
    row5 = jnp.pad(row5, ((0, 0), (0, D_MODEL - row5.shape[1])))
    row6 = jnp.zeros((1, D_MODEL), F32) if extra is None else extra
    return jnp.concatenate([g1, gp, g_pre, g_post, jnp.concatenate([an, hn], axis=1), row5, row6,
                            jnp.zeros((1, D_MODEL), F32)], axis=0)


def _unpack_small(p):
    return dict(mix_pre_norm=p[0:1], mix_post_norm=p[1:2], mlp_pre_norm=p[2:3], mlp_post_norm=p[3:4],
                attn_out_norm=p[4:5, :ATTN_WIDTH], hgrn_out_norm=p[4:5, ATTN_WIDTH:],
                hgrn_lb_logits=p[5].reshape(2, HGRN_WIDTH))


BIG = ("w_in", "w_out", "w_ff1", "w_ff2")
ORDER = ("mix_pre_norm", "w_in", "attn_out_norm", "hgrn_lb_logits", "hgrn_out_norm", "w_out", "mix_post_norm",
         "mlp_pre_norm", "w_ff1", "w_ff2", "mlp_post_norm")


def kernel(x, mix_pre_norm, w_in, attn_out_norm, hgrn_lb_logits, hgrn_out_norm, w_out, mix_post_norm, mlp_pre_norm, w_ff1, w_ff2, mlp_post_norm, loss_target, m_mix_pre_norm, m_w_in, m_attn_out_norm, m_hgrn_lb_logits, m_hgrn_out_norm, m_w_out, m_mix_post_norm, m_mlp_pre_norm, m_w_ff1, m_w_ff2, m_mlp_post_norm, v_mix_pre_norm, v_w_in, v_attn_out_norm, v_hgrn_lb_logits, v_hgrn_out_norm, v_w_out, v_mix_post_norm, v_mlp_pre_norm, v_w_ff1, v_w_ff2, v_mlp_post_norm):
    w = dict(w_in=w_in[0], w_out=w_out[0], w_ff1=w_ff1[0], w_ff2=w_ff2[0])
    m = dict(w_in=m_w_in[0], w_out=m_w_out[0], w_ff1=m_w_ff1[0], w_ff2=m_w_ff2[0])
    v = dict(w_in=v_w_in[0], w_out=v_w_out[0], w_ff1=v_w_ff1[0], w_ff2=v_w_ff2[0])

    dx, big, small = train_step(x[0], loss_target[0], mix_pre_norm, attn_out_norm, hgrn_lb_logits, hgrn_out_norm,
                                mix_post_norm, mlp_pre_norm, mlp_post_norm, w, m, v)

    packed_g = _pack_small(small["dg1"], small["dgp"], small["dg_pre"], small["dg_post"], small["dan"], small["dhn"],
                           small["dlb"], small["loss_vec"])
    reduced = small_all_reduce(packed_g)
    pack = lambda a, b, c2, d, e, f, g: _pack_small(a, b, c2, d, e, f, g)
    w_s = pack(mix_pre_norm, mix_post_norm, mlp_pre_norm, mlp_post_norm, attn_out_norm, hgrn_out_norm, hgrn_lb_logits)
    m_s = pack(m_mix_pre_norm, m_mix_post_norm, m_mlp_pre_norm, m_mlp_post_norm, m_attn_out_norm, m_hgrn_out_norm,
               m_hgrn_lb_logits)
    v_s = pack(v_mix_pre_norm, v_mix_post_norm, v_mlp_pre_norm, v_mlp_post_norm, v_attn_out_norm, v_hgrn_out_norm,
               v_hgrn_lb_logits)
    g_s, d_s, nm_s, nv_s, loss = small_adamw(reduced, w_s, m_s, v_s)
    small_out = [_unpack_small(t) for t in (g_s, d_s, nm_s, nv_s)]

    outs = [loss[0, 0], dx[None]]
    for kind in range(4):
        for name in ORDER:
            outs.append(big[name][kind][None] if name in BIG else small_out[kind][name])
    return tuple(outs)
```

```python
import functools
import math

import jax
import jax.numpy as jnp
from jax import lax
from jax.experimental import pallas as pl
from jax.experimental.pallas import tpu as pltpu

F32 = jnp.float32
BF16 = jnp.bfloat16

D_MODEL = 1024
SEQ = 4096
ATTN_WIDTH = 512
ATTN_HEAD_DIM = 64
ATTN_HEADS = 8
ATTN_BLOCK = 128
DILATIONS = (1, 4, 16)
HGRN_WIDTH = 512
HGRN_HEADS = 4
HGRN_HEAD_DIM = 128
HGRN_CHUNK = 64
IN_PROJ_WIDTH = 3584
D_FF = 4096
RMS_EPS = 1e-6
N_DEV = 8
ADAM_LR = 0.001
ADAM_B1 = 0.9
ADAM_B2 = 0.999
ADAM_EPS = 1e-08
ADAM_WD = 0.01
ADAM_STEP = 10

SUBLANES = 8
LANES = 128
COLUMN_UNROLL = 8
SUB_BLOCK = 16
TOKEN_TILE = 256
MLP_TILE = 256
PROJ_TILE = 512
VMEM_LIMIT = 56 * 1024 * 1024
NEG_BIG = -1e30
MESH = pl.DeviceIdType.MESH


def _params(**kw):
    return pltpu.CompilerParams(vmem_limit_bytes=VMEM_LIMIT, **kw)


def _vmem_spec():
    return pl.BlockSpec(memory_space=pltpu.VMEM)


def _dot(a, b):
    return jnp.dot(a, b, preferred_element_type=F32)


def _dot_nt(a, b):
    return lax.dot_general(a, b, (((1,), (1,)), ((), ())), preferred_element_type=F32)


def _dot_tn(a, b):
    return lax.dot_general(a, b, (((0,), (0,)), ((), ())), preferred_element_type=F32)


def _sigmoid(x):
    return 1.0 / (1.0 + jnp.exp(-x))


def _rms_fwd(x, gain, width):
    r = lax.rsqrt(jnp.sum(x * x, axis=-1, keepdims=True) * (1.0 / width) + RMS_EPS)
    return x * r * gain


def _rms_bwd(dy, x, gain, width):
    r = lax.rsqrt(jnp.sum(x * x, axis=-1, keepdims=True) * (1.0 / width) + RMS_EPS)
    xhat = x * r
    dxhat = dy * gain
    dx = r * (dxhat - xhat * (jnp.sum(dxhat * xhat, axis=-1, keepdims=True) * (1.0 / width)))
    return dx, dy * xhat


def _split3(x):
    hi = x.astype(BF16)
    r1 = x - hi.astype(F32)
    mid = r1.astype(BF16)
    lo = (r1 - mid.astype(F32)).astype(BF16)
    return hi, mid, lo


def _tri_sum(tri_bf16, x):
    hi, mid, lo = _split3(x)
    return _dot(tri_bf16, hi) + _dot(tri_bf16, mid) + _dot(tri_bf16, lo)


def _dilated_spec(d, tm, width):
    return pl.BlockSpec((d, tm // d, width), lambda i: (0, i, 0))


def _lane_blocks(ref, value):
    for c in range(ref.shape[0]):
        ref[c] = value[:, c * LANES:(c + 1) * LANES]


def _to_dilated(src_ref, dst_ref, d, tm, cast=None):
    for r in range(d):
        for c in range(src_ref.shape[0]):
            v = src_ref[c] if d == 1 else src_ref[c, pl.ds(r, tm // d, stride=d), :]
            dst_ref[r, :, c * LANES:(c + 1) * LANES] = v if cast is None else v.astype(cast)


def _from_dilated(src_ref, scratch_ref, d, tm):
    if d == 1:
        return src_ref[0].astype(F32)
    nblk = scratch_ref.shape[0]
    for r in range(d):
        for c in range(nblk):
            scratch_ref[c, pl.ds(r, tm // d, stride=d), :] = src_ref[r, :, c * LANES:(c + 1) * LANES].astype(F32)
    return jnp.concatenate([scratch_ref[c] for c in range(nblk)], axis=1)


def in_proj_fwd(x, g1, w_in_b):
    s = x.shape[0]
    tm = PROJ_TILE
    qkv_w = 3 * ATTN_WIDTH
    hg_w = IN_PROJ_WIDTH - qkv_w

    def body(x_ref, g_ref, w_ref, hg_ref, h_ref, *rest):
        qkv_refs, qkv_scr = rest[:len(DILATIONS)], rest[len(DILATIONS)]
        h = _rms_fwd(x_ref[...], g_ref[...], D_MODEL).astype(BF16)
        h_ref[...] = h
        proj = _dot(h, w_ref[...])
        hg_ref[...] = proj[:, qkv_w:]
        _lane_blocks(qkv_scr, proj[:, :qkv_w])
        for d, ref in zip(DILATIONS, qkv_refs):
            _to_dilated(qkv_scr, ref, d, tm, cast=BF16)

    return pl.pallas_call(
        body,
        name="in_proj_fwd",
        grid=(s // tm,),
        in_specs=[
            pl.BlockSpec((tm, D_MODEL), lambda i: (i, 0)),
            pl.BlockSpec((1, D_MODEL), lambda i: (0, 0)),
            _vmem_spec(),
        ],
        out_specs=[
            pl.BlockSpec((tm, hg_w), lambda i: (i, 0)),
            pl.BlockSpec((tm, D_MODEL), lambda i: (i, 0)),
        ] + [_dilated_spec(d, tm, qkv_w) for d in DILATIONS],
        out_shape=[jax.ShapeDtypeStruct((s, hg_w), F32), jax.ShapeDtypeStruct((s, D_MODEL), BF16)] + [
            jax.ShapeDtypeStruct((d, s // d, qkv_w), BF16) for d in DILATIONS],
        scratch_shapes=[pltpu.VMEM((qkv_w // LANES, tm, LANES), F32)],
        compiler_params=_params(dimension_semantics=("arbitrary",)),
    )(x, g1, w_in_b)


ATTN_SCALE = ATTN_HEAD_DIM ** -0.5


def _fill_attn_bias(bias_ref, dilation):
    qi = lax.broadcasted_iota(jnp.int32, (ATTN_BLOCK, 2 * ATTN_BLOCK), 0)
    kj = lax.broadcasted_iota(jnp.int32, (ATTN_BLOCK, 2 * ATTN_BLOCK), 1)
    dist = qi + ATTN_BLOCK - kj
    valid = (dist >= 0) & (dist <= ATTN_BLOCK)
    for head in range(ATTN_HEADS):
        slope = 2.0 ** (-8.0 * (head + 1) / ATTN_HEADS)
        bias = jnp.where(valid, dist.astype(F32) * (-slope * dilation), NEG_BIG)
        bias_ref[0, head] = bias
        bias_ref[1, head] = jnp.where(kj >= ATTN_BLOCK, bias, NEG_BIG)


def _attn_scores(qm, kcat, bias_ref, head, first_block):
    return _dot_nt(qm, kcat) + bias_ref[first_block.astype(jnp.int32), head]


def _lane_half(shape, sub):
    lane = lax.broadcasted_iota(jnp.int32, shape, 1)
    return (lane < ATTN_HEAD_DIM) if sub == 0 else (lane >= ATTN_HEAD_DIM)


def _sub_block(col, row):
    return pl.BlockSpec((None, ATTN_BLOCK, ATTN_WIDTH), lambda r, n: (r, row(n), col))


def attn_fwd(qkv, dilation):
    d, length, _ = qkv.shape
    assert d == dilation
    nb = length // ATTN_BLOCK

    def body(q_ref, kc_ref, kp_ref, vc_ref, vp_ref, o_ref, lse_ref, bias_ref):
        @pl.when((pl.program_id(0) == 0) & (pl.program_id(1) == 0))
        def _():
            _fill_attn_bias(bias_ref, d)

        first = pl.program_id(1) == 0
        for pair in range(ATTN_HEADS // 2):
            lanes = slice(pair * 128, (pair + 1) * 128)
            q2 = q_ref[:, lanes] * ATTN_SCALE
            kcat = jnp.concatenate([kp_ref[:, lanes], kc_ref[:, lanes]], axis=0)
            vcat = jnp.concatenate([vp_ref[:, lanes], vc_ref[:, lanes]], axis=0)
            o_pair = jnp.zeros((ATTN_BLOCK, 128), F32)
            lse_pair = jnp.zeros((ATTN_BLOCK, 128), F32)
            for sub in range(2):
                keep = _lane_half((ATTN_BLOCK, 128), sub)
                qm = jnp.where(keep, q2, jnp.zeros_like(q2))
                sc = _attn_scores(qm, kcat, bias_ref, 2 * pair + sub, first)
                m = jnp.max(sc, axis=-1, keepdims=True)
                p = jnp.exp(sc - m)
                den = jnp.sum(p, axis=-1, keepdims=True)
                o = _dot(p.astype(BF16), vcat) / den
                o_pair = jnp.where(keep, o, o_pair)
                lse_pair = jnp.where(keep, m + jnp.log(den), lse_pair)
            o_ref[:, lanes] = o_pair.astype(BF16)
            lse_ref[:, lanes] = lse_pair

    cur = lambda n: n
    prev = lambda n: jnp.maximum(n - 1, 0)
    return pl.pallas_call(
        body,
        name=f"attn_fwd_d{d}",
        grid=(d, nb),
        in_specs=[_sub_block(0, cur), _sub_block(1, cur), _sub_block(1, prev), _sub_block(2, cur), _sub_block(2, prev)],
        out_specs=[_sub_block(0, cur), _sub_block(0, cur)],
        out_shape=[jax.ShapeDtypeStruct((d, length, ATTN_WIDTH), BF16), jax.ShapeDtypeStruct((d, length, ATTN_WIDTH), F32)],
        scratch_shapes=[pltpu.VMEM((2, ATTN_HEADS, ATTN_BLOCK, 2 * ATTN_BLOCK), F32)],
        compiler_params=_params(dimension_semantics=("arbitrary", "arbitrary")),
    )(qkv, qkv, qkv, qkv, qkv)


def attn_bwd(qkv, d_out, lse, delta, dilation, ride=None):
    d, length, _ = qkv.shape
    assert d == dilation
    nb = length // ATTN_BLOCK

    steps = d * nb + 1

    def body(q_ref, kc_ref, kp_ref, vc_ref, vp_ref, do_ref, lse_ref, dl_ref, dq_ref, dk_ref, dv_ref, ck_ref, cv_ref,
             bias_ref):
        t = pl.program_id(0)

        @pl.when(t == 0)
        def _():
            ck_ref[...] = jnp.zeros_like(ck_ref)
            cv_ref[...] = jnp.zeros_like(cv_ref)
            _fill_attn_bias(bias_ref, d)

        @pl.when(t < steps - 1)
        def _():
            first = t % nb == 0
            for pair in range(ATTN_HEADS // 2):
                lanes = slice(pair * 128, (pair + 1) * 128)
                q2 = q_ref[:, lanes] * ATTN_SCALE
                do2 = do_ref[:, lanes]
                kcat = jnp.concatenate([kp_ref[:, lanes], kc_ref[:, lanes]], axis=0)
                vcat = jnp.concatenate([vp_ref[:, lanes], vc_ref[:, lanes]], axis=0)
                dq_pair = jnp.zeros((ATTN_BLOCK, 128), F32)
                dk_cat = jnp.zeros((2 * ATTN_BLOCK, 128), F32)
                dv_cat = jnp.zeros((2 * ATTN_BLOCK, 128), F32)
                for sub in range(2):
                    keep = _lane_half((ATTN_BLOCK, 128), sub)
                    col = pair * 128 + sub * ATTN_HEAD_DIM
                    qm = jnp.where(keep, q2, jnp.zeros_like(q2))
                    dom = jnp.where(keep, do2, jnp.zeros_like(do2))
                    sc = _attn_scores(qm, kcat, bias_ref, 2 * pair + sub, first)
                    p = jnp.exp(sc - lse_ref[:, col:col + 1])
                    dp = _dot_nt(dom, vcat)
                    ds = (p * (dp - dl_ref[:, col:col + 1])).astype(BF16)
                    dq_pair = jnp.where(keep, _dot(ds, kcat), dq_pair)
                    dk_cat = dk_cat + _dot_tn(ds, qm)
                    dv_cat = dv_cat + _dot_tn(p.astype(BF16), dom)
                dq_ref[:, lanes] = (dq_pair * ATTN_SCALE).astype(BF16)
                dk_ref[:, lanes] = (ck_ref[:, lanes] + dk_cat[:ATTN_BLOCK]).astype(BF16)
                dv_ref[:, lanes] = (cv_ref[:, lanes] + dv_cat[:ATTN_BLOCK]).astype(BF16)
                ck_ref[:, lanes] = dk_cat[ATTN_BLOCK:]
                cv_ref[:, lanes] = dv_cat[ATTN_BLOCK:]

        @pl.when(t == steps - 1)
        def _():
            dk_ref[...] = ck_ref[...].astype(BF16)
            dv_ref[...] = cv_ref[...].astype(BF16)

    blk = (ATTN_BLOCK, ATTN_WIDTH)

    def spec(col, shift):
        def index(t):
            f = jnp.minimum(t, steps - 2) if shift > -2 else jnp.maximum(t - 1, 0)
            r, n = f // nb, f % nb
            return (r, jnp.maximum(n - 1, 0) if shift == -1 else n, col)
        return pl.BlockSpec((None, ATTN_BLOCK, ATTN_WIDTH), index)

    step = lambda k: (lambda: pl.program_id(0) == k)
    e_in, e_out, e_shape, e_scr, e_args = _ride_specs(ride)
    return pl.pallas_call(
        _riding(body, 8, 3, 3, ride, step(0), step(steps // 2), step(steps - 1)),
        name=f"attn_bwd_d{d}",
        grid=(steps,),
        in_specs=[spec(0, 0), spec(1, 0), spec(1, -1), spec(2, 0), spec(2, -1), spec(0, 0), spec(0, 0), spec(0, 0)] + e_in,
        out_specs=[spec(0, 0), spec(0, -2), spec(0, -2)] + e_out,
        out_shape=[jax.ShapeDtypeStruct((d, length, ATTN_WIDTH), BF16)] * 3 + e_shape,
        scratch_shapes=[pltpu.VMEM(blk, F32), pltpu.VMEM(blk, F32),
                        pltpu.VMEM((2, ATTN_HEADS, ATTN_BLOCK, 2 * ATTN_BLOCK), F32)] + e_scr,
        compiler_params=_params(dimension_semantics=("arbitrary",)),
    )(qkv, qkv, qkv, qkv, qkv, d_out, lse, delta, *e_args)


def _lower_bound(logits):
    return _sigmoid(logits[0:1, :] - logits[1:2, :])


def _hgrn_gates(q, fp, lb):
    sq = _sigmoid(q)
    qf = q * sq
    sig = _sigmoid(fp)
    f = lb + (1.0 - lb) * sig
    kf = (1.0 - lb) * _sigmoid(-fp)
    return sq, qf, sig, f, kf


def _tril_bf16(n, upper=False):
    r = lax.broadcasted_iota(jnp.int32, (n, n), 0)
    c = lax.broadcasted_iota(jnp.int32, (n, n), 1)
    keep = (c >= r) if upper else (c <= r)
    return jnp.where(keep, 1.0, 0.0).astype(BF16)


def _hgrn_diagonal_loops(c_len, diagonal):
    for half in range(SUB_BLOCK // SUBLANES):
        def step(jj, carry, half=half):
            j = half * SUBLANES + jj
            for i in range(c_len // SUB_BLOCK):
                diagonal(slice(i * SUB_BLOCK + half * SUBLANES, (i + 1) * SUB_BLOCK), j, i * SUB_BLOCK + j)
            return carry

        lax.fori_loop(0, SUBLANES, step, 0, unroll=COLUMN_UNROLL)


def _hgrn_off_diagonal(b, qf, kf):
    c_len, width = b.shape
    edges = [b[0:1, :]] + [b[i * SUB_BLOCK - 1:i * SUB_BLOCK, :] for i in range(1, c_len // SUB_BLOCK)]
    eq = jnp.exp(b - jnp.concatenate([jnp.broadcast_to(e, (SUB_BLOCK, width)) for e in edges], axis=0))
    q_til = qf * eq
    k_til, ek = [], []
    for i in range(1, c_len // SUB_BLOCK):
        n = i * SUB_BLOCK
        e = jnp.exp(edges[i] - b[:n, :])
        ek.append(e)
        k_til.append(jnp.concatenate([kf[:n, :] * e, jnp.zeros((2 * c_len - n, width), F32)], axis=0))
    return q_til, k_til, eq, ek


def _split2(x):
    hi = x.astype(BF16)
    return hi, (x - hi.astype(F32)).astype(BF16)


def hgrn_fwd(proj, lb, ride=None):
    s = proj.shape[0]
    c_len, nh, hd = HGRN_CHUNK, HGRN_HEADS, HGRN_HEAD_DIM
    n_chunks = s // c_len
    col0 = 0

    def body(q_ref, f_ref, i_ref, lb_ref, o_ref, st_out_ref, a_out_ref, st_ref, b_ref, qf_ref, kf_ref, a_ref):
        @pl.when(pl.program_id(0) == 0)
        def _():
            st_ref[...] = jnp.zeros_like(st_ref)

        lbv = _lower_bound(lb_ref[...])
        _, qf, _, f, kf = _hgrn_gates(q_ref[...], f_ref[...], lbv)
        b = _tri_sum(_tril_bf16(c_len), jnp.log(f))
        b_ref[...] = b
        qf_ref[...] = qf
        kf_ref[...] = kf
        a_ref[...] = jnp.zeros_like(a_ref)

        def diagonal(rows, j, key):
            bj = b_ref[pl.ds(key, 1), :]
            kj = kf_ref[pl.ds(key, 1), :]
            nrow = rows.stop - rows.start
            t_loc = lax.broadcasted_iota(jnp.int32, (nrow, nh * hd), 0) + (rows.start % SUB_BLOCK)
            e = jnp.exp(jnp.where(t_loc >= j, b_ref[rows, :] - bj, NEG_BIG))
            prod = qf_ref[rows, :] * kj * e
            lane = lax.broadcasted_iota(jnp.int32, (nrow, hd), 1)
            for h in range(nh):
                col = jnp.sum(prod[:, h * hd:(h + 1) * hd], axis=-1, keepdims=True)
                a_ref[h, rows, :] = jnp.where(lane == key, col, a_ref[h, rows, :])

        _hgrn_diagonal_loops(c_len, diagonal)
        q_til, k_til, _, _ = _hgrn_off_diagonal(b, qf, kf)
        q_til = q_til.astype(BF16)
        k_til = [k.astype(BF16) for k in k_til]

        b_last = b[c_len - 1:c_len, :]
        qb = (qf * jnp.exp(b)).astype(BF16)
        kb2 = (kf * jnp.exp(b_last - b)).astype(BF16)
        vf = i_ref[...].astype(BF16)
        for h in range(nh):
            hs = slice(h * hd, (h + 1) * hd)
            st = st_ref[h]
            st_out_ref[0, h] = st
            off = [jnp.zeros((SUB_BLOCK, hd), F32)]
            for i in range(1, c_len // SUB_BLOCK):
                off.append(_dot_nt(q_til[i * SUB_BLOCK:(i + 1) * SUB_BLOCK, hs], k_til[i - 1][:, hs]))
            a_h = a_ref[h] + jnp.concatenate(off, axis=0)
            a_out_ref[:, hs] = a_h
            o_ref[:, hs] = _dot_nt(qb[:, hs], st.astype(BF16)) + _dot(a_h[:, :c_len].astype(BF16), vf[:, hs])
            st_ref[h] = st * jnp.exp(b_last[:, hs]) + _dot_tn(vf[:, hs], kb2[:, hs])

    blk = (c_len, HGRN_WIDTH)
    step = lambda k: (lambda: pl.program_id(0) == k)
    e_in, e_out, e_shape, e_scr, e_args = _ride_specs(ride)
    return pl.pallas_call(
        _riding(body, 4, 3, 5, ride, step(0), step((7 * n_chunks) // 8), step(n_chunks - 1)),
        name="hgrn_fwd",
        grid=(n_chunks,),
        in_specs=[
            pl.BlockSpec(blk, lambda c: (c, col0)),
            pl.BlockSpec(blk, lambda c: (c, col0 + 1)),
            pl.BlockSpec(blk, lambda c: (c, col0 + 2)),
            pl.BlockSpec((2, HGRN_WIDTH), lambda c: (0, 0)),
        ] + e_in,
        out_specs=[
            pl.BlockSpec(blk, lambda c: (c, 0)),
            pl.BlockSpec((1, nh, hd, hd), lambda c: (c, 0, 0, 0)),
            pl.BlockSpec(blk, lambda c: (c, 0)),
        ] + e_out,
        out_shape=[
            jax.ShapeDtypeStruct((s, HGRN_WIDTH), F32),
            jax.ShapeDtypeStruct((n_chunks, nh, hd, hd), F32),
            jax.ShapeDtypeStruct((s, nh * hd), F32),
        ] + e_shape,
        scratch_shapes=[
            pltpu.VMEM((nh, hd, hd), F32),
            pltpu.VMEM(blk, F32),
            pltpu.VMEM(blk, F32),
            pltpu.VMEM(blk, F32),
            pltpu.VMEM((nh, c_len, hd), F32),
        ] + e_scr,
        compiler_params=_params(dimension_semantics=("arbitrary",)),
    )(proj, proj, proj, lb, *e_args)


def hgrn_bwd(proj, lb, d_o, states, a_mat, ride=None):
    s = proj.shape[0]
    c_len, nh, hd = HGRN_CHUNK, HGRN_HEADS, HGRN_HEAD_DIM
    n_chunks = s // c_len
    col0 = 0
    last = n_chunks - 1

    def body(q_ref, f_ref, i_ref, lb_ref, do_ref, st_in_ref, a_in_ref, dq_ref, df_ref, di_ref, dlb_ref,
             dst_ref, b_ref, qf_ref, kf_ref, da_ref, dqi_ref, dki_ref):
        @pl.when(pl.program_id(0) == 0)
        def _():
            dst_ref[...] = jnp.zeros_like(dst_ref)
            dlb_ref[...] = jnp.zeros_like(dlb_ref)

        lbv = _lower_bound(lb_ref[...])
        q = q_ref[...]
        sq, qf, sig, f, kf = _hgrn_gates(q, f_ref[...], lbv)
        b = _tri_sum(_tril_bf16(c_len), jnp.log(f))
        b_ref[...] = b
        qf_ref[...] = qf
        kf_ref[...] = kf
        b_last = b[c_len - 1:c_len, :]
        eb = jnp.exp(b)
        ebl = jnp.exp(b_last - b)
        qb = qf * eb
        kb2 = kf * ebl
        vf = i_ref[...]
        d_o = do_ref[...]
        qb_b, kb2_b, vf_b, do_b = qb.astype(BF16), kb2.astype(BF16), vf.astype(BF16), d_o.astype(BF16)
        tq = lax.broadcasted_iota(jnp.int32, (c_len, hd), 0)
        lane = lax.broadcasted_iota(jnp.int32, (c_len, hd), 1)

        dqb_parts, dvf_parts, dkb2_parts, dbl_parts = [], [], [], []
        for h in range(nh):
            hs = slice(h * hd, (h + 1) * hd)
            st = st_in_ref[0, h]
            dst = dst_ref[h]
            st_b, dst_b = st.astype(BF16), dst.astype(BF16)
            a_h = a_in_ref[:, hs][:, :c_len].astype(BF16)
            dqb_parts.append(_dot(do_b[:, hs], st_b))
            dvf_parts.append(_dot_tn(a_h, do_b[:, hs]) + _dot_nt(kb2_b[:, hs], dst_b))
            dkb2_parts.append(_dot(vf_b[:, hs], dst_b))
            da = _dot_nt(do_b[:, hs], vf_b[:, hs])
            da = jnp.concatenate([da, jnp.zeros((c_len, hd - c_len), F32)], axis=1)
            da_ref[h] = jnp.where(tq >= lane, da, 0.0)
            dbl_parts.append(jnp.sum(dst * st, axis=0, keepdims=True) * jnp.exp(b_last[:, hs]))
            dst_ref[h] = dst * jnp.exp(b_last[:, hs]) + _dot_tn(do_b[:, hs], qb_b[:, hs])
        dqb = jnp.concatenate(dqb_parts, axis=1)
        dvf = jnp.concatenate(dvf_parts, axis=1)
        dkb2 = jnp.concatenate(dkb2_parts, axis=1)
        dbl = jnp.concatenate(dbl_parts, axis=1) + jnp.sum(dkb2 * kb2, axis=0, keepdims=True)

        dqi_ref[...] = jnp.zeros_like(dqi_ref)
        t_idx = lax.broadcasted_iota(jnp.int32, (c_len, nh * hd), 0)

        def diagonal(rows, j, key):
            bj = b_ref[pl.ds(key, 1), :]
            kj = kf_ref[pl.ds(key, 1), :]
            nrow = rows.stop - rows.start
            t_loc = lax.broadcasted_iota(jnp.int32, (nrow, nh * hd), 0) + (rows.start % SUB_BLOCK)
            e = jnp.exp(jnp.where(t_loc >= j, b_ref[rows, :] - bj, NEG_BIG))
            lane_r = lax.broadcasted_iota(jnp.int32, (nrow, hd), 1)
            cols = [jnp.sum(jnp.where(lane_r == key, da_ref[h, rows, :], 0.0), axis=-1, keepdims=True)
                    for h in range(nh)]
            w = e * jnp.concatenate([jnp.broadcast_to(cc, (nrow, hd)) for cc in cols], axis=1)
            dqi_ref[rows, :] += w * kj
            dki_ref[pl.ds(key, 1), :] = jnp.sum(w * qf_ref[rows, :], axis=0, keepdims=True)

        _hgrn_diagonal_loops(c_len, diagonal)

        q_til, k_til, eq, ek = _hgrn_off_diagonal(b, qf, kf)
        q_hi, q_lo = _split2(q_til)
        k_pairs = [_split2(k) for k in k_til]
        n_sub = c_len // SUB_BLOCK
        dq_heads, dk_heads = [], []
        for h in range(nh):
            hs = slice(h * hd, (h + 1) * hd)
            dq_rows = [jnp.zeros((SUB_BLOCK, hd), F32)]
            dk_h = jnp.zeros((c_len, hd), F32)
            for i in range(1, n_sub):
                rows = slice(i * SUB_BLOCK, (i + 1) * SUB_BLOCK)
                n = i * SUB_BLOCK
                da_i = da_ref[h, rows, :].astype(BF16)
                k_hi, k_lo = k_pairs[i - 1]
                dq_rows.append((_dot(da_i, k_hi[:, hs]) + _dot(da_i, k_lo[:, hs])) * eq[rows, hs])
                dk_t = (_dot_tn(da_i, q_hi[rows, hs]) + _dot_tn(da_i, q_lo[rows, hs]))[:n, :] * ek[i - 1][:, hs]
                dk_h = dk_h + jnp.concatenate([dk_t, jnp.zeros((c_len - n, hd), F32)], axis=0)
            dq_heads.append(jnp.concatenate(dq_rows, axis=0))
            dk_heads.append(dk_h)
        dq_intra = dqi_ref[...] + jnp.concatenate(dq_heads, axis=1)
        dk_intra = dki_ref[...] + jnp.concatenate(dk_heads, axis=1)

        db = dqb * qb + qf * dq_intra - kf * dk_intra - dkb2 * kb2
        db = db + jnp.where(t_idx == c_len - 1, dbl, 0.0)
        dg = _tri_sum(_tril_bf16(c_len, upper=True), db)
        dqf = dqb * eb + dq_intra
        dkf = dkb2 * ebl + dk_intra
        dq_ref[...] = (dqf * (sq * (1.0 + q * (1.0 - sq)))).astype(BF16)
        dfv = dg / f - dkf
        df_ref[...] = (dfv * (1.0 - lbv) * sig * (1.0 - sig)).astype(BF16)
        di_ref[...] = dvf.astype(BF16)
        dlb_ref[...] += jnp.sum(dfv * (1.0 - sig), axis=0, keepdims=True)

    blk = (c_len, HGRN_WIDTH)
    rev = lambda c: last - c
    step = lambda k: (lambda: pl.program_id(0) == k)
    e_in, e_out, e_shape, e_scr, e_args = _ride_specs(ride)
    return pl.pallas_call(
        _riding(body, 7, 4, 7, ride, step(0), step(n_chunks // 2), step(last)),
        name="hgrn_bwd",
        grid=(n_chunks,),
        in_specs=[
            pl.BlockSpec(blk, lambda c: (rev(c), col0)),
            pl.BlockSpec(blk, lambda c: (rev(c), col0 + 1)),
            pl.BlockSpec(blk, lambda c: (rev(c), col0 + 2)),
            pl.BlockSpec((2, HGRN_WIDTH), lambda c: (0, 0)),
            pl.BlockSpec(blk, lambda c: (rev(c), 0)),
            pl.BlockSpec((1, nh, hd, hd), lambda c: (rev(c), 0, 0, 0)),
            pl.BlockSpec(blk, lambda c: (rev(c), 0)),
        ] + e_in,
        out_specs=[
            pl.BlockSpec(blk, lambda c: (rev(c), 0)),
            pl.BlockSpec(blk, lambda c: (rev(c), 0)),
            pl.BlockSpec(blk, lambda c: (rev(c), 0)),
            pl.BlockSpec((1, HGRN_WIDTH), lambda c: (0, 0)),
        ] + e_out,
        out_shape=[jax.ShapeDtypeStruct((s, HGRN_WIDTH), BF16)] * 3 + [jax.ShapeDtypeStruct((1, HGRN_WIDTH), F32)] + e_shape,
        scratch_shapes=[
            pltpu.VMEM((nh, hd, hd), F32),
            pltpu.VMEM(blk, F32),
            pltpu.VMEM(blk, F32),
            pltpu.VMEM(blk, F32),
            pltpu.VMEM((nh, c_len, hd), F32),
            pltpu.VMEM(blk, F32),
            pltpu.VMEM(blk, F32),
        ] + e_scr,
        compiler_params=_params(dimension_semantics=("arbitrary",)),
    )(proj, proj, proj, lb, d_o, states, a_mat, *e_args)


def _row_spec(tm, width, col=0):
    return pl.BlockSpec((tm, width), lambda i: (i, col))


def _const_spec(width):
    return pl.BlockSpec((1, width), lambda i: (0, 0))


def _acc_rows(ref, value):
    @pl.when(pl.program_id(0) == 0)
    def _():
        ref[...] = jnp.zeros_like(ref)

    ref[...] += jnp.sum(value, axis=0, keepdims=True)


def mix_fwd(attn_parts, o_h, proj, an, hn, w_out_b, gp, x, ride=None):
    s = x.shape[0]
    tm = TOKEN_TILE
    gate_col = 3
    hd = HGRN_HEAD_DIM
    nd = len(DILATIONS)

    def body(*refs):
        o_refs, l_refs = refs[:nd], refs[nd:2 * nd]
        oh_ref, gate_ref, an_ref, hn_ref, w_ref, gp_ref, x_ref = refs[2 * nd:2 * nd + 7]
        x1_ref, cat_ref, mixed_ref, attn_ref = refs[2 * nd + 7:2 * nd + 11]
        lse_refs = refs[2 * nd + 11:3 * nd + 11]
        o_scr, l_scr, lse_scr = refs[3 * nd + 11:]
        os_ = [_from_dilated(r, o_scr.at[k], d, tm) for k, (r, d) in enumerate(zip(o_refs, DILATIONS))]
        ls = [_from_dilated(r, l_scr.at[k], d, tm) for k, (r, d) in enumerate(zip(l_refs, DILATIONS))]
        m = jnp.maximum(jnp.maximum(ls[0], ls[1]), ls[2])
        es = [jnp.exp(l - m) for l in ls]
        den = es[0] + es[1] + es[2]
        attn = (es[0] * os_[0] + es[1] * os_[1] + es[2] * os_[2]) / den
        attn_ref[...] = attn
        _lane_blocks(lse_scr, m + jnp.log(den))
        for d, ref in zip(DILATIONS, lse_refs):
            _to_dilated(lse_scr, ref, d, tm)
        cat_ref[:, :ATTN_WIDTH] = _rms_fwd(attn, an_ref[...], ATTN_WIDTH).astype(BF16)
        gate = gate_ref[...]
        silu_g = gate * _sigmoid(gate)
        for h in range(HGRN_HEADS):
            hs = slice(h * hd, (h + 1) * hd)
            rec = _rms_fwd(oh_ref[:, hs], hn_ref[:, hs], hd) * silu_g[:, hs]
            cat_ref[:, ATTN_WIDTH + h * hd:ATTN_WIDTH + (h + 1) * hd] = rec.astype(BF16)
        mixed = _dot(cat_ref[...], w_ref[...])
        mixed_ref[...] = mixed
        x1_ref[...] = x_ref[...] + _rms_fwd(mixed, gp_ref[...], D_MODEL)

    aw = ATTN_WIDTH
    n_steps = s // tm
    step = lambda k: (lambda: pl.program_id(0) == k)
    e_in, e_out, e_shape, e_scr, e_args = _ride_specs(ride)
    return pl.pallas_call(
        _riding(body, 2 * nd + 7, 4 + nd, 3, ride, step(0), step((13 * n_steps) // 16), step(n_steps - 1)),
        name="mix_fwd",
        grid=(n_steps,),
        in_specs=[_dilated_spec(d, tm, aw) for d in DILATIONS] * 2 + [
            _row_spec(tm, aw), _row_spec(tm, aw, gate_col), _const_spec(aw), _const_spec(aw), _vmem_spec(),
            _const_spec(D_MODEL), _row_spec(tm, D_MODEL)] + e_in,
        out_specs=[_row_spec(tm, D_MODEL), _row_spec(tm, D_MODEL), _row_spec(tm, D_MODEL), _row_spec(tm, aw)] + [
            _dilated_spec(d, tm, aw) for d in DILATIONS] + e_out,
        out_shape=[
            jax.ShapeDtypeStruct((s, D_MODEL), F32),
            jax.ShapeDtypeStruct((s, D_MODEL), BF16),
            jax.ShapeDtypeStruct((s, D_MODEL), F32),
            jax.ShapeDtypeStruct((s, aw), F32),
        ] + [jax.ShapeDtypeStruct((d, s // d, aw), F32) for d in DILATIONS] + e_shape,
        scratch_shapes=[pltpu.VMEM((nd, aw // LANES, tm, LANES), F32), pltpu.VMEM((nd, aw // LANES, tm, LANES), F32),
                        pltpu.VMEM((aw // LANES, tm, LANES), F32)] + e_scr,
        compiler_params=_params(dimension_semantics=("arbitrary",)),
    )(*[p[0] for p in attn_parts], *[p[1] for p in attn_parts], o_h, proj, an, hn, w_out_b, gp, x, *e_args)


def mix_bwd(dx1, mixed, gp, w_out_b, attn, an, o_h, proj, hn):
    s = dx1.shape[0]
    tm = TOKEN_TILE
    gate_col = 3
    hd = HGRN_HEAD_DIM
    aw = ATTN_WIDTH

    nd = len(DILATIONS)

    def body(*refs):
        dx1_ref, mixed_ref, gp_ref, w_ref, attn_ref, an_ref, oh_ref, gate_ref, hn_ref, dmix_ref = refs[:10]
        do_refs, delta_refs = refs[10:10 + nd], refs[10 + nd:10 + 2 * nd]
        doh_ref, dgate_ref, dgp_ref, dan_ref, dhn_ref, do_ref, delta_ref = refs[10 + 2 * nd:]
        dmixed, gp_c = _rms_bwd(dx1_ref[...], mixed_ref[...], gp_ref[...], D_MODEL)
        _acc_rows(dgp_ref, gp_c)
        dmixed_b = dmixed.astype(BF16)
        dmix_ref[...] = dmixed_b
        dcat = _dot_nt(dmixed_b, w_ref[...])
        attn = attn_ref[...]
        d_o, an_c = _rms_bwd(dcat[:, :aw], attn, an_ref[...], aw)
        _acc_rows(dan_ref, an_c)
        _lane_blocks(do_ref, d_o)
        prod = d_o * attn
        for pair in range(ATTN_HEADS // 2):
            pp = prod[:, pair * LANES:(pair + 1) * LANES]
            low = _lane_half((tm, LANES), 0)
            lo = jnp.sum(jnp.where(low, pp, 0.0), axis=-1, keepdims=True)
            hi = jnp.sum(jnp.where(low, 0.0, pp), axis=-1, keepdims=True)
            delta_ref[pair] = jnp.where(low, lo, hi)
        for d, o_ref, l_ref in zip(DILATIONS, do_refs, delta_refs):
            _to_dilated(do_ref, o_ref, d, tm, cast=BF16)
            _to_dilated(delta_ref, l_ref, d, tm)
        gate = gate_ref[...]
        sg = _sigmoid(gate)
        silu_g = gate * sg
        drec = dcat[:, aw:]
        hn_parts = []
        for h in range(HGRN_HEADS):
            hs = slice(h * hd, (h + 1) * hd)
            oh = oh_ref[:, hs]
            on = _rms_fwd(oh, hn_ref[:, hs], hd)
            dgate_ref[:, hs] = (drec[:, hs] * on * (sg[:, hs] * (1.0 + gate[:, hs] * (1.0 - sg[:, hs])))).astype(BF16)
            d_oh, hn_c = _rms_bwd(drec[:, hs] * silu_g[:, hs], oh, hn_ref[:, hs], hd)
            doh_ref[:, hs] = d_oh
            hn_parts.append(hn_c)
        _acc_rows(dhn_ref, jnp.concatenate(hn_parts, axis=1))

    return pl.pallas_call(
        body,
        name="mix_bwd",
        grid=(s // tm,),
        in_specs=[_row_spec(tm, D_MODEL), _row_spec(tm, D_MODEL), _const_spec(D_MODEL), _vmem_spec(), _row_spec(tm, aw),
                  _const_spec(aw), _row_spec(tm, aw), _row_spec(tm, aw, gate_col), _const_spec(aw)],
        out_specs=[_row_spec(tm, D_MODEL)] + [_dilated_spec(d, tm, aw) for d in DILATIONS] * 2 + [_row_spec(tm, aw)] * 2 + [
            _const_spec(D_MODEL), _const_spec(aw), _const_spec(aw)],
        out_shape=[jax.ShapeDtypeStruct((s, D_MODEL), BF16)] + [
            jax.ShapeDtypeStruct((d, s // d, aw), BF16) for d in DILATIONS] + [
            jax.ShapeDtypeStruct((d, s // d, aw), F32) for d in DILATIONS] + [
            jax.ShapeDtypeStruct((s, aw), F32), jax.ShapeDtypeStruct((s, aw), BF16),
            jax.ShapeDtypeStruct((1, D_MODEL), F32), jax.ShapeDtypeStruct((1, aw), F32),
            jax.ShapeDtypeStruct((1, aw), F32)],
        scratch_shapes=[pltpu.VMEM((aw // LANES, tm, LANES), F32), pltpu.VMEM((aw // LANES, tm, LANES), F32)],
        compiler_params=_params(dimension_semantics=("arbitrary",)),
    )(dx1, mixed, gp, w_out_b, attn, an, o_h, proj, hn)


def mlp_fwd_bwd(x1, g_pre, w1_blocks, w2_b, g_post, target):
    s = x1.shape[0]
    tm = MLP_TILE
    nblk, _, fb = w1_blocks.shape

    def body(x1_ref, gpre_ref, w1_ref, w2_ref, gpost_ref, t_ref,
             dx1_ref, h2_ref, a_ref, du_ref, dff_ref, loss_ref, dgpre_ref, dgpost_ref, u_ref):
        x1v = x1_ref[...]
        h2 = _rms_fwd(x1v, gpre_ref[...], D_MODEL).astype(BF16)
        h2_ref[...] = h2
        ff = jnp.zeros((tm, D_MODEL), F32)
        for j in range(nblk):
            cols = slice(j * fb, (j + 1) * fb)
            ru = jnp.maximum(_dot(h2, w1_ref[j]), 0.0)
            u_ref[:, cols] = ru.astype(BF16)
            a = (ru * ru).astype(BF16)
            a_ref[:, cols] = a
            ff = ff + _dot(a, w2_ref[cols, :])
        diff = x1v + _rms_fwd(ff, gpost_ref[...], D_MODEL) - t_ref[...]
        _acc_rows(loss_ref, diff * diff)
        dy = diff * (1.0 / D_MODEL)
        dff, gpost_c = _rms_bwd(dy, ff, gpost_ref[...], D_MODEL)
        _acc_rows(dgpost_ref, gpost_c)
        dff_b = dff.astype(BF16)
        dff_ref[...] = dff_b
        dh2 = jnp.zeros((tm, D_MODEL), F32)
        for j in range(nblk):
            cols = slice(j * fb, (j + 1) * fb)
            du = (_dot_nt(dff_b, w2_ref[cols, :]) * (2.0 * u_ref[:, cols])).astype(BF16)
            du_ref[:, cols] = du
            dh2 = dh2 + _dot_nt(du, w1_ref[j])
        dxa, gpre_c = _rms_bwd(dh2, x1v, gpre_ref[...], D_MODEL)
        _acc_rows(dgpre_ref, gpre_c)
        dx1_ref[...] = dy + dxa

    dm = D_MODEL
    return pl.pallas_call(
        body,
        name="mlp_fwd_bwd",
        grid=(s // tm,),
        in_specs=[_row_spec(tm, dm), _const_spec(dm), _vmem_spec(), _vmem_spec(), _const_spec(dm), _row_spec(tm, dm)],
        out_specs=[_row_spec(tm, dm), _row_spec(tm, dm), _row_spec(tm, D_FF), _row_spec(tm, D_FF), _row_spec(tm, dm),
                   _const_spec(dm), _const_spec(dm), _const_spec(dm)],
        out_shape=[
            jax.ShapeDtypeStruct((s, dm), F32),
            jax.ShapeDtypeStruct((s, dm), BF16),
            jax.ShapeDtypeStruct((s, D_FF), BF16),
            jax.ShapeDtypeStruct((s, D_FF), BF16),
            jax.ShapeDtypeStruct((s, dm), BF16),
            jax.ShapeDtypeStruct((1, dm), F32),
            jax.ShapeDtypeStruct((1, dm), F32),
            jax.ShapeDtypeStruct((1, dm), F32),
        ],
        scratch_shapes=[pltpu.VMEM((tm, D_FF), BF16)],
        compiler_params=_params(dimension_semantics=("arbitrary",)),
    )(x1, g_pre, w1_blocks, w2_b, g_post, target)


def in_proj_bwd(attn_grads, hgrn_grads, dgate, w_in_b, x, g1, dx1):
    s = x.shape[0]
    tm = PROJ_TILE
    aw = ATTN_WIDTH
    n_attn = len(attn_grads)
    flat = [g[k] for k in range(3) for g in attn_grads] + list(hgrn_grads) + [dgate]

    def body(*refs):
        parts = refs[:len(flat)]
        w_ref, x_ref, g_ref, dx1_ref, dx_ref, dproj_ref, dg_ref, scr = refs[len(flat):]
        groups = []
        for k in range(3):
            acc = None
            for p, d in zip(parts[k * n_attn:(k + 1) * n_attn], DILATIONS):
                v = _from_dilated(p, scr, d, tm)
                acc = v if acc is None else acc + v
            groups.append(acc)
        groups += [p[...] for p in parts[3 * n_attn:]]
        dh = jnp.zeros((tm, D_MODEL), F32)
        for gi, grp in enumerate(groups):
            cols = slice(gi * aw, (gi + 1) * aw)
            gb = grp.astype(BF16)
            dproj_ref[:, cols] = gb
            dh = dh + _dot_nt(gb, w_ref[:, cols])
        dxa, g_c = _rms_bwd(dh, x_ref[...], g_ref[...], D_MODEL)
        _acc_rows(dg_ref, g_c)
        dx_ref[...] = dx1_ref[...] + dxa

    dm = D_MODEL
    return pl.pallas_call(
        body,
        name="in_proj_bwd",
        grid=(s // tm,),
        in_specs=[_dilated_spec(d, tm, aw) for d in DILATIONS] * 3 + [_row_spec(tm, aw)] * 4 + [
            _vmem_spec(), _row_spec(tm, dm), _const_spec(dm), _row_spec(tm, dm)],
        out_specs=[_row_spec(tm, dm), _row_spec(tm, IN_PROJ_WIDTH), _const_spec(dm)],
        out_shape=[jax.ShapeDtypeStruct((s, dm), F32), jax.ShapeDtypeStruct((s, IN_PROJ_WIDTH), BF16),
                   jax.ShapeDtypeStruct((1, dm), F32)],
        scratch_shapes=[pltpu.VMEM((aw // LANES, tm, LANES), F32)],
        compiler_params=_params(dimension_semantics=("arbitrary",)),
    )(*flat, w_in_b, x, g1, dx1)


def wgrad(a_b, b_b, tn, name, ts=1024, per_step=1):
    s, k = a_b.shape
    n = b_b.shape[1]

    def body(a_ref, b_ref, o_ref):
        @pl.when(pl.program_id(1) == 0)
        def _():
            o_ref[...] = jnp.zeros_like(o_ref)

        a = a_ref[...]
        for jj in range(per_step):
            o_ref[jj] += _dot_tn(a, b_ref[:, jj * tn:(jj + 1) * tn])

    wide = tn * per_step
    return pl.pallas_call(
        body,
        name=name,
        grid=(n // wide, s // ts),
        in_specs=[pl.BlockSpec((ts, k), lambda j, i: (i, 0)), pl.BlockSpec((ts, wide), lambda j, i: (i, j))],
        out_specs=pl.BlockSpec((per_step, k, tn), lambda j, i: (j, 0, 0)),
        out_shape=jax.ShapeDtypeStruct((n // tn, k, tn), F32),
        compiler_params=_params(dimension_semantics=("arbitrary", "arbitrary")),
    )(a_b, b_b)


def train_step(x, target, g1, an, logits, hn, gp, g_pre, g_post, w, m, v):
    nd = len(DILATIONS)
    shard_b = {k: w[k].astype(BF16) for k in BIG}
    (w_in_g,) = run_exchange(gather_exchange([shard_b["w_in"]]), "gather_w_in")
    w_in_b = w_in_g.transpose(1, 0, 2).reshape(D_MODEL, IN_PROJ_WIDTH)

    proj, h_b, *qkvs = in_proj_fwd(x, g1, w_in_b)
    attn_parts = [attn_fwd(qkv, d) for qkv, d in zip(qkvs, DILATIONS)]
    o_h, states, a_mat, w_out_g, w1_blocks = hgrn_fwd(
        proj, logits, ride=gather_exchange([shard_b["w_out"], shard_b["w_ff1"]]))
    w_out_b = w_out_g.reshape(D_MODEL, D_MODEL)
    x1, cat_b, mixed, attn, *lses, w2_g = mix_fwd(attn_parts, o_h, proj, an, hn, w_out_b, gp, x,
                                                  ride=gather_exchange([shard_b["w_ff2"]]))
    w2_b = w2_g.reshape(D_FF, D_MODEL)
    dx1, h2_b, a_b, du_b, dff_b, loss_vec, dg_pre, dg_post = mlp_fwd_bwd(x1, g_pre, w1_blocks, w2_b, g_post, target)
    dw2 = wgrad(a_b, dff_b, D_MODEL, "wgrad_ff2", ts=512)
    dw1 = wgrad(h2_b, du_b, D_FF // N_DEV, "wgrad_ff1", per_step=2)
    dmix_b, *rest = mix_bwd(dx1, mixed, gp, w_out_b, attn, an, o_h, proj, hn)
    d_os, deltas = rest[:nd], rest[nd:2 * nd]
    d_oh, dgate, dgp, dan, dhn = rest[2 * nd:]
    dwout = wgrad(cat_b, dmix_b, D_MODEL, "wgrad_out")

    early = ("w_out", "w_ff1", "w_ff2")
    early_grads = [dwout.reshape(N_DEV, D_MODEL // N_DEV, D_MODEL), dw1, dw2.reshape(N_DEV, D_FF // N_DEV, D_MODEL)]
    attn_grads = []
    for k, d in enumerate(DILATIONS):
        ride = to_core_exchange(early_grads) if k == 0 else None
        res = attn_bwd(qkvs[k], d_os[k], lses[k], deltas[k], d, ride=ride)
        attn_grads.append(res[:3])
        if k == 0:
            pairs = [pair_sum(g, s, f"pair_sum_{name}") for g, s, name in zip(early_grads, res[3:], early)]
    dq_h, df_h, di_h, dlb, *others = hgrn_bwd(proj, logits, d_oh, states, a_mat,
                                              ride=to_chip_exchange([p[1] for p in pairs]))
    dx, dproj_b, dg1 = in_proj_bwd(attn_grads, (dq_h, df_h, di_h), dgate, w_in_b, x, g1, dx1)
    dwin = wgrad(h_b, dproj_b, 2 * IN_PROJ_WIDTH // N_DEV, "wgrad_in")
    big = {name: sum_adamw(p[0], o, w[name], m[name], v[name], f"sum_adamw_{name}")
           for name, p, o in zip(early, pairs, others)}

    shard_w = IN_PROJ_WIDTH // N_DEV
    dwin_blocks = dwin.reshape(N_DEV // 2, D_MODEL, 2, shard_w).transpose(0, 2, 1, 3).reshape(N_DEV, D_MODEL, shard_w)
    (from_sibling,) = run_exchange(to_core_exchange([dwin_blocks]), "reduce_w_in_to_core")
    pair_in, pair_in_b = pair_sum(dwin_blocks, from_sibling, "pair_sum_w_in")
    (others_in,) = run_exchange(to_chip_exchange([pair_in_b]), "reduce_w_in_to_chip")
    big["w_in"] = sum_adamw(pair_in, others_in, w["w_in"], m["w_in"], v["w_in"], "sum_adamw_w_in")
    small = dict(dg1=dg1, dan=dan, dlb=dlb, dhn=dhn, dgp=dgp, dg_pre=dg_pre, dg_post=dg_post, loss_vec=loss_vec)
    return dx, big, small


def _position():
    x, y, c = lax.axis_index("x"), lax.axis_index("y"), lax.axis_index("c")
    other_chips = [(1 - x, y), (x, 1 - y), (1 - x, 1 - y)]
    return x, y, c, other_chips


def _any_spec():
    return pl.BlockSpec(memory_space=pl.ANY)


class Exchange:
    def __init__(self, arrays, out_shape, sems, stages):
        self.arrays, self.out_shape, self.sems, self.stages = list(arrays), list(out_shape), list(sems), stages


def gather_exchange(shards):
    n = len(shards)

    def stages(ins, outs, sems):
        send_sems, recv_sems, local_sems = sems

        def parts():
            x, y, c, chips = _position()
            me, sibling = (x, y, c), (x, y, 1 - c)

            def slot(a, px, py, pc):
                return outs[a].at[4 * px + 2 * py + pc]

            def copy(a, k, block, to, src=None):
                return pltpu.make_async_remote_copy(
                    src_ref=slot(a, *block) if src is None else src, dst_ref=slot(a, *block),
                    send_sem=send_sems.at[a, k], recv_sem=recv_sems.at[a, k], device_id=to, device_id_type=MESH)

            local = [pltpu.make_async_copy(ins[a], slot(a, *me), local_sems.at[a]) for a in range(n)]
            first = []
            for a in range(n):
                first.append(copy(a, 0, me, sibling, src=ins[a]))
                first += [copy(a, 1 + j, me, (*chip, c), src=ins[a]) for j, chip in enumerate(chips)]
            passed = [copy(a, 4 + j, (*chip, c), sibling) for j, chip in enumerate(chips) for a in range(n)]
            return c, chips, me, sibling, copy, local, first, passed

        def begin():
            _, _, _, _, _, local, first, _ = parts()
            for cp in local + first:
                cp.start()

        def middle():
            c, chips, me, _, copy, _, _, passed = parts()
            k = 0
            for j, chip in enumerate(chips):
                for a in range(n):
                    copy(a, 1 + j, (*chip, c), me).wait_recv()
                    passed[k].start()
                    k += 1

        def end():
            c, chips, me, sibling, copy, local, first, passed = parts()
            for a in range(n):
                copy(a, 0, sibling, me).wait_recv()
                for j, chip in enumerate(chips):
                    copy(a, 4 + j, (*chip, 1 - c), me).wait_recv()
            for cp in first + passed:
                cp.wait_send()
            for cp in local:
                cp.wait()

        return begin, middle, end

    return Exchange(
        shards, [jax.ShapeDtypeStruct((N_DEV,) + sh.shape, sh.dtype) for sh in shards],
        [pltpu.SemaphoreType.DMA((n, 7)), pltpu.SemaphoreType.DMA((n, 7)), pltpu.SemaphoreType.DMA((n,))], stages)


def to_core_exchange(grads):
    n = len(grads)

    def stages(ins, outs, sems):
        send_sems, recv_sems = sems

        def copies():
            x, y, c, _ = _position()
            return [pltpu.make_async_remote_copy(
                src_ref=ins[a].at[2 * q + (1 - c)], dst_ref=outs[a].at[q], send_sem=send_sems.at[a, q],
                recv_sem=recv_sems.at[a, q], device_id=(x, y, 1 - c), device_id_type=MESH)
                for a in range(n) for q in range(4)]

        def begin():
            for cp in copies():
                cp.start()

        def end():
            for cp in copies():
                cp.wait()

        return begin, None, end

    return Exchange(grads, [jax.ShapeDtypeStruct((4,) + g.shape[1:], g.dtype) for g in grads],
                    [pltpu.SemaphoreType.DMA((n, 4)), pltpu.SemaphoreType.DMA((n, 4))], stages)


def pair_sum(grad, from_sibling, name):
    _, r, cdim = grad.shape
    tr = min(r, 256)
    c_idx = lax.axis_index("c").astype(jnp.int32).reshape(1)

    def body(c_ref, g_ref, s_ref, o_ref, ob_ref):
        total = g_ref[...] + s_ref[...]
        o_ref[...] = total
        ob_ref[...] = total.astype(BF16)

    blk = lambda: pl.BlockSpec((1, tr, cdim), lambda q, i, cr: (q, i, 0))
    return pl.pallas_call(
        body,
        name=name,
        grid_spec=pltpu.PrefetchScalarGridSpec(
            num_scalar_prefetch=1,
            grid=(4, r // tr),
            in_specs=[pl.BlockSpec((1, tr, cdim), lambda q, i, cr: (2 * q + cr[0], i, 0)), blk()],
            out_specs=[blk(), blk()],
        ),
        out_shape=[jax.ShapeDtypeStruct((4, r, cdim), F32), jax.ShapeDtypeStruct((4, r, cdim), BF16)],
        compiler_params=_params(dimension_semantics=("arbitrary", "arbitrary")),
    )(c_idx, grad, from_sibling)


def to_chip_exchange(pairs):
    n = len(pairs)

    def stages(ins, outs, sems):
        send_sems, recv_sems = sems

        def copies():
            x, y, c, chips = _position()
            return [pltpu.make_async_remote_copy(
                src_ref=ins[a].at[2 * px + py], dst_ref=outs[a].at[j], send_sem=send_sems.at[a, j],
                recv_sem=recv_sems.at[a, j], device_id=(px, py, c), device_id_type=MESH)
                for a in range(n) for j, (px, py) in enumerate(chips)]

        def begin():
            for cp in copies():
                cp.start()

        def end():
            for cp in copies():
                cp.wait()

        return begin, None, end

    return Exchange(pairs, [jax.ShapeDtypeStruct((3,) + p.shape[1:], p.dtype) for p in pairs],
                    [pltpu.SemaphoreType.DMA((n, 3)), pltpu.SemaphoreType.DMA((n, 3))], stages)


def run_exchange(ex, name):
    n_in, n_out = len(ex.arrays), len(ex.out_shape)

    def body(*refs):
        begin, middle, end = ex.stages(refs[:n_in], refs[n_in:n_in + n_out], refs[n_in + n_out:])
        begin()
        if middle is not None:
            middle()
        end()

    return pl.pallas_call(
        body,
        name=name,
        in_specs=[_any_spec()] * n_in,
        out_specs=[_any_spec()] * n_out,
        out_shape=ex.out_shape,
        scratch_shapes=ex.sems,
    )(*ex.arrays)


def _riding(body, n_in, n_out, n_scratch, ex, first, middle, last):
    if ex is None:
        return body
    r_in, r_out = len(ex.arrays), len(ex.out_shape)

    def wrapped(*refs):
        k_in, refs = refs[:n_in], refs[n_in:]
        e_in, refs = refs[:r_in], refs[r_in:]
        k_out, refs = refs[:n_out], refs[n_out:]
        e_out, refs = refs[:r_out], refs[r_out:]
        k_scr, e_sems = refs[:n_scratch], refs[n_scratch:]
        begin, mid, end = ex.stages(e_in, e_out, e_sems)
        pl.when(first())(begin)
        body(*k_in, *k_out, *k_scr)
        if mid is not None:
            pl.when(middle())(mid)
        pl.when(last())(end)

    return wrapped


def _ride_specs(ex):
    if ex is None:
        return [], [], [], [], []
    return [_any_spec()] * len(ex.arrays), [_any_spec()] * len(ex.out_shape), ex.out_shape, ex.sems, ex.arrays


def _adamw(w, g, m, v):
    m = ADAM_B1 * m + (1.0 - ADAM_B1) * g
    v = ADAM_B2 * v + (1.0 - ADAM_B2) * (g * g)
    m_hat = m / (1.0 - ADAM_B1 ** ADAM_STEP)
    v_hat = v / (1.0 - ADAM_B2 ** ADAM_STEP)
    delta = -ADAM_LR * (m_hat / (jnp.sqrt(v_hat) + ADAM_EPS) + ADAM_WD * w)
    return delta, m, v


def sum_adamw(pairs, others, w, m, v, name):
    r, cdim = w.shape
    tr = min(r, 256)
    chip_idx = (2 * lax.axis_index("x") + lax.axis_index("y")).astype(jnp.int32).reshape(1)

    def body(q_ref, p_ref, o_ref, w_ref, m_ref, v_ref, g_out, d_out, m_out, v_out):
        g = p_ref[0] + o_ref[0].astype(F32) + o_ref[1].astype(F32) + o_ref[2].astype(F32)
        g_out[...] = g
        d_out[...], m_out[...], v_out[...] = _adamw(w_ref[...], g, m_ref[...], v_ref[...])

    tile = lambda: pl.BlockSpec((tr, cdim), lambda i, qr: (i, 0))
    return pl.pallas_call(
        body,
        name=name,
        grid_spec=pltpu.PrefetchScalarGridSpec(
            num_scalar_prefetch=1,
            grid=(r // tr,),
            in_specs=[pl.BlockSpec((1, tr, cdim), lambda i, qr: (qr[0], i, 0)),
                      pl.BlockSpec((3, tr, cdim), lambda i, qr: (0, i, 0)), tile(), tile(), tile()],
            out_specs=[tile(), tile(), tile(), tile()],
        ),
        out_shape=[jax.ShapeDtypeStruct((r, cdim), F32)] * 4,
        compiler_params=_params(dimension_semantics=("arbitrary",)),
    )(chip_idx, pairs, others, w, m, v)


SMALL_ROWS = 8


def small_all_reduce(packed):
    shape = packed.shape

    def body(in_ref, out_ref, recv_ref, send_sems, recv_sems):
        x, y, c, _ = _position()
        my_id = 4 * x + 2 * y + c
        recv_ref[my_id] = in_ref[...]
        copies = []
        for rel in range(1, N_DEV):
            fx, fy, fc = (rel >> 2) & 1, (rel >> 1) & 1, rel & 1
            px = 1 - x if fx else x
            py = 1 - y if fy else y
            pc = 1 - c if fc else c
            cp = pltpu.make_async_remote_copy(
                src_ref=in_ref, dst_ref=recv_ref.at[my_id], send_sem=send_sems.at[rel - 1],
                recv_sem=recv_sems.at[rel - 1], device_id=(px, py, pc), device_id_type=MESH)
            cp.start()
            copies.append((cp, pltpu.make_async_remote_copy(
                src_ref=in_ref, dst_ref=recv_ref.at[4 * px + 2 * py + pc], send_sem=send_sems.at[rel - 1],
                recv_sem=recv_sems.at[rel - 1], device_id=(px, py, pc), device_id_type=MESH)))
        for cp, landing in copies:
            landing.wait_recv()
        for cp, landing in copies:
            cp.wait_send()
        total = recv_ref[0]
        for k in range(1, N_DEV):
            total = total + recv_ref[k]
        out_ref[...] = total

    return pl.pallas_call(
        body,
        name="small_all_reduce",
        in_specs=[_vmem_spec()],
        out_specs=_vmem_spec(),
        out_shape=jax.ShapeDtypeStruct(shape, F32),
        scratch_shapes=[pltpu.VMEM((N_DEV,) + shape, F32), pltpu.SemaphoreType.DMA((N_DEV - 1,)),
                        pltpu.SemaphoreType.DMA((N_DEV - 1,))],
    )(packed)


def small_adamw(reduced, w, m, v):
    def body(r_ref, w_ref, m_ref, v_ref, g_out, d_out, m_out, v_out, loss_out):
        red = r_ref[...]
        wv = w_ref[...]
        lb = _lower_bound(jnp.concatenate([wv[5:6, :HGRN_WIDTH], wv[5:6, HGRN_WIDTH:]], axis=0))
        t = red[5:6, :HGRN_WIDTH] * lb * (1.0 - lb)
        row = lax.broadcasted_iota(jnp.int32, red.shape, 0)
        g = jnp.where(row == 5, jnp.concatenate([t, -t], axis=1), jnp.where(row >= 6, 0.0, red))
        g_out[...] = g
        d_out[...], m_out[...], v_out[...] = _adamw(wv, g, m_ref[...], v_ref[...])
        loss = jnp.sum(red[6:7, :], axis=-1, keepdims=True) * (0.5 / D_MODEL)
        loss_out[...] = jnp.broadcast_to(loss, loss_out.shape)

    return pl.pallas_call(
        body,
        name="small_adamw",
        in_specs=[_vmem_spec()] * 4,
        out_specs=[_vmem_spec()] * 5,
        out_shape=[jax.ShapeDtypeStruct(reduced.shape, F32)] * 4 + [jax.ShapeDtypeStruct((8, 128), F32)],
    )(reduced, w, m, v)


def _pack_small(g1, gp, g_pre, g_post, an, hn, logits_or_dlb, extra=None):
    row5 = logits_or_dlb.reshape(1, -1)
    row5 = jnp.pad(row5, ((0, 0), (0, D_MODEL - row5.shape[1])))
    row6 = jnp.zeros((1, D_MODEL), F32) if extra is None else extra
    return jnp.concatenate([g1, gp, g_pre, g_post, jnp.concatenate([an, hn], axis=1), row5, row6,
                            jnp.zeros((1, D_MODEL), F32)], axis=0)


def _unpack_small(p):
    return dict(mix_pre_norm=p[0:1], mix_post_norm=p[1:2], mlp_pre_norm=p[2:3], mlp_post_norm=p[3:4],
                attn_out_norm=p[4:5, :ATTN_WIDTH], hgrn_out_norm=p[4:5, ATTN_WIDTH:],
                hgrn_lb_logits=p[5].reshape(2, HGRN_WIDTH))


BIG = ("w_in", "w_out", "w_ff1", "w_ff2")
ORDER = ("mix_pre_norm", "w_in", "attn_out_norm", "hgrn_lb_logits", "hgrn_out_norm", "w_out", "mix_post_norm",
         "mlp_pre_norm", "w_ff1", "w_ff2", "mlp_post_norm")


def kernel(x, mix_pre_norm, w_in, attn_out_norm, hgrn_lb_logits, hgrn_out_norm, w_out, mix_post_norm, mlp_pre_norm, w_ff1, w_ff2, mlp_post_norm, loss_target, m_mix_pre_norm, m_w_in, m_attn_out_norm, m_hgrn_lb_logits, m_hgrn_out_norm, m_w_out, m_mix_post_norm, m_mlp_pre_norm, m_w_ff1, m_w_ff2, m_mlp_post_norm, v_mix_pre_norm, v_w_in, v_attn_out_norm, v_hgrn_lb_logits, v_hgrn_out_norm, v_w_out, v_mix_post_norm, v_mlp_pre_norm, v_w_ff1, v_w_ff2, v_mlp_post_norm):
    w = dict(w_in=w_in[0], w_out=w_out[0], w_ff1=w_ff1[0], w_ff2=w_ff2[0])
    m = dict(w_in=m_w_in[0], w_out=m_w_out[0], w_ff1=m_w_ff1[0], w_ff2=m_w_ff2[0])
    v = dict(w_in=v_w_in[0], w_out=v_w_out[0], w_ff1=v_w_ff1[0], w_ff2=v_w_ff2[0])

    dx, big, small = train_step(x[0], loss_target[0], mix_pre_norm, attn_out_norm, hgrn_lb_logits, hgrn_out_norm,
                                mix_post_norm, mlp_pre_norm, mlp_post_norm, w, m, v)

    packed_g = _pack_small(small["dg1"], small["dgp"], small["dg_pre"], small["dg_post"], small["dan"], small["dhn"],
                           small["dlb"], small["loss_vec"])
    reduced = small_all_reduce(packed_g)
    pack = lambda a, b, c2, d, e, f, g: _pack_small(a, b, c2, d, e, f, g)
    w_s = pack(mix_pre_norm, mix_post_norm, mlp_pre_norm, mlp_post_norm, attn_out_norm, hgrn_out_norm, hgrn_lb_logits)
    m_s = pack(m_mix_pre_norm, m_mix_post_norm, m_mlp_pre_norm, m_mlp_post_norm, m_attn_out_norm, m_hgrn_out_norm,
               m_hgrn_lb_logits)
    v_s = pack(v_mix_pre_norm, v_mix_post_norm, v_mlp_pre_norm, v_mlp_post_norm, v_attn_out_norm, v_hgrn_out_norm,
               v_hgrn_lb_logits)
    g_s, d_s, nm_s, nv_s, loss = small_adamw(reduced, w_s, m_s, v_s)
    small_out = [_unpack_small(t) for t in (g_s, d_s, nm_s, nv_s)]

    outs = [loss[0, 0], dx[None]]
    for kind in range(4):
        for name in ORDER:
            outs.append(big[name][kind][None] if name in BIG else small_out[kind][name])
    return tuple(outs)
```

```python
import functools
import math

import jax
import jax.numpy as jnp
from jax import lax
from jax.experimental import pallas as pl
from jax.experimental.pallas import tpu as pltpu

F32 = jnp.float32
BF16 = jnp.bfloat16

D_MODEL = 1024
SEQ = 4096
ATTN_WIDTH = 512
ATTN_HEAD_DIM = 64
ATTN_HEADS = 8
ATTN_BLOCK = 128
DILATIONS = (1, 4, 16)
HGRN_WIDTH = 512
HGRN_HEADS = 4
HGRN_HEAD_DIM = 128
HGRN_CHUNK = 64
IN_PROJ_WIDTH = 3584
D_FF = 4096
RMS_EPS = 1e-6
N_DEV = 8
ADAM_LR = 0.001
ADAM_B1 = 0.9
ADAM_B2 = 0.999
ADAM_EPS = 1e-08
ADAM_WD = 0.01
ADAM_STEP = 10

SUBLANES = 8
LANES = 128
COLUMN_UNROLL = 8
SUB_BLOCK = 16
TOKEN_TILE = 256
ELEMENTWISE_ROWS = 1024
MLP_TILE = 256
PROJ_TILE = 512
VMEM_LIMIT = 56 * 1024 * 1024
NEG_BIG = -1e30
MESH = pl.DeviceIdType.MESH


def _params(**kw):
    return pltpu.CompilerParams(vmem_limit_bytes=VMEM_LIMIT, **kw)


def _vmem_spec():
    return pl.BlockSpec(memory_space=pltpu.VMEM)


def _dot(a, b):
    return jnp.dot(a, b, preferred_element_type=F32)


def _dot_nt(a, b):
    return lax.dot_general(a, b, (((1,), (1,)), ((), ())), preferred_element_type=F32)


def _dot_tn(a, b):
    return lax.dot_general(a, b, (((0,), (0,)), ((), ())), preferred_element_type=F32)


def _sigmoid(x):
    return 1.0 / (1.0 + jnp.exp(-x))


def _rms_fwd(x, gain, width):
    r = lax.rsqrt(jnp.sum(x * x, axis=-1, keepdims=True) * (1.0 / width) + RMS_EPS)
    return x * r * gain


def _rms_bwd(dy, x, gain, width):
    r = lax.rsqrt(jnp.sum(x * x, axis=-1, keepdims=True) * (1.0 / width) + RMS_EPS)
    xhat = x * r
    dxhat = dy * gain
    dx = r * (dxhat - xhat * (jnp.sum(dxhat * xhat, axis=-1, keepdims=True) * (1.0 / width)))
    return dx, dy * xhat


def _split3(x):
    hi = x.astype(BF16)
    r1 = x - hi.astype(F32)
    mid = r1.astype(BF16)
    lo = (r1 - mid.astype(F32)).astype(BF16)
    return hi, mid, lo


def _tri_sum(tri_bf16, x):
    hi, mid, lo = _split3(x)
    return _dot(tri_bf16, hi) + _dot(tri_bf16, mid) + _dot(tri_bf16, lo)


def _dilated_spec(d, tm, width):
    return pl.BlockSpec((d, tm // d, width), lambda i: (0, i, 0))


def _lane_blocks(ref, value):
    for c in range(ref.shape[0]):
        ref[c] = value[:, c * LANES:(c + 1) * LANES]


def _to_dilated(src_ref, dst_ref, d, tm, cast=None):
    for r in range(d):
        for c in range(src_ref.shape[0]):
            v = src_ref[c] if d == 1 else src_ref[c, pl.ds(r, tm // d, stride=d), :]
            dst_ref[r, :, c * LANES:(c + 1) * LANES] = v if cast is None else v.astype(cast)


def _from_dilated(src_ref, scratch_ref, d, tm):
    if d == 1:
        return src_ref[0].astype(F32)
    nblk = scratch_ref.shape[0]
    for r in range(d):
        for c in range(nblk):
            scratch_ref[c, pl.ds(r, tm // d, stride=d), :] = src_ref[r, :, c * LANES:(c + 1) * LANES].astype(F32)
    return jnp.concatenate([scratch_ref[c] for c in range(nblk)], axis=1)


def in_proj_fwd(x, g1, w_in_b):
    s = x.shape[0]
    tm = PROJ_TILE
    qkv_w = 3 * ATTN_WIDTH
    hg_w = IN_PROJ_WIDTH - qkv_w

    def body(x_ref, g_ref, w_ref, hg_ref, h_ref, *rest):
        qkv_refs, qkv_scr = rest[:len(DILATIONS)], rest[len(DILATIONS)]
        h = _rms_fwd(x_ref[...], g_ref[...], D_MODEL).astype(BF16)
        h_ref[...] = h
        proj = _dot(h, w_ref[...])
        hg_ref[...] = proj[:, qkv_w:]
        _lane_blocks(qkv_scr, proj[:, :qkv_w])
        for d, ref in zip(DILATIONS, qkv_refs):
            _to_dilated(qkv_scr, ref, d, tm, cast=BF16)

    return pl.pallas_call(
        body,
        name="in_proj_fwd",
        grid=(s // tm,),
        in_specs=[
            pl.BlockSpec((tm, D_MODEL), lambda i: (i, 0)),
            pl.BlockSpec((1, D_MODEL), lambda i: (0, 0)),
            _vmem_spec(),
        ],
        out_specs=[
            pl.BlockSpec((tm, hg_w), lambda i: (i, 0)),
            pl.BlockSpec((tm, D_MODEL), lambda i: (i, 0)),
        ] + [_dilated_spec(d, tm, qkv_w) for d in DILATIONS],
        out_shape=[jax.ShapeDtypeStruct((s, hg_w), F32), jax.ShapeDtypeStruct((s, D_MODEL), BF16)] + [
            jax.ShapeDtypeStruct((d, s // d, qkv_w), BF16) for d in DILATIONS],
        scratch_shapes=[pltpu.VMEM((qkv_w // LANES, tm, LANES), F32)],
        compiler_params=_params(dimension_semantics=("arbitrary",)),
    )(x, g1, w_in_b)


ATTN_SCALE = ATTN_HEAD_DIM ** -0.5


def _fill_attn_bias(bias_ref, dilation):
    qi = lax.broadcasted_iota(jnp.int32, (ATTN_BLOCK, 2 * ATTN_BLOCK), 0)
    kj = lax.broadcasted_iota(jnp.int32, (ATTN_BLOCK, 2 * ATTN_BLOCK), 1)
    dist = qi + ATTN_BLOCK - kj
    valid = (dist >= 0) & (dist <= ATTN_BLOCK)
    for head in range(ATTN_HEADS):
        slope = 2.0 ** (-8.0 * (head + 1) / ATTN_HEADS)
        bias = jnp.where(valid, dist.astype(F32) * (-slope * dilation), NEG_BIG)
        bias_ref[0, head] = bias
        bias_ref[1, head] = jnp.where(kj >= ATTN_BLOCK, bias, NEG_BIG)


def _stack_heads(x):
    low = _lane_half(x.shape, 0)
    zero = jnp.zeros_like(x)
    return jnp.concatenate([jnp.where(low, x, zero), jnp.where(low, zero, x)], axis=0)


def _unstack_heads(y):
    half = y.shape[0] // 2
    return jnp.where(_lane_half((half, y.shape[1]), 0), y[:half], y[half:])


def _attn_scores(q_stack, kcat, bias_ref, pair, first_block):
    f = first_block.astype(jnp.int32)
    bias = jnp.concatenate([bias_ref[f, 2 * pair], bias_ref[f, 2 * pair + 1]], axis=0)
    return _dot_nt(q_stack, kcat) + bias


def _lane_half(shape, sub):
    lane = lax.broadcasted_iota(jnp.int32, shape, 1)
    return (lane < ATTN_HEAD_DIM) if sub == 0 else (lane >= ATTN_HEAD_DIM)


def _sub_block(col, row):
    return pl.BlockSpec((None, ATTN_BLOCK, ATTN_WIDTH), lambda r, n: (r, row(n), col))


def attn_fwd(qkv, dilation):
    d, length, _ = qkv.shape
    assert d == dilation
    nb = length // ATTN_BLOCK

    def body(q_ref, kc_ref, kp_ref, vc_ref, vp_ref, o_ref, lse_ref, bias_ref):
        @pl.when((pl.program_id(0) == 0) & (pl.program_id(1) == 0))
        def _():
            _fill_attn_bias(bias_ref, d)

        first = pl.program_id(1) == 0
        for pair in range(ATTN_HEADS // 2):
            lanes = slice(pair * 128, (pair + 1) * 128)
            q_stack = _stack_heads(q_ref[:, lanes] * ATTN_SCALE)
            kcat = jnp.concatenate([kp_ref[:, lanes], kc_ref[:, lanes]], axis=0)
            vcat = jnp.concatenate([vp_ref[:, lanes], vc_ref[:, lanes]], axis=0)
            sc = _attn_scores(q_stack, kcat, bias_ref, pair, first)
            m = jnp.max(sc, axis=-1, keepdims=True)
            p = jnp.exp(sc - m)
            den = jnp.sum(p, axis=-1, keepdims=True)
            o_ref[:, lanes] = _unstack_heads(_dot(p.astype(BF16), vcat) / den).astype(BF16)
            lse_ref[:, lanes] = _unstack_heads(jnp.broadcast_to(m + jnp.log(den), (2 * ATTN_BLOCK, 128)))

    cur = lambda n: n
    prev = lambda n: jnp.maximum(n - 1, 0)
    return pl.pallas_call(
        body,
        name=f"attn_fwd_d{d}",
        grid=(d, nb),
        in_specs=[_sub_block(0, cur), _sub_block(1, cur), _sub_block(1, prev), _sub_block(2, cur), _sub_block(2, prev)],
        out_specs=[_sub_block(0, cur), _sub_block(0, cur)],
        out_shape=[jax.ShapeDtypeStruct((d, length, ATTN_WIDTH), BF16), jax.ShapeDtypeStruct((d, length, ATTN_WIDTH), F32)],
        scratch_shapes=[pltpu.VMEM((2, ATTN_HEADS, ATTN_BLOCK, 2 * ATTN_BLOCK), F32)],
        compiler_params=_params(dimension_semantics=("arbitrary", "arbitrary")),
    )(qkv, qkv, qkv, qkv, qkv)


def attn_bwd(qkv, d_out, lse, delta, dilation, ride=None):
    d, length, _ = qkv.shape
    assert d == dilation
    nb = length // ATTN_BLOCK

    steps = d * nb + 1

    def body(q_ref, kc_ref, kp_ref, vc_ref, vp_ref, do_ref, lse_ref, dl_ref, dq_ref, dk_ref, dv_ref, ck_ref, cv_ref,
             bias_ref):
        t = pl.program_id(0)

        @pl.when(t == 0)
        def _():
            ck_ref[...] = jnp.zeros_like(ck_ref)
            cv_ref[...] = jnp.zeros_like(cv_ref)
            _fill_attn_bias(bias_ref, d)

        @pl.when(t < steps - 1)
        def _():
            first = t % nb == 0
            for pair in range(ATTN_HEADS // 2):
                lanes = slice(pair * 128, (pair + 1) * 128)
                q_stack = _stack_heads(q_ref[:, lanes] * ATTN_SCALE)
                do_stack = _stack_heads(do_ref[:, lanes])
                kcat = jnp.concatenate([kp_ref[:, lanes], kc_ref[:, lanes]], axis=0)
                vcat = jnp.concatenate([vp_ref[:, lanes], vc_ref[:, lanes]], axis=0)
                col_a, col_b = pair * 128, pair * 128 + ATTN_HEAD_DIM
                lse_col = jnp.concatenate([lse_ref[:, col_a:col_a + 1], lse_ref[:, col_b:col_b + 1]], axis=0)
                dl_col = jnp.concatenate([dl_ref[:, col_a:col_a + 1], dl_ref[:, col_b:col_b + 1]], axis=0)
                p = jnp.exp(_attn_scores(q_stack, kcat, bias_ref, pair, first) - lse_col)
                ds = (p * (_dot_nt(do_stack, vcat) - dl_col)).astype(BF16)
                dq_ref[:, lanes] = (_unstack_heads(_dot(ds, kcat)) * ATTN_SCALE).astype(BF16)
                dk_cat = _dot_tn(ds, q_stack)
                dv_cat = _dot_tn(p.astype(BF16), do_stack)
                dk_ref[:, lanes] = (ck_ref[:, lanes] + dk_cat[:ATTN_BLOCK]).astype(BF16)
                dv_ref[:, lanes] = (cv_ref[:, lanes] + dv_cat[:ATTN_BLOCK]).astype(BF16)
                ck_ref[:, lanes] = dk_cat[ATTN_BLOCK:]
                cv_ref[:, lanes] = dv_cat[ATTN_BLOCK:]

        @pl.when(t == steps - 1)
        def _():
            dk_ref[...] = ck_ref[...].astype(BF16)
            dv_ref[...] = cv_ref[...].astype(BF16)

    blk = (ATTN_BLOCK, ATTN_WIDTH)

    def spec(col, shift):
        def index(t):
            f = jnp.minimum(t, steps - 2) if shift > -2 else jnp.maximum(t - 1, 0)
            r, n = f // nb, f % nb
            return (r, jnp.maximum(n - 1, 0) if shift == -1 else n, col)
        return pl.BlockSpec((None, ATTN_BLOCK, ATTN_WIDTH), index)

    step = lambda k: (lambda: pl.program_id(0) == k)
    e_in, e_out, e_shape, e_scr, e_args = _ride_specs(ride)
    return pl.pallas_call(
        _riding(body, 8, 3, 3, ride, step(0), step(steps // 2), step(steps - 1)),
        name=f"attn_bwd_d{d}",
        grid=(steps,),
        in_specs=[spec(0, 0), spec(1, 0), spec(1, -1), spec(2, 0), spec(2, -1), spec(0, 0), spec(0, 0), spec(0, 0)] + e_in,
        out_specs=[spec(0, 0), spec(0, -2), spec(0, -2)] + e_out,
        out_shape=[jax.ShapeDtypeStruct((d, length, ATTN_WIDTH), BF16)] * 3 + e_shape,
        scratch_shapes=[pltpu.VMEM(blk, F32), pltpu.VMEM(blk, F32),
                        pltpu.VMEM((2, ATTN_HEADS, ATTN_BLOCK, 2 * ATTN_BLOCK), F32)] + e_scr,
        compiler_params=_params(dimension_semantics=("arbitrary",)),
    )(qkv, qkv, qkv, qkv, qkv, d_out, lse, delta, *e_args)


def _lower_bound(logits):
    return _sigmoid(logits[0:1, :] - logits[1:2, :])


def _hgrn_gates(q, fp, lb):
    sq = _sigmoid(q)
    qf = q * sq
    sig = _sigmoid(fp)
    f = lb + (1.0 - lb) * sig
    kf = (1.0 - lb) * _sigmoid(-fp)
    return sq, qf, sig, f, kf


def _tril_bf16(n, upper=False):
    r = lax.broadcasted_iota(jnp.int32, (n, n), 0)
    c = lax.broadcasted_iota(jnp.int32, (n, n), 1)
    keep = (c >= r) if upper else (c <= r)
    return jnp.where(keep, 1.0, 0.0).astype(BF16)


def _hgrn_diagonal_loops(c_len, diagonal):
    for half in range(SUB_BLOCK // SUBLANES):
        def step(jj, carry, half=half):
            j = half * SUBLANES + jj
            for i in range(c_len // SUB_BLOCK):
                diagonal(slice(i * SUB_BLOCK + half * SUBLANES, (i + 1) * SUB_BLOCK), j, i * SUB_BLOCK + j)
            return carry

        lax.fori_loop(0, SUBLANES, step, 0, unroll=COLUMN_UNROLL)


def _hgrn_off_diagonal(b, qf, kf):
    c_len, width = b.shape
    edges = [b[0:1, :]] + [b[i * SUB_BLOCK - 1:i * SUB_BLOCK, :] for i in range(1, c_len // SUB_BLOCK)]
    eq = jnp.exp(b - jnp.concatenate([jnp.broadcast_to(e, (SUB_BLOCK, width)) for e in edges], axis=0))
    q_til = qf * eq
    k_til, ek = [], []
    for i in range(1, c_len // SUB_BLOCK):
        n = i * SUB_BLOCK
        e = jnp.exp(edges[i] - b[:n, :])
        ek.append(e)
        k_til.append(jnp.concatenate([kf[:n, :] * e, jnp.zeros((2 * c_len - n, width), F32)], axis=0))
    return q_til, k_til, eq, ek


def _split2(x):
    hi = x.astype(BF16)
    return hi, (x - hi.astype(F32)).astype(BF16)


def hgrn_fwd(proj, lb, ride=None):
    s = proj.shape[0]
    c_len, nh, hd = HGRN_CHUNK, HGRN_HEADS, HGRN_HEAD_DIM
    n_chunks = s // c_len
    col0 = 0

    def body(q_ref, f_ref, i_ref, lb_ref, o_ref, st_out_ref, a_out_ref, st_ref, b_ref, qf_ref, kf_ref, a_ref):
        @pl.when(pl.program_id(0) == 0)
        def _():
            st_ref[...] = jnp.zeros_like(st_ref)

        lbv = _lower_bound(lb_ref[...])
        _, qf, _, f, kf = _hgrn_gates(q_ref[...], f_ref[...], lbv)
        b = _tri_sum(_tril_bf16(c_len), jnp.log(f))
        b_ref[...] = b
        qf_ref[...] = qf
        kf_ref[...] = kf
        a_ref[...] = jnp.zeros_like(a_ref)

        def diagonal(rows, j, key):
            bj = b_ref[pl.ds(key, 1), :]
            kj = kf_ref[pl.ds(key, 1), :]
            nrow = rows.stop - rows.start
            t_loc = lax.broadcasted_iota(jnp.int32, (nrow, nh * hd), 0) + (rows.start % SUB_BLOCK)
            e = jnp.exp(jnp.where(t_loc >= j, b_ref[rows, :] - bj, NEG_BIG))
            prod = qf_ref[rows, :] * kj * e
            lane = lax.broadcasted_iota(jnp.int32, (nrow, hd), 1)
            for h in range(nh):
                col = jnp.sum(prod[:, h * hd:(h + 1) * hd], axis=-1, keepdims=True)
                a_ref[h, rows, :] = jnp.where(lane == key, col, a_ref[h, rows, :])

        _hgrn_diagonal_loops(c_len, diagonal)
        q_til, k_til, _, _ = _hgrn_off_diagonal(b, qf, kf)
        q_til = q_til.astype(BF16)
        k_til = [k.astype(BF16) for k in k_til]

        b_last = b[c_len - 1:c_len, :]
        qb = (qf * jnp.exp(b)).astype(BF16)
        kb2 = (kf * jnp.exp(b_last - b)).astype(BF16)
        vf = i_ref[...].astype(BF16)
        for h in range(nh):
            hs = slice(h * hd, (h + 1) * hd)
            st = st_ref[h]
            st_out_ref[0, h] = st
            off = [jnp.zeros((SUB_BLOCK, hd), F32)]
            for i in range(1, c_len // SUB_BLOCK):
                off.append(_dot_nt(q_til[i * SUB_BLOCK:(i + 1) * SUB_BLOCK, hs], k_til[i - 1][:, hs]))
            a_h = a_ref[h] + jnp.concatenate(off, axis=0)
            a_out_ref[:, hs] = a_h
            o_ref[:, hs] = _dot_nt(qb[:, hs], st.astype(BF16)) + _dot(a_h[:, :c_len].astype(BF16), vf[:, hs])
            st_ref[h] = st * jnp.exp(b_last[:, hs]) + _dot_tn(vf[:, hs], kb2[:, hs])

    blk = (c_len, HGRN_WIDTH)
    step = lambda k: (lambda: pl.program_id(0) == k)
    e_in, e_out, e_shape, e_scr, e_args = _ride_specs(ride)
    return pl.pallas_call(
        _riding(body, 4, 3, 5, ride, step(0), step((7 * n_chunks) // 8), step(n_chunks - 1)),
        name="hgrn_fwd",
        grid=(n_chunks,),
        in_specs=[
            pl.BlockSpec(blk, lambda c: (c, col0)),
            pl.BlockSpec(blk, lambda c: (c, col0 + 1)),
            pl.BlockSpec(blk, lambda c: (c, col0 + 2)),
            pl.BlockSpec((2, HGRN_WIDTH), lambda c: (0, 0)),
        ] + e_in,
        out_specs=[
            pl.BlockSpec(blk, lambda c: (c, 0)),
            pl.BlockSpec((1, nh, hd, hd), lambda c: (c, 0, 0, 0)),
            pl.BlockSpec(blk, lambda c: (c, 0)),
        ] + e_out,
        out_shape=[
            jax.ShapeDtypeStruct((s, HGRN_WIDTH), F32),
            jax.ShapeDtypeStruct((n_chunks, nh, hd, hd), F32),
            jax.ShapeDtypeStruct((s, nh * hd), F32),
        ] + e_shape,
        scratch_shapes=[
            pltpu.VMEM((nh, hd, hd), F32),
            pltpu.VMEM(blk, F32),
            pltpu.VMEM(blk, F32),
            pltpu.VMEM(blk, F32),
            pltpu.VMEM((nh, c_len, hd), F32),
        ] + e_scr,
        compiler_params=_params(dimension_semantics=("arbitrary",)),
    )(proj, proj, proj, lb, *e_args)


def hgrn_bwd(proj, lb, d_o, states, a_mat, ride=None):
    s = proj.shape[0]
    c_len, nh, hd = HGRN_CHUNK, HGRN_HEADS, HGRN_HEAD_DIM
    n_chunks = s // c_len
    col0 = 0
    last = n_chunks - 1

    def body(q_ref, f_ref, i_ref, lb_ref, do_ref, st_in_ref, a_in_ref, dq_ref, df_ref, di_ref, dlb_ref,
             dst_ref, b_ref, qf_ref, kf_ref, da_ref, dqi_ref, dki_ref):
        @pl.when(pl.program_id(0) == 0)
        def _():
            dst_ref[...] = jnp.zeros_like(dst_ref)
            dlb_ref[...] = jnp.zeros_like(dlb_ref)

        lbv = _lower_bound(lb_ref[...])
        q = q_ref[...]
        sq, qf, sig, f, kf = _hgrn_gates(q, f_ref[...], lbv)
        b = _tri_sum(_tril_bf16(c_len), jnp.log(f))
        b_ref[...] = b
        qf_ref[...] = qf
        kf_ref[...] = kf
        b_last = b[c_len - 1:c_len, :]
        eb = jnp.exp(b)
        ebl = jnp.exp(b_last - b)
        qb = qf * eb
        kb2 = kf * ebl
        vf = i_ref[...]
        d_o = do_ref[...]
        qb_b, kb2_b, vf_b, do_b = qb.astype(BF16), kb2.astype(BF16), vf.astype(BF16), d_o.astype(BF16)
        tq = lax.broadcasted_iota(jnp.int32, (c_len, hd), 0)
        lane = lax.broadcasted_iota(jnp.int32, (c_len, hd), 1)

        dqb_parts, dvf_parts, dkb2_parts, dbl_parts = [], [], [], []
        for h in range(nh):
            hs = slice(h * hd, (h + 1) * hd)
            st = st_in_ref[0, h]
            dst = dst_ref[h]
            st_b, dst_b = st.astype(BF16), dst.astype(BF16)
            a_h = a_in_ref[:, hs][:, :c_len].astype(BF16)
            dqb_parts.append(_dot(do_b[:, hs], st_b))
            dvf_parts.append(_dot_tn(a_h, do_b[:, hs]) + _dot_nt(kb2_b[:, hs], dst_b))
            dkb2_parts.append(_dot(vf_b[:, hs], dst_b))
            da = _dot_nt(do_b[:, hs], vf_b[:, hs])
            da = jnp.concatenate([da, jnp.zeros((c_len, hd - c_len), F32)], axis=1)
            da_ref[h] = jnp.where(tq >= lane, da, 0.0)
            dbl_parts.append(jnp.sum(dst * st, axis=0, keepdims=True) * jnp.exp(b_last[:, hs]))
            dst_ref[h] = dst * jnp.exp(b_last[:, hs]) + _dot_tn(do_b[:, hs], qb_b[:, hs])
        dqb = jnp.concatenate(dqb_parts, axis=1)
        dvf = jnp.concatenate(dvf_parts, axis=1)
        dkb2 = jnp.concatenate(dkb2_parts, axis=1)
        dbl = jnp.concatenate(dbl_parts, axis=1) + jnp.sum(dkb2 * kb2, axis=0, keepdims=True)

        dqi_ref[...] = jnp.zeros_like(dqi_ref)
        t_idx = lax.broadcasted_iota(jnp.int32, (c_len, nh * hd), 0)

        def diagonal(rows, j, key):
            bj = b_ref[pl.ds(key, 1), :]
            kj = kf_ref[pl.ds(key, 1), :]
            nrow = rows.stop - rows.start
            t_loc = lax.broadcasted_iota(jnp.int32, (nrow, nh * hd), 0) + (rows.start % SUB_BLOCK)
            e = jnp.exp(jnp.where(t_loc >= j, b_ref[rows, :] - bj, NEG_BIG))
            lane_r = lax.broadcasted_iota(jnp.int32, (nrow, hd), 1)
            cols = [jnp.sum(jnp.where(lane_r == key, da_ref[h, rows, :], 0.0), axis=-1, keepdims=True)
                    for h in range(nh)]
            w = e * jnp.concatenate([jnp.broadcast_to(cc, (nrow, hd)) for cc in cols], axis=1)
            dqi_ref[rows, :] += w * kj
            dki_ref[pl.ds(key, 1), :] = jnp.sum(w * qf_ref[rows, :], axis=0, keepdims=True)

        _hgrn_diagonal_loops(c_len, diagonal)

        q_til, k_til, eq, ek = _hgrn_off_diagonal(b, qf, kf)
        q_hi, q_lo = _split2(q_til)
        k_pairs = [_split2(k) for k in k_til]
        n_sub = c_len // SUB_BLOCK
        dq_heads, dk_heads = [], []
        for h in range(nh):
            hs = slice(h * hd, (h + 1) * hd)
            dq_rows = [jnp.zeros((SUB_BLOCK, hd), F32)]
            dk_h = jnp.zeros((c_len, hd), F32)
            for i in range(1, n_sub):
                rows = slice(i * SUB_BLOCK, (i + 1) * SUB_BLOCK)
                n = i * SUB_BLOCK
                da_i = da_ref[h, rows, :].astype(BF16)
                k_hi, k_lo = k_pairs[i - 1]
                dq_rows.append((_dot(da_i, k_hi[:, hs]) + _dot(da_i, k_lo[:, hs])) * eq[rows, hs])
                dk_t = (_dot_tn(da_i, q_hi[rows, hs]) + _dot_tn(da_i, q_lo[rows, hs]))[:n, :] * ek[i - 1][:, hs]
                dk_h = dk_h + jnp.concatenate([dk_t, jnp.zeros((c_len - n, hd), F32)], axis=0)
            dq_heads.append(jnp.concatenate(dq_rows, axis=0))
            dk_heads.append(dk_h)
        dq_intra = dqi_ref[...] + jnp.concatenate(dq_heads, axis=1)
        dk_intra = dki_ref[...] + jnp.concatenate(dk_heads, axis=1)

        db = dqb * qb + qf * dq_intra - kf * dk_intra - dkb2 * kb2
        db = db + jnp.where(t_idx == c_len - 1, dbl, 0.0)
        dg = _tri_sum(_tril_bf16(c_len, upper=True), db)
        dqf = dqb * eb + dq_intra
        dkf = dkb2 * ebl + dk_intra
        dq_ref[...] = (dqf * (sq * (1.0 + q * (1.0 - sq)))).astype(BF16)
        dfv = dg / f - dkf
        df_ref[...] = (dfv * (1.0 - lbv) * sig * (1.0 - sig)).astype(BF16)
        di_ref[...] = dvf.astype(BF16)
        dlb_ref[...] += jnp.sum(dfv * (1.0 - sig), axis=0, keepdims=True)

    blk = (c_len, HGRN_WIDTH)
    rev = lambda c: last - c
    step = lambda k: (lambda: pl.program_id(0) == k)
    e_in, e_out, e_shape, e_scr, e_args = _ride_specs(ride)
    return pl.pallas_call(
        _riding(body, 7, 4, 7, ride, step(0), step(n_chunks // 2), step(last)),
        name="hgrn_bwd",
        grid=(n_chunks,),
        in_specs=[
            pl.BlockSpec(blk, lambda c: (rev(c), col0)),
            pl.BlockSpec(blk, lambda c: (rev(c), col0 + 1)),
            pl.BlockSpec(blk, lambda c: (rev(c), col0 + 2)),
            pl.BlockSpec((2, HGRN_WIDTH), lambda c: (0, 0)),
            pl.BlockSpec(blk, lambda c: (rev(c), 0)),
            pl.BlockSpec((1, nh, hd, hd), lambda c: (rev(c), 0, 0, 0)),
            pl.BlockSpec(blk, lambda c: (rev(c), 0)),
        ] + e_in,
        out_specs=[
            pl.BlockSpec(blk, lambda c: (rev(c), 0)),
            pl.BlockSpec(blk, lambda c: (rev(c), 0)),
            pl.BlockSpec(blk, lambda c: (rev(c), 0)),
            pl.BlockSpec((1, HGRN_WIDTH), lambda c: (0, 0)),
        ] + e_out,
        out_shape=[jax.ShapeDtypeStruct((s, HGRN_WIDTH), BF16)] * 3 + [jax.ShapeDtypeStruct((1, HGRN_WIDTH), F32)] + e_shape,
        scratch_shapes=[
            pltpu.VMEM((nh, hd, hd), F32),
            pltpu.VMEM(blk, F32),
            pltpu.VMEM(blk, F32),
            pltpu.VMEM(blk, F32),
            pltpu.VMEM((nh, c_len, hd), F32),
            pltpu.VMEM(blk, F32),
            pltpu.VMEM(blk, F32),
        ] + e_scr,
        compiler_params=_params(dimension_semantics=("arbitrary",)),
    )(proj, proj, proj, lb, d_o, states, a_mat, *e_args)


def _row_spec(tm, width, col=0):
    return pl.BlockSpec((tm, width), lambda i: (i, col))


def _const_spec(width):
    return pl.BlockSpec((1, width), lambda i: (0, 0))


def _acc_rows(ref, value):
    @pl.when(pl.program_id(0) == 0)
    def _():
        ref[...] = jnp.zeros_like(ref)

    ref[...] += jnp.sum(value, axis=0, keepdims=True)


def mix_fwd(attn_parts, o_h, proj, an, hn, w_out_b, gp, x, ride=None):
    s = x.shape[0]
    tm = TOKEN_TILE
    gate_col = 3
    hd = HGRN_HEAD_DIM
    nd = len(DILATIONS)

    def body(*refs):
        o_refs, l_refs = refs[:nd], refs[nd:2 * nd]
        oh_ref, gate_ref, an_ref, hn_ref, w_ref, gp_ref, x_ref = refs[2 * nd:2 * nd + 7]
        x1_ref, cat_ref, mixed_ref, attn_ref = refs[2 * nd + 7:2 * nd + 11]
        lse_refs = refs[2 * nd + 11:3 * nd + 11]
        o_scr, l_scr, lse_scr = refs[3 * nd + 11:]
        os_ = [_from_dilated(r, o_scr.at[k], d, tm) for k, (r, d) in enumerate(zip(o_refs, DILATIONS))]
        ls = [_from_dilated(r, l_scr.at[k], d, tm) for k, (r, d) in enumerate(zip(l_refs, DILATIONS))]
        m = jnp.maximum(jnp.maximum(ls[0], ls[1]), ls[2])
        es = [jnp.exp(l - m) for l in ls]
        den = es[0] + es[1] + es[2]
        attn = (es[0] * os_[0] + es[1] * os_[1] + es[2] * os_[2]) / den
        attn_ref[...] = attn
        _lane_blocks(lse_scr, m + jnp.log(den))
        for d, ref in zip(DILATIONS, lse_refs):
            _to_dilated(lse_scr, ref, d, tm)
        cat_ref[:, :ATTN_WIDTH] = _rms_fwd(attn, an_ref[...], ATTN_WIDTH).astype(BF16)
        gate = gate_ref[...]
        silu_g = gate * _sigmoid(gate)
        for h in range(HGRN_HEADS):
            hs = slice(h * hd, (h + 1) * hd)
            rec = _rms_fwd(oh_ref[:, hs], hn_ref[:, hs], hd) * silu_g[:, hs]
            cat_ref[:, ATTN_WIDTH + h * hd:ATTN_WIDTH + (h + 1) * hd] = rec.astype(BF16)
        mixed = _dot(cat_ref[...], w_ref[...])
        mixed_ref[...] = mixed
        x1_ref[...] = x_ref[...] + _rms_fwd(mixed, gp_ref[...], D_MODEL)

    aw = ATTN_WIDTH
    n_steps = s // tm
    step = lambda k: (lambda: pl.program_id(0) == k)
    e_in, e_out, e_shape, e_scr, e_args = _ride_specs(ride)
    return pl.pallas_call(
        _riding(body, 2 * nd + 7, 4 + nd, 3, ride, step(0), step((13 * n_steps) // 16), step(n_steps - 1)),
        name="mix_fwd",
        grid=(n_steps,),
        in_specs=[_dilated_spec(d, tm, aw) for d in DILATIONS] * 2 + [
            _row_spec(tm, aw), _row_spec(tm, aw, gate_col), _const_spec(aw), _const_spec(aw), _vmem_spec(),
            _const_spec(D_MODEL), _row_spec(tm, D_MODEL)] + e_in,
        out_specs=[_row_spec(tm, D_MODEL), _row_spec(tm, D_MODEL), _row_spec(tm, D_MODEL), _row_spec(tm, aw)] + [
            _dilated_spec(d, tm, aw) for d in DILATIONS] + e_out,
        out_shape=[
            jax.ShapeDtypeStruct((s, D_MODEL), F32),
            jax.ShapeDtypeStruct((s, D_MODEL), BF16),
            jax.ShapeDtypeStruct((s, D_MODEL), F32),
            jax.ShapeDtypeStruct((s, aw), F32),
        ] + [jax.ShapeDtypeStruct((d, s // d, aw), F32) for d in DILATIONS] + e_shape,
        scratch_shapes=[pltpu.VMEM((nd, aw // LANES, tm, LANES), F32), pltpu.VMEM((nd, aw // LANES, tm, LANES), F32),
                        pltpu.VMEM((aw // LANES, tm, LANES), F32)] + e_scr,
        compiler_params=_params(dimension_semantics=("arbitrary",)),
    )(*[p[0] for p in attn_parts], *[p[1] for p in attn_parts], o_h, proj, an, hn, w_out_b, gp, x, *e_args)


def mix_bwd(dx1, mixed, gp, w_out_b, attn, an, o_h, proj, hn):
    s = dx1.shape[0]
    tm = TOKEN_TILE
    gate_col = 3
    hd = HGRN_HEAD_DIM
    aw = ATTN_WIDTH

    nd = len(DILATIONS)

    def body(*refs):
        dx1_ref, mixed_ref, gp_ref, w_ref, attn_ref, an_ref, oh_ref, gate_ref, hn_ref, dmix_ref = refs[:10]
        do_refs, delta_refs = refs[10:10 + nd], refs[10 + nd:10 + 2 * nd]
        doh_ref, dgate_ref, dgp_ref, dan_ref, dhn_ref, do_ref, delta_ref = refs[10 + 2 * nd:]
        dmixed, gp_c = _rms_bwd(dx1_ref[...], mixed_ref[...], gp_ref[...], D_MODEL)
        _acc_rows(dgp_ref, gp_c)
        dmixed_b = dmixed.astype(BF16)
        dmix_ref[...] = dmixed_b
        dcat = _dot_nt(dmixed_b, w_ref[...])
        attn = attn_ref[...]
        d_o, an_c = _rms_bwd(dcat[:, :aw], attn, an_ref[...], aw)
        _acc_rows(dan_ref, an_c)
        _lane_blocks(do_ref, d_o)
        prod = d_o * attn
        for pair in range(ATTN_HEADS // 2):
            pp = prod[:, pair * LANES:(pair + 1) * LANES]
            low = _lane_half((tm, LANES), 0)
            lo = jnp.sum(jnp.where(low, pp, 0.0), axis=-1, keepdims=True)
            hi = jnp.sum(jnp.where(low, 0.0, pp), axis=-1, keepdims=True)
            delta_ref[pair] = jnp.where(low, lo, hi)
        for d, o_ref, l_ref in zip(DILATIONS, do_refs, delta_refs):
            _to_dilated(do_ref, o_ref, d, tm, cast=BF16)
            _to_dilated(delta_ref, l_ref, d, tm)
        gate = gate_ref[...]
        sg = _sigmoid(gate)
        silu_g = gate * sg
        drec = dcat[:, aw:]
        hn_parts = []
        for h in range(HGRN_HEADS):
            hs = slice(h * hd, (h + 1) * hd)
            oh = oh_ref[:, hs]
            on = _rms_fwd(oh, hn_ref[:, hs], hd)
            dgate_ref[:, hs] = (drec[:, hs] * on * (sg[:, hs] * (1.0 + gate[:, hs] * (1.0 - sg[:, hs])))).astype(BF16)
            d_oh, hn_c = _rms_bwd(drec[:, hs] * silu_g[:, hs], oh, hn_ref[:, hs], hd)
            doh_ref[:, hs] = d_oh
            hn_parts.append(hn_c)
        _acc_rows(dhn_ref, jnp.concatenate(hn_parts, axis=1))

    return pl.pallas_call(
        body,
        name="mix_bwd",
        grid=(s // tm,),
        in_specs=[_row_spec(tm, D_MODEL), _row_spec(tm, D_MODEL), _const_spec(D_MODEL), _vmem_spec(), _row_spec(tm, aw),
                  _const_spec(aw), _row_spec(tm, aw), _row_spec(tm, aw, gate_col), _const_spec(aw)],
        out_specs=[_row_spec(tm, D_MODEL)] + [_dilated_spec(d, tm, aw) for d in DILATIONS] * 2 + [_row_spec(tm, aw)] * 2 + [
            _const_spec(D_MODEL), _const_spec(aw), _const_spec(aw)],
        out_shape=[jax.ShapeDtypeStruct((s, D_MODEL), BF16)] + [
            jax.ShapeDtypeStruct((d, s // d, aw), BF16) for d in DILATIONS] + [
            jax.ShapeDtypeStruct((d, s // d, aw), F32) for d in DILATIONS] + [
            jax.ShapeDtypeStruct((s, aw), F32), jax.ShapeDtypeStruct((s, aw), BF16),
            jax.ShapeDtypeStruct((1, D_MODEL), F32), jax.ShapeDtypeStruct((1, aw), F32),
            jax.ShapeDtypeStruct((1, aw), F32)],
        scratch_shapes=[pltpu.VMEM((aw // LANES, tm, LANES), F32), pltpu.VMEM((aw // LANES, tm, LANES), F32)],
        compiler_params=_params(dimension_semantics=("arbitrary",)),
    )(dx1, mixed, gp, w_out_b, attn, an, o_h, proj, hn)


def mlp_fwd_bwd(x1, g_pre, w1_blocks, w2_b, g_post, target):
    s = x1.shape[0]
    tm = MLP_TILE
    nblk, _, fb = w1_blocks.shape

    def body(x1_ref, gpre_ref, w1_ref, w2_ref, gpost_ref, t_ref,
             dx1_ref, h2_ref, a_ref, du_ref, dff_ref, loss_ref, dgpre_ref, dgpost_ref, u_ref):
        x1v = x1_ref[...]
        h2 = _rms_fwd(x1v, gpre_ref[...], D_MODEL).astype(BF16)
        h2_ref[...] = h2
        ff = jnp.zeros((tm, D_MODEL), F32)
        for j in range(nblk):
            cols = slice(j * fb, (j + 1) * fb)
            ru = jnp.maximum(_dot(h2, w1_ref[j]), 0.0)
            u_ref[:, cols] = ru.astype(BF16)
            a = (ru * ru).astype(BF16)
            a_ref[:, cols] = a
            ff = ff + _dot(a, w2_ref[cols, :])
        diff = x1v + _rms_fwd(ff, gpost_ref[...], D_MODEL) - t_ref[...]
        _acc_rows(loss_ref, diff * diff)
        dy = diff * (1.0 / D_MODEL)
        dff, gpost_c = _rms_bwd(dy, ff, gpost_ref[...], D_MODEL)
        _acc_rows(dgpost_ref, gpost_c)
        dff_b = dff.astype(BF16)
        dff_ref[...] = dff_b
        dh2 = jnp.zeros((tm, D_MODEL), F32)
        for j in range(nblk):
            cols = slice(j * fb, (j + 1) * fb)
            du = (_dot_nt(dff_b, w2_ref[cols, :]) * (2.0 * u_ref[:, cols])).astype(BF16)
            du_ref[:, cols] = du
            dh2 = dh2 + _dot_nt(du, w1_ref[j])
        dxa, gpre_c = _rms_bwd(dh2, x1v, gpre_ref[...], D_MODEL)
        _acc_rows(dgpre_ref, gpre_c)
        dx1_ref[...] = dy + dxa

    dm = D_MODEL
    return pl.pallas_call(
        body,
        name="mlp_fwd_bwd",
        grid=(s // tm,),
        in_specs=[_row_spec(tm, dm), _const_spec(dm), _vmem_spec(), _vmem_spec(), _const_spec(dm), _row_spec(tm, dm)],
        out_specs=[_row_spec(tm, dm), _row_spec(tm, dm), _row_spec(tm, D_FF), _row_spec(tm, D_FF), _row_spec(tm, dm),
                   _const_spec(dm), _const_spec(dm), _const_spec(dm)],
        out_shape=[
            jax.ShapeDtypeStruct((s, dm), F32),
            jax.ShapeDtypeStruct((s, dm), BF16),
            jax.ShapeDtypeStruct((s, D_FF), BF16),
            jax.ShapeDtypeStruct((s, D_FF), BF16),
            jax.ShapeDtypeStruct((s, dm), BF16),
            jax.ShapeDtypeStruct((1, dm), F32),
            jax.ShapeDtypeStruct((1, dm), F32),
            jax.ShapeDtypeStruct((1, dm), F32),
        ],
        scratch_shapes=[pltpu.VMEM((tm, D_FF), BF16)],
        compiler_params=_params(dimension_semantics=("arbitrary",)),
    )(x1, g_pre, w1_blocks, w2_b, g_post, target)


def in_proj_bwd(attn_grads, hgrn_grads, dgate, w_in_b, x, g1, dx1):
    s = x.shape[0]
    tm = PROJ_TILE
    aw = ATTN_WIDTH
    n_attn = len(attn_grads)
    flat = [g[k] for k in range(3) for g in attn_grads] + list(hgrn_grads) + [dgate]

    def body(*refs):
        parts = refs[:len(flat)]
        w_ref, x_ref, g_ref, dx1_ref, dx_ref, dproj_ref, dg_ref, scr = refs[len(flat):]
        groups = []
        for k in range(3):
            acc = None
            for p, d in zip(parts[k * n_attn:(k + 1) * n_attn], DILATIONS):
                v = _from_dilated(p, scr, d, tm)
                acc = v if acc is None else acc + v
            groups.append(acc)
        groups += [p[...] for p in parts[3 * n_attn:]]
        dh = jnp.zeros((tm, D_MODEL), F32)
        for gi, grp in enumerate(groups):
            cols = slice(gi * aw, (gi + 1) * aw)
            gb = grp.astype(BF16)
            dproj_ref[:, cols] = gb
            dh = dh + _dot_nt(gb, w_ref[:, cols])
        dxa, g_c = _rms_bwd(dh, x_ref[...], g_ref[...], D_MODEL)
        _acc_rows(dg_ref, g_c)
        dx_ref[...] = dx1_ref[...] + dxa

    dm = D_MODEL
    return pl.pallas_call(
        body,
        name="in_proj_bwd",
        grid=(s // tm,),
        in_specs=[_dilated_spec(d, tm, aw) for d in DILATIONS] * 3 + [_row_spec(tm, aw)] * 4 + [
            _vmem_spec(), _row_spec(tm, dm), _const_spec(dm), _row_spec(tm, dm)],
        out_specs=[_row_spec(tm, dm), _row_spec(tm, IN_PROJ_WIDTH), _const_spec(dm)],
        out_shape=[jax.ShapeDtypeStruct((s, dm), F32), jax.ShapeDtypeStruct((s, IN_PROJ_WIDTH), BF16),
                   jax.ShapeDtypeStruct((1, dm), F32)],
        scratch_shapes=[pltpu.VMEM((aw // LANES, tm, LANES), F32)],
        compiler_params=_params(dimension_semantics=("arbitrary",)),
    )(*flat, w_in_b, x, g1, dx1)


def wgrad(a_b, b_b, tn, name, ts=1024, per_step=1):
    s, k = a_b.shape
    n = b_b.shape[1]

    def body(a_ref, b_ref, o_ref):
        @pl.when(pl.program_id(1) == 0)
        def _():
            o_ref[...] = jnp.zeros_like(o_ref)

        a = a_ref[...]
        for jj in range(per_step):
            o_ref[jj] += _dot_tn(a, b_ref[:, jj * tn:(jj + 1) * tn])

    wide = tn * per_step
    return pl.pallas_call(
        body,
        name=name,
        grid=(n // wide, s // ts),
        in_specs=[pl.BlockSpec((ts, k), lambda j, i: (i, 0)), pl.BlockSpec((ts, wide), lambda j, i: (i, j))],
        out_specs=pl.BlockSpec((per_step, k, tn), lambda j, i: (j, 0, 0)),
        out_shape=jax.ShapeDtypeStruct((n // tn, k, tn), F32),
        compiler_params=_params(dimension_semantics=("arbitrary", "arbitrary")),
    )(a_b, b_b)


def train_step(x, target, g1, an, logits, hn, gp, g_pre, g_post, w, m, v):
    nd = len(DILATIONS)
    shard_b = {k: w[k].astype(BF16) for k in BIG}
    (w_in_g,) = run_exchange(gather_exchange([shard_b["w_in"]]), "gather_w_in")
    w_in_b = w_in_g.transpose(1, 0, 2).reshape(D_MODEL, IN_PROJ_WIDTH)

    proj, h_b, *qkvs = in_proj_fwd(x, g1, w_in_b)
    attn_parts = [attn_fwd(qkv, d) for qkv, d in zip(qkvs, DILATIONS)]
    o_h, states, a_mat, w_out_g, w1_blocks = hgrn_fwd(
        proj, logits, ride=gather_exchange([shard_b["w_out"], shard_b["w_ff1"]]))
    w_out_b = w_out_g.reshape(D_MODEL, D_MODEL)
    x1, cat_b, mixed, attn, *lses, w2_g = mix_fwd(attn_parts, o_h, proj, an, hn, w_out_b, gp, x,
                                                  ride=gather_exchange([shard_b["w_ff2"]]))
    w2_b = w2_g.reshape(D_FF, D_MODEL)
    dx1, h2_b, a_b, du_b, dff_b, loss_vec, dg_pre, dg_post = mlp_fwd_bwd(x1, g_pre, w1_blocks, w2_b, g_post, target)
    dw2 = wgrad(a_b, dff_b, D_MODEL, "wgrad_ff2", ts=512)
    dw1 = wgrad(h2_b, du_b, D_FF // N_DEV, "wgrad_ff1", per_step=2)
    dmix_b, *rest = mix_bwd(dx1, mixed, gp, w_out_b, attn, an, o_h, proj, hn)
    d_os, deltas = rest[:nd], rest[nd:2 * nd]
    d_oh, dgate, dgp, dan, dhn = rest[2 * nd:]
    dwout = wgrad(cat_b, dmix_b, D_MODEL, "wgrad_out")

    early = ("w_out", "w_ff1", "w_ff2")
    early_grads = [dwout.reshape(N_DEV, D_MODEL // N_DEV, D_MODEL), dw1, dw2.reshape(N_DEV, D_FF // N_DEV, D_MODEL)]
    attn_grads = []
    for k, d in enumerate(DILATIONS):
        ride = to_core_exchange(early_grads) if k == 0 else None
        res = attn_bwd(qkvs[k], d_os[k], lses[k], deltas[k], d, ride=ride)
        attn_grads.append(res[:3])
        if k == 0:
            pairs = [pair_sum(g, s, f"pair_sum_{name}") for g, s, name in zip(early_grads, res[3:], early)]
    dq_h, df_h, di_h, dlb, *others = hgrn_bwd(proj, logits, d_oh, states, a_mat,
                                              ride=to_chip_exchange([p[1] for p in pairs]))
    dx, dproj_b, dg1 = in_proj_bwd(attn_grads, (dq_h, df_h, di_h), dgate, w_in_b, x, g1, dx1)
    dwin = wgrad(h_b, dproj_b, 2 * IN_PROJ_WIDTH // N_DEV, "wgrad_in")
    big = {name: sum_adamw(p[0], o, w[name], m[name], v[name], f"sum_adamw_{name}")
           for name, p, o in zip(early, pairs, others)}

    shard_w = IN_PROJ_WIDTH // N_DEV
    dwin_blocks = dwin.reshape(N_DEV // 2, D_MODEL, 2, shard_w).transpose(0, 2, 1, 3).reshape(N_DEV, D_MODEL, shard_w)
    (from_sibling,) = run_exchange(to_core_exchange([dwin_blocks]), "reduce_w_in_to_core")
    pair_in, pair_in_b = pair_sum(dwin_blocks, from_sibling, "pair_sum_w_in")
    (others_in,) = run_exchange(to_chip_exchange([pair_in_b]), "reduce_w_in_to_chip")
    big["w_in"] = sum_adamw(pair_in, others_in, w["w_in"], m["w_in"], v["w_in"], "sum_adamw_w_in")
    small = dict(dg1=dg1, dan=dan, dlb=dlb, dhn=dhn, dgp=dgp, dg_pre=dg_pre, dg_post=dg_post, loss_vec=loss_vec)
    return dx, big, small


def _position():
    x, y, c = lax.axis_index("x"), lax.axis_index("y"), lax.axis_index("c")
    other_chips = [(1 - x, y), (x, 1 - y), (1 - x, 1 - y)]
    return x, y, c, other_chips


def _any_spec():
    return pl.BlockSpec(memory_space=pl.ANY)


class Exchange:
    def __init__(self, arrays, out_shape, sems, stages):
        self.arrays, self.out_shape, self.sems, self.stages = list(arrays), list(out_shape), list(sems), stages


def gather_exchange(shards):
    n = len(shards)

    def stages(ins, outs, sems):
        send_sems, recv_sems, local_sems = sems

        def parts():
            x, y, c, chips = _position()
            me, sibling = (x, y, c), (x, y, 1 - c)

            def slot(a, px, py, pc):
                return outs[a].at[4 * px + 2 * py + pc]

            def copy(a, k, block, to, src=None):
                return pltpu.make_async_remote_copy(
                    src_ref=slot(a, *block) if src is None else src, dst_ref=slot(a, *block),
                    send_sem=send_sems.at[a, k], recv_sem=recv_sems.at[a, k], device_id=to, device_id_type=MESH)

            local = [pltpu.make_async_copy(ins[a], slot(a, *me), local_sems.at[a]) for a in range(n)]
            first = []
            for a in range(n):
                first.append(copy(a, 0, me, sibling, src=ins[a]))
                first += [copy(a, 1 + j, me, (*chip, c), src=ins[a]) for j, chip in enumerate(chips)]
            passed = [copy(a, 4 + j, (*chip, c), sibling) for j, chip in enumerate(chips) for a in range(n)]
            return c, chips, me, sibling, copy, local, first, passed

        def begin():
            _, _, _, _, _, local, first, _ = parts()
            for cp in local + first:
                cp.start()

        def middle():
            c, chips, me, _, copy, _, _, passed = parts()
            k = 0
            for j, chip in enumerate(chips):
                for a in range(n):
                    copy(a, 1 + j, (*chip, c), me).wait_recv()
                    passed[k].start()
                    k += 1

        def end():
            c, chips, me, sibling, copy, local, first, passed = parts()
            for a in range(n):
                copy(a, 0, sibling, me).wait_recv()
                for j, chip in enumerate(chips):
                    copy(a, 4 + j, (*chip, 1 - c), me).wait_recv()
            for cp in first + passed:
                cp.wait_send()
            for cp in local:
                cp.wait()

        return begin, middle, end

    return Exchange(
        shards, [jax.ShapeDtypeStruct((N_DEV,) + sh.shape, sh.dtype) for sh in shards],
        [pltpu.SemaphoreType.DMA((n, 7)), pltpu.SemaphoreType.DMA((n, 7)), pltpu.SemaphoreType.DMA((n,))], stages)


def to_core_exchange(grads):
    n = len(grads)

    def stages(ins, outs, sems):
        send_sems, recv_sems = sems

        def copies():
            x, y, c, _ = _position()
            return [pltpu.make_async_remote_copy(
                src_ref=ins[a].at[2 * q + (1 - c)], dst_ref=outs[a].at[q], send_sem=send_sems.at[a, q],
                recv_sem=recv_sems.at[a, q], device_id=(x, y, 1 - c), device_id_type=MESH)
                for a in range(n) for q in range(4)]

        def begin():
            for cp in copies():
                cp.start()

        def end():
            for cp in copies():
                cp.wait()

        return begin, None, end

    return Exchange(grads, [jax.ShapeDtypeStruct((4,) + g.shape[1:], g.dtype) for g in grads],
                    [pltpu.SemaphoreType.DMA((n, 4)), pltpu.SemaphoreType.DMA((n, 4))], stages)


def pair_sum(grad, from_sibling, name):
    _, r, cdim = grad.shape
    tr = min(r, ELEMENTWISE_ROWS)
    c_idx = lax.axis_index("c").astype(jnp.int32).reshape(1)

    def body(c_ref, g_ref, s_ref, o_ref, ob_ref):
        total = g_ref[...] + s_ref[...]
        o_ref[...] = total
        ob_ref[...] = total.astype(BF16)

    blk = lambda: pl.BlockSpec((1, tr, cdim), lambda q, i, cr: (q, i, 0))
    return pl.pallas_call(
        body,
        name=name,
        grid_spec=pltpu.PrefetchScalarGridSpec(
            num_scalar_prefetch=1,
            grid=(4, r // tr),
            in_specs=[pl.BlockSpec((1, tr, cdim), lambda q, i, cr: (2 * q + cr[0], i, 0)), blk()],
            out_specs=[blk(), blk()],
        ),
        out_shape=[jax.ShapeDtypeStruct((4, r, cdim), F32), jax.ShapeDtypeStruct((4, r, cdim), BF16)],
        compiler_params=_params(dimension_semantics=("arbitrary", "arbitrary")),
    )(c_idx, grad, from_sibling)


def to_chip_exchange(pairs):
    n = len(pairs)

    def stages(ins, outs, sems):
        send_sems, recv_sems = sems

        def copies():
            x, y, c, chips = _position()
            return [pltpu.make_async_remote_copy(
                src_ref=ins[a].at[2 * px + py], dst_ref=outs[a].at[j], send_sem=send_sems.at[a, j],
                recv_sem=recv_sems.at[a, j], device_id=(px, py, c), device_id_type=MESH)
                for a in range(n) for j, (px, py) in enumerate(chips)]

        def begin():
            for cp in copies():
                cp.start()

        def end():
            for cp in copies():
                cp.wait()

        return begin, None, end

    return Exchange(pairs, [jax.ShapeDtypeStruct((3,) + p.shape[1:], p.dtype) for p in pairs],
                    [pltpu.SemaphoreType.DMA((n, 3)), pltpu.SemaphoreType.DMA((n, 3))], stages)


def run_exchange(ex, name):
    n_in, n_out = len(ex.arrays), len(ex.out_shape)

    def body(*refs):
        begin, middle, end = ex.stages(refs[:n_in], refs[n_in:n_in + n_out], refs[n_in + n_out:])
        begin()
        if middle is not None:
            middle()
        end()

    return pl.pallas_call(
        body,
        name=name,
        in_specs=[_any_spec()] * n_in,
        out_specs=[_any_spec()] * n_out,
        out_shape=ex.out_shape,
        scratch_shapes=ex.sems,
    )(*ex.arrays)


def _riding(body, n_in, n_out, n_scratch, ex, first, middle, last):
    if ex is None:
        return body
    r_in, r_out = len(ex.arrays), len(ex.out_shape)

    def wrapped(*refs):
        k_in, refs = refs[:n_in], refs[n_in:]
        e_in, refs = refs[:r_in], refs[r_in:]
        k_out, refs = refs[:n_out], refs[n_out:]
        e_out, refs = refs[:r_out], refs[r_out:]
        k_scr, e_sems = refs[:n_scratch], refs[n_scratch:]
        begin, mid, end = ex.stages(e_in, e_out, e_sems)
        pl.when(first())(begin)
        body(*k_in, *k_out, *k_scr)
        if mid is not None:
            pl.when(middle())(mid)
        pl.when(last())(end)

    return wrapped


def _ride_specs(ex):
    if ex is None:
        return [], [], [], [], []
    return [_any_spec()] * len(ex.arrays), [_any_spec()] * len(ex.out_shape), ex.out_shape, ex.sems, ex.arrays


def _adamw(w, g, m, v):
    m = ADAM_B1 * m + (1.0 - ADAM_B1) * g
    v = ADAM_B2 * v + (1.0 - ADAM_B2) * (g * g)
    m_hat = m / (1.0 - ADAM_B1 ** ADAM_STEP)
    v_hat = v / (1.0 - ADAM_B2 ** ADAM_STEP)
    delta = -ADAM_LR * (m_hat / (jnp.sqrt(v_hat) + ADAM_EPS) + ADAM_WD * w)
    return delta, m, v


def sum_adamw(pairs, others, w, m, v, name):
    r, cdim = w.shape
    tr = min(r, ELEMENTWISE_ROWS // 2)
    chip_idx =(2 * lax.axis_index("x") + lax.axis_index("y")).astype(jnp.int32).reshape(1)

    def body(q_ref, p_ref, o_ref, w_ref, m_ref, v_ref, g_out, d_out, m_out, v_out):
        g = p_ref[0] + o_ref[0].astype(F32) + o_ref[1].astype(F32) + o_ref[2].astype(F32)
        g_out[...] = g
        d_out[...], m_out[...], v_out[...] = _adamw(w_ref[...], g, m_ref[...], v_ref[...])

    tile = lambda: pl.BlockSpec((tr, cdim), lambda i, qr: (i, 0))
    return pl.pallas_call(
        body,
        name=name,
        grid_spec=pltpu.PrefetchScalarGridSpec(
            num_scalar_prefetch=1,
            grid=(r // tr,),
            in_specs=[pl.BlockSpec((1, tr, cdim), lambda i, qr: (qr[0], i, 0)),
                      pl.BlockSpec((3, tr, cdim), lambda i, qr: (0, i, 0)), tile(), tile(), tile()],
            out_specs=[tile(), tile(), tile(), tile()],
        ),
        out_shape=[jax.ShapeDtypeStruct((r, cdim), F32)] * 4,
        compiler_params=_params(dimension_semantics=("arbitrary",)),
    )(chip_idx, pairs, others, w, m, v)


SMALL_ROWS = 8


def small_all_reduce(packed):
    shape = packed.shape

    def body(in_ref, out_ref, recv_ref, send_sems, recv_sems):
        x, y, c, _ = _position()
        my_id = 4 * x + 2 * y + c
        recv_ref[my_id] = in_ref[...]
        copies = []
        for rel in range(1, N_DEV):
            fx, fy, fc = (rel >> 2) & 1, (rel >> 1) & 1, rel & 1
            px = 1 - x if fx else x
            py = 1 - y if fy else y
            pc = 1 - c if fc else c
            cp = pltpu.make_async_remote_copy(
                src_ref=in_ref, dst_ref=recv_ref.at[my_id], send_sem=send_sems.at[rel - 1],
                recv_sem=recv_sems.at[rel - 1], device_id=(px, py, pc), device_id_type=MESH)
            cp.start()
            copies.append((cp, pltpu.make_async_remote_copy(
                src_ref=in_ref, dst_ref=recv_ref.at[4 * px + 2 * py + pc], send_sem=send_sems.at[rel - 1],
                recv_sem=recv_sems.at[rel - 1], device_id=(px, py, pc), device_id_type=MESH)))
        for cp, landing in copies:
            landing.wait_recv()
        for cp, landing in copies:
            cp.wait_send()
        total = recv_ref[0]
        for k in range(1, N_DEV):
            total = total + recv_ref[k]
        out_ref[...] = total

    return pl.pallas_call(
        body,
        name="small_all_reduce",
        in_specs=[_vmem_spec()],
        out_specs=_vmem_spec(),
        out_shape=jax.ShapeDtypeStruct(shape, F32),
        scratch_shapes=[pltpu.VMEM((N_DEV,) + shape, F32), pltpu.SemaphoreType.DMA((N_DEV - 1,)),
                        pltpu.SemaphoreType.DMA((N_DEV - 1,))],
    )(packed)


def small_adamw(reduced, w, m, v):
    def body(r_ref, w_ref, m_ref, v_ref, g_out, d_out, m_out, v_out, loss_out):
        red = r_ref[...]
        wv = w_ref[...]
        lb = _lower_bound(jnp.concatenate([wv[5:6, :HGRN_WIDTH], wv[5:6, HGRN_WIDTH:]], axis=0))
        t = red[5:6, :HGRN_WIDTH] * lb * (1.0 - lb)
        row = lax.broadcasted_iota(jnp.int32, red.shape, 0)
        g = jnp.where(row == 5, jnp.concatenate([t, -t], axis=1), jnp.where(row >= 6, 0.0, red))
        g_out[...] = g
        d_out[...], m_out[...], v_out[...] = _adamw(wv, g, m_ref[...], v_ref[...])
        loss = jnp.sum(red[6:7, :], axis=-1, keepdims=True) * (0.5 / D_MODEL)
        loss_out[...] = jnp.broadcast_to(loss, loss_out.shape)

    return pl.pallas_call(
        body,
        name="small_adamw",
        in_specs=[_vmem_spec()] * 4,
        out_specs=[_vmem_spec()] * 5,
        out_shape=[jax.ShapeDtypeStruct(reduced.shape, F32)] * 4 + [jax.ShapeDtypeStruct((8, 128), F32)],
    )(reduced, w, m, v)


def _pack_small(g1, gp, g_pre, g_post, an, hn, logits_or_dlb, extra=None):
    row5 = logits_or_dlb.reshape(1, -1)
    row5 = jnp.pad(row5, ((0, 0), (0, D_MODEL - row5.shape[1])))
    row6 = jnp.zeros((1, D_MODEL), F32) if extra is None else extra
    return jnp.concatenate([g1, gp, g_pre, g_post, jnp.concatenate([an, hn], axis=1), row5, row6,
                            jnp.zeros((1, D_MODEL), F32)], axis=0)


def _unpack_small(p):
    return dict(mix_pre_norm=p[0:1], mix_post_norm=p[1:2], mlp_pre_norm=p[2:3], mlp_post_norm=p[3:4],
                attn_out_norm=p[4:5, :ATTN_WIDTH], hgrn_out_norm=p[4:5, ATTN_WIDTH:],
                hgrn_lb_logits=p[5].reshape(2, HGRN_WIDTH))


BIG = ("w_in", "w_out", "w_ff1", "w_ff2")
ORDER = ("mix_pre_norm", "w_in", "attn_out_norm", "hgrn_lb_logits", "hgrn_out_norm", "w_out", "mix_post_norm",
         "mlp_pre_norm", "w_ff1", "w_ff2", "mlp_post_norm")


def kernel(x, mix_pre_norm, w_in, attn_out_norm, hgrn_lb_logits, hgrn_out_norm, w_out, mix_post_norm, mlp_pre_norm, w_ff1, w_ff2, mlp_post_norm, loss_target, m_mix_pre_norm, m_w_in, m_attn_out_norm, m_hgrn_lb_logits, m_hgrn_out_norm, m_w_out, m_mix_post_norm, m_mlp_pre_norm, m_w_ff1, m_w_ff2, m_mlp_post_norm, v_mix_pre_norm, v_w_in, v_attn_out_norm, v_hgrn_lb_logits, v_hgrn_out_norm, v_w_out, v_mix_post_norm, v_mlp_pre_norm, v_w_ff1, v_w_ff2, v_mlp_post_norm):
    w = dict(w_in=w_in[0], w_out=w_out[0], w_ff1=w_ff1[0], w_ff2=w_ff2[0])
    m = dict(w_in=m_w_in[0], w_out=m_w_out[0], w_ff1=m_w_ff1[0], w_ff2=m_w_ff2[0])
    v = dict(w_in=v_w_in[0], w_out=v_w_out[0], w_ff1=v_w_ff1[0], w_ff2=v_w_ff2[0])

    dx, big, small = train_step(x[0], loss_target[0], mix_pre_norm, attn_out_norm, hgrn_lb_logits, hgrn_out_norm,
                                mix_post_norm, mlp_pre_norm, mlp_post_norm, w, m, v)

    packed_g = _pack_small(small["dg1"], small["dgp"], small["dg_pre"], small["dg_post"], small["dan"], small["dhn"],
                           small["dlb"], small["loss_vec"])
    reduced = small_all_reduce(packed_g)
    pack = lambda a, b, c2, d, e, f, g: _pack_small(a, b, c2, d, e, f, g)
    w_s = pack(mix_pre_norm, mix_post_norm, mlp_pre_norm, mlp_post_norm, attn_out_norm, hgrn_out_norm, hgrn_lb_logits)
    m_s = pack(m_mix_pre_norm, m_mix_post_norm, m_mlp_pre_norm, m_mlp_post_norm, m_attn_out_norm, m_hgrn_out_norm,
               m_hgrn_lb_logits)
    v_s = pack(v_mix_pre_norm, v_mix_post_norm, v_mlp_pre_norm, v_mlp_post_norm, v_attn_out_norm, v_hgrn_out_norm,
               v_hgrn_lb_logits)
    g_s, d_s, nm_s, nv_s, loss = small_adamw(reduced, w_s, m_s, v_s)
    small_out = [_unpack_small(t) for t in (g_s, d_s, nm_s, nv_s)]

    outs = [loss[0, 0], dx[None]]
    for kind in range(4):
        for name in ORDER:
            outs.append(big[name][kind][None] if name in BIG else small_out[kind][name])
    return tuple(outs)
```

```python
import functools
import math

import jax
import jax.numpy as jnp
from jax import lax
from jax.experimental import pallas as pl
from jax.experimental.pallas import tpu as pltpu

F32 = jnp.float32
BF16 = jnp.bfloat16

D_MODEL = 1024
SEQ = 4096
ATTN_WIDTH = 512
ATTN_HEAD_DIM = 64
ATTN_HEADS = 8
ATTN_BLOCK = 128
DILATIONS = (1, 4, 16)
HGRN_WIDTH = 512
HGRN_HEADS = 4
HGRN_HEAD_DIM = 128
HGRN_CHUNK = 64
IN_PROJ_WIDTH = 3584
D_FF = 4096
RMS_EPS = 1e-6
N_DEV = 8
ADAM_LR = 0.001
ADAM_B1 = 0.9
ADAM_B2 = 0.999
ADAM_EPS = 1e-08
ADAM_WD = 0.01
ADAM_STEP = 10

SUBLANES = 8
LANES = 128
COLUMN_UNROLL = 8
HGRN_CHUNKS_PER_STEP = 2
SUB_BLOCK = 16
TOKEN_TILE = 256
ELEMENTWISE_ROWS = 1024
MLP_TILE = 256
PROJ_TILE = 512
VMEM_LIMIT = 56 * 1024 * 1024
NEG_BIG = -1e30
MESH = pl.DeviceIdType.MESH


def _params(**kw):
    return pltpu.CompilerParams(vmem_limit_bytes=VMEM_LIMIT, **kw)


def _vmem_spec():
    return pl.BlockSpec(memory_space=pltpu.VMEM)


def _dot(a, b):
    return jnp.dot(a, b, preferred_element_type=F32)


def _dot_nt(a, b):
    return lax.dot_general(a, b, (((1,), (1,)), ((), ())), preferred_element_type=F32)


def _dot_tn(a, b):
    return lax.dot_general(a, b, (((0,), (0,)), ((), ())), preferred_element_type=F32)


def _sigmoid(x):
    return 1.0 / (1.0 + jnp.exp(-x))


def _rms_fwd(x, gain, width):
    r = lax.rsqrt(jnp.sum(x * x, axis=-1, keepdims=True) * (1.0 / width) + RMS_EPS)
    return x * r * gain


def _rms_bwd(dy, x, gain, width):
    r = lax.rsqrt(jnp.sum(x * x, axis=-1, keepdims=True) * (1.0 / width) + RMS_EPS)
    xhat = x * r
    dxhat = dy * gain
    dx = r * (dxhat - xhat * (jnp.sum(dxhat * xhat, axis=-1, keepdims=True) * (1.0 / width)))
    return dx, dy * xhat


def _split3(x):
    hi = x.astype(BF16)
    r1 = x - hi.astype(F32)
    mid = r1.astype(BF16)
    lo = (r1 - mid.astype(F32)).astype(BF16)
    return hi, mid, lo


def _tri_sum(tri_bf16, x):
    hi, mid, lo = _split3(x)
    return _dot(tri_bf16, hi) + _dot(tri_bf16, mid) + _dot(tri_bf16, lo)


def _dilated_spec(d, tm, width):
    return pl.BlockSpec((d, tm // d, width), lambda i: (0, i, 0))


def _lane_blocks(ref, value):
    for c in range(ref.shape[0]):
        ref[c] = value[:, c * LANES:(c + 1) * LANES]


def _to_dilated(src_ref, dst_ref, d, tm, cast=None):
    for r in range(d):
        for c in range(src_ref.shape[0]):
            v = src_ref[c] if d == 1 else src_ref[c, pl.ds(r, tm // d, stride=d), :]
            dst_ref[r, :, c * LANES:(c + 1) * LANES] = v if cast is None else v.astype(cast)


def _from_dilated(src_ref, scratch_ref, d, tm):
    if d == 1:
        return src_ref[0].astype(F32)
    nblk = scratch_ref.shape[0]
    for r in range(d):
        for c in range(nblk):
            scratch_ref[c, pl.ds(r, tm // d, stride=d), :] = src_ref[r, :, c * LANES:(c + 1) * LANES].astype(F32)
    return jnp.concatenate([scratch_ref[c] for c in range(nblk)], axis=1)


def in_proj_fwd(x, g1, w_in_b):
    s = x.shape[0]
    tm = PROJ_TILE
    qkv_w = 3 * ATTN_WIDTH
    hg_w = IN_PROJ_WIDTH - qkv_w

    def body(x_ref, g_ref, w_ref, hg_ref, h_ref, *rest):
        qkv_refs, qkv_scr = rest[:len(DILATIONS)], rest[len(DILATIONS)]
        h = _rms_fwd(x_ref[...], g_ref[...], D_MODEL).astype(BF16)
        h_ref[...] = h
        proj = _dot(h, w_ref[...])
        hg_ref[...] = proj[:, qkv_w:]
        _lane_blocks(qkv_scr, proj[:, :qkv_w])
        for d, ref in zip(DILATIONS, qkv_refs):
            _to_dilated(qkv_scr, ref, d, tm, cast=BF16)

    return pl.pallas_call(
        body,
        name="in_proj_fwd",
        grid=(s // tm,),
        in_specs=[
            pl.BlockSpec((tm, D_MODEL), lambda i: (i, 0)),
            pl.BlockSpec((1, D_MODEL), lambda i: (0, 0)),
            _vmem_spec(),
        ],
        out_specs=[
            pl.BlockSpec((tm, hg_w), lambda i: (i, 0)),
            pl.BlockSpec((tm, D_MODEL), lambda i: (i, 0)),
        ] + [_dilated_spec(d, tm, qkv_w) for d in DILATIONS],
        out_shape=[jax.ShapeDtypeStruct((s, hg_w), F32), jax.ShapeDtypeStruct((s, D_MODEL), BF16)] + [
            jax.ShapeDtypeStruct((d, s // d, qkv_w), BF16) for d in DILATIONS],
        scratch_shapes=[pltpu.VMEM((qkv_w // LANES, tm, LANES), F32)],
        compiler_params=_params(dimension_semantics=("arbitrary",)),
    )(x, g1, w_in_b)


ATTN_SCALE = ATTN_HEAD_DIM ** -0.5


def _fill_attn_bias(bias_ref, dilation):
    qi = lax.broadcasted_iota(jnp.int32, (ATTN_BLOCK, 2 * ATTN_BLOCK), 0)
    kj = lax.broadcasted_iota(jnp.int32, (ATTN_BLOCK, 2 * ATTN_BLOCK), 1)
    dist = qi + ATTN_BLOCK - kj
    valid = (dist >= 0) & (dist <= ATTN_BLOCK)
    for head in range(ATTN_HEADS):
        slope = 2.0 ** (-8.0 * (head + 1) / ATTN_HEADS)
        bias = jnp.where(valid, dist.astype(F32) * (-slope * dilation), NEG_BIG)
        bias_ref[0, head] = bias
        bias_ref[1, head] = jnp.where(kj >= ATTN_BLOCK, bias, NEG_BIG)


def _stack_heads(x):
    low = _lane_half(x.shape, 0)
    zero = jnp.zeros_like(x)
    return jnp.concatenate([jnp.where(low, x, zero), jnp.where(low, zero, x)], axis=0)


def _unstack_heads(y):
    half = y.shape[0] // 2
    return jnp.where(_lane_half((half, y.shape[1]), 0), y[:half], y[half:])


def _attn_scores(q_stack, kcat, bias_ref, pair, first_block):
    f = first_block.astype(jnp.int32)
    bias = jnp.concatenate([bias_ref[f, 2 * pair], bias_ref[f, 2 * pair + 1]], axis=0)
    return _dot_nt(q_stack, kcat) + bias


def _lane_half(shape, sub):
    lane = lax.broadcasted_iota(jnp.int32, shape, 1)
    return (lane < ATTN_HEAD_DIM) if sub == 0 else (lane >= ATTN_HEAD_DIM)


def _sub_block(col, row):
    return pl.BlockSpec((None, ATTN_BLOCK, ATTN_WIDTH), lambda r, n: (r, row(n), col))


def attn_fwd(qkv, dilation):
    d, length, _ = qkv.shape
    assert d == dilation
    nb = length // ATTN_BLOCK

    def body(q_ref, kc_ref, kp_ref, vc_ref, vp_ref, o_ref, lse_ref, bias_ref):
        @pl.when((pl.program_id(0) == 0) & (pl.program_id(1) == 0))
        def _():
            _fill_attn_bias(bias_ref, d)

        first = pl.program_id(1) == 0
        for pair in range(ATTN_HEADS // 2):
            lanes = slice(pair * 128, (pair + 1) * 128)
            q_stack = _stack_heads(q_ref[:, lanes] * ATTN_SCALE)
            kcat = jnp.concatenate([kp_ref[:, lanes], kc_ref[:, lanes]], axis=0)
            vcat = jnp.concatenate([vp_ref[:, lanes], vc_ref[:, lanes]], axis=0)
            sc = _attn_scores(q_stack, kcat, bias_ref, pair, first)
            m = jnp.max(sc, axis=-1, keepdims=True)
            p = jnp.exp(sc - m)
            den = jnp.sum(p, axis=-1, keepdims=True)
            o_ref[:, lanes] = _unstack_heads(_dot(p.astype(BF16), vcat) / den).astype(BF16)
            lse_ref[:, lanes] = _unstack_heads(jnp.broadcast_to(m + jnp.log(den), (2 * ATTN_BLOCK, 128)))

    cur = lambda n: n
    prev = lambda n: jnp.maximum(n - 1, 0)
    return pl.pallas_call(
        body,
        name=f"attn_fwd_d{d}",
        grid=(d, nb),
        in_specs=[_sub_block(0, cur), _sub_block(1, cur), _sub_block(1, prev), _sub_block(2, cur), _sub_block(2, prev)],
        out_specs=[_sub_block(0, cur), _sub_block(0, cur)],
        out_shape=[jax.ShapeDtypeStruct((d, length, ATTN_WIDTH), BF16), jax.ShapeDtypeStruct((d, length, ATTN_WIDTH), F32)],
        scratch_shapes=[pltpu.VMEM((2, ATTN_HEADS, ATTN_BLOCK, 2 * ATTN_BLOCK), F32)],
        compiler_params=_params(dimension_semantics=("arbitrary", "arbitrary")),
    )(qkv, qkv, qkv, qkv, qkv)


def attn_bwd(qkv, d_out, lse, delta, dilation, ride=None):
    d, length, _ = qkv.shape
    assert d == dilation
    nb = length // ATTN_BLOCK

    steps = d * nb + 1

    def body(q_ref, kc_ref, kp_ref, vc_ref, vp_ref, do_ref, lse_ref, dl_ref, dq_ref, dk_ref, dv_ref, ck_ref, cv_ref,
             bias_ref):
        t = pl.program_id(0)

        @pl.when(t == 0)
        def _():
            ck_ref[...] = jnp.zeros_like(ck_ref)
            cv_ref[...] = jnp.zeros_like(cv_ref)
            _fill_attn_bias(bias_ref, d)

        @pl.when(t < steps - 1)
        def _():
            first = t % nb == 0
            for pair in range(ATTN_HEADS // 2):
                lanes = slice(pair * 128, (pair + 1) * 128)
                q_stack = _stack_heads(q_ref[:, lanes] * ATTN_SCALE)
                do_stack = _stack_heads(do_ref[:, lanes])
                kcat = jnp.concatenate([kp_ref[:, lanes], kc_ref[:, lanes]], axis=0)
                vcat = jnp.concatenate([vp_ref[:, lanes], vc_ref[:, lanes]], axis=0)
                col_a, col_b = pair * 128, pair * 128 + ATTN_HEAD_DIM
                lse_col = jnp.concatenate([lse_ref[:, col_a:col_a + 1], lse_ref[:, col_b:col_b + 1]], axis=0)
                dl_col = jnp.concatenate([dl_ref[:, col_a:col_a + 1], dl_ref[:, col_b:col_b + 1]], axis=0)
                p = jnp.exp(_attn_scores(q_stack, kcat, bias_ref, pair, first) - lse_col)
                ds = (p * (_dot_nt(do_stack, vcat) - dl_col)).astype(BF16)
                dq_ref[:, lanes] = (_unstack_heads(_dot(ds, kcat)) * ATTN_SCALE).astype(BF16)
                dk_cat = _dot_tn(ds, q_stack)
                dv_cat = _dot_tn(p.astype(BF16), do_stack)
                dk_ref[:, lanes] = (ck_ref[:, lanes] + dk_cat[:ATTN_BLOCK]).astype(BF16)
                dv_ref[:, lanes] = (cv_ref[:, lanes] + dv_cat[:ATTN_BLOCK]).astype(BF16)
                ck_ref[:, lanes] = dk_cat[ATTN_BLOCK:]
                cv_ref[:, lanes] = dv_cat[ATTN_BLOCK:]

        @pl.when(t == steps - 1)
        def _():
            dk_ref[...] = ck_ref[...].astype(BF16)
            dv_ref[...] = cv_ref[...].astype(BF16)

    blk = (ATTN_BLOCK, ATTN_WIDTH)

    def spec(col, shift):
        def index(t):
            f = jnp.minimum(t, steps - 2) if shift > -2 else jnp.maximum(t - 1, 0)
            r, n = f // nb, f % nb
            return (r, jnp.maximum(n - 1, 0) if shift == -1 else n, col)
        return pl.BlockSpec((None, ATTN_BLOCK, ATTN_WIDTH), index)

    step = lambda k: (lambda: pl.program_id(0) == k)
    e_in, e_out, e_shape, e_scr, e_args = _ride_specs(ride)
    return pl.pallas_call(
        _riding(body, 8, 3, 3, ride, step(0), step(steps // 2), step(steps - 1)),
        name=f"attn_bwd_d{d}",
        grid=(steps,),
        in_specs=[spec(0, 0), spec(1, 0), spec(1, -1), spec(2, 0), spec(2, -1), spec(0, 0), spec(0, 0), spec(0, 0)] + e_in,
        out_specs=[spec(0, 0), spec(0, -2), spec(0, -2)] + e_out,
        out_shape=[jax.ShapeDtypeStruct((d, length, ATTN_WIDTH), BF16)] * 3 + e_shape,
        scratch_shapes=[pltpu.VMEM(blk, F32), pltpu.VMEM(blk, F32),
                        pltpu.VMEM((2, ATTN_HEADS, ATTN_BLOCK, 2 * ATTN_BLOCK), F32)] + e_scr,
        compiler_params=_params(dimension_semantics=("arbitrary",)),
    )(qkv, qkv, qkv, qkv, qkv, d_out, lse, delta, *e_args)


def _lower_bound(logits):
    return _sigmoid(logits[0:1, :] - logits[1:2, :])


def _hgrn_gates(q, fp, lb):
    sq = _sigmoid(q)
    qf = q * sq
    sig = _sigmoid(fp)
    f = lb + (1.0 - lb) * sig
    kf = (1.0 - lb) * _sigmoid(-fp)
    return sq, qf, sig, f, kf


def _tril_bf16(n, upper=False):
    r = lax.broadcasted_iota(jnp.int32, (n, n), 0)
    c = lax.broadcasted_iota(jnp.int32, (n, n), 1)
    keep = (c >= r) if upper else (c <= r)
    return jnp.where(keep, 1.0, 0.0).astype(BF16)


def _hgrn_diagonal_loops(c_len, diagonal):
    for half in range(SUB_BLOCK // SUBLANES):
        def step(jj, carry, half=half):
            j = half * SUBLANES + jj
            for i in range(c_len // SUB_BLOCK):
                diagonal(slice(i * SUB_BLOCK + half * SUBLANES, (i + 1) * SUB_BLOCK), j, i * SUB_BLOCK + j)
            return carry

        lax.fori_loop(0, SUBLANES, step, 0, unroll=COLUMN_UNROLL)


def _hgrn_off_diagonal(b, qf, kf):
    c_len, width = b.shape
    edges = [b[0:1, :]] + [b[i * SUB_BLOCK - 1:i * SUB_BLOCK, :] for i in range(1, c_len // SUB_BLOCK)]
    eq = jnp.exp(b - jnp.concatenate([jnp.broadcast_to(e, (SUB_BLOCK, width)) for e in edges], axis=0))
    q_til = qf * eq
    k_til, ek = [], []
    for i in range(1, c_len // SUB_BLOCK):
        n = i * SUB_BLOCK
        e = jnp.exp(edges[i] - b[:n, :])
        ek.append(e)
        k_til.append(jnp.concatenate([kf[:n, :] * e, jnp.zeros((2 * c_len - n, width), F32)], axis=0))
    return q_til, k_til, eq, ek


def _split2(x):
    hi = x.astype(BF16)
    return hi, (x - hi.astype(F32)).astype(BF16)


def hgrn_fwd(proj, lb, ride=None):
    s = proj.shape[0]
    c_len, nh, hd = HGRN_CHUNK, HGRN_HEADS, HGRN_HEAD_DIM
    n_chunks = s // c_len
    col0 = 0

    cps = HGRN_CHUNKS_PER_STEP
    n_steps = n_chunks // cps

    def body(q_ref, f_ref, i_ref, lb_ref, o_ref, st_out_ref, a_out_ref, st_ref, b_ref, qf_ref, kf_ref, a_ref):
        @pl.when(pl.program_id(0) == 0)
        def _():
            st_ref[...] = jnp.zeros_like(st_ref)

        lbv = _lower_bound(lb_ref[...])
        for u in range(cps):
            rs = slice(u * c_len, (u + 1) * c_len)
            b_u, qf_u, kf_u, a_u = b_ref.at[u], qf_ref.at[u], kf_ref.at[u], a_ref.at[u]
            _, qf, _, f, kf = _hgrn_gates(q_ref[rs, :], f_ref[rs, :], lbv)
            b = _tri_sum(_tril_bf16(c_len), jnp.log(f))
            b_u[...] = b
            qf_u[...] = qf
            kf_u[...] = kf
            a_u[...] = jnp.zeros_like(a_u)

            def diagonal(rows, j, key, b_u=b_u, qf_u=qf_u, kf_u=kf_u, a_u=a_u):
                bj = b_u[pl.ds(key, 1), :]
                kj = kf_u[pl.ds(key, 1), :]
                nrow = rows.stop - rows.start
                t_loc = lax.broadcasted_iota(jnp.int32, (nrow, nh * hd), 0) + (rows.start % SUB_BLOCK)
                e = jnp.exp(jnp.where(t_loc >= j, b_u[rows, :] - bj, NEG_BIG))
                prod = qf_u[rows, :] * kj * e
                lane = lax.broadcasted_iota(jnp.int32, (nrow, hd), 1)
                for h in range(nh):
                    col = jnp.sum(prod[:, h * hd:(h + 1) * hd], axis=-1, keepdims=True)
                    a_u[h, rows, :] = jnp.where(lane == key, col, a_u[h, rows, :])

            _hgrn_diagonal_loops(c_len, diagonal)
            q_til, k_til, _, _ = _hgrn_off_diagonal(b, qf, kf)
            q_til = q_til.astype(BF16)
            k_til = [k.astype(BF16) for k in k_til]

            b_last = b[c_len - 1:c_len, :]
            qb = (qf * jnp.exp(b)).astype(BF16)
            kb2 = (kf * jnp.exp(b_last - b)).astype(BF16)
            vf = i_ref[rs, :].astype(BF16)
            for h in range(nh):
                hs = slice(h * hd, (h + 1) * hd)
                st = st_ref[h]
                st_out_ref[u, h] = st
                off = [jnp.zeros((SUB_BLOCK, hd), F32)]
                for i in range(1, c_len // SUB_BLOCK):
                    off.append(_dot_nt(q_til[i * SUB_BLOCK:(i + 1) * SUB_BLOCK, hs], k_til[i - 1][:, hs]))
                a_h = a_u[h] + jnp.concatenate(off, axis=0)
                a_out_ref[rs, hs] = a_h
                o_ref[rs, hs] = _dot_nt(qb[:, hs], st.astype(BF16)) + _dot(a_h[:, :c_len].astype(BF16), vf[:, hs])
                st_ref[h] = st * jnp.exp(b_last[:, hs]) + _dot_tn(vf[:, hs], kb2[:, hs])

    blk = (cps * c_len, HGRN_WIDTH)
    sblk = (cps, c_len, HGRN_WIDTH)
    step = lambda k: (lambda: pl.program_id(0) == k)
    e_in, e_out, e_shape, e_scr, e_args = _ride_specs(ride)
    return pl.pallas_call(
        _riding(body, 4, 3, 5, ride, step(0), step((7 * n_steps) // 8), step(n_steps - 1)),
        name="hgrn_fwd",
        grid=(n_steps,),
        in_specs=[
            pl.BlockSpec(blk, lambda c: (c, col0)),
            pl.BlockSpec(blk, lambda c: (c, col0 + 1)),
            pl.BlockSpec(blk, lambda c: (c, col0 + 2)),
            pl.BlockSpec((2, HGRN_WIDTH), lambda c: (0, 0)),
        ] + e_in,
        out_specs=[
            pl.BlockSpec(blk, lambda c: (c, 0)),
            pl.BlockSpec((cps, nh, hd, hd), lambda c: (c, 0, 0, 0)),
            pl.BlockSpec(blk, lambda c: (c, 0)),
        ] + e_out,
        out_shape=[
            jax.ShapeDtypeStruct((s, HGRN_WIDTH), F32),
            jax.ShapeDtypeStruct((n_chunks, nh, hd, hd), F32),
            jax.ShapeDtypeStruct((s, nh * hd), F32),
        ] + e_shape,
        scratch_shapes=[
            pltpu.VMEM((nh, hd, hd), F32),
            pltpu.VMEM(sblk, F32),
            pltpu.VMEM(sblk, F32),
            pltpu.VMEM(sblk, F32),
            pltpu.VMEM((cps, nh, c_len, hd), F32),
        ] + e_scr,
        compiler_params=_params(dimension_semantics=("arbitrary",)),
    )(proj, proj, proj, lb, *e_args)


def hgrn_bwd(proj, lb, d_o, states, a_mat, ride=None):
    s = proj.shape[0]
    c_len, nh, hd = HGRN_CHUNK, HGRN_HEADS, HGRN_HEAD_DIM
    n_chunks = s // c_len
    col0 = 0
    cps = HGRN_CHUNKS_PER_STEP
    n_steps = n_chunks // cps
    last = n_steps - 1

    def body(q_ref, f_ref, i_ref, lb_ref, do_ref, st_in_ref, a_in_ref, dq_ref, df_ref, di_ref, dlb_ref,
             dst_ref, b_ref, qf_ref, kf_ref, da_ref, dqi_ref, dki_ref):
        @pl.when(pl.program_id(0) == 0)
        def _():
            dst_ref[...] = jnp.zeros_like(dst_ref)
            dlb_ref[...] = jnp.zeros_like(dlb_ref)

        lbv = _lower_bound(lb_ref[...])
        for u in reversed(range(cps)):
            rs = slice(u * c_len, (u + 1) * c_len)
            b_u, qf_u, kf_u, da_u, dqi_u, dki_u = (b_ref.at[u], qf_ref.at[u], kf_ref.at[u], da_ref.at[u], dqi_ref.at[u],
                                                   dki_ref.at[u])
            q = q_ref[rs, :]
            sq, qf, sig, f, kf = _hgrn_gates(q, f_ref[rs, :], lbv)
            b = _tri_sum(_tril_bf16(c_len), jnp.log(f))
            b_u[...] = b
            qf_u[...] = qf
            kf_u[...] = kf
            b_last = b[c_len - 1:c_len, :]
            eb = jnp.exp(b)
            ebl = jnp.exp(b_last - b)
            qb = qf * eb
            kb2 = kf * ebl
            vf = i_ref[rs, :]
            d_o = do_ref[rs, :]
            qb_b, kb2_b, vf_b, do_b = qb.astype(BF16), kb2.astype(BF16), vf.astype(BF16), d_o.astype(BF16)
            tq = lax.broadcasted_iota(jnp.int32, (c_len, hd), 0)
            lane = lax.broadcasted_iota(jnp.int32, (c_len, hd), 1)

            dqb_parts, dvf_parts, dkb2_parts, dbl_parts = [], [], [], []
            for h in range(nh):
                hs = slice(h * hd, (h + 1) * hd)
                st = st_in_ref[u, h]
                dst = dst_ref[h]
                st_b, dst_b = st.astype(BF16), dst.astype(BF16)
                a_h = a_in_ref[rs, hs][:, :c_len].astype(BF16)
                dqb_parts.append(_dot(do_b[:, hs], st_b))
                dvf_parts.append(_dot_tn(a_h, do_b[:, hs]) + _dot_nt(kb2_b[:, hs], dst_b))
                dkb2_parts.append(_dot(vf_b[:, hs], dst_b))
                da = _dot_nt(do_b[:, hs], vf_b[:, hs])
                da = jnp.concatenate([da, jnp.zeros((c_len, hd - c_len), F32)], axis=1)
                da_u[h] = jnp.where(tq >= lane, da, 0.0)
                dbl_parts.append(jnp.sum(dst * st, axis=0, keepdims=True) * jnp.exp(b_last[:, hs]))
                dst_ref[h] = dst * jnp.exp(b_last[:, hs]) + _dot_tn(do_b[:, hs], qb_b[:, hs])
            dqb = jnp.concatenate(dqb_parts, axis=1)
            dvf = jnp.concatenate(dvf_parts, axis=1)
            dkb2 = jnp.concatenate(dkb2_parts, axis=1)
            dbl = jnp.concatenate(dbl_parts, axis=1) + jnp.sum(dkb2 * kb2, axis=0, keepdims=True)

            dqi_u[...] = jnp.zeros_like(dqi_u)
            t_idx = lax.broadcasted_iota(jnp.int32, (c_len, nh * hd), 0)

            def diagonal(rows, j, key, b_u=b_u, qf_u=qf_u, kf_u=kf_u, da_u=da_u, dqi_u=dqi_u, dki_u=dki_u):
                bj = b_u[pl.ds(key, 1), :]
                kj = kf_u[pl.ds(key, 1), :]
                nrow = rows.stop - rows.start
                t_loc = lax.broadcasted_iota(jnp.int32, (nrow, nh * hd), 0) + (rows.start % SUB_BLOCK)
                e = jnp.exp(jnp.where(t_loc >= j, b_u[rows, :] - bj, NEG_BIG))
                lane_r = lax.broadcasted_iota(jnp.int32, (nrow, hd), 1)
                cols = [jnp.sum(jnp.where(lane_r == key, da_u[h, rows, :], 0.0), axis=-1, keepdims=True)
                        for h in range(nh)]
                w = e * jnp.concatenate([jnp.broadcast_to(cc, (nrow, hd)) for cc in cols], axis=1)
                dqi_u[rows, :] += w * kj
                dki_u[pl.ds(key, 1), :] = jnp.sum(w * qf_u[rows, :], axis=0, keepdims=True)

            _hgrn_diagonal_loops(c_len, diagonal)

            q_til, k_til, eq, ek = _hgrn_off_diagonal(b, qf, kf)
            q_hi, q_lo = _split2(q_til)
            k_pairs = [_split2(k) for k in k_til]
            n_sub = c_len // SUB_BLOCK
            dq_heads, dk_heads = [], []
            for h in range(nh):
                hs = slice(h * hd, (h + 1) * hd)
                dq_rows = [jnp.zeros((SUB_BLOCK, hd), F32)]
                dk_h = jnp.zeros((c_len, hd), F32)
                for i in range(1, n_sub):
                    rows = slice(i * SUB_BLOCK, (i + 1) * SUB_BLOCK)
                    n = i * SUB_BLOCK
                    da_i = da_u[h, rows, :].astype(BF16)
                    k_hi, k_lo = k_pairs[i - 1]
                    dq_rows.append((_dot(da_i, k_hi[:, hs]) + _dot(da_i, k_lo[:, hs])) * eq[rows, hs])
                    dk_t = (_dot_tn(da_i, q_hi[rows, hs]) + _dot_tn(da_i, q_lo[rows, hs]))[:n, :] * ek[i - 1][:, hs]
                    dk_h = dk_h + jnp.concatenate([dk_t, jnp.zeros((c_len - n, hd), F32)], axis=0)
                dq_heads.append(jnp.concatenate(dq_rows, axis=0))
                dk_heads.append(dk_h)
            dq_intra = dqi_u[...] + jnp.concatenate(dq_heads, axis=1)
            dk_intra = dki_u[...] + jnp.concatenate(dk_heads, axis=1)

            db = dqb * qb + qf * dq_intra - kf * dk_intra - dkb2 * kb2
            db = db + jnp.where(t_idx == c_len - 1, dbl, 0.0)
            dg = _tri_sum(_tril_bf16(c_len, upper=True), db)
            dqf = dqb * eb + dq_intra
            dkf = dkb2 * ebl + dk_intra
            dq_ref[rs, :] = (dqf * (sq * (1.0 + q * (1.0 - sq)))).astype(BF16)
            dfv = dg / f - dkf
            df_ref[rs, :] = (dfv * (1.0 - lbv) * sig * (1.0 - sig)).astype(BF16)
            di_ref[rs, :] = dvf.astype(BF16)
            dlb_ref[...] += jnp.sum(dfv * (1.0 - sig), axis=0, keepdims=True)

    blk = (cps * c_len, HGRN_WIDTH)
    sblk = (cps, c_len, HGRN_WIDTH)
    rev = lambda c: last - c
    step = lambda k: (lambda: pl.program_id(0) == k)
    e_in, e_out, e_shape, e_scr, e_args = _ride_specs(ride)
    return pl.pallas_call(
        _riding(body, 7, 4, 7, ride, step(0), step(n_steps // 2), step(last)),
        name="hgrn_bwd",
        grid=(n_steps,),
        in_specs=[
            pl.BlockSpec(blk, lambda c: (rev(c), col0)),
            pl.BlockSpec(blk, lambda c: (rev(c), col0 + 1)),
            pl.BlockSpec(blk, lambda c: (rev(c), col0 + 2)),
            pl.BlockSpec((2, HGRN_WIDTH), lambda c: (0, 0)),
            pl.BlockSpec(blk, lambda c: (rev(c), 0)),
            pl.BlockSpec((cps, nh, hd, hd), lambda c: (rev(c), 0, 0, 0)),
            pl.BlockSpec(blk, lambda c: (rev(c), 0)),
        ] + e_in,
        out_specs=[
            pl.BlockSpec(blk, lambda c: (rev(c), 0)),
            pl.BlockSpec(blk, lambda c: (rev(c), 0)),
            pl.BlockSpec(blk, lambda c: (rev(c), 0)),
            pl.BlockSpec((1, HGRN_WIDTH), lambda c: (0, 0)),
        ] + e_out,
        out_shape=[jax.ShapeDtypeStruct((s, HGRN_WIDTH), BF16)] * 3 + [jax.ShapeDtypeStruct((1, HGRN_WIDTH), F32)] + e_shape,
        scratch_shapes=[
            pltpu.VMEM((nh, hd, hd), F32),
            pltpu.VMEM(sblk, F32),
            pltpu.VMEM(sblk, F32),
            pltpu.VMEM(sblk, F32),
            pltpu.VMEM((cps, nh, c_len, hd), F32),
            pltpu.VMEM(sblk, F32),
            pltpu.VMEM(sblk, F32),
        ] + e_scr,
        compiler_params=_params(dimension_semantics=("arbitrary",)),
    )(proj, proj, proj, lb, d_o, states, a_mat, *e_args)


def _row_spec(tm, width, col=0):
    return pl.BlockSpec((tm, width), lambda i: (i, col))


def _const_spec(width):
    return pl.BlockSpec((1, width), lambda i: (0, 0))


def _acc_rows(ref, value):
    @pl.when(pl.program_id(0) == 0)
    def _():
        ref[...] = jnp.zeros_like(ref)

    ref[...] += jnp.sum(value, axis=0, keepdims=True)


def mix_fwd(attn_parts, o_h, proj, an, hn, w_out_b, gp, x, ride=None):
    s = x.shape[0]
    tm = TOKEN_TILE
    gate_col = 3
    hd = HGRN_HEAD_DIM
    nd = len(DILATIONS)

    def body(*refs):
        o_refs, l_refs = refs[:nd], refs[nd:2 * nd]
        oh_ref, gate_ref, an_ref, hn_ref, w_ref, gp_ref, x_ref = refs[2 * nd:2 * nd + 7]
        x1_ref, cat_ref, mixed_ref, attn_ref = refs[2 * nd + 7:2 * nd + 11]
        lse_refs = refs[2 * nd + 11:3 * nd + 11]
        o_scr, l_scr, lse_scr = refs[3 * nd + 11:]
        os_ = [_from_dilated(r, o_scr.at[k], d, tm) for k, (r, d) in enumerate(zip(o_refs, DILATIONS))]
        ls = [_from_dilated(r, l_scr.at[k], d, tm) for k, (r, d) in enumerate(zip(l_refs, DILATIONS))]
        m = jnp.maximum(jnp.maximum(ls[0], ls[1]), ls[2])
        es = [jnp.exp(l - m) for l in ls]
        den = es[0] + es[1] + es[2]
        attn = (es[0] * os_[0] + es[1] * os_[1] + es[2] * os_[2]) / den
        attn_ref[...] = attn
        _lane_blocks(lse_scr, m + jnp.log(den))
        for d, ref in zip(DILATIONS, lse_refs):
            _to_dilated(lse_scr, ref, d, tm)
        cat_ref[:, :ATTN_WIDTH] = _rms_fwd(attn, an_ref[...], ATTN_WIDTH).astype(BF16)
        gate = gate_ref[...]
        silu_g = gate * _sigmoid(gate)
        for h in range(HGRN_HEADS):
            hs = slice(h * hd, (h + 1) * hd)
            rec = _rms_fwd(oh_ref[:, hs], hn_ref[:, hs], hd) * silu_g[:, hs]
            cat_ref[:, ATTN_WIDTH + h * hd:ATTN_WIDTH + (h + 1) * hd] = rec.astype(BF16)
        mixed = _dot(cat_ref[...], w_ref[...])
        mixed_ref[...] = mixed
        x1_ref[...] = x_ref[...] + _rms_fwd(mixed, gp_ref[...], D_MODEL)

    aw = ATTN_WIDTH
    n_steps = s // tm
    step = lambda k: (lambda: pl.program_id(0) == k)
    e_in, e_out, e_shape, e_scr, e_args = _ride_specs(ride)
    return pl.pallas_call(
        _riding(body, 2 * nd + 7, 4 + nd, 3, ride, step(0), step((13 * n_steps) // 16), step(n_steps - 1)),
        name="mix_fwd",
        grid=(n_steps,),
        in_specs=[_dilated_spec(d, tm, aw) for d in DILATIONS] * 2 + [
            _row_spec(tm, aw), _row_spec(tm, aw, gate_col), _const_spec(aw), _const_spec(aw), _vmem_spec(),
            _const_spec(D_MODEL), _row_spec(tm, D_MODEL)] + e_in,
        out_specs=[_row_spec(tm, D_MODEL), _row_spec(tm, D_MODEL), _row_spec(tm, D_MODEL), _row_spec(tm, aw)] + [
            _dilated_spec(d, tm, aw) for d in DILATIONS] + e_out,
        out_shape=[
            jax.ShapeDtypeStruct((s, D_MODEL), F32),
            jax.ShapeDtypeStruct((s, D_MODEL), BF16),
            jax.ShapeDtypeStruct((s, D_MODEL), F32),
            jax.ShapeDtypeStruct((s, aw), F32),
        ] + [jax.ShapeDtypeStruct((d, s // d, aw), F32) for d in DILATIONS] + e_shape,
        scratch_shapes=[pltpu.VMEM((nd, aw // LANES, tm, LANES), F32), pltpu.VMEM((nd, aw // LANES, tm, LANES), F32),
                        pltpu.VMEM((aw // LANES, tm, LANES), F32)] + e_scr,
        compiler_params=_params(dimension_semantics=("arbitrary",)),
    )(*[p[0] for p in attn_parts], *[p[1] for p in attn_parts], o_h, proj, an, hn, w_out_b, gp, x, *e_args)


def mix_bwd(dx1, mixed, gp, w_out_b, attn, an, o_h, proj, hn):
    s = dx1.shape[0]
    tm = TOKEN_TILE
    gate_col = 3
    hd = HGRN_HEAD_DIM
    aw = ATTN_WIDTH

    nd = len(DILATIONS)

    def body(*refs):
        dx1_ref, mixed_ref, gp_ref, w_ref, attn_ref, an_ref, oh_ref, gate_ref, hn_ref, dmix_ref = refs[:10]
        do_refs, delta_refs = refs[10:10 + nd], refs[10 + nd:10 + 2 * nd]
        doh_ref, dgate_ref, dgp_ref, dan_ref, dhn_ref, do_ref, delta_ref = refs[10 + 2 * nd:]
        dmixed, gp_c = _rms_bwd(dx1_ref[...], mixed_ref[...], gp_ref[...], D_MODEL)
        _acc_rows(dgp_ref, gp_c)
        dmixed_b = dmixed.astype(BF16)
        dmix_ref[...] = dmixed_b
        dcat = _dot_nt(dmixed_b, w_ref[...])
        attn = attn_ref[...]
        d_o, an_c = _rms_bwd(dcat[:, :aw], attn, an_ref[...], aw)
        _acc_rows(dan_ref, an_c)
        _lane_blocks(do_ref, d_o)
        prod = d_o * attn
        for pair in range(ATTN_HEADS // 2):
            pp = prod[:, pair * LANES:(pair + 1) * LANES]
            low = _lane_half((tm, LANES), 0)
            lo = jnp.sum(jnp.where(low, pp, 0.0), axis=-1, keepdims=True)
            hi = jnp.sum(jnp.where(low, 0.0, pp), axis=-1, keepdims=True)
            delta_ref[pair] = jnp.where(low, lo, hi)
        for d, o_ref, l_ref in zip(DILATIONS, do_refs, delta_refs):
            _to_dilated(do_ref, o_ref, d, tm, cast=BF16)
            _to_dilated(delta_ref, l_ref, d, tm)
        gate = gate_ref[...]
        sg = _sigmoid(gate)
        silu_g = gate * sg
        drec = dcat[:, aw:]
        hn_parts = []
        for h in range(HGRN_HEADS):
            hs = slice(h * hd, (h + 1) * hd)
            oh = oh_ref[:, hs]
            on = _rms_fwd(oh, hn_ref[:, hs], hd)
            dgate_ref[:, hs] = (drec[:, hs] * on * (sg[:, hs] * (1.0 + gate[:, hs] * (1.0 - sg[:, hs])))).astype(BF16)
            d_oh, hn_c = _rms_bwd(drec[:, hs] * silu_g[:, hs], oh, hn_ref[:, hs], hd)
            doh_ref[:, hs] = d_oh
            hn_parts.append(hn_c)
        _acc_rows(dhn_ref, jnp.concatenate(hn_parts, axis=1))

    return pl.pallas_call(
        body,
        name="mix_bwd",
        grid=(s // tm,),
        in_specs=[_row_spec(tm, D_MODEL), _row_spec(tm, D_MODEL), _const_spec(D_MODEL), _vmem_spec(), _row_spec(tm, aw),
                  _const_spec(aw), _row_spec(tm, aw), _row_spec(tm, aw, gate_col), _const_spec(aw)],
        out_specs=[_row_spec(tm, D_MODEL)] + [_dilated_spec(d, tm, aw) for d in DILATIONS] * 2 + [_row_spec(tm, aw)] * 2 + [
            _const_spec(D_MODEL), _const_spec(aw), _const_spec(aw)],
        out_shape=[jax.ShapeDtypeStruct((s, D_MODEL), BF16)] + [
            jax.ShapeDtypeStruct((d, s // d, aw), BF16) for d in DILATIONS] + [
            jax.ShapeDtypeStruct((d, s // d, aw), F32) for d in DILATIONS] + [
            jax.ShapeDtypeStruct((s, aw), F32), jax.ShapeDtypeStruct((s, aw), BF16),
            jax.ShapeDtypeStruct((1, D_MODEL), F32), jax.ShapeDtypeStruct((1, aw), F32),
            jax.ShapeDtypeStruct((1, aw), F32)],
        scratch_shapes=[pltpu.VMEM((aw // LANES, tm, LANES), F32), pltpu.VMEM((aw // LANES, tm, LANES), F32)],
        compiler_params=_params(dimension_semantics=("arbitrary",)),
    )(dx1, mixed, gp, w_out_b, attn, an, o_h, proj, hn)


def mlp_fwd_bwd(x1, g_pre, w1_blocks, w2_b, g_post, target):
    s = x1.shape[0]
    tm = MLP_TILE
    nblk, _, fb = w1_blocks.shape

    def body(x1_ref, gpre_ref, w1_ref, w2_ref, gpost_ref, t_ref,
             dx1_ref, h2_ref, a_ref, du_ref, dff_ref, loss_ref, dgpre_ref, dgpost_ref, u_ref):
        x1v = x1_ref[...]
        h2 = _rms_fwd(x1v, gpre_ref[...], D_MODEL).astype(BF16)
        h2_ref[...] = h2
        ff = jnp.zeros((tm, D_MODEL), F32)
        for j in range(nblk):
            cols = slice(j * fb, (j + 1) * fb)
            ru = jnp.maximum(_dot(h2, w1_ref[j]), 0.0)
            u_ref[:, cols] = ru.astype(BF16)
            a = (ru * ru).astype(BF16)
            a_ref[:, cols] = a
            ff = ff + _dot(a, w2_ref[cols, :])
        diff = x1v + _rms_fwd(ff, gpost_ref[...], D_MODEL) - t_ref[...]
        _acc_rows(loss_ref, diff * diff)
        dy = diff * (1.0 / D_MODEL)
        dff, gpost_c = _rms_bwd(dy, ff, gpost_ref[...], D_MODEL)
        _acc_rows(dgpost_ref, gpost_c)
        dff_b = dff.astype(BF16)
        dff_ref[...] = dff_b
        dh2 = jnp.zeros((tm, D_MODEL), F32)
        for j in range(nblk):
            cols = slice(j * fb, (j + 1) * fb)
            du = (_dot_nt(dff_b, w2_ref[cols, :]) * (2.0 * u_ref[:, cols])).astype(BF16)
            du_ref[:, cols] = du
            dh2 = dh2 + _dot_nt(du, w1_ref[j])
        dxa, gpre_c = _rms_bwd(dh2, x1v, gpre_ref[...], D_MODEL)
        _acc_rows(dgpre_ref, gpre_c)
        dx1_ref[...] = dy + dxa

    dm = D_MODEL
    return pl.pallas_call(
        body,
        name="mlp_fwd_bwd",
        grid=(s // tm,),
        in_specs=[_row_spec(tm, dm), _const_spec(dm), _vmem_spec(), _vmem_spec(), _const_spec(dm), _row_spec(tm, dm)],
        out_specs=[_row_spec(tm, dm), _row_spec(tm, dm), _row_spec(tm, D_FF), _row_spec(tm, D_FF), _row_spec(tm, dm),
                   _const_spec(dm), _const_spec(dm), _const_spec(dm)],
        out_shape=[
            jax.ShapeDtypeStruct((s, dm), F32),
            jax.ShapeDtypeStruct((s, dm), BF16),
            jax.ShapeDtypeStruct((s, D_FF), BF16),
            jax.ShapeDtypeStruct((s, D_FF), BF16),
            jax.ShapeDtypeStruct((s, dm), BF16),
            jax.ShapeDtypeStruct((1, dm), F32),
            jax.ShapeDtypeStruct((1, dm), F32),
            jax.ShapeDtypeStruct((1, dm), F32),
        ],
        scratch_shapes=[pltpu.VMEM((tm, D_FF), BF16)],
        compiler_params=_params(dimension_semantics=("arbitrary",)),
    )(x1, g_pre, w1_blocks, w2_b, g_post, target)


def in_proj_bwd(attn_grads, hgrn_grads, dgate, w_in_b, x, g1, dx1):
    s = x.shape[0]
    tm = PROJ_TILE
    aw = ATTN_WIDTH
    n_attn = len(attn_grads)
    flat = [g[k] for k in range(3) for g in attn_grads] + list(hgrn_grads) + [dgate]

    def body(*refs):
        parts = refs[:len(flat)]
        w_ref, x_ref, g_ref, dx1_ref, dx_ref, dproj_ref, dg_ref, scr = refs[len(flat):]
        groups = []
        for k in range(3):
            acc = None
            for p, d in zip(parts[k * n_attn:(k + 1) * n_attn], DILATIONS):
                v = _from_dilated(p, scr, d, tm)
                acc = v if acc is None else acc + v
            groups.append(acc)
        groups += [p[...] for p in parts[3 * n_attn:]]
        dh = jnp.zeros((tm, D_MODEL), F32)
        for gi, grp in enumerate(groups):
            cols = slice(gi * aw, (gi + 1) * aw)
            gb = grp.astype(BF16)
            dproj_ref[:, cols] = gb
            dh = dh + _dot_nt(gb, w_ref[:, cols])
        dxa, g_c = _rms_bwd(dh, x_ref[...], g_ref[...], D_MODEL)
        _acc_rows(dg_ref, g_c)
        dx_ref[...] = dx1_ref[...] + dxa

    dm = D_MODEL
    return pl.pallas_call(
        body,
        name="in_proj_bwd",
        grid=(s // tm,),
        in_specs=[_dilated_spec(d, tm, aw) for d in DILATIONS] * 3 + [_row_spec(tm, aw)] * 4 + [
            _vmem_spec(), _row_spec(tm, dm), _const_spec(dm), _row_spec(tm, dm)],
        out_specs=[_row_spec(tm, dm), _row_spec(tm, IN_PROJ_WIDTH), _const_spec(dm)],
        out_shape=[jax.ShapeDtypeStruct((s, dm), F32), jax.ShapeDtypeStruct((s, IN_PROJ_WIDTH), BF16),
                   jax.ShapeDtypeStruct((1, dm), F32)],
        scratch_shapes=[pltpu.VMEM((aw // LANES, tm, LANES), F32)],
        compiler_params=_params(dimension_semantics=("arbitrary",)),
    )(*flat, w_in_b, x, g1, dx1)


def wgrad(a_b, b_b, tn, name, ts=1024, per_step=1):
    s, k = a_b.shape
    n = b_b.shape[1]

    def body(a_ref, b_ref, o_ref):
        @pl.when(pl.program_id(1) == 0)
        def _():
            o_ref[...] = jnp.zeros_like(o_ref)

        a = a_ref[...]
        for jj in range(per_step):
            o_ref[jj] += _dot_tn(a, b_ref[:, jj * tn:(jj + 1) * tn])

    wide = tn * per_step
    return pl.pallas_call(
        body,
        name=name,
        grid=(n // wide, s // ts),
        in_specs=[pl.BlockSpec((ts, k), lambda j, i: (i, 0)), pl.BlockSpec((ts, wide), lambda j, i: (i, j))],
        out_specs=pl.BlockSpec((per_step, k, tn), lambda j, i: (j, 0, 0)),
        out_shape=jax.ShapeDtypeStruct((n // tn, k, tn), F32),
        compiler_params=_params(dimension_semantics=("arbitrary", "arbitrary")),
    )(a_b, b_b)


def train_step(x, target, g1, an, logits, hn, gp, g_pre, g_post, w, m, v):
    nd = len(DILATIONS)
    shard_b = {k: w[k].astype(BF16) for k in BIG}
    (w_in_g,) = run_exchange(gather_exchange([shard_b["w_in"]]), "gather_w_in")
    w_in_b = w_in_g.transpose(1, 0, 2).reshape(D_MODEL, IN_PROJ_WIDTH)

    proj, h_b, *qkvs = in_proj_fwd(x, g1, w_in_b)
    attn_parts = [attn_fwd(qkv, d) for qkv, d in zip(qkvs, DILATIONS)]
    o_h, states, a_mat, w_out_g, w1_blocks = hgrn_fwd(
        proj, logits, ride=gather_exchange([shard_b["w_out"], shard_b["w_ff1"]]))
    w_out_b = w_out_g.reshape(D_MODEL, D_MODEL)
    x1, cat_b, mixed, attn, *lses, w2_g = mix_fwd(attn_parts, o_h, proj, an, hn, w_out_b, gp, x,
                                                  ride=gather_exchange([shard_b["w_ff2"]]))
    w2_b = w2_g.reshape(D_FF, D_MODEL)
    dx1, h2_b, a_b, du_b, dff_b, loss_vec, dg_pre, dg_post = mlp_fwd_bwd(x1, g_pre, w1_blocks, w2_b, g_post, target)
    dw2 = wgrad(a_b, dff_b, D_MODEL, "wgrad_ff2", ts=512)
    dw1 = wgrad(h2_b, du_b, D_FF // N_DEV, "wgrad_ff1", per_step=2)
    dmix_b, *rest = mix_bwd(dx1, mixed, gp, w_out_b, attn, an, o_h, proj, hn)
    d_os, deltas = rest[:nd], rest[nd:2 * nd]
    d_oh, dgate, dgp, dan, dhn = rest[2 * nd:]
    dwout = wgrad(cat_b, dmix_b, D_MODEL, "wgrad_out")

    early = ("w_out", "w_ff1", "w_ff2")
    early_grads = [dwout.reshape(N_DEV, D_MODEL // N_DEV, D_MODEL), dw1, dw2.reshape(N_DEV, D_FF // N_DEV, D_MODEL)]
    res = attn_bwd(qkvs[0], d_os[0], lses[0], deltas[0], DILATIONS[0], ride=to_core_exchange(early_grads))
    pairs = [pair_sum(g, s, f"pair_sum_{name}") for g, s, name in zip(early_grads, res[3:], early)]
    attn_grads = [res[:3]]
    *res, others_ff2 = attn_bwd(qkvs[1], d_os[1], lses[1], deltas[1], DILATIONS[1],
                                ride=to_chip_exchange([pairs[2][1]]))
    attn_grads.append(res)
    attn_grads.append(attn_bwd(qkvs[2], d_os[2], lses[2], deltas[2], DILATIONS[2]))
    dq_h, df_h, di_h, dlb, *others = hgrn_bwd(proj, logits, d_oh, states, a_mat,
                                              ride=to_chip_exchange([pairs[0][1], pairs[1][1]]))
    others.append(others_ff2)
    dx, dproj_b, dg1 = in_proj_bwd(attn_grads, (dq_h, df_h, di_h), dgate, w_in_b, x, g1, dx1)
    dwin = wgrad(h_b, dproj_b, 2 * IN_PROJ_WIDTH // N_DEV, "wgrad_in")
    big = {name: sum_adamw(p[0], o, w[name], m[name], v[name], f"sum_adamw_{name}")
           for name, p, o in zip(early, pairs, others)}

    shard_w = IN_PROJ_WIDTH // N_DEV
    dwin_blocks = dwin.reshape(N_DEV // 2, D_MODEL, 2, shard_w).transpose(0, 2, 1, 3).reshape(N_DEV, D_MODEL, shard_w)
    (from_sibling,) = run_exchange(to_core_exchange([dwin_blocks]), "reduce_w_in_to_core")
    pair_in, pair_in_b = pair_sum(dwin_blocks, from_sibling, "pair_sum_w_in")
    (others_in,) = run_exchange(to_chip_exchange([pair_in_b]), "reduce_w_in_to_chip")
    big["w_in"] = sum_adamw(pair_in, others_in, w["w_in"], m["w_in"], v["w_in"], "sum_adamw_w_in")
    small = dict(dg1=dg1, dan=dan, dlb=dlb, dhn=dhn, dgp=dgp, dg_pre=dg_pre, dg_post=dg_post, loss_vec=loss_vec)
    return dx, big, small


def _position():
    x, y, c = lax.axis_index("x"), lax.axis_index("y"), lax.axis_index("c")
    other_chips = [(1 - x, y), (x, 1 - y), (1 - x, 1 - y)]
    return x, y, c, other_chips


def _any_spec():
    return pl.BlockSpec(memory_space=pl.ANY)


class Exchange:
    def __init__(self, arrays, out_shape, sems, stages):
        self.arrays, self.out_shape, self.sems, self.stages = list(arrays), list(out_shape), list(sems), stages


def gather_exchange(shards):
    n = len(shards)

    def stages(ins, outs, sems):
        send_sems, recv_sems, local_sems = sems

        def parts():
            x, y, c, chips = _position()
            me, sibling = (x, y, c), (x, y, 1 - c)

            def slot(a, px, py, pc):
                return outs[a].at[4 * px + 2 * py + pc]

            def copy(a, k, block, to, src=None):
                return pltpu.make_async_remote_copy(
                    src_ref=slot(a, *block) if src is None else src, dst_ref=slot(a, *block),
                    send_sem=send_sems.at[a, k], recv_sem=recv_sems.at[a, k], device_id=to, device_id_type=MESH)

            local = [pltpu.make_async_copy(ins[a], slot(a, *me), local_sems.at[a]) for a in range(n)]
            first = []
            for a in range(n):
                first.append(copy(a, 0, me, sibling, src=ins[a]))
                first += [copy(a, 1 + j, me, (*chip, c), src=ins[a]) for j, chip in enumerate(chips)]
            passed = [copy(a, 4 + j, (*chip, c), sibling) for j, chip in enumerate(chips) for a in range(n)]
            return c, chips, me, sibling, copy, local, first, passed

        def begin():
            _, _, _, _, _, local, first, _ = parts()
            for cp in local + first:
                cp.start()

        def middle():
            c, chips, me, _, copy, _, _, passed = parts()
            k = 0
            for j, chip in enumerate(chips):
                for a in range(n):
                    copy(a, 1 + j, (*chip, c), me).wait_recv()
                    passed[k].start()
                    k += 1

        def end():
            c, chips, me, sibling, copy, local, first, passed = parts()
            for a in range(n):
                copy(a, 0, sibling, me).wait_recv()
                for j, chip in enumerate(chips):
                    copy(a, 4 + j, (*chip, 1 - c), me).wait_recv()
            for cp in first + passed:
                cp.wait_send()
            for cp in local:
                cp.wait()

        return begin, middle, end

    return Exchange(
        shards, [jax.ShapeDtypeStruct((N_DEV,) + sh.shape, sh.dtype) for sh in shards],
        [pltpu.SemaphoreType.DMA((n, 7)), pltpu.SemaphoreType.DMA((n, 7)), pltpu.SemaphoreType.DMA((n,))], stages)


def to_core_exchange(grads):
    n = len(grads)

    def stages(ins, outs, sems):
        send_sems, recv_sems = sems

        def copies():
            x, y, c, _ = _position()
            return [pltpu.make_async_remote_copy(
                src_ref=ins[a].at[2 * q + (1 - c)], dst_ref=outs[a].at[q], send_sem=send_sems.at[a, q],
                recv_sem=recv_sems.at[a, q], device_id=(x, y, 1 - c), device_id_type=MESH)
                for a in range(n) for q in range(4)]

        def begin():
            for cp in copies():
                cp.start()

        def end():
            for cp in copies():
                cp.wait()

        return begin, None, end

    return Exchange(grads, [jax.ShapeDtypeStruct((4,) + g.shape[1:], g.dtype) for g in grads],
                    [pltpu.SemaphoreType.DMA((n, 4)), pltpu.SemaphoreType.DMA((n, 4))], stages)


def pair_sum(grad, from_sibling, name):
    _, r, cdim = grad.shape
    tr = min(r, ELEMENTWISE_ROWS)
    c_idx = lax.axis_index("c").astype(jnp.int32).reshape(1)

    def body(c_ref, g_ref, s_ref, o_ref, ob_ref):
        total = g_ref[...] + s_ref[...]
        o_ref[...] = total
        ob_ref[...] = total.astype(BF16)

    blk = lambda: pl.BlockSpec((1, tr, cdim), lambda q, i, cr: (q, i, 0))
    return pl.pallas_call(
        body,
        name=name,
        grid_spec=pltpu.PrefetchScalarGridSpec(
            num_scalar_prefetch=1,
            grid=(4, r // tr),
            in_specs=[pl.BlockSpec((1, tr, cdim), lambda q, i, cr: (2 * q + cr[0], i, 0)), blk()],
            out_specs=[blk(), blk()],
        ),
        out_shape=[jax.ShapeDtypeStruct((4, r, cdim), F32), jax.ShapeDtypeStruct((4, r, cdim), BF16)],
        compiler_params=_params(dimension_semantics=("arbitrary", "arbitrary")),
    )(c_idx, grad, from_sibling)


def to_chip_exchange(pairs):
    n = len(pairs)

    def stages(ins, outs, sems):
        send_sems, recv_sems = sems

        def copies():
            x, y, c, chips = _position()
            return [pltpu.make_async_remote_copy(
                src_ref=ins[a].at[2 * px + py], dst_ref=outs[a].at[j], send_sem=send_sems.at[a, j],
                recv_sem=recv_sems.at[a, j], device_id=(px, py, c), device_id_type=MESH)
                for a in range(n) for j, (px, py) in enumerate(chips)]

        def begin():
            for cp in copies():
                cp.start()

        def end():
            for cp in copies():
                cp.wait()

        return begin, None, end

    return Exchange(pairs, [jax.ShapeDtypeStruct((3,) + p.shape[1:], p.dtype) for p in pairs],
                    [pltpu.SemaphoreType.DMA((n, 3)), pltpu.SemaphoreType.DMA((n, 3))], stages)


def run_exchange(ex, name):
    n_in, n_out = len(ex.arrays), len(ex.out_shape)

    def body(*refs):
        begin, middle, end = ex.stages(refs[:n_in], refs[n_in:n_in + n_out], refs[n_in + n_out:])
        begin()
        if middle is not None:
            middle()
        end()

    return pl.pallas_call(
        body,
        name=name,
        in_specs=[_any_spec()] * n_in,
        out_specs=[_any_spec()] * n_out,
        out_shape=ex.out_shape,
        scratch_shapes=ex.sems,
    )(*ex.arrays)


def _riding(body, n_in, n_out, n_scratch, ex, first, middle, last):
    if ex is None:
        return body
    r_in, r_out = len(ex.arrays), len(ex.out_shape)

    def wrapped(*refs):
        k_in, refs = refs[:n_in], refs[n_in:]
        e_in, refs = refs[:r_in], refs[r_in:]
        k_out, refs = refs[:n_out], refs[n_out:]
        e_out, refs = refs[:r_out], refs[r_out:]
        k_scr, e_sems = refs[:n_scratch], refs[n_scratch:]
        begin, mid, end = ex.stages(e_in, e_out, e_sems)
        pl.when(first())(begin)
        body(*k_in, *k_out, *k_scr)
        if mid is not None:
            pl.when(middle())(mid)
        pl.when(last())(end)

    return wrapped


def _ride_specs(ex):
    if ex is None:
        return [], [], [], [], []
    return [_any_spec()] * len(ex.arrays), [_any_spec()] * len(ex.out_shape), ex.out_shape, ex.sems, ex.arrays


def _adamw(w, g, m, v):
    m = ADAM_B1 * m + (1.0 - ADAM_B1) * g
    v = ADAM_B2 * v + (1.0 - ADAM_B2) * (g * g)
    m_hat = m / (1.0 - ADAM_B1 ** ADAM_STEP)
    v_hat = v / (1.0 - ADAM_B2 ** ADAM_STEP)
    delta = -ADAM_LR * (m_hat / (jnp.sqrt(v_hat) + ADAM_EPS) + ADAM_WD * w)
    return delta, m, v


def sum_adamw(pairs, others, w, m, v, name):
    r, cdim = w.shape
    tr = min(r, ELEMENTWISE_ROWS // 2)
    chip_idx =(2 * lax.axis_index("x") + lax.axis_index("y")).astype(jnp.int32).reshape(1)

    def body(q_ref, p_ref, o_ref, w_ref, m_ref, v_ref, g_out, d_out, m_out, v_out):
        g = p_ref[0] + o_ref[0].astype(F32) + o_ref[1].astype(F32) + o_ref[2].astype(F32)
        g_out[...] = g
        d_out[...], m_out[...], v_out[...] = _adamw(w_ref[...], g, m_ref[...], v_ref[...])

    tile = lambda: pl.BlockSpec((tr, cdim), lambda i, qr: (i, 0))
    return pl.pallas_call(
        body,
        name=name,
        grid_spec=pltpu.PrefetchScalarGridSpec(
            num_scalar_prefetch=1,
            grid=(r // tr,),
            in_specs=[pl.BlockSpec((1, tr, cdim), lambda i, qr: (qr[0], i, 0)),
                      pl.BlockSpec((3, tr, cdim), lambda i, qr: (0, i, 0)), tile(), tile(), tile()],
            out_specs=[tile(), tile(), tile(), tile()],
        ),
        out_shape=[jax.ShapeDtypeStruct((r, cdim), F32)] * 4,
        compiler_params=_params(dimension_semantics=("arbitrary",)),
    )(chip_idx, pairs, others, w, m, v)


SMALL_ROWS = 8


def small_all_reduce(packed):
    shape = packed.shape

    def body(in_ref, out_ref, recv_ref, send_sems, recv_sems):
        x, y, c, _ = _position()
        my_id = 4 * x + 2 * y + c
        recv_ref[my_id] = in_ref[...]
        copies = []
        for rel in range(1, N_DEV):
            fx, fy, fc = (rel >> 2) & 1, (rel >> 1) & 1, rel & 1
            px = 1 - x if fx else x
            py = 1 - y if fy else y
            pc = 1 - c if fc else c
            cp = pltpu.make_async_remote_copy(
                src_ref=in_ref, dst_ref=recv_ref.at[my_id], send_sem=send_sems.at[rel - 1],
                recv_sem=recv_sems.at[rel - 1], device_id=(px, py, pc), device_id_type=MESH)
            cp.start()
            copies.append((cp, pltpu.make_async_remote_copy(
                src_ref=in_ref, dst_ref=recv_ref.at[4 * px + 2 * py + pc], send_sem=send_sems.at[rel - 1],
                recv_sem=recv_sems.at[rel - 1], device_id=(px, py, pc), device_id_type=MESH)))
        for cp, landing in copies:
            landing.wait_recv()
        for cp, landing in copies:
            cp.wait_send()
        total = recv_ref[0]
        for k in range(1, N_DEV):
            total = total + recv_ref[k]
        out_ref[...] = total

    return pl.pallas_call(
        body,
        name="small_all_reduce",
        in_specs=[_vmem_spec()],
        out_specs=_vmem_spec(),
        out_shape=jax.ShapeDtypeStruct(shape, F32),
        scratch_shapes=[pltpu.VMEM((N_DEV,) + shape, F32), pltpu.SemaphoreType.DMA((N_DEV - 1,)),
                        pltpu.SemaphoreType.DMA((N_DEV - 1,))],
    )(packed)


def small_adamw(reduced, w, m, v):
    def body(r_ref, w_ref, m_ref, v_ref, g_out, d_out, m_out, v_out, loss_out):
        red = r_ref[...]
        wv = w_ref[...]
        lb = _lower_bound(jnp.concatenate([wv[5:6, :HGRN_WIDTH], wv[5:6, HGRN_WIDTH:]], axis=0))
        t = red[5:6, :HGRN_WIDTH] * lb * (1.0 - lb)
        row = lax.broadcasted_iota(jnp.int32, red.shape, 0)
        g = jnp.where(row == 5, jnp.concatenate([t, -t], axis=1), jnp.where(row >= 6, 0.0, red))
        g_out[...] = g
        d_out[...], m_out[...], v_out[...] = _adamw(wv, g, m_ref[...], v_ref[...])
        loss = jnp.sum(red[6:7, :], axis=-1, keepdims=True) * (0.5 / D_MODEL)
        loss_out[...] = jnp.broadcast_to(loss, loss_out.shape)

    return pl.pallas_call(
        body,
        name="small_adamw",
        in_specs=[_vmem_spec()] * 4,
        out_specs=[_vmem_spec()] * 5,
        out_shape=[jax.ShapeDtypeStruct(reduced.shape, F32)] * 4 + [jax.ShapeDtypeStruct((8, 128), F32)],
    )(reduced, w, m, v)


def _pack_small(g1, gp, g_pre, g_post, an, hn, logits_or_dlb, extra=None):
    row5 = logits_or_dlb.reshape(1, -1)
    row5 = jnp.pad(row5, ((0, 0), (0, D_MODEL - row5.shape[1])))
    row6 = jnp.zeros((1, D_MODEL), F32) if extra is None else extra
    return jnp.concatenate([g1, gp, g_pre, g_post, jnp.concatenate([an, hn], axis=1), row5, row6,
                            jnp.zeros((1, D_MODEL), F32)], axis=0)


def _unpack_small(p):
    return dict(mix_pre_norm=p[0:1], mix_post_norm=p[1:2], mlp_pre_norm=p[2:3], mlp_post_norm=p[3:4],
                attn_out_norm=p[4:5, :ATTN_WIDTH], hgrn_out_norm=p[4:5, ATTN_WIDTH:],
                hgrn_lb_logits=p[5].reshape(2, HGRN_WIDTH))


BIG = ("w_in", "w_out", "w_ff1", "w_ff2")
ORDER = ("mix_pre_norm", "w_in", "attn_out_norm", "hgrn_lb_logits", "hgrn_out_norm", "w_out", "mix_post_norm",
         "mlp_pre_norm", "w_ff1", "w_ff2", "mlp_post_norm")


def kernel(x, mix_pre_norm, w_in, attn_out_norm, hgrn_lb_logits, hgrn_out_norm, w_out, mix_post_norm, mlp_pre_norm, w_ff1, w_ff2, mlp_post_norm, loss_target, m_mix_pre_norm, m_w_in, m_attn_out_norm, m_hgrn_lb_logits, m_hgrn_out_norm, m_w_out, m_mix_post_norm, m_mlp_pre_norm, m_w_ff1, m_w_ff2, m_mlp_post_norm, v_mix_pre_norm, v_w_in, v_attn_out_norm, v_hgrn_lb_logits, v_hgrn_out_norm, v_w_out, v_mix_post_norm, v_mlp_pre_norm, v_w_ff1, v_w_ff2, v_mlp_post_norm):
    w = dict(w_in=w_in[0], w_out=w_out[0], w_ff1=w_ff1[0], w_ff2=w_ff2[0])
    m = dict(w_in=m_w_in[0], w_out=m_w_out[0], w_ff1=m_w_ff1[0], w_ff2=m_w_ff2[0])
    v = dict(w_in=v_w_in[0], w_out=v_w_out[0], w_ff1=v_w_ff1[0], w_ff2=v_w_ff2[0])

    dx, big, small = train_step(x[0], loss_target[0], mix_pre_norm, attn_out_norm, hgrn_lb_logits, hgrn_out_norm,
                                mix_post_norm, mlp_pre_norm, mlp_post_norm, w, m, v)

    packed_g = _pack_small(small["dg1"], small["dgp"], small["dg_pre"], small["dg_post"], small["dan"], small["dhn"],
                           small["dlb"], small["loss_vec"])
    reduced = small_all_reduce(packed_g)
    pack = lambda a, b, c2, d, e, f, g: _pack_small(a, b, c2, d, e, f, g)
    w_s = pack(mix_pre_norm, mix_post_norm, mlp_pre_norm, mlp_post_norm, attn_out_norm, hgrn_out_norm, hgrn_lb_logits)
    m_s = pack(m_mix_pre_norm, m_mix_post_norm, m_mlp_pre_norm, m_mlp_post_norm, m_attn_out_norm, m_hgrn_out_norm,
               m_hgrn_lb_logits)
    v_s = pack(v_mix_pre_norm, v_mix_post_norm, v_mlp_pre_norm, v_mlp_post_norm, v_attn_out_norm, v_hgrn_out_norm,
               v_hgrn_lb_logits)
    g_s, d_s, nm_s, nv_s, loss = small_adamw(reduced, w_s, m_s, v_s)
    small_out = [_unpack_small(t) for t in (g_s, d_s, nm_s, nv_s)]

    outs = [loss[0, 0], dx[None]]
    for kind in range(4):
        for name in ORDER:
            outs.append(big[name][kind][None] if name in BIG else small_out[kind][name])
    return tuple(outs)
```

```python
import functools
import math

import jax
import jax.numpy as jnp
from jax import lax
from jax.experimental import pallas as pl
from jax.experimental.pallas import tpu as pltpu

F32 = jnp.float32
BF16 = jnp.bfloat16

D_MODEL = 1024
SEQ = 4096
ATTN_WIDTH = 512
ATTN_HEAD_DIM = 64
ATTN_HEADS = 8
ATTN_BLOCK = 128
DILATIONS = (1, 4, 16)
HGRN_WIDTH = 512
HGRN_HEADS = 4
HGRN_HEAD_DIM = 128
HGRN_CHUNK = 64
IN_PROJ_WIDTH = 3584
D_FF = 4096
RMS_EPS = 1e-6
N_DEV = 8
ADAM_LR = 0.001
ADAM_B1 = 0.9
ADAM_B2 = 0.999
ADAM_EPS = 1e-08
ADAM_WD = 0.01
ADAM_STEP = 10

SUBLANES = 8
LANES = 128
COLUMN_UNROLL = 8
HGRN_CHUNKS_PER_STEP = 2
SUB_BLOCK = 16
TOKEN_TILE = 256
ELEMENTWISE_ROWS = 1024
MLP_TILE = 256
PROJ_TILE = 512
VMEM_LIMIT = 56 * 1024 * 1024
NEG_BIG = -1e30
MESH = pl.DeviceIdType.MESH


def _params(**kw):
    return pltpu.CompilerParams(vmem_limit_bytes=VMEM_LIMIT, **kw)


def _vmem_spec():
    return pl.BlockSpec(memory_space=pltpu.VMEM)


def _dot(a, b):
    return jnp.dot(a, b, preferred_element_type=F32)


def _dot_nt(a, b):
    return lax.dot_general(a, b, (((1,), (1,)), ((), ())), preferred_element_type=F32)


def _dot_tn(a, b):
    return lax.dot_general(a, b, (((0,), (0,)), ((), ())), preferred_element_type=F32)


def _sigmoid(x):
    return 1.0 / (1.0 + jnp.exp(-x))


def _rms_fwd(x, gain, width):
    r = lax.rsqrt(jnp.sum(x * x, axis=-1, keepdims=True) * (1.0 / width) + RMS_EPS)
    return x * r * gain


def _rms_bwd(dy, x, gain, width):
    r = lax.rsqrt(jnp.sum(x * x, axis=-1, keepdims=True) * (1.0 / width) + RMS_EPS)
    xhat = x * r
    dxhat = dy * gain
    dx = r * (dxhat - xhat * (jnp.sum(dxhat * xhat, axis=-1, keepdims=True) * (1.0 / width)))
    return dx, dy * xhat


def _split3(x):
    hi = x.astype(BF16)
    r1 = x - hi.astype(F32)
    mid = r1.astype(BF16)
    lo = (r1 - mid.astype(F32)).astype(BF16)
    return hi, mid, lo


def _tri_sum(tri_bf16, x):
    hi, mid, lo = _split3(x)
    return _dot(tri_bf16, hi) + _dot(tri_bf16, mid) + _dot(tri_bf16, lo)


def _dilated_spec(d, tm, width):
    return pl.BlockSpec((d, tm // d, width), lambda i: (0, i, 0))


def _lane_blocks(ref, value):
    for c in range(ref.shape[0]):
        ref[c] = value[:, c * LANES:(c + 1) * LANES]


def _to_dilated(src_ref, dst_ref, d, tm, cast=None):
    for r in range(d):
        for c in range(src_ref.shape[0]):
            v = src_ref[c] if d == 1 else src_ref[c, pl.ds(r, tm // d, stride=d), :]
            dst_ref[r, :, c * LANES:(c + 1) * LANES] = v if cast is None else v.astype(cast)


def _from_dilated(src_ref, scratch_ref, d, tm):
    if d == 1:
        return src_ref[0].astype(F32)
    nblk = scratch_ref.shape[0]
    for r in range(d):
        for c in range(nblk):
            scratch_ref[c, pl.ds(r, tm // d, stride=d), :] = src_ref[r, :, c * LANES:(c + 1) * LANES].astype(F32)
    return jnp.concatenate([scratch_ref[c] for c in range(nblk)], axis=1)


def in_proj_fwd(x, g1, w_in_b):
    s = x.shape[0]
    tm = PROJ_TILE
    qkv_w = 3 * ATTN_WIDTH
    hg_w = IN_PROJ_WIDTH - qkv_w

    def body(x_ref, g_ref, w_ref, hg_ref, h_ref, *rest):
        qkv_refs, qkv_scr = rest[:len(DILATIONS)], rest[len(DILATIONS)]
        h = _rms_fwd(x_ref[...], g_ref[...], D_MODEL).astype(BF16)
        h_ref[...] = h
        proj = _dot(h, w_ref[...])
        hg_ref[...] = proj[:, qkv_w:]
        _lane_blocks(qkv_scr, proj[:, :qkv_w])
        for d, ref in zip(DILATIONS, qkv_refs):
            _to_dilated(qkv_scr, ref, d, tm, cast=BF16)

    return pl.pallas_call(
        body,
        name="in_proj_fwd",
        grid=(s // tm,),
        in_specs=[
            pl.BlockSpec((tm, D_MODEL), lambda i: (i, 0)),
            pl.BlockSpec((1, D_MODEL), lambda i: (0, 0)),
            _vmem_spec(),
        ],
        out_specs=[
            pl.BlockSpec((tm, hg_w), lambda i: (i, 0)),
            pl.BlockSpec((tm, D_MODEL), lambda i: (i, 0)),
        ] + [_dilated_spec(d, tm, qkv_w) for d in DILATIONS],
        out_shape=[jax.ShapeDtypeStruct((s, hg_w), F32), jax.ShapeDtypeStruct((s, D_MODEL), BF16)] + [
            jax.ShapeDtypeStruct((d, s // d, qkv_w), BF16) for d in DILATIONS],
        scratch_shapes=[pltpu.VMEM((qkv_w // LANES, tm, LANES), F32)],
        compiler_params=_params(dimension_semantics=("arbitrary",)),
    )(x, g1, w_in_b)


ATTN_SCALE = ATTN_HEAD_DIM ** -0.5


def _fill_attn_bias(bias_ref, dilation):
    qi = lax.broadcasted_iota(jnp.int32, (ATTN_BLOCK, 2 * ATTN_BLOCK), 0)
    kj = lax.broadcasted_iota(jnp.int32, (ATTN_BLOCK, 2 * ATTN_BLOCK), 1)
    dist = qi + ATTN_BLOCK - kj
    valid = (dist >= 0) & (dist <= ATTN_BLOCK)
    for head in range(ATTN_HEADS):
        slope = 2.0 ** (-8.0 * (head + 1) / ATTN_HEADS)
        bias = jnp.where(valid, dist.astype(F32) * (-slope * dilation), NEG_BIG)
        bias_ref[0, head] = bias
        bias_ref[1, head] = jnp.where(kj >= ATTN_BLOCK, bias, NEG_BIG)


def _stack_heads(x):
    low = _lane_half(x.shape, 0)
    zero = jnp.zeros_like(x)
    return jnp.concatenate([jnp.where(low, x, zero), jnp.where(low, zero, x)], axis=0)


def _unstack_heads(y):
    half = y.shape[0] // 2
    return jnp.where(_lane_half((half, y.shape[1]), 0), y[:half], y[half:])


def _attn_scores(q_stack, kcat, bias_ref, pair, first_block):
    f = first_block.astype(jnp.int32)
    bias = jnp.concatenate([bias_ref[f, 2 * pair], bias_ref[f, 2 * pair + 1]], axis=0)
    return _dot_nt(q_stack, kcat) + bias


def _lane_half(shape, sub):
    lane = lax.broadcasted_iota(jnp.int32, shape, 1)
    return (lane < ATTN_HEAD_DIM) if sub == 0 else (lane >= ATTN_HEAD_DIM)


def _sub_block(col, row):
    return pl.BlockSpec((None, ATTN_BLOCK, ATTN_WIDTH), lambda r, n: (r, row(n), col))


def attn_fwd(qkv, dilation):
    d, length, _ = qkv.shape
    assert d == dilation
    nb = length // ATTN_BLOCK

    def body(q_ref, kc_ref, kp_ref, vc_ref, vp_ref, o_ref, lse_ref, bias_ref):
        @pl.when((pl.program_id(0) == 0) & (pl.program_id(1) == 0))
        def _():
            _fill_attn_bias(bias_ref, d)

        first = pl.program_id(1) == 0
        for pair in range(ATTN_HEADS // 2):
            lanes = slice(pair * 128, (pair + 1) * 128)
            q_stack = _stack_heads(q_ref[:, lanes] * ATTN_SCALE)
            kcat = jnp.concatenate([kp_ref[:, lanes], kc_ref[:, lanes]], axis=0)
            vcat = jnp.concatenate([vp_ref[:, lanes], vc_ref[:, lanes]], axis=0)
            sc = _attn_scores(q_stack, kcat, bias_ref, pair, first)
            m = jnp.max(sc, axis=-1, keepdims=True)
            p = jnp.exp(sc - m)
            den = jnp.sum(p, axis=-1, keepdims=True)
            o_ref[:, lanes] = _unstack_heads(_dot(p.astype(BF16), vcat) / den).astype(BF16)
            lse_ref[:, lanes] = _unstack_heads(jnp.broadcast_to(m + jnp.log(den), (2 * ATTN_BLOCK, 128)))

    cur = lambda n: n
    prev = lambda n: jnp.maximum(n - 1, 0)
    return pl.pallas_call(
        body,
        name=f"attn_fwd_d{d}",
        grid=(d, nb),
        in_specs=[_sub_block(0, cur), _sub_block(1, cur), _sub_block(1, prev), _sub_block(2, cur), _sub_block(2, prev)],
        out_specs=[_sub_block(0, cur), _sub_block(0, cur)],
        out_shape=[jax.ShapeDtypeStruct((d, length, ATTN_WIDTH), BF16), jax.ShapeDtypeStruct((d, length, ATTN_WIDTH), F32)],
        scratch_shapes=[pltpu.VMEM((2, ATTN_HEADS, ATTN_BLOCK, 2 * ATTN_BLOCK), F32)],
        compiler_params=_params(dimension_semantics=("arbitrary", "arbitrary")),
    )(qkv, qkv, qkv, qkv, qkv)


def attn_bwd(qkv, d_out, lse, delta, dilation, ride=None):
    d, length, _ = qkv.shape
    assert d == dilation
    nb = length // ATTN_BLOCK

    steps = d * nb + 1

    def body(q_ref, kc_ref, kp_ref, vc_ref, vp_ref, do_ref, lse_ref, dl_ref, dq_ref, dk_ref, dv_ref, ck_ref, cv_ref,
             bias_ref):
        t = pl.program_id(0)

        @pl.when(t == 0)
        def _():
            ck_ref[...] = jnp.zeros_like(ck_ref)
            cv_ref[...] = jnp.zeros_like(cv_ref)
            _fill_attn_bias(bias_ref, d)

        @pl.when(t < steps - 1)
        def _():
            first = t % nb == 0
            for pair in range(ATTN_HEADS // 2):
                lanes = slice(pair * 128, (pair + 1) * 128)
                q_stack = _stack_heads(q_ref[:, lanes] * ATTN_SCALE)
                do_stack = _stack_heads(do_ref[:, lanes])
                kcat = jnp.concatenate([kp_ref[:, lanes], kc_ref[:, lanes]], axis=0)
                vcat = jnp.concatenate([vp_ref[:, lanes], vc_ref[:, lanes]], axis=0)
                col_a, col_b = pair * 128, pair * 128 + ATTN_HEAD_DIM
                lse_col = jnp.concatenate([lse_ref[:, col_a:col_a + 1], lse_ref[:, col_b:col_b + 1]], axis=0)
                dl_col = jnp.concatenate([dl_ref[:, col_a:col_a + 1], dl_ref[:, col_b:col_b + 1]], axis=0)
                p = jnp.exp(_attn_scores(q_stack, kcat, bias_ref, pair, first) - lse_col)
                ds = (p * (_dot_nt(do_stack, vcat) - dl_col)).astype(BF16)
                dq_ref[:, lanes] = (_unstack_heads(_dot(ds, kcat)) * ATTN_SCALE).astype(BF16)
                dk_cat = _dot_tn(ds, q_stack)
                dv_cat = _dot_tn(p.astype(BF16), do_stack)
                dk_ref[:, lanes] = (ck_ref[:, lanes] + dk_cat[:ATTN_BLOCK]).astype(BF16)
                dv_ref[:, lanes] = (cv_ref[:, lanes] + dv_cat[:ATTN_BLOCK]).astype(BF16)
                ck_ref[:, lanes] = dk_cat[ATTN_BLOCK:]
                cv_ref[:, lanes] = dv_cat[ATTN_BLOCK:]

        @pl.when(t == steps - 1)
        def _():
            dk_ref[...] = ck_ref[...].astype(BF16)
            dv_ref[...] = cv_ref[...].astype(BF16)

    blk = (ATTN_BLOCK, ATTN_WIDTH)

    def spec(col, shift):
        def index(t):
            f = jnp.minimum(t, steps - 2) if shift > -2 else jnp.maximum(t - 1, 0)
            r, n = f // nb, f % nb
            return (r, jnp.maximum(n - 1, 0) if shift == -1 else n, col)
        return pl.BlockSpec((None, ATTN_BLOCK, ATTN_WIDTH), index)

    step = lambda k: (lambda: pl.program_id(0) == k)
    e_in, e_out, e_shape, e_scr, e_args = _ride_specs(ride)
    return pl.pallas_call(
        _riding(body, 8, 3, 3, ride, step(0), step(steps // 2), step(steps - 1)),
        name=f"attn_bwd_d{d}",
        grid=(steps,),
        in_specs=[spec(0, 0), spec(1, 0), spec(1, -1), spec(2, 0), spec(2, -1), spec(0, 0), spec(0, 0), spec(0, 0)] + e_in,
        out_specs=[spec(0, 0), spec(0, -2), spec(0, -2)] + e_out,
        out_shape=[jax.ShapeDtypeStruct((d, length, ATTN_WIDTH), BF16)] * 3 + e_shape,
        scratch_shapes=[pltpu.VMEM(blk, F32), pltpu.VMEM(blk, F32),
                        pltpu.VMEM((2, ATTN_HEADS, ATTN_BLOCK, 2 * ATTN_BLOCK), F32)] + e_scr,
        compiler_params=_params(dimension_semantics=("arbitrary",)),
    )(qkv, qkv, qkv, qkv, qkv, d_out, lse, delta, *e_args)


def _lower_bound(logits):
    return _sigmoid(logits[0:1, :] - logits[1:2, :])


def _hgrn_gates(q, fp, lb):
    sq = _sigmoid(q)
    qf = q * sq
    sig = _sigmoid(fp)
    sig_neg = _sigmoid(-fp)
    kf = (1.0 - lb) * sig_neg
    log_sig = jnp.minimum(fp, 0.0) - jnp.log(1.0 + jnp.exp(-jnp.abs(fp)))
    a = jnp.log(lb)
    c = jnp.log(1.0 - lb) + log_sig
    log_f = jnp.maximum(a, c) + jnp.log(1.0 + jnp.exp(-jnp.abs(a - c)))
    return sq, qf, (sig, sig_neg, c), log_f, kf


def _tril_bf16(n, upper=False):
    r = lax.broadcasted_iota(jnp.int32, (n, n), 0)
    c = lax.broadcasted_iota(jnp.int32, (n, n), 1)
    keep = (c >= r) if upper else (c <= r)
    return jnp.where(keep, 1.0, 0.0).astype(BF16)


def _hgrn_diagonal_loops(c_len, diagonal):
    for half in range(SUB_BLOCK // SUBLANES):
        def step(jj, carry, half=half):
            j = half * SUBLANES + jj
            for i in range(c_len // SUB_BLOCK):
                diagonal(slice(i * SUB_BLOCK + half * SUBLANES, (i + 1) * SUB_BLOCK), j, i * SUB_BLOCK + j)
            return carry

        lax.fori_loop(0, SUBLANES, step, 0, unroll=COLUMN_UNROLL)


def _hgrn_off_diagonal(b, qf, kf):
    c_len, width = b.shape
    edges = [b[0:1, :]] + [b[i * SUB_BLOCK - 1:i * SUB_BLOCK, :] for i in range(1, c_len // SUB_BLOCK)]
    eq = jnp.exp(b - jnp.concatenate([jnp.broadcast_to(e, (SUB_BLOCK, width)) for e in edges], axis=0))
    q_til = qf * eq
    k_til, ek = [], []
    for i in range(1, c_len // SUB_BLOCK):
        n = i * SUB_BLOCK
        e = jnp.exp(edges[i] - b[:n, :])
        ek.append(e)
        k_til.append(jnp.concatenate([kf[:n, :] * e, jnp.zeros((2 * c_len - n, width), F32)], axis=0))
    return q_til, k_til, eq, ek


def _split2(x):
    hi = x.astype(BF16)
    return hi, (x - hi.astype(F32)).astype(BF16)


def hgrn_fwd(proj, lb, ride=None):
    s = proj.shape[0]
    c_len, nh, hd = HGRN_CHUNK, HGRN_HEADS, HGRN_HEAD_DIM
    n_chunks = s // c_len
    col0 = 0

    cps = HGRN_CHUNKS_PER_STEP
    n_steps = n_chunks // cps

    def body(q_ref, f_ref, i_ref, lb_ref, o_ref, st_out_ref, a_out_ref, st_ref, b_ref, qf_ref, kf_ref, a_ref):
        @pl.when(pl.program_id(0) == 0)
        def _():
            st_ref[...] = jnp.zeros_like(st_ref)

        lbv = _lower_bound(lb_ref[...])
        for u in range(cps):
            rs = slice(u * c_len, (u + 1) * c_len)
            b_u, qf_u, kf_u, a_u = b_ref.at[u], qf_ref.at[u], kf_ref.at[u], a_ref.at[u]
            _, qf, _, log_f, kf = _hgrn_gates(q_ref[rs, :], f_ref[rs, :], lbv)
            b = _tri_sum(_tril_bf16(c_len), log_f)
            b_u[...] = b
            qf_u[...] = qf
            kf_u[...] = kf
            a_u[...] = jnp.zeros_like(a_u)

            def diagonal(rows, j, key, b_u=b_u, qf_u=qf_u, kf_u=kf_u, a_u=a_u):
                bj = b_u[pl.ds(key, 1), :]
                kj = kf_u[pl.ds(key, 1), :]
                nrow = rows.stop - rows.start
                t_loc = lax.broadcasted_iota(jnp.int32, (nrow, nh * hd), 0) + (rows.start % SUB_BLOCK)
                e = jnp.exp(jnp.where(t_loc >= j, b_u[rows, :] - bj, NEG_BIG))
                prod = qf_u[rows, :] * kj * e
                lane = lax.broadcasted_iota(jnp.int32, (nrow, hd), 1)
                for h in range(nh):
                    col = jnp.sum(prod[:, h * hd:(h + 1) * hd], axis=-1, keepdims=True)
                    a_u[h, rows, :] = jnp.where(lane == key, col, a_u[h, rows, :])

            _hgrn_diagonal_loops(c_len, diagonal)
            q_til, k_til, _, _ = _hgrn_off_diagonal(b, qf, kf)
            q_til = q_til.astype(BF16)
            k_til = [k.astype(BF16) for k in k_til]

            b_last = b[c_len - 1:c_len, :]
            qb = (qf * jnp.exp(b)).astype(BF16)
            kb2 = (kf * jnp.exp(b_last - b)).astype(BF16)
            vf = i_ref[rs, :].astype(BF16)
            for h in range(nh):
                hs = slice(h * hd, (h + 1) * hd)
                st = st_ref[h]
                st_out_ref[u, h] = st
                off = [jnp.zeros((SUB_BLOCK, hd), F32)]
                for i in range(1, c_len // SUB_BLOCK):
                    off.append(_dot_nt(q_til[i * SUB_BLOCK:(i + 1) * SUB_BLOCK, hs], k_til[i - 1][:, hs]))
                a_h = a_u[h] + jnp.concatenate(off, axis=0)
                a_out_ref[rs, hs] = a_h
                o_ref[rs, hs] = _dot_nt(qb[:, hs], st.astype(BF16)) + _dot(a_h[:, :c_len].astype(BF16), vf[:, hs])
                st_ref[h] = st * jnp.exp(b_last[:, hs]) + _dot_tn(vf[:, hs], kb2[:, hs])

    blk = (cps * c_len, HGRN_WIDTH)
    sblk = (cps, c_len, HGRN_WIDTH)
    step = lambda k: (lambda: pl.program_id(0) == k)
    e_in, e_out, e_shape, e_scr, e_args = _ride_specs(ride)
    return pl.pallas_call(
        _riding(body, 4, 3, 5, ride, step(0), step((7 * n_steps) // 8), step(n_steps - 1)),
        name="hgrn_fwd",
        grid=(n_steps,),
        in_specs=[
            pl.BlockSpec(blk, lambda c: (c, col0)),
            pl.BlockSpec(blk, lambda c: (c, col0 + 1)),
            pl.BlockSpec(blk, lambda c: (c, col0 + 2)),
            pl.BlockSpec((2, HGRN_WIDTH), lambda c: (0, 0)),
        ] + e_in,
        out_specs=[
            pl.BlockSpec(blk, lambda c: (c, 0)),
            pl.BlockSpec((cps, nh, hd, hd), lambda c: (c, 0, 0, 0)),
            pl.BlockSpec(blk, lambda c: (c, 0)),
        ] + e_out,
        out_shape=[
            jax.ShapeDtypeStruct((s, HGRN_WIDTH), F32),
            jax.ShapeDtypeStruct((n_chunks, nh, hd, hd), F32),
            jax.ShapeDtypeStruct((s, nh * hd), F32),
        ] + e_shape,
        scratch_shapes=[
            pltpu.VMEM((nh, hd, hd), F32),
            pltpu.VMEM(sblk, F32),
            pltpu.VMEM(sblk, F32),
            pltpu.VMEM(sblk, F32),
            pltpu.VMEM((cps, nh, c_len, hd), F32),
        ] + e_scr,
        compiler_params=_params(dimension_semantics=("arbitrary",)),
    )(proj, proj, proj, lb, *e_args)


def hgrn_bwd(proj, lb, d_o, states, a_mat, ride=None):
    s = proj.shape[0]
    c_len, nh, hd = HGRN_CHUNK, HGRN_HEADS, HGRN_HEAD_DIM
    n_chunks = s // c_len
    col0 = 0
    cps = HGRN_CHUNKS_PER_STEP
    n_steps = n_chunks // cps
    last = n_steps - 1

    def body(q_ref, f_ref, i_ref, lb_ref, do_ref, st_in_ref, a_in_ref, dq_ref, df_ref, di_ref, dlb_ref,
             dst_ref, b_ref, qf_ref, kf_ref, da_ref, dqi_ref, dki_ref):
        @pl.when(pl.program_id(0) == 0)
        def _():
            dst_ref[...] = jnp.zeros_like(dst_ref)
            dlb_ref[...] = jnp.zeros_like(dlb_ref)

        lbv = _lower_bound(lb_ref[...])
        for u in reversed(range(cps)):
            rs = slice(u * c_len, (u + 1) * c_len)
            b_u, qf_u, kf_u, da_u, dqi_u, dki_u = (b_ref.at[u], qf_ref.at[u], kf_ref.at[u], da_ref.at[u], dqi_ref.at[u],
                                                   dki_ref.at[u])
            q = q_ref[rs, :]
            sq, qf, (sig, sig_neg, log_c), log_f, kf = _hgrn_gates(q, f_ref[rs, :], lbv)
            b = _tri_sum(_tril_bf16(c_len), log_f)
            b_u[...] = b
            qf_u[...] = qf
            kf_u[...] = kf
            b_last = b[c_len - 1:c_len, :]
            eb = jnp.exp(b)
            ebl = jnp.exp(b_last - b)
            qb = qf * eb
            kb2 = kf * ebl
            vf = i_ref[rs, :]
            d_o = do_ref[rs, :]
            qb_b, kb2_b, vf_b, do_b = qb.astype(BF16), kb2.astype(BF16), vf.astype(BF16), d_o.astype(BF16)
            tq = lax.broadcasted_iota(jnp.int32, (c_len, hd), 0)
            lane = lax.broadcasted_iota(jnp.int32, (c_len, hd), 1)

            dqb_parts, dvf_parts, dkb2_parts, dbl_parts = [], [], [], []
            for h in range(nh):
                hs = slice(h * hd, (h + 1) * hd)
                st = st_in_ref[u, h]
                dst = dst_ref[h]
                st_b, dst_b = st.astype(BF16), dst.astype(BF16)
                a_h = a_in_ref[rs, hs][:, :c_len].astype(BF16)
                dqb_parts.append(_dot(do_b[:, hs], st_b))
                dvf_parts.append(_dot_tn(a_h, do_b[:, hs]) + _dot_nt(kb2_b[:, hs], dst_b))
                dkb2_parts.append(_dot(vf_b[:, hs], dst_b))
                da = _dot_nt(do_b[:, hs], vf_b[:, hs])
                da = jnp.concatenate([da, jnp.zeros((c_len, hd - c_len), F32)], axis=1)
                da_u[h] = jnp.where(tq >= lane, da, 0.0)
                dbl_parts.append(jnp.sum(dst * st, axis=0, keepdims=True) * jnp.exp(b_last[:, hs]))
                dst_ref[h] = dst * jnp.exp(b_last[:, hs]) + _dot_tn(do_b[:, hs], qb_b[:, hs])
            dqb = jnp.concatenate(dqb_parts, axis=1)
            dvf = jnp.concatenate(dvf_parts, axis=1)
            dkb2 = jnp.concatenate(dkb2_parts, axis=1)
            dbl = jnp.concatenate(dbl_parts, axis=1) + jnp.sum(dkb2 * kb2, axis=0, keepdims=True)

            dqi_u[...] = jnp.zeros_like(dqi_u)
            t_idx = lax.broadcasted_iota(jnp.int32, (c_len, nh * hd), 0)

            def diagonal(rows, j, key, b_u=b_u, qf_u=qf_u, kf_u=kf_u, da_u=da_u, dqi_u=dqi_u, dki_u=dki_u):
                bj = b_u[pl.ds(key, 1), :]
                kj = kf_u[pl.ds(key, 1), :]
                nrow = rows.stop - rows.start
                t_loc = lax.broadcasted_iota(jnp.int32, (nrow, nh * hd), 0) + (rows.start % SUB_BLOCK)
                e = jnp.exp(jnp.where(t_loc >= j, b_u[rows, :] - bj, NEG_BIG))
                lane_r = lax.broadcasted_iota(jnp.int32, (nrow, hd), 1)
                cols = [jnp.sum(jnp.where(lane_r == key, da_u[h, rows, :], 0.0), axis=-1, keepdims=True)
                        for h in range(nh)]
                w = e * jnp.concatenate([jnp.broadcast_to(cc, (nrow, hd)) for cc in cols], axis=1)
                dqi_u[rows, :] += w * kj
                dki_u[pl.ds(key, 1), :] = jnp.sum(w * qf_u[rows, :], axis=0, keepdims=True)

            _hgrn_diagonal_loops(c_len, diagonal)

            q_til, k_til, eq, ek = _hgrn_off_diagonal(b, qf, kf)
            q_hi, q_lo = _split2(q_til)
            k_pairs = [_split2(k) for k in k_til]
            n_sub = c_len // SUB_BLOCK
            dq_heads, dk_heads = [], []
            for h in range(nh):
                hs = slice(h * hd, (h + 1) * hd)
                dq_rows = [jnp.zeros((SUB_BLOCK, hd), F32)]
                dk_h = jnp.zeros((c_len, hd), F32)
                for i in range(1, n_sub):
                    rows = slice(i * SUB_BLOCK, (i + 1) * SUB_BLOCK)
                    n = i * SUB_BLOCK
                    da_i = da_u[h, rows, :].astype(BF16)
                    k_hi, k_lo = k_pairs[i - 1]
                    dq_rows.append((_dot(da_i, k_hi[:, hs]) + _dot(da_i, k_lo[:, hs])) * eq[rows, hs])
                    dk_t = (_dot_tn(da_i, q_hi[rows, hs]) + _dot_tn(da_i, q_lo[rows, hs]))[:n, :] * ek[i - 1][:, hs]
                    dk_h = dk_h + jnp.concatenate([dk_t, jnp.zeros((c_len - n, hd), F32)], axis=0)
                dq_heads.append(jnp.concatenate(dq_rows, axis=0))
                dk_heads.append(dk_h)
            dq_intra = dqi_u[...] + jnp.concatenate(dq_heads, axis=1)
            dk_intra = dki_u[...] + jnp.concatenate(dk_heads, axis=1)

            db = dqb * qb + qf * dq_intra - kf * dk_intra - dkb2 * kb2
            db = db + jnp.where(t_idx == c_len - 1, dbl, 0.0)
            dg = _tri_sum(_tril_bf16(c_len, upper=True), db)
            dqf = dqb * eb + dq_intra
            dkf = dkb2 * ebl + dk_intra
            dq_ref[rs, :] = (dqf * (sq * (1.0 + q * (1.0 - sq)))).astype(BF16)
            df_ref[rs, :] = (sig_neg * (dg * jnp.exp(log_c - log_f) - dkf * (1.0 - lbv) * sig)).astype(BF16)
            di_ref[rs, :] = dvf.astype(BF16)
            dlb_ref[...] += jnp.sum(sig_neg * (dg * jnp.exp(-log_f) - dkf), axis=0, keepdims=True)

    blk = (cps * c_len, HGRN_WIDTH)
    sblk = (cps, c_len, HGRN_WIDTH)
    rev = lambda c: last - c
    step = lambda k: (lambda: pl.program_id(0) == k)
    e_in, e_out, e_shape, e_scr, e_args = _ride_specs(ride)
    return pl.pallas_call(
        _riding(body, 7, 4, 7, ride, step(0), step(n_steps // 2), step(last)),
        name="hgrn_bwd",
        grid=(n_steps,),
        in_specs=[
            pl.BlockSpec(blk, lambda c: (rev(c), col0)),
            pl.BlockSpec(blk, lambda c: (rev(c), col0 + 1)),
            pl.BlockSpec(blk, lambda c: (rev(c), col0 + 2)),
            pl.BlockSpec((2, HGRN_WIDTH), lambda c: (0, 0)),
            pl.BlockSpec(blk, lambda c: (rev(c), 0)),
            pl.BlockSpec((cps, nh, hd, hd), lambda c: (rev(c), 0, 0, 0)),
            pl.BlockSpec(blk, lambda c: (rev(c), 0)),
        ] + e_in,
        out_specs=[
            pl.BlockSpec(blk, lambda c: (rev(c), 0)),
            pl.BlockSpec(blk, lambda c: (rev(c), 0)),
            pl.BlockSpec(blk, lambda c: (rev(c), 0)),
            pl.BlockSpec((1, HGRN_WIDTH), lambda c: (0, 0)),
        ] + e_out,
        out_shape=[jax.ShapeDtypeStruct((s, HGRN_WIDTH), BF16)] * 3 + [jax.ShapeDtypeStruct((1, HGRN_WIDTH), F32)] + e_shape,
        scratch_shapes=[
            pltpu.VMEM((nh, hd, hd), F32),
            pltpu.VMEM(sblk, F32),
            pltpu.VMEM(sblk, F32),
            pltpu.VMEM(sblk, F32),
            pltpu.VMEM((cps, nh, c_len, hd), F32),
            pltpu.VMEM(sblk, F32),
            pltpu.VMEM(sblk, F32),
        ] + e_scr,
        compiler_params=_params(dimension_semantics=("arbitrary",)),
    )(proj, proj, proj, lb, d_o, states, a_mat, *e_args)


def _row_spec(tm, width, col=0):
    return pl.BlockSpec((tm, width), lambda i: (i, col))


def _const_spec(width):
    return pl.BlockSpec((1, width), lambda i: (0, 0))


def _acc_rows(ref, value):
    @pl.when(pl.program_id(0) == 0)
    def _():
        ref[...] = jnp.zeros_like(ref)

    ref[...] += jnp.sum(value, axis=0, keepdims=True)


def mix_fwd(attn_parts, o_h, proj, an, hn, w_out_b, gp, x, ride=None):
    s = x.shape[0]
    tm = TOKEN_TILE
    gate_col = 3
    hd = HGRN_HEAD_DIM
    nd = len(DILATIONS)

    def body(*refs):
        o_refs, l_refs = refs[:nd], refs[nd:2 * nd]
        oh_ref, gate_ref, an_ref, hn_ref, w_ref, gp_ref, x_ref = refs[2 * nd:2 * nd + 7]
        x1_ref, cat_ref, mixed_ref, attn_ref = refs[2 * nd + 7:2 * nd + 11]
        lse_refs = refs[2 * nd + 11:3 * nd + 11]
        o_scr, l_scr, lse_scr = refs[3 * nd + 11:]
        os_ = [_from_dilated(r, o_scr.at[k], d, tm) for k, (r, d) in enumerate(zip(o_refs, DILATIONS))]
        ls = [_from_dilated(r, l_scr.at[k], d, tm) for k, (r, d) in enumerate(zip(l_refs, DILATIONS))]
        m = jnp.maximum(jnp.maximum(ls[0], ls[1]), ls[2])
        es = [jnp.exp(l - m) for l in ls]
        den = es[0] + es[1] + es[2]
        attn = (es[0] * os_[0] + es[1] * os_[1] + es[2] * os_[2]) / den
        attn_ref[...] = attn
        _lane_blocks(lse_scr, m + jnp.log(den))
        for d, ref in zip(DILATIONS, lse_refs):
            _to_dilated(lse_scr, ref, d, tm)
        cat_ref[:, :ATTN_WIDTH] = _rms_fwd(attn, an_ref[...], ATTN_WIDTH).astype(BF16)
        gate = gate_ref[...]
        silu_g = gate * _sigmoid(gate)
        for h in range(HGRN_HEADS):
            hs = slice(h * hd, (h + 1) * hd)
            rec = _rms_fwd(oh_ref[:, hs], hn_ref[:, hs], hd) * silu_g[:, hs]
            cat_ref[:, ATTN_WIDTH + h * hd:ATTN_WIDTH + (h + 1) * hd] = rec.astype(BF16)
        mixed = _dot(cat_ref[...], w_ref[...])
        mixed_ref[...] = mixed
        x1_ref[...] = x_ref[...] + _rms_fwd(mixed, gp_ref[...], D_MODEL)

    aw = ATTN_WIDTH
    n_steps = s // tm
    step = lambda k: (lambda: pl.program_id(0) == k)
    e_in, e_out, e_shape, e_scr, e_args = _ride_specs(ride)
    return pl.pallas_call(
        _riding(body, 2 * nd + 7, 4 + nd, 3, ride, step(0), step((13 * n_steps) // 16), step(n_steps - 1)),
        name="mix_fwd",
        grid=(n_steps,),
        in_specs=[_dilated_spec(d, tm, aw) for d in DILATIONS] * 2 + [
            _row_spec(tm, aw), _row_spec(tm, aw, gate_col), _const_spec(aw), _const_spec(aw), _vmem_spec(),
            _const_spec(D_MODEL), _row_spec(tm, D_MODEL)] + e_in,
        out_specs=[_row_spec(tm, D_MODEL), _row_spec(tm, D_MODEL), _row_spec(tm, D_MODEL), _row_spec(tm, aw)] + [
            _dilated_spec(d, tm, aw) for d in DILATIONS] + e_out,
        out_shape=[
            jax.ShapeDtypeStruct((s, D_MODEL), F32),
            jax.ShapeDtypeStruct((s, D_MODEL), BF16),
            jax.ShapeDtypeStruct((s, D_MODEL), F32),
            jax.ShapeDtypeStruct((s, aw), F32),
        ] + [jax.ShapeDtypeStruct((d, s // d, aw), F32) for d in DILATIONS] + e_shape,
        scratch_shapes=[pltpu.VMEM((nd, aw // LANES, tm, LANES), F32), pltpu.VMEM((nd, aw // LANES, tm, LANES), F32),
                        pltpu.VMEM((aw // LANES, tm, LANES), F32)] + e_scr,
        compiler_params=_params(dimension_semantics=("arbitrary",)),
    )(*[p[0] for p in attn_parts], *[p[1] for p in attn_parts], o_h, proj, an, hn, w_out_b, gp, x, *e_args)


def mix_bwd(dx1, mixed, gp, w_out_b, attn, an, o_h, proj, hn):
    s = dx1.shape[0]
    tm = TOKEN_TILE
    gate_col = 3
    hd = HGRN_HEAD_DIM
    aw = ATTN_WIDTH

    nd = len(DILATIONS)

    def body(*refs):
        dx1_ref, mixed_ref, gp_ref, w_ref, attn_ref, an_ref, oh_ref, gate_ref, hn_ref, dmix_ref = refs[:10]
        do_refs, delta_refs = refs[10:10 + nd], refs[10 + nd:10 + 2 * nd]
        doh_ref, dgate_ref, dgp_ref, dan_ref, dhn_ref, do_ref, delta_ref = refs[10 + 2 * nd:]
        dmixed, gp_c = _rms_bwd(dx1_ref[...], mixed_ref[...], gp_ref[...], D_MODEL)
        _acc_rows(dgp_ref, gp_c)
        dmixed_b = dmixed.astype(BF16)
        dmix_ref[...] = dmixed_b
        dcat = _dot_nt(dmixed_b, w_ref[...])
        attn = attn_ref[...]
        d_o, an_c = _rms_bwd(dcat[:, :aw], attn, an_ref[...], aw)
        _acc_rows(dan_ref, an_c)
        _lane_blocks(do_ref, d_o)
        prod = d_o * attn
        for pair in range(ATTN_HEADS // 2):
            pp = prod[:, pair * LANES:(pair + 1) * LANES]
            low = _lane_half((tm, LANES), 0)
            lo = jnp.sum(jnp.where(low, pp, 0.0), axis=-1, keepdims=True)
            hi = jnp.sum(jnp.where(low, 0.0, pp), axis=-1, keepdims=True)
            delta_ref[pair] = jnp.where(low, lo, hi)
        for d, o_ref, l_ref in zip(DILATIONS, do_refs, delta_refs):
            _to_dilated(do_ref, o_ref, d, tm, cast=BF16)
            _to_dilated(delta_ref, l_ref, d, tm)
        gate = gate_ref[...]
        sg = _sigmoid(gate)
        silu_g = gate * sg
        drec = dcat[:, aw:]
        hn_parts = []
        for h in range(HGRN_HEADS):
            hs = slice(h * hd, (h + 1) * hd)
            oh = oh_ref[:, hs]
            on = _rms_fwd(oh, hn_ref[:, hs], hd)
            dgate_ref[:, hs] = (drec[:, hs] * on * (sg[:, hs] * (1.0 + gate[:, hs] * (1.0 - sg[:, hs])))).astype(BF16)
            d_oh, hn_c = _rms_bwd(drec[:, hs] * silu_g[:, hs], oh, hn_ref[:, hs], hd)
            doh_ref[:, hs] = d_oh
            hn_parts.append(hn_c)
        _acc_rows(dhn_ref, jnp.concatenate(hn_parts, axis=1))

    return pl.pallas_call(
        body,
        name="mix_bwd",
        grid=(s // tm,),
        in_specs=[_row_spec(tm, D_MODEL), _row_spec(tm, D_MODEL), _const_spec(D_MODEL), _vmem_spec(), _row_spec(tm, aw),
                  _const_spec(aw), _row_spec(tm, aw), _row_spec(tm, aw, gate_col), _const_spec(aw)],
        out_specs=[_row_spec(tm, D_MODEL)] + [_dilated_spec(d, tm, aw) for d in DILATIONS] * 2 + [_row_spec(tm, aw)] * 2 + [
            _const_spec(D_MODEL), _const_spec(aw), _const_spec(aw)],
        out_shape=[jax.ShapeDtypeStruct((s, D_MODEL), BF16)] + [
            jax.ShapeDtypeStruct((d, s // d, aw), BF16) for d in DILATIONS] + [
            jax.ShapeDtypeStruct((d, s // d, aw), F32) for d in DILATIONS] + [
            jax.ShapeDtypeStruct((s, aw), F32), jax.ShapeDtypeStruct((s, aw), BF16),
            jax.ShapeDtypeStruct((1, D_MODEL), F32), jax.ShapeDtypeStruct((1, aw), F32),
            jax.ShapeDtypeStruct((1, aw), F32)],
        scratch_shapes=[pltpu.VMEM((aw // LANES, tm, LANES), F32), pltpu.VMEM((aw // LANES, tm, LANES), F32)],
        compiler_params=_params(dimension_semantics=("arbitrary",)),
    )(dx1, mixed, gp, w_out_b, attn, an, o_h, proj, hn)


def mlp_fwd_bwd(x1, g_pre, w1_blocks, w2_b, g_post, target):
    s = x1.shape[0]
    tm = MLP_TILE
    nblk, _, fb = w1_blocks.shape

    def body(x1_ref, gpre_ref, w1_ref, w2_ref, gpost_ref, t_ref,
             dx1_ref, h2_ref, a_ref, du_ref, dff_ref, loss_ref, dgpre_ref, dgpost_ref, u_ref):
        x1v = x1_ref[...]
        h2 = _rms_fwd(x1v, gpre_ref[...], D_MODEL).astype(BF16)
        h2_ref[...] = h2
        ff = jnp.zeros((tm, D_MODEL), F32)
        for j in range(nblk):
            cols = slice(j * fb, (j + 1) * fb)
            ru = jnp.maximum(_dot(h2, w1_ref[j]), 0.0)
            u_ref[:, cols] = ru.astype(BF16)
            a = (ru * ru).astype(BF16)
            a_ref[:, cols] = a
            ff = ff + _dot(a, w2_ref[cols, :])
        diff = x1v + _rms_fwd(ff, gpost_ref[...], D_MODEL) - t_ref[...]
        _acc_rows(loss_ref, diff * diff)
        dy = diff * (1.0 / D_MODEL)
        dff, gpost_c = _rms_bwd(dy, ff, gpost_ref[...], D_MODEL)
        _acc_rows(dgpost_ref, gpost_c)
        dff_b = dff.astype(BF16)
        dff_ref[...] = dff_b
        dh2 = jnp.zeros((tm, D_MODEL), F32)
        for j in range(nblk):
            cols = slice(j * fb, (j + 1) * fb)
            du = (_dot_nt(dff_b, w2_ref[cols, :]) * (2.0 * u_ref[:, cols])).astype(BF16)
            du_ref[:, cols] = du
            dh2 = dh2 + _dot_nt(du, w1_ref[j])
        dxa, gpre_c = _rms_bwd(dh2, x1v, gpre_ref[...], D_MODEL)
        _acc_rows(dgpre_ref, gpre_c)
        dx1_ref[...] = dy + dxa

    dm = D_MODEL
    return pl.pallas_call(
        body,
        name="mlp_fwd_bwd",
        grid=(s // tm,),
        in_specs=[_row_spec(tm, dm), _const_spec(dm), _vmem_spec(), _vmem_spec(), _const_spec(dm), _row_spec(tm, dm)],
        out_specs=[_row_spec(tm, dm), _row_spec(tm, dm), _row_spec(tm, D_FF), _row_spec(tm, D_FF), _row_spec(tm, dm),
                   _const_spec(dm), _const_spec(dm), _const_spec(dm)],
        out_shape=[
            jax.ShapeDtypeStruct((s, dm), F32),
            jax.ShapeDtypeStruct((s, dm), BF16),
            jax.ShapeDtypeStruct((s, D_FF), BF16),
            jax.ShapeDtypeStruct((s, D_FF), BF16),
            jax.ShapeDtypeStruct((s, dm), BF16),
            jax.ShapeDtypeStruct((1, dm), F32),
            jax.ShapeDtypeStruct((1, dm), F32),
            jax.ShapeDtypeStruct((1, dm), F32),
        ],
        scratch_shapes=[pltpu.VMEM((tm, D_FF), BF16)],
        compiler_params=_params(dimension_semantics=("arbitrary",)),
    )(x1, g_pre, w1_blocks, w2_b, g_post, target)


def in_proj_bwd(attn_grads, hgrn_grads, dgate, w_in_b, x, g1, dx1):
    s = x.shape[0]
    tm = PROJ_TILE
    aw = ATTN_WIDTH
    n_attn = len(attn_grads)
    flat = [g[k] for k in range(3) for g in attn_grads] + list(hgrn_grads) + [dgate]

    def body(*refs):
        parts = refs[:len(flat)]
        w_ref, x_ref, g_ref, dx1_ref, dx_ref, dproj_ref, dg_ref, scr = refs[len(flat):]
        groups = []
        for k in range(3):
            acc = None
            for p, d in zip(parts[k * n_attn:(k + 1) * n_attn], DILATIONS):
                v = _from_dilated(p, scr, d, tm)
                acc = v if acc is None else acc + v
            groups.append(acc)
        groups += [p[...] for p in parts[3 * n_attn:]]
        dh = jnp.zeros((tm, D_MODEL), F32)
        for gi, grp in enumerate(groups):
            cols = slice(gi * aw, (gi + 1) * aw)
            gb = grp.astype(BF16)
            dproj_ref[:, cols] = gb
            dh = dh + _dot_nt(gb, w_ref[:, cols])
        dxa, g_c = _rms_bwd(dh, x_ref[...], g_ref[...], D_MODEL)
        _acc_rows(dg_ref, g_c)
        dx_ref[...] = dx1_ref[...] + dxa

    dm = D_MODEL
    return pl.pallas_call(
        body,
        name="in_proj_bwd",
        grid=(s // tm,),
        in_specs=[_dilated_spec(d, tm, aw) for d in DILATIONS] * 3 + [_row_spec(tm, aw)] * 4 + [
            _vmem_spec(), _row_spec(tm, dm), _const_spec(dm), _row_spec(tm, dm)],
        out_specs=[_row_spec(tm, dm), _row_spec(tm, IN_PROJ_WIDTH), _const_spec(dm)],
        out_shape=[jax.ShapeDtypeStruct((s, dm), F32), jax.ShapeDtypeStruct((s, IN_PROJ_WIDTH), BF16),
                   jax.ShapeDtypeStruct((1, dm), F32)],
        scratch_shapes=[pltpu.VMEM((aw // LANES, tm, LANES), F32)],
        compiler_params=_params(dimension_semantics=("arbitrary",)),
    )(*flat, w_in_b, x, g1, dx1)


def wgrad(a_b, b_b, tn, name, ts=1024, per_step=1):
    s, k = a_b.shape
    n = b_b.shape[1]

    def body(a_ref, b_ref, o_ref):
        @pl.when(pl.program_id(1) == 0)
        def _():
            o_ref[...] = jnp.zeros_like(o_ref)

        a = a_ref[...]
        for jj in range(per_step):
            o_ref[jj] += _dot_tn(a, b_ref[:, jj * tn:(jj + 1) * tn])

    wide = tn * per_step
    return pl.pallas_call(
        body,
        name=name,
        grid=(n // wide, s // ts),
        in_specs=[pl.BlockSpec((ts, k), lambda j, i: (i, 0)), pl.BlockSpec((ts, wide), lambda j, i: (i, j))],
        out_specs=pl.BlockSpec((per_step, k, tn), lambda j, i: (j, 0, 0)),
        out_shape=jax.ShapeDtypeStruct((n // tn, k, tn), F32),
        compiler_params=_params(dimension_semantics=("arbitrary", "arbitrary")),
    )(a_b, b_b)


def train_step(x, target, g1, an, logits, hn, gp, g_pre, g_post, w, m, v):
    nd = len(DILATIONS)
    shard_b = {k: w[k].astype(BF16) for k in BIG}
    (w_in_g,) = run_exchange(gather_exchange([shard_b["w_in"]]), "gather_w_in")
    w_in_b = w_in_g.transpose(1, 0, 2).reshape(D_MODEL, IN_PROJ_WIDTH)

    proj, h_b, *qkvs = in_proj_fwd(x, g1, w_in_b)
    attn_parts = [attn_fwd(qkv, d) for qkv, d in zip(qkvs, DILATIONS)]
    o_h, states, a_mat, w_out_g, w1_blocks = hgrn_fwd(
        proj, logits, ride=gather_exchange([shard_b["w_out"], shard_b["w_ff1"]]))
    w_out_b = w_out_g.reshape(D_MODEL, D_MODEL)
    x1, cat_b, mixed, attn, *lses, w2_g = mix_fwd(attn_parts, o_h, proj, an, hn, w_out_b, gp, x,
                                                  ride=gather_exchange([shard_b["w_ff2"]]))
    w2_b = w2_g.reshape(D_FF, D_MODEL)
    dx1, h2_b, a_b, du_b, dff_b, loss_vec, dg_pre, dg_post = mlp_fwd_bwd(x1, g_pre, w1_blocks, w2_b, g_post, target)
    dw2 = wgrad(a_b, dff_b, D_MODEL, "wgrad_ff2", ts=512)
    dw1 = wgrad(h2_b, du_b, D_FF // N_DEV, "wgrad_ff1", per_step=2)
    dmix_b, *rest = mix_bwd(dx1, mixed, gp, w_out_b, attn, an, o_h, proj, hn)
    d_os, deltas = rest[:nd], rest[nd:2 * nd]
    d_oh, dgate, dgp, dan, dhn = rest[2 * nd:]
    dwout = wgrad(cat_b, dmix_b, D_MODEL, "wgrad_out")

    early = ("w_out", "w_ff1", "w_ff2")
    early_grads = [dwout.reshape(N_DEV, D_MODEL // N_DEV, D_MODEL), dw1, dw2.reshape(N_DEV, D_FF // N_DEV, D_MODEL)]
    res = attn_bwd(qkvs[0], d_os[0], lses[0], deltas[0], DILATIONS[0], ride=to_core_exchange(early_grads))
    pairs = [pair_sum(g, s, f"pair_sum_{name}") for g, s, name in zip(early_grads, res[3:], early)]
    attn_grads = [res[:3]]
    *res, others_ff2 = attn_bwd(qkvs[1], d_os[1], lses[1], deltas[1], DILATIONS[1],
                                ride=to_chip_exchange([pairs[2][1]]))
    attn_grads.append(res)
    attn_grads.append(attn_bwd(qkvs[2], d_os[2], lses[2], deltas[2], DILATIONS[2]))
    dq_h, df_h, di_h, dlb, *others = hgrn_bwd(proj, logits, d_oh, states, a_mat,
                                              ride=to_chip_exchange([pairs[0][1], pairs[1][1]]))
    others.append(others_ff2)
    dx, dproj_b, dg1 = in_proj_bwd(attn_grads, (dq_h, df_h, di_h), dgate, w_in_b, x, g1, dx1)
    dwin = wgrad(h_b, dproj_b, 2 * IN_PROJ_WIDTH // N_DEV, "wgrad_in")
    big = {name: sum_adamw(p[0], o, w[name], m[name], v[name], f"sum_adamw_{name}")
           for name, p, o in zip(early, pairs, others)}

    shard_w = IN_PROJ_WIDTH // N_DEV
    dwin_blocks = dwin.reshape(N_DEV // 2, D_MODEL, 2, shard_w).transpose(0, 2, 1, 3).reshape(N_DEV, D_MODEL, shard_w)
    (from_sibling,) = run_exchange(to_core_exchange([dwin_blocks]), "reduce_w_in_to_core")
    pair_in, pair_in_b = pair_sum(dwin_blocks, from_sibling, "pair_sum_w_in")
    (others_in,) = run_exchange(to_chip_exchange([pair_in_b]), "reduce_w_in_to_chip")
    big["w_in"] = sum_adamw(pair_in, others_in, w["w_in"], m["w_in"], v["w_in"], "sum_adamw_w_in")
    small = dict(dg1=dg1, dan=dan, dlb=dlb, dhn=dhn, dgp=dgp, dg_pre=dg_pre, dg_post=dg_post, loss_vec=loss_vec)
    return dx, big, small


def _position():
    x, y, c = lax.axis_index("x"), lax.axis_index("y"), lax.axis_index("c")
    other_chips = [(1 - x, y), (x, 1 - y), (1 - x, 1 - y)]
    return x, y, c, other_chips


def _any_spec():
    return pl.BlockSpec(memory_space=pl.ANY)


class Exchange:
    def __init__(self, arrays, out_shape, sems, stages):
        self.arrays, self.out_shape, self.sems, self.stages = list(arrays), list(out_shape), list(sems), stages


def gather_exchange(shards):
    n = len(shards)

    def stages(ins, outs, sems):
        send_sems, recv_sems, local_sems = sems

        def parts():
            x, y, c, chips = _position()
            me, sibling = (x, y, c), (x, y, 1 - c)

            def slot(a, px, py, pc):
                return outs[a].at[4 * px + 2 * py + pc]

            def copy(a, k, block, to, src=None):
                return pltpu.make_async_remote_copy(
                    src_ref=slot(a, *block) if src is None else src, dst_ref=slot(a, *block),
                    send_sem=send_sems.at[a, k], recv_sem=recv_sems.at[a, k], device_id=to, device_id_type=MESH)

            local = [pltpu.make_async_copy(ins[a], slot(a, *me), local_sems.at[a]) for a in range(n)]
            first = []
            for a in range(n):
                first.append(copy(a, 0, me, sibling, src=ins[a]))
                first += [copy(a, 1 + j, me, (*chip, c), src=ins[a]) for j, chip in enumerate(chips)]
            passed = [copy(a, 4 + j, (*chip, c), sibling) for j, chip in enumerate(chips) for a in range(n)]
            return c, chips, me, sibling, copy, local, first, passed

        def begin():
            _, _, _, _, _, local, first, _ = parts()
            for cp in local + first:
                cp.start()

        def middle():
            c, chips, me, _, copy, _, _, passed = parts()
            k = 0
            for j, chip in enumerate(chips):
                for a in range(n):
                    copy(a, 1 + j, (*chip, c), me).wait_recv()
                    passed[k].start()
                    k += 1

        def end():
            c, chips, me, sibling, copy, local, first, passed = parts()
            for a in range(n):
                copy(a, 0, sibling, me).wait_recv()
                for j, chip in enumerate(chips):
                    copy(a, 4 + j, (*chip, 1 - c), me).wait_recv()
            for cp in first + passed:
                cp.wait_send()
            for cp in local:
                cp.wait()

        return begin, middle, end

    return Exchange(
        shards, [jax.ShapeDtypeStruct((N_DEV,) + sh.shape, sh.dtype) for sh in shards],
        [pltpu.SemaphoreType.DMA((n, 7)), pltpu.SemaphoreType.DMA((n, 7)), pltpu.SemaphoreType.DMA((n,))], stages)


def to_core_exchange(grads):
    n = len(grads)

    def stages(ins, outs, sems):
        send_sems, recv_sems = sems

        def copies():
            x, y, c, _ = _position()
            return [pltpu.make_async_remote_copy(
                src_ref=ins[a].at[2 * q + (1 - c)], dst_ref=outs[a].at[q], send_sem=send_sems.at[a, q],
                recv_sem=recv_sems.at[a, q], device_id=(x, y, 1 - c), device_id_type=MESH)
                for a in range(n) for q in range(4)]

        def begin():
            for cp in copies():
                cp.start()

        def end():
            for cp in copies():
                cp.wait()

        return begin, None, end

    return Exchange(grads, [jax.ShapeDtypeStruct((4,) + g.shape[1:], g.dtype) for g in grads],
                    [pltpu.SemaphoreType.DMA((n, 4)), pltpu.SemaphoreType.DMA((n, 4))], stages)


def pair_sum(grad, from_sibling, name):
    _, r, cdim = grad.shape
    tr = min(r, ELEMENTWISE_ROWS)
    c_idx = lax.axis_index("c").astype(jnp.int32).reshape(1)

    def body(c_ref, g_ref, s_ref, o_ref, ob_ref):
        total = g_ref[...] + s_ref[...]
        o_ref[...] = total
        ob_ref[...] = total.astype(BF16)

    blk = lambda: pl.BlockSpec((1, tr, cdim), lambda q, i, cr: (q, i, 0))
    return pl.pallas_call(
        body,
        name=name,
        grid_spec=pltpu.PrefetchScalarGridSpec(
            num_scalar_prefetch=1,
            grid=(4, r // tr),
            in_specs=[pl.BlockSpec((1, tr, cdim), lambda q, i, cr: (2 * q + cr[0], i, 0)), blk()],
            out_specs=[blk(), blk()],
        ),
        out_shape=[jax.ShapeDtypeStruct((4, r, cdim), F32), jax.ShapeDtypeStruct((4, r, cdim), BF16)],
        compiler_params=_params(dimension_semantics=("arbitrary", "arbitrary")),
    )(c_idx, grad, from_sibling)


def to_chip_exchange(pairs):
    n = len(pairs)

    def stages(ins, outs, sems):
        send_sems, recv_sems = sems

        def copies():
            x, y, c, chips = _position()
            return [pltpu.make_async_remote_copy(
                src_ref=ins[a].at[2 * px + py], dst_ref=outs[a].at[j], send_sem=send_sems.at[a, j],
                recv_sem=recv_sems.at[a, j], device_id=(px, py, c), device_id_type=MESH)
                for a in range(n) for j, (px, py) in enumerate(chips)]

        def begin():
            for cp in copies():
                cp.start()

        def end():
            for cp in copies():
                cp.wait()

        return begin, None, end

    return Exchange(pairs, [jax.ShapeDtypeStruct((3,) + p.shape[1:], p.dtype) for p in pairs],
                    [pltpu.SemaphoreType.DMA((n, 3)), pltpu.SemaphoreType.DMA((n, 3))], stages)


def run_exchange(ex, name):
    n_in, n_out = len(ex.arrays), len(ex.out_shape)

    def body(*refs):
        begin, middle, end = ex.stages(refs[:n_in], refs[n_in:n_in + n_out], refs[n_in + n_out:])
        begin()
        if middle is not None:
            middle()
        end()

    return pl.pallas_call(
        body,
        name=name,
        in_specs=[_any_spec()] * n_in,
        out_specs=[_any_spec()] * n_out,
        out_shape=ex.out_shape,
        scratch_shapes=ex.sems,
    )(*ex.arrays)


def _riding(body, n_in, n_out, n_scratch, ex, first, middle, last):
    if ex is None:
        return body
    r_in, r_out = len(ex.arrays), len(ex.out_shape)

    def wrapped(*refs):
        k_in, refs = refs[:n_in], refs[n_in:]
        e_in, refs = refs[:r_in], refs[r_in:]
        k_out, refs = refs[:n_out], refs[n_out:]
        e_out, refs = refs[:r_out], refs[r_out:]
        k_scr, e_sems = refs[:n_scratch], refs[n_scratch:]
        begin, mid, end = ex.stages(e_in, e_out, e_sems)
        pl.when(first())(begin)
        body(*k_in, *k_out, *k_scr)
        if mid is not None:
            pl.when(middle())(mid)
        pl.when(last())(end)

    return wrapped


def _ride_specs(ex):
    if ex is None:
        return [], [], [], [], []
    return [_any_spec()] * len(ex.arrays), [_any_spec()] * len(ex.out_shape), ex.out_shape, ex.sems, ex.arrays


def _adamw(w, g, m, v):
    m = ADAM_B1 * m + (1.0 - ADAM_B1) * g
    v = ADAM_B2 * v + (1.0 - ADAM_B2) * (g * g)
    m_hat = m / (1.0 - ADAM_B1 ** ADAM_STEP)
    v_hat = v / (1.0 - ADAM_B2 ** ADAM_STEP)
    delta = -ADAM_LR * (m_hat / (jnp.sqrt(v_hat) + ADAM_EPS) + ADAM_WD * w)
    return delta, m, v


def sum_adamw(pairs, others, w, m, v, name):
    r, cdim = w.shape
    tr = min(r, ELEMENTWISE_ROWS // 2)
    chip_idx =(2 * lax.axis_index("x") + lax.axis_index("y")).astype(jnp.int32).reshape(1)

    def body(q_ref, p_ref, o_ref, w_ref, m_ref, v_ref, g_out, d_out, m_out, v_out):
        g = p_ref[0] + o_ref[0].astype(F32) + o_ref[1].astype(F32) + o_ref[2].astype(F32)
        g_out[...] = g
        d_out[...], m_out[...], v_out[...] = _adamw(w_ref[...], g, m_ref[...], v_ref[...])

    tile = lambda: pl.BlockSpec((tr, cdim), lambda i, qr: (i, 0))
    return pl.pallas_call(
        body,
        name=name,
        grid_spec=pltpu.PrefetchScalarGridSpec(
            num_scalar_prefetch=1,
            grid=(r // tr,),
            in_specs=[pl.BlockSpec((1, tr, cdim), lambda i, qr: (qr[0], i, 0)),
                      pl.BlockSpec((3, tr, cdim), lambda i, qr: (0, i, 0)), tile(), tile(), tile()],
            out_specs=[tile(), tile(), tile(), tile()],
        ),
        out_shape=[jax.ShapeDtypeStruct((r, cdim), F32)] * 4,
        compiler_params=_params(dimension_semantics=("arbitrary",)),
    )(chip_idx, pairs, others, w, m, v)


SMALL_ROWS = 8


def small_all_reduce(packed):
    shape = packed.shape

    def body(in_ref, out_ref, recv_ref, send_sems, recv_sems):
        x, y, c, _ = _position()
        my_id = 4 * x + 2 * y + c
        recv_ref[my_id] = in_ref[...]
        copies = []
        for rel in range(1, N_DEV):
            fx, fy, fc = (rel >> 2) & 1, (rel >> 1) & 1, rel & 1
            px = 1 - x if fx else x
            py = 1 - y if fy else y
            pc = 1 - c if fc else c
            cp = pltpu.make_async_remote_copy(
                src_ref=in_ref, dst_ref=recv_ref.at[my_id], send_sem=send_sems.at[rel - 1],
                recv_sem=recv_sems.at[rel - 1], device_id=(px, py, pc), device_id_type=MESH)
            cp.start()
            copies.append((cp, pltpu.make_async_remote_copy(
                src_ref=in_ref, dst_ref=recv_ref.at[4 * px + 2 * py + pc], send_sem=send_sems.at[rel - 1],
                recv_sem=recv_sems.at[rel - 1], device_id=(px, py, pc), device_id_type=MESH)))
        for cp, landing in copies:
            landing.wait_recv()
        for cp, landing in copies:
            cp.wait_send()
        total = recv_ref[0]
        for k in range(1, N_DEV):
            total = total + recv_ref[k]
        out_ref[...] = total

    return pl.pallas_call(
        body,
        name="small_all_reduce",
        in_specs=[_vmem_spec()],
        out_specs=_vmem_spec(),
        out_shape=jax.ShapeDtypeStruct(shape, F32),
        scratch_shapes=[pltpu.VMEM((N_DEV,) + shape, F32), pltpu.SemaphoreType.DMA((N_DEV - 1,)),
                        pltpu.SemaphoreType.DMA((N_DEV - 1,))],
    )(packed)


def small_adamw(reduced, w, m, v):
    def body(r_ref, w_ref, m_ref, v_ref, g_out, d_out, m_out, v_out, loss_out):
        red = r_ref[...]
        wv = w_ref[...]
        lb = _lower_bound(jnp.concatenate([wv[5:6, :HGRN_WIDTH], wv[5:6, HGRN_WIDTH:]], axis=0))
        t = red[5:6, :HGRN_WIDTH] * lb * (1.0 - lb)
        row = lax.broadcasted_iota(jnp.int32, red.shape, 0)
        g = jnp.where(row == 5, jnp.concatenate([t, -t], axis=1), jnp.where(row >= 6, 0.0, red))
        g_out[...] = g
        d_out[...], m_out[...], v_out[...] = _adamw(wv, g, m_ref[...], v_ref[...])
        loss = jnp.sum(red[6:7, :], axis=-1, keepdims=True) * (0.5 / D_MODEL)
        loss_out[...] = jnp.broadcast_to(loss, loss_out.shape)

    return pl.pallas_call(
        body,
        name="small_adamw",
        in_specs=[_vmem_spec()] * 4,
        out_specs=[_vmem_spec()] * 5,
        out_shape=[jax.ShapeDtypeStruct(reduced.shape, F32)] * 4 + [jax.ShapeDtypeStruct((8, 128), F32)],
    )(reduced, w, m, v)


def _pack_small(g1, gp, g_pre, g_post, an, hn, logits_or_dlb, extra=None):
    row5 = logits_or_dlb.reshape(1, -1)
    row5 = jnp.pad(row5, ((0, 0), (0, D_MODEL - row5.shape[1])))
    row6 = jnp.zeros((1, D_MODEL), F32) if extra is None else extra
    return jnp.concatenate([g1, gp, g_pre, g_post, jnp.concatenate([an, hn], axis=1), row5, row6,
                            jnp.zeros((1, D_MODEL), F32)], axis=0)


def _unpack_small(p):
    return dict(mix_pre_norm=p[0:1], mix_post_norm=p[1:2], mlp_pre_norm=p[2:3], mlp_post_norm=p[3:4],
                attn_out_norm=p[4:5, :ATTN_WIDTH], hgrn_out_norm=p[4:5, ATTN_WIDTH:],
                hgrn_lb_logits=p[5].reshape(2, HGRN_WIDTH))


BIG = ("w_in", "w_out", "w_ff1", "w_ff2")
ORDER = ("mix_pre_norm", "w_in", "attn_out_norm", "hgrn_lb_logits", "hgrn_out_norm", "w_out", "mix_post_norm",
         "mlp_pre_norm", "w_ff1", "w_ff2", "mlp_post_norm")


def kernel(x, mix_pre_norm, w_in, attn_out_norm, hgrn_lb_logits, hgrn_out_norm, w_out, mix_post_norm, mlp_pre_norm, w_ff1, w_ff2, mlp_post_norm, loss_target, m_mix_pre_norm, m_w_in, m_attn_out_norm, m_hgrn_lb_logits, m_hgrn_out_norm, m_w_out, m_mix_post_norm, m_mlp_pre_norm, m_w_ff1, m_w_ff2, m_mlp_post_norm, v_mix_pre_norm, v_w_in, v_attn_out_norm, v_hgrn_lb_logits, v_hgrn_out_norm, v_w_out, v_mix_post_norm, v_mlp_pre_norm, v_w_ff1, v_w_ff2, v_mlp_post_norm):
    w = dict(w_in=w_in[0], w_out=w_out[0], w_ff1=w_ff1[0], w_ff2=w_ff2[0])
    m = dict(w_in=m_w_in[0], w_out=m_w_out[0], w_ff1=m_w_ff1[0], w_ff2=m_w_ff2[0])
    v = dict(w_in=v_w_in[0], w_out=v_w_out[0], w_ff1=v_w_ff1[0], w_ff2=v_w_ff2[0])

    dx, big, small = train_step(x[0], loss_target[0], mix_pre_norm, attn_out_norm, hgrn_lb_logits, hgrn_out_norm,
                                mix_post_norm, mlp_pre_norm, mlp_post_norm, w, m, v)

    packed_g = _pack_small(small["dg1"], small["dgp"], small["dg_pre"], small["dg_post"], small["dan"], small["dhn"],
                           small["dlb"], small["loss_vec"])
    reduced = small_all_reduce(packed_g)
    pack = lambda a, b, c2, d, e, f, g: _pack_small(a, b, c2, d, e, f, g)
    w_s = pack(mix_pre_norm, mix_post_norm, mlp_pre_norm, mlp_post_norm, attn_out_norm, hgrn_out_norm, hgrn_lb_logits)
    m_s = pack(m_mix_pre_norm, m_mix_post_norm, m_mlp_pre_norm, m_mlp_post_norm, m_attn_out_norm, m_hgrn_out_norm,
               m_hgrn_lb_logits)
    v_s = pack(v_mix_pre_norm, v_mix_post_norm, v_mlp_pre_norm, v_mlp_post_norm, v_attn_out_norm, v_hgrn_out_norm,
               v_hgrn_lb_logits)
    g_s, d_s, nm_s, nv_s, loss = small_adamw(reduced, w_s, m_s, v_s)
    small_out = [_unpack_small(t) for t in (g_s, d_s, nm_s, nv_s)]

    outs = [loss[0, 0], dx[None]]
    for kind in range(4):
        for name in ORDER:
            outs.append(big[name][kind][None] if name in BIG else small_out[kind][name])
    return tuple(outs)
```

```python
import jax
import jax.numpy as jnp
from jax import lax
from jax.experimental import pallas as pl
from jax.experimental.pallas import tpu as pltpu

F32 = jnp.float32
BF16 = jnp.bfloat16

D_MODEL = 1024
ATTN_WIDTH = 512
ATTN_HEAD_DIM = 64
ATTN_HEADS = 8
ATTN_BLOCK = 128
DILATIONS = (1, 4, 16)
HGRN_WIDTH = 512
HGRN_HEADS = 4
HGRN_HEAD_DIM = 128
HGRN_CHUNK = 64
IN_PROJ_WIDTH = 3584
D_FF = 4096
RMS_EPS = 1e-6
N_DEV = 8
ADAM_LR = 0.001
ADAM_B1 = 0.9
ADAM_B2 = 0.999
ADAM_EPS = 1e-08
ADAM_WD = 0.01
ADAM_STEP = 10

SUBLANES = 8
LANES = 128
COLUMN_UNROLL = 8
HGRN_CHUNKS_PER_STEP = 2
SUB_BLOCK = 16
TOKEN_TILE = 256
ELEMENTWISE_ROWS = 1024
MLP_TILE = 256
PROJ_TILE = 512
VMEM_LIMIT = 56 * 1024 * 1024
NEG_BIG = -1e30
MESH = pl.DeviceIdType.MESH


def _params(**kw):
    return pltpu.CompilerParams(vmem_limit_bytes=VMEM_LIMIT, **kw)


def _vmem_spec():
    return pl.BlockSpec(memory_space=pltpu.VMEM)


def _dot(a, b):
    return jnp.dot(a, b, preferred_element_type=F32)


def _dot_nt(a, b):
    return lax.dot_general(a, b, (((1,), (1,)), ((), ())), preferred_element_type=F32)


def _dot_tn(a, b):
    return lax.dot_general(a, b, (((0,), (0,)), ((), ())), preferred_element_type=F32)


def _sigmoid(x):
    return 1.0 / (1.0 + jnp.exp(-x))


def _rms_fwd(x, gain, width):
    r = lax.rsqrt(jnp.sum(x * x, axis=-1, keepdims=True) * (1.0 / width) + RMS_EPS)
    return x * r * gain


def _rms_bwd(dy, x, gain, width):
    r = lax.rsqrt(jnp.sum(x * x, axis=-1, keepdims=True) * (1.0 / width) + RMS_EPS)
    xhat = x * r
    dxhat = dy * gain
    dx = r * (dxhat - xhat * (jnp.sum(dxhat * xhat, axis=-1, keepdims=True) * (1.0 / width)))
    return dx, dy * xhat


def _split3(x):
    hi = x.astype(BF16)
    r1 = x - hi.astype(F32)
    mid = r1.astype(BF16)
    lo = (r1 - mid.astype(F32)).astype(BF16)
    return hi, mid, lo


def _tri_sum(tri_bf16, x):
    hi, mid, lo = _split3(x)
    return _dot(tri_bf16, hi) + _dot(tri_bf16, mid) + _dot(tri_bf16, lo)


def _dilated_spec(d, tm, width):
    return pl.BlockSpec((d, tm // d, width), lambda i: (0, i, 0))


def _lane_blocks(ref, value):
    for c in range(ref.shape[0]):
        ref[c] = value[:, c * LANES:(c + 1) * LANES]


def _to_dilated(src_ref, dst_ref, d, tm, cast=None):
    for r in range(d):
        for c in range(src_ref.shape[0]):
            v = src_ref[c] if d == 1 else src_ref[c, pl.ds(r, tm // d, stride=d), :]
            dst_ref[r, :, c * LANES:(c + 1) * LANES] = v if cast is None else v.astype(cast)


def _from_dilated(src_ref, scratch_ref, d, tm):
    if d == 1:
        return src_ref[0].astype(F32)
    nblk = scratch_ref.shape[0]
    for r in range(d):
        for c in range(nblk):
            scratch_ref[c, pl.ds(r, tm // d, stride=d), :] = src_ref[r, :, c * LANES:(c + 1) * LANES].astype(F32)
    return jnp.concatenate([scratch_ref[c] for c in range(nblk)], axis=1)


def in_proj_fwd(x, g1, w_in_b):
    s = x.shape[0]
    tm = PROJ_TILE
    qkv_w = 3 * ATTN_WIDTH
    hg_w = IN_PROJ_WIDTH - qkv_w

    def body(x_ref, g_ref, w_ref, hg_ref, h_ref, *rest):
        qkv_refs, qkv_scr = rest[:len(DILATIONS)], rest[len(DILATIONS)]
        h = _rms_fwd(x_ref[...], g_ref[...], D_MODEL).astype(BF16)
        h_ref[...] = h
        proj = _dot(h, w_ref[...])
        hg_ref[...] = proj[:, qkv_w:]
        _lane_blocks(qkv_scr, proj[:, :qkv_w])
        for d, ref in zip(DILATIONS, qkv_refs):
            _to_dilated(qkv_scr, ref, d, tm, cast=BF16)

    return pl.pallas_call(
        body,
        name="in_proj_fwd",
        grid=(s // tm,),
        in_specs=[
            pl.BlockSpec((tm, D_MODEL), lambda i: (i, 0)),
            pl.BlockSpec((1, D_MODEL), lambda i: (0, 0)),
            _vmem_spec(),
        ],
        out_specs=[
            pl.BlockSpec((tm, hg_w), lambda i: (i, 0)),
            pl.BlockSpec((tm, D_MODEL), lambda i: (i, 0)),
        ] + [_dilated_spec(d, tm, qkv_w) for d in DILATIONS],
        out_shape=[jax.ShapeDtypeStruct((s, hg_w), F32), jax.ShapeDtypeStruct((s, D_MODEL), BF16)] + [
            jax.ShapeDtypeStruct((d, s // d, qkv_w), BF16) for d in DILATIONS],
        scratch_shapes=[pltpu.VMEM((qkv_w // LANES, tm, LANES), F32)],
        compiler_params=_params(dimension_semantics=("arbitrary",)),
    )(x, g1, w_in_b)


ATTN_SCALE = ATTN_HEAD_DIM ** -0.5


def _fill_attn_bias(bias_ref, dilation):
    qi = lax.broadcasted_iota(jnp.int32, (ATTN_BLOCK, 2 * ATTN_BLOCK), 0)
    kj = lax.broadcasted_iota(jnp.int32, (ATTN_BLOCK, 2 * ATTN_BLOCK), 1)
    dist = qi + ATTN_BLOCK - kj
    valid = (dist >= 0) & (dist <= ATTN_BLOCK)
    for head in range(ATTN_HEADS):
        slope = 2.0 ** (-8.0 * (head + 1) / ATTN_HEADS)
        bias = jnp.where(valid, dist.astype(F32) * (-slope * dilation), NEG_BIG)
        bias_ref[0, head] = bias
        bias_ref[1, head] = jnp.where(kj >= ATTN_BLOCK, bias, NEG_BIG)


def _stack_heads(x):
    low = _lane_half(x.shape, 0)
    zero = jnp.zeros_like(x)
    return jnp.concatenate([jnp.where(low, x, zero), jnp.where(low, zero, x)], axis=0)


def _unstack_heads(y):
    half = y.shape[0] // 2
    return jnp.where(_lane_half((half, y.shape[1]), 0), y[:half], y[half:])


def _attn_scores(q_stack, kcat, bias_ref, pair, first_block):
    f = first_block.astype(jnp.int32)
    bias = jnp.concatenate([bias_ref[f, 2 * pair], bias_ref[f, 2 * pair + 1]], axis=0)
    return _dot_nt(q_stack, kcat) + bias


def _lane_half(shape, sub):
    lane = lax.broadcasted_iota(jnp.int32, shape, 1)
    return (lane < ATTN_HEAD_DIM) if sub == 0 else (lane >= ATTN_HEAD_DIM)


def _sub_block(col, row):
    return pl.BlockSpec((None, ATTN_BLOCK, ATTN_WIDTH), lambda r, n: (r, row(n), col))


def attn_fwd(qkv, dilation):
    d, length, _ = qkv.shape
    assert d == dilation
    nb = length // ATTN_BLOCK

    def body(q_ref, kc_ref, kp_ref, vc_ref, vp_ref, o_ref, lse_ref, bias_ref):
        @pl.when((pl.program_id(0) == 0) & (pl.program_id(1) == 0))
        def _():
            _fill_attn_bias(bias_ref, d)

        first = pl.program_id(1) == 0
        for pair in range(ATTN_HEADS // 2):
            lanes = slice(pair * LANES, (pair + 1) * LANES)
            q_stack = _stack_heads(q_ref[:, lanes] * ATTN_SCALE)
            kcat = jnp.concatenate([kp_ref[:, lanes], kc_ref[:, lanes]], axis=0)
            vcat = jnp.concatenate([vp_ref[:, lanes], vc_ref[:, lanes]], axis=0)
            sc = _attn_scores(q_stack, kcat, bias_ref, pair, first)
            m = jnp.max(sc, axis=-1, keepdims=True)
            p = jnp.exp(sc - m)
            den = jnp.sum(p, axis=-1, keepdims=True)
            o_ref[:, lanes] = _unstack_heads(_dot(p.astype(BF16), vcat) / den).astype(BF16)
            lse_ref[:, lanes] = _unstack_heads(jnp.broadcast_to(m + jnp.log(den), (2 * ATTN_BLOCK, LANES)))

    cur = lambda n: n
    prev = lambda n: jnp.maximum(n - 1, 0)
    return pl.pallas_call(
        body,
        name=f"attn_fwd_d{d}",
        grid=(d, nb),
        in_specs=[_sub_block(0, cur), _sub_block(1, cur), _sub_block(1, prev), _sub_block(2, cur), _sub_block(2, prev)],
        out_specs=[_sub_block(0, cur), _sub_block(0, cur)],
        out_shape=[jax.ShapeDtypeStruct((d, length, ATTN_WIDTH), BF16), jax.ShapeDtypeStruct((d, length, ATTN_WIDTH), F32)],
        scratch_shapes=[pltpu.VMEM((2, ATTN_HEADS, ATTN_BLOCK, 2 * ATTN_BLOCK), F32)],
        compiler_params=_params(dimension_semantics=("arbitrary", "arbitrary")),
    )(qkv, qkv, qkv, qkv, qkv)


def attn_bwd(qkv, d_out, lse, delta, dilation, ride=None):
    d, length, _ = qkv.shape
    assert d == dilation
    nb = length // ATTN_BLOCK

    steps = d * nb + 1

    def body(q_ref, kc_ref, kp_ref, vc_ref, vp_ref, do_ref, lse_ref, dl_ref, dq_ref, dk_ref, dv_ref, ck_ref, cv_ref,
             bias_ref):
        t = pl.program_id(0)

        @pl.when(t == 0)
        def _():
            ck_ref[...] = jnp.zeros_like(ck_ref)
            cv_ref[...] = jnp.zeros_like(cv_ref)
            _fill_attn_bias(bias_ref, d)

        @pl.when(t < steps - 1)
        def _():
            first = t % nb == 0
            for pair in range(ATTN_HEADS // 2):
                lanes = slice(pair * LANES, (pair + 1) * LANES)
                q_stack = _stack_heads(q_ref[:, lanes] * ATTN_SCALE)
                do_stack = _stack_heads(do_ref[:, lanes])
                kcat = jnp.concatenate([kp_ref[:, lanes], kc_ref[:, lanes]], axis=0)
                vcat = jnp.concatenate([vp_ref[:, lanes], vc_ref[:, lanes]], axis=0)
                col_a, col_b = pair * LANES, pair * LANES + ATTN_HEAD_DIM
                lse_col = jnp.concatenate([lse_ref[:, col_a:col_a + 1], lse_ref[:, col_b:col_b + 1]], axis=0)
                dl_col = jnp.concatenate([dl_ref[:, col_a:col_a + 1], dl_ref[:, col_b:col_b + 1]], axis=0)
                p = jnp.exp(_attn_scores(q_stack, kcat, bias_ref, pair, first) - lse_col)
                ds = (p * (_dot_nt(do_stack, vcat) - dl_col)).astype(BF16)
                dq_ref[:, lanes] = (_unstack_heads(_dot(ds, kcat)) * ATTN_SCALE).astype(BF16)
                dk_cat = _dot_tn(ds, q_stack)
                dv_cat = _dot_tn(p.astype(BF16), do_stack)
                dk_ref[:, lanes] = (ck_ref[:, lanes] + dk_cat[:ATTN_BLOCK]).astype(BF16)
                dv_ref[:, lanes] = (cv_ref[:, lanes] + dv_cat[:ATTN_BLOCK]).astype(BF16)
                ck_ref[:, lanes] = dk_cat[ATTN_BLOCK:]
                cv_ref[:, lanes] = dv_cat[ATTN_BLOCK:]

        @pl.when(t == steps - 1)
        def _():
            dk_ref[...] = ck_ref[...].astype(BF16)
            dv_ref[...] = cv_ref[...].astype(BF16)

    blk = (ATTN_BLOCK, ATTN_WIDTH)

    def spec(col, shift):
        def index(t):
            f = jnp.minimum(t, steps - 2) if shift > -2 else jnp.maximum(t - 1, 0)
            r, n = f // nb, f % nb
            return (r, jnp.maximum(n - 1, 0) if shift == -1 else n, col)
        return pl.BlockSpec((None, ATTN_BLOCK, ATTN_WIDTH), index)

    step = lambda k: (lambda: pl.program_id(0) == k)
    e_in, e_out, e_shape, e_scr, e_args = _ride_specs(ride)
    return pl.pallas_call(
        _riding(body, 8, 3, 3, ride, step(0), step(steps // 2), step(steps - 1)),
        name=f"attn_bwd_d{d}",
        grid=(steps,),
        in_specs=[spec(0, 0), spec(1, 0), spec(1, -1), spec(2, 0), spec(2, -1), spec(0, 0), spec(0, 0), spec(0, 0)] + e_in,
        out_specs=[spec(0, 0), spec(0, -2), spec(0, -2)] + e_out,
        out_shape=[jax.ShapeDtypeStruct((d, length, ATTN_WIDTH), BF16)] * 3 + e_shape,
        scratch_shapes=[pltpu.VMEM(blk, F32), pltpu.VMEM(blk, F32),
                        pltpu.VMEM((2, ATTN_HEADS, ATTN_BLOCK, 2 * ATTN_BLOCK), F32)] + e_scr,
        compiler_params=_params(dimension_semantics=("arbitrary",)),
    )(qkv, qkv, qkv, qkv, qkv, d_out, lse, delta, *e_args)


def _lower_bound(logits):
    return _sigmoid(logits[0:1, :] - logits[1:2, :])


def _hgrn_gates(q, fp, lb):
    sq = _sigmoid(q)
    qf = q * sq
    sig = _sigmoid(fp)
    sig_neg = _sigmoid(-fp)
    kf = (1.0 - lb) * sig_neg
    log_sig = jnp.minimum(fp, 0.0) - jnp.log(1.0 + jnp.exp(-jnp.abs(fp)))
    a = jnp.log(lb)
    c = jnp.log(1.0 - lb) + log_sig
    log_f = jnp.maximum(a, c) + jnp.log(1.0 + jnp.exp(-jnp.abs(a - c)))
    return sq, qf, (sig, sig_neg, c), log_f, kf


def _tril_bf16(n, upper=False):
    r = lax.broadcasted_iota(jnp.int32, (n, n), 0)
    c = lax.broadcasted_iota(jnp.int32, (n, n), 1)
    keep = (c >= r) if upper else (c <= r)
    return jnp.where(keep, 1.0, 0.0).astype(BF16)


def _hgrn_diagonal_loops(c_len, diagonal):
    for half in range(SUB_BLOCK // SUBLANES):
        def step(jj, carry, half=half):
            j = half * SUBLANES + jj
            for i in range(c_len // SUB_BLOCK):
                diagonal(slice(i * SUB_BLOCK + half * SUBLANES, (i + 1) * SUB_BLOCK), j, i * SUB_BLOCK + j)
            return carry

        lax.fori_loop(0, SUBLANES, step, 0, unroll=COLUMN_UNROLL)


def _hgrn_off_diagonal(b, qf, kf):
    c_len, width = b.shape
    edges = [b[0:1, :]] + [b[i * SUB_BLOCK - 1:i * SUB_BLOCK, :] for i in range(1, c_len // SUB_BLOCK)]
    eq = jnp.exp(b - jnp.concatenate([jnp.broadcast_to(e, (SUB_BLOCK, width)) for e in edges], axis=0))
    q_til = qf * eq
    k_til, ek = [], []
    for i in range(1, c_len // SUB_BLOCK):
        n = i * SUB_BLOCK
        e = jnp.exp(edges[i] - b[:n, :])
        ek.append(e)
        k_til.append(jnp.concatenate([kf[:n, :] * e, jnp.zeros((2 * c_len - n, width), F32)], axis=0))
    return q_til, k_til, eq, ek


def _split2(x):
    hi = x.astype(BF16)
    return hi, (x - hi.astype(F32)).astype(BF16)


def hgrn_fwd(proj, lb, ride=None):
    s = proj.shape[0]
    c_len, nh, hd = HGRN_CHUNK, HGRN_HEADS, HGRN_HEAD_DIM
    n_chunks = s // c_len
    col0 = 0

    cps = 2 * HGRN_CHUNKS_PER_STEP
    n_steps = n_chunks // cps

    def body(q_ref, f_ref, i_ref, lb_ref, o_ref, st_out_ref, a_out_ref, st_ref, b_ref, qf_ref, kf_ref, a_ref):
        @pl.when(pl.program_id(0) == 0)
        def _():
            st_ref[...] = jnp.zeros_like(st_ref)

        lbv = _lower_bound(lb_ref[...])
        for u in range(cps):
            rs = slice(u * c_len, (u + 1) * c_len)
            b_u, qf_u, kf_u, a_u = b_ref.at[u], qf_ref.at[u], kf_ref.at[u], a_ref.at[u]
            _, qf, _, log_f, kf = _hgrn_gates(q_ref[rs, :], f_ref[rs, :], lbv)
            b = _tri_sum(_tril_bf16(c_len), log_f)
            b_u[...] = b
            qf_u[...] = qf
            kf_u[...] = kf
            a_u[...] = jnp.zeros_like(a_u)

            def diagonal(rows, j, key, b_u=b_u, qf_u=qf_u, kf_u=kf_u, a_u=a_u):
                bj = b_u[pl.ds(key, 1), :]
                kj = kf_u[pl.ds(key, 1), :]
                nrow = rows.stop - rows.start
                t_loc = lax.broadcasted_iota(jnp.int32, (nrow, nh * hd), 0) + (rows.start % SUB_BLOCK)
                e = jnp.exp(jnp.where(t_loc >= j, b_u[rows, :] - bj, NEG_BIG))
                prod = qf_u[rows, :] * kj * e
                lane = lax.broadcasted_iota(jnp.int32, (nrow, hd), 1)
                for h in range(nh):
                    col = jnp.sum(prod[:, h * hd:(h + 1) * hd], axis=-1, keepdims=True)
                    a_u[h, rows, :] = jnp.where(lane == key, col, a_u[h, rows, :])

            _hgrn_diagonal_loops(c_len, diagonal)
            q_til, k_til, _, _ = _hgrn_off_diagonal(b, qf, kf)
            q_til = q_til.astype(BF16)
            k_til = [k.astype(BF16) for k in k_til]

            b_last = b[c_len - 1:c_len, :]
            qb = (qf * jnp.exp(b)).astype(BF16)
            kb2 = (kf * jnp.exp(b_last - b)).astype(BF16)
            vf = i_ref[rs, :].astype(BF16)
            for h in range(nh):
                hs = slice(h * hd, (h + 1) * hd)
                st = st_ref[h]
                st_out_ref[u, h] = st
                off = [jnp.zeros((SUB_BLOCK, hd), F32)]
                for i in range(1, c_len // SUB_BLOCK):
                    off.append(_dot_nt(q_til[i * SUB_BLOCK:(i + 1) * SUB_BLOCK, hs], k_til[i - 1][:, hs]))
                a_h = a_u[h] + jnp.concatenate(off, axis=0)
                a_out_ref[rs, hs] = a_h
                o_ref[rs, hs] = _dot_nt(qb[:, hs], st.astype(BF16)) + _dot(a_h[:, :c_len].astype(BF16), vf[:, hs])
                st_ref[h] = st * jnp.exp(b_last[:, hs]) + _dot_tn(vf[:, hs], kb2[:, hs])

    blk = (cps * c_len, HGRN_WIDTH)
    sblk = (cps, c_len, HGRN_WIDTH)
    step = lambda k: (lambda: pl.program_id(0) == k)
    e_in, e_out, e_shape, e_scr, e_args = _ride_specs(ride)
    return pl.pallas_call(
        _riding(body, 4, 3, 5, ride, step(0), step((7 * n_steps) // 8), step(n_steps - 1)),
        name="hgrn_fwd",
        grid=(n_steps,),
        in_specs=[
            pl.BlockSpec(blk, lambda c: (c, col0)),
            pl.BlockSpec(blk, lambda c: (c, col0 + 1)),
            pl.BlockSpec(blk, lambda c: (c, col0 + 2)),
            pl.BlockSpec((2, HGRN_WIDTH), lambda c: (0, 0)),
        ] + e_in,
        out_specs=[
            pl.BlockSpec(blk, lambda c: (c, 0)),
            pl.BlockSpec((cps, nh, hd, hd), lambda c: (c, 0, 0, 0)),
            pl.BlockSpec(blk, lambda c: (c, 0)),
        ] + e_out,
        out_shape=[
            jax.ShapeDtypeStruct((s, HGRN_WIDTH), F32),
            jax.ShapeDtypeStruct((n_chunks, nh, hd, hd), F32),
            jax.ShapeDtypeStruct((s, nh * hd), F32),
        ] + e_shape,
        scratch_shapes=[
            pltpu.VMEM((nh, hd, hd), F32),
            pltpu.VMEM(sblk, F32),
            pltpu.VMEM(sblk, F32),
            pltpu.VMEM(sblk, F32),
            pltpu.VMEM((cps, nh, c_len, hd), F32),
        ] + e_scr,
        compiler_params=_params(dimension_semantics=("arbitrary",)),
    )(proj, proj, proj, lb, *e_args)


def hgrn_bwd(proj, lb, d_o, states, a_mat, ride=None):
    s = proj.shape[0]
    c_len, nh, hd = HGRN_CHUNK, HGRN_HEADS, HGRN_HEAD_DIM
    n_chunks = s // c_len
    col0 = 0
    cps = HGRN_CHUNKS_PER_STEP
    n_steps = n_chunks // cps
    last = n_steps - 1

    def body(q_ref, f_ref, i_ref, lb_ref, do_ref, st_in_ref, a_in_ref, dq_ref, df_ref, di_ref, dlb_ref,
             dst_ref, b_ref, qf_ref, kf_ref, da_ref, dqi_ref, dki_ref):
        @pl.when(pl.program_id(0) == 0)
        def _():
            dst_ref[...] = jnp.zeros_like(dst_ref)
            dlb_ref[...] = jnp.zeros_like(dlb_ref)

        lbv = _lower_bound(lb_ref[...])
        for u in reversed(range(cps)):
            rs = slice(u * c_len, (u + 1) * c_len)
            b_u, qf_u, kf_u, da_u, dqi_u, dki_u = (b_ref.at[u], qf_ref.at[u], kf_ref.at[u], da_ref.at[u], dqi_ref.at[u],
                                                   dki_ref.at[u])
            q = q_ref[rs, :]
            sq, qf, (sig, sig_neg, log_c), log_f, kf = _hgrn_gates(q, f_ref[rs, :], lbv)
            b = _tri_sum(_tril_bf16(c_len), log_f)
            b_u[...] = b
            qf_u[...] = qf
            kf_u[...] = kf
            b_last = b[c_len - 1:c_len, :]
            eb = jnp.exp(b)
            ebl = jnp.exp(b_last - b)
            qb = qf * eb
            kb2 = kf * ebl
            vf = i_ref[rs, :]
            d_o = do_ref[rs, :]
            qb_b, kb2_b, vf_b, do_b = qb.astype(BF16), kb2.astype(BF16), vf.astype(BF16), d_o.astype(BF16)
            tq = lax.broadcasted_iota(jnp.int32, (c_len, hd), 0)
            lane = lax.broadcasted_iota(jnp.int32, (c_len, hd), 1)

            dqb_parts, dvf_parts, dkb2_parts, dbl_parts = [], [], [], []
            for h in range(nh):
                hs = slice(h * hd, (h + 1) * hd)
                st = st_in_ref[u, h]
                dst = dst_ref[h]
                st_b, dst_b = st.astype(BF16), dst.astype(BF16)
                a_h = a_in_ref[rs, hs][:, :c_len].astype(BF16)
                dqb_parts.append(_dot(do_b[:, hs], st_b))
                dvf_parts.append(_dot_tn(a_h, do_b[:, hs]) + _dot_nt(kb2_b[:, hs], dst_b))
                dkb2_parts.append(_dot(vf_b[:, hs], dst_b))
                da = _dot_nt(do_b[:, hs], vf_b[:, hs])
                da = jnp.concatenate([da, jnp.zeros((c_len, hd - c_len), F32)], axis=1)
                da_u[h] = jnp.where(tq >= lane, da, 0.0)
                dbl_parts.append(jnp.sum(dst * st, axis=0, keepdims=True) * jnp.exp(b_last[:, hs]))
                dst_ref[h] = dst * jnp.exp(b_last[:, hs]) + _dot_tn(do_b[:, hs], qb_b[:, hs])
            dqb = jnp.concatenate(dqb_parts, axis=1)
            dvf = jnp.concatenate(dvf_parts, axis=1)
            dkb2 = jnp.concatenate(dkb2_parts, axis=1)
            dbl = jnp.concatenate(dbl_parts, axis=1) + jnp.sum(dkb2 * kb2, axis=0, keepdims=True)

            dqi_u[...] = jnp.zeros_like(dqi_u)
            t_idx = lax.broadcasted_iota(jnp.int32, (c_len, nh * hd), 0)

            def diagonal(rows, j, key, b_u=b_u, qf_u=qf_u, kf_u=kf_u, da_u=da_u, dqi_u=dqi_u, dki_u=dki_u):
                bj = b_u[pl.ds(key, 1), :]
                kj = kf_u[pl.ds(key, 1), :]
                nrow = rows.stop - rows.start
                t_loc = lax.broadcasted_iota(jnp.int32, (nrow, nh * hd), 0) + (rows.start % SUB_BLOCK)
                e = jnp.exp(jnp.where(t_loc >= j, b_u[rows, :] - bj, NEG_BIG))
                lane_r = lax.broadcasted_iota(jnp.int32, (nrow, hd), 1)
                cols = [jnp.sum(jnp.where(lane_r == key, da_u[h, rows, :], 0.0), axis=-1, keepdims=True)
                        for h in range(nh)]
                w = e * jnp.concatenate([jnp.broadcast_to(cc, (nrow, hd)) for cc in cols], axis=1)
                dqi_u[rows, :] += w * kj
                dki_u[pl.ds(key, 1), :] = jnp.sum(w * qf_u[rows, :], axis=0, keepdims=True)

            _hgrn_diagonal_loops(c_len, diagonal)

            q_til, k_til, eq, ek = _hgrn_off_diagonal(b, qf, kf)
            q_hi, q_lo = _split2(q_til)
            k_pairs = [_split2(k) for k in k_til]
            n_sub = c_len // SUB_BLOCK
            dq_heads, dk_heads = [], []
            for h in range(nh):
                hs = slice(h * hd, (h + 1) * hd)
                dq_rows = [jnp.zeros((SUB_BLOCK, hd), F32)]
                dk_h = jnp.zeros((c_len, hd), F32)
                for i in range(1, n_sub):
                    rows = slice(i * SUB_BLOCK, (i + 1) * SUB_BLOCK)
                    n = i * SUB_BLOCK
                    da_i = da_u[h, rows, :].astype(BF16)
                    k_hi, k_lo = k_pairs[i - 1]
                    dq_rows.append((_dot(da_i, k_hi[:, hs]) + _dot(da_i, k_lo[:, hs])) * eq[rows, hs])
                    dk_t = (_dot_tn(da_i, q_hi[rows, hs]) + _dot_tn(da_i, q_lo[rows, hs]))[:n, :] * ek[i - 1][:, hs]
                    dk_h = dk_h + jnp.concatenate([dk_t, jnp.zeros((c_len - n, hd), F32)], axis=0)
                dq_heads.append(jnp.concatenate(dq_rows, axis=0))
                dk_heads.append(dk_h)
            dq_intra = dqi_u[...] + jnp.concatenate(dq_heads, axis=1)
            dk_intra = dki_u[...] + jnp.concatenate(dk_heads, axis=1)

            db = dqb * qb + qf * dq_intra - kf * dk_intra - dkb2 * kb2
            db = db + jnp.where(t_idx == c_len - 1, dbl, 0.0)
            dg = _tri_sum(_tril_bf16(c_len, upper=True), db)
            dqf = dqb * eb + dq_intra
            dkf = dkb2 * ebl + dk_intra
            dq_ref[rs, :] = (dqf * (sq * (1.0 + q * (1.0 - sq)))).astype(BF16)
            df_ref[rs, :] = (sig_neg * (dg * jnp.exp(log_c - log_f) - dkf * (1.0 - lbv) * sig)).astype(BF16)
            di_ref[rs, :] = dvf.astype(BF16)
            dlb_ref[...] += jnp.sum(sig_neg * (dg * jnp.exp(-log_f) - dkf), axis=0, keepdims=True)

    blk = (cps * c_len, HGRN_WIDTH)
    sblk = (cps, c_len, HGRN_WIDTH)
    rev = lambda c: last - c
    step = lambda k: (lambda: pl.program_id(0) == k)
    e_in, e_out, e_shape, e_scr, e_args = _ride_specs(ride)
    return pl.pallas_call(
        _riding(body, 7, 4, 7, ride, step(0), step(n_steps // 2), step(last)),
        name="hgrn_bwd",
        grid=(n_steps,),
        in_specs=[
            pl.BlockSpec(blk, lambda c: (rev(c), col0)),
            pl.BlockSpec(blk, lambda c: (rev(c), col0 + 1)),
            pl.BlockSpec(blk, lambda c: (rev(c), col0 + 2)),
            pl.BlockSpec((2, HGRN_WIDTH), lambda c: (0, 0)),
            pl.BlockSpec(blk, lambda c: (rev(c), 0)),
            pl.BlockSpec((cps, nh, hd, hd), lambda c: (rev(c), 0, 0, 0)),
            pl.BlockSpec(blk, lambda c: (rev(c), 0)),
        ] + e_in,
        out_specs=[
            pl.BlockSpec(blk, lambda c: (rev(c), 0)),
            pl.BlockSpec(blk, lambda c: (rev(c), 0)),
            pl.BlockSpec(blk, lambda c: (rev(c), 0)),
            pl.BlockSpec((1, HGRN_WIDTH), lambda c: (0, 0)),
        ] + e_out,
        out_shape=[jax.ShapeDtypeStruct((s, HGRN_WIDTH), BF16)] * 3 + [jax.ShapeDtypeStruct((1, HGRN_WIDTH), F32)] + e_shape,
        scratch_shapes=[
            pltpu.VMEM((nh, hd, hd), F32),
            pltpu.VMEM(sblk, F32),
            pltpu.VMEM(sblk, F32),
            pltpu.VMEM(sblk, F32),
            pltpu.VMEM((cps, nh, c_len, hd), F32),
            pltpu.VMEM(sblk, F32),
            pltpu.VMEM(sblk, F32),
        ] + e_scr,
        compiler_params=_params(dimension_semantics=("arbitrary",)),
    )(proj, proj, proj, lb, d_o, states, a_mat, *e_args)


def _row_spec(tm, width, col=0):
    return pl.BlockSpec((tm, width), lambda i: (i, col))


def _const_spec(width):
    return pl.BlockSpec((1, width), lambda i: (0, 0))


def _acc_rows(ref, value):
    @pl.when(pl.program_id(0) == 0)
    def _():
        ref[...] = jnp.zeros_like(ref)

    ref[...] += jnp.sum(value, axis=0, keepdims=True)


def mix_fwd(attn_parts, o_h, proj, an, hn, w_out_b, gp, x, ride=None):
    s = x.shape[0]
    tm = TOKEN_TILE
    gate_col = 3
    hd = HGRN_HEAD_DIM
    nd = len(DILATIONS)

    def body(*refs):
        o_refs, l_refs = refs[:nd], refs[nd:2 * nd]
        oh_ref, gate_ref, an_ref, hn_ref, w_ref, gp_ref, x_ref = refs[2 * nd:2 * nd + 7]
        x1_ref, cat_ref, mixed_ref, attn_ref = refs[2 * nd + 7:2 * nd + 11]
        lse_refs = refs[2 * nd + 11:3 * nd + 11]
        o_scr, l_scr, lse_scr = refs[3 * nd + 11:]
        os_ = [_from_dilated(r, o_scr.at[k], d, tm) for k, (r, d) in enumerate(zip(o_refs, DILATIONS))]
        ls = [_from_dilated(r, l_scr.at[k], d, tm) for k, (r, d) in enumerate(zip(l_refs, DILATIONS))]
        m = jnp.maximum(jnp.maximum(ls[0], ls[1]), ls[2])
        es = [jnp.exp(l - m) for l in ls]
        den = es[0] + es[1] + es[2]
        attn = (es[0] * os_[0] + es[1] * os_[1] + es[2] * os_[2]) / den
        attn_ref[...] = attn
        _lane_blocks(lse_scr, m + jnp.log(den))
        for d, ref in zip(DILATIONS, lse_refs):
            _to_dilated(lse_scr, ref, d, tm)
        cat_ref[:, :ATTN_WIDTH] = _rms_fwd(attn, an_ref[...], ATTN_WIDTH).astype(BF16)
        gate = gate_ref[...]
        silu_g = gate * _sigmoid(gate)
        for h in range(HGRN_HEADS):
            hs = slice(h * hd, (h + 1) * hd)
            rec = _rms_fwd(oh_ref[:, hs], hn_ref[:, hs], hd) * silu_g[:, hs]
            cat_ref[:, ATTN_WIDTH + h * hd:ATTN_WIDTH + (h + 1) * hd] = rec.astype(BF16)
        mixed = _dot(cat_ref[...], w_ref[...])
        mixed_ref[...] = mixed
        x1_ref[...] = x_ref[...] + _rms_fwd(mixed, gp_ref[...], D_MODEL)

    aw = ATTN_WIDTH
    n_steps = s // tm
    step = lambda k: (lambda: pl.program_id(0) == k)
    e_in, e_out, e_shape, e_scr, e_args = _ride_specs(ride)
    return pl.pallas_call(
        _riding(body, 2 * nd + 7, 4 + nd, 3, ride, step(0), step((13 * n_steps) // 16), step(n_steps - 1)),
        name="mix_fwd",
        grid=(n_steps,),
        in_specs=[_dilated_spec(d, tm, aw) for d in DILATIONS] * 2 + [
            _row_spec(tm, aw), _row_spec(tm, aw, gate_col), _const_spec(aw), _const_spec(aw), _vmem_spec(),
            _const_spec(D_MODEL), _row_spec(tm, D_MODEL)] + e_in,
        out_specs=[_row_spec(tm, D_MODEL), _row_spec(tm, D_MODEL), _row_spec(tm, D_MODEL), _row_spec(tm, aw)] + [
            _dilated_spec(d, tm, aw) for d in DILATIONS] + e_out,
        out_shape=[
            jax.ShapeDtypeStruct((s, D_MODEL), F32),
            jax.ShapeDtypeStruct((s, D_MODEL), BF16),
            jax.ShapeDtypeStruct((s, D_MODEL), F32),
            jax.ShapeDtypeStruct((s, aw), F32),
        ] + [jax.ShapeDtypeStruct((d, s // d, aw), F32) for d in DILATIONS] + e_shape,
        scratch_shapes=[pltpu.VMEM((nd, aw // LANES, tm, LANES), F32), pltpu.VMEM((nd, aw // LANES, tm, LANES), F32),
                        pltpu.VMEM((aw // LANES, tm, LANES), F32)] + e_scr,
        compiler_params=_params(dimension_semantics=("arbitrary",)),
    )(*[p[0] for p in attn_parts], *[p[1] for p in attn_parts], o_h, proj, an, hn, w_out_b, gp, x, *e_args)


def mix_bwd(dx1, mixed, gp, w_out_b, attn, an, o_h, proj, hn):
    s = dx1.shape[0]
    tm = TOKEN_TILE
    gate_col = 3
    hd = HGRN_HEAD_DIM
    aw = ATTN_WIDTH

    nd = len(DILATIONS)

    def body(*refs):
        dx1_ref, mixed_ref, gp_ref, w_ref, attn_ref, an_ref, oh_ref, gate_ref, hn_ref, dmix_ref = refs[:10]
        do_refs, delta_refs = refs[10:10 + nd], refs[10 + nd:10 + 2 * nd]
        doh_ref, dgate_ref, dgp_ref, dan_ref, dhn_ref, do_ref, delta_ref = refs[10 + 2 * nd:]
        dmixed, gp_c = _rms_bwd(dx1_ref[...], mixed_ref[...], gp_ref[...], D_MODEL)
        _acc_rows(dgp_ref, gp_c)
        dmixed_b = dmixed.astype(BF16)
        dmix_ref[...] = dmixed_b
        dcat = _dot_nt(dmixed_b, w_ref[...])
        attn = attn_ref[...]
        d_o, an_c = _rms_bwd(dcat[:, :aw], attn, an_ref[...], aw)
        _acc_rows(dan_ref, an_c)
        _lane_blocks(do_ref, d_o)
        prod = d_o * attn
        for pair in range(ATTN_HEADS // 2):
            pp = prod[:, pair * LANES:(pair + 1) * LANES]
            low = _lane_half((tm, LANES), 0)
            lo = jnp.sum(jnp.where(low, pp, 0.0), axis=-1, keepdims=True)
            hi = jnp.sum(jnp.where(low, 0.0, pp), axis=-1, keepdims=True)
            delta_ref[pair] = jnp.where(low, lo, hi)
        for d, o_ref, l_ref in zip(DILATIONS, do_refs, delta_refs):
            _to_dilated(do_ref, o_ref, d, tm, cast=BF16)
            _to_dilated(delta_ref, l_ref, d, tm)
        gate = gate_ref[...]
        sg = _sigmoid(gate)
        silu_g = gate * sg
        drec = dcat[:, aw:]
        hn_parts = []
        for h in range(HGRN_HEADS):
            hs = slice(h * hd, (h + 1) * hd)
            oh = oh_ref[:, hs]
            on = _rms_fwd(oh, hn_ref[:, hs], hd)
            dgate_ref[:, hs] = (drec[:, hs] * on * (sg[:, hs] * (1.0 + gate[:, hs] * (1.0 - sg[:, hs])))).astype(BF16)
            d_oh, hn_c = _rms_bwd(drec[:, hs] * silu_g[:, hs], oh, hn_ref[:, hs], hd)
            doh_ref[:, hs] = d_oh
            hn_parts.append(hn_c)
        _acc_rows(dhn_ref, jnp.concatenate(hn_parts, axis=1))

    return pl.pallas_call(
        body,
        name="mix_bwd",
        grid=(s // tm,),
        in_specs=[_row_spec(tm, D_MODEL), _row_spec(tm, D_MODEL), _const_spec(D_MODEL), _vmem_spec(), _row_spec(tm, aw),
                  _const_spec(aw), _row_spec(tm, aw), _row_spec(tm, aw, gate_col), _const_spec(aw)],
        out_specs=[_row_spec(tm, D_MODEL)] + [_dilated_spec(d, tm, aw) for d in DILATIONS] * 2 + [_row_spec(tm, aw)] * 2 + [
            _const_spec(D_MODEL), _const_spec(aw), _const_spec(aw)],
        out_shape=[jax.ShapeDtypeStruct((s, D_MODEL), BF16)] + [
            jax.ShapeDtypeStruct((d, s // d, aw), BF16) for d in DILATIONS] + [
            jax.ShapeDtypeStruct((d, s // d, aw), F32) for d in DILATIONS] + [
            jax.ShapeDtypeStruct((s, aw), F32), jax.ShapeDtypeStruct((s, aw), BF16),
            jax.ShapeDtypeStruct((1, D_MODEL), F32), jax.ShapeDtypeStruct((1, aw), F32),
            jax.ShapeDtypeStruct((1, aw), F32)],
        scratch_shapes=[pltpu.VMEM((aw // LANES, tm, LANES), F32), pltpu.VMEM((aw // LANES, tm, LANES), F32)],
        compiler_params=_params(dimension_semantics=("arbitrary",)),
    )(dx1, mixed, gp, w_out_b, attn, an, o_h, proj, hn)


def mlp_fwd_bwd(x1, g_pre, w1_blocks, w2_b, g_post, target):
    s = x1.shape[0]
    tm = MLP_TILE
    nblk, _, fb = w1_blocks.shape

    def body(x1_ref, gpre_ref, w1_ref, w2_ref, gpost_ref, t_ref,
             dx1_ref, h2_ref, a_ref, du_ref, dff_ref, loss_ref, dgpre_ref, dgpost_ref, u_ref):
        x1v = x1_ref[...]
        h2 = _rms_fwd(x1v, gpre_ref[...], D_MODEL).astype(BF16)
        h2_ref[...] = h2
        ff = jnp.zeros((tm, D_MODEL), F32)
        for j in range(nblk):
            cols = slice(j * fb, (j + 1) * fb)
            ru = jnp.maximum(_dot(h2, w1_ref[j]), 0.0)
            u_ref[:, cols] = ru.astype(BF16)
            a = (ru * ru).astype(BF16)
            a_ref[:, cols] = a
            ff = ff + _dot(a, w2_ref[cols, :])
        diff = x1v + _rms_fwd(ff, gpost_ref[...], D_MODEL) - t_ref[...]
        _acc_rows(loss_ref, diff * diff)
        dy = diff * (1.0 / D_MODEL)
        dff, gpost_c = _rms_bwd(dy, ff, gpost_ref[...], D_MODEL)
        _acc_rows(dgpost_ref, gpost_c)
        dff_b = dff.astype(BF16)
        dff_ref[...] = dff_b
        dh2 = jnp.zeros((tm, D_MODEL), F32)
        for j in range(nblk):
            cols = slice(j * fb, (j + 1) * fb)
            du = (_dot_nt(dff_b, w2_ref[cols, :]) * (2.0 * u_ref[:, cols])).astype(BF16)
            du_ref[:, cols] = du
            dh2 = dh2 + _dot_nt(du, w1_ref[j])
        dxa, gpre_c = _rms_bwd(dh2, x1v, gpre_ref[...], D_MODEL)
        _acc_rows(dgpre_ref, gpre_c)
        dx1_ref[...] = dy + dxa

    dm = D_MODEL
    return pl.pallas_call(
        body,
        name="mlp_fwd_bwd",
        grid=(s // tm,),
        in_specs=[_row_spec(tm, dm), _const_spec(dm), _vmem_spec(), _vmem_spec(), _const_spec(dm), _row_spec(tm, dm)],
        out_specs=[_row_spec(tm, dm), _row_spec(tm, dm), _row_spec(tm, D_FF), _row_spec(tm, D_FF), _row_spec(tm, dm),
                   _const_spec(dm), _const_spec(dm), _const_spec(dm)],
        out_shape=[
            jax.ShapeDtypeStruct((s, dm), F32),
            jax.ShapeDtypeStruct((s, dm), BF16),
            jax.ShapeDtypeStruct((s, D_FF), BF16),
            jax.ShapeDtypeStruct((s, D_FF), BF16),
            jax.ShapeDtypeStruct((s, dm), BF16),
            jax.ShapeDtypeStruct((1, dm), F32),
            jax.ShapeDtypeStruct((1, dm), F32),
            jax.ShapeDtypeStruct((1, dm), F32),
        ],
        scratch_shapes=[pltpu.VMEM((tm, D_FF), BF16)],
        compiler_params=_params(dimension_semantics=("arbitrary",)),
    )(x1, g_pre, w1_blocks, w2_b, g_post, target)


def in_proj_bwd(attn_grads, hgrn_grads, dgate, w_in_b, x, g1, dx1):
    s = x.shape[0]
    tm = PROJ_TILE
    aw = ATTN_WIDTH
    n_attn = len(attn_grads)
    flat = [g[k] for k in range(3) for g in attn_grads] + list(hgrn_grads) + [dgate]

    def body(*refs):
        parts = refs[:len(flat)]
        w_ref, x_ref, g_ref, dx1_ref, dx_ref, dproj_ref, dg_ref, scr = refs[len(flat):]
        groups = []
        for k in range(3):
            acc = None
            for p, d in zip(parts[k * n_attn:(k + 1) * n_attn], DILATIONS):
                v = _from_dilated(p, scr, d, tm)
                acc = v if acc is None else acc + v
            groups.append(acc)
        groups += [p[...] for p in parts[3 * n_attn:]]
        dh = jnp.zeros((tm, D_MODEL), F32)
        for gi, grp in enumerate(groups):
            cols = slice(gi * aw, (gi + 1) * aw)
            gb = grp.astype(BF16)
            dproj_ref[:, cols] = gb
            dh = dh + _dot_nt(gb, w_ref[:, cols])
        dxa, g_c = _rms_bwd(dh, x_ref[...], g_ref[...], D_MODEL)
        _acc_rows(dg_ref, g_c)
        dx_ref[...] = dx1_ref[...] + dxa

    dm = D_MODEL
    return pl.pallas_call(
        body,
        name="in_proj_bwd",
        grid=(s // tm,),
        in_specs=[_dilated_spec(d, tm, aw) for d in DILATIONS] * 3 + [_row_spec(tm, aw)] * 4 + [
            _vmem_spec(), _row_spec(tm, dm), _const_spec(dm), _row_spec(tm, dm)],
        out_specs=[_row_spec(tm, dm), _row_spec(tm, IN_PROJ_WIDTH), _const_spec(dm)],
        out_shape=[jax.ShapeDtypeStruct((s, dm), F32), jax.ShapeDtypeStruct((s, IN_PROJ_WIDTH), BF16),
                   jax.ShapeDtypeStruct((1, dm), F32)],
        scratch_shapes=[pltpu.VMEM((aw // LANES, tm, LANES), F32)],
        compiler_params=_params(dimension_semantics=("arbitrary",)),
    )(*flat, w_in_b, x, g1, dx1)


def wgrad(a_b, b_b, tn, name, ts=1024, per_step=1):
    s, k = a_b.shape
    n = b_b.shape[1]

    def body(a_ref, b_ref, o_ref):
        @pl.when(pl.program_id(1) == 0)
        def _():
            o_ref[...] = jnp.zeros_like(o_ref)

        a = a_ref[...]
        for jj in range(per_step):
            o_ref[jj] += _dot_tn(a, b_ref[:, jj * tn:(jj + 1) * tn])

    wide = tn * per_step
    return pl.pallas_call(
        body,
        name=name,
        grid=(n // wide, s // ts),
        in_specs=[pl.BlockSpec((ts, k), lambda j, i: (i, 0)), pl.BlockSpec((ts, wide), lambda j, i: (i, j))],
        out_specs=pl.BlockSpec((per_step, k, tn), lambda j, i: (j, 0, 0)),
        out_shape=jax.ShapeDtypeStruct((n // tn, k, tn), F32),
        compiler_params=_params(dimension_semantics=("arbitrary", "arbitrary")),
    )(a_b, b_b)


def train_step(x, target, g1, an, logits, hn, gp, g_pre, g_post, w, m, v):
    nd = len(DILATIONS)
    shard_b = {k: w[k].astype(BF16) for k in BIG}
    (w_in_g,) = run_exchange(gather_exchange([shard_b["w_in"]]), "gather_w_in")
    w_in_b = w_in_g.transpose(1, 0, 2).reshape(D_MODEL, IN_PROJ_WIDTH)

    proj, h_b, *qkvs = in_proj_fwd(x, g1, w_in_b)
    attn_parts = [attn_fwd(qkv, d) for qkv, d in zip(qkvs, DILATIONS)]
    o_h, states, a_mat, w_out_g, w1_blocks = hgrn_fwd(
        proj, logits, ride=gather_exchange([shard_b["w_out"], shard_b["w_ff1"]]))
    w_out_b = w_out_g.reshape(D_MODEL, D_MODEL)
    x1, cat_b, mixed, attn, *lses, w2_g = mix_fwd(attn_parts, o_h, proj, an, hn, w_out_b, gp, x,
                                                  ride=gather_exchange([shard_b["w_ff2"]]))
    w2_b = w2_g.reshape(D_FF, D_MODEL)
    dx1, h2_b, a_b, du_b, dff_b, loss_vec, dg_pre, dg_post = mlp_fwd_bwd(x1, g_pre, w1_blocks, w2_b, g_post, target)
    dw2 = wgrad(a_b, dff_b, D_MODEL, "wgrad_ff2", ts=512)
    dw1 = wgrad(h2_b, du_b, D_FF // N_DEV, "wgrad_ff1", per_step=2)
    dmix_b, *rest = mix_bwd(dx1, mixed, gp, w_out_b, attn, an, o_h, proj, hn)
    d_os, deltas = rest[:nd], rest[nd:2 * nd]
    d_oh, dgate, dgp, dan, dhn = rest[2 * nd:]
    dwout = wgrad(cat_b, dmix_b, D_MODEL, "wgrad_out")

    early = ("w_out", "w_ff1", "w_ff2")
    early_grads = [dwout.reshape(N_DEV, D_MODEL // N_DEV, D_MODEL), dw1, dw2.reshape(N_DEV, D_FF // N_DEV, D_MODEL)]
    res = attn_bwd(qkvs[0], d_os[0], lses[0], deltas[0], DILATIONS[0], ride=to_core_exchange(early_grads))
    pairs = [pair_sum(g, s, f"pair_sum_{name}") for g, s, name in zip(early_grads, res[3:], early)]
    attn_grads = [res[:3]]
    *res, others_ff2 = attn_bwd(qkvs[1], d_os[1], lses[1], deltas[1], DILATIONS[1],
                                ride=to_chip_exchange([pairs[2][1]]))
    attn_grads.append(res)
    attn_grads.append(attn_bwd(qkvs[2], d_os[2], lses[2], deltas[2], DILATIONS[2]))
    dq_h, df_h, di_h, dlb, *others = hgrn_bwd(proj, logits, d_oh, states, a_mat,
                                              ride=to_chip_exchange([pairs[0][1], pairs[1][1]]))
    others.append(others_ff2)
    dx, dproj_b, dg1 = in_proj_bwd(attn_grads, (dq_h, df_h, di_h), dgate, w_in_b, x, g1, dx1)
    dwin = wgrad(h_b, dproj_b, 2 * IN_PROJ_WIDTH // N_DEV, "wgrad_in")
    big = {name: sum_adamw(p[0], o, w[name], m[name], v[name], f"sum_adamw_{name}")
           for name, p, o in zip(early, pairs, others)}

    shard_w = IN_PROJ_WIDTH // N_DEV
    dwin_blocks = dwin.reshape(N_DEV // 2, D_MODEL, 2, shard_w).transpose(0, 2, 1, 3).reshape(N_DEV, D_MODEL, shard_w)
    (from_sibling,) = run_exchange(to_core_exchange([dwin_blocks]), "reduce_w_in_to_core")
    pair_in, pair_in_b = pair_sum(dwin_blocks, from_sibling, "pair_sum_w_in")
    (others_in,) = run_exchange(to_chip_exchange([pair_in_b]), "reduce_w_in_to_chip")
    big["w_in"] = sum_adamw(pair_in, others_in, w["w_in"], m["w_in"], v["w_in"], "sum_adamw_w_in")
    small = dict(dg1=dg1, dan=dan, dlb=dlb, dhn=dhn, dgp=dgp, dg_pre=dg_pre, dg_post=dg_post, loss_vec=loss_vec)
    return dx, big, small


def _position():
    x, y, c = lax.axis_index("x"), lax.axis_index("y"), lax.axis_index("c")
    other_chips = [(1 - x, y), (x, 1 - y), (1 - x, 1 - y)]
    return x, y, c, other_chips


def _any_spec():
    return pl.BlockSpec(memory_space=pl.ANY)


class Exchange:
    def __init__(self, arrays, out_shape, sems, stages):
        self.arrays, self.out_shape, self.sems, self.stages = list(arrays), list(out_shape), list(sems), stages


def gather_exchange(shards):
    n = len(shards)

    def stages(ins, outs, sems):
        send_sems, recv_sems, local_sems = sems

        def parts():
            x, y, c, chips = _position()
            me, sibling = (x, y, c), (x, y, 1 - c)

            def slot(a, px, py, pc):
                return outs[a].at[4 * px + 2 * py + pc]

            def copy(a, k, block, to, src=None):
                return pltpu.make_async_remote_copy(
                    src_ref=slot(a, *block) if src is None else src, dst_ref=slot(a, *block),
                    send_sem=send_sems.at[a, k], recv_sem=recv_sems.at[a, k], device_id=to, device_id_type=MESH)

            local = [pltpu.make_async_copy(ins[a], slot(a, *me), local_sems.at[a]) for a in range(n)]
            first = []
            for a in range(n):
                first.append(copy(a, 0, me, sibling, src=ins[a]))
                first += [copy(a, 1 + j, me, (*chip, c), src=ins[a]) for j, chip in enumerate(chips)]
            passed = [copy(a, 4 + j, (*chip, c), sibling) for j, chip in enumerate(chips) for a in range(n)]
            return c, chips, me, sibling, copy, local, first, passed

        def begin():
            _, _, _, _, _, local, first, _ = parts()
            for cp in local + first:
                cp.start()

        def middle():
            c, chips, me, _, copy, _, _, passed = parts()
            k = 0
            for j, chip in enumerate(chips):
                for a in range(n):
                    copy(a, 1 + j, (*chip, c), me).wait_recv()
                    passed[k].start()
                    k += 1

        def end():
            c, chips, me, sibling, copy, local, first, passed = parts()
            for a in range(n):
                copy(a, 0, sibling, me).wait_recv()
                for j, chip in enumerate(chips):
                    copy(a, 4 + j, (*chip, 1 - c), me).wait_recv()
            for cp in first + passed:
                cp.wait_send()
            for cp in local:
                cp.wait()

        return begin, middle, end

    return Exchange(
        shards, [jax.ShapeDtypeStruct((N_DEV,) + sh.shape, sh.dtype) for sh in shards],
        [pltpu.SemaphoreType.DMA((n, 7)), pltpu.SemaphoreType.DMA((n, 7)), pltpu.SemaphoreType.DMA((n,))], stages)


def to_core_exchange(grads):
    n = len(grads)

    def stages(ins, outs, sems):
        send_sems, recv_sems = sems

        def copies():
            x, y, c, _ = _position()
            return [pltpu.make_async_remote_copy(
                src_ref=ins[a].at[2 * q + (1 - c)], dst_ref=outs[a].at[q], send_sem=send_sems.at[a, q],
                recv_sem=recv_sems.at[a, q], device_id=(x, y, 1 - c), device_id_type=MESH)
                for a in range(n) for q in range(4)]

        def begin():
            for cp in copies():
                cp.start()

        def end():
            for cp in copies():
                cp.wait()

        return begin, None, end

    return Exchange(grads, [jax.ShapeDtypeStruct((4,) + g.shape[1:], g.dtype) for g in grads],
                    [pltpu.SemaphoreType.DMA((n, 4)), pltpu.SemaphoreType.DMA((n, 4))], stages)


def pair_sum(grad, from_sibling, name):
    _, r, cdim = grad.shape
    tr = min(r, ELEMENTWISE_ROWS)
    c_idx = lax.axis_index("c").astype(jnp.int32).reshape(1)

    def body(c_ref, g_ref, s_ref, o_ref, ob_ref):
        total = g_ref[...] + s_ref[...]
        o_ref[...] = total
        ob_ref[...] = total.astype(BF16)

    blk = lambda: pl.BlockSpec((1, tr, cdim), lambda q, i, cr: (q, i, 0))
    return pl.pallas_call(
        body,
        name=name,
        grid_spec=pltpu.PrefetchScalarGridSpec(
            num_scalar_prefetch=1,
            grid=(4, r // tr),
            in_specs=[pl.BlockSpec((1, tr, cdim), lambda q, i, cr: (2 * q + cr[0], i, 0)), blk()],
            out_specs=[blk(), blk()],
        ),
        out_shape=[jax.ShapeDtypeStruct((4, r, cdim), F32), jax.ShapeDtypeStruct((4, r, cdim), BF16)],
        compiler_params=_params(dimension_semantics=("arbitrary", "arbitrary")),
    )(c_idx, grad, from_sibling)


def to_chip_exchange(pairs):
    n = len(pairs)

    def stages(ins, outs, sems):
        send_sems, recv_sems = sems

        def copies():
            x, y, c, chips = _position()
            return [pltpu.make_async_remote_copy(
                src_ref=ins[a].at[2 * px + py], dst_ref=outs[a].at[j], send_sem=send_sems.at[a, j],
                recv_sem=recv_sems.at[a, j], device_id=(px, py, c), device_id_type=MESH)
                for a in range(n) for j, (px, py) in enumerate(chips)]

        def begin():
            for cp in copies():
                cp.start()

        def end():
            for cp in copies():
                cp.wait()

        return begin, None, end

    return Exchange(pairs, [jax.ShapeDtypeStruct((3,) + p.shape[1:], p.dtype) for p in pairs],
                    [pltpu.SemaphoreType.DMA((n, 3)), pltpu.SemaphoreType.DMA((n, 3))], stages)


def run_exchange(ex, name):
    n_in, n_out = len(ex.arrays), len(ex.out_shape)

    def body(*refs):
        begin, middle, end = ex.stages(refs[:n_in], refs[n_in:n_in + n_out], refs[n_in + n_out:])
        begin()
        if middle is not None:
            middle()
        end()

    return pl.pallas_call(
        body,
        name=name,
        in_specs=[_any_spec()] * n_in,
        out_specs=[_any_spec()] * n_out,
        out_shape=ex.out_shape,
        scratch_shapes=ex.sems,
    )(*ex.arrays)


def _riding(body, n_in, n_out, n_scratch, ex, first, middle, last):
    if ex is None:
        return body
    r_in, r_out = len(ex.arrays), len(ex.out_shape)

    def wrapped(*refs):
        k_in, refs = refs[:n_in], refs[n_in:]
        e_in, refs = refs[:r_in], refs[r_in:]
        k_out, refs = refs[:n_out], refs[n_out:]
        e_out, refs = refs[:r_out], refs[r_out:]
        k_scr, e_sems = refs[:n_scratch], refs[n_scratch:]
        begin, mid, end = ex.stages(e_in, e_out, e_sems)
        pl.when(first())(begin)
        body(*k_in, *k_out, *k_scr)
        if mid is not None:
            pl.when(middle())(mid)
        pl.when(last())(end)

    return wrapped


def _ride_specs(ex):
    if ex is None:
        return [], [], [], [], []
    return [_any_spec()] * len(ex.arrays), [_any_spec()] * len(ex.out_shape), ex.out_shape, ex.sems, ex.arrays


def _adamw(w, g, m, v):
    m = ADAM_B1 * m + (1.0 - ADAM_B1) * g
    v = ADAM_B2 * v + (1.0 - ADAM_B2) * (g * g)
    m_hat = m / (1.0 - ADAM_B1 ** ADAM_STEP)
    v_hat = v / (1.0 - ADAM_B2 ** ADAM_STEP)
    delta = -ADAM_LR * (m_hat / (jnp.sqrt(v_hat) + ADAM_EPS) + ADAM_WD * w)
    return delta, m, v


def sum_adamw(pairs, others, w, m, v, name):
    r, cdim = w.shape
    tr = min(r, ELEMENTWISE_ROWS // 2)
    chip_idx =(2 * lax.axis_index("x") + lax.axis_index("y")).astype(jnp.int32).reshape(1)

    def body(q_ref, p_ref, o_ref, w_ref, m_ref, v_ref, g_out, d_out, m_out, v_out):
        g = p_ref[0] + o_ref[0].astype(F32) + o_ref[1].astype(F32) + o_ref[2].astype(F32)
        g_out[...] = g
        d_out[...], m_out[...], v_out[...] = _adamw(w_ref[...], g, m_ref[...], v_ref[...])

    tile = lambda: pl.BlockSpec((tr, cdim), lambda i, qr: (i, 0))
    return pl.pallas_call(
        body,
        name=name,
        grid_spec=pltpu.PrefetchScalarGridSpec(
            num_scalar_prefetch=1,
            grid=(r // tr,),
            in_specs=[pl.BlockSpec((1, tr, cdim), lambda i, qr: (qr[0], i, 0)),
                      pl.BlockSpec((3, tr, cdim), lambda i, qr: (0, i, 0)), tile(), tile(), tile()],
            out_specs=[tile(), tile(), tile(), tile()],
        ),
        out_shape=[jax.ShapeDtypeStruct((r, cdim), F32)] * 4,
        compiler_params=_params(dimension_semantics=("arbitrary",)),
    )(chip_idx, pairs, others, w, m, v)


def small_all_reduce(packed):
    shape = packed.shape

    def body(in_ref, out_ref, recv_ref, send_sems, recv_sems):
        x, y, c, _ = _position()
        my_id = 4 * x + 2 * y + c
        recv_ref[my_id] = in_ref[...]
        copies = []
        for rel in range(1, N_DEV):
            fx, fy, fc = (rel >> 2) & 1, (rel >> 1) & 1, rel & 1
            px = 1 - x if fx else x
            py = 1 - y if fy else y
            pc = 1 - c if fc else c
            cp = pltpu.make_async_remote_copy(
                src_ref=in_ref, dst_ref=recv_ref.at[my_id], send_sem=send_sems.at[rel - 1],
                recv_sem=recv_sems.at[rel - 1], device_id=(px, py, pc), device_id_type=MESH)
            cp.start()
            copies.append((cp, pltpu.make_async_remote_copy(
                src_ref=in_ref, dst_ref=recv_ref.at[4 * px + 2 * py + pc], send_sem=send_sems.at[rel - 1],
                recv_sem=recv_sems.at[rel - 1], device_id=(px, py, pc), device_id_type=MESH)))
        for cp, landing in copies:
            landing.wait_recv()
        for cp, landing in copies:
            cp.wait_send()
        total = recv_ref[0]
        for k in range(1, N_DEV):
            total = total + recv_ref[k]
        out_ref[...] = total

    return pl.pallas_call(
        body,
        name="small_all_reduce",
        in_specs=[_vmem_spec()],
        out_specs=_vmem_spec(),
        out_shape=jax.ShapeDtypeStruct(shape, F32),
        scratch_shapes=[pltpu.VMEM((N_DEV,) + shape, F32), pltpu.SemaphoreType.DMA((N_DEV - 1,)),
                        pltpu.SemaphoreType.DMA((N_DEV - 1,))],
    )(packed)


def small_adamw(reduced, w, m, v):
    def body(r_ref, w_ref, m_ref, v_ref, g_out, d_out, m_out, v_out, loss_out):
        red = r_ref[...]
        wv = w_ref[...]
        lb = _lower_bound(jnp.concatenate([wv[5:6, :HGRN_WIDTH], wv[5:6, HGRN_WIDTH:]], axis=0))
        t = red[5:6, :HGRN_WIDTH] * lb * (1.0 - lb)
        row = lax.broadcasted_iota(jnp.int32, red.shape, 0)
        g = jnp.where(row == 5, jnp.concatenate([t, -t], axis=1), jnp.where(row >= 6, 0.0, red))
        g_out[...] = g
        d_out[...], m_out[...], v_out[...] = _adamw(wv, g, m_ref[...], v_ref[...])
        loss = jnp.sum(red[6:7, :], axis=-1, keepdims=True) * (0.5 / D_MODEL)
        loss_out[...] = jnp.broadcast_to(loss, loss_out.shape)

    return pl.pallas_call(
        body,
        name="small_adamw",
        in_specs=[_vmem_spec()] * 4,
        out_specs=[_vmem_spec()] * 5,
        out_shape=[jax.ShapeDtypeStruct(reduced.shape, F32)] * 4 + [jax.ShapeDtypeStruct((SUBLANES, LANES), F32)],
    )(reduced, w, m, v)


def _pack_small(g1, gp, g_pre, g_post, an, hn, logits_or_dlb, extra=None):
    row5 = logits_or_dlb.reshape(1, -1)
    row5 = jnp.pad(row5, ((0, 0), (0, D_MODEL - row5.shape[1])))
    row6 = jnp.zeros((1, D_MODEL), F32) if extra is None else extra
    return jnp.concatenate([g1, gp, g_pre, g_post, jnp.concatenate([an, hn], axis=1), row5, row6,
                            jnp.zeros((1, D_MODEL), F32)], axis=0)


def _unpack_small(p):
    return dict(mix_pre_norm=p[0:1], mix_post_norm=p[1:2], mlp_pre_norm=p[2:3], mlp_post_norm=p[3:4],
                attn_out_norm=p[4:5, :ATTN_WIDTH], hgrn_out_norm=p[4:5, ATTN_WIDTH:],
                hgrn_lb_logits=p[5].reshape(2, HGRN_WIDTH))


BIG = ("w_in", "w_out", "w_ff1", "w_ff2")
ORDER = ("mix_pre_norm", "w_in", "attn_out_norm", "hgrn_lb_logits", "hgrn_out_norm", "w_out", "mix_post_norm",
         "mlp_pre_norm", "w_ff1", "w_ff2", "mlp_post_norm")


def kernel(x, mix_pre_norm, w_in, attn_out_norm, hgrn_lb_logits, hgrn_out_norm, w_out, mix_post_norm, mlp_pre_norm, w_ff1, w_ff2, mlp_post_norm, loss_target, m_mix_pre_norm, m_w_in, m_attn_out_norm, m_hgrn_lb_logits, m_hgrn_out_norm, m_w_out, m_mix_post_norm, m_mlp_pre_norm, m_w_ff1, m_w_ff2, m_mlp_post_norm, v_mix_pre_norm, v_w_in, v_attn_out_norm, v_hgrn_lb_logits, v_hgrn_out_norm, v_w_out, v_mix_post_norm, v_mlp_pre_norm, v_w_ff1, v_w_ff2, v_mlp_post_norm):
    w = dict(w_in=w_in[0], w_out=w_out[0], w_ff1=w_ff1[0], w_ff2=w_ff2[0])
    m = dict(w_in=m_w_in[0], w_out=m_w_out[0], w_ff1=m_w_ff1[0], w_ff2=m_w_ff2[0])
    v = dict(w_in=v_w_in[0], w_out=v_w_out[0], w_ff1=v_w_ff1[0], w_ff2=v_w_ff2[0])

    dx, big, small = train_step(x[0], loss_target[0], mix_pre_norm, attn_out_norm, hgrn_lb_logits, hgrn_out_norm,
                                mix_post_norm, mlp_pre_norm, mlp_post_norm, w, m, v)

    packed_g = _pack_small(small["dg1"], small["dgp"], small["dg_pre"], small["dg_post"], small["dan"], small["dhn"],
                           small["dlb"], small["loss_vec"])
    reduced = small_all_reduce(packed_g)
    pack = lambda a, b, c2, d, e, f, g: _pack_small(a, b, c2, d, e, f, g)
    w_s = pack(mix_pre_norm, mix_post_norm, mlp_pre_norm, mlp_post_norm, attn_out_norm, hgrn_out_norm, hgrn_lb_logits)
    m_s = pack(m_mix_pre_norm, m_mix_post_norm, m_mlp_pre_norm, m_mlp_post_norm, m_attn_out_norm, m_hgrn_out_norm,
               m_hgrn_lb_logits)
    v_s = pack(v_mix_pre_norm, v_mix_post_norm, v_mlp_pre_norm, v_mlp_post_norm, v_attn_out_norm, v_hgrn_out_norm,
               v_hgrn_lb_logits)
    g_s, d_s, nm_s, nv_s, loss = small_adamw(reduced, w_s, m_s, v_s)
    small_out = [_unpack_small(t) for t in (g_s, d_s, nm_s, nv_s)]

    outs = [loss[0, 0], dx[None]]
    for kind in range(4):
        for name in ORDER:
            outs.append(big[name][kind][None] if name in BIG else small_out[kind][name])
    return tuple(outs)
```

```python
import jax
import jax.numpy as jnp
from jax import lax
from jax.experimental import pallas as pl
from jax.experimental.pallas import tpu as pltpu

F32 = jnp.float32
BF16 = jnp.bfloat16

D_MODEL = 1024
ATTN_WIDTH = 512
ATTN_HEAD_DIM = 64
ATTN_HEADS = 8
ATTN_BLOCK = 128
DILATIONS = (1, 4, 16)
HGRN_WIDTH = 512
HGRN_HEADS = 4
HGRN_HEAD_DIM = 128
HGRN_CHUNK = 64
IN_PROJ_WIDTH = 3584
D_FF = 4096
RMS_EPS = 1e-6
N_DEV = 8
ADAM_LR = 0.001
ADAM_B1 = 0.9
ADAM_B2 = 0.999
ADAM_EPS = 1e-08
ADAM_WD = 0.01
ADAM_STEP = 10

SUBLANES = 8
LANES = 128
COLUMN_UNROLL = 8
HGRN_CHUNKS_PER_STEP = 2
SUB_BLOCK = 16
TOKEN_TILE = 256
ELEMENTWISE_ROWS = 1024
MLP_TILE = 256
PROJ_TILE = 512
VMEM_LIMIT = 56 * 1024 * 1024
NEG_BIG = -1e30
MESH = pl.DeviceIdType.MESH


def _params(**kw):
    return pltpu.CompilerParams(vmem_limit_bytes=VMEM_LIMIT, **kw)


def _vmem_spec():
    return pl.BlockSpec(memory_space=pltpu.VMEM)


def _dot(a, b):
    return jnp.dot(a, b, preferred_element_type=F32)


def _dot_nt(a, b):
    return lax.dot_general(a, b, (((1,), (1,)), ((), ())), preferred_element_type=F32)


def _dot_tn(a, b):
    return lax.dot_general(a, b, (((0,), (0,)), ((), ())), preferred_element_type=F32)


def _sigmoid(x):
    return 1.0 / (1.0 + jnp.exp(-x))


def _rms_fwd(x, gain, width):
    r = lax.rsqrt(jnp.sum(x * x, axis=-1, keepdims=True) * (1.0 / width) + RMS_EPS)
    return x * r * gain


def _rms_bwd(dy, x, gain, width):
    r = lax.rsqrt(jnp.sum(x * x, axis=-1, keepdims=True) * (1.0 / width) + RMS_EPS)
    xhat = x * r
    dxhat = dy * gain
    dx = r * (dxhat - xhat * (jnp.sum(dxhat * xhat, axis=-1, keepdims=True) * (1.0 / width)))
    return dx, dy * xhat


def _split3(x):
    hi = x.astype(BF16)
    r1 = x - hi.astype(F32)
    mid = r1.astype(BF16)
    lo = (r1 - mid.astype(F32)).astype(BF16)
    return hi, mid, lo


def _tri_sum(tri_bf16, x):
    hi, mid, lo = _split3(x)
    return _dot(tri_bf16, hi) + _dot(tri_bf16, mid) + _dot(tri_bf16, lo)


def _dilated_spec(d, tm, width):
    return pl.BlockSpec((d, tm // d, width), lambda i: (0, i, 0))


def _lane_blocks(ref, value):
    for c in range(ref.shape[0]):
        ref[c] = value[:, c * LANES:(c + 1) * LANES]


def _to_dilated(src_ref, dst_ref, d, tm, cast=None):
    for r in range(d):
        for c in range(src_ref.shape[0]):
            v = src_ref[c] if d == 1 else src_ref[c, pl.ds(r, tm // d, stride=d), :]
            dst_ref[r, :, c * LANES:(c + 1) * LANES] = v if cast is None else v.astype(cast)


def _from_dilated(src_ref, scratch_ref, d, tm):
    if d == 1:
        return src_ref[0].astype(F32)
    nblk = scratch_ref.shape[0]
    for r in range(d):
        for c in range(nblk):
            scratch_ref[c, pl.ds(r, tm // d, stride=d), :] = src_ref[r, :, c * LANES:(c + 1) * LANES].astype(F32)
    return jnp.concatenate([scratch_ref[c] for c in range(nblk)], axis=1)


def in_proj_fwd(x, g1, w_in_b):
    s = x.shape[0]
    tm = PROJ_TILE
    qkv_w = 3 * ATTN_WIDTH
    hg_w = IN_PROJ_WIDTH - qkv_w

    def body(x_ref, g_ref, w_ref, hg_ref, h_ref, *rest):
        qkv_refs, qkv_scr = rest[:len(DILATIONS)], rest[len(DILATIONS)]
        h = _rms_fwd(x_ref[...], g_ref[...], D_MODEL).astype(BF16)
        h_ref[...] = h
        proj = _dot(h, w_ref[...])
        hg_ref[...] = proj[:, qkv_w:]
        _lane_blocks(qkv_scr, proj[:, :qkv_w])
        for d, ref in zip(DILATIONS, qkv_refs):
            _to_dilated(qkv_scr, ref, d, tm, cast=BF16)

    return pl.pallas_call(
        body,
        name="in_proj_fwd",
        grid=(s // tm,),
        in_specs=[
            pl.BlockSpec((tm, D_MODEL), lambda i: (i, 0)),
            pl.BlockSpec((1, D_MODEL), lambda i: (0, 0)),
            _vmem_spec(),
        ],
        out_specs=[
            pl.BlockSpec((tm, hg_w), lambda i: (i, 0)),
            pl.BlockSpec((tm, D_MODEL), lambda i: (i, 0)),
        ] + [_dilated_spec(d, tm, qkv_w) for d in DILATIONS],
        out_shape=[jax.ShapeDtypeStruct((s, hg_w), F32), jax.ShapeDtypeStruct((s, D_MODEL), BF16)] + [
            jax.ShapeDtypeStruct((d, s // d, qkv_w), BF16) for d in DILATIONS],
        scratch_shapes=[pltpu.VMEM((qkv_w // LANES, tm, LANES), F32)],
        compiler_params=_params(dimension_semantics=("arbitrary",)),
    )(x, g1, w_in_b)


ATTN_SCALE = ATTN_HEAD_DIM ** -0.5


def _fill_attn_bias(bias_ref, dilation):
    qi = lax.broadcasted_iota(jnp.int32, (ATTN_BLOCK, 2 * ATTN_BLOCK), 0)
    kj = lax.broadcasted_iota(jnp.int32, (ATTN_BLOCK, 2 * ATTN_BLOCK), 1)
    dist = qi + ATTN_BLOCK - kj
    valid = (dist >= 0) & (dist <= ATTN_BLOCK)
    for head in range(ATTN_HEADS):
        slope = 2.0 ** (-8.0 * (head + 1) / ATTN_HEADS)
        bias = jnp.where(valid, dist.astype(F32) * (-slope * dilation), NEG_BIG)
        bias_ref[0, head] = bias
        bias_ref[1, head] = jnp.where(kj >= ATTN_BLOCK, bias, NEG_BIG)


def _stack_heads(x):
    low = _lane_half(x.shape, 0)
    zero = jnp.zeros_like(x)
    return jnp.concatenate([jnp.where(low, x, zero), jnp.where(low, zero, x)], axis=0)


def _unstack_heads(y):
    half = y.shape[0] // 2
    return jnp.where(_lane_half((half, y.shape[1]), 0), y[:half], y[half:])


def _attn_scores(q_stack, kcat, bias_ref, pair, first_block):
    f = first_block.astype(jnp.int32)
    bias = jnp.concatenate([bias_ref[f, 2 * pair], bias_ref[f, 2 * pair + 1]], axis=0)
    return _dot_nt(q_stack, kcat) + bias


def _lane_half(shape, sub):
    lane = lax.broadcasted_iota(jnp.int32, shape, 1)
    return (lane < ATTN_HEAD_DIM) if sub == 0 else (lane >= ATTN_HEAD_DIM)


def _sub_block(col, row):
    return pl.BlockSpec((None, ATTN_BLOCK, ATTN_WIDTH), lambda r, n: (r, row(n), col))


def attn_fwd(qkv, dilation):
    d, length, _ = qkv.shape
    assert d == dilation
    nb = length // ATTN_BLOCK

    def body(q_ref, kc_ref, kp_ref, vc_ref, vp_ref, o_ref, lse_ref, bias_ref):
        @pl.when((pl.program_id(0) == 0) & (pl.program_id(1) == 0))
        def _():
            _fill_attn_bias(bias_ref, d)

        first = pl.program_id(1) == 0
        for pair in range(ATTN_HEADS // 2):
            lanes = slice(pair * LANES, (pair + 1) * LANES)
            q_stack = _stack_heads(q_ref[:, lanes] * ATTN_SCALE)
            kcat = jnp.concatenate([kp_ref[:, lanes], kc_ref[:, lanes]], axis=0)
            vcat = jnp.concatenate([vp_ref[:, lanes], vc_ref[:, lanes]], axis=0)
            sc = _attn_scores(q_stack, kcat, bias_ref, pair, first)
            m = jnp.max(sc, axis=-1, keepdims=True)
            p = jnp.exp(sc - m)
            den = jnp.sum(p, axis=-1, keepdims=True)
            o_ref[:, lanes] = _unstack_heads(_dot(p.astype(BF16), vcat) / den).astype(BF16)
            lse_ref[:, lanes] = _unstack_heads(jnp.broadcast_to(m + jnp.log(den), (2 * ATTN_BLOCK, LANES)))

    cur = lambda n: n
    prev = lambda n: jnp.maximum(n - 1, 0)
    return pl.pallas_call(
        body,
        name=f"attn_fwd_d{d}",
        grid=(d, nb),
        in_specs=[_sub_block(0, cur), _sub_block(1, cur), _sub_block(1, prev), _sub_block(2, cur), _sub_block(2, prev)],
        out_specs=[_sub_block(0, cur), _sub_block(0, cur)],
        out_shape=[jax.ShapeDtypeStruct((d, length, ATTN_WIDTH), BF16), jax.ShapeDtypeStruct((d, length, ATTN_WIDTH), F32)],
        scratch_shapes=[pltpu.VMEM((2, ATTN_HEADS, ATTN_BLOCK, 2 * ATTN_BLOCK), F32)],
        compiler_params=_params(dimension_semantics=("arbitrary", "arbitrary")),
    )(qkv, qkv, qkv, qkv, qkv)


def attn_bwd(qkv, d_out, lse, delta, dilation, ride=None):
    d, length, _ = qkv.shape
    assert d == dilation
    nb = length // ATTN_BLOCK

    steps = d * nb + 1

    def body(q_ref, kc_ref, kp_ref, vc_ref, vp_ref, do_ref, lse_ref, dl_ref, dq_ref, dk_ref, dv_ref, ck_ref, cv_ref,
             bias_ref):
        t = pl.program_id(0)

        @pl.when(t == 0)
        def _():
            ck_ref[...] = jnp.zeros_like(ck_ref)
            cv_ref[...] = jnp.zeros_like(cv_ref)
            _fill_attn_bias(bias_ref, d)

        @pl.when(t < steps - 1)
        def _():
            first = t % nb == 0
            for pair in range(ATTN_HEADS // 2):
                lanes = slice(pair * LANES, (pair + 1) * LANES)
                q_stack = _stack_heads(q_ref[:, lanes] * ATTN_SCALE)
                do_stack = _stack_heads(do_ref[:, lanes])
                kcat = jnp.concatenate([kp_ref[:, lanes], kc_ref[:, lanes]], axis=0)
                vcat = jnp.concatenate([vp_ref[:, lanes], vc_ref[:, lanes]], axis=0)
                col_a, col_b = pair * LANES, pair * LANES + ATTN_HEAD_DIM
                lse_col = jnp.concatenate([lse_ref[:, col_a:col_a + 1], lse_ref[:, col_b:col_b + 1]], axis=0)
                dl_col = jnp.concatenate([dl_ref[:, col_a:col_a + 1], dl_ref[:, col_b:col_b + 1]], axis=0)
                p = jnp.exp(_attn_scores(q_stack, kcat, bias_ref, pair, first) - lse_col)
                ds = (p * (_dot_nt(do_stack, vcat) - dl_col)).astype(BF16)
                dq_ref[:, lanes] = (_unstack_heads(_dot(ds, kcat)) * ATTN_SCALE).astype(BF16)
                dk_cat = _dot_tn(ds, q_stack)
                dv_cat = _dot_tn(p.astype(BF16), do_stack)
                dk_ref[:, lanes] = (ck_ref[:, lanes] + dk_cat[:ATTN_BLOCK]).astype(BF16)
                dv_ref[:, lanes] = (cv_ref[:, lanes] + dv_cat[:ATTN_BLOCK]).astype(BF16)
                ck_ref[:, lanes] = dk_cat[ATTN_BLOCK:]
                cv_ref[:, lanes] = dv_cat[ATTN_BLOCK:]

        @pl.when(t == steps - 1)
        def _():
            dk_ref[...] = ck_ref[...].astype(BF16)
            dv_ref[...] = cv_ref[...].astype(BF16)

    blk = (ATTN_BLOCK, ATTN_WIDTH)

    def spec(col, shift):
        def index(t):
            f = jnp.minimum(t, steps - 2) if shift > -2 else jnp.maximum(t - 1, 0)
            r, n = f // nb, f % nb
            return (r, jnp.maximum(n - 1, 0) if shift == -1 else n, col)
        return pl.BlockSpec((None, ATTN_BLOCK, ATTN_WIDTH), index)

    step = lambda k: (lambda: pl.program_id(0) == k)
    e_in, e_out, e_shape, e_scr, e_args = _ride_specs(ride)
    return pl.pallas_call(
        _riding(body, 8, 3, 3, ride, step(0), step(steps // 2), step(steps - 1)),
        name=f"attn_bwd_d{d}",
        grid=(steps,),
        in_specs=[spec(0, 0), spec(1, 0), spec(1, -1), spec(2, 0), spec(2, -1), spec(0, 0), spec(0, 0), spec(0, 0)] + e_in,
        out_specs=[spec(0, 0), spec(0, -2), spec(0, -2)] + e_out,
        out_shape=[jax.ShapeDtypeStruct((d, length, ATTN_WIDTH), BF16)] * 3 + e_shape,
        scratch_shapes=[pltpu.VMEM(blk, F32), pltpu.VMEM(blk, F32),
                        pltpu.VMEM((2, ATTN_HEADS, ATTN_BLOCK, 2 * ATTN_BLOCK), F32)] + e_scr,
        compiler_params=_params(dimension_semantics=("arbitrary",)),
    )(qkv, qkv, qkv, qkv, qkv, d_out, lse, delta, *e_args)


def _lower_bound(logits):
    return _sigmoid(logits[0:1, :] - logits[1:2, :])


def _hgrn_gates(q, fp, lb):
    sq = _sigmoid(q)
    qf = q * sq
    sig = _sigmoid(fp)
    sig_neg = _sigmoid(-fp)
    kf = (1.0 - lb) * sig_neg
    log_sig = jnp.minimum(fp, 0.0) - jnp.log(1.0 + jnp.exp(-jnp.abs(fp)))
    a = jnp.log(lb)
    c = jnp.log(1.0 - lb) + log_sig
    log_f = jnp.maximum(a, c) + jnp.log(1.0 + jnp.exp(-jnp.abs(a - c)))
    return sq, qf, (sig, sig_neg, c), log_f, kf


def _tril_bf16(n, upper=False):
    r = lax.broadcasted_iota(jnp.int32, (n, n), 0)
    c = lax.broadcasted_iota(jnp.int32, (n, n), 1)
    keep = (c >= r) if upper else (c <= r)
    return jnp.where(keep, 1.0, 0.0).astype(BF16)


def _hgrn_diagonal_loops(c_len, diagonal):
    for half in range(SUB_BLOCK // SUBLANES):
        def step(jj, carry, half=half):
            j = half * SUBLANES + jj
            for i in range(c_len // SUB_BLOCK):
                diagonal(slice(i * SUB_BLOCK + half * SUBLANES, (i + 1) * SUB_BLOCK), j, i * SUB_BLOCK + j)
            return carry

        lax.fori_loop(0, SUBLANES, step, 0, unroll=COLUMN_UNROLL)


def _hgrn_off_diagonal(b, qf, kf):
    c_len, width = b.shape
    edges = [b[0:1, :]] + [b[i * SUB_BLOCK - 1:i * SUB_BLOCK, :] for i in range(1, c_len // SUB_BLOCK)]
    eq = jnp.exp(b - jnp.concatenate([jnp.broadcast_to(e, (SUB_BLOCK, width)) for e in edges], axis=0))
    q_til = qf * eq
    k_til, ek = [], []
    for i in range(1, c_len // SUB_BLOCK):
        n = i * SUB_BLOCK
        e = jnp.exp(edges[i] - b[:n, :])
        ek.append(e)
        k_til.append(jnp.concatenate([kf[:n, :] * e, jnp.zeros((2 * c_len - n, width), F32)], axis=0))
    return q_til, k_til, eq, ek


def _split2(x):
    hi = x.astype(BF16)
    return hi, (x - hi.astype(F32)).astype(BF16)


def hgrn_fwd(proj, lb, ride=None):
    s = proj.shape[0]
    c_len, nh, hd = HGRN_CHUNK, HGRN_HEADS, HGRN_HEAD_DIM
    n_chunks = s // c_len
    col0 = 0

    cps = 2 * HGRN_CHUNKS_PER_STEP
    n_steps = n_chunks // cps

    def body(q_ref, f_ref, i_ref, lb_ref, o_ref, st_out_ref, a_out_ref, st_ref, b_ref, qf_ref, kf_ref, a_ref):
        @pl.when(pl.program_id(0) == 0)
        def _():
            st_ref[...] = jnp.zeros_like(st_ref)

        lbv = _lower_bound(lb_ref[...])
        for u in range(cps):
            rs = slice(u * c_len, (u + 1) * c_len)
            b_u, qf_u, kf_u, a_u = b_ref.at[u], qf_ref.at[u], kf_ref.at[u], a_ref.at[u]
            _, qf, _, log_f, kf = _hgrn_gates(q_ref[rs, :], f_ref[rs, :], lbv)
            b = _tri_sum(_tril_bf16(c_len), log_f)
            b_u[...] = b
            qf_u[...] = qf
            kf_u[...] = kf
            a_u[...] = jnp.zeros_like(a_u)

            def diagonal(rows, j, key, b_u=b_u, qf_u=qf_u, kf_u=kf_u, a_u=a_u):
                bj = b_u[pl.ds(key, 1), :]
                kj = kf_u[pl.ds(key, 1), :]
                nrow = rows.stop - rows.start
                t_loc = lax.broadcasted_iota(jnp.int32, (nrow, nh * hd), 0) + (rows.start % SUB_BLOCK)
                e = jnp.exp(jnp.where(t_loc >= j, b_u[rows, :] - bj, NEG_BIG))
                prod = qf_u[rows, :] * kj * e
                lane = lax.broadcasted_iota(jnp.int32, (nrow, hd), 1)
                for h in range(nh):
                    col = jnp.sum(prod[:, h * hd:(h + 1) * hd], axis=-1, keepdims=True)
                    a_u[h, rows, :] = jnp.where(lane == key, col, a_u[h, rows, :])

            _hgrn_diagonal_loops(c_len, diagonal)
            q_til, k_til, _, _ = _hgrn_off_diagonal(b, qf, kf)
            q_til = q_til.astype(BF16)
            k_til = [k.astype(BF16) for k in k_til]

            b_last = b[c_len - 1:c_len, :]
            qb = (qf * jnp.exp(b)).astype(BF16)
            kb2 = (kf * jnp.exp(b_last - b)).astype(BF16)
            vf = i_ref[rs, :].astype(BF16)
            for h in range(nh):
                hs = slice(h * hd, (h + 1) * hd)
                st = st_ref[h]
                st_out_ref[u, h] = st
                off = [jnp.zeros((SUB_BLOCK, hd), F32)]
                for i in range(1, c_len // SUB_BLOCK):
                    off.append(_dot_nt(q_til[i * SUB_BLOCK:(i + 1) * SUB_BLOCK, hs], k_til[i - 1][:, hs]))
                a_h = a_u[h] + jnp.concatenate(off, axis=0)
                a_out_ref[rs, hs] = a_h
                o_ref[rs, hs] = _dot_nt(qb[:, hs], st.astype(BF16)) + _dot(a_h[:, :c_len].astype(BF16), vf[:, hs])
                st_ref[h] = st * jnp.exp(b_last[:, hs]) + _dot_tn(vf[:, hs], kb2[:, hs])

    blk = (cps * c_len, HGRN_WIDTH)
    sblk = (cps, c_len, HGRN_WIDTH)
    step = lambda k: (lambda: pl.program_id(0) == k)
    e_in, e_out, e_shape, e_scr, e_args = _ride_specs(ride)
    return pl.pallas_call(
        _riding(body, 4, 3, 5, ride, step(0), step((7 * n_steps) // 8), step(n_steps - 1)),
        name="hgrn_fwd",
        grid=(n_steps,),
        in_specs=[
            pl.BlockSpec(blk, lambda c: (c, col0)),
            pl.BlockSpec(blk, lambda c: (c, col0 + 1)),
            pl.BlockSpec(blk, lambda c: (c, col0 + 2)),
            pl.BlockSpec((2, HGRN_WIDTH), lambda c: (0, 0)),
        ] + e_in,
        out_specs=[
            pl.BlockSpec(blk, lambda c: (c, 0)),
            pl.BlockSpec((cps, nh, hd, hd), lambda c: (c, 0, 0, 0)),
            pl.BlockSpec(blk, lambda c: (c, 0)),
        ] + e_out,
        out_shape=[
            jax.ShapeDtypeStruct((s, HGRN_WIDTH), F32),
            jax.ShapeDtypeStruct((n_chunks, nh, hd, hd), F32),
            jax.ShapeDtypeStruct((s, nh * hd), F32),
        ] + e_shape,
        scratch_shapes=[
            pltpu.VMEM((nh, hd, hd), F32),
            pltpu.VMEM(sblk, F32),
            pltpu.VMEM(sblk, F32),
            pltpu.VMEM(sblk, F32),
            pltpu.VMEM((cps, nh, c_len, hd), F32),
        ] + e_scr,
        compiler_params=_params(dimension_semantics=("arbitrary",)),
    )(proj, proj, proj, lb, *e_args)


def hgrn_bwd(proj, lb, d_o, states, a_mat, ride=None):
    s = proj.shape[0]
    c_len, nh, hd = HGRN_CHUNK, HGRN_HEADS, HGRN_HEAD_DIM
    n_chunks = s // c_len
    col0 = 0
    cps = HGRN_CHUNKS_PER_STEP
    n_steps = n_chunks // cps
    last = n_steps - 1

    def body(q_ref, f_ref, i_ref, lb_ref, do_ref, st_in_ref, a_in_ref, dq_ref, df_ref, di_ref, dlb_ref,
             dst_ref, b_ref, qf_ref, kf_ref, da_ref, dqi_ref, dki_ref):
        @pl.when(pl.program_id(0) == 0)
        def _():
            dst_ref[...] = jnp.zeros_like(dst_ref)
            dlb_ref[...] = jnp.zeros_like(dlb_ref)

        lbv = _lower_bound(lb_ref[...])
        for u in reversed(range(cps)):
            rs = slice(u * c_len, (u + 1) * c_len)
            b_u, qf_u, kf_u, da_u, dqi_u, dki_u = (b_ref.at[u], qf_ref.at[u], kf_ref.at[u], da_ref.at[u], dqi_ref.at[u],
                                                   dki_ref.at[u])
            q = q_ref[rs, :]
            sq, qf, (sig, sig_neg, log_c), log_f, kf = _hgrn_gates(q, f_ref[rs, :], lbv)
            b = _tri_sum(_tril_bf16(c_len), log_f)
            b_u[...] = b
            qf_u[...] = qf
            kf_u[...] = kf
            b_last = b[c_len - 1:c_len, :]
            eb = jnp.exp(b)
            ebl = jnp.exp(b_last - b)
            qb = qf * eb
            kb2 = kf * ebl
            vf = i_ref[rs, :]
            d_o = do_ref[rs, :]
            qb_b, kb2_b, vf_b, do_b = qb.astype(BF16), kb2.astype(BF16), vf.astype(BF16), d_o.astype(BF16)
            tq = lax.broadcasted_iota(jnp.int32, (c_len, hd), 0)
            lane = lax.broadcasted_iota(jnp.int32, (c_len, hd), 1)

            dqb_parts, dvf_parts, dkb2_parts, dbl_parts = [], [], [], []
            for h in range(nh):
                hs = slice(h * hd, (h + 1) * hd)
                st = st_in_ref[u, h]
                dst = dst_ref[h]
                st_b, dst_b = st.astype(BF16), dst.astype(BF16)
                a_h = a_in_ref[rs, hs][:, :c_len].astype(BF16)
                dqb_parts.append(_dot(do_b[:, hs], st_b))
                dvf_parts.append(_dot_tn(a_h, do_b[:, hs]) + _dot_nt(kb2_b[:, hs], dst_b))
                dkb2_parts.append(_dot(vf_b[:, hs], dst_b))
                da = _dot_nt(do_b[:, hs], vf_b[:, hs])
                da = jnp.concatenate([da, jnp.zeros((c_len, hd - c_len), F32)], axis=1)
                da_u[h] = jnp.where(tq >= lane, da, 0.0)
                dbl_parts.append(jnp.sum(dst * st, axis=0, keepdims=True) * jnp.exp(b_last[:, hs]))
                dst_ref[h] = dst * jnp.exp(b_last[:, hs]) + _dot_tn(do_b[:, hs], qb_b[:, hs])
            dqb = jnp.concatenate(dqb_parts, axis=1)
            dvf = jnp.concatenate(dvf_parts, axis=1)
            dkb2 = jnp.concatenate(dkb2_parts, axis=1)
            dbl = jnp.concatenate(dbl_parts, axis=1) + jnp.sum(dkb2 * kb2, axis=0, keepdims=True)

            dqi_u[...] = jnp.zeros_like(dqi_u)
            t_idx = lax.broadcasted_iota(jnp.int32, (c_len, nh * hd), 0)

            def diagonal(rows, j, key, b_u=b_u, qf_u=qf_u, kf_u=kf_u, da_u=da_u, dqi_u=dqi_u, dki_u=dki_u):
                bj = b_u[pl.ds(key, 1), :]
                kj = kf_u[pl.ds(key, 1), :]
                nrow = rows.stop - rows.start
                t_loc = lax.broadcasted_iota(jnp.int32, (nrow, nh * hd), 0) + (rows.start % SUB_BLOCK)
                e = jnp.exp(jnp.where(t_loc >= j, b_u[rows, :] - bj, NEG_BIG))
                lane_r = lax.broadcasted_iota(jnp.int32, (nrow, hd), 1)
                cols = [jnp.sum(jnp.where(lane_r == key, da_u[h, rows, :], 0.0), axis=-1, keepdims=True)
                        for h in range(nh)]
                w = e * jnp.concatenate([jnp.broadcast_to(cc, (nrow, hd)) for cc in cols], axis=1)
                dqi_u[rows, :] += w * kj
                dki_u[pl.ds(key, 1), :] = jnp.sum(w * qf_u[rows, :], axis=0, keepdims=True)

            _hgrn_diagonal_loops(c_len, diagonal)

            q_til, k_til, eq, ek = _hgrn_off_diagonal(b, qf, kf)
            q_hi, q_lo = _split2(q_til)
            k_pairs = [_split2(k) for k in k_til]
            n_sub = c_len // SUB_BLOCK
            dq_heads, dk_heads = [], []
            for h in range(nh):
                hs = slice(h * hd, (h + 1) * hd)
                dq_rows = [jnp.zeros((SUB_BLOCK, hd), F32)]
                dk_h = jnp.zeros((c_len, hd), F32)
                for i in range(1, n_sub):
                    rows = slice(i * SUB_BLOCK, (i + 1) * SUB_BLOCK)
                    n = i * SUB_BLOCK
                    da_i = da_u[h, rows, :].astype(BF16)
                    k_hi, k_lo = k_pairs[i - 1]
                    dq_rows.append((_dot(da_i, k_hi[:, hs]) + _dot(da_i, k_lo[:, hs])) * eq[rows, hs])
                    dk_t = (_dot_tn(da_i, q_hi[rows, hs]) + _dot_tn(da_i, q_lo[rows, hs]))[:n, :] * ek[i - 1][:, hs]
                    dk_h = dk_h + jnp.concatenate([dk_t, jnp.zeros((c_len - n, hd), F32)], axis=0)
                dq_heads.append(jnp.concatenate(dq_rows, axis=0))
                dk_heads.append(dk_h)
            dq_intra = dqi_u[...] + jnp.concatenate(dq_heads, axis=1)
            dk_intra = dki_u[...] + jnp.concatenate(dk_heads, axis=1)

            db = dqb * qb + qf * dq_intra - kf * dk_intra - dkb2 * kb2
            db = db + jnp.where(t_idx == c_len - 1, dbl, 0.0)
            dg = _tri_sum(_tril_bf16(c_len, upper=True), db)
            dqf = dqb * eb + dq_intra
            dkf = dkb2 * ebl + dk_intra
            dq_ref[rs, :] = (dqf * (sq * (1.0 + q * (1.0 - sq)))).astype(BF16)
            df_ref[rs, :] = (sig_neg * (dg * jnp.exp(log_c - log_f) - dkf * (1.0 - lbv) * sig)).astype(BF16)
            di_ref[rs, :] = dvf.astype(BF16)
            dlb_ref[...] += jnp.sum(sig_neg * (dg * jnp.exp(-log_f) - dkf), axis=0, keepdims=True)

    blk = (cps * c_len, HGRN_WIDTH)
    sblk = (cps, c_len, HGRN_WIDTH)
    rev = lambda c: last - c
    step = lambda k: (lambda: pl.program_id(0) == k)
    e_in, e_out, e_shape, e_scr, e_args = _ride_specs(ride)
    return pl.pallas_call(
        _riding(body, 7, 4, 7, ride, step(0), step(n_steps // 2), step(last)),
        name="hgrn_bwd",
        grid=(n_steps,),
        in_specs=[
            pl.BlockSpec(blk, lambda c: (rev(c), col0)),
            pl.BlockSpec(blk, lambda c: (rev(c), col0 + 1)),
            pl.BlockSpec(blk, lambda c: (rev(c), col0 + 2)),
            pl.BlockSpec((2, HGRN_WIDTH), lambda c: (0, 0)),
            pl.BlockSpec(blk, lambda c: (rev(c), 0)),
            pl.BlockSpec((cps, nh, hd, hd), lambda c: (rev(c), 0, 0, 0)),
            pl.BlockSpec(blk, lambda c: (rev(c), 0)),
        ] + e_in,
        out_specs=[
            pl.BlockSpec(blk, lambda c: (rev(c), 0)),
            pl.BlockSpec(blk, lambda c: (rev(c), 0)),
            pl.BlockSpec(blk, lambda c: (rev(c), 0)),
            pl.BlockSpec((1, HGRN_WIDTH), lambda c: (0, 0)),
        ] + e_out,
        out_shape=[jax.ShapeDtypeStruct((s, HGRN_WIDTH), BF16)] * 3 + [jax.ShapeDtypeStruct((1, HGRN_WIDTH), F32)] + e_shape,
        scratch_shapes=[
            pltpu.VMEM((nh, hd, hd), F32),
            pltpu.VMEM(sblk, F32),
            pltpu.VMEM(sblk, F32),
            pltpu.VMEM(sblk, F32),
            pltpu.VMEM((cps, nh, c_len, hd), F32),
            pltpu.VMEM(sblk, F32),
            pltpu.VMEM(sblk, F32),
        ] + e_scr,
        compiler_params=_params(dimension_semantics=("arbitrary",)),
    )(proj, proj, proj, lb, d_o, states, a_mat, *e_args)


def _row_spec(tm, width, col=0):
    return pl.BlockSpec((tm, width), lambda i: (i, col))


def _const_spec(width):
    return pl.BlockSpec((1, width), lambda i: (0, 0))


def _acc_rows(ref, value):
    @pl.when(pl.program_id(0) == 0)
    def _():
        ref[...] = jnp.zeros_like(ref)

    ref[...] += jnp.sum(value, axis=0, keepdims=True)


def mix_fwd(attn_parts, o_h, proj, an, hn, w_out_b, gp, x, ride=None):
    s = x.shape[0]
    tm = TOKEN_TILE
    gate_col = 3
    hd = HGRN_HEAD_DIM
    nd = len(DILATIONS)

    def body(*refs):
        o_refs, l_refs = refs[:nd], refs[nd:2 * nd]
        oh_ref, gate_ref, an_ref, hn_ref, w_ref, gp_ref, x_ref = refs[2 * nd:2 * nd + 7]
        x1_ref, cat_ref, mixed_ref, attn_ref = refs[2 * nd + 7:2 * nd + 11]
        lse_refs = refs[2 * nd + 11:3 * nd + 11]
        o_scr, l_scr, lse_scr = refs[3 * nd + 11:]
        os_ = [_from_dilated(r, o_scr.at[k], d, tm) for k, (r, d) in enumerate(zip(o_refs, DILATIONS))]
        ls = [_from_dilated(r, l_scr.at[k], d, tm) for k, (r, d) in enumerate(zip(l_refs, DILATIONS))]
        m = jnp.maximum(jnp.maximum(ls[0], ls[1]), ls[2])
        es = [jnp.exp(l - m) for l in ls]
        den = es[0] + es[1] + es[2]
        attn = (es[0] * os_[0] + es[1] * os_[1] + es[2] * os_[2]) / den
        attn_ref[...] = attn
        _lane_blocks(lse_scr, m + jnp.log(den))
        for d, ref in zip(DILATIONS, lse_refs):
            _to_dilated(lse_scr, ref, d, tm)
        cat_ref[:, :ATTN_WIDTH] = _rms_fwd(attn, an_ref[...], ATTN_WIDTH).astype(BF16)
        gate = gate_ref[...]
        silu_g = gate * _sigmoid(gate)
        for h in range(HGRN_HEADS):
            hs = slice(h * hd, (h + 1) * hd)
            rec = _rms_fwd(oh_ref[:, hs], hn_ref[:, hs], hd) * silu_g[:, hs]
            cat_ref[:, ATTN_WIDTH + h * hd:ATTN_WIDTH + (h + 1) * hd] = rec.astype(BF16)
        mixed = _dot(cat_ref[...], w_ref[...])
        mixed_ref[...] = mixed
        x1_ref[...] = x_ref[...] + _rms_fwd(mixed, gp_ref[...], D_MODEL)

    aw = ATTN_WIDTH
    n_steps = s // tm
    step = lambda k: (lambda: pl.program_id(0) == k)
    e_in, e_out, e_shape, e_scr, e_args = _ride_specs(ride)
    return pl.pallas_call(
        _riding(body, 2 * nd + 7, 4 + nd, 3, ride, step(0), step((13 * n_steps) // 16), step(n_steps - 1)),
        name="mix_fwd",
        grid=(n_steps,),
        in_specs=[_dilated_spec(d, tm, aw) for d in DILATIONS] * 2 + [
            _row_spec(tm, aw), _row_spec(tm, aw, gate_col), _const_spec(aw), _const_spec(aw), _vmem_spec(),
            _const_spec(D_MODEL), _row_spec(tm, D_MODEL)] + e_in,
        out_specs=[_row_spec(tm, D_MODEL), _row_spec(tm, D_MODEL), _row_spec(tm, D_MODEL), _row_spec(tm, aw)] + [
            _dilated_spec(d, tm, aw) for d in DILATIONS] + e_out,
        out_shape=[
            jax.ShapeDtypeStruct((s, D_MODEL), F32),
            jax.ShapeDtypeStruct((s, D_MODEL), BF16),
            jax.ShapeDtypeStruct((s, D_MODEL), F32),
            jax.ShapeDtypeStruct((s, aw), F32),
        ] + [jax.ShapeDtypeStruct((d, s // d, aw), F32) for d in DILATIONS] + e_shape,
        scratch_shapes=[pltpu.VMEM((nd, aw // LANES, tm, LANES), F32), pltpu.VMEM((nd, aw // LANES, tm, LANES), F32),
                        pltpu.VMEM((aw // LANES, tm, LANES), F32)] + e_scr,
        compiler_params=_params(dimension_semantics=("arbitrary",)),
    )(*[p[0] for p in attn_parts], *[p[1] for p in attn_parts], o_h, proj, an, hn, w_out_b, gp, x, *e_args)


def mix_bwd(dx1, mixed, gp, w_out_b, attn, an, o_h, proj, hn):
    s = dx1.shape[0]
    tm = TOKEN_TILE
    gate_col = 3
    hd = HGRN_HEAD_DIM
    aw = ATTN_WIDTH

    nd = len(DILATIONS)

    def body(*refs):
        dx1_ref, mixed_ref, gp_ref, w_ref, attn_ref, an_ref, oh_ref, gate_ref, hn_ref, dmix_ref = refs[:10]
        do_refs, delta_refs = refs[10:10 + nd], refs[10 + nd:10 + 2 * nd]
        doh_ref, dgate_ref, dgp_ref, dan_ref, dhn_ref, do_ref, delta_ref = refs[10 + 2 * nd:]
        dmixed, gp_c = _rms_bwd(dx1_ref[...], mixed_ref[...], gp_ref[...], D_MODEL)
        _acc_rows(dgp_ref, gp_c)
        dmixed_b = dmixed.astype(BF16)
        dmix_ref[...] = dmixed_b
        dcat = _dot_nt(dmixed_b, w_ref[...])
        attn = attn_ref[...]
        d_o, an_c = _rms_bwd(dcat[:, :aw], attn, an_ref[...], aw)
        _acc_rows(dan_ref, an_c)
        _lane_blocks(do_ref, d_o)
        prod = d_o * attn
        for pair in range(ATTN_HEADS // 2):
            pp = prod[:, pair * LANES:(pair + 1) * LANES]
            low = _lane_half((tm, LANES), 0)
            lo = jnp.sum(jnp.where(low, pp, 0.0), axis=-1, keepdims=True)
            hi = jnp.sum(jnp.where(low, 0.0, pp), axis=-1, keepdims=True)
            delta_ref[pair] = jnp.where(low, lo, hi)
        for d, o_ref, l_ref in zip(DILATIONS, do_refs, delta_refs):
            _to_dilated(do_ref, o_ref, d, tm, cast=BF16)
            _to_dilated(delta_ref, l_ref, d, tm)
        gate = gate_ref[...]
        sg = _sigmoid(gate)
        silu_g = gate * sg
        drec = dcat[:, aw:]
        hn_parts = []
        for h in range(HGRN_HEADS):
            hs = slice(h * hd, (h + 1) * hd)
            oh = oh_ref[:, hs]
            on = _rms_fwd(oh, hn_ref[:, hs], hd)
            dgate_ref[:, hs] = (drec[:, hs] * on * (sg[:, hs] * (1.0 + gate[:, hs] * (1.0 - sg[:, hs])))).astype(BF16)
            d_oh, hn_c = _rms_bwd(drec[:, hs] * silu_g[:, hs], oh, hn_ref[:, hs], hd)
            doh_ref[:, hs] = d_oh
            hn_parts.append(hn_c)
        _acc_rows(dhn_ref, jnp.concatenate(hn_parts, axis=1))

    return pl.pallas_call(
        body,
        name="mix_bwd",
        grid=(s // tm,),
        in_specs=[_row_spec(tm, D_MODEL), _row_spec(tm, D_MODEL), _const_spec(D_MODEL), _vmem_spec(), _row_spec(tm, aw),
                  _const_spec(aw), _row_spec(tm, aw), _row_spec(tm, aw, gate_col), _const_spec(aw)],
        out_specs=[_row_spec(tm, D_MODEL)] + [_dilated_spec(d, tm, aw) for d in DILATIONS] * 2 + [_row_spec(tm, aw)] * 2 + [
            _const_spec(D_MODEL), _const_spec(aw), _const_spec(aw)],
        out_shape=[jax.ShapeDtypeStruct((s, D_MODEL), BF16)] + [
            jax.ShapeDtypeStruct((d, s // d, aw), BF16) for d in DILATIONS] + [
            jax.ShapeDtypeStruct((d, s // d, aw), F32) for d in DILATIONS] + [
            jax.ShapeDtypeStruct((s, aw), F32), jax.ShapeDtypeStruct((s, aw), BF16),
            jax.ShapeDtypeStruct((1, D_MODEL), F32), jax.ShapeDtypeStruct((1, aw), F32),
            jax.ShapeDtypeStruct((1, aw), F32)],
        scratch_shapes=[pltpu.VMEM((aw // LANES, tm, LANES), F32), pltpu.VMEM((aw // LANES, tm, LANES), F32)],
        compiler_params=_params(dimension_semantics=("arbitrary",)),
    )(dx1, mixed, gp, w_out_b, attn, an, o_h, proj, hn)


def mlp_fwd_bwd(x1, g_pre, w1_blocks, w2_b, g_post, target):
    s = x1.shape[0]
    tm = MLP_TILE
    nblk, _, fb = w1_blocks.shape

    def body(x1_ref, gpre_ref, w1_ref, w2_ref, gpost_ref, t_ref,
             dx1_ref, h2_ref, a_ref, du_ref, dff_ref, loss_ref, dgpre_ref, dgpost_ref, u_ref):
        x1v = x1_ref[...]
        h2 = _rms_fwd(x1v, gpre_ref[...], D_MODEL).astype(BF16)
        h2_ref[...] = h2
        ff = jnp.zeros((tm, D_MODEL), F32)
        for j in range(nblk):
            cols = slice(j * fb, (j + 1) * fb)
            ru = jnp.maximum(_dot(h2, w1_ref[j]), 0.0)
            u_ref[:, cols] = ru.astype(BF16)
            a = (ru * ru).astype(BF16)
            a_ref[:, cols] = a
            ff = ff + _dot(a, w2_ref[cols, :])
        diff = x1v + _rms_fwd(ff, gpost_ref[...], D_MODEL) - t_ref[...]
        _acc_rows(loss_ref, diff * diff)
        dy = diff * (1.0 / D_MODEL)
        dff, gpost_c = _rms_bwd(dy, ff, gpost_ref[...], D_MODEL)
        _acc_rows(dgpost_ref, gpost_c)
        dff_b = dff.astype(BF16)
        dff_ref[...] = dff_b
        dh2 = jnp.zeros((tm, D_MODEL), F32)
        for j in range(nblk):
            cols = slice(j * fb, (j + 1) * fb)
            du = (_dot_nt(dff_b, w2_ref[cols, :]) * (2.0 * u_ref[:, cols])).astype(BF16)
            du_ref[:, cols] = du
            dh2 = dh2 + _dot_nt(du, w1_ref[j])
        dxa, gpre_c = _rms_bwd(dh2, x1v, gpre_ref[...], D_MODEL)
        _acc_rows(dgpre_ref, gpre_c)
        dx1_ref[...] = dy + dxa

    dm = D_MODEL
    return pl.pallas_call(
        body,
        name="mlp_fwd_bwd",
        grid=(s // tm,),
        in_specs=[_row_spec(tm, dm), _const_spec(dm), _vmem_spec(), _vmem_spec(), _const_spec(dm), _row_spec(tm, dm)],
        out_specs=[_row_spec(tm, dm), _row_spec(tm, dm), _row_spec(tm, D_FF), _row_spec(tm, D_FF), _row_spec(tm, dm),
                   _const_spec(dm), _const_spec(dm), _const_spec(dm)],
        out_shape=[
            jax.ShapeDtypeStruct((s, dm), F32),
            jax.ShapeDtypeStruct((s, dm), BF16),
            jax.ShapeDtypeStruct((s, D_FF), BF16),
            jax.ShapeDtypeStruct((s, D_FF), BF16),
            jax.ShapeDtypeStruct((s, dm), BF16),
            jax.ShapeDtypeStruct((1, dm), F32),
            jax.ShapeDtypeStruct((1, dm), F32),
            jax.ShapeDtypeStruct((1, dm), F32),
        ],
        scratch_shapes=[pltpu.VMEM((tm, D_FF), BF16)],
        compiler_params=_params(dimension_semantics=("arbitrary",)),
    )(x1, g_pre, w1_blocks, w2_b, g_post, target)


def in_proj_bwd(attn_grads, hgrn_grads, dgate, w_in_b, x, g1, dx1):
    s = x.shape[0]
    tm = PROJ_TILE
    aw = ATTN_WIDTH
    n_attn = len(attn_grads)
    flat = [g[k] for k in range(3) for g in attn_grads] + list(hgrn_grads) + [dgate]

    def body(*refs):
        parts = refs[:len(flat)]
        w_ref, x_ref, g_ref, dx1_ref, dx_ref, dproj_ref, dg_ref, scr = refs[len(flat):]
        groups = []
        for k in range(3):
            acc = None
            for p, d in zip(parts[k * n_attn:(k + 1) * n_attn], DILATIONS):
                v = _from_dilated(p, scr, d, tm)
                acc = v if acc is None else acc + v
            groups.append(acc)
        groups += [p[...] for p in parts[3 * n_attn:]]
        dh = jnp.zeros((tm, D_MODEL), F32)
        for gi, grp in enumerate(groups):
            cols = slice(gi * aw, (gi + 1) * aw)
            gb = grp.astype(BF16)
            dproj_ref[:, cols] = gb
            dh = dh + _dot_nt(gb, w_ref[:, cols])
        dxa, g_c = _rms_bwd(dh, x_ref[...], g_ref[...], D_MODEL)
        _acc_rows(dg_ref, g_c)
        dx_ref[...] = dx1_ref[...] + dxa

    dm = D_MODEL
    return pl.pallas_call(
        body,
        name="in_proj_bwd",
        grid=(s // tm,),
        in_specs=[_dilated_spec(d, tm, aw) for d in DILATIONS] * 3 + [_row_spec(tm, aw)] * 4 + [
            _vmem_spec(), _row_spec(tm, dm), _const_spec(dm), _row_spec(tm, dm)],
        out_specs=[_row_spec(tm, dm), _row_spec(tm, IN_PROJ_WIDTH), _const_spec(dm)],
        out_shape=[jax.ShapeDtypeStruct((s, dm), F32), jax.ShapeDtypeStruct((s, IN_PROJ_WIDTH), BF16),
                   jax.ShapeDtypeStruct((1, dm), F32)],
        scratch_shapes=[pltpu.VMEM((aw // LANES, tm, LANES), F32)],
        compiler_params=_params(dimension_semantics=("arbitrary",)),
    )(*flat, w_in_b, x, g1, dx1)


def wgrad(a_b, b_b, tn, name, ts=1024, per_step=1, ride=None):
    s, k = a_b.shape
    n = b_b.shape[1]

    def body(a_ref, b_ref, o_ref):
        @pl.when(pl.program_id(1) == 0)
        def _():
            o_ref[...] = jnp.zeros_like(o_ref)

        a = a_ref[...]
        for jj in range(per_step):
            o_ref[jj] += _dot_tn(a, b_ref[:, jj * tn:(jj + 1) * tn])

    wide = tn * per_step
    gn, gs = n // wide, s // ts
    step = lambda j, i: (lambda: (pl.program_id(0) == j) & (pl.program_id(1) == i))
    e_in, e_out, e_shape, e_scr, e_args = _ride_specs(ride)
    out = pl.pallas_call(
        _riding(body, 2, 1, 0, ride, step(0, 0), step(gn // 2, 0), step(gn - 1, gs - 1)),
        name=name,
        grid=(gn, gs),
        in_specs=[pl.BlockSpec((ts, k), lambda j, i: (i, 0)), pl.BlockSpec((ts, wide), lambda j, i: (i, j))] + e_in,
        out_specs=[pl.BlockSpec((per_step, k, tn), lambda j, i: (j, 0, 0))] + e_out,
        out_shape=[jax.ShapeDtypeStruct((n // tn, k, tn), F32)] + e_shape,
        scratch_shapes=e_scr,
        compiler_params=_params(dimension_semantics=("arbitrary", "arbitrary")),
    )(a_b, b_b, *e_args)
    return out[0] if ride is None else out


def train_step(x, target, g1, an, logits, hn, gp, g_pre, g_post, w, m, v):
    nd = len(DILATIONS)
    shard_b = {k: w[k].astype(BF16) for k in BIG}
    (w_in_g,) = run_exchange(gather_exchange([shard_b["w_in"]]), "gather_w_in")
    w_in_b = w_in_g.transpose(1, 0, 2).reshape(D_MODEL, IN_PROJ_WIDTH)

    proj, h_b, *qkvs = in_proj_fwd(x, g1, w_in_b)
    attn_parts = [attn_fwd(qkv, d) for qkv, d in zip(qkvs, DILATIONS)]
    o_h, states, a_mat, w_out_g, w1_blocks = hgrn_fwd(
        proj, logits, ride=gather_exchange([shard_b["w_out"], shard_b["w_ff1"]]))
    w_out_b = w_out_g.reshape(D_MODEL, D_MODEL)
    x1, cat_b, mixed, attn, *lses, w2_g = mix_fwd(attn_parts, o_h, proj, an, hn, w_out_b, gp, x,
                                                  ride=gather_exchange([shard_b["w_ff2"]]))
    w2_b = w2_g.reshape(D_FF, D_MODEL)
    dx1, h2_b, a_b, du_b, dff_b, loss_vec, dg_pre, dg_post = mlp_fwd_bwd(x1, g_pre, w1_blocks, w2_b, g_post, target)
    dw2 = wgrad(a_b, dff_b, D_MODEL, "wgrad_ff2", ts=512)
    dw1 = wgrad(h2_b, du_b, D_FF // N_DEV, "wgrad_ff1", per_step=2)
    dmix_b, *rest = mix_bwd(dx1, mixed, gp, w_out_b, attn, an, o_h, proj, hn)
    d_os, deltas = rest[:nd], rest[nd:2 * nd]
    d_oh, dgate, dgp, dan, dhn = rest[2 * nd:]
    dwout = wgrad(cat_b, dmix_b, D_MODEL, "wgrad_out")

    early = ("w_out", "w_ff1", "w_ff2")
    early_grads = [dwout.reshape(N_DEV, D_MODEL // N_DEV, D_MODEL), dw1, dw2.reshape(N_DEV, D_FF // N_DEV, D_MODEL)]
    res = attn_bwd(qkvs[0], d_os[0], lses[0], deltas[0], DILATIONS[0], ride=to_core_exchange(early_grads))
    pairs = [pair_sum(g, s, f"pair_sum_{name}") for g, s, name in zip(early_grads, res[3:], early)]
    attn_grads = [res[:3]]
    *res, others_ff2 = attn_bwd(qkvs[1], d_os[1], lses[1], deltas[1], DILATIONS[1],
                                ride=to_chip_exchange([pairs[2][1]]))
    attn_grads.append(res)
    attn_grads.append(attn_bwd(qkvs[2], d_os[2], lses[2], deltas[2], DILATIONS[2]))
    dq_h, df_h, di_h, dlb, *others = hgrn_bwd(proj, logits, d_oh, states, a_mat,
                                              ride=to_chip_exchange([pairs[0][1], pairs[1][1]]))
    others.append(others_ff2)
    dx, dproj_b, dg1 = in_proj_bwd(attn_grads, (dq_h, df_h, di_h), dgate, w_in_b, x, g1, dx1)
    packed = _pack_small(dg1, dgp, dg_pre, dg_post, dan, dhn, dlb, loss_vec)
    dwin, small_slots = wgrad(h_b, dproj_b, 2 * IN_PROJ_WIDTH // N_DEV, "wgrad_in",
                              ride=small_exchange(packed))
    big = {name: sum_adamw(p[0], o, w[name], m[name], v[name], f"sum_adamw_{name}")
           for name, p, o in zip(early, pairs, others)}

    shard_w = IN_PROJ_WIDTH // N_DEV
    dwin_blocks = dwin.reshape(N_DEV // 2, D_MODEL, 2, shard_w).transpose(0, 2, 1, 3).reshape(N_DEV, D_MODEL, shard_w)
    (from_sibling,) = run_exchange(to_core_exchange([dwin_blocks.astype(BF16)]), "reduce_w_in_to_core")
    pair_in, pair_in_b = pair_sum(dwin_blocks, from_sibling, "pair_sum_w_in")
    (others_in,) = run_exchange(to_chip_exchange([pair_in_b]), "reduce_w_in_to_chip")
    big["w_in"] = sum_adamw(pair_in, others_in, w["w_in"], m["w_in"], v["w_in"], "sum_adamw_w_in")
    return dx, big, small_slots


def _position():
    x, y, c = lax.axis_index("x"), lax.axis_index("y"), lax.axis_index("c")
    other_chips = [(1 - x, y), (x, 1 - y), (1 - x, 1 - y)]
    return x, y, c, other_chips


def _any_spec():
    return pl.BlockSpec(memory_space=pl.ANY)


class Exchange:
    def __init__(self, arrays, out_shape, sems, stages):
        self.arrays, self.out_shape, self.sems, self.stages = list(arrays), list(out_shape), list(sems), stages


def gather_exchange(shards):
    n = len(shards)

    def stages(ins, outs, sems):
        send_sems, recv_sems, local_sems = sems

        def parts():
            x, y, c, chips = _position()
            me, sibling = (x, y, c), (x, y, 1 - c)

            def slot(a, px, py, pc):
                return outs[a].at[4 * px + 2 * py + pc]

            def copy(a, k, block, to, src=None):
                return pltpu.make_async_remote_copy(
                    src_ref=slot(a, *block) if src is None else src, dst_ref=slot(a, *block),
                    send_sem=send_sems.at[a, k], recv_sem=recv_sems.at[a, k], device_id=to, device_id_type=MESH)

            local = [pltpu.make_async_copy(ins[a], slot(a, *me), local_sems.at[a]) for a in range(n)]
            first = []
            for a in range(n):
                first.append(copy(a, 0, me, sibling, src=ins[a]))
                first += [copy(a, 1 + j, me, (*chip, c), src=ins[a]) for j, chip in enumerate(chips)]
            passed = [copy(a, 4 + j, (*chip, c), sibling) for j, chip in enumerate(chips) for a in range(n)]
            return c, chips, me, sibling, copy, local, first, passed

        def begin():
            _, _, _, _, _, local, first, _ = parts()
            for cp in local + first:
                cp.start()

        def middle():
            c, chips, me, _, copy, _, _, passed = parts()
            k = 0
            for j, chip in enumerate(chips):
                for a in range(n):
                    copy(a, 1 + j, (*chip, c), me).wait_recv()
                    passed[k].start()
                    k += 1

        def end():
            c, chips, me, sibling, copy, local, first, passed = parts()
            for a in range(n):
                copy(a, 0, sibling, me).wait_recv()
                for j, chip in enumerate(chips):
                    copy(a, 4 + j, (*chip, 1 - c), me).wait_recv()
            for cp in first + passed:
                cp.wait_send()
            for cp in local:
                cp.wait()

        return begin, middle, end

    return Exchange(
        shards, [jax.ShapeDtypeStruct((N_DEV,) + sh.shape, sh.dtype) for sh in shards],
        [pltpu.SemaphoreType.DMA((n, 7)), pltpu.SemaphoreType.DMA((n, 7)), pltpu.SemaphoreType.DMA((n,))], stages)


def to_core_exchange(grads):
    n = len(grads)

    def stages(ins, outs, sems):
        send_sems, recv_sems = sems

        def copies():
            x, y, c, _ = _position()
            return [pltpu.make_async_remote_copy(
                src_ref=ins[a].at[2 * q + (1 - c)], dst_ref=outs[a].at[q], send_sem=send_sems.at[a, q],
                recv_sem=recv_sems.at[a, q], device_id=(x, y, 1 - c), device_id_type=MESH)
                for a in range(n) for q in range(4)]

        def begin():
            for cp in copies():
                cp.start()

        def end():
            for cp in copies():
                cp.wait()

        return begin, None, end

    return Exchange(grads, [jax.ShapeDtypeStruct((4,) + g.shape[1:], g.dtype) for g in grads],
                    [pltpu.SemaphoreType.DMA((n, 4)), pltpu.SemaphoreType.DMA((n, 4))], stages)


def pair_sum(grad, from_sibling, name):
    _, r, cdim = grad.shape
    tr = min(r, ELEMENTWISE_ROWS)
    c_idx = lax.axis_index("c").astype(jnp.int32).reshape(1)

    def body(c_ref, g_ref, s_ref, o_ref, ob_ref):
        total = g_ref[...] + s_ref[...]
        o_ref[...] = total
        ob_ref[...] = total.astype(BF16)

    blk = lambda: pl.BlockSpec((1, tr, cdim), lambda q, i, cr: (q, i, 0))
    return pl.pallas_call(
        body,
        name=name,
        grid_spec=pltpu.PrefetchScalarGridSpec(
            num_scalar_prefetch=1,
            grid=(4, r // tr),
            in_specs=[pl.BlockSpec((1, tr, cdim), lambda q, i, cr: (2 * q + cr[0], i, 0)), blk()],
            out_specs=[blk(), blk()],
        ),
        out_shape=[jax.ShapeDtypeStruct((4, r, cdim), F32), jax.ShapeDtypeStruct((4, r, cdim), BF16)],
        compiler_params=_params(dimension_semantics=("arbitrary", "arbitrary")),
    )(c_idx, grad, from_sibling)


def to_chip_exchange(pairs):
    n = len(pairs)

    def stages(ins, outs, sems):
        send_sems, recv_sems = sems

        def copies():
            x, y, c, chips = _position()
            return [pltpu.make_async_remote_copy(
                src_ref=ins[a].at[2 * px + py], dst_ref=outs[a].at[j], send_sem=send_sems.at[a, j],
                recv_sem=recv_sems.at[a, j], device_id=(px, py, c), device_id_type=MESH)
                for a in range(n) for j, (px, py) in enumerate(chips)]

        def begin():
            for cp in copies():
                cp.start()

        def end():
            for cp in copies():
                cp.wait()

        return begin, None, end

    return Exchange(pairs, [jax.ShapeDtypeStruct((3,) + p.shape[1:], p.dtype) for p in pairs],
                    [pltpu.SemaphoreType.DMA((n, 3)), pltpu.SemaphoreType.DMA((n, 3))], stages)


def run_exchange(ex, name):
    n_in, n_out = len(ex.arrays), len(ex.out_shape)

    def body(*refs):
        begin, middle, end = ex.stages(refs[:n_in], refs[n_in:n_in + n_out], refs[n_in + n_out:])
        begin()
        if middle is not None:
            middle()
        end()

    return pl.pallas_call(
        body,
        name=name,
        in_specs=[_any_spec()] * n_in,
        out_specs=[_any_spec()] * n_out,
        out_shape=ex.out_shape,
        scratch_shapes=ex.sems,
    )(*ex.arrays)


def _riding(body, n_in, n_out, n_scratch, ex, first, middle, last):
    if ex is None:
        return body
    r_in, r_out = len(ex.arrays), len(ex.out_shape)

    def wrapped(*refs):
        k_in, refs = refs[:n_in], refs[n_in:]
        e_in, refs = refs[:r_in], refs[r_in:]
        k_out, refs = refs[:n_out], refs[n_out:]
        e_out, refs = refs[:r_out], refs[r_out:]
        k_scr, e_sems = refs[:n_scratch], refs[n_scratch:]
        begin, mid, end = ex.stages(e_in, e_out, e_sems)
        pl.when(first())(begin)
        body(*k_in, *k_out, *k_scr)
        if mid is not None:
            pl.when(middle())(mid)
        pl.when(last())(end)

    return wrapped


def _ride_specs(ex):
    if ex is None:
        return [], [], [], [], []
    return [_any_spec()] * len(ex.arrays), [_any_spec()] * len(ex.out_shape), ex.out_shape, ex.sems, ex.arrays


def _adamw(w, g, m, v):
    m = ADAM_B1 * m + (1.0 - ADAM_B1) * g
    v = ADAM_B2 * v + (1.0 - ADAM_B2) * (g * g)
    m_hat = m / (1.0 - ADAM_B1 ** ADAM_STEP)
    v_hat = v / (1.0 - ADAM_B2 ** ADAM_STEP)
    delta = -ADAM_LR * (m_hat / (jnp.sqrt(v_hat) + ADAM_EPS) + ADAM_WD * w)
    return delta, m, v


def sum_adamw(pairs, others, w, m, v, name):
    r, cdim = w.shape
    tr = min(r, ELEMENTWISE_ROWS // 2)
    chip_idx =(2 * lax.axis_index("x") + lax.axis_index("y")).astype(jnp.int32).reshape(1)

    def body(q_ref, p_ref, o_ref, w_ref, m_ref, v_ref, g_out, d_out, m_out, v_out):
        g = p_ref[0] + o_ref[0].astype(F32) + o_ref[1].astype(F32) + o_ref[2].astype(F32)
        g_out[...] = g
        d_out[...], m_out[...], v_out[...] = _adamw(w_ref[...], g, m_ref[...], v_ref[...])

    tile = lambda: pl.BlockSpec((tr, cdim), lambda i, qr: (i, 0))
    return pl.pallas_call(
        body,
        name=name,
        grid_spec=pltpu.PrefetchScalarGridSpec(
            num_scalar_prefetch=1,
            grid=(r // tr,),
            in_specs=[pl.BlockSpec((1, tr, cdim), lambda i, qr: (qr[0], i, 0)),
                      pl.BlockSpec((3, tr, cdim), lambda i, qr: (0, i, 0)), tile(), tile(), tile()],
            out_specs=[tile(), tile(), tile(), tile()],
        ),
        out_shape=[jax.ShapeDtypeStruct((r, cdim), F32)] * 4,
        compiler_params=_params(dimension_semantics=("arbitrary",)),
    )(chip_idx, pairs, others, w, m, v)


def small_exchange(packed):
    def stages(ins, outs, sems):
        send_sems, recv_sems, local_sem = sems
        (src,), (slots,) = ins, outs

        def copies():
            x, y, c, _ = _position()
            my_id = 4 * x + 2 * y + c
            sends, landings = [], []
            for rel in range(1, N_DEV):
                px = 1 - x if (rel >> 2) & 1 else x
                py = 1 - y if (rel >> 1) & 1 else y
                pc = 1 - c if rel & 1 else c
                peer = dict(send_sem=send_sems.at[rel - 1], recv_sem=recv_sems.at[rel - 1], device_id=(px, py, pc),
                            device_id_type=MESH)
                sends.append(pltpu.make_async_remote_copy(src_ref=src, dst_ref=slots.at[my_id], **peer))
                landings.append(pltpu.make_async_remote_copy(src_ref=src, dst_ref=slots.at[4 * px + 2 * py + pc], **peer))
            return pltpu.make_async_copy(src, slots.at[my_id], local_sem), sends, landings

        def begin():
            local, sends, _ = copies()
            local.start()
            for cp in sends:
                cp.start()

        def end():
            local, sends, landings = copies()
            for cp in landings:
                cp.wait_recv()
            for cp in sends:
                cp.wait_send()
            local.wait()

        return begin, None, end

    return Exchange([packed], [jax.ShapeDtypeStruct((N_DEV,) + packed.shape, packed.dtype)],
                    [pltpu.SemaphoreType.DMA((N_DEV - 1,)), pltpu.SemaphoreType.DMA((N_DEV - 1,)),
                     pltpu.SemaphoreType.DMA(())], stages)


def small_adamw(slots, w, m, v):
    def body(r_ref, w_ref, m_ref, v_ref, g_out, d_out, m_out, v_out, loss_out):
        red = r_ref[0]
        for k in range(1, N_DEV):
            red = red + r_ref[k]
        wv = w_ref[...]
        lb = _lower_bound(jnp.concatenate([wv[5:6, :HGRN_WIDTH], wv[5:6, HGRN_WIDTH:]], axis=0))
        t = red[5:6, :HGRN_WIDTH] * lb * (1.0 - lb)
        row = lax.broadcasted_iota(jnp.int32, red.shape, 0)
        g = jnp.where(row == 5, jnp.concatenate([t, -t], axis=1), jnp.where(row >= 6, 0.0, red))
        g_out[...] = g
        d_out[...], m_out[...], v_out[...] = _adamw(wv, g, m_ref[...], v_ref[...])
        loss = jnp.sum(red[6:7, :], axis=-1, keepdims=True) * (0.5 / D_MODEL)
        loss_out[...] = jnp.broadcast_to(loss, loss_out.shape)

    return pl.pallas_call(
        body,
        name="small_adamw",
        in_specs=[_vmem_spec()] * 4,
        out_specs=[_vmem_spec()] * 5,
        out_shape=[jax.ShapeDtypeStruct(w.shape, F32)] * 4 + [jax.ShapeDtypeStruct((SUBLANES, LANES), F32)],
    )(slots, w, m, v)


def _pack_small(g1, gp, g_pre, g_post, an, hn, logits_or_dlb, extra=None):
    row5 = logits_or_dlb.reshape(1, -1)
    row5 = jnp.pad(row5, ((0, 0), (0, D_MODEL - row5.shape[1])))
    row6 = jnp.zeros((1, D_MODEL), F32) if extra is None else extra
    return jnp.concatenate([g1, gp, g_pre, g_post, jnp.concatenate([an, hn], axis=1), row5, row6,
                            jnp.zeros((1, D_MODEL), F32)], axis=0)


def _unpack_small(p):
    return dict(mix_pre_norm=p[0:1], mix_post_norm=p[1:2], mlp_pre_norm=p[2:3], mlp_post_norm=p[3:4],
                attn_out_norm=p[4:5, :ATTN_WIDTH], hgrn_out_norm=p[4:5, ATTN_WIDTH:],
                hgrn_lb_logits=p[5].reshape(2, HGRN_WIDTH))


BIG = ("w_in", "w_out", "w_ff1", "w_ff2")
ORDER = ("mix_pre_norm", "w_in", "attn_out_norm", "hgrn_lb_logits", "hgrn_out_norm", "w_out", "mix_post_norm",
         "mlp_pre_norm", "w_ff1", "w_ff2", "mlp_post_norm")


def kernel(x, mix_pre_norm, w_in, attn_out_norm, hgrn_lb_logits, hgrn_out_norm, w_out, mix_post_norm, mlp_pre_norm, w_ff1, w_ff2, mlp_post_norm, loss_target, m_mix_pre_norm, m_w_in, m_attn_out_norm, m_hgrn_lb_logits, m_hgrn_out_norm, m_w_out, m_mix_post_norm, m_mlp_pre_norm, m_w_ff1, m_w_ff2, m_mlp_post_norm, v_mix_pre_norm, v_w_in, v_attn_out_norm, v_hgrn_lb_logits, v_hgrn_out_norm, v_w_out, v_mix_post_norm, v_mlp_pre_norm, v_w_ff1, v_w_ff2, v_mlp_post_norm):
    w = dict(w_in=w_in[0], w_out=w_out[0], w_ff1=w_ff1[0], w_ff2=w_ff2[0])
    m = dict(w_in=m_w_in[0], w_out=m_w_out[0], w_ff1=m_w_ff1[0], w_ff2=m_w_ff2[0])
    v = dict(w_in=v_w_in[0], w_out=v_w_out[0], w_ff1=v_w_ff1[0], w_ff2=v_w_ff2[0])

    dx, big, small_slots = train_step(x[0], loss_target[0], mix_pre_norm, attn_out_norm, hgrn_lb_logits, hgrn_out_norm,
                                      mix_post_norm, mlp_pre_norm, mlp_post_norm, w, m, v)

    pack = lambda a, b, c2, d, e, f, g: _pack_small(a, b, c2, d, e, f, g)
    w_s = pack(mix_pre_norm, mix_post_norm, mlp_pre_norm, mlp_post_norm, attn_out_norm, hgrn_out_norm, hgrn_lb_logits)
    m_s = pack(m_mix_pre_norm, m_mix_post_norm, m_mlp_pre_norm, m_mlp_post_norm, m_attn_out_norm, m_hgrn_out_norm,
               m_hgrn_lb_logits)
    v_s = pack(v_mix_pre_norm, v_mix_post_norm, v_mlp_pre_norm, v_mlp_post_norm, v_attn_out_norm, v_hgrn_out_norm,
               v_hgrn_lb_logits)
    g_s, d_s, nm_s, nv_s, loss = small_adamw(small_slots, w_s, m_s, v_s)
    small_out = [_unpack_small(t) for t in (g_s, d_s, nm_s, nv_s)]

    outs = [loss[0, 0], dx[None]]
    for kind in range(4):
        for name in ORDER:
            outs.append(big[name][kind][None] if name in BIG else small_out[kind][name])
    return tuple(outs)
```

```python
import jax
import jax.numpy as jnp
from jax import lax
from jax.experimental import pallas as pl
from jax.experimental.pallas import tpu as pltpu

F32 = jnp.float32
BF16 = jnp.bfloat16

D_MODEL = 1024
ATTN_WIDTH = 512
ATTN_HEAD_DIM = 64
ATTN_HEADS = 8
ATTN_BLOCK = 128
DILATIONS = (1, 4, 16)
HGRN_WIDTH = 512
HGRN_HEADS = 4
HGRN_HEAD_DIM = 128
HGRN_CHUNK = 64
IN_PROJ_WIDTH = 3584
D_FF = 4096
RMS_EPS = 1e-6
N_DEV = 8
ADAM_LR = 0.001
ADAM_B1 = 0.9
ADAM_B2 = 0.999
ADAM_EPS = 1e-08
ADAM_WD = 0.01
ADAM_STEP = 10

SUBLANES = 8
LANES = 128
COLUMN_UNROLL = 8
HGRN_CHUNKS_PER_STEP = 2
SUB_BLOCK = 16
TOKEN_TILE = 512
ELEMENTWISE_ROWS = 1024
MLP_TILE = 256
PROJ_TILE = 512
VMEM_BYTES_V7X = 64 * 1024 * 1024
VMEM_LIMIT = VMEM_BYTES_V7X // 8 * 7
NEG_BIG = -1e30
MESH = pl.DeviceIdType.MESH


def _params(**kw):
    return pltpu.CompilerParams(vmem_limit_bytes=VMEM_LIMIT, **kw)


def _vmem_spec():
    return pl.BlockSpec(memory_space=pltpu.VMEM)


def _dot(a, b):
    return jnp.dot(a, b, preferred_element_type=F32)


def _dot_nt(a, b):
    return lax.dot_general(a, b, (((1,), (1,)), ((), ())), preferred_element_type=F32)


def _dot_tn(a, b):
    return lax.dot_general(a, b, (((0,), (0,)), ((), ())), preferred_element_type=F32)


def _sigmoid(x):
    return 1.0 / (1.0 + jnp.exp(-x))


def _rms_fwd(x, gain, width):
    r = lax.rsqrt(jnp.sum(x * x, axis=-1, keepdims=True) * (1.0 / width) + RMS_EPS)
    return x * r * gain


def _rms_bwd(dy, x, gain, width):
    r = lax.rsqrt(jnp.sum(x * x, axis=-1, keepdims=True) * (1.0 / width) + RMS_EPS)
    xhat = x * r
    dxhat = dy * gain
    dx = r * (dxhat - xhat * (jnp.sum(dxhat * xhat, axis=-1, keepdims=True) * (1.0 / width)))
    return dx, dy * xhat


def _split3(x):
    hi = x.astype(BF16)
    r1 = x - hi.astype(F32)
    mid = r1.astype(BF16)
    lo = (r1 - mid.astype(F32)).astype(BF16)
    return hi, mid, lo


def _tri_sum(tri_bf16, x):
    hi, mid, lo = _split3(x)
    return _dot(tri_bf16, hi) + _dot(tri_bf16, mid) + _dot(tri_bf16, lo)


def _dilated_spec(d, tm, width):
    return pl.BlockSpec((d, tm // d, width), lambda i: (0, i, 0))


def _lane_blocks(ref, value):
    for c in range(ref.shape[0]):
        ref[c] = value[:, c * LANES:(c + 1) * LANES]


def _to_dilated(src_ref, dst_ref, d, tm, cast=None):
    for r in range(d):
        for c in range(src_ref.shape[0]):
            v = src_ref[c] if d == 1 else src_ref[c, pl.ds(r, tm // d, stride=d), :]
            dst_ref[r, :, c * LANES:(c + 1) * LANES] = v if cast is None else v.astype(cast)


def _from_dilated(src_ref, scratch_ref, d, tm):
    if d == 1:
        return src_ref[0].astype(F32)
    nblk = scratch_ref.shape[0]
    for r in range(d):
        for c in range(nblk):
            scratch_ref[c, pl.ds(r, tm // d, stride=d), :] = src_ref[r, :, c * LANES:(c + 1) * LANES].astype(F32)
    return jnp.concatenate([scratch_ref[c] for c in range(nblk)], axis=1)


def in_proj_fwd(x, g1, w_in_b):
    s = x.shape[0]
    tm = PROJ_TILE
    qkv_w = 3 * ATTN_WIDTH
    hg_w = IN_PROJ_WIDTH - qkv_w

    def body(x_ref, g_ref, w_ref, hg_ref, h_ref, *rest):
        qkv_refs, qkv_scr = rest[:len(DILATIONS)], rest[len(DILATIONS)]
        h = _rms_fwd(x_ref[...], g_ref[...], D_MODEL).astype(BF16)
        h_ref[...] = h
        proj = _dot(h, w_ref[...])
        hg_ref[...] = proj[:, qkv_w:]
        _lane_blocks(qkv_scr, proj[:, :qkv_w])
        for d, ref in zip(DILATIONS, qkv_refs):
            _to_dilated(qkv_scr, ref, d, tm, cast=BF16)

    return pl.pallas_call(
        body,
        name="in_proj_fwd",
        grid=(s // tm,),
        in_specs=[
            pl.BlockSpec((tm, D_MODEL), lambda i: (i, 0)),
            pl.BlockSpec((1, D_MODEL), lambda i: (0, 0)),
            _vmem_spec(),
        ],
        out_specs=[
            pl.BlockSpec((tm, hg_w), lambda i: (i, 0)),
            pl.BlockSpec((tm, D_MODEL), lambda i: (i, 0)),
        ] + [_dilated_spec(d, tm, qkv_w) for d in DILATIONS],
        out_shape=[jax.ShapeDtypeStruct((s, hg_w), F32), jax.ShapeDtypeStruct((s, D_MODEL), BF16)] + [
            jax.ShapeDtypeStruct((d, s // d, qkv_w), BF16) for d in DILATIONS],
        scratch_shapes=[pltpu.VMEM((qkv_w // LANES, tm, LANES), F32)],
        compiler_params=_params(dimension_semantics=("arbitrary",)),
    )(x, g1, w_in_b)


ATTN_SCALE = ATTN_HEAD_DIM ** -0.5


def _fill_attn_bias(bias_ref, dilation):
    qi = lax.broadcasted_iota(jnp.int32, (ATTN_BLOCK, 2 * ATTN_BLOCK), 0)
    kj = lax.broadcasted_iota(jnp.int32, (ATTN_BLOCK, 2 * ATTN_BLOCK), 1)
    dist = qi + ATTN_BLOCK - kj
    valid = (dist >= 0) & (dist <= ATTN_BLOCK)
    for head in range(ATTN_HEADS):
        slope = 2.0 ** (-8.0 * (head + 1) / ATTN_HEADS)
        bias = jnp.where(valid, dist.astype(F32) * (-slope * dilation), NEG_BIG)
        bias_ref[0, head] = bias
        bias_ref[1, head] = jnp.where(kj >= ATTN_BLOCK, bias, NEG_BIG)


def _stack_heads(x):
    low = _lane_half(x.shape, 0)
    zero = jnp.zeros_like(x)
    return jnp.concatenate([jnp.where(low, x, zero), jnp.where(low, zero, x)], axis=0)


def _unstack_heads(y):
    half = y.shape[0] // 2
    return jnp.where(_lane_half((half, y.shape[1]), 0), y[:half], y[half:])


def _attn_scores(q_stack, kcat, bias_ref, pair, first_block):
    f = first_block.astype(jnp.int32)
    bias = jnp.concatenate([bias_ref[f, 2 * pair], bias_ref[f, 2 * pair + 1]], axis=0)
    return _dot_nt(q_stack, kcat) + bias


def _lane_half(shape, sub):
    lane = lax.broadcasted_iota(jnp.int32, shape, 1)
    return (lane < ATTN_HEAD_DIM) if sub == 0 else (lane >= ATTN_HEAD_DIM)


def _sub_block(col, row):
    return pl.BlockSpec((None, ATTN_BLOCK, ATTN_WIDTH), lambda r, n: (r, row(n), col))


def attn_fwd(qkv, dilation):
    d, length, _ = qkv.shape
    assert d == dilation
    nb = length // ATTN_BLOCK

    def body(q_ref, kc_ref, kp_ref, vc_ref, vp_ref, o_ref, lse_ref, bias_ref):
        @pl.when((pl.program_id(0) == 0) & (pl.program_id(1) == 0))
        def _():
            _fill_attn_bias(bias_ref, d)

        first = pl.program_id(1) == 0
        for pair in range(ATTN_HEADS // 2):
            lanes = slice(pair * LANES, (pair + 1) * LANES)
            q_stack = _stack_heads(q_ref[:, lanes] * ATTN_SCALE)
            kcat = jnp.concatenate([kp_ref[:, lanes], kc_ref[:, lanes]], axis=0)
            vcat = jnp.concatenate([vp_ref[:, lanes], vc_ref[:, lanes]], axis=0)
            sc = _attn_scores(q_stack, kcat, bias_ref, pair, first)
            m = jnp.max(sc, axis=-1, keepdims=True)
            p = jnp.exp(sc - m)
            den = jnp.sum(p, axis=-1, keepdims=True)
            o_ref[:, lanes] = _unstack_heads(_dot(p.astype(BF16), vcat) / den).astype(BF16)
            lse_ref[:, lanes] = _unstack_heads(jnp.broadcast_to(m + jnp.log(den), (2 * ATTN_BLOCK, LANES)))

    cur = lambda n: n
    prev = lambda n: jnp.maximum(n - 1, 0)
    return pl.pallas_call(
        body,
        name=f"attn_fwd_d{d}",
        grid=(d, nb),
        in_specs=[_sub_block(0, cur), _sub_block(1, cur), _sub_block(1, prev), _sub_block(2, cur), _sub_block(2, prev)],
        out_specs=[_sub_block(0, cur), _sub_block(0, cur)],
        out_shape=[jax.ShapeDtypeStruct((d, length, ATTN_WIDTH), BF16), jax.ShapeDtypeStruct((d, length, ATTN_WIDTH), F32)],
        scratch_shapes=[pltpu.VMEM((2, ATTN_HEADS, ATTN_BLOCK, 2 * ATTN_BLOCK), F32)],
        compiler_params=_params(dimension_semantics=("arbitrary", "arbitrary")),
    )(qkv, qkv, qkv, qkv, qkv)


def attn_bwd(qkv, d_out, lse, delta, dilation, ride=None):
    d, length, _ = qkv.shape
    assert d == dilation
    nb = length // ATTN_BLOCK

    steps = d * nb + 1

    def body(q_ref, kc_ref, kp_ref, vc_ref, vp_ref, do_ref, lse_ref, dl_ref, dq_ref, dk_ref, dv_ref, ck_ref, cv_ref,
             bias_ref):
        t = pl.program_id(0)

        @pl.when(t == 0)
        def _():
            ck_ref[...] = jnp.zeros_like(ck_ref)
            cv_ref[...] = jnp.zeros_like(cv_ref)
            _fill_attn_bias(bias_ref, d)

        @pl.when(t < steps - 1)
        def _():
            first = t % nb == 0
            for pair in range(ATTN_HEADS // 2):
                lanes = slice(pair * LANES, (pair + 1) * LANES)
                q_stack = _stack_heads(q_ref[:, lanes] * ATTN_SCALE)
                do_stack = _stack_heads(do_ref[:, lanes])
                kcat = jnp.concatenate([kp_ref[:, lanes], kc_ref[:, lanes]], axis=0)
                vcat = jnp.concatenate([vp_ref[:, lanes], vc_ref[:, lanes]], axis=0)
                col_a, col_b = pair * LANES, pair * LANES + ATTN_HEAD_DIM
                lse_col = jnp.concatenate([lse_ref[:, col_a:col_a + 1], lse_ref[:, col_b:col_b + 1]], axis=0)
                dl_col = jnp.concatenate([dl_ref[:, col_a:col_a + 1], dl_ref[:, col_b:col_b + 1]], axis=0)
                p = jnp.exp(_attn_scores(q_stack, kcat, bias_ref, pair, first) - lse_col)
                ds = (p * (_dot_nt(do_stack, vcat) - dl_col)).astype(BF16)
                dq_ref[:, lanes] = (_unstack_heads(_dot(ds, kcat)) * ATTN_SCALE).astype(BF16)
                dk_cat = _dot_tn(ds, q_stack)
                dv_cat = _dot_tn(p.astype(BF16), do_stack)
                dk_ref[:, lanes] = (ck_ref[:, lanes] + dk_cat[:ATTN_BLOCK]).astype(BF16)
                dv_ref[:, lanes] = (cv_ref[:, lanes] + dv_cat[:ATTN_BLOCK]).astype(BF16)
                ck_ref[:, lanes] = dk_cat[ATTN_BLOCK:]
                cv_ref[:, lanes] = dv_cat[ATTN_BLOCK:]

        @pl.when(t == steps - 1)
        def _():
            dk_ref[...] = ck_ref[...].astype(BF16)
            dv_ref[...] = cv_ref[...].astype(BF16)

    blk = (ATTN_BLOCK, ATTN_WIDTH)

    def spec(col, shift):
        def index(t):
            f = jnp.minimum(t, steps - 2) if shift > -2 else jnp.maximum(t - 1, 0)
            r, n = f // nb, f % nb
            return (r, jnp.maximum(n - 1, 0) if shift == -1 else n, col)
        return pl.BlockSpec((None, ATTN_BLOCK, ATTN_WIDTH), index)

    step = lambda k: (lambda: pl.program_id(0) == k)
    e_in, e_out, e_shape, e_scr, e_args = _ride_specs(ride)
    return pl.pallas_call(
        _riding(body, 8, 3, 3, ride, step(0), step(steps // 2), step(steps - 1)),
        name=f"attn_bwd_d{d}",
        grid=(steps,),
        in_specs=[spec(0, 0), spec(1, 0), spec(1, -1), spec(2, 0), spec(2, -1), spec(0, 0), spec(0, 0), spec(0, 0)] + e_in,
        out_specs=[spec(0, 0), spec(0, -2), spec(0, -2)] + e_out,
        out_shape=[jax.ShapeDtypeStruct((d, length, ATTN_WIDTH), BF16)] * 3 + e_shape,
        scratch_shapes=[pltpu.VMEM(blk, F32), pltpu.VMEM(blk, F32),
                        pltpu.VMEM((2, ATTN_HEADS, ATTN_BLOCK, 2 * ATTN_BLOCK), F32)] + e_scr,
        compiler_params=_params(dimension_semantics=("arbitrary",)),
    )(qkv, qkv, qkv, qkv, qkv, d_out, lse, delta, *e_args)


def _lower_bound(logits):
    return _sigmoid(logits[0:1, :] - logits[1:2, :])


def _hgrn_gates(q, fp, lb):
    sq = _sigmoid(q)
    qf = q * sq
    sig = _sigmoid(fp)
    sig_neg = _sigmoid(-fp)
    kf = (1.0 - lb) * sig_neg
    log_sig = jnp.minimum(fp, 0.0) - jnp.log(1.0 + jnp.exp(-jnp.abs(fp)))
    a = jnp.log(lb)
    c = jnp.log(1.0 - lb) + log_sig
    log_f = jnp.maximum(a, c) + jnp.log(1.0 + jnp.exp(-jnp.abs(a - c)))
    return sq, qf, (sig, sig_neg, c), log_f, kf


def _tril_bf16(n, upper=False):
    r = lax.broadcasted_iota(jnp.int32, (n, n), 0)
    c = lax.broadcasted_iota(jnp.int32, (n, n), 1)
    keep = (c >= r) if upper else (c <= r)
    return jnp.where(keep, 1.0, 0.0).astype(BF16)


def _hgrn_diagonal_loops(c_len, diagonal):
    for half in range(SUB_BLOCK // SUBLANES):
        def step(jj, carry, half=half):
            j = half * SUBLANES + jj
            for i in range(c_len // SUB_BLOCK):
                diagonal(slice(i * SUB_BLOCK + half * SUBLANES, (i + 1) * SUB_BLOCK), j, i * SUB_BLOCK + j)
            return carry

        lax.fori_loop(0, SUBLANES, step, 0, unroll=COLUMN_UNROLL)


def _hgrn_off_diagonal(b, qf, kf):
    c_len, width = b.shape
    edges = [b[0:1, :]] + [b[i * SUB_BLOCK - 1:i * SUB_BLOCK, :] for i in range(1, c_len // SUB_BLOCK)]
    eq = jnp.exp(b - jnp.concatenate([jnp.broadcast_to(e, (SUB_BLOCK, width)) for e in edges], axis=0))
    q_til = qf * eq
    k_til, ek = [], []
    for i in range(1, c_len // SUB_BLOCK):
        n = i * SUB_BLOCK
        e = jnp.exp(edges[i] - b[:n, :])
        ek.append(e)
        k_til.append(jnp.concatenate([kf[:n, :] * e, jnp.zeros((2 * c_len - n, width), F32)], axis=0))
    return q_til, k_til, eq, ek


def _split2(x):
    hi = x.astype(BF16)
    return hi, (x - hi.astype(F32)).astype(BF16)


def hgrn_fwd(proj, lb, ride=None):
    s = proj.shape[0]
    c_len, nh, hd = HGRN_CHUNK, HGRN_HEADS, HGRN_HEAD_DIM
    n_chunks = s // c_len
    col0 = 0

    cps = 2 * HGRN_CHUNKS_PER_STEP
    n_steps = n_chunks // cps

    def body(q_ref, f_ref, i_ref, lb_ref, o_ref, st_out_ref, a_out_ref, st_ref, b_ref, qf_ref, kf_ref, a_ref):
        @pl.when(pl.program_id(0) == 0)
        def _():
            st_ref[...] = jnp.zeros_like(st_ref)

        lbv = _lower_bound(lb_ref[...])
        for u in range(cps):
            rs = slice(u * c_len, (u + 1) * c_len)
            b_u, qf_u, kf_u, a_u = b_ref.at[u], qf_ref.at[u], kf_ref.at[u], a_ref.at[u]
            _, qf, _, log_f, kf = _hgrn_gates(q_ref[rs, :], f_ref[rs, :], lbv)
            b = _tri_sum(_tril_bf16(c_len), log_f)
            b_u[...] = b
            qf_u[...] = qf
            kf_u[...] = kf
            a_u[...] = jnp.zeros_like(a_u)

            def diagonal(rows, j, key, b_u=b_u, qf_u=qf_u, kf_u=kf_u, a_u=a_u):
                bj = b_u[pl.ds(key, 1), :]
                kj = kf_u[pl.ds(key, 1), :]
                nrow = rows.stop - rows.start
                t_loc = lax.broadcasted_iota(jnp.int32, (nrow, nh * hd), 0) + (rows.start % SUB_BLOCK)
                e = jnp.exp(jnp.where(t_loc >= j, b_u[rows, :] - bj, NEG_BIG))
                prod = qf_u[rows, :] * kj * e
                lane = lax.broadcasted_iota(jnp.int32, (nrow, hd), 1)
                for h in range(nh):
                    col = jnp.sum(prod[:, h * hd:(h + 1) * hd], axis=-1, keepdims=True)
                    a_u[h, rows, :] = jnp.where(lane == key, col, a_u[h, rows, :])

            _hgrn_diagonal_loops(c_len, diagonal)
            q_til, k_til, _, _ = _hgrn_off_diagonal(b, qf, kf)
            q_til = q_til.astype(BF16)
            k_til = [k.astype(BF16) for k in k_til]

            b_last = b[c_len - 1:c_len, :]
            qb = (qf * jnp.exp(b)).astype(BF16)
            kb2 = (kf * jnp.exp(b_last - b)).astype(BF16)
            vf = i_ref[rs, :].astype(BF16)
            for h in range(nh):
                hs = slice(h * hd, (h + 1) * hd)
                st = st_ref[h]
                st_out_ref[u, h] = st
                off = [jnp.zeros((SUB_BLOCK, hd), F32)]
                for i in range(1, c_len // SUB_BLOCK):
                    off.append(_dot_nt(q_til[i * SUB_BLOCK:(i + 1) * SUB_BLOCK, hs], k_til[i - 1][:, hs]))
                a_h = a_u[h] + jnp.concatenate(off, axis=0)
                a_out_ref[rs, hs] = a_h
                o_ref[rs, hs] = _dot_nt(qb[:, hs], st.astype(BF16)) + _dot(a_h[:, :c_len].astype(BF16), vf[:, hs])
                st_ref[h] = st * jnp.exp(b_last[:, hs]) + _dot_tn(vf[:, hs], kb2[:, hs])

    blk = (cps * c_len, HGRN_WIDTH)
    sblk = (cps, c_len, HGRN_WIDTH)
    step = lambda k: (lambda: pl.program_id(0) == k)
    e_in, e_out, e_shape, e_scr, e_args = _ride_specs(ride)
    return pl.pallas_call(
        _riding(body, 4, 3, 5, ride, step(0), step((7 * n_steps) // 8), step(n_steps - 1)),
        name="hgrn_fwd",
        grid=(n_steps,),
        in_specs=[
            pl.BlockSpec(blk, lambda c: (c, col0)),
            pl.BlockSpec(blk, lambda c: (c, col0 + 1)),
            pl.BlockSpec(blk, lambda c: (c, col0 + 2)),
            pl.BlockSpec((2, HGRN_WIDTH), lambda c: (0, 0)),
        ] + e_in,
        out_specs=[
            pl.BlockSpec(blk, lambda c: (c, 0)),
            pl.BlockSpec((cps, nh, hd, hd), lambda c: (c, 0, 0, 0)),
            pl.BlockSpec(blk, lambda c: (c, 0)),
        ] + e_out,
        out_shape=[
            jax.ShapeDtypeStruct((s, HGRN_WIDTH), F32),
            jax.ShapeDtypeStruct((n_chunks, nh, hd, hd), F32),
            jax.ShapeDtypeStruct((s, nh * hd), F32),
        ] + e_shape,
        scratch_shapes=[
            pltpu.VMEM((nh, hd, hd), F32),
            pltpu.VMEM(sblk, F32),
            pltpu.VMEM(sblk, F32),
            pltpu.VMEM(sblk, F32),
            pltpu.VMEM((cps, nh, c_len, hd), F32),
        ] + e_scr,
        compiler_params=_params(dimension_semantics=("arbitrary",)),
    )(proj, proj, proj, lb, *e_args)


def hgrn_bwd(proj, lb, d_o, states, a_mat, ride=None):
    s = proj.shape[0]
    c_len, nh, hd = HGRN_CHUNK, HGRN_HEADS, HGRN_HEAD_DIM
    n_chunks = s // c_len
    col0 = 0
    cps = HGRN_CHUNKS_PER_STEP
    n_steps = n_chunks // cps
    last = n_steps - 1

    def body(q_ref, f_ref, i_ref, lb_ref, do_ref, st_in_ref, a_in_ref, dq_ref, df_ref, di_ref, dlb_ref,
             dst_ref, b_ref, qf_ref, kf_ref, da_ref, dqi_ref, dki_ref):
        @pl.when(pl.program_id(0) == 0)
        def _():
            dst_ref[...] = jnp.zeros_like(dst_ref)
            dlb_ref[...] = jnp.zeros_like(dlb_ref)

        lbv = _lower_bound(lb_ref[...])
        for u in reversed(range(cps)):
            rs = slice(u * c_len, (u + 1) * c_len)
            b_u, qf_u, kf_u, da_u, dqi_u, dki_u = (b_ref.at[u], qf_ref.at[u], kf_ref.at[u], da_ref.at[u], dqi_ref.at[u],
                                                   dki_ref.at[u])
            q = q_ref[rs, :]
            sq, qf, (sig, sig_neg, log_c), log_f, kf = _hgrn_gates(q, f_ref[rs, :], lbv)
            b = _tri_sum(_tril_bf16(c_len), log_f)
            b_u[...] = b
            qf_u[...] = qf
            kf_u[...] = kf
            b_last = b[c_len - 1:c_len, :]
            eb = jnp.exp(b)
            ebl = jnp.exp(b_last - b)
            qb = qf * eb
            kb2 = kf * ebl
            vf = i_ref[rs, :]
            d_o = do_ref[rs, :]
            qb_b, kb2_b, vf_b, do_b = qb.astype(BF16), kb2.astype(BF16), vf.astype(BF16), d_o.astype(BF16)
            tq = lax.broadcasted_iota(jnp.int32, (c_len, hd), 0)
            lane = lax.broadcasted_iota(jnp.int32, (c_len, hd), 1)

            dqb_parts, dvf_parts, dkb2_parts, dbl_parts = [], [], [], []
            for h in range(nh):
                hs = slice(h * hd, (h + 1) * hd)
                st = st_in_ref[u, h]
                dst = dst_ref[h]
                st_b, dst_b = st.astype(BF16), dst.astype(BF16)
                a_h = a_in_ref[rs, hs][:, :c_len].astype(BF16)
                dqb_parts.append(_dot(do_b[:, hs], st_b))
                dvf_parts.append(_dot_tn(a_h, do_b[:, hs]) + _dot_nt(kb2_b[:, hs], dst_b))
                dkb2_parts.append(_dot(vf_b[:, hs], dst_b))
                da = _dot_nt(do_b[:, hs], vf_b[:, hs])
                da = jnp.concatenate([da, jnp.zeros((c_len, hd - c_len), F32)], axis=1)
                da_u[h] = jnp.where(tq >= lane, da, 0.0)
                dbl_parts.append(jnp.sum(dst * st, axis=0, keepdims=True) * jnp.exp(b_last[:, hs]))
                dst_ref[h] = dst * jnp.exp(b_last[:, hs]) + _dot_tn(do_b[:, hs], qb_b[:, hs])
            dqb = jnp.concatenate(dqb_parts, axis=1)
            dvf = jnp.concatenate(dvf_parts, axis=1)
            dkb2 = jnp.concatenate(dkb2_parts, axis=1)
            dbl = jnp.concatenate(dbl_parts, axis=1) + jnp.sum(dkb2 * kb2, axis=0, keepdims=True)

            dqi_u[...] = jnp.zeros_like(dqi_u)
            t_idx = lax.broadcasted_iota(jnp.int32, (c_len, nh * hd), 0)

            def diagonal(rows, j, key, b_u=b_u, qf_u=qf_u, kf_u=kf_u, da_u=da_u, dqi_u=dqi_u, dki_u=dki_u):
                bj = b_u[pl.ds(key, 1), :]
                kj = kf_u[pl.ds(key, 1), :]
                nrow = rows.stop - rows.start
                t_loc = lax.broadcasted_iota(jnp.int32, (nrow, nh * hd), 0) + (rows.start % SUB_BLOCK)
                e = jnp.exp(jnp.where(t_loc >= j, b_u[rows, :] - bj, NEG_BIG))
                lane_r = lax.broadcasted_iota(jnp.int32, (nrow, hd), 1)
                cols = [jnp.sum(jnp.where(lane_r == key, da_u[h, rows, :], 0.0), axis=-1, keepdims=True)
                        for h in range(nh)]
                w = e * jnp.concatenate([jnp.broadcast_to(cc, (nrow, hd)) for cc in cols], axis=1)
                dqi_u[rows, :] += w * kj
                dki_u[pl.ds(key, 1), :] = jnp.sum(w * qf_u[rows, :], axis=0, keepdims=True)

            _hgrn_diagonal_loops(c_len, diagonal)

            q_til, k_til, eq, ek = _hgrn_off_diagonal(b, qf, kf)
            q_hi, q_lo = _split2(q_til)
            k_pairs = [_split2(k) for k in k_til]
            n_sub = c_len // SUB_BLOCK
            dq_heads, dk_heads = [], []
            for h in range(nh):
                hs = slice(h * hd, (h + 1) * hd)
                dq_rows = [jnp.zeros((SUB_BLOCK, hd), F32)]
                dk_h = jnp.zeros((c_len, hd), F32)
                for i in range(1, n_sub):
                    rows = slice(i * SUB_BLOCK, (i + 1) * SUB_BLOCK)
                    n = i * SUB_BLOCK
                    da_i = da_u[h, rows, :].astype(BF16)
                    k_hi, k_lo = k_pairs[i - 1]
                    dq_rows.append((_dot(da_i, k_hi[:, hs]) + _dot(da_i, k_lo[:, hs])) * eq[rows, hs])
                    dk_t = (_dot_tn(da_i, q_hi[rows, hs]) + _dot_tn(da_i, q_lo[rows, hs]))[:n, :] * ek[i - 1][:, hs]
                    dk_h = dk_h + jnp.concatenate([dk_t, jnp.zeros((c_len - n, hd), F32)], axis=0)
                dq_heads.append(jnp.concatenate(dq_rows, axis=0))
                dk_heads.append(dk_h)
            dq_intra = dqi_u[...] + jnp.concatenate(dq_heads, axis=1)
            dk_intra = dki_u[...] + jnp.concatenate(dk_heads, axis=1)

            db = dqb * qb + qf * dq_intra - kf * dk_intra - dkb2 * kb2
            db = db + jnp.where(t_idx == c_len - 1, dbl, 0.0)
            dg = _tri_sum(_tril_bf16(c_len, upper=True), db)
            dqf = dqb * eb + dq_intra
            dkf = dkb2 * ebl + dk_intra
            dq_ref[rs, :] = (dqf * (sq * (1.0 + q * (1.0 - sq)))).astype(BF16)
            df_ref[rs, :] = (sig_neg * (dg * jnp.exp(log_c - log_f) - dkf * (1.0 - lbv) * sig)).astype(BF16)
            di_ref[rs, :] = dvf.astype(BF16)
            dlb_ref[...] += jnp.sum(sig_neg * (dg * jnp.exp(-log_f) - dkf), axis=0, keepdims=True)

    blk = (cps * c_len, HGRN_WIDTH)
    sblk = (cps, c_len, HGRN_WIDTH)
    rev = lambda c: last - c
    step = lambda k: (lambda: pl.program_id(0) == k)
    e_in, e_out, e_shape, e_scr, e_args = _ride_specs(ride)
    return pl.pallas_call(
        _riding(body, 7, 4, 7, ride, step(0), step(n_steps // 2), step(last)),
        name="hgrn_bwd",
        grid=(n_steps,),
        in_specs=[
            pl.BlockSpec(blk, lambda c: (rev(c), col0)),
            pl.BlockSpec(blk, lambda c: (rev(c), col0 + 1)),
            pl.BlockSpec(blk, lambda c: (rev(c), col0 + 2)),
            pl.BlockSpec((2, HGRN_WIDTH), lambda c: (0, 0)),
            pl.BlockSpec(blk, lambda c: (rev(c), 0)),
            pl.BlockSpec((cps, nh, hd, hd), lambda c: (rev(c), 0, 0, 0)),
            pl.BlockSpec(blk, lambda c: (rev(c), 0)),
        ] + e_in,
        out_specs=[
            pl.BlockSpec(blk, lambda c: (rev(c), 0)),
            pl.BlockSpec(blk, lambda c: (rev(c), 0)),
            pl.BlockSpec(blk, lambda c: (rev(c), 0)),
            pl.BlockSpec((1, HGRN_WIDTH), lambda c: (0, 0)),
        ] + e_out,
        out_shape=[jax.ShapeDtypeStruct((s, HGRN_WIDTH), BF16)] * 3 + [jax.ShapeDtypeStruct((1, HGRN_WIDTH), F32)] + e_shape,
        scratch_shapes=[
            pltpu.VMEM((nh, hd, hd), F32),
            pltpu.VMEM(sblk, F32),
            pltpu.VMEM(sblk, F32),
            pltpu.VMEM(sblk, F32),
            pltpu.VMEM((cps, nh, c_len, hd), F32),
            pltpu.VMEM(sblk, F32),
            pltpu.VMEM(sblk, F32),
        ] + e_scr,
        compiler_params=_params(dimension_semantics=("arbitrary",)),
    )(proj, proj, proj, lb, d_o, states, a_mat, *e_args)


def _row_spec(tm, width, col=0):
    return pl.BlockSpec((tm, width), lambda i: (i, col))


def _const_spec(width):
    return pl.BlockSpec((1, width), lambda i: (0, 0))


def _acc_rows(ref, value):
    @pl.when(pl.program_id(0) == 0)
    def _():
        ref[...] = jnp.zeros_like(ref)

    ref[...] += jnp.sum(value, axis=0, keepdims=True)


def mix_fwd(attn_parts, o_h, proj, an, hn, w_out_b, gp, x, ride=None):
    s = x.shape[0]
    tm = TOKEN_TILE
    gate_col = 3
    hd = HGRN_HEAD_DIM
    nd = len(DILATIONS)

    def body(*refs):
        o_refs, l_refs = refs[:nd], refs[nd:2 * nd]
        oh_ref, gate_ref, an_ref, hn_ref, w_ref, gp_ref, x_ref = refs[2 * nd:2 * nd + 7]
        x1_ref, cat_ref, mixed_ref, attn_ref = refs[2 * nd + 7:2 * nd + 11]
        lse_refs = refs[2 * nd + 11:3 * nd + 11]
        o_scr, l_scr, lse_scr = refs[3 * nd + 11:]
        os_ = [_from_dilated(r, o_scr.at[k], d, tm) for k, (r, d) in enumerate(zip(o_refs, DILATIONS))]
        ls = [_from_dilated(r, l_scr.at[k], d, tm) for k, (r, d) in enumerate(zip(l_refs, DILATIONS))]
        m = jnp.maximum(jnp.maximum(ls[0], ls[1]), ls[2])
        es = [jnp.exp(l - m) for l in ls]
        den = es[0] + es[1] + es[2]
        attn = (es[0] * os_[0] + es[1] * os_[1] + es[2] * os_[2]) / den
        attn_ref[...] = attn
        _lane_blocks(lse_scr, m + jnp.log(den))
        for d, ref in zip(DILATIONS, lse_refs):
            _to_dilated(lse_scr, ref, d, tm)
        cat_ref[:, :ATTN_WIDTH] = _rms_fwd(attn, an_ref[...], ATTN_WIDTH).astype(BF16)
        gate = gate_ref[...]
        silu_g = gate * _sigmoid(gate)
        for h in range(HGRN_HEADS):
            hs = slice(h * hd, (h + 1) * hd)
            rec = _rms_fwd(oh_ref[:, hs], hn_ref[:, hs], hd) * silu_g[:, hs]
            cat_ref[:, ATTN_WIDTH + h * hd:ATTN_WIDTH + (h + 1) * hd] = rec.astype(BF16)
        mixed = _dot(cat_ref[...], w_ref[...])
        mixed_ref[...] = mixed
        x1_ref[...] = x_ref[...] + _rms_fwd(mixed, gp_ref[...], D_MODEL)

    aw = ATTN_WIDTH
    n_steps = s // tm
    step = lambda k: (lambda: pl.program_id(0) == k)
    e_in, e_out, e_shape, e_scr, e_args = _ride_specs(ride)
    return pl.pallas_call(
        _riding(body, 2 * nd + 7, 4 + nd, 3, ride, step(0), step((13 * n_steps) // 16), step(n_steps - 1)),
        name="mix_fwd",
        grid=(n_steps,),
        in_specs=[_dilated_spec(d, tm, aw) for d in DILATIONS] * 2 + [
            _row_spec(tm, aw), _row_spec(tm, aw, gate_col), _const_spec(aw), _const_spec(aw), _vmem_spec(),
            _const_spec(D_MODEL), _row_spec(tm, D_MODEL)] + e_in,
        out_specs=[_row_spec(tm, D_MODEL), _row_spec(tm, D_MODEL), _row_spec(tm, D_MODEL), _row_spec(tm, aw)] + [
            _dilated_spec(d, tm, aw) for d in DILATIONS] + e_out,
        out_shape=[
            jax.ShapeDtypeStruct((s, D_MODEL), F32),
            jax.ShapeDtypeStruct((s, D_MODEL), BF16),
            jax.ShapeDtypeStruct((s, D_MODEL), F32),
            jax.ShapeDtypeStruct((s, aw), F32),
        ] + [jax.ShapeDtypeStruct((d, s // d, aw), F32) for d in DILATIONS] + e_shape,
        scratch_shapes=[pltpu.VMEM((nd, aw // LANES, tm, LANES), F32), pltpu.VMEM((nd, aw // LANES, tm, LANES), F32),
                        pltpu.VMEM((aw // LANES, tm, LANES), F32)] + e_scr,
        compiler_params=_params(dimension_semantics=("arbitrary",)),
    )(*[p[0] for p in attn_parts], *[p[1] for p in attn_parts], o_h, proj, an, hn, w_out_b, gp, x, *e_args)


def mix_bwd(dx1, mixed, gp, w_out_b, attn, an, o_h, proj, hn):
    s = dx1.shape[0]
    tm = TOKEN_TILE
    gate_col = 3
    hd = HGRN_HEAD_DIM
    aw = ATTN_WIDTH

    nd = len(DILATIONS)

    def body(*refs):
        dx1_ref, mixed_ref, gp_ref, w_ref, attn_ref, an_ref, oh_ref, gate_ref, hn_ref, dmix_ref = refs[:10]
        do_refs, delta_refs = refs[10:10 + nd], refs[10 + nd:10 + 2 * nd]
        doh_ref, dgate_ref, dgp_ref, dan_ref, dhn_ref, do_ref, delta_ref = refs[10 + 2 * nd:]
        dmixed, gp_c = _rms_bwd(dx1_ref[...], mixed_ref[...], gp_ref[...], D_MODEL)
        _acc_rows(dgp_ref, gp_c)
        dmixed_b = dmixed.astype(BF16)
        dmix_ref[...] = dmixed_b
        dcat = _dot_nt(dmixed_b, w_ref[...])
        attn = attn_ref[...]
        d_o, an_c = _rms_bwd(dcat[:, :aw], attn, an_ref[...], aw)
        _acc_rows(dan_ref, an_c)
        _lane_blocks(do_ref, d_o)
        prod = d_o * attn
        for pair in range(ATTN_HEADS // 2):
            pp = prod[:, pair * LANES:(pair + 1) * LANES]
            low = _lane_half((tm, LANES), 0)
            lo = jnp.sum(jnp.where(low, pp, 0.0), axis=-1, keepdims=True)
            hi = jnp.sum(jnp.where(low, 0.0, pp), axis=-1, keepdims=True)
            delta_ref[pair] = jnp.where(low, lo, hi)
        for d, o_ref, l_ref in zip(DILATIONS, do_refs, delta_refs):
            _to_dilated(do_ref, o_ref, d, tm, cast=BF16)
            _to_dilated(delta_ref, l_ref, d, tm)
        gate = gate_ref[...]
        sg = _sigmoid(gate)
        silu_g = gate * sg
        drec = dcat[:, aw:]
        hn_parts = []
        for h in range(HGRN_HEADS):
            hs = slice(h * hd, (h + 1) * hd)
            oh = oh_ref[:, hs]
            on = _rms_fwd(oh, hn_ref[:, hs], hd)
            dgate_ref[:, hs] = (drec[:, hs] * on * (sg[:, hs] * (1.0 + gate[:, hs] * (1.0 - sg[:, hs])))).astype(BF16)
            d_oh, hn_c = _rms_bwd(drec[:, hs] * silu_g[:, hs], oh, hn_ref[:, hs], hd)
            doh_ref[:, hs] = d_oh
            hn_parts.append(hn_c)
        _acc_rows(dhn_ref, jnp.concatenate(hn_parts, axis=1))

    return pl.pallas_call(
        body,
        name="mix_bwd",
        grid=(s // tm,),
        in_specs=[_row_spec(tm, D_MODEL), _row_spec(tm, D_MODEL), _const_spec(D_MODEL), _vmem_spec(), _row_spec(tm, aw),
                  _const_spec(aw), _row_spec(tm, aw), _row_spec(tm, aw, gate_col), _const_spec(aw)],
        out_specs=[_row_spec(tm, D_MODEL)] + [_dilated_spec(d, tm, aw) for d in DILATIONS] * 2 + [_row_spec(tm, aw)] * 2 + [
            _const_spec(D_MODEL), _const_spec(aw), _const_spec(aw)],
        out_shape=[jax.ShapeDtypeStruct((s, D_MODEL), BF16)] + [
            jax.ShapeDtypeStruct((d, s // d, aw), BF16) for d in DILATIONS] + [
            jax.ShapeDtypeStruct((d, s // d, aw), F32) for d in DILATIONS] + [
            jax.ShapeDtypeStruct((s, aw), F32), jax.ShapeDtypeStruct((s, aw), BF16),
            jax.ShapeDtypeStruct((1, D_MODEL), F32), jax.ShapeDtypeStruct((1, aw), F32),
            jax.ShapeDtypeStruct((1, aw), F32)],
        scratch_shapes=[pltpu.VMEM((aw // LANES, tm, LANES), F32), pltpu.VMEM((aw // LANES, tm, LANES), F32)],
        compiler_params=_params(dimension_semantics=("arbitrary",)),
    )(dx1, mixed, gp, w_out_b, attn, an, o_h, proj, hn)


def mlp_fwd_bwd(x1, g_pre, w1_blocks, w2_b, g_post, target):
    s = x1.shape[0]
    tm = MLP_TILE
    nblk, _, fb = w1_blocks.shape

    def body(x1_ref, gpre_ref, w1_ref, w2_ref, gpost_ref, t_ref,
             dx1_ref, h2_ref, a_ref, du_ref, dff_ref, loss_ref, dgpre_ref, dgpost_ref, u_ref):
        x1v = x1_ref[...]
        h2 = _rms_fwd(x1v, gpre_ref[...], D_MODEL).astype(BF16)
        h2_ref[...] = h2
        ff = jnp.zeros((tm, D_MODEL), F32)
        for j in range(nblk):
            cols = slice(j * fb, (j + 1) * fb)
            ru = jnp.maximum(_dot(h2, w1_ref[j]), 0.0)
            u_ref[:, cols] = ru.astype(BF16)
            a = (ru * ru).astype(BF16)
            a_ref[:, cols] = a
            ff = ff + _dot(a, w2_ref[cols, :])
        diff = x1v + _rms_fwd(ff, gpost_ref[...], D_MODEL) - t_ref[...]
        _acc_rows(loss_ref, diff * diff)
        dy = diff * (1.0 / D_MODEL)
        dff, gpost_c = _rms_bwd(dy, ff, gpost_ref[...], D_MODEL)
        _acc_rows(dgpost_ref, gpost_c)
        dff_b = dff.astype(BF16)
        dff_ref[...] = dff_b
        dh2 = jnp.zeros((tm, D_MODEL), F32)
        for j in range(nblk):
            cols = slice(j * fb, (j + 1) * fb)
            du = (_dot_nt(dff_b, w2_ref[cols, :]) * (2.0 * u_ref[:, cols])).astype(BF16)
            du_ref[:, cols] = du
            dh2 = dh2 + _dot_nt(du, w1_ref[j])
        dxa, gpre_c = _rms_bwd(dh2, x1v, gpre_ref[...], D_MODEL)
        _acc_rows(dgpre_ref, gpre_c)
        dx1_ref[...] = dy + dxa

    dm = D_MODEL
    return pl.pallas_call(
        body,
        name="mlp_fwd_bwd",
        grid=(s // tm,),
        in_specs=[_row_spec(tm, dm), _const_spec(dm), _vmem_spec(), _vmem_spec(), _const_spec(dm), _row_spec(tm, dm)],
        out_specs=[_row_spec(tm, dm), _row_spec(tm, dm), _row_spec(tm, D_FF), _row_spec(tm, D_FF), _row_spec(tm, dm),
                   _const_spec(dm), _const_spec(dm), _const_spec(dm)],
        out_shape=[
            jax.ShapeDtypeStruct((s, dm), F32),
            jax.ShapeDtypeStruct((s, dm), BF16),
            jax.ShapeDtypeStruct((s, D_FF), BF16),
            jax.ShapeDtypeStruct((s, D_FF), BF16),
            jax.ShapeDtypeStruct((s, dm), BF16),
            jax.ShapeDtypeStruct((1, dm), F32),
            jax.ShapeDtypeStruct((1, dm), F32),
            jax.ShapeDtypeStruct((1, dm), F32),
        ],
        scratch_shapes=[pltpu.VMEM((tm, D_FF), BF16)],
        compiler_params=_params(dimension_semantics=("arbitrary",)),
    )(x1, g_pre, w1_blocks, w2_b, g_post, target)


def in_proj_bwd(attn_grads, hgrn_grads, dgate, w_in_b, x, g1, dx1):
    s = x.shape[0]
    tm = PROJ_TILE
    aw = ATTN_WIDTH
    n_attn = len(attn_grads)
    flat = [g[k] for k in range(3) for g in attn_grads] + list(hgrn_grads) + [dgate]

    def body(*refs):
        parts = refs[:len(flat)]
        w_ref, x_ref, g_ref, dx1_ref, dx_ref, dproj_ref, dg_ref, scr = refs[len(flat):]
        groups = []
        for k in range(3):
            acc = None
            for p, d in zip(parts[k * n_attn:(k + 1) * n_attn], DILATIONS):
                v = _from_dilated(p, scr, d, tm)
                acc = v if acc is None else acc + v
            groups.append(acc)
        groups += [p[...] for p in parts[3 * n_attn:]]
        dh = jnp.zeros((tm, D_MODEL), F32)
        for gi, grp in enumerate(groups):
            cols = slice(gi * aw, (gi + 1) * aw)
            gb = grp.astype(BF16)
            dproj_ref[:, cols] = gb
            dh = dh + _dot_nt(gb, w_ref[:, cols])
        dxa, g_c = _rms_bwd(dh, x_ref[...], g_ref[...], D_MODEL)
        _acc_rows(dg_ref, g_c)
        dx_ref[...] = dx1_ref[...] + dxa

    dm = D_MODEL
    return pl.pallas_call(
        body,
        name="in_proj_bwd",
        grid=(s // tm,),
        in_specs=[_dilated_spec(d, tm, aw) for d in DILATIONS] * 3 + [_row_spec(tm, aw)] * 4 + [
            _vmem_spec(), _row_spec(tm, dm), _const_spec(dm), _row_spec(tm, dm)],
        out_specs=[_row_spec(tm, dm), _row_spec(tm, IN_PROJ_WIDTH), _const_spec(dm)],
        out_shape=[jax.ShapeDtypeStruct((s, dm), F32), jax.ShapeDtypeStruct((s, IN_PROJ_WIDTH), BF16),
                   jax.ShapeDtypeStruct((1, dm), F32)],
        scratch_shapes=[pltpu.VMEM((aw // LANES, tm, LANES), F32)],
        compiler_params=_params(dimension_semantics=("arbitrary",)),
    )(*flat, w_in_b, x, g1, dx1)


def wgrad(a_b, b_b, tn, name, ts=2048, per_step=1, ride=None):
    s, k = a_b.shape
    n = b_b.shape[1]

    def body(a_ref, b_ref, o_ref):
        @pl.when(pl.program_id(1) == 0)
        def _():
            o_ref[...] = jnp.zeros_like(o_ref)

        a = a_ref[...]
        for jj in range(per_step):
            o_ref[jj] += _dot_tn(a, b_ref[:, jj * tn:(jj + 1) * tn])

    wide = tn * per_step
    gn, gs = n // wide, s // ts
    step = lambda j, i: (lambda: (pl.program_id(0) == j) & (pl.program_id(1) == i))
    e_in, e_out, e_shape, e_scr, e_args = _ride_specs(ride)
    out = pl.pallas_call(
        _riding(body, 2, 1, 0, ride, step(0, 0), step(gn // 2, 0), step(gn - 1, gs - 1)),
        name=name,
        grid=(gn, gs),
        in_specs=[pl.BlockSpec((ts, k), lambda j, i: (i, 0)), pl.BlockSpec((ts, wide), lambda j, i: (i, j))] + e_in,
        out_specs=[pl.BlockSpec((per_step, k, tn), lambda j, i: (j, 0, 0))] + e_out,
        out_shape=[jax.ShapeDtypeStruct((n // tn, k, tn), F32)] + e_shape,
        scratch_shapes=e_scr,
        compiler_params=_params(dimension_semantics=("arbitrary", "arbitrary")),
    )(a_b, b_b, *e_args)
    return out[0] if ride is None else out


def train_step(x, target, g1, an, logits, hn, gp, g_pre, g_post, w, m, v):
    nd = len(DILATIONS)
    shard_b = {k: w[k].astype(BF16) for k in BIG}
    (w_in_g,) = run_exchange(gather_exchange([shard_b["w_in"]]), "gather_w_in")
    w_in_b = w_in_g.transpose(1, 0, 2).reshape(D_MODEL, IN_PROJ_WIDTH)

    proj, h_b, *qkvs = in_proj_fwd(x, g1, w_in_b)
    attn_parts = [attn_fwd(qkv, d) for qkv, d in zip(qkvs, DILATIONS)]
    o_h, states, a_mat, w_out_g, w1_blocks = hgrn_fwd(
        proj, logits, ride=gather_exchange([shard_b["w_out"], shard_b["w_ff1"]]))
    w_out_b = w_out_g.reshape(D_MODEL, D_MODEL)
    x1, cat_b, mixed, attn, *lses, w2_g = mix_fwd(attn_parts, o_h, proj, an, hn, w_out_b, gp, x,
                                                  ride=gather_exchange([shard_b["w_ff2"]]))
    w2_b = w2_g.reshape(D_FF, D_MODEL)
    dx1, h2_b, a_b, du_b, dff_b, loss_vec, dg_pre, dg_post = mlp_fwd_bwd(x1, g_pre, w1_blocks, w2_b, g_post, target)
    dw2 = wgrad(a_b, dff_b, D_MODEL, "wgrad_ff2", ts=512)
    dw1 = wgrad(h2_b, du_b, D_FF // N_DEV, "wgrad_ff1", per_step=2)
    dmix_b, *rest = mix_bwd(dx1, mixed, gp, w_out_b, attn, an, o_h, proj, hn)
    d_os, deltas = rest[:nd], rest[nd:2 * nd]
    d_oh, dgate, dgp, dan, dhn = rest[2 * nd:]
    dwout = wgrad(cat_b, dmix_b, D_MODEL, "wgrad_out")

    early = ("w_out", "w_ff1", "w_ff2")
    early_grads = [dwout.reshape(N_DEV, D_MODEL // N_DEV, D_MODEL), dw1, dw2.reshape(N_DEV, D_FF // N_DEV, D_MODEL)]
    res = attn_bwd(qkvs[0], d_os[0], lses[0], deltas[0], DILATIONS[0], ride=to_core_exchange(early_grads))
    pairs = [pair_sum(g, s, f"pair_sum_{name}") for g, s, name in zip(early_grads, res[3:], early)]
    attn_grads = [res[:3]]
    *res, others_ff2 = attn_bwd(qkvs[1], d_os[1], lses[1], deltas[1], DILATIONS[1],
                                ride=to_chip_exchange([pairs[2][1]]))
    attn_grads.append(res)
    attn_grads.append(attn_bwd(qkvs[2], d_os[2], lses[2], deltas[2], DILATIONS[2]))
    dq_h, df_h, di_h, dlb, *others = hgrn_bwd(proj, logits, d_oh, states, a_mat,
                                              ride=to_chip_exchange([pairs[0][1], pairs[1][1]]))
    others.append(others_ff2)
    dx, dproj_b, dg1 = in_proj_bwd(attn_grads, (dq_h, df_h, di_h), dgate, w_in_b, x, g1, dx1)
    packed = _pack_small(dg1, dgp, dg_pre, dg_post, dan, dhn, dlb, loss_vec)
    dwin, small_slots = wgrad(h_b, dproj_b, 2 * IN_PROJ_WIDTH // N_DEV, "wgrad_in",
                              ride=small_exchange(packed))
    big = {name: sum_adamw(p[0], o, w[name], m[name], v[name], f"sum_adamw_{name}")
           for name, p, o in zip(early, pairs, others)}

    shard_w = IN_PROJ_WIDTH // N_DEV
    dwin_blocks = dwin.reshape(N_DEV // 2, D_MODEL, 2, shard_w).transpose(0, 2, 1, 3).reshape(N_DEV, D_MODEL, shard_w)
    (from_sibling,) = run_exchange(to_core_exchange([dwin_blocks.astype(BF16)]), "reduce_w_in_to_core")
    pair_in, pair_in_b = pair_sum(dwin_blocks, from_sibling, "pair_sum_w_in")
    (others_in,) = run_exchange(to_chip_exchange([pair_in_b]), "reduce_w_in_to_chip")
    big["w_in"] = sum_adamw(pair_in, others_in, w["w_in"], m["w_in"], v["w_in"], "sum_adamw_w_in")
    return dx, big, small_slots


def _position():
    x, y, c = lax.axis_index("x"), lax.axis_index("y"), lax.axis_index("c")
    other_chips = [(1 - x, y), (x, 1 - y), (1 - x, 1 - y)]
    return x, y, c, other_chips


def _any_spec():
    return pl.BlockSpec(memory_space=pl.ANY)


class Exchange:
    def __init__(self, arrays, out_shape, sems, stages):
        self.arrays, self.out_shape, self.sems, self.stages = list(arrays), list(out_shape), list(sems), stages


def gather_exchange(shards):
    n = len(shards)

    def stages(ins, outs, sems):
        send_sems, recv_sems, local_sems = sems

        def parts():
            x, y, c, chips = _position()
            me, sibling = (x, y, c), (x, y, 1 - c)

            def slot(a, px, py, pc):
                return outs[a].at[4 * px + 2 * py + pc]

            def copy(a, k, block, to, src=None):
                return pltpu.make_async_remote_copy(
                    src_ref=slot(a, *block) if src is None else src, dst_ref=slot(a, *block),
                    send_sem=send_sems.at[a, k], recv_sem=recv_sems.at[a, k], device_id=to, device_id_type=MESH)

            local = [pltpu.make_async_copy(ins[a], slot(a, *me), local_sems.at[a]) for a in range(n)]
            first = []
            for a in range(n):
                first.append(copy(a, 0, me, sibling, src=ins[a]))
                first += [copy(a, 1 + j, me, (*chip, c), src=ins[a]) for j, chip in enumerate(chips)]
            passed = [copy(a, 4 + j, (*chip, c), sibling) for j, chip in enumerate(chips) for a in range(n)]
            return c, chips, me, sibling, copy, local, first, passed

        def begin():
            _, _, _, _, _, local, first, _ = parts()
            for cp in local + first:
                cp.start()

        def middle():
            c, chips, me, _, copy, _, _, passed = parts()
            k = 0
            for j, chip in enumerate(chips):
                for a in range(n):
                    copy(a, 1 + j, (*chip, c), me).wait_recv()
                    passed[k].start()
                    k += 1

        def end():
            c, chips, me, sibling, copy, local, first, passed = parts()
            for a in range(n):
                copy(a, 0, sibling, me).wait_recv()
                for j, chip in enumerate(chips):
                    copy(a, 4 + j, (*chip, 1 - c), me).wait_recv()
            for cp in first + passed:
                cp.wait_send()
            for cp in local:
                cp.wait()

        return begin, middle, end

    return Exchange(
        shards, [jax.ShapeDtypeStruct((N_DEV,) + sh.shape, sh.dtype) for sh in shards],
        [pltpu.SemaphoreType.DMA((n, 7)), pltpu.SemaphoreType.DMA((n, 7)), pltpu.SemaphoreType.DMA((n,))], stages)


def to_core_exchange(grads):
    n = len(grads)

    def stages(ins, outs, sems):
        send_sems, recv_sems = sems

        def copies():
            x, y, c, _ = _position()
            return [pltpu.make_async_remote_copy(
                src_ref=ins[a].at[2 * q + (1 - c)], dst_ref=outs[a].at[q], send_sem=send_sems.at[a, q],
                recv_sem=recv_sems.at[a, q], device_id=(x, y, 1 - c), device_id_type=MESH)
                for a in range(n) for q in range(4)]

        def begin():
            for cp in copies():
                cp.start()

        def end():
            for cp in copies():
                cp.wait()

        return begin, None, end

    return Exchange(grads, [jax.ShapeDtypeStruct((4,) + g.shape[1:], g.dtype) for g in grads],
                    [pltpu.SemaphoreType.DMA((n, 4)), pltpu.SemaphoreType.DMA((n, 4))], stages)


def pair_sum(grad, from_sibling, name):
    _, r, cdim = grad.shape
    tr = min(r, ELEMENTWISE_ROWS)
    c_idx = lax.axis_index("c").astype(jnp.int32).reshape(1)

    def body(c_ref, g_ref, s_ref, o_ref, ob_ref):
        total = g_ref[...] + s_ref[...]
        o_ref[...] = total
        ob_ref[...] = total.astype(BF16)

    blk = lambda: pl.BlockSpec((1, tr, cdim), lambda q, i, cr: (q, i, 0))
    return pl.pallas_call(
        body,
        name=name,
        grid_spec=pltpu.PrefetchScalarGridSpec(
            num_scalar_prefetch=1,
            grid=(4, r // tr),
            in_specs=[pl.BlockSpec((1, tr, cdim), lambda q, i, cr: (2 * q + cr[0], i, 0)), blk()],
            out_specs=[blk(), blk()],
        ),
        out_shape=[jax.ShapeDtypeStruct((4, r, cdim), F32), jax.ShapeDtypeStruct((4, r, cdim), BF16)],
        compiler_params=_params(dimension_semantics=("arbitrary", "arbitrary")),
    )(c_idx, grad, from_sibling)


def to_chip_exchange(pairs):
    n = len(pairs)

    def stages(ins, outs, sems):
        send_sems, recv_sems = sems

        def copies():
            x, y, c, chips = _position()
            return [pltpu.make_async_remote_copy(
                src_ref=ins[a].at[2 * px + py], dst_ref=outs[a].at[j], send_sem=send_sems.at[a, j],
                recv_sem=recv_sems.at[a, j], device_id=(px, py, c), device_id_type=MESH)
                for a in range(n) for j, (px, py) in enumerate(chips)]

        def begin():
            for cp in copies():
                cp.start()

        def end():
            for cp in copies():
                cp.wait()

        return begin, None, end

    return Exchange(pairs, [jax.ShapeDtypeStruct((3,) + p.shape[1:], p.dtype) for p in pairs],
                    [pltpu.SemaphoreType.DMA((n, 3)), pltpu.SemaphoreType.DMA((n, 3))], stages)


def run_exchange(ex, name):
    n_in, n_out = len(ex.arrays), len(ex.out_shape)

    def body(*refs):
        begin, middle, end = ex.stages(refs[:n_in], refs[n_in:n_in + n_out], refs[n_in + n_out:])
        begin()
        if middle is not None:
            middle()
        end()

    return pl.pallas_call(
        body,
        name=name,
        in_specs=[_any_spec()] * n_in,
        out_specs=[_any_spec()] * n_out,
        out_shape=ex.out_shape,
        scratch_shapes=ex.sems,
    )(*ex.arrays)


def _riding(body, n_in, n_out, n_scratch, ex, first, middle, last):
    if ex is None:
        return body
    r_in, r_out = len(ex.arrays), len(ex.out_shape)

    def wrapped(*refs):
        k_in, refs = refs[:n_in], refs[n_in:]
        e_in, refs = refs[:r_in], refs[r_in:]
        k_out, refs = refs[:n_out], refs[n_out:]
        e_out, refs = refs[:r_out], refs[r_out:]
        k_scr, e_sems = refs[:n_scratch], refs[n_scratch:]
        begin, mid, end = ex.stages(e_in, e_out, e_sems)
        pl.when(first())(begin)
        body(*k_in, *k_out, *k_scr)
        if mid is not None:
            pl.when(middle())(mid)
        pl.when(last())(end)

    return wrapped


def _ride_specs(ex):
    if ex is None:
        return [], [], [], [], []
    return [_any_spec()] * len(ex.arrays), [_any_spec()] * len(ex.out_shape), ex.out_shape, ex.sems, ex.arrays


def _adamw(w, g, m, v):
    m = ADAM_B1 * m + (1.0 - ADAM_B1) * g
    v = ADAM_B2 * v + (1.0 - ADAM_B2) * (g * g)
    m_hat = m / (1.0 - ADAM_B1 ** ADAM_STEP)
    v_hat = v / (1.0 - ADAM_B2 ** ADAM_STEP)
    delta = -ADAM_LR * (m_hat / (jnp.sqrt(v_hat) + ADAM_EPS) + ADAM_WD * w)
    return delta, m, v


def sum_adamw(pairs, others, w, m, v, name):
    r, cdim = w.shape
    tr = min(r, ELEMENTWISE_ROWS // 2)
    chip_idx =(2 * lax.axis_index("x") + lax.axis_index("y")).astype(jnp.int32).reshape(1)

    def body(q_ref, p_ref, o_ref, w_ref, m_ref, v_ref, g_out, d_out, m_out, v_out):
        g = p_ref[0] + o_ref[0].astype(F32) + o_ref[1].astype(F32) + o_ref[2].astype(F32)
        g_out[...] = g
        d_out[...], m_out[...], v_out[...] = _adamw(w_ref[...], g, m_ref[...], v_ref[...])

    tile = lambda: pl.BlockSpec((tr, cdim), lambda i, qr: (i, 0))
    return pl.pallas_call(
        body,
        name=name,
        grid_spec=pltpu.PrefetchScalarGridSpec(
            num_scalar_prefetch=1,
            grid=(r // tr,),
            in_specs=[pl.BlockSpec((1, tr, cdim), lambda i, qr: (qr[0], i, 0)),
                      pl.BlockSpec((3, tr, cdim), lambda i, qr: (0, i, 0)), tile(), tile(), tile()],
            out_specs=[tile(), tile(), tile(), tile()],
        ),
        out_shape=[jax.ShapeDtypeStruct((r, cdim), F32)] * 4,
        compiler_params=_params(dimension_semantics=("arbitrary",)),
    )(chip_idx, pairs, others, w, m, v)


def small_exchange(packed):
    def stages(ins, outs, sems):
        send_sems, recv_sems, local_sem = sems
        (src,), (slots,) = ins, outs

        def copies():
            x, y, c, _ = _position()
            my_id = 4 * x + 2 * y + c
            sends, landings = [], []
            for rel in range(1, N_DEV):
                px = 1 - x if (rel >> 2) & 1 else x
                py = 1 - y if (rel >> 1) & 1 else y
                pc = 1 - c if rel & 1 else c
                peer = dict(send_sem=send_sems.at[rel - 1], recv_sem=recv_sems.at[rel - 1], device_id=(px, py, pc),
                            device_id_type=MESH)
                sends.append(pltpu.make_async_remote_copy(src_ref=src, dst_ref=slots.at[my_id], **peer))
                landings.append(pltpu.make_async_remote_copy(src_ref=src, dst_ref=slots.at[4 * px + 2 * py + pc], **peer))
            return pltpu.make_async_copy(src, slots.at[my_id], local_sem), sends, landings

        def begin():
            local, sends, _ = copies()
            local.start()
            for cp in sends:
                cp.start()

        def end():
            local, sends, landings = copies()
            for cp in landings:
                cp.wait_recv()
            for cp in sends:
                cp.wait_send()
            local.wait()

        return begin, None, end

    return Exchange([packed], [jax.ShapeDtypeStruct((N_DEV,) + packed.shape, packed.dtype)],
                    [pltpu.SemaphoreType.DMA((N_DEV - 1,)), pltpu.SemaphoreType.DMA((N_DEV - 1,)),
                     pltpu.SemaphoreType.DMA(())], stages)


def small_adamw(slots, w, m, v):
    def body(r_ref, w_ref, m_ref, v_ref, g_out, d_out, m_out, v_out, loss_out):
        red = r_ref[0]
        for k in range(1, N_DEV):
            red = red + r_ref[k]
        wv = w_ref[...]
        lb = _lower_bound(jnp.concatenate([wv[5:6, :HGRN_WIDTH], wv[5:6, HGRN_WIDTH:]], axis=0))
        t = red[5:6, :HGRN_WIDTH] * lb * (1.0 - lb)
        row = lax.broadcasted_iota(jnp.int32, red.shape, 0)
        g = jnp.where(row == 5, jnp.concatenate([t, -t], axis=1), jnp.where(row >= 6, 0.0, red))
        g_out[...] = g
        d_out[...], m_out[...], v_out[...] = _adamw(wv, g, m_ref[...], v_ref[...])
        loss = jnp.sum(red[6:7, :], axis=-1, keepdims=True) * (0.5 / D_MODEL)
        loss_out[...] = jnp.broadcast_to(loss, loss_out.shape)

    return pl.pallas_call(
        body,
        name="small_adamw",
        in_specs=[_vmem_spec()] * 4,
        out_specs=[_vmem_spec()] * 5,
        out_shape=[jax.ShapeDtypeStruct(w.shape, F32)] * 4 + [jax.ShapeDtypeStruct((SUBLANES, LANES), F32)],
    )(slots, w, m, v)


def _pack_small(g1, gp, g_pre, g_post, an, hn, logits_or_dlb, extra=None):
    row5 = logits_or_dlb.reshape(1, -1)
    row5 = jnp.pad(row5, ((0, 0), (0, D_MODEL - row5.shape[1])))
    row6 = jnp.zeros((1, D_MODEL), F32) if extra is None else extra
    return jnp.concatenate([g1, gp, g_pre, g_post, jnp.concatenate([an, hn], axis=1), row5, row6,
                            jnp.zeros((1, D_MODEL), F32)], axis=0)


def _unpack_small(p):
    return dict(mix_pre_norm=p[0:1], mix_post_norm=p[1:2], mlp_pre_norm=p[2:3], mlp_post_norm=p[3:4],
                attn_out_norm=p[4:5, :ATTN_WIDTH], hgrn_out_norm=p[4:5, ATTN_WIDTH:],
                hgrn_lb_logits=p[5].reshape(2, HGRN_WIDTH))


BIG = ("w_in", "w_out", "w_ff1", "w_ff2")
ORDER = ("mix_pre_norm", "w_in", "attn_out_norm", "hgrn_lb_logits", "hgrn_out_norm", "w_out", "mix_post_norm",
         "mlp_pre_norm", "w_ff1", "w_ff2", "mlp_post_norm")


def kernel(x, mix_pre_norm, w_in, attn_out_norm, hgrn_lb_logits, hgrn_out_norm, w_out, mix_post_norm, mlp_pre_norm, w_ff1, w_ff2, mlp_post_norm, loss_target, m_mix_pre_norm, m_w_in, m_attn_out_norm, m_hgrn_lb_logits, m_hgrn_out_norm, m_w_out, m_mix_post_norm, m_mlp_pre_norm, m_w_ff1, m_w_ff2, m_mlp_post_norm, v_mix_pre_norm, v_w_in, v_attn_out_norm, v_hgrn_lb_logits, v_hgrn_out_norm, v_w_out, v_mix_post_norm, v_mlp_pre_norm, v_w_ff1, v_w_ff2, v_mlp_post_norm):
    w = dict(w_in=w_in[0], w_out=w_out[0], w_ff1=w_ff1[0], w_ff2=w_ff2[0])
    m = dict(w_in=m_w_in[0], w_out=m_w_out[0], w_ff1=m_w_ff1[0], w_ff2=m_w_ff2[0])
    v = dict(w_in=v_w_in[0], w_out=v_w_out[0], w_ff1=v_w_ff1[0], w_ff2=v_w_ff2[0])

    dx, big, small_slots = train_step(x[0], loss_target[0], mix_pre_norm, attn_out_norm, hgrn_lb_logits, hgrn_out_norm,
                                      mix_post_norm, mlp_pre_norm, mlp_post_norm, w, m, v)

    pack = lambda a, b, c2, d, e, f, g: _pack_small(a, b, c2, d, e, f, g)
    w_s = pack(mix_pre_norm, mix_post_norm, mlp_pre_norm, mlp_post_norm, attn_out_norm, hgrn_out_norm, hgrn_lb_logits)
    m_s = pack(m_mix_pre_norm, m_mix_post_norm, m_mlp_pre_norm, m_mlp_post_norm, m_attn_out_norm, m_hgrn_out_norm,
               m_hgrn_lb_logits)
    v_s = pack(v_mix_pre_norm, v_mix_post_norm, v_mlp_pre_norm, v_mlp_post_norm, v_attn_out_norm, v_hgrn_out_norm,
               v_hgrn_lb_logits)
    g_s, d_s, nm_s, nv_s, loss = small_adamw(small_slots, w_s, m_s, v_s)
    small_out = [_unpack_small(t) for t in (g_s, d_s, nm_s, nv_s)]

    outs = [loss[0, 0], dx[None]]
    for kind in range(4):
        for name in ORDER:
            outs.append(big[name][kind][None] if name in BIG else small_out[kind][name])
    return tuple(outs)
```

```python
import jax
import jax.numpy as jnp
from jax import lax
from jax.experimental import pallas as pl
from jax.experimental.pallas import tpu as pltpu

F32 = jnp.float32
BF16 = jnp.bfloat16

D_MODEL = 1024
ATTN_WIDTH = 512
ATTN_HEAD_DIM = 64
ATTN_HEADS = 8
ATTN_BLOCK = 128
DILATIONS = (1, 4, 16)
HGRN_WIDTH = 512
HGRN_HEADS = 4
HGRN_HEAD_DIM = 128
HGRN_CHUNK = 64
IN_PROJ_WIDTH = 3584
D_FF = 4096
RMS_EPS = 1e-6
N_DEV = 8
ADAM_LR = 0.001
ADAM_B1 = 0.9
ADAM_B2 = 0.999
ADAM_EPS = 1e-08
ADAM_WD = 0.01
ADAM_STEP = 10

SUBLANES = 8
LANES = 128
COLUMN_UNROLL = 8
HGRN_CHUNKS_PER_STEP = 2
SUB_BLOCK = 16
TOKEN_TILE = 512
ELEMENTWISE_ROWS = 1024
MLP_TILE = 256
PROJ_TILE = 512
VMEM_BYTES_V7X = 64 * 1024 * 1024
VMEM_LIMIT = VMEM_BYTES_V7X // 8 * 7
NEG_BIG = -1e30
MESH = pl.DeviceIdType.MESH


def _params(ride=None, **kw):
    if ride is not None:
        kw["collective_id"] = ride.collective_id
    return pltpu.CompilerParams(vmem_limit_bytes=VMEM_LIMIT, **kw)


def _vmem_spec():
    return pl.BlockSpec(memory_space=pltpu.VMEM)


def _dot(a, b):
    return jnp.dot(a, b, preferred_element_type=F32)


def _dot_nt(a, b):
    return lax.dot_general(a, b, (((1,), (1,)), ((), ())), preferred_element_type=F32)


def _dot_tn(a, b):
    return lax.dot_general(a, b, (((0,), (0,)), ((), ())), preferred_element_type=F32)


def _sigmoid(x):
    return 1.0 / (1.0 + jnp.exp(-x))


def _rms_fwd(x, gain, width):
    r = lax.rsqrt(jnp.sum(x * x, axis=-1, keepdims=True) * (1.0 / width) + RMS_EPS)
    return x * r * gain


def _rms_bwd(dy, x, gain, width):
    r = lax.rsqrt(jnp.sum(x * x, axis=-1, keepdims=True) * (1.0 / width) + RMS_EPS)
    xhat = x * r
    dxhat = dy * gain
    dx = r * (dxhat - xhat * (jnp.sum(dxhat * xhat, axis=-1, keepdims=True) * (1.0 / width)))
    return dx, dy * xhat


def _split3(x):
    hi = x.astype(BF16)
    r1 = x - hi.astype(F32)
    mid = r1.astype(BF16)
    lo = (r1 - mid.astype(F32)).astype(BF16)
    return hi, mid, lo


def _tri_sum(tri_bf16, x):
    hi, mid, lo = _split3(x)
    return _dot(tri_bf16, hi) + _dot(tri_bf16, mid) + _dot(tri_bf16, lo)


def _dilated_spec(d, tm, width):
    return pl.BlockSpec((d, tm // d, width), lambda i: (0, i, 0))


def _lane_blocks(ref, value):
    for c in range(ref.shape[0]):
        ref[c] = value[:, c * LANES:(c + 1) * LANES]


def _to_dilated(src_ref, dst_ref, d, tm, cast=None):
    for r in range(d):
        for c in range(src_ref.shape[0]):
            v = src_ref[c] if d == 1 else src_ref[c, pl.ds(r, tm // d, stride=d), :]
            dst_ref[r, :, c * LANES:(c + 1) * LANES] = v if cast is None else v.astype(cast)


def _from_dilated(src_ref, scratch_ref, d, tm):
    if d == 1:
        return src_ref[0].astype(F32)
    nblk = scratch_ref.shape[0]
    for r in range(d):
        for c in range(nblk):
            scratch_ref[c, pl.ds(r, tm // d, stride=d), :] = src_ref[r, :, c * LANES:(c + 1) * LANES].astype(F32)
    return jnp.concatenate([scratch_ref[c] for c in range(nblk)], axis=1)


def in_proj_fwd(x, g1, w_in_b):
    s = x.shape[0]
    tm = PROJ_TILE
    qkv_w = 3 * ATTN_WIDTH
    hg_w = IN_PROJ_WIDTH - qkv_w

    def body(x_ref, g_ref, w_ref, hg_ref, h_ref, *rest):
        qkv_refs, qkv_scr = rest[:len(DILATIONS)], rest[len(DILATIONS)]
        h = _rms_fwd(x_ref[...], g_ref[...], D_MODEL).astype(BF16)
        h_ref[...] = h
        proj = _dot(h, w_ref[...])
        hg_ref[...] = proj[:, qkv_w:]
        _lane_blocks(qkv_scr, proj[:, :qkv_w])
        for d, ref in zip(DILATIONS, qkv_refs):
            _to_dilated(qkv_scr, ref, d, tm, cast=BF16)

    return pl.pallas_call(
        body,
        name="in_proj_fwd",
        grid=(s // tm,),
        in_specs=[
            pl.BlockSpec((tm, D_MODEL), lambda i: (i, 0)),
            pl.BlockSpec((1, D_MODEL), lambda i: (0, 0)),
            _vmem_spec(),
        ],
        out_specs=[
            pl.BlockSpec((tm, hg_w), lambda i: (i, 0)),
            pl.BlockSpec((tm, D_MODEL), lambda i: (i, 0)),
        ] + [_dilated_spec(d, tm, qkv_w) for d in DILATIONS],
        out_shape=[jax.ShapeDtypeStruct((s, hg_w), F32), jax.ShapeDtypeStruct((s, D_MODEL), BF16)] + [
            jax.ShapeDtypeStruct((d, s // d, qkv_w), BF16) for d in DILATIONS],
        scratch_shapes=[pltpu.VMEM((qkv_w // LANES, tm, LANES), F32)],
        compiler_params=_params(dimension_semantics=("arbitrary",)),
    )(x, g1, w_in_b)


ATTN_SCALE = ATTN_HEAD_DIM ** -0.5


def _fill_attn_bias(bias_ref, dilation):
    qi = lax.broadcasted_iota(jnp.int32, (ATTN_BLOCK, 2 * ATTN_BLOCK), 0)
    kj = lax.broadcasted_iota(jnp.int32, (ATTN_BLOCK, 2 * ATTN_BLOCK), 1)
    dist = qi + ATTN_BLOCK - kj
    valid = (dist >= 0) & (dist <= ATTN_BLOCK)
    for head in range(ATTN_HEADS):
        slope = 2.0 ** (-8.0 * (head + 1) / ATTN_HEADS)
        bias = jnp.where(valid, dist.astype(F32) * (-slope * dilation), NEG_BIG)
        bias_ref[0, head] = bias
        bias_ref[1, head] = jnp.where(kj >= ATTN_BLOCK, bias, NEG_BIG)


def _stack_heads(x):
    low = _lane_half(x.shape, 0)
    zero = jnp.zeros_like(x)
    return jnp.concatenate([jnp.where(low, x, zero), jnp.where(low, zero, x)], axis=0)


def _unstack_heads(y):
    half = y.shape[0] // 2
    return jnp.where(_lane_half((half, y.shape[1]), 0), y[:half], y[half:])


def _attn_scores(q_stack, kcat, bias_ref, pair, first_block):
    f = first_block.astype(jnp.int32)
    bias = jnp.concatenate([bias_ref[f, 2 * pair], bias_ref[f, 2 * pair + 1]], axis=0)
    return _dot_nt(q_stack, kcat) + bias


def _lane_half(shape, sub):
    lane = lax.broadcasted_iota(jnp.int32, shape, 1)
    return (lane < ATTN_HEAD_DIM) if sub == 0 else (lane >= ATTN_HEAD_DIM)


def _sub_block(col, row):
    return pl.BlockSpec((None, ATTN_BLOCK, ATTN_WIDTH), lambda r, n: (r, row(n), col))


def attn_fwd(qkv, dilation):
    d, length, _ = qkv.shape
    assert d == dilation
    nb = length // ATTN_BLOCK

    def body(q_ref, kc_ref, kp_ref, vc_ref, vp_ref, o_ref, lse_ref, bias_ref):
        @pl.when((pl.program_id(0) == 0) & (pl.program_id(1) == 0))
        def _():
            _fill_attn_bias(bias_ref, d)

        first = pl.program_id(1) == 0
        for pair in range(ATTN_HEADS // 2):
            lanes = slice(pair * LANES, (pair + 1) * LANES)
            q_stack = _stack_heads(q_ref[:, lanes] * ATTN_SCALE)
            kcat = jnp.concatenate([kp_ref[:, lanes], kc_ref[:, lanes]], axis=0)
            vcat = jnp.concatenate([vp_ref[:, lanes], vc_ref[:, lanes]], axis=0)
            sc = _attn_scores(q_stack, kcat, bias_ref, pair, first)
            m = jnp.max(sc, axis=-1, keepdims=True)
            p = jnp.exp(sc - m)
            den = jnp.sum(p, axis=-1, keepdims=True)
            o_ref[:, lanes] = _unstack_heads(_dot(p.astype(BF16), vcat) / den).astype(BF16)
            lse_ref[:, lanes] = _unstack_heads(jnp.broadcast_to(m + jnp.log(den), (2 * ATTN_BLOCK, LANES)))

    cur = lambda n: n
    prev = lambda n: jnp.maximum(n - 1, 0)
    return pl.pallas_call(
        body,
        name=f"attn_fwd_d{d}",
        grid=(d, nb),
        in_specs=[_sub_block(0, cur), _sub_block(1, cur), _sub_block(1, prev), _sub_block(2, cur), _sub_block(2, prev)],
        out_specs=[_sub_block(0, cur), _sub_block(0, cur)],
        out_shape=[jax.ShapeDtypeStruct((d, length, ATTN_WIDTH), BF16), jax.ShapeDtypeStruct((d, length, ATTN_WIDTH), F32)],
        scratch_shapes=[pltpu.VMEM((2, ATTN_HEADS, ATTN_BLOCK, 2 * ATTN_BLOCK), F32)],
        compiler_params=_params(dimension_semantics=("arbitrary", "arbitrary")),
    )(qkv, qkv, qkv, qkv, qkv)


def attn_bwd(qkv, d_out, lse, delta, dilation, ride=None):
    d, length, _ = qkv.shape
    assert d == dilation
    nb = length // ATTN_BLOCK

    steps = d * nb + 1

    def body(q_ref, kc_ref, kp_ref, vc_ref, vp_ref, do_ref, lse_ref, dl_ref, dq_ref, dk_ref, dv_ref, ck_ref, cv_ref,
             bias_ref):
        t = pl.program_id(0)

        @pl.when(t == 0)
        def _():
            ck_ref[...] = jnp.zeros_like(ck_ref)
            cv_ref[...] = jnp.zeros_like(cv_ref)
            _fill_attn_bias(bias_ref, d)

        @pl.when(t < steps - 1)
        def _():
            first = t % nb == 0
            for pair in range(ATTN_HEADS // 2):
                lanes = slice(pair * LANES, (pair + 1) * LANES)
                q_stack = _stack_heads(q_ref[:, lanes] * ATTN_SCALE)
                do_stack = _stack_heads(do_ref[:, lanes])
                kcat = jnp.concatenate([kp_ref[:, lanes], kc_ref[:, lanes]], axis=0)
                vcat = jnp.concatenate([vp_ref[:, lanes], vc_ref[:, lanes]], axis=0)
                col_a, col_b = pair * LANES, pair * LANES + ATTN_HEAD_DIM
                lse_col = jnp.concatenate([lse_ref[:, col_a:col_a + 1], lse_ref[:, col_b:col_b + 1]], axis=0)
                dl_col = jnp.concatenate([dl_ref[:, col_a:col_a + 1], dl_ref[:, col_b:col_b + 1]], axis=0)
                p = jnp.exp(_attn_scores(q_stack, kcat, bias_ref, pair, first) - lse_col)
                ds = (p * (_dot_nt(do_stack, vcat) - dl_col)).astype(BF16)
                dq_ref[:, lanes] = (_unstack_heads(_dot(ds, kcat)) * ATTN_SCALE).astype(BF16)
                dk_cat = _dot_tn(ds, q_stack)
                dv_cat = _dot_tn(p.astype(BF16), do_stack)
                dk_ref[:, lanes] = (ck_ref[:, lanes] + dk_cat[:ATTN_BLOCK]).astype(BF16)
                dv_ref[:, lanes] = (cv_ref[:, lanes] + dv_cat[:ATTN_BLOCK]).astype(BF16)
                ck_ref[:, lanes] = dk_cat[ATTN_BLOCK:]
                cv_ref[:, lanes] = dv_cat[ATTN_BLOCK:]

        @pl.when(t == steps - 1)
        def _():
            dk_ref[...] = ck_ref[...].astype(BF16)
            dv_ref[...] = cv_ref[...].astype(BF16)

    blk = (ATTN_BLOCK, ATTN_WIDTH)

    def spec(col, shift):
        def index(t):
            f = jnp.minimum(t, steps - 2) if shift > -2 else jnp.maximum(t - 1, 0)
            r, n = f // nb, f % nb
            return (r, jnp.maximum(n - 1, 0) if shift == -1 else n, col)
        return pl.BlockSpec((None, ATTN_BLOCK, ATTN_WIDTH), index)

    step = lambda k: (lambda: pl.program_id(0) == k)
    e_in, e_out, e_shape, e_scr, e_args = _ride_specs(ride)
    return pl.pallas_call(
        _riding(body, 8, 3, 3, ride, step(0), step(steps // 2), step(steps - 1)),
        name=f"attn_bwd_d{d}",
        grid=(steps,),
        in_specs=[spec(0, 0), spec(1, 0), spec(1, -1), spec(2, 0), spec(2, -1), spec(0, 0), spec(0, 0), spec(0, 0)] + e_in,
        out_specs=[spec(0, 0), spec(0, -2), spec(0, -2)] + e_out,
        out_shape=[jax.ShapeDtypeStruct((d, length, ATTN_WIDTH), BF16)] * 3 + e_shape,
        scratch_shapes=[pltpu.VMEM(blk, F32), pltpu.VMEM(blk, F32),
                        pltpu.VMEM((2, ATTN_HEADS, ATTN_BLOCK, 2 * ATTN_BLOCK), F32)] + e_scr,
        compiler_params=_params(ride, dimension_semantics=("arbitrary",)),
    )(qkv, qkv, qkv, qkv, qkv, d_out, lse, delta, *e_args)


def _lower_bound(logits):
    return _sigmoid(logits[0:1, :] - logits[1:2, :])


def _hgrn_gates(q, fp, lb):
    sq = _sigmoid(q)
    qf = q * sq
    sig = _sigmoid(fp)
    sig_neg = _sigmoid(-fp)
    kf = (1.0 - lb) * sig_neg
    log_sig = jnp.minimum(fp, 0.0) - jnp.log(1.0 + jnp.exp(-jnp.abs(fp)))
    a = jnp.log(lb)
    c = jnp.log(1.0 - lb) + log_sig
    log_f = jnp.maximum(a, c) + jnp.log(1.0 + jnp.exp(-jnp.abs(a - c)))
    return sq, qf, (sig, sig_neg, c), log_f, kf


def _tril_bf16(n, upper=False):
    r = lax.broadcasted_iota(jnp.int32, (n, n), 0)
    c = lax.broadcasted_iota(jnp.int32, (n, n), 1)
    keep = (c >= r) if upper else (c <= r)
    return jnp.where(keep, 1.0, 0.0).astype(BF16)


def _hgrn_diagonal_loops(c_len, diagonal):
    for half in range(SUB_BLOCK // SUBLANES):
        def step(jj, carry, half=half):
            j = half * SUBLANES + jj
            for i in range(c_len // SUB_BLOCK):
                diagonal(slice(i * SUB_BLOCK + half * SUBLANES, (i + 1) * SUB_BLOCK), j, i * SUB_BLOCK + j)
            return carry

        lax.fori_loop(0, SUBLANES, step, 0, unroll=COLUMN_UNROLL)


def _hgrn_off_diagonal(b, qf, kf):
    c_len, width = b.shape
    edges = [b[0:1, :]] + [b[i * SUB_BLOCK - 1:i * SUB_BLOCK, :] for i in range(1, c_len // SUB_BLOCK)]
    eq = jnp.exp(b - jnp.concatenate([jnp.broadcast_to(e, (SUB_BLOCK, width)) for e in edges], axis=0))
    q_til = qf * eq
    k_til, ek = [], []
    for i in range(1, c_len // SUB_BLOCK):
        n = i * SUB_BLOCK
        e = jnp.exp(edges[i] - b[:n, :])
        ek.append(e)
        k_til.append(jnp.concatenate([kf[:n, :] * e, jnp.zeros((2 * c_len - n, width), F32)], axis=0))
    return q_til, k_til, eq, ek


def _split2(x):
    hi = x.astype(BF16)
    return hi, (x - hi.astype(F32)).astype(BF16)


def hgrn_fwd(proj, lb, ride=None):
    s = proj.shape[0]
    c_len, nh, hd = HGRN_CHUNK, HGRN_HEADS, HGRN_HEAD_DIM
    n_chunks = s // c_len
    col0 = 0

    cps = 2 * HGRN_CHUNKS_PER_STEP
    n_steps = n_chunks // cps

    def body(q_ref, f_ref, i_ref, lb_ref, o_ref, st_out_ref, a_out_ref, st_ref, b_ref, qf_ref, kf_ref, a_ref):
        @pl.when(pl.program_id(0) == 0)
        def _():
            st_ref[...] = jnp.zeros_like(st_ref)

        lbv = _lower_bound(lb_ref[...])
        for u in range(cps):
            rs = slice(u * c_len, (u + 1) * c_len)
            b_u, qf_u, kf_u, a_u = b_ref.at[u], qf_ref.at[u], kf_ref.at[u], a_ref.at[u]
            _, qf, _, log_f, kf = _hgrn_gates(q_ref[rs, :], f_ref[rs, :], lbv)
            b = _tri_sum(_tril_bf16(c_len), log_f)
            b_u[...] = b
            qf_u[...] = qf
            kf_u[...] = kf
            a_u[...] = jnp.zeros_like(a_u)

            def diagonal(rows, j, key, b_u=b_u, qf_u=qf_u, kf_u=kf_u, a_u=a_u):
                bj = b_u[pl.ds(key, 1), :]
                kj = kf_u[pl.ds(key, 1), :]
                nrow = rows.stop - rows.start
                t_loc = lax.broadcasted_iota(jnp.int32, (nrow, nh * hd), 0) + (rows.start % SUB_BLOCK)
                e = jnp.exp(jnp.where(t_loc >= j, b_u[rows, :] - bj, NEG_BIG))
                prod = qf_u[rows, :] * kj * e
                lane = lax.broadcasted_iota(jnp.int32, (nrow, hd), 1)
                for h in range(nh):
                    col = jnp.sum(prod[:, h * hd:(h + 1) * hd], axis=-1, keepdims=True)
                    a_u[h, rows, :] = jnp.where(lane == key, col, a_u[h, rows, :])

            _hgrn_diagonal_loops(c_len, diagonal)
            q_til, k_til, _, _ = _hgrn_off_diagonal(b, qf, kf)
            q_til = q_til.astype(BF16)
            k_til = [k.astype(BF16) for k in k_til]

            b_last = b[c_len - 1:c_len, :]
            qb = (qf * jnp.exp(b)).astype(BF16)
            kb2 = (kf * jnp.exp(b_last - b)).astype(BF16)
            vf = i_ref[rs, :].astype(BF16)
            for h in range(nh):
                hs = slice(h * hd, (h + 1) * hd)
                st = st_ref[h]
                st_out_ref[u, h] = st
                off = [jnp.zeros((SUB_BLOCK, hd), F32)]
                for i in range(1, c_len // SUB_BLOCK):
                    off.append(_dot_nt(q_til[i * SUB_BLOCK:(i + 1) * SUB_BLOCK, hs], k_til[i - 1][:, hs]))
                a_h = a_u[h] + jnp.concatenate(off, axis=0)
                a_out_ref[rs, hs] = a_h
                o_ref[rs, hs] = _dot_nt(qb[:, hs], st.astype(BF16)) + _dot(a_h[:, :c_len].astype(BF16), vf[:, hs])
                st_ref[h] = st * jnp.exp(b_last[:, hs]) + _dot_tn(vf[:, hs], kb2[:, hs])

    blk = (cps * c_len, HGRN_WIDTH)
    sblk = (cps, c_len, HGRN_WIDTH)
    step = lambda k: (lambda: pl.program_id(0) == k)
    e_in, e_out, e_shape, e_scr, e_args = _ride_specs(ride)
    return pl.pallas_call(
        _riding(body, 4, 3, 5, ride, step(0), step((7 * n_steps) // 8), step(n_steps - 1)),
        name="hgrn_fwd",
        grid=(n_steps,),
        in_specs=[
            pl.BlockSpec(blk, lambda c: (c, col0)),
            pl.BlockSpec(blk, lambda c: (c, col0 + 1)),
            pl.BlockSpec(blk, lambda c: (c, col0 + 2)),
            pl.BlockSpec((2, HGRN_WIDTH), lambda c: (0, 0)),
        ] + e_in,
        out_specs=[
            pl.BlockSpec(blk, lambda c: (c, 0)),
            pl.BlockSpec((cps, nh, hd, hd), lambda c: (c, 0, 0, 0)),
            pl.BlockSpec(blk, lambda c: (c, 0)),
        ] + e_out,
        out_shape=[
            jax.ShapeDtypeStruct((s, HGRN_WIDTH), F32),
            jax.ShapeDtypeStruct((n_chunks, nh, hd, hd), F32),
            jax.ShapeDtypeStruct((s, nh * hd), F32),
        ] + e_shape,
        scratch_shapes=[
            pltpu.VMEM((nh, hd, hd), F32),
            pltpu.VMEM(sblk, F32),
            pltpu.VMEM(sblk, F32),
            pltpu.VMEM(sblk, F32),
            pltpu.VMEM((cps, nh, c_len, hd), F32),
        ] + e_scr,
        compiler_params=_params(ride, dimension_semantics=("arbitrary",)),
    )(proj, proj, proj, lb, *e_args)


def hgrn_bwd(proj, lb, d_o, states, a_mat, ride=None):
    s = proj.shape[0]
    c_len, nh, hd = HGRN_CHUNK, HGRN_HEADS, HGRN_HEAD_DIM
    n_chunks = s // c_len
    col0 = 0
    cps = HGRN_CHUNKS_PER_STEP
    n_steps = n_chunks // cps
    last = n_steps - 1

    def body(q_ref, f_ref, i_ref, lb_ref, do_ref, st_in_ref, a_in_ref, dq_ref, df_ref, di_ref, dlb_ref,
             dst_ref, b_ref, qf_ref, kf_ref, da_ref, dqi_ref, dki_ref):
        @pl.when(pl.program_id(0) == 0)
        def _():
            dst_ref[...] = jnp.zeros_like(dst_ref)
            dlb_ref[...] = jnp.zeros_like(dlb_ref)

        lbv = _lower_bound(lb_ref[...])
        for u in reversed(range(cps)):
            rs = slice(u * c_len, (u + 1) * c_len)
            b_u, qf_u, kf_u, da_u, dqi_u, dki_u = (b_ref.at[u], qf_ref.at[u], kf_ref.at[u], da_ref.at[u], dqi_ref.at[u],
                                                   dki_ref.at[u])
            q = q_ref[rs, :]
            sq, qf, (sig, sig_neg, log_c), log_f, kf = _hgrn_gates(q, f_ref[rs, :], lbv)
            b = _tri_sum(_tril_bf16(c_len), log_f)
            b_u[...] = b
            qf_u[...] = qf
            kf_u[...] = kf
            b_last = b[c_len - 1:c_len, :]
            eb = jnp.exp(b)
            ebl = jnp.exp(b_last - b)
            qb = qf * eb
            kb2 = kf * ebl
            vf = i_ref[rs, :]
            d_o = do_ref[rs, :]
            qb_b, kb2_b, vf_b, do_b = qb.astype(BF16), kb2.astype(BF16), vf.astype(BF16), d_o.astype(BF16)
            tq = lax.broadcasted_iota(jnp.int32, (c_len, hd), 0)
            lane = lax.broadcasted_iota(jnp.int32, (c_len, hd), 1)

            dqb_parts, dvf_parts, dkb2_parts, dbl_parts = [], [], [], []
            for h in range(nh):
                hs = slice(h * hd, (h + 1) * hd)
                st = st_in_ref[u, h]
                dst = dst_ref[h]
                st_b, dst_b = st.astype(BF16), dst.astype(BF16)
                a_h = a_in_ref[rs, hs][:, :c_len].astype(BF16)
                dqb_parts.append(_dot(do_b[:, hs], st_b))
                dvf_parts.append(_dot_tn(a_h, do_b[:, hs]) + _dot_nt(kb2_b[:, hs], dst_b))
                dkb2_parts.append(_dot(vf_b[:, hs], dst_b))
                da = _dot_nt(do_b[:, hs], vf_b[:, hs])
                da = jnp.concatenate([da, jnp.zeros((c_len, hd - c_len), F32)], axis=1)
                da_u[h] = jnp.where(tq >= lane, da, 0.0)
                dbl_parts.append(jnp.sum(dst * st, axis=0, keepdims=True) * jnp.exp(b_last[:, hs]))
                dst_ref[h] = dst * jnp.exp(b_last[:, hs]) + _dot_tn(do_b[:, hs], qb_b[:, hs])
            dqb = jnp.concatenate(dqb_parts, axis=1)
            dvf = jnp.concatenate(dvf_parts, axis=1)
            dkb2 = jnp.concatenate(dkb2_parts, axis=1)
            dbl = jnp.concatenate(dbl_parts, axis=1) + jnp.sum(dkb2 * kb2, axis=0, keepdims=True)

            dqi_u[...] = jnp.zeros_like(dqi_u)
            t_idx = lax.broadcasted_iota(jnp.int32, (c_len, nh * hd), 0)

            def diagonal(rows, j, key, b_u=b_u, qf_u=qf_u, kf_u=kf_u, da_u=da_u, dqi_u=dqi_u, dki_u=dki_u):
                bj = b_u[pl.ds(key, 1), :]
                kj = kf_u[pl.ds(key, 1), :]
                nrow = rows.stop - rows.start
                t_loc = lax.broadcasted_iota(jnp.int32, (nrow, nh * hd), 0) + (rows.start % SUB_BLOCK)
                e = jnp.exp(jnp.where(t_loc >= j, b_u[rows, :] - bj, NEG_BIG))
                lane_r = lax.broadcasted_iota(jnp.int32, (nrow, hd), 1)
                cols = [jnp.sum(jnp.where(lane_r == key, da_u[h, rows, :], 0.0), axis=-1, keepdims=True)
                        for h in range(nh)]
                w = e * jnp.concatenate([jnp.broadcast_to(cc, (nrow, hd)) for cc in cols], axis=1)
                dqi_u[rows, :] += w * kj
                dki_u[pl.ds(key, 1), :] = jnp.sum(w * qf_u[rows, :], axis=0, keepdims=True)

            _hgrn_diagonal_loops(c_len, diagonal)

            q_til, k_til, eq, ek = _hgrn_off_diagonal(b, qf, kf)
            q_hi, q_lo = _split2(q_til)
            k_pairs = [_split2(k) for k in k_til]
            n_sub = c_len // SUB_BLOCK
            dq_heads, dk_heads = [], []
            for h in range(nh):
                hs = slice(h * hd, (h + 1) * hd)
                dq_rows = [jnp.zeros((SUB_BLOCK, hd), F32)]
                dk_h = jnp.zeros((c_len, hd), F32)
                for i in range(1, n_sub):
                    rows = slice(i * SUB_BLOCK, (i + 1) * SUB_BLOCK)
                    n = i * SUB_BLOCK
                    da_i = da_u[h, rows, :].astype(BF16)
                    k_hi, k_lo = k_pairs[i - 1]
                    dq_rows.append((_dot(da_i, k_hi[:, hs]) + _dot(da_i, k_lo[:, hs])) * eq[rows, hs])
                    dk_t = (_dot_tn(da_i, q_hi[rows, hs]) + _dot_tn(da_i, q_lo[rows, hs]))[:n, :] * ek[i - 1][:, hs]
                    dk_h = dk_h + jnp.concatenate([dk_t, jnp.zeros((c_len - n, hd), F32)], axis=0)
                dq_heads.append(jnp.concatenate(dq_rows, axis=0))
                dk_heads.append(dk_h)
            dq_intra = dqi_u[...] + jnp.concatenate(dq_heads, axis=1)
            dk_intra = dki_u[...] + jnp.concatenate(dk_heads, axis=1)

            db = dqb * qb + qf * dq_intra - kf * dk_intra - dkb2 * kb2
            db = db + jnp.where(t_idx == c_len - 1, dbl, 0.0)
            dg = _tri_sum(_tril_bf16(c_len, upper=True), db)
            dqf = dqb * eb + dq_intra
            dkf = dkb2 * ebl + dk_intra
            dq_ref[rs, :] = (dqf * (sq * (1.0 + q * (1.0 - sq)))).astype(BF16)
            df_ref[rs, :] = (sig_neg * (dg * jnp.exp(log_c - log_f) - dkf * (1.0 - lbv) * sig)).astype(BF16)
            di_ref[rs, :] = dvf.astype(BF16)
            dlb_ref[...] += jnp.sum(sig_neg * (dg * jnp.exp(-log_f) - dkf), axis=0, keepdims=True)

    blk = (cps * c_len, HGRN_WIDTH)
    sblk = (cps, c_len, HGRN_WIDTH)
    rev = lambda c: last - c
    step = lambda k: (lambda: pl.program_id(0) == k)
    e_in, e_out, e_shape, e_scr, e_args = _ride_specs(ride)
    return pl.pallas_call(
        _riding(body, 7, 4, 7, ride, step(0), step(n_steps // 2), step(last)),
        name="hgrn_bwd",
        grid=(n_steps,),
        in_specs=[
            pl.BlockSpec(blk, lambda c: (rev(c), col0)),
            pl.BlockSpec(blk, lambda c: (rev(c), col0 + 1)),
            pl.BlockSpec(blk, lambda c: (rev(c), col0 + 2)),
            pl.BlockSpec((2, HGRN_WIDTH), lambda c: (0, 0)),
            pl.BlockSpec(blk, lambda c: (rev(c), 0)),
            pl.BlockSpec((cps, nh, hd, hd), lambda c: (rev(c), 0, 0, 0)),
            pl.BlockSpec(blk, lambda c: (rev(c), 0)),
        ] + e_in,
        out_specs=[
            pl.BlockSpec(blk, lambda c: (rev(c), 0)),
            pl.BlockSpec(blk, lambda c: (rev(c), 0)),
            pl.BlockSpec(blk, lambda c: (rev(c), 0)),
            pl.BlockSpec((1, HGRN_WIDTH), lambda c: (0, 0)),
        ] + e_out,
        out_shape=[jax.ShapeDtypeStruct((s, HGRN_WIDTH), BF16)] * 3 + [jax.ShapeDtypeStruct((1, HGRN_WIDTH), F32)] + e_shape,
        scratch_shapes=[
            pltpu.VMEM((nh, hd, hd), F32),
            pltpu.VMEM(sblk, F32),
            pltpu.VMEM(sblk, F32),
            pltpu.VMEM(sblk, F32),
            pltpu.VMEM((cps, nh, c_len, hd), F32),
            pltpu.VMEM(sblk, F32),
            pltpu.VMEM(sblk, F32),
        ] + e_scr,
        compiler_params=_params(ride, dimension_semantics=("arbitrary",)),
    )(proj, proj, proj, lb, d_o, states, a_mat, *e_args)


def _row_spec(tm, width, col=0):
    return pl.BlockSpec((tm, width), lambda i: (i, col))


def _const_spec(width):
    return pl.BlockSpec((1, width), lambda i: (0, 0))


def _acc_rows(ref, value):
    @pl.when(pl.program_id(0) == 0)
    def _():
        ref[...] = jnp.zeros_like(ref)

    ref[...] += jnp.sum(value, axis=0, keepdims=True)


def mix_fwd(attn_parts, o_h, proj, an, hn, w_out_b, gp, x, ride=None):
    s = x.shape[0]
    tm = TOKEN_TILE
    gate_col = 3
    hd = HGRN_HEAD_DIM
    nd = len(DILATIONS)

    def body(*refs):
        o_refs, l_refs = refs[:nd], refs[nd:2 * nd]
        oh_ref, gate_ref, an_ref, hn_ref, w_ref, gp_ref, x_ref = refs[2 * nd:2 * nd + 7]
        x1_ref, cat_ref, mixed_ref, attn_ref = refs[2 * nd + 7:2 * nd + 11]
        lse_refs = refs[2 * nd + 11:3 * nd + 11]
        o_scr, l_scr, lse_scr = refs[3 * nd + 11:]
        os_ = [_from_dilated(r, o_scr.at[k], d, tm) for k, (r, d) in enumerate(zip(o_refs, DILATIONS))]
        ls = [_from_dilated(r, l_scr.at[k], d, tm) for k, (r, d) in enumerate(zip(l_refs, DILATIONS))]
        m = jnp.maximum(jnp.maximum(ls[0], ls[1]), ls[2])
        es = [jnp.exp(l - m) for l in ls]
        den = es[0] + es[1] + es[2]
        attn = (es[0] * os_[0] + es[1] * os_[1] + es[2] * os_[2]) / den
        attn_ref[...] = attn
        _lane_blocks(lse_scr, m + jnp.log(den))
        for d, ref in zip(DILATIONS, lse_refs):
            _to_dilated(lse_scr, ref, d, tm)
        cat_ref[:, :ATTN_WIDTH] = _rms_fwd(attn, an_ref[...], ATTN_WIDTH).astype(BF16)
        gate = gate_ref[...]
        silu_g = gate * _sigmoid(gate)
        for h in range(HGRN_HEADS):
            hs = slice(h * hd, (h + 1) * hd)
            rec = _rms_fwd(oh_ref[:, hs], hn_ref[:, hs], hd) * silu_g[:, hs]
            cat_ref[:, ATTN_WIDTH + h * hd:ATTN_WIDTH + (h + 1) * hd] = rec.astype(BF16)
        mixed = _dot(cat_ref[...], w_ref[...])
        mixed_ref[...] = mixed
        x1_ref[...] = x_ref[...] + _rms_fwd(mixed, gp_ref[...], D_MODEL)

    aw = ATTN_WIDTH
    n_steps = s // tm
    step = lambda k: (lambda: pl.program_id(0) == k)
    e_in, e_out, e_shape, e_scr, e_args = _ride_specs(ride)
    return pl.pallas_call(
        _riding(body, 2 * nd + 7, 4 + nd, 3, ride, step(0), step((13 * n_steps) // 16), step(n_steps - 1)),
        name="mix_fwd",
        grid=(n_steps,),
        in_specs=[_dilated_spec(d, tm, aw) for d in DILATIONS] * 2 + [
            _row_spec(tm, aw), _row_spec(tm, aw, gate_col), _const_spec(aw), _const_spec(aw), _vmem_spec(),
            _const_spec(D_MODEL), _row_spec(tm, D_MODEL)] + e_in,
        out_specs=[_row_spec(tm, D_MODEL), _row_spec(tm, D_MODEL), _row_spec(tm, D_MODEL), _row_spec(tm, aw)] + [
            _dilated_spec(d, tm, aw) for d in DILATIONS] + e_out,
        out_shape=[
            jax.ShapeDtypeStruct((s, D_MODEL), F32),
            jax.ShapeDtypeStruct((s, D_MODEL), BF16),
            jax.ShapeDtypeStruct((s, D_MODEL), F32),
            jax.ShapeDtypeStruct((s, aw), F32),
        ] + [jax.ShapeDtypeStruct((d, s // d, aw), F32) for d in DILATIONS] + e_shape,
        scratch_shapes=[pltpu.VMEM((nd, aw // LANES, tm, LANES), F32), pltpu.VMEM((nd, aw // LANES, tm, LANES), F32),
                        pltpu.VMEM((aw // LANES, tm, LANES), F32)] + e_scr,
        compiler_params=_params(ride, dimension_semantics=("arbitrary",)),
    )(*[p[0] for p in attn_parts], *[p[1] for p in attn_parts], o_h, proj, an, hn, w_out_b, gp, x, *e_args)


def mix_bwd(dx1, mixed, gp, w_out_b, attn, an, o_h, proj, hn):
    s = dx1.shape[0]
    tm = TOKEN_TILE
    gate_col = 3
    hd = HGRN_HEAD_DIM
    aw = ATTN_WIDTH

    nd = len(DILATIONS)

    def body(*refs):
        dx1_ref, mixed_ref, gp_ref, w_ref, attn_ref, an_ref, oh_ref, gate_ref, hn_ref, dmix_ref = refs[:10]
        do_refs, delta_refs = refs[10:10 + nd], refs[10 + nd:10 + 2 * nd]
        doh_ref, dgate_ref, dgp_ref, dan_ref, dhn_ref, do_ref, delta_ref = refs[10 + 2 * nd:]
        dmixed, gp_c = _rms_bwd(dx1_ref[...], mixed_ref[...], gp_ref[...], D_MODEL)
        _acc_rows(dgp_ref, gp_c)
        dmixed_b = dmixed.astype(BF16)
        dmix_ref[...] = dmixed_b
        dcat = _dot_nt(dmixed_b, w_ref[...])
        attn = attn_ref[...]
        d_o, an_c = _rms_bwd(dcat[:, :aw], attn, an_ref[...], aw)
        _acc_rows(dan_ref, an_c)
        _lane_blocks(do_ref, d_o)
        prod = d_o * attn
        for pair in range(ATTN_HEADS // 2):
            pp = prod[:, pair * LANES:(pair + 1) * LANES]
            low = _lane_half((tm, LANES), 0)
            lo = jnp.sum(jnp.where(low, pp, 0.0), axis=-1, keepdims=True)
            hi = jnp.sum(jnp.where(low, 0.0, pp), axis=-1, keepdims=True)
            delta_ref[pair] = jnp.where(low, lo, hi)
        for d, o_ref, l_ref in zip(DILATIONS, do_refs, delta_refs):
            _to_dilated(do_ref, o_ref, d, tm, cast=BF16)
            _to_dilated(delta_ref, l_ref, d, tm)
        gate = gate_ref[...]
        sg = _sigmoid(gate)
        silu_g = gate * sg
        drec = dcat[:, aw:]
        hn_parts = []
        for h in range(HGRN_HEADS):
            hs = slice(h * hd, (h + 1) * hd)
            oh = oh_ref[:, hs]
            on = _rms_fwd(oh, hn_ref[:, hs], hd)
            dgate_ref[:, hs] = (drec[:, hs] * on * (sg[:, hs] * (1.0 + gate[:, hs] * (1.0 - sg[:, hs])))).astype(BF16)
            d_oh, hn_c = _rms_bwd(drec[:, hs] * silu_g[:, hs], oh, hn_ref[:, hs], hd)
            doh_ref[:, hs] = d_oh
            hn_parts.append(hn_c)
        _acc_rows(dhn_ref, jnp.concatenate(hn_parts, axis=1))

    return pl.pallas_call(
        body,
        name="mix_bwd",
        grid=(s // tm,),
        in_specs=[_row_spec(tm, D_MODEL), _row_spec(tm, D_MODEL), _const_spec(D_MODEL), _vmem_spec(), _row_spec(tm, aw),
                  _const_spec(aw), _row_spec(tm, aw), _row_spec(tm, aw, gate_col), _const_spec(aw)],
        out_specs=[_row_spec(tm, D_MODEL)] + [_dilated_spec(d, tm, aw) for d in DILATIONS] * 2 + [_row_spec(tm, aw)] * 2 + [
            _const_spec(D_MODEL), _const_spec(aw), _const_spec(aw)],
        out_shape=[jax.ShapeDtypeStruct((s, D_MODEL), BF16)] + [
            jax.ShapeDtypeStruct((d, s // d, aw), BF16) for d in DILATIONS] + [
            jax.ShapeDtypeStruct((d, s // d, aw), F32) for d in DILATIONS] + [
            jax.ShapeDtypeStruct((s, aw), F32), jax.ShapeDtypeStruct((s, aw), BF16),
            jax.ShapeDtypeStruct((1, D_MODEL), F32), jax.ShapeDtypeStruct((1, aw), F32),
            jax.ShapeDtypeStruct((1, aw), F32)],
        scratch_shapes=[pltpu.VMEM((aw // LANES, tm, LANES), F32), pltpu.VMEM((aw // LANES, tm, LANES), F32)],
        compiler_params=_params(dimension_semantics=("arbitrary",)),
    )(dx1, mixed, gp, w_out_b, attn, an, o_h, proj, hn)


def mlp_fwd_bwd(x1, g_pre, w1_blocks, w2_b, g_post, target):
    s = x1.shape[0]
    tm = MLP_TILE
    nblk, _, fb = w1_blocks.shape

    def body(x1_ref, gpre_ref, w1_ref, w2_ref, gpost_ref, t_ref,
             dx1_ref, h2_ref, a_ref, du_ref, dff_ref, loss_ref, dgpre_ref, dgpost_ref, u_ref):
        x1v = x1_ref[...]
        h2 = _rms_fwd(x1v, gpre_ref[...], D_MODEL).astype(BF16)
        h2_ref[...] = h2
        ff = jnp.zeros((tm, D_MODEL), F32)
        for j in range(nblk):
            cols = slice(j * fb, (j + 1) * fb)
            ru = jnp.maximum(_dot(h2, w1_ref[j]), 0.0)
            u_ref[:, cols] = ru.astype(BF16)
            a = (ru * ru).astype(BF16)
            a_ref[:, cols] = a
            ff = ff + _dot(a, w2_ref[cols, :])
        diff = x1v + _rms_fwd(ff, gpost_ref[...], D_MODEL) - t_ref[...]
        _acc_rows(loss_ref, diff * diff)
        dy = diff * (1.0 / D_MODEL)
        dff, gpost_c = _rms_bwd(dy, ff, gpost_ref[...], D_MODEL)
        _acc_rows(dgpost_ref, gpost_c)
        dff_b = dff.astype(BF16)
        dff_ref[...] = dff_b
        dh2 = jnp.zeros((tm, D_MODEL), F32)
        for j in range(nblk):
            cols = slice(j * fb, (j + 1) * fb)
            du = (_dot_nt(dff_b, w2_ref[cols, :]) * (2.0 * u_ref[:, cols])).astype(BF16)
            du_ref[:, cols] = du
            dh2 = dh2 + _dot_nt(du, w1_ref[j])
        dxa, gpre_c = _rms_bwd(dh2, x1v, gpre_ref[...], D_MODEL)
        _acc_rows(dgpre_ref, gpre_c)
        dx1_ref[...] = dy + dxa

    dm = D_MODEL
    return pl.pallas_call(
        body,
        name="mlp_fwd_bwd",
        grid=(s // tm,),
        in_specs=[_row_spec(tm, dm), _const_spec(dm), _vmem_spec(), _vmem_spec(), _const_spec(dm), _row_spec(tm, dm)],
        out_specs=[_row_spec(tm, dm), _row_spec(tm, dm), _row_spec(tm, D_FF), _row_spec(tm, D_FF), _row_spec(tm, dm),
                   _const_spec(dm), _const_spec(dm), _const_spec(dm)],
        out_shape=[
            jax.ShapeDtypeStruct((s, dm), F32),
            jax.ShapeDtypeStruct((s, dm), BF16),
            jax.ShapeDtypeStruct((s, D_FF), BF16),
            jax.ShapeDtypeStruct((s, D_FF), BF16),
            jax.ShapeDtypeStruct((s, dm), BF16),
            jax.ShapeDtypeStruct((1, dm), F32),
            jax.ShapeDtypeStruct((1, dm), F32),
            jax.ShapeDtypeStruct((1, dm), F32),
        ],
        scratch_shapes=[pltpu.VMEM((tm, D_FF), BF16)],
        compiler_params=_params(dimension_semantics=("arbitrary",)),
    )(x1, g_pre, w1_blocks, w2_b, g_post, target)


def in_proj_bwd(attn_grads, hgrn_grads, dgate, w_in_b, x, g1, dx1):
    s = x.shape[0]
    tm = PROJ_TILE
    aw = ATTN_WIDTH
    n_attn = len(attn_grads)
    flat = [g[k] for k in range(3) for g in attn_grads] + list(hgrn_grads) + [dgate]

    def body(*refs):
        parts = refs[:len(flat)]
        w_ref, x_ref, g_ref, dx1_ref, dx_ref, dproj_ref, dg_ref, scr = refs[len(flat):]
        groups = []
        for k in range(3):
            acc = None
            for p, d in zip(parts[k * n_attn:(k + 1) * n_attn], DILATIONS):
                v = _from_dilated(p, scr, d, tm)
                acc = v if acc is None else acc + v
            groups.append(acc)
        groups += [p[...] for p in parts[3 * n_attn:]]
        dh = jnp.zeros((tm, D_MODEL), F32)
        for gi, grp in enumerate(groups):
            cols = slice(gi * aw, (gi + 1) * aw)
            gb = grp.astype(BF16)
            dproj_ref[:, cols] = gb
            dh = dh + _dot_nt(gb, w_ref[:, cols])
        dxa, g_c = _rms_bwd(dh, x_ref[...], g_ref[...], D_MODEL)
        _acc_rows(dg_ref, g_c)
        dx_ref[...] = dx1_ref[...] + dxa

    dm = D_MODEL
    return pl.pallas_call(
        body,
        name="in_proj_bwd",
        grid=(s // tm,),
        in_specs=[_dilated_spec(d, tm, aw) for d in DILATIONS] * 3 + [_row_spec(tm, aw)] * 4 + [
            _vmem_spec(), _row_spec(tm, dm), _const_spec(dm), _row_spec(tm, dm)],
        out_specs=[_row_spec(tm, dm), _row_spec(tm, IN_PROJ_WIDTH), _const_spec(dm)],
        out_shape=[jax.ShapeDtypeStruct((s, dm), F32), jax.ShapeDtypeStruct((s, IN_PROJ_WIDTH), BF16),
                   jax.ShapeDtypeStruct((1, dm), F32)],
        scratch_shapes=[pltpu.VMEM((aw // LANES, tm, LANES), F32)],
        compiler_params=_params(dimension_semantics=("arbitrary",)),
    )(*flat, w_in_b, x, g1, dx1)


def wgrad(a_b, b_b, tn, name, ts=2048, per_step=1, ride=None):
    s, k = a_b.shape
    n = b_b.shape[1]

    def body(a_ref, b_ref, o_ref):
        @pl.when(pl.program_id(1) == 0)
        def _():
            o_ref[...] = jnp.zeros_like(o_ref)

        a = a_ref[...]
        for jj in range(per_step):
            o_ref[jj] += _dot_tn(a, b_ref[:, jj * tn:(jj + 1) * tn])

    wide = tn * per_step
    gn, gs = n // wide, s // ts
    step = lambda j, i: (lambda: (pl.program_id(0) == j) & (pl.program_id(1) == i))
    e_in, e_out, e_shape, e_scr, e_args = _ride_specs(ride)
    out = pl.pallas_call(
        _riding(body, 2, 1, 0, ride, step(0, 0), step(gn // 2, 0), step(gn - 1, gs - 1)),
        name=name,
        grid=(gn, gs),
        in_specs=[pl.BlockSpec((ts, k), lambda j, i: (i, 0)), pl.BlockSpec((ts, wide), lambda j, i: (i, j))] + e_in,
        out_specs=[pl.BlockSpec((per_step, k, tn), lambda j, i: (j, 0, 0))] + e_out,
        out_shape=[jax.ShapeDtypeStruct((n // tn, k, tn), F32)] + e_shape,
        scratch_shapes=e_scr,
        compiler_params=_params(ride, dimension_semantics=("arbitrary", "arbitrary")),
    )(a_b, b_b, *e_args)
    return out[0] if ride is None else out


def train_step(x, target, g1, an, logits, hn, gp, g_pre, g_post, w, m, v):
    nd = len(DILATIONS)
    shard_b = {k: w[k].astype(BF16) for k in BIG}
    (w_in_g,) = run_exchange(gather_exchange([shard_b["w_in"]]), "gather_w_in")
    w_in_b = w_in_g.transpose(1, 0, 2).reshape(D_MODEL, IN_PROJ_WIDTH)

    proj, h_b, *qkvs = in_proj_fwd(x, g1, w_in_b)
    attn_parts = [attn_fwd(qkv, d) for qkv, d in zip(qkvs, DILATIONS)]
    o_h, states, a_mat, w_out_g, w1_blocks = hgrn_fwd(
        proj, logits, ride=gather_exchange([shard_b["w_out"], shard_b["w_ff1"]]))
    w_out_b = w_out_g.reshape(D_MODEL, D_MODEL)
    x1, cat_b, mixed, attn, *lses, w2_g = mix_fwd(attn_parts, o_h, proj, an, hn, w_out_b, gp, x,
                                                  ride=gather_exchange([shard_b["w_ff2"]]))
    w2_b = w2_g.reshape(D_FF, D_MODEL)
    dx1, h2_b, a_b, du_b, dff_b, loss_vec, dg_pre, dg_post = mlp_fwd_bwd(x1, g_pre, w1_blocks, w2_b, g_post, target)
    dw2 = wgrad(a_b, dff_b, D_MODEL, "wgrad_ff2", ts=512)
    dw1 = wgrad(h2_b, du_b, D_FF // N_DEV, "wgrad_ff1", per_step=2)
    dmix_b, *rest = mix_bwd(dx1, mixed, gp, w_out_b, attn, an, o_h, proj, hn)
    d_os, deltas = rest[:nd], rest[nd:2 * nd]
    d_oh, dgate, dgp, dan, dhn = rest[2 * nd:]
    dwout = wgrad(cat_b, dmix_b, D_MODEL, "wgrad_out")

    early = ("w_out", "w_ff1", "w_ff2")
    early_grads = [dwout.reshape(N_DEV, D_MODEL // N_DEV, D_MODEL), dw1, dw2.reshape(N_DEV, D_FF // N_DEV, D_MODEL)]
    res = attn_bwd(qkvs[0], d_os[0], lses[0], deltas[0], DILATIONS[0], ride=to_core_exchange(early_grads))
    pairs = [pair_sum(g, s, f"pair_sum_{name}") for g, s, name in zip(early_grads, res[3:], early)]
    attn_grads = [res[:3]]
    *res, others_ff2 = attn_bwd(qkvs[1], d_os[1], lses[1], deltas[1], DILATIONS[1],
                                ride=to_chip_exchange([pairs[2][1]]))
    attn_grads.append(res)
    attn_grads.append(attn_bwd(qkvs[2], d_os[2], lses[2], deltas[2], DILATIONS[2]))
    dq_h, df_h, di_h, dlb, *others = hgrn_bwd(proj, logits, d_oh, states, a_mat,
                                              ride=to_chip_exchange([pairs[0][1], pairs[1][1]]))
    others.append(others_ff2)
    dx, dproj_b, dg1 = in_proj_bwd(attn_grads, (dq_h, df_h, di_h), dgate, w_in_b, x, g1, dx1)
    packed = _pack_small(dg1, dgp, dg_pre, dg_post, dan, dhn, dlb, loss_vec)
    dwin, small_slots = wgrad(h_b, dproj_b, 2 * IN_PROJ_WIDTH // N_DEV, "wgrad_in",
                              ride=small_exchange(packed))
    big = {name: sum_adamw(p[0], o, w[name], m[name], v[name], f"sum_adamw_{name}")
           for name, p, o in zip(early, pairs, others)}

    shard_w = IN_PROJ_WIDTH // N_DEV
    dwin_blocks = dwin.reshape(N_DEV // 2, D_MODEL, 2, shard_w).transpose(0, 2, 1, 3).reshape(N_DEV, D_MODEL, shard_w)
    (from_sibling,) = run_exchange(to_core_exchange([dwin_blocks.astype(BF16)]), "reduce_w_in_to_core")
    pair_in, pair_in_b = pair_sum(dwin_blocks, from_sibling, "pair_sum_w_in")
    (others_in,) = run_exchange(to_chip_exchange([pair_in_b]), "reduce_w_in_to_chip")
    big["w_in"] = sum_adamw(pair_in, others_in, w["w_in"], m["w_in"], v["w_in"], "sum_adamw_w_in")
    return dx, big, small_slots


def _position():
    x, y, c = lax.axis_index("x"), lax.axis_index("y"), lax.axis_index("c")
    other_chips = [(1 - x, y), (x, 1 - y), (1 - x, 1 - y)]
    return x, y, c, other_chips


def _any_spec():
    return pl.BlockSpec(memory_space=pl.ANY)


class Exchange:
    def __init__(self, arrays, out_shape, sems, stages, collective_id, peers):
        self.arrays, self.out_shape, self.sems, self.stages = list(arrays), list(out_shape), list(sems), stages
        self.collective_id, self.peers = collective_id, peers

    def open(self):
        barrier = pltpu.get_barrier_semaphore()
        peers = self.peers()
        for peer in peers:
            pl.semaphore_signal(barrier, inc=1, device_id=peer, device_id_type=MESH)
        pl.semaphore_wait(barrier, len(peers))


def _siblings():
    x, y, c, _ = _position()
    return [(x, y, 1 - c)]


def _same_core_of_other_chips():
    x, y, c, chips = _position()
    return [(px, py, c) for px, py in chips]


def _gather_peers():
    return _siblings() + _same_core_of_other_chips()


def _all_others():
    x, y, c, _ = _position()
    return [(1 - x if rel & 4 else x, 1 - y if rel & 2 else y, 1 - c if rel & 1 else c) for rel in range(1, N_DEV)]


def gather_exchange(shards):
    n = len(shards)

    def stages(ins, outs, sems):
        send_sems, recv_sems, local_sems = sems

        def parts():
            x, y, c, chips = _position()
            me, sibling = (x, y, c), (x, y, 1 - c)

            def slot(a, px, py, pc):
                return outs[a].at[4 * px + 2 * py + pc]

            def copy(a, k, block, to, src=None):
                return pltpu.make_async_remote_copy(
                    src_ref=slot(a, *block) if src is None else src, dst_ref=slot(a, *block),
                    send_sem=send_sems.at[a, k], recv_sem=recv_sems.at[a, k], device_id=to, device_id_type=MESH)

            local = [pltpu.make_async_copy(ins[a], slot(a, *me), local_sems.at[a]) for a in range(n)]
            first = []
            for a in range(n):
                first.append(copy(a, 0, me, sibling, src=ins[a]))
                first += [copy(a, 1 + j, me, (*chip, c), src=ins[a]) for j, chip in enumerate(chips)]
            passed = [copy(a, 4 + j, (*chip, c), sibling) for j, chip in enumerate(chips) for a in range(n)]
            return c, chips, me, sibling, copy, local, first, passed

        def begin():
            _, _, _, _, _, local, first, _ = parts()
            for cp in local + first:
                cp.start()

        def middle():
            c, chips, me, _, copy, _, _, passed = parts()
            k = 0
            for j, chip in enumerate(chips):
                for a in range(n):
                    copy(a, 1 + j, (*chip, c), me).wait_recv()
                    passed[k].start()
                    k += 1

        def end():
            c, chips, me, sibling, copy, local, first, passed = parts()
            for a in range(n):
                copy(a, 0, sibling, me).wait_recv()
                for j, chip in enumerate(chips):
                    copy(a, 4 + j, (*chip, 1 - c), me).wait_recv()
            for cp in first + passed:
                cp.wait_send()
            for cp in local:
                cp.wait()

        return begin, middle, end

    return Exchange(
        shards, [jax.ShapeDtypeStruct((N_DEV,) + sh.shape, sh.dtype) for sh in shards],
        [pltpu.SemaphoreType.DMA((n, 7)), pltpu.SemaphoreType.DMA((n, 7)), pltpu.SemaphoreType.DMA((n,))], stages,
        collective_id=0, peers=_gather_peers)


def to_core_exchange(grads):
    n = len(grads)

    def stages(ins, outs, sems):
        send_sems, recv_sems = sems

        def copies():
            x, y, c, _ = _position()
            return [pltpu.make_async_remote_copy(
                src_ref=ins[a].at[2 * q + (1 - c)], dst_ref=outs[a].at[q], send_sem=send_sems.at[a, q],
                recv_sem=recv_sems.at[a, q], device_id=(x, y, 1 - c), device_id_type=MESH)
                for a in range(n) for q in range(4)]

        def begin():
            for cp in copies():
                cp.start()

        def end():
            for cp in copies():
                cp.wait()

        return begin, None, end

    return Exchange(grads, [jax.ShapeDtypeStruct((4,) + g.shape[1:], g.dtype) for g in grads],
                    [pltpu.SemaphoreType.DMA((n, 4)), pltpu.SemaphoreType.DMA((n, 4))], stages,
                    collective_id=1, peers=_siblings)


def pair_sum(grad, from_sibling, name):
    _, r, cdim = grad.shape
    tr = min(r, ELEMENTWISE_ROWS)
    c_idx = lax.axis_index("c").astype(jnp.int32).reshape(1)

    def body(c_ref, g_ref, s_ref, o_ref, ob_ref):
        total = g_ref[...] + s_ref[...]
        o_ref[...] = total
        ob_ref[...] = total.astype(BF16)

    blk = lambda: pl.BlockSpec((1, tr, cdim), lambda q, i, cr: (q, i, 0))
    return pl.pallas_call(
        body,
        name=name,
        grid_spec=pltpu.PrefetchScalarGridSpec(
            num_scalar_prefetch=1,
            grid=(4, r // tr),
            in_specs=[pl.BlockSpec((1, tr, cdim), lambda q, i, cr: (2 * q + cr[0], i, 0)), blk()],
            out_specs=[blk(), blk()],
        ),
        out_shape=[jax.ShapeDtypeStruct((4, r, cdim), F32), jax.ShapeDtypeStruct((4, r, cdim), BF16)],
        compiler_params=_params(dimension_semantics=("arbitrary", "arbitrary")),
    )(c_idx, grad, from_sibling)


def to_chip_exchange(pairs):
    n = len(pairs)

    def stages(ins, outs, sems):
        send_sems, recv_sems = sems

        def copies():
            x, y, c, chips = _position()
            return [pltpu.make_async_remote_copy(
                src_ref=ins[a].at[2 * px + py], dst_ref=outs[a].at[j], send_sem=send_sems.at[a, j],
                recv_sem=recv_sems.at[a, j], device_id=(px, py, c), device_id_type=MESH)
                for a in range(n) for j, (px, py) in enumerate(chips)]

        def begin():
            for cp in copies():
                cp.start()

        def end():
            for cp in copies():
                cp.wait()

        return begin, None, end

    return Exchange(pairs, [jax.ShapeDtypeStruct((3,) + p.shape[1:], p.dtype) for p in pairs],
                    [pltpu.SemaphoreType.DMA((n, 3)), pltpu.SemaphoreType.DMA((n, 3))], stages,
                    collective_id=2, peers=_same_core_of_other_chips)


def run_exchange(ex, name):
    n_in, n_out = len(ex.arrays), len(ex.out_shape)

    def body(*refs):
        begin, middle, end = ex.stages(refs[:n_in], refs[n_in:n_in + n_out], refs[n_in + n_out:])
        ex.open()
        begin()
        if middle is not None:
            middle()
        end()

    return pl.pallas_call(
        body,
        name=name,
        in_specs=[_any_spec()] * n_in,
        out_specs=[_any_spec()] * n_out,
        out_shape=ex.out_shape,
        scratch_shapes=ex.sems,
        compiler_params=pltpu.CompilerParams(collective_id=ex.collective_id),
    )(*ex.arrays)


def _riding(body, n_in, n_out, n_scratch, ex, first, middle, last):
    if ex is None:
        return body
    r_in, r_out = len(ex.arrays), len(ex.out_shape)

    def wrapped(*refs):
        k_in, refs = refs[:n_in], refs[n_in:]
        e_in, refs = refs[:r_in], refs[r_in:]
        k_out, refs = refs[:n_out], refs[n_out:]
        e_out, refs = refs[:r_out], refs[r_out:]
        k_scr, e_sems = refs[:n_scratch], refs[n_scratch:]
        begin, mid, end = ex.stages(e_in, e_out, e_sems)

        @pl.when(first())
        def _():
            ex.open()
            begin()

        body(*k_in, *k_out, *k_scr)
        if mid is not None:
            pl.when(middle())(mid)
        pl.when(last())(end)

    return wrapped


def _ride_specs(ex):
    if ex is None:
        return [], [], [], [], []
    return [_any_spec()] * len(ex.arrays), [_any_spec()] * len(ex.out_shape), ex.out_shape, ex.sems, ex.arrays


def _adamw(w, g, m, v):
    m = ADAM_B1 * m + (1.0 - ADAM_B1) * g
    v = ADAM_B2 * v + (1.0 - ADAM_B2) * (g * g)
    m_hat = m / (1.0 - ADAM_B1 ** ADAM_STEP)
    v_hat = v / (1.0 - ADAM_B2 ** ADAM_STEP)
    delta = -ADAM_LR * (m_hat / (jnp.sqrt(v_hat) + ADAM_EPS) + ADAM_WD * w)
    return delta, m, v


def sum_adamw(pairs, others, w, m, v, name):
    r, cdim = w.shape
    tr = min(r, ELEMENTWISE_ROWS // 2)
    chip_idx =(2 * lax.axis_index("x") + lax.axis_index("y")).astype(jnp.int32).reshape(1)

    def body(q_ref, p_ref, o_ref, w_ref, m_ref, v_ref, g_out, d_out, m_out, v_out):
        g = p_ref[0] + o_ref[0].astype(F32) + o_ref[1].astype(F32) + o_ref[2].astype(F32)
        g_out[...] = g
        d_out[...], m_out[...], v_out[...] = _adamw(w_ref[...], g, m_ref[...], v_ref[...])

    tile = lambda: pl.BlockSpec((tr, cdim), lambda i, qr: (i, 0))
    return pl.pallas_call(
        body,
        name=name,
        grid_spec=pltpu.PrefetchScalarGridSpec(
            num_scalar_prefetch=1,
            grid=(r // tr,),
            in_specs=[pl.BlockSpec((1, tr, cdim), lambda i, qr: (qr[0], i, 0)),
                      pl.BlockSpec((3, tr, cdim), lambda i, qr: (0, i, 0)), tile(), tile(), tile()],
            out_specs=[tile(), tile(), tile(), tile()],
        ),
        out_shape=[jax.ShapeDtypeStruct((r, cdim), F32)] * 4,
        compiler_params=_params(dimension_semantics=("arbitrary",)),
    )(chip_idx, pairs, others, w, m, v)


def small_exchange(packed):
    def stages(ins, outs, sems):
        send_sems, recv_sems, local_sem = sems
        (src,), (slots,) = ins, outs

        def copies():
            x, y, c, _ = _position()
            my_id = 4 * x + 2 * y + c
            sends, landings = [], []
            for rel in range(1, N_DEV):
                px = 1 - x if (rel >> 2) & 1 else x
                py = 1 - y if (rel >> 1) & 1 else y
                pc = 1 - c if rel & 1 else c
                peer = dict(send_sem=send_sems.at[rel - 1], recv_sem=recv_sems.at[rel - 1], device_id=(px, py, pc),
                            device_id_type=MESH)
                sends.append(pltpu.make_async_remote_copy(src_ref=src, dst_ref=slots.at[my_id], **peer))
                landings.append(pltpu.make_async_remote_copy(src_ref=src, dst_ref=slots.at[4 * px + 2 * py + pc], **peer))
            return pltpu.make_async_copy(src, slots.at[my_id], local_sem), sends, landings

        def begin():
            local, sends, _ = copies()
            local.start()
            for cp in sends:
                cp.start()

        def end():
            local, sends, landings = copies()
            for cp in landings:
                cp.wait_recv()
            for cp in sends:
                cp.wait_send()
            local.wait()

        return begin, None, end

    return Exchange([packed], [jax.ShapeDtypeStruct((N_DEV,) + packed.shape, packed.dtype)],
                    [pltpu.SemaphoreType.DMA((N_DEV - 1,)), pltpu.SemaphoreType.DMA((N_DEV - 1,)),
                     pltpu.SemaphoreType.DMA(())], stages, collective_id=3, peers=_all_others)


def small_adamw(slots, w, m, v):
    def body(r_ref, w_ref, m_ref, v_ref, g_out, d_out, m_out, v_out, loss_out):
        red = r_ref[0]
        for k in range(1, N_DEV):
            red = red + r_ref[k]
        wv = w_ref[...]
        lb = _lower_bound(jnp.concatenate([wv[5:6, :HGRN_WIDTH], wv[5:6, HGRN_WIDTH:]], axis=0))
        t = red[5:6, :HGRN_WIDTH] * lb * (1.0 - lb)
        row = lax.broadcasted_iota(jnp.int32, red.shape, 0)
        g = jnp.where(row == 5, jnp.concatenate([t, -t], axis=1), jnp.where(row >= 6, 0.0, red))
        g_out[...] = g
        d_out[...], m_out[...], v_out[...] = _adamw(wv, g, m_ref[...], v_ref[...])
        loss = jnp.sum(red[6:7, :], axis=-1, keepdims=True) * (0.5 / D_MODEL)
        loss_out[...] = jnp.broadcast_to(loss, loss_out.shape)

    return pl.pallas_call(
        body,
        name="small_adamw",
        in_specs=[_vmem_spec()] * 4,
        out_specs=[_vmem_spec()] * 5,
        out_shape=[jax.ShapeDtypeStruct(w.shape, F32)] * 4 + [jax.ShapeDtypeStruct((SUBLANES, LANES), F32)],
    )(slots, w, m, v)


def _pack_small(g1, gp, g_pre, g_post, an, hn, logits_or_dlb, extra=None):
    row5 = logits_or_dlb.reshape(1, -1)
    row5 = jnp.pad(row5, ((0, 0), (0, D_MODEL - row5.shape[1])))
    row6 = jnp.zeros((1, D_MODEL), F32) if extra is None else extra
    return jnp.concatenate([g1, gp, g_pre, g_post, jnp.concatenate([an, hn], axis=1), row5, row6,
                            jnp.zeros((1, D_MODEL), F32)], axis=0)


def _unpack_small(p):
    return dict(mix_pre_norm=p[0:1], mix_post_norm=p[1:2], mlp_pre_norm=p[2:3], mlp_post_norm=p[3:4],
                attn_out_norm=p[4:5, :ATTN_WIDTH], hgrn_out_norm=p[4:5, ATTN_WIDTH:],
                hgrn_lb_logits=p[5].reshape(2, HGRN_WIDTH))


BIG = ("w_in", "w_out", "w_ff1", "w_ff2")
ORDER = ("mix_pre_norm", "w_in", "attn_out_norm", "hgrn_lb_logits", "hgrn_out_norm", "w_out", "mix_post_norm",
         "mlp_pre_norm", "w_ff1", "w_ff2", "mlp_post_norm")


def kernel(x, mix_pre_norm, w_in, attn_out_norm, hgrn_lb_logits, hgrn_out_norm, w_out, mix_post_norm, mlp_pre_norm, w_ff1, w_ff2, mlp_post_norm, loss_target, m_mix_pre_norm, m_w_in, m_attn_out_norm, m_hgrn_lb_logits, m_hgrn_out_norm, m_w_out, m_mix_post_norm, m_mlp_pre_norm, m_w_ff1, m_w_ff2, m_mlp_post_norm, v_mix_pre_norm, v_w_in, v_attn_out_norm, v_hgrn_lb_logits, v_hgrn_out_norm, v_w_out, v_mix_post_norm, v_mlp_pre_norm, v_w_ff1, v_w_ff2, v_mlp_post_norm):
    w = dict(w_in=w_in[0], w_out=w_out[0], w_ff1=w_ff1[0], w_ff2=w_ff2[0])
    m = dict(w_in=m_w_in[0], w_out=m_w_out[0], w_ff1=m_w_ff1[0], w_ff2=m_w_ff2[0])
    v = dict(w_in=v_w_in[0], w_out=v_w_out[0], w_ff1=v_w_ff1[0], w_ff2=v_w_ff2[0])

    dx, big, small_slots = train_step(x[0], loss_target[0], mix_pre_norm, attn_out_norm, hgrn_lb_logits, hgrn_out_norm,
                                      mix_post_norm, mlp_pre_norm, mlp_post_norm, w, m, v)

    pack = lambda a, b, c2, d, e, f, g: _pack_small(a, b, c2, d, e, f, g)
    w_s = pack(mix_pre_norm, mix_post_norm, mlp_pre_norm, mlp_post_norm, attn_out_norm, hgrn_out_norm, hgrn_lb_logits)
    m_s = pack(m_mix_pre_norm, m_mix_post_norm, m_mlp_pre_norm, m_mlp_post_norm, m_attn_out_norm, m_hgrn_out_norm,
               m_hgrn_lb_logits)
    v_s = pack(v_mix_pre_norm, v_mix_post_norm, v_mlp_pre_norm, v_mlp_post_norm, v_attn_out_norm, v_hgrn_out_norm,
               v_hgrn_lb_logits)
    g_s, d_s, nm_s, nv_s, loss = small_adamw(small_slots, w_s, m_s, v_s)
    small_out = [_unpack_small(t) for t in (g_s, d_s, nm_s, nv_s)]

    outs = [loss[0, 0], dx[None]]
    for kind in range(4):
        for name in ORDER:
            outs.append(big[name][kind][None] if name in BIG else small_out[kind][name])
    return tuple(outs)
```

```python
import jax
import jax.numpy as jnp
from jax import lax
from jax.experimental import pallas as pl
from jax.experimental.pallas import tpu as pltpu

F32 = jnp.float32
BF16 = jnp.bfloat16

D_MODEL = 1024
ATTN_WIDTH = 512
ATTN_HEAD_DIM = 64
ATTN_HEADS = 8
ATTN_BLOCK = 128
DILATIONS = (1, 4, 16)
HGRN_WIDTH = 512
HGRN_HEADS = 4
HGRN_HEAD_DIM = 128
HGRN_CHUNK = 64
IN_PROJ_WIDTH = 3584
D_FF = 4096
RMS_EPS = 1e-6
N_DEV = 8
ADAM_LR = 0.001
ADAM_B1 = 0.9
ADAM_B2 = 0.999
ADAM_EPS = 1e-08
ADAM_WD = 0.01
ADAM_STEP = 10

SUBLANES = 8
LANES = 128
COLUMN_UNROLL = 8
HGRN_CHUNKS_PER_STEP = 2
SUB_BLOCK = 16
TOKEN_TILE = 512
ELEMENTWISE_ROWS = 1024
MLP_TILE = 256
PROJ_TILE = 512
VMEM_BYTES_V7X = 64 * 1024 * 1024
VMEM_LIMIT = VMEM_BYTES_V7X // 8 * 7
NEG_BIG = -1e30
MESH = pl.DeviceIdType.MESH


def _params(ride=None, **kw):
    if ride is not None:
        kw["collective_id"] = ride.collective_id
    return pltpu.CompilerParams(vmem_limit_bytes=VMEM_LIMIT, **kw)


def _vmem_spec():
    return pl.BlockSpec(memory_space=pltpu.VMEM)


def _dot(a, b):
    return jnp.dot(a, b, preferred_element_type=F32)


def _dot_nt(a, b):
    return lax.dot_general(a, b, (((1,), (1,)), ((), ())), preferred_element_type=F32)


def _dot_tn(a, b):
    return lax.dot_general(a, b, (((0,), (0,)), ((), ())), preferred_element_type=F32)


def _sigmoid(x):
    return 1.0 / (1.0 + jnp.exp(-x))


def _rms_fwd(x, gain, width):
    r = lax.rsqrt(jnp.sum(x * x, axis=-1, keepdims=True) * (1.0 / width) + RMS_EPS)
    return x * r * gain


def _rms_bwd(dy, x, gain, width):
    r = lax.rsqrt(jnp.sum(x * x, axis=-1, keepdims=True) * (1.0 / width) + RMS_EPS)
    xhat = x * r
    dxhat = dy * gain
    dx = r * (dxhat - xhat * (jnp.sum(dxhat * xhat, axis=-1, keepdims=True) * (1.0 / width)))
    return dx, dy * xhat


def _split3(x):
    hi = x.astype(BF16)
    r1 = x - hi.astype(F32)
    mid = r1.astype(BF16)
    lo = (r1 - mid.astype(F32)).astype(BF16)
    return hi, mid, lo


def _tri_sum(tri_bf16, x):
    hi, mid, lo = _split3(x)
    return _dot(tri_bf16, hi) + _dot(tri_bf16, mid) + _dot(tri_bf16, lo)


def _dilated_spec(d, tm, width):
    return pl.BlockSpec((d, tm // d, width), lambda i: (0, i, 0))


def _lane_blocks(ref, value):
    for c in range(ref.shape[0]):
        ref[c] = value[:, c * LANES:(c + 1) * LANES]


def _to_dilated(src_ref, dst_ref, d, tm, cast=None):
    for r in range(d):
        for c in range(src_ref.shape[0]):
            v = src_ref[c] if d == 1 else src_ref[c, pl.ds(r, tm // d, stride=d), :]
            dst_ref[r, :, c * LANES:(c + 1) * LANES] = v if cast is None else v.astype(cast)


def _from_dilated(src_ref, scratch_ref, d, tm):
    if d == 1:
        return src_ref[0].astype(F32)
    nblk = scratch_ref.shape[0]
    for r in range(d):
        for c in range(nblk):
            scratch_ref[c, pl.ds(r, tm // d, stride=d), :] = src_ref[r, :, c * LANES:(c + 1) * LANES].astype(F32)
    return jnp.concatenate([scratch_ref[c] for c in range(nblk)], axis=1)


def in_proj_fwd(x, g1, w_in_b, ride=None):
    s = x.shape[0]
    tm = PROJ_TILE
    qkv_w = 3 * ATTN_WIDTH
    hg_w = IN_PROJ_WIDTH - qkv_w

    def body(x_ref, g_ref, w_ref, hg_ref, h_ref, *rest):
        qkv_refs, qkv_scr = rest[:len(DILATIONS)], rest[len(DILATIONS)]
        h = _rms_fwd(x_ref[...], g_ref[...], D_MODEL).astype(BF16)
        h_ref[...] = h
        proj = _dot(h, w_ref[...])
        hg_ref[...] = proj[:, qkv_w:]
        _lane_blocks(qkv_scr, proj[:, :qkv_w])
        for d, ref in zip(DILATIONS, qkv_refs):
            _to_dilated(qkv_scr, ref, d, tm, cast=BF16)

    n_steps = s // tm
    step = lambda k: (lambda: pl.program_id(0) == k)
    e_in, e_out, e_shape, e_scr, e_args = _ride_specs(ride)
    return pl.pallas_call(
        _riding(body, 3, 2 + len(DILATIONS), 1, ride, step(0), step(n_steps - 1), step(n_steps - 1)),
        name="in_proj_fwd",
        grid=(n_steps,),
        in_specs=[
            pl.BlockSpec((tm, D_MODEL), lambda i: (i, 0)),
            pl.BlockSpec((1, D_MODEL), lambda i: (0, 0)),
            _vmem_spec(),
        ] + e_in,
        out_specs=[
            pl.BlockSpec((tm, hg_w), lambda i: (i, 0)),
            pl.BlockSpec((tm, D_MODEL), lambda i: (i, 0)),
        ] + [_dilated_spec(d, tm, qkv_w) for d in DILATIONS] + e_out,
        out_shape=[jax.ShapeDtypeStruct((s, hg_w), F32), jax.ShapeDtypeStruct((s, D_MODEL), BF16)] + [
            jax.ShapeDtypeStruct((d, s // d, qkv_w), BF16) for d in DILATIONS] + e_shape,
        scratch_shapes=[pltpu.VMEM((qkv_w // LANES, tm, LANES), F32)] + e_scr,
        compiler_params=_params(ride, dimension_semantics=("arbitrary",)),
    )(x, g1, w_in_b, *e_args)


ATTN_SCALE = ATTN_HEAD_DIM ** -0.5


def _fill_attn_bias(bias_ref, dilation):
    qi = lax.broadcasted_iota(jnp.int32, (ATTN_BLOCK, 2 * ATTN_BLOCK), 0)
    kj = lax.broadcasted_iota(jnp.int32, (ATTN_BLOCK, 2 * ATTN_BLOCK), 1)
    dist = qi + ATTN_BLOCK - kj
    valid = (dist >= 0) & (dist <= ATTN_BLOCK)
    for head in range(ATTN_HEADS):
        slope = 2.0 ** (-8.0 * (head + 1) / ATTN_HEADS)
        bias = jnp.where(valid, dist.astype(F32) * (-slope * dilation), NEG_BIG)
        bias_ref[0, head] = bias
        bias_ref[1, head] = jnp.where(kj >= ATTN_BLOCK, bias, NEG_BIG)


def _stack_heads(x):
    low = _lane_half(x.shape, 0)
    zero = jnp.zeros_like(x)
    return jnp.concatenate([jnp.where(low, x, zero), jnp.where(low, zero, x)], axis=0)


def _unstack_heads(y):
    half = y.shape[0] // 2
    return jnp.where(_lane_half((half, y.shape[1]), 0), y[:half], y[half:])


def _attn_scores(q_stack, kcat, bias_ref, pair, first_block):
    f = first_block.astype(jnp.int32)
    bias = jnp.concatenate([bias_ref[f, 2 * pair], bias_ref[f, 2 * pair + 1]], axis=0)
    return _dot_nt(q_stack, kcat) + bias


def _lane_half(shape, sub):
    lane = lax.broadcasted_iota(jnp.int32, shape, 1)
    return (lane < ATTN_HEAD_DIM) if sub == 0 else (lane >= ATTN_HEAD_DIM)


def _sub_block(col, row):
    return pl.BlockSpec((None, ATTN_BLOCK, ATTN_WIDTH), lambda r, n: (r, row(n), col))


def attn_fwd(qkv, dilation):
    d, length, _ = qkv.shape
    assert d == dilation
    nb = length // ATTN_BLOCK

    def body(q_ref, kc_ref, kp_ref, vc_ref, vp_ref, o_ref, lse_ref, bias_ref):
        @pl.when((pl.program_id(0) == 0) & (pl.program_id(1) == 0))
        def _():
            _fill_attn_bias(bias_ref, d)

        first = pl.program_id(1) == 0
        for pair in range(ATTN_HEADS // 2):
            lanes = slice(pair * LANES, (pair + 1) * LANES)
            q_stack = _stack_heads(q_ref[:, lanes] * ATTN_SCALE)
            kcat = jnp.concatenate([kp_ref[:, lanes], kc_ref[:, lanes]], axis=0)
            vcat = jnp.concatenate([vp_ref[:, lanes], vc_ref[:, lanes]], axis=0)
            sc = _attn_scores(q_stack, kcat, bias_ref, pair, first)
            m = jnp.max(sc, axis=-1, keepdims=True)
            p = jnp.exp(sc - m)
            den = jnp.sum(p, axis=-1, keepdims=True)
            o_ref[:, lanes] = _unstack_heads(_dot(p.astype(BF16), vcat) / den).astype(BF16)
            lse_ref[:, lanes] = _unstack_heads(jnp.broadcast_to(m + jnp.log(den), (2 * ATTN_BLOCK, LANES)))

    cur = lambda n: n
    prev = lambda n: jnp.maximum(n - 1, 0)
    return pl.pallas_call(
        body,
        name=f"attn_fwd_d{d}",
        grid=(d, nb),
        in_specs=[_sub_block(0, cur), _sub_block(1, cur), _sub_block(1, prev), _sub_block(2, cur), _sub_block(2, prev)],
        out_specs=[_sub_block(0, cur), _sub_block(0, cur)],
        out_shape=[jax.ShapeDtypeStruct((d, length, ATTN_WIDTH), BF16), jax.ShapeDtypeStruct((d, length, ATTN_WIDTH), F32)],
        scratch_shapes=[pltpu.VMEM((2, ATTN_HEADS, ATTN_BLOCK, 2 * ATTN_BLOCK), F32)],
        compiler_params=_params(dimension_semantics=("arbitrary", "arbitrary")),
    )(qkv, qkv, qkv, qkv, qkv)


def attn_bwd(qkv, d_out, lse, delta, dilation, ride=None):
    d, length, _ = qkv.shape
    assert d == dilation
    nb = length // ATTN_BLOCK

    steps = d * nb + 1

    def body(q_ref, kc_ref, kp_ref, vc_ref, vp_ref, do_ref, lse_ref, dl_ref, dq_ref, dk_ref, dv_ref, ck_ref, cv_ref,
             bias_ref):
        t = pl.program_id(0)

        @pl.when(t == 0)
        def _():
            ck_ref[...] = jnp.zeros_like(ck_ref)
            cv_ref[...] = jnp.zeros_like(cv_ref)
            _fill_attn_bias(bias_ref, d)

        @pl.when(t < steps - 1)
        def _():
            first = t % nb == 0
            for pair in range(ATTN_HEADS // 2):
                lanes = slice(pair * LANES, (pair + 1) * LANES)
                q_stack = _stack_heads(q_ref[:, lanes] * ATTN_SCALE)
                do_stack = _stack_heads(do_ref[:, lanes])
                kcat = jnp.concatenate([kp_ref[:, lanes], kc_ref[:, lanes]], axis=0)
                vcat = jnp.concatenate([vp_ref[:, lanes], vc_ref[:, lanes]], axis=0)
                col_a, col_b = pair * LANES, pair * LANES + ATTN_HEAD_DIM
                lse_col = jnp.concatenate([lse_ref[:, col_a:col_a + 1], lse_ref[:, col_b:col_b + 1]], axis=0)
                dl_col = jnp.concatenate([dl_ref[:, col_a:col_a + 1], dl_ref[:, col_b:col_b + 1]], axis=0)
                p = jnp.exp(_attn_scores(q_stack, kcat, bias_ref, pair, first) - lse_col)
                ds = (p * (_dot_nt(do_stack, vcat) - dl_col)).astype(BF16)
                dq_ref[:, lanes] = (_unstack_heads(_dot(ds, kcat)) * ATTN_SCALE).astype(BF16)
                dk_cat = _dot_tn(ds, q_stack)
                dv_cat = _dot_tn(p.astype(BF16), do_stack)
                dk_ref[:, lanes] = (ck_ref[:, lanes] + dk_cat[:ATTN_BLOCK]).astype(BF16)
                dv_ref[:, lanes] = (cv_ref[:, lanes] + dv_cat[:ATTN_BLOCK]).astype(BF16)
                ck_ref[:, lanes] = dk_cat[ATTN_BLOCK:]
                cv_ref[:, lanes] = dv_cat[ATTN_BLOCK:]

        @pl.when(t == steps - 1)
        def _():
            dk_ref[...] = ck_ref[...].astype(BF16)
            dv_ref[...] = cv_ref[...].astype(BF16)

    blk = (ATTN_BLOCK, ATTN_WIDTH)

    def spec(col, shift):
        def index(t):
            f = jnp.minimum(t, steps - 2) if shift > -2 else jnp.maximum(t - 1, 0)
            r, n = f // nb, f % nb
            return (r, jnp.maximum(n - 1, 0) if shift == -1 else n, col)
        return pl.BlockSpec((None, ATTN_BLOCK, ATTN_WIDTH), index)

    step = lambda k: (lambda: pl.program_id(0) == k)
    e_in, e_out, e_shape, e_scr, e_args = _ride_specs(ride)
    return pl.pallas_call(
        _riding(body, 8, 3, 3, ride, step(0), step(steps // 2), step(steps - 1)),
        name=f"attn_bwd_d{d}",
        grid=(steps,),
        in_specs=[spec(0, 0), spec(1, 0), spec(1, -1), spec(2, 0), spec(2, -1), spec(0, 0), spec(0, 0), spec(0, 0)] + e_in,
        out_specs=[spec(0, 0), spec(0, -2), spec(0, -2)] + e_out,
        out_shape=[jax.ShapeDtypeStruct((d, length, ATTN_WIDTH), BF16)] * 3 + e_shape,
        scratch_shapes=[pltpu.VMEM(blk, F32), pltpu.VMEM(blk, F32),
                        pltpu.VMEM((2, ATTN_HEADS, ATTN_BLOCK, 2 * ATTN_BLOCK), F32)] + e_scr,
        compiler_params=_params(ride, dimension_semantics=("arbitrary",)),
    )(qkv, qkv, qkv, qkv, qkv, d_out, lse, delta, *e_args)


def _lower_bound(logits):
    return _sigmoid(logits[0:1, :] - logits[1:2, :])


def _hgrn_gates(q, fp, lb):
    sq = _sigmoid(q)
    qf = q * sq
    sig = _sigmoid(fp)
    sig_neg = _sigmoid(-fp)
    kf = (1.0 - lb) * sig_neg
    log_sig = jnp.minimum(fp, 0.0) - jnp.log(1.0 + jnp.exp(-jnp.abs(fp)))
    a = jnp.log(lb)
    c = jnp.log(1.0 - lb) + log_sig
    log_f = jnp.maximum(a, c) + jnp.log(1.0 + jnp.exp(-jnp.abs(a - c)))
    return sq, qf, (sig, sig_neg, c), log_f, kf


def _tril_bf16(n, upper=False):
    r = lax.broadcasted_iota(jnp.int32, (n, n), 0)
    c = lax.broadcasted_iota(jnp.int32, (n, n), 1)
    keep = (c >= r) if upper else (c <= r)
    return jnp.where(keep, 1.0, 0.0).astype(BF16)


def _hgrn_diagonal_loops(c_len, diagonal):
    for half in range(SUB_BLOCK // SUBLANES):
        def step(jj, carry, half=half):
            j = half * SUBLANES + jj
            for i in range(c_len // SUB_BLOCK):
                diagonal(slice(i * SUB_BLOCK + half * SUBLANES, (i + 1) * SUB_BLOCK), j, i * SUB_BLOCK + j)
            return carry

        lax.fori_loop(0, SUBLANES, step, 0, unroll=COLUMN_UNROLL)


def _hgrn_off_diagonal(b, qf, kf):
    c_len, width = b.shape
    edges = [b[0:1, :]] + [b[i * SUB_BLOCK - 1:i * SUB_BLOCK, :] for i in range(1, c_len // SUB_BLOCK)]
    eq = jnp.exp(b - jnp.concatenate([jnp.broadcast_to(e, (SUB_BLOCK, width)) for e in edges], axis=0))
    q_til = qf * eq
    k_til, ek = [], []
    for i in range(1, c_len // SUB_BLOCK):
        n = i * SUB_BLOCK
        e = jnp.exp(edges[i] - b[:n, :])
        ek.append(e)
        k_til.append(jnp.concatenate([kf[:n, :] * e, jnp.zeros((2 * c_len - n, width), F32)], axis=0))
    return q_til, k_til, eq, ek


def _split2(x):
    hi = x.astype(BF16)
    return hi, (x - hi.astype(F32)).astype(BF16)


def hgrn_fwd(proj, lb, ride=None):
    s = proj.shape[0]
    c_len, nh, hd = HGRN_CHUNK, HGRN_HEADS, HGRN_HEAD_DIM
    n_chunks = s // c_len
    col0 = 0

    cps = 2 * HGRN_CHUNKS_PER_STEP
    n_steps = n_chunks // cps

    def body(q_ref, f_ref, i_ref, lb_ref, o_ref, st_out_ref, a_out_ref, st_ref, b_ref, qf_ref, kf_ref, a_ref):
        @pl.when(pl.program_id(0) == 0)
        def _():
            st_ref[...] = jnp.zeros_like(st_ref)

        lbv = _lower_bound(lb_ref[...])
        for u in range(cps):
            rs = slice(u * c_len, (u + 1) * c_len)
            b_u, qf_u, kf_u, a_u = b_ref.at[u], qf_ref.at[u], kf_ref.at[u], a_ref.at[u]
            _, qf, _, log_f, kf = _hgrn_gates(q_ref[rs, :], f_ref[rs, :], lbv)
            b = _tri_sum(_tril_bf16(c_len), log_f)
            b_u[...] = b
            qf_u[...] = qf
            kf_u[...] = kf
            a_u[...] = jnp.zeros_like(a_u)

            def diagonal(rows, j, key, b_u=b_u, qf_u=qf_u, kf_u=kf_u, a_u=a_u):
                bj = b_u[pl.ds(key, 1), :]
                kj = kf_u[pl.ds(key, 1), :]
                nrow = rows.stop - rows.start
                t_loc = lax.broadcasted_iota(jnp.int32, (nrow, nh * hd), 0) + (rows.start % SUB_BLOCK)
                e = jnp.exp(jnp.where(t_loc >= j, b_u[rows, :] - bj, NEG_BIG))
                prod = qf_u[rows, :] * kj * e
                lane = lax.broadcasted_iota(jnp.int32, (nrow, hd), 1)
                for h in range(nh):
                    col = jnp.sum(prod[:, h * hd:(h + 1) * hd], axis=-1, keepdims=True)
                    a_u[h, rows, :] = jnp.where(lane == key, col, a_u[h, rows, :])

            _hgrn_diagonal_loops(c_len, diagonal)
            q_til, k_til, _, _ = _hgrn_off_diagonal(b, qf, kf)
            q_til = q_til.astype(BF16)
            k_til = [k.astype(BF16) for k in k_til]

            b_last = b[c_len - 1:c_len, :]
            qb = (qf * jnp.exp(b)).astype(BF16)
            kb2 = (kf * jnp.exp(b_last - b)).astype(BF16)
            vf = i_ref[rs, :].astype(BF16)
            for h in range(nh):
                hs = slice(h * hd, (h + 1) * hd)
                st = st_ref[h]
                st_out_ref[u, h] = st
                off = [jnp.zeros((SUB_BLOCK, hd), F32)]
                for i in range(1, c_len // SUB_BLOCK):
                    off.append(_dot_nt(q_til[i * SUB_BLOCK:(i + 1) * SUB_BLOCK, hs], k_til[i - 1][:, hs]))
                a_h = a_u[h] + jnp.concatenate(off, axis=0)
                a_out_ref[rs, hs] = a_h
                o_ref[rs, hs] = _dot_nt(qb[:, hs], st.astype(BF16)) + _dot(a_h[:, :c_len].astype(BF16), vf[:, hs])
                st_ref[h] = st * jnp.exp(b_last[:, hs]) + _dot_tn(vf[:, hs], kb2[:, hs])

    blk = (cps * c_len, HGRN_WIDTH)
    sblk = (cps, c_len, HGRN_WIDTH)
    step = lambda k: (lambda: pl.program_id(0) == k)
    e_in, e_out, e_shape, e_scr, e_args = _ride_specs(ride)
    return pl.pallas_call(
        _riding(body, 4, 3, 5, ride, step(0), step((7 * n_steps) // 8), step(n_steps - 1)),
        name="hgrn_fwd",
        grid=(n_steps,),
        in_specs=[
            pl.BlockSpec(blk, lambda c: (c, col0)),
            pl.BlockSpec(blk, lambda c: (c, col0 + 1)),
            pl.BlockSpec(blk, lambda c: (c, col0 + 2)),
            pl.BlockSpec((2, HGRN_WIDTH), lambda c: (0, 0)),
        ] + e_in,
        out_specs=[
            pl.BlockSpec(blk, lambda c: (c, 0)),
            pl.BlockSpec((cps, nh, hd, hd), lambda c: (c, 0, 0, 0)),
            pl.BlockSpec(blk, lambda c: (c, 0)),
        ] + e_out,
        out_shape=[
            jax.ShapeDtypeStruct((s, HGRN_WIDTH), F32),
            jax.ShapeDtypeStruct((n_chunks, nh, hd, hd), F32),
            jax.ShapeDtypeStruct((s, nh * hd), F32),
        ] + e_shape,
        scratch_shapes=[
            pltpu.VMEM((nh, hd, hd), F32),
            pltpu.VMEM(sblk, F32),
            pltpu.VMEM(sblk, F32),
            pltpu.VMEM(sblk, F32),
            pltpu.VMEM((cps, nh, c_len, hd), F32),
        ] + e_scr,
        compiler_params=_params(ride, dimension_semantics=("arbitrary",)),
    )(proj, proj, proj, lb, *e_args)


def hgrn_bwd(proj, lb, d_o, states, a_mat, ride=None):
    s = proj.shape[0]
    c_len, nh, hd = HGRN_CHUNK, HGRN_HEADS, HGRN_HEAD_DIM
    n_chunks = s // c_len
    col0 = 0
    cps = HGRN_CHUNKS_PER_STEP
    n_steps = n_chunks // cps
    last = n_steps - 1

    def body(q_ref, f_ref, i_ref, lb_ref, do_ref, st_in_ref, a_in_ref, dq_ref, df_ref, di_ref, dlb_ref,
             dst_ref, b_ref, qf_ref, kf_ref, da_ref, dqi_ref, dki_ref):
        @pl.when(pl.program_id(0) == 0)
        def _():
            dst_ref[...] = jnp.zeros_like(dst_ref)
            dlb_ref[...] = jnp.zeros_like(dlb_ref)

        lbv = _lower_bound(lb_ref[...])
        for u in reversed(range(cps)):
            rs = slice(u * c_len, (u + 1) * c_len)
            b_u, qf_u, kf_u, da_u, dqi_u, dki_u = (b_ref.at[u], qf_ref.at[u], kf_ref.at[u], da_ref.at[u], dqi_ref.at[u],
                                                   dki_ref.at[u])
            q = q_ref[rs, :]
            sq, qf, (sig, sig_neg, log_c), log_f, kf = _hgrn_gates(q, f_ref[rs, :], lbv)
            b = _tri_sum(_tril_bf16(c_len), log_f)
            b_u[...] = b
            qf_u[...] = qf
            kf_u[...] = kf
            b_last = b[c_len - 1:c_len, :]
            eb = jnp.exp(b)
            ebl = jnp.exp(b_last - b)
            qb = qf * eb
            kb2 = kf * ebl
            vf = i_ref[rs, :]
            d_o = do_ref[rs, :]
            qb_b, kb2_b, vf_b, do_b = qb.astype(BF16), kb2.astype(BF16), vf.astype(BF16), d_o.astype(BF16)
            tq = lax.broadcasted_iota(jnp.int32, (c_len, hd), 0)
            lane = lax.broadcasted_iota(jnp.int32, (c_len, hd), 1)

            dqb_parts, dvf_parts, dkb2_parts, dbl_parts = [], [], [], []
            for h in range(nh):
                hs = slice(h * hd, (h + 1) * hd)
                st = st_in_ref[u, h]
                dst = dst_ref[h]
                st_b, dst_b = st.astype(BF16), dst.astype(BF16)
                a_h = a_in_ref[rs, hs][:, :c_len].astype(BF16)
                dqb_parts.append(_dot(do_b[:, hs], st_b))
                dvf_parts.append(_dot_tn(a_h, do_b[:, hs]) + _dot_nt(kb2_b[:, hs], dst_b))
                dkb2_parts.append(_dot(vf_b[:, hs], dst_b))
                da = _dot_nt(do_b[:, hs], vf_b[:, hs])
                da = jnp.concatenate([da, jnp.zeros((c_len, hd - c_len), F32)], axis=1)
                da_u[h] = jnp.where(tq >= lane, da, 0.0)
                dbl_parts.append(jnp.sum(dst * st, axis=0, keepdims=True) * jnp.exp(b_last[:, hs]))
                dst_ref[h] = dst * jnp.exp(b_last[:, hs]) + _dot_tn(do_b[:, hs], qb_b[:, hs])
            dqb = jnp.concatenate(dqb_parts, axis=1)
            dvf = jnp.concatenate(dvf_parts, axis=1)
            dkb2 = jnp.concatenate(dkb2_parts, axis=1)
            dbl = jnp.concatenate(dbl_parts, axis=1) + jnp.sum(dkb2 * kb2, axis=0, keepdims=True)

            dqi_u[...] = jnp.zeros_like(dqi_u)
            t_idx = lax.broadcasted_iota(jnp.int32, (c_len, nh * hd), 0)

            def diagonal(rows, j, key, b_u=b_u, qf_u=qf_u, kf_u=kf_u, da_u=da_u, dqi_u=dqi_u, dki_u=dki_u):
                bj = b_u[pl.ds(key, 1), :]
                kj = kf_u[pl.ds(key, 1), :]
                nrow = rows.stop - rows.start
                t_loc = lax.broadcasted_iota(jnp.int32, (nrow, nh * hd), 0) + (rows.start % SUB_BLOCK)
                e = jnp.exp(jnp.where(t_loc >= j, b_u[rows, :] - bj, NEG_BIG))
                lane_r = lax.broadcasted_iota(jnp.int32, (nrow, hd), 1)
                cols = [jnp.sum(jnp.where(lane_r == key, da_u[h, rows, :], 0.0), axis=-1, keepdims=True)
                        for h in range(nh)]
                w = e * jnp.concatenate([jnp.broadcast_to(cc, (nrow, hd)) for cc in cols], axis=1)
                dqi_u[rows, :] += w * kj
                dki_u[pl.ds(key, 1), :] = jnp.sum(w * qf_u[rows, :], axis=0, keepdims=True)

            _hgrn_diagonal_loops(c_len, diagonal)

            q_til, k_til, eq, ek = _hgrn_off_diagonal(b, qf, kf)
            q_hi, q_lo = _split2(q_til)
            k_pairs = [_split2(k) for k in k_til]
            n_sub = c_len // SUB_BLOCK
            dq_heads, dk_heads = [], []
            for h in range(nh):
                hs = slice(h * hd, (h + 1) * hd)
                dq_rows = [jnp.zeros((SUB_BLOCK, hd), F32)]
                dk_h = jnp.zeros((c_len, hd), F32)
                for i in range(1, n_sub):
                    rows = slice(i * SUB_BLOCK, (i + 1) * SUB_BLOCK)
                    n = i * SUB_BLOCK
                    da_i = da_u[h, rows, :].astype(BF16)
                    k_hi, k_lo = k_pairs[i - 1]
                    dq_rows.append((_dot(da_i, k_hi[:, hs]) + _dot(da_i, k_lo[:, hs])) * eq[rows, hs])
                    dk_t = (_dot_tn(da_i, q_hi[rows, hs]) + _dot_tn(da_i, q_lo[rows, hs]))[:n, :] * ek[i - 1][:, hs]
                    dk_h = dk_h + jnp.concatenate([dk_t, jnp.zeros((c_len - n, hd), F32)], axis=0)
                dq_heads.append(jnp.concatenate(dq_rows, axis=0))
                dk_heads.append(dk_h)
            dq_intra = dqi_u[...] + jnp.concatenate(dq_heads, axis=1)
            dk_intra = dki_u[...] + jnp.concatenate(dk_heads, axis=1)

            db = dqb * qb + qf * dq_intra - kf * dk_intra - dkb2 * kb2
            db = db + jnp.where(t_idx == c_len - 1, dbl, 0.0)
            dg = _tri_sum(_tril_bf16(c_len, upper=True), db)
            dqf = dqb * eb + dq_intra
            dkf = dkb2 * ebl + dk_intra
            dq_ref[rs, :] = (dqf * (sq * (1.0 + q * (1.0 - sq)))).astype(BF16)
            df_ref[rs, :] = (sig_neg * (dg * jnp.exp(log_c - log_f) - dkf * (1.0 - lbv) * sig)).astype(BF16)
            di_ref[rs, :] = dvf.astype(BF16)
            dlb_ref[...] += jnp.sum(sig_neg * (dg * jnp.exp(-log_f) - dkf), axis=0, keepdims=True)

    blk = (cps * c_len, HGRN_WIDTH)
    sblk = (cps, c_len, HGRN_WIDTH)
    rev = lambda c: last - c
    step = lambda k: (lambda: pl.program_id(0) == k)
    e_in, e_out, e_shape, e_scr, e_args = _ride_specs(ride)
    return pl.pallas_call(
        _riding(body, 7, 4, 7, ride, step(0), step(n_steps // 2), step(last)),
        name="hgrn_bwd",
        grid=(n_steps,),
        in_specs=[
            pl.BlockSpec(blk, lambda c: (rev(c), col0)),
            pl.BlockSpec(blk, lambda c: (rev(c), col0 + 1)),
            pl.BlockSpec(blk, lambda c: (rev(c), col0 + 2)),
            pl.BlockSpec((2, HGRN_WIDTH), lambda c: (0, 0)),
            pl.BlockSpec(blk, lambda c: (rev(c), 0)),
            pl.BlockSpec((cps, nh, hd, hd), lambda c: (rev(c), 0, 0, 0)),
            pl.BlockSpec(blk, lambda c: (rev(c), 0)),
        ] + e_in,
        out_specs=[
            pl.BlockSpec(blk, lambda c: (rev(c), 0)),
            pl.BlockSpec(blk, lambda c: (rev(c), 0)),
            pl.BlockSpec(blk, lambda c: (rev(c), 0)),
            pl.BlockSpec((1, HGRN_WIDTH), lambda c: (0, 0)),
        ] + e_out,
        out_shape=[jax.ShapeDtypeStruct((s, HGRN_WIDTH), BF16)] * 3 + [jax.ShapeDtypeStruct((1, HGRN_WIDTH), F32)] + e_shape,
        scratch_shapes=[
            pltpu.VMEM((nh, hd, hd), F32),
            pltpu.VMEM(sblk, F32),
            pltpu.VMEM(sblk, F32),
            pltpu.VMEM(sblk, F32),
            pltpu.VMEM((cps, nh, c_len, hd), F32),
            pltpu.VMEM(sblk, F32),
            pltpu.VMEM(sblk, F32),
        ] + e_scr,
        compiler_params=_params(ride, dimension_semantics=("arbitrary",)),
    )(proj, proj, proj, lb, d_o, states, a_mat, *e_args)


def _row_spec(tm, width, col=0):
    return pl.BlockSpec((tm, width), lambda i: (i, col))


def _const_spec(width):
    return pl.BlockSpec((1, width), lambda i: (0, 0))


def _acc_rows(ref, value):
    @pl.when(pl.program_id(0) == 0)
    def _():
        ref[...] = jnp.zeros_like(ref)

    ref[...] += jnp.sum(value, axis=0, keepdims=True)


def mix_fwd(attn_parts, o_h, proj, an, hn, w_out_b, gp, x, ride=None):
    s = x.shape[0]
    tm = TOKEN_TILE
    gate_col = 3
    hd = HGRN_HEAD_DIM
    nd = len(DILATIONS)

    def body(*refs):
        o_refs, l_refs = refs[:nd], refs[nd:2 * nd]
        oh_ref, gate_ref, an_ref, hn_ref, w_ref, gp_ref, x_ref = refs[2 * nd:2 * nd + 7]
        x1_ref, cat_ref, mixed_ref, attn_ref = refs[2 * nd + 7:2 * nd + 11]
        lse_refs = refs[2 * nd + 11:3 * nd + 11]
        o_scr, l_scr, lse_scr = refs[3 * nd + 11:]
        os_ = [_from_dilated(r, o_scr.at[k], d, tm) for k, (r, d) in enumerate(zip(o_refs, DILATIONS))]
        ls = [_from_dilated(r, l_scr.at[k], d, tm) for k, (r, d) in enumerate(zip(l_refs, DILATIONS))]
        m = jnp.maximum(jnp.maximum(ls[0], ls[1]), ls[2])
        es = [jnp.exp(l - m) for l in ls]
        den = es[0] + es[1] + es[2]
        attn = (es[0] * os_[0] + es[1] * os_[1] + es[2] * os_[2]) / den
        attn_ref[...] = attn
        _lane_blocks(lse_scr, m + jnp.log(den))
        for d, ref in zip(DILATIONS, lse_refs):
            _to_dilated(lse_scr, ref, d, tm)
        cat_ref[:, :ATTN_WIDTH] = _rms_fwd(attn, an_ref[...], ATTN_WIDTH).astype(BF16)
        gate = gate_ref[...]
        silu_g = gate * _sigmoid(gate)
        for h in range(HGRN_HEADS):
            hs = slice(h * hd, (h + 1) * hd)
            rec = _rms_fwd(oh_ref[:, hs], hn_ref[:, hs], hd) * silu_g[:, hs]
            cat_ref[:, ATTN_WIDTH + h * hd:ATTN_WIDTH + (h + 1) * hd] = rec.astype(BF16)
        mixed = _dot(cat_ref[...], w_ref[...])
        mixed_ref[...] = mixed
        x1_ref[...] = x_ref[...] + _rms_fwd(mixed, gp_ref[...], D_MODEL)

    aw = ATTN_WIDTH
    n_steps = s // tm
    step = lambda k: (lambda: pl.program_id(0) == k)
    e_in, e_out, e_shape, e_scr, e_args = _ride_specs(ride)
    return pl.pallas_call(
        _riding(body, 2 * nd + 7, 4 + nd, 3, ride, step(0), step((13 * n_steps) // 16), step(n_steps - 1)),
        name="mix_fwd",
        grid=(n_steps,),
        in_specs=[_dilated_spec(d, tm, aw) for d in DILATIONS] * 2 + [
            _row_spec(tm, aw), _row_spec(tm, aw, gate_col), _const_spec(aw), _const_spec(aw), _vmem_spec(),
            _const_spec(D_MODEL), _row_spec(tm, D_MODEL)] + e_in,
        out_specs=[_row_spec(tm, D_MODEL), _row_spec(tm, D_MODEL), _row_spec(tm, D_MODEL), _row_spec(tm, aw)] + [
            _dilated_spec(d, tm, aw) for d in DILATIONS] + e_out,
        out_shape=[
            jax.ShapeDtypeStruct((s, D_MODEL), F32),
            jax.ShapeDtypeStruct((s, D_MODEL), BF16),
            jax.ShapeDtypeStruct((s, D_MODEL), F32),
            jax.ShapeDtypeStruct((s, aw), F32),
        ] + [jax.ShapeDtypeStruct((d, s // d, aw), F32) for d in DILATIONS] + e_shape,
        scratch_shapes=[pltpu.VMEM((nd, aw // LANES, tm, LANES), F32), pltpu.VMEM((nd, aw // LANES, tm, LANES), F32),
                        pltpu.VMEM((aw // LANES, tm, LANES), F32)] + e_scr,
        compiler_params=_params(ride, dimension_semantics=("arbitrary",)),
    )(*[p[0] for p in attn_parts], *[p[1] for p in attn_parts], o_h, proj, an, hn, w_out_b, gp, x, *e_args)


def mix_bwd(dx1, mixed, gp, w_out_b, attn, an, o_h, proj, hn):
    s = dx1.shape[0]
    tm = TOKEN_TILE
    gate_col = 3
    hd = HGRN_HEAD_DIM
    aw = ATTN_WIDTH

    nd = len(DILATIONS)

    def body(*refs):
        dx1_ref, mixed_ref, gp_ref, w_ref, attn_ref, an_ref, oh_ref, gate_ref, hn_ref, dmix_ref = refs[:10]
        do_refs, delta_refs = refs[10:10 + nd], refs[10 + nd:10 + 2 * nd]
        doh_ref, dgate_ref, dgp_ref, dan_ref, dhn_ref, do_ref, delta_ref = refs[10 + 2 * nd:]
        dmixed, gp_c = _rms_bwd(dx1_ref[...], mixed_ref[...], gp_ref[...], D_MODEL)
        _acc_rows(dgp_ref, gp_c)
        dmixed_b = dmixed.astype(BF16)
        dmix_ref[...] = dmixed_b
        dcat = _dot_nt(dmixed_b, w_ref[...])
        attn = attn_ref[...]
        d_o, an_c = _rms_bwd(dcat[:, :aw], attn, an_ref[...], aw)
        _acc_rows(dan_ref, an_c)
        _lane_blocks(do_ref, d_o)
        prod = d_o * attn
        for pair in range(ATTN_HEADS // 2):
            pp = prod[:, pair * LANES:(pair + 1) * LANES]
            low = _lane_half((tm, LANES), 0)
            lo = jnp.sum(jnp.where(low, pp, 0.0), axis=-1, keepdims=True)
            hi = jnp.sum(jnp.where(low, 0.0, pp), axis=-1, keepdims=True)
            delta_ref[pair] = jnp.where(low, lo, hi)
        for d, o_ref, l_ref in zip(DILATIONS, do_refs, delta_refs):
            _to_dilated(do_ref, o_ref, d, tm, cast=BF16)
            _to_dilated(delta_ref, l_ref, d, tm)
        gate = gate_ref[...]
        sg = _sigmoid(gate)
        silu_g = gate * sg
        drec = dcat[:, aw:]
        hn_parts = []
        for h in range(HGRN_HEADS):
            hs = slice(h * hd, (h + 1) * hd)
            oh = oh_ref[:, hs]
            on = _rms_fwd(oh, hn_ref[:, hs], hd)
            dgate_ref[:, hs] = (drec[:, hs] * on * (sg[:, hs] * (1.0 + gate[:, hs] * (1.0 - sg[:, hs])))).astype(BF16)
            d_oh, hn_c = _rms_bwd(drec[:, hs] * silu_g[:, hs], oh, hn_ref[:, hs], hd)
            doh_ref[:, hs] = d_oh
            hn_parts.append(hn_c)
        _acc_rows(dhn_ref, jnp.concatenate(hn_parts, axis=1))

    return pl.pallas_call(
        body,
        name="mix_bwd",
        grid=(s // tm,),
        in_specs=[_row_spec(tm, D_MODEL), _row_spec(tm, D_MODEL), _const_spec(D_MODEL), _vmem_spec(), _row_spec(tm, aw),
                  _const_spec(aw), _row_spec(tm, aw), _row_spec(tm, aw, gate_col), _const_spec(aw)],
        out_specs=[_row_spec(tm, D_MODEL)] + [_dilated_spec(d, tm, aw) for d in DILATIONS] * 2 + [_row_spec(tm, aw)] * 2 + [
            _const_spec(D_MODEL), _const_spec(aw), _const_spec(aw)],
        out_shape=[jax.ShapeDtypeStruct((s, D_MODEL), BF16)] + [
            jax.ShapeDtypeStruct((d, s // d, aw), BF16) for d in DILATIONS] + [
            jax.ShapeDtypeStruct((d, s // d, aw), F32) for d in DILATIONS] + [
            jax.ShapeDtypeStruct((s, aw), F32), jax.ShapeDtypeStruct((s, aw), BF16),
            jax.ShapeDtypeStruct((1, D_MODEL), F32), jax.ShapeDtypeStruct((1, aw), F32),
            jax.ShapeDtypeStruct((1, aw), F32)],
        scratch_shapes=[pltpu.VMEM((aw // LANES, tm, LANES), F32), pltpu.VMEM((aw // LANES, tm, LANES), F32)],
        compiler_params=_params(dimension_semantics=("arbitrary",)),
    )(dx1, mixed, gp, w_out_b, attn, an, o_h, proj, hn)


def mlp_fwd_bwd(x1, g_pre, w1_blocks, w2_b, g_post, target):
    s = x1.shape[0]
    tm = MLP_TILE
    nblk, _, fb = w1_blocks.shape

    def body(x1_ref, gpre_ref, w1_ref, w2_ref, gpost_ref, t_ref,
             dx1_ref, h2_ref, a_ref, du_ref, dff_ref, loss_ref, dgpre_ref, dgpost_ref, u_ref):
        x1v = x1_ref[...]
        h2 = _rms_fwd(x1v, gpre_ref[...], D_MODEL).astype(BF16)
        h2_ref[...] = h2
        ff = jnp.zeros((tm, D_MODEL), F32)
        for j in range(nblk):
            cols = slice(j * fb, (j + 1) * fb)
            ru = jnp.maximum(_dot(h2, w1_ref[j]), 0.0)
            u_ref[:, cols] = ru.astype(BF16)
            a = (ru * ru).astype(BF16)
            a_ref[:, cols] = a
            ff = ff + _dot(a, w2_ref[cols, :])
        diff = x1v + _rms_fwd(ff, gpost_ref[...], D_MODEL) - t_ref[...]
        _acc_rows(loss_ref, diff * diff)
        dy = diff * (1.0 / D_MODEL)
        dff, gpost_c = _rms_bwd(dy, ff, gpost_ref[...], D_MODEL)
        _acc_rows(dgpost_ref, gpost_c)
        dff_b = dff.astype(BF16)
        dff_ref[...] = dff_b
        dh2 = jnp.zeros((tm, D_MODEL), F32)
        for j in range(nblk):
            cols = slice(j * fb, (j + 1) * fb)
            du = (_dot_nt(dff_b, w2_ref[cols, :]) * (2.0 * u_ref[:, cols])).astype(BF16)
            du_ref[:, cols] = du
            dh2 = dh2 + _dot_nt(du, w1_ref[j])
        dxa, gpre_c = _rms_bwd(dh2, x1v, gpre_ref[...], D_MODEL)
        _acc_rows(dgpre_ref, gpre_c)
        dx1_ref[...] = dy + dxa

    dm = D_MODEL
    return pl.pallas_call(
        body,
        name="mlp_fwd_bwd",
        grid=(s // tm,),
        in_specs=[_row_spec(tm, dm), _const_spec(dm), _vmem_spec(), _vmem_spec(), _const_spec(dm), _row_spec(tm, dm)],
        out_specs=[_row_spec(tm, dm), _row_spec(tm, dm), _row_spec(tm, D_FF), _row_spec(tm, D_FF), _row_spec(tm, dm),
                   _const_spec(dm), _const_spec(dm), _const_spec(dm)],
        out_shape=[
            jax.ShapeDtypeStruct((s, dm), F32),
            jax.ShapeDtypeStruct((s, dm), BF16),
            jax.ShapeDtypeStruct((s, D_FF), BF16),
            jax.ShapeDtypeStruct((s, D_FF), BF16),
            jax.ShapeDtypeStruct((s, dm), BF16),
            jax.ShapeDtypeStruct((1, dm), F32),
            jax.ShapeDtypeStruct((1, dm), F32),
            jax.ShapeDtypeStruct((1, dm), F32),
        ],
        scratch_shapes=[pltpu.VMEM((tm, D_FF), BF16)],
        compiler_params=_params(dimension_semantics=("arbitrary",)),
    )(x1, g_pre, w1_blocks, w2_b, g_post, target)


def in_proj_bwd(attn_grads, hgrn_grads, dgate, w_in_b, x, g1, dx1):
    s = x.shape[0]
    tm = PROJ_TILE
    aw = ATTN_WIDTH
    n_attn = len(attn_grads)
    flat = [g[k] for k in range(3) for g in attn_grads] + list(hgrn_grads) + [dgate]

    def body(*refs):
        parts = refs[:len(flat)]
        w_ref, x_ref, g_ref, dx1_ref, dx_ref, dproj_ref, dg_ref, scr = refs[len(flat):]
        groups = []
        for k in range(3):
            acc = None
            for p, d in zip(parts[k * n_attn:(k + 1) * n_attn], DILATIONS):
                v = _from_dilated(p, scr, d, tm)
                acc = v if acc is None else acc + v
            groups.append(acc)
        groups += [p[...] for p in parts[3 * n_attn:]]
        dh = jnp.zeros((tm, D_MODEL), F32)
        for gi, grp in enumerate(groups):
            cols = slice(gi * aw, (gi + 1) * aw)
            gb = grp.astype(BF16)
            dproj_ref[:, cols] = gb
            dh = dh + _dot_nt(gb, w_ref[:, cols])
        dxa, g_c = _rms_bwd(dh, x_ref[...], g_ref[...], D_MODEL)
        _acc_rows(dg_ref, g_c)
        dx_ref[...] = dx1_ref[...] + dxa

    dm = D_MODEL
    return pl.pallas_call(
        body,
        name="in_proj_bwd",
        grid=(s // tm,),
        in_specs=[_dilated_spec(d, tm, aw) for d in DILATIONS] * 3 + [_row_spec(tm, aw)] * 4 + [
            _vmem_spec(), _row_spec(tm, dm), _const_spec(dm), _row_spec(tm, dm)],
        out_specs=[_row_spec(tm, dm), _row_spec(tm, IN_PROJ_WIDTH), _const_spec(dm)],
        out_shape=[jax.ShapeDtypeStruct((s, dm), F32), jax.ShapeDtypeStruct((s, IN_PROJ_WIDTH), BF16),
                   jax.ShapeDtypeStruct((1, dm), F32)],
        scratch_shapes=[pltpu.VMEM((aw // LANES, tm, LANES), F32)],
        compiler_params=_params(dimension_semantics=("arbitrary",)),
    )(*flat, w_in_b, x, g1, dx1)


def wgrad(a_b, b_b, tn, name, ts=2048, per_step=1, ride=None):
    s, k = a_b.shape
    n = b_b.shape[1]

    def body(a_ref, b_ref, o_ref):
        @pl.when(pl.program_id(1) == 0)
        def _():
            o_ref[...] = jnp.zeros_like(o_ref)

        a = a_ref[...]
        for jj in range(per_step):
            o_ref[jj] += _dot_tn(a, b_ref[:, jj * tn:(jj + 1) * tn])

    wide = tn * per_step
    gn, gs = n // wide, s // ts
    step = lambda j, i: (lambda: (pl.program_id(0) == j) & (pl.program_id(1) == i))
    e_in, e_out, e_shape, e_scr, e_args = _ride_specs(ride)
    out = pl.pallas_call(
        _riding(body, 2, 1, 0, ride, step(0, 0), step(gn // 2, 0), step(gn - 1, gs - 1)),
        name=name,
        grid=(gn, gs),
        in_specs=[pl.BlockSpec((ts, k), lambda j, i: (i, 0)), pl.BlockSpec((ts, wide), lambda j, i: (i, j))] + e_in,
        out_specs=[pl.BlockSpec((per_step, k, tn), lambda j, i: (j, 0, 0))] + e_out,
        out_shape=[jax.ShapeDtypeStruct((n // tn, k, tn), F32)] + e_shape,
        scratch_shapes=e_scr,
        compiler_params=_params(ride, dimension_semantics=("arbitrary", "arbitrary")),
    )(a_b, b_b, *e_args)
    return out[0] if ride is None else out


def train_step(x, target, g1, an, logits, hn, gp, g_pre, g_post, w, m, v):
    nd = len(DILATIONS)
    shard_b = {k: w[k].astype(BF16) for k in BIG}
    (w_in_g,) = run_exchange(gather_exchange([shard_b["w_in"]]), "gather_w_in")
    w_in_b = w_in_g.transpose(1, 0, 2).reshape(D_MODEL, IN_PROJ_WIDTH)

    proj, h_b, *qkvs, w2_g = in_proj_fwd(x, g1, w_in_b, ride=gather_exchange([shard_b["w_ff2"]]))
    w2_b = w2_g.reshape(D_FF, D_MODEL)
    attn_parts = [attn_fwd(qkv, d) for qkv, d in zip(qkvs, DILATIONS)]
    o_h, states, a_mat, w_out_g, w1_blocks = hgrn_fwd(
        proj, logits, ride=gather_exchange([shard_b["w_out"], shard_b["w_ff1"]]))
    w_out_b = w_out_g.reshape(D_MODEL, D_MODEL)
    x1, cat_b, mixed, attn, *lses = mix_fwd(attn_parts, o_h, proj, an, hn, w_out_b, gp, x)
    dx1, h2_b, a_b, du_b, dff_b, loss_vec, dg_pre, dg_post = mlp_fwd_bwd(x1, g_pre, w1_blocks, w2_b, g_post, target)
    dw2 = wgrad(a_b, dff_b, D_MODEL, "wgrad_ff2", ts=512)
    dw1 = wgrad(h2_b, du_b, D_FF // N_DEV, "wgrad_ff1", per_step=2)
    dmix_b, *rest = mix_bwd(dx1, mixed, gp, w_out_b, attn, an, o_h, proj, hn)
    d_os, deltas = rest[:nd], rest[nd:2 * nd]
    d_oh, dgate, dgp, dan, dhn = rest[2 * nd:]
    dwout = wgrad(cat_b, dmix_b, D_MODEL, "wgrad_out")

    early = ("w_out", "w_ff1", "w_ff2")
    early_grads = [dwout.reshape(N_DEV, D_MODEL // N_DEV, D_MODEL), dw1, dw2.reshape(N_DEV, D_FF // N_DEV, D_MODEL)]
    res = attn_bwd(qkvs[0], d_os[0], lses[0], deltas[0], DILATIONS[0], ride=to_core_exchange(early_grads))
    pairs = [pair_sum(g, s, f"pair_sum_{name}") for g, s, name in zip(early_grads, res[3:], early)]
    attn_grads = [res[:3]]
    *res, others_ff2 = attn_bwd(qkvs[1], d_os[1], lses[1], deltas[1], DILATIONS[1],
                                ride=to_chip_exchange([pairs[2][1]]))
    attn_grads.append(res)
    attn_grads.append(attn_bwd(qkvs[2], d_os[2], lses[2], deltas[2], DILATIONS[2]))
    dq_h, df_h, di_h, dlb, *others = hgrn_bwd(proj, logits, d_oh, states, a_mat,
                                              ride=to_chip_exchange([pairs[0][1], pairs[1][1]]))
    others.append(others_ff2)
    dx, dproj_b, dg1 = in_proj_bwd(attn_grads, (dq_h, df_h, di_h), dgate, w_in_b, x, g1, dx1)
    packed = _pack_small(dg1, dgp, dg_pre, dg_post, dan, dhn, dlb, loss_vec)
    dwin, small_slots = wgrad(h_b, dproj_b, 2 * IN_PROJ_WIDTH // N_DEV, "wgrad_in",
                              ride=small_exchange(packed))
    big = {name: sum_adamw(p[0], o, w[name], m[name], v[name], f"sum_adamw_{name}")
           for name, p, o in zip(early, pairs, others)}

    shard_w = IN_PROJ_WIDTH // N_DEV
    dwin_blocks = dwin.reshape(N_DEV // 2, D_MODEL, 2, shard_w).transpose(0, 2, 1, 3).reshape(N_DEV, D_MODEL, shard_w)
    (from_sibling,) = run_exchange(to_core_exchange([dwin_blocks.astype(BF16)]), "reduce_w_in_to_core")
    pair_in, pair_in_b = pair_sum(dwin_blocks, from_sibling, "pair_sum_w_in")
    (others_in,) = run_exchange(to_chip_exchange([pair_in_b]), "reduce_w_in_to_chip")
    big["w_in"] = sum_adamw(pair_in, others_in, w["w_in"], m["w_in"], v["w_in"], "sum_adamw_w_in")
    return dx, big, small_slots


def _position():
    x, y, c = lax.axis_index("x"), lax.axis_index("y"), lax.axis_index("c")
    other_chips = [(1 - x, y), (x, 1 - y), (1 - x, 1 - y)]
    return x, y, c, other_chips


def _any_spec():
    return pl.BlockSpec(memory_space=pl.ANY)


class Exchange:
    def __init__(self, arrays, out_shape, sems, stages, collective_id, peers):
        self.arrays, self.out_shape, self.sems, self.stages = list(arrays), list(out_shape), list(sems), stages
        self.collective_id, self.peers = collective_id, peers

    def open(self):
        barrier = pltpu.get_barrier_semaphore()
        peers = self.peers()
        for peer in peers:
            pl.semaphore_signal(barrier, inc=1, device_id=peer, device_id_type=MESH)
        pl.semaphore_wait(barrier, len(peers))


def _siblings():
    x, y, c, _ = _position()
    return [(x, y, 1 - c)]


def _same_core_of_other_chips():
    x, y, c, chips = _position()
    return [(px, py, c) for px, py in chips]


def _gather_peers():
    return _siblings() + _same_core_of_other_chips()


def _all_others():
    x, y, c, _ = _position()
    return [(1 - x if rel & 4 else x, 1 - y if rel & 2 else y, 1 - c if rel & 1 else c) for rel in range(1, N_DEV)]


def gather_exchange(shards):
    n = len(shards)

    def stages(ins, outs, sems):
        send_sems, recv_sems, local_sems = sems

        def parts():
            x, y, c, chips = _position()
            me, sibling = (x, y, c), (x, y, 1 - c)

            def slot(a, px, py, pc):
                return outs[a].at[4 * px + 2 * py + pc]

            def copy(a, k, block, to, src=None):
                return pltpu.make_async_remote_copy(
                    src_ref=slot(a, *block) if src is None else src, dst_ref=slot(a, *block),
                    send_sem=send_sems.at[a, k], recv_sem=recv_sems.at[a, k], device_id=to, device_id_type=MESH)

            local = [pltpu.make_async_copy(ins[a], slot(a, *me), local_sems.at[a]) for a in range(n)]
            first = []
            for a in range(n):
                first.append(copy(a, 0, me, sibling, src=ins[a]))
                first += [copy(a, 1 + j, me, (*chip, c), src=ins[a]) for j, chip in enumerate(chips)]
            passed = [copy(a, 4 + j, (*chip, c), sibling) for j, chip in enumerate(chips) for a in range(n)]
            return c, chips, me, sibling, copy, local, first, passed

        def begin():
            _, _, _, _, _, local, first, _ = parts()
            for cp in local + first:
                cp.start()

        def middle():
            c, chips, me, _, copy, _, _, passed = parts()
            k = 0
            for j, chip in enumerate(chips):
                for a in range(n):
                    copy(a, 1 + j, (*chip, c), me).wait_recv()
                    passed[k].start()
                    k += 1

        def end():
            c, chips, me, sibling, copy, local, first, passed = parts()
            for a in range(n):
                copy(a, 0, sibling, me).wait_recv()
                for j, chip in enumerate(chips):
                    copy(a, 4 + j, (*chip, 1 - c), me).wait_recv()
            for cp in first + passed:
                cp.wait_send()
            for cp in local:
                cp.wait()

        return begin, middle, end

    return Exchange(
        shards, [jax.ShapeDtypeStruct((N_DEV,) + sh.shape, sh.dtype) for sh in shards],
        [pltpu.SemaphoreType.DMA((n, 7)), pltpu.SemaphoreType.DMA((n, 7)), pltpu.SemaphoreType.DMA((n,))], stages,
        collective_id=0, peers=_gather_peers)


def to_core_exchange(grads):
    n = len(grads)

    def stages(ins, outs, sems):
        send_sems, recv_sems = sems

        def copies():
            x, y, c, _ = _position()
            return [pltpu.make_async_remote_copy(
                src_ref=ins[a].at[2 * q + (1 - c)], dst_ref=outs[a].at[q], send_sem=send_sems.at[a, q],
                recv_sem=recv_sems.at[a, q], device_id=(x, y, 1 - c), device_id_type=MESH)
                for a in range(n) for q in range(4)]

        def begin():
            for cp in copies():
                cp.start()

        def end():
            for cp in copies():
                cp.wait()

        return begin, None, end

    return Exchange(grads, [jax.ShapeDtypeStruct((4,) + g.shape[1:], g.dtype) for g in grads],
                    [pltpu.SemaphoreType.DMA((n, 4)), pltpu.SemaphoreType.DMA((n, 4))], stages,
                    collective_id=1, peers=_siblings)


def pair_sum(grad, from_sibling, name):
    _, r, cdim = grad.shape
    tr = min(r, ELEMENTWISE_ROWS)
    c_idx = lax.axis_index("c").astype(jnp.int32).reshape(1)

    def body(c_ref, g_ref, s_ref, o_ref, ob_ref):
        total = g_ref[...] + s_ref[...]
        o_ref[...] = total
        ob_ref[...] = total.astype(BF16)

    blk = lambda: pl.BlockSpec((1, tr, cdim), lambda q, i, cr: (q, i, 0))
    return pl.pallas_call(
        body,
        name=name,
        grid_spec=pltpu.PrefetchScalarGridSpec(
            num_scalar_prefetch=1,
            grid=(4, r // tr),
            in_specs=[pl.BlockSpec((1, tr, cdim), lambda q, i, cr: (2 * q + cr[0], i, 0)), blk()],
            out_specs=[blk(), blk()],
        ),
        out_shape=[jax.ShapeDtypeStruct((4, r, cdim), F32), jax.ShapeDtypeStruct((4, r, cdim), BF16)],
        compiler_params=_params(dimension_semantics=("arbitrary", "arbitrary")),
    )(c_idx, grad, from_sibling)


def to_chip_exchange(pairs):
    n = len(pairs)

    def stages(ins, outs, sems):
        send_sems, recv_sems = sems

        def copies():
            x, y, c, chips = _position()
            return [pltpu.make_async_remote_copy(
                src_ref=ins[a].at[2 * px + py], dst_ref=outs[a].at[j], send_sem=send_sems.at[a, j],
                recv_sem=recv_sems.at[a, j], device_id=(px, py, c), device_id_type=MESH)
                for a in range(n) for j, (px, py) in enumerate(chips)]

        def begin():
            for cp in copies():
                cp.start()

        def end():
            for cp in copies():
                cp.wait()

        return begin, None, end

    return Exchange(pairs, [jax.ShapeDtypeStruct((3,) + p.shape[1:], p.dtype) for p in pairs],
                    [pltpu.SemaphoreType.DMA((n, 3)), pltpu.SemaphoreType.DMA((n, 3))], stages,
                    collective_id=2, peers=_same_core_of_other_chips)


def run_exchange(ex, name):
    n_in, n_out = len(ex.arrays), len(ex.out_shape)

    def body(*refs):
        begin, middle, end = ex.stages(refs[:n_in], refs[n_in:n_in + n_out], refs[n_in + n_out:])
        ex.open()
        begin()
        if middle is not None:
            middle()
        end()

    return pl.pallas_call(
        body,
        name=name,
        in_specs=[_any_spec()] * n_in,
        out_specs=[_any_spec()] * n_out,
        out_shape=ex.out_shape,
        scratch_shapes=ex.sems,
        compiler_params=pltpu.CompilerParams(collective_id=ex.collective_id),
    )(*ex.arrays)


def _riding(body, n_in, n_out, n_scratch, ex, first, middle, last):
    if ex is None:
        return body
    r_in, r_out = len(ex.arrays), len(ex.out_shape)

    def wrapped(*refs):
        k_in, refs = refs[:n_in], refs[n_in:]
        e_in, refs = refs[:r_in], refs[r_in:]
        k_out, refs = refs[:n_out], refs[n_out:]
        e_out, refs = refs[:r_out], refs[r_out:]
        k_scr, e_sems = refs[:n_scratch], refs[n_scratch:]
        begin, mid, end = ex.stages(e_in, e_out, e_sems)

        @pl.when(first())
        def _():
            ex.open()
            begin()

        body(*k_in, *k_out, *k_scr)
        if mid is not None:
            pl.when(middle())(mid)
        pl.when(last())(end)

    return wrapped


def _ride_specs(ex):
    if ex is None:
        return [], [], [], [], []
    return [_any_spec()] * len(ex.arrays), [_any_spec()] * len(ex.out_shape), ex.out_shape, ex.sems, ex.arrays


def _adamw(w, g, m, v):
    m = ADAM_B1 * m + (1.0 - ADAM_B1) * g
    v = ADAM_B2 * v + (1.0 - ADAM_B2) * (g * g)
    m_hat = m / (1.0 - ADAM_B1 ** ADAM_STEP)
    v_hat = v / (1.0 - ADAM_B2 ** ADAM_STEP)
    delta = -ADAM_LR * (m_hat / (jnp.sqrt(v_hat) + ADAM_EPS) + ADAM_WD * w)
    return delta, m, v


def sum_adamw(pairs, others, w, m, v, name):
    r, cdim = w.shape
    tr = min(r, ELEMENTWISE_ROWS // 2)
    chip_idx =(2 * lax.axis_index("x") + lax.axis_index("y")).astype(jnp.int32).reshape(1)

    def body(q_ref, p_ref, o_ref, w_ref, m_ref, v_ref, g_out, d_out, m_out, v_out):
        g = p_ref[0] + o_ref[0].astype(F32) + o_ref[1].astype(F32) + o_ref[2].astype(F32)
        g_out[...] = g
        d_out[...], m_out[...], v_out[...] = _adamw(w_ref[...], g, m_ref[...], v_ref[...])

    tile = lambda: pl.BlockSpec((tr, cdim), lambda i, qr: (i, 0))
    return pl.pallas_call(
        body,
        name=name,
        grid_spec=pltpu.PrefetchScalarGridSpec(
            num_scalar_prefetch=1,
            grid=(r // tr,),
            in_specs=[pl.BlockSpec((1, tr, cdim), lambda i, qr: (qr[0], i, 0)),
                      pl.BlockSpec((3, tr, cdim), lambda i, qr: (0, i, 0)), tile(), tile(), tile()],
            out_specs=[tile(), tile(), tile(), tile()],
        ),
        out_shape=[jax.ShapeDtypeStruct((r, cdim), F32)] * 4,
        compiler_params=_params(dimension_semantics=("arbitrary",)),
    )(chip_idx, pairs, others, w, m, v)


def small_exchange(packed):
    def stages(ins, outs, sems):
        send_sems, recv_sems, local_sem = sems
        (src,), (slots,) = ins, outs

        def copies():
            x, y, c, _ = _position()
            my_id = 4 * x + 2 * y + c
            sends, landings = [], []
            for rel in range(1, N_DEV):
                px = 1 - x if (rel >> 2) & 1 else x
                py = 1 - y if (rel >> 1) & 1 else y
                pc = 1 - c if rel & 1 else c
                peer = dict(send_sem=send_sems.at[rel - 1], recv_sem=recv_sems.at[rel - 1], device_id=(px, py, pc),
                            device_id_type=MESH)
                sends.append(pltpu.make_async_remote_copy(src_ref=src, dst_ref=slots.at[my_id], **peer))
                landings.append(pltpu.make_async_remote_copy(src_ref=src, dst_ref=slots.at[4 * px + 2 * py + pc], **peer))
            return pltpu.make_async_copy(src, slots.at[my_id], local_sem), sends, landings

        def begin():
            local, sends, _ = copies()
            local.start()
            for cp in sends:
                cp.start()

        def end():
            local, sends, landings = copies()
            for cp in landings:
                cp.wait_recv()
            for cp in sends:
                cp.wait_send()
            local.wait()

        return begin, None, end

    return Exchange([packed], [jax.ShapeDtypeStruct((N_DEV,) + packed.shape, packed.dtype)],
                    [pltpu.SemaphoreType.DMA((N_DEV - 1,)), pltpu.SemaphoreType.DMA((N_DEV - 1,)),
                     pltpu.SemaphoreType.DMA(())], stages, collective_id=3, peers=_all_others)


def small_adamw(slots, w, m, v):
    def body(r_ref, w_ref, m_ref, v_ref, g_out, d_out, m_out, v_out, loss_out):
        red = r_ref[0]
        for k in range(1, N_DEV):
            red = red + r_ref[k]
        wv = w_ref[...]
        lb = _lower_bound(jnp.concatenate([wv[5:6, :HGRN_WIDTH], wv[5:6, HGRN_WIDTH:]], axis=0))
        t = red[5:6, :HGRN_WIDTH] * lb * (1.0 - lb)
        row = lax.broadcasted_iota(jnp.int32, red.shape, 0)
        g = jnp.where(row == 5, jnp.concatenate([t, -t], axis=1), jnp.where(row >= 6, 0.0, red))
        g_out[...] = g
        d_out[...], m_out[...], v_out[...] = _adamw(wv, g, m_ref[...], v_ref[...])
        loss = jnp.sum(red[6:7, :], axis=-1, keepdims=True) * (0.5 / D_MODEL)
        loss_out[...] = jnp.broadcast_to(loss, loss_out.shape)

    return pl.pallas_call(
        body,
        name="small_adamw",
        in_specs=[_vmem_spec()] * 4,
        out_specs=[_vmem_spec()] * 5,
        out_shape=[jax.ShapeDtypeStruct(w.shape, F32)] * 4 + [jax.ShapeDtypeStruct((SUBLANES, LANES), F32)],
    )(slots, w, m, v)


def _pack_small(g1, gp, g_pre, g_post, an, hn, logits_or_dlb, extra=None):
    row5 = logits_or_dlb.reshape(1, -1)
    row5 = jnp.pad(row5, ((0, 0), (0, D_MODEL - row5.shape[1])))
    row6 = jnp.zeros((1, D_MODEL), F32) if extra is None else extra
    return jnp.concatenate([g1, gp, g_pre, g_post, jnp.concatenate([an, hn], axis=1), row5, row6,
                            jnp.zeros((1, D_MODEL), F32)], axis=0)


def _unpack_small(p):
    return dict(mix_pre_norm=p[0:1], mix_post_norm=p[1:2], mlp_pre_norm=p[2:3], mlp_post_norm=p[3:4],
                attn_out_norm=p[4:5, :ATTN_WIDTH], hgrn_out_norm=p[4:5, ATTN_WIDTH:],
                hgrn_lb_logits=p[5].reshape(2, HGRN_WIDTH))


BIG = ("w_in", "w_out", "w_ff1", "w_ff2")
ORDER = ("mix_pre_norm", "w_in", "attn_out_norm", "hgrn_lb_logits", "hgrn_out_norm", "w_out", "mix_post_norm",
         "mlp_pre_norm", "w_ff1", "w_ff2", "mlp_post_norm")


def kernel(x, mix_pre_norm, w_in, attn_out_norm, hgrn_lb_logits, hgrn_out_norm, w_out, mix_post_norm, mlp_pre_norm, w_ff1, w_ff2, mlp_post_norm, loss_target, m_mix_pre_norm, m_w_in, m_attn_out_norm, m_hgrn_lb_logits, m_hgrn_out_norm, m_w_out, m_mix_post_norm, m_mlp_pre_norm, m_w_ff1, m_w_ff2, m_mlp_post_norm, v_mix_pre_norm, v_w_in, v_attn_out_norm, v_hgrn_lb_logits, v_hgrn_out_norm, v_w_out, v_mix_post_norm, v_mlp_pre_norm, v_w_ff1, v_w_ff2, v_mlp_post_norm):
    w = dict(w_in=w_in[0], w_out=w_out[0], w_ff1=w_ff1[0], w_ff2=w_ff2[0])
    m = dict(w_in=m_w_in[0], w_out=m_w_out[0], w_ff1=m_w_ff1[0], w_ff2=m_w_ff2[0])
    v = dict(w_in=v_w_in[0], w_out=v_w_out[0], w_ff1=v_w_ff1[0], w_ff2=v_w_ff2[0])

    dx, big, small_slots = train_step(x[0], loss_target[0], mix_pre_norm, attn_out_norm, hgrn_lb_logits, hgrn_out_norm,
                                      mix_post_norm, mlp_pre_norm, mlp_post_norm, w, m, v)

    pack = lambda a, b, c2, d, e, f, g: _pack_small(a, b, c2, d, e, f, g)
    w_s = pack(mix_pre_norm, mix_post_norm, mlp_pre_norm, mlp_post_norm, attn_out_norm, hgrn_out_norm, hgrn_lb_logits)
    m_s = pack(m_mix_pre_norm, m_mix_post_norm, m_mlp_pre_norm, m_mlp_post_norm, m_attn_out_norm, m_hgrn_out_norm,
               m_hgrn_lb_logits)
    v_s = pack(v_mix_pre_norm, v_mix_post_norm, v_mlp_pre_norm, v_mlp_post_norm, v_attn_out_norm, v_hgrn_out_norm,
               v_hgrn_lb_logits)
    g_s, d_s, nm_s, nv_s, loss = small_adamw(small_slots, w_s, m_s, v_s)
    small_out = [_unpack_small(t) for t in (g_s, d_s, nm_s, nv_s)]

    outs = [loss[0, 0], dx[None]]
    for kind in range(4):
        for name in ORDER:
            outs.append(big[name][kind][None] if name in BIG else small_out[kind][name])
    return tuple(outs)
```

```python
import jax
import jax.numpy as jnp
from jax import lax
from jax.experimental import pallas as pl
from jax.experimental.pallas import tpu as pltpu

F32 = jnp.float32
BF16 = jnp.bfloat16

D_MODEL = 1024
ATTN_WIDTH = 512
ATTN_HEAD_DIM = 64
ATTN_HEADS = 8
ATTN_BLOCK = 128
DILATIONS = (1, 4, 16)
HGRN_WIDTH = 512
HGRN_HEADS = 4
HGRN_HEAD_DIM = 128
HGRN_CHUNK = 64
IN_PROJ_WIDTH = 3584
D_FF = 4096
RMS_EPS = 1e-6
N_DEV = 8
ADAM_LR = 0.001
ADAM_B1 = 0.9
ADAM_B2 = 0.999
ADAM_EPS = 1e-08
ADAM_WD = 0.01
ADAM_STEP = 10

SUBLANES = 8
LANES = 128
COLUMN_UNROLL = 8
HGRN_CHUNKS_PER_STEP = 2
SUB_BLOCK = 16
TOKEN_TILE = 512
ELEMENTWISE_ROWS = 1024
MLP_TILE = 256
PROJ_TILE = 512
VMEM_BYTES_V7X = 64 * 1024 * 1024
VMEM_LIMIT = VMEM_BYTES_V7X // 8 * 7
NEG_BIG = -1e30
MESH = pl.DeviceIdType.MESH


def _params(ride=None, **kw):
    if ride is not None:
        kw["collective_id"] = ride.collective_id
    return pltpu.CompilerParams(vmem_limit_bytes=VMEM_LIMIT, **kw)


def _vmem_spec():
    return pl.BlockSpec(memory_space=pltpu.VMEM)


def _dot(a, b):
    return jnp.dot(a, b, preferred_element_type=F32)


def _dot_nt(a, b):
    return lax.dot_general(a, b, (((1,), (1,)), ((), ())), preferred_element_type=F32)


def _dot_tn(a, b):
    return lax.dot_general(a, b, (((0,), (0,)), ((), ())), preferred_element_type=F32)


def _sigmoid(x):
    return 1.0 / (1.0 + jnp.exp(-x))


def _rms_fwd(x, gain, width):
    r = lax.rsqrt(jnp.sum(x * x, axis=-1, keepdims=True) * (1.0 / width) + RMS_EPS)
    return x * r * gain


def _rms_bwd(dy, x, gain, width):
    r = lax.rsqrt(jnp.sum(x * x, axis=-1, keepdims=True) * (1.0 / width) + RMS_EPS)
    xhat = x * r
    dxhat = dy * gain
    dx = r * (dxhat - xhat * (jnp.sum(dxhat * xhat, axis=-1, keepdims=True) * (1.0 / width)))
    return dx, dy * xhat


def _split3(x):
    hi = x.astype(BF16)
    r1 = x - hi.astype(F32)
    mid = r1.astype(BF16)
    lo = (r1 - mid.astype(F32)).astype(BF16)
    return hi, mid, lo


def _tri_sum(tri_bf16, x):
    hi, mid, lo = _split3(x)
    return _dot(tri_bf16, hi) + _dot(tri_bf16, mid) + _dot(tri_bf16, lo)


def _dilated_spec(d, tm, width):
    return pl.BlockSpec((d, tm // d, width), lambda i: (0, i, 0))


def _lane_blocks(ref, value):
    for c in range(ref.shape[0]):
        ref[c] = value[:, c * LANES:(c + 1) * LANES]


def _to_dilated(src_ref, dst_ref, d, tm, cast=None):
    for r in range(d):
        for c in range(src_ref.shape[0]):
            v = src_ref[c] if d == 1 else src_ref[c, pl.ds(r, tm // d, stride=d), :]
            dst_ref[r, :, c * LANES:(c + 1) * LANES] = v if cast is None else v.astype(cast)


def _from_dilated(src_ref, scratch_ref, d, tm):
    if d == 1:
        return src_ref[0].astype(F32)
    nblk = scratch_ref.shape[0]
    for r in range(d):
        for c in range(nblk):
            scratch_ref[c, pl.ds(r, tm // d, stride=d), :] = src_ref[r, :, c * LANES:(c + 1) * LANES].astype(F32)
    return jnp.concatenate([scratch_ref[c] for c in range(nblk)], axis=1)


def in_proj_fwd(x, g1, w_in_b, ride=None):
    s = x.shape[0]
    tm = PROJ_TILE
    qkv_w = 3 * ATTN_WIDTH
    hg_w = IN_PROJ_WIDTH - qkv_w

    def body(x_ref, g_ref, w_ref, hg_ref, h_ref, *rest):
        qkv_refs, qkv_scr = rest[:len(DILATIONS)], rest[len(DILATIONS)]
        h = _rms_fwd(x_ref[...], g_ref[...], D_MODEL).astype(BF16)
        h_ref[...] = h
        proj = _dot(h, w_ref[...])
        hg_ref[...] = proj[:, qkv_w:]
        _lane_blocks(qkv_scr, proj[:, :qkv_w])
        for d, ref in zip(DILATIONS, qkv_refs):
            _to_dilated(qkv_scr, ref, d, tm, cast=BF16)

    n_steps = s // tm
    step = lambda k: (lambda: pl.program_id(0) == k)
    e_in, e_out, e_shape, e_scr, e_args = _ride_specs(ride)
    return pl.pallas_call(
        _riding(body, 3, 2 + len(DILATIONS), 1, ride, step(0), step(n_steps - 2), step(n_steps - 1)),
        name="in_proj_fwd",
        grid=(n_steps,),
        in_specs=[
            pl.BlockSpec((tm, D_MODEL), lambda i: (i, 0)),
            pl.BlockSpec((1, D_MODEL), lambda i: (0, 0)),
            _vmem_spec(),
        ] + e_in,
        out_specs=[
            pl.BlockSpec((tm, hg_w), lambda i: (i, 0)),
            pl.BlockSpec((tm, D_MODEL), lambda i: (i, 0)),
        ] + [_dilated_spec(d, tm, qkv_w) for d in DILATIONS] + e_out,
        out_shape=[jax.ShapeDtypeStruct((s, hg_w), F32), jax.ShapeDtypeStruct((s, D_MODEL), BF16)] + [
            jax.ShapeDtypeStruct((d, s // d, qkv_w), BF16) for d in DILATIONS] + e_shape,
        scratch_shapes=[pltpu.VMEM((qkv_w // LANES, tm, LANES), F32)] + e_scr,
        compiler_params=_params(ride, dimension_semantics=("arbitrary",)),
    )(x, g1, w_in_b, *e_args)


ATTN_SCALE = ATTN_HEAD_DIM ** -0.5


def _fill_attn_bias(bias_ref, dilation):
    qi = lax.broadcasted_iota(jnp.int32, (ATTN_BLOCK, 2 * ATTN_BLOCK), 0)
    kj = lax.broadcasted_iota(jnp.int32, (ATTN_BLOCK, 2 * ATTN_BLOCK), 1)
    dist = qi + ATTN_BLOCK - kj
    valid = (dist >= 0) & (dist <= ATTN_BLOCK)
    for head in range(ATTN_HEADS):
        slope = 2.0 ** (-8.0 * (head + 1) / ATTN_HEADS)
        bias = jnp.where(valid, dist.astype(F32) * (-slope * dilation), NEG_BIG)
        bias_ref[0, head] = bias
        bias_ref[1, head] = jnp.where(kj >= ATTN_BLOCK, bias, NEG_BIG)


def _stack_heads(x):
    low = _lane_half(x.shape, 0)
    zero = jnp.zeros_like(x)
    return jnp.concatenate([jnp.where(low, x, zero), jnp.where(low, zero, x)], axis=0)


def _unstack_heads(y):
    half = y.shape[0] // 2
    return jnp.where(_lane_half((half, y.shape[1]), 0), y[:half], y[half:])


def _attn_scores(q_stack, kcat, bias_ref, pair, first_block):
    f = first_block.astype(jnp.int32)
    bias = jnp.concatenate([bias_ref[f, 2 * pair], bias_ref[f, 2 * pair + 1]], axis=0)
    return _dot_nt(q_stack, kcat) + bias


def _lane_half(shape, sub):
    lane = lax.broadcasted_iota(jnp.int32, shape, 1)
    return (lane < ATTN_HEAD_DIM) if sub == 0 else (lane >= ATTN_HEAD_DIM)


def _sub_block(col, row):
    return pl.BlockSpec((None, ATTN_BLOCK, ATTN_WIDTH), lambda r, n: (r, row(n), col))


def attn_fwd(qkv, dilation):
    d, length, _ = qkv.shape
    assert d == dilation
    nb = length // ATTN_BLOCK

    def body(q_ref, kc_ref, kp_ref, vc_ref, vp_ref, o_ref, lse_ref, bias_ref):
        @pl.when((pl.program_id(0) == 0) & (pl.program_id(1) == 0))
        def _():
            _fill_attn_bias(bias_ref, d)

        first = pl.program_id(1) == 0
        for pair in range(ATTN_HEADS // 2):
            lanes = slice(pair * LANES, (pair + 1) * LANES)
            q_stack = _stack_heads(q_ref[:, lanes] * ATTN_SCALE)
            kcat = jnp.concatenate([kp_ref[:, lanes], kc_ref[:, lanes]], axis=0)
            vcat = jnp.concatenate([vp_ref[:, lanes], vc_ref[:, lanes]], axis=0)
            sc = _attn_scores(q_stack, kcat, bias_ref, pair, first)
            m = jnp.max(sc, axis=-1, keepdims=True)
            p = jnp.exp(sc - m)
            den = jnp.sum(p, axis=-1, keepdims=True)
            o_ref[:, lanes] = _unstack_heads(_dot(p.astype(BF16), vcat) / den).astype(BF16)
            lse_ref[:, lanes] = _unstack_heads(jnp.broadcast_to(m + jnp.log(den), (2 * ATTN_BLOCK, LANES)))

    cur = lambda n: n
    prev = lambda n: jnp.maximum(n - 1, 0)
    return pl.pallas_call(
        body,
        name=f"attn_fwd_d{d}",
        grid=(d, nb),
        in_specs=[_sub_block(0, cur), _sub_block(1, cur), _sub_block(1, prev), _sub_block(2, cur), _sub_block(2, prev)],
        out_specs=[_sub_block(0, cur), _sub_block(0, cur)],
        out_shape=[jax.ShapeDtypeStruct((d, length, ATTN_WIDTH), BF16), jax.ShapeDtypeStruct((d, length, ATTN_WIDTH), F32)],
        scratch_shapes=[pltpu.VMEM((2, ATTN_HEADS, ATTN_BLOCK, 2 * ATTN_BLOCK), F32)],
        compiler_params=_params(dimension_semantics=("arbitrary", "arbitrary")),
    )(qkv, qkv, qkv, qkv, qkv)


def attn_bwd(qkv, d_out, lse, delta, dilation, ride=None):
    d, length, _ = qkv.shape
    assert d == dilation
    nb = length // ATTN_BLOCK

    steps = d * nb + 1

    def body(q_ref, kc_ref, kp_ref, vc_ref, vp_ref, do_ref, lse_ref, dl_ref, dq_ref, dk_ref, dv_ref, ck_ref, cv_ref,
             bias_ref):
        t = pl.program_id(0)

        @pl.when(t == 0)
        def _():
            ck_ref[...] = jnp.zeros_like(ck_ref)
            cv_ref[...] = jnp.zeros_like(cv_ref)
            _fill_attn_bias(bias_ref, d)

        @pl.when(t < steps - 1)
        def _():
            first = t % nb == 0
            for pair in range(ATTN_HEADS // 2):
                lanes = slice(pair * LANES, (pair + 1) * LANES)
                q_stack = _stack_heads(q_ref[:, lanes] * ATTN_SCALE)
                do_stack = _stack_heads(do_ref[:, lanes])
                kcat = jnp.concatenate([kp_ref[:, lanes], kc_ref[:, lanes]], axis=0)
                vcat = jnp.concatenate([vp_ref[:, lanes], vc_ref[:, lanes]], axis=0)
                col_a, col_b = 2 * pair, 2 * pair + 1
                lse_col = jnp.concatenate([lse_ref[:, col_a:col_a + 1], lse_ref[:, col_b:col_b + 1]], axis=0)
                dl_col = jnp.concatenate([dl_ref[:, col_a:col_a + 1], dl_ref[:, col_b:col_b + 1]], axis=0)
                p = jnp.exp(_attn_scores(q_stack, kcat, bias_ref, pair, first) - lse_col)
                ds = (p * (_dot_nt(do_stack, vcat) - dl_col)).astype(BF16)
                dq_ref[:, lanes] = (_unstack_heads(_dot(ds, kcat)) * ATTN_SCALE).astype(BF16)
                dk_cat = _dot_tn(ds, q_stack)
                dv_cat = _dot_tn(p.astype(BF16), do_stack)
                dk_ref[:, lanes] = (ck_ref[:, lanes] + dk_cat[:ATTN_BLOCK]).astype(BF16)
                dv_ref[:, lanes] = (cv_ref[:, lanes] + dv_cat[:ATTN_BLOCK]).astype(BF16)
                ck_ref[:, lanes] = dk_cat[ATTN_BLOCK:]
                cv_ref[:, lanes] = dv_cat[ATTN_BLOCK:]

        @pl.when(t == steps - 1)
        def _():
            dk_ref[...] = ck_ref[...].astype(BF16)
            dv_ref[...] = cv_ref[...].astype(BF16)

    blk = (ATTN_BLOCK, ATTN_WIDTH)

    def spec(col, shift, width=ATTN_WIDTH):
        def index(t):
            f = jnp.minimum(t, steps - 2) if shift > -2 else jnp.maximum(t - 1, 0)
            r, n = f // nb, f % nb
            return (r, jnp.maximum(n - 1, 0) if shift == -1 else n, col)
        return pl.BlockSpec((None, ATTN_BLOCK, width), index)

    step = lambda k: (lambda: pl.program_id(0) == k)
    e_in, e_out, e_shape, e_scr, e_args = _ride_specs(ride)
    return pl.pallas_call(
        _riding(body, 8, 3, 3, ride, step(0), step(steps // 2), step(steps - 1)),
        name=f"attn_bwd_d{d}",
        grid=(steps,),
        in_specs=[spec(0, 0), spec(1, 0), spec(1, -1), spec(2, 0), spec(2, -1), spec(0, 0), spec(0, 0, LANES),
                  spec(0, 0, LANES)] + e_in,
        out_specs=[spec(0, 0), spec(0, -2), spec(0, -2)] + e_out,
        out_shape=[jax.ShapeDtypeStruct((d, length, ATTN_WIDTH), BF16)] * 3 + e_shape,
        scratch_shapes=[pltpu.VMEM(blk, F32), pltpu.VMEM(blk, F32),
                        pltpu.VMEM((2, ATTN_HEADS, ATTN_BLOCK, 2 * ATTN_BLOCK), F32)] + e_scr,
        compiler_params=_params(ride, dimension_semantics=("arbitrary",)),
    )(qkv, qkv, qkv, qkv, qkv, d_out, lse, delta, *e_args)


def _lower_bound(logits):
    return _sigmoid(logits[0:1, :] - logits[1:2, :])


def _hgrn_gates(q, fp, lb):
    sq = _sigmoid(q)
    qf = q * sq
    sig = _sigmoid(fp)
    sig_neg = _sigmoid(-fp)
    kf = (1.0 - lb) * sig_neg
    log_sig = jnp.minimum(fp, 0.0) - jnp.log(1.0 + jnp.exp(-jnp.abs(fp)))
    a = jnp.log(lb)
    c = jnp.log(1.0 - lb) + log_sig
    log_f = jnp.maximum(a, c) + jnp.log(1.0 + jnp.exp(-jnp.abs(a - c)))
    return sq, qf, (sig, sig_neg, c), log_f, kf


def _tril_bf16(n, upper=False):
    r = lax.broadcasted_iota(jnp.int32, (n, n), 0)
    c = lax.broadcasted_iota(jnp.int32, (n, n), 1)
    keep = (c >= r) if upper else (c <= r)
    return jnp.where(keep, 1.0, 0.0).astype(BF16)


def _hgrn_diagonal_loops(c_len, diagonal):
    for half in range(SUB_BLOCK // SUBLANES):
        def step(jj, carry, half=half):
            j = half * SUBLANES + jj
            for i in range(c_len // SUB_BLOCK):
                diagonal(slice(i * SUB_BLOCK + half * SUBLANES, (i + 1) * SUB_BLOCK), j, i * SUB_BLOCK + j)
            return carry

        lax.fori_loop(0, SUBLANES, step, 0, unroll=COLUMN_UNROLL)


def _hgrn_off_diagonal(b, qf, kf):
    c_len, width = b.shape
    edges = [b[0:1, :]] + [b[i * SUB_BLOCK - 1:i * SUB_BLOCK, :] for i in range(1, c_len // SUB_BLOCK)]
    eq = jnp.exp(b - jnp.concatenate([jnp.broadcast_to(e, (SUB_BLOCK, width)) for e in edges], axis=0))
    q_til = qf * eq
    k_til, ek = [], []
    for i in range(1, c_len // SUB_BLOCK):
        n = i * SUB_BLOCK
        e = jnp.exp(edges[i] - b[:n, :])
        ek.append(e)
        k_til.append(jnp.concatenate([kf[:n, :] * e, jnp.zeros((2 * c_len - n, width), F32)], axis=0))
    return q_til, k_til, eq, ek


def _split2(x):
    hi = x.astype(BF16)
    return hi, (x - hi.astype(F32)).astype(BF16)


def hgrn_fwd(proj, lb, ride=None):
    s = proj.shape[0]
    c_len, nh, hd = HGRN_CHUNK, HGRN_HEADS, HGRN_HEAD_DIM
    n_chunks = s // c_len
    col0 = 0

    cps = 2 * HGRN_CHUNKS_PER_STEP
    n_steps = n_chunks // cps

    def body(q_ref, f_ref, i_ref, lb_ref, o_ref, st_out_ref, a_out_ref, st_ref, b_ref, qf_ref, kf_ref, a_ref):
        @pl.when(pl.program_id(0) == 0)
        def _():
            st_ref[...] = jnp.zeros_like(st_ref)

        lbv = _lower_bound(lb_ref[...])
        for u in range(cps):
            rs = slice(u * c_len, (u + 1) * c_len)
            b_u, qf_u, kf_u, a_u = b_ref.at[u], qf_ref.at[u], kf_ref.at[u], a_ref.at[u]
            _, qf, _, log_f, kf = _hgrn_gates(q_ref[rs, :], f_ref[rs, :], lbv)
            b = _tri_sum(_tril_bf16(c_len), log_f)
            b_u[...] = b
            qf_u[...] = qf
            kf_u[...] = kf
            a_u[...] = jnp.zeros_like(a_u)

            def diagonal(rows, j, key, b_u=b_u, qf_u=qf_u, kf_u=kf_u, a_u=a_u):
                bj = b_u[pl.ds(key, 1), :]
                kj = kf_u[pl.ds(key, 1), :]
                nrow = rows.stop - rows.start
                t_loc = lax.broadcasted_iota(jnp.int32, (nrow, nh * hd), 0) + (rows.start % SUB_BLOCK)
                e = jnp.exp(jnp.where(t_loc >= j, b_u[rows, :] - bj, NEG_BIG))
                prod = qf_u[rows, :] * kj * e
                lane = lax.broadcasted_iota(jnp.int32, (nrow, hd), 1)
                for h in range(nh):
                    col = jnp.sum(prod[:, h * hd:(h + 1) * hd], axis=-1, keepdims=True)
                    a_u[h, rows, :] = jnp.where(lane == key, col, a_u[h, rows, :])

            _hgrn_diagonal_loops(c_len, diagonal)
            q_til, k_til, _, _ = _hgrn_off_diagonal(b, qf, kf)
            q_til = q_til.astype(BF16)
            k_til = [k.astype(BF16) for k in k_til]

            b_last = b[c_len - 1:c_len, :]
            qb = (qf * jnp.exp(b)).astype(BF16)
            kb2 = (kf * jnp.exp(b_last - b)).astype(BF16)
            vf = i_ref[rs, :].astype(BF16)
            for h in range(nh):
                hs = slice(h * hd, (h + 1) * hd)
                st = st_ref[h]
                st_out_ref[u, h] = st
                off = [jnp.zeros((SUB_BLOCK, hd), F32)]
                for i in range(1, c_len // SUB_BLOCK):
                    off.append(_dot_nt(q_til[i * SUB_BLOCK:(i + 1) * SUB_BLOCK, hs], k_til[i - 1][:, hs]))
                a_h = a_u[h] + jnp.concatenate(off, axis=0)
                a_out_ref[rs, hs] = a_h
                o_ref[rs, hs] = _dot_nt(qb[:, hs], st.astype(BF16)) + _dot(a_h[:, :c_len].astype(BF16), vf[:, hs])
                st_ref[h] = st * jnp.exp(b_last[:, hs]) + _dot_tn(vf[:, hs], kb2[:, hs])

    blk = (cps * c_len, HGRN_WIDTH)
    sblk = (cps, c_len, HGRN_WIDTH)
    step = lambda k: (lambda: pl.program_id(0) == k)
    e_in, e_out, e_shape, e_scr, e_args = _ride_specs(ride)
    return pl.pallas_call(
        _riding(body, 4, 3, 5, ride, step(0), step((7 * n_steps) // 8), step(n_steps - 1)),
        name="hgrn_fwd",
        grid=(n_steps,),
        in_specs=[
            pl.BlockSpec(blk, lambda c: (c, col0)),
            pl.BlockSpec(blk, lambda c: (c, col0 + 1)),
            pl.BlockSpec(blk, lambda c: (c, col0 + 2)),
            pl.BlockSpec((2, HGRN_WIDTH), lambda c: (0, 0)),
        ] + e_in,
        out_specs=[
            pl.BlockSpec(blk, lambda c: (c, 0)),
            pl.BlockSpec((cps, nh, hd, hd), lambda c: (c, 0, 0, 0)),
            pl.BlockSpec(blk, lambda c: (c, 0)),
        ] + e_out,
        out_shape=[
            jax.ShapeDtypeStruct((s, HGRN_WIDTH), F32),
            jax.ShapeDtypeStruct((n_chunks, nh, hd, hd), F32),
            jax.ShapeDtypeStruct((s, nh * hd), F32),
        ] + e_shape,
        scratch_shapes=[
            pltpu.VMEM((nh, hd, hd), F32),
            pltpu.VMEM(sblk, F32),
            pltpu.VMEM(sblk, F32),
            pltpu.VMEM(sblk, F32),
            pltpu.VMEM((cps, nh, c_len, hd), F32),
        ] + e_scr,
        compiler_params=_params(ride, dimension_semantics=("arbitrary",)),
    )(proj, proj, proj, lb, *e_args)


def hgrn_bwd(proj, lb, d_o, states, a_mat, ride=None):
    s = proj.shape[0]
    c_len, nh, hd = HGRN_CHUNK, HGRN_HEADS, HGRN_HEAD_DIM
    n_chunks = s // c_len
    col0 = 0
    cps = HGRN_CHUNKS_PER_STEP
    n_steps = n_chunks // cps
    last = n_steps - 1

    def body(q_ref, f_ref, i_ref, lb_ref, do_ref, st_in_ref, a_in_ref, dq_ref, df_ref, di_ref, dlb_ref,
             dst_ref, b_ref, qf_ref, kf_ref, da_ref, dqi_ref, dki_ref):
        @pl.when(pl.program_id(0) == 0)
        def _():
            dst_ref[...] = jnp.zeros_like(dst_ref)
            dlb_ref[...] = jnp.zeros_like(dlb_ref)

        lbv = _lower_bound(lb_ref[...])
        for u in reversed(range(cps)):
            rs = slice(u * c_len, (u + 1) * c_len)
            b_u, qf_u, kf_u, da_u, dqi_u, dki_u = (b_ref.at[u], qf_ref.at[u], kf_ref.at[u], da_ref.at[u], dqi_ref.at[u],
                                                   dki_ref.at[u])
            q = q_ref[rs, :]
            sq, qf, (sig, sig_neg, log_c), log_f, kf = _hgrn_gates(q, f_ref[rs, :], lbv)
            b = _tri_sum(_tril_bf16(c_len), log_f)
            b_u[...] = b
            qf_u[...] = qf
            kf_u[...] = kf
            b_last = b[c_len - 1:c_len, :]
            eb = jnp.exp(b)
            ebl = jnp.exp(b_last - b)
            qb = qf * eb
            kb2 = kf * ebl
            vf = i_ref[rs, :]
            d_o = do_ref[rs, :]
            qb_b, kb2_b, vf_b, do_b = qb.astype(BF16), kb2.astype(BF16), vf.astype(BF16), d_o.astype(BF16)
            tq = lax.broadcasted_iota(jnp.int32, (c_len, hd), 0)
            lane = lax.broadcasted_iota(jnp.int32, (c_len, hd), 1)

            dqb_parts, dvf_parts, dkb2_parts, dbl_parts = [], [], [], []
            for h in range(nh):
                hs = slice(h * hd, (h + 1) * hd)
                st = st_in_ref[u, h]
                dst = dst_ref[h]
                st_b, dst_b = st.astype(BF16), dst.astype(BF16)
                a_h = a_in_ref[rs, hs][:, :c_len].astype(BF16)
                dqb_parts.append(_dot(do_b[:, hs], st_b))
                dvf_parts.append(_dot_tn(a_h, do_b[:, hs]) + _dot_nt(kb2_b[:, hs], dst_b))
                dkb2_parts.append(_dot(vf_b[:, hs], dst_b))
                da = _dot_nt(do_b[:, hs], vf_b[:, hs])
                da = jnp.concatenate([da, jnp.zeros((c_len, hd - c_len), F32)], axis=1)
                da_u[h] = jnp.where(tq >= lane, da, 0.0)
                dbl_parts.append(jnp.sum(dst * st, axis=0, keepdims=True) * jnp.exp(b_last[:, hs]))
                dst_ref[h] = dst * jnp.exp(b_last[:, hs]) + _dot_tn(do_b[:, hs], qb_b[:, hs])
            dqb = jnp.concatenate(dqb_parts, axis=1)
            dvf = jnp.concatenate(dvf_parts, axis=1)
            dkb2 = jnp.concatenate(dkb2_parts, axis=1)
            dbl = jnp.concatenate(dbl_parts, axis=1) + jnp.sum(dkb2 * kb2, axis=0, keepdims=True)

            dqi_u[...] = jnp.zeros_like(dqi_u)
            t_idx = lax.broadcasted_iota(jnp.int32, (c_len, nh * hd), 0)

            def diagonal(rows, j, key, b_u=b_u, qf_u=qf_u, kf_u=kf_u, da_u=da_u, dqi_u=dqi_u, dki_u=dki_u):
                bj = b_u[pl.ds(key, 1), :]
                kj = kf_u[pl.ds(key, 1), :]
                nrow = rows.stop - rows.start
                t_loc = lax.broadcasted_iota(jnp.int32, (nrow, nh * hd), 0) + (rows.start % SUB_BLOCK)
                e = jnp.exp(jnp.where(t_loc >= j, b_u[rows, :] - bj, NEG_BIG))
                lane_r = lax.broadcasted_iota(jnp.int32, (nrow, hd), 1)
                cols = [jnp.sum(jnp.where(lane_r == key, da_u[h, rows, :], 0.0), axis=-1, keepdims=True)
                        for h in range(nh)]
                w = e * jnp.concatenate([jnp.broadcast_to(cc, (nrow, hd)) for cc in cols], axis=1)
                dqi_u[rows, :] += w * kj
                dki_u[pl.ds(key, 1), :] = jnp.sum(w * qf_u[rows, :], axis=0, keepdims=True)

            _hgrn_diagonal_loops(c_len, diagonal)

            q_til, k_til, eq, ek = _hgrn_off_diagonal(b, qf, kf)
            q_hi, q_lo = _split2(q_til)
            k_pairs = [_split2(k) for k in k_til]
            n_sub = c_len // SUB_BLOCK
            dq_heads, dk_heads = [], []
            for h in range(nh):
                hs = slice(h * hd, (h + 1) * hd)
                dq_rows = [jnp.zeros((SUB_BLOCK, hd), F32)]
                dk_h = jnp.zeros((c_len, hd), F32)
                for i in range(1, n_sub):
                    rows = slice(i * SUB_BLOCK, (i + 1) * SUB_BLOCK)
                    n = i * SUB_BLOCK
                    da_i = da_u[h, rows, :].astype(BF16)
                    k_hi, k_lo = k_pairs[i - 1]
                    dq_rows.append((_dot(da_i, k_hi[:, hs]) + _dot(da_i, k_lo[:, hs])) * eq[rows, hs])
                    dk_t = (_dot_tn(da_i, q_hi[rows, hs]) + _dot_tn(da_i, q_lo[rows, hs]))[:n, :] * ek[i - 1][:, hs]
                    dk_h = dk_h + jnp.concatenate([dk_t, jnp.zeros((c_len - n, hd), F32)], axis=0)
                dq_heads.append(jnp.concatenate(dq_rows, axis=0))
                dk_heads.append(dk_h)
            dq_intra = dqi_u[...] + jnp.concatenate(dq_heads, axis=1)
            dk_intra = dki_u[...] + jnp.concatenate(dk_heads, axis=1)

            db = dqb * qb + qf * dq_intra - kf * dk_intra - dkb2 * kb2
            db = db + jnp.where(t_idx == c_len - 1, dbl, 0.0)
            dg = _tri_sum(_tril_bf16(c_len, upper=True), db)
            dqf = dqb * eb + dq_intra
            dkf = dkb2 * ebl + dk_intra
            dq_ref[rs, :] = (dqf * (sq * (1.0 + q * (1.0 - sq)))).astype(BF16)
            df_ref[rs, :] = (sig_neg * (dg * jnp.exp(log_c - log_f) - dkf * (1.0 - lbv) * sig)).astype(BF16)
            di_ref[rs, :] = dvf.astype(BF16)
            dlb_ref[...] += jnp.sum(sig_neg * (dg * jnp.exp(-log_f) - dkf), axis=0, keepdims=True)

    blk = (cps * c_len, HGRN_WIDTH)
    sblk = (cps, c_len, HGRN_WIDTH)
    rev = lambda c: last - c
    step = lambda k: (lambda: pl.program_id(0) == k)
    e_in, e_out, e_shape, e_scr, e_args = _ride_specs(ride)
    return pl.pallas_call(
        _riding(body, 7, 4, 7, ride, step(0), step(n_steps // 2), step(last)),
        name="hgrn_bwd",
        grid=(n_steps,),
        in_specs=[
            pl.BlockSpec(blk, lambda c: (rev(c), col0)),
            pl.BlockSpec(blk, lambda c: (rev(c), col0 + 1)),
            pl.BlockSpec(blk, lambda c: (rev(c), col0 + 2)),
            pl.BlockSpec((2, HGRN_WIDTH), lambda c: (0, 0)),
            pl.BlockSpec(blk, lambda c: (rev(c), 0)),
            pl.BlockSpec((cps, nh, hd, hd), lambda c: (rev(c), 0, 0, 0)),
            pl.BlockSpec(blk, lambda c: (rev(c), 0)),
        ] + e_in,
        out_specs=[
            pl.BlockSpec(blk, lambda c: (rev(c), 0)),
            pl.BlockSpec(blk, lambda c: (rev(c), 0)),
            pl.BlockSpec(blk, lambda c: (rev(c), 0)),
            pl.BlockSpec((1, HGRN_WIDTH), lambda c: (0, 0)),
        ] + e_out,
        out_shape=[jax.ShapeDtypeStruct((s, HGRN_WIDTH), BF16)] * 3 + [jax.ShapeDtypeStruct((1, HGRN_WIDTH), F32)] + e_shape,
        scratch_shapes=[
            pltpu.VMEM((nh, hd, hd), F32),
            pltpu.VMEM(sblk, F32),
            pltpu.VMEM(sblk, F32),
            pltpu.VMEM(sblk, F32),
            pltpu.VMEM((cps, nh, c_len, hd), F32),
            pltpu.VMEM(sblk, F32),
            pltpu.VMEM(sblk, F32),
        ] + e_scr,
        compiler_params=_params(ride, dimension_semantics=("arbitrary",)),
    )(proj, proj, proj, lb, d_o, states, a_mat, *e_args)


def _per_head_lanes(x):
    lane = lax.broadcasted_iota(jnp.int32, (x.shape[0], LANES), 1)
    out = jnp.zeros((x.shape[0], LANES), F32)
    for h in range(ATTN_HEADS):
        out = jnp.where(lane == h, x[:, h * ATTN_HEAD_DIM:h * ATTN_HEAD_DIM + 1], out)
    return out


def _row_spec(tm, width, col=0):
    return pl.BlockSpec((tm, width), lambda i: (i, col))


def _const_spec(width):
    return pl.BlockSpec((1, width), lambda i: (0, 0))


def _acc_rows(ref, value):
    @pl.when(pl.program_id(0) == 0)
    def _():
        ref[...] = jnp.zeros_like(ref)

    ref[...] += jnp.sum(value, axis=0, keepdims=True)


def mix_fwd(attn_parts, o_h, proj, an, hn, w_out_b, gp, x, ride=None):
    s = x.shape[0]
    tm = TOKEN_TILE
    gate_col = 3
    hd = HGRN_HEAD_DIM
    nd = len(DILATIONS)

    def body(*refs):
        o_refs, l_refs = refs[:nd], refs[nd:2 * nd]
        oh_ref, gate_ref, an_ref, hn_ref, w_ref, gp_ref, x_ref = refs[2 * nd:2 * nd + 7]
        x1_ref, cat_ref, mixed_ref, attn_ref = refs[2 * nd + 7:2 * nd + 11]
        lse_refs = refs[2 * nd + 11:3 * nd + 11]
        o_scr, l_scr, lse_scr = refs[3 * nd + 11:]
        os_ = [_from_dilated(r, o_scr.at[k], d, tm) for k, (r, d) in enumerate(zip(o_refs, DILATIONS))]
        ls = [_from_dilated(r, l_scr.at[k], d, tm) for k, (r, d) in enumerate(zip(l_refs, DILATIONS))]
        m = jnp.maximum(jnp.maximum(ls[0], ls[1]), ls[2])
        es = [jnp.exp(l - m) for l in ls]
        den = es[0] + es[1] + es[2]
        attn = (es[0] * os_[0] + es[1] * os_[1] + es[2] * os_[2]) / den
        attn_ref[...] = attn
        lse_scr[0] = _per_head_lanes(m + jnp.log(den))
        for d, ref in zip(DILATIONS, lse_refs):
            _to_dilated(lse_scr, ref, d, tm)
        cat_ref[:, :ATTN_WIDTH] = _rms_fwd(attn, an_ref[...], ATTN_WIDTH).astype(BF16)
        gate = gate_ref[...]
        silu_g = gate * _sigmoid(gate)
        for h in range(HGRN_HEADS):
            hs = slice(h * hd, (h + 1) * hd)
            rec = _rms_fwd(oh_ref[:, hs], hn_ref[:, hs], hd) * silu_g[:, hs]
            cat_ref[:, ATTN_WIDTH + h * hd:ATTN_WIDTH + (h + 1) * hd] = rec.astype(BF16)
        mixed = _dot(cat_ref[...], w_ref[...])
        mixed_ref[...] = mixed
        x1_ref[...] = x_ref[...] + _rms_fwd(mixed, gp_ref[...], D_MODEL)

    aw = ATTN_WIDTH
    n_steps = s // tm
    step = lambda k: (lambda: pl.program_id(0) == k)
    e_in, e_out, e_shape, e_scr, e_args = _ride_specs(ride)
    return pl.pallas_call(
        _riding(body, 2 * nd + 7, 4 + nd, 3, ride, step(0), step((13 * n_steps) // 16), step(n_steps - 1)),
        name="mix_fwd",
        grid=(n_steps,),
        in_specs=[_dilated_spec(d, tm, aw) for d in DILATIONS] * 2 + [
            _row_spec(tm, aw), _row_spec(tm, aw, gate_col), _const_spec(aw), _const_spec(aw), _vmem_spec(),
            _const_spec(D_MODEL), _row_spec(tm, D_MODEL)] + e_in,
        out_specs=[_row_spec(tm, D_MODEL), _row_spec(tm, D_MODEL), _row_spec(tm, D_MODEL), _row_spec(tm, aw)] + [
            _dilated_spec(d, tm, LANES) for d in DILATIONS] + e_out,
        out_shape=[
            jax.ShapeDtypeStruct((s, D_MODEL), F32),
            jax.ShapeDtypeStruct((s, D_MODEL), BF16),
            jax.ShapeDtypeStruct((s, D_MODEL), F32),
            jax.ShapeDtypeStruct((s, aw), F32),
        ] + [jax.ShapeDtypeStruct((d, s // d, LANES), F32) for d in DILATIONS] + e_shape,
        scratch_shapes=[pltpu.VMEM((nd, aw // LANES, tm, LANES), F32), pltpu.VMEM((nd, aw // LANES, tm, LANES), F32),
                        pltpu.VMEM((1, tm, LANES), F32)] + e_scr,
        compiler_params=_params(ride, dimension_semantics=("arbitrary",)),
    )(*[p[0] for p in attn_parts], *[p[1] for p in attn_parts], o_h, proj, an, hn, w_out_b, gp, x, *e_args)


def mix_bwd(dx1, mixed, gp, w_out_b, attn, an, o_h, proj, hn):
    s = dx1.shape[0]
    tm = TOKEN_TILE
    gate_col = 3
    hd = HGRN_HEAD_DIM
    aw = ATTN_WIDTH

    nd = len(DILATIONS)

    def body(*refs):
        dx1_ref, mixed_ref, gp_ref, w_ref, attn_ref, an_ref, oh_ref, gate_ref, hn_ref, dmix_ref = refs[:10]
        do_refs, delta_refs = refs[10:10 + nd], refs[10 + nd:10 + 2 * nd]
        doh_ref, dgate_ref, dgp_ref, dan_ref, dhn_ref, do_ref, delta_ref = refs[10 + 2 * nd:]
        dmixed, gp_c = _rms_bwd(dx1_ref[...], mixed_ref[...], gp_ref[...], D_MODEL)
        _acc_rows(dgp_ref, gp_c)
        dmixed_b = dmixed.astype(BF16)
        dmix_ref[...] = dmixed_b
        dcat = _dot_nt(dmixed_b, w_ref[...])
        attn = attn_ref[...]
        d_o, an_c = _rms_bwd(dcat[:, :aw], attn, an_ref[...], aw)
        _acc_rows(dan_ref, an_c)
        _lane_blocks(do_ref, d_o)
        prod = d_o * attn
        lane = lax.broadcasted_iota(jnp.int32, (tm, LANES), 1)
        delta = jnp.zeros((tm, LANES), F32)
        for pair in range(ATTN_HEADS // 2):
            pp = prod[:, pair * LANES:(pair + 1) * LANES]
            low = _lane_half((tm, LANES), 0)
            lo = jnp.sum(jnp.where(low, pp, 0.0), axis=-1, keepdims=True)
            hi = jnp.sum(jnp.where(low, 0.0, pp), axis=-1, keepdims=True)
            delta = jnp.where(lane == 2 * pair, lo, jnp.where(lane == 2 * pair + 1, hi, delta))
        delta_ref[0] = delta
        for d, o_ref, l_ref in zip(DILATIONS, do_refs, delta_refs):
            _to_dilated(do_ref, o_ref, d, tm, cast=BF16)
            _to_dilated(delta_ref, l_ref, d, tm)
        gate = gate_ref[...]
        sg = _sigmoid(gate)
        silu_g = gate * sg
        drec = dcat[:, aw:]
        hn_parts = []
        for h in range(HGRN_HEADS):
            hs = slice(h * hd, (h + 1) * hd)
            oh = oh_ref[:, hs]
            on = _rms_fwd(oh, hn_ref[:, hs], hd)
            dgate_ref[:, hs] = (drec[:, hs] * on * (sg[:, hs] * (1.0 + gate[:, hs] * (1.0 - sg[:, hs])))).astype(BF16)
            d_oh, hn_c = _rms_bwd(drec[:, hs] * silu_g[:, hs], oh, hn_ref[:, hs], hd)
            doh_ref[:, hs] = d_oh
            hn_parts.append(hn_c)
        _acc_rows(dhn_ref, jnp.concatenate(hn_parts, axis=1))

    return pl.pallas_call(
        body,
        name="mix_bwd",
        grid=(s // tm,),
        in_specs=[_row_spec(tm, D_MODEL), _row_spec(tm, D_MODEL), _const_spec(D_MODEL), _vmem_spec(), _row_spec(tm, aw),
                  _const_spec(aw), _row_spec(tm, aw), _row_spec(tm, aw, gate_col), _const_spec(aw)],
        out_specs=[_row_spec(tm, D_MODEL)] + [_dilated_spec(d, tm, aw) for d in DILATIONS] + [
            _dilated_spec(d, tm, LANES) for d in DILATIONS] + [_row_spec(tm, aw)] * 2 + [
            _const_spec(D_MODEL), _const_spec(aw), _const_spec(aw)],
        out_shape=[jax.ShapeDtypeStruct((s, D_MODEL), BF16)] + [
            jax.ShapeDtypeStruct((d, s // d, aw), BF16) for d in DILATIONS] + [
            jax.ShapeDtypeStruct((d, s // d, LANES), F32) for d in DILATIONS] + [
            jax.ShapeDtypeStruct((s, aw), F32), jax.ShapeDtypeStruct((s, aw), BF16),
            jax.ShapeDtypeStruct((1, D_MODEL), F32), jax.ShapeDtypeStruct((1, aw), F32),
            jax.ShapeDtypeStruct((1, aw), F32)],
        scratch_shapes=[pltpu.VMEM((aw // LANES, tm, LANES), F32), pltpu.VMEM((1, tm, LANES), F32)],
        compiler_params=_params(dimension_semantics=("arbitrary",)),
    )(dx1, mixed, gp, w_out_b, attn, an, o_h, proj, hn)


def mlp_fwd_bwd(x1, g_pre, w1_blocks, w2_b, g_post, target):
    s = x1.shape[0]
    tm = MLP_TILE
    nblk, _, fb = w1_blocks.shape

    def body(x1_ref, gpre_ref, w1_ref, w2_ref, gpost_ref, t_ref,
             dx1_ref, h2_ref, a_ref, du_ref, dff_ref, loss_ref, dgpre_ref, dgpost_ref, u_ref):
        x1v = x1_ref[...]
        h2 = _rms_fwd(x1v, gpre_ref[...], D_MODEL).astype(BF16)
        h2_ref[...] = h2
        ff = jnp.zeros((tm, D_MODEL), F32)
        for j in range(nblk):
            cols = slice(j * fb, (j + 1) * fb)
            ru = jnp.maximum(_dot(h2, w1_ref[j]), 0.0)
            u_ref[:, cols] = ru.astype(BF16)
            a = (ru * ru).astype(BF16)
            a_ref[:, cols] = a
            ff = ff + _dot(a, w2_ref[cols, :])
        diff = x1v + _rms_fwd(ff, gpost_ref[...], D_MODEL) - t_ref[...]
        _acc_rows(loss_ref, diff * diff)
        dy = diff * (1.0 / D_MODEL)
        dff, gpost_c = _rms_bwd(dy, ff, gpost_ref[...], D_MODEL)
        _acc_rows(dgpost_ref, gpost_c)
        dff_b = dff.astype(BF16)
        dff_ref[...] = dff_b
        dh2 = jnp.zeros((tm, D_MODEL), F32)
        for j in range(nblk):
            cols = slice(j * fb, (j + 1) * fb)
            du = (_dot_nt(dff_b, w2_ref[cols, :]) * (2.0 * u_ref[:, cols])).astype(BF16)
            du_ref[:, cols] = du
            dh2 = dh2 + _dot_nt(du, w1_ref[j])
        dxa, gpre_c = _rms_bwd(dh2, x1v, gpre_ref[...], D_MODEL)
        _acc_rows(dgpre_ref, gpre_c)
        dx1_ref[...] = dy + dxa

    dm = D_MODEL
    return pl.pallas_call(
        body,
        name="mlp_fwd_bwd",
        grid=(s // tm,),
        in_specs=[_row_spec(tm, dm), _const_spec(dm), _vmem_spec(), _vmem_spec(), _const_spec(dm), _row_spec(tm, dm)],
        out_specs=[_row_spec(tm, dm), _row_spec(tm, dm), _row_spec(tm, D_FF), _row_spec(tm, D_FF), _row_spec(tm, dm),
                   _const_spec(dm), _const_spec(dm), _const_spec(dm)],
        out_shape=[
            jax.ShapeDtypeStruct((s, dm), F32),
            jax.ShapeDtypeStruct((s, dm), BF16),
            jax.ShapeDtypeStruct((s, D_FF), BF16),
            jax.ShapeDtypeStruct((s, D_FF), BF16),
            jax.ShapeDtypeStruct((s, dm), BF16),
            jax.ShapeDtypeStruct((1, dm), F32),
            jax.ShapeDtypeStruct((1, dm), F32),
            jax.ShapeDtypeStruct((1, dm), F32),
        ],
        scratch_shapes=[pltpu.VMEM((tm, D_FF), BF16)],
        compiler_params=_params(dimension_semantics=("arbitrary",)),
    )(x1, g_pre, w1_blocks, w2_b, g_post, target)


def in_proj_bwd(attn_grads, hgrn_grads, dgate, w_in_b, x, g1, dx1):
    s = x.shape[0]
    tm = PROJ_TILE
    aw = ATTN_WIDTH
    n_attn = len(attn_grads)
    flat = [g[k] for k in range(3) for g in attn_grads] + list(hgrn_grads) + [dgate]

    def body(*refs):
        parts = refs[:len(flat)]
        w_ref, x_ref, g_ref, dx1_ref, dx_ref, dproj_ref, dg_ref, scr = refs[len(flat):]
        groups = []
        for k in range(3):
            acc = None
            for p, d in zip(parts[k * n_attn:(k + 1) * n_attn], DILATIONS):
                v = _from_dilated(p, scr, d, tm)
                acc = v if acc is None else acc + v
            groups.append(acc)
        groups += [p[...] for p in parts[3 * n_attn:]]
        dh = jnp.zeros((tm, D_MODEL), F32)
        for gi, grp in enumerate(groups):
            cols = slice(gi * aw, (gi + 1) * aw)
            gb = grp.astype(BF16)
            dproj_ref[:, cols] = gb
            dh = dh + _dot_nt(gb, w_ref[:, cols])
        dxa, g_c = _rms_bwd(dh, x_ref[...], g_ref[...], D_MODEL)
        _acc_rows(dg_ref, g_c)
        dx_ref[...] = dx1_ref[...] + dxa

    dm = D_MODEL
    return pl.pallas_call(
        body,
        name="in_proj_bwd",
        grid=(s // tm,),
        in_specs=[_dilated_spec(d, tm, aw) for d in DILATIONS] * 3 + [_row_spec(tm, aw)] * 4 + [
            _vmem_spec(), _row_spec(tm, dm), _const_spec(dm), _row_spec(tm, dm)],
        out_specs=[_row_spec(tm, dm), _row_spec(tm, IN_PROJ_WIDTH), _const_spec(dm)],
        out_shape=[jax.ShapeDtypeStruct((s, dm), F32), jax.ShapeDtypeStruct((s, IN_PROJ_WIDTH), BF16),
                   jax.ShapeDtypeStruct((1, dm), F32)],
        scratch_shapes=[pltpu.VMEM((aw // LANES, tm, LANES), F32)],
        compiler_params=_params(dimension_semantics=("arbitrary",)),
    )(*flat, w_in_b, x, g1, dx1)


def wgrad(a_b, b_b, tn, name, ts=2048, per_step=1, ride=None):
    s, k = a_b.shape
    n = b_b.shape[1]

    def body(a_ref, b_ref, o_ref):
        @pl.when(pl.program_id(1) == 0)
        def _():
            o_ref[...] = jnp.zeros_like(o_ref)

        a = a_ref[...]
        for jj in range(per_step):
            o_ref[jj] += _dot_tn(a, b_ref[:, jj * tn:(jj + 1) * tn])

    wide = tn * per_step
    gn, gs = n // wide, s // ts
    step = lambda j, i: (lambda: (pl.program_id(0) == j) & (pl.program_id(1) == i))
    e_in, e_out, e_shape, e_scr, e_args = _ride_specs(ride)
    out = pl.pallas_call(
        _riding(body, 2, 1, 0, ride, step(0, 0), step(gn // 2, 0), step(gn - 1, gs - 1)),
        name=name,
        grid=(gn, gs),
        in_specs=[pl.BlockSpec((ts, k), lambda j, i: (i, 0)), pl.BlockSpec((ts, wide), lambda j, i: (i, j))] + e_in,
        out_specs=[pl.BlockSpec((per_step, k, tn), lambda j, i: (j, 0, 0))] + e_out,
        out_shape=[jax.ShapeDtypeStruct((n // tn, k, tn), F32)] + e_shape,
        scratch_shapes=e_scr,
        compiler_params=_params(ride, dimension_semantics=("arbitrary", "arbitrary")),
    )(a_b, b_b, *e_args)
    return out[0] if ride is None else out


def train_step(x, target, g1, an, logits, hn, gp, g_pre, g_post, w, m, v):
    nd = len(DILATIONS)
    shard_b = {k: w[k].astype(BF16) for k in BIG}
    (w_in_g,) = run_exchange(gather_exchange([shard_b["w_in"]]), "gather_w_in")
    w_in_b = w_in_g.transpose(1, 0, 2).reshape(D_MODEL, IN_PROJ_WIDTH)

    proj, h_b, *qkvs, w2_g = in_proj_fwd(x, g1, w_in_b, ride=gather_exchange([shard_b["w_ff2"]]))
    w2_b = w2_g.reshape(D_FF, D_MODEL)
    attn_parts = [attn_fwd(qkv, d) for qkv, d in zip(qkvs, DILATIONS)]
    o_h, states, a_mat, w_out_g, w1_blocks = hgrn_fwd(
        proj, logits, ride=gather_exchange([shard_b["w_out"], shard_b["w_ff1"]]))
    w_out_b = w_out_g.reshape(D_MODEL, D_MODEL)
    x1, cat_b, mixed, attn, *lses = mix_fwd(attn_parts, o_h, proj, an, hn, w_out_b, gp, x)
    dx1, h2_b, a_b, du_b, dff_b, loss_vec, dg_pre, dg_post = mlp_fwd_bwd(x1, g_pre, w1_blocks, w2_b, g_post, target)
    dw2 = wgrad(a_b, dff_b, D_MODEL, "wgrad_ff2", ts=512)
    dw1 = wgrad(h2_b, du_b, D_FF // N_DEV, "wgrad_ff1", per_step=2)
    dmix_b, *rest = mix_bwd(dx1, mixed, gp, w_out_b, attn, an, o_h, proj, hn)
    d_os, deltas = rest[:nd], rest[nd:2 * nd]
    d_oh, dgate, dgp, dan, dhn = rest[2 * nd:]
    dwout = wgrad(cat_b, dmix_b, D_MODEL, "wgrad_out")

    early = ("w_out", "w_ff1", "w_ff2")
    early_grads = [dwout.reshape(N_DEV, D_MODEL // N_DEV, D_MODEL), dw1, dw2.reshape(N_DEV, D_FF // N_DEV, D_MODEL)]
    res = attn_bwd(qkvs[0], d_os[0], lses[0], deltas[0], DILATIONS[0], ride=to_core_exchange(early_grads))
    pairs = [pair_sum(g, s, f"pair_sum_{name}") for g, s, name in zip(early_grads, res[3:], early)]
    attn_grads = [res[:3]]
    *res, others_ff2 = attn_bwd(qkvs[1], d_os[1], lses[1], deltas[1], DILATIONS[1],
                                ride=to_chip_exchange([pairs[2][1]]))
    attn_grads.append(res)
    attn_grads.append(attn_bwd(qkvs[2], d_os[2], lses[2], deltas[2], DILATIONS[2]))
    dq_h, df_h, di_h, dlb, *others = hgrn_bwd(proj, logits, d_oh, states, a_mat,
                                              ride=to_chip_exchange([pairs[0][1], pairs[1][1]]))
    others.append(others_ff2)
    dx, dproj_b, dg1 = in_proj_bwd(attn_grads, (dq_h, df_h, di_h), dgate, w_in_b, x, g1, dx1)
    packed = _pack_small(dg1, dgp, dg_pre, dg_post, dan, dhn, dlb, loss_vec)
    dwin, small_slots = wgrad(h_b, dproj_b, 2 * IN_PROJ_WIDTH // N_DEV, "wgrad_in",
                              ride=small_exchange(packed))
    big = {name: sum_adamw(p[0], o, w[name], m[name], v[name], f"sum_adamw_{name}")
           for name, p, o in zip(early, pairs, others)}

    shard_w = IN_PROJ_WIDTH // N_DEV
    dwin_blocks = dwin.reshape(N_DEV // 2, D_MODEL, 2, shard_w).transpose(0, 2, 1, 3).reshape(N_DEV, D_MODEL, shard_w)
    (from_sibling,) = run_exchange(to_core_exchange([dwin_blocks.astype(BF16)]), "reduce_w_in_to_core")
    pair_in, pair_in_b = pair_sum(dwin_blocks, from_sibling, "pair_sum_w_in")
    (others_in,) = run_exchange(to_chip_exchange([pair_in_b]), "reduce_w_in_to_chip")
    big["w_in"] = sum_adamw(pair_in, others_in, w["w_in"], m["w_in"], v["w_in"], "sum_adamw_w_in")
    return dx, big, small_slots


def _position():
    x, y, c = lax.axis_index("x"), lax.axis_index("y"), lax.axis_index("c")
    other_chips = [(1 - x, y), (x, 1 - y), (1 - x, 1 - y)]
    return x, y, c, other_chips


def _any_spec():
    return pl.BlockSpec(memory_space=pl.ANY)


class Exchange:
    def __init__(self, arrays, out_shape, sems, stages, collective_id, peers):
        self.arrays, self.out_shape, self.sems, self.stages = list(arrays), list(out_shape), list(sems), stages
        self.collective_id, self.peers = collective_id, peers

    def open(self):
        barrier = pltpu.get_barrier_semaphore()
        peers = self.peers()
        for peer in peers:
            pl.semaphore_signal(barrier, inc=1, device_id=peer, device_id_type=MESH)
        pl.semaphore_wait(barrier, len(peers))


def _siblings():
    x, y, c, _ = _position()
    return [(x, y, 1 - c)]


def _same_core_of_other_chips():
    x, y, c, chips = _position()
    return [(px, py, c) for px, py in chips]


def _gather_peers():
    return _siblings() + _same_core_of_other_chips()


def _all_others():
    x, y, c, _ = _position()
    return [(1 - x if rel & 4 else x, 1 - y if rel & 2 else y, 1 - c if rel & 1 else c) for rel in range(1, N_DEV)]


def gather_exchange(shards):
    n = len(shards)

    def stages(ins, outs, sems):
        send_sems, recv_sems, local_sems = sems

        def parts():
            x, y, c, chips = _position()
            me, sibling = (x, y, c), (x, y, 1 - c)

            def slot(a, px, py, pc):
                return outs[a].at[4 * px + 2 * py + pc]

            def copy(a, k, block, to, src=None):
                return pltpu.make_async_remote_copy(
                    src_ref=slot(a, *block) if src is None else src, dst_ref=slot(a, *block),
                    send_sem=send_sems.at[a, k], recv_sem=recv_sems.at[a, k], device_id=to, device_id_type=MESH)

            local = [pltpu.make_async_copy(ins[a], slot(a, *me), local_sems.at[a]) for a in range(n)]
            first = []
            for a in range(n):
                first.append(copy(a, 0, me, sibling, src=ins[a]))
                first += [copy(a, 1 + j, me, (*chip, c), src=ins[a]) for j, chip in enumerate(chips)]
            passed = [copy(a, 4 + j, (*chip, c), sibling) for j, chip in enumerate(chips) for a in range(n)]
            return c, chips, me, sibling, copy, local, first, passed

        def begin():
            _, _, _, _, _, local, first, _ = parts()
            for cp in local + first:
                cp.start()

        def middle():
            c, chips, me, _, copy, _, _, passed = parts()
            k = 0
            for j, chip in enumerate(chips):
                for a in range(n):
                    copy(a, 1 + j, (*chip, c), me).wait_recv()
                    passed[k].start()
                    k += 1

        def end():
            c, chips, me, sibling, copy, local, first, passed = parts()
            for a in range(n):
                copy(a, 0, sibling, me).wait_recv()
                for j, chip in enumerate(chips):
                    copy(a, 4 + j, (*chip, 1 - c), me).wait_recv()
            for cp in first + passed:
                cp.wait_send()
            for cp in local:
                cp.wait()

        return begin, middle, end

    return Exchange(
        shards, [jax.ShapeDtypeStruct((N_DEV,) + sh.shape, sh.dtype) for sh in shards],
        [pltpu.SemaphoreType.DMA((n, 7)), pltpu.SemaphoreType.DMA((n, 7)), pltpu.SemaphoreType.DMA((n,))], stages,
        collective_id=0, peers=_gather_peers)


def to_core_exchange(grads):
    n = len(grads)

    def stages(ins, outs, sems):
        send_sems, recv_sems = sems

        def copies():
            x, y, c, _ = _position()
            return [pltpu.make_async_remote_copy(
                src_ref=ins[a].at[2 * q + (1 - c)], dst_ref=outs[a].at[q], send_sem=send_sems.at[a, q],
                recv_sem=recv_sems.at[a, q], device_id=(x, y, 1 - c), device_id_type=MESH)
                for a in range(n) for q in range(4)]

        def begin():
            for cp in copies():
                cp.start()

        def end():
            for cp in copies():
                cp.wait()

        return begin, None, end

    return Exchange(grads, [jax.ShapeDtypeStruct((4,) + g.shape[1:], g.dtype) for g in grads],
                    [pltpu.SemaphoreType.DMA((n, 4)), pltpu.SemaphoreType.DMA((n, 4))], stages,
                    collective_id=1, peers=_siblings)


def pair_sum(grad, from_sibling, name):
    _, r, cdim = grad.shape
    tr = min(r, ELEMENTWISE_ROWS)
    c_idx = lax.axis_index("c").astype(jnp.int32).reshape(1)

    def body(c_ref, g_ref, s_ref, o_ref, ob_ref):
        total = g_ref[...] + s_ref[...]
        o_ref[...] = total
        ob_ref[...] = total.astype(BF16)

    blk = lambda: pl.BlockSpec((1, tr, cdim), lambda q, i, cr: (q, i, 0))
    return pl.pallas_call(
        body,
        name=name,
        grid_spec=pltpu.PrefetchScalarGridSpec(
            num_scalar_prefetch=1,
            grid=(4, r // tr),
            in_specs=[pl.BlockSpec((1, tr, cdim), lambda q, i, cr: (2 * q + cr[0], i, 0)), blk()],
            out_specs=[blk(), blk()],
        ),
        out_shape=[jax.ShapeDtypeStruct((4, r, cdim), F32), jax.ShapeDtypeStruct((4, r, cdim), BF16)],
        compiler_params=_params(dimension_semantics=("arbitrary", "arbitrary")),
    )(c_idx, grad, from_sibling)


def to_chip_exchange(pairs):
    n = len(pairs)

    def stages(ins, outs, sems):
        send_sems, recv_sems = sems

        def copies():
            x, y, c, chips = _position()
            return [pltpu.make_async_remote_copy(
                src_ref=ins[a].at[2 * px + py], dst_ref=outs[a].at[j], send_sem=send_sems.at[a, j],
                recv_sem=recv_sems.at[a, j], device_id=(px, py, c), device_id_type=MESH)
                for a in range(n) for j, (px, py) in enumerate(chips)]

        def begin():
            for cp in copies():
                cp.start()

        def end():
            for cp in copies():
                cp.wait()

        return begin, None, end

    return Exchange(pairs, [jax.ShapeDtypeStruct((3,) + p.shape[1:], p.dtype) for p in pairs],
                    [pltpu.SemaphoreType.DMA((n, 3)), pltpu.SemaphoreType.DMA((n, 3))], stages,
                    collective_id=2, peers=_same_core_of_other_chips)


def run_exchange(ex, name):
    n_in, n_out = len(ex.arrays), len(ex.out_shape)

    def body(*refs):
        begin, middle, end = ex.stages(refs[:n_in], refs[n_in:n_in + n_out], refs[n_in + n_out:])
        ex.open()
        begin()
        if middle is not None:
            middle()
        end()

    return pl.pallas_call(
        body,
        name=name,
        in_specs=[_any_spec()] * n_in,
        out_specs=[_any_spec()] * n_out,
        out_shape=ex.out_shape,
        scratch_shapes=ex.sems,
        compiler_params=pltpu.CompilerParams(collective_id=ex.collective_id),
    )(*ex.arrays)


def _riding(body, n_in, n_out, n_scratch, ex, first, middle, last):
    if ex is None:
        return body
    r_in, r_out = len(ex.arrays), len(ex.out_shape)

    def wrapped(*refs):
        k_in, refs = refs[:n_in], refs[n_in:]
        e_in, refs = refs[:r_in], refs[r_in:]
        k_out, refs = refs[:n_out], refs[n_out:]
        e_out, refs = refs[:r_out], refs[r_out:]
        k_scr, e_sems = refs[:n_scratch], refs[n_scratch:]
        begin, mid, end = ex.stages(e_in, e_out, e_sems)

        @pl.when(first())
        def _():
            ex.open()
            begin()

        body(*k_in, *k_out, *k_scr)
        if mid is not None:
            pl.when(middle())(mid)
        pl.when(last())(end)

    return wrapped


def _ride_specs(ex):
    if ex is None:
        return [], [], [], [], []
    return [_any_spec()] * len(ex.arrays), [_any_spec()] * len(ex.out_shape), ex.out_shape, ex.sems, ex.arrays


def _adamw(w, g, m, v):
    m = ADAM_B1 * m + (1.0 - ADAM_B1) * g
    v = ADAM_B2 * v + (1.0 - ADAM_B2) * (g * g)
    m_hat = m / (1.0 - ADAM_B1 ** ADAM_STEP)
    v_hat = v / (1.0 - ADAM_B2 ** ADAM_STEP)
    delta = -ADAM_LR * (m_hat / (jnp.sqrt(v_hat) + ADAM_EPS) + ADAM_WD * w)
    return delta, m, v


def sum_adamw(pairs, others, w, m, v, name):
    r, cdim = w.shape
    tr = min(r, ELEMENTWISE_ROWS // 2)
    chip_idx =(2 * lax.axis_index("x") + lax.axis_index("y")).astype(jnp.int32).reshape(1)

    def body(q_ref, p_ref, o_ref, w_ref, m_ref, v_ref, g_out, d_out, m_out, v_out):
        g = p_ref[0] + o_ref[0].astype(F32) + o_ref[1].astype(F32) + o_ref[2].astype(F32)
        g_out[...] = g
        d_out[...], m_out[...], v_out[...] = _adamw(w_ref[...], g, m_ref[...], v_ref[...])

    tile = lambda: pl.BlockSpec((tr, cdim), lambda i, qr: (i, 0))
    return pl.pallas_call(
        body,
        name=name,
        grid_spec=pltpu.PrefetchScalarGridSpec(
            num_scalar_prefetch=1,
            grid=(r // tr,),
            in_specs=[pl.BlockSpec((1, tr, cdim), lambda i, qr: (qr[0], i, 0)),
                      pl.BlockSpec((3, tr, cdim), lambda i, qr: (0, i, 0)), tile(), tile(), tile()],
            out_specs=[tile(), tile(), tile(), tile()],
        ),
        out_shape=[jax.ShapeDtypeStruct((r, cdim), F32)] * 4,
        compiler_params=_params(dimension_semantics=("arbitrary",)),
    )(chip_idx, pairs, others, w, m, v)


def small_exchange(packed):
    def stages(ins, outs, sems):
        send_sems, recv_sems, local_sem = sems
        (src,), (slots,) = ins, outs

        def copies():
            x, y, c, _ = _position()
            my_id = 4 * x + 2 * y + c
            sends, landings = [], []
            for rel in range(1, N_DEV):
                px = 1 - x if (rel >> 2) & 1 else x
                py = 1 - y if (rel >> 1) & 1 else y
                pc = 1 - c if rel & 1 else c
                peer = dict(send_sem=send_sems.at[rel - 1], recv_sem=recv_sems.at[rel - 1], device_id=(px, py, pc),
                            device_id_type=MESH)
                sends.append(pltpu.make_async_remote_copy(src_ref=src, dst_ref=slots.at[my_id], **peer))
                landings.append(pltpu.make_async_remote_copy(src_ref=src, dst_ref=slots.at[4 * px + 2 * py + pc], **peer))
            return pltpu.make_async_copy(src, slots.at[my_id], local_sem), sends, landings

        def begin():
            local, sends, _ = copies()
            local.start()
            for cp in sends:
                cp.start()

        def end():
            local, sends, landings = copies()
            for cp in landings:
                cp.wait_recv()
            for cp in sends:
                cp.wait_send()
            local.wait()

        return begin, None, end

    return Exchange([packed], [jax.ShapeDtypeStruct((N_DEV,) + packed.shape, packed.dtype)],
                    [pltpu.SemaphoreType.DMA((N_DEV - 1,)), pltpu.SemaphoreType.DMA((N_DEV - 1,)),
                     pltpu.SemaphoreType.DMA(())], stages, collective_id=3, peers=_all_others)


def small_adamw(slots, w, m, v):
    def body(r_ref, w_ref, m_ref, v_ref, g_out, d_out, m_out, v_out, loss_out):
        red = r_ref[0]
        for k in range(1, N_DEV):
            red = red + r_ref[k]
        wv = w_ref[...]
        lb = _lower_bound(jnp.concatenate([wv[5:6, :HGRN_WIDTH], wv[5:6, HGRN_WIDTH:]], axis=0))
        t = red[5:6, :HGRN_WIDTH] * lb * (1.0 - lb)
        row = lax.broadcasted_iota(jnp.int32, red.shape, 0)
        g = jnp.where(row == 5, jnp.concatenate([t, -t], axis=1), jnp.where(row >= 6, 0.0, red))
        g_out[...] = g
        d_out[...], m_out[...], v_out[...] = _adamw(wv, g, m_ref[...], v_ref[...])
        loss = jnp.sum(red[6:7, :], axis=-1, keepdims=True) * (0.5 / D_MODEL)
        loss_out[...] = jnp.broadcast_to(loss, loss_out.shape)

    return pl.pallas_call(
        body,
        name="small_adamw",
        in_specs=[_vmem_spec()] * 4,
        out_specs=[_vmem_spec()] * 5,
        out_shape=[jax.ShapeDtypeStruct(w.shape, F32)] * 4 + [jax.ShapeDtypeStruct((SUBLANES, LANES), F32)],
    )(slots, w, m, v)


def _pack_small(g1, gp, g_pre, g_post, an, hn, logits_or_dlb, extra=None):
    row5 = logits_or_dlb.reshape(1, -1)
    row5 = jnp.pad(row5, ((0, 0), (0, D_MODEL - row5.shape[1])))
    row6 = jnp.zeros((1, D_MODEL), F32) if extra is None else extra
    return jnp.concatenate([g1, gp, g_pre, g_post, jnp.concatenate([an, hn], axis=1), row5, row6,
                            jnp.zeros((1, D_MODEL), F32)], axis=0)


def _unpack_small(p):
    return dict(mix_pre_norm=p[0:1], mix_post_norm=p[1:2], mlp_pre_norm=p[2:3], mlp_post_norm=p[3:4],
                attn_out_norm=p[4:5, :ATTN_WIDTH], hgrn_out_norm=p[4:5, ATTN_WIDTH:],
                hgrn_lb_logits=p[5].reshape(2, HGRN_WIDTH))


BIG = ("w_in", "w_out", "w_ff1", "w_ff2")
ORDER = ("mix_pre_norm", "w_in", "attn_out_norm", "hgrn_lb_logits", "hgrn_out_norm", "w_out", "mix_post_norm",
         "mlp_pre_norm", "w_ff1", "w_ff2", "mlp_post_norm")


def kernel(x, mix_pre_norm, w_in, attn_out_norm, hgrn_lb_logits, hgrn_out_norm, w_out, mix_post_norm, mlp_pre_norm, w_ff1, w_ff2, mlp_post_norm, loss_target, m_mix_pre_norm, m_w_in, m_attn_out_norm, m_hgrn_lb_logits, m_hgrn_out_norm, m_w_out, m_mix_post_norm, m_mlp_pre_norm, m_w_ff1, m_w_ff2, m_mlp_post_norm, v_mix_pre_norm, v_w_in, v_attn_out_norm, v_hgrn_lb_logits, v_hgrn_out_norm, v_w_out, v_mix_post_norm, v_mlp_pre_norm, v_w_ff1, v_w_ff2, v_mlp_post_norm):
    w = dict(w_in=w_in[0], w_out=w_out[0], w_ff1=w_ff1[0], w_ff2=w_ff2[0])
    m = dict(w_in=m_w_in[0], w_out=m_w_out[0], w_ff1=m_w_ff1[0], w_ff2=m_w_ff2[0])
    v = dict(w_in=v_w_in[0], w_out=v_w_out[0], w_ff1=v_w_ff1[0], w_ff2=v_w_ff2[0])

    dx, big, small_slots = train_step(x[0], loss_target[0], mix_pre_norm, attn_out_norm, hgrn_lb_logits, hgrn_out_norm,
                                      mix_post_norm, mlp_pre_norm, mlp_post_norm, w, m, v)

    pack = lambda a, b, c2, d, e, f, g: _pack_small(a, b, c2, d, e, f, g)
    w_s = pack(mix_pre_norm, mix_post_norm, mlp_pre_norm, mlp_post_norm, attn_out_norm, hgrn_out_norm, hgrn_lb_logits)
    m_s = pack(m_mix_pre_norm, m_mix_post_norm, m_mlp_pre_norm, m_mlp_post_norm, m_attn_out_norm, m_hgrn_out_norm,
               m_hgrn_lb_logits)
    v_s = pack(v_mix_pre_norm, v_mix_post_norm, v_mlp_pre_norm, v_mlp_post_norm, v_attn_out_norm, v_hgrn_out_norm,
               v_hgrn_lb_logits)
    g_s, d_s, nm_s, nv_s, loss = small_adamw(small_slots, w_s, m_s, v_s)
    small_out = [_unpack_small(t) for t in (g_s, d_s, nm_s, nv_s)]

    outs = [loss[0, 0], dx[None]]
    for kind in range(4):
        for name in ORDER:
            outs.append(big[name][kind][None] if name in BIG else small_out[kind][name])
    return tuple(outs)
```

```python
import jax
import jax.numpy as jnp
from jax import lax
from jax.experimental import pallas as pl
from jax.experimental.pallas import tpu as pltpu

F32 = jnp.float32
BF16 = jnp.bfloat16

D_MODEL = 1024
ATTN_WIDTH = 512
ATTN_HEAD_DIM = 64
ATTN_HEADS = 8
ATTN_BLOCK = 128
DILATIONS = (1, 4, 16)
HGRN_WIDTH = 512
HGRN_HEADS = 4
HGRN_HEAD_DIM = 128
HGRN_CHUNK = 64
IN_PROJ_WIDTH = 3584
D_FF = 4096
RMS_EPS = 1e-6
N_DEV = 8
ADAM_LR = 0.001
ADAM_B1 = 0.9
ADAM_B2 = 0.999
ADAM_EPS = 1e-08
ADAM_WD = 0.01
ADAM_STEP = 10

SUBLANES = 8
LANES = 128
COLUMN_UNROLL = 8
HGRN_CHUNKS_PER_STEP = 2
SUB_BLOCK = 16
TOKEN_TILE = 512
ELEMENTWISE_ROWS = 1024
MLP_TILE = 256
PROJ_TILE = 512
VMEM_BYTES_V7X = 64 * 1024 * 1024
VMEM_LIMIT = VMEM_BYTES_V7X // 8 * 7
NEG_BIG = -1e30
MESH = pl.DeviceIdType.MESH


def _params(ride=None, **kw):
    if ride is not None:
        kw["collective_id"] = ride.collective_id
    return pltpu.CompilerParams(vmem_limit_bytes=VMEM_LIMIT, **kw)


def _vmem_spec():
    return pl.BlockSpec(memory_space=pltpu.VMEM)


def _dot(a, b):
    return jnp.dot(a, b, preferred_element_type=F32)


def _dot_nt(a, b):
    return lax.dot_general(a, b, (((1,), (1,)), ((), ())), preferred_element_type=F32)


def _dot_tn(a, b):
    return lax.dot_general(a, b, (((0,), (0,)), ((), ())), preferred_element_type=F32)


def _sigmoid(x):
    return 1.0 / (1.0 + jnp.exp(-x))


def _rms_fwd(x, gain, width):
    r = lax.rsqrt(jnp.sum(x * x, axis=-1, keepdims=True) * (1.0 / width) + RMS_EPS)
    return x * r * gain


def _rms_bwd(dy, x, gain, width):
    r = lax.rsqrt(jnp.sum(x * x, axis=-1, keepdims=True) * (1.0 / width) + RMS_EPS)
    xhat = x * r
    dxhat = dy * gain
    dx = r * (dxhat - xhat * (jnp.sum(dxhat * xhat, axis=-1, keepdims=True) * (1.0 / width)))
    return dx, dy * xhat


def _split3(x):
    hi = x.astype(BF16)
    r1 = x - hi.astype(F32)
    mid = r1.astype(BF16)
    lo = (r1 - mid.astype(F32)).astype(BF16)
    return hi, mid, lo


def _tri_sum(tri_bf16, x):
    hi, mid, lo = _split3(x)
    return _dot(tri_bf16, hi) + _dot(tri_bf16, mid) + _dot(tri_bf16, lo)


def _dilated_spec(d, tm, width):
    return pl.BlockSpec((d, tm // d, width), lambda i: (0, i, 0))


def _lane_blocks(ref, value):
    for c in range(ref.shape[0]):
        ref[c] = value[:, c * LANES:(c + 1) * LANES]


def _to_dilated(src_ref, dst_ref, d, tm, cast=None):
    for r in range(d):
        for c in range(src_ref.shape[0]):
            v = src_ref[c] if d == 1 else src_ref[c, pl.ds(r, tm // d, stride=d), :]
            dst_ref[r, :, c * LANES:(c + 1) * LANES] = v if cast is None else v.astype(cast)


def _from_dilated(src_ref, scratch_ref, d, tm):
    if d == 1:
        return src_ref[0].astype(F32)
    nblk = scratch_ref.shape[0]
    for r in range(d):
        for c in range(nblk):
            scratch_ref[c, pl.ds(r, tm // d, stride=d), :] = src_ref[r, :, c * LANES:(c + 1) * LANES].astype(F32)
    return jnp.concatenate([scratch_ref[c] for c in range(nblk)], axis=1)


def in_proj_fwd(x, g1, w_in_b, ride=None):
    s = x.shape[0]
    tm = PROJ_TILE
    qkv_w = 3 * ATTN_WIDTH
    hg_w = IN_PROJ_WIDTH - qkv_w

    def body(x_ref, g_ref, w_ref, hg_ref, h_ref, *rest):
        qkv_refs, qkv_scr = rest[:len(DILATIONS)], rest[len(DILATIONS)]
        h = _rms_fwd(x_ref[...], g_ref[...], D_MODEL).astype(BF16)
        h_ref[...] = h
        proj = _dot(h, w_ref[...])
        hg_ref[...] = proj[:, qkv_w:]
        _lane_blocks(qkv_scr, proj[:, :qkv_w])
        for d, ref in zip(DILATIONS, qkv_refs):
            _to_dilated(qkv_scr, ref, d, tm, cast=BF16)

    n_steps = s // tm
    step = lambda k: (lambda: pl.program_id(0) == k)
    e_in, e_out, e_shape, e_scr, e_args = _ride_specs(ride)
    return pl.pallas_call(
        _riding(body, 3, 2 + len(DILATIONS), 1, ride, step(0), step(n_steps - 2), step(n_steps - 1)),
        name="in_proj_fwd",
        grid=(n_steps,),
        in_specs=[
            pl.BlockSpec((tm, D_MODEL), lambda i: (i, 0)),
            pl.BlockSpec((1, D_MODEL), lambda i: (0, 0)),
            _vmem_spec(),
        ] + e_in,
        out_specs=[
            pl.BlockSpec((tm, hg_w), lambda i: (i, 0)),
            pl.BlockSpec((tm, D_MODEL), lambda i: (i, 0)),
        ] + [_dilated_spec(d, tm, qkv_w) for d in DILATIONS] + e_out,
        out_shape=[jax.ShapeDtypeStruct((s, hg_w), F32), jax.ShapeDtypeStruct((s, D_MODEL), BF16)] + [
            jax.ShapeDtypeStruct((d, s // d, qkv_w), BF16) for d in DILATIONS] + e_shape,
        scratch_shapes=[pltpu.VMEM((qkv_w // LANES, tm, LANES), F32)] + e_scr,
        compiler_params=_params(ride, dimension_semantics=("arbitrary",)),
    )(x, g1, w_in_b, *e_args)


ATTN_SCALE = ATTN_HEAD_DIM ** -0.5


def _fill_attn_bias(bias_ref, dilation):
    qi = lax.broadcasted_iota(jnp.int32, (ATTN_BLOCK, 2 * ATTN_BLOCK), 0)
    kj = lax.broadcasted_iota(jnp.int32, (ATTN_BLOCK, 2 * ATTN_BLOCK), 1)
    dist = qi + ATTN_BLOCK - kj
    valid = (dist >= 0) & (dist <= ATTN_BLOCK)
    for head in range(ATTN_HEADS):
        slope = 2.0 ** (-8.0 * (head + 1) / ATTN_HEADS)
        bias = jnp.where(valid, dist.astype(F32) * (-slope * dilation), NEG_BIG)
        bias_ref[0, head] = bias
        bias_ref[1, head] = jnp.where(kj >= ATTN_BLOCK, bias, NEG_BIG)


def _stack_heads(x):
    low = _lane_half(x.shape, 0)
    zero = jnp.zeros_like(x)
    return jnp.concatenate([jnp.where(low, x, zero), jnp.where(low, zero, x)], axis=0)


def _unstack_heads(y):
    half = y.shape[0] // 2
    return jnp.where(_lane_half((half, y.shape[1]), 0), y[:half], y[half:])


def _attn_scores(q_stack, kcat, bias_ref, pair, first_block):
    f = first_block.astype(jnp.int32)
    bias = jnp.concatenate([bias_ref[f, 2 * pair], bias_ref[f, 2 * pair + 1]], axis=0)
    return _dot_nt(q_stack, kcat) + bias


def _lane_half(shape, sub):
    lane = lax.broadcasted_iota(jnp.int32, shape, 1)
    return (lane < ATTN_HEAD_DIM) if sub == 0 else (lane >= ATTN_HEAD_DIM)


def _sub_block(col, row):
    return pl.BlockSpec((None, ATTN_BLOCK, ATTN_WIDTH), lambda r, n: (r, row(n), col))


def attn_fwd(qkv, dilation, ride=None):
    d, length, _ = qkv.shape
    assert d == dilation
    nb = length // ATTN_BLOCK

    def body(q_ref, kc_ref, kp_ref, vc_ref, vp_ref, o_ref, lse_ref, bias_ref):
        @pl.when((pl.program_id(0) == 0) & (pl.program_id(1) == 0))
        def _():
            _fill_attn_bias(bias_ref, d)

        first = pl.program_id(1) == 0
        for pair in range(ATTN_HEADS // 2):
            lanes = slice(pair * LANES, (pair + 1) * LANES)
            q_stack = _stack_heads(q_ref[:, lanes] * ATTN_SCALE)
            kcat = jnp.concatenate([kp_ref[:, lanes], kc_ref[:, lanes]], axis=0)
            vcat = jnp.concatenate([vp_ref[:, lanes], vc_ref[:, lanes]], axis=0)
            sc = _attn_scores(q_stack, kcat, bias_ref, pair, first)
            m = jnp.max(sc, axis=-1, keepdims=True)
            p = jnp.exp(sc - m)
            den = jnp.sum(p, axis=-1, keepdims=True)
            o_ref[:, lanes] = _unstack_heads(_dot(p.astype(BF16), vcat) / den).astype(BF16)
            lse_ref[:, lanes] = _unstack_heads(jnp.broadcast_to(m + jnp.log(den), (2 * ATTN_BLOCK, LANES)))

    cur = lambda n: n
    prev = lambda n: jnp.maximum(n - 1, 0)
    flat = lambda k: (lambda: pl.program_id(0) * nb + pl.program_id(1) == k)
    e_in, e_out, e_shape, e_scr, e_args = _ride_specs(ride)
    return pl.pallas_call(
        _riding(body, 5, 2, 1, ride, flat(0), flat((3 * d * nb) // 4), flat(d * nb - 1)),
        name=f"attn_fwd_d{d}",
        grid=(d, nb),
        in_specs=[_sub_block(0, cur), _sub_block(1, cur), _sub_block(1, prev), _sub_block(2, cur),
                  _sub_block(2, prev)] + e_in,
        out_specs=[_sub_block(0, cur), _sub_block(0, cur)] + e_out,
        out_shape=[jax.ShapeDtypeStruct((d, length, ATTN_WIDTH), BF16),
                   jax.ShapeDtypeStruct((d, length, ATTN_WIDTH), F32)] + e_shape,
        scratch_shapes=[pltpu.VMEM((2, ATTN_HEADS, ATTN_BLOCK, 2 * ATTN_BLOCK), F32)] + e_scr,
        compiler_params=_params(ride, dimension_semantics=("arbitrary", "arbitrary")),
    )(qkv, qkv, qkv, qkv, qkv, *e_args)


def attn_bwd(qkv, d_out, lse, delta, dilation, ride=None):
    d, length, _ = qkv.shape
    assert d == dilation
    nb = length // ATTN_BLOCK

    steps = d * nb + 1

    def body(q_ref, kc_ref, kp_ref, vc_ref, vp_ref, do_ref, lse_ref, dl_ref, dq_ref, dk_ref, dv_ref, ck_ref, cv_ref,
             bias_ref):
        t = pl.program_id(0)

        @pl.when(t == 0)
        def _():
            ck_ref[...] = jnp.zeros_like(ck_ref)
            cv_ref[...] = jnp.zeros_like(cv_ref)
            _fill_attn_bias(bias_ref, d)

        @pl.when(t < steps - 1)
        def _():
            first = t % nb == 0
            for pair in range(ATTN_HEADS // 2):
                lanes = slice(pair * LANES, (pair + 1) * LANES)
                q_stack = _stack_heads(q_ref[:, lanes] * ATTN_SCALE)
                do_stack = _stack_heads(do_ref[:, lanes])
                kcat = jnp.concatenate([kp_ref[:, lanes], kc_ref[:, lanes]], axis=0)
                vcat = jnp.concatenate([vp_ref[:, lanes], vc_ref[:, lanes]], axis=0)
                col_a, col_b = 2 * pair, 2 * pair + 1
                lse_col = jnp.concatenate([lse_ref[:, col_a:col_a + 1], lse_ref[:, col_b:col_b + 1]], axis=0)
                dl_col = jnp.concatenate([dl_ref[:, col_a:col_a + 1], dl_ref[:, col_b:col_b + 1]], axis=0)
                p = jnp.exp(_attn_scores(q_stack, kcat, bias_ref, pair, first) - lse_col)
                ds = (p * (_dot_nt(do_stack, vcat) - dl_col)).astype(BF16)
                dq_ref[:, lanes] = (_unstack_heads(_dot(ds, kcat)) * ATTN_SCALE).astype(BF16)
                dk_cat = _dot_tn(ds, q_stack)
                dv_cat = _dot_tn(p.astype(BF16), do_stack)
                dk_ref[:, lanes] = (ck_ref[:, lanes] + dk_cat[:ATTN_BLOCK]).astype(BF16)
                dv_ref[:, lanes] = (cv_ref[:, lanes] + dv_cat[:ATTN_BLOCK]).astype(BF16)
                ck_ref[:, lanes] = dk_cat[ATTN_BLOCK:]
                cv_ref[:, lanes] = dv_cat[ATTN_BLOCK:]

        @pl.when(t == steps - 1)
        def _():
            dk_ref[...] = ck_ref[...].astype(BF16)
            dv_ref[...] = cv_ref[...].astype(BF16)

    blk = (ATTN_BLOCK, ATTN_WIDTH)

    def spec(col, shift, width=ATTN_WIDTH):
        def index(t):
            f = jnp.minimum(t, steps - 2) if shift > -2 else jnp.maximum(t - 1, 0)
            r, n = f // nb, f % nb
            return (r, jnp.maximum(n - 1, 0) if shift == -1 else n, col)
        return pl.BlockSpec((None, ATTN_BLOCK, width), index)

    step = lambda k: (lambda: pl.program_id(0) == k)
    e_in, e_out, e_shape, e_scr, e_args = _ride_specs(ride)
    return pl.pallas_call(
        _riding(body, 8, 3, 3, ride, step(0), step(steps // 2), step(steps - 1)),
        name=f"attn_bwd_d{d}",
        grid=(steps,),
        in_specs=[spec(0, 0), spec(1, 0), spec(1, -1), spec(2, 0), spec(2, -1), spec(0, 0), spec(0, 0, LANES),
                  spec(0, 0, LANES)] + e_in,
        out_specs=[spec(0, 0), spec(0, -2), spec(0, -2)] + e_out,
        out_shape=[jax.ShapeDtypeStruct((d, length, ATTN_WIDTH), BF16)] * 3 + e_shape,
        scratch_shapes=[pltpu.VMEM(blk, F32), pltpu.VMEM(blk, F32),
                        pltpu.VMEM((2, ATTN_HEADS, ATTN_BLOCK, 2 * ATTN_BLOCK), F32)] + e_scr,
        compiler_params=_params(ride, dimension_semantics=("arbitrary",)),
    )(qkv, qkv, qkv, qkv, qkv, d_out, lse, delta, *e_args)


def _lower_bound(logits):
    return _sigmoid(logits[0:1, :] - logits[1:2, :])


def _hgrn_gates(q, fp, lb):
    sq = _sigmoid(q)
    qf = q * sq
    sig = _sigmoid(fp)
    sig_neg = _sigmoid(-fp)
    kf = (1.0 - lb) * sig_neg
    log_sig = jnp.minimum(fp, 0.0) - jnp.log(1.0 + jnp.exp(-jnp.abs(fp)))
    a = jnp.log(lb)
    c = jnp.log(1.0 - lb) + log_sig
    log_f = jnp.maximum(a, c) + jnp.log(1.0 + jnp.exp(-jnp.abs(a - c)))
    return sq, qf, (sig, sig_neg, c), log_f, kf


def _tril_bf16(n, upper=False):
    r = lax.broadcasted_iota(jnp.int32, (n, n), 0)
    c = lax.broadcasted_iota(jnp.int32, (n, n), 1)
    keep = (c >= r) if upper else (c <= r)
    return jnp.where(keep, 1.0, 0.0).astype(BF16)


def _hgrn_diagonal_loops(c_len, diagonal):
    for half in range(SUB_BLOCK // SUBLANES):
        def step(jj, carry, half=half):
            j = half * SUBLANES + jj
            for i in range(c_len // SUB_BLOCK):
                diagonal(slice(i * SUB_BLOCK + half * SUBLANES, (i + 1) * SUB_BLOCK), j, i * SUB_BLOCK + j)
            return carry

        lax.fori_loop(0, SUBLANES, step, 0, unroll=COLUMN_UNROLL)


def _hgrn_off_diagonal(b, qf, kf):
    c_len, width = b.shape
    edges = [b[0:1, :]] + [b[i * SUB_BLOCK - 1:i * SUB_BLOCK, :] for i in range(1, c_len // SUB_BLOCK)]
    eq = jnp.exp(b - jnp.concatenate([jnp.broadcast_to(e, (SUB_BLOCK, width)) for e in edges], axis=0))
    q_til = qf * eq
    k_til, ek = [], []
    for i in range(1, c_len // SUB_BLOCK):
        n = i * SUB_BLOCK
        e = jnp.exp(edges[i] - b[:n, :])
        ek.append(e)
        k_til.append(jnp.concatenate([kf[:n, :] * e, jnp.zeros((2 * c_len - n, width), F32)], axis=0))
    return q_til, k_til, eq, ek


def _split2(x):
    hi = x.astype(BF16)
    return hi, (x - hi.astype(F32)).astype(BF16)


def hgrn_fwd(proj, lb, ride=None):
    s = proj.shape[0]
    c_len, nh, hd = HGRN_CHUNK, HGRN_HEADS, HGRN_HEAD_DIM
    n_chunks = s // c_len
    col0 = 0

    cps = 2 * HGRN_CHUNKS_PER_STEP
    n_steps = n_chunks // cps

    def body(q_ref, f_ref, i_ref, lb_ref, o_ref, st_out_ref, a_out_ref, st_ref, b_ref, qf_ref, kf_ref, a_ref):
        @pl.when(pl.program_id(0) == 0)
        def _():
            st_ref[...] = jnp.zeros_like(st_ref)

        lbv = _lower_bound(lb_ref[...])
        for u in range(cps):
            rs = slice(u * c_len, (u + 1) * c_len)
            b_u, qf_u, kf_u, a_u = b_ref.at[u], qf_ref.at[u], kf_ref.at[u], a_ref.at[u]
            _, qf, _, log_f, kf = _hgrn_gates(q_ref[rs, :], f_ref[rs, :], lbv)
            b = _tri_sum(_tril_bf16(c_len), log_f)
            b_u[...] = b
            qf_u[...] = qf
            kf_u[...] = kf
            a_u[...] = jnp.zeros_like(a_u)

            def diagonal(rows, j, key, b_u=b_u, qf_u=qf_u, kf_u=kf_u, a_u=a_u):
                bj = b_u[pl.ds(key, 1), :]
                kj = kf_u[pl.ds(key, 1), :]
                nrow = rows.stop - rows.start
                t_loc = lax.broadcasted_iota(jnp.int32, (nrow, nh * hd), 0) + (rows.start % SUB_BLOCK)
                e = jnp.exp(jnp.where(t_loc >= j, b_u[rows, :] - bj, NEG_BIG))
                prod = qf_u[rows, :] * kj * e
                lane = lax.broadcasted_iota(jnp.int32, (nrow, hd), 1)
                for h in range(nh):
                    col = jnp.sum(prod[:, h * hd:(h + 1) * hd], axis=-1, keepdims=True)
                    a_u[h, rows, :] = jnp.where(lane == key, col, a_u[h, rows, :])

            _hgrn_diagonal_loops(c_len, diagonal)
            q_til, k_til, _, _ = _hgrn_off_diagonal(b, qf, kf)
            q_til = q_til.astype(BF16)
            k_til = [k.astype(BF16) for k in k_til]

            b_last = b[c_len - 1:c_len, :]
            qb = (qf * jnp.exp(b)).astype(BF16)
            kb2 = (kf * jnp.exp(b_last - b)).astype(BF16)
            vf = i_ref[rs, :].astype(BF16)
            for h in range(nh):
                hs = slice(h * hd, (h + 1) * hd)
                st = st_ref[h]
                st_out_ref[u, h] = st
                off = [jnp.zeros((SUB_BLOCK, hd), F32)]
                for i in range(1, c_len // SUB_BLOCK):
                    off.append(_dot_nt(q_til[i * SUB_BLOCK:(i + 1) * SUB_BLOCK, hs], k_til[i - 1][:, hs]))
                a_h = a_u[h] + jnp.concatenate(off, axis=0)
                a_out_ref[rs, hs] = a_h
                o_ref[rs, hs] = _dot_nt(qb[:, hs], st.astype(BF16)) + _dot(a_h[:, :c_len].astype(BF16), vf[:, hs])
                st_ref[h] = st * jnp.exp(b_last[:, hs]) + _dot_tn(vf[:, hs], kb2[:, hs])

    blk = (cps * c_len, HGRN_WIDTH)
    sblk = (cps, c_len, HGRN_WIDTH)
    step = lambda k: (lambda: pl.program_id(0) == k)
    e_in, e_out, e_shape, e_scr, e_args = _ride_specs(ride)
    return pl.pallas_call(
        _riding(body, 4, 3, 5, ride, step(0), step((7 * n_steps) // 8), step(n_steps - 1)),
        name="hgrn_fwd",
        grid=(n_steps,),
        in_specs=[
            pl.BlockSpec(blk, lambda c: (c, col0)),
            pl.BlockSpec(blk, lambda c: (c, col0 + 1)),
            pl.BlockSpec(blk, lambda c: (c, col0 + 2)),
            pl.BlockSpec((2, HGRN_WIDTH), lambda c: (0, 0)),
        ] + e_in,
        out_specs=[
            pl.BlockSpec(blk, lambda c: (c, 0)),
            pl.BlockSpec((cps, nh, hd, hd), lambda c: (c, 0, 0, 0)),
            pl.BlockSpec(blk, lambda c: (c, 0)),
        ] + e_out,
        out_shape=[
            jax.ShapeDtypeStruct((s, HGRN_WIDTH), F32),
            jax.ShapeDtypeStruct((n_chunks, nh, hd, hd), F32),
            jax.ShapeDtypeStruct((s, nh * hd), F32),
        ] + e_shape,
        scratch_shapes=[
            pltpu.VMEM((nh, hd, hd), F32),
            pltpu.VMEM(sblk, F32),
            pltpu.VMEM(sblk, F32),
            pltpu.VMEM(sblk, F32),
            pltpu.VMEM((cps, nh, c_len, hd), F32),
        ] + e_scr,
        compiler_params=_params(ride, dimension_semantics=("arbitrary",)),
    )(proj, proj, proj, lb, *e_args)


def hgrn_bwd(proj, lb, d_o, states, a_mat, ride=None):
    s = proj.shape[0]
    c_len, nh, hd = HGRN_CHUNK, HGRN_HEADS, HGRN_HEAD_DIM
    n_chunks = s // c_len
    col0 = 0
    cps = HGRN_CHUNKS_PER_STEP
    n_steps = n_chunks // cps
    last = n_steps - 1

    def body(q_ref, f_ref, i_ref, lb_ref, do_ref, st_in_ref, a_in_ref, dq_ref, df_ref, di_ref, dlb_ref,
             dst_ref, b_ref, qf_ref, kf_ref, da_ref, dqi_ref, dki_ref):
        @pl.when(pl.program_id(0) == 0)
        def _():
            dst_ref[...] = jnp.zeros_like(dst_ref)
            dlb_ref[...] = jnp.zeros_like(dlb_ref)

        lbv = _lower_bound(lb_ref[...])
        for u in reversed(range(cps)):
            rs = slice(u * c_len, (u + 1) * c_len)
            b_u, qf_u, kf_u, da_u, dqi_u, dki_u = (b_ref.at[u], qf_ref.at[u], kf_ref.at[u], da_ref.at[u], dqi_ref.at[u],
                                                   dki_ref.at[u])
            q = q_ref[rs, :]
            sq, qf, (sig, sig_neg, log_c), log_f, kf = _hgrn_gates(q, f_ref[rs, :], lbv)
            b = _tri_sum(_tril_bf16(c_len), log_f)
            b_u[...] = b
            qf_u[...] = qf
            kf_u[...] = kf
            b_last = b[c_len - 1:c_len, :]
            eb = jnp.exp(b)
            ebl = jnp.exp(b_last - b)
            qb = qf * eb
            kb2 = kf * ebl
            vf = i_ref[rs, :]
            d_o = do_ref[rs, :]
            qb_b, kb2_b, vf_b, do_b = qb.astype(BF16), kb2.astype(BF16), vf.astype(BF16), d_o.astype(BF16)
            tq = lax.broadcasted_iota(jnp.int32, (c_len, hd), 0)
            lane = lax.broadcasted_iota(jnp.int32, (c_len, hd), 1)

            dqb_parts, dvf_parts, dkb2_parts, dbl_parts = [], [], [], []
            for h in range(nh):
                hs = slice(h * hd, (h + 1) * hd)
                st = st_in_ref[u, h]
                dst = dst_ref[h]
                st_b, dst_b = st.astype(BF16), dst.astype(BF16)
                a_h = a_in_ref[rs, hs][:, :c_len].astype(BF16)
                dqb_parts.append(_dot(do_b[:, hs], st_b))
                dvf_parts.append(_dot_tn(a_h, do_b[:, hs]) + _dot_nt(kb2_b[:, hs], dst_b))
                dkb2_parts.append(_dot(vf_b[:, hs], dst_b))
                da = _dot_nt(do_b[:, hs], vf_b[:, hs])
                da = jnp.concatenate([da, jnp.zeros((c_len, hd - c_len), F32)], axis=1)
                da_u[h] = jnp.where(tq >= lane, da, 0.0)
                dbl_parts.append(jnp.sum(dst * st, axis=0, keepdims=True) * jnp.exp(b_last[:, hs]))
                dst_ref[h] = dst * jnp.exp(b_last[:, hs]) + _dot_tn(do_b[:, hs], qb_b[:, hs])
            dqb = jnp.concatenate(dqb_parts, axis=1)
            dvf = jnp.concatenate(dvf_parts, axis=1)
            dkb2 = jnp.concatenate(dkb2_parts, axis=1)
            dbl = jnp.concatenate(dbl_parts, axis=1) + jnp.sum(dkb2 * kb2, axis=0, keepdims=True)

            dqi_u[...] = jnp.zeros_like(dqi_u)
            t_idx = lax.broadcasted_iota(jnp.int32, (c_len, nh * hd), 0)

            def diagonal(rows, j, key, b_u=b_u, qf_u=qf_u, kf_u=kf_u, da_u=da_u, dqi_u=dqi_u, dki_u=dki_u):
                bj = b_u[pl.ds(key, 1), :]
                kj = kf_u[pl.ds(key, 1), :]
                nrow = rows.stop - rows.start
                t_loc = lax.broadcasted_iota(jnp.int32, (nrow, nh * hd), 0) + (rows.start % SUB_BLOCK)
                e = jnp.exp(jnp.where(t_loc >= j, b_u[rows, :] - bj, NEG_BIG))
                lane_r = lax.broadcasted_iota(jnp.int32, (nrow, hd), 1)
                cols = [jnp.sum(jnp.where(lane_r == key, da_u[h, rows, :], 0.0), axis=-1, keepdims=True)
                        for h in range(nh)]
                w = e * jnp.concatenate([jnp.broadcast_to(cc, (nrow, hd)) for cc in cols], axis=1)
                dqi_u[rows, :] += w * kj
                dki_u[pl.ds(key, 1), :] = jnp.sum(w * qf_u[rows, :], axis=0, keepdims=True)

            _hgrn_diagonal_loops(c_len, diagonal)

            q_til, k_til, eq, ek = _hgrn_off_diagonal(b, qf, kf)
            q_hi, q_lo = _split2(q_til)
            k_pairs = [_split2(k) for k in k_til]
            n_sub = c_len // SUB_BLOCK
            dq_heads, dk_heads = [], []
            for h in range(nh):
                hs = slice(h * hd, (h + 1) * hd)
                dq_rows = [jnp.zeros((SUB_BLOCK, hd), F32)]
                dk_h = jnp.zeros((c_len, hd), F32)
                for i in range(1, n_sub):
                    rows = slice(i * SUB_BLOCK, (i + 1) * SUB_BLOCK)
                    n = i * SUB_BLOCK
                    da_i = da_u[h, rows, :].astype(BF16)
                    k_hi, k_lo = k_pairs[i - 1]
                    dq_rows.append((_dot(da_i, k_hi[:, hs]) + _dot(da_i, k_lo[:, hs])) * eq[rows, hs])
                    dk_t = (_dot_tn(da_i, q_hi[rows, hs]) + _dot_tn(da_i, q_lo[rows, hs]))[:n, :] * ek[i - 1][:, hs]
                    dk_h = dk_h + jnp.concatenate([dk_t, jnp.zeros((c_len - n, hd), F32)], axis=0)
                dq_heads.append(jnp.concatenate(dq_rows, axis=0))
                dk_heads.append(dk_h)
            dq_intra = dqi_u[...] + jnp.concatenate(dq_heads, axis=1)
            dk_intra = dki_u[...] + jnp.concatenate(dk_heads, axis=1)

            db = dqb * qb + qf * dq_intra - kf * dk_intra - dkb2 * kb2
            db = db + jnp.where(t_idx == c_len - 1, dbl, 0.0)
            dg = _tri_sum(_tril_bf16(c_len, upper=True), db)
            dqf = dqb * eb + dq_intra
            dkf = dkb2 * ebl + dk_intra
            dq_ref[rs, :] = (dqf * (sq * (1.0 + q * (1.0 - sq)))).astype(BF16)
            df_ref[rs, :] = (sig_neg * (dg * jnp.exp(log_c - log_f) - dkf * (1.0 - lbv) * sig)).astype(BF16)
            di_ref[rs, :] = dvf.astype(BF16)
            dlb_ref[...] += jnp.sum(sig_neg * (dg * jnp.exp(-log_f) - dkf), axis=0, keepdims=True)

    blk = (cps * c_len, HGRN_WIDTH)
    sblk = (cps, c_len, HGRN_WIDTH)
    rev = lambda c: last - c
    step = lambda k: (lambda: pl.program_id(0) == k)
    e_in, e_out, e_shape, e_scr, e_args = _ride_specs(ride)
    return pl.pallas_call(
        _riding(body, 7, 4, 7, ride, step(0), step(n_steps // 2), step(last)),
        name="hgrn_bwd",
        grid=(n_steps,),
        in_specs=[
            pl.BlockSpec(blk, lambda c: (rev(c), col0)),
            pl.BlockSpec(blk, lambda c: (rev(c), col0 + 1)),
            pl.BlockSpec(blk, lambda c: (rev(c), col0 + 2)),
            pl.BlockSpec((2, HGRN_WIDTH), lambda c: (0, 0)),
            pl.BlockSpec(blk, lambda c: (rev(c), 0)),
            pl.BlockSpec((cps, nh, hd, hd), lambda c: (rev(c), 0, 0, 0)),
            pl.BlockSpec(blk, lambda c: (rev(c), 0)),
        ] + e_in,
        out_specs=[
            pl.BlockSpec(blk, lambda c: (rev(c), 0)),
            pl.BlockSpec(blk, lambda c: (rev(c), 0)),
            pl.BlockSpec(blk, lambda c: (rev(c), 0)),
            pl.BlockSpec((1, HGRN_WIDTH), lambda c: (0, 0)),
        ] + e_out,
        out_shape=[jax.ShapeDtypeStruct((s, HGRN_WIDTH), BF16)] * 3 + [jax.ShapeDtypeStruct((1, HGRN_WIDTH), F32)] + e_shape,
        scratch_shapes=[
            pltpu.VMEM((nh, hd, hd), F32),
            pltpu.VMEM(sblk, F32),
            pltpu.VMEM(sblk, F32),
            pltpu.VMEM(sblk, F32),
            pltpu.VMEM((cps, nh, c_len, hd), F32),
            pltpu.VMEM(sblk, F32),
            pltpu.VMEM(sblk, F32),
        ] + e_scr,
        compiler_params=_params(ride, dimension_semantics=("arbitrary",)),
    )(proj, proj, proj, lb, d_o, states, a_mat, *e_args)


def _per_head_lanes(x):
    lane = lax.broadcasted_iota(jnp.int32, (x.shape[0], LANES), 1)
    out = jnp.zeros((x.shape[0], LANES), F32)
    for h in range(ATTN_HEADS):
        out = jnp.where(lane == h, x[:, h * ATTN_HEAD_DIM:h * ATTN_HEAD_DIM + 1], out)
    return out


def _row_spec(tm, width, col=0):
    return pl.BlockSpec((tm, width), lambda i: (i, col))


def _const_spec(width):
    return pl.BlockSpec((1, width), lambda i: (0, 0))


def _acc_rows(ref, value):
    @pl.when(pl.program_id(0) == 0)
    def _():
        ref[...] = jnp.zeros_like(ref)

    ref[...] += jnp.sum(value, axis=0, keepdims=True)


def mix_fwd(attn_parts, o_h, proj, an, hn, w_out_b, gp, x, ride=None):
    s = x.shape[0]
    tm = TOKEN_TILE
    gate_col = 3
    hd = HGRN_HEAD_DIM
    nd = len(DILATIONS)

    def body(*refs):
        o_refs, l_refs = refs[:nd], refs[nd:2 * nd]
        oh_ref, gate_ref, an_ref, hn_ref, w_ref, gp_ref, x_ref = refs[2 * nd:2 * nd + 7]
        x1_ref, cat_ref, mixed_ref, attn_ref = refs[2 * nd + 7:2 * nd + 11]
        lse_refs = refs[2 * nd + 11:3 * nd + 11]
        o_scr, l_scr, lse_scr = refs[3 * nd + 11:]
        os_ = [_from_dilated(r, o_scr.at[k], d, tm) for k, (r, d) in enumerate(zip(o_refs, DILATIONS))]
        ls = [_from_dilated(r, l_scr.at[k], d, tm) for k, (r, d) in enumerate(zip(l_refs, DILATIONS))]
        m = jnp.maximum(jnp.maximum(ls[0], ls[1]), ls[2])
        es = [jnp.exp(l - m) for l in ls]
        den = es[0] + es[1] + es[2]
        attn = (es[0] * os_[0] + es[1] * os_[1] + es[2] * os_[2]) / den
        attn_ref[...] = attn
        lse_scr[0] = _per_head_lanes(m + jnp.log(den))
        for d, ref in zip(DILATIONS, lse_refs):
            _to_dilated(lse_scr, ref, d, tm)
        cat_ref[:, :ATTN_WIDTH] = _rms_fwd(attn, an_ref[...], ATTN_WIDTH).astype(BF16)
        gate = gate_ref[...]
        silu_g = gate * _sigmoid(gate)
        for h in range(HGRN_HEADS):
            hs = slice(h * hd, (h + 1) * hd)
            rec = _rms_fwd(oh_ref[:, hs], hn_ref[:, hs], hd) * silu_g[:, hs]
            cat_ref[:, ATTN_WIDTH + h * hd:ATTN_WIDTH + (h + 1) * hd] = rec.astype(BF16)
        mixed = _dot(cat_ref[...], w_ref[...])
        mixed_ref[...] = mixed
        x1_ref[...] = x_ref[...] + _rms_fwd(mixed, gp_ref[...], D_MODEL)

    aw = ATTN_WIDTH
    n_steps = s // tm
    step = lambda k: (lambda: pl.program_id(0) == k)
    e_in, e_out, e_shape, e_scr, e_args = _ride_specs(ride)
    return pl.pallas_call(
        _riding(body, 2 * nd + 7, 4 + nd, 3, ride, step(0), step((13 * n_steps) // 16), step(n_steps - 1)),
        name="mix_fwd",
        grid=(n_steps,),
        in_specs=[_dilated_spec(d, tm, aw) for d in DILATIONS] * 2 + [
            _row_spec(tm, aw), _row_spec(tm, aw, gate_col), _const_spec(aw), _const_spec(aw), _vmem_spec(),
            _const_spec(D_MODEL), _row_spec(tm, D_MODEL)] + e_in,
        out_specs=[_row_spec(tm, D_MODEL), _row_spec(tm, D_MODEL), _row_spec(tm, D_MODEL), _row_spec(tm, aw)] + [
            _dilated_spec(d, tm, LANES) for d in DILATIONS] + e_out,
        out_shape=[
            jax.ShapeDtypeStruct((s, D_MODEL), F32),
            jax.ShapeDtypeStruct((s, D_MODEL), BF16),
            jax.ShapeDtypeStruct((s, D_MODEL), F32),
            jax.ShapeDtypeStruct((s, aw), F32),
        ] + [jax.ShapeDtypeStruct((d, s // d, LANES), F32) for d in DILATIONS] + e_shape,
        scratch_shapes=[pltpu.VMEM((nd, aw // LANES, tm, LANES), F32), pltpu.VMEM((nd, aw // LANES, tm, LANES), F32),
                        pltpu.VMEM((1, tm, LANES), F32)] + e_scr,
        compiler_params=_params(ride, dimension_semantics=("arbitrary",)),
    )(*[p[0] for p in attn_parts], *[p[1] for p in attn_parts], o_h, proj, an, hn, w_out_b, gp, x, *e_args)


def mix_bwd(dx1, mixed, gp, w_out_b, attn, an, o_h, proj, hn):
    s = dx1.shape[0]
    tm = TOKEN_TILE
    gate_col = 3
    hd = HGRN_HEAD_DIM
    aw = ATTN_WIDTH

    nd = len(DILATIONS)

    def body(*refs):
        dx1_ref, mixed_ref, gp_ref, w_ref, attn_ref, an_ref, oh_ref, gate_ref, hn_ref, dmix_ref = refs[:10]
        do_refs, delta_refs = refs[10:10 + nd], refs[10 + nd:10 + 2 * nd]
        doh_ref, dgate_ref, dgp_ref, dan_ref, dhn_ref, do_ref, delta_ref = refs[10 + 2 * nd:]
        dmixed, gp_c = _rms_bwd(dx1_ref[...], mixed_ref[...], gp_ref[...], D_MODEL)
        _acc_rows(dgp_ref, gp_c)
        dmixed_b = dmixed.astype(BF16)
        dmix_ref[...] = dmixed_b
        dcat = _dot_nt(dmixed_b, w_ref[...])
        attn = attn_ref[...]
        d_o, an_c = _rms_bwd(dcat[:, :aw], attn, an_ref[...], aw)
        _acc_rows(dan_ref, an_c)
        _lane_blocks(do_ref, d_o)
        prod = d_o * attn
        lane = lax.broadcasted_iota(jnp.int32, (tm, LANES), 1)
        delta = jnp.zeros((tm, LANES), F32)
        for pair in range(ATTN_HEADS // 2):
            pp = prod[:, pair * LANES:(pair + 1) * LANES]
            low = _lane_half((tm, LANES), 0)
            lo = jnp.sum(jnp.where(low, pp, 0.0), axis=-1, keepdims=True)
            hi = jnp.sum(jnp.where(low, 0.0, pp), axis=-1, keepdims=True)
            delta = jnp.where(lane == 2 * pair, lo, jnp.where(lane == 2 * pair + 1, hi, delta))
        delta_ref[0] = delta
        for d, o_ref, l_ref in zip(DILATIONS, do_refs, delta_refs):
            _to_dilated(do_ref, o_ref, d, tm, cast=BF16)
            _to_dilated(delta_ref, l_ref, d, tm)
        gate = gate_ref[...]
        sg = _sigmoid(gate)
        silu_g = gate * sg
        drec = dcat[:, aw:]
        hn_parts = []
        for h in range(HGRN_HEADS):
            hs = slice(h * hd, (h + 1) * hd)
            oh = oh_ref[:, hs]
            on = _rms_fwd(oh, hn_ref[:, hs], hd)
            dgate_ref[:, hs] = (drec[:, hs] * on * (sg[:, hs] * (1.0 + gate[:, hs] * (1.0 - sg[:, hs])))).astype(BF16)
            d_oh, hn_c = _rms_bwd(drec[:, hs] * silu_g[:, hs], oh, hn_ref[:, hs], hd)
            doh_ref[:, hs] = d_oh
            hn_parts.append(hn_c)
        _acc_rows(dhn_ref, jnp.concatenate(hn_parts, axis=1))

    return pl.pallas_call(
        body,
        name="mix_bwd",
        grid=(s // tm,),
        in_specs=[_row_spec(tm, D_MODEL), _row_spec(tm, D_MODEL), _const_spec(D_MODEL), _vmem_spec(), _row_spec(tm, aw),
                  _const_spec(aw), _row_spec(tm, aw), _row_spec(tm, aw, gate_col), _const_spec(aw)],
        out_specs=[_row_spec(tm, D_MODEL)] + [_dilated_spec(d, tm, aw) for d in DILATIONS] + [
            _dilated_spec(d, tm, LANES) for d in DILATIONS] + [_row_spec(tm, aw)] * 2 + [
            _const_spec(D_MODEL), _const_spec(aw), _const_spec(aw)],
        out_shape=[jax.ShapeDtypeStruct((s, D_MODEL), BF16)] + [
            jax.ShapeDtypeStruct((d, s // d, aw), BF16) for d in DILATIONS] + [
            jax.ShapeDtypeStruct((d, s // d, LANES), F32) for d in DILATIONS] + [
            jax.ShapeDtypeStruct((s, aw), F32), jax.ShapeDtypeStruct((s, aw), BF16),
            jax.ShapeDtypeStruct((1, D_MODEL), F32), jax.ShapeDtypeStruct((1, aw), F32),
            jax.ShapeDtypeStruct((1, aw), F32)],
        scratch_shapes=[pltpu.VMEM((aw // LANES, tm, LANES), F32), pltpu.VMEM((1, tm, LANES), F32)],
        compiler_params=_params(dimension_semantics=("arbitrary",)),
    )(dx1, mixed, gp, w_out_b, attn, an, o_h, proj, hn)


def mlp_fwd_bwd(x1, g_pre, w1_blocks, w2_halves, g_post, target):
    s = x1.shape[0]
    tm = MLP_TILE
    nblk, _, fb = w1_blocks.shape
    half = fb // 2

    def body(x1_ref, gpre_ref, w1_ref, w2a_ref, w2b_ref, gpost_ref, t_ref,
             dx1_ref, h2_ref, a_ref, du_ref, dff_ref, loss_ref, dgpre_ref, dgpost_ref, u_ref):
        x1v = x1_ref[...]
        h2 = _rms_fwd(x1v, gpre_ref[...], D_MODEL).astype(BF16)
        h2_ref[...] = h2
        ff = jnp.zeros((tm, D_MODEL), F32)
        for j in range(nblk):
            cols = slice(j * fb, (j + 1) * fb)
            ru = jnp.maximum(_dot(h2, w1_ref[j]), 0.0)
            u_ref[:, cols] = ru.astype(BF16)
            a = (ru * ru).astype(BF16)
            a_ref[:, cols] = a
            ff = ff + _dot(a[:, :half], w2a_ref[j]) + _dot(a[:, half:], w2b_ref[j])
        diff = x1v + _rms_fwd(ff, gpost_ref[...], D_MODEL) - t_ref[...]
        _acc_rows(loss_ref, diff * diff)
        dy = diff * (1.0 / D_MODEL)
        dff, gpost_c = _rms_bwd(dy, ff, gpost_ref[...], D_MODEL)
        _acc_rows(dgpost_ref, gpost_c)
        dff_b = dff.astype(BF16)
        dff_ref[...] = dff_b
        dh2 = jnp.zeros((tm, D_MODEL), F32)
        for j in range(nblk):
            cols = slice(j * fb, (j + 1) * fb)
            da = jnp.concatenate([_dot_nt(dff_b, w2a_ref[j]), _dot_nt(dff_b, w2b_ref[j])], axis=1)
            du = (da * (2.0 * u_ref[:, cols])).astype(BF16)
            du_ref[:, cols] = du
            dh2 = dh2 + _dot_nt(du, w1_ref[j])
        dxa, gpre_c = _rms_bwd(dh2, x1v, gpre_ref[...], D_MODEL)
        _acc_rows(dgpre_ref, gpre_c)
        dx1_ref[...] = dy + dxa

    dm = D_MODEL
    return pl.pallas_call(
        body,
        name="mlp_fwd_bwd",
        grid=(s // tm,),
        in_specs=[_row_spec(tm, dm), _const_spec(dm), _vmem_spec(), _vmem_spec(), _vmem_spec(), _const_spec(dm),
                  _row_spec(tm, dm)],
        out_specs=[_row_spec(tm, dm), _row_spec(tm, dm), _row_spec(tm, D_FF), _row_spec(tm, D_FF), _row_spec(tm, dm),
                   _const_spec(dm), _const_spec(dm), _const_spec(dm)],
        out_shape=[
            jax.ShapeDtypeStruct((s, dm), F32),
            jax.ShapeDtypeStruct((s, dm), BF16),
            jax.ShapeDtypeStruct((s, D_FF), BF16),
            jax.ShapeDtypeStruct((s, D_FF), BF16),
            jax.ShapeDtypeStruct((s, dm), BF16),
            jax.ShapeDtypeStruct((1, dm), F32),
            jax.ShapeDtypeStruct((1, dm), F32),
            jax.ShapeDtypeStruct((1, dm), F32),
        ],
        scratch_shapes=[pltpu.VMEM((tm, D_FF), BF16)],
        compiler_params=_params(dimension_semantics=("arbitrary",)),
    )(x1, g_pre, w1_blocks, *w2_halves, g_post, target)


def in_proj_bwd(attn_grads, hgrn_grads, dgate, w_in_b, x, g1, dx1):
    s = x.shape[0]
    tm = PROJ_TILE
    aw = ATTN_WIDTH
    n_attn = len(attn_grads)
    flat = [g[k] for k in range(3) for g in attn_grads] + list(hgrn_grads) + [dgate]

    def body(*refs):
        parts = refs[:len(flat)]
        w_ref, x_ref, g_ref, dx1_ref, dx_ref, dproj_ref, dg_ref, scr = refs[len(flat):]
        groups = []
        for k in range(3):
            acc = None
            for p, d in zip(parts[k * n_attn:(k + 1) * n_attn], DILATIONS):
                v = _from_dilated(p, scr, d, tm)
                acc = v if acc is None else acc + v
            groups.append(acc)
        groups += [p[...] for p in parts[3 * n_attn:]]
        dh = jnp.zeros((tm, D_MODEL), F32)
        for gi, grp in enumerate(groups):
            cols = slice(gi * aw, (gi + 1) * aw)
            gb = grp.astype(BF16)
            dproj_ref[:, cols] = gb
            dh = dh + _dot_nt(gb, w_ref[:, cols])
        dxa, g_c = _rms_bwd(dh, x_ref[...], g_ref[...], D_MODEL)
        _acc_rows(dg_ref, g_c)
        dx_ref[...] = dx1_ref[...] + dxa

    dm = D_MODEL
    return pl.pallas_call(
        body,
        name="in_proj_bwd",
        grid=(s // tm,),
        in_specs=[_dilated_spec(d, tm, aw) for d in DILATIONS] * 3 + [_row_spec(tm, aw)] * 4 + [
            _vmem_spec(), _row_spec(tm, dm), _const_spec(dm), _row_spec(tm, dm)],
        out_specs=[_row_spec(tm, dm), _row_spec(tm, IN_PROJ_WIDTH), _const_spec(dm)],
        out_shape=[jax.ShapeDtypeStruct((s, dm), F32), jax.ShapeDtypeStruct((s, IN_PROJ_WIDTH), BF16),
                   jax.ShapeDtypeStruct((1, dm), F32)],
        scratch_shapes=[pltpu.VMEM((aw // LANES, tm, LANES), F32)],
        compiler_params=_params(dimension_semantics=("arbitrary",)),
    )(*flat, w_in_b, x, g1, dx1)


def wgrad(a_b, b_b, tn, name, ts=2048, per_step=1, ride=None):
    s, k = a_b.shape
    n = b_b.shape[1]

    def body(a_ref, b_ref, o_ref):
        @pl.when(pl.program_id(1) == 0)
        def _():
            o_ref[...] = jnp.zeros_like(o_ref)

        a = a_ref[...]
        for jj in range(per_step):
            o_ref[jj] += _dot_tn(a, b_ref[:, jj * tn:(jj + 1) * tn])

    wide = tn * per_step
    gn, gs = n // wide, s // ts
    step = lambda j, i: (lambda: (pl.program_id(0) == j) & (pl.program_id(1) == i))
    e_in, e_out, e_shape, e_scr, e_args = _ride_specs(ride)
    out = pl.pallas_call(
        _riding(body, 2, 1, 0, ride, step(0, 0), step(gn // 2, 0), step(gn - 1, gs - 1)),
        name=name,
        grid=(gn, gs),
        in_specs=[pl.BlockSpec((ts, k), lambda j, i: (i, 0)), pl.BlockSpec((ts, wide), lambda j, i: (i, j))] + e_in,
        out_specs=[pl.BlockSpec((per_step, k, tn), lambda j, i: (j, 0, 0))] + e_out,
        out_shape=[jax.ShapeDtypeStruct((n // tn, k, tn), F32)] + e_shape,
        scratch_shapes=e_scr,
        compiler_params=_params(ride, dimension_semantics=("arbitrary", "arbitrary")),
    )(a_b, b_b, *e_args)
    return out[0] if ride is None else out


def train_step(x, target, g1, an, logits, hn, gp, g_pre, g_post, w, m, v):
    nd = len(DILATIONS)
    shard_b = {k: w[k].astype(BF16) for k in BIG}
    (w_in_g,) = run_exchange(gather_exchange([shard_b["w_in"]]), "gather_w_in")
    w_in_b = w_in_g.transpose(1, 0, 2).reshape(D_MODEL, IN_PROJ_WIDTH)

    proj, h_b, *qkvs = in_proj_fwd(x, g1, w_in_b)
    rows = shard_b["w_ff2"].shape[0] // 2
    attn_parts, w2_halves = [], []
    for k, (qkv, d) in enumerate(zip(qkvs, DILATIONS)):
        if k < 2:
            *part, w2_half = attn_fwd(qkv, d, ride=gather_exchange([shard_b["w_ff2"][k * rows:(k + 1) * rows]]))
            w2_halves.append(w2_half)
        else:
            part = attn_fwd(qkv, d)
        attn_parts.append(part)
    o_h, states, a_mat, w_out_g, w1_blocks = hgrn_fwd(
        proj, logits, ride=gather_exchange([shard_b["w_out"], shard_b["w_ff1"]]))
    w_out_b = w_out_g.reshape(D_MODEL, D_MODEL)
    x1, cat_b, mixed, attn, *lses = mix_fwd(attn_parts, o_h, proj, an, hn, w_out_b, gp, x)
    dx1, h2_b, a_b, du_b, dff_b, loss_vec, dg_pre, dg_post = mlp_fwd_bwd(x1, g_pre, w1_blocks, w2_halves, g_post,
                                                                         target)
    dw2 = wgrad(a_b, dff_b, D_MODEL, "wgrad_ff2", ts=512)
    dw1 = wgrad(h2_b, du_b, D_FF // N_DEV, "wgrad_ff1", per_step=2)
    dmix_b, *rest = mix_bwd(dx1, mixed, gp, w_out_b, attn, an, o_h, proj, hn)
    d_os, deltas = rest[:nd], rest[nd:2 * nd]
    d_oh, dgate, dgp, dan, dhn = rest[2 * nd:]
    dwout = wgrad(cat_b, dmix_b, D_MODEL, "wgrad_out")

    early = ("w_out", "w_ff1", "w_ff2")
    early_grads = [dwout.reshape(N_DEV, D_MODEL // N_DEV, D_MODEL), dw1, dw2.reshape(N_DEV, D_FF // N_DEV, D_MODEL)]
    res = attn_bwd(qkvs[0], d_os[0], lses[0], deltas[0], DILATIONS[0], ride=to_core_exchange(early_grads))
    pairs = [pair_sum(g, s, f"pair_sum_{name}") for g, s, name in zip(early_grads, res[3:], early)]
    attn_grads = [res[:3]]
    *res, others_ff2 = attn_bwd(qkvs[1], d_os[1], lses[1], deltas[1], DILATIONS[1],
                                ride=to_chip_exchange([pairs[2][1]]))
    attn_grads.append(res)
    attn_grads.append(attn_bwd(qkvs[2], d_os[2], lses[2], deltas[2], DILATIONS[2]))
    dq_h, df_h, di_h, dlb, *others = hgrn_bwd(proj, logits, d_oh, states, a_mat,
                                              ride=to_chip_exchange([pairs[0][1], pairs[1][1]]))
    others.append(others_ff2)
    dx, dproj_b, dg1 = in_proj_bwd(attn_grads, (dq_h, df_h, di_h), dgate, w_in_b, x, g1, dx1)
    packed = _pack_small(dg1, dgp, dg_pre, dg_post, dan, dhn, dlb, loss_vec)
    dwin, small_slots = wgrad(h_b, dproj_b, 2 * IN_PROJ_WIDTH // N_DEV, "wgrad_in",
                              ride=small_exchange(packed))
    big = {name: sum_adamw(p[0], o, w[name], m[name], v[name], f"sum_adamw_{name}")
           for name, p, o in zip(early, pairs, others)}

    shard_w = IN_PROJ_WIDTH // N_DEV
    dwin_blocks = dwin.reshape(N_DEV // 2, D_MODEL, 2, shard_w).transpose(0, 2, 1, 3).reshape(N_DEV, D_MODEL, shard_w)
    (from_sibling,) = run_exchange(to_core_exchange([dwin_blocks.astype(BF16)]), "reduce_w_in_to_core")
    pair_in, pair_in_b = pair_sum(dwin_blocks, from_sibling, "pair_sum_w_in")
    (others_in,) = run_exchange(to_chip_exchange([pair_in_b]), "reduce_w_in_to_chip")
    big["w_in"] = sum_adamw(pair_in, others_in, w["w_in"], m["w_in"], v["w_in"], "sum_adamw_w_in")
    return dx, big, small_slots


def _position():
    x, y, c = lax.axis_index("x"), lax.axis_index("y"), lax.axis_index("c")
    other_chips = [(1 - x, y), (x, 1 - y), (1 - x, 1 - y)]
    return x, y, c, other_chips


def _any_spec():
    return pl.BlockSpec(memory_space=pl.ANY)


class Exchange:
    def __init__(self, arrays, out_shape, sems, stages, collective_id, peers):
        self.arrays, self.out_shape, self.sems, self.stages = list(arrays), list(out_shape), list(sems), stages
        self.collective_id, self.peers = collective_id, peers

    def open(self):
        barrier = pltpu.get_barrier_semaphore()
        peers = self.peers()
        for peer in peers:
            pl.semaphore_signal(barrier, inc=1, device_id=peer, device_id_type=MESH)
        pl.semaphore_wait(barrier, len(peers))


def _siblings():
    x, y, c, _ = _position()
    return [(x, y, 1 - c)]


def _same_core_of_other_chips():
    x, y, c, chips = _position()
    return [(px, py, c) for px, py in chips]


def _gather_peers():
    return _siblings() + _same_core_of_other_chips()


def _all_others():
    x, y, c, _ = _position()
    return [(1 - x if rel & 4 else x, 1 - y if rel & 2 else y, 1 - c if rel & 1 else c) for rel in range(1, N_DEV)]


def gather_exchange(shards):
    n = len(shards)

    def stages(ins, outs, sems):
        send_sems, recv_sems, local_sems = sems

        def parts():
            x, y, c, chips = _position()
            me, sibling = (x, y, c), (x, y, 1 - c)

            def slot(a, px, py, pc):
                return outs[a].at[4 * px + 2 * py + pc]

            def copy(a, k, block, to, src=None):
                return pltpu.make_async_remote_copy(
                    src_ref=slot(a, *block) if src is None else src, dst_ref=slot(a, *block),
                    send_sem=send_sems.at[a, k], recv_sem=recv_sems.at[a, k], device_id=to, device_id_type=MESH)

            local = [pltpu.make_async_copy(ins[a], slot(a, *me), local_sems.at[a]) for a in range(n)]
            first = []
            for a in range(n):
                first.append(copy(a, 0, me, sibling, src=ins[a]))
                first += [copy(a, 1 + j, me, (*chip, c), src=ins[a]) for j, chip in enumerate(chips)]
            passed = [copy(a, 4 + j, (*chip, c), sibling) for j, chip in enumerate(chips) for a in range(n)]
            return c, chips, me, sibling, copy, local, first, passed

        def begin():
            _, _, _, _, _, local, first, _ = parts()
            for cp in local + first:
                cp.start()

        def middle():
            c, chips, me, _, copy, _, _, passed = parts()
            k = 0
            for j, chip in enumerate(chips):
                for a in range(n):
                    copy(a, 1 + j, (*chip, c), me).wait_recv()
                    passed[k].start()
                    k += 1

        def end():
            c, chips, me, sibling, copy, local, first, passed = parts()
            for a in range(n):
                copy(a, 0, sibling, me).wait_recv()
                for j, chip in enumerate(chips):
                    copy(a, 4 + j, (*chip, 1 - c), me).wait_recv()
            for cp in first + passed:
                cp.wait_send()
            for cp in local:
                cp.wait()

        return begin, middle, end

    return Exchange(
        shards, [jax.ShapeDtypeStruct((N_DEV,) + sh.shape, sh.dtype) for sh in shards],
        [pltpu.SemaphoreType.DMA((n, 7)), pltpu.SemaphoreType.DMA((n, 7)), pltpu.SemaphoreType.DMA((n,))], stages,
        collective_id=0, peers=_gather_peers)


def to_core_exchange(grads):
    n = len(grads)

    def stages(ins, outs, sems):
        send_sems, recv_sems = sems

        def copies():
            x, y, c, _ = _position()
            return [pltpu.make_async_remote_copy(
                src_ref=ins[a].at[2 * q + (1 - c)], dst_ref=outs[a].at[q], send_sem=send_sems.at[a, q],
                recv_sem=recv_sems.at[a, q], device_id=(x, y, 1 - c), device_id_type=MESH)
                for a in range(n) for q in range(4)]

        def begin():
            for cp in copies():
                cp.start()

        def end():
            for cp in copies():
                cp.wait()

        return begin, None, end

    return Exchange(grads, [jax.ShapeDtypeStruct((4,) + g.shape[1:], g.dtype) for g in grads],
                    [pltpu.SemaphoreType.DMA((n, 4)), pltpu.SemaphoreType.DMA((n, 4))], stages,
                    collective_id=1, peers=_siblings)


def pair_sum(grad, from_sibling, name):
    _, r, cdim = grad.shape
    tr = min(r, ELEMENTWISE_ROWS)
    c_idx = lax.axis_index("c").astype(jnp.int32).reshape(1)

    def body(c_ref, g_ref, s_ref, o_ref, ob_ref):
        total = g_ref[...] + s_ref[...]
        o_ref[...] = total
        ob_ref[...] = total.astype(BF16)

    blk = lambda: pl.BlockSpec((1, tr, cdim), lambda q, i, cr: (q, i, 0))
    return pl.pallas_call(
        body,
        name=name,
        grid_spec=pltpu.PrefetchScalarGridSpec(
            num_scalar_prefetch=1,
            grid=(4, r // tr),
            in_specs=[pl.BlockSpec((1, tr, cdim), lambda q, i, cr: (2 * q + cr[0], i, 0)), blk()],
            out_specs=[blk(), blk()],
        ),
        out_shape=[jax.ShapeDtypeStruct((4, r, cdim), F32), jax.ShapeDtypeStruct((4, r, cdim), BF16)],
        compiler_params=_params(dimension_semantics=("arbitrary", "arbitrary")),
    )(c_idx, grad, from_sibling)


def to_chip_exchange(pairs):
    n = len(pairs)

    def stages(ins, outs, sems):
        send_sems, recv_sems = sems

        def copies():
            x, y, c, chips = _position()
            return [pltpu.make_async_remote_copy(
                src_ref=ins[a].at[2 * px + py], dst_ref=outs[a].at[j], send_sem=send_sems.at[a, j],
                recv_sem=recv_sems.at[a, j], device_id=(px, py, c), device_id_type=MESH)
                for a in range(n) for j, (px, py) in enumerate(chips)]

        def begin():
            for cp in copies():
                cp.start()

        def end():
            for cp in copies():
                cp.wait()

        return begin, None, end

    return Exchange(pairs, [jax.ShapeDtypeStruct((3,) + p.shape[1:], p.dtype) for p in pairs],
                    [pltpu.SemaphoreType.DMA((n, 3)), pltpu.SemaphoreType.DMA((n, 3))], stages,
                    collective_id=2, peers=_same_core_of_other_chips)


def run_exchange(ex, name):
    n_in, n_out = len(ex.arrays), len(ex.out_shape)

    def body(*refs):
        begin, middle, end = ex.stages(refs[:n_in], refs[n_in:n_in + n_out], refs[n_in + n_out:])
        ex.open()
        begin()
        if middle is not None:
            middle()
        end()

    return pl.pallas_call(
        body,
        name=name,
        in_specs=[_any_spec()] * n_in,
        out_specs=[_any_spec()] * n_out,
        out_shape=ex.out_shape,
        scratch_shapes=ex.sems,
        compiler_params=pltpu.CompilerParams(collective_id=ex.collective_id),
    )(*ex.arrays)


def _riding(body, n_in, n_out, n_scratch, ex, first, middle, last):
    if ex is None:
        return body
    r_in, r_out = len(ex.arrays), len(ex.out_shape)

    def wrapped(*refs):
        k_in, refs = refs[:n_in], refs[n_in:]
        e_in, refs = refs[:r_in], refs[r_in:]
        k_out, refs = refs[:n_out], refs[n_out:]
        e_out, refs = refs[:r_out], refs[r_out:]
        k_scr, e_sems = refs[:n_scratch], refs[n_scratch:]
        begin, mid, end = ex.stages(e_in, e_out, e_sems)

        @pl.when(first())
        def _():
            ex.open()
            begin()

        body(*k_in, *k_out, *k_scr)
        if mid is not None:
            pl.when(middle())(mid)
        pl.when(last())(end)

    return wrapped


def _ride_specs(ex):
    if ex is None:
        return [], [], [], [], []
    return [_any_spec()] * len(ex.arrays), [_any_spec()] * len(ex.out_shape), ex.out_shape, ex.sems, ex.arrays


def _adamw(w, g, m, v):
    m = ADAM_B1 * m + (1.0 - ADAM_B1) * g
    v = ADAM_B2 * v + (1.0 - ADAM_B2) * (g * g)
    m_hat = m / (1.0 - ADAM_B1 ** ADAM_STEP)
    v_hat = v / (1.0 - ADAM_B2 ** ADAM_STEP)
    delta = -ADAM_LR * (m_hat / (jnp.sqrt(v_hat) + ADAM_EPS) + ADAM_WD * w)
    return delta, m, v


def sum_adamw(pairs, others, w, m, v, name):
    r, cdim = w.shape
    tr = min(r, ELEMENTWISE_ROWS // 2)
    chip_idx =(2 * lax.axis_index("x") + lax.axis_index("y")).astype(jnp.int32).reshape(1)

    def body(q_ref, p_ref, o_ref, w_ref, m_ref, v_ref, g_out, d_out, m_out, v_out):
        g = p_ref[0] + o_ref[0].astype(F32) + o_ref[1].astype(F32) + o_ref[2].astype(F32)
        g_out[...] = g
        d_out[...], m_out[...], v_out[...] = _adamw(w_ref[...], g, m_ref[...], v_ref[...])

    tile = lambda: pl.BlockSpec((tr, cdim), lambda i, qr: (i, 0))
    return pl.pallas_call(
        body,
        name=name,
        grid_spec=pltpu.PrefetchScalarGridSpec(
            num_scalar_prefetch=1,
            grid=(r // tr,),
            in_specs=[pl.BlockSpec((1, tr, cdim), lambda i, qr: (qr[0], i, 0)),
                      pl.BlockSpec((3, tr, cdim), lambda i, qr: (0, i, 0)), tile(), tile(), tile()],
            out_specs=[tile(), tile(), tile(), tile()],
        ),
        out_shape=[jax.ShapeDtypeStruct((r, cdim), F32)] * 4,
        compiler_params=_params(dimension_semantics=("arbitrary",)),
    )(chip_idx, pairs, others, w, m, v)


def small_exchange(packed):
    def stages(ins, outs, sems):
        send_sems, recv_sems, local_sem = sems
        (src,), (slots,) = ins, outs

        def copies():
            x, y, c, _ = _position()
            my_id = 4 * x + 2 * y + c
            sends, landings = [], []
            for rel in range(1, N_DEV):
                px = 1 - x if (rel >> 2) & 1 else x
                py = 1 - y if (rel >> 1) & 1 else y
                pc = 1 - c if rel & 1 else c
                peer = dict(send_sem=send_sems.at[rel - 1], recv_sem=recv_sems.at[rel - 1], device_id=(px, py, pc),
                            device_id_type=MESH)
                sends.append(pltpu.make_async_remote_copy(src_ref=src, dst_ref=slots.at[my_id], **peer))
                landings.append(pltpu.make_async_remote_copy(src_ref=src, dst_ref=slots.at[4 * px + 2 * py + pc], **peer))
            return pltpu.make_async_copy(src, slots.at[my_id], local_sem), sends, landings

        def begin():
            local, sends, _ = copies()
            local.start()
            for cp in sends:
                cp.start()

        def end():
            local, sends, landings = copies()
            for cp in landings:
                cp.wait_recv()
            for cp in sends:
                cp.wait_send()
            local.wait()

        return begin, None, end

    return Exchange([packed], [jax.ShapeDtypeStruct((N_DEV,) + packed.shape, packed.dtype)],
                    [pltpu.SemaphoreType.DMA((N_DEV - 1,)), pltpu.SemaphoreType.DMA((N_DEV - 1,)),
                     pltpu.SemaphoreType.DMA(())], stages, collective_id=3, peers=_all_others)


def small_adamw(slots, w, m, v):
    def body(r_ref, w_ref, m_ref, v_ref, g_out, d_out, m_out, v_out, loss_out):
        red = r_ref[0]
        for k in range(1, N_DEV):
            red = red + r_ref[k]
        wv = w_ref[...]
        lb = _lower_bound(jnp.concatenate([wv[5:6, :HGRN_WIDTH], wv[5:6, HGRN_WIDTH:]], axis=0))
        t = red[5:6, :HGRN_WIDTH] * lb * (1.0 - lb)
        row = lax.broadcasted_iota(jnp.int32, red.shape, 0)
        g = jnp.where(row == 5, jnp.concatenate([t, -t], axis=1), jnp.where(row >= 6, 0.0, red))
        g_out[...] = g
        d_out[...], m_out[...], v_out[...] = _adamw(wv, g, m_ref[...], v_ref[...])
        loss = jnp.sum(red[6:7, :], axis=-1, keepdims=True) * (0.5 / D_MODEL)
        loss_out[...] = jnp.broadcast_to(loss, loss_out.shape)

    return pl.pallas_call(
        body,
        name="small_adamw",
        in_specs=[_vmem_spec()] * 4,
        out_specs=[_vmem_spec()] * 5,
        out_shape=[jax.ShapeDtypeStruct(w.shape, F32)] * 4 + [jax.ShapeDtypeStruct((SUBLANES, LANES), F32)],
    )(slots, w, m, v)


def _pack_small(g1, gp, g_pre, g_post, an, hn, logits_or_dlb, extra=None):
    row5 = logits_or_dlb.reshape(1, -1)
    row5 = jnp.pad(row5, ((0, 0), (0, D_MODEL - row5.shape[1])))
    row6 = jnp.zeros((1, D_MODEL), F32) if extra is None else extra
    return jnp.concatenate([g1, gp, g_pre, g_post, jnp.concatenate([an, hn], axis=1), row5, row6,
                            jnp.zeros((1, D_MODEL), F32)], axis=0)


def _unpack_small(p):
    return dict(mix_pre_norm=p[0:1], mix_post_norm=p[1:2], mlp_pre_norm=p[2:3], mlp_post_norm=p[3:4],
                attn_out_norm=p[4:5, :ATTN_WIDTH], hgrn_out_norm=p[4:5, ATTN_WIDTH:],
                hgrn_lb_logits=p[5].reshape(2, HGRN_WIDTH))


BIG = ("w_in", "w_out", "w_ff1", "w_ff2")
ORDER = ("mix_pre_norm", "w_in", "attn_out_norm", "hgrn_lb_logits", "hgrn_out_norm", "w_out", "mix_post_norm",
         "mlp_pre_norm", "w_ff1", "w_ff2", "mlp_post_norm")


def kernel(x, mix_pre_norm, w_in, attn_out_norm, hgrn_lb_logits, hgrn_out_norm, w_out, mix_post_norm, mlp_pre_norm, w_ff1, w_ff2, mlp_post_norm, loss_target, m_mix_pre_norm, m_w_in, m_attn_out_norm, m_hgrn_lb_logits, m_hgrn_out_norm, m_w_out, m_mix_post_norm, m_mlp_pre_norm, m_w_ff1, m_w_ff2, m_mlp_post_norm, v_mix_pre_norm, v_w_in, v_attn_out_norm, v_hgrn_lb_logits, v_hgrn_out_norm, v_w_out, v_mix_post_norm, v_mlp_pre_norm, v_w_ff1, v_w_ff2, v_mlp_post_norm):
    w = dict(w_in=w_in[0], w_out=w_out[0], w_ff1=w_ff1[0], w_ff2=w_ff2[0])
    m = dict(w_in=m_w_in[0], w_out=m_w_out[0], w_ff1=m_w_ff1[0], w_ff2=m_w_ff2[0])
    v = dict(w_in=v_w_in[0], w_out=v_w_out[0], w_ff1=v_w_ff1[0], w_ff2=v_w_ff2[0])

    dx, big, small_slots = train_step(x[0], loss_target[0], mix_pre_norm, attn_out_norm, hgrn_lb_logits, hgrn_out_norm,
                                      mix_post_norm, mlp_pre_norm, mlp_post_norm, w, m, v)

    pack = lambda a, b, c2, d, e, f, g: _pack_small(a, b, c2, d, e, f, g)
    w_s = pack(mix_pre_norm, mix_post_norm, mlp_pre_norm, mlp_post_norm, attn_out_norm, hgrn_out_norm, hgrn_lb_logits)
    m_s = pack(m_mix_pre_norm, m_mix_post_norm, m_mlp_pre_norm, m_mlp_post_norm, m_attn_out_norm, m_hgrn_out_norm,
               m_hgrn_lb_logits)
    v_s = pack(v_mix_pre_norm, v_mix_post_norm, v_mlp_pre_norm, v_mlp_post_norm, v_attn_out_norm, v_hgrn_out_norm,
               v_hgrn_lb_logits)
    g_s, d_s, nm_s, nv_s, loss = small_adamw(small_slots, w_s, m_s, v_s)
    small_out = [_unpack_small(t) for t in (g_s, d_s, nm_s, nv_s)]

    outs = [loss[0, 0], dx[None]]
    for kind in range(4):
        for name in ORDER:
            outs.append(big[name][kind][None] if name in BIG else small_out[kind][name])
    return tuple(outs)
```

```python
import jax
import jax.numpy as jnp
from jax import lax
from jax.experimental import pallas as pl
from jax.experimental.pallas import tpu as pltpu

F32 = jnp.float32
BF16 = jnp.bfloat16

D_MODEL = 1024
ATTN_WIDTH = 512
ATTN_HEAD_DIM = 64
ATTN_HEADS = 8
ATTN_BLOCK = 128
DILATIONS = (1, 4, 16)
HGRN_WIDTH = 512
HGRN_HEADS = 4
HGRN_HEAD_DIM = 128
HGRN_CHUNK = 64
IN_PROJ_WIDTH = 3584
D_FF = 4096
RMS_EPS = 1e-6
N_DEV = 8
ADAM_LR = 0.001
ADAM_B1 = 0.9
ADAM_B2 = 0.999
ADAM_EPS = 1e-08
ADAM_WD = 0.01
ADAM_STEP = 10

SUBLANES = 8
LANES = 128
COLUMN_UNROLL = 8
HGRN_CHUNKS_PER_STEP = 2
SUB_BLOCK = 16
TOKEN_TILE = 512
ELEMENTWISE_ROWS = 1024
MLP_TILE = 256
PROJ_TILE = 512
VMEM_BYTES_V7X = 64 * 1024 * 1024
VMEM_LIMIT = VMEM_BYTES_V7X // 8 * 7
NEG_BIG = -1e30
MESH = pl.DeviceIdType.MESH


def _params(ride=None, **kw):
    if ride is not None:
        kw["collective_id"] = ride.collective_id
    return pltpu.CompilerParams(vmem_limit_bytes=VMEM_LIMIT, **kw)


def _vmem_spec():
    return pl.BlockSpec(memory_space=pltpu.VMEM)


def _dot(a, b):
    return jnp.dot(a, b, preferred_element_type=F32)


def _dot_nt(a, b):
    return lax.dot_general(a, b, (((1,), (1,)), ((), ())), preferred_element_type=F32)


def _dot_tn(a, b):
    return lax.dot_general(a, b, (((0,), (0,)), ((), ())), preferred_element_type=F32)


def _sigmoid(x):
    return 1.0 / (1.0 + jnp.exp(-x))


def _rms_fwd(x, gain, width):
    r = lax.rsqrt(jnp.sum(x * x, axis=-1, keepdims=True) * (1.0 / width) + RMS_EPS)
    return x * r * gain


def _rms_bwd(dy, x, gain, width):
    r = lax.rsqrt(jnp.sum(x * x, axis=-1, keepdims=True) * (1.0 / width) + RMS_EPS)
    xhat = x * r
    dxhat = dy * gain
    dx = r * (dxhat - xhat * (jnp.sum(dxhat * xhat, axis=-1, keepdims=True) * (1.0 / width)))
    return dx, dy * xhat


def _split3(x):
    hi = x.astype(BF16)
    r1 = x - hi.astype(F32)
    mid = r1.astype(BF16)
    lo = (r1 - mid.astype(F32)).astype(BF16)
    return hi, mid, lo


def _tri_sum(tri_bf16, x):
    hi, mid, lo = _split3(x)
    return _dot(tri_bf16, hi) + _dot(tri_bf16, mid) + _dot(tri_bf16, lo)


def _dilated_spec(d, tm, width):
    return pl.BlockSpec((d, tm // d, width), lambda i: (0, i, 0))


def _lane_blocks(ref, value):
    for c in range(ref.shape[0]):
        ref[c] = value[:, c * LANES:(c + 1) * LANES]


def _to_dilated(src_ref, dst_ref, d, tm, cast=None):
    for r in range(d):
        for c in range(src_ref.shape[0]):
            v = src_ref[c] if d == 1 else src_ref[c, pl.ds(r, tm // d, stride=d), :]
            dst_ref[r, :, c * LANES:(c + 1) * LANES] = v if cast is None else v.astype(cast)


def _from_dilated(src_ref, scratch_ref, d, tm):
    if d == 1:
        return src_ref[0].astype(F32)
    nblk = scratch_ref.shape[0]
    for r in range(d):
        for c in range(nblk):
            scratch_ref[c, pl.ds(r, tm // d, stride=d), :] = src_ref[r, :, c * LANES:(c + 1) * LANES].astype(F32)
    return jnp.concatenate([scratch_ref[c] for c in range(nblk)], axis=1)


def in_proj_fwd(x, g1, w_in_b, ride=None):
    s = x.shape[0]
    tm = PROJ_TILE
    qkv_w = 3 * ATTN_WIDTH
    hg_w = IN_PROJ_WIDTH - qkv_w

    def body(x_ref, g_ref, w_ref, hg_ref, h_ref, *rest):
        qkv_refs, qkv_scr = rest[:len(DILATIONS)], rest[len(DILATIONS)]
        h = _rms_fwd(x_ref[...], g_ref[...], D_MODEL).astype(BF16)
        h_ref[...] = h
        proj = _dot(h, w_ref[...])
        hg_ref[...] = proj[:, qkv_w:]
        _lane_blocks(qkv_scr, proj[:, :qkv_w])
        for d, ref in zip(DILATIONS, qkv_refs):
            _to_dilated(qkv_scr, ref, d, tm, cast=BF16)

    n_steps = s // tm
    step = lambda k: (lambda: pl.program_id(0) == k)
    e_in, e_out, e_shape, e_scr, e_args = _ride_specs(ride)
    return pl.pallas_call(
        _riding(body, 3, 2 + len(DILATIONS), 1, ride, step(0), step(n_steps - 2), step(n_steps - 1)),
        name="in_proj_fwd",
        grid=(n_steps,),
        in_specs=[
            pl.BlockSpec((tm, D_MODEL), lambda i: (i, 0)),
            pl.BlockSpec((1, D_MODEL), lambda i: (0, 0)),
            _vmem_spec(),
        ] + e_in,
        out_specs=[
            pl.BlockSpec((tm, hg_w), lambda i: (i, 0)),
            pl.BlockSpec((tm, D_MODEL), lambda i: (i, 0)),
        ] + [_dilated_spec(d, tm, qkv_w) for d in DILATIONS] + e_out,
        out_shape=[jax.ShapeDtypeStruct((s, hg_w), F32), jax.ShapeDtypeStruct((s, D_MODEL), BF16)] + [
            jax.ShapeDtypeStruct((d, s // d, qkv_w), BF16) for d in DILATIONS] + e_shape,
        scratch_shapes=[pltpu.VMEM((qkv_w // LANES, tm, LANES), F32)] + e_scr,
        compiler_params=_params(ride, dimension_semantics=("arbitrary",)),
    )(x, g1, w_in_b, *e_args)


ATTN_SCALE = ATTN_HEAD_DIM ** -0.5


def _fill_attn_bias(bias_ref, dilation):
    qi = lax.broadcasted_iota(jnp.int32, (ATTN_BLOCK, 2 * ATTN_BLOCK), 0)
    kj = lax.broadcasted_iota(jnp.int32, (ATTN_BLOCK, 2 * ATTN_BLOCK), 1)
    dist = qi + ATTN_BLOCK - kj
    valid = (dist >= 0) & (dist <= ATTN_BLOCK)
    for head in range(ATTN_HEADS):
        slope = 2.0 ** (-8.0 * (head + 1) / ATTN_HEADS)
        bias = jnp.where(valid, dist.astype(F32) * (-slope * dilation), NEG_BIG)
        bias_ref[0, head] = bias
        bias_ref[1, head] = jnp.where(kj >= ATTN_BLOCK, bias, NEG_BIG)


def _stack_heads(x):
    low = _lane_half(x.shape, 0)
    zero = jnp.zeros_like(x)
    return jnp.concatenate([jnp.where(low, x, zero), jnp.where(low, zero, x)], axis=0)


def _unstack_heads(y):
    half = y.shape[0] // 2
    return jnp.where(_lane_half((half, y.shape[1]), 0), y[:half], y[half:])


def _attn_scores(q_stack, kcat, bias_ref, pair, first_block):
    f = first_block.astype(jnp.int32)
    bias = jnp.concatenate([bias_ref[f, 2 * pair], bias_ref[f, 2 * pair + 1]], axis=0)
    return _dot_nt(q_stack, kcat) + bias


def _lane_half(shape, sub):
    lane = lax.broadcasted_iota(jnp.int32, shape, 1)
    return (lane < ATTN_HEAD_DIM) if sub == 0 else (lane >= ATTN_HEAD_DIM)


def _sub_block(col, row):
    return pl.BlockSpec((None, ATTN_BLOCK, ATTN_WIDTH), lambda r, n: (r, row(n), col))


def attn_fwd(qkv, dilation):
    d, length, _ = qkv.shape
    assert d == dilation
    nb = length // ATTN_BLOCK

    def body(q_ref, kc_ref, kp_ref, vc_ref, vp_ref, o_ref, lse_ref, bias_ref):
        @pl.when((pl.program_id(0) == 0) & (pl.program_id(1) == 0))
        def _():
            _fill_attn_bias(bias_ref, d)

        first = pl.program_id(1) == 0
        for pair in range(ATTN_HEADS // 2):
            lanes = slice(pair * LANES, (pair + 1) * LANES)
            q_stack = _stack_heads(q_ref[:, lanes] * ATTN_SCALE)
            kcat = jnp.concatenate([kp_ref[:, lanes], kc_ref[:, lanes]], axis=0)
            vcat = jnp.concatenate([vp_ref[:, lanes], vc_ref[:, lanes]], axis=0)
            sc = _attn_scores(q_stack, kcat, bias_ref, pair, first)
            m = jnp.max(sc, axis=-1, keepdims=True)
            p = jnp.exp(sc - m)
            den = jnp.sum(p, axis=-1, keepdims=True)
            o_ref[:, lanes] = _unstack_heads(_dot(p.astype(BF16), vcat) / den).astype(BF16)
            lse_ref[:, lanes] = _unstack_heads(jnp.broadcast_to(m + jnp.log(den), (2 * ATTN_BLOCK, LANES)))

    cur = lambda n: n
    prev = lambda n: jnp.maximum(n - 1, 0)
    return pl.pallas_call(
        body,
        name=f"attn_fwd_d{d}",
        grid=(d, nb),
        in_specs=[_sub_block(0, cur), _sub_block(1, cur), _sub_block(1, prev), _sub_block(2, cur), _sub_block(2, prev)],
        out_specs=[_sub_block(0, cur), _sub_block(0, cur)],
        out_shape=[jax.ShapeDtypeStruct((d, length, ATTN_WIDTH), BF16), jax.ShapeDtypeStruct((d, length, ATTN_WIDTH), F32)],
        scratch_shapes=[pltpu.VMEM((2, ATTN_HEADS, ATTN_BLOCK, 2 * ATTN_BLOCK), F32)],
        compiler_params=_params(dimension_semantics=("arbitrary", "arbitrary")),
    )(qkv, qkv, qkv, qkv, qkv)


def attn_bwd(qkv, d_out, lse, delta, dilation, ride=None):
    d, length, _ = qkv.shape
    assert d == dilation
    nb = length // ATTN_BLOCK

    steps = d * nb + 1

    def body(q_ref, kc_ref, kp_ref, vc_ref, vp_ref, do_ref, lse_ref, dl_ref, dq_ref, dk_ref, dv_ref, ck_ref, cv_ref,
             bias_ref):
        t = pl.program_id(0)

        @pl.when(t == 0)
        def _():
            ck_ref[...] = jnp.zeros_like(ck_ref)
            cv_ref[...] = jnp.zeros_like(cv_ref)
            _fill_attn_bias(bias_ref, d)

        @pl.when(t < steps - 1)
        def _():
            first = t % nb == 0
            for pair in range(ATTN_HEADS // 2):
                lanes = slice(pair * LANES, (pair + 1) * LANES)
                q_stack = _stack_heads(q_ref[:, lanes] * ATTN_SCALE)
                do_stack = _stack_heads(do_ref[:, lanes])
                kcat = jnp.concatenate([kp_ref[:, lanes], kc_ref[:, lanes]], axis=0)
                vcat = jnp.concatenate([vp_ref[:, lanes], vc_ref[:, lanes]], axis=0)
                col_a, col_b = 2 * pair, 2 * pair + 1
                lse_col = jnp.concatenate([lse_ref[:, col_a:col_a + 1], lse_ref[:, col_b:col_b + 1]], axis=0)
                dl_col = jnp.concatenate([dl_ref[:, col_a:col_a + 1], dl_ref[:, col_b:col_b + 1]], axis=0)
                p = jnp.exp(_attn_scores(q_stack, kcat, bias_ref, pair, first) - lse_col)
                ds = (p * (_dot_nt(do_stack, vcat) - dl_col)).astype(BF16)
                dq_ref[:, lanes] = (_unstack_heads(_dot(ds, kcat)) * ATTN_SCALE).astype(BF16)
                dk_cat = _dot_tn(ds, q_stack)
                dv_cat = _dot_tn(p.astype(BF16), do_stack)
                dk_ref[:, lanes] = (ck_ref[:, lanes] + dk_cat[:ATTN_BLOCK]).astype(BF16)
                dv_ref[:, lanes] = (cv_ref[:, lanes] + dv_cat[:ATTN_BLOCK]).astype(BF16)
                ck_ref[:, lanes] = dk_cat[ATTN_BLOCK:]
                cv_ref[:, lanes] = dv_cat[ATTN_BLOCK:]

        @pl.when(t == steps - 1)
        def _():
            dk_ref[...] = ck_ref[...].astype(BF16)
            dv_ref[...] = cv_ref[...].astype(BF16)

    blk = (ATTN_BLOCK, ATTN_WIDTH)

    def spec(col, shift, width=ATTN_WIDTH):
        def index(t):
            f = jnp.minimum(t, steps - 2) if shift > -2 else jnp.maximum(t - 1, 0)
            r, n = f // nb, f % nb
            return (r, jnp.maximum(n - 1, 0) if shift == -1 else n, col)
        return pl.BlockSpec((None, ATTN_BLOCK, width), index)

    step = lambda k: (lambda: pl.program_id(0) == k)
    e_in, e_out, e_shape, e_scr, e_args = _ride_specs(ride)
    return pl.pallas_call(
        _riding(body, 8, 3, 3, ride, step(0), step(steps // 2), step(steps - 1)),
        name=f"attn_bwd_d{d}",
        grid=(steps,),
        in_specs=[spec(0, 0), spec(1, 0), spec(1, -1), spec(2, 0), spec(2, -1), spec(0, 0), spec(0, 0, LANES),
                  spec(0, 0, LANES)] + e_in,
        out_specs=[spec(0, 0), spec(0, -2), spec(0, -2)] + e_out,
        out_shape=[jax.ShapeDtypeStruct((d, length, ATTN_WIDTH), BF16)] * 3 + e_shape,
        scratch_shapes=[pltpu.VMEM(blk, F32), pltpu.VMEM(blk, F32),
                        pltpu.VMEM((2, ATTN_HEADS, ATTN_BLOCK, 2 * ATTN_BLOCK), F32)] + e_scr,
        compiler_params=_params(ride, dimension_semantics=("arbitrary",)),
    )(qkv, qkv, qkv, qkv, qkv, d_out, lse, delta, *e_args)


def _lower_bound(logits):
    return _sigmoid(logits[0:1, :] - logits[1:2, :])


def _hgrn_gates(q, fp, lb):
    sq = _sigmoid(q)
    qf = q * sq
    sig = _sigmoid(fp)
    sig_neg = _sigmoid(-fp)
    kf = (1.0 - lb) * sig_neg
    log_sig = jnp.minimum(fp, 0.0) - jnp.log(1.0 + jnp.exp(-jnp.abs(fp)))
    a = jnp.log(lb)
    c = jnp.log(1.0 - lb) + log_sig
    log_f = jnp.maximum(a, c) + jnp.log(1.0 + jnp.exp(-jnp.abs(a - c)))
    return sq, qf, (sig, sig_neg, c), log_f, kf


def _tril_bf16(n, upper=False):
    r = lax.broadcasted_iota(jnp.int32, (n, n), 0)
    c = lax.broadcasted_iota(jnp.int32, (n, n), 1)
    keep = (c >= r) if upper else (c <= r)
    return jnp.where(keep, 1.0, 0.0).astype(BF16)


def _hgrn_diagonal_loops(c_len, diagonal):
    for half in range(SUB_BLOCK // SUBLANES):
        def step(jj, carry, half=half):
            j = half * SUBLANES + jj
            for i in range(c_len // SUB_BLOCK):
                diagonal(slice(i * SUB_BLOCK + half * SUBLANES, (i + 1) * SUB_BLOCK), j, i * SUB_BLOCK + j)
            return carry

        lax.fori_loop(0, SUBLANES, step, 0, unroll=COLUMN_UNROLL)


def _hgrn_off_diagonal(b, qf, kf):
    c_len, width = b.shape
    edges = [b[0:1, :]] + [b[i * SUB_BLOCK - 1:i * SUB_BLOCK, :] for i in range(1, c_len // SUB_BLOCK)]
    eq = jnp.exp(b - jnp.concatenate([jnp.broadcast_to(e, (SUB_BLOCK, width)) for e in edges], axis=0))
    q_til = qf * eq
    k_til, ek = [], []
    for i in range(1, c_len // SUB_BLOCK):
        n = i * SUB_BLOCK
        e = jnp.exp(edges[i] - b[:n, :])
        ek.append(e)
        k_til.append(jnp.concatenate([kf[:n, :] * e, jnp.zeros((2 * c_len - n, width), F32)], axis=0))
    return q_til, k_til, eq, ek


def _split2(x):
    hi = x.astype(BF16)
    return hi, (x - hi.astype(F32)).astype(BF16)


def hgrn_fwd(proj, lb, ride=None):
    s = proj.shape[0]
    c_len, nh, hd = HGRN_CHUNK, HGRN_HEADS, HGRN_HEAD_DIM
    n_chunks = s // c_len
    col0 = 0

    cps = 2 * HGRN_CHUNKS_PER_STEP
    n_steps = n_chunks // cps

    def body(q_ref, f_ref, i_ref, lb_ref, o_ref, st_out_ref, a_out_ref, st_ref, b_ref, qf_ref, kf_ref, a_ref):
        @pl.when(pl.program_id(0) == 0)
        def _():
            st_ref[...] = jnp.zeros_like(st_ref)

        lbv = _lower_bound(lb_ref[...])
        for u in range(cps):
            rs = slice(u * c_len, (u + 1) * c_len)
            b_u, qf_u, kf_u, a_u = b_ref.at[u], qf_ref.at[u], kf_ref.at[u], a_ref.at[u]
            _, qf, _, log_f, kf = _hgrn_gates(q_ref[rs, :], f_ref[rs, :], lbv)
            b = _tri_sum(_tril_bf16(c_len), log_f)
            b_u[...] = b
            qf_u[...] = qf
            kf_u[...] = kf
            a_u[...] = jnp.zeros_like(a_u)

            def diagonal(rows, j, key, b_u=b_u, qf_u=qf_u, kf_u=kf_u, a_u=a_u):
                bj = b_u[pl.ds(key, 1), :]
                kj = kf_u[pl.ds(key, 1), :]
                nrow = rows.stop - rows.start
                t_loc = lax.broadcasted_iota(jnp.int32, (nrow, nh * hd), 0) + (rows.start % SUB_BLOCK)
                e = jnp.exp(jnp.where(t_loc >= j, b_u[rows, :] - bj, NEG_BIG))
                prod = qf_u[rows, :] * kj * e
                lane = lax.broadcasted_iota(jnp.int32, (nrow, hd), 1)
                for h in range(nh):
                    col = jnp.sum(prod[:, h * hd:(h + 1) * hd], axis=-1, keepdims=True)
                    a_u[h, rows, :] = jnp.where(lane == key, col, a_u[h, rows, :])

            _hgrn_diagonal_loops(c_len, diagonal)
            q_til, k_til, _, _ = _hgrn_off_diagonal(b, qf, kf)
            q_til = q_til.astype(BF16)
            k_til = [k.astype(BF16) for k in k_til]

            b_last = b[c_len - 1:c_len, :]
            qb = (qf * jnp.exp(b)).astype(BF16)
            kb2 = (kf * jnp.exp(b_last - b)).astype(BF16)
            vf = i_ref[rs, :].astype(BF16)
            for h in range(nh):
                hs = slice(h * hd, (h + 1) * hd)
                st = st_ref[h]
                st_out_ref[u, h] = st
                off = [jnp.zeros((SUB_BLOCK, hd), F32)]
                for i in range(1, c_len // SUB_BLOCK):
                    off.append(_dot_nt(q_til[i * SUB_BLOCK:(i + 1) * SUB_BLOCK, hs], k_til[i - 1][:, hs]))
                a_h = a_u[h] + jnp.concatenate(off, axis=0)
                a_out_ref[rs, hs] = a_h
                o_ref[rs, hs] = _dot_nt(qb[:, hs], st.astype(BF16)) + _dot(a_h[:, :c_len].astype(BF16), vf[:, hs])
                st_ref[h] = st * jnp.exp(b_last[:, hs]) + _dot_tn(vf[:, hs], kb2[:, hs])

    blk = (cps * c_len, HGRN_WIDTH)
    sblk = (cps, c_len, HGRN_WIDTH)
    step = lambda k: (lambda: pl.program_id(0) == k)
    e_in, e_out, e_shape, e_scr, e_args = _ride_specs(ride)
    return pl.pallas_call(
        _riding(body, 4, 3, 5, ride, step(0), step((7 * n_steps) // 8), step(n_steps - 1)),
        name="hgrn_fwd",
        grid=(n_steps,),
        in_specs=[
            pl.BlockSpec(blk, lambda c: (c, col0)),
            pl.BlockSpec(blk, lambda c: (c, col0 + 1)),
            pl.BlockSpec(blk, lambda c: (c, col0 + 2)),
            pl.BlockSpec((2, HGRN_WIDTH), lambda c: (0, 0)),
        ] + e_in,
        out_specs=[
            pl.BlockSpec(blk, lambda c: (c, 0)),
            pl.BlockSpec((cps, nh, hd, hd), lambda c: (c, 0, 0, 0)),
            pl.BlockSpec(blk, lambda c: (c, 0)),
        ] + e_out,
        out_shape=[
            jax.ShapeDtypeStruct((s, HGRN_WIDTH), F32),
            jax.ShapeDtypeStruct((n_chunks, nh, hd, hd), F32),
            jax.ShapeDtypeStruct((s, nh * hd), F32),
        ] + e_shape,
        scratch_shapes=[
            pltpu.VMEM((nh, hd, hd), F32),
            pltpu.VMEM(sblk, F32),
            pltpu.VMEM(sblk, F32),
            pltpu.VMEM(sblk, F32),
            pltpu.VMEM((cps, nh, c_len, hd), F32),
        ] + e_scr,
        compiler_params=_params(ride, dimension_semantics=("arbitrary",)),
    )(proj, proj, proj, lb, *e_args)


def hgrn_bwd(proj, lb, d_o, states, a_mat, ride=None):
    s = proj.shape[0]
    c_len, nh, hd = HGRN_CHUNK, HGRN_HEADS, HGRN_HEAD_DIM
    n_chunks = s // c_len
    col0 = 0
    cps = HGRN_CHUNKS_PER_STEP
    n_steps = n_chunks // cps
    last = n_steps - 1

    def body(q_ref, f_ref, i_ref, lb_ref, do_ref, st_in_ref, a_in_ref, dq_ref, df_ref, di_ref, dlb_ref,
             dst_ref, b_ref, qf_ref, kf_ref, da_ref, dqi_ref, dki_ref):
        @pl.when(pl.program_id(0) == 0)
        def _():
            dst_ref[...] = jnp.zeros_like(dst_ref)
            dlb_ref[...] = jnp.zeros_like(dlb_ref)

        lbv = _lower_bound(lb_ref[...])
        for u in reversed(range(cps)):
            rs = slice(u * c_len, (u + 1) * c_len)
            b_u, qf_u, kf_u, da_u, dqi_u, dki_u = (b_ref.at[u], qf_ref.at[u], kf_ref.at[u], da_ref.at[u], dqi_ref.at[u],
                                                   dki_ref.at[u])
            q = q_ref[rs, :]
            sq, qf, (sig, sig_neg, log_c), log_f, kf = _hgrn_gates(q, f_ref[rs, :], lbv)
            b = _tri_sum(_tril_bf16(c_len), log_f)
            b_u[...] = b
            qf_u[...] = qf
            kf_u[...] = kf
            b_last = b[c_len - 1:c_len, :]
            eb = jnp.exp(b)
            ebl = jnp.exp(b_last - b)
            qb = qf * eb
            kb2 = kf * ebl
            vf = i_ref[rs, :]
            d_o = do_ref[rs, :]
            qb_b, kb2_b, vf_b, do_b = qb.astype(BF16), kb2.astype(BF16), vf.astype(BF16), d_o.astype(BF16)
            tq = lax.broadcasted_iota(jnp.int32, (c_len, hd), 0)
            lane = lax.broadcasted_iota(jnp.int32, (c_len, hd), 1)

            dqb_parts, dvf_parts, dkb2_parts, dbl_parts = [], [], [], []
            for h in range(nh):
                hs = slice(h * hd, (h + 1) * hd)
                st = st_in_ref[u, h]
                dst = dst_ref[h]
                st_b, dst_b = st.astype(BF16), dst.astype(BF16)
                a_h = a_in_ref[rs, hs][:, :c_len].astype(BF16)
                dqb_parts.append(_dot(do_b[:, hs], st_b))
                dvf_parts.append(_dot_tn(a_h, do_b[:, hs]) + _dot_nt(kb2_b[:, hs], dst_b))
                dkb2_parts.append(_dot(vf_b[:, hs], dst_b))
                da = _dot_nt(do_b[:, hs], vf_b[:, hs])
                da = jnp.concatenate([da, jnp.zeros((c_len, hd - c_len), F32)], axis=1)
                da_u[h] = jnp.where(tq >= lane, da, 0.0)
                dbl_parts.append(jnp.sum(dst * st, axis=0, keepdims=True) * jnp.exp(b_last[:, hs]))
                dst_ref[h] = dst * jnp.exp(b_last[:, hs]) + _dot_tn(do_b[:, hs], qb_b[:, hs])
            dqb = jnp.concatenate(dqb_parts, axis=1)
            dvf = jnp.concatenate(dvf_parts, axis=1)
            dkb2 = jnp.concatenate(dkb2_parts, axis=1)
            dbl = jnp.concatenate(dbl_parts, axis=1) + jnp.sum(dkb2 * kb2, axis=0, keepdims=True)

            dqi_u[...] = jnp.zeros_like(dqi_u)
            t_idx = lax.broadcasted_iota(jnp.int32, (c_len, nh * hd), 0)

            def diagonal(rows, j, key, b_u=b_u, qf_u=qf_u, kf_u=kf_u, da_u=da_u, dqi_u=dqi_u, dki_u=dki_u):
                bj = b_u[pl.ds(key, 1), :]
                kj = kf_u[pl.ds(key, 1), :]
                nrow = rows.stop - rows.start
                t_loc = lax.broadcasted_iota(jnp.int32, (nrow, nh * hd), 0) + (rows.start % SUB_BLOCK)
                e = jnp.exp(jnp.where(t_loc >= j, b_u[rows, :] - bj, NEG_BIG))
                lane_r = lax.broadcasted_iota(jnp.int32, (nrow, hd), 1)
                cols = [jnp.sum(jnp.where(lane_r == key, da_u[h, rows, :], 0.0), axis=-1, keepdims=True)
                        for h in range(nh)]
                w = e * jnp.concatenate([jnp.broadcast_to(cc, (nrow, hd)) for cc in cols], axis=1)
                dqi_u[rows, :] += w * kj
                dki_u[pl.ds(key, 1), :] = jnp.sum(w * qf_u[rows, :], axis=0, keepdims=True)

            _hgrn_diagonal_loops(c_len, diagonal)

            q_til, k_til, eq, ek = _hgrn_off_diagonal(b, qf, kf)
            q_hi, q_lo = _split2(q_til)
            k_pairs = [_split2(k) for k in k_til]
            n_sub = c_len // SUB_BLOCK
            dq_heads, dk_heads = [], []
            for h in range(nh):
                hs = slice(h * hd, (h + 1) * hd)
                dq_rows = [jnp.zeros((SUB_BLOCK, hd), F32)]
                dk_h = jnp.zeros((c_len, hd), F32)
                for i in range(1, n_sub):
                    rows = slice(i * SUB_BLOCK, (i + 1) * SUB_BLOCK)
                    n = i * SUB_BLOCK
                    da_i = da_u[h, rows, :].astype(BF16)
                    k_hi, k_lo = k_pairs[i - 1]
                    dq_rows.append((_dot(da_i, k_hi[:, hs]) + _dot(da_i, k_lo[:, hs])) * eq[rows, hs])
                    dk_t = (_dot_tn(da_i, q_hi[rows, hs]) + _dot_tn(da_i, q_lo[rows, hs]))[:n, :] * ek[i - 1][:, hs]
                    dk_h = dk_h + jnp.concatenate([dk_t, jnp.zeros((c_len - n, hd), F32)], axis=0)
                dq_heads.append(jnp.concatenate(dq_rows, axis=0))
                dk_heads.append(dk_h)
            dq_intra = dqi_u[...] + jnp.concatenate(dq_heads, axis=1)
            dk_intra = dki_u[...] + jnp.concatenate(dk_heads, axis=1)

            db = dqb * qb + qf * dq_intra - kf * dk_intra - dkb2 * kb2
            db = db + jnp.where(t_idx == c_len - 1, dbl, 0.0)
            dg = _tri_sum(_tril_bf16(c_len, upper=True), db)
            dqf = dqb * eb + dq_intra
            dkf = dkb2 * ebl + dk_intra
            dq_ref[rs, :] = (dqf * (sq * (1.0 + q * (1.0 - sq)))).astype(BF16)
            df_ref[rs, :] = (sig_neg * (dg * jnp.exp(log_c - log_f) - dkf * (1.0 - lbv) * sig)).astype(BF16)
            di_ref[rs, :] = dvf.astype(BF16)
            dlb_ref[...] += jnp.sum(sig_neg * (dg * jnp.exp(-log_f) - dkf), axis=0, keepdims=True)

    blk = (cps * c_len, HGRN_WIDTH)
    sblk = (cps, c_len, HGRN_WIDTH)
    rev = lambda c: last - c
    step = lambda k: (lambda: pl.program_id(0) == k)
    e_in, e_out, e_shape, e_scr, e_args = _ride_specs(ride)
    return pl.pallas_call(
        _riding(body, 7, 4, 7, ride, step(0), step(n_steps // 2), step(last)),
        name="hgrn_bwd",
        grid=(n_steps,),
        in_specs=[
            pl.BlockSpec(blk, lambda c: (rev(c), col0)),
            pl.BlockSpec(blk, lambda c: (rev(c), col0 + 1)),
            pl.BlockSpec(blk, lambda c: (rev(c), col0 + 2)),
            pl.BlockSpec((2, HGRN_WIDTH), lambda c: (0, 0)),
            pl.BlockSpec(blk, lambda c: (rev(c), 0)),
            pl.BlockSpec((cps, nh, hd, hd), lambda c: (rev(c), 0, 0, 0)),
            pl.BlockSpec(blk, lambda c: (rev(c), 0)),
        ] + e_in,
        out_specs=[
            pl.BlockSpec(blk, lambda c: (rev(c), 0)),
            pl.BlockSpec(blk, lambda c: (rev(c), 0)),
            pl.BlockSpec(blk, lambda c: (rev(c), 0)),
            pl.BlockSpec((1, HGRN_WIDTH), lambda c: (0, 0)),
        ] + e_out,
        out_shape=[jax.ShapeDtypeStruct((s, HGRN_WIDTH), BF16)] * 3 + [jax.ShapeDtypeStruct((1, HGRN_WIDTH), F32)] + e_shape,
        scratch_shapes=[
            pltpu.VMEM((nh, hd, hd), F32),
            pltpu.VMEM(sblk, F32),
            pltpu.VMEM(sblk, F32),
            pltpu.VMEM(sblk, F32),
            pltpu.VMEM((cps, nh, c_len, hd), F32),
            pltpu.VMEM(sblk, F32),
            pltpu.VMEM(sblk, F32),
        ] + e_scr,
        compiler_params=_params(ride, dimension_semantics=("arbitrary",)),
    )(proj, proj, proj, lb, d_o, states, a_mat, *e_args)


def _per_head_lanes(x):
    lane = lax.broadcasted_iota(jnp.int32, (x.shape[0], LANES), 1)
    out = jnp.zeros((x.shape[0], LANES), F32)
    for h in range(ATTN_HEADS):
        out = jnp.where(lane == h, x[:, h * ATTN_HEAD_DIM:h * ATTN_HEAD_DIM + 1], out)
    return out


def _row_spec(tm, width, col=0):
    return pl.BlockSpec((tm, width), lambda i: (i, col))


def _const_spec(width):
    return pl.BlockSpec((1, width), lambda i: (0, 0))


def _acc_rows(ref, value):
    @pl.when(pl.program_id(0) == 0)
    def _():
        ref[...] = jnp.zeros_like(ref)

    ref[...] += jnp.sum(value, axis=0, keepdims=True)


def mix_fwd(attn_parts, o_h, proj, an, hn, w_out_b, gp, x, ride=None):
    s = x.shape[0]
    tm = TOKEN_TILE
    gate_col = 3
    hd = HGRN_HEAD_DIM
    nd = len(DILATIONS)

    def body(*refs):
        o_refs, l_refs = refs[:nd], refs[nd:2 * nd]
        oh_ref, gate_ref, an_ref, hn_ref, w_ref, gp_ref, x_ref = refs[2 * nd:2 * nd + 7]
        x1_ref, cat_ref, mixed_ref, attn_ref = refs[2 * nd + 7:2 * nd + 11]
        lse_refs = refs[2 * nd + 11:3 * nd + 11]
        o_scr, l_scr, lse_scr = refs[3 * nd + 11:]
        os_ = [_from_dilated(r, o_scr.at[k], d, tm) for k, (r, d) in enumerate(zip(o_refs, DILATIONS))]
        ls = [_from_dilated(r, l_scr.at[k], d, tm) for k, (r, d) in enumerate(zip(l_refs, DILATIONS))]
        m = jnp.maximum(jnp.maximum(ls[0], ls[1]), ls[2])
        es = [jnp.exp(l - m) for l in ls]
        den = es[0] + es[1] + es[2]
        attn = (es[0] * os_[0] + es[1] * os_[1] + es[2] * os_[2]) / den
        attn_ref[...] = attn
        lse_scr[0] = _per_head_lanes(m + jnp.log(den))
        for d, ref in zip(DILATIONS, lse_refs):
            _to_dilated(lse_scr, ref, d, tm)
        cat_ref[:, :ATTN_WIDTH] = _rms_fwd(attn, an_ref[...], ATTN_WIDTH).astype(BF16)
        gate = gate_ref[...]
        silu_g = gate * _sigmoid(gate)
        for h in range(HGRN_HEADS):
            hs = slice(h * hd, (h + 1) * hd)
            rec = _rms_fwd(oh_ref[:, hs], hn_ref[:, hs], hd) * silu_g[:, hs]
            cat_ref[:, ATTN_WIDTH + h * hd:ATTN_WIDTH + (h + 1) * hd] = rec.astype(BF16)
        mixed = _dot(cat_ref[...], w_ref[...])
        mixed_ref[...] = mixed
        x1_ref[...] = x_ref[...] + _rms_fwd(mixed, gp_ref[...], D_MODEL)

    aw = ATTN_WIDTH
    n_steps = s // tm
    step = lambda k: (lambda: pl.program_id(0) == k)
    e_in, e_out, e_shape, e_scr, e_args = _ride_specs(ride)
    return pl.pallas_call(
        _riding(body, 2 * nd + 7, 4 + nd, 3, ride, step(0), step((13 * n_steps) // 16), step(n_steps - 1)),
        name="mix_fwd",
        grid=(n_steps,),
        in_specs=[_dilated_spec(d, tm, aw) for d in DILATIONS] * 2 + [
            _row_spec(tm, aw), _row_spec(tm, aw, gate_col), _const_spec(aw), _const_spec(aw), _vmem_spec(),
            _const_spec(D_MODEL), _row_spec(tm, D_MODEL)] + e_in,
        out_specs=[_row_spec(tm, D_MODEL), _row_spec(tm, D_MODEL), _row_spec(tm, D_MODEL), _row_spec(tm, aw)] + [
            _dilated_spec(d, tm, LANES) for d in DILATIONS] + e_out,
        out_shape=[
            jax.ShapeDtypeStruct((s, D_MODEL), F32),
            jax.ShapeDtypeStruct((s, D_MODEL), BF16),
            jax.ShapeDtypeStruct((s, D_MODEL), F32),
            jax.ShapeDtypeStruct((s, aw), F32),
        ] + [jax.ShapeDtypeStruct((d, s // d, LANES), F32) for d in DILATIONS] + e_shape,
        scratch_shapes=[pltpu.VMEM((nd, aw // LANES, tm, LANES), F32), pltpu.VMEM((nd, aw // LANES, tm, LANES), F32),
                        pltpu.VMEM((1, tm, LANES), F32)] + e_scr,
        compiler_params=_params(ride, dimension_semantics=("arbitrary",)),
    )(*[p[0] for p in attn_parts], *[p[1] for p in attn_parts], o_h, proj, an, hn, w_out_b, gp, x, *e_args)


def mix_bwd(dx1, mixed, gp, w_out_b, attn, an, o_h, proj, hn):
    s = dx1.shape[0]
    tm = TOKEN_TILE
    gate_col = 3
    hd = HGRN_HEAD_DIM
    aw = ATTN_WIDTH

    nd = len(DILATIONS)

    def body(*refs):
        dx1_ref, mixed_ref, gp_ref, w_ref, attn_ref, an_ref, oh_ref, gate_ref, hn_ref, dmix_ref = refs[:10]
        do_refs, delta_refs = refs[10:10 + nd], refs[10 + nd:10 + 2 * nd]
        doh_ref, dgate_ref, dgp_ref, dan_ref, dhn_ref, do_ref, delta_ref = refs[10 + 2 * nd:]
        dmixed, gp_c = _rms_bwd(dx1_ref[...], mixed_ref[...], gp_ref[...], D_MODEL)
        _acc_rows(dgp_ref, gp_c)
        dmixed_b = dmixed.astype(BF16)
        dmix_ref[...] = dmixed_b
        dcat = _dot_nt(dmixed_b, w_ref[...])
        attn = attn_ref[...]
        d_o, an_c = _rms_bwd(dcat[:, :aw], attn, an_ref[...], aw)
        _acc_rows(dan_ref, an_c)
        _lane_blocks(do_ref, d_o)
        prod = d_o * attn
        lane = lax.broadcasted_iota(jnp.int32, (tm, LANES), 1)
        delta = jnp.zeros((tm, LANES), F32)
        for pair in range(ATTN_HEADS // 2):
            pp = prod[:, pair * LANES:(pair + 1) * LANES]
            low = _lane_half((tm, LANES), 0)
            lo = jnp.sum(jnp.where(low, pp, 0.0), axis=-1, keepdims=True)
            hi = jnp.sum(jnp.where(low, 0.0, pp), axis=-1, keepdims=True)
            delta = jnp.where(lane == 2 * pair, lo, jnp.where(lane == 2 * pair + 1, hi, delta))
        delta_ref[0] = delta
        for d, o_ref, l_ref in zip(DILATIONS, do_refs, delta_refs):
            _to_dilated(do_ref, o_ref, d, tm, cast=BF16)
            _to_dilated(delta_ref, l_ref, d, tm)
        gate = gate_ref[...]
        sg = _sigmoid(gate)
        silu_g = gate * sg
        drec = dcat[:, aw:]
        hn_parts = []
        for h in range(HGRN_HEADS):
            hs = slice(h * hd, (h + 1) * hd)
            oh = oh_ref[:, hs]
            on = _rms_fwd(oh, hn_ref[:, hs], hd)
            dgate_ref[:, hs] = (drec[:, hs] * on * (sg[:, hs] * (1.0 + gate[:, hs] * (1.0 - sg[:, hs])))).astype(BF16)
            d_oh, hn_c = _rms_bwd(drec[:, hs] * silu_g[:, hs], oh, hn_ref[:, hs], hd)
            doh_ref[:, hs] = d_oh
            hn_parts.append(hn_c)
        _acc_rows(dhn_ref, jnp.concatenate(hn_parts, axis=1))

    return pl.pallas_call(
        body,
        name="mix_bwd",
        grid=(s // tm,),
        in_specs=[_row_spec(tm, D_MODEL), _row_spec(tm, D_MODEL), _const_spec(D_MODEL), _vmem_spec(), _row_spec(tm, aw),
                  _const_spec(aw), _row_spec(tm, aw), _row_spec(tm, aw, gate_col), _const_spec(aw)],
        out_specs=[_row_spec(tm, D_MODEL)] + [_dilated_spec(d, tm, aw) for d in DILATIONS] + [
            _dilated_spec(d, tm, LANES) for d in DILATIONS] + [_row_spec(tm, aw)] * 2 + [
            _const_spec(D_MODEL), _const_spec(aw), _const_spec(aw)],
        out_shape=[jax.ShapeDtypeStruct((s, D_MODEL), BF16)] + [
            jax.ShapeDtypeStruct((d, s // d, aw), BF16) for d in DILATIONS] + [
            jax.ShapeDtypeStruct((d, s // d, LANES), F32) for d in DILATIONS] + [
            jax.ShapeDtypeStruct((s, aw), F32), jax.ShapeDtypeStruct((s, aw), BF16),
            jax.ShapeDtypeStruct((1, D_MODEL), F32), jax.ShapeDtypeStruct((1, aw), F32),
            jax.ShapeDtypeStruct((1, aw), F32)],
        scratch_shapes=[pltpu.VMEM((aw // LANES, tm, LANES), F32), pltpu.VMEM((1, tm, LANES), F32)],
        compiler_params=_params(dimension_semantics=("arbitrary",)),
    )(dx1, mixed, gp, w_out_b, attn, an, o_h, proj, hn)


def mlp_fwd_bwd(x1, g_pre, w1_blocks, w2_b, g_post, target):
    s = x1.shape[0]
    tm = MLP_TILE
    nblk, _, fb = w1_blocks.shape

    def body(x1_ref, gpre_ref, w1_ref, w2_ref, gpost_ref, t_ref,
             dx1_ref, h2_ref, a_ref, du_ref, dff_ref, loss_ref, dgpre_ref, dgpost_ref, u_ref):
        x1v = x1_ref[...]
        h2 = _rms_fwd(x1v, gpre_ref[...], D_MODEL).astype(BF16)
        h2_ref[...] = h2
        ff = jnp.zeros((tm, D_MODEL), F32)
        for j in range(nblk):
            cols = slice(j * fb, (j + 1) * fb)
            ru = jnp.maximum(_dot(h2, w1_ref[j]), 0.0)
            u_ref[:, cols] = ru.astype(BF16)
            a = (ru * ru).astype(BF16)
            a_ref[:, cols] = a
            ff = ff + _dot(a, w2_ref[cols, :])
        diff = x1v + _rms_fwd(ff, gpost_ref[...], D_MODEL) - t_ref[...]
        _acc_rows(loss_ref, diff * diff)
        dy = diff * (1.0 / D_MODEL)
        dff, gpost_c = _rms_bwd(dy, ff, gpost_ref[...], D_MODEL)
        _acc_rows(dgpost_ref, gpost_c)
        dff_b = dff.astype(BF16)
        dff_ref[...] = dff_b
        dh2 = jnp.zeros((tm, D_MODEL), F32)
        for j in range(nblk):
            cols = slice(j * fb, (j + 1) * fb)
            du = (_dot_nt(dff_b, w2_ref[cols, :]) * (2.0 * u_ref[:, cols])).astype(BF16)
            du_ref[:, cols] = du
            dh2 = dh2 + _dot_nt(du, w1_ref[j])
        dxa, gpre_c = _rms_bwd(dh2, x1v, gpre_ref[...], D_MODEL)
        _acc_rows(dgpre_ref, gpre_c)
        dx1_ref[...] = dy + dxa

    dm = D_MODEL
    return pl.pallas_call(
        body,
        name="mlp_fwd_bwd",
        grid=(s // tm,),
        in_specs=[_row_spec(tm, dm), _const_spec(dm), _vmem_spec(), _vmem_spec(), _const_spec(dm), _row_spec(tm, dm)],
        out_specs=[_row_spec(tm, dm), _row_spec(tm, dm), _row_spec(tm, D_FF), _row_spec(tm, D_FF), _row_spec(tm, dm),
                   _const_spec(dm), _const_spec(dm), _const_spec(dm)],
        out_shape=[
            jax.ShapeDtypeStruct((s, dm), F32),
            jax.ShapeDtypeStruct((s, dm), BF16),
            jax.ShapeDtypeStruct((s, D_FF), BF16),
            jax.ShapeDtypeStruct((s, D_FF), BF16),
            jax.ShapeDtypeStruct((s, dm), BF16),
            jax.ShapeDtypeStruct((1, dm), F32),
            jax.ShapeDtypeStruct((1, dm), F32),
            jax.ShapeDtypeStruct((1, dm), F32),
        ],
        scratch_shapes=[pltpu.VMEM((tm, D_FF), BF16)],
        compiler_params=_params(dimension_semantics=("arbitrary",)),
    )(x1, g_pre, w1_blocks, w2_b, g_post, target)


def in_proj_bwd(attn_grads, hgrn_grads, dgate, w_in_b, x, g1, dx1):
    s = x.shape[0]
    tm = PROJ_TILE
    aw = ATTN_WIDTH
    n_attn = len(attn_grads)
    flat = [g[k] for k in range(3) for g in attn_grads] + list(hgrn_grads) + [dgate]

    def body(*refs):
        parts = refs[:len(flat)]
        w_ref, x_ref, g_ref, dx1_ref, dx_ref, dproj_ref, dg_ref, scr = refs[len(flat):]
        groups = []
        for k in range(3):
            acc = None
            for p, d in zip(parts[k * n_attn:(k + 1) * n_attn], DILATIONS):
                v = _from_dilated(p, scr, d, tm)
                acc = v if acc is None else acc + v
            groups.append(acc)
        groups += [p[...] for p in parts[3 * n_attn:]]
        dh = jnp.zeros((tm, D_MODEL), F32)
        for gi, grp in enumerate(groups):
            cols = slice(gi * aw, (gi + 1) * aw)
            gb = grp.astype(BF16)
            dproj_ref[:, cols] = gb
            dh = dh + _dot_nt(gb, w_ref[:, cols])
        dxa, g_c = _rms_bwd(dh, x_ref[...], g_ref[...], D_MODEL)
        _acc_rows(dg_ref, g_c)
        dx_ref[...] = dx1_ref[...] + dxa

    dm = D_MODEL
    return pl.pallas_call(
        body,
        name="in_proj_bwd",
        grid=(s // tm,),
        in_specs=[_dilated_spec(d, tm, aw) for d in DILATIONS] * 3 + [_row_spec(tm, aw)] * 4 + [
            _vmem_spec(), _row_spec(tm, dm), _const_spec(dm), _row_spec(tm, dm)],
        out_specs=[_row_spec(tm, dm), _row_spec(tm, IN_PROJ_WIDTH), _const_spec(dm)],
        out_shape=[jax.ShapeDtypeStruct((s, dm), F32), jax.ShapeDtypeStruct((s, IN_PROJ_WIDTH), BF16),
                   jax.ShapeDtypeStruct((1, dm), F32)],
        scratch_shapes=[pltpu.VMEM((aw // LANES, tm, LANES), F32)],
        compiler_params=_params(dimension_semantics=("arbitrary",)),
    )(*flat, w_in_b, x, g1, dx1)


def wgrad(a_b, b_b, tn, name, ts=2048, per_step=1, ride=None):
    s, k = a_b.shape
    n = b_b.shape[1]

    def body(a_ref, b_ref, o_ref):
        @pl.when(pl.program_id(1) == 0)
        def _():
            o_ref[...] = jnp.zeros_like(o_ref)

        a = a_ref[...]
        for jj in range(per_step):
            o_ref[jj] += _dot_tn(a, b_ref[:, jj * tn:(jj + 1) * tn])

    wide = tn * per_step
    gn, gs = n // wide, s // ts
    step = lambda j, i: (lambda: (pl.program_id(0) == j) & (pl.program_id(1) == i))
    e_in, e_out, e_shape, e_scr, e_args = _ride_specs(ride)
    out = pl.pallas_call(
        _riding(body, 2, 1, 0, ride, step(0, 0), step(gn // 2, 0), step(gn - 1, gs - 1)),
        name=name,
        grid=(gn, gs),
        in_specs=[pl.BlockSpec((ts, k), lambda j, i: (i, 0)), pl.BlockSpec((ts, wide), lambda j, i: (i, j))] + e_in,
        out_specs=[pl.BlockSpec((per_step, k, tn), lambda j, i: (j, 0, 0))] + e_out,
        out_shape=[jax.ShapeDtypeStruct((n // tn, k, tn), F32)] + e_shape,
        scratch_shapes=e_scr,
        compiler_params=_params(ride, dimension_semantics=("arbitrary", "arbitrary")),
    )(a_b, b_b, *e_args)
    return out[0] if ride is None else out


def train_step(x, target, g1, an, logits, hn, gp, g_pre, g_post, w, m, v):
    nd = len(DILATIONS)
    shard_b = {k: w[k].astype(BF16) for k in BIG}
    (w_in_g,) = run_exchange(gather_exchange([shard_b["w_in"]]), "gather_w_in")
    w_in_b = w_in_g.transpose(1, 0, 2).reshape(D_MODEL, IN_PROJ_WIDTH)

    proj, h_b, *qkvs, w2_g = in_proj_fwd(x, g1, w_in_b, ride=gather_exchange([shard_b["w_ff2"]]))
    w2_b = w2_g.reshape(D_FF, D_MODEL)
    attn_parts = [attn_fwd(qkv, d) for qkv, d in zip(qkvs, DILATIONS)]
    o_h, states, a_mat, w_out_g, w1_blocks = hgrn_fwd(
        proj, logits, ride=gather_exchange([shard_b["w_out"], shard_b["w_ff1"]]))
    w_out_b = w_out_g.reshape(D_MODEL, D_MODEL)
    x1, cat_b, mixed, attn, *lses = mix_fwd(attn_parts, o_h, proj, an, hn, w_out_b, gp, x)
    dx1, h2_b, a_b, du_b, dff_b, loss_vec, dg_pre, dg_post = mlp_fwd_bwd(x1, g_pre, w1_blocks, w2_b, g_post, target)
    dw2 = wgrad(a_b, dff_b, D_MODEL, "wgrad_ff2", ts=512)
    dw1 = wgrad(h2_b, du_b, D_FF // N_DEV, "wgrad_ff1", per_step=2)
    dmix_b, *rest = mix_bwd(dx1, mixed, gp, w_out_b, attn, an, o_h, proj, hn)
    d_os, deltas = rest[:nd], rest[nd:2 * nd]
    d_oh, dgate, dgp, dan, dhn = rest[2 * nd:]
    dwout = wgrad(cat_b, dmix_b, D_MODEL, "wgrad_out")

    early = ("w_out", "w_ff1", "w_ff2")
    early_grads = [dwout.reshape(N_DEV, D_MODEL // N_DEV, D_MODEL), dw1, dw2.reshape(N_DEV, D_FF // N_DEV, D_MODEL)]
    res = attn_bwd(qkvs[0], d_os[0], lses[0], deltas[0], DILATIONS[0], ride=to_core_exchange(early_grads))
    pairs = [pair_sum(g, s, f"pair_sum_{name}") for g, s, name in zip(early_grads, res[3:], early)]
    attn_grads = [res[:3]]
    *res, others_ff2 = attn_bwd(qkvs[1], d_os[1], lses[1], deltas[1], DILATIONS[1],
                                ride=to_chip_exchange([pairs[2][1]]))
    attn_grads.append(res)
    attn_grads.append(attn_bwd(qkvs[2], d_os[2], lses[2], deltas[2], DILATIONS[2]))
    dq_h, df_h, di_h, dlb, *others = hgrn_bwd(proj, logits, d_oh, states, a_mat,
                                              ride=to_chip_exchange([pairs[0][1], pairs[1][1]]))
    others.append(others_ff2)
    dx, dproj_b, dg1 = in_proj_bwd(attn_grads, (dq_h, df_h, di_h), dgate, w_in_b, x, g1, dx1)
    packed = _pack_small(dg1, dgp, dg_pre, dg_post, dan, dhn, dlb, loss_vec)
    dwin, small_slots = wgrad(h_b, dproj_b, 2 * IN_PROJ_WIDTH // N_DEV, "wgrad_in",
                              ride=small_exchange(packed))
    def update(k, ride=None):
        name = early[k]
        return sum_adamw(pairs[k][0], others[k], w[name], m[name], v[name], f"sum_adamw_{name}", ride=ride)

    shard_w = IN_PROJ_WIDTH // N_DEV
    dwin_blocks = dwin.reshape(N_DEV // 2, D_MODEL, 2, shard_w).transpose(0, 2, 1, 3).reshape(N_DEV, D_MODEL, shard_w)
    big = {}
    *big["w_ff1"], from_sibling = update(1, ride=to_core_exchange([dwin_blocks.astype(BF16)]))
    pair_in, pair_in_b = pair_sum(dwin_blocks, from_sibling, "pair_sum_w_in")
    *big["w_ff2"], others_in = update(2, ride=to_chip_exchange([pair_in_b]))
    big["w_out"] = update(0)
    big["w_in"] = sum_adamw(pair_in, others_in, w["w_in"], m["w_in"], v["w_in"], "sum_adamw_w_in")
    return dx, big, small_slots


def _position():
    x, y, c = lax.axis_index("x"), lax.axis_index("y"), lax.axis_index("c")
    other_chips = [(1 - x, y), (x, 1 - y), (1 - x, 1 - y)]
    return x, y, c, other_chips


def _any_spec():
    return pl.BlockSpec(memory_space=pl.ANY)


class Exchange:
    def __init__(self, arrays, out_shape, sems, stages, collective_id, peers):
        self.arrays, self.out_shape, self.sems, self.stages = list(arrays), list(out_shape), list(sems), stages
        self.collective_id, self.peers = collective_id, peers

    def open(self):
        barrier = pltpu.get_barrier_semaphore()
        peers = self.peers()
        for peer in peers:
            pl.semaphore_signal(barrier, inc=1, device_id=peer, device_id_type=MESH)
        pl.semaphore_wait(barrier, len(peers))


def _siblings():
    x, y, c, _ = _position()
    return [(x, y, 1 - c)]


def _same_core_of_other_chips():
    x, y, c, chips = _position()
    return [(px, py, c) for px, py in chips]


def _gather_peers():
    return _siblings() + _same_core_of_other_chips()


def _all_others():
    x, y, c, _ = _position()
    return [(1 - x if rel & 4 else x, 1 - y if rel & 2 else y, 1 - c if rel & 1 else c) for rel in range(1, N_DEV)]


def gather_exchange(shards):
    n = len(shards)

    def stages(ins, outs, sems):
        send_sems, recv_sems, local_sems = sems

        def parts():
            x, y, c, chips = _position()
            me, sibling = (x, y, c), (x, y, 1 - c)

            def slot(a, px, py, pc):
                return outs[a].at[4 * px + 2 * py + pc]

            def copy(a, k, block, to, src=None):
                return pltpu.make_async_remote_copy(
                    src_ref=slot(a, *block) if src is None else src, dst_ref=slot(a, *block),
                    send_sem=send_sems.at[a, k], recv_sem=recv_sems.at[a, k], device_id=to, device_id_type=MESH)

            local = [pltpu.make_async_copy(ins[a], slot(a, *me), local_sems.at[a]) for a in range(n)]
            first = []
            for a in range(n):
                first.append(copy(a, 0, me, sibling, src=ins[a]))
                first += [copy(a, 1 + j, me, (*chip, c), src=ins[a]) for j, chip in enumerate(chips)]
            passed = [copy(a, 4 + j, (*chip, c), sibling) for j, chip in enumerate(chips) for a in range(n)]
            return c, chips, me, sibling, copy, local, first, passed

        def begin():
            _, _, _, _, _, local, first, _ = parts()
            for cp in local + first:
                cp.start()

        def middle():
            c, chips, me, _, copy, _, _, passed = parts()
            k = 0
            for j, chip in enumerate(chips):
                for a in range(n):
                    copy(a, 1 + j, (*chip, c), me).wait_recv()
                    passed[k].start()
                    k += 1

        def end():
            c, chips, me, sibling, copy, local, first, passed = parts()
            for a in range(n):
                copy(a, 0, sibling, me).wait_recv()
                for j, chip in enumerate(chips):
                    copy(a, 4 + j, (*chip, 1 - c), me).wait_recv()
            for cp in first + passed:
                cp.wait_send()
            for cp in local:
                cp.wait()

        return begin, middle, end

    return Exchange(
        shards, [jax.ShapeDtypeStruct((N_DEV,) + sh.shape, sh.dtype) for sh in shards],
        [pltpu.SemaphoreType.DMA((n, 7)), pltpu.SemaphoreType.DMA((n, 7)), pltpu.SemaphoreType.DMA((n,))], stages,
        collective_id=0, peers=_gather_peers)


def to_core_exchange(grads):
    n = len(grads)

    def stages(ins, outs, sems):
        send_sems, recv_sems = sems

        def copies():
            x, y, c, _ = _position()
            return [pltpu.make_async_remote_copy(
                src_ref=ins[a].at[2 * q + (1 - c)], dst_ref=outs[a].at[q], send_sem=send_sems.at[a, q],
                recv_sem=recv_sems.at[a, q], device_id=(x, y, 1 - c), device_id_type=MESH)
                for a in range(n) for q in range(4)]

        def begin():
            for cp in copies():
                cp.start()

        def end():
            for cp in copies():
                cp.wait()

        return begin, None, end

    return Exchange(grads, [jax.ShapeDtypeStruct((4,) + g.shape[1:], g.dtype) for g in grads],
                    [pltpu.SemaphoreType.DMA((n, 4)), pltpu.SemaphoreType.DMA((n, 4))], stages,
                    collective_id=1, peers=_siblings)


def pair_sum(grad, from_sibling, name):
    _, r, cdim = grad.shape
    tr = min(r, ELEMENTWISE_ROWS)
    c_idx = lax.axis_index("c").astype(jnp.int32).reshape(1)

    def body(c_ref, g_ref, s_ref, o_ref, ob_ref):
        total = g_ref[...] + s_ref[...]
        o_ref[...] = total
        ob_ref[...] = total.astype(BF16)

    blk = lambda: pl.BlockSpec((1, tr, cdim), lambda q, i, cr: (q, i, 0))
    return pl.pallas_call(
        body,
        name=name,
        grid_spec=pltpu.PrefetchScalarGridSpec(
            num_scalar_prefetch=1,
            grid=(4, r // tr),
            in_specs=[pl.BlockSpec((1, tr, cdim), lambda q, i, cr: (2 * q + cr[0], i, 0)), blk()],
            out_specs=[blk(), blk()],
        ),
        out_shape=[jax.ShapeDtypeStruct((4, r, cdim), F32), jax.ShapeDtypeStruct((4, r, cdim), BF16)],
        compiler_params=_params(dimension_semantics=("arbitrary", "arbitrary")),
    )(c_idx, grad, from_sibling)


def to_chip_exchange(pairs):
    n = len(pairs)

    def stages(ins, outs, sems):
        send_sems, recv_sems = sems

        def copies():
            x, y, c, chips = _position()
            return [pltpu.make_async_remote_copy(
                src_ref=ins[a].at[2 * px + py], dst_ref=outs[a].at[j], send_sem=send_sems.at[a, j],
                recv_sem=recv_sems.at[a, j], device_id=(px, py, c), device_id_type=MESH)
                for a in range(n) for j, (px, py) in enumerate(chips)]

        def begin():
            for cp in copies():
                cp.start()

        def end():
            for cp in copies():
                cp.wait()

        return begin, None, end

    return Exchange(pairs, [jax.ShapeDtypeStruct((3,) + p.shape[1:], p.dtype) for p in pairs],
                    [pltpu.SemaphoreType.DMA((n, 3)), pltpu.SemaphoreType.DMA((n, 3))], stages,
                    collective_id=2, peers=_same_core_of_other_chips)


def run_exchange(ex, name):
    n_in, n_out = len(ex.arrays), len(ex.out_shape)

    def body(*refs):
        begin, middle, end = ex.stages(refs[:n_in], refs[n_in:n_in + n_out], refs[n_in + n_out:])
        ex.open()
        begin()
        if middle is not None:
            middle()
        end()

    return pl.pallas_call(
        body,
        name=name,
        in_specs=[_any_spec()] * n_in,
        out_specs=[_any_spec()] * n_out,
        out_shape=ex.out_shape,
        scratch_shapes=ex.sems,
        compiler_params=pltpu.CompilerParams(collective_id=ex.collective_id),
    )(*ex.arrays)


def _riding(body, n_in, n_out, n_scratch, ex, first, middle, last):
    if ex is None:
        return body
    r_in, r_out = len(ex.arrays), len(ex.out_shape)

    def wrapped(*refs):
        k_in, refs = refs[:n_in], refs[n_in:]
        e_in, refs = refs[:r_in], refs[r_in:]
        k_out, refs = refs[:n_out], refs[n_out:]
        e_out, refs = refs[:r_out], refs[r_out:]
        k_scr, e_sems = refs[:n_scratch], refs[n_scratch:]
        begin, mid, end = ex.stages(e_in, e_out, e_sems)

        @pl.when(first())
        def _():
            ex.open()
            begin()

        body(*k_in, *k_out, *k_scr)
        if mid is not None:
            pl.when(middle())(mid)
        pl.when(last())(end)

    return wrapped


def _ride_specs(ex):
    if ex is None:
        return [], [], [], [], []
    return [_any_spec()] * len(ex.arrays), [_any_spec()] * len(ex.out_shape), ex.out_shape, ex.sems, ex.arrays


def _adamw(w, g, m, v):
    m = ADAM_B1 * m + (1.0 - ADAM_B1) * g
    v = ADAM_B2 * v + (1.0 - ADAM_B2) * (g * g)
    m_hat = m / (1.0 - ADAM_B1 ** ADAM_STEP)
    v_hat = v / (1.0 - ADAM_B2 ** ADAM_STEP)
    delta = -ADAM_LR * (m_hat / (jnp.sqrt(v_hat) + ADAM_EPS) + ADAM_WD * w)
    return delta, m, v


def sum_adamw(pairs, others, w, m, v, name, ride=None):
    r, cdim = w.shape
    tr = min(r, ELEMENTWISE_ROWS // 2)
    n_steps = r // tr
    chip_idx =(2 * lax.axis_index("x") + lax.axis_index("y")).astype(jnp.int32).reshape(1)

    def body(q_ref, p_ref, o_ref, w_ref, m_ref, v_ref, g_out, d_out, m_out, v_out):
        g = p_ref[0] + o_ref[0].astype(F32) + o_ref[1].astype(F32) + o_ref[2].astype(F32)
        g_out[...] = g
        d_out[...], m_out[...], v_out[...] = _adamw(w_ref[...], g, m_ref[...], v_ref[...])

    tile = lambda: pl.BlockSpec((tr, cdim), lambda i, qr: (i, 0))
    step = lambda k: (lambda: pl.program_id(0) == k)
    e_in, e_out, e_shape, e_scr, e_args = _ride_specs(ride)
    return pl.pallas_call(
        _riding(body, 6, 4, 0, ride, step(0), step(0), step(n_steps - 1)),
        name=name,
        grid_spec=pltpu.PrefetchScalarGridSpec(
            num_scalar_prefetch=1,
            grid=(n_steps,),
            in_specs=[pl.BlockSpec((1, tr, cdim), lambda i, qr: (qr[0], i, 0)),
                      pl.BlockSpec((3, tr, cdim), lambda i, qr: (0, i, 0)), tile(), tile(), tile()] + e_in,
            out_specs=[tile(), tile(), tile(), tile()] + e_out,
            scratch_shapes=e_scr,
        ),
        out_shape=[jax.ShapeDtypeStruct((r, cdim), F32)] * 4 + e_shape,
        compiler_params=_params(ride, dimension_semantics=("arbitrary",)),
    )(chip_idx, pairs, others, w, m, v, *e_args)


def small_exchange(packed):
    def stages(ins, outs, sems):
        send_sems, recv_sems, local_sem = sems
        (src,), (slots,) = ins, outs

        def copies():
            x, y, c, _ = _position()
            my_id = 4 * x + 2 * y + c
            sends, landings = [], []
            for rel in range(1, N_DEV):
                px = 1 - x if (rel >> 2) & 1 else x
                py = 1 - y if (rel >> 1) & 1 else y
                pc = 1 - c if rel & 1 else c
                peer = dict(send_sem=send_sems.at[rel - 1], recv_sem=recv_sems.at[rel - 1], device_id=(px, py, pc),
                            device_id_type=MESH)
                sends.append(pltpu.make_async_remote_copy(src_ref=src, dst_ref=slots.at[my_id], **peer))
                landings.append(pltpu.make_async_remote_copy(src_ref=src, dst_ref=slots.at[4 * px + 2 * py + pc], **peer))
            return pltpu.make_async_copy(src, slots.at[my_id], local_sem), sends, landings

        def begin():
            local, sends, _ = copies()
            local.start()
            for cp in sends:
                cp.start()

        def end():
            local, sends, landings = copies()
            for cp in landings:
                cp.wait_recv()
            for cp in sends:
                cp.wait_send()
            local.wait()

        return begin, None, end

    return Exchange([packed], [jax.ShapeDtypeStruct((N_DEV,) + packed.shape, packed.dtype)],
                    [pltpu.SemaphoreType.DMA((N_DEV - 1,)), pltpu.SemaphoreType.DMA((N_DEV - 1,)),
                     pltpu.SemaphoreType.DMA(())], stages, collective_id=3, peers=_all_others)


def small_adamw(slots, w, m, v):
    def body(r_ref, w_ref, m_ref, v_ref, g_out, d_out, m_out, v_out, loss_out):
        red = r_ref[0]
        for k in range(1, N_DEV):
            red = red + r_ref[k]
        wv = w_ref[...]
        lb = _lower_bound(jnp.concatenate([wv[5:6, :HGRN_WIDTH], wv[5:6, HGRN_WIDTH:]], axis=0))
        t = red[5:6, :HGRN_WIDTH] * lb * (1.0 - lb)
        row = lax.broadcasted_iota(jnp.int32, red.shape, 0)
        g = jnp.where(row == 5, jnp.concatenate([t, -t], axis=1), jnp.where(row >= 6, 0.0, red))
        g_out[...] = g
        d_out[...], m_out[...], v_out[...] = _adamw(wv, g, m_ref[...], v_ref[...])
        loss = jnp.sum(red[6:7, :], axis=-1, keepdims=True) * (0.5 / D_MODEL)
        loss_out[...] = jnp.broadcast_to(loss, loss_out.shape)

    return pl.pallas_call(
        body,
        name="small_adamw",
        in_specs=[_vmem_spec()] * 4,
        out_specs=[_vmem_spec()] * 5,
        out_shape=[jax.ShapeDtypeStruct(w.shape, F32)] * 4 + [jax.ShapeDtypeStruct((SUBLANES, LANES), F32)],
    )(slots, w, m, v)


def _pack_small(g1, gp, g_pre, g_post, an, hn, logits_or_dlb, extra=None):
    row5 = logits_or_dlb.reshape(1, -1)
    row5 = jnp.pad(row5, ((0, 0), (0, D_MODEL - row5.shape[1])))
    row6 = jnp.zeros((1, D_MODEL), F32) if extra is None else extra
    return jnp.concatenate([g1, gp, g_pre, g_post, jnp.concatenate([an, hn], axis=1), row5, row6,
                            jnp.zeros((1, D_MODEL), F32)], axis=0)


def _unpack_small(p):
    return dict(mix_pre_norm=p[0:1], mix_post_norm=p[1:2], mlp_pre_norm=p[2:3], mlp_post_norm=p[3:4],
                attn_out_norm=p[4:5, :ATTN_WIDTH], hgrn_out_norm=p[4:5, ATTN_WIDTH:],
                hgrn_lb_logits=p[5].reshape(2, HGRN_WIDTH))


BIG = ("w_in", "w_out", "w_ff1", "w_ff2")
ORDER = ("mix_pre_norm", "w_in", "attn_out_norm", "hgrn_lb_logits", "hgrn_out_norm", "w_out", "mix_post_norm",
         "mlp_pre_norm", "w_ff1", "w_ff2", "mlp_post_norm")


def kernel(x, mix_pre_norm, w_in, attn_out_norm, hgrn_lb_logits, hgrn_out_norm, w_out, mix_post_norm, mlp_pre_norm, w_ff1, w_ff2, mlp_post_norm, loss_target, m_mix_pre_norm, m_w_in, m_attn_out_norm, m_hgrn_lb_logits, m_hgrn_out_norm, m_w_out, m_mix_post_norm, m_mlp_pre_norm, m_w_ff1, m_w_ff2, m_mlp_post_norm, v_mix_pre_norm, v_w_in, v_attn_out_norm, v_hgrn_lb_logits, v_hgrn_out_norm, v_w_out, v_mix_post_norm, v_mlp_pre_norm, v_w_ff1, v_w_ff2, v_mlp_post_norm):
    w = dict(w_in=w_in[0], w_out=w_out[0], w_ff1=w_ff1[0], w_ff2=w_ff2[0])
    m = dict(w_in=m_w_in[0], w_out=m_w_out[0], w_ff1=m_w_ff1[0], w_ff2=m_w_ff2[0])
    v = dict(w_in=v_w_in[0], w_out=v_w_out[0], w_ff1=v_w_ff1[0], w_ff2=v_w_ff2[0])

    dx, big, small_slots = train_step(x[0], loss_target[0], mix_pre_norm, attn_out_norm, hgrn_lb_logits, hgrn_out_norm,
                                      mix_post_norm, mlp_pre_norm, mlp_post_norm, w, m, v)

    pack = lambda a, b, c2, d, e, f, g: _pack_small(a, b, c2, d, e, f, g)
    w_s = pack(mix_pre_norm, mix_post_norm, mlp_pre_norm, mlp_post_norm, attn_out_norm, hgrn_out_norm, hgrn_lb_logits)
    m_s = pack(m_mix_pre_norm, m_mix_post_norm, m_mlp_pre_norm, m_mlp_post_norm, m_attn_out_norm, m_hgrn_out_norm,
               m_hgrn_lb_logits)
    v_s = pack(v_mix_pre_norm, v_mix_post_norm, v_mlp_pre_norm, v_mlp_post_norm, v_attn_out_norm, v_hgrn_out_norm,
               v_hgrn_lb_logits)
    g_s, d_s, nm_s, nv_s, loss = small_adamw(small_slots, w_s, m_s, v_s)
    small_out = [_unpack_small(t) for t in (g_s, d_s, nm_s, nv_s)]

    outs = [loss[0, 0], dx[None]]
    for kind in range(4):
        for name in ORDER:
            outs.append(big[name][kind][None] if name in BIG else small_out[kind][name])
    return tuple(outs)
```

```python
import jax
import jax.numpy as jnp
from jax import lax
from jax.experimental import pallas as pl
from jax.experimental.pallas import tpu as pltpu

F32 = jnp.float32
BF16 = jnp.bfloat16

D_MODEL = 1024
ATTN_WIDTH = 512
ATTN_HEAD_DIM = 64
ATTN_HEADS = 8
ATTN_BLOCK = 128
DILATIONS = (1, 4, 16)
HGRN_WIDTH = 512
HGRN_HEADS = 4
HGRN_HEAD_DIM = 128
HGRN_CHUNK = 64
IN_PROJ_WIDTH = 3584
D_FF = 4096
RMS_EPS = 1e-6
N_DEV = 8
ADAM_LR = 0.001
ADAM_B1 = 0.9
ADAM_B2 = 0.999
ADAM_EPS = 1e-08
ADAM_WD = 0.01
ADAM_STEP = 10

SUBLANES = 8
LANES = 128
COLUMN_UNROLL = 8
HGRN_CHUNKS_PER_STEP = 2
SUB_BLOCK = 16
TOKEN_TILE = 512
ELEMENTWISE_ROWS = 1024
MLP_TILE = 256
PROJ_TILE = 512
VMEM_BYTES_V7X = 64 * 1024 * 1024
VMEM_LIMIT = VMEM_BYTES_V7X // 8 * 7
NEG_BIG = -1e30
MESH = pl.DeviceIdType.MESH


def _params(ride=None, **kw):
    if ride is not None:
        kw["collective_id"] = ride.collective_id
    return pltpu.CompilerParams(vmem_limit_bytes=VMEM_LIMIT, **kw)


def _vmem_spec():
    return pl.BlockSpec(memory_space=pltpu.VMEM)


def _dot(a, b):
    return jnp.dot(a, b, preferred_element_type=F32)


def _dot_nt(a, b):
    return lax.dot_general(a, b, (((1,), (1,)), ((), ())), preferred_element_type=F32)


def _dot_tn(a, b):
    return lax.dot_general(a, b, (((0,), (0,)), ((), ())), preferred_element_type=F32)


def _sigmoid(x):
    return 1.0 / (1.0 + jnp.exp(-x))


def _rms_fwd(x, gain, width):
    r = lax.rsqrt(jnp.sum(x * x, axis=-1, keepdims=True) * (1.0 / width) + RMS_EPS)
    return x * r * gain


def _rms_bwd(dy, x, gain, width):
    r = lax.rsqrt(jnp.sum(x * x, axis=-1, keepdims=True) * (1.0 / width) + RMS_EPS)
    xhat = x * r
    dxhat = dy * gain
    dx = r * (dxhat - xhat * (jnp.sum(dxhat * xhat, axis=-1, keepdims=True) * (1.0 / width)))
    return dx, dy * xhat


def _split3(x):
    hi = x.astype(BF16)
    r1 = x - hi.astype(F32)
    mid = r1.astype(BF16)
    lo = (r1 - mid.astype(F32)).astype(BF16)
    return hi, mid, lo


def _tri_sum(tri_bf16, x):
    hi, mid, lo = _split3(x)
    return _dot(tri_bf16, hi) + _dot(tri_bf16, mid) + _dot(tri_bf16, lo)


def _dilated_spec(d, tm, width):
    return pl.BlockSpec((d, tm // d, width), lambda i: (0, i, 0))


def _lane_blocks(ref, value):
    for c in range(ref.shape[0]):
        ref[c] = value[:, c * LANES:(c + 1) * LANES]


def _to_dilated(src_ref, dst_ref, d, tm, cast=None):
    for r in range(d):
        for c in range(src_ref.shape[0]):
            v = src_ref[c] if d == 1 else src_ref[c, pl.ds(r, tm // d, stride=d), :]
            dst_ref[r, :, c * LANES:(c + 1) * LANES] = v if cast is None else v.astype(cast)


def _from_dilated(src_ref, scratch_ref, d, tm):
    if d == 1:
        return src_ref[0].astype(F32)
    nblk = scratch_ref.shape[0]
    for r in range(d):
        for c in range(nblk):
            scratch_ref[c, pl.ds(r, tm // d, stride=d), :] = src_ref[r, :, c * LANES:(c + 1) * LANES].astype(F32)
    return jnp.concatenate([scratch_ref[c] for c in range(nblk)], axis=1)


def in_proj_fwd(x, g1, w_in_b, ride=None):
    s = x.shape[0]
    tm = PROJ_TILE
    qkv_w = 3 * ATTN_WIDTH
    hg_w = IN_PROJ_WIDTH - qkv_w

    def body(x_ref, g_ref, w_ref, hg_ref, h_ref, *rest):
        qkv_refs, qkv_scr = rest[:len(DILATIONS)], rest[len(DILATIONS)]
        h = _rms_fwd(x_ref[...], g_ref[...], D_MODEL).astype(BF16)
        h_ref[...] = h
        proj = _dot(h, w_ref[...])
        hg_ref[...] = proj[:, qkv_w:]
        _lane_blocks(qkv_scr, proj[:, :qkv_w])
        for d, ref in zip(DILATIONS, qkv_refs):
            _to_dilated(qkv_scr, ref, d, tm, cast=BF16)

    n_steps = s // tm
    step = lambda k: (lambda: pl.program_id(0) == k)
    e_in, e_out, e_shape, e_scr, e_args = _ride_specs(ride)
    return pl.pallas_call(
        _riding(body, 3, 2 + len(DILATIONS), 1, ride, step(0), step(n_steps // 2), step(n_steps - 1)),
        name="in_proj_fwd",
        grid=(n_steps,),
        in_specs=[
            pl.BlockSpec((tm, D_MODEL), lambda i: (i, 0)),
            pl.BlockSpec((1, D_MODEL), lambda i: (0, 0)),
            _vmem_spec(),
        ] + e_in,
        out_specs=[
            pl.BlockSpec((tm, hg_w), lambda i: (i, 0)),
            pl.BlockSpec((tm, D_MODEL), lambda i: (i, 0)),
        ] + [_dilated_spec(d, tm, qkv_w) for d in DILATIONS] + e_out,
        out_shape=[jax.ShapeDtypeStruct((s, hg_w), F32), jax.ShapeDtypeStruct((s, D_MODEL), BF16)] + [
            jax.ShapeDtypeStruct((d, s // d, qkv_w), BF16) for d in DILATIONS] + e_shape,
        scratch_shapes=[pltpu.VMEM((qkv_w // LANES, tm, LANES), F32)] + e_scr,
        compiler_params=_params(ride, dimension_semantics=("arbitrary",)),
    )(x, g1, w_in_b, *e_args)


ATTN_SCALE = ATTN_HEAD_DIM ** -0.5


def _fill_attn_bias(bias_ref, dilation):
    qi = lax.broadcasted_iota(jnp.int32, (ATTN_BLOCK, 2 * ATTN_BLOCK), 0)
    kj = lax.broadcasted_iota(jnp.int32, (ATTN_BLOCK, 2 * ATTN_BLOCK), 1)
    dist = qi + ATTN_BLOCK - kj
    valid = (dist >= 0) & (dist <= ATTN_BLOCK)
    for head in range(ATTN_HEADS):
        slope = 2.0 ** (-8.0 * (head + 1) / ATTN_HEADS)
        bias = jnp.where(valid, dist.astype(F32) * (-slope * dilation), NEG_BIG)
        bias_ref[0, head] = bias
        bias_ref[1, head] = jnp.where(kj >= ATTN_BLOCK, bias, NEG_BIG)


def _stack_heads(x):
    low = _lane_half(x.shape, 0)
    zero = jnp.zeros_like(x)
    return jnp.concatenate([jnp.where(low, x, zero), jnp.where(low, zero, x)], axis=0)


def _unstack_heads(y):
    half = y.shape[0] // 2
    return jnp.where(_lane_half((half, y.shape[1]), 0), y[:half], y[half:])


def _attn_scores(q_stack, kcat, bias_ref, pair, first_block):
    f = first_block.astype(jnp.int32)
    bias = jnp.concatenate([bias_ref[f, 2 * pair], bias_ref[f, 2 * pair + 1]], axis=0)
    return _dot_nt(q_stack, kcat) + bias


def _lane_half(shape, sub):
    lane = lax.broadcasted_iota(jnp.int32, shape, 1)
    return (lane < ATTN_HEAD_DIM) if sub == 0 else (lane >= ATTN_HEAD_DIM)


def _sub_block(col, row):
    return pl.BlockSpec((None, ATTN_BLOCK, ATTN_WIDTH), lambda r, n: (r, row(n), col))


def attn_fwd(qkv, dilation):
    d, length, _ = qkv.shape
    assert d == dilation
    nb = length // ATTN_BLOCK

    def body(q_ref, kc_ref, kp_ref, vc_ref, vp_ref, o_ref, lse_ref, bias_ref):
        @pl.when((pl.program_id(0) == 0) & (pl.program_id(1) == 0))
        def _():
            _fill_attn_bias(bias_ref, d)

        first = pl.program_id(1) == 0
        for pair in range(ATTN_HEADS // 2):
            lanes = slice(pair * LANES, (pair + 1) * LANES)
            q_stack = _stack_heads(q_ref[:, lanes] * ATTN_SCALE)
            kcat = jnp.concatenate([kp_ref[:, lanes], kc_ref[:, lanes]], axis=0)
            vcat = jnp.concatenate([vp_ref[:, lanes], vc_ref[:, lanes]], axis=0)
            sc = _attn_scores(q_stack, kcat, bias_ref, pair, first)
            m = jnp.max(sc, axis=-1, keepdims=True)
            p = jnp.exp(sc - m)
            den = jnp.sum(p, axis=-1, keepdims=True)
            o_ref[:, lanes] = _unstack_heads(_dot(p.astype(BF16), vcat) / den).astype(BF16)
            lse_ref[:, lanes] = _unstack_heads(jnp.broadcast_to(m + jnp.log(den), (2 * ATTN_BLOCK, LANES)))

    cur = lambda n: n
    prev = lambda n: jnp.maximum(n - 1, 0)
    return pl.pallas_call(
        body,
        name=f"attn_fwd_d{d}",
        grid=(d, nb),
        in_specs=[_sub_block(0, cur), _sub_block(1, cur), _sub_block(1, prev), _sub_block(2, cur), _sub_block(2, prev)],
        out_specs=[_sub_block(0, cur), _sub_block(0, cur)],
        out_shape=[jax.ShapeDtypeStruct((d, length, ATTN_WIDTH), BF16), jax.ShapeDtypeStruct((d, length, ATTN_WIDTH), F32)],
        scratch_shapes=[pltpu.VMEM((2, ATTN_HEADS, ATTN_BLOCK, 2 * ATTN_BLOCK), F32)],
        compiler_params=_params(dimension_semantics=("arbitrary", "arbitrary")),
    )(qkv, qkv, qkv, qkv, qkv)


def attn_bwd(qkv, d_out, lse, delta, dilation, ride=None):
    d, length, _ = qkv.shape
    assert d == dilation
    nb = length // ATTN_BLOCK

    steps = d * nb + 1

    def body(q_ref, kc_ref, kp_ref, vc_ref, vp_ref, do_ref, lse_ref, dl_ref, dq_ref, dk_ref, dv_ref, ck_ref, cv_ref,
             bias_ref):
        t = pl.program_id(0)

        @pl.when(t == 0)
        def _():
            ck_ref[...] = jnp.zeros_like(ck_ref)
            cv_ref[...] = jnp.zeros_like(cv_ref)
            _fill_attn_bias(bias_ref, d)

        @pl.when(t < steps - 1)
        def _():
            first = t % nb == 0
            for pair in range(ATTN_HEADS // 2):
                lanes = slice(pair * LANES, (pair + 1) * LANES)
                q_stack = _stack_heads(q_ref[:, lanes] * ATTN_SCALE)
                do_stack = _stack_heads(do_ref[:, lanes])
                kcat = jnp.concatenate([kp_ref[:, lanes], kc_ref[:, lanes]], axis=0)
                vcat = jnp.concatenate([vp_ref[:, lanes], vc_ref[:, lanes]], axis=0)
                col_a, col_b = 2 * pair, 2 * pair + 1
                lse_col = jnp.concatenate([lse_ref[:, col_a:col_a + 1], lse_ref[:, col_b:col_b + 1]], axis=0)
                dl_col = jnp.concatenate([dl_ref[:, col_a:col_a + 1], dl_ref[:, col_b:col_b + 1]], axis=0)
                p = jnp.exp(_attn_scores(q_stack, kcat, bias_ref, pair, first) - lse_col)
                ds = (p * (_dot_nt(do_stack, vcat) - dl_col)).astype(BF16)
                dq_ref[:, lanes] = (_unstack_heads(_dot(ds, kcat)) * ATTN_SCALE).astype(BF16)
                dk_cat = _dot_tn(ds, q_stack)
                dv_cat = _dot_tn(p.astype(BF16), do_stack)
                dk_ref[:, lanes] = (ck_ref[:, lanes] + dk_cat[:ATTN_BLOCK]).astype(BF16)
                dv_ref[:, lanes] = (cv_ref[:, lanes] + dv_cat[:ATTN_BLOCK]).astype(BF16)
                ck_ref[:, lanes] = dk_cat[ATTN_BLOCK:]
                cv_ref[:, lanes] = dv_cat[ATTN_BLOCK:]

        @pl.when(t == steps - 1)
        def _():
            dk_ref[...] = ck_ref[...].astype(BF16)
            dv_ref[...] = cv_ref[...].astype(BF16)

    blk = (ATTN_BLOCK, ATTN_WIDTH)

    def spec(col, shift, width=ATTN_WIDTH):
        def index(t):
            f = jnp.minimum(t, steps - 2) if shift > -2 else jnp.maximum(t - 1, 0)
            r, n = f // nb, f % nb
            return (r, jnp.maximum(n - 1, 0) if shift == -1 else n, col)
        return pl.BlockSpec((None, ATTN_BLOCK, width), index)

    step = lambda k: (lambda: pl.program_id(0) == k)
    e_in, e_out, e_shape, e_scr, e_args = _ride_specs(ride)
    return pl.pallas_call(
        _riding(body, 8, 3, 3, ride, step(0), step(steps // 2), step(steps - 1)),
        name=f"attn_bwd_d{d}",
        grid=(steps,),
        in_specs=[spec(0, 0), spec(1, 0), spec(1, -1), spec(2, 0), spec(2, -1), spec(0, 0), spec(0, 0, LANES),
                  spec(0, 0, LANES)] + e_in,
        out_specs=[spec(0, 0), spec(0, -2), spec(0, -2)] + e_out,
        out_shape=[jax.ShapeDtypeStruct((d, length, ATTN_WIDTH), BF16)] * 3 + e_shape,
        scratch_shapes=[pltpu.VMEM(blk, F32), pltpu.VMEM(blk, F32),
                        pltpu.VMEM((2, ATTN_HEADS, ATTN_BLOCK, 2 * ATTN_BLOCK), F32)] + e_scr,
        compiler_params=_params(ride, dimension_semantics=("arbitrary",)),
    )(qkv, qkv, qkv, qkv, qkv, d_out, lse, delta, *e_args)


def _lower_bound(logits):
    return _sigmoid(logits[0:1, :] - logits[1:2, :])


def _hgrn_gates(q, fp, lb):
    sq = _sigmoid(q)
    qf = q * sq
    sig = _sigmoid(fp)
    sig_neg = _sigmoid(-fp)
    kf = (1.0 - lb) * sig_neg
    log_sig = jnp.minimum(fp, 0.0) - jnp.log(1.0 + jnp.exp(-jnp.abs(fp)))
    a = jnp.log(lb)
    c = jnp.log(1.0 - lb) + log_sig
    log_f = jnp.maximum(a, c) + jnp.log(1.0 + jnp.exp(-jnp.abs(a - c)))
    return sq, qf, (sig, sig_neg, c), log_f, kf


def _tril_bf16(n, upper=False):
    r = lax.broadcasted_iota(jnp.int32, (n, n), 0)
    c = lax.broadcasted_iota(jnp.int32, (n, n), 1)
    keep = (c >= r) if upper else (c <= r)
    return jnp.where(keep, 1.0, 0.0).astype(BF16)


def _hgrn_diagonal_loops(c_len, diagonal):
    for half in range(SUB_BLOCK // SUBLANES):
        def step(jj, carry, half=half):
            j = half * SUBLANES + jj
            for i in range(c_len // SUB_BLOCK):
                diagonal(slice(i * SUB_BLOCK + half * SUBLANES, (i + 1) * SUB_BLOCK), j, i * SUB_BLOCK + j)
            return carry

        lax.fori_loop(0, SUBLANES, step, 0, unroll=COLUMN_UNROLL)


def _hgrn_off_diagonal(b, qf, kf):
    c_len, width = b.shape
    edges = [b[0:1, :]] + [b[i * SUB_BLOCK - 1:i * SUB_BLOCK, :] for i in range(1, c_len // SUB_BLOCK)]
    eq = jnp.exp(b - jnp.concatenate([jnp.broadcast_to(e, (SUB_BLOCK, width)) for e in edges], axis=0))
    q_til = qf * eq
    k_til, ek = [], []
    for i in range(1, c_len // SUB_BLOCK):
        n = i * SUB_BLOCK
        e = jnp.exp(edges[i] - b[:n, :])
        ek.append(e)
        k_til.append(jnp.concatenate([kf[:n, :] * e, jnp.zeros((2 * c_len - n, width), F32)], axis=0))
    return q_til, k_til, eq, ek


def _split2(x):
    hi = x.astype(BF16)
    return hi, (x - hi.astype(F32)).astype(BF16)


def hgrn_fwd(proj, lb, ride=None):
    s = proj.shape[0]
    c_len, nh, hd = HGRN_CHUNK, HGRN_HEADS, HGRN_HEAD_DIM
    n_chunks = s // c_len
    col0 = 0

    cps = 2 * HGRN_CHUNKS_PER_STEP
    n_steps = n_chunks // cps

    def body(q_ref, f_ref, i_ref, lb_ref, o_ref, st_out_ref, a_out_ref, st_ref, b_ref, qf_ref, kf_ref, a_ref):
        @pl.when(pl.program_id(0) == 0)
        def _():
            st_ref[...] = jnp.zeros_like(st_ref)

        lbv = _lower_bound(lb_ref[...])
        for u in range(cps):
            rs = slice(u * c_len, (u + 1) * c_len)
            b_u, qf_u, kf_u, a_u = b_ref.at[u], qf_ref.at[u], kf_ref.at[u], a_ref.at[u]
            _, qf, _, log_f, kf = _hgrn_gates(q_ref[rs, :], f_ref[rs, :], lbv)
            b = _tri_sum(_tril_bf16(c_len), log_f)
            b_u[...] = b
            qf_u[...] = qf
            kf_u[...] = kf
            a_u[...] = jnp.zeros_like(a_u)

            def diagonal(rows, j, key, b_u=b_u, qf_u=qf_u, kf_u=kf_u, a_u=a_u):
                bj = b_u[pl.ds(key, 1), :]
                kj = kf_u[pl.ds(key, 1), :]
                nrow = rows.stop - rows.start
                t_loc = lax.broadcasted_iota(jnp.int32, (nrow, nh * hd), 0) + (rows.start % SUB_BLOCK)
                e = jnp.exp(jnp.where(t_loc >= j, b_u[rows, :] - bj, NEG_BIG))
                prod = qf_u[rows, :] * kj * e
                lane = lax.broadcasted_iota(jnp.int32, (nrow, hd), 1)
                for h in range(nh):
                    col = jnp.sum(prod[:, h * hd:(h + 1) * hd], axis=-1, keepdims=True)
                    a_u[h, rows, :] = jnp.where(lane == key, col, a_u[h, rows, :])

            _hgrn_diagonal_loops(c_len, diagonal)
            q_til, k_til, _, _ = _hgrn_off_diagonal(b, qf, kf)
            q_til = q_til.astype(BF16)
            k_til = [k.astype(BF16) for k in k_til]

            b_last = b[c_len - 1:c_len, :]
            qb = (qf * jnp.exp(b)).astype(BF16)
            kb2 = (kf * jnp.exp(b_last - b)).astype(BF16)
            vf = i_ref[rs, :].astype(BF16)
            for h in range(nh):
                hs = slice(h * hd, (h + 1) * hd)
                st = st_ref[h]
                st_out_ref[u, h] = st
                off = [jnp.zeros((SUB_BLOCK, hd), F32)]
                for i in range(1, c_len // SUB_BLOCK):
                    off.append(_dot_nt(q_til[i * SUB_BLOCK:(i + 1) * SUB_BLOCK, hs], k_til[i - 1][:, hs]))
                a_h = a_u[h] + jnp.concatenate(off, axis=0)
                a_out_ref[rs, hs] = a_h
                o_ref[rs, hs] = _dot_nt(qb[:, hs], st.astype(BF16)) + _dot(a_h[:, :c_len].astype(BF16), vf[:, hs])
                st_ref[h] = st * jnp.exp(b_last[:, hs]) + _dot_tn(vf[:, hs], kb2[:, hs])

    blk = (cps * c_len, HGRN_WIDTH)
    sblk = (cps, c_len, HGRN_WIDTH)
    step = lambda k: (lambda: pl.program_id(0) == k)
    e_in, e_out, e_shape, e_scr, e_args = _ride_specs(ride)
    return pl.pallas_call(
        _riding(body, 4, 3, 5, ride, step(0), step(n_steps // 2), step(n_steps - 1)),
        name="hgrn_fwd",
        grid=(n_steps,),
        in_specs=[
            pl.BlockSpec(blk, lambda c: (c, col0)),
            pl.BlockSpec(blk, lambda c: (c, col0 + 1)),
            pl.BlockSpec(blk, lambda c: (c, col0 + 2)),
            pl.BlockSpec((2, HGRN_WIDTH), lambda c: (0, 0)),
        ] + e_in,
        out_specs=[
            pl.BlockSpec(blk, lambda c: (c, 0)),
            pl.BlockSpec((cps, nh, hd, hd), lambda c: (c, 0, 0, 0)),
            pl.BlockSpec(blk, lambda c: (c, 0)),
        ] + e_out,
        out_shape=[
            jax.ShapeDtypeStruct((s, HGRN_WIDTH), F32),
            jax.ShapeDtypeStruct((n_chunks, nh, hd, hd), F32),
            jax.ShapeDtypeStruct((s, nh * hd), F32),
        ] + e_shape,
        scratch_shapes=[
            pltpu.VMEM((nh, hd, hd), F32),
            pltpu.VMEM(sblk, F32),
            pltpu.VMEM(sblk, F32),
            pltpu.VMEM(sblk, F32),
            pltpu.VMEM((cps, nh, c_len, hd), F32),
        ] + e_scr,
        compiler_params=_params(ride, dimension_semantics=("arbitrary",)),
    )(proj, proj, proj, lb, *e_args)


def hgrn_bwd(proj, lb, d_o, states, a_mat, ride=None):
    s = proj.shape[0]
    c_len, nh, hd = HGRN_CHUNK, HGRN_HEADS, HGRN_HEAD_DIM
    n_chunks = s // c_len
    col0 = 0
    cps = HGRN_CHUNKS_PER_STEP
    n_steps = n_chunks // cps
    last = n_steps - 1

    def body(q_ref, f_ref, i_ref, lb_ref, do_ref, st_in_ref, a_in_ref, dq_ref, df_ref, di_ref, dlb_ref,
             dst_ref, b_ref, qf_ref, kf_ref, da_ref, dqi_ref, dki_ref):
        @pl.when(pl.program_id(0) == 0)
        def _():
            dst_ref[...] = jnp.zeros_like(dst_ref)
            dlb_ref[...] = jnp.zeros_like(dlb_ref)

        lbv = _lower_bound(lb_ref[...])
        for u in reversed(range(cps)):
            rs = slice(u * c_len, (u + 1) * c_len)
            b_u, qf_u, kf_u, da_u, dqi_u, dki_u = (b_ref.at[u], qf_ref.at[u], kf_ref.at[u], da_ref.at[u], dqi_ref.at[u],
                                                   dki_ref.at[u])
            q = q_ref[rs, :]
            sq, qf, (sig, sig_neg, log_c), log_f, kf = _hgrn_gates(q, f_ref[rs, :], lbv)
            b = _tri_sum(_tril_bf16(c_len), log_f)
            b_u[...] = b
            qf_u[...] = qf
            kf_u[...] = kf
            b_last = b[c_len - 1:c_len, :]
            eb = jnp.exp(b)
            ebl = jnp.exp(b_last - b)
            qb = qf * eb
            kb2 = kf * ebl
            vf = i_ref[rs, :]
            d_o = do_ref[rs, :]
            qb_b, kb2_b, vf_b, do_b = qb.astype(BF16), kb2.astype(BF16), vf.astype(BF16), d_o.astype(BF16)
            tq = lax.broadcasted_iota(jnp.int32, (c_len, hd), 0)
            lane = lax.broadcasted_iota(jnp.int32, (c_len, hd), 1)

            dqb_parts, dvf_parts, dkb2_parts, dbl_parts = [], [], [], []
            for h in range(nh):
                hs = slice(h * hd, (h + 1) * hd)
                st = st_in_ref[u, h]
                dst = dst_ref[h]
                st_b, dst_b = st.astype(BF16), dst.astype(BF16)
                a_h = a_in_ref[rs, hs][:, :c_len].astype(BF16)
                dqb_parts.append(_dot(do_b[:, hs], st_b))
                dvf_parts.append(_dot_tn(a_h, do_b[:, hs]) + _dot_nt(kb2_b[:, hs], dst_b))
                dkb2_parts.append(_dot(vf_b[:, hs], dst_b))
                da = _dot_nt(do_b[:, hs], vf_b[:, hs])
                da = jnp.concatenate([da, jnp.zeros((c_len, hd - c_len), F32)], axis=1)
                da_u[h] = jnp.where(tq >= lane, da, 0.0)
                dbl_parts.append(jnp.sum(dst * st, axis=0, keepdims=True) * jnp.exp(b_last[:, hs]))
                dst_ref[h] = dst * jnp.exp(b_last[:, hs]) + _dot_tn(do_b[:, hs], qb_b[:, hs])
            dqb = jnp.concatenate(dqb_parts, axis=1)
            dvf = jnp.concatenate(dvf_parts, axis=1)
            dkb2 = jnp.concatenate(dkb2_parts, axis=1)
            dbl = jnp.concatenate(dbl_parts, axis=1) + jnp.sum(dkb2 * kb2, axis=0, keepdims=True)

            dqi_u[...] = jnp.zeros_like(dqi_u)
            t_idx = lax.broadcasted_iota(jnp.int32, (c_len, nh * hd), 0)

            def diagonal(rows, j, key, b_u=b_u, qf_u=qf_u, kf_u=kf_u, da_u=da_u, dqi_u=dqi_u, dki_u=dki_u):
                bj = b_u[pl.ds(key, 1), :]
                kj = kf_u[pl.ds(key, 1), :]
                nrow = rows.stop - rows.start
                t_loc = lax.broadcasted_iota(jnp.int32, (nrow, nh * hd), 0) + (rows.start % SUB_BLOCK)
                e = jnp.exp(jnp.where(t_loc >= j, b_u[rows, :] - bj, NEG_BIG))
                lane_r = lax.broadcasted_iota(jnp.int32, (nrow, hd), 1)
                cols = [jnp.sum(jnp.where(lane_r == key, da_u[h, rows, :], 0.0), axis=-1, keepdims=True)
                        for h in range(nh)]
                w = e * jnp.concatenate([jnp.broadcast_to(cc, (nrow, hd)) for cc in cols], axis=1)
                dqi_u[rows, :] += w * kj
                dki_u[pl.ds(key, 1), :] = jnp.sum(w * qf_u[rows, :], axis=0, keepdims=True)

            _hgrn_diagonal_loops(c_len, diagonal)

            q_til, k_til, eq, ek = _hgrn_off_diagonal(b, qf, kf)
            q_hi, q_lo = _split2(q_til)
            k_pairs = [_split2(k) for k in k_til]
            n_sub = c_len // SUB_BLOCK
            dq_heads, dk_heads = [], []
            for h in range(nh):
                hs = slice(h * hd, (h + 1) * hd)
                dq_rows = [jnp.zeros((SUB_BLOCK, hd), F32)]
                dk_h = jnp.zeros((c_len, hd), F32)
                for i in range(1, n_sub):
                    rows = slice(i * SUB_BLOCK, (i + 1) * SUB_BLOCK)
                    n = i * SUB_BLOCK
                    da_i = da_u[h, rows, :].astype(BF16)
                    k_hi, k_lo = k_pairs[i - 1]
                    dq_rows.append((_dot(da_i, k_hi[:, hs]) + _dot(da_i, k_lo[:, hs])) * eq[rows, hs])
                    dk_t = (_dot_tn(da_i, q_hi[rows, hs]) + _dot_tn(da_i, q_lo[rows, hs]))[:n, :] * ek[i - 1][:, hs]
                    dk_h = dk_h + jnp.concatenate([dk_t, jnp.zeros((c_len - n, hd), F32)], axis=0)
                dq_heads.append(jnp.concatenate(dq_rows, axis=0))
                dk_heads.append(dk_h)
            dq_intra = dqi_u[...] + jnp.concatenate(dq_heads, axis=1)
            dk_intra = dki_u[...] + jnp.concatenate(dk_heads, axis=1)

            db = dqb * qb + qf * dq_intra - kf * dk_intra - dkb2 * kb2
            db = db + jnp.where(t_idx == c_len - 1, dbl, 0.0)
            dg = _tri_sum(_tril_bf16(c_len, upper=True), db)
            dqf = dqb * eb + dq_intra
            dkf = dkb2 * ebl + dk_intra
            dq_ref[rs, :] = (dqf * (sq * (1.0 + q * (1.0 - sq)))).astype(BF16)
            df_ref[rs, :] = (sig_neg * (dg * jnp.exp(log_c - log_f) - dkf * (1.0 - lbv) * sig)).astype(BF16)
            di_ref[rs, :] = dvf.astype(BF16)
            dlb_ref[...] += jnp.sum(sig_neg * (dg * jnp.exp(-log_f) - dkf), axis=0, keepdims=True)

    blk = (cps * c_len, HGRN_WIDTH)
    sblk = (cps, c_len, HGRN_WIDTH)
    rev = lambda c: last - c
    step = lambda k: (lambda: pl.program_id(0) == k)
    e_in, e_out, e_shape, e_scr, e_args = _ride_specs(ride)
    return pl.pallas_call(
        _riding(body, 7, 4, 7, ride, step(0), step(n_steps // 2), step(last)),
        name="hgrn_bwd",
        grid=(n_steps,),
        in_specs=[
            pl.BlockSpec(blk, lambda c: (rev(c), col0)),
            pl.BlockSpec(blk, lambda c: (rev(c), col0 + 1)),
            pl.BlockSpec(blk, lambda c: (rev(c), col0 + 2)),
            pl.BlockSpec((2, HGRN_WIDTH), lambda c: (0, 0)),
            pl.BlockSpec(blk, lambda c: (rev(c), 0)),
            pl.BlockSpec((cps, nh, hd, hd), lambda c: (rev(c), 0, 0, 0)),
            pl.BlockSpec(blk, lambda c: (rev(c), 0)),
        ] + e_in,
        out_specs=[
            pl.BlockSpec(blk, lambda c: (rev(c), 0)),
            pl.BlockSpec(blk, lambda c: (rev(c), 0)),
            pl.BlockSpec(blk, lambda c: (rev(c), 0)),
            pl.BlockSpec((1, HGRN_WIDTH), lambda c: (0, 0)),
        ] + e_out,
        out_shape=[jax.ShapeDtypeStruct((s, HGRN_WIDTH), BF16)] * 3 + [jax.ShapeDtypeStruct((1, HGRN_WIDTH), F32)] + e_shape,
        scratch_shapes=[
            pltpu.VMEM((nh, hd, hd), F32),
            pltpu.VMEM(sblk, F32),
            pltpu.VMEM(sblk, F32),
            pltpu.VMEM(sblk, F32),
            pltpu.VMEM((cps, nh, c_len, hd), F32),
            pltpu.VMEM(sblk, F32),
            pltpu.VMEM(sblk, F32),
        ] + e_scr,
        compiler_params=_params(ride, dimension_semantics=("arbitrary",)),
    )(proj, proj, proj, lb, d_o, states, a_mat, *e_args)


def _per_head_lanes(x):
    lane = lax.broadcasted_iota(jnp.int32, (x.shape[0], LANES), 1)
    out = jnp.zeros((x.shape[0], LANES), F32)
    for h in range(ATTN_HEADS):
        out = jnp.where(lane == h, x[:, h * ATTN_HEAD_DIM:h * ATTN_HEAD_DIM + 1], out)
    return out


def _row_spec(tm, width, col=0):
    return pl.BlockSpec((tm, width), lambda i: (i, col))


def _const_spec(width):
    return pl.BlockSpec((1, width), lambda i: (0, 0))


def _acc_rows(ref, value):
    @pl.when(pl.program_id(0) == 0)
    def _():
        ref[...] = jnp.zeros_like(ref)

    ref[...] += jnp.sum(value, axis=0, keepdims=True)


def mix_fwd(attn_parts, o_h, proj, an, hn, w_out_b, gp, x, ride=None):
    s = x.shape[0]
    tm = TOKEN_TILE
    gate_col = 3
    hd = HGRN_HEAD_DIM
    nd = len(DILATIONS)

    def body(*refs):
        o_refs, l_refs = refs[:nd], refs[nd:2 * nd]
        oh_ref, gate_ref, an_ref, hn_ref, w_ref, gp_ref, x_ref = refs[2 * nd:2 * nd + 7]
        x1_ref, cat_ref, mixed_ref, attn_ref = refs[2 * nd + 7:2 * nd + 11]
        lse_refs = refs[2 * nd + 11:3 * nd + 11]
        o_scr, l_scr, lse_scr = refs[3 * nd + 11:]
        os_ = [_from_dilated(r, o_scr.at[k], d, tm) for k, (r, d) in enumerate(zip(o_refs, DILATIONS))]
        ls = [_from_dilated(r, l_scr.at[k], d, tm) for k, (r, d) in enumerate(zip(l_refs, DILATIONS))]
        m = jnp.maximum(jnp.maximum(ls[0], ls[1]), ls[2])
        es = [jnp.exp(l - m) for l in ls]
        den = es[0] + es[1] + es[2]
        attn = (es[0] * os_[0] + es[1] * os_[1] + es[2] * os_[2]) / den
        attn_ref[...] = attn
        lse_scr[0] = _per_head_lanes(m + jnp.log(den))
        for d, ref in zip(DILATIONS, lse_refs):
            _to_dilated(lse_scr, ref, d, tm)
        cat_ref[:, :ATTN_WIDTH] = _rms_fwd(attn, an_ref[...], ATTN_WIDTH).astype(BF16)
        gate = gate_ref[...]
        silu_g = gate * _sigmoid(gate)
        for h in range(HGRN_HEADS):
            hs = slice(h * hd, (h + 1) * hd)
            rec = _rms_fwd(oh_ref[:, hs], hn_ref[:, hs], hd) * silu_g[:, hs]
            cat_ref[:, ATTN_WIDTH + h * hd:ATTN_WIDTH + (h + 1) * hd] = rec.astype(BF16)
        mixed = _dot(cat_ref[...], w_ref[...])
        mixed_ref[...] = mixed
        x1_ref[...] = x_ref[...] + _rms_fwd(mixed, gp_ref[...], D_MODEL)

    aw = ATTN_WIDTH
    n_steps = s // tm
    step = lambda k: (lambda: pl.program_id(0) == k)
    e_in, e_out, e_shape, e_scr, e_args = _ride_specs(ride)
    return pl.pallas_call(
        _riding(body, 2 * nd + 7, 4 + nd, 3, ride, step(0), step((13 * n_steps) // 16), step(n_steps - 1)),
        name="mix_fwd",
        grid=(n_steps,),
        in_specs=[_dilated_spec(d, tm, aw) for d in DILATIONS] * 2 + [
            _row_spec(tm, aw), _row_spec(tm, aw, gate_col), _const_spec(aw), _const_spec(aw), _vmem_spec(),
            _const_spec(D_MODEL), _row_spec(tm, D_MODEL)] + e_in,
        out_specs=[_row_spec(tm, D_MODEL), _row_spec(tm, D_MODEL), _row_spec(tm, D_MODEL), _row_spec(tm, aw)] + [
            _dilated_spec(d, tm, LANES) for d in DILATIONS] + e_out,
        out_shape=[
            jax.ShapeDtypeStruct((s, D_MODEL), F32),
            jax.ShapeDtypeStruct((s, D_MODEL), BF16),
            jax.ShapeDtypeStruct((s, D_MODEL), F32),
            jax.ShapeDtypeStruct((s, aw), F32),
        ] + [jax.ShapeDtypeStruct((d, s // d, LANES), F32) for d in DILATIONS] + e_shape,
        scratch_shapes=[pltpu.VMEM((nd, aw // LANES, tm, LANES), F32), pltpu.VMEM((nd, aw // LANES, tm, LANES), F32),
                        pltpu.VMEM((1, tm, LANES), F32)] + e_scr,
        compiler_params=_params(ride, dimension_semantics=("arbitrary",)),
    )(*[p[0] for p in attn_parts], *[p[1] for p in attn_parts], o_h, proj, an, hn, w_out_b, gp, x, *e_args)


def mix_bwd(dx1, mixed, gp, w_out_b, attn, an, o_h, proj, hn):
    s = dx1.shape[0]
    tm = TOKEN_TILE
    gate_col = 3
    hd = HGRN_HEAD_DIM
    aw = ATTN_WIDTH

    nd = len(DILATIONS)

    def body(*refs):
        dx1_ref, mixed_ref, gp_ref, w_ref, attn_ref, an_ref, oh_ref, gate_ref, hn_ref, dmix_ref = refs[:10]
        do_refs, delta_refs = refs[10:10 + nd], refs[10 + nd:10 + 2 * nd]
        doh_ref, dgate_ref, dgp_ref, dan_ref, dhn_ref, do_ref, delta_ref = refs[10 + 2 * nd:]
        dmixed, gp_c = _rms_bwd(dx1_ref[...], mixed_ref[...], gp_ref[...], D_MODEL)
        _acc_rows(dgp_ref, gp_c)
        dmixed_b = dmixed.astype(BF16)
        dmix_ref[...] = dmixed_b
        dcat = _dot_nt(dmixed_b, w_ref[...])
        attn = attn_ref[...]
        d_o, an_c = _rms_bwd(dcat[:, :aw], attn, an_ref[...], aw)
        _acc_rows(dan_ref, an_c)
        _lane_blocks(do_ref, d_o)
        prod = d_o * attn
        lane = lax.broadcasted_iota(jnp.int32, (tm, LANES), 1)
        delta = jnp.zeros((tm, LANES), F32)
        for pair in range(ATTN_HEADS // 2):
            pp = prod[:, pair * LANES:(pair + 1) * LANES]
            low = _lane_half((tm, LANES), 0)
            lo = jnp.sum(jnp.where(low, pp, 0.0), axis=-1, keepdims=True)
            hi = jnp.sum(jnp.where(low, 0.0, pp), axis=-1, keepdims=True)
            delta = jnp.where(lane == 2 * pair, lo, jnp.where(lane == 2 * pair + 1, hi, delta))
        delta_ref[0] = delta
        for d, o_ref, l_ref in zip(DILATIONS, do_refs, delta_refs):
            _to_dilated(do_ref, o_ref, d, tm, cast=BF16)
            _to_dilated(delta_ref, l_ref, d, tm)
        gate = gate_ref[...]
        sg = _sigmoid(gate)
        silu_g = gate * sg
        drec = dcat[:, aw:]
        hn_parts = []
        for h in range(HGRN_HEADS):
            hs = slice(h * hd, (h + 1) * hd)
            oh = oh_ref[:, hs]
            on = _rms_fwd(oh, hn_ref[:, hs], hd)
            dgate_ref[:, hs] = (drec[:, hs] * on * (sg[:, hs] * (1.0 + gate[:, hs] * (1.0 - sg[:, hs])))).astype(BF16)
            d_oh, hn_c = _rms_bwd(drec[:, hs] * silu_g[:, hs], oh, hn_ref[:, hs], hd)
            doh_ref[:, hs] = d_oh
            hn_parts.append(hn_c)
        _acc_rows(dhn_ref, jnp.concatenate(hn_parts, axis=1))

    return pl.pallas_call(
        body,
        name="mix_bwd",
        grid=(s // tm,),
        in_specs=[_row_spec(tm, D_MODEL), _row_spec(tm, D_MODEL), _const_spec(D_MODEL), _vmem_spec(), _row_spec(tm, aw),
                  _const_spec(aw), _row_spec(tm, aw), _row_spec(tm, aw, gate_col), _const_spec(aw)],
        out_specs=[_row_spec(tm, D_MODEL)] + [_dilated_spec(d, tm, aw) for d in DILATIONS] + [
            _dilated_spec(d, tm, LANES) for d in DILATIONS] + [_row_spec(tm, aw)] * 2 + [
            _const_spec(D_MODEL), _const_spec(aw), _const_spec(aw)],
        out_shape=[jax.ShapeDtypeStruct((s, D_MODEL), BF16)] + [
            jax.ShapeDtypeStruct((d, s // d, aw), BF16) for d in DILATIONS] + [
            jax.ShapeDtypeStruct((d, s // d, LANES), F32) for d in DILATIONS] + [
            jax.ShapeDtypeStruct((s, aw), F32), jax.ShapeDtypeStruct((s, aw), BF16),
            jax.ShapeDtypeStruct((1, D_MODEL), F32), jax.ShapeDtypeStruct((1, aw), F32),
            jax.ShapeDtypeStruct((1, aw), F32)],
        scratch_shapes=[pltpu.VMEM((aw // LANES, tm, LANES), F32), pltpu.VMEM((1, tm, LANES), F32)],
        compiler_params=_params(dimension_semantics=("arbitrary",)),
    )(dx1, mixed, gp, w_out_b, attn, an, o_h, proj, hn)


def mlp_fwd_bwd(x1, g_pre, w1_blocks, w2_b, g_post, target):
    s = x1.shape[0]
    tm = MLP_TILE
    nblk, _, fb = w1_blocks.shape

    def body(x1_ref, gpre_ref, w1_ref, w2_ref, gpost_ref, t_ref,
             dx1_ref, h2_ref, a_ref, du_ref, dff_ref, loss_ref, dgpre_ref, dgpost_ref, u_ref):
        x1v = x1_ref[...]
        h2 = _rms_fwd(x1v, gpre_ref[...], D_MODEL).astype(BF16)
        h2_ref[...] = h2
        ff = jnp.zeros((tm, D_MODEL), F32)
        for j in range(nblk):
            cols = slice(j * fb, (j + 1) * fb)
            ru = jnp.maximum(_dot(h2, w1_ref[j]), 0.0)
            u_ref[:, cols] = ru.astype(BF16)
            a = (ru * ru).astype(BF16)
            a_ref[:, cols] = a
            ff = ff + _dot(a, w2_ref[cols, :])
        diff = x1v + _rms_fwd(ff, gpost_ref[...], D_MODEL) - t_ref[...]
        _acc_rows(loss_ref, diff * diff)
        dy = diff * (1.0 / D_MODEL)
        dff, gpost_c = _rms_bwd(dy, ff, gpost_ref[...], D_MODEL)
        _acc_rows(dgpost_ref, gpost_c)
        dff_b = dff.astype(BF16)
        dff_ref[...] = dff_b
        dh2 = jnp.zeros((tm, D_MODEL), F32)
        for j in range(nblk):
            cols = slice(j * fb, (j + 1) * fb)
            du = (_dot_nt(dff_b, w2_ref[cols, :]) * (2.0 * u_ref[:, cols])).astype(BF16)
            du_ref[:, cols] = du
            dh2 = dh2 + _dot_nt(du, w1_ref[j])
        dxa, gpre_c = _rms_bwd(dh2, x1v, gpre_ref[...], D_MODEL)
        _acc_rows(dgpre_ref, gpre_c)
        dx1_ref[...] = dy + dxa

    dm = D_MODEL
    return pl.pallas_call(
        body,
        name="mlp_fwd_bwd",
        grid=(s // tm,),
        in_specs=[_row_spec(tm, dm), _const_spec(dm), _vmem_spec(), _vmem_spec(), _const_spec(dm), _row_spec(tm, dm)],
        out_specs=[_row_spec(tm, dm), _row_spec(tm, dm), _row_spec(tm, D_FF), _row_spec(tm, D_FF), _row_spec(tm, dm),
                   _const_spec(dm), _const_spec(dm), _const_spec(dm)],
        out_shape=[
            jax.ShapeDtypeStruct((s, dm), F32),
            jax.ShapeDtypeStruct((s, dm), BF16),
            jax.ShapeDtypeStruct((s, D_FF), BF16),
            jax.ShapeDtypeStruct((s, D_FF), BF16),
            jax.ShapeDtypeStruct((s, dm), BF16),
            jax.ShapeDtypeStruct((1, dm), F32),
            jax.ShapeDtypeStruct((1, dm), F32),
            jax.ShapeDtypeStruct((1, dm), F32),
        ],
        scratch_shapes=[pltpu.VMEM((tm, D_FF), BF16)],
        compiler_params=_params(dimension_semantics=("arbitrary",)),
    )(x1, g_pre, w1_blocks, w2_b, g_post, target)


def in_proj_bwd(attn_grads, hgrn_grads, dgate, w_in_b, x, g1, dx1):
    s = x.shape[0]
    tm = PROJ_TILE
    aw = ATTN_WIDTH
    n_attn = len(attn_grads)
    flat = [g[k] for k in range(3) for g in attn_grads] + list(hgrn_grads) + [dgate]

    def body(*refs):
        parts = refs[:len(flat)]
        w_ref, x_ref, g_ref, dx1_ref, dx_ref, dproj_ref, dg_ref, scr = refs[len(flat):]
        groups = []
        for k in range(3):
            acc = None
            for p, d in zip(parts[k * n_attn:(k + 1) * n_attn], DILATIONS):
                v = _from_dilated(p, scr, d, tm)
                acc = v if acc is None else acc + v
            groups.append(acc)
        groups += [p[...] for p in parts[3 * n_attn:]]
        dh = jnp.zeros((tm, D_MODEL), F32)
        for gi, grp in enumerate(groups):
            cols = slice(gi * aw, (gi + 1) * aw)
            gb = grp.astype(BF16)
            dproj_ref[:, cols] = gb
            dh = dh + _dot_nt(gb, w_ref[:, cols])
        dxa, g_c = _rms_bwd(dh, x_ref[...], g_ref[...], D_MODEL)
        _acc_rows(dg_ref, g_c)
        dx_ref[...] = dx1_ref[...] + dxa

    dm = D_MODEL
    return pl.pallas_call(
        body,
        name="in_proj_bwd",
        grid=(s // tm,),
        in_specs=[_dilated_spec(d, tm, aw) for d in DILATIONS] * 3 + [_row_spec(tm, aw)] * 4 + [
            _vmem_spec(), _row_spec(tm, dm), _const_spec(dm), _row_spec(tm, dm)],
        out_specs=[_row_spec(tm, dm), _row_spec(tm, IN_PROJ_WIDTH), _const_spec(dm)],
        out_shape=[jax.ShapeDtypeStruct((s, dm), F32), jax.ShapeDtypeStruct((s, IN_PROJ_WIDTH), BF16),
                   jax.ShapeDtypeStruct((1, dm), F32)],
        scratch_shapes=[pltpu.VMEM((aw // LANES, tm, LANES), F32)],
        compiler_params=_params(dimension_semantics=("arbitrary",)),
    )(*flat, w_in_b, x, g1, dx1)


def wgrad(a_b, b_b, tn, name, ts=2048, per_step=1, ride=None):
    s, k = a_b.shape
    n = b_b.shape[1]

    def body(a_ref, b_ref, o_ref):
        @pl.when(pl.program_id(1) == 0)
        def _():
            o_ref[...] = jnp.zeros_like(o_ref)

        a = a_ref[...]
        for jj in range(per_step):
            o_ref[jj] += _dot_tn(a, b_ref[:, jj * tn:(jj + 1) * tn])

    wide = tn * per_step
    gn, gs = n // wide, s // ts
    step = lambda j, i: (lambda: (pl.program_id(0) == j) & (pl.program_id(1) == i))
    e_in, e_out, e_shape, e_scr, e_args = _ride_specs(ride)
    out = pl.pallas_call(
        _riding(body, 2, 1, 0, ride, step(0, 0), step(gn // 2, 0), step(gn - 1, gs - 1)),
        name=name,
        grid=(gn, gs),
        in_specs=[pl.BlockSpec((ts, k), lambda j, i: (i, 0)), pl.BlockSpec((ts, wide), lambda j, i: (i, j))] + e_in,
        out_specs=[pl.BlockSpec((per_step, k, tn), lambda j, i: (j, 0, 0))] + e_out,
        out_shape=[jax.ShapeDtypeStruct((n // tn, k, tn), F32)] + e_shape,
        scratch_shapes=e_scr,
        compiler_params=_params(ride, dimension_semantics=("arbitrary", "arbitrary")),
    )(a_b, b_b, *e_args)
    return out[0] if ride is None else out


def train_step(x, target, g1, an, logits, hn, gp, g_pre, g_post, w, m, v):
    nd = len(DILATIONS)
    shard_b = {k: w[k].astype(BF16) for k in BIG}
    (w_in_g,) = run_exchange(gather_exchange([shard_b["w_in"]]), "gather_w_in")
    w_in_b = w_in_g.transpose(1, 0, 2).reshape(D_MODEL, IN_PROJ_WIDTH)

    proj, h_b, *qkvs, w2_g = in_proj_fwd(x, g1, w_in_b, ride=gather_exchange([shard_b["w_ff2"]]))
    w2_b = w2_g.reshape(D_FF, D_MODEL)
    attn_parts = [attn_fwd(qkv, d) for qkv, d in zip(qkvs, DILATIONS)]
    o_h, states, a_mat, w_out_g, w1_blocks = hgrn_fwd(
        proj, logits, ride=gather_exchange([shard_b["w_out"], shard_b["w_ff1"]]))
    w_out_b = w_out_g.reshape(D_MODEL, D_MODEL)
    x1, cat_b, mixed, attn, *lses = mix_fwd(attn_parts, o_h, proj, an, hn, w_out_b, gp, x)
    dx1, h2_b, a_b, du_b, dff_b, loss_vec, dg_pre, dg_post = mlp_fwd_bwd(x1, g_pre, w1_blocks, w2_b, g_post, target)
    dw2 = wgrad(a_b, dff_b, D_MODEL, "wgrad_ff2", ts=512)
    dw1 = wgrad(h2_b, du_b, D_FF // N_DEV, "wgrad_ff1", per_step=2)
    dmix_b, *rest = mix_bwd(dx1, mixed, gp, w_out_b, attn, an, o_h, proj, hn)
    d_os, deltas = rest[:nd], rest[nd:2 * nd]
    d_oh, dgate, dgp, dan, dhn = rest[2 * nd:]
    dwout = wgrad(cat_b, dmix_b, D_MODEL, "wgrad_out")

    early = ("w_out", "w_ff1", "w_ff2")
    early_grads = [dwout.reshape(N_DEV, D_MODEL // N_DEV, D_MODEL), dw1, dw2.reshape(N_DEV, D_FF // N_DEV, D_MODEL)]
    res = attn_bwd(qkvs[0], d_os[0], lses[0], deltas[0], DILATIONS[0], ride=to_core_exchange(early_grads))
    pairs = [pair_sum(g, s, f"pair_sum_{name}") for g, s, name in zip(early_grads, res[3:], early)]
    attn_grads = [res[:3]]
    *res, others_ff2 = attn_bwd(qkvs[1], d_os[1], lses[1], deltas[1], DILATIONS[1],
                                ride=to_chip_exchange([pairs[2][1]]))
    attn_grads.append(res)
    attn_grads.append(attn_bwd(qkvs[2], d_os[2], lses[2], deltas[2], DILATIONS[2]))
    dq_h, df_h, di_h, dlb, *others = hgrn_bwd(proj, logits, d_oh, states, a_mat,
                                              ride=to_chip_exchange([pairs[0][1], pairs[1][1]]))
    others.append(others_ff2)
    dx, dproj_b, dg1 = in_proj_bwd(attn_grads, (dq_h, df_h, di_h), dgate, w_in_b, x, g1, dx1)
    packed = _pack_small(dg1, dgp, dg_pre, dg_post, dan, dhn, dlb, loss_vec)
    dwin, small_slots = wgrad(h_b, dproj_b, 2 * IN_PROJ_WIDTH // N_DEV, "wgrad_in",
                              ride=small_exchange(packed))
    big = {name: sum_adamw(p[0], o, w[name], m[name], v[name], f"sum_adamw_{name}")
           for name, p, o in zip(early, pairs, others)}

    shard_w = IN_PROJ_WIDTH // N_DEV
    dwin_blocks = dwin.reshape(N_DEV // 2, D_MODEL, 2, shard_w).transpose(0, 2, 1, 3).reshape(N_DEV, D_MODEL, shard_w)
    (from_sibling,) = run_exchange(to_core_exchange([dwin_blocks.astype(BF16)]), "reduce_w_in_to_core")
    pair_in, pair_in_b = pair_sum(dwin_blocks, from_sibling, "pair_sum_w_in")
    (others_in,) = run_exchange(to_chip_exchange([pair_in_b]), "reduce_w_in_to_chip")
    big["w_in"] = sum_adamw(pair_in, others_in, w["w_in"], m["w_in"], v["w_in"], "sum_adamw_w_in")
    return dx, big, small_slots


def _position():
    x, y, c = lax.axis_index("x"), lax.axis_index("y"), lax.axis_index("c")
    other_chips = [(1 - x, y), (x, 1 - y), (1 - x, 1 - y)]
    return x, y, c, other_chips


def _any_spec():
    return pl.BlockSpec(memory_space=pl.ANY)


class Exchange:
    def __init__(self, arrays, out_shape, sems, stages, collective_id, peers):
        self.arrays, self.out_shape, self.sems, self.stages = list(arrays), list(out_shape), list(sems), stages
        self.collective_id, self.peers = collective_id, peers

    def open(self):
        barrier = pltpu.get_barrier_semaphore()
        peers = self.peers()
        for peer in peers:
            pl.semaphore_signal(barrier, inc=1, device_id=peer, device_id_type=MESH)
        pl.semaphore_wait(barrier, len(peers))


def _siblings():
    x, y, c, _ = _position()
    return [(x, y, 1 - c)]


def _same_core_of_other_chips():
    x, y, c, chips = _position()
    return [(px, py, c) for px, py in chips]


def _gather_peers():
    x, y, c, _ = _position()
    return [(x, y, 1 - c), (1 - x, y, c), (x, 1 - y, c)]


def _all_others():
    x, y, c, _ = _position()
    return [(1 - x if rel & 4 else x, 1 - y if rel & 2 else y, 1 - c if rel & 1 else c) for rel in range(1, N_DEV)]


def gather_exchange(shards):
    n = len(shards)
    halves = [sh.shape[0] // 2 for sh in shards]

    def stages(ins, outs, sems):
        send_sems, recv_sems, local_sems = sems

        def parts():
            x, y, c, _ = _position()
            me, sibling = (x, y, c), (x, y, 1 - c)
            nbr_x, nbr_y, diag = (1 - x, y, c), (x, 1 - y, c), (1 - x, 1 - y, c)

            def slot(a, dev, rows=None):
                ref = outs[a].at[4 * dev[0] + 2 * dev[1] + dev[2]]
                return ref if rows is None else ref.at[rows]

            def copy(a, k, block, to, rows=None, src=None):
                return pltpu.make_async_remote_copy(
                    src_ref=slot(a, block, rows) if src is None else src, dst_ref=slot(a, block, rows),
                    send_sem=send_sems.at[a, k], recv_sem=recv_sems.at[a, k], device_id=to, device_id_type=MESH)

            upper = lambda a: pl.ds(0, halves[a])
            lower = lambda a: pl.ds(halves[a], halves[a])
            return me, sibling, nbr_x, nbr_y, diag, slot, copy, upper, lower

        def begin():
            me, sibling, nbr_x, nbr_y, _, slot, copy, _, _ = parts()
            for a in range(n):
                pltpu.make_async_copy(ins[a], slot(a, me), local_sems.at[a]).start()
                for k, to in enumerate((sibling, nbr_x, nbr_y)):
                    copy(a, k, me, to, src=ins[a]).start()

        def middle():
            me, sibling, nbr_x, nbr_y, _, _, copy, upper, lower = parts()
            for a in range(n):
                copy(a, 1, nbr_x, me).wait_recv()
                copy(a, 3, nbr_x, sibling).start()
                copy(a, 5, nbr_x, nbr_y, rows=lower(a)).start()
                copy(a, 2, nbr_y, me).wait_recv()
                copy(a, 4, nbr_y, sibling).start()
                copy(a, 6, nbr_y, nbr_x, rows=upper(a)).start()

        def end():
            me, sibling, nbr_x, nbr_y, diag, slot, copy, upper, lower = parts()
            sib = lambda dev: (dev[0], dev[1], sibling[2])
            for a in range(n):
                copy(a, 6, diag, me, rows=upper(a)).wait_recv()
                copy(a, 5, diag, me, rows=lower(a)).wait_recv()
                copy(a, 7, diag, sibling).start()
            for a in range(n):
                for k, block in ((0, sibling), (3, sib(nbr_x)), (4, sib(nbr_y)), (7, sib(diag))):
                    copy(a, k, block, me).wait_recv()
                copy(a, 0, me, sibling, src=ins[a]).wait_send()
                copy(a, 1, me, nbr_x, src=ins[a]).wait_send()
                copy(a, 2, me, nbr_y, src=ins[a]).wait_send()
                copy(a, 3, nbr_x, sibling).wait_send()
                copy(a, 4, nbr_y, sibling).wait_send()
                copy(a, 5, nbr_x, nbr_y, rows=lower(a)).wait_send()
                copy(a, 6, nbr_y, nbr_x, rows=upper(a)).wait_send()
                copy(a, 7, diag, sibling).wait_send()
                pltpu.make_async_copy(ins[a], slot(a, me), local_sems.at[a]).wait()

        return begin, middle, end

    return Exchange(
        shards, [jax.ShapeDtypeStruct((N_DEV,) + sh.shape, sh.dtype) for sh in shards],
        [pltpu.SemaphoreType.DMA((n, 8)), pltpu.SemaphoreType.DMA((n, 8)), pltpu.SemaphoreType.DMA((n,))], stages,
        collective_id=0, peers=_gather_peers)


def to_core_exchange(grads):
    n = len(grads)

    def stages(ins, outs, sems):
        send_sems, recv_sems = sems

        def copies():
            x, y, c, _ = _position()
            return [pltpu.make_async_remote_copy(
                src_ref=ins[a].at[2 * q + (1 - c)], dst_ref=outs[a].at[q], send_sem=send_sems.at[a, q],
                recv_sem=recv_sems.at[a, q], device_id=(x, y, 1 - c), device_id_type=MESH)
                for a in range(n) for q in range(4)]

        def begin():
            for cp in copies():
                cp.start()

        def end():
            for cp in copies():
                cp.wait()

        return begin, None, end

    return Exchange(grads, [jax.ShapeDtypeStruct((4,) + g.shape[1:], g.dtype) for g in grads],
                    [pltpu.SemaphoreType.DMA((n, 4)), pltpu.SemaphoreType.DMA((n, 4))], stages,
                    collective_id=1, peers=_siblings)


def pair_sum(grad, from_sibling, name):
    _, r, cdim = grad.shape
    tr = min(r, ELEMENTWISE_ROWS)
    c_idx = lax.axis_index("c").astype(jnp.int32).reshape(1)

    def body(c_ref, g_ref, s_ref, o_ref, ob_ref):
        total = g_ref[...] + s_ref[...]
        o_ref[...] = total
        ob_ref[...] = total.astype(BF16)

    blk = lambda: pl.BlockSpec((1, tr, cdim), lambda q, i, cr: (q, i, 0))
    return pl.pallas_call(
        body,
        name=name,
        grid_spec=pltpu.PrefetchScalarGridSpec(
            num_scalar_prefetch=1,
            grid=(4, r // tr),
            in_specs=[pl.BlockSpec((1, tr, cdim), lambda q, i, cr: (2 * q + cr[0], i, 0)), blk()],
            out_specs=[blk(), blk()],
        ),
        out_shape=[jax.ShapeDtypeStruct((4, r, cdim), F32), jax.ShapeDtypeStruct((4, r, cdim), BF16)],
        compiler_params=_params(dimension_semantics=("arbitrary", "arbitrary")),
    )(c_idx, grad, from_sibling)


def to_chip_exchange(pairs):
    n = len(pairs)

    def stages(ins, outs, sems):
        send_sems, recv_sems = sems

        def copies():
            x, y, c, chips = _position()
            return [pltpu.make_async_remote_copy(
                src_ref=ins[a].at[2 * px + py], dst_ref=outs[a].at[j], send_sem=send_sems.at[a, j],
                recv_sem=recv_sems.at[a, j], device_id=(px, py, c), device_id_type=MESH)
                for a in range(n) for j, (px, py) in enumerate(chips)]

        def begin():
            for cp in copies():
                cp.start()

        def end():
            for cp in copies():
                cp.wait()

        return begin, None, end

    return Exchange(pairs, [jax.ShapeDtypeStruct((3,) + p.shape[1:], p.dtype) for p in pairs],
                    [pltpu.SemaphoreType.DMA((n, 3)), pltpu.SemaphoreType.DMA((n, 3))], stages,
                    collective_id=2, peers=_same_core_of_other_chips)


def run_exchange(ex, name):
    n_in, n_out = len(ex.arrays), len(ex.out_shape)

    def body(*refs):
        begin, middle, end = ex.stages(refs[:n_in], refs[n_in:n_in + n_out], refs[n_in + n_out:])
        ex.open()
        begin()
        if middle is not None:
            middle()
        end()

    return pl.pallas_call(
        body,
        name=name,
        in_specs=[_any_spec()] * n_in,
        out_specs=[_any_spec()] * n_out,
        out_shape=ex.out_shape,
        scratch_shapes=ex.sems,
        compiler_params=pltpu.CompilerParams(collective_id=ex.collective_id),
    )(*ex.arrays)


def _riding(body, n_in, n_out, n_scratch, ex, first, middle, last):
    if ex is None:
        return body
    r_in, r_out = len(ex.arrays), len(ex.out_shape)

    def wrapped(*refs):
        k_in, refs = refs[:n_in], refs[n_in:]
        e_in, refs = refs[:r_in], refs[r_in:]
        k_out, refs = refs[:n_out], refs[n_out:]
        e_out, refs = refs[:r_out], refs[r_out:]
        k_scr, e_sems = refs[:n_scratch], refs[n_scratch:]
        begin, mid, end = ex.stages(e_in, e_out, e_sems)

        @pl.when(first())
        def _():
            ex.open()
            begin()

        body(*k_in, *k_out, *k_scr)
        if mid is not None:
            pl.when(middle())(mid)
        pl.when(last())(end)

    return wrapped


def _ride_specs(ex):
    if ex is None:
        return [], [], [], [], []
    return [_any_spec()] * len(ex.arrays), [_any_spec()] * len(ex.out_shape), ex.out_shape, ex.sems, ex.arrays


def _adamw(w, g, m, v):
    m = ADAM_B1 * m + (1.0 - ADAM_B1) * g
    v = ADAM_B2 * v + (1.0 - ADAM_B2) * (g * g)
    m_hat = m / (1.0 - ADAM_B1 ** ADAM_STEP)
    v_hat = v / (1.0 - ADAM_B2 ** ADAM_STEP)
    delta = -ADAM_LR * (m_hat / (jnp.sqrt(v_hat) + ADAM_EPS) + ADAM_WD * w)
    return delta, m, v


def sum_adamw(pairs, others, w, m, v, name):
    r, cdim = w.shape
    tr = min(r, ELEMENTWISE_ROWS // 2)
    chip_idx =(2 * lax.axis_index("x") + lax.axis_index("y")).astype(jnp.int32).reshape(1)

    def body(q_ref, p_ref, o_ref, w_ref, m_ref, v_ref, g_out, d_out, m_out, v_out):
        g = p_ref[0] + o_ref[0].astype(F32) + o_ref[1].astype(F32) + o_ref[2].astype(F32)
        g_out[...] = g
        d_out[...], m_out[...], v_out[...] = _adamw(w_ref[...], g, m_ref[...], v_ref[...])

    tile = lambda: pl.BlockSpec((tr, cdim), lambda i, qr: (i, 0))
    return pl.pallas_call(
        body,
        name=name,
        grid_spec=pltpu.PrefetchScalarGridSpec(
            num_scalar_prefetch=1,
            grid=(r // tr,),
            in_specs=[pl.BlockSpec((1, tr, cdim), lambda i, qr: (qr[0], i, 0)),
                      pl.BlockSpec((3, tr, cdim), lambda i, qr: (0, i, 0)), tile(), tile(), tile()],
            out_specs=[tile(), tile(), tile(), tile()],
        ),
        out_shape=[jax.ShapeDtypeStruct((r, cdim), F32)] * 4,
        compiler_params=_params(dimension_semantics=("arbitrary",)),
    )(chip_idx, pairs, others, w, m, v)


def small_exchange(packed):
    def stages(ins, outs, sems):
        send_sems, recv_sems, local_sem = sems
        (src,), (slots,) = ins, outs

        def copies():
            x, y, c, _ = _position()
            my_id = 4 * x + 2 * y + c
            sends, landings = [], []
            for rel in range(1, N_DEV):
                px = 1 - x if (rel >> 2) & 1 else x
                py = 1 - y if (rel >> 1) & 1 else y
                pc = 1 - c if rel & 1 else c
                peer = dict(send_sem=send_sems.at[rel - 1], recv_sem=recv_sems.at[rel - 1], device_id=(px, py, pc),
                            device_id_type=MESH)
                sends.append(pltpu.make_async_remote_copy(src_ref=src, dst_ref=slots.at[my_id], **peer))
                landings.append(pltpu.make_async_remote_copy(src_ref=src, dst_ref=slots.at[4 * px + 2 * py + pc], **peer))
            return pltpu.make_async_copy(src, slots.at[my_id], local_sem), sends, landings

        def begin():
            local, sends, _ = copies()
            local.start()
            for cp in sends:
                cp.start()

        def end():
            local, sends, landings = copies()
            for cp in landings:
                cp.wait_recv()
            for cp in sends:
                cp.wait_send()
            local.wait()

        return begin, None, end

    return Exchange([packed], [jax.ShapeDtypeStruct((N_DEV,) + packed.shape, packed.dtype)],
                    [pltpu.SemaphoreType.DMA((N_DEV - 1,)), pltpu.SemaphoreType.DMA((N_DEV - 1,)),
                     pltpu.SemaphoreType.DMA(())], stages, collective_id=3, peers=_all_others)


def small_adamw(slots, w, m, v):
    def body(r_ref, w_ref, m_ref, v_ref, g_out, d_out, m_out, v_out, loss_out):
        red = r_ref[0]
        for k in range(1, N_DEV):
            red = red + r_ref[k]
        wv = w_ref[...]
        lb = _lower_bound(jnp.concatenate([wv[5:6, :HGRN_WIDTH], wv[5:6, HGRN_WIDTH:]], axis=0))
        t = red[5:6, :HGRN_WIDTH] * lb * (1.0 - lb)
        row = lax.broadcasted_iota(jnp.int32, red.shape, 0)
        g = jnp.where(row == 5, jnp.concatenate([t, -t], axis=1), jnp.where(row >= 6, 0.0, red))
        g_out[...] = g
        d_out[...], m_out[...], v_out[...] = _adamw(wv, g, m_ref[...], v_ref[...])
        loss = jnp.sum(red[6:7, :], axis=-1, keepdims=True) * (0.5 / D_MODEL)
        loss_out[...] = jnp.broadcast_to(loss, loss_out.shape)

    return pl.pallas_call(
        body,
        name="small_adamw",
        in_specs=[_vmem_spec()] * 4,
        out_specs=[_vmem_spec()] * 5,
        out_shape=[jax.ShapeDtypeStruct(w.shape, F32)] * 4 + [jax.ShapeDtypeStruct((SUBLANES, LANES), F32)],
    )(slots, w, m, v)


def _pack_small(g1, gp, g_pre, g_post, an, hn, logits_or_dlb, extra=None):
    row5 = logits_or_dlb.reshape(1, -1)
    row5 = jnp.pad(row5, ((0, 0), (0, D_MODEL - row5.shape[1])))
    row6 = jnp.zeros((1, D_MODEL), F32) if extra is None else extra
    return jnp.concatenate([g1, gp, g_pre, g_post, jnp.concatenate([an, hn], axis=1), row5, row6,
                            jnp.zeros((1, D_MODEL), F32)], axis=0)


def _unpack_small(p):
    return dict(mix_pre_norm=p[0:1], mix_post_norm=p[1:2], mlp_pre_norm=p[2:3], mlp_post_norm=p[3:4],
                attn_out_norm=p[4:5, :ATTN_WIDTH], hgrn_out_norm=p[4:5, ATTN_WIDTH:],
                hgrn_lb_logits=p[5].reshape(2, HGRN_WIDTH))


BIG = ("w_in", "w_out", "w_ff1", "w_ff2")
ORDER = ("mix_pre_norm", "w_in", "attn_out_norm", "hgrn_lb_logits", "hgrn_out_norm", "w_out", "mix_post_norm",
         "mlp_pre_norm", "w_ff1", "w_ff2", "mlp_post_norm")


def kernel(x, mix_pre_norm, w_in, attn_out_norm, hgrn_lb_logits, hgrn_out_norm, w_out, mix_post_norm, mlp_pre_norm, w_ff1, w_ff2, mlp_post_norm, loss_target, m_mix_pre_norm, m_w_in, m_attn_out_norm, m_hgrn_lb_logits, m_hgrn_out_norm, m_w_out, m_mix_post_norm, m_mlp_pre_norm, m_w_ff1, m_w_ff2, m_mlp_post_norm, v_mix_pre_norm, v_w_in, v_attn_out_norm, v_hgrn_lb_logits, v_hgrn_out_norm, v_w_out, v_mix_post_norm, v_mlp_pre_norm, v_w_ff1, v_w_ff2, v_mlp_post_norm):
    w = dict(w_in=w_in[0], w_out=w_out[0], w_ff1=w_ff1[0], w_ff2=w_ff2[0])
    m = dict(w_in=m_w_in[0], w_out=m_w_out[0], w_ff1=m_w_ff1[0], w_ff2=m_w_ff2[0])
    v = dict(w_in=v_w_in[0], w_out=v_w_out[0], w_ff1=v_w_ff1[0], w_ff2=v_w_ff2[0])

    dx, big, small_slots = train_step(x[0], loss_target[0], mix_pre_norm, attn_out_norm, hgrn_lb_logits, hgrn_out_norm,
                                      mix_post_norm, mlp_pre_norm, mlp_post_norm, w, m, v)

    pack = lambda a, b, c2, d, e, f, g: _pack_small(a, b, c2, d, e, f, g)
    w_s = pack(mix_pre_norm, mix_post_norm, mlp_pre_norm, mlp_post_norm, attn_out_norm, hgrn_out_norm, hgrn_lb_logits)
    m_s = pack(m_mix_pre_norm, m_mix_post_norm, m_mlp_pre_norm, m_mlp_post_norm, m_attn_out_norm, m_hgrn_out_norm,
               m_hgrn_lb_logits)
    v_s = pack(v_mix_pre_norm, v_mix_post_norm, v_mlp_pre_norm, v_mlp_post_norm, v_attn_out_norm, v_hgrn_out_norm,
               v_hgrn_lb_logits)
    g_s, d_s, nm_s, nv_s, loss = small_adamw(small_slots, w_s, m_s, v_s)
    small_out = [_unpack_small(t) for t in (g_s, d_s, nm_s, nv_s)]

    outs = [loss[0, 0], dx[None]]
    for kind in range(4):
        for name in ORDER:
            outs.append(big[name][kind][None] if name in BIG else small_out[kind][name])
    return tuple(outs)
```

```python
import jax
import jax.numpy as jnp
from jax import lax
from jax.experimental import pallas as pl
from jax.experimental.pallas import tpu as pltpu

F32 = jnp.float32
BF16 = jnp.bfloat16

D_MODEL = 1024
ATTN_WIDTH = 512
ATTN_HEAD_DIM = 64
ATTN_HEADS = 8
ATTN_BLOCK = 128
DILATIONS = (1, 4, 16)
HGRN_WIDTH = 512
HGRN_HEADS = 4
HGRN_HEAD_DIM = 128
HGRN_CHUNK = 64
IN_PROJ_WIDTH = 3584
D_FF = 4096
RMS_EPS = 1e-6
N_DEV = 8
ADAM_LR = 0.001
ADAM_B1 = 0.9
ADAM_B2 = 0.999
ADAM_EPS = 1e-08
ADAM_WD = 0.01
ADAM_STEP = 10

SUBLANES = 8
LANES = 128
COLUMN_UNROLL = 8
HGRN_CHUNKS_PER_STEP = 2
SUB_BLOCK = 16
TOKEN_TILE = 512
ELEMENTWISE_ROWS = 1024
MLP_TILE = 256
PROJ_TILE = 512
VMEM_BYTES_V7X = 64 * 1024 * 1024
VMEM_LIMIT = VMEM_BYTES_V7X // 8 * 7
NEG_BIG = -1e30
MESH = pl.DeviceIdType.MESH


def _params(ride=None, **kw):
    if ride is not None:
        kw["collective_id"] = ride.collective_id
    return pltpu.CompilerParams(vmem_limit_bytes=VMEM_LIMIT, **kw)


def _vmem_spec():
    return pl.BlockSpec(memory_space=pltpu.VMEM)


def _dot(a, b):
    return jnp.dot(a, b, preferred_element_type=F32)


def _dot_nt(a, b):
    return lax.dot_general(a, b, (((1,), (1,)), ((), ())), preferred_element_type=F32)


def _dot_tn(a, b):
    return lax.dot_general(a, b, (((0,), (0,)), ((), ())), preferred_element_type=F32)


def _sigmoid(x):
    return 1.0 / (1.0 + jnp.exp(-x))


def _rms_fwd(x, gain, width):
    r = lax.rsqrt(jnp.sum(x * x, axis=-1, keepdims=True) * (1.0 / width) + RMS_EPS)
    return x * r * gain


def _rms_bwd(dy, x, gain, width):
    r = lax.rsqrt(jnp.sum(x * x, axis=-1, keepdims=True) * (1.0 / width) + RMS_EPS)
    xhat = x * r
    dxhat = dy * gain
    dx = r * (dxhat - xhat * (jnp.sum(dxhat * xhat, axis=-1, keepdims=True) * (1.0 / width)))
    return dx, dy * xhat


def _split3(x):
    hi = x.astype(BF16)
    r1 = x - hi.astype(F32)
    mid = r1.astype(BF16)
    lo = (r1 - mid.astype(F32)).astype(BF16)
    return hi, mid, lo


def _tri_sum(tri_bf16, x):
    hi, mid, lo = _split3(x)
    return _dot(tri_bf16, hi) + _dot(tri_bf16, mid) + _dot(tri_bf16, lo)


def _dilated_spec(d, tm, width):
    return pl.BlockSpec((d, tm // d, width), lambda i: (0, i, 0))


def _lane_blocks(ref, value):
    for c in range(ref.shape[0]):
        ref[c] = value[:, c * LANES:(c + 1) * LANES]


def _to_dilated(src_ref, dst_ref, d, tm, cast=None):
    for r in range(d):
        for c in range(src_ref.shape[0]):
            v = src_ref[c] if d == 1 else src_ref[c, pl.ds(r, tm // d, stride=d), :]
            dst_ref[r, :, c * LANES:(c + 1) * LANES] = v if cast is None else v.astype(cast)


def _from_dilated(src_ref, scratch_ref, d, tm):
    if d == 1:
        return src_ref[0].astype(F32)
    nblk = scratch_ref.shape[0]
    for r in range(d):
        for c in range(nblk):
            scratch_ref[c, pl.ds(r, tm // d, stride=d), :] = src_ref[r, :, c * LANES:(c + 1) * LANES].astype(F32)
    return jnp.concatenate([scratch_ref[c] for c in range(nblk)], axis=1)


def in_proj_fwd(x, g1, w_in_b, ride=None):
    s = x.shape[0]
    tm = PROJ_TILE
    qkv_w = 3 * ATTN_WIDTH
    hg_w = IN_PROJ_WIDTH - qkv_w

    def body(x_ref, g_ref, w_ref, hg_ref, h_ref, *rest):
        qkv_refs, qkv_scr = rest[:len(DILATIONS)], rest[len(DILATIONS)]
        h = _rms_fwd(x_ref[...], g_ref[...], D_MODEL).astype(BF16)
        h_ref[...] = h
        proj = _dot(h, w_ref[...])
        hg_ref[...] = proj[:, qkv_w:]
        _lane_blocks(qkv_scr, proj[:, :qkv_w])
        for d, ref in zip(DILATIONS, qkv_refs):
            _to_dilated(qkv_scr, ref, d, tm, cast=BF16)

    n_steps = s // tm
    step = lambda k: (lambda: pl.program_id(0) == k)
    e_in, e_out, e_shape, e_scr, e_args = _ride_specs(ride)
    return pl.pallas_call(
        _riding(body, 3, 2 + len(DILATIONS), 1, ride, step(0), step(n_steps // 2), step(n_steps - 1),
                late=step(n_steps - 2)),
        name="in_proj_fwd",
        grid=(n_steps,),
        in_specs=[
            pl.BlockSpec((tm, D_MODEL), lambda i: (i, 0)),
            pl.BlockSpec((1, D_MODEL), lambda i: (0, 0)),
            _vmem_spec(),
        ] + e_in,
        out_specs=[
            pl.BlockSpec((tm, hg_w), lambda i: (i, 0)),
            pl.BlockSpec((tm, D_MODEL), lambda i: (i, 0)),
        ] + [_dilated_spec(d, tm, qkv_w) for d in DILATIONS] + e_out,
        out_shape=[jax.ShapeDtypeStruct((s, hg_w), F32), jax.ShapeDtypeStruct((s, D_MODEL), BF16)] + [
            jax.ShapeDtypeStruct((d, s // d, qkv_w), BF16) for d in DILATIONS] + e_shape,
        scratch_shapes=[pltpu.VMEM((qkv_w // LANES, tm, LANES), F32)] + e_scr,
        compiler_params=_params(ride, dimension_semantics=("arbitrary",)),
    )(x, g1, w_in_b, *e_args)


ATTN_SCALE = ATTN_HEAD_DIM ** -0.5


def _fill_attn_bias(bias_ref, dilation):
    qi = lax.broadcasted_iota(jnp.int32, (ATTN_BLOCK, 2 * ATTN_BLOCK), 0)
    kj = lax.broadcasted_iota(jnp.int32, (ATTN_BLOCK, 2 * ATTN_BLOCK), 1)
    dist = qi + ATTN_BLOCK - kj
    valid = (dist >= 0) & (dist <= ATTN_BLOCK)
    for head in range(ATTN_HEADS):
        slope = 2.0 ** (-8.0 * (head + 1) / ATTN_HEADS)
        bias = jnp.where(valid, dist.astype(F32) * (-slope * dilation), NEG_BIG)
        bias_ref[0, head] = bias
        bias_ref[1, head] = jnp.where(kj >= ATTN_BLOCK, bias, NEG_BIG)


def _stack_heads(x):
    low = _lane_half(x.shape, 0)
    zero = jnp.zeros_like(x)
    return jnp.concatenate([jnp.where(low, x, zero), jnp.where(low, zero, x)], axis=0)


def _unstack_heads(y):
    half = y.shape[0] // 2
    return jnp.where(_lane_half((half, y.shape[1]), 0), y[:half], y[half:])


def _attn_scores(q_stack, kcat, bias_ref, pair, first_block):
    f = first_block.astype(jnp.int32)
    bias = jnp.concatenate([bias_ref[f, 2 * pair], bias_ref[f, 2 * pair + 1]], axis=0)
    return _dot_nt(q_stack, kcat) + bias


def _lane_half(shape, sub):
    lane = lax.broadcasted_iota(jnp.int32, shape, 1)
    return (lane < ATTN_HEAD_DIM) if sub == 0 else (lane >= ATTN_HEAD_DIM)


def _sub_block(col, row):
    return pl.BlockSpec((None, ATTN_BLOCK, ATTN_WIDTH), lambda r, n: (r, row(n), col))


def attn_fwd(qkv, dilation):
    d, length, _ = qkv.shape
    assert d == dilation
    nb = length // ATTN_BLOCK

    def body(q_ref, kc_ref, kp_ref, vc_ref, vp_ref, o_ref, lse_ref, bias_ref):
        @pl.when((pl.program_id(0) == 0) & (pl.program_id(1) == 0))
        def _():
            _fill_attn_bias(bias_ref, d)

        first = pl.program_id(1) == 0
        for pair in range(ATTN_HEADS // 2):
            lanes = slice(pair * LANES, (pair + 1) * LANES)
            q_stack = _stack_heads(q_ref[:, lanes] * ATTN_SCALE)
            kcat = jnp.concatenate([kp_ref[:, lanes], kc_ref[:, lanes]], axis=0)
            vcat = jnp.concatenate([vp_ref[:, lanes], vc_ref[:, lanes]], axis=0)
            sc = _attn_scores(q_stack, kcat, bias_ref, pair, first)
            m = jnp.max(sc, axis=-1, keepdims=True)
            p = jnp.exp(sc - m)
            den = jnp.sum(p, axis=-1, keepdims=True)
            o_ref[:, lanes] = _unstack_heads(_dot(p.astype(BF16), vcat) / den).astype(BF16)
            lse_ref[:, lanes] = _unstack_heads(jnp.broadcast_to(m + jnp.log(den), (2 * ATTN_BLOCK, LANES)))

    cur = lambda n: n
    prev = lambda n: jnp.maximum(n - 1, 0)
    return pl.pallas_call(
        body,
        name=f"attn_fwd_d{d}",
        grid=(d, nb),
        in_specs=[_sub_block(0, cur), _sub_block(1, cur), _sub_block(1, prev), _sub_block(2, cur), _sub_block(2, prev)],
        out_specs=[_sub_block(0, cur), _sub_block(0, cur)],
        out_shape=[jax.ShapeDtypeStruct((d, length, ATTN_WIDTH), BF16), jax.ShapeDtypeStruct((d, length, ATTN_WIDTH), F32)],
        scratch_shapes=[pltpu.VMEM((2, ATTN_HEADS, ATTN_BLOCK, 2 * ATTN_BLOCK), F32)],
        compiler_params=_params(dimension_semantics=("arbitrary", "arbitrary")),
    )(qkv, qkv, qkv, qkv, qkv)


def attn_bwd(qkv, d_out, lse, delta, dilation, ride=None):
    d, length, _ = qkv.shape
    assert d == dilation
    nb = length // ATTN_BLOCK

    steps = d * nb + 1

    def body(q_ref, kc_ref, kp_ref, vc_ref, vp_ref, do_ref, lse_ref, dl_ref, dq_ref, dk_ref, dv_ref, ck_ref, cv_ref,
             bias_ref):
        t = pl.program_id(0)

        @pl.when(t == 0)
        def _():
            ck_ref[...] = jnp.zeros_like(ck_ref)
            cv_ref[...] = jnp.zeros_like(cv_ref)
            _fill_attn_bias(bias_ref, d)

        @pl.when(t < steps - 1)
        def _():
            first = t % nb == 0
            for pair in range(ATTN_HEADS // 2):
                lanes = slice(pair * LANES, (pair + 1) * LANES)
                q_stack = _stack_heads(q_ref[:, lanes] * ATTN_SCALE)
                do_stack = _stack_heads(do_ref[:, lanes])
                kcat = jnp.concatenate([kp_ref[:, lanes], kc_ref[:, lanes]], axis=0)
                vcat = jnp.concatenate([vp_ref[:, lanes], vc_ref[:, lanes]], axis=0)
                col_a, col_b = 2 * pair, 2 * pair + 1
                lse_col = jnp.concatenate([lse_ref[:, col_a:col_a + 1], lse_ref[:, col_b:col_b + 1]], axis=0)
                dl_col = jnp.concatenate([dl_ref[:, col_a:col_a + 1], dl_ref[:, col_b:col_b + 1]], axis=0)
                p = jnp.exp(_attn_scores(q_stack, kcat, bias_ref, pair, first) - lse_col)
                ds = (p * (_dot_nt(do_stack, vcat) - dl_col)).astype(BF16)
                dq_ref[:, lanes] = (_unstack_heads(_dot(ds, kcat)) * ATTN_SCALE).astype(BF16)
                dk_cat = _dot_tn(ds, q_stack)
                dv_cat = _dot_tn(p.astype(BF16), do_stack)
                dk_ref[:, lanes] = (ck_ref[:, lanes] + dk_cat[:ATTN_BLOCK]).astype(BF16)
                dv_ref[:, lanes] = (cv_ref[:, lanes] + dv_cat[:ATTN_BLOCK]).astype(BF16)
                ck_ref[:, lanes] = dk_cat[ATTN_BLOCK:]
                cv_ref[:, lanes] = dv_cat[ATTN_BLOCK:]

        @pl.when(t == steps - 1)
        def _():
            dk_ref[...] = ck_ref[...].astype(BF16)
            dv_ref[...] = cv_ref[...].astype(BF16)

    blk = (ATTN_BLOCK, ATTN_WIDTH)

    def spec(col, shift, width=ATTN_WIDTH):
        def index(t):
            f = jnp.minimum(t, steps - 2) if shift > -2 else jnp.maximum(t - 1, 0)
            r, n = f // nb, f % nb
            return (r, jnp.maximum(n - 1, 0) if shift == -1 else n, col)
        return pl.BlockSpec((None, ATTN_BLOCK, width), index)

    step = lambda k: (lambda: pl.program_id(0) == k)
    e_in, e_out, e_shape, e_scr, e_args = _ride_specs(ride)
    return pl.pallas_call(
        _riding(body, 8, 3, 3, ride, step(0), step(steps // 2), step(steps - 1)),
        name=f"attn_bwd_d{d}",
        grid=(steps,),
        in_specs=[spec(0, 0), spec(1, 0), spec(1, -1), spec(2, 0), spec(2, -1), spec(0, 0), spec(0, 0, LANES),
                  spec(0, 0, LANES)] + e_in,
        out_specs=[spec(0, 0), spec(0, -2), spec(0, -2)] + e_out,
        out_shape=[jax.ShapeDtypeStruct((d, length, ATTN_WIDTH), BF16)] * 3 + e_shape,
        scratch_shapes=[pltpu.VMEM(blk, F32), pltpu.VMEM(blk, F32),
                        pltpu.VMEM((2, ATTN_HEADS, ATTN_BLOCK, 2 * ATTN_BLOCK), F32)] + e_scr,
        compiler_params=_params(ride, dimension_semantics=("arbitrary",)),
    )(qkv, qkv, qkv, qkv, qkv, d_out, lse, delta, *e_args)


def _lower_bound(logits):
    return _sigmoid(logits[0:1, :] - logits[1:2, :])


def _hgrn_gates(q, fp, lb):
    sq = _sigmoid(q)
    qf = q * sq
    sig = _sigmoid(fp)
    sig_neg = _sigmoid(-fp)
    kf = (1.0 - lb) * sig_neg
    log_sig = jnp.minimum(fp, 0.0) - jnp.log(1.0 + jnp.exp(-jnp.abs(fp)))
    a = jnp.log(lb)
    c = jnp.log(1.0 - lb) + log_sig
    log_f = jnp.maximum(a, c) + jnp.log(1.0 + jnp.exp(-jnp.abs(a - c)))
    return sq, qf, (sig, sig_neg, c), log_f, kf


def _tril_bf16(n, upper=False):
    r = lax.broadcasted_iota(jnp.int32, (n, n), 0)
    c = lax.broadcasted_iota(jnp.int32, (n, n), 1)
    keep = (c >= r) if upper else (c <= r)
    return jnp.where(keep, 1.0, 0.0).astype(BF16)


def _hgrn_diagonal_loops(c_len, diagonal):
    for half in range(SUB_BLOCK // SUBLANES):
        def step(jj, carry, half=half):
            j = half * SUBLANES + jj
            for i in range(c_len // SUB_BLOCK):
                diagonal(slice(i * SUB_BLOCK + half * SUBLANES, (i + 1) * SUB_BLOCK), j, i * SUB_BLOCK + j)
            return carry

        lax.fori_loop(0, SUBLANES, step, 0, unroll=COLUMN_UNROLL)


def _hgrn_off_diagonal(b, qf, kf):
    c_len, width = b.shape
    edges = [b[0:1, :]] + [b[i * SUB_BLOCK - 1:i * SUB_BLOCK, :] for i in range(1, c_len // SUB_BLOCK)]
    eq = jnp.exp(b - jnp.concatenate([jnp.broadcast_to(e, (SUB_BLOCK, width)) for e in edges], axis=0))
    q_til = qf * eq
    k_til, ek = [], []
    for i in range(1, c_len // SUB_BLOCK):
        n = i * SUB_BLOCK
        e = jnp.exp(edges[i] - b[:n, :])
        ek.append(e)
        k_til.append(jnp.concatenate([kf[:n, :] * e, jnp.zeros((2 * c_len - n, width), F32)], axis=0))
    return q_til, k_til, eq, ek


def _split2(x):
    hi = x.astype(BF16)
    return hi, (x - hi.astype(F32)).astype(BF16)


def hgrn_fwd(proj, lb, ride=None):
    s = proj.shape[0]
    c_len, nh, hd = HGRN_CHUNK, HGRN_HEADS, HGRN_HEAD_DIM
    n_chunks = s // c_len
    col0 = 0

    cps = 2 * HGRN_CHUNKS_PER_STEP
    n_steps = n_chunks // cps

    def body(q_ref, f_ref, i_ref, lb_ref, o_ref, st_out_ref, a_out_ref, st_ref, b_ref, qf_ref, kf_ref, a_ref):
        @pl.when(pl.program_id(0) == 0)
        def _():
            st_ref[...] = jnp.zeros_like(st_ref)

        lbv = _lower_bound(lb_ref[...])
        for u in range(cps):
            rs = slice(u * c_len, (u + 1) * c_len)
            b_u, qf_u, kf_u, a_u = b_ref.at[u], qf_ref.at[u], kf_ref.at[u], a_ref.at[u]
            _, qf, _, log_f, kf = _hgrn_gates(q_ref[rs, :], f_ref[rs, :], lbv)
            b = _tri_sum(_tril_bf16(c_len), log_f)
            b_u[...] = b
            qf_u[...] = qf
            kf_u[...] = kf
            a_u[...] = jnp.zeros_like(a_u)

            def diagonal(rows, j, key, b_u=b_u, qf_u=qf_u, kf_u=kf_u, a_u=a_u):
                bj = b_u[pl.ds(key, 1), :]
                kj = kf_u[pl.ds(key, 1), :]
                nrow = rows.stop - rows.start
                t_loc = lax.broadcasted_iota(jnp.int32, (nrow, nh * hd), 0) + (rows.start % SUB_BLOCK)
                e = jnp.exp(jnp.where(t_loc >= j, b_u[rows, :] - bj, NEG_BIG))
                prod = qf_u[rows, :] * kj * e
                lane = lax.broadcasted_iota(jnp.int32, (nrow, hd), 1)
                for h in range(nh):
                    col = jnp.sum(prod[:, h * hd:(h + 1) * hd], axis=-1, keepdims=True)
                    a_u[h, rows, :] = jnp.where(lane == key, col, a_u[h, rows, :])

            _hgrn_diagonal_loops(c_len, diagonal)
            q_til, k_til, _, _ = _hgrn_off_diagonal(b, qf, kf)
            q_til = q_til.astype(BF16)
            k_til = [k.astype(BF16) for k in k_til]

            b_last = b[c_len - 1:c_len, :]
            qb = (qf * jnp.exp(b)).astype(BF16)
            kb2 = (kf * jnp.exp(b_last - b)).astype(BF16)
            vf = i_ref[rs, :].astype(BF16)
            for h in range(nh):
                hs = slice(h * hd, (h + 1) * hd)
                st = st_ref[h]
                st_out_ref[u, h] = st
                off = [jnp.zeros((SUB_BLOCK, hd), F32)]
                for i in range(1, c_len // SUB_BLOCK):
                    off.append(_dot_nt(q_til[i * SUB_BLOCK:(i + 1) * SUB_BLOCK, hs], k_til[i - 1][:, hs]))
                a_h = a_u[h] + jnp.concatenate(off, axis=0)
                a_out_ref[rs, hs] = a_h
                o_ref[rs, hs] = _dot_nt(qb[:, hs], st.astype(BF16)) + _dot(a_h[:, :c_len].astype(BF16), vf[:, hs])
                st_ref[h] = st * jnp.exp(b_last[:, hs]) + _dot_tn(vf[:, hs], kb2[:, hs])

    blk = (cps * c_len, HGRN_WIDTH)
    sblk = (cps, c_len, HGRN_WIDTH)
    step = lambda k: (lambda: pl.program_id(0) == k)
    e_in, e_out, e_shape, e_scr, e_args = _ride_specs(ride)
    return pl.pallas_call(
        _riding(body, 4, 3, 5, ride, step(0), step(n_steps // 2), step(n_steps - 1), late=step((3 * n_steps) // 4)),
        name="hgrn_fwd",
        grid=(n_steps,),
        in_specs=[
            pl.BlockSpec(blk, lambda c: (c, col0)),
            pl.BlockSpec(blk, lambda c: (c, col0 + 1)),
            pl.BlockSpec(blk, lambda c: (c, col0 + 2)),
            pl.BlockSpec((2, HGRN_WIDTH), lambda c: (0, 0)),
        ] + e_in,
        out_specs=[
            pl.BlockSpec(blk, lambda c: (c, 0)),
            pl.BlockSpec((cps, nh, hd, hd), lambda c: (c, 0, 0, 0)),
            pl.BlockSpec(blk, lambda c: (c, 0)),
        ] + e_out,
        out_shape=[
            jax.ShapeDtypeStruct((s, HGRN_WIDTH), F32),
            jax.ShapeDtypeStruct((n_chunks, nh, hd, hd), F32),
            jax.ShapeDtypeStruct((s, nh * hd), F32),
        ] + e_shape,
        scratch_shapes=[
            pltpu.VMEM((nh, hd, hd), F32),
            pltpu.VMEM(sblk, F32),
            pltpu.VMEM(sblk, F32),
            pltpu.VMEM(sblk, F32),
            pltpu.VMEM((cps, nh, c_len, hd), F32),
        ] + e_scr,
        compiler_params=_params(ride, dimension_semantics=("arbitrary",)),
    )(proj, proj, proj, lb, *e_args)


def hgrn_bwd(proj, lb, d_o, states, a_mat, ride=None):
    s = proj.shape[0]
    c_len, nh, hd = HGRN_CHUNK, HGRN_HEADS, HGRN_HEAD_DIM
    n_chunks = s // c_len
    col0 = 0
    cps = HGRN_CHUNKS_PER_STEP
    n_steps = n_chunks // cps
    last = n_steps - 1

    def body(q_ref, f_ref, i_ref, lb_ref, do_ref, st_in_ref, a_in_ref, dq_ref, df_ref, di_ref, dlb_ref,
             dst_ref, b_ref, qf_ref, kf_ref, da_ref, dqi_ref, dki_ref):
        @pl.when(pl.program_id(0) == 0)
        def _():
            dst_ref[...] = jnp.zeros_like(dst_ref)
            dlb_ref[...] = jnp.zeros_like(dlb_ref)

        lbv = _lower_bound(lb_ref[...])
        for u in reversed(range(cps)):
            rs = slice(u * c_len, (u + 1) * c_len)
            b_u, qf_u, kf_u, da_u, dqi_u, dki_u = (b_ref.at[u], qf_ref.at[u], kf_ref.at[u], da_ref.at[u], dqi_ref.at[u],
                                                   dki_ref.at[u])
            q = q_ref[rs, :]
            sq, qf, (sig, sig_neg, log_c), log_f, kf = _hgrn_gates(q, f_ref[rs, :], lbv)
            b = _tri_sum(_tril_bf16(c_len), log_f)
            b_u[...] = b
            qf_u[...] = qf
            kf_u[...] = kf
            b_last = b[c_len - 1:c_len, :]
            eb = jnp.exp(b)
            ebl = jnp.exp(b_last - b)
            qb = qf * eb
            kb2 = kf * ebl
            vf = i_ref[rs, :]
            d_o = do_ref[rs, :]
            qb_b, kb2_b, vf_b, do_b = qb.astype(BF16), kb2.astype(BF16), vf.astype(BF16), d_o.astype(BF16)
            tq = lax.broadcasted_iota(jnp.int32, (c_len, hd), 0)
            lane = lax.broadcasted_iota(jnp.int32, (c_len, hd), 1)

            dqb_parts, dvf_parts, dkb2_parts, dbl_parts = [], [], [], []
            for h in range(nh):
                hs = slice(h * hd, (h + 1) * hd)
                st = st_in_ref[u, h]
                dst = dst_ref[h]
                st_b, dst_b = st.astype(BF16), dst.astype(BF16)
                a_h = a_in_ref[rs, hs][:, :c_len].astype(BF16)
                dqb_parts.append(_dot(do_b[:, hs], st_b))
                dvf_parts.append(_dot_tn(a_h, do_b[:, hs]) + _dot_nt(kb2_b[:, hs], dst_b))
                dkb2_parts.append(_dot(vf_b[:, hs], dst_b))
                da = _dot_nt(do_b[:, hs], vf_b[:, hs])
                da = jnp.concatenate([da, jnp.zeros((c_len, hd - c_len), F32)], axis=1)
                da_u[h] = jnp.where(tq >= lane, da, 0.0)
                dbl_parts.append(jnp.sum(dst * st, axis=0, keepdims=True) * jnp.exp(b_last[:, hs]))
                dst_ref[h] = dst * jnp.exp(b_last[:, hs]) + _dot_tn(do_b[:, hs], qb_b[:, hs])
            dqb = jnp.concatenate(dqb_parts, axis=1)
            dvf = jnp.concatenate(dvf_parts, axis=1)
            dkb2 = jnp.concatenate(dkb2_parts, axis=1)
            dbl = jnp.concatenate(dbl_parts, axis=1) + jnp.sum(dkb2 * kb2, axis=0, keepdims=True)

            dqi_u[...] = jnp.zeros_like(dqi_u)
            t_idx = lax.broadcasted_iota(jnp.int32, (c_len, nh * hd), 0)

            def diagonal(rows, j, key, b_u=b_u, qf_u=qf_u, kf_u=kf_u, da_u=da_u, dqi_u=dqi_u, dki_u=dki_u):
                bj = b_u[pl.ds(key, 1), :]
                kj = kf_u[pl.ds(key, 1), :]
                nrow = rows.stop - rows.start
                t_loc = lax.broadcasted_iota(jnp.int32, (nrow, nh * hd), 0) + (rows.start % SUB_BLOCK)
                e = jnp.exp(jnp.where(t_loc >= j, b_u[rows, :] - bj, NEG_BIG))
                lane_r = lax.broadcasted_iota(jnp.int32, (nrow, hd), 1)
                cols = [jnp.sum(jnp.where(lane_r == key, da_u[h, rows, :], 0.0), axis=-1, keepdims=True)
                        for h in range(nh)]
                w = e * jnp.concatenate([jnp.broadcast_to(cc, (nrow, hd)) for cc in cols], axis=1)
                dqi_u[rows, :] += w * kj
                dki_u[pl.ds(key, 1), :] = jnp.sum(w * qf_u[rows, :], axis=0, keepdims=True)

            _hgrn_diagonal_loops(c_len, diagonal)

            q_til, k_til, eq, ek = _hgrn_off_diagonal(b, qf, kf)
            q_hi, q_lo = _split2(q_til)
            k_pairs = [_split2(k) for k in k_til]
            n_sub = c_len // SUB_BLOCK
            dq_heads, dk_heads = [], []
            for h in range(nh):
                hs = slice(h * hd, (h + 1) * hd)
                dq_rows = [jnp.zeros((SUB_BLOCK, hd), F32)]
                dk_h = jnp.zeros((c_len, hd), F32)
                for i in range(1, n_sub):
                    rows = slice(i * SUB_BLOCK, (i + 1) * SUB_BLOCK)
                    n = i * SUB_BLOCK
                    da_i = da_u[h, rows, :].astype(BF16)
                    k_hi, k_lo = k_pairs[i - 1]
                    dq_rows.append((_dot(da_i, k_hi[:, hs]) + _dot(da_i, k_lo[:, hs])) * eq[rows, hs])
                    dk_t = (_dot_tn(da_i, q_hi[rows, hs]) + _dot_tn(da_i, q_lo[rows, hs]))[:n, :] * ek[i - 1][:, hs]
                    dk_h = dk_h + jnp.concatenate([dk_t, jnp.zeros((c_len - n, hd), F32)], axis=0)
                dq_heads.append(jnp.concatenate(dq_rows, axis=0))
                dk_heads.append(dk_h)
            dq_intra = dqi_u[...] + jnp.concatenate(dq_heads, axis=1)
            dk_intra = dki_u[...] + jnp.concatenate(dk_heads, axis=1)

            db = dqb * qb + qf * dq_intra - kf * dk_intra - dkb2 * kb2
            db = db + jnp.where(t_idx == c_len - 1, dbl, 0.0)
            dg = _tri_sum(_tril_bf16(c_len, upper=True), db)
            dqf = dqb * eb + dq_intra
            dkf = dkb2 * ebl + dk_intra
            dq_ref[rs, :] = (dqf * (sq * (1.0 + q * (1.0 - sq)))).astype(BF16)
            df_ref[rs, :] = (sig_neg * (dg * jnp.exp(log_c - log_f) - dkf * (1.0 - lbv) * sig)).astype(BF16)
            di_ref[rs, :] = dvf.astype(BF16)
            dlb_ref[...] += jnp.sum(sig_neg * (dg * jnp.exp(-log_f) - dkf), axis=0, keepdims=True)

    blk = (cps * c_len, HGRN_WIDTH)
    sblk = (cps, c_len, HGRN_WIDTH)
    rev = lambda c: last - c
    step = lambda k: (lambda: pl.program_id(0) == k)
    e_in, e_out, e_shape, e_scr, e_args = _ride_specs(ride)
    return pl.pallas_call(
        _riding(body, 7, 4, 7, ride, step(0), step(n_steps // 2), step(last)),
        name="hgrn_bwd",
        grid=(n_steps,),
        in_specs=[
            pl.BlockSpec(blk, lambda c: (rev(c), col0)),
            pl.BlockSpec(blk, lambda c: (rev(c), col0 + 1)),
            pl.BlockSpec(blk, lambda c: (rev(c), col0 + 2)),
            pl.BlockSpec((2, HGRN_WIDTH), lambda c: (0, 0)),
            pl.BlockSpec(blk, lambda c: (rev(c), 0)),
            pl.BlockSpec((cps, nh, hd, hd), lambda c: (rev(c), 0, 0, 0)),
            pl.BlockSpec(blk, lambda c: (rev(c), 0)),
        ] + e_in,
        out_specs=[
            pl.BlockSpec(blk, lambda c: (rev(c), 0)),
            pl.BlockSpec(blk, lambda c: (rev(c), 0)),
            pl.BlockSpec(blk, lambda c: (rev(c), 0)),
            pl.BlockSpec((1, HGRN_WIDTH), lambda c: (0, 0)),
        ] + e_out,
        out_shape=[jax.ShapeDtypeStruct((s, HGRN_WIDTH), BF16)] * 3 + [jax.ShapeDtypeStruct((1, HGRN_WIDTH), F32)] + e_shape,
        scratch_shapes=[
            pltpu.VMEM((nh, hd, hd), F32),
            pltpu.VMEM(sblk, F32),
            pltpu.VMEM(sblk, F32),
            pltpu.VMEM(sblk, F32),
            pltpu.VMEM((cps, nh, c_len, hd), F32),
            pltpu.VMEM(sblk, F32),
            pltpu.VMEM(sblk, F32),
        ] + e_scr,
        compiler_params=_params(ride, dimension_semantics=("arbitrary",)),
    )(proj, proj, proj, lb, d_o, states, a_mat, *e_args)


def _per_head_lanes(x):
    lane = lax.broadcasted_iota(jnp.int32, (x.shape[0], LANES), 1)
    out = jnp.zeros((x.shape[0], LANES), F32)
    for h in range(ATTN_HEADS):
        out = jnp.where(lane == h, x[:, h * ATTN_HEAD_DIM:h * ATTN_HEAD_DIM + 1], out)
    return out


def _row_spec(tm, width, col=0):
    return pl.BlockSpec((tm, width), lambda i: (i, col))


def _const_spec(width):
    return pl.BlockSpec((1, width), lambda i: (0, 0))


def _acc_rows(ref, value):
    @pl.when(pl.program_id(0) == 0)
    def _():
        ref[...] = jnp.zeros_like(ref)

    ref[...] += jnp.sum(value, axis=0, keepdims=True)


def mix_fwd(attn_parts, o_h, proj, an, hn, w_out_b, gp, x, ride=None):
    s = x.shape[0]
    tm = TOKEN_TILE
    gate_col = 3
    hd = HGRN_HEAD_DIM
    nd = len(DILATIONS)

    def body(*refs):
        o_refs, l_refs = refs[:nd], refs[nd:2 * nd]
        oh_ref, gate_ref, an_ref, hn_ref, w_ref, gp_ref, x_ref = refs[2 * nd:2 * nd + 7]
        x1_ref, cat_ref, mixed_ref, attn_ref = refs[2 * nd + 7:2 * nd + 11]
        lse_refs = refs[2 * nd + 11:3 * nd + 11]
        o_scr, l_scr, lse_scr = refs[3 * nd + 11:]
        os_ = [_from_dilated(r, o_scr.at[k], d, tm) for k, (r, d) in enumerate(zip(o_refs, DILATIONS))]
        ls = [_from_dilated(r, l_scr.at[k], d, tm) for k, (r, d) in enumerate(zip(l_refs, DILATIONS))]
        m = jnp.maximum(jnp.maximum(ls[0], ls[1]), ls[2])
        es = [jnp.exp(l - m) for l in ls]
        den = es[0] + es[1] + es[2]
        attn = (es[0] * os_[0] + es[1] * os_[1] + es[2] * os_[2]) / den
        attn_ref[...] = attn
        lse_scr[0] = _per_head_lanes(m + jnp.log(den))
        for d, ref in zip(DILATIONS, lse_refs):
            _to_dilated(lse_scr, ref, d, tm)
        cat_ref[:, :ATTN_WIDTH] = _rms_fwd(attn, an_ref[...], ATTN_WIDTH).astype(BF16)
        gate = gate_ref[...]
        silu_g = gate * _sigmoid(gate)
        for h in range(HGRN_HEADS):
            hs = slice(h * hd, (h + 1) * hd)
            rec = _rms_fwd(oh_ref[:, hs], hn_ref[:, hs], hd) * silu_g[:, hs]
            cat_ref[:, ATTN_WIDTH + h * hd:ATTN_WIDTH + (h + 1) * hd] = rec.astype(BF16)
        mixed = _dot(cat_ref[...], w_ref[...])
        mixed_ref[...] = mixed
        x1_ref[...] = x_ref[...] + _rms_fwd(mixed, gp_ref[...], D_MODEL)

    aw = ATTN_WIDTH
    n_steps = s // tm
    step = lambda k: (lambda: pl.program_id(0) == k)
    e_in, e_out, e_shape, e_scr, e_args = _ride_specs(ride)
    return pl.pallas_call(
        _riding(body, 2 * nd + 7, 4 + nd, 3, ride, step(0), step((13 * n_steps) // 16), step(n_steps - 1)),
        name="mix_fwd",
        grid=(n_steps,),
        in_specs=[_dilated_spec(d, tm, aw) for d in DILATIONS] * 2 + [
            _row_spec(tm, aw), _row_spec(tm, aw, gate_col), _const_spec(aw), _const_spec(aw), _vmem_spec(),
            _const_spec(D_MODEL), _row_spec(tm, D_MODEL)] + e_in,
        out_specs=[_row_spec(tm, D_MODEL), _row_spec(tm, D_MODEL), _row_spec(tm, D_MODEL), _row_spec(tm, aw)] + [
            _dilated_spec(d, tm, LANES) for d in DILATIONS] + e_out,
        out_shape=[
            jax.ShapeDtypeStruct((s, D_MODEL), F32),
            jax.ShapeDtypeStruct((s, D_MODEL), BF16),
            jax.ShapeDtypeStruct((s, D_MODEL), F32),
            jax.ShapeDtypeStruct((s, aw), F32),
        ] + [jax.ShapeDtypeStruct((d, s // d, LANES), F32) for d in DILATIONS] + e_shape,
        scratch_shapes=[pltpu.VMEM((nd, aw // LANES, tm, LANES), F32), pltpu.VMEM((nd, aw // LANES, tm, LANES), F32),
                        pltpu.VMEM((1, tm, LANES), F32)] + e_scr,
        compiler_params=_params(ride, dimension_semantics=("arbitrary",)),
    )(*[p[0] for p in attn_parts], *[p[1] for p in attn_parts], o_h, proj, an, hn, w_out_b, gp, x, *e_args)


def mix_bwd(dx1, mixed, gp, w_out_b, attn, an, o_h, proj, hn):
    s = dx1.shape[0]
    tm = TOKEN_TILE
    gate_col = 3
    hd = HGRN_HEAD_DIM
    aw = ATTN_WIDTH

    nd = len(DILATIONS)

    def body(*refs):
        dx1_ref, mixed_ref, gp_ref, w_ref, attn_ref, an_ref, oh_ref, gate_ref, hn_ref, dmix_ref = refs[:10]
        do_refs, delta_refs = refs[10:10 + nd], refs[10 + nd:10 + 2 * nd]
        doh_ref, dgate_ref, dgp_ref, dan_ref, dhn_ref, do_ref, delta_ref = refs[10 + 2 * nd:]
        dmixed, gp_c = _rms_bwd(dx1_ref[...], mixed_ref[...], gp_ref[...], D_MODEL)
        _acc_rows(dgp_ref, gp_c)
        dmixed_b = dmixed.astype(BF16)
        dmix_ref[...] = dmixed_b
        dcat = _dot_nt(dmixed_b, w_ref[...])
        attn = attn_ref[...]
        d_o, an_c = _rms_bwd(dcat[:, :aw], attn, an_ref[...], aw)
        _acc_rows(dan_ref, an_c)
        _lane_blocks(do_ref, d_o)
        prod = d_o * attn
        lane = lax.broadcasted_iota(jnp.int32, (tm, LANES), 1)
        delta = jnp.zeros((tm, LANES), F32)
        for pair in range(ATTN_HEADS // 2):
            pp = prod[:, pair * LANES:(pair + 1) * LANES]
            low = _lane_half((tm, LANES), 0)
            lo = jnp.sum(jnp.where(low, pp, 0.0), axis=-1, keepdims=True)
            hi = jnp.sum(jnp.where(low, 0.0, pp), axis=-1, keepdims=True)
            delta = jnp.where(lane == 2 * pair, lo, jnp.where(lane == 2 * pair + 1, hi, delta))
        delta_ref[0] = delta
        for d, o_ref, l_ref in zip(DILATIONS, do_refs, delta_refs):
            _to_dilated(do_ref, o_ref, d, tm, cast=BF16)
            _to_dilated(delta_ref, l_ref, d, tm)
        gate = gate_ref[...]
        sg = _sigmoid(gate)
        silu_g = gate * sg
        drec = dcat[:, aw:]
        hn_parts = []
        for h in range(HGRN_HEADS):
            hs = slice(h * hd, (h + 1) * hd)
            oh = oh_ref[:, hs]
            on = _rms_fwd(oh, hn_ref[:, hs], hd)
            dgate_ref[:, hs] = (drec[:, hs] * on * (sg[:, hs] * (1.0 + gate[:, hs] * (1.0 - sg[:, hs])))).astype(BF16)
            d_oh, hn_c = _rms_bwd(drec[:, hs] * silu_g[:, hs], oh, hn_ref[:, hs], hd)
            doh_ref[:, hs] = d_oh
            hn_parts.append(hn_c)
        _acc_rows(dhn_ref, jnp.concatenate(hn_parts, axis=1))

    return pl.pallas_call(
        body,
        name="mix_bwd",
        grid=(s // tm,),
        in_specs=[_row_spec(tm, D_MODEL), _row_spec(tm, D_MODEL), _const_spec(D_MODEL), _vmem_spec(), _row_spec(tm, aw),
                  _const_spec(aw), _row_spec(tm, aw), _row_spec(tm, aw, gate_col), _const_spec(aw)],
        out_specs=[_row_spec(tm, D_MODEL)] + [_dilated_spec(d, tm, aw) for d in DILATIONS] + [
            _dilated_spec(d, tm, LANES) for d in DILATIONS] + [_row_spec(tm, aw)] * 2 + [
            _const_spec(D_MODEL), _const_spec(aw), _const_spec(aw)],
        out_shape=[jax.ShapeDtypeStruct((s, D_MODEL), BF16)] + [
            jax.ShapeDtypeStruct((d, s // d, aw), BF16) for d in DILATIONS] + [
            jax.ShapeDtypeStruct((d, s // d, LANES), F32) for d in DILATIONS] + [
            jax.ShapeDtypeStruct((s, aw), F32), jax.ShapeDtypeStruct((s, aw), BF16),
            jax.ShapeDtypeStruct((1, D_MODEL), F32), jax.ShapeDtypeStruct((1, aw), F32),
            jax.ShapeDtypeStruct((1, aw), F32)],
        scratch_shapes=[pltpu.VMEM((aw // LANES, tm, LANES), F32), pltpu.VMEM((1, tm, LANES), F32)],
        compiler_params=_params(dimension_semantics=("arbitrary",)),
    )(dx1, mixed, gp, w_out_b, attn, an, o_h, proj, hn)


def mlp_fwd_bwd(x1, g_pre, w1_blocks, w2_b, g_post, target):
    s = x1.shape[0]
    tm = MLP_TILE
    nblk, _, fb = w1_blocks.shape

    def body(x1_ref, gpre_ref, w1_ref, w2_ref, gpost_ref, t_ref,
             dx1_ref, h2_ref, a_ref, du_ref, dff_ref, loss_ref, dgpre_ref, dgpost_ref, u_ref):
        x1v = x1_ref[...]
        h2 = _rms_fwd(x1v, gpre_ref[...], D_MODEL).astype(BF16)
        h2_ref[...] = h2
        ff = jnp.zeros((tm, D_MODEL), F32)
        for j in range(nblk):
            cols = slice(j * fb, (j + 1) * fb)
            ru = jnp.maximum(_dot(h2, w1_ref[j]), 0.0)
            u_ref[:, cols] = ru.astype(BF16)
            a = (ru * ru).astype(BF16)
            a_ref[:, cols] = a
            ff = ff + _dot(a, w2_ref[cols, :])
        diff = x1v + _rms_fwd(ff, gpost_ref[...], D_MODEL) - t_ref[...]
        _acc_rows(loss_ref, diff * diff)
        dy = diff * (1.0 / D_MODEL)
        dff, gpost_c = _rms_bwd(dy, ff, gpost_ref[...], D_MODEL)
        _acc_rows(dgpost_ref, gpost_c)
        dff_b = dff.astype(BF16)
        dff_ref[...] = dff_b
        dh2 = jnp.zeros((tm, D_MODEL), F32)
        for j in range(nblk):
            cols = slice(j * fb, (j + 1) * fb)
            du = (_dot_nt(dff_b, w2_ref[cols, :]) * (2.0 * u_ref[:, cols])).astype(BF16)
            du_ref[:, cols] = du
            dh2 = dh2 + _dot_nt(du, w1_ref[j])
        dxa, gpre_c = _rms_bwd(dh2, x1v, gpre_ref[...], D_MODEL)
        _acc_rows(dgpre_ref, gpre_c)
        dx1_ref[...] = dy + dxa

    dm = D_MODEL
    return pl.pallas_call(
        body,
        name="mlp_fwd_bwd",
        grid=(s // tm,),
        in_specs=[_row_spec(tm, dm), _const_spec(dm), _vmem_spec(), _vmem_spec(), _const_spec(dm), _row_spec(tm, dm)],
        out_specs=[_row_spec(tm, dm), _row_spec(tm, dm), _row_spec(tm, D_FF), _row_spec(tm, D_FF), _row_spec(tm, dm),
                   _const_spec(dm), _const_spec(dm), _const_spec(dm)],
        out_shape=[
            jax.ShapeDtypeStruct((s, dm), F32),
            jax.ShapeDtypeStruct((s, dm), BF16),
            jax.ShapeDtypeStruct((s, D_FF), BF16),
            jax.ShapeDtypeStruct((s, D_FF), BF16),
            jax.ShapeDtypeStruct((s, dm), BF16),
            jax.ShapeDtypeStruct((1, dm), F32),
            jax.ShapeDtypeStruct((1, dm), F32),
            jax.ShapeDtypeStruct((1, dm), F32),
        ],
        scratch_shapes=[pltpu.VMEM((tm, D_FF), BF16)],
        compiler_params=_params(dimension_semantics=("arbitrary",)),
    )(x1, g_pre, w1_blocks, w2_b, g_post, target)


def in_proj_bwd(attn_grads, hgrn_grads, dgate, w_in_b, x, g1, dx1):
    s = x.shape[0]
    tm = PROJ_TILE
    aw = ATTN_WIDTH
    n_attn = len(attn_grads)
    flat = [g[k] for k in range(3) for g in attn_grads] + list(hgrn_grads) + [dgate]

    def body(*refs):
        parts = refs[:len(flat)]
        w_ref, x_ref, g_ref, dx1_ref, dx_ref, dproj_ref, dg_ref, scr = refs[len(flat):]
        groups = []
        for k in range(3):
            acc = None
            for p, d in zip(parts[k * n_attn:(k + 1) * n_attn], DILATIONS):
                v = _from_dilated(p, scr, d, tm)
                acc = v if acc is None else acc + v
            groups.append(acc)
        groups += [p[...] for p in parts[3 * n_attn:]]
        dh = jnp.zeros((tm, D_MODEL), F32)
        for gi, grp in enumerate(groups):
            cols = slice(gi * aw, (gi + 1) * aw)
            gb = grp.astype(BF16)
            dproj_ref[:, cols] = gb
            dh = dh + _dot_nt(gb, w_ref[:, cols])
        dxa, g_c = _rms_bwd(dh, x_ref[...], g_ref[...], D_MODEL)
        _acc_rows(dg_ref, g_c)
        dx_ref[...] = dx1_ref[...] + dxa

    dm = D_MODEL
    return pl.pallas_call(
        body,
        name="in_proj_bwd",
        grid=(s // tm,),
        in_specs=[_dilated_spec(d, tm, aw) for d in DILATIONS] * 3 + [_row_spec(tm, aw)] * 4 + [
            _vmem_spec(), _row_spec(tm, dm), _const_spec(dm), _row_spec(tm, dm)],
        out_specs=[_row_spec(tm, dm), _row_spec(tm, IN_PROJ_WIDTH), _const_spec(dm)],
        out_shape=[jax.ShapeDtypeStruct((s, dm), F32), jax.ShapeDtypeStruct((s, IN_PROJ_WIDTH), BF16),
                   jax.ShapeDtypeStruct((1, dm), F32)],
        scratch_shapes=[pltpu.VMEM((aw // LANES, tm, LANES), F32)],
        compiler_params=_params(dimension_semantics=("arbitrary",)),
    )(*flat, w_in_b, x, g1, dx1)


def wgrad(a_b, b_b, tn, name, ts=2048, per_step=1, ride=None):
    s, k = a_b.shape
    n = b_b.shape[1]

    def body(a_ref, b_ref, o_ref):
        @pl.when(pl.program_id(1) == 0)
        def _():
            o_ref[...] = jnp.zeros_like(o_ref)

        a = a_ref[...]
        for jj in range(per_step):
            o_ref[jj] += _dot_tn(a, b_ref[:, jj * tn:(jj + 1) * tn])

    wide = tn * per_step
    gn, gs = n // wide, s // ts
    step = lambda j, i: (lambda: (pl.program_id(0) == j) & (pl.program_id(1) == i))
    e_in, e_out, e_shape, e_scr, e_args = _ride_specs(ride)
    out = pl.pallas_call(
        _riding(body, 2, 1, 0, ride, step(0, 0), step(gn // 2, 0), step(gn - 1, gs - 1)),
        name=name,
        grid=(gn, gs),
        in_specs=[pl.BlockSpec((ts, k), lambda j, i: (i, 0)), pl.BlockSpec((ts, wide), lambda j, i: (i, j))] + e_in,
        out_specs=[pl.BlockSpec((per_step, k, tn), lambda j, i: (j, 0, 0))] + e_out,
        out_shape=[jax.ShapeDtypeStruct((n // tn, k, tn), F32)] + e_shape,
        scratch_shapes=e_scr,
        compiler_params=_params(ride, dimension_semantics=("arbitrary", "arbitrary")),
    )(a_b, b_b, *e_args)
    return out[0] if ride is None else out


def train_step(x, target, g1, an, logits, hn, gp, g_pre, g_post, w, m, v):
    nd = len(DILATIONS)
    shard_b = {k: w[k].astype(BF16) for k in BIG}
    (w_in_g,) = run_exchange(gather_exchange([shard_b["w_in"]]), "gather_w_in")
    w_in_b = w_in_g.transpose(1, 0, 2).reshape(D_MODEL, IN_PROJ_WIDTH)

    proj, h_b, *qkvs, w2_g = in_proj_fwd(x, g1, w_in_b, ride=gather_exchange([shard_b["w_ff2"]]))
    w2_b = w2_g.reshape(D_FF, D_MODEL)
    attn_parts = [attn_fwd(qkv, d) for qkv, d in zip(qkvs, DILATIONS)]
    o_h, states, a_mat, w_out_g, w1_blocks = hgrn_fwd(
        proj, logits, ride=gather_exchange([shard_b["w_out"], shard_b["w_ff1"]]))
    w_out_b = w_out_g.reshape(D_MODEL, D_MODEL)
    x1, cat_b, mixed, attn, *lses = mix_fwd(attn_parts, o_h, proj, an, hn, w_out_b, gp, x)
    dx1, h2_b, a_b, du_b, dff_b, loss_vec, dg_pre, dg_post = mlp_fwd_bwd(x1, g_pre, w1_blocks, w2_b, g_post, target)
    dw2 = wgrad(a_b, dff_b, D_MODEL, "wgrad_ff2", ts=512)
    dw1 = wgrad(h2_b, du_b, D_FF // N_DEV, "wgrad_ff1", per_step=2)
    dmix_b, *rest = mix_bwd(dx1, mixed, gp, w_out_b, attn, an, o_h, proj, hn)
    d_os, deltas = rest[:nd], rest[nd:2 * nd]
    d_oh, dgate, dgp, dan, dhn = rest[2 * nd:]
    dwout = wgrad(cat_b, dmix_b, D_MODEL, "wgrad_out")

    early = ("w_out", "w_ff1", "w_ff2")
    early_grads = [dwout.reshape(N_DEV, D_MODEL // N_DEV, D_MODEL), dw1, dw2.reshape(N_DEV, D_FF // N_DEV, D_MODEL)]
    res = attn_bwd(qkvs[0], d_os[0], lses[0], deltas[0], DILATIONS[0], ride=to_core_exchange(early_grads))
    pairs = [pair_sum(g, s, f"pair_sum_{name}") for g, s, name in zip(early_grads, res[3:], early)]
    attn_grads = [res[:3]]
    *res, others_ff2 = attn_bwd(qkvs[1], d_os[1], lses[1], deltas[1], DILATIONS[1],
                                ride=to_chip_exchange([pairs[2][1]]))
    attn_grads.append(res)
    attn_grads.append(attn_bwd(qkvs[2], d_os[2], lses[2], deltas[2], DILATIONS[2]))
    dq_h, df_h, di_h, dlb, *others = hgrn_bwd(proj, logits, d_oh, states, a_mat,
                                              ride=to_chip_exchange([pairs[0][1], pairs[1][1]]))
    others.append(others_ff2)
    dx, dproj_b, dg1 = in_proj_bwd(attn_grads, (dq_h, df_h, di_h), dgate, w_in_b, x, g1, dx1)
    packed = _pack_small(dg1, dgp, dg_pre, dg_post, dan, dhn, dlb, loss_vec)
    dwin, small_slots = wgrad(h_b, dproj_b, 2 * IN_PROJ_WIDTH // N_DEV, "wgrad_in",
                              ride=small_exchange(packed))
    big = {name: sum_adamw(p[0], o, w[name], m[name], v[name], f"sum_adamw_{name}")
           for name, p, o in zip(early, pairs, others)}

    shard_w = IN_PROJ_WIDTH // N_DEV
    dwin_blocks = dwin.reshape(N_DEV // 2, D_MODEL, 2, shard_w).transpose(0, 2, 1, 3).reshape(N_DEV, D_MODEL, shard_w)
    (from_sibling,) = run_exchange(to_core_exchange([dwin_blocks.astype(BF16)]), "reduce_w_in_to_core")
    pair_in, pair_in_b = pair_sum(dwin_blocks, from_sibling, "pair_sum_w_in")
    (others_in,) = run_exchange(to_chip_exchange([pair_in_b]), "reduce_w_in_to_chip")
    big["w_in"] = sum_adamw(pair_in, others_in, w["w_in"], m["w_in"], v["w_in"], "sum_adamw_w_in")
    return dx, big, small_slots


def _position():
    x, y, c = lax.axis_index("x"), lax.axis_index("y"), lax.axis_index("c")
    other_chips = [(1 - x, y), (x, 1 - y), (1 - x, 1 - y)]
    return x, y, c, other_chips


def _any_spec():
    return pl.BlockSpec(memory_space=pl.ANY)


class Exchange:
    def __init__(self, arrays, out_shape, sems, stages, collective_id, peers):
        self.arrays, self.out_shape, self.sems, self.stages = list(arrays), list(out_shape), list(sems), stages
        self.collective_id, self.peers = collective_id, peers

    def open(self):
        barrier = pltpu.get_barrier_semaphore()
        peers = self.peers()
        for peer in peers:
            pl.semaphore_signal(barrier, inc=1, device_id=peer, device_id_type=MESH)
        pl.semaphore_wait(barrier, len(peers))


def _siblings():
    x, y, c, _ = _position()
    return [(x, y, 1 - c)]


def _same_core_of_other_chips():
    x, y, c, chips = _position()
    return [(px, py, c) for px, py in chips]


def _gather_peers():
    x, y, c, _ = _position()
    return [(x, y, 1 - c), (1 - x, y, c), (x, 1 - y, c)]


def _all_others():
    x, y, c, _ = _position()
    return [(1 - x if rel & 4 else x, 1 - y if rel & 2 else y, 1 - c if rel & 1 else c) for rel in range(1, N_DEV)]


def gather_exchange(shards):
    n = len(shards)
    halves = [sh.shape[0] // 2 for sh in shards]

    def stages(ins, outs, sems):
        send_sems, recv_sems, local_sems = sems

        def parts():
            x, y, c, _ = _position()
            me, sibling = (x, y, c), (x, y, 1 - c)
            nbr_x, nbr_y, diag = (1 - x, y, c), (x, 1 - y, c), (1 - x, 1 - y, c)

            def slot(a, dev, rows=None):
                ref = outs[a].at[4 * dev[0] + 2 * dev[1] + dev[2]]
                return ref if rows is None else ref.at[rows]

            def copy(a, k, block, to, rows=None, src=None):
                return pltpu.make_async_remote_copy(
                    src_ref=slot(a, block, rows) if src is None else src, dst_ref=slot(a, block, rows),
                    send_sem=send_sems.at[a, k], recv_sem=recv_sems.at[a, k], device_id=to, device_id_type=MESH)

            upper = lambda a: pl.ds(0, halves[a])
            lower = lambda a: pl.ds(halves[a], halves[a])
            return me, sibling, nbr_x, nbr_y, diag, slot, copy, upper, lower

        def begin():
            me, sibling, nbr_x, nbr_y, _, slot, copy, _, _ = parts()
            for a in range(n):
                pltpu.make_async_copy(ins[a], slot(a, me), local_sems.at[a]).start()
                for k, to in enumerate((sibling, nbr_x, nbr_y)):
                    copy(a, k, me, to, src=ins[a]).start()

        def middle():
            me, sibling, nbr_x, nbr_y, _, _, copy, upper, lower = parts()
            for a in range(n):
                copy(a, 1, nbr_x, me).wait_recv()
                copy(a, 3, nbr_x, sibling).start()
                copy(a, 5, nbr_x, nbr_y, rows=lower(a)).start()
                copy(a, 2, nbr_y, me).wait_recv()
                copy(a, 4, nbr_y, sibling).start()
                copy(a, 6, nbr_y, nbr_x, rows=upper(a)).start()

        def late():
            me, sibling, _, _, diag, _, copy, upper, lower = parts()
            for a in range(n):
                copy(a, 6, diag, me, rows=upper(a)).wait_recv()
                copy(a, 5, diag, me, rows=lower(a)).wait_recv()
                copy(a, 7, diag, sibling).start()

        def end():
            me, sibling, nbr_x, nbr_y, diag, slot, copy, upper, lower = parts()
            sib = lambda dev: (dev[0], dev[1], sibling[2])
            for a in range(n):
                for k, block in ((0, sibling), (3, sib(nbr_x)), (4, sib(nbr_y)), (7, sib(diag))):
                    copy(a, k, block, me).wait_recv()
                copy(a, 0, me, sibling, src=ins[a]).wait_send()
                copy(a, 1, me, nbr_x, src=ins[a]).wait_send()
                copy(a, 2, me, nbr_y, src=ins[a]).wait_send()
                copy(a, 3, nbr_x, sibling).wait_send()
                copy(a, 4, nbr_y, sibling).wait_send()
                copy(a, 5, nbr_x, nbr_y, rows=lower(a)).wait_send()
                copy(a, 6, nbr_y, nbr_x, rows=upper(a)).wait_send()
                copy(a, 7, diag, sibling).wait_send()
                pltpu.make_async_copy(ins[a], slot(a, me), local_sems.at[a]).wait()

        return begin, (middle, late), end

    return Exchange(
        shards, [jax.ShapeDtypeStruct((N_DEV,) + sh.shape, sh.dtype) for sh in shards],
        [pltpu.SemaphoreType.DMA((n, 8)), pltpu.SemaphoreType.DMA((n, 8)), pltpu.SemaphoreType.DMA((n,))], stages,
        collective_id=0, peers=_gather_peers)


def to_core_exchange(grads):
    n = len(grads)

    def stages(ins, outs, sems):
        send_sems, recv_sems = sems

        def copies():
            x, y, c, _ = _position()
            return [pltpu.make_async_remote_copy(
                src_ref=ins[a].at[2 * q + (1 - c)], dst_ref=outs[a].at[q], send_sem=send_sems.at[a, q],
                recv_sem=recv_sems.at[a, q], device_id=(x, y, 1 - c), device_id_type=MESH)
                for a in range(n) for q in range(4)]

        def begin():
            for cp in copies():
                cp.start()

        def end():
            for cp in copies():
                cp.wait()

        return begin, None, end

    return Exchange(grads, [jax.ShapeDtypeStruct((4,) + g.shape[1:], g.dtype) for g in grads],
                    [pltpu.SemaphoreType.DMA((n, 4)), pltpu.SemaphoreType.DMA((n, 4))], stages,
                    collective_id=1, peers=_siblings)


def pair_sum(grad, from_sibling, name):
    _, r, cdim = grad.shape
    tr = min(r, ELEMENTWISE_ROWS)
    c_idx = lax.axis_index("c").astype(jnp.int32).reshape(1)

    def body(c_ref, g_ref, s_ref, o_ref, ob_ref):
        total = g_ref[...] + s_ref[...]
        o_ref[...] = total
        ob_ref[...] = total.astype(BF16)

    blk = lambda: pl.BlockSpec((1, tr, cdim), lambda q, i, cr: (q, i, 0))
    return pl.pallas_call(
        body,
        name=name,
        grid_spec=pltpu.PrefetchScalarGridSpec(
            num_scalar_prefetch=1,
            grid=(4, r // tr),
            in_specs=[pl.BlockSpec((1, tr, cdim), lambda q, i, cr: (2 * q + cr[0], i, 0)), blk()],
            out_specs=[blk(), blk()],
        ),
        out_shape=[jax.ShapeDtypeStruct((4, r, cdim), F32), jax.ShapeDtypeStruct((4, r, cdim), BF16)],
        compiler_params=_params(dimension_semantics=("arbitrary", "arbitrary")),
    )(c_idx, grad, from_sibling)


def to_chip_exchange(pairs):
    n = len(pairs)

    def stages(ins, outs, sems):
        send_sems, recv_sems = sems

        def copies():
            x, y, c, chips = _position()
            return [pltpu.make_async_remote_copy(
                src_ref=ins[a].at[2 * px + py], dst_ref=outs[a].at[j], send_sem=send_sems.at[a, j],
                recv_sem=recv_sems.at[a, j], device_id=(px, py, c), device_id_type=MESH)
                for a in range(n) for j, (px, py) in enumerate(chips)]

        def begin():
            for cp in copies():
                cp.start()

        def end():
            for cp in copies():
                cp.wait()

        return begin, None, end

    return Exchange(pairs, [jax.ShapeDtypeStruct((3,) + p.shape[1:], p.dtype) for p in pairs],
                    [pltpu.SemaphoreType.DMA((n, 3)), pltpu.SemaphoreType.DMA((n, 3))], stages,
                    collective_id=2, peers=_same_core_of_other_chips)


def run_exchange(ex, name):
    n_in, n_out = len(ex.arrays), len(ex.out_shape)

    def body(*refs):
        begin, middle, end = ex.stages(refs[:n_in], refs[n_in:n_in + n_out], refs[n_in + n_out:])
        ex.open()
        begin()
        for stage in _as_tuple(middle):
            stage()
        end()

    return pl.pallas_call(
        body,
        name=name,
        in_specs=[_any_spec()] * n_in,
        out_specs=[_any_spec()] * n_out,
        out_shape=ex.out_shape,
        scratch_shapes=ex.sems,
        compiler_params=pltpu.CompilerParams(collective_id=ex.collective_id),
    )(*ex.arrays)


def _as_tuple(stages):
    return () if stages is None else stages if isinstance(stages, tuple) else (stages,)


def _riding(body, n_in, n_out, n_scratch, ex, first, middle, last, late=None):
    if ex is None:
        return body
    r_in, r_out = len(ex.arrays), len(ex.out_shape)

    def wrapped(*refs):
        k_in, refs = refs[:n_in], refs[n_in:]
        e_in, refs = refs[:r_in], refs[r_in:]
        k_out, refs = refs[:n_out], refs[n_out:]
        e_out, refs = refs[:r_out], refs[r_out:]
        k_scr, e_sems = refs[:n_scratch], refs[n_scratch:]
        begin, mid, end = ex.stages(e_in, e_out, e_sems)

        @pl.when(first())
        def _():
            ex.open()
            begin()

        body(*k_in, *k_out, *k_scr)
        for stage, at in zip(_as_tuple(mid), (middle, late or last)):
            pl.when(at())(stage)
        pl.when(last())(end)

    return wrapped


def _ride_specs(ex):
    if ex is None:
        return [], [], [], [], []
    return [_any_spec()] * len(ex.arrays), [_any_spec()] * len(ex.out_shape), ex.out_shape, ex.sems, ex.arrays


def _adamw(w, g, m, v):
    m = ADAM_B1 * m + (1.0 - ADAM_B1) * g
    v = ADAM_B2 * v + (1.0 - ADAM_B2) * (g * g)
    m_hat = m / (1.0 - ADAM_B1 ** ADAM_STEP)
    v_hat = v / (1.0 - ADAM_B2 ** ADAM_STEP)
    delta = -ADAM_LR * (m_hat / (jnp.sqrt(v_hat) + ADAM_EPS) + ADAM_WD * w)
    return delta, m, v


def sum_adamw(pairs, others, w, m, v, name):
    r, cdim = w.shape
    tr = min(r, ELEMENTWISE_ROWS // 2)
    chip_idx =(2 * lax.axis_index("x") + lax.axis_index("y")).astype(jnp.int32).reshape(1)

    def body(q_ref, p_ref, o_ref, w_ref, m_ref, v_ref, g_out, d_out, m_out, v_out):
        g = p_ref[0] + o_ref[0].astype(F32) + o_ref[1].astype(F32) + o_ref[2].astype(F32)
        g_out[...] = g
        d_out[...], m_out[...], v_out[...] = _adamw(w_ref[...], g, m_ref[...], v_ref[...])

    tile = lambda: pl.BlockSpec((tr, cdim), lambda i, qr: (i, 0))
    return pl.pallas_call(
        body,
        name=name,
        grid_spec=pltpu.PrefetchScalarGridSpec(
            num_scalar_prefetch=1,
            grid=(r // tr,),
            in_specs=[pl.BlockSpec((1, tr, cdim), lambda i, qr: (qr[0], i, 0)),
                      pl.BlockSpec((3, tr, cdim), lambda i, qr: (0, i, 0)), tile(), tile(), tile()],
            out_specs=[tile(), tile(), tile(), tile()],
        ),
        out_shape=[jax.ShapeDtypeStruct((r, cdim), F32)] * 4,
        compiler_params=_params(dimension_semantics=("arbitrary",)),
    )(chip_idx, pairs, others, w, m, v)


def small_exchange(packed):
    def stages(ins, outs, sems):
        send_sems, recv_sems, local_sem = sems
        (src,), (slots,) = ins, outs

        def copies():
            x, y, c, _ = _position()
            my_id = 4 * x + 2 * y + c
            sends, landings = [], []
            for rel in range(1, N_DEV):
                px = 1 - x if (rel >> 2) & 1 else x
                py = 1 - y if (rel >> 1) & 1 else y
                pc = 1 - c if rel & 1 else c
                peer = dict(send_sem=send_sems.at[rel - 1], recv_sem=recv_sems.at[rel - 1], device_id=(px, py, pc),
                            device_id_type=MESH)
                sends.append(pltpu.make_async_remote_copy(src_ref=src, dst_ref=slots.at[my_id], **peer))
                landings.append(pltpu.make_async_remote_copy(src_ref=src, dst_ref=slots.at[4 * px + 2 * py + pc], **peer))
            return pltpu.make_async_copy(src, slots.at[my_id], local_sem), sends, landings

        def begin():
            local, sends, _ = copies()
            local.start()
            for cp in sends:
                cp.start()

        def end():
            local, sends, landings = copies()
            for cp in landings:
                cp.wait_recv()
            for cp in sends:
                cp.wait_send()
            local.wait()

        return begin, None, end

    return Exchange([packed], [jax.ShapeDtypeStruct((N_DEV,) + packed.shape, packed.dtype)],
                    [pltpu.SemaphoreType.DMA((N_DEV - 1,)), pltpu.SemaphoreType.DMA((N_DEV - 1,)),
                     pltpu.SemaphoreType.DMA(())], stages, collective_id=3, peers=_all_others)


def small_adamw(slots, w, m, v):
    def body(r_ref, w_ref, m_ref, v_ref, g_out, d_out, m_out, v_out, loss_out):
        red = r_ref[0]
        for k in range(1, N_DEV):
            red = red + r_ref[k]
        wv = w_ref[...]
        lb = _lower_bound(jnp.concatenate([wv[5:6, :HGRN_WIDTH], wv[5:6, HGRN_WIDTH:]], axis=0))
        t = red[5:6, :HGRN_WIDTH] * lb * (1.0 - lb)
        row = lax.broadcasted_iota(jnp.int32, red.shape, 0)
        g = jnp.where(row == 5, jnp.concatenate([t, -t], axis=1), jnp.where(row >= 6, 0.0, red))
        g_out[...] = g
        d_out[...], m_out[...], v_out[...] = _adamw(wv, g, m_ref[...], v_ref[...])
        loss = jnp.sum(red[6:7, :], axis=-1, keepdims=True) * (0.5 / D_MODEL)
        loss_out[...] = jnp.broadcast_to(loss, loss_out.shape)

    return pl.pallas_call(
        body,
        name="small_adamw",
        in_specs=[_vmem_spec()] * 4,
        out_specs=[_vmem_spec()] * 5,
        out_shape=[jax.ShapeDtypeStruct(w.shape, F32)] * 4 + [jax.ShapeDtypeStruct((SUBLANES, LANES), F32)],
    )(slots, w, m, v)


def _pack_small(g1, gp, g_pre, g_post, an, hn, logits_or_dlb, extra=None):
    row5 = logits_or_dlb.reshape(1, -1)
    row5 = jnp.pad(row5, ((0, 0), (0, D_MODEL - row5.shape[1])))
    row6 = jnp.zeros((1, D_MODEL), F32) if extra is None else extra
    return jnp.concatenate([g1, gp, g_pre, g_post, jnp.concatenate([an, hn], axis=1), row5, row6,
                            jnp.zeros((1, D_MODEL), F32)], axis=0)


def _unpack_small(p):
    return dict(mix_pre_norm=p[0:1], mix_post_norm=p[1:2], mlp_pre_norm=p[2:3], mlp_post_norm=p[3:4],
                attn_out_norm=p[4:5, :ATTN_WIDTH], hgrn_out_norm=p[4:5, ATTN_WIDTH:],
                hgrn_lb_logits=p[5].reshape(2, HGRN_WIDTH))


BIG = ("w_in", "w_out", "w_ff1", "w_ff2")
ORDER = ("mix_pre_norm", "w_in", "attn_out_norm", "hgrn_lb_logits", "hgrn_out_norm", "w_out", "mix_post_norm",
         "mlp_pre_norm", "w_ff1", "w_ff2", "mlp_post_norm")


def kernel(x, mix_pre_norm, w_in, attn_out_norm, hgrn_lb_logits, hgrn_out_norm, w_out, mix_post_norm, mlp_pre_norm, w_ff1, w_ff2, mlp_post_norm, loss_target, m_mix_pre_norm, m_w_in, m_attn_out_norm, m_hgrn_lb_logits, m_hgrn_out_norm, m_w_out, m_mix_post_norm, m_mlp_pre_norm, m_w_ff1, m_w_ff2, m_mlp_post_norm, v_mix_pre_norm, v_w_in, v_attn_out_norm, v_hgrn_lb_logits, v_hgrn_out_norm, v_w_out, v_mix_post_norm, v_mlp_pre_norm, v_w_ff1, v_w_ff2, v_mlp_post_norm):
    w = dict(w_in=w_in[0], w_out=w_out[0], w_ff1=w_ff1[0], w_ff2=w_ff2[0])
    m = dict(w_in=m_w_in[0], w_out=m_w_out[0], w_ff1=m_w_ff1[0], w_ff2=m_w_ff2[0])
    v = dict(w_in=v_w_in[0], w_out=v_w_out[0], w_ff1=v_w_ff1[0], w_ff2=v_w_ff2[0])

    dx, big, small_slots = train_step(x[0], loss_target[0], mix_pre_norm, attn_out_norm, hgrn_lb_logits, hgrn_out_norm,
                                      mix_post_norm, mlp_pre_norm, mlp_post_norm, w, m, v)

    pack = lambda a, b, c2, d, e, f, g: _pack_small(a, b, c2, d, e, f, g)
    w_s = pack(mix_pre_norm, mix_post_norm, mlp_pre_norm, mlp_post_norm, attn_out_norm, hgrn_out_norm, hgrn_lb_logits)
    m_s = pack(m_mix_pre_norm, m_mix_post_norm, m_mlp_pre_norm, m_mlp_post_norm, m_attn_out_norm, m_hgrn_out_norm,
               m_hgrn_lb_logits)
    v_s = pack(v_mix_pre_norm, v_mix_post_norm, v_mlp_pre_norm, v_mlp_post_norm, v_attn_out_norm, v_hgrn_out_norm,
               v_hgrn_lb_logits)
    g_s, d_s, nm_s, nv_s, loss = small_adamw(small_slots, w_s, m_s, v_s)
    small_out = [_unpack_small(t) for t in (g_s, d_s, nm_s, nv_s)]

    outs = [loss[0, 0], dx[None]]
    for kind in range(4):
        for name in ORDER:
            outs.append(big[name][kind][None] if name in BIG else small_out[kind][name])
    return tuple(outs)
```

```python
import jax
import jax.numpy as jnp
from jax import lax
from jax.experimental import pallas as pl
from jax.experimental.pallas import tpu as pltpu

F32 = jnp.float32
BF16 = jnp.bfloat16

D_MODEL = 1024
ATTN_WIDTH = 512
ATTN_HEAD_DIM = 64
ATTN_HEADS = 8
ATTN_BLOCK = 128
DILATIONS = (1, 4, 16)
HGRN_WIDTH = 512
HGRN_HEADS = 4
HGRN_HEAD_DIM = 128
HGRN_CHUNK = 64
IN_PROJ_WIDTH = 3584
D_FF = 4096
RMS_EPS = 1e-6
N_DEV = 8
ADAM_LR = 0.001
ADAM_B1 = 0.9
ADAM_B2 = 0.999
ADAM_EPS = 1e-08
ADAM_WD = 0.01
ADAM_STEP = 10

SUBLANES = 8
LANES = 128
COLUMN_UNROLL = 8
HGRN_CHUNKS_PER_STEP = 2
SUB_BLOCK = 16
TOKEN_TILE = 512
ELEMENTWISE_ROWS = 1024
MLP_TILE = 256
PROJ_TILE = 512
VMEM_BYTES_V7X = 64 * 1024 * 1024
VMEM_LIMIT = VMEM_BYTES_V7X // 8 * 7
NEG_BIG = -1e30
MESH = pl.DeviceIdType.MESH


def _params(ride=None, **kw):
    if ride is not None:
        kw["collective_id"] = ride.collective_id
    return pltpu.CompilerParams(vmem_limit_bytes=VMEM_LIMIT, **kw)


def _vmem_spec():
    return pl.BlockSpec(memory_space=pltpu.VMEM)


def _dot(a, b):
    return jnp.dot(a, b, preferred_element_type=F32)


def _dot_nt(a, b):
    return lax.dot_general(a, b, (((1,), (1,)), ((), ())), preferred_element_type=F32)


def _dot_tn(a, b):
    return lax.dot_general(a, b, (((0,), (0,)), ((), ())), preferred_element_type=F32)


def _sigmoid(x):
    return 1.0 / (1.0 + jnp.exp(-x))


def _rms_fwd(x, gain, width):
    r = lax.rsqrt(jnp.sum(x * x, axis=-1, keepdims=True) * (1.0 / width) + RMS_EPS)
    return x * r * gain


def _rms_bwd(dy, x, gain, width):
    r = lax.rsqrt(jnp.sum(x * x, axis=-1, keepdims=True) * (1.0 / width) + RMS_EPS)
    xhat = x * r
    dxhat = dy * gain
    dx = r * (dxhat - xhat * (jnp.sum(dxhat * xhat, axis=-1, keepdims=True) * (1.0 / width)))
    return dx, dy * xhat


def _split3(x):
    hi = x.astype(BF16)
    r1 = x - hi.astype(F32)
    mid = r1.astype(BF16)
    lo = (r1 - mid.astype(F32)).astype(BF16)
    return hi, mid, lo


def _tri_sum(tri_bf16, x):
    hi, mid, lo = _split3(x)
    return _dot(tri_bf16, hi) + _dot(tri_bf16, mid) + _dot(tri_bf16, lo)


def _dilated_spec(d, tm, width):
    return pl.BlockSpec((d, tm // d, width), lambda i: (0, i, 0))


def _lane_blocks(ref, value):
    for c in range(ref.shape[0]):
        ref[c] = value[:, c * LANES:(c + 1) * LANES]


def _to_dilated(src_ref, dst_ref, d, tm, cast=None):
    for r in range(d):
        for c in range(src_ref.shape[0]):
            v = src_ref[c] if d == 1 else src_ref[c, pl.ds(r, tm // d, stride=d), :]
            dst_ref[r, :, c * LANES:(c + 1) * LANES] = v if cast is None else v.astype(cast)


def _from_dilated(src_ref, scratch_ref, d, tm):
    if d == 1:
        return src_ref[0].astype(F32)
    nblk = scratch_ref.shape[0]
    for r in range(d):
        for c in range(nblk):
            scratch_ref[c, pl.ds(r, tm // d, stride=d), :] = src_ref[r, :, c * LANES:(c + 1) * LANES].astype(F32)
    return jnp.concatenate([scratch_ref[c] for c in range(nblk)], axis=1)


def in_proj_fwd(x, g1, w_in_b, ride=None):
    s = x.shape[0]
    tm = PROJ_TILE
    qkv_w = 3 * ATTN_WIDTH
    hg_w = IN_PROJ_WIDTH - qkv_w

    def body(x_ref, g_ref, w_ref, hg_ref, h_ref, *rest):
        qkv_refs, qkv_scr = rest[:len(DILATIONS)], rest[len(DILATIONS)]
        h = _rms_fwd(x_ref[...], g_ref[...], D_MODEL).astype(BF16)
        h_ref[...] = h
        proj = _dot(h, w_ref[...])
        hg_ref[...] = proj[:, qkv_w:]
        _lane_blocks(qkv_scr, proj[:, :qkv_w])
        for d, ref in zip(DILATIONS, qkv_refs):
            _to_dilated(qkv_scr, ref, d, tm, cast=BF16)

    n_steps = s // tm
    step = lambda k: (lambda: pl.program_id(0) == k)
    e_in, e_out, e_shape, e_scr, e_args = _ride_specs(ride)
    return pl.pallas_call(
        _riding(body, 3, 2 + len(DILATIONS), 1, ride, step(0), step(n_steps // 2), step(n_steps - 1),
                late=step(n_steps - 2)),
        name="in_proj_fwd",
        grid=(n_steps,),
        in_specs=[
            pl.BlockSpec((tm, D_MODEL), lambda i: (i, 0)),
            pl.BlockSpec((1, D_MODEL), lambda i: (0, 0)),
            _vmem_spec(),
        ] + e_in,
        out_specs=[
            pl.BlockSpec((tm, hg_w), lambda i: (i, 0)),
            pl.BlockSpec((tm, D_MODEL), lambda i: (i, 0)),
        ] + [_dilated_spec(d, tm, qkv_w) for d in DILATIONS] + e_out,
        out_shape=[jax.ShapeDtypeStruct((s, hg_w), F32), jax.ShapeDtypeStruct((s, D_MODEL), BF16)] + [
            jax.ShapeDtypeStruct((d, s // d, qkv_w), BF16) for d in DILATIONS] + e_shape,
        scratch_shapes=[pltpu.VMEM((qkv_w // LANES, tm, LANES), F32)] + e_scr,
        compiler_params=_params(ride, dimension_semantics=("arbitrary",)),
    )(x, g1, w_in_b, *e_args)


ATTN_SCALE = ATTN_HEAD_DIM ** -0.5


def _fill_attn_bias(bias_ref, dilation):
    qi = lax.broadcasted_iota(jnp.int32, (ATTN_BLOCK, 2 * ATTN_BLOCK), 0)
    kj = lax.broadcasted_iota(jnp.int32, (ATTN_BLOCK, 2 * ATTN_BLOCK), 1)
    dist = qi + ATTN_BLOCK - kj
    valid = (dist >= 0) & (dist <= ATTN_BLOCK)
    for head in range(ATTN_HEADS):
        slope = 2.0 ** (-8.0 * (head + 1) / ATTN_HEADS)
        bias = jnp.where(valid, dist.astype(F32) * (-slope * dilation), NEG_BIG)
        bias_ref[0, head] = bias
        bias_ref[1, head] = jnp.where(kj >= ATTN_BLOCK, bias, NEG_BIG)


def _stack_heads(x):
    low = _lane_half(x.shape, 0)
    zero = jnp.zeros_like(x)
    return jnp.concatenate([jnp.where(low, x, zero), jnp.where(low, zero, x)], axis=0)


def _unstack_heads(y):
    half = y.shape[0] // 2
    return jnp.where(_lane_half((half, y.shape[1]), 0), y[:half], y[half:])


def _attn_scores(q_stack, kcat, bias_ref, pair, first_block):
    f = first_block.astype(jnp.int32)
    bias = jnp.concatenate([bias_ref[f, 2 * pair], bias_ref[f, 2 * pair + 1]], axis=0)
    return _dot_nt(q_stack, kcat) + bias


def _lane_half(shape, sub):
    lane = lax.broadcasted_iota(jnp.int32, shape, 1)
    return (lane < ATTN_HEAD_DIM) if sub == 0 else (lane >= ATTN_HEAD_DIM)


def _sub_block(col, row):
    return pl.BlockSpec((None, ATTN_BLOCK, ATTN_WIDTH), lambda r, n: (r, row(n), col))


def attn_fwd(qkv, dilation):
    d, length, _ = qkv.shape
    assert d == dilation
    nb = length // ATTN_BLOCK

    def body(q_ref, kc_ref, kp_ref, vc_ref, vp_ref, o_ref, lse_ref, bias_ref):
        @pl.when((pl.program_id(0) == 0) & (pl.program_id(1) == 0))
        def _():
            _fill_attn_bias(bias_ref, d)

        first = pl.program_id(1) == 0
        for pair in range(ATTN_HEADS // 2):
            lanes = slice(pair * LANES, (pair + 1) * LANES)
            q_stack = _stack_heads(q_ref[:, lanes] * ATTN_SCALE)
            kcat = jnp.concatenate([kp_ref[:, lanes], kc_ref[:, lanes]], axis=0)
            vcat = jnp.concatenate([vp_ref[:, lanes], vc_ref[:, lanes]], axis=0)
            sc = _attn_scores(q_stack, kcat, bias_ref, pair, first)
            m = jnp.max(sc, axis=-1, keepdims=True)
            p = jnp.exp(sc - m)
            den = jnp.sum(p, axis=-1, keepdims=True)
            o_ref[:, lanes] = _unstack_heads(_dot(p.astype(BF16), vcat) / den).astype(BF16)
            lse_ref[:, lanes] = _unstack_heads(jnp.broadcast_to(m + jnp.log(den), (2 * ATTN_BLOCK, LANES)))

    cur = lambda n: n
    prev = lambda n: jnp.maximum(n - 1, 0)
    return pl.pallas_call(
        body,
        name=f"attn_fwd_d{d}",
        grid=(d, nb),
        in_specs=[_sub_block(0, cur), _sub_block(1, cur), _sub_block(1, prev), _sub_block(2, cur), _sub_block(2, prev)],
        out_specs=[_sub_block(0, cur), _sub_block(0, cur)],
        out_shape=[jax.ShapeDtypeStruct((d, length, ATTN_WIDTH), BF16), jax.ShapeDtypeStruct((d, length, ATTN_WIDTH), F32)],
        scratch_shapes=[pltpu.VMEM((2, ATTN_HEADS, ATTN_BLOCK, 2 * ATTN_BLOCK), F32)],
        compiler_params=_params(dimension_semantics=("arbitrary", "arbitrary")),
    )(qkv, qkv, qkv, qkv, qkv)


def attn_bwd(qkv, d_out, lse, delta, dilation, ride=None):
    d, length, _ = qkv.shape
    assert d == dilation
    nb = length // ATTN_BLOCK

    steps = d * nb + 1

    def body(q_ref, kc_ref, kp_ref, vc_ref, vp_ref, do_ref, lse_ref, dl_ref, dq_ref, dk_ref, dv_ref, ck_ref, cv_ref,
             bias_ref):
        t = pl.program_id(0)

        @pl.when(t == 0)
        def _():
            ck_ref[...] = jnp.zeros_like(ck_ref)
            cv_ref[...] = jnp.zeros_like(cv_ref)
            _fill_attn_bias(bias_ref, d)

        @pl.when(t < steps - 1)
        def _():
            first = t % nb == 0
            for pair in range(ATTN_HEADS // 2):
                lanes = slice(pair * LANES, (pair + 1) * LANES)
                q_stack = _stack_heads(q_ref[:, lanes] * ATTN_SCALE)
                do_stack = _stack_heads(do_ref[:, lanes])
                kcat = jnp.concatenate([kp_ref[:, lanes], kc_ref[:, lanes]], axis=0)
                vcat = jnp.concatenate([vp_ref[:, lanes], vc_ref[:, lanes]], axis=0)
                col_a, col_b = 2 * pair, 2 * pair + 1
                lse_col = jnp.concatenate([lse_ref[:, col_a:col_a + 1], lse_ref[:, col_b:col_b + 1]], axis=0)
                dl_col = jnp.concatenate([dl_ref[:, col_a:col_a + 1], dl_ref[:, col_b:col_b + 1]], axis=0)
                p = jnp.exp(_attn_scores(q_stack, kcat, bias_ref, pair, first) - lse_col)
                ds = (p * (_dot_nt(do_stack, vcat) - dl_col)).astype(BF16)
                dq_ref[:, lanes] = (_unstack_heads(_dot(ds, kcat)) * ATTN_SCALE).astype(BF16)
                dk_cat = _dot_tn(ds, q_stack)
                dv_cat = _dot_tn(p.astype(BF16), do_stack)
                dk_ref[:, lanes] = (ck_ref[:, lanes] + dk_cat[:ATTN_BLOCK]).astype(BF16)
                dv_ref[:, lanes] = (cv_ref[:, lanes] + dv_cat[:ATTN_BLOCK]).astype(BF16)
                ck_ref[:, lanes] = dk_cat[ATTN_BLOCK:]
                cv_ref[:, lanes] = dv_cat[ATTN_BLOCK:]

        @pl.when(t == steps - 1)
        def _():
            dk_ref[...] = ck_ref[...].astype(BF16)
            dv_ref[...] = cv_ref[...].astype(BF16)

    blk = (ATTN_BLOCK, ATTN_WIDTH)

    def spec(col, shift, width=ATTN_WIDTH):
        def index(t):
            f = jnp.minimum(t, steps - 2) if shift > -2 else jnp.maximum(t - 1, 0)
            r, n = f // nb, f % nb
            return (r, jnp.maximum(n - 1, 0) if shift == -1 else n, col)
        return pl.BlockSpec((None, ATTN_BLOCK, width), index)

    step = lambda k: (lambda: pl.program_id(0) == k)
    e_in, e_out, e_shape, e_scr, e_args = _ride_specs(ride)
    return pl.pallas_call(
        _riding(body, 8, 3, 3, ride, step(0), step(steps // 2), step(steps - 1)),
        name=f"attn_bwd_d{d}",
        grid=(steps,),
        in_specs=[spec(0, 0), spec(1, 0), spec(1, -1), spec(2, 0), spec(2, -1), spec(0, 0), spec(0, 0, LANES),
                  spec(0, 0, LANES)] + e_in,
        out_specs=[spec(0, 0), spec(0, -2), spec(0, -2)] + e_out,
        out_shape=[jax.ShapeDtypeStruct((d, length, ATTN_WIDTH), BF16)] * 3 + e_shape,
        scratch_shapes=[pltpu.VMEM(blk, F32), pltpu.VMEM(blk, F32),
                        pltpu.VMEM((2, ATTN_HEADS, ATTN_BLOCK, 2 * ATTN_BLOCK), F32)] + e_scr,
        compiler_params=_params(ride, dimension_semantics=("arbitrary",)),
    )(qkv, qkv, qkv, qkv, qkv, d_out, lse, delta, *e_args)


def _lower_bound(logits):
    return _sigmoid(logits[0:1, :] - logits[1:2, :])


def _hgrn_gates(q, fp, lb):
    sq = _sigmoid(q)
    qf = q * sq
    sig = _sigmoid(fp)
    sig_neg = _sigmoid(-fp)
    kf = (1.0 - lb) * sig_neg
    log_sig = jnp.minimum(fp, 0.0) - jnp.log(1.0 + jnp.exp(-jnp.abs(fp)))
    a = jnp.log(lb)
    c = jnp.log(1.0 - lb) + log_sig
    log_f = jnp.maximum(a, c) + jnp.log(1.0 + jnp.exp(-jnp.abs(a - c)))
    return sq, qf, (sig, sig_neg, c), log_f, kf


def _tril_bf16(n, upper=False):
    r = lax.broadcasted_iota(jnp.int32, (n, n), 0)
    c = lax.broadcasted_iota(jnp.int32, (n, n), 1)
    keep = (c >= r) if upper else (c <= r)
    return jnp.where(keep, 1.0, 0.0).astype(BF16)


def _hgrn_diagonal_loops(c_len, diagonal):
    for half in range(SUB_BLOCK // SUBLANES):
        def step(jj, carry, half=half):
            j = half * SUBLANES + jj
            for i in range(c_len // SUB_BLOCK):
                diagonal(slice(i * SUB_BLOCK + half * SUBLANES, (i + 1) * SUB_BLOCK), j, i * SUB_BLOCK + j)
            return carry

        lax.fori_loop(0, SUBLANES, step, 0, unroll=COLUMN_UNROLL)


def _hgrn_off_diagonal(b, qf, kf):
    c_len, width = b.shape
    edges = [b[0:1, :]] + [b[i * SUB_BLOCK - 1:i * SUB_BLOCK, :] for i in range(1, c_len // SUB_BLOCK)]
    eq = jnp.exp(b - jnp.concatenate([jnp.broadcast_to(e, (SUB_BLOCK, width)) for e in edges], axis=0))
    q_til = qf * eq
    k_til, ek = [], []
    for i in range(1, c_len // SUB_BLOCK):
        n = i * SUB_BLOCK
        e = jnp.exp(edges[i] - b[:n, :])
        ek.append(e)
        k_til.append(jnp.concatenate([kf[:n, :] * e, jnp.zeros((2 * c_len - n, width), F32)], axis=0))
    return q_til, k_til, eq, ek


def _split2(x):
    hi = x.astype(BF16)
    return hi, (x - hi.astype(F32)).astype(BF16)


def hgrn_fwd(proj, lb, ride=None):
    s = proj.shape[0]
    c_len, nh, hd = HGRN_CHUNK, HGRN_HEADS, HGRN_HEAD_DIM
    n_chunks = s // c_len
    col0 = 0

    cps = 2 * HGRN_CHUNKS_PER_STEP
    n_steps = n_chunks // cps

    def body(q_ref, f_ref, i_ref, lb_ref, o_ref, st_out_ref, a_out_ref, st_ref, b_ref, qf_ref, kf_ref, a_ref):
        @pl.when(pl.program_id(0) == 0)
        def _():
            st_ref[...] = jnp.zeros_like(st_ref)

        lbv = _lower_bound(lb_ref[...])
        for u in range(cps):
            rs = slice(u * c_len, (u + 1) * c_len)
            b_u, qf_u, kf_u, a_u = b_ref.at[u], qf_ref.at[u], kf_ref.at[u], a_ref.at[u]
            _, qf, _, log_f, kf = _hgrn_gates(q_ref[rs, :], f_ref[rs, :], lbv)
            b = _tri_sum(_tril_bf16(c_len), log_f)
            b_u[...] = b
            qf_u[...] = qf
            kf_u[...] = kf
            a_u[...] = jnp.zeros_like(a_u)

            def diagonal(rows, j, key, b_u=b_u, qf_u=qf_u, kf_u=kf_u, a_u=a_u):
                bj = b_u[pl.ds(key, 1), :]
                kj = kf_u[pl.ds(key, 1), :]
                nrow = rows.stop - rows.start
                t_loc = lax.broadcasted_iota(jnp.int32, (nrow, nh * hd), 0) + (rows.start % SUB_BLOCK)
                e = jnp.exp(jnp.where(t_loc >= j, b_u[rows, :] - bj, NEG_BIG))
                prod = qf_u[rows, :] * kj * e
                lane = lax.broadcasted_iota(jnp.int32, (nrow, hd), 1)
                for h in range(nh):
                    col = jnp.sum(prod[:, h * hd:(h + 1) * hd], axis=-1, keepdims=True)
                    a_u[h, rows, :] = jnp.where(lane == key, col, a_u[h, rows, :])

            _hgrn_diagonal_loops(c_len, diagonal)
            q_til, k_til, _, _ = _hgrn_off_diagonal(b, qf, kf)
            q_til = q_til.astype(BF16)
            k_til = [k.astype(BF16) for k in k_til]

            b_last = b[c_len - 1:c_len, :]
            qb = (qf * jnp.exp(b)).astype(BF16)
            kb2 = (kf * jnp.exp(b_last - b)).astype(BF16)
            vf = i_ref[rs, :].astype(BF16)
            for h in range(nh):
                hs = slice(h * hd, (h + 1) * hd)
                st = st_ref[h]
                st_out_ref[u, h] = st
                off = [jnp.zeros((SUB_BLOCK, hd), F32)]
                for i in range(1, c_len // SUB_BLOCK):
                    off.append(_dot_nt(q_til[i * SUB_BLOCK:(i + 1) * SUB_BLOCK, hs], k_til[i - 1][:, hs]))
                a_h = a_u[h] + jnp.concatenate(off, axis=0)
                a_out_ref[rs, hs] = a_h
                o_ref[rs, hs] = _dot_nt(qb[:, hs], st.astype(BF16)) + _dot(a_h[:, :c_len].astype(BF16), vf[:, hs])
                st_ref[h] = st * jnp.exp(b_last[:, hs]) + _dot_tn(vf[:, hs], kb2[:, hs])

    blk = (cps * c_len, HGRN_WIDTH)
    sblk = (cps, c_len, HGRN_WIDTH)
    step = lambda k: (lambda: pl.program_id(0) == k)
    e_in, e_out, e_shape, e_scr, e_args = _ride_specs(ride)
    return pl.pallas_call(
        _riding(body, 4, 3, 5, ride, step(0), step(n_steps // 2), step(n_steps - 1), late=step((3 * n_steps) // 4)),
        name="hgrn_fwd",
        grid=(n_steps,),
        in_specs=[
            pl.BlockSpec(blk, lambda c: (c, col0)),
            pl.BlockSpec(blk, lambda c: (c, col0 + 1)),
            pl.BlockSpec(blk, lambda c: (c, col0 + 2)),
            pl.BlockSpec((2, HGRN_WIDTH), lambda c: (0, 0)),
        ] + e_in,
        out_specs=[
            pl.BlockSpec(blk, lambda c: (c, 0)),
            pl.BlockSpec((cps, nh, hd, hd), lambda c: (c, 0, 0, 0)),
            pl.BlockSpec(blk, lambda c: (c, 0)),
        ] + e_out,
        out_shape=[
            jax.ShapeDtypeStruct((s, HGRN_WIDTH), F32),
            jax.ShapeDtypeStruct((n_chunks, nh, hd, hd), F32),
            jax.ShapeDtypeStruct((s, nh * hd), F32),
        ] + e_shape,
        scratch_shapes=[
            pltpu.VMEM((nh, hd, hd), F32),
            pltpu.VMEM(sblk, F32),
            pltpu.VMEM(sblk, F32),
            pltpu.VMEM(sblk, F32),
            pltpu.VMEM((cps, nh, c_len, hd), F32),
        ] + e_scr,
        compiler_params=_params(ride, dimension_semantics=("arbitrary",)),
    )(proj, proj, proj, lb, *e_args)


def hgrn_bwd(proj, lb, d_o, states, a_mat, ride=None):
    s = proj.shape[0]
    c_len, nh, hd = HGRN_CHUNK, HGRN_HEADS, HGRN_HEAD_DIM
    n_chunks = s // c_len
    col0 = 0
    cps = HGRN_CHUNKS_PER_STEP
    n_steps = n_chunks // cps
    last = n_steps - 1

    def body(q_ref, f_ref, i_ref, lb_ref, do_ref, st_in_ref, a_in_ref, dq_ref, df_ref, di_ref, dlb_ref,
             dst_ref, b_ref, qf_ref, kf_ref, da_ref, dqi_ref, dki_ref):
        @pl.when(pl.program_id(0) == 0)
        def _():
            dst_ref[...] = jnp.zeros_like(dst_ref)
            dlb_ref[...] = jnp.zeros_like(dlb_ref)

        lbv = _lower_bound(lb_ref[...])
        for u in reversed(range(cps)):
            rs = slice(u * c_len, (u + 1) * c_len)
            b_u, qf_u, kf_u, da_u, dqi_u, dki_u = (b_ref.at[u], qf_ref.at[u], kf_ref.at[u], da_ref.at[u], dqi_ref.at[u],
                                                   dki_ref.at[u])
            q = q_ref[rs, :]
            sq, qf, (sig, sig_neg, log_c), log_f, kf = _hgrn_gates(q, f_ref[rs, :], lbv)
            b = _tri_sum(_tril_bf16(c_len), log_f)
            b_u[...] = b
            qf_u[...] = qf
            kf_u[...] = kf
            b_last = b[c_len - 1:c_len, :]
            eb = jnp.exp(b)
            ebl = jnp.exp(b_last - b)
            qb = qf * eb
            kb2 = kf * ebl
            vf = i_ref[rs, :]
            d_o = do_ref[rs, :]
            qb_b, kb2_b, vf_b, do_b = qb.astype(BF16), kb2.astype(BF16), vf.astype(BF16), d_o.astype(BF16)
            tq = lax.broadcasted_iota(jnp.int32, (c_len, hd), 0)
            lane = lax.broadcasted_iota(jnp.int32, (c_len, hd), 1)

            dqb_parts, dvf_parts, dkb2_parts, dbl_parts = [], [], [], []
            for h in range(nh):
                hs = slice(h * hd, (h + 1) * hd)
                st = st_in_ref[u, h]
                dst = dst_ref[h]
                st_b, dst_b = st.astype(BF16), dst.astype(BF16)
                a_h = a_in_ref[rs, hs][:, :c_len].astype(BF16)
                dqb_parts.append(_dot(do_b[:, hs], st_b))
                dvf_parts.append(_dot_tn(a_h, do_b[:, hs]) + _dot_nt(kb2_b[:, hs], dst_b))
                dkb2_parts.append(_dot(vf_b[:, hs], dst_b))
                da = _dot_nt(do_b[:, hs], vf_b[:, hs])
                da = jnp.concatenate([da, jnp.zeros((c_len, hd - c_len), F32)], axis=1)
                da_u[h] = jnp.where(tq >= lane, da, 0.0)
                dbl_parts.append(jnp.sum(dst * st, axis=0, keepdims=True) * jnp.exp(b_last[:, hs]))
                dst_ref[h] = dst * jnp.exp(b_last[:, hs]) + _dot_tn(do_b[:, hs], qb_b[:, hs])
            dqb = jnp.concatenate(dqb_parts, axis=1)
            dvf = jnp.concatenate(dvf_parts, axis=1)
            dkb2 = jnp.concatenate(dkb2_parts, axis=1)
            dbl = jnp.concatenate(dbl_parts, axis=1) + jnp.sum(dkb2 * kb2, axis=0, keepdims=True)

            dqi_u[...] = jnp.zeros_like(dqi_u)
            t_idx = lax.broadcasted_iota(jnp.int32, (c_len, nh * hd), 0)

            def diagonal(rows, j, key, b_u=b_u, qf_u=qf_u, kf_u=kf_u, da_u=da_u, dqi_u=dqi_u, dki_u=dki_u):
                bj = b_u[pl.ds(key, 1), :]
                kj = kf_u[pl.ds(key, 1), :]
                nrow = rows.stop - rows.start
                t_loc = lax.broadcasted_iota(jnp.int32, (nrow, nh * hd), 0) + (rows.start % SUB_BLOCK)
                e = jnp.exp(jnp.where(t_loc >= j, b_u[rows, :] - bj, NEG_BIG))
                lane_r = lax.broadcasted_iota(jnp.int32, (nrow, hd), 1)
                cols = [jnp.sum(jnp.where(lane_r == key, da_u[h, rows, :], 0.0), axis=-1, keepdims=True)
                        for h in range(nh)]
                w = e * jnp.concatenate([jnp.broadcast_to(cc, (nrow, hd)) for cc in cols], axis=1)
                dqi_u[rows, :] += w * kj
                dki_u[pl.ds(key, 1), :] = jnp.sum(w * qf_u[rows, :], axis=0, keepdims=True)

            _hgrn_diagonal_loops(c_len, diagonal)

            q_til, k_til, eq, ek = _hgrn_off_diagonal(b, qf, kf)
            q_hi, q_lo = _split2(q_til)
            k_pairs = [_split2(k) for k in k_til]
            n_sub = c_len // SUB_BLOCK
            dq_heads, dk_heads = [], []
            for h in range(nh):
                hs = slice(h * hd, (h + 1) * hd)
                dq_rows = [jnp.zeros((SUB_BLOCK, hd), F32)]
                dk_h = jnp.zeros((c_len, hd), F32)
                for i in range(1, n_sub):
                    rows = slice(i * SUB_BLOCK, (i + 1) * SUB_BLOCK)
                    n = i * SUB_BLOCK
                    da_i = da_u[h, rows, :].astype(BF16)
                    k_hi, k_lo = k_pairs[i - 1]
                    dq_rows.append((_dot(da_i, k_hi[:, hs]) + _dot(da_i, k_lo[:, hs])) * eq[rows, hs])
                    dk_t = (_dot_tn(da_i, q_hi[rows, hs]) + _dot_tn(da_i, q_lo[rows, hs]))[:n, :] * ek[i - 1][:, hs]
                    dk_h = dk_h + jnp.concatenate([dk_t, jnp.zeros((c_len - n, hd), F32)], axis=0)
                dq_heads.append(jnp.concatenate(dq_rows, axis=0))
                dk_heads.append(dk_h)
            dq_intra = dqi_u[...] + jnp.concatenate(dq_heads, axis=1)
            dk_intra = dki_u[...] + jnp.concatenate(dk_heads, axis=1)

            db = dqb * qb + qf * dq_intra - kf * dk_intra - dkb2 * kb2
            db = db + jnp.where(t_idx == c_len - 1, dbl, 0.0)
            dg = _tri_sum(_tril_bf16(c_len, upper=True), db)
            dqf = dqb * eb + dq_intra
            dkf = dkb2 * ebl + dk_intra
            dq_ref[rs, :] = (dqf * (sq * (1.0 + q * (1.0 - sq)))).astype(BF16)
            df_ref[rs, :] = (sig_neg * (dg * jnp.exp(log_c - log_f) - dkf * (1.0 - lbv) * sig)).astype(BF16)
            di_ref[rs, :] = dvf.astype(BF16)
            dlb_ref[...] += jnp.sum(sig_neg * (dg * jnp.exp(-log_f) - dkf), axis=0, keepdims=True)

    blk = (cps * c_len, HGRN_WIDTH)
    sblk = (cps, c_len, HGRN_WIDTH)
    rev = lambda c: last - c
    step = lambda k: (lambda: pl.program_id(0) == k)
    e_in, e_out, e_shape, e_scr, e_args = _ride_specs(ride)
    return pl.pallas_call(
        _riding(body, 7, 4, 7, ride, step(0), step(n_steps // 2), step(last)),
        name="hgrn_bwd",
        grid=(n_steps,),
        in_specs=[
            pl.BlockSpec(blk, lambda c: (rev(c), col0)),
            pl.BlockSpec(blk, lambda c: (rev(c), col0 + 1)),
            pl.BlockSpec(blk, lambda c: (rev(c), col0 + 2)),
            pl.BlockSpec((2, HGRN_WIDTH), lambda c: (0, 0)),
            pl.BlockSpec(blk, lambda c: (rev(c), 0)),
            pl.BlockSpec((cps, nh, hd, hd), lambda c: (rev(c), 0, 0, 0)),
            pl.BlockSpec(blk, lambda c: (rev(c), 0)),
        ] + e_in,
        out_specs=[
            pl.BlockSpec(blk, lambda c: (rev(c), 0)),
            pl.BlockSpec(blk, lambda c: (rev(c), 0)),
            pl.BlockSpec(blk, lambda c: (rev(c), 0)),
            pl.BlockSpec((1, HGRN_WIDTH), lambda c: (0, 0)),
        ] + e_out,
        out_shape=[jax.ShapeDtypeStruct((s, HGRN_WIDTH), BF16)] * 3 + [jax.ShapeDtypeStruct((1, HGRN_WIDTH), F32)] + e_shape,
        scratch_shapes=[
            pltpu.VMEM((nh, hd, hd), F32),
            pltpu.VMEM(sblk, F32),
            pltpu.VMEM(sblk, F32),
            pltpu.VMEM(sblk, F32),
            pltpu.VMEM((cps, nh, c_len, hd), F32),
            pltpu.VMEM(sblk, F32),
            pltpu.VMEM(sblk, F32),
        ] + e_scr,
        compiler_params=_params(ride, dimension_semantics=("arbitrary",)),
    )(proj, proj, proj, lb, d_o, states, a_mat, *e_args)


def _per_head_lanes(x):
    lane = lax.broadcasted_iota(jnp.int32, (x.shape[0], LANES), 1)
    out = jnp.zeros((x.shape[0], LANES), F32)
    for h in range(ATTN_HEADS):
        out = jnp.where(lane == h, x[:, h * ATTN_HEAD_DIM:h * ATTN_HEAD_DIM + 1], out)
    return out


def _row_spec(tm, width, col=0):
    return pl.BlockSpec((tm, width), lambda i: (i, col))


def _const_spec(width):
    return pl.BlockSpec((1, width), lambda i: (0, 0))


def _acc_rows(ref, value):
    @pl.when(pl.program_id(0) == 0)
    def _():
        ref[...] = jnp.zeros_like(ref)

    ref[...] += jnp.sum(value, axis=0, keepdims=True)


def mix_fwd(attn_parts, o_h, proj, an, hn, w_out_b, gp, x, ride=None):
    s = x.shape[0]
    tm = TOKEN_TILE
    gate_col = 3
    hd = HGRN_HEAD_DIM
    nd = len(DILATIONS)

    def body(*refs):
        o_refs, l_refs = refs[:nd], refs[nd:2 * nd]
        oh_ref, gate_ref, an_ref, hn_ref, w_ref, gp_ref, x_ref = refs[2 * nd:2 * nd + 7]
        x1_ref, cat_ref, mixed_ref, attn_ref = refs[2 * nd + 7:2 * nd + 11]
        lse_refs = refs[2 * nd + 11:3 * nd + 11]
        o_scr, l_scr, lse_scr = refs[3 * nd + 11:]
        os_ = [_from_dilated(r, o_scr.at[k], d, tm) for k, (r, d) in enumerate(zip(o_refs, DILATIONS))]
        ls = [_from_dilated(r, l_scr.at[k], d, tm) for k, (r, d) in enumerate(zip(l_refs, DILATIONS))]
        m = jnp.maximum(jnp.maximum(ls[0], ls[1]), ls[2])
        es = [jnp.exp(l - m) for l in ls]
        den = es[0] + es[1] + es[2]
        attn = (es[0] * os_[0] + es[1] * os_[1] + es[2] * os_[2]) / den
        attn_ref[...] = attn
        lse_scr[0] = _per_head_lanes(m + jnp.log(den))
        for d, ref in zip(DILATIONS, lse_refs):
            _to_dilated(lse_scr, ref, d, tm)
        cat_ref[:, :ATTN_WIDTH] = _rms_fwd(attn, an_ref[...], ATTN_WIDTH).astype(BF16)
        gate = gate_ref[...]
        silu_g = gate * _sigmoid(gate)
        for h in range(HGRN_HEADS):
            hs = slice(h * hd, (h + 1) * hd)
            rec = _rms_fwd(oh_ref[:, hs], hn_ref[:, hs], hd) * silu_g[:, hs]
            cat_ref[:, ATTN_WIDTH + h * hd:ATTN_WIDTH + (h + 1) * hd] = rec.astype(BF16)
        mixed = _dot(cat_ref[...], w_ref[...])
        mixed_ref[...] = mixed
        x1_ref[...] = x_ref[...] + _rms_fwd(mixed, gp_ref[...], D_MODEL)

    aw = ATTN_WIDTH
    n_steps = s // tm
    step = lambda k: (lambda: pl.program_id(0) == k)
    e_in, e_out, e_shape, e_scr, e_args = _ride_specs(ride)
    return pl.pallas_call(
        _riding(body, 2 * nd + 7, 4 + nd, 3, ride, step(0), step((13 * n_steps) // 16), step(n_steps - 1)),
        name="mix_fwd",
        grid=(n_steps,),
        in_specs=[_dilated_spec(d, tm, aw) for d in DILATIONS] * 2 + [
            _row_spec(tm, aw), _row_spec(tm, aw, gate_col), _const_spec(aw), _const_spec(aw), _vmem_spec(),
            _const_spec(D_MODEL), _row_spec(tm, D_MODEL)] + e_in,
        out_specs=[_row_spec(tm, D_MODEL), _row_spec(tm, D_MODEL), _row_spec(tm, D_MODEL), _row_spec(tm, aw)] + [
            _dilated_spec(d, tm, LANES) for d in DILATIONS] + e_out,
        out_shape=[
            jax.ShapeDtypeStruct((s, D_MODEL), F32),
            jax.ShapeDtypeStruct((s, D_MODEL), BF16),
            jax.ShapeDtypeStruct((s, D_MODEL), F32),
            jax.ShapeDtypeStruct((s, aw), F32),
        ] + [jax.ShapeDtypeStruct((d, s // d, LANES), F32) for d in DILATIONS] + e_shape,
        scratch_shapes=[pltpu.VMEM((nd, aw // LANES, tm, LANES), F32), pltpu.VMEM((nd, aw // LANES, tm, LANES), F32),
                        pltpu.VMEM((1, tm, LANES), F32)] + e_scr,
        compiler_params=_params(ride, dimension_semantics=("arbitrary",)),
    )(*[p[0] for p in attn_parts], *[p[1] for p in attn_parts], o_h, proj, an, hn, w_out_b, gp, x, *e_args)


def mix_bwd(dx1, mixed, gp, w_out_b, attn, an, o_h, proj, hn):
    s = dx1.shape[0]
    tm = TOKEN_TILE
    gate_col = 3
    hd = HGRN_HEAD_DIM
    aw = ATTN_WIDTH

    nd = len(DILATIONS)

    def body(*refs):
        dx1_ref, mixed_ref, gp_ref, w_ref, attn_ref, an_ref, oh_ref, gate_ref, hn_ref, dmix_ref = refs[:10]
        do_refs, delta_refs = refs[10:10 + nd], refs[10 + nd:10 + 2 * nd]
        doh_ref, dgate_ref, dgp_ref, dan_ref, dhn_ref, do_ref, delta_ref = refs[10 + 2 * nd:]
        dmixed, gp_c = _rms_bwd(dx1_ref[...], mixed_ref[...], gp_ref[...], D_MODEL)
        _acc_rows(dgp_ref, gp_c)
        dmixed_b = dmixed.astype(BF16)
        dmix_ref[...] = dmixed_b
        dcat = _dot_nt(dmixed_b, w_ref[...])
        attn = attn_ref[...]
        d_o, an_c = _rms_bwd(dcat[:, :aw], attn, an_ref[...], aw)
        _acc_rows(dan_ref, an_c)
        _lane_blocks(do_ref, d_o)
        prod = d_o * attn
        lane = lax.broadcasted_iota(jnp.int32, (tm, LANES), 1)
        delta = jnp.zeros((tm, LANES), F32)
        for pair in range(ATTN_HEADS // 2):
            pp = prod[:, pair * LANES:(pair + 1) * LANES]
            low = _lane_half((tm, LANES), 0)
            lo = jnp.sum(jnp.where(low, pp, 0.0), axis=-1, keepdims=True)
            hi = jnp.sum(jnp.where(low, 0.0, pp), axis=-1, keepdims=True)
            delta = jnp.where(lane == 2 * pair, lo, jnp.where(lane == 2 * pair + 1, hi, delta))
        delta_ref[0] = delta
        for d, o_ref, l_ref in zip(DILATIONS, do_refs, delta_refs):
            _to_dilated(do_ref, o_ref, d, tm, cast=BF16)
            _to_dilated(delta_ref, l_ref, d, tm)
        gate = gate_ref[...]
        sg = _sigmoid(gate)
        silu_g = gate * sg
        drec = dcat[:, aw:]
        hn_parts = []
        for h in range(HGRN_HEADS):
            hs = slice(h * hd, (h + 1) * hd)
            oh = oh_ref[:, hs]
            on = _rms_fwd(oh, hn_ref[:, hs], hd)
            dgate_ref[:, hs] = (drec[:, hs] * on * (sg[:, hs] * (1.0 + gate[:, hs] * (1.0 - sg[:, hs])))).astype(BF16)
            d_oh, hn_c = _rms_bwd(drec[:, hs] * silu_g[:, hs], oh, hn_ref[:, hs], hd)
            doh_ref[:, hs] = d_oh
            hn_parts.append(hn_c)
        _acc_rows(dhn_ref, jnp.concatenate(hn_parts, axis=1))

    return pl.pallas_call(
        body,
        name="mix_bwd",
        grid=(s // tm,),
        in_specs=[_row_spec(tm, D_MODEL), _row_spec(tm, D_MODEL), _const_spec(D_MODEL), _vmem_spec(), _row_spec(tm, aw),
                  _const_spec(aw), _row_spec(tm, aw), _row_spec(tm, aw, gate_col), _const_spec(aw)],
        out_specs=[_row_spec(tm, D_MODEL)] + [_dilated_spec(d, tm, aw) for d in DILATIONS] + [
            _dilated_spec(d, tm, LANES) for d in DILATIONS] + [_row_spec(tm, aw)] * 2 + [
            _const_spec(D_MODEL), _const_spec(aw), _const_spec(aw)],
        out_shape=[jax.ShapeDtypeStruct((s, D_MODEL), BF16)] + [
            jax.ShapeDtypeStruct((d, s // d, aw), BF16) for d in DILATIONS] + [
            jax.ShapeDtypeStruct((d, s // d, LANES), F32) for d in DILATIONS] + [
            jax.ShapeDtypeStruct((s, aw), F32), jax.ShapeDtypeStruct((s, aw), BF16),
            jax.ShapeDtypeStruct((1, D_MODEL), F32), jax.ShapeDtypeStruct((1, aw), F32),
            jax.ShapeDtypeStruct((1, aw), F32)],
        scratch_shapes=[pltpu.VMEM((aw // LANES, tm, LANES), F32), pltpu.VMEM((1, tm, LANES), F32)],
        compiler_params=_params(dimension_semantics=("arbitrary",)),
    )(dx1, mixed, gp, w_out_b, attn, an, o_h, proj, hn)


def mlp_fwd_bwd(x1, g_pre, w1_blocks, w2_b, g_post, target):
    s = x1.shape[0]
    tm = MLP_TILE
    nblk, _, fb = w1_blocks.shape

    def body(x1_ref, gpre_ref, w1_ref, w2_ref, gpost_ref, t_ref,
             dx1_ref, h2_ref, a_ref, du_ref, dff_ref, loss_ref, dgpre_ref, dgpost_ref, u_ref):
        x1v = x1_ref[...]
        h2 = _rms_fwd(x1v, gpre_ref[...], D_MODEL).astype(BF16)
        h2_ref[...] = h2
        ff = jnp.zeros((tm, D_MODEL), F32)
        for j in range(nblk):
            cols = slice(j * fb, (j + 1) * fb)
            ru = jnp.maximum(_dot(h2, w1_ref[j]), 0.0)
            u_ref[:, cols] = ru.astype(BF16)
            a = (ru * ru).astype(BF16)
            a_ref[:, cols] = a
            ff = ff + _dot(a, w2_ref[cols, :])
        diff = x1v + _rms_fwd(ff, gpost_ref[...], D_MODEL) - t_ref[...]
        _acc_rows(loss_ref, diff * diff)
        dy = diff * (1.0 / D_MODEL)
        dff, gpost_c = _rms_bwd(dy, ff, gpost_ref[...], D_MODEL)
        _acc_rows(dgpost_ref, gpost_c)
        dff_b = dff.astype(BF16)
        dff_ref[...] = dff_b
        dh2 = jnp.zeros((tm, D_MODEL), F32)
        for j in range(nblk):
            cols = slice(j * fb, (j + 1) * fb)
            du = (_dot_nt(dff_b, w2_ref[cols, :]) * (2.0 * u_ref[:, cols])).astype(BF16)
            du_ref[:, cols] = du
            dh2 = dh2 + _dot_nt(du, w1_ref[j])
        dxa, gpre_c = _rms_bwd(dh2, x1v, gpre_ref[...], D_MODEL)
        _acc_rows(dgpre_ref, gpre_c)
        dx1_ref[...] = dy + dxa

    dm = D_MODEL
    return pl.pallas_call(
        body,
        name="mlp_fwd_bwd",
        grid=(s // tm,),
        in_specs=[_row_spec(tm, dm), _const_spec(dm), _vmem_spec(), _vmem_spec(), _const_spec(dm), _row_spec(tm, dm)],
        out_specs=[_row_spec(tm, dm), _row_spec(tm, dm), _row_spec(tm, D_FF), _row_spec(tm, D_FF), _row_spec(tm, dm),
                   _const_spec(dm), _const_spec(dm), _const_spec(dm)],
        out_shape=[
            jax.ShapeDtypeStruct((s, dm), F32),
            jax.ShapeDtypeStruct((s, dm), BF16),
            jax.ShapeDtypeStruct((s, D_FF), BF16),
            jax.ShapeDtypeStruct((s, D_FF), BF16),
            jax.ShapeDtypeStruct((s, dm), BF16),
            jax.ShapeDtypeStruct((1, dm), F32),
            jax.ShapeDtypeStruct((1, dm), F32),
            jax.ShapeDtypeStruct((1, dm), F32),
        ],
        scratch_shapes=[pltpu.VMEM((tm, D_FF), BF16)],
        compiler_params=_params(dimension_semantics=("arbitrary",)),
    )(x1, g_pre, w1_blocks, w2_b, g_post, target)


def in_proj_bwd(attn_grads, hgrn_grads, dgate, w_in_b, x, g1, dx1):
    s = x.shape[0]
    tm = PROJ_TILE
    aw = ATTN_WIDTH
    n_attn = len(attn_grads)
    flat = [g[k] for k in range(3) for g in attn_grads] + list(hgrn_grads) + [dgate]

    def body(*refs):
        parts = refs[:len(flat)]
        w_ref, x_ref, g_ref, dx1_ref, dx_ref, dproj_ref, dg_ref, scr = refs[len(flat):]
        groups = []
        for k in range(3):
            acc = None
            for p, d in zip(parts[k * n_attn:(k + 1) * n_attn], DILATIONS):
                v = _from_dilated(p, scr, d, tm)
                acc = v if acc is None else acc + v
            groups.append(acc)
        groups += [p[...] for p in parts[3 * n_attn:]]
        dh = jnp.zeros((tm, D_MODEL), F32)
        for gi, grp in enumerate(groups):
            cols = slice(gi * aw, (gi + 1) * aw)
            gb = grp.astype(BF16)
            dproj_ref[:, cols] = gb
            dh = dh + _dot_nt(gb, w_ref[:, cols])
        dxa, g_c = _rms_bwd(dh, x_ref[...], g_ref[...], D_MODEL)
        _acc_rows(dg_ref, g_c)
        dx_ref[...] = dx1_ref[...] + dxa

    dm = D_MODEL
    return pl.pallas_call(
        body,
        name="in_proj_bwd",
        grid=(s // tm,),
        in_specs=[_dilated_spec(d, tm, aw) for d in DILATIONS] * 3 + [_row_spec(tm, aw)] * 4 + [
            _vmem_spec(), _row_spec(tm, dm), _const_spec(dm), _row_spec(tm, dm)],
        out_specs=[_row_spec(tm, dm), _row_spec(tm, IN_PROJ_WIDTH), _const_spec(dm)],
        out_shape=[jax.ShapeDtypeStruct((s, dm), F32), jax.ShapeDtypeStruct((s, IN_PROJ_WIDTH), BF16),
                   jax.ShapeDtypeStruct((1, dm), F32)],
        scratch_shapes=[pltpu.VMEM((aw // LANES, tm, LANES), F32)],
        compiler_params=_params(dimension_semantics=("arbitrary",)),
    )(*flat, w_in_b, x, g1, dx1)


def wgrad(a_b, b_b, tn, name, ts=2048, per_step=1, ride=None):
    s, k = a_b.shape
    n = b_b.shape[1]

    def body(a_ref, b_ref, o_ref):
        @pl.when(pl.program_id(1) == 0)
        def _():
            o_ref[...] = jnp.zeros_like(o_ref)

        a = a_ref[...]
        for jj in range(per_step):
            o_ref[jj] += _dot_tn(a, b_ref[:, jj * tn:(jj + 1) * tn])

    wide = tn * per_step
    gn, gs = n // wide, s // ts
    step = lambda j, i: (lambda: (pl.program_id(0) == j) & (pl.program_id(1) == i))
    e_in, e_out, e_shape, e_scr, e_args = _ride_specs(ride)
    out = pl.pallas_call(
        _riding(body, 2, 1, 0, ride, step(0, 0), step(gn // 2, 0), step(gn - 1, gs - 1)),
        name=name,
        grid=(gn, gs),
        in_specs=[pl.BlockSpec((ts, k), lambda j, i: (i, 0)), pl.BlockSpec((ts, wide), lambda j, i: (i, j))] + e_in,
        out_specs=[pl.BlockSpec((per_step, k, tn), lambda j, i: (j, 0, 0))] + e_out,
        out_shape=[jax.ShapeDtypeStruct((n // tn, k, tn), F32)] + e_shape,
        scratch_shapes=e_scr,
        compiler_params=_params(ride, dimension_semantics=("arbitrary", "arbitrary")),
    )(a_b, b_b, *e_args)
    return out[0] if ride is None else out


def train_step(x, target, g1, an, logits, hn, gp, g_pre, g_post, w, m, v):
    nd = len(DILATIONS)
    shard_b = {k: w[k].astype(BF16) for k in BIG}
    (w_in_g,) = run_exchange(gather_exchange([shard_b["w_in"]]), "gather_w_in")
    w_in_b = w_in_g.transpose(1, 0, 2).reshape(D_MODEL, IN_PROJ_WIDTH)

    proj, h_b, *qkvs, w2_g = in_proj_fwd(x, g1, w_in_b, ride=gather_exchange([shard_b["w_ff2"]]))
    w2_b = w2_g.reshape(D_FF, D_MODEL)
    attn_parts = [attn_fwd(qkv, d) for qkv, d in zip(qkvs, DILATIONS)]
    o_h, states, a_mat, w_out_g, w1_blocks = hgrn_fwd(
        proj, logits, ride=gather_exchange([shard_b["w_out"], shard_b["w_ff1"]]))
    w_out_b = w_out_g.reshape(D_MODEL, D_MODEL)
    x1, cat_b, mixed, attn, *lses = mix_fwd(attn_parts, o_h, proj, an, hn, w_out_b, gp, x)
    dx1, h2_b, a_b, du_b, dff_b, loss_vec, dg_pre, dg_post = mlp_fwd_bwd(x1, g_pre, w1_blocks, w2_b, g_post, target)
    dw2 = wgrad(a_b, dff_b, D_MODEL, "wgrad_ff2", ts=512)
    dw1 = wgrad(h2_b, du_b, D_FF // N_DEV, "wgrad_ff1", per_step=2)
    dmix_b, *rest = mix_bwd(dx1, mixed, gp, w_out_b, attn, an, o_h, proj, hn)
    d_os, deltas = rest[:nd], rest[nd:2 * nd]
    d_oh, dgate, dgp, dan, dhn = rest[2 * nd:]
    dwout = wgrad(cat_b, dmix_b, D_MODEL, "wgrad_out")

    early = ("w_out", "w_ff1", "w_ff2")
    early_grads = [dwout.reshape(N_DEV, D_MODEL // N_DEV, D_MODEL), dw1, dw2.reshape(N_DEV, D_FF // N_DEV, D_MODEL)]
    res = attn_bwd(qkvs[0], d_os[0], lses[0], deltas[0], DILATIONS[0], ride=to_core_exchange(early_grads))
    pairs = [pair_sum(g, s, f"pair_sum_{name}") for g, s, name in zip(early_grads, res[3:], early)]
    attn_grads = [res[:3]]
    *res, others_ff2 = attn_bwd(qkvs[1], d_os[1], lses[1], deltas[1], DILATIONS[1],
                                ride=to_chip_exchange([pairs[2][1]]))
    attn_grads.append(res)
    attn_grads.append(attn_bwd(qkvs[2], d_os[2], lses[2], deltas[2], DILATIONS[2]))
    dq_h, df_h, di_h, dlb, *others = hgrn_bwd(proj, logits, d_oh, states, a_mat,
                                              ride=to_chip_exchange([pairs[0][1], pairs[1][1]]))
    others.append(others_ff2)
    dx, dproj_b, dg1 = in_proj_bwd(attn_grads, (dq_h, df_h, di_h), dgate, w_in_b, x, g1, dx1)
    packed = _pack_small(dg1, dgp, dg_pre, dg_post, dan, dhn, dlb, loss_vec)
    dwin, small_slots = wgrad(h_b, dproj_b, 2 * IN_PROJ_WIDTH // N_DEV, "wgrad_in",
                              ride=small_exchange(packed))
    big = {name: sum_adamw(p[0], o, w[name], m[name], v[name], f"sum_adamw_{name}")
           for name, p, o in zip(early, pairs, others)}

    shard_w = IN_PROJ_WIDTH // N_DEV
    dwin_blocks = dwin.reshape(N_DEV // 2, D_MODEL, 2, shard_w).transpose(0, 2, 1, 3).reshape(N_DEV, D_MODEL, shard_w)
    (from_sibling,) = run_exchange(to_core_exchange([dwin_blocks.astype(BF16)]), "reduce_w_in_to_core")
    pair_in, others_in = reduce_last_to_chip(dwin_blocks, from_sibling)
    big["w_in"] = sum_adamw(pair_in, others_in, w["w_in"], m["w_in"], v["w_in"], "sum_adamw_w_in")
    return dx, big, small_slots


def _position():
    x, y, c = lax.axis_index("x"), lax.axis_index("y"), lax.axis_index("c")
    other_chips = [(1 - x, y), (x, 1 - y), (1 - x, 1 - y)]
    return x, y, c, other_chips


def _any_spec():
    return pl.BlockSpec(memory_space=pl.ANY)


class Exchange:
    def __init__(self, arrays, out_shape, sems, stages, collective_id, peers):
        self.arrays, self.out_shape, self.sems, self.stages = list(arrays), list(out_shape), list(sems), stages
        self.collective_id, self.peers = collective_id, peers

    def open(self):
        barrier = pltpu.get_barrier_semaphore()
        peers = self.peers()
        for peer in peers:
            pl.semaphore_signal(barrier, inc=1, device_id=peer, device_id_type=MESH)
        pl.semaphore_wait(barrier, len(peers))


def _siblings():
    x, y, c, _ = _position()
    return [(x, y, 1 - c)]


def _same_core_of_other_chips():
    x, y, c, chips = _position()
    return [(px, py, c) for px, py in chips]


def _gather_peers():
    x, y, c, _ = _position()
    return [(x, y, 1 - c), (1 - x, y, c), (x, 1 - y, c)]


def _all_others():
    x, y, c, _ = _position()
    return [(1 - x if rel & 4 else x, 1 - y if rel & 2 else y, 1 - c if rel & 1 else c) for rel in range(1, N_DEV)]


def gather_exchange(shards):
    n = len(shards)
    halves = [sh.shape[0] // 2 for sh in shards]

    def stages(ins, outs, sems):
        send_sems, recv_sems, local_sems = sems

        def parts():
            x, y, c, _ = _position()
            me, sibling = (x, y, c), (x, y, 1 - c)
            nbr_x, nbr_y, diag = (1 - x, y, c), (x, 1 - y, c), (1 - x, 1 - y, c)

            def slot(a, dev, rows=None):
                ref = outs[a].at[4 * dev[0] + 2 * dev[1] + dev[2]]
                return ref if rows is None else ref.at[rows]

            def copy(a, k, block, to, rows=None, src=None):
                return pltpu.make_async_remote_copy(
                    src_ref=slot(a, block, rows) if src is None else src, dst_ref=slot(a, block, rows),
                    send_sem=send_sems.at[a, k], recv_sem=recv_sems.at[a, k], device_id=to, device_id_type=MESH)

            upper = lambda a: pl.ds(0, halves[a])
            lower = lambda a: pl.ds(halves[a], halves[a])
            return me, sibling, nbr_x, nbr_y, diag, slot, copy, upper, lower

        def begin():
            me, sibling, nbr_x, nbr_y, _, slot, copy, _, _ = parts()
            for a in range(n):
                pltpu.make_async_copy(ins[a], slot(a, me), local_sems.at[a]).start()
                for k, to in enumerate((sibling, nbr_x, nbr_y)):
                    copy(a, k, me, to, src=ins[a]).start()

        def middle():
            me, sibling, nbr_x, nbr_y, _, _, copy, upper, lower = parts()
            for a in range(n):
                copy(a, 1, nbr_x, me).wait_recv()
                copy(a, 3, nbr_x, sibling).start()
                copy(a, 5, nbr_x, nbr_y, rows=lower(a)).start()
                copy(a, 2, nbr_y, me).wait_recv()
                copy(a, 4, nbr_y, sibling).start()
                copy(a, 6, nbr_y, nbr_x, rows=upper(a)).start()

        def late():
            me, sibling, _, _, diag, _, copy, upper, lower = parts()
            for a in range(n):
                copy(a, 6, diag, me, rows=upper(a)).wait_recv()
                copy(a, 5, diag, me, rows=lower(a)).wait_recv()
                copy(a, 7, diag, sibling).start()

        def end():
            me, sibling, nbr_x, nbr_y, diag, slot, copy, upper, lower = parts()
            sib = lambda dev: (dev[0], dev[1], sibling[2])
            for a in range(n):
                for k, block in ((0, sibling), (3, sib(nbr_x)), (4, sib(nbr_y)), (7, sib(diag))):
                    copy(a, k, block, me).wait_recv()
                copy(a, 0, me, sibling, src=ins[a]).wait_send()
                copy(a, 1, me, nbr_x, src=ins[a]).wait_send()
                copy(a, 2, me, nbr_y, src=ins[a]).wait_send()
                copy(a, 3, nbr_x, sibling).wait_send()
                copy(a, 4, nbr_y, sibling).wait_send()
                copy(a, 5, nbr_x, nbr_y, rows=lower(a)).wait_send()
                copy(a, 6, nbr_y, nbr_x, rows=upper(a)).wait_send()
                copy(a, 7, diag, sibling).wait_send()
                pltpu.make_async_copy(ins[a], slot(a, me), local_sems.at[a]).wait()

        return begin, (middle, late), end

    return Exchange(
        shards, [jax.ShapeDtypeStruct((N_DEV,) + sh.shape, sh.dtype) for sh in shards],
        [pltpu.SemaphoreType.DMA((n, 8)), pltpu.SemaphoreType.DMA((n, 8)), pltpu.SemaphoreType.DMA((n,))], stages,
        collective_id=0, peers=_gather_peers)


def to_core_exchange(grads):
    n = len(grads)

    def stages(ins, outs, sems):
        send_sems, recv_sems = sems

        def copies():
            x, y, c, _ = _position()
            return [pltpu.make_async_remote_copy(
                src_ref=ins[a].at[2 * q + (1 - c)], dst_ref=outs[a].at[q], send_sem=send_sems.at[a, q],
                recv_sem=recv_sems.at[a, q], device_id=(x, y, 1 - c), device_id_type=MESH)
                for a in range(n) for q in range(4)]

        def begin():
            for cp in copies():
                cp.start()

        def end():
            for cp in copies():
                cp.wait()

        return begin, None, end

    return Exchange(grads, [jax.ShapeDtypeStruct((4,) + g.shape[1:], g.dtype) for g in grads],
                    [pltpu.SemaphoreType.DMA((n, 4)), pltpu.SemaphoreType.DMA((n, 4))], stages,
                    collective_id=1, peers=_siblings)


def pair_sum(grad, from_sibling, name):
    _, r, cdim = grad.shape
    tr = min(r, ELEMENTWISE_ROWS)
    c_idx = lax.axis_index("c").astype(jnp.int32).reshape(1)

    def body(c_ref, g_ref, s_ref, o_ref, ob_ref):
        total = g_ref[...] + s_ref[...]
        o_ref[...] = total
        ob_ref[...] = total.astype(BF16)

    blk = lambda: pl.BlockSpec((1, tr, cdim), lambda q, i, cr: (q, i, 0))
    return pl.pallas_call(
        body,
        name=name,
        grid_spec=pltpu.PrefetchScalarGridSpec(
            num_scalar_prefetch=1,
            grid=(4, r // tr),
            in_specs=[pl.BlockSpec((1, tr, cdim), lambda q, i, cr: (2 * q + cr[0], i, 0)), blk()],
            out_specs=[blk(), blk()],
        ),
        out_shape=[jax.ShapeDtypeStruct((4, r, cdim), F32), jax.ShapeDtypeStruct((4, r, cdim), BF16)],
        compiler_params=_params(dimension_semantics=("arbitrary", "arbitrary")),
    )(c_idx, grad, from_sibling)


def to_chip_exchange(pairs):
    n = len(pairs)

    def stages(ins, outs, sems):
        send_sems, recv_sems = sems

        def copies():
            x, y, c, chips = _position()
            return [pltpu.make_async_remote_copy(
                src_ref=ins[a].at[2 * px + py], dst_ref=outs[a].at[j], send_sem=send_sems.at[a, j],
                recv_sem=recv_sems.at[a, j], device_id=(px, py, c), device_id_type=MESH)
                for a in range(n) for j, (px, py) in enumerate(chips)]

        def begin():
            for cp in copies():
                cp.start()

        def end():
            for cp in copies():
                cp.wait()

        return begin, None, end

    return Exchange(pairs, [jax.ShapeDtypeStruct((3,) + p.shape[1:], p.dtype) for p in pairs],
                    [pltpu.SemaphoreType.DMA((n, 3)), pltpu.SemaphoreType.DMA((n, 3))], stages,
                    collective_id=2, peers=_same_core_of_other_chips)


def run_exchange(ex, name):
    n_in, n_out = len(ex.arrays), len(ex.out_shape)

    def body(*refs):
        begin, middle, end = ex.stages(refs[:n_in], refs[n_in:n_in + n_out], refs[n_in + n_out:])
        ex.open()
        begin()
        for stage in _as_tuple(middle):
            stage()
        end()

    return pl.pallas_call(
        body,
        name=name,
        in_specs=[_any_spec()] * n_in,
        out_specs=[_any_spec()] * n_out,
        out_shape=ex.out_shape,
        scratch_shapes=ex.sems,
        compiler_params=pltpu.CompilerParams(collective_id=ex.collective_id),
    )(*ex.arrays)


def _as_tuple(stages):
    return () if stages is None else stages if isinstance(stages, tuple) else (stages,)


def _riding(body, n_in, n_out, n_scratch, ex, first, middle, last, late=None):
    if ex is None:
        return body
    r_in, r_out = len(ex.arrays), len(ex.out_shape)

    def wrapped(*refs):
        k_in, refs = refs[:n_in], refs[n_in:]
        e_in, refs = refs[:r_in], refs[r_in:]
        k_out, refs = refs[:n_out], refs[n_out:]
        e_out, refs = refs[:r_out], refs[r_out:]
        k_scr, e_sems = refs[:n_scratch], refs[n_scratch:]
        begin, mid, end = ex.stages(e_in, e_out, e_sems)

        @pl.when(first())
        def _():
            ex.open()
            begin()

        body(*k_in, *k_out, *k_scr)
        for stage, at in zip(_as_tuple(mid), (middle, late or last)):
            pl.when(at())(stage)
        pl.when(last())(end)

    return wrapped


def _ride_specs(ex):
    if ex is None:
        return [], [], [], [], []
    return [_any_spec()] * len(ex.arrays), [_any_spec()] * len(ex.out_shape), ex.out_shape, ex.sems, ex.arrays


def reduce_last_to_chip(grad, from_sibling):
    _, r, cdim = grad.shape
    ex_peers, ex_id = _same_core_of_other_chips, 2

    def body(g_hbm, s_hbm, own_ref, others_hbm, g_buf, s_buf, send_buf, load_sems, send_sems, recv_sems):
        x, y, c, chips = _position()
        barrier = pltpu.get_barrier_semaphore()
        for peer in ex_peers():
            pl.semaphore_signal(barrier, inc=1, device_id=peer, device_id_type=MESH)
        pl.semaphore_wait(barrier, len(chips))

        def pair_sum_of(q):
            loads = [pltpu.make_async_copy(g_hbm.at[2 * q + c], g_buf, load_sems.at[0]),
                     pltpu.make_async_copy(s_hbm.at[q], s_buf, load_sems.at[1])]
            for cp in loads:
                cp.start()
            for cp in loads:
                cp.wait()
            return g_buf[...] + s_buf[...].astype(F32)

        sends = []
        for j, (px, py) in enumerate(chips):
            send_buf[j] = pair_sum_of(2 * px + py).astype(BF16)
            sends.append(pltpu.make_async_remote_copy(
                src_ref=send_buf.at[j], dst_ref=others_hbm.at[j], send_sem=send_sems.at[j], recv_sem=recv_sems.at[j],
                device_id=(px, py, c), device_id_type=MESH))
            sends[-1].start()
        own_ref[0] = pair_sum_of(2 * x + y)
        for cp in sends:
            cp.wait()

    return pl.pallas_call(
        body,
        name="reduce_w_in_to_chip",
        in_specs=[_any_spec(), _any_spec()],
        out_specs=[_vmem_spec(), _any_spec()],
        out_shape=[jax.ShapeDtypeStruct((1, r, cdim), F32), jax.ShapeDtypeStruct((3, r, cdim), BF16)],
        scratch_shapes=[pltpu.VMEM((r, cdim), F32), pltpu.VMEM((r, cdim), BF16), pltpu.VMEM((3, r, cdim), BF16),
                        pltpu.SemaphoreType.DMA((2,)), pltpu.SemaphoreType.DMA((3,)), pltpu.SemaphoreType.DMA((3,))],
        compiler_params=pltpu.CompilerParams(collective_id=ex_id, vmem_limit_bytes=VMEM_LIMIT),
    )(grad, from_sibling)


def _adamw(w, g, m, v):
    m = ADAM_B1 * m + (1.0 - ADAM_B1) * g
    v = ADAM_B2 * v + (1.0 - ADAM_B2) * (g * g)
    m_hat = m / (1.0 - ADAM_B1 ** ADAM_STEP)
    v_hat = v / (1.0 - ADAM_B2 ** ADAM_STEP)
    delta = -ADAM_LR * (m_hat / (jnp.sqrt(v_hat) + ADAM_EPS) + ADAM_WD * w)
    return delta, m, v


def sum_adamw(pairs, others, w, m, v, name):
    r, cdim = w.shape
    tr = min(r, ELEMENTWISE_ROWS // 2)
    if pairs.shape[0] == 1:
        chip_idx = jnp.zeros((1,), jnp.int32)
    else:
        chip_idx = (2 * lax.axis_index("x") + lax.axis_index("y")).astype(jnp.int32).reshape(1)

    def body(q_ref, p_ref, o_ref, w_ref, m_ref, v_ref, g_out, d_out, m_out, v_out):
        g = p_ref[0] + o_ref[0].astype(F32) + o_ref[1].astype(F32) + o_ref[2].astype(F32)
        g_out[...] = g
        d_out[...], m_out[...], v_out[...] = _adamw(w_ref[...], g, m_ref[...], v_ref[...])

    tile = lambda: pl.BlockSpec((tr, cdim), lambda i, qr: (i, 0))
    return pl.pallas_call(
        body,
        name=name,
        grid_spec=pltpu.PrefetchScalarGridSpec(
            num_scalar_prefetch=1,
            grid=(r // tr,),
            in_specs=[pl.BlockSpec((1, tr, cdim), lambda i, qr: (qr[0], i, 0)),
                      pl.BlockSpec((3, tr, cdim), lambda i, qr: (0, i, 0)), tile(), tile(), tile()],
            out_specs=[tile(), tile(), tile(), tile()],
        ),
        out_shape=[jax.ShapeDtypeStruct((r, cdim), F32)] * 4,
        compiler_params=_params(dimension_semantics=("arbitrary",)),
    )(chip_idx, pairs, others, w, m, v)


def small_exchange(packed):
    def stages(ins, outs, sems):
        send_sems, recv_sems, local_sem = sems
        (src,), (slots,) = ins, outs

        def copies():
            x, y, c, _ = _position()
            my_id = 4 * x + 2 * y + c
            sends, landings = [], []
            for rel in range(1, N_DEV):
                px = 1 - x if (rel >> 2) & 1 else x
                py = 1 - y if (rel >> 1) & 1 else y
                pc = 1 - c if rel & 1 else c
                peer = dict(send_sem=send_sems.at[rel - 1], recv_sem=recv_sems.at[rel - 1], device_id=(px, py, pc),
                            device_id_type=MESH)
                sends.append(pltpu.make_async_remote_copy(src_ref=src, dst_ref=slots.at[my_id], **peer))
                landings.append(pltpu.make_async_remote_copy(src_ref=src, dst_ref=slots.at[4 * px + 2 * py + pc], **peer))
            return pltpu.make_async_copy(src, slots.at[my_id], local_sem), sends, landings

        def begin():
            local, sends, _ = copies()
            local.start()
            for cp in sends:
                cp.start()

        def end():
            local, sends, landings = copies()
            for cp in landings:
                cp.wait_recv()
            for cp in sends:
                cp.wait_send()
            local.wait()

        return begin, None, end

    return Exchange([packed], [jax.ShapeDtypeStruct((N_DEV,) + packed.shape, packed.dtype)],
                    [pltpu.SemaphoreType.DMA((N_DEV - 1,)), pltpu.SemaphoreType.DMA((N_DEV - 1,)),
                     pltpu.SemaphoreType.DMA(())], stages, collective_id=3, peers=_all_others)


def small_adamw(slots, w, m, v):
    def body(r_ref, w_ref, m_ref, v_ref, g_out, d_out, m_out, v_out, loss_out):
        red = r_ref[0]
        for k in range(1, N_DEV):
            red = red + r_ref[k]
        wv = w_ref[...]
        lb = _lower_bound(jnp.concatenate([wv[5:6, :HGRN_WIDTH], wv[5:6, HGRN_WIDTH:]], axis=0))
        t = red[5:6, :HGRN_WIDTH] * lb * (1.0 - lb)
        row = lax.broadcasted_iota(jnp.int32, red.shape, 0)
        g = jnp.where(row == 5, jnp.concatenate([t, -t], axis=1), jnp.where(row >= 6, 0.0, red))
        g_out[...] = g
        d_out[...], m_out[...], v_out[...] = _adamw(wv, g, m_ref[...], v_ref[...])
        loss = jnp.sum(red[6:7, :], axis=-1, keepdims=True) * (0.5 / D_MODEL)
        loss_out[...] = jnp.broadcast_to(loss, loss_out.shape)

    return pl.pallas_call(
        body,
        name="small_adamw",
        in_specs=[_vmem_spec()] * 4,
        out_specs=[_vmem_spec()] * 5,
        out_shape=[jax.ShapeDtypeStruct(w.shape, F32)] * 4 + [jax.ShapeDtypeStruct((SUBLANES, LANES), F32)],
    )(slots, w, m, v)


def _pack_small(g1, gp, g_pre, g_post, an, hn, logits_or_dlb, extra=None):
    row5 = logits_or_dlb.reshape(1, -1)
    row5 = jnp.pad(row5, ((0, 0), (0, D_MODEL - row5.shape[1])))
    row6 = jnp.zeros((1, D_MODEL), F32) if extra is None else extra
    return jnp.concatenate([g1, gp, g_pre, g_post, jnp.concatenate([an, hn], axis=1), row5, row6,
                            jnp.zeros((1, D_MODEL), F32)], axis=0)


def _unpack_small(p):
    return dict(mix_pre_norm=p[0:1], mix_post_norm=p[1:2], mlp_pre_norm=p[2:3], mlp_post_norm=p[3:4],
                attn_out_norm=p[4:5, :ATTN_WIDTH], hgrn_out_norm=p[4:5, ATTN_WIDTH:],
                hgrn_lb_logits=p[5].reshape(2, HGRN_WIDTH))


BIG = ("w_in", "w_out", "w_ff1", "w_ff2")
ORDER = ("mix_pre_norm", "w_in", "attn_out_norm", "hgrn_lb_logits", "hgrn_out_norm", "w_out", "mix_post_norm",
         "mlp_pre_norm", "w_ff1", "w_ff2", "mlp_post_norm")


def kernel(x, mix_pre_norm, w_in, attn_out_norm, hgrn_lb_logits, hgrn_out_norm, w_out, mix_post_norm, mlp_pre_norm, w_ff1, w_ff2, mlp_post_norm, loss_target, m_mix_pre_norm, m_w_in, m_attn_out_norm, m_hgrn_lb_logits, m_hgrn_out_norm, m_w_out, m_mix_post_norm, m_mlp_pre_norm, m_w_ff1, m_w_ff2, m_mlp_post_norm, v_mix_pre_norm, v_w_in, v_attn_out_norm, v_hgrn_lb_logits, v_hgrn_out_norm, v_w_out, v_mix_post_norm, v_mlp_pre_norm, v_w_ff1, v_w_ff2, v_mlp_post_norm):
    w = dict(w_in=w_in[0], w_out=w_out[0], w_ff1=w_ff1[0], w_ff2=w_ff2[0])
    m = dict(w_in=m_w_in[0], w_out=m_w_out[0], w_ff1=m_w_ff1[0], w_ff2=m_w_ff2[0])
    v = dict(w_in=v_w_in[0], w_out=v_w_out[0], w_ff1=v_w_ff1[0], w_ff2=v_w_ff2[0])

    dx, big, small_slots = train_step(x[0], loss_target[0], mix_pre_norm, attn_out_norm, hgrn_lb_logits, hgrn_out_norm,
                                      mix_post_norm, mlp_pre_norm, mlp_post_norm, w, m, v)

    pack = lambda a, b, c2, d, e, f, g: _pack_small(a, b, c2, d, e, f, g)
    w_s = pack(mix_pre_norm, mix_post_norm, mlp_pre_norm, mlp_post_norm, attn_out_norm, hgrn_out_norm, hgrn_lb_logits)
    m_s = pack(m_mix_pre_norm, m_mix_post_norm, m_mlp_pre_norm, m_mlp_post_norm, m_attn_out_norm, m_hgrn_out_norm,
               m_hgrn_lb_logits)
    v_s = pack(v_mix_pre_norm, v_mix_post_norm, v_mlp_pre_norm, v_mlp_post_norm, v_attn_out_norm, v_hgrn_out_norm,
               v_hgrn_lb_logits)
    g_s, d_s, nm_s, nv_s, loss = small_adamw(small_slots, w_s, m_s, v_s)
    small_out = [_unpack_small(t) for t in (g_s, d_s, nm_s, nv_s)]

    outs = [loss[0, 0], dx[None]]
    for kind in range(4):
        for name in ORDER:
            outs.append(big[name][kind][None] if name in BIG else small_out[kind][name])
    return tuple(outs)
```

```python
import jax
import jax.numpy as jnp
from jax import lax
from jax.experimental import pallas as pl
from jax.experimental.pallas import tpu as pltpu

F32 = jnp.float32
BF16 = jnp.bfloat16

D_MODEL = 1024
ATTN_WIDTH = 512
ATTN_HEAD_DIM = 64
ATTN_HEADS = 8
ATTN_BLOCK = 128
DILATIONS = (1, 4, 16)
HGRN_WIDTH = 512
HGRN_HEADS = 4
HGRN_HEAD_DIM = 128
HGRN_CHUNK = 64
IN_PROJ_WIDTH = 3584
D_FF = 4096
RMS_EPS = 1e-6
N_DEV = 8
ADAM_LR = 0.001
ADAM_B1 = 0.9
ADAM_B2 = 0.999
ADAM_EPS = 1e-08
ADAM_WD = 0.01
ADAM_STEP = 10

SUBLANES = 8
LANES = 128
COLUMN_UNROLL = 8
HGRN_CHUNKS_PER_STEP = 2
SUB_BLOCK = 16
TOKEN_TILE = 512
ELEMENTWISE_ROWS = 1024
MLP_TILE = 256
PROJ_TILE = 512
VMEM_BYTES_V7X = 64 * 1024 * 1024
VMEM_LIMIT = VMEM_BYTES_V7X // 8 * 7
NEG_BIG = -1e30
MESH = pl.DeviceIdType.MESH


def _params(ride=None, **kw):
    if ride is not None:
        kw["collective_id"] = ride.collective_id
    return pltpu.CompilerParams(vmem_limit_bytes=VMEM_LIMIT, **kw)


def _vmem_spec():
    return pl.BlockSpec(memory_space=pltpu.VMEM)


def _dot(a, b):
    return jnp.dot(a, b, preferred_element_type=F32)


def _dot_nt(a, b):
    return lax.dot_general(a, b, (((1,), (1,)), ((), ())), preferred_element_type=F32)


def _dot_tn(a, b):
    return lax.dot_general(a, b, (((0,), (0,)), ((), ())), preferred_element_type=F32)


def _sigmoid(x):
    return 1.0 / (1.0 + jnp.exp(-x))


def _rms_fwd(x, gain, width):
    r = lax.rsqrt(jnp.sum(x * x, axis=-1, keepdims=True) * (1.0 / width) + RMS_EPS)
    return x * r * gain


def _rms_bwd(dy, x, gain, width):
    r = lax.rsqrt(jnp.sum(x * x, axis=-1, keepdims=True) * (1.0 / width) + RMS_EPS)
    xhat = x * r
    dxhat = dy * gain
    dx = r * (dxhat - xhat * (jnp.sum(dxhat * xhat, axis=-1, keepdims=True) * (1.0 / width)))
    return dx, dy * xhat


def _split3(x):
    hi = x.astype(BF16)
    r1 = x - hi.astype(F32)
    mid = r1.astype(BF16)
    lo = (r1 - mid.astype(F32)).astype(BF16)
    return hi, mid, lo


def _tri_sum(tri_bf16, x):
    hi, mid, lo = _split3(x)
    return _dot(tri_bf16, hi) + _dot(tri_bf16, mid) + _dot(tri_bf16, lo)


def _dilated_spec(d, tm, width):
    return pl.BlockSpec((d, tm // d, width), lambda i: (0, i, 0))


def _lane_blocks(ref, value):
    for c in range(ref.shape[0]):
        ref[c] = value[:, c * LANES:(c + 1) * LANES]


def _to_dilated(src_ref, dst_ref, d, tm, cast=None):
    for r in range(d):
        for c in range(src_ref.shape[0]):
            v = src_ref[c] if d == 1 else src_ref[c, pl.ds(r, tm // d, stride=d), :]
            dst_ref[r, :, c * LANES:(c + 1) * LANES] = v if cast is None else v.astype(cast)


def _from_dilated(src_ref, scratch_ref, d, tm):
    if d == 1:
        return src_ref[0].astype(F32)
    nblk = scratch_ref.shape[0]
    for r in range(d):
        for c in range(nblk):
            scratch_ref[c, pl.ds(r, tm // d, stride=d), :] = src_ref[r, :, c * LANES:(c + 1) * LANES].astype(F32)
    return jnp.concatenate([scratch_ref[c] for c in range(nblk)], axis=1)


def in_proj_fwd(x, g1, w_in_b, ride=None):
    s = x.shape[0]
    tm = PROJ_TILE
    qkv_w = 3 * ATTN_WIDTH
    hg_w = IN_PROJ_WIDTH - qkv_w

    def body(x_ref, g_ref, w_ref, hg_ref, h_ref, *rest):
        qkv_refs, qkv_scr = rest[:len(DILATIONS)], rest[len(DILATIONS)]
        h = _rms_fwd(x_ref[...], g_ref[...], D_MODEL).astype(BF16)
        h_ref[...] = h
        proj = _dot(h, w_ref[...])
        hg_ref[...] = proj[:, qkv_w:]
        _lane_blocks(qkv_scr, proj[:, :qkv_w])
        for d, ref in zip(DILATIONS, qkv_refs):
            _to_dilated(qkv_scr, ref, d, tm, cast=BF16)

    n_steps = s // tm
    step = lambda k: (lambda: pl.program_id(0) == k)
    e_in, e_out, e_shape, e_scr, e_args = _ride_specs(ride)
    return pl.pallas_call(
        _riding(body, 3, 2 + len(DILATIONS), 1, ride, step(0), step(n_steps // 2), step(n_steps - 1),
                late=step(n_steps - 2)),
        name="in_proj_fwd",
        grid=(n_steps,),
        in_specs=[
            pl.BlockSpec((tm, D_MODEL), lambda i: (i, 0)),
            pl.BlockSpec((1, D_MODEL), lambda i: (0, 0)),
            _vmem_spec(),
        ] + e_in,
        out_specs=[
            pl.BlockSpec((tm, hg_w), lambda i: (i, 0)),
            pl.BlockSpec((tm, D_MODEL), lambda i: (i, 0)),
        ] + [_dilated_spec(d, tm, qkv_w) for d in DILATIONS] + e_out,
        out_shape=[jax.ShapeDtypeStruct((s, hg_w), F32), jax.ShapeDtypeStruct((s, D_MODEL), BF16)] + [
            jax.ShapeDtypeStruct((d, s // d, qkv_w), BF16) for d in DILATIONS] + e_shape,
        scratch_shapes=[pltpu.VMEM((qkv_w // LANES, tm, LANES), F32)] + e_scr,
        compiler_params=_params(ride, dimension_semantics=("arbitrary",)),
    )(x, g1, w_in_b, *e_args)


ATTN_SCALE = ATTN_HEAD_DIM ** -0.5


def _fill_attn_bias(bias_ref, dilation):
    qi = lax.broadcasted_iota(jnp.int32, (ATTN_BLOCK, 2 * ATTN_BLOCK), 0)
    kj = lax.broadcasted_iota(jnp.int32, (ATTN_BLOCK, 2 * ATTN_BLOCK), 1)
    dist = qi + ATTN_BLOCK - kj
    valid = (dist >= 0) & (dist <= ATTN_BLOCK)
    for head in range(ATTN_HEADS):
        slope = 2.0 ** (-8.0 * (head + 1) / ATTN_HEADS)
        bias = jnp.where(valid, dist.astype(F32) * (-slope * dilation), NEG_BIG)
        bias_ref[0, head] = bias
        bias_ref[1, head] = jnp.where(kj >= ATTN_BLOCK, bias, NEG_BIG)


def _stack_heads(x):
    low = _lane_half(x.shape, 0)
    zero = jnp.zeros_like(x)
    return jnp.concatenate([jnp.where(low, x, zero), jnp.where(low, zero, x)], axis=0)


def _unstack_heads(y):
    half = y.shape[0] // 2
    return jnp.where(_lane_half((half, y.shape[1]), 0), y[:half], y[half:])


def _attn_scores(q_stack, kcat, bias_ref, pair, first_block):
    f = first_block.astype(jnp.int32)
    bias = jnp.concatenate([bias_ref[f, 2 * pair], bias_ref[f, 2 * pair + 1]], axis=0)
    return _dot_nt(q_stack, kcat) + bias


def _lane_half(shape, sub):
    lane = lax.broadcasted_iota(jnp.int32, shape, 1)
    return (lane < ATTN_HEAD_DIM) if sub == 0 else (lane >= ATTN_HEAD_DIM)


def _sub_block(col, row):
    return pl.BlockSpec((None, ATTN_BLOCK, ATTN_WIDTH), lambda r, n: (r, row(n), col))


def attn_fwd(qkv, dilation):
    d, length, _ = qkv.shape
    assert d == dilation
    nb = length // ATTN_BLOCK

    def body(q_ref, kc_ref, kp_ref, vc_ref, vp_ref, o_ref, lse_ref, bias_ref):
        @pl.when((pl.program_id(0) == 0) & (pl.program_id(1) == 0))
        def _():
            _fill_attn_bias(bias_ref, d)

        first = pl.program_id(1) == 0
        for pair in range(ATTN_HEADS // 2):
            lanes = slice(pair * LANES, (pair + 1) * LANES)
            q_stack = _stack_heads(q_ref[:, lanes] * ATTN_SCALE)
            kcat = jnp.concatenate([kp_ref[:, lanes], kc_ref[:, lanes]], axis=0)
            vcat = jnp.concatenate([vp_ref[:, lanes], vc_ref[:, lanes]], axis=0)
            sc = _attn_scores(q_stack, kcat, bias_ref, pair, first)
            m = jnp.max(sc, axis=-1, keepdims=True)
            p = jnp.exp(sc - m)
            den = jnp.sum(p, axis=-1, keepdims=True)
            o_ref[:, lanes] = _unstack_heads(_dot(p.astype(BF16), vcat) / den).astype(BF16)
            lse_ref[:, lanes] = _unstack_heads(jnp.broadcast_to(m + jnp.log(den), (2 * ATTN_BLOCK, LANES)))

    cur = lambda n: n
    prev = lambda n: jnp.maximum(n - 1, 0)
    return pl.pallas_call(
        body,
        name=f"attn_fwd_d{d}",
        grid=(d, nb),
        in_specs=[_sub_block(0, cur), _sub_block(1, cur), _sub_block(1, prev), _sub_block(2, cur), _sub_block(2, prev)],
        out_specs=[_sub_block(0, cur), _sub_block(0, cur)],
        out_shape=[jax.ShapeDtypeStruct((d, length, ATTN_WIDTH), BF16), jax.ShapeDtypeStruct((d, length, ATTN_WIDTH), F32)],
        scratch_shapes=[pltpu.VMEM((2, ATTN_HEADS, ATTN_BLOCK, 2 * ATTN_BLOCK), F32)],
        compiler_params=_params(dimension_semantics=("arbitrary", "arbitrary")),
    )(qkv, qkv, qkv, qkv, qkv)


def attn_bwd(qkv, d_out, lse, delta, dilation, ride=None):
    d, length, _ = qkv.shape
    assert d == dilation
    nb = length // ATTN_BLOCK

    steps = d * nb + 1

    def body(q_ref, kc_ref, kp_ref, vc_ref, vp_ref, do_ref, lse_ref, dl_ref, dq_ref, dk_ref, dv_ref, ck_ref, cv_ref,
             bias_ref):
        t = pl.program_id(0)

        @pl.when(t == 0)
        def _():
            ck_ref[...] = jnp.zeros_like(ck_ref)
            cv_ref[...] = jnp.zeros_like(cv_ref)
            _fill_attn_bias(bias_ref, d)

        @pl.when(t < steps - 1)
        def _():
            first = t % nb == 0
            for pair in range(ATTN_HEADS // 2):
                lanes = slice(pair * LANES, (pair + 1) * LANES)
                q_stack = _stack_heads(q_ref[:, lanes] * ATTN_SCALE)
                do_stack = _stack_heads(do_ref[:, lanes])
                kcat = jnp.concatenate([kp_ref[:, lanes], kc_ref[:, lanes]], axis=0)
                vcat = jnp.concatenate([vp_ref[:, lanes], vc_ref[:, lanes]], axis=0)
                col_a, col_b = 2 * pair, 2 * pair + 1
                lse_col = jnp.concatenate([lse_ref[:, col_a:col_a + 1], lse_ref[:, col_b:col_b + 1]], axis=0)
                dl_col = jnp.concatenate([dl_ref[:, col_a:col_a + 1], dl_ref[:, col_b:col_b + 1]], axis=0)
                p = jnp.exp(_attn_scores(q_stack, kcat, bias_ref, pair, first) - lse_col)
                ds = (p * (_dot_nt(do_stack, vcat) - dl_col)).astype(BF16)
                dq_ref[:, lanes] = (_unstack_heads(_dot(ds, kcat)) * ATTN_SCALE).astype(BF16)
                dk_cat = _dot_tn(ds, q_stack)
                dv_cat = _dot_tn(p.astype(BF16), do_stack)
                dk_ref[:, lanes] = (ck_ref[:, lanes] + dk_cat[:ATTN_BLOCK]).astype(BF16)
                dv_ref[:, lanes] = (cv_ref[:, lanes] + dv_cat[:ATTN_BLOCK]).astype(BF16)
                ck_ref[:, lanes] = dk_cat[ATTN_BLOCK:]
                cv_ref[:, lanes] = dv_cat[ATTN_BLOCK:]

        @pl.when(t == steps - 1)
        def _():
            dk_ref[...] = ck_ref[...].astype(BF16)
            dv_ref[...] = cv_ref[...].astype(BF16)

    blk = (ATTN_BLOCK, ATTN_WIDTH)

    def spec(col, shift, width=ATTN_WIDTH):
        def index(t):
            f = jnp.minimum(t, steps - 2) if shift > -2 else jnp.maximum(t - 1, 0)
            r, n = f // nb, f % nb
            return (r, jnp.maximum(n - 1, 0) if shift == -1 else n, col)
        return pl.BlockSpec((None, ATTN_BLOCK, width), index)

    step = lambda k: (lambda: pl.program_id(0) == k)
    e_in, e_out, e_shape, e_scr, e_args = _ride_specs(ride)
    return pl.pallas_call(
        _riding(body, 8, 3, 3, ride, step(0), step(steps // 2), step(steps - 1)),
        name=f"attn_bwd_d{d}",
        grid=(steps,),
        in_specs=[spec(0, 0), spec(1, 0), spec(1, -1), spec(2, 0), spec(2, -1), spec(0, 0), spec(0, 0, LANES),
                  spec(0, 0, LANES)] + e_in,
        out_specs=[spec(0, 0), spec(0, -2), spec(0, -2)] + e_out,
        out_shape=[jax.ShapeDtypeStruct((d, length, ATTN_WIDTH), BF16)] * 3 + e_shape,
        scratch_shapes=[pltpu.VMEM(blk, F32), pltpu.VMEM(blk, F32),
                        pltpu.VMEM((2, ATTN_HEADS, ATTN_BLOCK, 2 * ATTN_BLOCK), F32)] + e_scr,
        compiler_params=_params(ride, dimension_semantics=("arbitrary",)),
    )(qkv, qkv, qkv, qkv, qkv, d_out, lse, delta, *e_args)


def _lower_bound(logits):
    return _sigmoid(logits[0:1, :] - logits[1:2, :])


def _hgrn_gates(q, fp, lb):
    sq = _sigmoid(q)
    qf = q * sq
    sig = _sigmoid(fp)
    sig_neg = _sigmoid(-fp)
    kf = (1.0 - lb) * sig_neg
    log_sig = jnp.minimum(fp, 0.0) - jnp.log(1.0 + jnp.exp(-jnp.abs(fp)))
    a = jnp.log(lb)
    c = jnp.log(1.0 - lb) + log_sig
    log_f = jnp.maximum(a, c) + jnp.log(1.0 + jnp.exp(-jnp.abs(a - c)))
    return sq, qf, (sig, sig_neg, c), log_f, kf


def _tril_bf16(n, upper=False):
    r = lax.broadcasted_iota(jnp.int32, (n, n), 0)
    c = lax.broadcasted_iota(jnp.int32, (n, n), 1)
    keep = (c >= r) if upper else (c <= r)
    return jnp.where(keep, 1.0, 0.0).astype(BF16)


def _hgrn_diagonal_loops(c_len, diagonal):
    for half in range(SUB_BLOCK // SUBLANES):
        def step(jj, carry, half=half):
            j = half * SUBLANES + jj
            for i in range(c_len // SUB_BLOCK):
                diagonal(slice(i * SUB_BLOCK + half * SUBLANES, (i + 1) * SUB_BLOCK), j, i * SUB_BLOCK + j)
            return carry

        lax.fori_loop(0, SUBLANES, step, 0, unroll=COLUMN_UNROLL)


def _hgrn_off_diagonal(b, qf, kf):
    c_len, width = b.shape
    edges = [b[0:1, :]] + [b[i * SUB_BLOCK - 1:i * SUB_BLOCK, :] for i in range(1, c_len // SUB_BLOCK)]
    eq = jnp.exp(b - jnp.concatenate([jnp.broadcast_to(e, (SUB_BLOCK, width)) for e in edges], axis=0))
    q_til = qf * eq
    k_til, ek = [], []
    for i in range(1, c_len // SUB_BLOCK):
        n = i * SUB_BLOCK
        e = jnp.exp(edges[i] - b[:n, :])
        ek.append(e)
        k_til.append(jnp.concatenate([kf[:n, :] * e, jnp.zeros((2 * c_len - n, width), F32)], axis=0))
    return q_til, k_til, eq, ek


def _split2(x):
    hi = x.astype(BF16)
    return hi, (x - hi.astype(F32)).astype(BF16)


def hgrn_fwd(proj, lb, ride=None):
    s = proj.shape[0]
    c_len, nh, hd = HGRN_CHUNK, HGRN_HEADS, HGRN_HEAD_DIM
    n_chunks = s // c_len
    col0 = 0

    cps = 2 * HGRN_CHUNKS_PER_STEP
    n_steps = n_chunks // cps

    def body(q_ref, f_ref, i_ref, lb_ref, o_ref, st_out_ref, a_out_ref, st_ref, b_ref, qf_ref, kf_ref, a_ref):
        @pl.when(pl.program_id(0) == 0)
        def _():
            st_ref[...] = jnp.zeros_like(st_ref)

        lbv = _lower_bound(lb_ref[...])
        for u in range(cps):
            rs = slice(u * c_len, (u + 1) * c_len)
            b_u, qf_u, kf_u, a_u = b_ref.at[u], qf_ref.at[u], kf_ref.at[u], a_ref.at[u]
            _, qf, _, log_f, kf = _hgrn_gates(q_ref[rs, :], f_ref[rs, :], lbv)
            b = _tri_sum(_tril_bf16(c_len), log_f)
            b_u[...] = b
            qf_u[...] = qf
            kf_u[...] = kf
            a_u[...] = jnp.zeros_like(a_u)

            def diagonal(rows, j, key, b_u=b_u, qf_u=qf_u, kf_u=kf_u, a_u=a_u):
                bj = b_u[pl.ds(key, 1), :]
                kj = kf_u[pl.ds(key, 1), :]
                nrow = rows.stop - rows.start
                t_loc = lax.broadcasted_iota(jnp.int32, (nrow, nh * hd), 0) + (rows.start % SUB_BLOCK)
                e = jnp.exp(jnp.where(t_loc >= j, b_u[rows, :] - bj, NEG_BIG))
                prod = qf_u[rows, :] * kj * e
                lane = lax.broadcasted_iota(jnp.int32, (nrow, hd), 1)
                for h in range(nh):
                    col = jnp.sum(prod[:, h * hd:(h + 1) * hd], axis=-1, keepdims=True)
                    a_u[h, rows, :] = jnp.where(lane == key, col, a_u[h, rows, :])

            _hgrn_diagonal_loops(c_len, diagonal)
            q_til, k_til, _, _ = _hgrn_off_diagonal(b, qf, kf)
            q_til = q_til.astype(BF16)
            k_til = [k.astype(BF16) for k in k_til]

            b_last = b[c_len - 1:c_len, :]
            qb = (qf * jnp.exp(b)).astype(BF16)
            kb2 = (kf * jnp.exp(b_last - b)).astype(BF16)
            vf = i_ref[rs, :].astype(BF16)
            for h in range(nh):
                hs = slice(h * hd, (h + 1) * hd)
                st = st_ref[h]
                st_out_ref[u, h] = st
                off = [jnp.zeros((SUB_BLOCK, hd), F32)]
                for i in range(1, c_len // SUB_BLOCK):
                    off.append(_dot_nt(q_til[i * SUB_BLOCK:(i + 1) * SUB_BLOCK, hs], k_til[i - 1][:, hs]))
                a_h = a_u[h] + jnp.concatenate(off, axis=0)
                a_out_ref[rs, hs] = a_h
                o_ref[rs, hs] = _dot_nt(qb[:, hs], st.astype(BF16)) + _dot(a_h[:, :c_len].astype(BF16), vf[:, hs])
                st_ref[h] = st * jnp.exp(b_last[:, hs]) + _dot_tn(vf[:, hs], kb2[:, hs])

    blk = (cps * c_len, HGRN_WIDTH)
    sblk = (cps, c_len, HGRN_WIDTH)
    step = lambda k: (lambda: pl.program_id(0) == k)
    e_in, e_out, e_shape, e_scr, e_args = _ride_specs(ride)
    return pl.pallas_call(
        _riding(body, 4, 3, 5, ride, step(0), step(n_steps // 2), step(n_steps - 1), late=step((3 * n_steps) // 4)),
        name="hgrn_fwd",
        grid=(n_steps,),
        in_specs=[
            pl.BlockSpec(blk, lambda c: (c, col0)),
            pl.BlockSpec(blk, lambda c: (c, col0 + 1)),
            pl.BlockSpec(blk, lambda c: (c, col0 + 2)),
            pl.BlockSpec((2, HGRN_WIDTH), lambda c: (0, 0)),
        ] + e_in,
        out_specs=[
            pl.BlockSpec(blk, lambda c: (c, 0)),
            pl.BlockSpec((cps, nh, hd, hd), lambda c: (c, 0, 0, 0)),
            pl.BlockSpec(blk, lambda c: (c, 0)),
        ] + e_out,
        out_shape=[
            jax.ShapeDtypeStruct((s, HGRN_WIDTH), F32),
            jax.ShapeDtypeStruct((n_chunks, nh, hd, hd), F32),
            jax.ShapeDtypeStruct((s, nh * hd), F32),
        ] + e_shape,
        scratch_shapes=[
            pltpu.VMEM((nh, hd, hd), F32),
            pltpu.VMEM(sblk, F32),
            pltpu.VMEM(sblk, F32),
            pltpu.VMEM(sblk, F32),
            pltpu.VMEM((cps, nh, c_len, hd), F32),
        ] + e_scr,
        compiler_params=_params(ride, dimension_semantics=("arbitrary",)),
    )(proj, proj, proj, lb, *e_args)


def hgrn_bwd(proj, lb, d_o, states, a_mat, ride=None):
    s = proj.shape[0]
    c_len, nh, hd = HGRN_CHUNK, HGRN_HEADS, HGRN_HEAD_DIM
    n_chunks = s // c_len
    col0 = 0
    cps = HGRN_CHUNKS_PER_STEP
    n_steps = n_chunks // cps
    last = n_steps - 1

    def body(q_ref, f_ref, i_ref, lb_ref, do_ref, st_in_ref, a_in_ref, dq_ref, df_ref, di_ref, dlb_ref,
             dst_ref, b_ref, qf_ref, kf_ref, da_ref, dqi_ref, dki_ref):
        @pl.when(pl.program_id(0) == 0)
        def _():
            dst_ref[...] = jnp.zeros_like(dst_ref)
            dlb_ref[...] = jnp.zeros_like(dlb_ref)

        lbv = _lower_bound(lb_ref[...])
        for u in reversed(range(cps)):
            rs = slice(u * c_len, (u + 1) * c_len)
            b_u, qf_u, kf_u, da_u, dqi_u, dki_u = (b_ref.at[u], qf_ref.at[u], kf_ref.at[u], da_ref.at[u], dqi_ref.at[u],
                                                   dki_ref.at[u])
            q = q_ref[rs, :]
            sq, qf, (sig, sig_neg, log_c), log_f, kf = _hgrn_gates(q, f_ref[rs, :], lbv)
            b = _tri_sum(_tril_bf16(c_len), log_f)
            b_u[...] = b
            qf_u[...] = qf
            kf_u[...] = kf
            b_last = b[c_len - 1:c_len, :]
            eb = jnp.exp(b)
            ebl = jnp.exp(b_last - b)
            qb = qf * eb
            kb2 = kf * ebl
            vf = i_ref[rs, :]
            d_o = do_ref[rs, :]
            qb_b, kb2_b, vf_b, do_b = qb.astype(BF16), kb2.astype(BF16), vf.astype(BF16), d_o.astype(BF16)
            tq = lax.broadcasted_iota(jnp.int32, (c_len, hd), 0)
            lane = lax.broadcasted_iota(jnp.int32, (c_len, hd), 1)

            dqb_parts, dvf_parts, dkb2_parts, dbl_parts = [], [], [], []
            for h in range(nh):
                hs = slice(h * hd, (h + 1) * hd)
                st = st_in_ref[u, h]
                dst = dst_ref[h]
                st_b, dst_b = st.astype(BF16), dst.astype(BF16)
                a_h = a_in_ref[rs, hs][:, :c_len].astype(BF16)
                dqb_parts.append(_dot(do_b[:, hs], st_b))
                dvf_parts.append(_dot_tn(a_h, do_b[:, hs]) + _dot_nt(kb2_b[:, hs], dst_b))
                dkb2_parts.append(_dot(vf_b[:, hs], dst_b))
                da = _dot_nt(do_b[:, hs], vf_b[:, hs])
                da = jnp.concatenate([da, jnp.zeros((c_len, hd - c_len), F32)], axis=1)
                da_u[h] = jnp.where(tq >= lane, da, 0.0)
                dbl_parts.append(jnp.sum(dst * st, axis=0, keepdims=True) * jnp.exp(b_last[:, hs]))
                dst_ref[h] = dst * jnp.exp(b_last[:, hs]) + _dot_tn(do_b[:, hs], qb_b[:, hs])
            dqb = jnp.concatenate(dqb_parts, axis=1)
            dvf = jnp.concatenate(dvf_parts, axis=1)
            dkb2 = jnp.concatenate(dkb2_parts, axis=1)
            dbl = jnp.concatenate(dbl_parts, axis=1) + jnp.sum(dkb2 * kb2, axis=0, keepdims=True)

            dqi_u[...] = jnp.zeros_like(dqi_u)
            t_idx = lax.broadcasted_iota(jnp.int32, (c_len, nh * hd), 0)

            def diagonal(rows, j, key, b_u=b_u, qf_u=qf_u, kf_u=kf_u, da_u=da_u, dqi_u=dqi_u, dki_u=dki_u):
                bj = b_u[pl.ds(key, 1), :]
                kj = kf_u[pl.ds(key, 1), :]
                nrow = rows.stop - rows.start
                t_loc = lax.broadcasted_iota(jnp.int32, (nrow, nh * hd), 0) + (rows.start % SUB_BLOCK)
                e = jnp.exp(jnp.where(t_loc >= j, b_u[rows, :] - bj, NEG_BIG))
                lane_r = lax.broadcasted_iota(jnp.int32, (nrow, hd), 1)
                cols = [jnp.sum(jnp.where(lane_r == key, da_u[h, rows, :], 0.0), axis=-1, keepdims=True)
                        for h in range(nh)]
                w = e * jnp.concatenate([jnp.broadcast_to(cc, (nrow, hd)) for cc in cols], axis=1)
                dqi_u[rows, :] += w * kj
                dki_u[pl.ds(key, 1), :] = jnp.sum(w * qf_u[rows, :], axis=0, keepdims=True)

            _hgrn_diagonal_loops(c_len, diagonal)

            q_til, k_til, eq, ek = _hgrn_off_diagonal(b, qf, kf)
            q_hi, q_lo = _split2(q_til)
            k_pairs = [_split2(k) for k in k_til]
            n_sub = c_len // SUB_BLOCK
            dq_heads, dk_heads = [], []
            for h in range(nh):
                hs = slice(h * hd, (h + 1) * hd)
                dq_rows = [jnp.zeros((SUB_BLOCK, hd), F32)]
                dk_h = jnp.zeros((c_len, hd), F32)
                for i in range(1, n_sub):
                    rows = slice(i * SUB_BLOCK, (i + 1) * SUB_BLOCK)
                    n = i * SUB_BLOCK
                    da_i = da_u[h, rows, :].astype(BF16)
                    k_hi, k_lo = k_pairs[i - 1]
                    dq_rows.append((_dot(da_i, k_hi[:, hs]) + _dot(da_i, k_lo[:, hs])) * eq[rows, hs])
                    dk_t = (_dot_tn(da_i, q_hi[rows, hs]) + _dot_tn(da_i, q_lo[rows, hs]))[:n, :] * ek[i - 1][:, hs]
                    dk_h = dk_h + jnp.concatenate([dk_t, jnp.zeros((c_len - n, hd), F32)], axis=0)
                dq_heads.append(jnp.concatenate(dq_rows, axis=0))
                dk_heads.append(dk_h)
            dq_intra = dqi_u[...] + jnp.concatenate(dq_heads, axis=1)
            dk_intra = dki_u[...] + jnp.concatenate(dk_heads, axis=1)

            db = dqb * qb + qf * dq_intra - kf * dk_intra - dkb2 * kb2
            db = db + jnp.where(t_idx == c_len - 1, dbl, 0.0)
            dg = _tri_sum(_tril_bf16(c_len, upper=True), db)
            dqf = dqb * eb + dq_intra
            dkf = dkb2 * ebl + dk_intra
            dq_ref[rs, :] = (dqf * (sq * (1.0 + q * (1.0 - sq)))).astype(BF16)
            df_ref[rs, :] = (sig_neg * (dg * jnp.exp(log_c - log_f) - dkf * (1.0 - lbv) * sig)).astype(BF16)
            di_ref[rs, :] = dvf.astype(BF16)
            dlb_ref[...] += jnp.sum(sig_neg * (dg * jnp.exp(-log_f) - dkf), axis=0, keepdims=True)

    blk = (cps * c_len, HGRN_WIDTH)
    sblk = (cps, c_len, HGRN_WIDTH)
    rev = lambda c: last - c
    step = lambda k: (lambda: pl.program_id(0) == k)
    e_in, e_out, e_shape, e_scr, e_args = _ride_specs(ride)
    return pl.pallas_call(
        _riding(body, 7, 4, 7, ride, step(0), step(n_steps // 2), step(last)),
        name="hgrn_bwd",
        grid=(n_steps,),
        in_specs=[
            pl.BlockSpec(blk, lambda c: (rev(c), col0)),
            pl.BlockSpec(blk, lambda c: (rev(c), col0 + 1)),
            pl.BlockSpec(blk, lambda c: (rev(c), col0 + 2)),
            pl.BlockSpec((2, HGRN_WIDTH), lambda c: (0, 0)),
            pl.BlockSpec(blk, lambda c: (rev(c), 0)),
            pl.BlockSpec((cps, nh, hd, hd), lambda c: (rev(c), 0, 0, 0)),
            pl.BlockSpec(blk, lambda c: (rev(c), 0)),
        ] + e_in,
        out_specs=[
            pl.BlockSpec(blk, lambda c: (rev(c), 0)),
            pl.BlockSpec(blk, lambda c: (rev(c), 0)),
            pl.BlockSpec(blk, lambda c: (rev(c), 0)),
            pl.BlockSpec((1, HGRN_WIDTH), lambda c: (0, 0)),
        ] + e_out,
        out_shape=[jax.ShapeDtypeStruct((s, HGRN_WIDTH), BF16)] * 3 + [jax.ShapeDtypeStruct((1, HGRN_WIDTH), F32)] + e_shape,
        scratch_shapes=[
            pltpu.VMEM((nh, hd, hd), F32),
            pltpu.VMEM(sblk, F32),
            pltpu.VMEM(sblk, F32),
            pltpu.VMEM(sblk, F32),
            pltpu.VMEM((cps, nh, c_len, hd), F32),
            pltpu.VMEM(sblk, F32),
            pltpu.VMEM(sblk, F32),
        ] + e_scr,
        compiler_params=_params(ride, dimension_semantics=("arbitrary",)),
    )(proj, proj, proj, lb, d_o, states, a_mat, *e_args)


def _per_head_lanes(x):
    lane = lax.broadcasted_iota(jnp.int32, (x.shape[0], LANES), 1)
    out = jnp.zeros((x.shape[0], LANES), F32)
    for h in range(ATTN_HEADS):
        out = jnp.where(lane == h, x[:, h * ATTN_HEAD_DIM:h * ATTN_HEAD_DIM + 1], out)
    return out


def _row_spec(tm, width, col=0):
    return pl.BlockSpec((tm, width), lambda i: (i, col))


def _const_spec(width):
    return pl.BlockSpec((1, width), lambda i: (0, 0))


def _acc_rows(ref, value):
    @pl.when(pl.program_id(0) == 0)
    def _():
        ref[...] = jnp.zeros_like(ref)

    ref[...] += jnp.sum(value, axis=0, keepdims=True)


def mix_fwd(attn_parts, o_h, proj, an, hn, w_out_b, gp, x, ride=None):
    s = x.shape[0]
    tm = TOKEN_TILE
    gate_col = 3
    hd = HGRN_HEAD_DIM
    nd = len(DILATIONS)

    def body(*refs):
        o_refs, l_refs = refs[:nd], refs[nd:2 * nd]
        oh_ref, gate_ref, an_ref, hn_ref, w_ref, gp_ref, x_ref = refs[2 * nd:2 * nd + 7]
        x1_ref, cat_ref, mixed_ref, attn_ref = refs[2 * nd + 7:2 * nd + 11]
        lse_refs = refs[2 * nd + 11:3 * nd + 11]
        o_scr, l_scr, lse_scr = refs[3 * nd + 11:]
        os_ = [_from_dilated(r, o_scr.at[k], d, tm) for k, (r, d) in enumerate(zip(o_refs, DILATIONS))]
        ls = [_from_dilated(r, l_scr.at[k], d, tm) for k, (r, d) in enumerate(zip(l_refs, DILATIONS))]
        m = jnp.maximum(jnp.maximum(ls[0], ls[1]), ls[2])
        es = [jnp.exp(l - m) for l in ls]
        den = es[0] + es[1] + es[2]
        attn = (es[0] * os_[0] + es[1] * os_[1] + es[2] * os_[2]) / den
        attn_ref[...] = attn
        lse_scr[0] = _per_head_lanes(m + jnp.log(den))
        for d, ref in zip(DILATIONS, lse_refs):
            _to_dilated(lse_scr, ref, d, tm)
        cat_ref[:, :ATTN_WIDTH] = _rms_fwd(attn, an_ref[...], ATTN_WIDTH).astype(BF16)
        gate = gate_ref[...]
        silu_g = gate * _sigmoid(gate)
        for h in range(HGRN_HEADS):
            hs = slice(h * hd, (h + 1) * hd)
            rec = _rms_fwd(oh_ref[:, hs], hn_ref[:, hs], hd) * silu_g[:, hs]
            cat_ref[:, ATTN_WIDTH + h * hd:ATTN_WIDTH + (h + 1) * hd] = rec.astype(BF16)
        mixed = _dot(cat_ref[...], w_ref[...])
        mixed_ref[...] = mixed
        x1_ref[...] = x_ref[...] + _rms_fwd(mixed, gp_ref[...], D_MODEL)

    aw = ATTN_WIDTH
    n_steps = s // tm
    step = lambda k: (lambda: pl.program_id(0) == k)
    e_in, e_out, e_shape, e_scr, e_args = _ride_specs(ride)
    return pl.pallas_call(
        _riding(body, 2 * nd + 7, 4 + nd, 3, ride, step(0), step((13 * n_steps) // 16), step(n_steps - 1)),
        name="mix_fwd",
        grid=(n_steps,),
        in_specs=[_dilated_spec(d, tm, aw) for d in DILATIONS] * 2 + [
            _row_spec(tm, aw), _row_spec(tm, aw, gate_col), _const_spec(aw), _const_spec(aw), _vmem_spec(),
            _const_spec(D_MODEL), _row_spec(tm, D_MODEL)] + e_in,
        out_specs=[_row_spec(tm, D_MODEL), _row_spec(tm, D_MODEL), _row_spec(tm, D_MODEL), _row_spec(tm, aw)] + [
            _dilated_spec(d, tm, LANES) for d in DILATIONS] + e_out,
        out_shape=[
            jax.ShapeDtypeStruct((s, D_MODEL), F32),
            jax.ShapeDtypeStruct((s, D_MODEL), BF16),
            jax.ShapeDtypeStruct((s, D_MODEL), F32),
            jax.ShapeDtypeStruct((s, aw), F32),
        ] + [jax.ShapeDtypeStruct((d, s // d, LANES), F32) for d in DILATIONS] + e_shape,
        scratch_shapes=[pltpu.VMEM((nd, aw // LANES, tm, LANES), F32), pltpu.VMEM((nd, aw // LANES, tm, LANES), F32),
                        pltpu.VMEM((1, tm, LANES), F32)] + e_scr,
        compiler_params=_params(ride, dimension_semantics=("arbitrary",)),
    )(*[p[0] for p in attn_parts], *[p[1] for p in attn_parts], o_h, proj, an, hn, w_out_b, gp, x, *e_args)


def mix_bwd(dx1, mixed, gp, w_out_b, attn, an, o_h, proj, hn):
    s = dx1.shape[0]
    tm = TOKEN_TILE
    gate_col = 3
    hd = HGRN_HEAD_DIM
    aw = ATTN_WIDTH

    nd = len(DILATIONS)

    def body(*refs):
        dx1_ref, mixed_ref, gp_ref, w_ref, attn_ref, an_ref, oh_ref, gate_ref, hn_ref, dmix_ref = refs[:10]
        do_refs, delta_refs = refs[10:10 + nd], refs[10 + nd:10 + 2 * nd]
        doh_ref, dgate_ref, dgp_ref, dan_ref, dhn_ref, do_ref, delta_ref = refs[10 + 2 * nd:]
        dmixed, gp_c = _rms_bwd(dx1_ref[...], mixed_ref[...], gp_ref[...], D_MODEL)
        _acc_rows(dgp_ref, gp_c)
        dmixed_b = dmixed.astype(BF16)
        dmix_ref[...] = dmixed_b
        dcat = _dot_nt(dmixed_b, w_ref[...])
        attn = attn_ref[...]
        d_o, an_c = _rms_bwd(dcat[:, :aw], attn, an_ref[...], aw)
        _acc_rows(dan_ref, an_c)
        _lane_blocks(do_ref, d_o)
        prod = d_o * attn
        lane = lax.broadcasted_iota(jnp.int32, (tm, LANES), 1)
        delta = jnp.zeros((tm, LANES), F32)
        for pair in range(ATTN_HEADS // 2):
            pp = prod[:, pair * LANES:(pair + 1) * LANES]
            low = _lane_half((tm, LANES), 0)
            lo = jnp.sum(jnp.where(low, pp, 0.0), axis=-1, keepdims=True)
            hi = jnp.sum(jnp.where(low, 0.0, pp), axis=-1, keepdims=True)
            delta = jnp.where(lane == 2 * pair, lo, jnp.where(lane == 2 * pair + 1, hi, delta))
        delta_ref[0] = delta
        for d, o_ref, l_ref in zip(DILATIONS, do_refs, delta_refs):
            _to_dilated(do_ref, o_ref, d, tm, cast=BF16)
            _to_dilated(delta_ref, l_ref, d, tm)
        gate = gate_ref[...]
        sg = _sigmoid(gate)
        silu_g = gate * sg
        drec = dcat[:, aw:]
        hn_parts = []
        for h in range(HGRN_HEADS):
            hs = slice(h * hd, (h + 1) * hd)
            oh = oh_ref[:, hs]
            on = _rms_fwd(oh, hn_ref[:, hs], hd)
            dgate_ref[:, hs] = (drec[:, hs] * on * (sg[:, hs] * (1.0 + gate[:, hs] * (1.0 - sg[:, hs])))).astype(BF16)
            d_oh, hn_c = _rms_bwd(drec[:, hs] * silu_g[:, hs], oh, hn_ref[:, hs], hd)
            doh_ref[:, hs] = d_oh
            hn_parts.append(hn_c)
        _acc_rows(dhn_ref, jnp.concatenate(hn_parts, axis=1))

    return pl.pallas_call(
        body,
        name="mix_bwd",
        grid=(s // tm,),
        in_specs=[_row_spec(tm, D_MODEL), _row_spec(tm, D_MODEL), _const_spec(D_MODEL), _vmem_spec(), _row_spec(tm, aw),
                  _const_spec(aw), _row_spec(tm, aw), _row_spec(tm, aw, gate_col), _const_spec(aw)],
        out_specs=[_row_spec(tm, D_MODEL)] + [_dilated_spec(d, tm, aw) for d in DILATIONS] + [
            _dilated_spec(d, tm, LANES) for d in DILATIONS] + [_row_spec(tm, aw)] * 2 + [
            _const_spec(D_MODEL), _const_spec(aw), _const_spec(aw)],
        out_shape=[jax.ShapeDtypeStruct((s, D_MODEL), BF16)] + [
            jax.ShapeDtypeStruct((d, s // d, aw), BF16) for d in DILATIONS] + [
            jax.ShapeDtypeStruct((d, s // d, LANES), F32) for d in DILATIONS] + [
            jax.ShapeDtypeStruct((s, aw), F32), jax.ShapeDtypeStruct((s, aw), BF16),
            jax.ShapeDtypeStruct((1, D_MODEL), F32), jax.ShapeDtypeStruct((1, aw), F32),
            jax.ShapeDtypeStruct((1, aw), F32)],
        scratch_shapes=[pltpu.VMEM((aw // LANES, tm, LANES), F32), pltpu.VMEM((1, tm, LANES), F32)],
        compiler_params=_params(dimension_semantics=("arbitrary",)),
    )(dx1, mixed, gp, w_out_b, attn, an, o_h, proj, hn)


def mlp_fwd_bwd(x1, g_pre, w1_blocks, w2_b, g_post, target):
    s = x1.shape[0]
    tm = MLP_TILE
    nblk, _, fb = w1_blocks.shape

    def body(x1_ref, gpre_ref, w1_ref, w2_ref, gpost_ref, t_ref,
             dx1_ref, h2_ref, a_ref, du_ref, dff_ref, loss_ref, dgpre_ref, dgpost_ref, u_ref):
        x1v = x1_ref[...]
        h2 = _rms_fwd(x1v, gpre_ref[...], D_MODEL).astype(BF16)
        h2_ref[...] = h2
        ff = jnp.zeros((tm, D_MODEL), F32)
        for j in range(nblk):
            cols = slice(j * fb, (j + 1) * fb)
            ru = jnp.maximum(_dot(h2, w1_ref[j]), 0.0)
            u_ref[:, cols] = ru.astype(BF16)
            a = (ru * ru).astype(BF16)
            a_ref[:, cols] = a
            ff = ff + _dot(a, w2_ref[cols, :])
        diff = x1v + _rms_fwd(ff, gpost_ref[...], D_MODEL) - t_ref[...]
        _acc_rows(loss_ref, diff * diff)
        dy = diff * (1.0 / D_MODEL)
        dff, gpost_c = _rms_bwd(dy, ff, gpost_ref[...], D_MODEL)
        _acc_rows(dgpost_ref, gpost_c)
        dff_b = dff.astype(BF16)
        dff_ref[...] = dff_b
        dh2 = jnp.zeros((tm, D_MODEL), F32)
        for j in range(nblk):
            cols = slice(j * fb, (j + 1) * fb)
            du = (_dot_nt(dff_b, w2_ref[cols, :]) * (2.0 * u_ref[:, cols])).astype(BF16)
            du_ref[:, cols] = du
            dh2 = dh2 + _dot_nt(du, w1_ref[j])
        dxa, gpre_c = _rms_bwd(dh2, x1v, gpre_ref[...], D_MODEL)
        _acc_rows(dgpre_ref, gpre_c)
        dx1_ref[...] = dy + dxa

    dm = D_MODEL
    return pl.pallas_call(
        body,
        name="mlp_fwd_bwd",
        grid=(s // tm,),
        in_specs=[_row_spec(tm, dm), _const_spec(dm), _vmem_spec(), _vmem_spec(), _const_spec(dm), _row_spec(tm, dm)],
        out_specs=[_row_spec(tm, dm), _row_spec(tm, dm), _row_spec(tm, D_FF), _row_spec(tm, D_FF), _row_spec(tm, dm),
                   _const_spec(dm), _const_spec(dm), _const_spec(dm)],
        out_shape=[
            jax.ShapeDtypeStruct((s, dm), F32),
            jax.ShapeDtypeStruct((s, dm), BF16),
            jax.ShapeDtypeStruct((s, D_FF), BF16),
            jax.ShapeDtypeStruct((s, D_FF), BF16),
            jax.ShapeDtypeStruct((s, dm), BF16),
            jax.ShapeDtypeStruct((1, dm), F32),
            jax.ShapeDtypeStruct((1, dm), F32),
            jax.ShapeDtypeStruct((1, dm), F32),
        ],
        scratch_shapes=[pltpu.VMEM((tm, D_FF), BF16)],
        compiler_params=_params(dimension_semantics=("arbitrary",)),
    )(x1, g_pre, w1_blocks, w2_b, g_post, target)


def in_proj_bwd(attn_grads, hgrn_grads, dgate, w_in_b, x, g1, dx1):
    s = x.shape[0]
    tm = PROJ_TILE
    aw = ATTN_WIDTH
    n_attn = len(attn_grads)
    flat = [g[k] for k in range(3) for g in attn_grads] + list(hgrn_grads) + [dgate]

    def body(*refs):
        parts = refs[:len(flat)]
        w_ref, x_ref, g_ref, dx1_ref, dx_ref, dproj_ref, dg_ref, scr = refs[len(flat):]
        groups = []
        for k in range(3):
            acc = None
            for p, d in zip(parts[k * n_attn:(k + 1) * n_attn], DILATIONS):
                v = _from_dilated(p, scr, d, tm)
                acc = v if acc is None else acc + v
            groups.append(acc)
        groups += [p[...] for p in parts[3 * n_attn:]]
        dh = jnp.zeros((tm, D_MODEL), F32)
        for gi, grp in enumerate(groups):
            cols = slice(gi * aw, (gi + 1) * aw)
            gb = grp.astype(BF16)
            dproj_ref[:, cols] = gb
            dh = dh + _dot_nt(gb, w_ref[:, cols])
        dxa, g_c = _rms_bwd(dh, x_ref[...], g_ref[...], D_MODEL)
        _acc_rows(dg_ref, g_c)
        dx_ref[...] = dx1_ref[...] + dxa

    dm = D_MODEL
    return pl.pallas_call(
        body,
        name="in_proj_bwd",
        grid=(s // tm,),
        in_specs=[_dilated_spec(d, tm, aw) for d in DILATIONS] * 3 + [_row_spec(tm, aw)] * 4 + [
            _vmem_spec(), _row_spec(tm, dm), _const_spec(dm), _row_spec(tm, dm)],
        out_specs=[_row_spec(tm, dm), _row_spec(tm, IN_PROJ_WIDTH), _const_spec(dm)],
        out_shape=[jax.ShapeDtypeStruct((s, dm), F32), jax.ShapeDtypeStruct((s, IN_PROJ_WIDTH), BF16),
                   jax.ShapeDtypeStruct((1, dm), F32)],
        scratch_shapes=[pltpu.VMEM((aw // LANES, tm, LANES), F32)],
        compiler_params=_params(dimension_semantics=("arbitrary",)),
    )(*flat, w_in_b, x, g1, dx1)


def wgrad(a_b, b_b, tn, name, ts=2048, per_step=1, ride=None):
    s, k = a_b.shape
    n = b_b.shape[1]

    def body(a_ref, b_ref, o_ref):
        @pl.when(pl.program_id(1) == 0)
        def _():
            o_ref[...] = jnp.zeros_like(o_ref)

        a = a_ref[...]
        for jj in range(per_step):
            o_ref[jj] += _dot_tn(a, b_ref[:, jj * tn:(jj + 1) * tn])

    wide = tn * per_step
    gn, gs = n // wide, s // ts
    step = lambda j, i: (lambda: (pl.program_id(0) == j) & (pl.program_id(1) == i))
    e_in, e_out, e_shape, e_scr, e_args = _ride_specs(ride)
    out = pl.pallas_call(
        _riding(body, 2, 1, 0, ride, step(0, 0), step(gn // 2, 0), step(gn - 1, gs - 1)),
        name=name,
        grid=(gn, gs),
        in_specs=[pl.BlockSpec((ts, k), lambda j, i: (i, 0)), pl.BlockSpec((ts, wide), lambda j, i: (i, j))] + e_in,
        out_specs=[pl.BlockSpec((per_step, k, tn), lambda j, i: (j, 0, 0))] + e_out,
        out_shape=[jax.ShapeDtypeStruct((n // tn, k, tn), F32)] + e_shape,
        scratch_shapes=e_scr,
        compiler_params=_params(ride, dimension_semantics=("arbitrary", "arbitrary")),
    )(a_b, b_b, *e_args)
    return out[0] if ride is None else out


def train_step(x, target, g1, an, logits, hn, gp, g_pre, g_post, w, m, v):
    nd = len(DILATIONS)
    shard_b = {k: w[k].astype(BF16) for k in BIG}
    (w_in_g,) = run_exchange(gather_exchange([shard_b["w_in"]]), "gather_w_in")
    w_in_b = w_in_g.transpose(1, 0, 2).reshape(D_MODEL, IN_PROJ_WIDTH)

    proj, h_b, *qkvs, w2_g = in_proj_fwd(x, g1, w_in_b, ride=gather_exchange([shard_b["w_ff2"]]))
    w2_b = w2_g.reshape(D_FF, D_MODEL)
    attn_parts = [attn_fwd(qkv, d) for qkv, d in zip(qkvs, DILATIONS)]
    o_h, states, a_mat, w_out_g, w1_blocks = hgrn_fwd(
        proj, logits, ride=gather_exchange([shard_b["w_out"], shard_b["w_ff1"]]))
    w_out_b = w_out_g.reshape(D_MODEL, D_MODEL)
    x1, cat_b, mixed, attn, *lses = mix_fwd(attn_parts, o_h, proj, an, hn, w_out_b, gp, x)
    dx1, h2_b, a_b, du_b, dff_b, loss_vec, dg_pre, dg_post = mlp_fwd_bwd(x1, g_pre, w1_blocks, w2_b, g_post, target)
    dw2 = wgrad(a_b, dff_b, D_MODEL, "wgrad_ff2", ts=512)
    dw1 = wgrad(h2_b, du_b, D_FF // N_DEV, "wgrad_ff1", per_step=2)
    dmix_b, *rest = mix_bwd(dx1, mixed, gp, w_out_b, attn, an, o_h, proj, hn)
    d_os, deltas = rest[:nd], rest[nd:2 * nd]
    d_oh, dgate, dgp, dan, dhn = rest[2 * nd:]
    dwout = wgrad(cat_b, dmix_b, D_MODEL, "wgrad_out")

    early = ("w_out", "w_ff1", "w_ff2")
    early_grads = [dwout.reshape(N_DEV, D_MODEL // N_DEV, D_MODEL), dw1, dw2.reshape(N_DEV, D_FF // N_DEV, D_MODEL)]
    res = attn_bwd(qkvs[0], d_os[0], lses[0], deltas[0], DILATIONS[0], ride=to_core_exchange(early_grads))
    pairs = [pair_sum(g, s, f"pair_sum_{name}") for g, s, name in zip(early_grads, res[3:], early)]
    attn_grads = [res[:3]]
    *res, others_ff2 = attn_bwd(qkvs[1], d_os[1], lses[1], deltas[1], DILATIONS[1],
                                ride=to_chip_exchange([pairs[2][1]]))
    attn_grads.append(res)
    attn_grads.append(attn_bwd(qkvs[2], d_os[2], lses[2], deltas[2], DILATIONS[2]))
    dq_h, df_h, di_h, dlb, *others = hgrn_bwd(proj, logits, d_oh, states, a_mat,
                                              ride=to_chip_exchange([pairs[0][1], pairs[1][1]]))
    others.append(others_ff2)
    dx, dproj_b, dg1 = in_proj_bwd(attn_grads, (dq_h, df_h, di_h), dgate, w_in_b, x, g1, dx1)
    packed = _pack_small(dg1, dgp, dg_pre, dg_post, dan, dhn, dlb, loss_vec)
    dwin, small_slots = wgrad(h_b, dproj_b, 2 * IN_PROJ_WIDTH // N_DEV, "wgrad_in",
                              ride=small_exchange(packed))
    big = {name: sum_adamw(p[0], o, w[name], m[name], v[name], f"sum_adamw_{name}")
           for name, p, o in zip(early, pairs, others)}

    shard_w = IN_PROJ_WIDTH // N_DEV
    dwin_blocks = dwin.reshape(N_DEV // 2, D_MODEL, 2, shard_w).transpose(0, 2, 1, 3).reshape(N_DEV, D_MODEL, shard_w)
    pair_in, others_in = reduce_last(dwin_blocks)
    big["w_in"] = sum_adamw(pair_in, others_in, w["w_in"], m["w_in"], v["w_in"], "sum_adamw_w_in")
    return dx, big, small_slots


def _position():
    x, y, c = lax.axis_index("x"), lax.axis_index("y"), lax.axis_index("c")
    other_chips = [(1 - x, y), (x, 1 - y), (1 - x, 1 - y)]
    return x, y, c, other_chips


def _any_spec():
    return pl.BlockSpec(memory_space=pl.ANY)


class Exchange:
    def __init__(self, arrays, out_shape, sems, stages, collective_id, peers):
        self.arrays, self.out_shape, self.sems, self.stages = list(arrays), list(out_shape), list(sems), stages
        self.collective_id, self.peers = collective_id, peers

    def open(self):
        barrier = pltpu.get_barrier_semaphore()
        peers = self.peers()
        for peer in peers:
            pl.semaphore_signal(barrier, inc=1, device_id=peer, device_id_type=MESH)
        pl.semaphore_wait(barrier, len(peers))


def _siblings():
    x, y, c, _ = _position()
    return [(x, y, 1 - c)]


def _same_core_of_other_chips():
    x, y, c, chips = _position()
    return [(px, py, c) for px, py in chips]


def _gather_peers():
    x, y, c, _ = _position()
    return [(x, y, 1 - c), (1 - x, y, c), (x, 1 - y, c)]


def _all_others():
    x, y, c, _ = _position()
    return [(1 - x if rel & 4 else x, 1 - y if rel & 2 else y, 1 - c if rel & 1 else c) for rel in range(1, N_DEV)]


def gather_exchange(shards):
    n = len(shards)
    halves = [sh.shape[0] // 2 for sh in shards]

    def stages(ins, outs, sems):
        send_sems, recv_sems, local_sems = sems

        def parts():
            x, y, c, _ = _position()
            me, sibling = (x, y, c), (x, y, 1 - c)
            nbr_x, nbr_y, diag = (1 - x, y, c), (x, 1 - y, c), (1 - x, 1 - y, c)

            def slot(a, dev, rows=None):
                ref = outs[a].at[4 * dev[0] + 2 * dev[1] + dev[2]]
                return ref if rows is None else ref.at[rows]

            def copy(a, k, block, to, rows=None, src=None):
                return pltpu.make_async_remote_copy(
                    src_ref=slot(a, block, rows) if src is None else src, dst_ref=slot(a, block, rows),
                    send_sem=send_sems.at[a, k], recv_sem=recv_sems.at[a, k], device_id=to, device_id_type=MESH)

            upper = lambda a: pl.ds(0, halves[a])
            lower = lambda a: pl.ds(halves[a], halves[a])
            return me, sibling, nbr_x, nbr_y, diag, slot, copy, upper, lower

        def begin():
            me, sibling, nbr_x, nbr_y, _, slot, copy, _, _ = parts()
            for a in range(n):
                pltpu.make_async_copy(ins[a], slot(a, me), local_sems.at[a]).start()
                for k, to in enumerate((sibling, nbr_x, nbr_y)):
                    copy(a, k, me, to, src=ins[a]).start()

        def middle():
            me, sibling, nbr_x, nbr_y, _, _, copy, upper, lower = parts()
            for a in range(n):
                copy(a, 1, nbr_x, me).wait_recv()
                copy(a, 3, nbr_x, sibling).start()
                copy(a, 5, nbr_x, nbr_y, rows=lower(a)).start()
                copy(a, 2, nbr_y, me).wait_recv()
                copy(a, 4, nbr_y, sibling).start()
                copy(a, 6, nbr_y, nbr_x, rows=upper(a)).start()

        def late():
            me, sibling, _, _, diag, _, copy, upper, lower = parts()
            for a in range(n):
                copy(a, 6, diag, me, rows=upper(a)).wait_recv()
                copy(a, 5, diag, me, rows=lower(a)).wait_recv()
                copy(a, 7, diag, sibling).start()

        def end():
            me, sibling, nbr_x, nbr_y, diag, slot, copy, upper, lower = parts()
            sib = lambda dev: (dev[0], dev[1], sibling[2])
            for a in range(n):
                for k, block in ((0, sibling), (3, sib(nbr_x)), (4, sib(nbr_y)), (7, sib(diag))):
                    copy(a, k, block, me).wait_recv()
                copy(a, 0, me, sibling, src=ins[a]).wait_send()
                copy(a, 1, me, nbr_x, src=ins[a]).wait_send()
                copy(a, 2, me, nbr_y, src=ins[a]).wait_send()
                copy(a, 3, nbr_x, sibling).wait_send()
                copy(a, 4, nbr_y, sibling).wait_send()
                copy(a, 5, nbr_x, nbr_y, rows=lower(a)).wait_send()
                copy(a, 6, nbr_y, nbr_x, rows=upper(a)).wait_send()
                copy(a, 7, diag, sibling).wait_send()
                pltpu.make_async_copy(ins[a], slot(a, me), local_sems.at[a]).wait()

        return begin, (middle, late), end

    return Exchange(
        shards, [jax.ShapeDtypeStruct((N_DEV,) + sh.shape, sh.dtype) for sh in shards],
        [pltpu.SemaphoreType.DMA((n, 8)), pltpu.SemaphoreType.DMA((n, 8)), pltpu.SemaphoreType.DMA((n,))], stages,
        collective_id=0, peers=_gather_peers)


def to_core_exchange(grads):
    n = len(grads)

    def stages(ins, outs, sems):
        send_sems, recv_sems = sems

        def copies():
            x, y, c, _ = _position()
            return [pltpu.make_async_remote_copy(
                src_ref=ins[a].at[2 * q + (1 - c)], dst_ref=outs[a].at[q], send_sem=send_sems.at[a, q],
                recv_sem=recv_sems.at[a, q], device_id=(x, y, 1 - c), device_id_type=MESH)
                for a in range(n) for q in range(4)]

        def begin():
            for cp in copies():
                cp.start()

        def end():
            for cp in copies():
                cp.wait()

        return begin, None, end

    return Exchange(grads, [jax.ShapeDtypeStruct((4,) + g.shape[1:], g.dtype) for g in grads],
                    [pltpu.SemaphoreType.DMA((n, 4)), pltpu.SemaphoreType.DMA((n, 4))], stages,
                    collective_id=1, peers=_siblings)


def pair_sum(grad, from_sibling, name):
    _, r, cdim = grad.shape
    tr = min(r, ELEMENTWISE_ROWS)
    c_idx = lax.axis_index("c").astype(jnp.int32).reshape(1)

    def body(c_ref, g_ref, s_ref, o_ref, ob_ref):
        total = g_ref[...] + s_ref[...]
        o_ref[...] = total
        ob_ref[...] = total.astype(BF16)

    blk = lambda: pl.BlockSpec((1, tr, cdim), lambda q, i, cr: (q, i, 0))
    return pl.pallas_call(
        body,
        name=name,
        grid_spec=pltpu.PrefetchScalarGridSpec(
            num_scalar_prefetch=1,
            grid=(4, r // tr),
            in_specs=[pl.BlockSpec((1, tr, cdim), lambda q, i, cr: (2 * q + cr[0], i, 0)), blk()],
            out_specs=[blk(), blk()],
        ),
        out_shape=[jax.ShapeDtypeStruct((4, r, cdim), F32), jax.ShapeDtypeStruct((4, r, cdim), BF16)],
        compiler_params=_params(dimension_semantics=("arbitrary", "arbitrary")),
    )(c_idx, grad, from_sibling)


def to_chip_exchange(pairs):
    n = len(pairs)

    def stages(ins, outs, sems):
        send_sems, recv_sems = sems

        def copies():
            x, y, c, chips = _position()
            return [pltpu.make_async_remote_copy(
                src_ref=ins[a].at[2 * px + py], dst_ref=outs[a].at[j], send_sem=send_sems.at[a, j],
                recv_sem=recv_sems.at[a, j], device_id=(px, py, c), device_id_type=MESH)
                for a in range(n) for j, (px, py) in enumerate(chips)]

        def begin():
            for cp in copies():
                cp.start()

        def end():
            for cp in copies():
                cp.wait()

        return begin, None, end

    return Exchange(pairs, [jax.ShapeDtypeStruct((3,) + p.shape[1:], p.dtype) for p in pairs],
                    [pltpu.SemaphoreType.DMA((n, 3)), pltpu.SemaphoreType.DMA((n, 3))], stages,
                    collective_id=2, peers=_same_core_of_other_chips)


def run_exchange(ex, name):
    n_in, n_out = len(ex.arrays), len(ex.out_shape)

    def body(*refs):
        begin, middle, end = ex.stages(refs[:n_in], refs[n_in:n_in + n_out], refs[n_in + n_out:])
        ex.open()
        begin()
        for stage in _as_tuple(middle):
            stage()
        end()

    return pl.pallas_call(
        body,
        name=name,
        in_specs=[_any_spec()] * n_in,
        out_specs=[_any_spec()] * n_out,
        out_shape=ex.out_shape,
        scratch_shapes=ex.sems,
        compiler_params=pltpu.CompilerParams(collective_id=ex.collective_id),
    )(*ex.arrays)


def _as_tuple(stages):
    return () if stages is None else stages if isinstance(stages, tuple) else (stages,)


def _riding(body, n_in, n_out, n_scratch, ex, first, middle, last, late=None):
    if ex is None:
        return body
    r_in, r_out = len(ex.arrays), len(ex.out_shape)

    def wrapped(*refs):
        k_in, refs = refs[:n_in], refs[n_in:]
        e_in, refs = refs[:r_in], refs[r_in:]
        k_out, refs = refs[:n_out], refs[n_out:]
        e_out, refs = refs[:r_out], refs[r_out:]
        k_scr, e_sems = refs[:n_scratch], refs[n_scratch:]
        begin, mid, end = ex.stages(e_in, e_out, e_sems)

        @pl.when(first())
        def _():
            ex.open()
            begin()

        body(*k_in, *k_out, *k_scr)
        for stage, at in zip(_as_tuple(mid), (middle, late or last)):
            pl.when(at())(stage)
        pl.when(last())(end)

    return wrapped


def _ride_specs(ex):
    if ex is None:
        return [], [], [], [], []
    return [_any_spec()] * len(ex.arrays), [_any_spec()] * len(ex.out_shape), ex.out_shape, ex.sems, ex.arrays


def reduce_last(grad):
    _, r, cdim = grad.shape

    def body(g_hbm, own_ref, others_hbm, g_buf, to_sib, from_sib, send_buf, load_sem, send_sems, recv_sems):
        x, y, c, chips = _position()
        order = chips + [(x, y)]
        n_other = len(chips)
        sibling = (x, y, 1 - c)
        barrier = pltpu.get_barrier_semaphore()
        peers = [sibling] + [(px, py, c) for px, py in chips]
        for peer in peers:
            pl.semaphore_signal(barrier, inc=1, device_id=peer, device_id_type=MESH)
        pl.semaphore_wait(barrier, len(peers))

        def load(block):
            cp = pltpu.make_async_copy(g_hbm.at[block], g_buf, load_sem)
            cp.start()
            cp.wait()
            return g_buf[...]

        def to_sibling(j):
            return pltpu.make_async_remote_copy(
                src_ref=to_sib.at[j], dst_ref=from_sib.at[j], send_sem=send_sems.at[n_other + j],
                recv_sem=recv_sems.at[n_other + j], device_id=sibling, device_id_type=MESH)

        def to_chip(j):
            px, py = chips[j]
            return pltpu.make_async_remote_copy(
                src_ref=send_buf.at[j], dst_ref=others_hbm.at[j], send_sem=send_sems.at[j], recv_sem=recv_sems.at[j],
                device_id=(px, py, c), device_id_type=MESH)

        for j, (px, py) in enumerate(order):
            to_sib[j] = load(2 * (2 * px + py) + (1 - c)).astype(BF16)
            to_sibling(j).start()
        for j, (px, py) in enumerate(order):
            to_sibling(j).wait_recv()
            total = load(2 * (2 * px + py) + c) + from_sib[j].astype(F32)
            if j < n_other:
                send_buf[j] = total.astype(BF16)
                to_chip(j).start()
            else:
                own_ref[0] = total
        for j in range(len(order)):
            to_sibling(j).wait_send()
        for j in range(n_other):
            to_chip(j).wait()

    n_blocks = N_DEV // 2
    return pl.pallas_call(
        body,
        name="reduce_w_in",
        in_specs=[_any_spec()],
        out_specs=[_vmem_spec(), _any_spec()],
        out_shape=[jax.ShapeDtypeStruct((1, r, cdim), F32), jax.ShapeDtypeStruct((n_blocks - 1, r, cdim), BF16)],
        scratch_shapes=[pltpu.VMEM((r, cdim), F32), pltpu.VMEM((n_blocks, r, cdim), BF16),
                        pltpu.VMEM((n_blocks, r, cdim), BF16), pltpu.VMEM((n_blocks - 1, r, cdim), BF16),
                        pltpu.SemaphoreType.DMA(()), pltpu.SemaphoreType.DMA((2 * n_blocks - 1,)),
                        pltpu.SemaphoreType.DMA((2 * n_blocks - 1,))],
        compiler_params=pltpu.CompilerParams(collective_id=4, vmem_limit_bytes=VMEM_LIMIT),
    )(grad)


def _adamw(w, g, m, v):
    m = ADAM_B1 * m + (1.0 - ADAM_B1) * g
    v = ADAM_B2 * v + (1.0 - ADAM_B2) * (g * g)
    m_hat = m / (1.0 - ADAM_B1 ** ADAM_STEP)
    v_hat = v / (1.0 - ADAM_B2 ** ADAM_STEP)
    delta = -ADAM_LR * (m_hat / (jnp.sqrt(v_hat) + ADAM_EPS) + ADAM_WD * w)
    return delta, m, v


def sum_adamw(pairs, others, w, m, v, name):
    r, cdim = w.shape
    tr = min(r, ELEMENTWISE_ROWS // 2)
    if pairs.shape[0] == 1:
        chip_idx = jnp.zeros((1,), jnp.int32)
    else:
        chip_idx = (2 * lax.axis_index("x") + lax.axis_index("y")).astype(jnp.int32).reshape(1)

    def body(q_ref, p_ref, o_ref, w_ref, m_ref, v_ref, g_out, d_out, m_out, v_out):
        g = p_ref[0] + o_ref[0].astype(F32) + o_ref[1].astype(F32) + o_ref[2].astype(F32)
        g_out[...] = g
        d_out[...], m_out[...], v_out[...] = _adamw(w_ref[...], g, m_ref[...], v_ref[...])

    tile = lambda: pl.BlockSpec((tr, cdim), lambda i, qr: (i, 0))
    return pl.pallas_call(
        body,
        name=name,
        grid_spec=pltpu.PrefetchScalarGridSpec(
            num_scalar_prefetch=1,
            grid=(r // tr,),
            in_specs=[pl.BlockSpec((1, tr, cdim), lambda i, qr: (qr[0], i, 0)),
                      pl.BlockSpec((3, tr, cdim), lambda i, qr: (0, i, 0)), tile(), tile(), tile()],
            out_specs=[tile(), tile(), tile(), tile()],
        ),
        out_shape=[jax.ShapeDtypeStruct((r, cdim), F32)] * 4,
        compiler_params=_params(dimension_semantics=("arbitrary",)),
    )(chip_idx, pairs, others, w, m, v)


def small_exchange(packed):
    def stages(ins, outs, sems):
        send_sems, recv_sems, local_sem = sems
        (src,), (slots,) = ins, outs

        def copies():
            x, y, c, _ = _position()
            my_id = 4 * x + 2 * y + c
            sends, landings = [], []
            for rel in range(1, N_DEV):
                px = 1 - x if (rel >> 2) & 1 else x
                py = 1 - y if (rel >> 1) & 1 else y
                pc = 1 - c if rel & 1 else c
                peer = dict(send_sem=send_sems.at[rel - 1], recv_sem=recv_sems.at[rel - 1], device_id=(px, py, pc),
                            device_id_type=MESH)
                sends.append(pltpu.make_async_remote_copy(src_ref=src, dst_ref=slots.at[my_id], **peer))
                landings.append(pltpu.make_async_remote_copy(src_ref=src, dst_ref=slots.at[4 * px + 2 * py + pc], **peer))
            return pltpu.make_async_copy(src, slots.at[my_id], local_sem), sends, landings

        def begin():
            local, sends, _ = copies()
            local.start()
            for cp in sends:
                cp.start()

        def end():
            local, sends, landings = copies()
            for cp in landings:
                cp.wait_recv()
            for cp in sends:
                cp.wait_send()
            local.wait()

        return begin, None, end

    return Exchange([packed], [jax.ShapeDtypeStruct((N_DEV,) + packed.shape, packed.dtype)],
                    [pltpu.SemaphoreType.DMA((N_DEV - 1,)), pltpu.SemaphoreType.DMA((N_DEV - 1,)),
                     pltpu.SemaphoreType.DMA(())], stages, collective_id=3, peers=_all_others)


def small_adamw(slots, w, m, v):
    def body(r_ref, w_ref, m_ref, v_ref, g_out, d_out, m_out, v_out, loss_out):
        red = r_ref[0]
        for k in range(1, N_DEV):
            red = red + r_ref[k]
        wv = w_ref[...]
        lb = _lower_bound(jnp.concatenate([wv[5:6, :HGRN_WIDTH], wv[5:6, HGRN_WIDTH:]], axis=0))
        t = red[5:6, :HGRN_WIDTH] * lb * (1.0 - lb)
        row = lax.broadcasted_iota(jnp.int32, red.shape, 0)
        g = jnp.where(row == 5, jnp.concatenate([t, -t], axis=1), jnp.where(row >= 6, 0.0, red))
        g_out[...] = g
        d_out[...], m_out[...], v_out[...] = _adamw(wv, g, m_ref[...], v_ref[...])
        loss = jnp.sum(red[6:7, :], axis=-1, keepdims=True) * (0.5 / D_MODEL)
        loss_out[...] = jnp.broadcast_to(loss, loss_out.shape)

    return pl.pallas_call(
        body,
        name="small_adamw",
        in_specs=[_vmem_spec()] * 4,
        out_specs=[_vmem_spec()] * 5,
        out_shape=[jax.ShapeDtypeStruct(w.shape, F32)] * 4 + [jax.ShapeDtypeStruct((SUBLANES, LANES), F32)],
    )(slots, w, m, v)


def _pack_small(g1, gp, g_pre, g_post, an, hn, logits_or_dlb, extra=None):
    row5 = logits_or_dlb.reshape(1, -1)
    row5 = jnp.pad(row5, ((0, 0), (0, D_MODEL - row5.shape[1])))
    row6 = jnp.zeros((1, D_MODEL), F32) if extra is None else extra
    return jnp.concatenate([g1, gp, g_pre, g_post, jnp.concatenate([an, hn], axis=1), row5, row6,
                            jnp.zeros((1, D_MODEL), F32)], axis=0)


def _unpack_small(p):
    return dict(mix_pre_norm=p[0:1], mix_post_norm=p[1:2], mlp_pre_norm=p[2:3], mlp_post_norm=p[3:4],
                attn_out_norm=p[4:5, :ATTN_WIDTH], hgrn_out_norm=p[4:5, ATTN_WIDTH:],
                hgrn_lb_logits=p[5].reshape(2, HGRN_WIDTH))


BIG = ("w_in", "w_out", "w_ff1", "w_ff2")
ORDER = ("mix_pre_norm", "w_in", "attn_out_norm", "hgrn_lb_logits", "hgrn_out_norm", "w_out", "mix_post_norm",
         "mlp_pre_norm", "w_ff1", "w_ff2", "mlp_post_norm")


def kernel(x, mix_pre_norm, w_in, attn_out_norm, hgrn_lb_logits, hgrn_out_norm, w_out, mix_post_norm, mlp_pre_norm, w_ff1, w_ff2, mlp_post_norm, loss_target, m_mix_pre_norm, m_w_in, m_attn_out_norm, m_hgrn_lb_logits, m_hgrn_out_norm, m_w_out, m_mix_post_norm, m_mlp_pre_norm, m_w_ff1, m_w_ff2, m_mlp_post_norm, v_mix_pre_norm, v_w_in, v_attn_out_norm, v_hgrn_lb_logits, v_hgrn_out_norm, v_w_out, v_mix_post_norm, v_mlp_pre_norm, v_w_ff1, v_w_ff2, v_mlp_post_norm):
    w = dict(w_in=w_in[0], w_out=w_out[0], w_ff1=w_ff1[0], w_ff2=w_ff2[0])
    m = dict(w_in=m_w_in[0], w_out=m_w_out[0], w_ff1=m_w_ff1[0], w_ff2=m_w_ff2[0])
    v = dict(w_in=v_w_in[0], w_out=v_w_out[0], w_ff1=v_w_ff1[0], w_ff2=v_w_ff2[0])

    dx, big, small_slots = train_step(x[0], loss_target[0], mix_pre_norm, attn_out_norm, hgrn_lb_logits, hgrn_out_norm,
                                      mix_post_norm, mlp_pre_norm, mlp_post_norm, w, m, v)

    pack = lambda a, b, c2, d, e, f, g: _pack_small(a, b, c2, d, e, f, g)
    w_s = pack(mix_pre_norm, mix_post_norm, mlp_pre_norm, mlp_post_norm, attn_out_norm, hgrn_out_norm, hgrn_lb_logits)
    m_s = pack(m_mix_pre_norm, m_mix_post_norm, m_mlp_pre_norm, m_mlp_post_norm, m_attn_out_norm, m_hgrn_out_norm,
               m_hgrn_lb_logits)
    v_s = pack(v_mix_pre_norm, v_mix_post_norm, v_mlp_pre_norm, v_mlp_post_norm, v_attn_out_norm, v_hgrn_out_norm,
               v_hgrn_lb_logits)
    g_s, d_s, nm_s, nv_s, loss = small_adamw(small_slots, w_s, m_s, v_s)
    small_out = [_unpack_small(t) for t in (g_s, d_s, nm_s, nv_s)]

    outs = [loss[0, 0], dx[None]]
    for kind in range(4):
        for name in ORDER:
            outs.append(big[name][kind][None] if name in BIG else small_out[kind][name])
    return tuple(outs)
```

```python
import jax
import jax.numpy as jnp
from jax import lax
from jax.experimental import pallas as pl
from jax.experimental.pallas import tpu as pltpu

F32 = jnp.float32
BF16 = jnp.bfloat16

D_MODEL = 1024
ATTN_WIDTH = 512
ATTN_HEAD_DIM = 64
ATTN_HEADS = 8
ATTN_BLOCK = 128
DILATIONS = (1, 4, 16)
HGRN_WIDTH = 512
HGRN_HEADS = 4
HGRN_HEAD_DIM = 128
HGRN_CHUNK = 64
IN_PROJ_WIDTH = 3584
D_FF = 4096
RMS_EPS = 1e-6
N_DEV = 8
ADAM_LR = 0.001
ADAM_B1 = 0.9
ADAM_B2 = 0.999
ADAM_EPS = 1e-08
ADAM_WD = 0.01
ADAM_STEP = 10

SUBLANES = 8
LANES = 128
COLUMN_UNROLL = 8
HGRN_CHUNKS_PER_STEP = 2
SUB_BLOCK = 16
TOKEN_TILE = 512
ELEMENTWISE_ROWS = 1024
MLP_TILE = 256
PROJ_TILE = 512
VMEM_BYTES_V7X = 64 * 1024 * 1024
VMEM_LIMIT = VMEM_BYTES_V7X // 8 * 7
NEG_BIG = -1e30
MESH = pl.DeviceIdType.MESH


def _params(ride=None, **kw):
    if ride is not None:
        kw["collective_id"] = ride.collective_id
    return pltpu.CompilerParams(vmem_limit_bytes=VMEM_LIMIT, **kw)


def _vmem_spec():
    return pl.BlockSpec(memory_space=pltpu.VMEM)


def _dot(a, b):
    return jnp.dot(a, b, preferred_element_type=F32)


def _dot_nt(a, b):
    return lax.dot_general(a, b, (((1,), (1,)), ((), ())), preferred_element_type=F32)


def _dot_tn(a, b):
    return lax.dot_general(a, b, (((0,), (0,)), ((), ())), preferred_element_type=F32)


def _sigmoid(x):
    return 1.0 / (1.0 + jnp.exp(-x))


def _rms_fwd(x, gain, width):
    r = lax.rsqrt(jnp.sum(x * x, axis=-1, keepdims=True) * (1.0 / width) + RMS_EPS)
    return x * r * gain


def _rms_bwd(dy, x, gain, width):
    r = lax.rsqrt(jnp.sum(x * x, axis=-1, keepdims=True) * (1.0 / width) + RMS_EPS)
    xhat = x * r
    dxhat = dy * gain
    dx = r * (dxhat - xhat * (jnp.sum(dxhat * xhat, axis=-1, keepdims=True) * (1.0 / width)))
    return dx, dy * xhat


def _split3(x):
    hi = x.astype(BF16)
    r1 = x - hi.astype(F32)
    mid = r1.astype(BF16)
    lo = (r1 - mid.astype(F32)).astype(BF16)
    return hi, mid, lo


def _tri_sum(tri_bf16, x):
    hi, mid, lo = _split3(x)
    return _dot(tri_bf16, hi) + _dot(tri_bf16, mid) + _dot(tri_bf16, lo)


def _dilated_spec(d, tm, width):
    return pl.BlockSpec((d, tm // d, width), lambda i: (0, i, 0))


def _lane_blocks(ref, value):
    for c in range(ref.shape[0]):
        ref[c] = value[:, c * LANES:(c + 1) * LANES]


def _to_dilated(src_ref, dst_ref, d, tm, cast=None):
    for r in range(d):
        for c in range(src_ref.shape[0]):
            v = src_ref[c] if d == 1 else src_ref[c, pl.ds(r, tm // d, stride=d), :]
            dst_ref[r, :, c * LANES:(c + 1) * LANES] = v if cast is None else v.astype(cast)


def _from_dilated(src_ref, scratch_ref, d, tm):
    if d == 1:
        return src_ref[0].astype(F32)
    nblk = scratch_ref.shape[0]
    for r in range(d):
        for c in range(nblk):
            scratch_ref[c, pl.ds(r, tm // d, stride=d), :] = src_ref[r, :, c * LANES:(c + 1) * LANES].astype(F32)
    return jnp.concatenate([scratch_ref[c] for c in range(nblk)], axis=1)


def in_proj_fwd(x, g1, w_in_b, ride=None):
    s = x.shape[0]
    tm = PROJ_TILE
    qkv_w = 3 * ATTN_WIDTH
    hg_w = IN_PROJ_WIDTH - qkv_w

    def body(x_ref, g_ref, w_ref, hg_ref, h_ref, *rest):
        qkv_refs, qkv_scr = rest[:len(DILATIONS)], rest[len(DILATIONS)]
        h = _rms_fwd(x_ref[...], g_ref[...], D_MODEL).astype(BF16)
        h_ref[...] = h
        proj = _dot(h, w_ref[...])
        hg_ref[...] = proj[:, qkv_w:]
        _lane_blocks(qkv_scr, proj[:, :qkv_w])
        for d, ref in zip(DILATIONS, qkv_refs):
            _to_dilated(qkv_scr, ref, d, tm, cast=BF16)

    n_steps = s // tm
    step = lambda k: (lambda: pl.program_id(0) == k)
    e_in, e_out, e_shape, e_scr, e_args = _ride_specs(ride)
    return pl.pallas_call(
        _riding(body, 3, 2 + len(DILATIONS), 1, ride, step(0), step(n_steps // 2), step(n_steps - 1),
                late=step(n_steps - 2)),
        name="in_proj_fwd",
        grid=(n_steps,),
        in_specs=[
            pl.BlockSpec((tm, D_MODEL), lambda i: (i, 0)),
            pl.BlockSpec((1, D_MODEL), lambda i: (0, 0)),
            _vmem_spec(),
        ] + e_in,
        out_specs=[
            pl.BlockSpec((tm, hg_w), lambda i: (i, 0)),
            pl.BlockSpec((tm, D_MODEL), lambda i: (i, 0)),
        ] + [_dilated_spec(d, tm, qkv_w) for d in DILATIONS] + e_out,
        out_shape=[jax.ShapeDtypeStruct((s, hg_w), F32), jax.ShapeDtypeStruct((s, D_MODEL), BF16)] + [
            jax.ShapeDtypeStruct((d, s // d, qkv_w), BF16) for d in DILATIONS] + e_shape,
        scratch_shapes=[pltpu.VMEM((qkv_w // LANES, tm, LANES), F32)] + e_scr,
        compiler_params=_params(ride, dimension_semantics=("arbitrary",)),
    )(x, g1, w_in_b, *e_args)


ATTN_SCALE = ATTN_HEAD_DIM ** -0.5


def _fill_attn_bias(bias_ref, dilation):
    qi = lax.broadcasted_iota(jnp.int32, (ATTN_BLOCK, 2 * ATTN_BLOCK), 0)
    kj = lax.broadcasted_iota(jnp.int32, (ATTN_BLOCK, 2 * ATTN_BLOCK), 1)
    dist = qi + ATTN_BLOCK - kj
    valid = (dist >= 0) & (dist <= ATTN_BLOCK)
    for head in range(ATTN_HEADS):
        slope = 2.0 ** (-8.0 * (head + 1) / ATTN_HEADS)
        bias = jnp.where(valid, dist.astype(F32) * (-slope * dilation), NEG_BIG)
        bias_ref[0, head] = bias
        bias_ref[1, head] = jnp.where(kj >= ATTN_BLOCK, bias, NEG_BIG)


def _stack_heads(x):
    low = _lane_half(x.shape, 0)
    zero = jnp.zeros_like(x)
    return jnp.concatenate([jnp.where(low, x, zero), jnp.where(low, zero, x)], axis=0)


def _unstack_heads(y):
    half = y.shape[0] // 2
    return jnp.where(_lane_half((half, y.shape[1]), 0), y[:half], y[half:])


def _attn_scores(q_stack, kcat, bias_ref, pair, first_block):
    f = first_block.astype(jnp.int32)
    bias = jnp.concatenate([bias_ref[f, 2 * pair], bias_ref[f, 2 * pair + 1]], axis=0)
    return _dot_nt(q_stack, kcat) + bias


def _lane_half(shape, sub):
    lane = lax.broadcasted_iota(jnp.int32, shape, 1)
    return (lane < ATTN_HEAD_DIM) if sub == 0 else (lane >= ATTN_HEAD_DIM)


def _sub_block(col, row):
    return pl.BlockSpec((None, ATTN_BLOCK, ATTN_WIDTH), lambda r, n: (r, row(n), col))


def attn_fwd(qkv, dilation):
    d, length, _ = qkv.shape
    assert d == dilation
    nb = length // ATTN_BLOCK

    def body(q_ref, kc_ref, kp_ref, vc_ref, vp_ref, o_ref, lse_ref, bias_ref):
        @pl.when((pl.program_id(0) == 0) & (pl.program_id(1) == 0))
        def _():
            _fill_attn_bias(bias_ref, d)

        first = pl.program_id(1) == 0
        for pair in range(ATTN_HEADS // 2):
            lanes = slice(pair * LANES, (pair + 1) * LANES)
            q_stack = _stack_heads(q_ref[:, lanes] * ATTN_SCALE)
            kcat = jnp.concatenate([kp_ref[:, lanes], kc_ref[:, lanes]], axis=0)
            vcat = jnp.concatenate([vp_ref[:, lanes], vc_ref[:, lanes]], axis=0)
            sc = _attn_scores(q_stack, kcat, bias_ref, pair, first)
            m = jnp.max(sc, axis=-1, keepdims=True)
            p = jnp.exp(sc - m)
            den = jnp.sum(p, axis=-1, keepdims=True)
            o_ref[:, lanes] = _unstack_heads(_dot(p.astype(BF16), vcat) / den).astype(BF16)
            lse_ref[:, lanes] = _unstack_heads(jnp.broadcast_to(m + jnp.log(den), (2 * ATTN_BLOCK, LANES)))

    cur = lambda n: n
    prev = lambda n: jnp.maximum(n - 1, 0)
    return pl.pallas_call(
        body,
        name=f"attn_fwd_d{d}",
        grid=(d, nb),
        in_specs=[_sub_block(0, cur), _sub_block(1, cur), _sub_block(1, prev), _sub_block(2, cur), _sub_block(2, prev)],
        out_specs=[_sub_block(0, cur), _sub_block(0, cur)],
        out_shape=[jax.ShapeDtypeStruct((d, length, ATTN_WIDTH), BF16), jax.ShapeDtypeStruct((d, length, ATTN_WIDTH), F32)],
        scratch_shapes=[pltpu.VMEM((2, ATTN_HEADS, ATTN_BLOCK, 2 * ATTN_BLOCK), F32)],
        compiler_params=_params(dimension_semantics=("arbitrary", "arbitrary")),
    )(qkv, qkv, qkv, qkv, qkv)


def attn_bwd(qkv, d_out, lse, delta, dilation, ride=None):
    d, length, _ = qkv.shape
    assert d == dilation
    nb = length // ATTN_BLOCK

    steps = d * nb + 1

    def body(q_ref, kc_ref, kp_ref, vc_ref, vp_ref, do_ref, lse_ref, dl_ref, dq_ref, dk_ref, dv_ref, ck_ref, cv_ref,
             bias_ref):
        t = pl.program_id(0)

        @pl.when(t == 0)
        def _():
            ck_ref[...] = jnp.zeros_like(ck_ref)
            cv_ref[...] = jnp.zeros_like(cv_ref)
            _fill_attn_bias(bias_ref, d)

        @pl.when(t < steps - 1)
        def _():
            first = t % nb == 0
            for pair in range(ATTN_HEADS // 2):
                lanes = slice(pair * LANES, (pair + 1) * LANES)
                q_stack = _stack_heads(q_ref[:, lanes] * ATTN_SCALE)
                do_stack = _stack_heads(do_ref[:, lanes])
                kcat = jnp.concatenate([kp_ref[:, lanes], kc_ref[:, lanes]], axis=0)
                vcat = jnp.concatenate([vp_ref[:, lanes], vc_ref[:, lanes]], axis=0)
                col_a, col_b = 2 * pair, 2 * pair + 1
                lse_col = jnp.concatenate([lse_ref[:, col_a:col_a + 1], lse_ref[:, col_b:col_b + 1]], axis=0)
                dl_col = jnp.concatenate([dl_ref[:, col_a:col_a + 1], dl_ref[:, col_b:col_b + 1]], axis=0)
                p = jnp.exp(_attn_scores(q_stack, kcat, bias_ref, pair, first) - lse_col)
                ds = (p * (_dot_nt(do_stack, vcat) - dl_col)).astype(BF16)
                dq_ref[:, lanes] = (_unstack_heads(_dot(ds, kcat)) * ATTN_SCALE).astype(BF16)
                dk_cat = _dot_tn(ds, q_stack)
                dv_cat = _dot_tn(p.astype(BF16), do_stack)
                dk_ref[:, lanes] = (ck_ref[:, lanes] + dk_cat[:ATTN_BLOCK]).astype(BF16)
                dv_ref[:, lanes] = (cv_ref[:, lanes] + dv_cat[:ATTN_BLOCK]).astype(BF16)
                ck_ref[:, lanes] = dk_cat[ATTN_BLOCK:]
                cv_ref[:, lanes] = dv_cat[ATTN_BLOCK:]

        @pl.when(t == steps - 1)
        def _():
            dk_ref[...] = ck_ref[...].astype(BF16)
            dv_ref[...] = cv_ref[...].astype(BF16)

    blk = (ATTN_BLOCK, ATTN_WIDTH)

    def spec(col, shift, width=ATTN_WIDTH):
        def index(t):
            f = jnp.minimum(t, steps - 2) if shift > -2 else jnp.maximum(t - 1, 0)
            r, n = f // nb, f % nb
            return (r, jnp.maximum(n - 1, 0) if shift == -1 else n, col)
        return pl.BlockSpec((None, ATTN_BLOCK, width), index)

    step = lambda k: (lambda: pl.program_id(0) == k)
    e_in, e_out, e_shape, e_scr, e_args = _ride_specs(ride)
    return pl.pallas_call(
        _riding(body, 8, 3, 3, ride, step(0), step(steps // 2), step(steps - 1)),
        name=f"attn_bwd_d{d}",
        grid=(steps,),
        in_specs=[spec(0, 0), spec(1, 0), spec(1, -1), spec(2, 0), spec(2, -1), spec(0, 0), spec(0, 0, LANES),
                  spec(0, 0, LANES)] + e_in,
        out_specs=[spec(0, 0), spec(0, -2), spec(0, -2)] + e_out,
        out_shape=[jax.ShapeDtypeStruct((d, length, ATTN_WIDTH), BF16)] * 3 + e_shape,
        scratch_shapes=[pltpu.VMEM(blk, F32), pltpu.VMEM(blk, F32),
                        pltpu.VMEM((2, ATTN_HEADS, ATTN_BLOCK, 2 * ATTN_BLOCK), F32)] + e_scr,
        compiler_params=_params(ride, dimension_semantics=("arbitrary",)),
    )(qkv, qkv, qkv, qkv, qkv, d_out, lse, delta, *e_args)


def _lower_bound(logits):
    return _sigmoid(logits[0:1, :] - logits[1:2, :])


def _hgrn_gates(q, fp, lb):
    sq = _sigmoid(q)
    qf = q * sq
    sig = _sigmoid(fp)
    sig_neg = _sigmoid(-fp)
    kf = (1.0 - lb) * sig_neg
    log_sig = jnp.minimum(fp, 0.0) - jnp.log(1.0 + jnp.exp(-jnp.abs(fp)))
    a = jnp.log(lb)
    c = jnp.log(1.0 - lb) + log_sig
    log_f = jnp.maximum(a, c) + jnp.log(1.0 + jnp.exp(-jnp.abs(a - c)))
    return sq, qf, (sig, sig_neg, c), log_f, kf


def _tril_bf16(n, upper=False):
    r = lax.broadcasted_iota(jnp.int32, (n, n), 0)
    c = lax.broadcasted_iota(jnp.int32, (n, n), 1)
    keep = (c >= r) if upper else (c <= r)
    return jnp.where(keep, 1.0, 0.0).astype(BF16)


def _hgrn_diagonal_loops(c_len, diagonal):
    for half in range(SUB_BLOCK // SUBLANES):
        def step(jj, carry, half=half):
            j = half * SUBLANES + jj
            for i in range(c_len // SUB_BLOCK):
                diagonal(slice(i * SUB_BLOCK + half * SUBLANES, (i + 1) * SUB_BLOCK), j, i * SUB_BLOCK + j)
            return carry

        lax.fori_loop(0, SUBLANES, step, 0, unroll=COLUMN_UNROLL)


def _hgrn_off_diagonal(b, qf, kf):
    c_len, width = b.shape
    edges = [b[0:1, :]] + [b[i * SUB_BLOCK - 1:i * SUB_BLOCK, :] for i in range(1, c_len // SUB_BLOCK)]
    eq = jnp.exp(b - jnp.concatenate([jnp.broadcast_to(e, (SUB_BLOCK, width)) for e in edges], axis=0))
    q_til = qf * eq
    k_til, ek = [], []
    for i in range(1, c_len // SUB_BLOCK):
        n = i * SUB_BLOCK
        e = jnp.exp(edges[i] - b[:n, :])
        ek.append(e)
        k_til.append(jnp.concatenate([kf[:n, :] * e, jnp.zeros((2 * c_len - n, width), F32)], axis=0))
    return q_til, k_til, eq, ek


def _split2(x):
    hi = x.astype(BF16)
    return hi, (x - hi.astype(F32)).astype(BF16)


def hgrn_fwd(proj, lb, ride=None):
    s = proj.shape[0]
    c_len, nh, hd = HGRN_CHUNK, HGRN_HEADS, HGRN_HEAD_DIM
    n_chunks = s // c_len
    col0 = 0

    cps = 2 * HGRN_CHUNKS_PER_STEP
    n_steps = n_chunks // cps

    def body(q_ref, f_ref, i_ref, lb_ref, o_ref, st_out_ref, a_out_ref, st_ref, b_ref, qf_ref, kf_ref, a_ref):
        @pl.when(pl.program_id(0) == 0)
        def _():
            st_ref[...] = jnp.zeros_like(st_ref)

        lbv = _lower_bound(lb_ref[...])
        for u in range(cps):
            rs = slice(u * c_len, (u + 1) * c_len)
            b_u, qf_u, kf_u, a_u = b_ref.at[u], qf_ref.at[u], kf_ref.at[u], a_ref.at[u]
            _, qf, _, log_f, kf = _hgrn_gates(q_ref[rs, :], f_ref[rs, :], lbv)
            b = _tri_sum(_tril_bf16(c_len), log_f)
            b_u[...] = b
            qf_u[...] = qf
            kf_u[...] = kf
            a_u[...] = jnp.zeros_like(a_u)

            def diagonal(rows, j, key, b_u=b_u, qf_u=qf_u, kf_u=kf_u, a_u=a_u):
                bj = b_u[pl.ds(key, 1), :]
                kj = kf_u[pl.ds(key, 1), :]
                nrow = rows.stop - rows.start
                t_loc = lax.broadcasted_iota(jnp.int32, (nrow, nh * hd), 0) + (rows.start % SUB_BLOCK)
                e = jnp.exp(jnp.where(t_loc >= j, b_u[rows, :] - bj, NEG_BIG))
                prod = qf_u[rows, :] * kj * e
                lane = lax.broadcasted_iota(jnp.int32, (nrow, hd), 1)
                for h in range(nh):
                    col = jnp.sum(prod[:, h * hd:(h + 1) * hd], axis=-1, keepdims=True)
                    a_u[h, rows, :] = jnp.where(lane == key, col, a_u[h, rows, :])

            _hgrn_diagonal_loops(c_len, diagonal)
            q_til, k_til, _, _ = _hgrn_off_diagonal(b, qf, kf)
            q_til = q_til.astype(BF16)
            k_til = [k.astype(BF16) for k in k_til]

            b_last = b[c_len - 1:c_len, :]
            qb = (qf * jnp.exp(b)).astype(BF16)
            kb2 = (kf * jnp.exp(b_last - b)).astype(BF16)
            vf = i_ref[rs, :].astype(BF16)
            for h in range(nh):
                hs = slice(h * hd, (h + 1) * hd)
                st = st_ref[h]
                st_out_ref[u, h] = st
                off = [jnp.zeros((SUB_BLOCK, hd), F32)]
                for i in range(1, c_len // SUB_BLOCK):
                    off.append(_dot_nt(q_til[i * SUB_BLOCK:(i + 1) * SUB_BLOCK, hs], k_til[i - 1][:, hs]))
                a_h = a_u[h] + jnp.concatenate(off, axis=0)
                a_out_ref[rs, hs] = a_h
                o_ref[rs, hs] = _dot_nt(qb[:, hs], st.astype(BF16)) + _dot(a_h[:, :c_len].astype(BF16), vf[:, hs])
                st_ref[h] = st * jnp.exp(b_last[:, hs]) + _dot_tn(vf[:, hs], kb2[:, hs])

    blk = (cps * c_len, HGRN_WIDTH)
    sblk = (cps, c_len, HGRN_WIDTH)
    step = lambda k: (lambda: pl.program_id(0) == k)
    e_in, e_out, e_shape, e_scr, e_args = _ride_specs(ride)
    return pl.pallas_call(
        _riding(body, 4, 3, 5, ride, step(0), step(n_steps // 2), step(n_steps - 1), late=step((3 * n_steps) // 4)),
        name="hgrn_fwd",
        grid=(n_steps,),
        in_specs=[
            pl.BlockSpec(blk, lambda c: (c, col0)),
            pl.BlockSpec(blk, lambda c: (c, col0 + 1)),
            pl.BlockSpec(blk, lambda c: (c, col0 + 2)),
            pl.BlockSpec((2, HGRN_WIDTH), lambda c: (0, 0)),
        ] + e_in,
        out_specs=[
            pl.BlockSpec(blk, lambda c: (c, 0)),
            pl.BlockSpec((cps, nh, hd, hd), lambda c: (c, 0, 0, 0)),
            pl.BlockSpec(blk, lambda c: (c, 0)),
        ] + e_out,
        out_shape=[
            jax.ShapeDtypeStruct((s, HGRN_WIDTH), F32),
            jax.ShapeDtypeStruct((n_chunks, nh, hd, hd), F32),
            jax.ShapeDtypeStruct((s, nh * hd), F32),
        ] + e_shape,
        scratch_shapes=[
            pltpu.VMEM((nh, hd, hd), F32),
            pltpu.VMEM(sblk, F32),
            pltpu.VMEM(sblk, F32),
            pltpu.VMEM(sblk, F32),
            pltpu.VMEM((cps, nh, c_len, hd), F32),
        ] + e_scr,
        compiler_params=_params(ride, dimension_semantics=("arbitrary",)),
    )(proj, proj, proj, lb, *e_args)


def hgrn_bwd(proj, lb, d_o, states, a_mat, ride=None):
    s = proj.shape[0]
    c_len, nh, hd = HGRN_CHUNK, HGRN_HEADS, HGRN_HEAD_DIM
    n_chunks = s // c_len
    col0 = 0
    cps = HGRN_CHUNKS_PER_STEP
    n_steps = n_chunks // cps
    last = n_steps - 1

    def body(q_ref, f_ref, i_ref, lb_ref, do_ref, st_in_ref, a_in_ref, dq_ref, df_ref, di_ref, dlb_ref,
             dst_ref, b_ref, qf_ref, kf_ref, da_ref, dqi_ref, dki_ref):
        @pl.when(pl.program_id(0) == 0)
        def _():
            dst_ref[...] = jnp.zeros_like(dst_ref)
            dlb_ref[...] = jnp.zeros_like(dlb_ref)

        lbv = _lower_bound(lb_ref[...])
        for u in reversed(range(cps)):
            rs = slice(u * c_len, (u + 1) * c_len)
            b_u, qf_u, kf_u, da_u, dqi_u, dki_u = (b_ref.at[u], qf_ref.at[u], kf_ref.at[u], da_ref.at[u], dqi_ref.at[u],
                                                   dki_ref.at[u])
            q = q_ref[rs, :]
            sq, qf, (sig, sig_neg, log_c), log_f, kf = _hgrn_gates(q, f_ref[rs, :], lbv)
            b = _tri_sum(_tril_bf16(c_len), log_f)
            b_u[...] = b
            qf_u[...] = qf
            kf_u[...] = kf
            b_last = b[c_len - 1:c_len, :]
            eb = jnp.exp(b)
            ebl = jnp.exp(b_last - b)
            qb = qf * eb
            kb2 = kf * ebl
            vf = i_ref[rs, :]
            d_o = do_ref[rs, :]
            qb_b, kb2_b, vf_b, do_b = qb.astype(BF16), kb2.astype(BF16), vf.astype(BF16), d_o.astype(BF16)
            tq = lax.broadcasted_iota(jnp.int32, (c_len, hd), 0)
            lane = lax.broadcasted_iota(jnp.int32, (c_len, hd), 1)

            dqb_parts, dvf_parts, dkb2_parts, dbl_parts = [], [], [], []
            for h in range(nh):
                hs = slice(h * hd, (h + 1) * hd)
                st = st_in_ref[u, h]
                dst = dst_ref[h]
                st_b, dst_b = st.astype(BF16), dst.astype(BF16)
                a_h = a_in_ref[rs, hs][:, :c_len].astype(BF16)
                dqb_parts.append(_dot(do_b[:, hs], st_b))
                dvf_parts.append(_dot_tn(a_h, do_b[:, hs]) + _dot_nt(kb2_b[:, hs], dst_b))
                dkb2_parts.append(_dot(vf_b[:, hs], dst_b))
                da = _dot_nt(do_b[:, hs], vf_b[:, hs])
                da = jnp.concatenate([da, jnp.zeros((c_len, hd - c_len), F32)], axis=1)
                da_u[h] = jnp.where(tq >= lane, da, 0.0)
                dbl_parts.append(jnp.sum(dst * st, axis=0, keepdims=True) * jnp.exp(b_last[:, hs]))
                dst_ref[h] = dst * jnp.exp(b_last[:, hs]) + _dot_tn(do_b[:, hs], qb_b[:, hs])
            dqb = jnp.concatenate(dqb_parts, axis=1)
            dvf = jnp.concatenate(dvf_parts, axis=1)
            dkb2 = jnp.concatenate(dkb2_parts, axis=1)
            dbl = jnp.concatenate(dbl_parts, axis=1) + jnp.sum(dkb2 * kb2, axis=0, keepdims=True)

            dqi_u[...] = jnp.zeros_like(dqi_u)
            t_idx = lax.broadcasted_iota(jnp.int32, (c_len, nh * hd), 0)

            def diagonal(rows, j, key, b_u=b_u, qf_u=qf_u, kf_u=kf_u, da_u=da_u, dqi_u=dqi_u, dki_u=dki_u):
                bj = b_u[pl.ds(key, 1), :]
                kj = kf_u[pl.ds(key, 1), :]
                nrow = rows.stop - rows.start
                t_loc = lax.broadcasted_iota(jnp.int32, (nrow, nh * hd), 0) + (rows.start % SUB_BLOCK)
                e = jnp.exp(jnp.where(t_loc >= j, b_u[rows, :] - bj, NEG_BIG))
                lane_r = lax.broadcasted_iota(jnp.int32, (nrow, hd), 1)
                cols = [jnp.sum(jnp.where(lane_r == key, da_u[h, rows, :], 0.0), axis=-1, keepdims=True)
                        for h in range(nh)]
                w = e * jnp.concatenate([jnp.broadcast_to(cc, (nrow, hd)) for cc in cols], axis=1)
                dqi_u[rows, :] += w * kj
                dki_u[pl.ds(key, 1), :] = jnp.sum(w * qf_u[rows, :], axis=0, keepdims=True)

            _hgrn_diagonal_loops(c_len, diagonal)

            q_til, k_til, eq, ek = _hgrn_off_diagonal(b, qf, kf)
            q_hi, q_lo = _split2(q_til)
            k_pairs = [_split2(k) for k in k_til]
            n_sub = c_len // SUB_BLOCK
            dq_heads, dk_heads = [], []
            for h in range(nh):
                hs = slice(h * hd, (h + 1) * hd)
                dq_rows = [jnp.zeros((SUB_BLOCK, hd), F32)]
                dk_h = jnp.zeros((c_len, hd), F32)
                for i in range(1, n_sub):
                    rows = slice(i * SUB_BLOCK, (i + 1) * SUB_BLOCK)
                    n = i * SUB_BLOCK
                    da_i = da_u[h, rows, :].astype(BF16)
                    k_hi, k_lo = k_pairs[i - 1]
                    dq_rows.append((_dot(da_i, k_hi[:, hs]) + _dot(da_i, k_lo[:, hs])) * eq[rows, hs])
                    dk_t = (_dot_tn(da_i, q_hi[rows, hs]) + _dot_tn(da_i, q_lo[rows, hs]))[:n, :] * ek[i - 1][:, hs]
                    dk_h = dk_h + jnp.concatenate([dk_t, jnp.zeros((c_len - n, hd), F32)], axis=0)
                dq_heads.append(jnp.concatenate(dq_rows, axis=0))
                dk_heads.append(dk_h)
            dq_intra = dqi_u[...] + jnp.concatenate(dq_heads, axis=1)
            dk_intra = dki_u[...] + jnp.concatenate(dk_heads, axis=1)

            db = dqb * qb + qf * dq_intra - kf * dk_intra - dkb2 * kb2
            db = db + jnp.where(t_idx == c_len - 1, dbl, 0.0)
            dg = _tri_sum(_tril_bf16(c_len, upper=True), db)
            dqf = dqb * eb + dq_intra
            dkf = dkb2 * ebl + dk_intra
            dq_ref[rs, :] = (dqf * (sq * (1.0 + q * (1.0 - sq)))).astype(BF16)
            df_ref[rs, :] = (sig_neg * (dg * jnp.exp(log_c - log_f) - dkf * (1.0 - lbv) * sig)).astype(BF16)
            di_ref[rs, :] = dvf.astype(BF16)
            dlb_ref[...] += jnp.sum(sig_neg * (dg * jnp.exp(-log_f) - dkf), axis=0, keepdims=True)

    blk = (cps * c_len, HGRN_WIDTH)
    sblk = (cps, c_len, HGRN_WIDTH)
    rev = lambda c: last - c
    step = lambda k: (lambda: pl.program_id(0) == k)
    e_in, e_out, e_shape, e_scr, e_args = _ride_specs(ride)
    return pl.pallas_call(
        _riding(body, 7, 4, 7, ride, step(0), step(n_steps // 2), step(last)),
        name="hgrn_bwd",
        grid=(n_steps,),
        in_specs=[
            pl.BlockSpec(blk, lambda c: (rev(c), col0)),
            pl.BlockSpec(blk, lambda c: (rev(c), col0 + 1)),
            pl.BlockSpec(blk, lambda c: (rev(c), col0 + 2)),
            pl.BlockSpec((2, HGRN_WIDTH), lambda c: (0, 0)),
            pl.BlockSpec(blk, lambda c: (rev(c), 0)),
            pl.BlockSpec((cps, nh, hd, hd), lambda c: (rev(c), 0, 0, 0)),
            pl.BlockSpec(blk, lambda c: (rev(c), 0)),
        ] + e_in,
        out_specs=[
            pl.BlockSpec(blk, lambda c: (rev(c), 0)),
            pl.BlockSpec(blk, lambda c: (rev(c), 0)),
            pl.BlockSpec(blk, lambda c: (rev(c), 0)),
            pl.BlockSpec((1, HGRN_WIDTH), lambda c: (0, 0)),
        ] + e_out,
        out_shape=[jax.ShapeDtypeStruct((s, HGRN_WIDTH), BF16)] * 3 + [jax.ShapeDtypeStruct((1, HGRN_WIDTH), F32)] + e_shape,
        scratch_shapes=[
            pltpu.VMEM((nh, hd, hd), F32),
            pltpu.VMEM(sblk, F32),
            pltpu.VMEM(sblk, F32),
            pltpu.VMEM(sblk, F32),
            pltpu.VMEM((cps, nh, c_len, hd), F32),
            pltpu.VMEM(sblk, F32),
            pltpu.VMEM(sblk, F32),
        ] + e_scr,
        compiler_params=_params(ride, dimension_semantics=("arbitrary",)),
    )(proj, proj, proj, lb, d_o, states, a_mat, *e_args)


def _per_head_lanes(x):
    lane = lax.broadcasted_iota(jnp.int32, (x.shape[0], LANES), 1)
    out = jnp.zeros((x.shape[0], LANES), F32)
    for h in range(ATTN_HEADS):
        out = jnp.where(lane == h, x[:, h * ATTN_HEAD_DIM:h * ATTN_HEAD_DIM + 1], out)
    return out


def _row_spec(tm, width, col=0):
    return pl.BlockSpec((tm, width), lambda i: (i, col))


def _const_spec(width):
    return pl.BlockSpec((1, width), lambda i: (0, 0))


def _acc_rows(ref, value):
    @pl.when(pl.program_id(0) == 0)
    def _():
        ref[...] = jnp.zeros_like(ref)

    ref[...] += jnp.sum(value, axis=0, keepdims=True)


def mix_fwd(attn_parts, o_h, proj, an, hn, w_out_b, gp, x, ride=None):
    s = x.shape[0]
    tm = TOKEN_TILE
    gate_col = 3
    hd = HGRN_HEAD_DIM
    nd = len(DILATIONS)

    def body(*refs):
        o_refs, l_refs = refs[:nd], refs[nd:2 * nd]
        oh_ref, gate_ref, an_ref, hn_ref, w_ref, gp_ref, x_ref = refs[2 * nd:2 * nd + 7]
        x1_ref, cat_ref, mixed_ref, attn_ref = refs[2 * nd + 7:2 * nd + 11]
        lse_refs = refs[2 * nd + 11:3 * nd + 11]
        o_scr, l_scr, lse_scr = refs[3 * nd + 11:]
        os_ = [_from_dilated(r, o_scr.at[k], d, tm) for k, (r, d) in enumerate(zip(o_refs, DILATIONS))]
        ls = [_from_dilated(r, l_scr.at[k], d, tm) for k, (r, d) in enumerate(zip(l_refs, DILATIONS))]
        m = jnp.maximum(jnp.maximum(ls[0], ls[1]), ls[2])
        es = [jnp.exp(l - m) for l in ls]
        den = es[0] + es[1] + es[2]
        attn = (es[0] * os_[0] + es[1] * os_[1] + es[2] * os_[2]) / den
        attn_ref[...] = attn
        lse_scr[0] = _per_head_lanes(m + jnp.log(den))
        for d, ref in zip(DILATIONS, lse_refs):
            _to_dilated(lse_scr, ref, d, tm)
        cat_ref[:, :ATTN_WIDTH] = _rms_fwd(attn, an_ref[...], ATTN_WIDTH).astype(BF16)
        gate = gate_ref[...]
        silu_g = gate * _sigmoid(gate)
        for h in range(HGRN_HEADS):
            hs = slice(h * hd, (h + 1) * hd)
            rec = _rms_fwd(oh_ref[:, hs], hn_ref[:, hs], hd) * silu_g[:, hs]
            cat_ref[:, ATTN_WIDTH + h * hd:ATTN_WIDTH + (h + 1) * hd] = rec.astype(BF16)
        mixed = _dot(cat_ref[...], w_ref[...])
        mixed_ref[...] = mixed
        x1_ref[...] = x_ref[...] + _rms_fwd(mixed, gp_ref[...], D_MODEL)

    aw = ATTN_WIDTH
    n_steps = s // tm
    step = lambda k: (lambda: pl.program_id(0) == k)
    e_in, e_out, e_shape, e_scr, e_args = _ride_specs(ride)
    return pl.pallas_call(
        _riding(body, 2 * nd + 7, 4 + nd, 3, ride, step(0), step((13 * n_steps) // 16), step(n_steps - 1)),
        name="mix_fwd",
        grid=(n_steps,),
        in_specs=[_dilated_spec(d, tm, aw) for d in DILATIONS] * 2 + [
            _row_spec(tm, aw), _row_spec(tm, aw, gate_col), _const_spec(aw), _const_spec(aw), _vmem_spec(),
            _const_spec(D_MODEL), _row_spec(tm, D_MODEL)] + e_in,
        out_specs=[_row_spec(tm, D_MODEL), _row_spec(tm, D_MODEL), _row_spec(tm, D_MODEL), _row_spec(tm, aw)] + [
            _dilated_spec(d, tm, LANES) for d in DILATIONS] + e_out,
        out_shape=[
            jax.ShapeDtypeStruct((s, D_MODEL), F32),
            jax.ShapeDtypeStruct((s, D_MODEL), BF16),
            jax.ShapeDtypeStruct((s, D_MODEL), F32),
            jax.ShapeDtypeStruct((s, aw), F32),
        ] + [jax.ShapeDtypeStruct((d, s // d, LANES), F32) for d in DILATIONS] + e_shape,
        scratch_shapes=[pltpu.VMEM((nd, aw // LANES, tm, LANES), F32), pltpu.VMEM((nd, aw // LANES, tm, LANES), F32),
                        pltpu.VMEM((1, tm, LANES), F32)] + e_scr,
        compiler_params=_params(ride, dimension_semantics=("arbitrary",)),
    )(*[p[0] for p in attn_parts], *[p[1] for p in attn_parts], o_h, proj, an, hn, w_out_b, gp, x, *e_args)


def mix_bwd(dx1, mixed, gp, w_out_b, attn, an, o_h, proj, hn):
    s = dx1.shape[0]
    tm = TOKEN_TILE
    gate_col = 3
    hd = HGRN_HEAD_DIM
    aw = ATTN_WIDTH

    nd = len(DILATIONS)

    def body(*refs):
        dx1_ref, mixed_ref, gp_ref, w_ref, attn_ref, an_ref, oh_ref, gate_ref, hn_ref, dmix_ref = refs[:10]
        do_refs, delta_refs = refs[10:10 + nd], refs[10 + nd:10 + 2 * nd]
        doh_ref, dgate_ref, dgp_ref, dan_ref, dhn_ref, do_ref, delta_ref = refs[10 + 2 * nd:]
        dmixed, gp_c = _rms_bwd(dx1_ref[...], mixed_ref[...], gp_ref[...], D_MODEL)
        _acc_rows(dgp_ref, gp_c)
        dmixed_b = dmixed.astype(BF16)
        dmix_ref[...] = dmixed_b
        dcat = _dot_nt(dmixed_b, w_ref[...])
        attn = attn_ref[...]
        d_o, an_c = _rms_bwd(dcat[:, :aw], attn, an_ref[...], aw)
        _acc_rows(dan_ref, an_c)
        _lane_blocks(do_ref, d_o)
        prod = d_o * attn
        lane = lax.broadcasted_iota(jnp.int32, (tm, LANES), 1)
        delta = jnp.zeros((tm, LANES), F32)
        for pair in range(ATTN_HEADS // 2):
            pp = prod[:, pair * LANES:(pair + 1) * LANES]
            low = _lane_half((tm, LANES), 0)
            lo = jnp.sum(jnp.where(low, pp, 0.0), axis=-1, keepdims=True)
            hi = jnp.sum(jnp.where(low, 0.0, pp), axis=-1, keepdims=True)
            delta = jnp.where(lane == 2 * pair, lo, jnp.where(lane == 2 * pair + 1, hi, delta))
        delta_ref[0] = delta
        for d, o_ref, l_ref in zip(DILATIONS, do_refs, delta_refs):
            _to_dilated(do_ref, o_ref, d, tm, cast=BF16)
            _to_dilated(delta_ref, l_ref, d, tm)
        gate = gate_ref[...]
        sg = _sigmoid(gate)
        silu_g = gate * sg
        drec = dcat[:, aw:]
        hn_parts = []
        for h in range(HGRN_HEADS):
            hs = slice(h * hd, (h + 1) * hd)
            oh = oh_ref[:, hs]
            on = _rms_fwd(oh, hn_ref[:, hs], hd)
            dgate_ref[:, hs] = (drec[:, hs] * on * (sg[:, hs] * (1.0 + gate[:, hs] * (1.0 - sg[:, hs])))).astype(BF16)
            d_oh, hn_c = _rms_bwd(drec[:, hs] * silu_g[:, hs], oh, hn_ref[:, hs], hd)
            doh_ref[:, hs] = d_oh
            hn_parts.append(hn_c)
        _acc_rows(dhn_ref, jnp.concatenate(hn_parts, axis=1))

    return pl.pallas_call(
        body,
        name="mix_bwd",
        grid=(s // tm,),
        in_specs=[_row_spec(tm, D_MODEL), _row_spec(tm, D_MODEL), _const_spec(D_MODEL), _vmem_spec(), _row_spec(tm, aw),
                  _const_spec(aw), _row_spec(tm, aw), _row_spec(tm, aw, gate_col), _const_spec(aw)],
        out_specs=[_row_spec(tm, D_MODEL)] + [_dilated_spec(d, tm, aw) for d in DILATIONS] + [
            _dilated_spec(d, tm, LANES) for d in DILATIONS] + [_row_spec(tm, aw)] * 2 + [
            _const_spec(D_MODEL), _const_spec(aw), _const_spec(aw)],
        out_shape=[jax.ShapeDtypeStruct((s, D_MODEL), BF16)] + [
            jax.ShapeDtypeStruct((d, s // d, aw), BF16) for d in DILATIONS] + [
            jax.ShapeDtypeStruct((d, s // d, LANES), F32) for d in DILATIONS] + [
            jax.ShapeDtypeStruct((s, aw), F32), jax.ShapeDtypeStruct((s, aw), BF16),
            jax.ShapeDtypeStruct((1, D_MODEL), F32), jax.ShapeDtypeStruct((1, aw), F32),
            jax.ShapeDtypeStruct((1, aw), F32)],
        scratch_shapes=[pltpu.VMEM((aw // LANES, tm, LANES), F32), pltpu.VMEM((1, tm, LANES), F32)],
        compiler_params=_params(dimension_semantics=("arbitrary",)),
    )(dx1, mixed, gp, w_out_b, attn, an, o_h, proj, hn)


def mlp_fwd_bwd(x1, g_pre, w1_blocks, w2_b, g_post, target):
    s = x1.shape[0]
    tm = MLP_TILE
    nblk, _, fb = w1_blocks.shape

    def body(x1_ref, gpre_ref, w1_ref, w2_ref, gpost_ref, t_ref,
             dx1_ref, h2_ref, a_ref, du_ref, dff_ref, loss_ref, dgpre_ref, dgpost_ref, u_ref):
        x1v = x1_ref[...]
        h2 = _rms_fwd(x1v, gpre_ref[...], D_MODEL).astype(BF16)
        h2_ref[...] = h2
        ff = jnp.zeros((tm, D_MODEL), F32)
        for j in range(nblk):
            cols = slice(j * fb, (j + 1) * fb)
            ru = jnp.maximum(_dot(h2, w1_ref[j]), 0.0)
            u_ref[:, cols] = ru.astype(BF16)
            a = (ru * ru).astype(BF16)
            a_ref[:, cols] = a
            ff = ff + _dot(a, w2_ref[cols, :])
        diff = x1v + _rms_fwd(ff, gpost_ref[...], D_MODEL) - t_ref[...]
        _acc_rows(loss_ref, diff * diff)
        dy = diff * (1.0 / D_MODEL)
        dff, gpost_c = _rms_bwd(dy, ff, gpost_ref[...], D_MODEL)
        _acc_rows(dgpost_ref, gpost_c)
        dff_b = dff.astype(BF16)
        dff_ref[...] = dff_b
        dh2 = jnp.zeros((tm, D_MODEL), F32)
        for j in range(nblk):
            cols = slice(j * fb, (j + 1) * fb)
            du = (_dot_nt(dff_b, w2_ref[cols, :]) * (2.0 * u_ref[:, cols])).astype(BF16)
            du_ref[:, cols] = du
            dh2 = dh2 + _dot_nt(du, w1_ref[j])
        dxa, gpre_c = _rms_bwd(dh2, x1v, gpre_ref[...], D_MODEL)
        _acc_rows(dgpre_ref, gpre_c)
        dx1_ref[...] = dy + dxa

    dm = D_MODEL
    return pl.pallas_call(
        body,
        name="mlp_fwd_bwd",
        grid=(s // tm,),
        in_specs=[_row_spec(tm, dm), _const_spec(dm), _vmem_spec(), _vmem_spec(), _const_spec(dm), _row_spec(tm, dm)],
        out_specs=[_row_spec(tm, dm), _row_spec(tm, dm), _row_spec(tm, D_FF), _row_spec(tm, D_FF), _row_spec(tm, dm),
                   _const_spec(dm), _const_spec(dm), _const_spec(dm)],
        out_shape=[
            jax.ShapeDtypeStruct((s, dm), F32),
            jax.ShapeDtypeStruct((s, dm), BF16),
            jax.ShapeDtypeStruct((s, D_FF), BF16),
            jax.ShapeDtypeStruct((s, D_FF), BF16),
            jax.ShapeDtypeStruct((s, dm), BF16),
            jax.ShapeDtypeStruct((1, dm), F32),
            jax.ShapeDtypeStruct((1, dm), F32),
            jax.ShapeDtypeStruct((1, dm), F32),
        ],
        scratch_shapes=[pltpu.VMEM((tm, D_FF), BF16)],
        compiler_params=_params(dimension_semantics=("arbitrary",)),
    )(x1, g_pre, w1_blocks, w2_b, g_post, target)


def in_proj_bwd(attn_grads, hgrn_grads, dgate, w_in_b, x, g1, dx1):
    s = x.shape[0]
    tm = PROJ_TILE
    aw = ATTN_WIDTH
    n_attn = len(attn_grads)
    flat = [g[k] for k in range(3) for g in attn_grads] + list(hgrn_grads) + [dgate]

    def body(*refs):
        parts = refs[:len(flat)]
        w_ref, x_ref, g_ref, dx1_ref, dx_ref, dproj_ref, dg_ref, scr = refs[len(flat):]
        groups = []
        for k in range(3):
            acc = None
            for p, d in zip(parts[k * n_attn:(k + 1) * n_attn], DILATIONS):
                v = _from_dilated(p, scr, d, tm)
                acc = v if acc is None else acc + v
            groups.append(acc)
        groups += [p[...] for p in parts[3 * n_attn:]]
        dh = jnp.zeros((tm, D_MODEL), F32)
        for gi, grp in enumerate(groups):
            cols = slice(gi * aw, (gi + 1) * aw)
            gb = grp.astype(BF16)
            dproj_ref[:, cols] = gb
            dh = dh + _dot_nt(gb, w_ref[:, cols])
        dxa, g_c = _rms_bwd(dh, x_ref[...], g_ref[...], D_MODEL)
        _acc_rows(dg_ref, g_c)
        dx_ref[...] = dx1_ref[...] + dxa

    dm = D_MODEL
    return pl.pallas_call(
        body,
        name="in_proj_bwd",
        grid=(s // tm,),
        in_specs=[_dilated_spec(d, tm, aw) for d in DILATIONS] * 3 + [_row_spec(tm, aw)] * 4 + [
            _vmem_spec(), _row_spec(tm, dm), _const_spec(dm), _row_spec(tm, dm)],
        out_specs=[_row_spec(tm, dm), _row_spec(tm, IN_PROJ_WIDTH), _const_spec(dm)],
        out_shape=[jax.ShapeDtypeStruct((s, dm), F32), jax.ShapeDtypeStruct((s, IN_PROJ_WIDTH), BF16),
                   jax.ShapeDtypeStruct((1, dm), F32)],
        scratch_shapes=[pltpu.VMEM((aw // LANES, tm, LANES), F32)],
        compiler_params=_params(dimension_semantics=("arbitrary",)),
    )(*flat, w_in_b, x, g1, dx1)


def wgrad(a_b, b_b, tn, name, ts=2048, per_step=1, ride=None):
    s, k = a_b.shape
    n = b_b.shape[1]

    def body(a_ref, b_ref, o_ref):
        @pl.when(pl.program_id(1) == 0)
        def _():
            o_ref[...] = jnp.zeros_like(o_ref)

        a = a_ref[...]
        for jj in range(per_step):
            o_ref[jj] += _dot_tn(a, b_ref[:, jj * tn:(jj + 1) * tn])

    wide = tn * per_step
    gn, gs = n // wide, s // ts
    step = lambda j, i: (lambda: (pl.program_id(0) == j) & (pl.program_id(1) == i))
    e_in, e_out, e_shape, e_scr, e_args = _ride_specs(ride)
    out = pl.pallas_call(
        _riding(body, 2, 1, 0, ride, step(0, 0), step(gn // 2, 0), step(gn - 1, gs - 1)),
        name=name,
        grid=(gn, gs),
        in_specs=[pl.BlockSpec((ts, k), lambda j, i: (i, 0)), pl.BlockSpec((ts, wide), lambda j, i: (i, j))] + e_in,
        out_specs=[pl.BlockSpec((per_step, k, tn), lambda j, i: (j, 0, 0))] + e_out,
        out_shape=[jax.ShapeDtypeStruct((n // tn, k, tn), F32)] + e_shape,
        scratch_shapes=e_scr,
        compiler_params=_params(ride, dimension_semantics=("arbitrary", "arbitrary")),
    )(a_b, b_b, *e_args)
    return out[0] if ride is None else out


def train_step(x, target, g1, an, logits, hn, gp, g_pre, g_post, w, m, v):
    nd = len(DILATIONS)
    shard_b = {k: w[k].astype(BF16) for k in BIG}
    (w_in_g,) = run_exchange(gather_exchange([shard_b["w_in"]]), "gather_w_in")
    w_in_b = w_in_g.transpose(1, 0, 2).reshape(D_MODEL, IN_PROJ_WIDTH)

    proj, h_b, *qkvs, w2_g = in_proj_fwd(x, g1, w_in_b, ride=gather_exchange([shard_b["w_ff2"]]))
    w2_b = w2_g.reshape(D_FF, D_MODEL)
    attn_parts = [attn_fwd(qkv, d) for qkv, d in zip(qkvs, DILATIONS)]
    o_h, states, a_mat, w_out_g, w1_blocks = hgrn_fwd(
        proj, logits, ride=gather_exchange([shard_b["w_out"], shard_b["w_ff1"]]))
    w_out_b = w_out_g.reshape(D_MODEL, D_MODEL)
    x1, cat_b, mixed, attn, *lses = mix_fwd(attn_parts, o_h, proj, an, hn, w_out_b, gp, x)
    dx1, h2_b, a_b, du_b, dff_b, loss_vec, dg_pre, dg_post = mlp_fwd_bwd(x1, g_pre, w1_blocks, w2_b, g_post, target)
    dw2 = wgrad(a_b, dff_b, D_MODEL, "wgrad_ff2", ts=512)
    dw1 = wgrad(h2_b, du_b, D_FF // N_DEV, "wgrad_ff1", per_step=2)
    dmix_b, *rest = mix_bwd(dx1, mixed, gp, w_out_b, attn, an, o_h, proj, hn)
    d_os, deltas = rest[:nd], rest[nd:2 * nd]
    d_oh, dgate, dgp, dan, dhn = rest[2 * nd:]
    dwout = wgrad(cat_b, dmix_b, D_MODEL, "wgrad_out")

    early = ("w_out", "w_ff1", "w_ff2")
    early_grads = [dwout.reshape(N_DEV, D_MODEL // N_DEV, D_MODEL), dw1, dw2.reshape(N_DEV, D_FF // N_DEV, D_MODEL)]
    res = attn_bwd(qkvs[0], d_os[0], lses[0], deltas[0], DILATIONS[0], ride=to_core_exchange(early_grads))
    pairs = [pair_sum(g, s, f"pair_sum_{name}") for g, s, name in zip(early_grads, res[3:], early)]
    attn_grads = [res[:3]]
    *res, others_ff2 = attn_bwd(qkvs[1], d_os[1], lses[1], deltas[1], DILATIONS[1],
                                ride=to_chip_exchange([pairs[2][1]]))
    attn_grads.append(res)
    attn_grads.append(attn_bwd(qkvs[2], d_os[2], lses[2], deltas[2], DILATIONS[2]))
    dq_h, df_h, di_h, dlb, *others = hgrn_bwd(proj, logits, d_oh, states, a_mat,
                                              ride=to_chip_exchange([pairs[0][1], pairs[1][1]]))
    others.append(others_ff2)
    dx, dproj_b, dg1 = in_proj_bwd(attn_grads, (dq_h, df_h, di_h), dgate, w_in_b, x, g1, dx1)
    packed = _pack_small(dg1, dgp, dg_pre, dg_post, dan, dhn, dlb, loss_vec)
    dwin, small_slots = wgrad(h_b, dproj_b, 2 * IN_PROJ_WIDTH // N_DEV, "wgrad_in",
                              ride=small_exchange(packed))
    big = {name: sum_adamw(p[0], o, w[name], m[name], v[name], f"sum_adamw_{name}")
           for name, p, o in zip(early, pairs, others)}

    shard_w = IN_PROJ_WIDTH // N_DEV
    dwin_blocks = dwin.reshape(N_DEV // 2, D_MODEL, 2, shard_w).transpose(0, 2, 1, 3).reshape(N_DEV, D_MODEL, shard_w)
    pair_in, others_in = reduce_last(dwin_blocks)
    big["w_in"] = sum_adamw(pair_in, others_in, w["w_in"], m["w_in"], v["w_in"], "sum_adamw_w_in")
    return dx, big, small_slots


def _position():
    x, y, c = lax.axis_index("x"), lax.axis_index("y"), lax.axis_index("c")
    other_chips = [(1 - x, y), (x, 1 - y), (1 - x, 1 - y)]
    return x, y, c, other_chips


def _any_spec():
    return pl.BlockSpec(memory_space=pl.ANY)


class Exchange:
    def __init__(self, arrays, out_shape, sems, stages, collective_id, peers):
        self.arrays, self.out_shape, self.sems, self.stages = list(arrays), list(out_shape), list(sems), stages
        self.collective_id, self.peers = collective_id, peers

    def open(self):
        barrier = pltpu.get_barrier_semaphore()
        peers = self.peers()
        for peer in peers:
            pl.semaphore_signal(barrier, inc=1, device_id=peer, device_id_type=MESH)
        pl.semaphore_wait(barrier, len(peers))


def _siblings():
    x, y, c, _ = _position()
    return [(x, y, 1 - c)]


def _same_core_of_other_chips():
    x, y, c, chips = _position()
    return [(px, py, c) for px, py in chips]


def _gather_peers():
    x, y, c, _ = _position()
    return [(x, y, 1 - c), (1 - x, y, c), (x, 1 - y, c)]


def _all_others():
    x, y, c, _ = _position()
    return [(1 - x if rel & 4 else x, 1 - y if rel & 2 else y, 1 - c if rel & 1 else c) for rel in range(1, N_DEV)]


def gather_exchange(shards):
    n = len(shards)
    halves = [sh.shape[0] // 2 for sh in shards]

    def stages(ins, outs, sems):
        send_sems, recv_sems, local_sems = sems

        def parts():
            x, y, c, _ = _position()
            me, sibling = (x, y, c), (x, y, 1 - c)
            nbr_x, nbr_y, diag = (1 - x, y, c), (x, 1 - y, c), (1 - x, 1 - y, c)

            def slot(a, dev, rows=None):
                ref = outs[a].at[4 * dev[0] + 2 * dev[1] + dev[2]]
                return ref if rows is None else ref.at[rows]

            def copy(a, k, block, to, rows=None, src=None):
                return pltpu.make_async_remote_copy(
                    src_ref=slot(a, block, rows) if src is None else src, dst_ref=slot(a, block, rows),
                    send_sem=send_sems.at[a, k], recv_sem=recv_sems.at[a, k], device_id=to, device_id_type=MESH)

            upper = lambda a: pl.ds(0, halves[a])
            lower = lambda a: pl.ds(halves[a], halves[a])
            return me, sibling, nbr_x, nbr_y, diag, slot, copy, upper, lower

        def begin():
            me, sibling, nbr_x, nbr_y, _, slot, copy, _, _ = parts()
            for a in range(n):
                pltpu.make_async_copy(ins[a], slot(a, me), local_sems.at[a]).start()
                for k, to in enumerate((sibling, nbr_x, nbr_y)):
                    copy(a, k, me, to, src=ins[a]).start()

        def middle():
            me, sibling, nbr_x, nbr_y, _, _, copy, upper, lower = parts()
            for a in range(n):
                copy(a, 1, nbr_x, me).wait_recv()
                copy(a, 3, nbr_x, sibling).start()
                copy(a, 5, nbr_x, nbr_y, rows=lower(a)).start()
                copy(a, 2, nbr_y, me).wait_recv()
                copy(a, 4, nbr_y, sibling).start()
                copy(a, 6, nbr_y, nbr_x, rows=upper(a)).start()

        def late():
            me, sibling, _, _, diag, _, copy, upper, lower = parts()
            for a in range(n):
                copy(a, 6, diag, me, rows=upper(a)).wait_recv()
                copy(a, 5, diag, me, rows=lower(a)).wait_recv()
                copy(a, 7, diag, sibling).start()

        def end():
            me, sibling, nbr_x, nbr_y, diag, slot, copy, upper, lower = parts()
            sib = lambda dev: (dev[0], dev[1], sibling[2])
            for a in range(n):
                for k, block in ((0, sibling), (3, sib(nbr_x)), (4, sib(nbr_y)), (7, sib(diag))):
                    copy(a, k, block, me).wait_recv()
                copy(a, 0, me, sibling, src=ins[a]).wait_send()
                copy(a, 1, me, nbr_x, src=ins[a]).wait_send()
                copy(a, 2, me, nbr_y, src=ins[a]).wait_send()
                copy(a, 3, nbr_x, sibling).wait_send()
                copy(a, 4, nbr_y, sibling).wait_send()
                copy(a, 5, nbr_x, nbr_y, rows=lower(a)).wait_send()
                copy(a, 6, nbr_y, nbr_x, rows=upper(a)).wait_send()
                copy(a, 7, diag, sibling).wait_send()
                pltpu.make_async_copy(ins[a], slot(a, me), local_sems.at[a]).wait()

        return begin, (middle, late), end

    return Exchange(
        shards, [jax.ShapeDtypeStruct((N_DEV,) + sh.shape, sh.dtype) for sh in shards],
        [pltpu.SemaphoreType.DMA((n, 8)), pltpu.SemaphoreType.DMA((n, 8)), pltpu.SemaphoreType.DMA((n,))], stages,
        collective_id=0, peers=_gather_peers)


def to_core_exchange(grads):
    n = len(grads)

    def stages(ins, outs, sems):
        send_sems, recv_sems = sems

        def copies():
            x, y, c, _ = _position()
            return [pltpu.make_async_remote_copy(
                src_ref=ins[a].at[2 * q + (1 - c)], dst_ref=outs[a].at[q], send_sem=send_sems.at[a, q],
                recv_sem=recv_sems.at[a, q], device_id=(x, y, 1 - c), device_id_type=MESH)
                for a in range(n) for q in range(4)]

        def begin():
            for cp in copies():
                cp.start()

        def end():
            for cp in copies():
                cp.wait()

        return begin, None, end

    return Exchange(grads, [jax.ShapeDtypeStruct((4,) + g.shape[1:], g.dtype) for g in grads],
                    [pltpu.SemaphoreType.DMA((n, 4)), pltpu.SemaphoreType.DMA((n, 4))], stages,
                    collective_id=1, peers=_siblings)


def pair_sum(grad, from_sibling, name):
    _, r, cdim = grad.shape
    tr = min(r, ELEMENTWISE_ROWS)
    c_idx = lax.axis_index("c").astype(jnp.int32).reshape(1)

    def body(c_ref, g_ref, s_ref, o_ref, ob_ref):
        total = g_ref[...] + s_ref[...]
        o_ref[...] = total
        ob_ref[...] = total.astype(BF16)

    blk = lambda: pl.BlockSpec((1, tr, cdim), lambda q, i, cr: (q, i, 0))
    return pl.pallas_call(
        body,
        name=name,
        grid_spec=pltpu.PrefetchScalarGridSpec(
            num_scalar_prefetch=1,
            grid=(4, r // tr),
            in_specs=[pl.BlockSpec((1, tr, cdim), lambda q, i, cr: (2 * q + cr[0], i, 0)), blk()],
            out_specs=[blk(), blk()],
        ),
        out_shape=[jax.ShapeDtypeStruct((4, r, cdim), F32), jax.ShapeDtypeStruct((4, r, cdim), BF16)],
        compiler_params=_params(dimension_semantics=("arbitrary", "arbitrary")),
    )(c_idx, grad, from_sibling)


def to_chip_exchange(pairs):
    n = len(pairs)

    def stages(ins, outs, sems):
        send_sems, recv_sems = sems

        def copies():
            x, y, c, chips = _position()
            return [pltpu.make_async_remote_copy(
                src_ref=ins[a].at[2 * px + py], dst_ref=outs[a].at[j], send_sem=send_sems.at[a, j],
                recv_sem=recv_sems.at[a, j], device_id=(px, py, c), device_id_type=MESH)
                for a in range(n) for j, (px, py) in enumerate(chips)]

        def begin():
            for cp in copies():
                cp.start()

        def end():
            for cp in copies():
                cp.wait()

        return begin, None, end

    return Exchange(pairs, [jax.ShapeDtypeStruct((3,) + p.shape[1:], p.dtype) for p in pairs],
                    [pltpu.SemaphoreType.DMA((n, 3)), pltpu.SemaphoreType.DMA((n, 3))], stages,
                    collective_id=2, peers=_same_core_of_other_chips)


def run_exchange(ex, name):
    n_in, n_out = len(ex.arrays), len(ex.out_shape)

    def body(*refs):
        begin, middle, end = ex.stages(refs[:n_in], refs[n_in:n_in + n_out], refs[n_in + n_out:])
        ex.open()
        begin()
        for stage in _as_tuple(middle):
            stage()
        end()

    return pl.pallas_call(
        body,
        name=name,
        in_specs=[_any_spec()] * n_in,
        out_specs=[_any_spec()] * n_out,
        out_shape=ex.out_shape,
        scratch_shapes=ex.sems,
        compiler_params=pltpu.CompilerParams(collective_id=ex.collective_id),
    )(*ex.arrays)


def _as_tuple(stages):
    return () if stages is None else stages if isinstance(stages, tuple) else (stages,)


def _riding(body, n_in, n_out, n_scratch, ex, first, middle, last, late=None):
    if ex is None:
        return body
    r_in, r_out = len(ex.arrays), len(ex.out_shape)

    def wrapped(*refs):
        k_in, refs = refs[:n_in], refs[n_in:]
        e_in, refs = refs[:r_in], refs[r_in:]
        k_out, refs = refs[:n_out], refs[n_out:]
        e_out, refs = refs[:r_out], refs[r_out:]
        k_scr, e_sems = refs[:n_scratch], refs[n_scratch:]
        begin, mid, end = ex.stages(e_in, e_out, e_sems)

        @pl.when(first())
        def _():
            ex.open()
            begin()

        body(*k_in, *k_out, *k_scr)
        for stage, at in zip(_as_tuple(mid), (middle, late or last)):
            pl.when(at())(stage)
        pl.when(last())(end)

    return wrapped


def _ride_specs(ex):
    if ex is None:
        return [], [], [], [], []
    return [_any_spec()] * len(ex.arrays), [_any_spec()] * len(ex.out_shape), ex.out_shape, ex.sems, ex.arrays


def reduce_last(grad):
    _, r, cdim = grad.shape

    def body(g_hbm, own_ref, others_hbm, g_buf, to_sib, from_sib, send_buf, load_sem, send_sems, recv_sems):
        x, y, c, chips = _position()
        order = chips + [(x, y)]
        n_other = len(chips)
        sibling = (x, y, 1 - c)
        barrier = pltpu.get_barrier_semaphore()
        peers = [sibling] + [(px, py, c) for px, py in chips]
        for peer in peers:
            pl.semaphore_signal(barrier, inc=1, device_id=peer, device_id_type=MESH)
        pl.semaphore_wait(barrier, len(peers))

        def load(block):
            cp = pltpu.make_async_copy(g_hbm.at[block], g_buf, load_sem)
            cp.start()
            cp.wait()
            return g_buf[...]

        def to_sibling(j):
            return pltpu.make_async_remote_copy(
                src_ref=to_sib.at[j], dst_ref=from_sib.at[j], send_sem=send_sems.at[n_other + j],
                recv_sem=recv_sems.at[n_other + j], device_id=sibling, device_id_type=MESH)

        def to_chip(j):
            px, py = chips[j]
            return pltpu.make_async_remote_copy(
                src_ref=send_buf.at[j], dst_ref=others_hbm.at[j], send_sem=send_sems.at[j], recv_sem=recv_sems.at[j],
                device_id=(px, py, c), device_id_type=MESH)

        def hand_over(j):
            px, py = order[j]
            to_sib[j] = load(2 * (2 * px + py) + (1 - c)).astype(BF16)
            to_sibling(j).start()

        def pair_sum_of(j):
            px, py = order[j]
            to_sibling(j).wait_recv()
            total = load(2 * (2 * px + py) + c) + from_sib[j].astype(F32)
            if j < n_other:
                send_buf[j] = total.astype(BF16)
                to_chip(j).start()
            else:
                own_ref[0] = total

        hand_over(0)
        for j in range(1, len(order)):
            hand_over(j)
            pair_sum_of(j - 1)
        pair_sum_of(len(order) - 1)
        for j in range(len(order)):
            to_sibling(j).wait_send()
        for j in range(n_other):
            to_chip(j).wait()

    n_blocks = N_DEV // 2
    return pl.pallas_call(
        body,
        name="reduce_w_in",
        in_specs=[_any_spec()],
        out_specs=[_vmem_spec(), _any_spec()],
        out_shape=[jax.ShapeDtypeStruct((1, r, cdim), F32), jax.ShapeDtypeStruct((n_blocks - 1, r, cdim), BF16)],
        scratch_shapes=[pltpu.VMEM((r, cdim), F32), pltpu.VMEM((n_blocks, r, cdim), BF16),
                        pltpu.VMEM((n_blocks, r, cdim), BF16), pltpu.VMEM((n_blocks - 1, r, cdim), BF16),
                        pltpu.SemaphoreType.DMA(()), pltpu.SemaphoreType.DMA((2 * n_blocks - 1,)),
                        pltpu.SemaphoreType.DMA((2 * n_blocks - 1,))],
        compiler_params=pltpu.CompilerParams(collective_id=4, vmem_limit_bytes=VMEM_LIMIT),
    )(grad)


def _adamw(w, g, m, v):
    m = ADAM_B1 * m + (1.0 - ADAM_B1) * g
    v = ADAM_B2 * v + (1.0 - ADAM_B2) * (g * g)
    m_hat = m / (1.0 - ADAM_B1 ** ADAM_STEP)
    v_hat = v / (1.0 - ADAM_B2 ** ADAM_STEP)
    delta = -ADAM_LR * (m_hat / (jnp.sqrt(v_hat) + ADAM_EPS) + ADAM_WD * w)
    return delta, m, v


def sum_adamw(pairs, others, w, m, v, name):
    r, cdim = w.shape
    tr = min(r, ELEMENTWISE_ROWS // 2)
    if pairs.shape[0] == 1:
        chip_idx = jnp.zeros((1,), jnp.int32)
    else:
        chip_idx = (2 * lax.axis_index("x") + lax.axis_index("y")).astype(jnp.int32).reshape(1)

    def body(q_ref, p_ref, o_ref, w_ref, m_ref, v_ref, g_out, d_out, m_out, v_out):
        g = p_ref[0] + o_ref[0].astype(F32) + o_ref[1].astype(F32) + o_ref[2].astype(F32)
        g_out[...] = g
        d_out[...], m_out[...], v_out[...] = _adamw(w_ref[...], g, m_ref[...], v_ref[...])

    tile = lambda: pl.BlockSpec((tr, cdim), lambda i, qr: (i, 0))
    return pl.pallas_call(
        body,
        name=name,
        grid_spec=pltpu.PrefetchScalarGridSpec(
            num_scalar_prefetch=1,
            grid=(r // tr,),
            in_specs=[pl.BlockSpec((1, tr, cdim), lambda i, qr: (qr[0], i, 0)),
                      pl.BlockSpec((3, tr, cdim), lambda i, qr: (0, i, 0)), tile(), tile(), tile()],
            out_specs=[tile(), tile(), tile(), tile()],
        ),
        out_shape=[jax.ShapeDtypeStruct((r, cdim), F32)] * 4,
        compiler_params=_params(dimension_semantics=("arbitrary",)),
    )(chip_idx, pairs, others, w, m, v)


def small_exchange(packed):
    def stages(ins, outs, sems):
        send_sems, recv_sems, local_sem = sems
        (src,), (slots,) = ins, outs

        def copies():
            x, y, c, _ = _position()
            my_id = 4 * x + 2 * y + c
            sends, landings = [], []
            for rel in range(1, N_DEV):
                px = 1 - x if (rel >> 2) & 1 else x
                py = 1 - y if (rel >> 1) & 1 else y
                pc = 1 - c if rel & 1 else c
                peer = dict(send_sem=send_sems.at[rel - 1], recv_sem=recv_sems.at[rel - 1], device_id=(px, py, pc),
                            device_id_type=MESH)
                sends.append(pltpu.make_async_remote_copy(src_ref=src, dst_ref=slots.at[my_id], **peer))
                landings.append(pltpu.make_async_remote_copy(src_ref=src, dst_ref=slots.at[4 * px + 2 * py + pc], **peer))
            return pltpu.make_async_copy(src, slots.at[my_id], local_sem), sends, landings

        def begin():
            local, sends, _ = copies()
            local.start()
            for cp in sends:
                cp.start()

        def end():
            local, sends, landings = copies()
            for cp in landings:
                cp.wait_recv()
            for cp in sends:
                cp.wait_send()
            local.wait()

        return begin, None, end

    return Exchange([packed], [jax.ShapeDtypeStruct((N_DEV,) + packed.shape, packed.dtype)],
                    [pltpu.SemaphoreType.DMA((N_DEV - 1,)), pltpu.SemaphoreType.DMA((N_DEV - 1,)),
                     pltpu.SemaphoreType.DMA(())], stages, collective_id=3, peers=_all_others)


def small_adamw(slots, w, m, v):
    def body(r_ref, w_ref, m_ref, v_ref, g_out, d_out, m_out, v_out, loss_out):
        red = r_ref[0]
        for k in range(1, N_DEV):
            red = red + r_ref[k]
        wv = w_ref[...]
        lb = _lower_bound(jnp.concatenate([wv[5:6, :HGRN_WIDTH], wv[5:6, HGRN_WIDTH:]], axis=0))
        t = red[5:6, :HGRN_WIDTH] * lb * (1.0 - lb)
        row = lax.broadcasted_iota(jnp.int32, red.shape, 0)
        g = jnp.where(row == 5, jnp.concatenate([t, -t], axis=1), jnp.where(row >= 6, 0.0, red))
        g_out[...] = g
        d_out[...], m_out[...], v_out[...] = _adamw(wv, g, m_ref[...], v_ref[...])
        loss = jnp.sum(red[6:7, :], axis=-1, keepdims=True) * (0.5 / D_MODEL)
        loss_out[...] = jnp.broadcast_to(loss, loss_out.shape)

    return pl.pallas_call(
        body,
        name="small_adamw",
        in_specs=[_vmem_spec()] * 4,
        out_specs=[_vmem_spec()] * 5,
        out_shape=[jax.ShapeDtypeStruct(w.shape, F32)] * 4 + [jax.ShapeDtypeStruct((SUBLANES, LANES), F32)],
    )(slots, w, m, v)


def _pack_small(g1, gp, g_pre, g_post, an, hn, logits_or_dlb, extra=None):
    row5 = logits_or_dlb.reshape(1, -1)
    row5 = jnp.pad(row5, ((0, 0), (0, D_MODEL - row5.shape[1])))
    row6 = jnp.zeros((1, D_MODEL), F32) if extra is None else extra
    return jnp.concatenate([g1, gp, g_pre, g_post, jnp.concatenate([an, hn], axis=1), row5, row6,
                            jnp.zeros((1, D_MODEL), F32)], axis=0)


def _unpack_small(p):
    return dict(mix_pre_norm=p[0:1], mix_post_norm=p[1:2], mlp_pre_norm=p[2:3], mlp_post_norm=p[3:4],
                attn_out_norm=p[4:5, :ATTN_WIDTH], hgrn_out_norm=p[4:5, ATTN_WIDTH:],
                hgrn_lb_logits=p[5].reshape(2, HGRN_WIDTH))


BIG = ("w_in", "w_out", "w_ff1", "w_ff2")
ORDER = ("mix_pre_norm", "w_in", "attn_out_norm", "hgrn_lb_logits", "hgrn_out_norm", "w_out", "mix_post_norm",
         "mlp_pre_norm", "w_ff1", "w_ff2", "mlp_post_norm")


def kernel(x, mix_pre_norm, w_in, attn_out_norm, hgrn_lb_logits, hgrn_out_norm, w_out, mix_post_norm, mlp_pre_norm, w_ff1, w_ff2, mlp_post_norm, loss_target, m_mix_pre_norm, m_w_in, m_attn_out_norm, m_hgrn_lb_logits, m_hgrn_out_norm, m_w_out, m_mix_post_norm, m_mlp_pre_norm, m_w_ff1, m_w_ff2, m_mlp_post_norm, v_mix_pre_norm, v_w_in, v_attn_out_norm, v_hgrn_lb_logits, v_hgrn_out_norm, v_w_out, v_mix_post_norm, v_mlp_pre_norm, v_w_ff1, v_w_ff2, v_mlp_post_norm):
    w = dict(w_in=w_in[0], w_out=w_out[0], w_ff1=w_ff1[0], w_ff2=w_ff2[0])
    m = dict(w_in=m_w_in[0], w_out=m_w_out[0], w_ff1=m_w_ff1[0], w_ff2=m_w_ff2[0])
    v = dict(w_in=v_w_in[0], w_out=v_w_out[0], w_ff1=v_w_ff1[0], w_ff2=v_w_ff2[0])

    dx, big, small_slots = train_step(x[0], loss_target[0], mix_pre_norm, attn_out_norm, hgrn_lb_logits, hgrn_out_norm,
                                      mix_post_norm, mlp_pre_norm, mlp_post_norm, w, m, v)

    pack = lambda a, b, c2, d, e, f, g: _pack_small(a, b, c2, d, e, f, g)
    w_s = pack(mix_pre_norm, mix_post_norm, mlp_pre_norm, mlp_post_norm, attn_out_norm, hgrn_out_norm, hgrn_lb_logits)
    m_s = pack(m_mix_pre_norm, m_mix_post_norm, m_mlp_pre_norm, m_mlp_post_norm, m_attn_out_norm, m_hgrn_out_norm,
               m_hgrn_lb_logits)
    v_s = pack(v_mix_pre_norm, v_mix_post_norm, v_mlp_pre_norm, v_mlp_post_norm, v_attn_out_norm, v_hgrn_out_norm,
               v_hgrn_lb_logits)
    g_s, d_s, nm_s, nv_s, loss = small_adamw(small_slots, w_s, m_s, v_s)
    small_out = [_unpack_small(t) for t in (g_s, d_s, nm_s, nv_s)]

    outs = [loss[0, 0], dx[None]]
    for kind in range(4):
        for name in ORDER:
            outs.append(big[name][kind][None] if name in BIG else small_out[kind][name])
    return tuple(outs)
```

```python
import jax
import jax.numpy as jnp
from jax import lax
from jax.experimental import pallas as pl
from jax.experimental.pallas import tpu as pltpu

F32 = jnp.float32
BF16 = jnp.bfloat16

D_MODEL = 1024
ATTN_WIDTH = 512
ATTN_HEAD_DIM = 64
ATTN_HEADS = 8
ATTN_BLOCK = 128
DILATIONS = (1, 4, 16)
HGRN_WIDTH = 512
HGRN_HEADS = 4
HGRN_HEAD_DIM = 128
HGRN_CHUNK = 64
IN_PROJ_WIDTH = 3584
D_FF = 4096
RMS_EPS = 1e-6
N_DEV = 8
ADAM_LR = 0.001
ADAM_B1 = 0.9
ADAM_B2 = 0.999
ADAM_EPS = 1e-08
ADAM_WD = 0.01
ADAM_STEP = 10

SUBLANES = 8
LANES = 128
COLUMN_UNROLL = 8
HGRN_CHUNKS_PER_STEP = 2
SUB_BLOCK = 16
TOKEN_TILE = 512
ELEMENTWISE_ROWS = 1024
MLP_TILE = 256
PROJ_TILE = 512
VMEM_BYTES_V7X = 64 * 1024 * 1024
VMEM_LIMIT = VMEM_BYTES_V7X // 8 * 7
NEG_BIG = -1e30
MESH = pl.DeviceIdType.MESH


def _params(ride=None, **kw):
    if ride is not None:
        kw["collective_id"] = ride.collective_id
    return pltpu.CompilerParams(vmem_limit_bytes=VMEM_LIMIT, **kw)


def _vmem_spec():
    return pl.BlockSpec(memory_space=pltpu.VMEM)


def _dot(a, b):
    return jnp.dot(a, b, preferred_element_type=F32)


def _dot_nt(a, b):
    return lax.dot_general(a, b, (((1,), (1,)), ((), ())), preferred_element_type=F32)


def _dot_tn(a, b):
    return lax.dot_general(a, b, (((0,), (0,)), ((), ())), preferred_element_type=F32)


def _sigmoid(x):
    return 1.0 / (1.0 + jnp.exp(-x))


def _rms_fwd(x, gain, width):
    r = lax.rsqrt(jnp.sum(x * x, axis=-1, keepdims=True) * (1.0 / width) + RMS_EPS)
    return x * r * gain


def _rms_bwd(dy, x, gain, width):
    r = lax.rsqrt(jnp.sum(x * x, axis=-1, keepdims=True) * (1.0 / width) + RMS_EPS)
    xhat = x * r
    dxhat = dy * gain
    dx = r * (dxhat - xhat * (jnp.sum(dxhat * xhat, axis=-1, keepdims=True) * (1.0 / width)))
    return dx, dy * xhat


def _split3(x):
    hi = x.astype(BF16)
    r1 = x - hi.astype(F32)
    mid = r1.astype(BF16)
    lo = (r1 - mid.astype(F32)).astype(BF16)
    return hi, mid, lo


def _tri_sum(tri_bf16, x):
    hi, mid, lo = _split3(x)
    return _dot(tri_bf16, hi) + _dot(tri_bf16, mid) + _dot(tri_bf16, lo)


def _dilated_spec(d, tm, width):
    return pl.BlockSpec((d, tm // d, width), lambda i: (0, i, 0))


def _lane_blocks(ref, value):
    for c in range(ref.shape[0]):
        ref[c] = value[:, c * LANES:(c + 1) * LANES]


def _to_dilated(src_ref, dst_ref, d, tm, cast=None):
    for r in range(d):
        for c in range(src_ref.shape[0]):
            v = src_ref[c] if d == 1 else src_ref[c, pl.ds(r, tm // d, stride=d), :]
            dst_ref[r, :, c * LANES:(c + 1) * LANES] = v if cast is None else v.astype(cast)


def _from_dilated(src_ref, scratch_ref, d, tm):
    if d == 1:
        return src_ref[0].astype(F32)
    nblk = scratch_ref.shape[0]
    for r in range(d):
        for c in range(nblk):
            scratch_ref[c, pl.ds(r, tm // d, stride=d), :] = src_ref[r, :, c * LANES:(c + 1) * LANES].astype(F32)
    return jnp.concatenate([scratch_ref[c] for c in range(nblk)], axis=1)


def in_proj_fwd(x, g1, w_in_b, ride=None):
    s = x.shape[0]
    tm = PROJ_TILE
    qkv_w = 3 * ATTN_WIDTH
    hg_w = IN_PROJ_WIDTH - qkv_w

    def body(x_ref, g_ref, w_ref, hg_ref, h_ref, *rest):
        qkv_refs, qkv_scr = rest[:len(DILATIONS)], rest[len(DILATIONS)]
        h = _rms_fwd(x_ref[...], g_ref[...], D_MODEL).astype(BF16)
        h_ref[...] = h
        proj = _dot(h, w_ref[...])
        hg_ref[...] = proj[:, qkv_w:]
        _lane_blocks(qkv_scr, proj[:, :qkv_w])
        for d, ref in zip(DILATIONS, qkv_refs):
            _to_dilated(qkv_scr, ref, d, tm, cast=BF16)

    n_steps = s // tm
    step = lambda k: (lambda: pl.program_id(0) == k)
    e_in, e_out, e_shape, e_scr, e_args = _ride_specs(ride)
    return pl.pallas_call(
        _riding(body, 3, 2 + len(DILATIONS), 1, ride, step(0), step(n_steps // 2), step(n_steps - 1),
                late=step(n_steps - 2)),
        name="in_proj_fwd",
        grid=(n_steps,),
        in_specs=[
            pl.BlockSpec((tm, D_MODEL), lambda i: (i, 0)),
            pl.BlockSpec((1, D_MODEL), lambda i: (0, 0)),
            _vmem_spec(),
        ] + e_in,
        out_specs=[
            pl.BlockSpec((tm, hg_w), lambda i: (i, 0)),
            pl.BlockSpec((tm, D_MODEL), lambda i: (i, 0)),
        ] + [_dilated_spec(d, tm, qkv_w) for d in DILATIONS] + e_out,
        out_shape=[jax.ShapeDtypeStruct((s, hg_w), F32), jax.ShapeDtypeStruct((s, D_MODEL), BF16)] + [
            jax.ShapeDtypeStruct((d, s // d, qkv_w), BF16) for d in DILATIONS] + e_shape,
        scratch_shapes=[pltpu.VMEM((qkv_w // LANES, tm, LANES), F32)] + e_scr,
        compiler_params=_params(ride, dimension_semantics=("arbitrary",)),
    )(x, g1, w_in_b, *e_args)


ATTN_SCALE = ATTN_HEAD_DIM ** -0.5


def _fill_attn_bias(bias_ref, dilation):
    qi = lax.broadcasted_iota(jnp.int32, (ATTN_BLOCK, 2 * ATTN_BLOCK), 0)
    kj = lax.broadcasted_iota(jnp.int32, (ATTN_BLOCK, 2 * ATTN_BLOCK), 1)
    dist = qi + ATTN_BLOCK - kj
    valid = (dist >= 0) & (dist <= ATTN_BLOCK)
    for head in range(ATTN_HEADS):
        slope = 2.0 ** (-8.0 * (head + 1) / ATTN_HEADS)
        bias = jnp.where(valid, dist.astype(F32) * (-slope * dilation), NEG_BIG)
        bias_ref[0, head] = bias
        bias_ref[1, head] = jnp.where(kj >= ATTN_BLOCK, bias, NEG_BIG)


def _stack_heads(x):
    low = _lane_half(x.shape, 0)
    zero = jnp.zeros_like(x)
    return jnp.concatenate([jnp.where(low, x, zero), jnp.where(low, zero, x)], axis=0)


def _unstack_heads(y):
    half = y.shape[0] // 2
    return jnp.where(_lane_half((half, y.shape[1]), 0), y[:half], y[half:])


def _attn_scores(q_stack, kcat, bias_ref, pair, first_block):
    f = first_block.astype(jnp.int32)
    bias = jnp.concatenate([bias_ref[f, 2 * pair], bias_ref[f, 2 * pair + 1]], axis=0)
    return _dot_nt(q_stack, kcat) + bias


def _lane_half(shape, sub):
    lane = lax.broadcasted_iota(jnp.int32, shape, 1)
    return (lane < ATTN_HEAD_DIM) if sub == 0 else (lane >= ATTN_HEAD_DIM)


def _sub_block(col, row):
    return pl.BlockSpec((None, ATTN_BLOCK, ATTN_WIDTH), lambda r, n: (r, row(n), col))


def attn_fwd(qkv, dilation):
    d, length, _ = qkv.shape
    assert d == dilation
    nb = length // ATTN_BLOCK

    def body(q_ref, kc_ref, kp_ref, vc_ref, vp_ref, o_ref, lse_ref, bias_ref):
        @pl.when((pl.program_id(0) == 0) & (pl.program_id(1) == 0))
        def _():
            _fill_attn_bias(bias_ref, d)

        first = pl.program_id(1) == 0
        for pair in range(ATTN_HEADS // 2):
            lanes = slice(pair * LANES, (pair + 1) * LANES)
            q_stack = _stack_heads(q_ref[:, lanes] * ATTN_SCALE)
            kcat = jnp.concatenate([kp_ref[:, lanes], kc_ref[:, lanes]], axis=0)
            vcat = jnp.concatenate([vp_ref[:, lanes], vc_ref[:, lanes]], axis=0)
            sc = _attn_scores(q_stack, kcat, bias_ref, pair, first)
            m = jnp.max(sc, axis=-1, keepdims=True)
            p = jnp.exp(sc - m)
            den = jnp.sum(p, axis=-1, keepdims=True)
            o_ref[:, lanes] = _unstack_heads(_dot(p.astype(BF16), vcat) / den).astype(BF16)
            lse_ref[:, lanes] = _unstack_heads(jnp.broadcast_to(m + jnp.log(den), (2 * ATTN_BLOCK, LANES)))

    cur = lambda n: n
    prev = lambda n: jnp.maximum(n - 1, 0)
    return pl.pallas_call(
        body,
        name=f"attn_fwd_d{d}",
        grid=(d, nb),
        in_specs=[_sub_block(0, cur), _sub_block(1, cur), _sub_block(1, prev), _sub_block(2, cur), _sub_block(2, prev)],
        out_specs=[_sub_block(0, cur), _sub_block(0, cur)],
        out_shape=[jax.ShapeDtypeStruct((d, length, ATTN_WIDTH), BF16), jax.ShapeDtypeStruct((d, length, ATTN_WIDTH), F32)],
        scratch_shapes=[pltpu.VMEM((2, ATTN_HEADS, ATTN_BLOCK, 2 * ATTN_BLOCK), F32)],
        compiler_params=_params(dimension_semantics=("arbitrary", "arbitrary")),
    )(qkv, qkv, qkv, qkv, qkv)


def attn_bwd(qkv, d_out, lse, delta, dilation, ride=None):
    d, length, _ = qkv.shape
    assert d == dilation
    nb = length // ATTN_BLOCK

    steps = d * nb + 1

    def body(q_ref, kc_ref, kp_ref, vc_ref, vp_ref, do_ref, lse_ref, dl_ref, dq_ref, dk_ref, dv_ref, ck_ref, cv_ref,
             bias_ref):
        t = pl.program_id(0)

        @pl.when(t == 0)
        def _():
            ck_ref[...] = jnp.zeros_like(ck_ref)
            cv_ref[...] = jnp.zeros_like(cv_ref)
            _fill_attn_bias(bias_ref, d)

        @pl.when(t < steps - 1)
        def _():
            first = t % nb == 0
            for pair in range(ATTN_HEADS // 2):
                lanes = slice(pair * LANES, (pair + 1) * LANES)
                q_stack = _stack_heads(q_ref[:, lanes] * ATTN_SCALE)
                do_stack = _stack_heads(do_ref[:, lanes])
                kcat = jnp.concatenate([kp_ref[:, lanes], kc_ref[:, lanes]], axis=0)
                vcat = jnp.concatenate([vp_ref[:, lanes], vc_ref[:, lanes]], axis=0)
                col_a, col_b = 2 * pair, 2 * pair + 1
                lse_col = jnp.concatenate([lse_ref[:, col_a:col_a + 1], lse_ref[:, col_b:col_b + 1]], axis=0)
                dl_col = jnp.concatenate([dl_ref[:, col_a:col_a + 1], dl_ref[:, col_b:col_b + 1]], axis=0)
                p = jnp.exp(_attn_scores(q_stack, kcat, bias_ref, pair, first) - lse_col)
                ds = (p * (_dot_nt(do_stack, vcat) - dl_col)).astype(BF16)
                dq_ref[:, lanes] = (_unstack_heads(_dot(ds, kcat)) * ATTN_SCALE).astype(BF16)
                dk_cat = _dot_tn(ds, q_stack)
                dv_cat = _dot_tn(p.astype(BF16), do_stack)
                dk_ref[:, lanes] = (ck_ref[:, lanes] + dk_cat[:ATTN_BLOCK]).astype(BF16)
                dv_ref[:, lanes] = (cv_ref[:, lanes] + dv_cat[:ATTN_BLOCK]).astype(BF16)
                ck_ref[:, lanes] = dk_cat[ATTN_BLOCK:]
                cv_ref[:, lanes] = dv_cat[ATTN_BLOCK:]

        @pl.when(t == steps - 1)
        def _():
            dk_ref[...] = ck_ref[...].astype(BF16)
            dv_ref[...] = cv_ref[...].astype(BF16)

    blk = (ATTN_BLOCK, ATTN_WIDTH)

    def spec(col, shift, width=ATTN_WIDTH):
        def index(t):
            f = jnp.minimum(t, steps - 2) if shift > -2 else jnp.maximum(t - 1, 0)
            r, n = f // nb, f % nb
            return (r, jnp.maximum(n - 1, 0) if shift == -1 else n, col)
        return pl.BlockSpec((None, ATTN_BLOCK, width), index)

    step = lambda k: (lambda: pl.program_id(0) == k)
    e_in, e_out, e_shape, e_scr, e_args = _ride_specs(ride)
    return pl.pallas_call(
        _riding(body, 8, 3, 3, ride, step(0), step(steps // 2), step(steps - 1)),
        name=f"attn_bwd_d{d}",
        grid=(steps,),
        in_specs=[spec(0, 0), spec(1, 0), spec(1, -1), spec(2, 0), spec(2, -1), spec(0, 0), spec(0, 0, LANES),
                  spec(0, 0, LANES)] + e_in,
        out_specs=[spec(0, 0), spec(0, -2), spec(0, -2)] + e_out,
        out_shape=[jax.ShapeDtypeStruct((d, length, ATTN_WIDTH), BF16)] * 3 + e_shape,
        scratch_shapes=[pltpu.VMEM(blk, F32), pltpu.VMEM(blk, F32),
                        pltpu.VMEM((2, ATTN_HEADS, ATTN_BLOCK, 2 * ATTN_BLOCK), F32)] + e_scr,
        compiler_params=_params(ride, dimension_semantics=("arbitrary",)),
    )(qkv, qkv, qkv, qkv, qkv, d_out, lse, delta, *e_args)


def _lower_bound(logits):
    return _sigmoid(logits[0:1, :] - logits[1:2, :])


def _hgrn_gates(q, fp, lb):
    sq = _sigmoid(q)
    qf = q * sq
    sig = _sigmoid(fp)
    sig_neg = _sigmoid(-fp)
    kf = (1.0 - lb) * sig_neg
    log_sig = jnp.minimum(fp, 0.0) - jnp.log(1.0 + jnp.exp(-jnp.abs(fp)))
    a = jnp.log(lb)
    c = jnp.log(1.0 - lb) + log_sig
    log_f = jnp.maximum(a, c) + jnp.log(1.0 + jnp.exp(-jnp.abs(a - c)))
    return sq, qf, (sig, sig_neg, c), log_f, kf


def _tril_bf16(n, upper=False):
    r = lax.broadcasted_iota(jnp.int32, (n, n), 0)
    c = lax.broadcasted_iota(jnp.int32, (n, n), 1)
    keep = (c >= r) if upper else (c <= r)
    return jnp.where(keep, 1.0, 0.0).astype(BF16)


def _hgrn_diagonal_loops(c_len, diagonal):
    for half in range(SUB_BLOCK // SUBLANES):
        def step(jj, carry, half=half):
            j = half * SUBLANES + jj
            for i in range(c_len // SUB_BLOCK):
                diagonal(slice(i * SUB_BLOCK + half * SUBLANES, (i + 1) * SUB_BLOCK), j, i * SUB_BLOCK + j)
            return carry

        lax.fori_loop(0, SUBLANES, step, 0, unroll=COLUMN_UNROLL)


def _hgrn_off_diagonal(b, qf, kf):
    c_len, width = b.shape
    edges = [b[0:1, :]] + [b[i * SUB_BLOCK - 1:i * SUB_BLOCK, :] for i in range(1, c_len // SUB_BLOCK)]
    eq = jnp.exp(b - jnp.concatenate([jnp.broadcast_to(e, (SUB_BLOCK, width)) for e in edges], axis=0))
    q_til = qf * eq
    k_til, ek = [], []
    for i in range(1, c_len // SUB_BLOCK):
        n = i * SUB_BLOCK
        e = jnp.exp(edges[i] - b[:n, :])
        ek.append(e)
        k_til.append(jnp.concatenate([kf[:n, :] * e, jnp.zeros((2 * c_len - n, width), F32)], axis=0))
    return q_til, k_til, eq, ek


def _split2(x):
    hi = x.astype(BF16)
    return hi, (x - hi.astype(F32)).astype(BF16)


def hgrn_fwd(proj, lb, ride=None):
    s = proj.shape[0]
    c_len, nh, hd = HGRN_CHUNK, HGRN_HEADS, HGRN_HEAD_DIM
    n_chunks = s // c_len
    col0 = 0

    cps = 2 * HGRN_CHUNKS_PER_STEP
    n_steps = n_chunks // cps

    def body(q_ref, f_ref, i_ref, lb_ref, o_ref, st_out_ref, a_out_ref, st_ref, b_ref, qf_ref, kf_ref, a_ref):
        @pl.when(pl.program_id(0) == 0)
        def _():
            st_ref[...] = jnp.zeros_like(st_ref)

        lbv = _lower_bound(lb_ref[...])
        for u in range(cps):
            rs = slice(u * c_len, (u + 1) * c_len)
            b_u, qf_u, kf_u, a_u = b_ref.at[u], qf_ref.at[u], kf_ref.at[u], a_ref.at[u]
            _, qf, _, log_f, kf = _hgrn_gates(q_ref[rs, :], f_ref[rs, :], lbv)
            b = _tri_sum(_tril_bf16(c_len), log_f)
            b_u[...] = b
            qf_u[...] = qf
            kf_u[...] = kf
            a_u[...] = jnp.zeros_like(a_u)

            def diagonal(rows, j, key, b_u=b_u, qf_u=qf_u, kf_u=kf_u, a_u=a_u):
                bj = b_u[pl.ds(key, 1), :]
                kj = kf_u[pl.ds(key, 1), :]
                nrow = rows.stop - rows.start
                t_loc = lax.broadcasted_iota(jnp.int32, (nrow, nh * hd), 0) + (rows.start % SUB_BLOCK)
                e = jnp.exp(jnp.where(t_loc >= j, b_u[rows, :] - bj, NEG_BIG))
                prod = qf_u[rows, :] * kj * e
                lane = lax.broadcasted_iota(jnp.int32, (nrow, hd), 1)
                for h in range(nh):
                    col = jnp.sum(prod[:, h * hd:(h + 1) * hd], axis=-1, keepdims=True)
                    a_u[h, rows, :] = jnp.where(lane == key, col, a_u[h, rows, :])

            _hgrn_diagonal_loops(c_len, diagonal)
            q_til, k_til, _, _ = _hgrn_off_diagonal(b, qf, kf)
            q_til = q_til.astype(BF16)
            k_til = [k.astype(BF16) for k in k_til]

            b_last = b[c_len - 1:c_len, :]
            qb = (qf * jnp.exp(b)).astype(BF16)
            kb2 = (kf * jnp.exp(b_last - b)).astype(BF16)
            vf = i_ref[rs, :].astype(BF16)
            for h in range(nh):
                hs = slice(h * hd, (h + 1) * hd)
                st = st_ref[h]
                st_out_ref[u, h] = st
                off = [jnp.zeros((SUB_BLOCK, hd), F32)]
                for i in range(1, c_len // SUB_BLOCK):
                    off.append(_dot_nt(q_til[i * SUB_BLOCK:(i + 1) * SUB_BLOCK, hs], k_til[i - 1][:, hs]))
                a_h = a_u[h] + jnp.concatenate(off, axis=0)
                a_out_ref[rs, hs] = a_h
                o_ref[rs, hs] = _dot_nt(qb[:, hs], st.astype(BF16)) + _dot(a_h[:, :c_len].astype(BF16), vf[:, hs])
                st_ref[h] = st * jnp.exp(b_last[:, hs]) + _dot_tn(vf[:, hs], kb2[:, hs])

    blk = (cps * c_len, HGRN_WIDTH)
    sblk = (cps, c_len, HGRN_WIDTH)
    step = lambda k: (lambda: pl.program_id(0) == k)
    e_in, e_out, e_shape, e_scr, e_args = _ride_specs(ride)
    return pl.pallas_call(
        _riding(body, 4, 3, 5, ride, step(0), step(n_steps // 2), step(n_steps - 1), late=step((3 * n_steps) // 4)),
        name="hgrn_fwd",
        grid=(n_steps,),
        in_specs=[
            pl.BlockSpec(blk, lambda c: (c, col0)),
            pl.BlockSpec(blk, lambda c: (c, col0 + 1)),
            pl.BlockSpec(blk, lambda c: (c, col0 + 2)),
            pl.BlockSpec((2, HGRN_WIDTH), lambda c: (0, 0)),
        ] + e_in,
        out_specs=[
            pl.BlockSpec(blk, lambda c: (c, 0)),
            pl.BlockSpec((cps, nh, hd, hd), lambda c: (c, 0, 0, 0)),
            pl.BlockSpec(blk, lambda c: (c, 0)),
        ] + e_out,
        out_shape=[
            jax.ShapeDtypeStruct((s, HGRN_WIDTH), F32),
            jax.ShapeDtypeStruct((n_chunks, nh, hd, hd), F32),
            jax.ShapeDtypeStruct((s, nh * hd), F32),
        ] + e_shape,
        scratch_shapes=[
            pltpu.VMEM((nh, hd, hd), F32),
            pltpu.VMEM(sblk, F32),
            pltpu.VMEM(sblk, F32),
            pltpu.VMEM(sblk, F32),
            pltpu.VMEM((cps, nh, c_len, hd), F32),
        ] + e_scr,
        compiler_params=_params(ride, dimension_semantics=("arbitrary",)),
    )(proj, proj, proj, lb, *e_args)


def hgrn_bwd(proj, lb, d_o, states, a_mat, ride=None):
    s = proj.shape[0]
    c_len, nh, hd = HGRN_CHUNK, HGRN_HEADS, HGRN_HEAD_DIM
    n_chunks = s // c_len
    col0 = 0
    cps = HGRN_CHUNKS_PER_STEP
    n_steps = n_chunks // cps
    last = n_steps - 1

    def body(q_ref, f_ref, i_ref, lb_ref, do_ref, st_in_ref, a_in_ref, dq_ref, df_ref, di_ref, dlb_ref,
             dst_ref, b_ref, qf_ref, kf_ref, da_ref, dqi_ref, dki_ref):
        @pl.when(pl.program_id(0) == 0)
        def _():
            dst_ref[...] = jnp.zeros_like(dst_ref)
            dlb_ref[...] = jnp.zeros_like(dlb_ref)

        lbv = _lower_bound(lb_ref[...])
        for u in reversed(range(cps)):
            rs = slice(u * c_len, (u + 1) * c_len)
            b_u, qf_u, kf_u, da_u, dqi_u, dki_u = (b_ref.at[u], qf_ref.at[u], kf_ref.at[u], da_ref.at[u], dqi_ref.at[u],
                                                   dki_ref.at[u])
            q = q_ref[rs, :]
            sq, qf, (sig, sig_neg, log_c), log_f, kf = _hgrn_gates(q, f_ref[rs, :], lbv)
            b = _tri_sum(_tril_bf16(c_len), log_f)
            b_u[...] = b
            qf_u[...] = qf
            kf_u[...] = kf
            b_last = b[c_len - 1:c_len, :]
            eb = jnp.exp(b)
            ebl = jnp.exp(b_last - b)
            qb = qf * eb
            kb2 = kf * ebl
            vf = i_ref[rs, :]
            d_o = do_ref[rs, :]
            qb_b, kb2_b, vf_b, do_b = qb.astype(BF16), kb2.astype(BF16), vf.astype(BF16), d_o.astype(BF16)
            tq = lax.broadcasted_iota(jnp.int32, (c_len, hd), 0)
            lane = lax.broadcasted_iota(jnp.int32, (c_len, hd), 1)

            dqb_parts, dvf_parts, dkb2_parts, dbl_parts = [], [], [], []
            for h in range(nh):
                hs = slice(h * hd, (h + 1) * hd)
                st = st_in_ref[u, h]
                dst = dst_ref[h]
                st_b, dst_b = st.astype(BF16), dst.astype(BF16)
                a_h = a_in_ref[rs, hs][:, :c_len].astype(BF16)
                dqb_parts.append(_dot(do_b[:, hs], st_b))
                dvf_parts.append(_dot_tn(a_h, do_b[:, hs]) + _dot_nt(kb2_b[:, hs], dst_b))
                dkb2_parts.append(_dot(vf_b[:, hs], dst_b))
                da = _dot_nt(do_b[:, hs], vf_b[:, hs])
                da = jnp.concatenate([da, jnp.zeros((c_len, hd - c_len), F32)], axis=1)
                da_u[h] = jnp.where(tq >= lane, da, 0.0)
                dbl_parts.append(jnp.sum(dst * st, axis=0, keepdims=True) * jnp.exp(b_last[:, hs]))
                dst_ref[h] = dst * jnp.exp(b_last[:, hs]) + _dot_tn(do_b[:, hs], qb_b[:, hs])
            dqb = jnp.concatenate(dqb_parts, axis=1)
            dvf = jnp.concatenate(dvf_parts, axis=1)
            dkb2 = jnp.concatenate(dkb2_parts, axis=1)
            dbl = jnp.concatenate(dbl_parts, axis=1) + jnp.sum(dkb2 * kb2, axis=0, keepdims=True)

            dqi_u[...] = jnp.zeros_like(dqi_u)
            t_idx = lax.broadcasted_iota(jnp.int32, (c_len, nh * hd), 0)

            def diagonal(rows, j, key, b_u=b_u, qf_u=qf_u, kf_u=kf_u, da_u=da_u, dqi_u=dqi_u, dki_u=dki_u):
                bj = b_u[pl.ds(key, 1), :]
                kj = kf_u[pl.ds(key, 1), :]
                nrow = rows.stop - rows.start
                t_loc = lax.broadcasted_iota(jnp.int32, (nrow, nh * hd), 0) + (rows.start % SUB_BLOCK)
                e = jnp.exp(jnp.where(t_loc >= j, b_u[rows, :] - bj, NEG_BIG))
                lane_r = lax.broadcasted_iota(jnp.int32, (nrow, hd), 1)
                cols = [jnp.sum(jnp.where(lane_r == key, da_u[h, rows, :], 0.0), axis=-1, keepdims=True)
                        for h in range(nh)]
                w = e * jnp.concatenate([jnp.broadcast_to(cc, (nrow, hd)) for cc in cols], axis=1)
                dqi_u[rows, :] += w * kj
                dki_u[pl.ds(key, 1), :] = jnp.sum(w * qf_u[rows, :], axis=0, keepdims=True)

            _hgrn_diagonal_loops(c_len, diagonal)

            q_til, k_til, eq, ek = _hgrn_off_diagonal(b, qf, kf)
            q_hi, q_lo = _split2(q_til)
            k_pairs = [_split2(k) for k in k_til]
            n_sub = c_len // SUB_BLOCK
            dq_heads, dk_heads = [], []
            for h in range(nh):
                hs = slice(h * hd, (h + 1) * hd)
                dq_rows = [jnp.zeros((SUB_BLOCK, hd), F32)]
                dk_h = jnp.zeros((c_len, hd), F32)
                for i in range(1, n_sub):
                    rows = slice(i * SUB_BLOCK, (i + 1) * SUB_BLOCK)
                    n = i * SUB_BLOCK
                    da_i = da_u[h, rows, :].astype(BF16)
                    k_hi, k_lo = k_pairs[i - 1]
                    dq_rows.append((_dot(da_i, k_hi[:, hs]) + _dot(da_i, k_lo[:, hs])) * eq[rows, hs])
                    dk_t = (_dot_tn(da_i, q_hi[rows, hs]) + _dot_tn(da_i, q_lo[rows, hs]))[:n, :] * ek[i - 1][:, hs]
                    dk_h = dk_h + jnp.concatenate([dk_t, jnp.zeros((c_len - n, hd), F32)], axis=0)
                dq_heads.append(jnp.concatenate(dq_rows, axis=0))
                dk_heads.append(dk_h)
            dq_intra = dqi_u[...] + jnp.concatenate(dq_heads, axis=1)
            dk_intra = dki_u[...] + jnp.concatenate(dk_heads, axis=1)

            db = dqb * qb + qf * dq_intra - kf * dk_intra - dkb2 * kb2
            db = db + jnp.where(t_idx == c_len - 1, dbl, 0.0)
            dg = _tri_sum(_tril_bf16(c_len, upper=True), db)
            dqf = dqb * eb + dq_intra
            dkf = dkb2 * ebl + dk_intra
            dq_ref[rs, :] = (dqf * (sq * (1.0 + q * (1.0 - sq)))).astype(BF16)
            df_ref[rs, :] = (sig_neg * (dg * jnp.exp(log_c - log_f) - dkf * (1.0 - lbv) * sig)).astype(BF16)
            di_ref[rs, :] = dvf.astype(BF16)
            dlb_ref[...] += jnp.sum(sig_neg * (dg * jnp.exp(-log_f) - dkf), axis=0, keepdims=True)

    blk = (cps * c_len, HGRN_WIDTH)
    sblk = (cps, c_len, HGRN_WIDTH)
    rev = lambda c: last - c
    step = lambda k: (lambda: pl.program_id(0) == k)
    e_in, e_out, e_shape, e_scr, e_args = _ride_specs(ride)
    return pl.pallas_call(
        _riding(body, 7, 4, 7, ride, step(0), step(n_steps // 2), step(last)),
        name="hgrn_bwd",
        grid=(n_steps,),
        in_specs=[
            pl.BlockSpec(blk, lambda c: (rev(c), col0)),
            pl.BlockSpec(blk, lambda c: (rev(c), col0 + 1)),
            pl.BlockSpec(blk, lambda c: (rev(c), col0 + 2)),
            pl.BlockSpec((2, HGRN_WIDTH), lambda c: (0, 0)),
            pl.BlockSpec(blk, lambda c: (rev(c), 0)),
            pl.BlockSpec((cps, nh, hd, hd), lambda c: (rev(c), 0, 0, 0)),
            pl.BlockSpec(blk, lambda c: (rev(c), 0)),
        ] + e_in,
        out_specs=[
            pl.BlockSpec(blk, lambda c: (rev(c), 0)),
            pl.BlockSpec(blk, lambda c: (rev(c), 0)),
            pl.BlockSpec(blk, lambda c: (rev(c), 0)),
            pl.BlockSpec((1, HGRN_WIDTH), lambda c: (0, 0)),
        ] + e_out,
        out_shape=[jax.ShapeDtypeStruct((s, HGRN_WIDTH), BF16)] * 3 + [jax.ShapeDtypeStruct((1, HGRN_WIDTH), F32)] + e_shape,
        scratch_shapes=[
            pltpu.VMEM((nh, hd, hd), F32),
            pltpu.VMEM(sblk, F32),
            pltpu.VMEM(sblk, F32),
            pltpu.VMEM(sblk, F32),
            pltpu.VMEM((cps, nh, c_len, hd), F32),
            pltpu.VMEM(sblk, F32),
            pltpu.VMEM(sblk, F32),
        ] + e_scr,
        compiler_params=_params(ride, dimension_semantics=("arbitrary",)),
    )(proj, proj, proj, lb, d_o, states, a_mat, *e_args)


def _per_head_lanes(x):
    lane = lax.broadcasted_iota(jnp.int32, (x.shape[0], LANES), 1)
    out = jnp.zeros((x.shape[0], LANES), F32)
    for h in range(ATTN_HEADS):
        out = jnp.where(lane == h, x[:, h * ATTN_HEAD_DIM:h * ATTN_HEAD_DIM + 1], out)
    return out


def _row_spec(tm, width, col=0):
    return pl.BlockSpec((tm, width), lambda i: (i, col))


def _const_spec(width):
    return pl.BlockSpec((1, width), lambda i: (0, 0))


def _acc_rows(ref, value):
    @pl.when(pl.program_id(0) == 0)
    def _():
        ref[...] = jnp.zeros_like(ref)

    ref[...] += jnp.sum(value, axis=0, keepdims=True)


def mix_fwd(attn_parts, o_h, proj, an, hn, w_out_b, gp, x, ride=None):
    s = x.shape[0]
    tm = TOKEN_TILE
    gate_col = 3
    hd = HGRN_HEAD_DIM
    nd = len(DILATIONS)

    def body(*refs):
        o_refs, l_refs = refs[:nd], refs[nd:2 * nd]
        oh_ref, gate_ref, an_ref, hn_ref, w_ref, gp_ref, x_ref = refs[2 * nd:2 * nd + 7]
        x1_ref, cat_ref, mixed_ref, attn_ref = refs[2 * nd + 7:2 * nd + 11]
        lse_refs = refs[2 * nd + 11:3 * nd + 11]
        o_scr, l_scr, lse_scr = refs[3 * nd + 11:]
        os_ = [_from_dilated(r, o_scr.at[k], d, tm) for k, (r, d) in enumerate(zip(o_refs, DILATIONS))]
        ls = [_from_dilated(r, l_scr.at[k], d, tm) for k, (r, d) in enumerate(zip(l_refs, DILATIONS))]
        m = jnp.maximum(jnp.maximum(ls[0], ls[1]), ls[2])
        es = [jnp.exp(l - m) for l in ls]
        den = es[0] + es[1] + es[2]
        attn = (es[0] * os_[0] + es[1] * os_[1] + es[2] * os_[2]) / den
        attn_ref[...] = attn
        lse_scr[0] = _per_head_lanes(m + jnp.log(den))
        for d, ref in zip(DILATIONS, lse_refs):
            _to_dilated(lse_scr, ref, d, tm)
        cat_ref[:, :ATTN_WIDTH] = _rms_fwd(attn, an_ref[...], ATTN_WIDTH).astype(BF16)
        gate = gate_ref[...]
        silu_g = gate * _sigmoid(gate)
        for h in range(HGRN_HEADS):
            hs = slice(h * hd, (h + 1) * hd)
            rec = _rms_fwd(oh_ref[:, hs], hn_ref[:, hs], hd) * silu_g[:, hs]
            cat_ref[:, ATTN_WIDTH + h * hd:ATTN_WIDTH + (h + 1) * hd] = rec.astype(BF16)
        mixed = _dot(cat_ref[...], w_ref[...])
        mixed_ref[...] = mixed
        x1_ref[...] = x_ref[...] + _rms_fwd(mixed, gp_ref[...], D_MODEL)

    aw = ATTN_WIDTH
    n_steps = s // tm
    step = lambda k: (lambda: pl.program_id(0) == k)
    e_in, e_out, e_shape, e_scr, e_args = _ride_specs(ride)
    return pl.pallas_call(
        _riding(body, 2 * nd + 7, 4 + nd, 3, ride, step(0), step((13 * n_steps) // 16), step(n_steps - 1)),
        name="mix_fwd",
        grid=(n_steps,),
        in_specs=[_dilated_spec(d, tm, aw) for d in DILATIONS] * 2 + [
            _row_spec(tm, aw), _row_spec(tm, aw, gate_col), _const_spec(aw), _const_spec(aw), _vmem_spec(),
            _const_spec(D_MODEL), _row_spec(tm, D_MODEL)] + e_in,
        out_specs=[_row_spec(tm, D_MODEL), _row_spec(tm, D_MODEL), _row_spec(tm, D_MODEL), _row_spec(tm, aw)] + [
            _dilated_spec(d, tm, LANES) for d in DILATIONS] + e_out,
        out_shape=[
            jax.ShapeDtypeStruct((s, D_MODEL), F32),
            jax.ShapeDtypeStruct((s, D_MODEL), BF16),
            jax.ShapeDtypeStruct((s, D_MODEL), F32),
            jax.ShapeDtypeStruct((s, aw), F32),
        ] + [jax.ShapeDtypeStruct((d, s // d, LANES), F32) for d in DILATIONS] + e_shape,
        scratch_shapes=[pltpu.VMEM((nd, aw // LANES, tm, LANES), F32), pltpu.VMEM((nd, aw // LANES, tm, LANES), F32),
                        pltpu.VMEM((1, tm, LANES), F32)] + e_scr,
        compiler_params=_params(ride, dimension_semantics=("arbitrary",)),
    )(*[p[0] for p in attn_parts], *[p[1] for p in attn_parts], o_h, proj, an, hn, w_out_b, gp, x, *e_args)


def mix_bwd(dx1, mixed, gp, w_out_b, attn, an, o_h, proj, hn):
    s = dx1.shape[0]
    tm = TOKEN_TILE
    gate_col = 3
    hd = HGRN_HEAD_DIM
    aw = ATTN_WIDTH

    nd = len(DILATIONS)

    def body(*refs):
        dx1_ref, mixed_ref, gp_ref, w_ref, attn_ref, an_ref, oh_ref, gate_ref, hn_ref, dmix_ref = refs[:10]
        do_refs, delta_refs = refs[10:10 + nd], refs[10 + nd:10 + 2 * nd]
        doh_ref, dgate_ref, dgp_ref, dan_ref, dhn_ref, do_ref, delta_ref = refs[10 + 2 * nd:]
        dmixed, gp_c = _rms_bwd(dx1_ref[...], mixed_ref[...], gp_ref[...], D_MODEL)
        _acc_rows(dgp_ref, gp_c)
        dmixed_b = dmixed.astype(BF16)
        dmix_ref[...] = dmixed_b
        dcat = _dot_nt(dmixed_b, w_ref[...])
        attn = attn_ref[...]
        d_o, an_c = _rms_bwd(dcat[:, :aw], attn, an_ref[...], aw)
        _acc_rows(dan_ref, an_c)
        _lane_blocks(do_ref, d_o)
        prod = d_o * attn
        lane = lax.broadcasted_iota(jnp.int32, (tm, LANES), 1)
        delta = jnp.zeros((tm, LANES), F32)
        for pair in range(ATTN_HEADS // 2):
            pp = prod[:, pair * LANES:(pair + 1) * LANES]
            low = _lane_half((tm, LANES), 0)
            lo = jnp.sum(jnp.where(low, pp, 0.0), axis=-1, keepdims=True)
            hi = jnp.sum(jnp.where(low, 0.0, pp), axis=-1, keepdims=True)
            delta = jnp.where(lane == 2 * pair, lo, jnp.where(lane == 2 * pair + 1, hi, delta))
        delta_ref[0] = delta
        for d, o_ref, l_ref in zip(DILATIONS, do_refs, delta_refs):
            _to_dilated(do_ref, o_ref, d, tm, cast=BF16)
            _to_dilated(delta_ref, l_ref, d, tm)
        gate = gate_ref[...]
        sg = _sigmoid(gate)
        silu_g = gate * sg
        drec = dcat[:, aw:]
        hn_parts = []
        for h in range(HGRN_HEADS):
            hs = slice(h * hd, (h + 1) * hd)
            oh = oh_ref[:, hs]
            on = _rms_fwd(oh, hn_ref[:, hs], hd)
            dgate_ref[:, hs] = (drec[:, hs] * on * (sg[:, hs] * (1.0 + gate[:, hs] * (1.0 - sg[:, hs])))).astype(BF16)
            d_oh, hn_c = _rms_bwd(drec[:, hs] * silu_g[:, hs], oh, hn_ref[:, hs], hd)
            doh_ref[:, hs] = d_oh
            hn_parts.append(hn_c)
        _acc_rows(dhn_ref, jnp.concatenate(hn_parts, axis=1))

    return pl.pallas_call(
        body,
        name="mix_bwd",
        grid=(s // tm,),
        in_specs=[_row_spec(tm, D_MODEL), _row_spec(tm, D_MODEL), _const_spec(D_MODEL), _vmem_spec(), _row_spec(tm, aw),
                  _const_spec(aw), _row_spec(tm, aw), _row_spec(tm, aw, gate_col), _const_spec(aw)],
        out_specs=[_row_spec(tm, D_MODEL)] + [_dilated_spec(d, tm, aw) for d in DILATIONS] + [
            _dilated_spec(d, tm, LANES) for d in DILATIONS] + [_row_spec(tm, aw)] * 2 + [
            _const_spec(D_MODEL), _const_spec(aw), _const_spec(aw)],
        out_shape=[jax.ShapeDtypeStruct((s, D_MODEL), BF16)] + [
            jax.ShapeDtypeStruct((d, s // d, aw), BF16) for d in DILATIONS] + [
            jax.ShapeDtypeStruct((d, s // d, LANES), F32) for d in DILATIONS] + [
            jax.ShapeDtypeStruct((s, aw), F32), jax.ShapeDtypeStruct((s, aw), BF16),
            jax.ShapeDtypeStruct((1, D_MODEL), F32), jax.ShapeDtypeStruct((1, aw), F32),
            jax.ShapeDtypeStruct((1, aw), F32)],
        scratch_shapes=[pltpu.VMEM((aw // LANES, tm, LANES), F32), pltpu.VMEM((1, tm, LANES), F32)],
        compiler_params=_params(dimension_semantics=("arbitrary",)),
    )(dx1, mixed, gp, w_out_b, attn, an, o_h, proj, hn)


def mlp_fwd_bwd(x1, g_pre, w1_blocks, w2_b, g_post, target):
    s = x1.shape[0]
    tm = MLP_TILE
    nblk, _, fb = w1_blocks.shape

    def body(x1_ref, gpre_ref, w1_ref, w2_ref, gpost_ref, t_ref,
             dx1_ref, h2_ref, a_ref, du_ref, dff_ref, loss_ref, dgpre_ref, dgpost_ref, u_ref):
        x1v = x1_ref[...]
        h2 = _rms_fwd(x1v, gpre_ref[...], D_MODEL).astype(BF16)
        h2_ref[...] = h2
        ff = jnp.zeros((tm, D_MODEL), F32)
        for j in range(nblk):
            cols = slice(j * fb, (j + 1) * fb)
            ru = jnp.maximum(_dot(h2, w1_ref[j]), 0.0)
            u_ref[:, cols] = ru.astype(BF16)
            a = (ru * ru).astype(BF16)
            a_ref[:, cols] = a
            ff = ff + _dot(a, w2_ref[cols, :])
        diff = x1v + _rms_fwd(ff, gpost_ref[...], D_MODEL) - t_ref[...]
        _acc_rows(loss_ref, diff * diff)
        dy = diff * (1.0 / D_MODEL)
        dff, gpost_c = _rms_bwd(dy, ff, gpost_ref[...], D_MODEL)
        _acc_rows(dgpost_ref, gpost_c)
        dff_b = dff.astype(BF16)
        dff_ref[...] = dff_b
        dh2 = jnp.zeros((tm, D_MODEL), F32)
        for j in range(nblk):
            cols = slice(j * fb, (j + 1) * fb)
            du = (_dot_nt(dff_b, w2_ref[cols, :]) * (2.0 * u_ref[:, cols])).astype(BF16)
            du_ref[:, cols] = du
            dh2 = dh2 + _dot_nt(du, w1_ref[j])
        dxa, gpre_c = _rms_bwd(dh2, x1v, gpre_ref[...], D_MODEL)
        _acc_rows(dgpre_ref, gpre_c)
        dx1_ref[...] = dy + dxa

    dm = D_MODEL
    return pl.pallas_call(
        body,
        name="mlp_fwd_bwd",
        grid=(s // tm,),
        in_specs=[_row_spec(tm, dm), _const_spec(dm), _vmem_spec(), _vmem_spec(), _const_spec(dm), _row_spec(tm, dm)],
        out_specs=[_row_spec(tm, dm), _row_spec(tm, dm), _row_spec(tm, D_FF), _row_spec(tm, D_FF), _row_spec(tm, dm),
                   _const_spec(dm), _const_spec(dm), _const_spec(dm)],
        out_shape=[
            jax.ShapeDtypeStruct((s, dm), F32),
            jax.ShapeDtypeStruct((s, dm), BF16),
            jax.ShapeDtypeStruct((s, D_FF), BF16),
            jax.ShapeDtypeStruct((s, D_FF), BF16),
            jax.ShapeDtypeStruct((s, dm), BF16),
            jax.ShapeDtypeStruct((1, dm), F32),
            jax.ShapeDtypeStruct((1, dm), F32),
            jax.ShapeDtypeStruct((1, dm), F32),
        ],
        scratch_shapes=[pltpu.VMEM((tm, D_FF), BF16)],
        compiler_params=_params(dimension_semantics=("arbitrary",)),
    )(x1, g_pre, w1_blocks, w2_b, g_post, target)


def in_proj_bwd(attn_grads, hgrn_grads, dgate, w_in_b, x, g1, dx1):
    s = x.shape[0]
    tm = PROJ_TILE
    aw = ATTN_WIDTH
    n_attn = len(attn_grads)
    flat = [g[k] for k in range(3) for g in attn_grads] + list(hgrn_grads) + [dgate]

    def body(*refs):
        parts = refs[:len(flat)]
        w_ref, x_ref, g_ref, dx1_ref, dx_ref, dproj_ref, dg_ref, scr = refs[len(flat):]
        groups = []
        for k in range(3):
            acc = None
            for p, d in zip(parts[k * n_attn:(k + 1) * n_attn], DILATIONS):
                v = _from_dilated(p, scr, d, tm)
                acc = v if acc is None else acc + v
            groups.append(acc)
        groups += [p[...] for p in parts[3 * n_attn:]]
        dh = jnp.zeros((tm, D_MODEL), F32)
        for gi, grp in enumerate(groups):
            cols = slice(gi * aw, (gi + 1) * aw)
            gb = grp.astype(BF16)
            dproj_ref[:, cols] = gb
            dh = dh + _dot_nt(gb, w_ref[:, cols])
        dxa, g_c = _rms_bwd(dh, x_ref[...], g_ref[...], D_MODEL)
        _acc_rows(dg_ref, g_c)
        dx_ref[...] = dx1_ref[...] + dxa

    dm = D_MODEL
    return pl.pallas_call(
        body,
        name="in_proj_bwd",
        grid=(s // tm,),
        in_specs=[_dilated_spec(d, tm, aw) for d in DILATIONS] * 3 + [_row_spec(tm, aw)] * 4 + [
            _vmem_spec(), _row_spec(tm, dm), _const_spec(dm), _row_spec(tm, dm)],
        out_specs=[_row_spec(tm, dm), _row_spec(tm, IN_PROJ_WIDTH), _const_spec(dm)],
        out_shape=[jax.ShapeDtypeStruct((s, dm), F32), jax.ShapeDtypeStruct((s, IN_PROJ_WIDTH), BF16),
                   jax.ShapeDtypeStruct((1, dm), F32)],
        scratch_shapes=[pltpu.VMEM((aw // LANES, tm, LANES), F32)],
        compiler_params=_params(dimension_semantics=("arbitrary",)),
    )(*flat, w_in_b, x, g1, dx1)


def wgrad(a_b, b_b, tn, name, ts=2048, per_step=1, ride=None):
    s, k = a_b.shape
    n = b_b.shape[1]

    def body(a_ref, b_ref, o_ref):
        @pl.when(pl.program_id(1) == 0)
        def _():
            o_ref[...] = jnp.zeros_like(o_ref)

        a = a_ref[...]
        for jj in range(per_step):
            o_ref[jj] += _dot_tn(a, b_ref[:, jj * tn:(jj + 1) * tn])

    wide = tn * per_step
    gn, gs = n // wide, s // ts
    step = lambda j, i: (lambda: (pl.program_id(0) == j) & (pl.program_id(1) == i))
    e_in, e_out, e_shape, e_scr, e_args = _ride_specs(ride)
    out = pl.pallas_call(
        _riding(body, 2, 1, 0, ride, step(0, 0), step(gn // 2, 0), step(gn - 1, gs - 1)),
        name=name,
        grid=(gn, gs),
        in_specs=[pl.BlockSpec((ts, k), lambda j, i: (i, 0)), pl.BlockSpec((ts, wide), lambda j, i: (i, j))] + e_in,
        out_specs=[pl.BlockSpec((per_step, k, tn), lambda j, i: (j, 0, 0))] + e_out,
        out_shape=[jax.ShapeDtypeStruct((n // tn, k, tn), F32)] + e_shape,
        scratch_shapes=e_scr,
        compiler_params=_params(ride, dimension_semantics=("arbitrary", "arbitrary")),
    )(a_b, b_b, *e_args)
    return out[0] if ride is None else out


def train_step(x, target, g1, an, logits, hn, gp, g_pre, g_post, w, m, v):
    nd = len(DILATIONS)
    shard_b = {k: w[k].astype(BF16) for k in BIG}
    (w_in_g,) = run_exchange(gather_exchange([shard_b["w_in"]]), "gather_w_in")
    w_in_b = w_in_g.transpose(1, 0, 2).reshape(D_MODEL, IN_PROJ_WIDTH)

    proj, h_b, *qkvs, w2_g = in_proj_fwd(x, g1, w_in_b, ride=gather_exchange([shard_b["w_ff2"]]))
    w2_b = w2_g.reshape(D_FF, D_MODEL)
    attn_parts = [attn_fwd(qkv, d) for qkv, d in zip(qkvs, DILATIONS)]
    o_h, states, a_mat, w_out_g, w1_blocks = hgrn_fwd(
        proj, logits, ride=gather_exchange([shard_b["w_out"], shard_b["w_ff1"]]))
    w_out_b = w_out_g.reshape(D_MODEL, D_MODEL)
    x1, cat_b, mixed, attn, *lses = mix_fwd(attn_parts, o_h, proj, an, hn, w_out_b, gp, x)
    dx1, h2_b, a_b, du_b, dff_b, loss_vec, dg_pre, dg_post = mlp_fwd_bwd(x1, g_pre, w1_blocks, w2_b, g_post, target)
    dw2 = wgrad(a_b, dff_b, D_MODEL, "wgrad_ff2", ts=512)
    dw1 = wgrad(h2_b, du_b, D_FF // N_DEV, "wgrad_ff1", per_step=2)
    dmix_b, *rest = mix_bwd(dx1, mixed, gp, w_out_b, attn, an, o_h, proj, hn)
    d_os, deltas = rest[:nd], rest[nd:2 * nd]
    d_oh, dgate, dgp, dan, dhn = rest[2 * nd:]
    dwout = wgrad(cat_b, dmix_b, D_MODEL, "wgrad_out")

    early = ("w_out", "w_ff1", "w_ff2")
    early_grads = [dwout.reshape(N_DEV, D_MODEL // N_DEV, D_MODEL), dw1, dw2.reshape(N_DEV, D_FF // N_DEV, D_MODEL)]
    res = attn_bwd(qkvs[0], d_os[0], lses[0], deltas[0], DILATIONS[0], ride=to_core_exchange(early_grads))
    pairs = [pair_sum(g, s, f"pair_sum_{name}") for g, s, name in zip(early_grads, res[3:], early)]
    attn_grads = [res[:3]]
    *res, others_ff2 = attn_bwd(qkvs[1], d_os[1], lses[1], deltas[1], DILATIONS[1],
                                ride=to_chip_exchange([pairs[2][1]]))
    attn_grads.append(res)
    attn_grads.append(attn_bwd(qkvs[2], d_os[2], lses[2], deltas[2], DILATIONS[2]))
    dq_h, df_h, di_h, dlb, *others = hgrn_bwd(proj, logits, d_oh, states, a_mat,
                                              ride=to_chip_exchange([pairs[0][1], pairs[1][1]]))
    others.append(others_ff2)
    dx, dproj_b, dg1 = in_proj_bwd(attn_grads, (dq_h, df_h, di_h), dgate, w_in_b, x, g1, dx1)
    packed = _pack_small(dg1, dgp, dg_pre, dg_post, dan, dhn, dlb, loss_vec)
    dwin, small_slots = wgrad(h_b, dproj_b, 2 * IN_PROJ_WIDTH // N_DEV, "wgrad_in",
                              ride=small_exchange(packed))
    big = {name: sum_adamw(p[0], o, w[name], m[name], v[name], f"sum_adamw_{name}")
           for name, p, o in zip(early, pairs, others)}

    shard_w = IN_PROJ_WIDTH // N_DEV
    dwin_blocks = dwin.reshape(N_DEV // 2, D_MODEL, 2, shard_w).transpose(0, 2, 1, 3).reshape(N_DEV, D_MODEL, shard_w)
    pair_in, others_in = reduce_last(dwin_blocks)
    big["w_in"] = sum_adamw(pair_in, others_in, w["w_in"], m["w_in"], v["w_in"], "sum_adamw_w_in")
    return dx, big, small_slots


def _position():
    x, y, c = lax.axis_index("x"), lax.axis_index("y"), lax.axis_index("c")
    other_chips = [(1 - x, y), (x, 1 - y), (1 - x, 1 - y)]
    return x, y, c, other_chips


def _any_spec():
    return pl.BlockSpec(memory_space=pl.ANY)


class Exchange:
    def __init__(self, arrays, out_shape, sems, stages, collective_id, peers):
        self.arrays, self.out_shape, self.sems, self.stages = list(arrays), list(out_shape), list(sems), stages
        self.collective_id, self.peers = collective_id, peers

    def open(self):
        barrier = pltpu.get_barrier_semaphore()
        peers = self.peers()
        for peer in peers:
            pl.semaphore_signal(barrier, inc=1, device_id=peer, device_id_type=MESH)
        pl.semaphore_wait(barrier, len(peers))


def _siblings():
    x, y, c, _ = _position()
    return [(x, y, 1 - c)]


def _same_core_of_other_chips():
    x, y, c, chips = _position()
    return [(px, py, c) for px, py in chips]


def _gather_peers():
    x, y, c, _ = _position()
    return [(x, y, 1 - c), (1 - x, y, c), (x, 1 - y, c)]


def _all_others():
    x, y, c, _ = _position()
    return [(1 - x if rel & 4 else x, 1 - y if rel & 2 else y, 1 - c if rel & 1 else c) for rel in range(1, N_DEV)]


def gather_exchange(shards):
    n = len(shards)
    halves = [sh.shape[0] // 2 for sh in shards]

    def stages(ins, outs, sems):
        send_sems, recv_sems, local_sems = sems

        def parts():
            x, y, c, _ = _position()
            me, sibling = (x, y, c), (x, y, 1 - c)
            nbr_x, nbr_y, diag = (1 - x, y, c), (x, 1 - y, c), (1 - x, 1 - y, c)

            def slot(a, dev, rows=None):
                ref = outs[a].at[4 * dev[0] + 2 * dev[1] + dev[2]]
                return ref if rows is None else ref.at[rows]

            def copy(a, k, block, to, rows=None, src=None):
                return pltpu.make_async_remote_copy(
                    src_ref=slot(a, block, rows) if src is None else src, dst_ref=slot(a, block, rows),
                    send_sem=send_sems.at[a, k], recv_sem=recv_sems.at[a, k], device_id=to, device_id_type=MESH)

            upper = lambda a: pl.ds(0, halves[a])
            lower = lambda a: pl.ds(halves[a], halves[a])
            return me, sibling, nbr_x, nbr_y, diag, slot, copy, upper, lower

        def begin():
            me, sibling, nbr_x, nbr_y, _, slot, copy, _, _ = parts()
            for a in range(n):
                pltpu.make_async_copy(ins[a], slot(a, me), local_sems.at[a]).start()
                for k, to in enumerate((sibling, nbr_x, nbr_y)):
                    copy(a, k, me, to, src=ins[a]).start()

        def middle():
            me, sibling, nbr_x, nbr_y, _, _, copy, upper, lower = parts()
            for a in range(n):
                copy(a, 1, nbr_x, me).wait_recv()
                copy(a, 3, nbr_x, sibling).start()
                copy(a, 5, nbr_x, nbr_y, rows=lower(a)).start()
                copy(a, 2, nbr_y, me).wait_recv()
                copy(a, 4, nbr_y, sibling).start()
                copy(a, 6, nbr_y, nbr_x, rows=upper(a)).start()

        def late():
            me, sibling, _, _, diag, _, copy, upper, lower = parts()
            for a in range(n):
                copy(a, 6, diag, me, rows=upper(a)).wait_recv()
                copy(a, 5, diag, me, rows=lower(a)).wait_recv()
                copy(a, 7, diag, sibling).start()

        def end():
            me, sibling, nbr_x, nbr_y, diag, slot, copy, upper, lower = parts()
            sib = lambda dev: (dev[0], dev[1], sibling[2])
            for a in range(n):
                for k, block in ((0, sibling), (3, sib(nbr_x)), (4, sib(nbr_y)), (7, sib(diag))):
                    copy(a, k, block, me).wait_recv()
                copy(a, 0, me, sibling, src=ins[a]).wait_send()
                copy(a, 1, me, nbr_x, src=ins[a]).wait_send()
                copy(a, 2, me, nbr_y, src=ins[a]).wait_send()
                copy(a, 3, nbr_x, sibling).wait_send()
                copy(a, 4, nbr_y, sibling).wait_send()
                copy(a, 5, nbr_x, nbr_y, rows=lower(a)).wait_send()
                copy(a, 6, nbr_y, nbr_x, rows=upper(a)).wait_send()
                copy(a, 7, diag, sibling).wait_send()
                pltpu.make_async_copy(ins[a], slot(a, me), local_sems.at[a]).wait()

        return begin, (middle, late), end

    return Exchange(
        shards, [jax.ShapeDtypeStruct((N_DEV,) + sh.shape, sh.dtype) for sh in shards],
        [pltpu.SemaphoreType.DMA((n, 8)), pltpu.SemaphoreType.DMA((n, 8)), pltpu.SemaphoreType.DMA((n,))], stages,
        collective_id=0, peers=_gather_peers)


def to_core_exchange(grads):
    n = len(grads)

    def stages(ins, outs, sems):
        send_sems, recv_sems = sems

        def copies():
            x, y, c, _ = _position()
            return [pltpu.make_async_remote_copy(
                src_ref=ins[a].at[2 * q + (1 - c)], dst_ref=outs[a].at[q], send_sem=send_sems.at[a, q],
                recv_sem=recv_sems.at[a, q], device_id=(x, y, 1 - c), device_id_type=MESH)
                for a in range(n) for q in range(4)]

        def begin():
            for cp in copies():
                cp.start()

        def end():
            for cp in copies():
                cp.wait()

        return begin, None, end

    return Exchange(grads, [jax.ShapeDtypeStruct((4,) + g.shape[1:], g.dtype) for g in grads],
                    [pltpu.SemaphoreType.DMA((n, 4)), pltpu.SemaphoreType.DMA((n, 4))], stages,
                    collective_id=1, peers=_siblings)


def pair_sum(grad, from_sibling, name):
    _, r, cdim = grad.shape
    tr = min(r, ELEMENTWISE_ROWS)
    c_idx = lax.axis_index("c").astype(jnp.int32).reshape(1)

    def body(c_ref, g_ref, s_ref, o_ref, ob_ref):
        total = g_ref[...] + s_ref[...]
        o_ref[...] = total
        ob_ref[...] = total.astype(BF16)

    blk = lambda: pl.BlockSpec((1, tr, cdim), lambda q, i, cr: (q, i, 0))
    return pl.pallas_call(
        body,
        name=name,
        grid_spec=pltpu.PrefetchScalarGridSpec(
            num_scalar_prefetch=1,
            grid=(4, r // tr),
            in_specs=[pl.BlockSpec((1, tr, cdim), lambda q, i, cr: (2 * q + cr[0], i, 0)), blk()],
            out_specs=[blk(), blk()],
        ),
        out_shape=[jax.ShapeDtypeStruct((4, r, cdim), F32), jax.ShapeDtypeStruct((4, r, cdim), BF16)],
        compiler_params=_params(dimension_semantics=("arbitrary", "arbitrary")),
    )(c_idx, grad, from_sibling)


def to_chip_exchange(pairs):
    n = len(pairs)

    def stages(ins, outs, sems):
        send_sems, recv_sems = sems

        def copies():
            x, y, c, chips = _position()
            return [pltpu.make_async_remote_copy(
                src_ref=ins[a].at[2 * px + py], dst_ref=outs[a].at[j], send_sem=send_sems.at[a, j],
                recv_sem=recv_sems.at[a, j], device_id=(px, py, c), device_id_type=MESH)
                for a in range(n) for j, (px, py) in enumerate(chips)]

        def begin():
            for cp in copies():
                cp.start()

        def end():
            for cp in copies():
                cp.wait()

        return begin, None, end

    return Exchange(pairs, [jax.ShapeDtypeStruct((3,) + p.shape[1:], p.dtype) for p in pairs],
                    [pltpu.SemaphoreType.DMA((n, 3)), pltpu.SemaphoreType.DMA((n, 3))], stages,
                    collective_id=2, peers=_same_core_of_other_chips)


def run_exchange(ex, name):
    n_in, n_out = len(ex.arrays), len(ex.out_shape)

    def body(*refs):
        begin, middle, end = ex.stages(refs[:n_in], refs[n_in:n_in + n_out], refs[n_in + n_out:])
        ex.open()
        begin()
        for stage in _as_tuple(middle):
            stage()
        end()

    return pl.pallas_call(
        body,
        name=name,
        in_specs=[_any_spec()] * n_in,
        out_specs=[_any_spec()] * n_out,
        out_shape=ex.out_shape,
        scratch_shapes=ex.sems,
        compiler_params=pltpu.CompilerParams(collective_id=ex.collective_id),
    )(*ex.arrays)


def _as_tuple(stages):
    return () if stages is None else stages if isinstance(stages, tuple) else (stages,)


def _riding(body, n_in, n_out, n_scratch, ex, first, middle, last, late=None):
    if ex is None:
        return body
    r_in, r_out = len(ex.arrays), len(ex.out_shape)

    def wrapped(*refs):
        k_in, refs = refs[:n_in], refs[n_in:]
        e_in, refs = refs[:r_in], refs[r_in:]
        k_out, refs = refs[:n_out], refs[n_out:]
        e_out, refs = refs[:r_out], refs[r_out:]
        k_scr, e_sems = refs[:n_scratch], refs[n_scratch:]
        begin, mid, end = ex.stages(e_in, e_out, e_sems)

        @pl.when(first())
        def _():
            ex.open()
            begin()

        body(*k_in, *k_out, *k_scr)
        for stage, at in zip(_as_tuple(mid), (middle, late or last)):
            pl.when(at())(stage)
        pl.when(last())(end)

    return wrapped


def _ride_specs(ex):
    if ex is None:
        return [], [], [], [], []
    return [_any_spec()] * len(ex.arrays), [_any_spec()] * len(ex.out_shape), ex.out_shape, ex.sems, ex.arrays


def reduce_last(grad):
    _, r, cdim = grad.shape

    def body(g_hbm, own_ref, others_hbm, g_buf, to_sib, from_sib, send_buf, load_sem, send_sems, recv_sems):
        x, y, c, chips = _position()
        order = chips + [(x, y)]
        n_other = len(chips)
        sibling = (x, y, 1 - c)
        barrier = pltpu.get_barrier_semaphore()
        peers = [sibling] + [(px, py, c) for px, py in chips]
        for peer in peers:
            pl.semaphore_signal(barrier, inc=1, device_id=peer, device_id_type=MESH)
        pl.semaphore_wait(barrier, len(peers))

        def load(block):
            cp = pltpu.make_async_copy(g_hbm.at[block], g_buf, load_sem)
            cp.start()
            cp.wait()
            return g_buf[...]

        def to_sibling(j):
            return pltpu.make_async_remote_copy(
                src_ref=to_sib.at[j], dst_ref=from_sib.at[j], send_sem=send_sems.at[n_other + j],
                recv_sem=recv_sems.at[n_other + j], device_id=sibling, device_id_type=MESH)

        def to_chip(j):
            px, py = chips[j]
            return pltpu.make_async_remote_copy(
                src_ref=send_buf.at[j], dst_ref=others_hbm.at[j], send_sem=send_sems.at[j], recv_sem=recv_sems.at[j],
                device_id=(px, py, c), device_id_type=MESH)

        def hand_over(j):
            px, py = order[j]
            to_sib[j] = load(2 * (2 * px + py) + (1 - c)).astype(BF16)
            to_sibling(j).start()

        def pair_sum_of(j):
            px, py = order[j]
            to_sibling(j).wait_recv()
            total = load(2 * (2 * px + py) + c) + from_sib[j].astype(F32)
            if j < n_other:
                send_buf[j] = total.astype(BF16)
                to_chip(j).start()
            else:
                own_ref[0] = total

        turn = [n_other - 1] + list(range(n_other - 1)) + [n_other]
        hand_over(turn[0])
        for before, j in zip(turn, turn[1:]):
            hand_over(j)
            pair_sum_of(before)
        pair_sum_of(turn[-1])
        for j in range(len(order)):
            to_sibling(j).wait_send()
        for j in range(n_other):
            to_chip(j).wait()

    n_blocks = N_DEV // 2
    return pl.pallas_call(
        body,
        name="reduce_w_in",
        in_specs=[_any_spec()],
        out_specs=[_vmem_spec(), _any_spec()],
        out_shape=[jax.ShapeDtypeStruct((1, r, cdim), F32), jax.ShapeDtypeStruct((n_blocks - 1, r, cdim), BF16)],
        scratch_shapes=[pltpu.VMEM((r, cdim), F32), pltpu.VMEM((n_blocks, r, cdim), BF16),
                        pltpu.VMEM((n_blocks, r, cdim), BF16), pltpu.VMEM((n_blocks - 1, r, cdim), BF16),
                        pltpu.SemaphoreType.DMA(()), pltpu.SemaphoreType.DMA((2 * n_blocks - 1,)),
                        pltpu.SemaphoreType.DMA((2 * n_blocks - 1,))],
        compiler_params=pltpu.CompilerParams(collective_id=4, vmem_limit_bytes=VMEM_LIMIT),
    )(grad)


def _adamw(w, g, m, v):
    m = ADAM_B1 * m + (1.0 - ADAM_B1) * g
    v = ADAM_B2 * v + (1.0 - ADAM_B2) * (g * g)
    m_hat = m / (1.0 - ADAM_B1 ** ADAM_STEP)
    v_hat = v / (1.0 - ADAM_B2 ** ADAM_STEP)
    delta = -ADAM_LR * (m_hat / (jnp.sqrt(v_hat) + ADAM_EPS) + ADAM_WD * w)
    return delta, m, v


def sum_adamw(pairs, others, w, m, v, name):
    r, cdim = w.shape
    tr = min(r, ELEMENTWISE_ROWS // 2)
    if pairs.shape[0] == 1:
        chip_idx = jnp.zeros((1,), jnp.int32)
    else:
        chip_idx = (2 * lax.axis_index("x") + lax.axis_index("y")).astype(jnp.int32).reshape(1)

    def body(q_ref, p_ref, o_ref, w_ref, m_ref, v_ref, g_out, d_out, m_out, v_out):
        g = p_ref[0] + o_ref[0].astype(F32) + o_ref[1].astype(F32) + o_ref[2].astype(F32)
        g_out[...] = g
        d_out[...], m_out[...], v_out[...] = _adamw(w_ref[...], g, m_ref[...], v_ref[...])

    tile = lambda: pl.BlockSpec((tr, cdim), lambda i, qr: (i, 0))
    return pl.pallas_call(
        body,
        name=name,
        grid_spec=pltpu.PrefetchScalarGridSpec(
            num_scalar_prefetch=1,
            grid=(r // tr,),
            in_specs=[pl.BlockSpec((1, tr, cdim), lambda i, qr: (qr[0], i, 0)),
                      pl.BlockSpec((3, tr, cdim), lambda i, qr: (0, i, 0)), tile(), tile(), tile()],
            out_specs=[tile(), tile(), tile(), tile()],
        ),
        out_shape=[jax.ShapeDtypeStruct((r, cdim), F32)] * 4,
        compiler_params=_params(dimension_semantics=("arbitrary",)),
    )(chip_idx, pairs, others, w, m, v)


def small_exchange(packed):
    def stages(ins, outs, sems):
        send_sems, recv_sems, local_sem = sems
        (src,), (slots,) = ins, outs

        def copies():
            x, y, c, _ = _position()
            my_id = 4 * x + 2 * y + c
            sends, landings = [], []
            for rel in range(1, N_DEV):
                px = 1 - x if (rel >> 2) & 1 else x
                py = 1 - y if (rel >> 1) & 1 else y
                pc = 1 - c if rel & 1 else c
                peer = dict(send_sem=send_sems.at[rel - 1], recv_sem=recv_sems.at[rel - 1], device_id=(px, py, pc),
                            device_id_type=MESH)
                sends.append(pltpu.make_async_remote_copy(src_ref=src, dst_ref=slots.at[my_id], **peer))
                landings.append(pltpu.make_async_remote_copy(src_ref=src, dst_ref=slots.at[4 * px + 2 * py + pc], **peer))
            return pltpu.make_async_copy(src, slots.at[my_id], local_sem), sends, landings

        def begin():
            local, sends, _ = copies()
            local.start()
            for cp in sends:
                cp.start()

        def end():
            local, sends, landings = copies()
            for cp in landings:
                cp.wait_recv()
            for cp in sends:
                cp.wait_send()
            local.wait()

        return begin, None, end

    return Exchange([packed], [jax.ShapeDtypeStruct((N_DEV,) + packed.shape, packed.dtype)],
                    [pltpu.SemaphoreType.DMA((N_DEV - 1,)), pltpu.SemaphoreType.DMA((N_DEV - 1,)),
                     pltpu.SemaphoreType.DMA(())], stages, collective_id=3, peers=_all_others)


def small_adamw(slots, w, m, v):
    def body(r_ref, w_ref, m_ref, v_ref, g_out, d_out, m_out, v_out, loss_out):
        red = r_ref[0]
        for k in range(1, N_DEV):
            red = red + r_ref[k]
        wv = w_ref[...]
        lb = _lower_bound(jnp.concatenate([wv[5:6, :HGRN_WIDTH], wv[5:6, HGRN_WIDTH:]], axis=0))
        t = red[5:6, :HGRN_WIDTH] * lb * (1.0 - lb)
        row = lax.broadcasted_iota(jnp.int32, red.shape, 0)
        g = jnp.where(row == 5, jnp.concatenate([t, -t], axis=1), jnp.where(row >= 6, 0.0, red))
        g_out[...] = g
        d_out[...], m_out[...], v_out[...] = _adamw(wv, g, m_ref[...], v_ref[...])
        loss = jnp.sum(red[6:7, :], axis=-1, keepdims=True) * (0.5 / D_MODEL)
        loss_out[...] = jnp.broadcast_to(loss, loss_out.shape)

    return pl.pallas_call(
        body,
        name="small_adamw",
        in_specs=[_vmem_spec()] * 4,
        out_specs=[_vmem_spec()] * 5,
        out_shape=[jax.ShapeDtypeStruct(w.shape, F32)] * 4 + [jax.ShapeDtypeStruct((SUBLANES, LANES), F32)],
    )(slots, w, m, v)


def _pack_small(g1, gp, g_pre, g_post, an, hn, logits_or_dlb, extra=None):
    row5 = logits_or_dlb.reshape(1, -1)
    row5 = jnp.pad(row5, ((0, 0), (0, D_MODEL - row5.shape[1])))
    row6 = jnp.zeros((1, D_MODEL), F32) if extra is None else extra
    return jnp.concatenate([g1, gp, g_pre, g_post, jnp.concatenate([an, hn], axis=1), row5, row6,
                            jnp.zeros((1, D_MODEL), F32)], axis=0)


def _unpack_small(p):
    return dict(mix_pre_norm=p[0:1], mix_post_norm=p[1:2], mlp_pre_norm=p[2:3], mlp_post_norm=p[3:4],
                attn_out_norm=p[4:5, :ATTN_WIDTH], hgrn_out_norm=p[4:5, ATTN_WIDTH:],
                hgrn_lb_logits=p[5].reshape(2, HGRN_WIDTH))


BIG = ("w_in", "w_out", "w_ff1", "w_ff2")
ORDER = ("mix_pre_norm", "w_in", "attn_out_norm", "hgrn_lb_logits", "hgrn_out_norm", "w_out", "mix_post_norm",
         "mlp_pre_norm", "w_ff1", "w_ff2", "mlp_post_norm")


def kernel(x, mix_pre_norm, w_in, attn_out_norm, hgrn_lb_logits, hgrn_out_norm, w_out, mix_post_norm, mlp_pre_norm, w_ff1, w_ff2, mlp_post_norm, loss_target, m_mix_pre_norm, m_w_in, m_attn_out_norm, m_hgrn_lb_logits, m_hgrn_out_norm, m_w_out, m_mix_post_norm, m_mlp_pre_norm, m_w_ff1, m_w_ff2, m_mlp_post_norm, v_mix_pre_norm, v_w_in, v_attn_out_norm, v_hgrn_lb_logits, v_hgrn_out_norm, v_w_out, v_mix_post_norm, v_mlp_pre_norm, v_w_ff1, v_w_ff2, v_mlp_post_norm):
    w = dict(w_in=w_in[0], w_out=w_out[0], w_ff1=w_ff1[0], w_ff2=w_ff2[0])
    m = dict(w_in=m_w_in[0], w_out=m_w_out[0], w_ff1=m_w_ff1[0], w_ff2=m_w_ff2[0])
    v = dict(w_in=v_w_in[0], w_out=v_w_out[0], w_ff1=v_w_ff1[0], w_ff2=v_w_ff2[0])

    dx, big, small_slots = train_step(x[0], loss_target[0], mix_pre_norm, attn_out_norm, hgrn_lb_logits, hgrn_out_norm,
                                      mix_post_norm, mlp_pre_norm, mlp_post_norm, w, m, v)

    pack = lambda a, b, c2, d, e, f, g: _pack_small(a, b, c2, d, e, f, g)
    w_s = pack(mix_pre_norm, mix_post_norm, mlp_pre_norm, mlp_post_norm, attn_out_norm, hgrn_out_norm, hgrn_lb_logits)
    m_s = pack(m_mix_pre_norm, m_mix_post_norm, m_mlp_pre_norm, m_mlp_post_norm, m_attn_out_norm, m_hgrn_out_norm,
               m_hgrn_lb_logits)
    v_s = pack(v_mix_pre_norm, v_mix_post_norm, v_mlp_pre_norm, v_mlp_post_norm, v_attn_out_norm, v_hgrn_out_norm,
               v_hgrn_lb_logits)
    g_s, d_s, nm_s, nv_s, loss = small_adamw(small_slots, w_s, m_s, v_s)
    small_out = [_unpack_small(t) for t in (g_s, d_s, nm_s, nv_s)]

    outs = [loss[0, 0], dx[None]]
    for kind in range(4):
        for name in ORDER:
            outs.append(big[name][kind][None] if name in BIG else small_out[kind][name])
    return tuple(outs)
```

```python
import jax
import jax.numpy as jnp
from jax import lax
from jax.experimental import pallas as pl
from jax.experimental.pallas import tpu as pltpu

F32 = jnp.float32
BF16 = jnp.bfloat16

D_MODEL = 1024
ATTN_WIDTH = 512
ATTN_HEAD_DIM = 64
ATTN_HEADS = 8
ATTN_BLOCK = 128
DILATIONS = (1, 4, 16)
HGRN_WIDTH = 512
HGRN_HEADS = 4
HGRN_HEAD_DIM = 128
HGRN_CHUNK = 64
IN_PROJ_WIDTH = 3584
D_FF = 4096
RMS_EPS = 1e-6
N_DEV = 8
ADAM_LR = 0.001
ADAM_B1 = 0.9
ADAM_B2 = 0.999
ADAM_EPS = 1e-08
ADAM_WD = 0.01
ADAM_STEP = 10

SUBLANES = 8
LANES = 128
COLUMN_UNROLL = 8
HGRN_CHUNKS_PER_STEP = 2
SUB_BLOCK = 16
TOKEN_TILE = 512
ELEMENTWISE_ROWS = 1024
MLP_TILE = 256
PROJ_TILE = 512
VMEM_BYTES_V7X = 64 * 1024 * 1024
VMEM_LIMIT = VMEM_BYTES_V7X // 8 * 7
NEG_BIG = -1e30
MESH = pl.DeviceIdType.MESH


def _params(ride=None, **kw):
    if ride is not None:
        kw["collective_id"] = ride.collective_id
    return pltpu.CompilerParams(vmem_limit_bytes=VMEM_LIMIT, **kw)


def _vmem_spec():
    return pl.BlockSpec(memory_space=pltpu.VMEM)


def _dot(a, b):
    return jnp.dot(a, b, preferred_element_type=F32)


def _dot_nt(a, b):
    return lax.dot_general(a, b, (((1,), (1,)), ((), ())), preferred_element_type=F32)


def _dot_tn(a, b):
    return lax.dot_general(a, b, (((0,), (0,)), ((), ())), preferred_element_type=F32)


def _sigmoid(x):
    return 1.0 / (1.0 + jnp.exp(-x))


def _rms_fwd(x, gain, width):
    r = lax.rsqrt(jnp.sum(x * x, axis=-1, keepdims=True) * (1.0 / width) + RMS_EPS)
    return x * r * gain


def _rms_bwd(dy, x, gain, width):
    r = lax.rsqrt(jnp.sum(x * x, axis=-1, keepdims=True) * (1.0 / width) + RMS_EPS)
    xhat = x * r
    dxhat = dy * gain
    dx = r * (dxhat - xhat * (jnp.sum(dxhat * xhat, axis=-1, keepdims=True) * (1.0 / width)))
    return dx, dy * xhat


def _split3(x):
    hi = x.astype(BF16)
    r1 = x - hi.astype(F32)
    mid = r1.astype(BF16)
    lo = (r1 - mid.astype(F32)).astype(BF16)
    return hi, mid, lo


def _tri_sum(tri_bf16, x):
    hi, mid, lo = _split3(x)
    return _dot(tri_bf16, hi) + _dot(tri_bf16, mid) + _dot(tri_bf16, lo)


def _dilated_spec(d, tm, width):
    return pl.BlockSpec((d, tm // d, width), lambda i: (0, i, 0))


def _lane_blocks(ref, value):
    for c in range(ref.shape[0]):
        ref[c] = value[:, c * LANES:(c + 1) * LANES]


def _to_dilated(src_ref, dst_ref, d, tm, cast=None):
    for r in range(d):
        for c in range(src_ref.shape[0]):
            v = src_ref[c] if d == 1 else src_ref[c, pl.ds(r, tm // d, stride=d), :]
            dst_ref[r, :, c * LANES:(c + 1) * LANES] = v if cast is None else v.astype(cast)


def _from_dilated(src_ref, scratch_ref, d, tm):
    if d == 1:
        return src_ref[0].astype(F32)
    nblk = scratch_ref.shape[0]
    for r in range(d):
        for c in range(nblk):
            scratch_ref[c, pl.ds(r, tm // d, stride=d), :] = src_ref[r, :, c * LANES:(c + 1) * LANES].astype(F32)
    return jnp.concatenate([scratch_ref[c] for c in range(nblk)], axis=1)


def in_proj_fwd(x, g1, w_in_b, ride=None):
    s = x.shape[0]
    tm = PROJ_TILE
    qkv_w = 3 * ATTN_WIDTH
    hg_w = IN_PROJ_WIDTH - qkv_w

    def body(x_ref, g_ref, w_ref, hg_ref, h_ref, *rest):
        qkv_refs, qkv_scr = rest[:len(DILATIONS)], rest[len(DILATIONS)]
        h = _rms_fwd(x_ref[...], g_ref[...], D_MODEL).astype(BF16)
        h_ref[...] = h
        proj = _dot(h, w_ref[...])
        hg_ref[...] = proj[:, qkv_w:]
        _lane_blocks(qkv_scr, proj[:, :qkv_w])
        for d, ref in zip(DILATIONS, qkv_refs):
            _to_dilated(qkv_scr, ref, d, tm, cast=BF16)

    n_steps = s // tm
    step = lambda k: (lambda: pl.program_id(0) == k)
    e_in, e_out, e_shape, e_scr, e_args = _ride_specs(ride)
    return pl.pallas_call(
        _riding(body, 3, 2 + len(DILATIONS), 1, ride, step(0), step(n_steps // 2), step(n_steps - 1),
                late=step(n_steps - 2)),
        name="in_proj_fwd",
        grid=(n_steps,),
        in_specs=[
            pl.BlockSpec((tm, D_MODEL), lambda i: (i, 0)),
            pl.BlockSpec((1, D_MODEL), lambda i: (0, 0)),
            _vmem_spec(),
        ] + e_in,
        out_specs=[
            pl.BlockSpec((tm, hg_w), lambda i: (i, 0)),
            pl.BlockSpec((tm, D_MODEL), lambda i: (i, 0)),
        ] + [_dilated_spec(d, tm, qkv_w) for d in DILATIONS] + e_out,
        out_shape=[jax.ShapeDtypeStruct((s, hg_w), F32), jax.ShapeDtypeStruct((s, D_MODEL), BF16)] + [
            jax.ShapeDtypeStruct((d, s // d, qkv_w), BF16) for d in DILATIONS] + e_shape,
        scratch_shapes=[pltpu.VMEM((qkv_w // LANES, tm, LANES), F32)] + e_scr,
        compiler_params=_params(ride, dimension_semantics=("arbitrary",)),
    )(x, g1, w_in_b, *e_args)


ATTN_SCALE = ATTN_HEAD_DIM ** -0.5


def _fill_attn_bias(bias_ref, dilation):
    qi = lax.broadcasted_iota(jnp.int32, (ATTN_BLOCK, 2 * ATTN_BLOCK), 0)
    kj = lax.broadcasted_iota(jnp.int32, (ATTN_BLOCK, 2 * ATTN_BLOCK), 1)
    dist = qi + ATTN_BLOCK - kj
    valid = (dist >= 0) & (dist <= ATTN_BLOCK)
    for head in range(ATTN_HEADS):
        slope = 2.0 ** (-8.0 * (head + 1) / ATTN_HEADS)
        bias = jnp.where(valid, dist.astype(F32) * (-slope * dilation), NEG_BIG)
        bias_ref[0, head] = bias
        bias_ref[1, head] = jnp.where(kj >= ATTN_BLOCK, bias, NEG_BIG)


def _stack_heads(x):
    low = _lane_half(x.shape, 0)
    zero = jnp.zeros_like(x)
    return jnp.concatenate([jnp.where(low, x, zero), jnp.where(low, zero, x)], axis=0)


def _unstack_heads(y):
    half = y.shape[0] // 2
    return jnp.where(_lane_half((half, y.shape[1]), 0), y[:half], y[half:])


def _attn_scores(q_stack, kcat, bias_ref, pair, first_block):
    f = first_block.astype(jnp.int32)
    bias = jnp.concatenate([bias_ref[f, 2 * pair], bias_ref[f, 2 * pair + 1]], axis=0)
    return _dot_nt(q_stack, kcat) + bias


def _lane_half(shape, sub):
    lane = lax.broadcasted_iota(jnp.int32, shape, 1)
    return (lane < ATTN_HEAD_DIM) if sub == 0 else (lane >= ATTN_HEAD_DIM)


def _sub_block(col, row):
    return pl.BlockSpec((None, ATTN_BLOCK, ATTN_WIDTH), lambda r, n: (r, row(n), col))


def attn_fwd(qkv, dilation):
    d, length, _ = qkv.shape
    assert d == dilation
    nb = length // ATTN_BLOCK

    def body(q_ref, kc_ref, kp_ref, vc_ref, vp_ref, o_ref, lse_ref, bias_ref):
        @pl.when((pl.program_id(0) == 0) & (pl.program_id(1) == 0))
        def _():
            _fill_attn_bias(bias_ref, d)

        first = pl.program_id(1) == 0
        for pair in range(ATTN_HEADS // 2):
            lanes = slice(pair * LANES, (pair + 1) * LANES)
            q_stack = _stack_heads(q_ref[:, lanes] * ATTN_SCALE)
            kcat = jnp.concatenate([kp_ref[:, lanes], kc_ref[:, lanes]], axis=0)
            vcat = jnp.concatenate([vp_ref[:, lanes], vc_ref[:, lanes]], axis=0)
            sc = _attn_scores(q_stack, kcat, bias_ref, pair, first)
            m = jnp.max(sc, axis=-1, keepdims=True)
            p = jnp.exp(sc - m)
            den = jnp.sum(p, axis=-1, keepdims=True)
            o_ref[:, lanes] = _unstack_heads(_dot(p.astype(BF16), vcat) / den).astype(BF16)
            lse_ref[:, lanes] = _unstack_heads(jnp.broadcast_to(m + jnp.log(den), (2 * ATTN_BLOCK, LANES)))

    cur = lambda n: n
    prev = lambda n: jnp.maximum(n - 1, 0)
    return pl.pallas_call(
        body,
        name=f"attn_fwd_d{d}",
        grid=(d, nb),
        in_specs=[_sub_block(0, cur), _sub_block(1, cur), _sub_block(1, prev), _sub_block(2, cur), _sub_block(2, prev)],
        out_specs=[_sub_block(0, cur), _sub_block(0, cur)],
        out_shape=[jax.ShapeDtypeStruct((d, length, ATTN_WIDTH), BF16), jax.ShapeDtypeStruct((d, length, ATTN_WIDTH), F32)],
        scratch_shapes=[pltpu.VMEM((2, ATTN_HEADS, ATTN_BLOCK, 2 * ATTN_BLOCK), F32)],
        compiler_params=_params(dimension_semantics=("arbitrary", "arbitrary")),
    )(qkv, qkv, qkv, qkv, qkv)


def attn_bwd(qkv, d_out, lse, delta, dilation, ride=None):
    d, length, _ = qkv.shape
    assert d == dilation
    nb = length // ATTN_BLOCK

    steps = d * nb + 1

    def body(q_ref, kc_ref, kp_ref, vc_ref, vp_ref, do_ref, lse_ref, dl_ref, dq_ref, dk_ref, dv_ref, ck_ref, cv_ref,
             bias_ref):
        t = pl.program_id(0)

        @pl.when(t == 0)
        def _():
            ck_ref[...] = jnp.zeros_like(ck_ref)
            cv_ref[...] = jnp.zeros_like(cv_ref)
            _fill_attn_bias(bias_ref, d)

        @pl.when(t < steps - 1)
        def _():
            first = t % nb == 0
            for pair in range(ATTN_HEADS // 2):
                lanes = slice(pair * LANES, (pair + 1) * LANES)
                q_stack = _stack_heads(q_ref[:, lanes] * ATTN_SCALE)
                do_stack = _stack_heads(do_ref[:, lanes])
                kcat = jnp.concatenate([kp_ref[:, lanes], kc_ref[:, lanes]], axis=0)
                vcat = jnp.concatenate([vp_ref[:, lanes], vc_ref[:, lanes]], axis=0)
                col_a, col_b = 2 * pair, 2 * pair + 1
                lse_col = jnp.concatenate([lse_ref[:, col_a:col_a + 1], lse_ref[:, col_b:col_b + 1]], axis=0)
                dl_col = jnp.concatenate([dl_ref[:, col_a:col_a + 1], dl_ref[:, col_b:col_b + 1]], axis=0)
                p = jnp.exp(_attn_scores(q_stack, kcat, bias_ref, pair, first) - lse_col)
                ds = (p * (_dot_nt(do_stack, vcat) - dl_col)).astype(BF16)
                dq_ref[:, lanes] = (_unstack_heads(_dot(ds, kcat)) * ATTN_SCALE).astype(BF16)
                dk_cat = _dot_tn(ds, q_stack)
                dv_cat = _dot_tn(p.astype(BF16), do_stack)
                dk_ref[:, lanes] = (ck_ref[:, lanes] + dk_cat[:ATTN_BLOCK]).astype(BF16)
                dv_ref[:, lanes] = (cv_ref[:, lanes] + dv_cat[:ATTN_BLOCK]).astype(BF16)
                ck_ref[:, lanes] = dk_cat[ATTN_BLOCK:]
                cv_ref[:, lanes] = dv_cat[ATTN_BLOCK:]

        @pl.when(t == steps - 1)
        def _():
            dk_ref[...] = ck_ref[...].astype(BF16)
            dv_ref[...] = cv_ref[...].astype(BF16)

    blk = (ATTN_BLOCK, ATTN_WIDTH)

    def spec(col, shift, width=ATTN_WIDTH):
        def index(t):
            f = jnp.minimum(t, steps - 2) if shift > -2 else jnp.maximum(t - 1, 0)
            r, n = f // nb, f % nb
            return (r, jnp.maximum(n - 1, 0) if shift == -1 else n, col)
        return pl.BlockSpec((None, ATTN_BLOCK, width), index)

    step = lambda k: (lambda: pl.program_id(0) == k)
    e_in, e_out, e_shape, e_scr, e_args = _ride_specs(ride)
    return pl.pallas_call(
        _riding(body, 8, 3, 3, ride, step(0), step(steps // 2), step(steps - 1)),
        name=f"attn_bwd_d{d}",
        grid=(steps,),
        in_specs=[spec(0, 0), spec(1, 0), spec(1, -1), spec(2, 0), spec(2, -1), spec(0, 0), spec(0, 0, LANES),
                  spec(0, 0, LANES)] + e_in,
        out_specs=[spec(0, 0), spec(0, -2), spec(0, -2)] + e_out,
        out_shape=[jax.ShapeDtypeStruct((d, length, ATTN_WIDTH), BF16)] * 3 + e_shape,
        scratch_shapes=[pltpu.VMEM(blk, F32), pltpu.VMEM(blk, F32),
                        pltpu.VMEM((2, ATTN_HEADS, ATTN_BLOCK, 2 * ATTN_BLOCK), F32)] + e_scr,
        compiler_params=_params(ride, dimension_semantics=("arbitrary",)),
    )(qkv, qkv, qkv, qkv, qkv, d_out, lse, delta, *e_args)


def _lower_bound(logits):
    return _sigmoid(logits[0:1, :] - logits[1:2, :])


def _hgrn_gates(q, fp, lb):
    sq = _sigmoid(q)
    qf = q * sq
    sig = _sigmoid(fp)
    sig_neg = _sigmoid(-fp)
    kf = (1.0 - lb) * sig_neg
    log_sig = jnp.minimum(fp, 0.0) - jnp.log(1.0 + jnp.exp(-jnp.abs(fp)))
    a = jnp.log(lb)
    c = jnp.log(1.0 - lb) + log_sig
    log_f = jnp.maximum(a, c) + jnp.log(1.0 + jnp.exp(-jnp.abs(a - c)))
    return sq, qf, (sig, sig_neg, c), log_f, kf


def _tril_bf16(n, upper=False):
    r = lax.broadcasted_iota(jnp.int32, (n, n), 0)
    c = lax.broadcasted_iota(jnp.int32, (n, n), 1)
    keep = (c >= r) if upper else (c <= r)
    return jnp.where(keep, 1.0, 0.0).astype(BF16)


def _hgrn_diagonal_loops(c_len, diagonal):
    for half in range(SUB_BLOCK // SUBLANES):
        def step(jj, carry, half=half):
            j = half * SUBLANES + jj
            for i in range(c_len // SUB_BLOCK):
                diagonal(slice(i * SUB_BLOCK + half * SUBLANES, (i + 1) * SUB_BLOCK), j, i * SUB_BLOCK + j)
            return carry

        lax.fori_loop(0, SUBLANES, step, 0, unroll=COLUMN_UNROLL)


def _hgrn_off_diagonal(b, qf, kf):
    c_len, width = b.shape
    edges = [b[0:1, :]] + [b[i * SUB_BLOCK - 1:i * SUB_BLOCK, :] for i in range(1, c_len // SUB_BLOCK)]
    eq = jnp.exp(b - jnp.concatenate([jnp.broadcast_to(e, (SUB_BLOCK, width)) for e in edges], axis=0))
    q_til = qf * eq
    k_til, ek = [], []
    for i in range(1, c_len // SUB_BLOCK):
        n = i * SUB_BLOCK
        e = jnp.exp(edges[i] - b[:n, :])
        ek.append(e)
        k_til.append(jnp.concatenate([kf[:n, :] * e, jnp.zeros((2 * c_len - n, width), F32)], axis=0))
    return q_til, k_til, eq, ek


def _split2(x):
    hi = x.astype(BF16)
    return hi, (x - hi.astype(F32)).astype(BF16)


def hgrn_fwd(proj, lb, ride=None):
    s = proj.shape[0]
    c_len, nh, hd = HGRN_CHUNK, HGRN_HEADS, HGRN_HEAD_DIM
    n_chunks = s // c_len
    col0 = 0

    cps = 2 * HGRN_CHUNKS_PER_STEP
    n_steps = n_chunks // cps

    def body(q_ref, f_ref, i_ref, lb_ref, o_ref, st_out_ref, a_out_ref, st_ref, b_ref, qf_ref, kf_ref, a_ref):
        @pl.when(pl.program_id(0) == 0)
        def _():
            st_ref[...] = jnp.zeros_like(st_ref)

        lbv = _lower_bound(lb_ref[...])
        for u in range(cps):
            rs = slice(u * c_len, (u + 1) * c_len)
            b_u, qf_u, kf_u, a_u = b_ref.at[u], qf_ref.at[u], kf_ref.at[u], a_ref.at[u]
            _, qf, _, log_f, kf = _hgrn_gates(q_ref[rs, :], f_ref[rs, :], lbv)
            b = _tri_sum(_tril_bf16(c_len), log_f)
            b_u[...] = b
            qf_u[...] = qf
            kf_u[...] = kf
            a_u[...] = jnp.zeros_like(a_u)

            def diagonal(rows, j, key, b_u=b_u, qf_u=qf_u, kf_u=kf_u, a_u=a_u):
                bj = b_u[pl.ds(key, 1), :]
                kj = kf_u[pl.ds(key, 1), :]
                nrow = rows.stop - rows.start
                t_loc = lax.broadcasted_iota(jnp.int32, (nrow, nh * hd), 0) + (rows.start % SUB_BLOCK)
                e = jnp.exp(jnp.where(t_loc >= j, b_u[rows, :] - bj, NEG_BIG))
                prod = qf_u[rows, :] * kj * e
                lane = lax.broadcasted_iota(jnp.int32, (nrow, hd), 1)
                for h in range(nh):
                    col = jnp.sum(prod[:, h * hd:(h + 1) * hd], axis=-1, keepdims=True)
                    a_u[h, rows, :] = jnp.where(lane == key, col, a_u[h, rows, :])

            _hgrn_diagonal_loops(c_len, diagonal)
            q_til, k_til, _, _ = _hgrn_off_diagonal(b, qf, kf)
            q_til = q_til.astype(BF16)
            k_til = [k.astype(BF16) for k in k_til]

            b_last = b[c_len - 1:c_len, :]
            qb = (qf * jnp.exp(b)).astype(BF16)
            kb2 = (kf * jnp.exp(b_last - b)).astype(BF16)
            vf = i_ref[rs, :].astype(BF16)
            for h in range(nh):
                hs = slice(h * hd, (h + 1) * hd)
                st = st_ref[h]
                st_out_ref[u, h] = st
                off = [jnp.zeros((SUB_BLOCK, hd), F32)]
                for i in range(1, c_len // SUB_BLOCK):
                    off.append(_dot_nt(q_til[i * SUB_BLOCK:(i + 1) * SUB_BLOCK, hs], k_til[i - 1][:, hs]))
                a_h = a_u[h] + jnp.concatenate(off, axis=0)
                a_out_ref[rs, hs] = a_h
                o_ref[rs, hs] = _dot_nt(qb[:, hs], st.astype(BF16)) + _dot(a_h[:, :c_len].astype(BF16), vf[:, hs])
                st_ref[h] = st * jnp.exp(b_last[:, hs]) + _dot_tn(vf[:, hs], kb2[:, hs])

    blk = (cps * c_len, HGRN_WIDTH)
    sblk = (cps, c_len, HGRN_WIDTH)
    step = lambda k: (lambda: pl.program_id(0) == k)
    e_in, e_out, e_shape, e_scr, e_args = _ride_specs(ride)
    return pl.pallas_call(
        _riding(body, 4, 3, 5, ride, step(0), step(n_steps // 2), step(n_steps - 1), late=step((3 * n_steps) // 4)),
        name="hgrn_fwd",
        grid=(n_steps,),
        in_specs=[
            pl.BlockSpec(blk, lambda c: (c, col0)),
            pl.BlockSpec(blk, lambda c: (c, col0 + 1)),
            pl.BlockSpec(blk, lambda c: (c, col0 + 2)),
            pl.BlockSpec((2, HGRN_WIDTH), lambda c: (0, 0)),
        ] + e_in,
        out_specs=[
            pl.BlockSpec(blk, lambda c: (c, 0)),
            pl.BlockSpec((cps, nh, hd, hd), lambda c: (c, 0, 0, 0)),
            pl.BlockSpec(blk, lambda c: (c, 0)),
        ] + e_out,
        out_shape=[
            jax.ShapeDtypeStruct((s, HGRN_WIDTH), F32),
            jax.ShapeDtypeStruct((n_chunks, nh, hd, hd), F32),
            jax.ShapeDtypeStruct((s, nh * hd), F32),
        ] + e_shape,
        scratch_shapes=[
            pltpu.VMEM((nh, hd, hd), F32),
            pltpu.VMEM(sblk, F32),
            pltpu.VMEM(sblk, F32),
            pltpu.VMEM(sblk, F32),
            pltpu.VMEM((cps, nh, c_len, hd), F32),
        ] + e_scr,
        compiler_params=_params(ride, dimension_semantics=("arbitrary",)),
    )(proj, proj, proj, lb, *e_args)


def hgrn_bwd(proj, lb, d_o, states, a_mat, ride=None):
    s = proj.shape[0]
    c_len, nh, hd = HGRN_CHUNK, HGRN_HEADS, HGRN_HEAD_DIM
    n_chunks = s // c_len
    col0 = 0
    cps = HGRN_CHUNKS_PER_STEP
    n_steps = n_chunks // cps
    last = n_steps - 1

    def body(q_ref, f_ref, i_ref, lb_ref, do_ref, st_in_ref, a_in_ref, dq_ref, df_ref, di_ref, dlb_ref,
             dst_ref, b_ref, qf_ref, kf_ref, da_ref, dqi_ref, dki_ref):
        @pl.when(pl.program_id(0) == 0)
        def _():
            dst_ref[...] = jnp.zeros_like(dst_ref)
            dlb_ref[...] = jnp.zeros_like(dlb_ref)

        lbv = _lower_bound(lb_ref[...])
        for u in reversed(range(cps)):
            rs = slice(u * c_len, (u + 1) * c_len)
            b_u, qf_u, kf_u, da_u, dqi_u, dki_u = (b_ref.at[u], qf_ref.at[u], kf_ref.at[u], da_ref.at[u], dqi_ref.at[u],
                                                   dki_ref.at[u])
            q = q_ref[rs, :]
            sq, qf, (sig, sig_neg, log_c), log_f, kf = _hgrn_gates(q, f_ref[rs, :], lbv)
            b = _tri_sum(_tril_bf16(c_len), log_f)
            b_u[...] = b
            qf_u[...] = qf
            kf_u[...] = kf
            b_last = b[c_len - 1:c_len, :]
            eb = jnp.exp(b)
            ebl = jnp.exp(b_last - b)
            qb = qf * eb
            kb2 = kf * ebl
            vf = i_ref[rs, :]
            d_o = do_ref[rs, :]
            qb_b, kb2_b, vf_b, do_b = qb.astype(BF16), kb2.astype(BF16), vf.astype(BF16), d_o.astype(BF16)
            tq = lax.broadcasted_iota(jnp.int32, (c_len, hd), 0)
            lane = lax.broadcasted_iota(jnp.int32, (c_len, hd), 1)

            dqb_parts, dvf_parts, dkb2_parts, dbl_parts = [], [], [], []
            for h in range(nh):
                hs = slice(h * hd, (h + 1) * hd)
                st = st_in_ref[u, h]
                dst = dst_ref[h]
                st_b, dst_b = st.astype(BF16), dst.astype(BF16)
                a_h = a_in_ref[rs, hs][:, :c_len].astype(BF16)
                dqb_parts.append(_dot(do_b[:, hs], st_b))
                dvf_parts.append(_dot_tn(a_h, do_b[:, hs]) + _dot_nt(kb2_b[:, hs], dst_b))
                dkb2_parts.append(_dot(vf_b[:, hs], dst_b))
                da = _dot_nt(do_b[:, hs], vf_b[:, hs])
                da = jnp.concatenate([da, jnp.zeros((c_len, hd - c_len), F32)], axis=1)
                da_u[h] = jnp.where(tq >= lane, da, 0.0)
                dbl_parts.append(jnp.sum(dst * st, axis=0, keepdims=True) * jnp.exp(b_last[:, hs]))
                dst_ref[h] = dst * jnp.exp(b_last[:, hs]) + _dot_tn(do_b[:, hs], qb_b[:, hs])
            dqb = jnp.concatenate(dqb_parts, axis=1)
            dvf = jnp.concatenate(dvf_parts, axis=1)
            dkb2 = jnp.concatenate(dkb2_parts, axis=1)
            dbl = jnp.concatenate(dbl_parts, axis=1) + jnp.sum(dkb2 * kb2, axis=0, keepdims=True)

            dqi_u[...] = jnp.zeros_like(dqi_u)
            t_idx = lax.broadcasted_iota(jnp.int32, (c_len, nh * hd), 0)

            def diagonal(rows, j, key, b_u=b_u, qf_u=qf_u, kf_u=kf_u, da_u=da_u, dqi_u=dqi_u, dki_u=dki_u):
                bj = b_u[pl.ds(key, 1), :]
                kj = kf_u[pl.ds(key, 1), :]
                nrow = rows.stop - rows.start
                t_loc = lax.broadcasted_iota(jnp.int32, (nrow, nh * hd), 0) + (rows.start % SUB_BLOCK)
                e = jnp.exp(jnp.where(t_loc >= j, b_u[rows, :] - bj, NEG_BIG))
                lane_r = lax.broadcasted_iota(jnp.int32, (nrow, hd), 1)
                cols = [jnp.sum(jnp.where(lane_r == key, da_u[h, rows, :], 0.0), axis=-1, keepdims=True)
                        for h in range(nh)]
                w = e * jnp.concatenate([jnp.broadcast_to(cc, (nrow, hd)) for cc in cols], axis=1)
                dqi_u[rows, :] += w * kj
                dki_u[pl.ds(key, 1), :] = jnp.sum(w * qf_u[rows, :], axis=0, keepdims=True)

            _hgrn_diagonal_loops(c_len, diagonal)

            q_til, k_til, eq, ek = _hgrn_off_diagonal(b, qf, kf)
            q_hi, q_lo = _split2(q_til)
            k_pairs = [_split2(k) for k in k_til]
            n_sub = c_len // SUB_BLOCK
            dq_heads, dk_heads = [], []
            for h in range(nh):
                hs = slice(h * hd, (h + 1) * hd)
                dq_rows = [jnp.zeros((SUB_BLOCK, hd), F32)]
                dk_h = jnp.zeros((c_len, hd), F32)
                for i in range(1, n_sub):
                    rows = slice(i * SUB_BLOCK, (i + 1) * SUB_BLOCK)
                    n = i * SUB_BLOCK
                    da_i = da_u[h, rows, :].astype(BF16)
                    k_hi, k_lo = k_pairs[i - 1]
                    dq_rows.append((_dot(da_i, k_hi[:, hs]) + _dot(da_i, k_lo[:, hs])) * eq[rows, hs])
                    dk_t = (_dot_tn(da_i, q_hi[rows, hs]) + _dot_tn(da_i, q_lo[rows, hs]))[:n, :] * ek[i - 1][:, hs]
                    dk_h = dk_h + jnp.concatenate([dk_t, jnp.zeros((c_len - n, hd), F32)], axis=0)
                dq_heads.append(jnp.concatenate(dq_rows, axis=0))
                dk_heads.append(dk_h)
            dq_intra = dqi_u[...] + jnp.concatenate(dq_heads, axis=1)
            dk_intra = dki_u[...] + jnp.concatenate(dk_heads, axis=1)

            db = dqb * qb + qf * dq_intra - kf * dk_intra - dkb2 * kb2
            db = db + jnp.where(t_idx == c_len - 1, dbl, 0.0)
            dg = _tri_sum(_tril_bf16(c_len, upper=True), db)
            dqf = dqb * eb + dq_intra
            dkf = dkb2 * ebl + dk_intra
            dq_ref[rs, :] = (dqf * (sq * (1.0 + q * (1.0 - sq)))).astype(BF16)
            df_ref[rs, :] = (sig_neg * (dg * jnp.exp(log_c - log_f) - dkf * (1.0 - lbv) * sig)).astype(BF16)
            di_ref[rs, :] = dvf.astype(BF16)
            dlb_ref[...] += jnp.sum(sig_neg * (dg * jnp.exp(-log_f) - dkf), axis=0, keepdims=True)

    blk = (cps * c_len, HGRN_WIDTH)
    sblk = (cps, c_len, HGRN_WIDTH)
    rev = lambda c: last - c
    step = lambda k: (lambda: pl.program_id(0) == k)
    e_in, e_out, e_shape, e_scr, e_args = _ride_specs(ride)
    return pl.pallas_call(
        _riding(body, 7, 4, 7, ride, step(0), step(n_steps // 2), step(last)),
        name="hgrn_bwd",
        grid=(n_steps,),
        in_specs=[
            pl.BlockSpec(blk, lambda c: (rev(c), col0)),
            pl.BlockSpec(blk, lambda c: (rev(c), col0 + 1)),
            pl.BlockSpec(blk, lambda c: (rev(c), col0 + 2)),
            pl.BlockSpec((2, HGRN_WIDTH), lambda c: (0, 0)),
            pl.BlockSpec(blk, lambda c: (rev(c), 0)),
            pl.BlockSpec((cps, nh, hd, hd), lambda c: (rev(c), 0, 0, 0)),
            pl.BlockSpec(blk, lambda c: (rev(c), 0)),
        ] + e_in,
        out_specs=[
            pl.BlockSpec(blk, lambda c: (rev(c), 0)),
            pl.BlockSpec(blk, lambda c: (rev(c), 0)),
            pl.BlockSpec(blk, lambda c: (rev(c), 0)),
            pl.BlockSpec((1, HGRN_WIDTH), lambda c: (0, 0)),
        ] + e_out,
        out_shape=[jax.ShapeDtypeStruct((s, HGRN_WIDTH), BF16)] * 3 + [jax.ShapeDtypeStruct((1, HGRN_WIDTH), F32)] + e_shape,
        scratch_shapes=[
            pltpu.VMEM((nh, hd, hd), F32),
            pltpu.VMEM(sblk, F32),
            pltpu.VMEM(sblk, F32),
            pltpu.VMEM(sblk, F32),
            pltpu.VMEM((cps, nh, c_len, hd), F32),
            pltpu.VMEM(sblk, F32),
            pltpu.VMEM(sblk, F32),
        ] + e_scr,
        compiler_params=_params(ride, dimension_semantics=("arbitrary",)),
    )(proj, proj, proj, lb, d_o, states, a_mat, *e_args)


def _per_head_lanes(x):
    lane = lax.broadcasted_iota(jnp.int32, (x.shape[0], LANES), 1)
    out = jnp.zeros((x.shape[0], LANES), F32)
    for h in range(ATTN_HEADS):
        out = jnp.where(lane == h, x[:, h * ATTN_HEAD_DIM:h * ATTN_HEAD_DIM + 1], out)
    return out


def _row_spec(tm, width, col=0):
    return pl.BlockSpec((tm, width), lambda i: (i, col))


def _const_spec(width):
    return pl.BlockSpec((1, width), lambda i: (0, 0))


def _acc_rows(ref, value):
    @pl.when(pl.program_id(0) == 0)
    def _():
        ref[...] = jnp.zeros_like(ref)

    ref[...] += jnp.sum(value, axis=0, keepdims=True)


def mix_fwd(attn_parts, o_h, proj, an, hn, w_out_b, gp, x, ride=None):
    s = x.shape[0]
    tm = TOKEN_TILE
    gate_col = 3
    hd = HGRN_HEAD_DIM
    nd = len(DILATIONS)

    def body(*refs):
        o_refs, l_refs = refs[:nd], refs[nd:2 * nd]
        oh_ref, gate_ref, an_ref, hn_ref, w_ref, gp_ref, x_ref = refs[2 * nd:2 * nd + 7]
        x1_ref, cat_ref, mixed_ref, attn_ref = refs[2 * nd + 7:2 * nd + 11]
        lse_refs = refs[2 * nd + 11:3 * nd + 11]
        o_scr, l_scr, lse_scr = refs[3 * nd + 11:]
        os_ = [_from_dilated(r, o_scr.at[k], d, tm) for k, (r, d) in enumerate(zip(o_refs, DILATIONS))]
        ls = [_from_dilated(r, l_scr.at[k], d, tm) for k, (r, d) in enumerate(zip(l_refs, DILATIONS))]
        m = jnp.maximum(jnp.maximum(ls[0], ls[1]), ls[2])
        es = [jnp.exp(l - m) for l in ls]
        den = es[0] + es[1] + es[2]
        attn = (es[0] * os_[0] + es[1] * os_[1] + es[2] * os_[2]) / den
        attn_ref[...] = attn
        lse_scr[0] = _per_head_lanes(m + jnp.log(den))
        for d, ref in zip(DILATIONS, lse_refs):
            _to_dilated(lse_scr, ref, d, tm)
        cat_ref[:, :ATTN_WIDTH] = _rms_fwd(attn, an_ref[...], ATTN_WIDTH).astype(BF16)
        gate = gate_ref[...]
        silu_g = gate * _sigmoid(gate)
        for h in range(HGRN_HEADS):
            hs = slice(h * hd, (h + 1) * hd)
            rec = _rms_fwd(oh_ref[:, hs], hn_ref[:, hs], hd) * silu_g[:, hs]
            cat_ref[:, ATTN_WIDTH + h * hd:ATTN_WIDTH + (h + 1) * hd] = rec.astype(BF16)
        mixed = _dot(cat_ref[...], w_ref[...])
        mixed_ref[...] = mixed
        x1_ref[...] = x_ref[...] + _rms_fwd(mixed, gp_ref[...], D_MODEL)

    aw = ATTN_WIDTH
    n_steps = s // tm
    step = lambda k: (lambda: pl.program_id(0) == k)
    e_in, e_out, e_shape, e_scr, e_args = _ride_specs(ride)
    return pl.pallas_call(
        _riding(body, 2 * nd + 7, 4 + nd, 3, ride, step(0), step((13 * n_steps) // 16), step(n_steps - 1)),
        name="mix_fwd",
        grid=(n_steps,),
        in_specs=[_dilated_spec(d, tm, aw) for d in DILATIONS] * 2 + [
            _row_spec(tm, aw), _row_spec(tm, aw, gate_col), _const_spec(aw), _const_spec(aw), _vmem_spec(),
            _const_spec(D_MODEL), _row_spec(tm, D_MODEL)] + e_in,
        out_specs=[_row_spec(tm, D_MODEL), _row_spec(tm, D_MODEL), _row_spec(tm, D_MODEL), _row_spec(tm, aw)] + [
            _dilated_spec(d, tm, LANES) for d in DILATIONS] + e_out,
        out_shape=[
            jax.ShapeDtypeStruct((s, D_MODEL), F32),
            jax.ShapeDtypeStruct((s, D_MODEL), BF16),
            jax.ShapeDtypeStruct((s, D_MODEL), F32),
            jax.ShapeDtypeStruct((s, aw), F32),
        ] + [jax.ShapeDtypeStruct((d, s // d, LANES), F32) for d in DILATIONS] + e_shape,
        scratch_shapes=[pltpu.VMEM((nd, aw // LANES, tm, LANES), F32), pltpu.VMEM((nd, aw // LANES, tm, LANES), F32),
                        pltpu.VMEM((1, tm, LANES), F32)] + e_scr,
        compiler_params=_params(ride, dimension_semantics=("arbitrary",)),
    )(*[p[0] for p in attn_parts], *[p[1] for p in attn_parts], o_h, proj, an, hn, w_out_b, gp, x, *e_args)


def mix_bwd(dx1, mixed, gp, w_out_b, attn, an, o_h, proj, hn):
    s = dx1.shape[0]
    tm = TOKEN_TILE
    gate_col = 3
    hd = HGRN_HEAD_DIM
    aw = ATTN_WIDTH

    nd = len(DILATIONS)

    def body(*refs):
        dx1_ref, mixed_ref, gp_ref, w_ref, attn_ref, an_ref, oh_ref, gate_ref, hn_ref, dmix_ref = refs[:10]
        do_refs, delta_refs = refs[10:10 + nd], refs[10 + nd:10 + 2 * nd]
        doh_ref, dgate_ref, dgp_ref, dan_ref, dhn_ref, do_ref, delta_ref = refs[10 + 2 * nd:]
        dmixed, gp_c = _rms_bwd(dx1_ref[...], mixed_ref[...], gp_ref[...], D_MODEL)
        _acc_rows(dgp_ref, gp_c)
        dmixed_b = dmixed.astype(BF16)
        dmix_ref[...] = dmixed_b
        dcat = _dot_nt(dmixed_b, w_ref[...])
        attn = attn_ref[...]
        d_o, an_c = _rms_bwd(dcat[:, :aw], attn, an_ref[...], aw)
        _acc_rows(dan_ref, an_c)
        _lane_blocks(do_ref, d_o)
        prod = d_o * attn
        lane = lax.broadcasted_iota(jnp.int32, (tm, LANES), 1)
        delta = jnp.zeros((tm, LANES), F32)
        for pair in range(ATTN_HEADS // 2):
            pp = prod[:, pair * LANES:(pair + 1) * LANES]
            low = _lane_half((tm, LANES), 0)
            lo = jnp.sum(jnp.where(low, pp, 0.0), axis=-1, keepdims=True)
            hi = jnp.sum(jnp.where(low, 0.0, pp), axis=-1, keepdims=True)
            delta = jnp.where(lane == 2 * pair, lo, jnp.where(lane == 2 * pair + 1, hi, delta))
        delta_ref[0] = delta
        for d, o_ref, l_ref in zip(DILATIONS, do_refs, delta_refs):
            _to_dilated(do_ref, o_ref, d, tm, cast=BF16)
            _to_dilated(delta_ref, l_ref, d, tm)
        gate = gate_ref[...]
        sg = _sigmoid(gate)
        silu_g = gate * sg
        drec = dcat[:, aw:]
        hn_parts = []
        for h in range(HGRN_HEADS):
            hs = slice(h * hd, (h + 1) * hd)
            oh = oh_ref[:, hs]
            on = _rms_fwd(oh, hn_ref[:, hs], hd)
            dgate_ref[:, hs] = (drec[:, hs] * on * (sg[:, hs] * (1.0 + gate[:, hs] * (1.0 - sg[:, hs])))).astype(BF16)
            d_oh, hn_c = _rms_bwd(drec[:, hs] * silu_g[:, hs], oh, hn_ref[:, hs], hd)
            doh_ref[:, hs] = d_oh
            hn_parts.append(hn_c)
        _acc_rows(dhn_ref, jnp.concatenate(hn_parts, axis=1))

    return pl.pallas_call(
        body,
        name="mix_bwd",
        grid=(s // tm,),
        in_specs=[_row_spec(tm, D_MODEL), _row_spec(tm, D_MODEL), _const_spec(D_MODEL), _vmem_spec(), _row_spec(tm, aw),
                  _const_spec(aw), _row_spec(tm, aw), _row_spec(tm, aw, gate_col), _const_spec(aw)],
        out_specs=[_row_spec(tm, D_MODEL)] + [_dilated_spec(d, tm, aw) for d in DILATIONS] + [
            _dilated_spec(d, tm, LANES) for d in DILATIONS] + [_row_spec(tm, aw)] * 2 + [
            _const_spec(D_MODEL), _const_spec(aw), _const_spec(aw)],
        out_shape=[jax.ShapeDtypeStruct((s, D_MODEL), BF16)] + [
            jax.ShapeDtypeStruct((d, s // d, aw), BF16) for d in DILATIONS] + [
            jax.ShapeDtypeStruct((d, s // d, LANES), F32) for d in DILATIONS] + [
            jax.ShapeDtypeStruct((s, aw), F32), jax.ShapeDtypeStruct((s, aw), BF16),
            jax.ShapeDtypeStruct((1, D_MODEL), F32), jax.ShapeDtypeStruct((1, aw), F32),
            jax.ShapeDtypeStruct((1, aw), F32)],
        scratch_shapes=[pltpu.VMEM((aw // LANES, tm, LANES), F32), pltpu.VMEM((1, tm, LANES), F32)],
        compiler_params=_params(dimension_semantics=("arbitrary",)),
    )(dx1, mixed, gp, w_out_b, attn, an, o_h, proj, hn)


def mlp_fwd_bwd(x1, g_pre, w1_blocks, w2_b, g_post, target):
    s = x1.shape[0]
    tm = MLP_TILE
    nblk, _, fb = w1_blocks.shape

    def body(x1_ref, gpre_ref, w1_ref, w2_ref, gpost_ref, t_ref,
             dx1_ref, h2_ref, a_ref, du_ref, dff_ref, loss_ref, dgpre_ref, dgpost_ref, u_ref):
        x1v = x1_ref[...]
        h2 = _rms_fwd(x1v, gpre_ref[...], D_MODEL).astype(BF16)
        h2_ref[...] = h2
        ff = jnp.zeros((tm, D_MODEL), F32)
        for j in range(nblk):
            cols = slice(j * fb, (j + 1) * fb)
            ru = jnp.maximum(_dot(h2, w1_ref[j]), 0.0)
            u_ref[:, cols] = ru.astype(BF16)
            a = (ru * ru).astype(BF16)
            a_ref[:, cols] = a
            ff = ff + _dot(a, w2_ref[cols, :])
        diff = x1v + _rms_fwd(ff, gpost_ref[...], D_MODEL) - t_ref[...]
        _acc_rows(loss_ref, diff * diff)
        dy = diff * (1.0 / D_MODEL)
        dff, gpost_c = _rms_bwd(dy, ff, gpost_ref[...], D_MODEL)
        _acc_rows(dgpost_ref, gpost_c)
        dff_b = dff.astype(BF16)
        dff_ref[...] = dff_b
        dh2 = jnp.zeros((tm, D_MODEL), F32)
        for j in range(nblk):
            cols = slice(j * fb, (j + 1) * fb)
            du = (_dot_nt(dff_b, w2_ref[cols, :]) * (2.0 * u_ref[:, cols])).astype(BF16)
            du_ref[:, cols] = du
            dh2 = dh2 + _dot_nt(du, w1_ref[j])
        dxa, gpre_c = _rms_bwd(dh2, x1v, gpre_ref[...], D_MODEL)
        _acc_rows(dgpre_ref, gpre_c)
        dx1_ref[...] = dy + dxa

    dm = D_MODEL
    return pl.pallas_call(
        body,
        name="mlp_fwd_bwd",
        grid=(s // tm,),
        in_specs=[_row_spec(tm, dm), _const_spec(dm), _vmem_spec(), _vmem_spec(), _const_spec(dm), _row_spec(tm, dm)],
        out_specs=[_row_spec(tm, dm), _row_spec(tm, dm), _row_spec(tm, D_FF), _row_spec(tm, D_FF), _row_spec(tm, dm),
                   _const_spec(dm), _const_spec(dm), _const_spec(dm)],
        out_shape=[
            jax.ShapeDtypeStruct((s, dm), F32),
            jax.ShapeDtypeStruct((s, dm), BF16),
            jax.ShapeDtypeStruct((s, D_FF), BF16),
            jax.ShapeDtypeStruct((s, D_FF), BF16),
            jax.ShapeDtypeStruct((s, dm), BF16),
            jax.ShapeDtypeStruct((1, dm), F32),
            jax.ShapeDtypeStruct((1, dm), F32),
            jax.ShapeDtypeStruct((1, dm), F32),
        ],
        scratch_shapes=[pltpu.VMEM((tm, D_FF), BF16)],
        compiler_params=_params(dimension_semantics=("arbitrary",)),
    )(x1, g_pre, w1_blocks, w2_b, g_post, target)


def in_proj_bwd(attn_grads, hgrn_grads, dgate, w_in_b, x, g1, dx1):
    s = x.shape[0]
    tm = PROJ_TILE
    aw = ATTN_WIDTH
    n_attn = len(attn_grads)
    flat = [g[k] for k in range(3) for g in attn_grads] + list(hgrn_grads) + [dgate]

    def body(*refs):
        parts = refs[:len(flat)]
        w_ref, x_ref, g_ref, dx1_ref, dx_ref, dproj_ref, dg_ref, scr = refs[len(flat):]
        groups = []
        for k in range(3):
            acc = None
            for p, d in zip(parts[k * n_attn:(k + 1) * n_attn], DILATIONS):
                v = _from_dilated(p, scr, d, tm)
                acc = v if acc is None else acc + v
            groups.append(acc)
        groups += [p[...] for p in parts[3 * n_attn:]]
        dh = jnp.zeros((tm, D_MODEL), F32)
        for gi, grp in enumerate(groups):
            cols = slice(gi * aw, (gi + 1) * aw)
            gb = grp.astype(BF16)
            dproj_ref[:, cols] = gb
            dh = dh + _dot_nt(gb, w_ref[:, cols])
        dxa, g_c = _rms_bwd(dh, x_ref[...], g_ref[...], D_MODEL)
        _acc_rows(dg_ref, g_c)
        dx_ref[...] = dx1_ref[...] + dxa

    dm = D_MODEL
    return pl.pallas_call(
        body,
        name="in_proj_bwd",
        grid=(s // tm,),
        in_specs=[_dilated_spec(d, tm, aw) for d in DILATIONS] * 3 + [_row_spec(tm, aw)] * 4 + [
            _vmem_spec(), _row_spec(tm, dm), _const_spec(dm), _row_spec(tm, dm)],
        out_specs=[_row_spec(tm, dm), _row_spec(tm, IN_PROJ_WIDTH), _const_spec(dm)],
        out_shape=[jax.ShapeDtypeStruct((s, dm), F32), jax.ShapeDtypeStruct((s, IN_PROJ_WIDTH), BF16),
                   jax.ShapeDtypeStruct((1, dm), F32)],
        scratch_shapes=[pltpu.VMEM((aw // LANES, tm, LANES), F32)],
        compiler_params=_params(dimension_semantics=("arbitrary",)),
    )(*flat, w_in_b, x, g1, dx1)


def wgrad(a_b, b_b, tn, name, ts=2048, per_step=1, ride=None):
    s, k = a_b.shape
    n = b_b.shape[1]

    def body(a_ref, b_ref, o_ref):
        @pl.when(pl.program_id(1) == 0)
        def _():
            o_ref[...] = jnp.zeros_like(o_ref)

        a = a_ref[...]
        for jj in range(per_step):
            o_ref[jj] += _dot_tn(a, b_ref[:, jj * tn:(jj + 1) * tn])

    wide = tn * per_step
    gn, gs = n // wide, s // ts
    step = lambda j, i: (lambda: (pl.program_id(0) == j) & (pl.program_id(1) == i))
    e_in, e_out, e_shape, e_scr, e_args = _ride_specs(ride)
    out = pl.pallas_call(
        _riding(body, 2, 1, 0, ride, step(0, 0), step(gn // 2, 0), step(gn - 1, gs - 1)),
        name=name,
        grid=(gn, gs),
        in_specs=[pl.BlockSpec((ts, k), lambda j, i: (i, 0)), pl.BlockSpec((ts, wide), lambda j, i: (i, j))] + e_in,
        out_specs=[pl.BlockSpec((per_step, k, tn), lambda j, i: (j, 0, 0))] + e_out,
        out_shape=[jax.ShapeDtypeStruct((n // tn, k, tn), F32)] + e_shape,
        scratch_shapes=e_scr,
        compiler_params=_params(ride, dimension_semantics=("arbitrary", "arbitrary")),
    )(a_b, b_b, *e_args)
    return out[0] if ride is None else out


def train_step(x, target, g1, an, logits, hn, gp, g_pre, g_post, w, m, v):
    nd = len(DILATIONS)
    shard_b = {k: w[k].astype(BF16) for k in BIG}
    (w_in_g,) = run_exchange(gather_exchange([shard_b["w_in"]]), "gather_w_in")
    w_in_b = w_in_g.transpose(1, 0, 2).reshape(D_MODEL, IN_PROJ_WIDTH)

    proj, h_b, *qkvs, w2_g = in_proj_fwd(x, g1, w_in_b, ride=gather_exchange([shard_b["w_ff2"]]))
    w2_b = w2_g.reshape(D_FF, D_MODEL)
    attn_parts = [attn_fwd(qkv, d) for qkv, d in zip(qkvs, DILATIONS)]
    o_h, states, a_mat, w_out_g, w1_blocks = hgrn_fwd(
        proj, logits, ride=gather_exchange([shard_b["w_out"], shard_b["w_ff1"]]))
    w_out_b = w_out_g.reshape(D_MODEL, D_MODEL)
    x1, cat_b, mixed, attn, *lses = mix_fwd(attn_parts, o_h, proj, an, hn, w_out_b, gp, x)
    dx1, h2_b, a_b, du_b, dff_b, loss_vec, dg_pre, dg_post = mlp_fwd_bwd(x1, g_pre, w1_blocks, w2_b, g_post, target)
    dw2 = wgrad(a_b, dff_b, D_MODEL, "wgrad_ff2", ts=512)
    dw1 = wgrad(h2_b, du_b, D_FF // N_DEV, "wgrad_ff1", per_step=2)
    dmix_b, *rest = mix_bwd(dx1, mixed, gp, w_out_b, attn, an, o_h, proj, hn)
    d_os, deltas = rest[:nd], rest[nd:2 * nd]
    d_oh, dgate, dgp, dan, dhn = rest[2 * nd:]
    dwout = wgrad(cat_b, dmix_b, D_MODEL, "wgrad_out")

    early = ("w_out", "w_ff1", "w_ff2")
    early_grads = [dwout.reshape(N_DEV, D_MODEL // N_DEV, D_MODEL), dw1, dw2.reshape(N_DEV, D_FF // N_DEV, D_MODEL)]
    res = attn_bwd(qkvs[0], d_os[0], lses[0], deltas[0], DILATIONS[0], ride=to_core_exchange(early_grads))
    pairs = [pair_sum(g, s, f"pair_sum_{name}") for g, s, name in zip(early_grads, res[3:], early)]
    attn_grads = [res[:3]]
    *res, others_ff2 = attn_bwd(qkvs[1], d_os[1], lses[1], deltas[1], DILATIONS[1],
                                ride=to_chip_exchange([pairs[2][1]]))
    attn_grads.append(res)
    attn_grads.append(attn_bwd(qkvs[2], d_os[2], lses[2], deltas[2], DILATIONS[2]))
    dq_h, df_h, di_h, dlb, *others = hgrn_bwd(proj, logits, d_oh, states, a_mat,
                                              ride=to_chip_exchange([pairs[0][1], pairs[1][1]]))
    others.append(others_ff2)
    dx, dproj_b, dg1 = in_proj_bwd(attn_grads, (dq_h, df_h, di_h), dgate, w_in_b, x, g1, dx1)
    packed = _pack_small(dg1, dgp, dg_pre, dg_post, dan, dhn, dlb, loss_vec)
    dwin, small_slots = wgrad(h_b, dproj_b, 2 * IN_PROJ_WIDTH // N_DEV, "wgrad_in",
                              ride=small_exchange(packed))
    big = {name: sum_adamw(p[0], o, w[name], m[name], v[name], f"sum_adamw_{name}")
           for name, p, o in zip(early, pairs, others)}

    shard_w = IN_PROJ_WIDTH // N_DEV
    dwin_blocks = dwin.reshape(N_DEV // 2, D_MODEL, 2, shard_w).transpose(0, 2, 1, 3).reshape(N_DEV, D_MODEL, shard_w)
    big["w_in"] = reduce_last(dwin_blocks, w["w_in"], m["w_in"], v["w_in"])
    return dx, big, small_slots


def _position():
    x, y, c = lax.axis_index("x"), lax.axis_index("y"), lax.axis_index("c")
    other_chips = [(1 - x, y), (x, 1 - y), (1 - x, 1 - y)]
    return x, y, c, other_chips


def _any_spec():
    return pl.BlockSpec(memory_space=pl.ANY)


class Exchange:
    def __init__(self, arrays, out_shape, sems, stages, collective_id, peers):
        self.arrays, self.out_shape, self.sems, self.stages = list(arrays), list(out_shape), list(sems), stages
        self.collective_id, self.peers = collective_id, peers

    def open(self):
        barrier = pltpu.get_barrier_semaphore()
        peers = self.peers()
        for peer in peers:
            pl.semaphore_signal(barrier, inc=1, device_id=peer, device_id_type=MESH)
        pl.semaphore_wait(barrier, len(peers))


def _siblings():
    x, y, c, _ = _position()
    return [(x, y, 1 - c)]


def _same_core_of_other_chips():
    x, y, c, chips = _position()
    return [(px, py, c) for px, py in chips]


def _gather_peers():
    x, y, c, _ = _position()
    return [(x, y, 1 - c), (1 - x, y, c), (x, 1 - y, c)]


def _all_others():
    x, y, c, _ = _position()
    return [(1 - x if rel & 4 else x, 1 - y if rel & 2 else y, 1 - c if rel & 1 else c) for rel in range(1, N_DEV)]


def gather_exchange(shards):
    n = len(shards)
    halves = [sh.shape[0] // 2 for sh in shards]

    def stages(ins, outs, sems):
        send_sems, recv_sems, local_sems = sems

        def parts():
            x, y, c, _ = _position()
            me, sibling = (x, y, c), (x, y, 1 - c)
            nbr_x, nbr_y, diag = (1 - x, y, c), (x, 1 - y, c), (1 - x, 1 - y, c)

            def slot(a, dev, rows=None):
                ref = outs[a].at[4 * dev[0] + 2 * dev[1] + dev[2]]
                return ref if rows is None else ref.at[rows]

            def copy(a, k, block, to, rows=None, src=None):
                return pltpu.make_async_remote_copy(
                    src_ref=slot(a, block, rows) if src is None else src, dst_ref=slot(a, block, rows),
                    send_sem=send_sems.at[a, k], recv_sem=recv_sems.at[a, k], device_id=to, device_id_type=MESH)

            upper = lambda a: pl.ds(0, halves[a])
            lower = lambda a: pl.ds(halves[a], halves[a])
            return me, sibling, nbr_x, nbr_y, diag, slot, copy, upper, lower

        def begin():
            me, sibling, nbr_x, nbr_y, _, slot, copy, _, _ = parts()
            for a in range(n):
                pltpu.make_async_copy(ins[a], slot(a, me), local_sems.at[a]).start()
                for k, to in enumerate((sibling, nbr_x, nbr_y)):
                    copy(a, k, me, to, src=ins[a]).start()

        def middle():
            me, sibling, nbr_x, nbr_y, _, _, copy, upper, lower = parts()
            for a in range(n):
                copy(a, 1, nbr_x, me).wait_recv()
                copy(a, 3, nbr_x, sibling).start()
                copy(a, 5, nbr_x, nbr_y, rows=lower(a)).start()
                copy(a, 2, nbr_y, me).wait_recv()
                copy(a, 4, nbr_y, sibling).start()
                copy(a, 6, nbr_y, nbr_x, rows=upper(a)).start()

        def late():
            me, sibling, _, _, diag, _, copy, upper, lower = parts()
            for a in range(n):
                copy(a, 6, diag, me, rows=upper(a)).wait_recv()
                copy(a, 5, diag, me, rows=lower(a)).wait_recv()
                copy(a, 7, diag, sibling).start()

        def end():
            me, sibling, nbr_x, nbr_y, diag, slot, copy, upper, lower = parts()
            sib = lambda dev: (dev[0], dev[1], sibling[2])
            for a in range(n):
                for k, block in ((0, sibling), (3, sib(nbr_x)), (4, sib(nbr_y)), (7, sib(diag))):
                    copy(a, k, block, me).wait_recv()
                copy(a, 0, me, sibling, src=ins[a]).wait_send()
                copy(a, 1, me, nbr_x, src=ins[a]).wait_send()
                copy(a, 2, me, nbr_y, src=ins[a]).wait_send()
                copy(a, 3, nbr_x, sibling).wait_send()
                copy(a, 4, nbr_y, sibling).wait_send()
                copy(a, 5, nbr_x, nbr_y, rows=lower(a)).wait_send()
                copy(a, 6, nbr_y, nbr_x, rows=upper(a)).wait_send()
                copy(a, 7, diag, sibling).wait_send()
                pltpu.make_async_copy(ins[a], slot(a, me), local_sems.at[a]).wait()

        return begin, (middle, late), end

    return Exchange(
        shards, [jax.ShapeDtypeStruct((N_DEV,) + sh.shape, sh.dtype) for sh in shards],
        [pltpu.SemaphoreType.DMA((n, 8)), pltpu.SemaphoreType.DMA((n, 8)), pltpu.SemaphoreType.DMA((n,))], stages,
        collective_id=0, peers=_gather_peers)


def to_core_exchange(grads):
    n = len(grads)

    def stages(ins, outs, sems):
        send_sems, recv_sems = sems

        def copies():
            x, y, c, _ = _position()
            return [pltpu.make_async_remote_copy(
                src_ref=ins[a].at[2 * q + (1 - c)], dst_ref=outs[a].at[q], send_sem=send_sems.at[a, q],
                recv_sem=recv_sems.at[a, q], device_id=(x, y, 1 - c), device_id_type=MESH)
                for a in range(n) for q in range(4)]

        def begin():
            for cp in copies():
                cp.start()

        def end():
            for cp in copies():
                cp.wait()

        return begin, None, end

    return Exchange(grads, [jax.ShapeDtypeStruct((4,) + g.shape[1:], g.dtype) for g in grads],
                    [pltpu.SemaphoreType.DMA((n, 4)), pltpu.SemaphoreType.DMA((n, 4))], stages,
                    collective_id=1, peers=_siblings)


def pair_sum(grad, from_sibling, name):
    _, r, cdim = grad.shape
    tr = min(r, ELEMENTWISE_ROWS)
    c_idx = lax.axis_index("c").astype(jnp.int32).reshape(1)

    def body(c_ref, g_ref, s_ref, o_ref, ob_ref):
        total = g_ref[...] + s_ref[...]
        o_ref[...] = total
        ob_ref[...] = total.astype(BF16)

    blk = lambda: pl.BlockSpec((1, tr, cdim), lambda q, i, cr: (q, i, 0))
    return pl.pallas_call(
        body,
        name=name,
        grid_spec=pltpu.PrefetchScalarGridSpec(
            num_scalar_prefetch=1,
            grid=(4, r // tr),
            in_specs=[pl.BlockSpec((1, tr, cdim), lambda q, i, cr: (2 * q + cr[0], i, 0)), blk()],
            out_specs=[blk(), blk()],
        ),
        out_shape=[jax.ShapeDtypeStruct((4, r, cdim), F32), jax.ShapeDtypeStruct((4, r, cdim), BF16)],
        compiler_params=_params(dimension_semantics=("arbitrary", "arbitrary")),
    )(c_idx, grad, from_sibling)


def to_chip_exchange(pairs):
    n = len(pairs)

    def stages(ins, outs, sems):
        send_sems, recv_sems = sems

        def copies():
            x, y, c, chips = _position()
            return [pltpu.make_async_remote_copy(
                src_ref=ins[a].at[2 * px + py], dst_ref=outs[a].at[j], send_sem=send_sems.at[a, j],
                recv_sem=recv_sems.at[a, j], device_id=(px, py, c), device_id_type=MESH)
                for a in range(n) for j, (px, py) in enumerate(chips)]

        def begin():
            for cp in copies():
                cp.start()

        def end():
            for cp in copies():
                cp.wait()

        return begin, None, end

    return Exchange(pairs, [jax.ShapeDtypeStruct((3,) + p.shape[1:], p.dtype) for p in pairs],
                    [pltpu.SemaphoreType.DMA((n, 3)), pltpu.SemaphoreType.DMA((n, 3))], stages,
                    collective_id=2, peers=_same_core_of_other_chips)


def run_exchange(ex, name):
    n_in, n_out = len(ex.arrays), len(ex.out_shape)

    def body(*refs):
        begin, middle, end = ex.stages(refs[:n_in], refs[n_in:n_in + n_out], refs[n_in + n_out:])
        ex.open()
        begin()
        for stage in _as_tuple(middle):
            stage()
        end()

    return pl.pallas_call(
        body,
        name=name,
        in_specs=[_any_spec()] * n_in,
        out_specs=[_any_spec()] * n_out,
        out_shape=ex.out_shape,
        scratch_shapes=ex.sems,
        compiler_params=pltpu.CompilerParams(collective_id=ex.collective_id),
    )(*ex.arrays)


def _as_tuple(stages):
    return () if stages is None else stages if isinstance(stages, tuple) else (stages,)


def _riding(body, n_in, n_out, n_scratch, ex, first, middle, last, late=None):
    if ex is None:
        return body
    r_in, r_out = len(ex.arrays), len(ex.out_shape)

    def wrapped(*refs):
        k_in, refs = refs[:n_in], refs[n_in:]
        e_in, refs = refs[:r_in], refs[r_in:]
        k_out, refs = refs[:n_out], refs[n_out:]
        e_out, refs = refs[:r_out], refs[r_out:]
        k_scr, e_sems = refs[:n_scratch], refs[n_scratch:]
        begin, mid, end = ex.stages(e_in, e_out, e_sems)

        @pl.when(first())
        def _():
            ex.open()
            begin()

        body(*k_in, *k_out, *k_scr)
        for stage, at in zip(_as_tuple(mid), (middle, late or last)):
            pl.when(at())(stage)
        pl.when(last())(end)

    return wrapped


def _ride_specs(ex):
    if ex is None:
        return [], [], [], [], []
    return [_any_spec()] * len(ex.arrays), [_any_spec()] * len(ex.out_shape), ex.out_shape, ex.sems, ex.arrays


def reduce_last(grad, w, m, v):
    _, r, cdim = grad.shape

    def body(g_hbm, w_hbm, m_hbm, v_hbm, g_out, d_out, m_out, v_out, g_buf, to_sib, from_sib, send_buf, others_hbm,
             wmv_buf, load_sem, wmv_sems, send_sems, recv_sems):
        x, y, c, chips = _position()
        order = chips + [(x, y)]
        n_other = len(chips)
        sibling = (x, y, 1 - c)
        barrier = pltpu.get_barrier_semaphore()
        peers = [sibling] + [(px, py, c) for px, py in chips]
        for peer in peers:
            pl.semaphore_signal(barrier, inc=1, device_id=peer, device_id_type=MESH)
        pl.semaphore_wait(barrier, len(peers))

        def load(block):
            cp = pltpu.make_async_copy(g_hbm.at[block], g_buf, load_sem)
            cp.start()
            cp.wait()
            return g_buf[...]

        def to_sibling(j):
            return pltpu.make_async_remote_copy(
                src_ref=to_sib.at[j], dst_ref=from_sib.at[j], send_sem=send_sems.at[n_other + j],
                recv_sem=recv_sems.at[n_other + j], device_id=sibling, device_id_type=MESH)

        def to_chip(j):
            px, py = chips[j]
            return pltpu.make_async_remote_copy(
                src_ref=send_buf.at[j], dst_ref=others_hbm.at[j], send_sem=send_sems.at[j], recv_sem=recv_sems.at[j],
                device_id=(px, py, c), device_id_type=MESH)

        def hand_over(j):
            px, py = order[j]
            to_sib[j] = load(2 * (2 * px + py) + (1 - c)).astype(BF16)
            to_sibling(j).start()

        def pair_sum_of(j):
            px, py = order[j]
            to_sibling(j).wait_recv()
            total = load(2 * (2 * px + py) + c) + from_sib[j].astype(F32)
            if j < n_other:
                send_buf[j] = total.astype(BF16)
                to_chip(j).start()
            else:
                g_out[...] = total

        shard_loads = [pltpu.make_async_copy(src, wmv_buf.at[k], wmv_sems.at[k])
                       for k, src in enumerate((w_hbm, m_hbm, v_hbm))]
        for cp in shard_loads:
            cp.start()
        turn = [n_other - 1] + list(range(n_other - 1)) + [n_other]
        hand_over(turn[0])
        for before, j in zip(turn, turn[1:]):
            hand_over(j)
            pair_sum_of(before)
        pair_sum_of(turn[-1])
        for j in range(len(order)):
            to_sibling(j).wait_send()
        for j in range(n_other):
            to_chip(j).wait()
        for cp in shard_loads:
            cp.wait()
        g = g_out[...]
        for j in range(n_other):
            g = g + others_hbm[j].astype(F32)
        g_out[...] = g
        d_out[...], m_out[...], v_out[...] = _adamw(wmv_buf[0], g, wmv_buf[1], wmv_buf[2])

    n_blocks = N_DEV // 2
    return pl.pallas_call(
        body,
        name="reduce_w_in",
        in_specs=[_any_spec()] * 4,
        out_specs=[_vmem_spec()] * 4,
        out_shape=[jax.ShapeDtypeStruct((r, cdim), F32)] * 4,
        scratch_shapes=[pltpu.VMEM((r, cdim), F32), pltpu.VMEM((n_blocks, r, cdim), BF16),
                        pltpu.VMEM((n_blocks, r, cdim), BF16), pltpu.VMEM((n_blocks - 1, r, cdim), BF16),
                        pltpu.VMEM((n_blocks - 1, r, cdim), BF16), pltpu.VMEM((3, r, cdim), F32),
                        pltpu.SemaphoreType.DMA(()), pltpu.SemaphoreType.DMA((3,)),
                        pltpu.SemaphoreType.DMA((2 * n_blocks - 1,)), pltpu.SemaphoreType.DMA((2 * n_blocks - 1,))],
        compiler_params=pltpu.CompilerParams(collective_id=4, vmem_limit_bytes=VMEM_LIMIT),
    )(grad, w, m, v)


def _adamw(w, g, m, v):
    m = ADAM_B1 * m + (1.0 - ADAM_B1) * g
    v = ADAM_B2 * v + (1.0 - ADAM_B2) * (g * g)
    m_hat = m / (1.0 - ADAM_B1 ** ADAM_STEP)
    v_hat = v / (1.0 - ADAM_B2 ** ADAM_STEP)
    delta = -ADAM_LR * (m_hat / (jnp.sqrt(v_hat) + ADAM_EPS) + ADAM_WD * w)
    return delta, m, v


def sum_adamw(pairs, others, w, m, v, name):
    r, cdim = w.shape
    tr = min(r, ELEMENTWISE_ROWS // 2)
    if pairs.shape[0] == 1:
        chip_idx = jnp.zeros((1,), jnp.int32)
    else:
        chip_idx = (2 * lax.axis_index("x") + lax.axis_index("y")).astype(jnp.int32).reshape(1)

    def body(q_ref, p_ref, o_ref, w_ref, m_ref, v_ref, g_out, d_out, m_out, v_out):
        g = p_ref[0] + o_ref[0].astype(F32) + o_ref[1].astype(F32) + o_ref[2].astype(F32)
        g_out[...] = g
        d_out[...], m_out[...], v_out[...] = _adamw(w_ref[...], g, m_ref[...], v_ref[...])

    tile = lambda: pl.BlockSpec((tr, cdim), lambda i, qr: (i, 0))
    return pl.pallas_call(
        body,
        name=name,
        grid_spec=pltpu.PrefetchScalarGridSpec(
            num_scalar_prefetch=1,
            grid=(r // tr,),
            in_specs=[pl.BlockSpec((1, tr, cdim), lambda i, qr: (qr[0], i, 0)),
                      pl.BlockSpec((3, tr, cdim), lambda i, qr: (0, i, 0)), tile(), tile(), tile()],
            out_specs=[tile(), tile(), tile(), tile()],
        ),
        out_shape=[jax.ShapeDtypeStruct((r, cdim), F32)] * 4,
        compiler_params=_params(dimension_semantics=("arbitrary",)),
    )(chip_idx, pairs, others, w, m, v)


def small_exchange(packed):
    def stages(ins, outs, sems):
        send_sems, recv_sems, local_sem = sems
        (src,), (slots,) = ins, outs

        def copies():
            x, y, c, _ = _position()
            my_id = 4 * x + 2 * y + c
            sends, landings = [], []
            for rel in range(1, N_DEV):
                px = 1 - x if (rel >> 2) & 1 else x
                py = 1 - y if (rel >> 1) & 1 else y
                pc = 1 - c if rel & 1 else c
                peer = dict(send_sem=send_sems.at[rel - 1], recv_sem=recv_sems.at[rel - 1], device_id=(px, py, pc),
                            device_id_type=MESH)
                sends.append(pltpu.make_async_remote_copy(src_ref=src, dst_ref=slots.at[my_id], **peer))
                landings.append(pltpu.make_async_remote_copy(src_ref=src, dst_ref=slots.at[4 * px + 2 * py + pc], **peer))
            return pltpu.make_async_copy(src, slots.at[my_id], local_sem), sends, landings

        def begin():
            local, sends, _ = copies()
            local.start()
            for cp in sends:
                cp.start()

        def end():
            local, sends, landings = copies()
            for cp in landings:
                cp.wait_recv()
            for cp in sends:
                cp.wait_send()
            local.wait()

        return begin, None, end

    return Exchange([packed], [jax.ShapeDtypeStruct((N_DEV,) + packed.shape, packed.dtype)],
                    [pltpu.SemaphoreType.DMA((N_DEV - 1,)), pltpu.SemaphoreType.DMA((N_DEV - 1,)),
                     pltpu.SemaphoreType.DMA(())], stages, collective_id=3, peers=_all_others)


def small_adamw(slots, w, m, v):
    def body(r_ref, w_ref, m_ref, v_ref, g_out, d_out, m_out, v_out, loss_out):
        red = r_ref[0]
        for k in range(1, N_DEV):
            red = red + r_ref[k]
        wv = w_ref[...]
        lb = _lower_bound(jnp.concatenate([wv[5:6, :HGRN_WIDTH], wv[5:6, HGRN_WIDTH:]], axis=0))
        t = red[5:6, :HGRN_WIDTH] * lb * (1.0 - lb)
        row = lax.broadcasted_iota(jnp.int32, red.shape, 0)
        g = jnp.where(row == 5, jnp.concatenate([t, -t], axis=1), jnp.where(row >= 6, 0.0, red))
        g_out[...] = g
        d_out[...], m_out[...], v_out[...] = _adamw(wv, g, m_ref[...], v_ref[...])
        loss = jnp.sum(red[6:7, :], axis=-1, keepdims=True) * (0.5 / D_MODEL)
        loss_out[...] = jnp.broadcast_to(loss, loss_out.shape)

    return pl.pallas_call(
        body,
        name="small_adamw",
        in_specs=[_vmem_spec()] * 4,
        out_specs=[_vmem_spec()] * 5,
        out_shape=[jax.ShapeDtypeStruct(w.shape, F32)] * 4 + [jax.ShapeDtypeStruct((SUBLANES, LANES), F32)],
    )(slots, w, m, v)


def _pack_small(g1, gp, g_pre, g_post, an, hn, logits_or_dlb, extra=None):
    row5 = logits_or_dlb.reshape(1, -1)
    row5 = jnp.pad(row5, ((0, 0), (0, D_MODEL - row5.shape[1])))
    row6 = jnp.zeros((1, D_MODEL), F32) if extra is None else extra
    return jnp.concatenate([g1, gp, g_pre, g_post, jnp.concatenate([an, hn], axis=1), row5, row6,
                            jnp.zeros((1, D_MODEL), F32)], axis=0)


def _unpack_small(p):
    return dict(mix_pre_norm=p[0:1], mix_post_norm=p[1:2], mlp_pre_norm=p[2:3], mlp_post_norm=p[3:4],
                attn_out_norm=p[4:5, :ATTN_WIDTH], hgrn_out_norm=p[4:5, ATTN_WIDTH:],
                hgrn_lb_logits=p[5].reshape(2, HGRN_WIDTH))


BIG = ("w_in", "w_out", "w_ff1", "w_ff2")
ORDER = ("mix_pre_norm", "w_in", "attn_out_norm", "hgrn_lb_logits", "hgrn_out_norm", "w_out", "mix_post_norm",
         "mlp_pre_norm", "w_ff1", "w_ff2", "mlp_post_norm")


def kernel(x, mix_pre_norm, w_in, attn_out_norm, hgrn_lb_logits, hgrn_out_norm, w_out, mix_post_norm, mlp_pre_norm, w_ff1, w_ff2, mlp_post_norm, loss_target, m_mix_pre_norm, m_w_in, m_attn_out_norm, m_hgrn_lb_logits, m_hgrn_out_norm, m_w_out, m_mix_post_norm, m_mlp_pre_norm, m_w_ff1, m_w_ff2, m_mlp_post_norm, v_mix_pre_norm, v_w_in, v_attn_out_norm, v_hgrn_lb_logits, v_hgrn_out_norm, v_w_out, v_mix_post_norm, v_mlp_pre_norm, v_w_ff1, v_w_ff2, v_mlp_post_norm):
    w = dict(w_in=w_in[0], w_out=w_out[0], w_ff1=w_ff1[0], w_ff2=w_ff2[0])
    m = dict(w_in=m_w_in[0], w_out=m_w_out[0], w_ff1=m_w_ff1[0], w_ff2=m_w_ff2[0])
    v = dict(w_in=v_w_in[0], w_out=v_w_out[0], w_ff1=v_w_ff1[0], w_ff2=v_w_ff2[0])

    dx, big, small_slots = train_step(x[0], loss_target[0], mix_pre_norm, attn_out_norm, hgrn_lb_logits, hgrn_out_norm,
                                      mix_post_norm, mlp_pre_norm, mlp_post_norm, w, m, v)

    pack = lambda a, b, c2, d, e, f, g: _pack_small(a, b, c2, d, e, f, g)
    w_s = pack(mix_pre_norm, mix_post_norm, mlp_pre_norm, mlp_post_norm, attn_out_norm, hgrn_out_norm, hgrn_lb_logits)
    m_s = pack(m_mix_pre_norm, m_mix_post_norm, m_mlp_pre_norm, m_mlp_post_norm, m_attn_out_norm, m_hgrn_out_norm,
               m_hgrn_lb_logits)
    v_s = pack(v_mix_pre_norm, v_mix_post_norm, v_mlp_pre_norm, v_mlp_post_norm, v_attn_out_norm, v_hgrn_out_norm,
               v_hgrn_lb_logits)
    g_s, d_s, nm_s, nv_s, loss = small_adamw(small_slots, w_s, m_s, v_s)
    small_out = [_unpack_small(t) for t in (g_s, d_s, nm_s, nv_s)]

    outs = [loss[0, 0], dx[None]]
    for kind in range(4):
        for name in ORDER:
            outs.append(big[name][kind][None] if name in BIG else small_out[kind][name])
    return tuple(outs)
```

```python
import jax
import jax.numpy as jnp
from jax import lax
from jax.experimental import pallas as pl
from jax.experimental.pallas import tpu as pltpu

F32 = jnp.float32
BF16 = jnp.bfloat16

D_MODEL = 1024
ATTN_WIDTH = 512
ATTN_HEAD_DIM = 64
ATTN_HEADS = 8
ATTN_BLOCK = 128
DILATIONS = (1, 4, 16)
HGRN_WIDTH = 512
HGRN_HEADS = 4
HGRN_HEAD_DIM = 128
HGRN_CHUNK = 64
IN_PROJ_WIDTH = 3584
D_FF = 4096
RMS_EPS = 1e-6
N_DEV = 8
ADAM_LR = 0.001
ADAM_B1 = 0.9
ADAM_B2 = 0.999
ADAM_EPS = 1e-08
ADAM_WD = 0.01
ADAM_STEP = 10

SUBLANES = 8
LANES = 128
COLUMN_UNROLL = 8
HGRN_CHUNKS_PER_STEP = 2
SUB_BLOCK = 16
TOKEN_TILE = 512
ELEMENTWISE_ROWS = 1024
MLP_TILE = 256
PROJ_TILE = 512
VMEM_BYTES_V7X = 64 * 1024 * 1024
VMEM_LIMIT = VMEM_BYTES_V7X // 8 * 7
NEG_BIG = -1e30
MESH = pl.DeviceIdType.MESH


def _params(ride=None, **kw):
    if ride is not None:
        kw["collective_id"] = ride.collective_id
    return pltpu.CompilerParams(vmem_limit_bytes=VMEM_LIMIT, **kw)


def _vmem_spec():
    return pl.BlockSpec(memory_space=pltpu.VMEM)


def _dot(a, b):
    return jnp.dot(a, b, preferred_element_type=F32)


def _dot_nt(a, b):
    return lax.dot_general(a, b, (((1,), (1,)), ((), ())), preferred_element_type=F32)


def _dot_tn(a, b):
    return lax.dot_general(a, b, (((0,), (0,)), ((), ())), preferred_element_type=F32)


def _sigmoid(x):
    return 1.0 / (1.0 + jnp.exp(-x))


def _rms_fwd(x, gain, width):
    r = lax.rsqrt(jnp.sum(x * x, axis=-1, keepdims=True) * (1.0 / width) + RMS_EPS)
    return x * r * gain


def _rms_bwd(dy, x, gain, width):
    r = lax.rsqrt(jnp.sum(x * x, axis=-1, keepdims=True) * (1.0 / width) + RMS_EPS)
    xhat = x * r
    dxhat = dy * gain
    dx = r * (dxhat - xhat * (jnp.sum(dxhat * xhat, axis=-1, keepdims=True) * (1.0 / width)))
    return dx, dy * xhat


def _split3(x):
    hi = x.astype(BF16)
    r1 = x - hi.astype(F32)
    mid = r1.astype(BF16)
    lo = (r1 - mid.astype(F32)).astype(BF16)
    return hi, mid, lo


def _tri_sum(tri_bf16, x):
    hi, mid, lo = _split3(x)
    return _dot(tri_bf16, hi) + _dot(tri_bf16, mid) + _dot(tri_bf16, lo)


def _dilated_spec(d, tm, width):
    return pl.BlockSpec((d, tm // d, width), lambda i: (0, i, 0))


def _lane_blocks(ref, value):
    for c in range(ref.shape[0]):
        ref[c] = value[:, c * LANES:(c + 1) * LANES]


def _to_dilated(src_ref, dst_ref, d, tm, cast=None):
    for r in range(d):
        for c in range(src_ref.shape[0]):
            v = src_ref[c] if d == 1 else src_ref[c, pl.ds(r, tm // d, stride=d), :]
            dst_ref[r, :, c * LANES:(c + 1) * LANES] = v if cast is None else v.astype(cast)


def _from_dilated(src_ref, scratch_ref, d, tm):
    if d == 1:
        return src_ref[0].astype(F32)
    nblk = scratch_ref.shape[0]
    for r in range(d):
        for c in range(nblk):
            scratch_ref[c, pl.ds(r, tm // d, stride=d), :] = src_ref[r, :, c * LANES:(c + 1) * LANES].astype(F32)
    return jnp.concatenate([scratch_ref[c] for c in range(nblk)], axis=1)


def in_proj_fwd(x, g1, w_in_b, ride=None):
    s = x.shape[0]
    tm = PROJ_TILE
    qkv_w = 3 * ATTN_WIDTH
    hg_w = IN_PROJ_WIDTH - qkv_w

    def body(x_ref, g_ref, w_ref, hg_ref, h_ref, *rest):
        qkv_refs, qkv_scr = rest[:len(DILATIONS)], rest[len(DILATIONS)]
        h = _rms_fwd(x_ref[...], g_ref[...], D_MODEL).astype(BF16)
        h_ref[...] = h
        proj = _dot(h, w_ref[...])
        hg_ref[...] = proj[:, qkv_w:]
        _lane_blocks(qkv_scr, proj[:, :qkv_w])
        for d, ref in zip(DILATIONS, qkv_refs):
            _to_dilated(qkv_scr, ref, d, tm, cast=BF16)

    n_steps = s // tm
    step = lambda k: (lambda: pl.program_id(0) == k)
    e_in, e_out, e_shape, e_scr, e_args = _ride_specs(ride)
    return pl.pallas_call(
        _riding(body, 3, 2 + len(DILATIONS), 1, ride, step(0), step(n_steps // 2), step(n_steps - 1),
                late=step(n_steps - 2)),
        name="in_proj_fwd",
        grid=(n_steps,),
        in_specs=[
            pl.BlockSpec((tm, D_MODEL), lambda i: (i, 0)),
            pl.BlockSpec((1, D_MODEL), lambda i: (0, 0)),
            _vmem_spec(),
        ] + e_in,
        out_specs=[
            pl.BlockSpec((tm, hg_w), lambda i: (i, 0)),
            pl.BlockSpec((tm, D_MODEL), lambda i: (i, 0)),
        ] + [_dilated_spec(d, tm, qkv_w) for d in DILATIONS] + e_out,
        out_shape=[jax.ShapeDtypeStruct((s, hg_w), F32), jax.ShapeDtypeStruct((s, D_MODEL), BF16)] + [
            jax.ShapeDtypeStruct((d, s // d, qkv_w), BF16) for d in DILATIONS] + e_shape,
        scratch_shapes=[pltpu.VMEM((qkv_w // LANES, tm, LANES), F32)] + e_scr,
        compiler_params=_params(ride, dimension_semantics=("arbitrary",)),
    )(x, g1, w_in_b, *e_args)


ATTN_SCALE = ATTN_HEAD_DIM ** -0.5


def _fill_attn_bias(bias_ref, dilation):
    qi = lax.broadcasted_iota(jnp.int32, (ATTN_BLOCK, 2 * ATTN_BLOCK), 0)
    kj = lax.broadcasted_iota(jnp.int32, (ATTN_BLOCK, 2 * ATTN_BLOCK), 1)
    dist = qi + ATTN_BLOCK - kj
    valid = (dist >= 0) & (dist <= ATTN_BLOCK)
    for head in range(ATTN_HEADS):
        slope = 2.0 ** (-8.0 * (head + 1) / ATTN_HEADS)
        bias = jnp.where(valid, dist.astype(F32) * (-slope * dilation), NEG_BIG)
        bias_ref[0, head] = bias
        bias_ref[1, head] = jnp.where(kj >= ATTN_BLOCK, bias, NEG_BIG)


def _stack_heads(x):
    low = _lane_half(x.shape, 0)
    zero = jnp.zeros_like(x)
    return jnp.concatenate([jnp.where(low, x, zero), jnp.where(low, zero, x)], axis=0)


def _unstack_heads(y):
    half = y.shape[0] // 2
    return jnp.where(_lane_half((half, y.shape[1]), 0), y[:half], y[half:])


def _attn_scores(q_stack, kcat, bias_ref, pair, first_block):
    f = first_block.astype(jnp.int32)
    bias = jnp.concatenate([bias_ref[f, 2 * pair], bias_ref[f, 2 * pair + 1]], axis=0)
    return _dot_nt(q_stack, kcat) + bias


def _lane_half(shape, sub):
    lane = lax.broadcasted_iota(jnp.int32, shape, 1)
    return (lane < ATTN_HEAD_DIM) if sub == 0 else (lane >= ATTN_HEAD_DIM)


def _sub_block(col, row):
    return pl.BlockSpec((None, ATTN_BLOCK, ATTN_WIDTH), lambda r, n: (r, row(n), col))


def attn_fwd(qkv, dilation, ride=None):
    d, length, _ = qkv.shape
    assert d == dilation
    nb = length // ATTN_BLOCK

    def body(q_ref, kc_ref, kp_ref, vc_ref, vp_ref, o_ref, lse_ref, bias_ref):
        @pl.when((pl.program_id(0) == 0) & (pl.program_id(1) == 0))
        def _():
            _fill_attn_bias(bias_ref, d)

        first = pl.program_id(1) == 0
        for pair in range(ATTN_HEADS // 2):
            lanes = slice(pair * LANES, (pair + 1) * LANES)
            q_stack = _stack_heads(q_ref[:, lanes] * ATTN_SCALE)
            kcat = jnp.concatenate([kp_ref[:, lanes], kc_ref[:, lanes]], axis=0)
            vcat = jnp.concatenate([vp_ref[:, lanes], vc_ref[:, lanes]], axis=0)
            sc = _attn_scores(q_stack, kcat, bias_ref, pair, first)
            m = jnp.max(sc, axis=-1, keepdims=True)
            p = jnp.exp(sc - m)
            den = jnp.sum(p, axis=-1, keepdims=True)
            o_ref[:, lanes] = _unstack_heads(_dot(p.astype(BF16), vcat) / den).astype(BF16)
            lse_ref[:, lanes] = _unstack_heads(jnp.broadcast_to(m + jnp.log(den), (2 * ATTN_BLOCK, LANES)))

    cur = lambda n: n
    prev = lambda n: jnp.maximum(n - 1, 0)
    steps = d * nb
    flat = lambda k: (lambda: pl.program_id(0) * nb + pl.program_id(1) == k)
    e_in, e_out, e_shape, e_scr, e_args = _ride_specs(ride)
    return pl.pallas_call(
        _riding(body, 5, 2, 1, ride, flat(0), flat((5 * steps) // 8), flat(steps - 1), late=flat((15 * steps) // 16)),
        name=f"attn_fwd_d{d}",
        grid=(d, nb),
        in_specs=[_sub_block(0, cur), _sub_block(1, cur), _sub_block(1, prev), _sub_block(2, cur),
                  _sub_block(2, prev)] + e_in,
        out_specs=[_sub_block(0, cur), _sub_block(0, cur)] + e_out,
        out_shape=[jax.ShapeDtypeStruct((d, length, ATTN_WIDTH), BF16),
                   jax.ShapeDtypeStruct((d, length, ATTN_WIDTH), F32)] + e_shape,
        scratch_shapes=[pltpu.VMEM((2, ATTN_HEADS, ATTN_BLOCK, 2 * ATTN_BLOCK), F32)] + e_scr,
        compiler_params=_params(ride, dimension_semantics=("arbitrary", "arbitrary")),
    )(qkv, qkv, qkv, qkv, qkv, *e_args)


def attn_bwd(qkv, d_out, lse, delta, dilation, ride=None):
    d, length, _ = qkv.shape
    assert d == dilation
    nb = length // ATTN_BLOCK

    steps = d * nb + 1

    def body(q_ref, kc_ref, kp_ref, vc_ref, vp_ref, do_ref, lse_ref, dl_ref, dq_ref, dk_ref, dv_ref, ck_ref, cv_ref,
             bias_ref):
        t = pl.program_id(0)

        @pl.when(t == 0)
        def _():
            ck_ref[...] = jnp.zeros_like(ck_ref)
            cv_ref[...] = jnp.zeros_like(cv_ref)
            _fill_attn_bias(bias_ref, d)

        @pl.when(t < steps - 1)
        def _():
            first = t % nb == 0
            for pair in range(ATTN_HEADS // 2):
                lanes = slice(pair * LANES, (pair + 1) * LANES)
                q_stack = _stack_heads(q_ref[:, lanes] * ATTN_SCALE)
                do_stack = _stack_heads(do_ref[:, lanes])
                kcat = jnp.concatenate([kp_ref[:, lanes], kc_ref[:, lanes]], axis=0)
                vcat = jnp.concatenate([vp_ref[:, lanes], vc_ref[:, lanes]], axis=0)
                col_a, col_b = 2 * pair, 2 * pair + 1
                lse_col = jnp.concatenate([lse_ref[:, col_a:col_a + 1], lse_ref[:, col_b:col_b + 1]], axis=0)
                dl_col = jnp.concatenate([dl_ref[:, col_a:col_a + 1], dl_ref[:, col_b:col_b + 1]], axis=0)
                p = jnp.exp(_attn_scores(q_stack, kcat, bias_ref, pair, first) - lse_col)
                ds = (p * (_dot_nt(do_stack, vcat) - dl_col)).astype(BF16)
                dq_ref[:, lanes] = (_unstack_heads(_dot(ds, kcat)) * ATTN_SCALE).astype(BF16)
                dk_cat = _dot_tn(ds, q_stack)
                dv_cat = _dot_tn(p.astype(BF16), do_stack)
                dk_ref[:, lanes] = (ck_ref[:, lanes] + dk_cat[:ATTN_BLOCK]).astype(BF16)
                dv_ref[:, lanes] = (cv_ref[:, lanes] + dv_cat[:ATTN_BLOCK]).astype(BF16)
                ck_ref[:, lanes] = dk_cat[ATTN_BLOCK:]
                cv_ref[:, lanes] = dv_cat[ATTN_BLOCK:]

        @pl.when(t == steps - 1)
        def _():
            dk_ref[...] = ck_ref[...].astype(BF16)
            dv_ref[...] = cv_ref[...].astype(BF16)

    blk = (ATTN_BLOCK, ATTN_WIDTH)

    def spec(col, shift, width=ATTN_WIDTH):
        def index(t):
            f = jnp.minimum(t, steps - 2) if shift > -2 else jnp.maximum(t - 1, 0)
            r, n = f // nb, f % nb
            return (r, jnp.maximum(n - 1, 0) if shift == -1 else n, col)
        return pl.BlockSpec((None, ATTN_BLOCK, width), index)

    step = lambda k: (lambda: pl.program_id(0) == k)
    e_in, e_out, e_shape, e_scr, e_args = _ride_specs(ride)
    return pl.pallas_call(
        _riding(body, 8, 3, 3, ride, step(0), step(steps // 2), step(steps - 1)),
        name=f"attn_bwd_d{d}",
        grid=(steps,),
        in_specs=[spec(0, 0), spec(1, 0), spec(1, -1), spec(2, 0), spec(2, -1), spec(0, 0), spec(0, 0, LANES),
                  spec(0, 0, LANES)] + e_in,
        out_specs=[spec(0, 0), spec(0, -2), spec(0, -2)] + e_out,
        out_shape=[jax.ShapeDtypeStruct((d, length, ATTN_WIDTH), BF16)] * 3 + e_shape,
        scratch_shapes=[pltpu.VMEM(blk, F32), pltpu.VMEM(blk, F32),
                        pltpu.VMEM((2, ATTN_HEADS, ATTN_BLOCK, 2 * ATTN_BLOCK), F32)] + e_scr,
        compiler_params=_params(ride, dimension_semantics=("arbitrary",)),
    )(qkv, qkv, qkv, qkv, qkv, d_out, lse, delta, *e_args)


def _lower_bound(logits):
    return _sigmoid(logits[0:1, :] - logits[1:2, :])


def _hgrn_gates(q, fp, lb):
    sq = _sigmoid(q)
    qf = q * sq
    sig = _sigmoid(fp)
    sig_neg = _sigmoid(-fp)
    kf = (1.0 - lb) * sig_neg
    log_sig = jnp.minimum(fp, 0.0) - jnp.log(1.0 + jnp.exp(-jnp.abs(fp)))
    a = jnp.log(lb)
    c = jnp.log(1.0 - lb) + log_sig
    log_f = jnp.maximum(a, c) + jnp.log(1.0 + jnp.exp(-jnp.abs(a - c)))
    return sq, qf, (sig, sig_neg, c), log_f, kf


def _tril_bf16(n, upper=False):
    r = lax.broadcasted_iota(jnp.int32, (n, n), 0)
    c = lax.broadcasted_iota(jnp.int32, (n, n), 1)
    keep = (c >= r) if upper else (c <= r)
    return jnp.where(keep, 1.0, 0.0).astype(BF16)


def _hgrn_diagonal_loops(c_len, diagonal):
    for half in range(SUB_BLOCK // SUBLANES):
        def step(jj, carry, half=half):
            j = half * SUBLANES + jj
            for i in range(c_len // SUB_BLOCK):
                diagonal(slice(i * SUB_BLOCK + half * SUBLANES, (i + 1) * SUB_BLOCK), j, i * SUB_BLOCK + j)
            return carry

        lax.fori_loop(0, SUBLANES, step, 0, unroll=COLUMN_UNROLL)


def _hgrn_off_diagonal(b, qf, kf):
    c_len, width = b.shape
    edges = [b[0:1, :]] + [b[i * SUB_BLOCK - 1:i * SUB_BLOCK, :] for i in range(1, c_len // SUB_BLOCK)]
    eq = jnp.exp(b - jnp.concatenate([jnp.broadcast_to(e, (SUB_BLOCK, width)) for e in edges], axis=0))
    q_til = qf * eq
    k_til, ek = [], []
    for i in range(1, c_len // SUB_BLOCK):
        n = i * SUB_BLOCK
        e = jnp.exp(edges[i] - b[:n, :])
        ek.append(e)
        k_til.append(jnp.concatenate([kf[:n, :] * e, jnp.zeros((2 * c_len - n, width), F32)], axis=0))
    return q_til, k_til, eq, ek


def _split2(x):
    hi = x.astype(BF16)
    return hi, (x - hi.astype(F32)).astype(BF16)


def hgrn_fwd(proj, lb, ride=None):
    s = proj.shape[0]
    c_len, nh, hd = HGRN_CHUNK, HGRN_HEADS, HGRN_HEAD_DIM
    n_chunks = s // c_len
    col0 = 0

    cps = 2 * HGRN_CHUNKS_PER_STEP
    n_steps = n_chunks // cps

    def body(q_ref, f_ref, i_ref, lb_ref, o_ref, st_out_ref, a_out_ref, st_ref, b_ref, qf_ref, kf_ref, a_ref):
        @pl.when(pl.program_id(0) == 0)
        def _():
            st_ref[...] = jnp.zeros_like(st_ref)

        lbv = _lower_bound(lb_ref[...])
        for u in range(cps):
            rs = slice(u * c_len, (u + 1) * c_len)
            b_u, qf_u, kf_u, a_u = b_ref.at[u], qf_ref.at[u], kf_ref.at[u], a_ref.at[u]
            _, qf, _, log_f, kf = _hgrn_gates(q_ref[rs, :], f_ref[rs, :], lbv)
            b = _tri_sum(_tril_bf16(c_len), log_f)
            b_u[...] = b
            qf_u[...] = qf
            kf_u[...] = kf
            a_u[...] = jnp.zeros_like(a_u)

            def diagonal(rows, j, key, b_u=b_u, qf_u=qf_u, kf_u=kf_u, a_u=a_u):
                bj = b_u[pl.ds(key, 1), :]
                kj = kf_u[pl.ds(key, 1), :]
                nrow = rows.stop - rows.start
                t_loc = lax.broadcasted_iota(jnp.int32, (nrow, nh * hd), 0) + (rows.start % SUB_BLOCK)
                e = jnp.exp(jnp.where(t_loc >= j, b_u[rows, :] - bj, NEG_BIG))
                prod = qf_u[rows, :] * kj * e
                lane = lax.broadcasted_iota(jnp.int32, (nrow, hd), 1)
                for h in range(nh):
                    col = jnp.sum(prod[:, h * hd:(h + 1) * hd], axis=-1, keepdims=True)
                    a_u[h, rows, :] = jnp.where(lane == key, col, a_u[h, rows, :])

            _hgrn_diagonal_loops(c_len, diagonal)
            q_til, k_til, _, _ = _hgrn_off_diagonal(b, qf, kf)
            q_til = q_til.astype(BF16)
            k_til = [k.astype(BF16) for k in k_til]

            b_last = b[c_len - 1:c_len, :]
            qb = (qf * jnp.exp(b)).astype(BF16)
            kb2 = (kf * jnp.exp(b_last - b)).astype(BF16)
            vf = i_ref[rs, :].astype(BF16)
            for h in range(nh):
                hs = slice(h * hd, (h + 1) * hd)
                st = st_ref[h]
                st_out_ref[u, h] = st
                off = [jnp.zeros((SUB_BLOCK, hd), F32)]
                for i in range(1, c_len // SUB_BLOCK):
                    off.append(_dot_nt(q_til[i * SUB_BLOCK:(i + 1) * SUB_BLOCK, hs], k_til[i - 1][:, hs]))
                a_h = a_u[h] + jnp.concatenate(off, axis=0)
                a_out_ref[rs, hs] = a_h
                o_ref[rs, hs] = _dot_nt(qb[:, hs], st.astype(BF16)) + _dot(a_h[:, :c_len].astype(BF16), vf[:, hs])
                st_ref[h] = st * jnp.exp(b_last[:, hs]) + _dot_tn(vf[:, hs], kb2[:, hs])

    blk = (cps * c_len, HGRN_WIDTH)
    sblk = (cps, c_len, HGRN_WIDTH)
    step = lambda k: (lambda: pl.program_id(0) == k)
    e_in, e_out, e_shape, e_scr, e_args = _ride_specs(ride)
    return pl.pallas_call(
        _riding(body, 4, 3, 5, ride, step(0), step(n_steps // 2), step(n_steps - 1), late=step((3 * n_steps) // 4)),
        name="hgrn_fwd",
        grid=(n_steps,),
        in_specs=[
            pl.BlockSpec(blk, lambda c: (c, col0)),
            pl.BlockSpec(blk, lambda c: (c, col0 + 1)),
            pl.BlockSpec(blk, lambda c: (c, col0 + 2)),
            pl.BlockSpec((2, HGRN_WIDTH), lambda c: (0, 0)),
        ] + e_in,
        out_specs=[
            pl.BlockSpec(blk, lambda c: (c, 0)),
            pl.BlockSpec((cps, nh, hd, hd), lambda c: (c, 0, 0, 0)),
            pl.BlockSpec(blk, lambda c: (c, 0)),
        ] + e_out,
        out_shape=[
            jax.ShapeDtypeStruct((s, HGRN_WIDTH), F32),
            jax.ShapeDtypeStruct((n_chunks, nh, hd, hd), F32),
            jax.ShapeDtypeStruct((s, nh * hd), F32),
        ] + e_shape,
        scratch_shapes=[
            pltpu.VMEM((nh, hd, hd), F32),
            pltpu.VMEM(sblk, F32),
            pltpu.VMEM(sblk, F32),
            pltpu.VMEM(sblk, F32),
            pltpu.VMEM((cps, nh, c_len, hd), F32),
        ] + e_scr,
        compiler_params=_params(ride, dimension_semantics=("arbitrary",)),
    )(proj, proj, proj, lb, *e_args)


def hgrn_bwd(proj, lb, d_o, states, a_mat, ride=None):
    s = proj.shape[0]
    c_len, nh, hd = HGRN_CHUNK, HGRN_HEADS, HGRN_HEAD_DIM
    n_chunks = s // c_len
    col0 = 0
    cps = HGRN_CHUNKS_PER_STEP
    n_steps = n_chunks // cps
    last = n_steps - 1

    def body(q_ref, f_ref, i_ref, lb_ref, do_ref, st_in_ref, a_in_ref, dq_ref, df_ref, di_ref, dlb_ref,
             dst_ref, b_ref, qf_ref, kf_ref, da_ref, dqi_ref, dki_ref):
        @pl.when(pl.program_id(0) == 0)
        def _():
            dst_ref[...] = jnp.zeros_like(dst_ref)
            dlb_ref[...] = jnp.zeros_like(dlb_ref)

        lbv = _lower_bound(lb_ref[...])
        for u in reversed(range(cps)):
            rs = slice(u * c_len, (u + 1) * c_len)
            b_u, qf_u, kf_u, da_u, dqi_u, dki_u = (b_ref.at[u], qf_ref.at[u], kf_ref.at[u], da_ref.at[u], dqi_ref.at[u],
                                                   dki_ref.at[u])
            q = q_ref[rs, :]
            sq, qf, (sig, sig_neg, log_c), log_f, kf = _hgrn_gates(q, f_ref[rs, :], lbv)
            b = _tri_sum(_tril_bf16(c_len), log_f)
            b_u[...] = b
            qf_u[...] = qf
            kf_u[...] = kf
            b_last = b[c_len - 1:c_len, :]
            eb = jnp.exp(b)
            ebl = jnp.exp(b_last - b)
            qb = qf * eb
            kb2 = kf * ebl
            vf = i_ref[rs, :]
            d_o = do_ref[rs, :]
            qb_b, kb2_b, vf_b, do_b = qb.astype(BF16), kb2.astype(BF16), vf.astype(BF16), d_o.astype(BF16)
            tq = lax.broadcasted_iota(jnp.int32, (c_len, hd), 0)
            lane = lax.broadcasted_iota(jnp.int32, (c_len, hd), 1)

            dqb_parts, dvf_parts, dkb2_parts, dbl_parts = [], [], [], []
            for h in range(nh):
                hs = slice(h * hd, (h + 1) * hd)
                st = st_in_ref[u, h]
                dst = dst_ref[h]
                st_b, dst_b = st.astype(BF16), dst.astype(BF16)
                a_h = a_in_ref[rs, hs][:, :c_len].astype(BF16)
                dqb_parts.append(_dot(do_b[:, hs], st_b))
                dvf_parts.append(_dot_tn(a_h, do_b[:, hs]) + _dot_nt(kb2_b[:, hs], dst_b))
                dkb2_parts.append(_dot(vf_b[:, hs], dst_b))
                da = _dot_nt(do_b[:, hs], vf_b[:, hs])
                da = jnp.concatenate([da, jnp.zeros((c_len, hd - c_len), F32)], axis=1)
                da_u[h] = jnp.where(tq >= lane, da, 0.0)
                dbl_parts.append(jnp.sum(dst * st, axis=0, keepdims=True) * jnp.exp(b_last[:, hs]))
                dst_ref[h] = dst * jnp.exp(b_last[:, hs]) + _dot_tn(do_b[:, hs], qb_b[:, hs])
            dqb = jnp.concatenate(dqb_parts, axis=1)
            dvf = jnp.concatenate(dvf_parts, axis=1)
            dkb2 = jnp.concatenate(dkb2_parts, axis=1)
            dbl = jnp.concatenate(dbl_parts, axis=1) + jnp.sum(dkb2 * kb2, axis=0, keepdims=True)

            dqi_u[...] = jnp.zeros_like(dqi_u)
            t_idx = lax.broadcasted_iota(jnp.int32, (c_len, nh * hd), 0)

            def diagonal(rows, j, key, b_u=b_u, qf_u=qf_u, kf_u=kf_u, da_u=da_u, dqi_u=dqi_u, dki_u=dki_u):
                bj = b_u[pl.ds(key, 1), :]
                kj = kf_u[pl.ds(key, 1), :]
                nrow = rows.stop - rows.start
                t_loc = lax.broadcasted_iota(jnp.int32, (nrow, nh * hd), 0) + (rows.start % SUB_BLOCK)
                e = jnp.exp(jnp.where(t_loc >= j, b_u[rows, :] - bj, NEG_BIG))
                lane_r = lax.broadcasted_iota(jnp.int32, (nrow, hd), 1)
                cols = [jnp.sum(jnp.where(lane_r == key, da_u[h, rows, :], 0.0), axis=-1, keepdims=True)
                        for h in range(nh)]
                w = e * jnp.concatenate([jnp.broadcast_to(cc, (nrow, hd)) for cc in cols], axis=1)
                dqi_u[rows, :] += w * kj
                dki_u[pl.ds(key, 1), :] = jnp.sum(w * qf_u[rows, :], axis=0, keepdims=True)

            _hgrn_diagonal_loops(c_len, diagonal)

            q_til, k_til, eq, ek = _hgrn_off_diagonal(b, qf, kf)
            q_hi, q_lo = _split2(q_til)
            k_pairs = [_split2(k) for k in k_til]
            n_sub = c_len // SUB_BLOCK
            dq_heads, dk_heads = [], []
            for h in range(nh):
                hs = slice(h * hd, (h + 1) * hd)
                dq_rows = [jnp.zeros((SUB_BLOCK, hd), F32)]
                dk_h = jnp.zeros((c_len, hd), F32)
                for i in range(1, n_sub):
                    rows = slice(i * SUB_BLOCK, (i + 1) * SUB_BLOCK)
                    n = i * SUB_BLOCK
                    da_i = da_u[h, rows, :].astype(BF16)
                    k_hi, k_lo = k_pairs[i - 1]
                    dq_rows.append((_dot(da_i, k_hi[:, hs]) + _dot(da_i, k_lo[:, hs])) * eq[rows, hs])
                    dk_t = (_dot_tn(da_i, q_hi[rows, hs]) + _dot_tn(da_i, q_lo[rows, hs]))[:n, :] * ek[i - 1][:, hs]
                    dk_h = dk_h + jnp.concatenate([dk_t, jnp.zeros((c_len - n, hd), F32)], axis=0)
                dq_heads.append(jnp.concatenate(dq_rows, axis=0))
                dk_heads.append(dk_h)
            dq_intra = dqi_u[...] + jnp.concatenate(dq_heads, axis=1)
            dk_intra = dki_u[...] + jnp.concatenate(dk_heads, axis=1)

            db = dqb * qb + qf * dq_intra - kf * dk_intra - dkb2 * kb2
            db = db + jnp.where(t_idx == c_len - 1, dbl, 0.0)
            dg = _tri_sum(_tril_bf16(c_len, upper=True), db)
            dqf = dqb * eb + dq_intra
            dkf = dkb2 * ebl + dk_intra
            dq_ref[rs, :] = (dqf * (sq * (1.0 + q * (1.0 - sq)))).astype(BF16)
            df_ref[rs, :] = (sig_neg * (dg * jnp.exp(log_c - log_f) - dkf * (1.0 - lbv) * sig)).astype(BF16)
            di_ref[rs, :] = dvf.astype(BF16)
            dlb_ref[...] += jnp.sum(sig_neg * (dg * jnp.exp(-log_f) - dkf), axis=0, keepdims=True)

    blk = (cps * c_len, HGRN_WIDTH)
    sblk = (cps, c_len, HGRN_WIDTH)
    rev = lambda c: last - c
    step = lambda k: (lambda: pl.program_id(0) == k)
    e_in, e_out, e_shape, e_scr, e_args = _ride_specs(ride)
    return pl.pallas_call(
        _riding(body, 7, 4, 7, ride, step(0), step(n_steps // 2), step(last)),
        name="hgrn_bwd",
        grid=(n_steps,),
        in_specs=[
            pl.BlockSpec(blk, lambda c: (rev(c), col0)),
            pl.BlockSpec(blk, lambda c: (rev(c), col0 + 1)),
            pl.BlockSpec(blk, lambda c: (rev(c), col0 + 2)),
            pl.BlockSpec((2, HGRN_WIDTH), lambda c: (0, 0)),
            pl.BlockSpec(blk, lambda c: (rev(c), 0)),
            pl.BlockSpec((cps, nh, hd, hd), lambda c: (rev(c), 0, 0, 0)),
            pl.BlockSpec(blk, lambda c: (rev(c), 0)),
        ] + e_in,
        out_specs=[
            pl.BlockSpec(blk, lambda c: (rev(c), 0)),
            pl.BlockSpec(blk, lambda c: (rev(c), 0)),
            pl.BlockSpec(blk, lambda c: (rev(c), 0)),
            pl.BlockSpec((1, HGRN_WIDTH), lambda c: (0, 0)),
        ] + e_out,
        out_shape=[jax.ShapeDtypeStruct((s, HGRN_WIDTH), BF16)] * 3 + [jax.ShapeDtypeStruct((1, HGRN_WIDTH), F32)] + e_shape,
        scratch_shapes=[
            pltpu.VMEM((nh, hd, hd), F32),
            pltpu.VMEM(sblk, F32),
            pltpu.VMEM(sblk, F32),
            pltpu.VMEM(sblk, F32),
            pltpu.VMEM((cps, nh, c_len, hd), F32),
            pltpu.VMEM(sblk, F32),
            pltpu.VMEM(sblk, F32),
        ] + e_scr,
        compiler_params=_params(ride, dimension_semantics=("arbitrary",)),
    )(proj, proj, proj, lb, d_o, states, a_mat, *e_args)


def _per_head_lanes(x):
    lane = lax.broadcasted_iota(jnp.int32, (x.shape[0], LANES), 1)
    out = jnp.zeros((x.shape[0], LANES), F32)
    for h in range(ATTN_HEADS):
        out = jnp.where(lane == h, x[:, h * ATTN_HEAD_DIM:h * ATTN_HEAD_DIM + 1], out)
    return out


def _row_spec(tm, width, col=0):
    return pl.BlockSpec((tm, width), lambda i: (i, col))


def _const_spec(width):
    return pl.BlockSpec((1, width), lambda i: (0, 0))


def _acc_rows(ref, value):
    @pl.when(pl.program_id(0) == 0)
    def _():
        ref[...] = jnp.zeros_like(ref)

    ref[...] += jnp.sum(value, axis=0, keepdims=True)


def mix_fwd(attn_parts, o_h, proj, an, hn, w_out_b, gp, x, ride=None):
    s = x.shape[0]
    tm = TOKEN_TILE
    gate_col = 3
    hd = HGRN_HEAD_DIM
    nd = len(DILATIONS)

    def body(*refs):
        o_refs, l_refs = refs[:nd], refs[nd:2 * nd]
        oh_ref, gate_ref, an_ref, hn_ref, w_ref, gp_ref, x_ref = refs[2 * nd:2 * nd + 7]
        x1_ref, cat_ref, mixed_ref, attn_ref = refs[2 * nd + 7:2 * nd + 11]
        lse_refs = refs[2 * nd + 11:3 * nd + 11]
        o_scr, l_scr, lse_scr = refs[3 * nd + 11:]
        os_ = [_from_dilated(r, o_scr.at[k], d, tm) for k, (r, d) in enumerate(zip(o_refs, DILATIONS))]
        ls = [_from_dilated(r, l_scr.at[k], d, tm) for k, (r, d) in enumerate(zip(l_refs, DILATIONS))]
        m = jnp.maximum(jnp.maximum(ls[0], ls[1]), ls[2])
        es = [jnp.exp(l - m) for l in ls]
        den = es[0] + es[1] + es[2]
        attn = (es[0] * os_[0] + es[1] * os_[1] + es[2] * os_[2]) / den
        attn_ref[...] = attn
        lse_scr[0] = _per_head_lanes(m + jnp.log(den))
        for d, ref in zip(DILATIONS, lse_refs):
            _to_dilated(lse_scr, ref, d, tm)
        cat_ref[:, :ATTN_WIDTH] = _rms_fwd(attn, an_ref[...], ATTN_WIDTH).astype(BF16)
        gate = gate_ref[...]
        silu_g = gate * _sigmoid(gate)
        for h in range(HGRN_HEADS):
            hs = slice(h * hd, (h + 1) * hd)
            rec = _rms_fwd(oh_ref[:, hs], hn_ref[:, hs], hd) * silu_g[:, hs]
            cat_ref[:, ATTN_WIDTH + h * hd:ATTN_WIDTH + (h + 1) * hd] = rec.astype(BF16)
        mixed = _dot(cat_ref[...], w_ref[...])
        mixed_ref[...] = mixed
        x1_ref[...] = x_ref[...] + _rms_fwd(mixed, gp_ref[...], D_MODEL)

    aw = ATTN_WIDTH
    n_steps = s // tm
    step = lambda k: (lambda: pl.program_id(0) == k)
    e_in, e_out, e_shape, e_scr, e_args = _ride_specs(ride)
    return pl.pallas_call(
        _riding(body, 2 * nd + 7, 4 + nd, 3, ride, step(0), step((13 * n_steps) // 16), step(n_steps - 1)),
        name="mix_fwd",
        grid=(n_steps,),
        in_specs=[_dilated_spec(d, tm, aw) for d in DILATIONS] * 2 + [
            _row_spec(tm, aw), _row_spec(tm, aw, gate_col), _const_spec(aw), _const_spec(aw), _vmem_spec(),
            _const_spec(D_MODEL), _row_spec(tm, D_MODEL)] + e_in,
        out_specs=[_row_spec(tm, D_MODEL), _row_spec(tm, D_MODEL), _row_spec(tm, D_MODEL), _row_spec(tm, aw)] + [
            _dilated_spec(d, tm, LANES) for d in DILATIONS] + e_out,
        out_shape=[
            jax.ShapeDtypeStruct((s, D_MODEL), F32),
            jax.ShapeDtypeStruct((s, D_MODEL), BF16),
            jax.ShapeDtypeStruct((s, D_MODEL), F32),
            jax.ShapeDtypeStruct((s, aw), F32),
        ] + [jax.ShapeDtypeStruct((d, s // d, LANES), F32) for d in DILATIONS] + e_shape,
        scratch_shapes=[pltpu.VMEM((nd, aw // LANES, tm, LANES), F32), pltpu.VMEM((nd, aw // LANES, tm, LANES), F32),
                        pltpu.VMEM((1, tm, LANES), F32)] + e_scr,
        compiler_params=_params(ride, dimension_semantics=("arbitrary",)),
    )(*[p[0] for p in attn_parts], *[p[1] for p in attn_parts], o_h, proj, an, hn, w_out_b, gp, x, *e_args)


def mix_bwd(dx1, mixed, gp, w_out_b, attn, an, o_h, proj, hn):
    s = dx1.shape[0]
    tm = TOKEN_TILE
    gate_col = 3
    hd = HGRN_HEAD_DIM
    aw = ATTN_WIDTH

    nd = len(DILATIONS)

    def body(*refs):
        dx1_ref, mixed_ref, gp_ref, w_ref, attn_ref, an_ref, oh_ref, gate_ref, hn_ref, dmix_ref = refs[:10]
        do_refs, delta_refs = refs[10:10 + nd], refs[10 + nd:10 + 2 * nd]
        doh_ref, dgate_ref, dgp_ref, dan_ref, dhn_ref, do_ref, delta_ref = refs[10 + 2 * nd:]
        dmixed, gp_c = _rms_bwd(dx1_ref[...], mixed_ref[...], gp_ref[...], D_MODEL)
        _acc_rows(dgp_ref, gp_c)
        dmixed_b = dmixed.astype(BF16)
        dmix_ref[...] = dmixed_b
        dcat = _dot_nt(dmixed_b, w_ref[...])
        attn = attn_ref[...]
        d_o, an_c = _rms_bwd(dcat[:, :aw], attn, an_ref[...], aw)
        _acc_rows(dan_ref, an_c)
        _lane_blocks(do_ref, d_o)
        prod = d_o * attn
        lane = lax.broadcasted_iota(jnp.int32, (tm, LANES), 1)
        delta = jnp.zeros((tm, LANES), F32)
        for pair in range(ATTN_HEADS // 2):
            pp = prod[:, pair * LANES:(pair + 1) * LANES]
            low = _lane_half((tm, LANES), 0)
            lo = jnp.sum(jnp.where(low, pp, 0.0), axis=-1, keepdims=True)
            hi = jnp.sum(jnp.where(low, 0.0, pp), axis=-1, keepdims=True)
            delta = jnp.where(lane == 2 * pair, lo, jnp.where(lane == 2 * pair + 1, hi, delta))
        delta_ref[0] = delta
        for d, o_ref, l_ref in zip(DILATIONS, do_refs, delta_refs):
            _to_dilated(do_ref, o_ref, d, tm, cast=BF16)
            _to_dilated(delta_ref, l_ref, d, tm)
        gate = gate_ref[...]
        sg = _sigmoid(gate)
        silu_g = gate * sg
        drec = dcat[:, aw:]
        hn_parts = []
        for h in range(HGRN_HEADS):
            hs = slice(h * hd, (h + 1) * hd)
            oh = oh_ref[:, hs]
            on = _rms_fwd(oh, hn_ref[:, hs], hd)
            dgate_ref[:, hs] = (drec[:, hs] * on * (sg[:, hs] * (1.0 + gate[:, hs] * (1.0 - sg[:, hs])))).astype(BF16)
            d_oh, hn_c = _rms_bwd(drec[:, hs] * silu_g[:, hs], oh, hn_ref[:, hs], hd)
            doh_ref[:, hs] = d_oh
            hn_parts.append(hn_c)
        _acc_rows(dhn_ref, jnp.concatenate(hn_parts, axis=1))

    return pl.pallas_call(
        body,
        name="mix_bwd",
        grid=(s // tm,),
        in_specs=[_row_spec(tm, D_MODEL), _row_spec(tm, D_MODEL), _const_spec(D_MODEL), _vmem_spec(), _row_spec(tm, aw),
                  _const_spec(aw), _row_spec(tm, aw), _row_spec(tm, aw, gate_col), _const_spec(aw)],
        out_specs=[_row_spec(tm, D_MODEL)] + [_dilated_spec(d, tm, aw) for d in DILATIONS] + [
            _dilated_spec(d, tm, LANES) for d in DILATIONS] + [_row_spec(tm, aw)] * 2 + [
            _const_spec(D_MODEL), _const_spec(aw), _const_spec(aw)],
        out_shape=[jax.ShapeDtypeStruct((s, D_MODEL), BF16)] + [
            jax.ShapeDtypeStruct((d, s // d, aw), BF16) for d in DILATIONS] + [
            jax.ShapeDtypeStruct((d, s // d, LANES), F32) for d in DILATIONS] + [
            jax.ShapeDtypeStruct((s, aw), F32), jax.ShapeDtypeStruct((s, aw), BF16),
            jax.ShapeDtypeStruct((1, D_MODEL), F32), jax.ShapeDtypeStruct((1, aw), F32),
            jax.ShapeDtypeStruct((1, aw), F32)],
        scratch_shapes=[pltpu.VMEM((aw // LANES, tm, LANES), F32), pltpu.VMEM((1, tm, LANES), F32)],
        compiler_params=_params(dimension_semantics=("arbitrary",)),
    )(dx1, mixed, gp, w_out_b, attn, an, o_h, proj, hn)


def mlp_fwd_bwd(x1, g_pre, w1_blocks, w2_b, g_post, target):
    s = x1.shape[0]
    tm = MLP_TILE
    nblk, _, fb = w1_blocks.shape

    def body(x1_ref, gpre_ref, w1_ref, w2_ref, gpost_ref, t_ref,
             dx1_ref, h2_ref, a_ref, du_ref, dff_ref, loss_ref, dgpre_ref, dgpost_ref, u_ref):
        x1v = x1_ref[...]
        h2 = _rms_fwd(x1v, gpre_ref[...], D_MODEL).astype(BF16)
        h2_ref[...] = h2
        ff = jnp.zeros((tm, D_MODEL), F32)
        for j in range(nblk):
            cols = slice(j * fb, (j + 1) * fb)
            ru = jnp.maximum(_dot(h2, w1_ref[j]), 0.0)
            u_ref[:, cols] = ru.astype(BF16)
            a = (ru * ru).astype(BF16)
            a_ref[:, cols] = a
            ff = ff + _dot(a, w2_ref[cols, :])
        diff = x1v + _rms_fwd(ff, gpost_ref[...], D_MODEL) - t_ref[...]
        _acc_rows(loss_ref, diff * diff)
        dy = diff * (1.0 / D_MODEL)
        dff, gpost_c = _rms_bwd(dy, ff, gpost_ref[...], D_MODEL)
        _acc_rows(dgpost_ref, gpost_c)
        dff_b = dff.astype(BF16)
        dff_ref[...] = dff_b
        dh2 = jnp.zeros((tm, D_MODEL), F32)
        for j in range(nblk):
            cols = slice(j * fb, (j + 1) * fb)
            du = (_dot_nt(dff_b, w2_ref[cols, :]) * (2.0 * u_ref[:, cols])).astype(BF16)
            du_ref[:, cols] = du
            dh2 = dh2 + _dot_nt(du, w1_ref[j])
        dxa, gpre_c = _rms_bwd(dh2, x1v, gpre_ref[...], D_MODEL)
        _acc_rows(dgpre_ref, gpre_c)
        dx1_ref[...] = dy + dxa

    dm = D_MODEL
    return pl.pallas_call(
        body,
        name="mlp_fwd_bwd",
        grid=(s // tm,),
        in_specs=[_row_spec(tm, dm), _const_spec(dm), _vmem_spec(), _vmem_spec(), _const_spec(dm), _row_spec(tm, dm)],
        out_specs=[_row_spec(tm, dm), _row_spec(tm, dm), _row_spec(tm, D_FF), _row_spec(tm, D_FF), _row_spec(tm, dm),
                   _const_spec(dm), _const_spec(dm), _const_spec(dm)],
        out_shape=[
            jax.ShapeDtypeStruct((s, dm), F32),
            jax.ShapeDtypeStruct((s, dm), BF16),
            jax.ShapeDtypeStruct((s, D_FF), BF16),
            jax.ShapeDtypeStruct((s, D_FF), BF16),
            jax.ShapeDtypeStruct((s, dm), BF16),
            jax.ShapeDtypeStruct((1, dm), F32),
            jax.ShapeDtypeStruct((1, dm), F32),
            jax.ShapeDtypeStruct((1, dm), F32),
        ],
        scratch_shapes=[pltpu.VMEM((tm, D_FF), BF16)],
        compiler_params=_params(dimension_semantics=("arbitrary",)),
    )(x1, g_pre, w1_blocks, w2_b, g_post, target)


def in_proj_bwd(attn_grads, hgrn_grads, dgate, w_in_b, x, g1, dx1):
    s = x.shape[0]
    tm = PROJ_TILE
    aw = ATTN_WIDTH
    n_attn = len(attn_grads)
    flat = [g[k] for k in range(3) for g in attn_grads] + list(hgrn_grads) + [dgate]

    def body(*refs):
        parts = refs[:len(flat)]
        w_ref, x_ref, g_ref, dx1_ref, dx_ref, dproj_ref, dg_ref, scr = refs[len(flat):]
        groups = []
        for k in range(3):
            acc = None
            for p, d in zip(parts[k * n_attn:(k + 1) * n_attn], DILATIONS):
                v = _from_dilated(p, scr, d, tm)
                acc = v if acc is None else acc + v
            groups.append(acc)
        groups += [p[...] for p in parts[3 * n_attn:]]
        dh = jnp.zeros((tm, D_MODEL), F32)
        for gi, grp in enumerate(groups):
            cols = slice(gi * aw, (gi + 1) * aw)
            gb = grp.astype(BF16)
            dproj_ref[:, cols] = gb
            dh = dh + _dot_nt(gb, w_ref[:, cols])
        dxa, g_c = _rms_bwd(dh, x_ref[...], g_ref[...], D_MODEL)
        _acc_rows(dg_ref, g_c)
        dx_ref[...] = dx1_ref[...] + dxa

    dm = D_MODEL
    return pl.pallas_call(
        body,
        name="in_proj_bwd",
        grid=(s // tm,),
        in_specs=[_dilated_spec(d, tm, aw) for d in DILATIONS] * 3 + [_row_spec(tm, aw)] * 4 + [
            _vmem_spec(), _row_spec(tm, dm), _const_spec(dm), _row_spec(tm, dm)],
        out_specs=[_row_spec(tm, dm), _row_spec(tm, IN_PROJ_WIDTH), _const_spec(dm)],
        out_shape=[jax.ShapeDtypeStruct((s, dm), F32), jax.ShapeDtypeStruct((s, IN_PROJ_WIDTH), BF16),
                   jax.ShapeDtypeStruct((1, dm), F32)],
        scratch_shapes=[pltpu.VMEM((aw // LANES, tm, LANES), F32)],
        compiler_params=_params(dimension_semantics=("arbitrary",)),
    )(*flat, w_in_b, x, g1, dx1)


def wgrad(a_b, b_b, tn, name, ts=2048, per_step=1, ride=None):
    s, k = a_b.shape
    n = b_b.shape[1]

    def body(a_ref, b_ref, o_ref):
        @pl.when(pl.program_id(1) == 0)
        def _():
            o_ref[...] = jnp.zeros_like(o_ref)

        a = a_ref[...]
        for jj in range(per_step):
            o_ref[jj] += _dot_tn(a, b_ref[:, jj * tn:(jj + 1) * tn])

    wide = tn * per_step
    gn, gs = n // wide, s // ts
    step = lambda j, i: (lambda: (pl.program_id(0) == j) & (pl.program_id(1) == i))
    e_in, e_out, e_shape, e_scr, e_args = _ride_specs(ride)
    out = pl.pallas_call(
        _riding(body, 2, 1, 0, ride, step(0, 0), step(gn // 2, 0), step(gn - 1, gs - 1)),
        name=name,
        grid=(gn, gs),
        in_specs=[pl.BlockSpec((ts, k), lambda j, i: (i, 0)), pl.BlockSpec((ts, wide), lambda j, i: (i, j))] + e_in,
        out_specs=[pl.BlockSpec((per_step, k, tn), lambda j, i: (j, 0, 0))] + e_out,
        out_shape=[jax.ShapeDtypeStruct((n // tn, k, tn), F32)] + e_shape,
        scratch_shapes=e_scr,
        compiler_params=_params(ride, dimension_semantics=("arbitrary", "arbitrary")),
    )(a_b, b_b, *e_args)
    return out[0] if ride is None else out


def train_step(x, target, g1, an, logits, hn, gp, g_pre, g_post, w, m, v):
    nd = len(DILATIONS)
    shard_b = {k: w[k].astype(BF16) for k in BIG}
    (w_in_g,) = run_exchange(gather_exchange([shard_b["w_in"]]), "gather_w_in")
    w_in_b = w_in_g.transpose(1, 0, 2).reshape(D_MODEL, IN_PROJ_WIDTH)

    proj, h_b, *qkvs = in_proj_fwd(x, g1, w_in_b)
    *first_part, w2_g = attn_fwd(qkvs[0], DILATIONS[0], ride=gather_exchange([shard_b["w_ff2"]]))
    w2_b = w2_g.reshape(D_FF, D_MODEL)
    attn_parts = [first_part] + [attn_fwd(qkv, d) for qkv, d in zip(qkvs[1:], DILATIONS[1:])]
    o_h, states, a_mat, w_out_g, w1_blocks = hgrn_fwd(
        proj, logits, ride=gather_exchange([shard_b["w_out"], shard_b["w_ff1"]]))
    w_out_b = w_out_g.reshape(D_MODEL, D_MODEL)
    x1, cat_b, mixed, attn, *lses = mix_fwd(attn_parts, o_h, proj, an, hn, w_out_b, gp, x)
    dx1, h2_b, a_b, du_b, dff_b, loss_vec, dg_pre, dg_post = mlp_fwd_bwd(x1, g_pre, w1_blocks, w2_b, g_post, target)
    dw2 = wgrad(a_b, dff_b, D_MODEL, "wgrad_ff2", ts=512)
    dw1 = wgrad(h2_b, du_b, D_FF // N_DEV, "wgrad_ff1", per_step=2)
    dmix_b, *rest = mix_bwd(dx1, mixed, gp, w_out_b, attn, an, o_h, proj, hn)
    d_os, deltas = rest[:nd], rest[nd:2 * nd]
    d_oh, dgate, dgp, dan, dhn = rest[2 * nd:]
    dwout = wgrad(cat_b, dmix_b, D_MODEL, "wgrad_out")

    early = ("w_out", "w_ff1", "w_ff2")
    early_grads = [dwout.reshape(N_DEV, D_MODEL // N_DEV, D_MODEL), dw1, dw2.reshape(N_DEV, D_FF // N_DEV, D_MODEL)]
    res = attn_bwd(qkvs[0], d_os[0], lses[0], deltas[0], DILATIONS[0], ride=to_core_exchange(early_grads))
    pairs = [pair_sum(g, s, f"pair_sum_{name}") for g, s, name in zip(early_grads, res[3:], early)]
    attn_grads = [res[:3]]
    *res, others_ff2 = attn_bwd(qkvs[1], d_os[1], lses[1], deltas[1], DILATIONS[1],
                                ride=to_chip_exchange([pairs[2][1]]))
    attn_grads.append(res)
    attn_grads.append(attn_bwd(qkvs[2], d_os[2], lses[2], deltas[2], DILATIONS[2]))
    dq_h, df_h, di_h, dlb, *others = hgrn_bwd(proj, logits, d_oh, states, a_mat,
                                              ride=to_chip_exchange([pairs[0][1], pairs[1][1]]))
    others.append(others_ff2)
    dx, dproj_b, dg1 = in_proj_bwd(attn_grads, (dq_h, df_h, di_h), dgate, w_in_b, x, g1, dx1)
    packed = _pack_small(dg1, dgp, dg_pre, dg_post, dan, dhn, dlb, loss_vec)
    dwin, small_slots = wgrad(h_b, dproj_b, 2 * IN_PROJ_WIDTH // N_DEV, "wgrad_in",
                              ride=small_exchange(packed))
    big = {name: sum_adamw(p[0], o, w[name], m[name], v[name], f"sum_adamw_{name}")
           for name, p, o in zip(early, pairs, others)}

    shard_w = IN_PROJ_WIDTH // N_DEV
    dwin_blocks = dwin.reshape(N_DEV // 2, D_MODEL, 2, shard_w).transpose(0, 2, 1, 3).reshape(N_DEV, D_MODEL, shard_w)
    pair_in, others_in = reduce_last(dwin_blocks)
    big["w_in"] = sum_adamw(pair_in, others_in, w["w_in"], m["w_in"], v["w_in"], "sum_adamw_w_in")
    return dx, big, small_slots


def _position():
    x, y, c = lax.axis_index("x"), lax.axis_index("y"), lax.axis_index("c")
    other_chips = [(1 - x, y), (x, 1 - y), (1 - x, 1 - y)]
    return x, y, c, other_chips


def _any_spec():
    return pl.BlockSpec(memory_space=pl.ANY)


class Exchange:
    def __init__(self, arrays, out_shape, sems, stages, collective_id, peers):
        self.arrays, self.out_shape, self.sems, self.stages = list(arrays), list(out_shape), list(sems), stages
        self.collective_id, self.peers = collective_id, peers

    def open(self):
        barrier = pltpu.get_barrier_semaphore()
        peers = self.peers()
        for peer in peers:
            pl.semaphore_signal(barrier, inc=1, device_id=peer, device_id_type=MESH)
        pl.semaphore_wait(barrier, len(peers))


def _siblings():
    x, y, c, _ = _position()
    return [(x, y, 1 - c)]


def _same_core_of_other_chips():
    x, y, c, chips = _position()
    return [(px, py, c) for px, py in chips]


def _gather_peers():
    x, y, c, _ = _position()
    return [(x, y, 1 - c), (1 - x, y, c), (x, 1 - y, c)]


def _all_others():
    x, y, c, _ = _position()
    return [(1 - x if rel & 4 else x, 1 - y if rel & 2 else y, 1 - c if rel & 1 else c) for rel in range(1, N_DEV)]


def gather_exchange(shards):
    n = len(shards)
    halves = [sh.shape[0] // 2 for sh in shards]

    def stages(ins, outs, sems):
        send_sems, recv_sems, local_sems = sems

        def parts():
            x, y, c, _ = _position()
            me, sibling = (x, y, c), (x, y, 1 - c)
            nbr_x, nbr_y, diag = (1 - x, y, c), (x, 1 - y, c), (1 - x, 1 - y, c)

            def slot(a, dev, rows=None):
                ref = outs[a].at[4 * dev[0] + 2 * dev[1] + dev[2]]
                return ref if rows is None else ref.at[rows]

            def copy(a, k, block, to, rows=None, src=None):
                return pltpu.make_async_remote_copy(
                    src_ref=slot(a, block, rows) if src is None else src, dst_ref=slot(a, block, rows),
                    send_sem=send_sems.at[a, k], recv_sem=recv_sems.at[a, k], device_id=to, device_id_type=MESH)

            upper = lambda a: pl.ds(0, halves[a])
            lower = lambda a: pl.ds(halves[a], halves[a])
            return me, sibling, nbr_x, nbr_y, diag, slot, copy, upper, lower

        def begin():
            me, sibling, nbr_x, nbr_y, _, slot, copy, _, _ = parts()
            for a in range(n):
                pltpu.make_async_copy(ins[a], slot(a, me), local_sems.at[a]).start()
                for k, to in enumerate((sibling, nbr_x, nbr_y)):
                    copy(a, k, me, to, src=ins[a]).start()

        def middle():
            me, sibling, nbr_x, nbr_y, _, _, copy, upper, lower = parts()
            for a in range(n):
                copy(a, 1, nbr_x, me).wait_recv()
                copy(a, 3, nbr_x, sibling).start()
                copy(a, 5, nbr_x, nbr_y, rows=lower(a)).start()
                copy(a, 2, nbr_y, me).wait_recv()
                copy(a, 4, nbr_y, sibling).start()
                copy(a, 6, nbr_y, nbr_x, rows=upper(a)).start()

        def late():
            me, sibling, _, _, diag, _, copy, upper, lower = parts()
            for a in range(n):
                copy(a, 6, diag, me, rows=upper(a)).wait_recv()
                copy(a, 5, diag, me, rows=lower(a)).wait_recv()
                copy(a, 7, diag, sibling).start()

        def end():
            me, sibling, nbr_x, nbr_y, diag, slot, copy, upper, lower = parts()
            sib = lambda dev: (dev[0], dev[1], sibling[2])
            for a in range(n):
                for k, block in ((0, sibling), (3, sib(nbr_x)), (4, sib(nbr_y)), (7, sib(diag))):
                    copy(a, k, block, me).wait_recv()
                copy(a, 0, me, sibling, src=ins[a]).wait_send()
                copy(a, 1, me, nbr_x, src=ins[a]).wait_send()
                copy(a, 2, me, nbr_y, src=ins[a]).wait_send()
                copy(a, 3, nbr_x, sibling).wait_send()
                copy(a, 4, nbr_y, sibling).wait_send()
                copy(a, 5, nbr_x, nbr_y, rows=lower(a)).wait_send()
                copy(a, 6, nbr_y, nbr_x, rows=upper(a)).wait_send()
                copy(a, 7, diag, sibling).wait_send()
                pltpu.make_async_copy(ins[a], slot(a, me), local_sems.at[a]).wait()

        return begin, (middle, late), end

    return Exchange(
        shards, [jax.ShapeDtypeStruct((N_DEV,) + sh.shape, sh.dtype) for sh in shards],
        [pltpu.SemaphoreType.DMA((n, 8)), pltpu.SemaphoreType.DMA((n, 8)), pltpu.SemaphoreType.DMA((n,))], stages,
        collective_id=0, peers=_gather_peers)


def to_core_exchange(grads):
    n = len(grads)

    def stages(ins, outs, sems):
        send_sems, recv_sems = sems

        def copies():
            x, y, c, _ = _position()
            return [pltpu.make_async_remote_copy(
                src_ref=ins[a].at[2 * q + (1 - c)], dst_ref=outs[a].at[q], send_sem=send_sems.at[a, q],
                recv_sem=recv_sems.at[a, q], device_id=(x, y, 1 - c), device_id_type=MESH)
                for a in range(n) for q in range(4)]

        def begin():
            for cp in copies():
                cp.start()

        def end():
            for cp in copies():
                cp.wait()

        return begin, None, end

    return Exchange(grads, [jax.ShapeDtypeStruct((4,) + g.shape[1:], g.dtype) for g in grads],
                    [pltpu.SemaphoreType.DMA((n, 4)), pltpu.SemaphoreType.DMA((n, 4))], stages,
                    collective_id=1, peers=_siblings)


def pair_sum(grad, from_sibling, name):
    _, r, cdim = grad.shape
    tr = min(r, ELEMENTWISE_ROWS)
    c_idx = lax.axis_index("c").astype(jnp.int32).reshape(1)

    def body(c_ref, g_ref, s_ref, o_ref, ob_ref):
        total = g_ref[...] + s_ref[...]
        o_ref[...] = total
        ob_ref[...] = total.astype(BF16)

    blk = lambda: pl.BlockSpec((1, tr, cdim), lambda q, i, cr: (q, i, 0))
    return pl.pallas_call(
        body,
        name=name,
        grid_spec=pltpu.PrefetchScalarGridSpec(
            num_scalar_prefetch=1,
            grid=(4, r // tr),
            in_specs=[pl.BlockSpec((1, tr, cdim), lambda q, i, cr: (2 * q + cr[0], i, 0)), blk()],
            out_specs=[blk(), blk()],
        ),
        out_shape=[jax.ShapeDtypeStruct((4, r, cdim), F32), jax.ShapeDtypeStruct((4, r, cdim), BF16)],
        compiler_params=_params(dimension_semantics=("arbitrary", "arbitrary")),
    )(c_idx, grad, from_sibling)


def to_chip_exchange(pairs):
    n = len(pairs)

    def stages(ins, outs, sems):
        send_sems, recv_sems = sems

        def copies():
            x, y, c, chips = _position()
            return [pltpu.make_async_remote_copy(
                src_ref=ins[a].at[2 * px + py], dst_ref=outs[a].at[j], send_sem=send_sems.at[a, j],
                recv_sem=recv_sems.at[a, j], device_id=(px, py, c), device_id_type=MESH)
                for a in range(n) for j, (px, py) in enumerate(chips)]

        def begin():
            for cp in copies():
                cp.start()

        def end():
            for cp in copies():
                cp.wait()

        return begin, None, end

    return Exchange(pairs, [jax.ShapeDtypeStruct((3,) + p.shape[1:], p.dtype) for p in pairs],
                    [pltpu.SemaphoreType.DMA((n, 3)), pltpu.SemaphoreType.DMA((n, 3))], stages,
                    collective_id=2, peers=_same_core_of_other_chips)


def run_exchange(ex, name):
    n_in, n_out = len(ex.arrays), len(ex.out_shape)

    def body(*refs):
        begin, middle, end = ex.stages(refs[:n_in], refs[n_in:n_in + n_out], refs[n_in + n_out:])
        ex.open()
        begin()
        for stage in _as_tuple(middle):
            stage()
        end()

    return pl.pallas_call(
        body,
        name=name,
        in_specs=[_any_spec()] * n_in,
        out_specs=[_any_spec()] * n_out,
        out_shape=ex.out_shape,
        scratch_shapes=ex.sems,
        compiler_params=pltpu.CompilerParams(collective_id=ex.collective_id),
    )(*ex.arrays)


def _as_tuple(stages):
    return () if stages is None else stages if isinstance(stages, tuple) else (stages,)


def _riding(body, n_in, n_out, n_scratch, ex, first, middle, last, late=None):
    if ex is None:
        return body
    r_in, r_out = len(ex.arrays), len(ex.out_shape)

    def wrapped(*refs):
        k_in, refs = refs[:n_in], refs[n_in:]
        e_in, refs = refs[:r_in], refs[r_in:]
        k_out, refs = refs[:n_out], refs[n_out:]
        e_out, refs = refs[:r_out], refs[r_out:]
        k_scr, e_sems = refs[:n_scratch], refs[n_scratch:]
        begin, mid, end = ex.stages(e_in, e_out, e_sems)

        @pl.when(first())
        def _():
            ex.open()
            begin()

        body(*k_in, *k_out, *k_scr)
        for stage, at in zip(_as_tuple(mid), (middle, late or last)):
            pl.when(at())(stage)
        pl.when(last())(end)

    return wrapped


def _ride_specs(ex):
    if ex is None:
        return [], [], [], [], []
    return [_any_spec()] * len(ex.arrays), [_any_spec()] * len(ex.out_shape), ex.out_shape, ex.sems, ex.arrays


def reduce_last(grad):
    _, r, cdim = grad.shape

    def body(g_hbm, own_ref, others_hbm, g_buf, to_sib, from_sib, send_buf, load_sem, send_sems, recv_sems):
        x, y, c, chips = _position()
        order = chips + [(x, y)]
        n_other = len(chips)
        sibling = (x, y, 1 - c)
        barrier = pltpu.get_barrier_semaphore()
        peers = [sibling] + [(px, py, c) for px, py in chips]
        for peer in peers:
            pl.semaphore_signal(barrier, inc=1, device_id=peer, device_id_type=MESH)
        pl.semaphore_wait(barrier, len(peers))

        def load(block):
            cp = pltpu.make_async_copy(g_hbm.at[block], g_buf, load_sem)
            cp.start()
            cp.wait()
            return g_buf[...]

        def to_sibling(j):
            return pltpu.make_async_remote_copy(
                src_ref=to_sib.at[j], dst_ref=from_sib.at[j], send_sem=send_sems.at[n_other + j],
                recv_sem=recv_sems.at[n_other + j], device_id=sibling, device_id_type=MESH)

        def to_chip(j):
            px, py = chips[j]
            return pltpu.make_async_remote_copy(
                src_ref=send_buf.at[j], dst_ref=others_hbm.at[j], send_sem=send_sems.at[j], recv_sem=recv_sems.at[j],
                device_id=(px, py, c), device_id_type=MESH)

        def hand_over(j):
            px, py = order[j]
            to_sib[j] = load(2 * (2 * px + py) + (1 - c)).astype(BF16)
            to_sibling(j).start()

        def pair_sum_of(j):
            px, py = order[j]
            to_sibling(j).wait_recv()
            total = load(2 * (2 * px + py) + c) + from_sib[j].astype(F32)
            if j < n_other:
                send_buf[j] = total.astype(BF16)
                to_chip(j).start()
            else:
                own_ref[0] = total

        turn = [n_other - 1] + list(range(n_other - 1)) + [n_other]
        hand_over(turn[0])
        for before, j in zip(turn, turn[1:]):
            hand_over(j)
            pair_sum_of(before)
        pair_sum_of(turn[-1])
        for j in range(len(order)):
            to_sibling(j).wait_send()
        for j in range(n_other):
            to_chip(j).wait()

    n_blocks = N_DEV // 2
    return pl.pallas_call(
        body,
        name="reduce_w_in",
        in_specs=[_any_spec()],
        out_specs=[_vmem_spec(), _any_spec()],
        out_shape=[jax.ShapeDtypeStruct((1, r, cdim), F32), jax.ShapeDtypeStruct((n_blocks - 1, r, cdim), BF16)],
        scratch_shapes=[pltpu.VMEM((r, cdim), F32), pltpu.VMEM((n_blocks, r, cdim), BF16),
                        pltpu.VMEM((n_blocks, r, cdim), BF16), pltpu.VMEM((n_blocks - 1, r, cdim), BF16),
                        pltpu.SemaphoreType.DMA(()), pltpu.SemaphoreType.DMA((2 * n_blocks - 1,)),
                        pltpu.SemaphoreType.DMA((2 * n_blocks - 1,))],
        compiler_params=pltpu.CompilerParams(collective_id=4, vmem_limit_bytes=VMEM_LIMIT),
    )(grad)


def _adamw(w, g, m, v):
    m = ADAM_B1 * m + (1.0 - ADAM_B1) * g
    v = ADAM_B2 * v + (1.0 - ADAM_B2) * (g * g)
    m_hat = m / (1.0 - ADAM_B1 ** ADAM_STEP)
    v_hat = v / (1.0 - ADAM_B2 ** ADAM_STEP)
    delta = -ADAM_LR * (m_hat / (jnp.sqrt(v_hat) + ADAM_EPS) + ADAM_WD * w)
    return delta, m, v


def sum_adamw(pairs, others, w, m, v, name):
    r, cdim = w.shape
    tr = min(r, ELEMENTWISE_ROWS // 2)
    if pairs.shape[0] == 1:
        chip_idx = jnp.zeros((1,), jnp.int32)
    else:
        chip_idx = (2 * lax.axis_index("x") + lax.axis_index("y")).astype(jnp.int32).reshape(1)

    def body(q_ref, p_ref, o_ref, w_ref, m_ref, v_ref, g_out, d_out, m_out, v_out):
        g = p_ref[0] + o_ref[0].astype(F32) + o_ref[1].astype(F32) + o_ref[2].astype(F32)
        g_out[...] = g
        d_out[...], m_out[...], v_out[...] = _adamw(w_ref[...], g, m_ref[...], v_ref[...])

    tile = lambda: pl.BlockSpec((tr, cdim), lambda i, qr: (i, 0))
    return pl.pallas_call(
        body,
        name=name,
        grid_spec=pltpu.PrefetchScalarGridSpec(
            num_scalar_prefetch=1,
            grid=(r // tr,),
            in_specs=[pl.BlockSpec((1, tr, cdim), lambda i, qr: (qr[0], i, 0)),
                      pl.BlockSpec((3, tr, cdim), lambda i, qr: (0, i, 0)), tile(), tile(), tile()],
            out_specs=[tile(), tile(), tile(), tile()],
        ),
        out_shape=[jax.ShapeDtypeStruct((r, cdim), F32)] * 4,
        compiler_params=_params(dimension_semantics=("arbitrary",)),
    )(chip_idx, pairs, others, w, m, v)


def small_exchange(packed):
    def stages(ins, outs, sems):
        send_sems, recv_sems, local_sem = sems
        (src,), (slots,) = ins, outs

        def copies():
            x, y, c, _ = _position()
            my_id = 4 * x + 2 * y + c
            sends, landings = [], []
            for rel in range(1, N_DEV):
                px = 1 - x if (rel >> 2) & 1 else x
                py = 1 - y if (rel >> 1) & 1 else y
                pc = 1 - c if rel & 1 else c
                peer = dict(send_sem=send_sems.at[rel - 1], recv_sem=recv_sems.at[rel - 1], device_id=(px, py, pc),
                            device_id_type=MESH)
                sends.append(pltpu.make_async_remote_copy(src_ref=src, dst_ref=slots.at[my_id], **peer))
                landings.append(pltpu.make_async_remote_copy(src_ref=src, dst_ref=slots.at[4 * px + 2 * py + pc], **peer))
            return pltpu.make_async_copy(src, slots.at[my_id], local_sem), sends, landings

        def begin():
            local, sends, _ = copies()
            local.start()
            for cp in sends:
                cp.start()

        def end():
            local, sends, landings = copies()
            for cp in landings:
                cp.wait_recv()
            for cp in sends:
                cp.wait_send()
            local.wait()

        return begin, None, end

    return Exchange([packed], [jax.ShapeDtypeStruct((N_DEV,) + packed.shape, packed.dtype)],
                    [pltpu.SemaphoreType.DMA((N_DEV - 1,)), pltpu.SemaphoreType.DMA((N_DEV - 1,)),
                     pltpu.SemaphoreType.DMA(())], stages, collective_id=3, peers=_all_others)


def small_adamw(slots, w, m, v):
    def body(r_ref, w_ref, m_ref, v_ref, g_out, d_out, m_out, v_out, loss_out):
        red = r_ref[0]
        for k in range(1, N_DEV):
            red = red + r_ref[k]
        wv = w_ref[...]
        lb = _lower_bound(jnp.concatenate([wv[5:6, :HGRN_WIDTH], wv[5:6, HGRN_WIDTH:]], axis=0))
        t = red[5:6, :HGRN_WIDTH] * lb * (1.0 - lb)
        row = lax.broadcasted_iota(jnp.int32, red.shape, 0)
        g = jnp.where(row == 5, jnp.concatenate([t, -t], axis=1), jnp.where(row >= 6, 0.0, red))
        g_out[...] = g
        d_out[...], m_out[...], v_out[...] = _adamw(wv, g, m_ref[...], v_ref[...])
        loss = jnp.sum(red[6:7, :], axis=-1, keepdims=True) * (0.5 / D_MODEL)
        loss_out[...] = jnp.broadcast_to(loss, loss_out.shape)

    return pl.pallas_call(
        body,
        name="small_adamw",
        in_specs=[_vmem_spec()] * 4,
        out_specs=[_vmem_spec()] * 5,
        out_shape=[jax.ShapeDtypeStruct(w.shape, F32)] * 4 + [jax.ShapeDtypeStruct((SUBLANES, LANES), F32)],
    )(slots, w, m, v)


def _pack_small(g1, gp, g_pre, g_post, an, hn, logits_or_dlb, extra=None):
    row5 = logits_or_dlb.reshape(1, -1)
    row5 = jnp.pad(row5, ((0, 0), (0, D_MODEL - row5.shape[1])))
    row6 = jnp.zeros((1, D_MODEL), F32) if extra is None else extra
    return jnp.concatenate([g1, gp, g_pre, g_post, jnp.concatenate([an, hn], axis=1), row5, row6,
                            jnp.zeros((1, D_MODEL), F32)], axis=0)


def _unpack_small(p):
    return dict(mix_pre_norm=p[0:1], mix_post_norm=p[1:2], mlp_pre_norm=p[2:3], mlp_post_norm=p[3:4],
                attn_out_norm=p[4:5, :ATTN_WIDTH], hgrn_out_norm=p[4:5, ATTN_WIDTH:],
                hgrn_lb_logits=p[5].reshape(2, HGRN_WIDTH))


BIG = ("w_in", "w_out", "w_ff1", "w_ff2")
ORDER = ("mix_pre_norm", "w_in", "attn_out_norm", "hgrn_lb_logits", "hgrn_out_norm", "w_out", "mix_post_norm",
         "mlp_pre_norm", "w_ff1", "w_ff2", "mlp_post_norm")


def kernel(x, mix_pre_norm, w_in, attn_out_norm, hgrn_lb_logits, hgrn_out_norm, w_out, mix_post_norm, mlp_pre_norm, w_ff1, w_ff2, mlp_post_norm, loss_target, m_mix_pre_norm, m_w_in, m_attn_out_norm, m_hgrn_lb_logits, m_hgrn_out_norm, m_w_out, m_mix_post_norm, m_mlp_pre_norm, m_w_ff1, m_w_ff2, m_mlp_post_norm, v_mix_pre_norm, v_w_in, v_attn_out_norm, v_hgrn_lb_logits, v_hgrn_out_norm, v_w_out, v_mix_post_norm, v_mlp_pre_norm, v_w_ff1, v_w_ff2, v_mlp_post_norm):
    w = dict(w_in=w_in[0], w_out=w_out[0], w_ff1=w_ff1[0], w_ff2=w_ff2[0])
    m = dict(w_in=m_w_in[0], w_out=m_w_out[0], w_ff1=m_w_ff1[0], w_ff2=m_w_ff2[0])
    v = dict(w_in=v_w_in[0], w_out=v_w_out[0], w_ff1=v_w_ff1[0], w_ff2=v_w_ff2[0])

    dx, big, small_slots = train_step(x[0], loss_target[0], mix_pre_norm, attn_out_norm, hgrn_lb_logits, hgrn_out_norm,
                                      mix_post_norm, mlp_pre_norm, mlp_post_norm, w, m, v)

    pack = lambda a, b, c2, d, e, f, g: _pack_small(a, b, c2, d, e, f, g)
    w_s = pack(mix_pre_norm, mix_post_norm, mlp_pre_norm, mlp_post_norm, attn_out_norm, hgrn_out_norm, hgrn_lb_logits)
    m_s = pack(m_mix_pre_norm, m_mix_post_norm, m_mlp_pre_norm, m_mlp_post_norm, m_attn_out_norm, m_hgrn_out_norm,
               m_hgrn_lb_logits)
    v_s = pack(v_mix_pre_norm, v_mix_post_norm, v_mlp_pre_norm, v_mlp_post_norm, v_attn_out_norm, v_hgrn_out_norm,
               v_hgrn_lb_logits)
    g_s, d_s, nm_s, nv_s, loss = small_adamw(small_slots, w_s, m_s, v_s)
    small_out = [_unpack_small(t) for t in (g_s, d_s, nm_s, nv_s)]

    outs = [loss[0, 0], dx[None]]
    for kind in range(4):
        for name in ORDER:
            outs.append(big[name][kind][None] if name in BIG else small_out[kind][name])
    return tuple(outs)
```

```python
import jax
import jax.numpy as jnp
from jax import lax
from jax.experimental import pallas as pl
from jax.experimental.pallas import tpu as pltpu

F32 = jnp.float32
BF16 = jnp.bfloat16

D_MODEL = 1024
ATTN_WIDTH = 512
ATTN_HEAD_DIM = 64
ATTN_HEADS = 8
ATTN_BLOCK = 128
DILATIONS = (1, 4, 16)
HGRN_WIDTH = 512
HGRN_HEADS = 4
HGRN_HEAD_DIM = 128
HGRN_CHUNK = 64
IN_PROJ_WIDTH = 3584
D_FF = 4096
RMS_EPS = 1e-6
N_DEV = 8
ADAM_LR = 0.001
ADAM_B1 = 0.9
ADAM_B2 = 0.999
ADAM_EPS = 1e-08
ADAM_WD = 0.01
ADAM_STEP = 10

SUBLANES = 8
LANES = 128
COLUMN_UNROLL = 8
HGRN_CHUNKS_PER_STEP = 2
SUB_BLOCK = 16
TOKEN_TILE = 512
ELEMENTWISE_ROWS = 1024
MLP_TILE = 256
PROJ_TILE = 512
VMEM_BYTES_V7X = 64 * 1024 * 1024
VMEM_LIMIT = VMEM_BYTES_V7X // 8 * 7
NEG_BIG = -1e30
MESH = pl.DeviceIdType.MESH


def _params(ride=None, **kw):
    if ride is not None:
        kw["collective_id"] = ride.collective_id
    return pltpu.CompilerParams(vmem_limit_bytes=VMEM_LIMIT, **kw)


def _vmem_spec():
    return pl.BlockSpec(memory_space=pltpu.VMEM)


def _dot(a, b):
    return jnp.dot(a, b, preferred_element_type=F32)


def _dot_nt(a, b):
    return lax.dot_general(a, b, (((1,), (1,)), ((), ())), preferred_element_type=F32)


def _dot_tn(a, b):
    return lax.dot_general(a, b, (((0,), (0,)), ((), ())), preferred_element_type=F32)


def _sigmoid(x):
    return 1.0 / (1.0 + jnp.exp(-x))


def _rms_fwd(x, gain, width):
    r = lax.rsqrt(jnp.sum(x * x, axis=-1, keepdims=True) * (1.0 / width) + RMS_EPS)
    return x * r * gain


def _rms_bwd(dy, x, gain, width):
    r = lax.rsqrt(jnp.sum(x * x, axis=-1, keepdims=True) * (1.0 / width) + RMS_EPS)
    xhat = x * r
    dxhat = dy * gain
    dx = r * (dxhat - xhat * (jnp.sum(dxhat * xhat, axis=-1, keepdims=True) * (1.0 / width)))
    return dx, dy * xhat


def _split3(x):
    hi = x.astype(BF16)
    r1 = x - hi.astype(F32)
    mid = r1.astype(BF16)
    lo = (r1 - mid.astype(F32)).astype(BF16)
    return hi, mid, lo


def _tri_sum(tri_bf16, x):
    hi, mid, lo = _split3(x)
    return _dot(tri_bf16, hi) + _dot(tri_bf16, mid) + _dot(tri_bf16, lo)


def _dilated_spec(d, tm, width):
    return pl.BlockSpec((d, tm // d, width), lambda i: (0, i, 0))


def _lane_blocks(ref, value):
    for c in range(ref.shape[0]):
        ref[c] = value[:, c * LANES:(c + 1) * LANES]


def _to_dilated(src_ref, dst_ref, d, tm, cast=None):
    for r in range(d):
        for c in range(src_ref.shape[0]):
            v = src_ref[c] if d == 1 else src_ref[c, pl.ds(r, tm // d, stride=d), :]
            dst_ref[r, :, c * LANES:(c + 1) * LANES] = v if cast is None else v.astype(cast)


def _from_dilated(src_ref, scratch_ref, d, tm):
    if d == 1:
        return src_ref[0].astype(F32)
    nblk = scratch_ref.shape[0]
    for r in range(d):
        for c in range(nblk):
            scratch_ref[c, pl.ds(r, tm // d, stride=d), :] = src_ref[r, :, c * LANES:(c + 1) * LANES].astype(F32)
    return jnp.concatenate([scratch_ref[c] for c in range(nblk)], axis=1)


def in_proj_fwd(x, g1, w_in_b, ride=None):
    s = x.shape[0]
    tm = PROJ_TILE
    qkv_w = 3 * ATTN_WIDTH
    hg_w = IN_PROJ_WIDTH - qkv_w

    def body(x_ref, g_ref, w_ref, hg_ref, h_ref, *rest):
        qkv_refs, qkv_scr = rest[:len(DILATIONS)], rest[len(DILATIONS)]
        h = _rms_fwd(x_ref[...], g_ref[...], D_MODEL).astype(BF16)
        h_ref[...] = h
        proj = _dot(h, w_ref[...])
        hg_ref[...] = proj[:, qkv_w:]
        _lane_blocks(qkv_scr, proj[:, :qkv_w])
        for d, ref in zip(DILATIONS, qkv_refs):
            _to_dilated(qkv_scr, ref, d, tm, cast=BF16)

    n_steps = s // tm
    step = lambda k: (lambda: pl.program_id(0) == k)
    e_in, e_out, e_shape, e_scr, e_args = _ride_specs(ride)
    return pl.pallas_call(
        _riding(body, 3, 2 + len(DILATIONS), 1, ride, step(0), step(n_steps // 2), step(n_steps - 1),
                late=step(n_steps - 2)),
        name="in_proj_fwd",
        grid=(n_steps,),
        in_specs=[
            pl.BlockSpec((tm, D_MODEL), lambda i: (i, 0)),
            pl.BlockSpec((1, D_MODEL), lambda i: (0, 0)),
            _vmem_spec(),
        ] + e_in,
        out_specs=[
            pl.BlockSpec((tm, hg_w), lambda i: (i, 0)),
            pl.BlockSpec((tm, D_MODEL), lambda i: (i, 0)),
        ] + [_dilated_spec(d, tm, qkv_w) for d in DILATIONS] + e_out,
        out_shape=[jax.ShapeDtypeStruct((s, hg_w), F32), jax.ShapeDtypeStruct((s, D_MODEL), BF16)] + [
            jax.ShapeDtypeStruct((d, s // d, qkv_w), BF16) for d in DILATIONS] + e_shape,
        scratch_shapes=[pltpu.VMEM((qkv_w // LANES, tm, LANES), F32)] + e_scr,
        compiler_params=_params(ride, dimension_semantics=("arbitrary",)),
    )(x, g1, w_in_b, *e_args)


ATTN_SCALE = ATTN_HEAD_DIM ** -0.5


def _fill_attn_bias(bias_ref, dilation):
    qi = lax.broadcasted_iota(jnp.int32, (ATTN_BLOCK, 2 * ATTN_BLOCK), 0)
    kj = lax.broadcasted_iota(jnp.int32, (ATTN_BLOCK, 2 * ATTN_BLOCK), 1)
    dist = qi + ATTN_BLOCK - kj
    valid = (dist >= 0) & (dist <= ATTN_BLOCK)
    for head in range(ATTN_HEADS):
        slope = 2.0 ** (-8.0 * (head + 1) / ATTN_HEADS)
        bias = jnp.where(valid, dist.astype(F32) * (-slope * dilation), NEG_BIG)
        bias_ref[0, head] = bias
        bias_ref[1, head] = jnp.where(kj >= ATTN_BLOCK, bias, NEG_BIG)


def _stack_heads(x):
    low = _lane_half(x.shape, 0)
    zero = jnp.zeros_like(x)
    return jnp.concatenate([jnp.where(low, x, zero), jnp.where(low, zero, x)], axis=0)


def _unstack_heads(y):
    half = y.shape[0] // 2
    return jnp.where(_lane_half((half, y.shape[1]), 0), y[:half], y[half:])


def _attn_scores(q_stack, kcat, bias_ref, pair, first_block):
    f = first_block.astype(jnp.int32)
    bias = jnp.concatenate([bias_ref[f, 2 * pair], bias_ref[f, 2 * pair + 1]], axis=0)
    return _dot_nt(q_stack, kcat) + bias


def _lane_half(shape, sub):
    lane = lax.broadcasted_iota(jnp.int32, shape, 1)
    return (lane < ATTN_HEAD_DIM) if sub == 0 else (lane >= ATTN_HEAD_DIM)


def _sub_block(col, row):
    return pl.BlockSpec((None, ATTN_BLOCK, ATTN_WIDTH), lambda r, n: (r, row(n), col))


def attn_fwd(qkv, dilation):
    d, length, _ = qkv.shape
    assert d == dilation
    nb = length // ATTN_BLOCK

    def body(q_ref, kc_ref, kp_ref, vc_ref, vp_ref, o_ref, lse_ref, bias_ref):
        @pl.when((pl.program_id(0) == 0) & (pl.program_id(1) == 0))
        def _():
            _fill_attn_bias(bias_ref, d)

        first = pl.program_id(1) == 0
        for pair in range(ATTN_HEADS // 2):
            lanes = slice(pair * LANES, (pair + 1) * LANES)
            q_stack = _stack_heads(q_ref[:, lanes] * ATTN_SCALE)
            kcat = jnp.concatenate([kp_ref[:, lanes], kc_ref[:, lanes]], axis=0)
            vcat = jnp.concatenate([vp_ref[:, lanes], vc_ref[:, lanes]], axis=0)
            sc = _attn_scores(q_stack, kcat, bias_ref, pair, first)
            m = jnp.max(sc, axis=-1, keepdims=True)
            p = jnp.exp(sc - m)
            den = jnp.sum(p, axis=-1, keepdims=True)
            o_ref[:, lanes] = _unstack_heads(_dot(p.astype(BF16), vcat) / den).astype(BF16)
            lse_ref[:, lanes] = _unstack_heads(jnp.broadcast_to(m + jnp.log(den), (2 * ATTN_BLOCK, LANES)))

    cur = lambda n: n
    prev = lambda n: jnp.maximum(n - 1, 0)
    return pl.pallas_call(
        body,
        name=f"attn_fwd_d{d}",
        grid=(d, nb),
        in_specs=[_sub_block(0, cur), _sub_block(1, cur), _sub_block(1, prev), _sub_block(2, cur), _sub_block(2, prev)],
        out_specs=[_sub_block(0, cur), _sub_block(0, cur)],
        out_shape=[jax.ShapeDtypeStruct((d, length, ATTN_WIDTH), BF16), jax.ShapeDtypeStruct((d, length, ATTN_WIDTH), F32)],
        scratch_shapes=[pltpu.VMEM((2, ATTN_HEADS, ATTN_BLOCK, 2 * ATTN_BLOCK), F32)],
        compiler_params=_params(dimension_semantics=("arbitrary", "arbitrary")),
    )(qkv, qkv, qkv, qkv, qkv)


def attn_bwd(qkv, d_out, lse, delta, dilation, ride=None):
    d, length, _ = qkv.shape
    assert d == dilation
    nb = length // ATTN_BLOCK

    steps = d * nb + 1

    def body(q_ref, kc_ref, kp_ref, vc_ref, vp_ref, do_ref, lse_ref, dl_ref, dq_ref, dk_ref, dv_ref, ck_ref, cv_ref,
             bias_ref):
        t = pl.program_id(0)

        @pl.when(t == 0)
        def _():
            ck_ref[...] = jnp.zeros_like(ck_ref)
            cv_ref[...] = jnp.zeros_like(cv_ref)
            _fill_attn_bias(bias_ref, d)

        @pl.when(t < steps - 1)
        def _():
            first = t % nb == 0
            for pair in range(ATTN_HEADS // 2):
                lanes = slice(pair * LANES, (pair + 1) * LANES)
                q_stack = _stack_heads(q_ref[:, lanes] * ATTN_SCALE)
                do_stack = _stack_heads(do_ref[:, lanes])
                kcat = jnp.concatenate([kp_ref[:, lanes], kc_ref[:, lanes]], axis=0)
                vcat = jnp.concatenate([vp_ref[:, lanes], vc_ref[:, lanes]], axis=0)
                col_a, col_b = 2 * pair, 2 * pair + 1
                lse_col = jnp.concatenate([lse_ref[:, col_a:col_a + 1], lse_ref[:, col_b:col_b + 1]], axis=0)
                dl_col = jnp.concatenate([dl_ref[:, col_a:col_a + 1], dl_ref[:, col_b:col_b + 1]], axis=0)
                p = jnp.exp(_attn_scores(q_stack, kcat, bias_ref, pair, first) - lse_col)
                ds = (p * (_dot_nt(do_stack, vcat) - dl_col)).astype(BF16)
                dq_ref[:, lanes] = (_unstack_heads(_dot(ds, kcat)) * ATTN_SCALE).astype(BF16)
                dk_cat = _dot_tn(ds, q_stack)
                dv_cat = _dot_tn(p.astype(BF16), do_stack)
                dk_ref[:, lanes] = (ck_ref[:, lanes] + dk_cat[:ATTN_BLOCK]).astype(BF16)
                dv_ref[:, lanes] = (cv_ref[:, lanes] + dv_cat[:ATTN_BLOCK]).astype(BF16)
                ck_ref[:, lanes] = dk_cat[ATTN_BLOCK:]
                cv_ref[:, lanes] = dv_cat[ATTN_BLOCK:]

        @pl.when(t == steps - 1)
        def _():
            dk_ref[...] = ck_ref[...].astype(BF16)
            dv_ref[...] = cv_ref[...].astype(BF16)

    blk = (ATTN_BLOCK, ATTN_WIDTH)

    def spec(col, shift, width=ATTN_WIDTH):
        def index(t):
            f = jnp.minimum(t, steps - 2) if shift > -2 else jnp.maximum(t - 1, 0)
            r, n = f // nb, f % nb
            return (r, jnp.maximum(n - 1, 0) if shift == -1 else n, col)
        return pl.BlockSpec((None, ATTN_BLOCK, width), index)

    step = lambda k: (lambda: pl.program_id(0) == k)
    e_in, e_out, e_shape, e_scr, e_args = _ride_specs(ride)
    return pl.pallas_call(
        _riding(body, 8, 3, 3, ride, step(0), step(steps // 2), step(steps - 1)),
        name=f"attn_bwd_d{d}",
        grid=(steps,),
        in_specs=[spec(0, 0), spec(1, 0), spec(1, -1), spec(2, 0), spec(2, -1), spec(0, 0), spec(0, 0, LANES),
                  spec(0, 0, LANES)] + e_in,
        out_specs=[spec(0, 0), spec(0, -2), spec(0, -2)] + e_out,
        out_shape=[jax.ShapeDtypeStruct((d, length, ATTN_WIDTH), BF16)] * 3 + e_shape,
        scratch_shapes=[pltpu.VMEM(blk, F32), pltpu.VMEM(blk, F32),
                        pltpu.VMEM((2, ATTN_HEADS, ATTN_BLOCK, 2 * ATTN_BLOCK), F32)] + e_scr,
        compiler_params=_params(ride, dimension_semantics=("arbitrary",)),
    )(qkv, qkv, qkv, qkv, qkv, d_out, lse, delta, *e_args)


def _lower_bound(logits):
    return _sigmoid(logits[0:1, :] - logits[1:2, :])


def _hgrn_gates(q, fp, lb):
    sq = _sigmoid(q)
    qf = q * sq
    sig = _sigmoid(fp)
    sig_neg = _sigmoid(-fp)
    kf = (1.0 - lb) * sig_neg
    log_sig = jnp.minimum(fp, 0.0) - jnp.log(1.0 + jnp.exp(-jnp.abs(fp)))
    a = jnp.log(lb)
    c = jnp.log(1.0 - lb) + log_sig
    log_f = jnp.maximum(a, c) + jnp.log(1.0 + jnp.exp(-jnp.abs(a - c)))
    return sq, qf, (sig, sig_neg, c), log_f, kf


def _tril_bf16(n, upper=False):
    r = lax.broadcasted_iota(jnp.int32, (n, n), 0)
    c = lax.broadcasted_iota(jnp.int32, (n, n), 1)
    keep = (c >= r) if upper else (c <= r)
    return jnp.where(keep, 1.0, 0.0).astype(BF16)


def _hgrn_diagonal_loops(c_len, diagonal):
    for half in range(SUB_BLOCK // SUBLANES):
        def step(jj, carry, half=half):
            j = half * SUBLANES + jj
            for i in range(c_len // SUB_BLOCK):
                diagonal(slice(i * SUB_BLOCK + half * SUBLANES, (i + 1) * SUB_BLOCK), j, i * SUB_BLOCK + j)
            return carry

        lax.fori_loop(0, SUBLANES, step, 0, unroll=COLUMN_UNROLL)


def _hgrn_off_diagonal(b, qf, kf):
    c_len, width = b.shape
    edges = [b[0:1, :]] + [b[i * SUB_BLOCK - 1:i * SUB_BLOCK, :] for i in range(1, c_len // SUB_BLOCK)]
    eq = jnp.exp(b - jnp.concatenate([jnp.broadcast_to(e, (SUB_BLOCK, width)) for e in edges], axis=0))
    q_til = qf * eq
    k_til, ek = [], []
    for i in range(1, c_len // SUB_BLOCK):
        n = i * SUB_BLOCK
        e = jnp.exp(edges[i] - b[:n, :])
        ek.append(e)
        k_til.append(jnp.concatenate([kf[:n, :] * e, jnp.zeros((2 * c_len - n, width), F32)], axis=0))
    return q_til, k_til, eq, ek


def _split2(x):
    hi = x.astype(BF16)
    return hi, (x - hi.astype(F32)).astype(BF16)


def hgrn_fwd(proj, lb, ride=None):
    s = proj.shape[0]
    c_len, nh, hd = HGRN_CHUNK, HGRN_HEADS, HGRN_HEAD_DIM
    n_chunks = s // c_len
    col0 = 0

    cps = 2 * HGRN_CHUNKS_PER_STEP
    n_steps = n_chunks // cps

    def body(q_ref, f_ref, i_ref, lb_ref, o_ref, st_out_ref, a_out_ref, st_ref, b_ref, qf_ref, kf_ref, a_ref):
        @pl.when(pl.program_id(0) == 0)
        def _():
            st_ref[...] = jnp.zeros_like(st_ref)

        lbv = _lower_bound(lb_ref[...])
        for u in range(cps):
            rs = slice(u * c_len, (u + 1) * c_len)
            b_u, qf_u, kf_u, a_u = b_ref.at[u], qf_ref.at[u], kf_ref.at[u], a_ref.at[u]
            _, qf, _, log_f, kf = _hgrn_gates(q_ref[rs, :], f_ref[rs, :], lbv)
            b = _tri_sum(_tril_bf16(c_len), log_f)
            b_u[...] = b
            qf_u[...] = qf
            kf_u[...] = kf
            a_u[...] = jnp.zeros_like(a_u)

            def diagonal(rows, j, key, b_u=b_u, qf_u=qf_u, kf_u=kf_u, a_u=a_u):
                bj = b_u[pl.ds(key, 1), :]
                kj = kf_u[pl.ds(key, 1), :]
                nrow = rows.stop - rows.start
                t_loc = lax.broadcasted_iota(jnp.int32, (nrow, nh * hd), 0) + (rows.start % SUB_BLOCK)
                e = jnp.exp(jnp.where(t_loc >= j, b_u[rows, :] - bj, NEG_BIG))
                prod = qf_u[rows, :] * kj * e
                lane = lax.broadcasted_iota(jnp.int32, (nrow, hd), 1)
                for h in range(nh):
                    col = jnp.sum(prod[:, h * hd:(h + 1) * hd], axis=-1, keepdims=True)
                    a_u[h, rows, :] = jnp.where(lane == key, col, a_u[h, rows, :])

            _hgrn_diagonal_loops(c_len, diagonal)
            q_til, k_til, _, _ = _hgrn_off_diagonal(b, qf, kf)
            q_til = q_til.astype(BF16)
            k_til = [k.astype(BF16) for k in k_til]

            b_last = b[c_len - 1:c_len, :]
            qb = (qf * jnp.exp(b)).astype(BF16)
            kb2 = (kf * jnp.exp(b_last - b)).astype(BF16)
            vf = i_ref[rs, :].astype(BF16)
            for h in range(nh):
                hs = slice(h * hd, (h + 1) * hd)
                st = st_ref[h]
                st_out_ref[u, h] = st
                off = [jnp.zeros((SUB_BLOCK, hd), F32)]
                for i in range(1, c_len // SUB_BLOCK):
                    off.append(_dot_nt(q_til[i * SUB_BLOCK:(i + 1) * SUB_BLOCK, hs], k_til[i - 1][:, hs]))
                a_h = a_u[h] + jnp.concatenate(off, axis=0)
                a_out_ref[rs, hs] = a_h
                o_ref[rs, hs] = _dot_nt(qb[:, hs], st.astype(BF16)) + _dot(a_h[:, :c_len].astype(BF16), vf[:, hs])
                st_ref[h] = st * jnp.exp(b_last[:, hs]) + _dot_tn(vf[:, hs], kb2[:, hs])

    blk = (cps * c_len, HGRN_WIDTH)
    sblk = (cps, c_len, HGRN_WIDTH)
    step = lambda k: (lambda: pl.program_id(0) == k)
    e_in, e_out, e_shape, e_scr, e_args = _ride_specs(ride)
    return pl.pallas_call(
        _riding(body, 4, 3, 5, ride, step(0), step(n_steps // 2), step(n_steps - 1), late=step((3 * n_steps) // 4)),
        name="hgrn_fwd",
        grid=(n_steps,),
        in_specs=[
            pl.BlockSpec(blk, lambda c: (c, col0)),
            pl.BlockSpec(blk, lambda c: (c, col0 + 1)),
            pl.BlockSpec(blk, lambda c: (c, col0 + 2)),
            pl.BlockSpec((2, HGRN_WIDTH), lambda c: (0, 0)),
        ] + e_in,
        out_specs=[
            pl.BlockSpec(blk, lambda c: (c, 0)),
            pl.BlockSpec((cps, nh, hd, hd), lambda c: (c, 0, 0, 0)),
            pl.BlockSpec(blk, lambda c: (c, 0)),
        ] + e_out,
        out_shape=[
            jax.ShapeDtypeStruct((s, HGRN_WIDTH), F32),
            jax.ShapeDtypeStruct((n_chunks, nh, hd, hd), F32),
            jax.ShapeDtypeStruct((s, nh * hd), F32),
        ] + e_shape,
        scratch_shapes=[
            pltpu.VMEM((nh, hd, hd), F32),
            pltpu.VMEM(sblk, F32),
            pltpu.VMEM(sblk, F32),
            pltpu.VMEM(sblk, F32),
            pltpu.VMEM((cps, nh, c_len, hd), F32),
        ] + e_scr,
        compiler_params=_params(ride, dimension_semantics=("arbitrary",)),
    )(proj, proj, proj, lb, *e_args)


def hgrn_bwd(proj, lb, d_o, states, a_mat, ride=None):
    s = proj.shape[0]
    c_len, nh, hd = HGRN_CHUNK, HGRN_HEADS, HGRN_HEAD_DIM
    n_chunks = s // c_len
    col0 = 0
    cps = HGRN_CHUNKS_PER_STEP
    n_steps = n_chunks // cps
    last = n_steps - 1

    def body(q_ref, f_ref, i_ref, lb_ref, do_ref, st_in_ref, a_in_ref, dq_ref, df_ref, di_ref, dlb_ref,
             dst_ref, b_ref, qf_ref, kf_ref, da_ref, dqi_ref, dki_ref):
        @pl.when(pl.program_id(0) == 0)
        def _():
            dst_ref[...] = jnp.zeros_like(dst_ref)
            dlb_ref[...] = jnp.zeros_like(dlb_ref)

        lbv = _lower_bound(lb_ref[...])
        for u in reversed(range(cps)):
            rs = slice(u * c_len, (u + 1) * c_len)
            b_u, qf_u, kf_u, da_u, dqi_u, dki_u = (b_ref.at[u], qf_ref.at[u], kf_ref.at[u], da_ref.at[u], dqi_ref.at[u],
                                                   dki_ref.at[u])
            q = q_ref[rs, :]
            sq, qf, (sig, sig_neg, log_c), log_f, kf = _hgrn_gates(q, f_ref[rs, :], lbv)
            b = _tri_sum(_tril_bf16(c_len), log_f)
            b_u[...] = b
            qf_u[...] = qf
            kf_u[...] = kf
            b_last = b[c_len - 1:c_len, :]
            eb = jnp.exp(b)
            ebl = jnp.exp(b_last - b)
            qb = qf * eb
            kb2 = kf * ebl
            vf = i_ref[rs, :]
            d_o = do_ref[rs, :]
            qb_b, kb2_b, vf_b, do_b = qb.astype(BF16), kb2.astype(BF16), vf.astype(BF16), d_o.astype(BF16)
            tq = lax.broadcasted_iota(jnp.int32, (c_len, hd), 0)
            lane = lax.broadcasted_iota(jnp.int32, (c_len, hd), 1)

            dqb_parts, dvf_parts, dkb2_parts, dbl_parts = [], [], [], []
            for h in range(nh):
                hs = slice(h * hd, (h + 1) * hd)
                st = st_in_ref[u, h]
                dst = dst_ref[h]
                st_b, dst_b = st.astype(BF16), dst.astype(BF16)
                a_h = a_in_ref[rs, hs][:, :c_len].astype(BF16)
                dqb_parts.append(_dot(do_b[:, hs], st_b))
                dvf_parts.append(_dot_tn(a_h, do_b[:, hs]) + _dot_nt(kb2_b[:, hs], dst_b))
                dkb2_parts.append(_dot(vf_b[:, hs], dst_b))
                da = _dot_nt(do_b[:, hs], vf_b[:, hs])
                da = jnp.concatenate([da, jnp.zeros((c_len, hd - c_len), F32)], axis=1)
                da_u[h] = jnp.where(tq >= lane, da, 0.0)
                dbl_parts.append(jnp.sum(dst * st, axis=0, keepdims=True) * jnp.exp(b_last[:, hs]))
                dst_ref[h] = dst * jnp.exp(b_last[:, hs]) + _dot_tn(do_b[:, hs], qb_b[:, hs])
            dqb = jnp.concatenate(dqb_parts, axis=1)
            dvf = jnp.concatenate(dvf_parts, axis=1)
            dkb2 = jnp.concatenate(dkb2_parts, axis=1)
            dbl = jnp.concatenate(dbl_parts, axis=1) + jnp.sum(dkb2 * kb2, axis=0, keepdims=True)

            dqi_u[...] = jnp.zeros_like(dqi_u)
            t_idx = lax.broadcasted_iota(jnp.int32, (c_len, nh * hd), 0)

            def diagonal(rows, j, key, b_u=b_u, qf_u=qf_u, kf_u=kf_u, da_u=da_u, dqi_u=dqi_u, dki_u=dki_u):
                bj = b_u[pl.ds(key, 1), :]
                kj = kf_u[pl.ds(key, 1), :]
                nrow = rows.stop - rows.start
                t_loc = lax.broadcasted_iota(jnp.int32, (nrow, nh * hd), 0) + (rows.start % SUB_BLOCK)
                e = jnp.exp(jnp.where(t_loc >= j, b_u[rows, :] - bj, NEG_BIG))
                lane_r = lax.broadcasted_iota(jnp.int32, (nrow, hd), 1)
                cols = [jnp.sum(jnp.where(lane_r == key, da_u[h, rows, :], 0.0), axis=-1, keepdims=True)
                        for h in range(nh)]
                w = e * jnp.concatenate([jnp.broadcast_to(cc, (nrow, hd)) for cc in cols], axis=1)
                dqi_u[rows, :] += w * kj
                dki_u[pl.ds(key, 1), :] = jnp.sum(w * qf_u[rows, :], axis=0, keepdims=True)

            _hgrn_diagonal_loops(c_len, diagonal)

            q_til, k_til, eq, ek = _hgrn_off_diagonal(b, qf, kf)
            q_hi, q_lo = _split2(q_til)
            k_pairs = [_split2(k) for k in k_til]
            n_sub = c_len // SUB_BLOCK
            dq_heads, dk_heads = [], []
            for h in range(nh):
                hs = slice(h * hd, (h + 1) * hd)
                dq_rows = [jnp.zeros((SUB_BLOCK, hd), F32)]
                dk_h = jnp.zeros((c_len, hd), F32)
                for i in range(1, n_sub):
                    rows = slice(i * SUB_BLOCK, (i + 1) * SUB_BLOCK)
                    n = i * SUB_BLOCK
                    da_i = da_u[h, rows, :].astype(BF16)
                    k_hi, k_lo = k_pairs[i - 1]
                    dq_rows.append((_dot(da_i, k_hi[:, hs]) + _dot(da_i, k_lo[:, hs])) * eq[rows, hs])
                    dk_t = (_dot_tn(da_i, q_hi[rows, hs]) + _dot_tn(da_i, q_lo[rows, hs]))[:n, :] * ek[i - 1][:, hs]
                    dk_h = dk_h + jnp.concatenate([dk_t, jnp.zeros((c_len - n, hd), F32)], axis=0)
                dq_heads.append(jnp.concatenate(dq_rows, axis=0))
                dk_heads.append(dk_h)
            dq_intra = dqi_u[...] + jnp.concatenate(dq_heads, axis=1)
            dk_intra = dki_u[...] + jnp.concatenate(dk_heads, axis=1)

            db = dqb * qb + qf * dq_intra - kf * dk_intra - dkb2 * kb2
            db = db + jnp.where(t_idx == c_len - 1, dbl, 0.0)
            dg = _tri_sum(_tril_bf16(c_len, upper=True), db)
            dqf = dqb * eb + dq_intra
            dkf = dkb2 * ebl + dk_intra
            dq_ref[rs, :] = (dqf * (sq * (1.0 + q * (1.0 - sq)))).astype(BF16)
            df_ref[rs, :] = (sig_neg * (dg * jnp.exp(log_c - log_f) - dkf * (1.0 - lbv) * sig)).astype(BF16)
            di_ref[rs, :] = dvf.astype(BF16)
            dlb_ref[...] += jnp.sum(sig_neg * (dg * jnp.exp(-log_f) - dkf), axis=0, keepdims=True)

    blk = (cps * c_len, HGRN_WIDTH)
    sblk = (cps, c_len, HGRN_WIDTH)
    rev = lambda c: last - c
    step = lambda k: (lambda: pl.program_id(0) == k)
    e_in, e_out, e_shape, e_scr, e_args = _ride_specs(ride)
    return pl.pallas_call(
        _riding(body, 7, 4, 7, ride, step(0), step(n_steps // 2), step(last)),
        name="hgrn_bwd",
        grid=(n_steps,),
        in_specs=[
            pl.BlockSpec(blk, lambda c: (rev(c), col0)),
            pl.BlockSpec(blk, lambda c: (rev(c), col0 + 1)),
            pl.BlockSpec(blk, lambda c: (rev(c), col0 + 2)),
            pl.BlockSpec((2, HGRN_WIDTH), lambda c: (0, 0)),
            pl.BlockSpec(blk, lambda c: (rev(c), 0)),
            pl.BlockSpec((cps, nh, hd, hd), lambda c: (rev(c), 0, 0, 0)),
            pl.BlockSpec(blk, lambda c: (rev(c), 0)),
        ] + e_in,
        out_specs=[
            pl.BlockSpec(blk, lambda c: (rev(c), 0)),
            pl.BlockSpec(blk, lambda c: (rev(c), 0)),
            pl.BlockSpec(blk, lambda c: (rev(c), 0)),
            pl.BlockSpec((1, HGRN_WIDTH), lambda c: (0, 0)),
        ] + e_out,
        out_shape=[jax.ShapeDtypeStruct((s, HGRN_WIDTH), BF16)] * 3 + [jax.ShapeDtypeStruct((1, HGRN_WIDTH), F32)] + e_shape,
        scratch_shapes=[
            pltpu.VMEM((nh, hd, hd), F32),
            pltpu.VMEM(sblk, F32),
            pltpu.VMEM(sblk, F32),
            pltpu.VMEM(sblk, F32),
            pltpu.VMEM((cps, nh, c_len, hd), F32),
            pltpu.VMEM(sblk, F32),
            pltpu.VMEM(sblk, F32),
        ] + e_scr,
        compiler_params=_params(ride, dimension_semantics=("arbitrary",)),
    )(proj, proj, proj, lb, d_o, states, a_mat, *e_args)


def _per_head_lanes(x):
    lane = lax.broadcasted_iota(jnp.int32, (x.shape[0], LANES), 1)
    out = jnp.zeros((x.shape[0], LANES), F32)
    for h in range(ATTN_HEADS):
        out = jnp.where(lane == h, x[:, h * ATTN_HEAD_DIM:h * ATTN_HEAD_DIM + 1], out)
    return out


def _row_spec(tm, width, col=0):
    return pl.BlockSpec((tm, width), lambda i: (i, col))


def _const_spec(width):
    return pl.BlockSpec((1, width), lambda i: (0, 0))


def _acc_rows(ref, value):
    @pl.when(pl.program_id(0) == 0)
    def _():
        ref[...] = jnp.zeros_like(ref)

    ref[...] += jnp.sum(value, axis=0, keepdims=True)


def mix_fwd(attn_parts, o_h, proj, an, hn, w_out_b, gp, x, ride=None):
    s = x.shape[0]
    tm = TOKEN_TILE
    gate_col = 3
    hd = HGRN_HEAD_DIM
    nd = len(DILATIONS)

    def body(*refs):
        o_refs, l_refs = refs[:nd], refs[nd:2 * nd]
        oh_ref, gate_ref, an_ref, hn_ref, w_ref, gp_ref, x_ref = refs[2 * nd:2 * nd + 7]
        x1_ref, cat_ref, mixed_ref, attn_ref = refs[2 * nd + 7:2 * nd + 11]
        lse_refs = refs[2 * nd + 11:3 * nd + 11]
        o_scr, l_scr, lse_scr = refs[3 * nd + 11:]
        os_ = [_from_dilated(r, o_scr.at[k], d, tm) for k, (r, d) in enumerate(zip(o_refs, DILATIONS))]
        ls = [_from_dilated(r, l_scr.at[k], d, tm) for k, (r, d) in enumerate(zip(l_refs, DILATIONS))]
        m = jnp.maximum(jnp.maximum(ls[0], ls[1]), ls[2])
        es = [jnp.exp(l - m) for l in ls]
        den = es[0] + es[1] + es[2]
        attn = (es[0] * os_[0] + es[1] * os_[1] + es[2] * os_[2]) / den
        attn_ref[...] = attn
        lse_scr[0] = _per_head_lanes(m + jnp.log(den))
        for d, ref in zip(DILATIONS, lse_refs):
            _to_dilated(lse_scr, ref, d, tm)
        cat_ref[:, :ATTN_WIDTH] = _rms_fwd(attn, an_ref[...], ATTN_WIDTH).astype(BF16)
        gate = gate_ref[...]
        silu_g = gate * _sigmoid(gate)
        for h in range(HGRN_HEADS):
            hs = slice(h * hd, (h + 1) * hd)
            rec = _rms_fwd(oh_ref[:, hs], hn_ref[:, hs], hd) * silu_g[:, hs]
            cat_ref[:, ATTN_WIDTH + h * hd:ATTN_WIDTH + (h + 1) * hd] = rec.astype(BF16)
        mixed = _dot(cat_ref[...], w_ref[...])
        mixed_ref[...] = mixed
        x1_ref[...] = x_ref[...] + _rms_fwd(mixed, gp_ref[...], D_MODEL)

    aw = ATTN_WIDTH
    n_steps = s // tm
    step = lambda k: (lambda: pl.program_id(0) == k)
    e_in, e_out, e_shape, e_scr, e_args = _ride_specs(ride)
    return pl.pallas_call(
        _riding(body, 2 * nd + 7, 4 + nd, 3, ride, step(0), step((13 * n_steps) // 16), step(n_steps - 1)),
        name="mix_fwd",
        grid=(n_steps,),
        in_specs=[_dilated_spec(d, tm, aw) for d in DILATIONS] * 2 + [
            _row_spec(tm, aw), _row_spec(tm, aw, gate_col), _const_spec(aw), _const_spec(aw), _vmem_spec(),
            _const_spec(D_MODEL), _row_spec(tm, D_MODEL)] + e_in,
        out_specs=[_row_spec(tm, D_MODEL), _row_spec(tm, D_MODEL), _row_spec(tm, D_MODEL), _row_spec(tm, aw)] + [
            _dilated_spec(d, tm, LANES) for d in DILATIONS] + e_out,
        out_shape=[
            jax.ShapeDtypeStruct((s, D_MODEL), F32),
            jax.ShapeDtypeStruct((s, D_MODEL), BF16),
            jax.ShapeDtypeStruct((s, D_MODEL), F32),
            jax.ShapeDtypeStruct((s, aw), F32),
        ] + [jax.ShapeDtypeStruct((d, s // d, LANES), F32) for d in DILATIONS] + e_shape,
        scratch_shapes=[pltpu.VMEM((nd, aw // LANES, tm, LANES), F32), pltpu.VMEM((nd, aw // LANES, tm, LANES), F32),
                        pltpu.VMEM((1, tm, LANES), F32)] + e_scr,
        compiler_params=_params(ride, dimension_semantics=("arbitrary",)),
    )(*[p[0] for p in attn_parts], *[p[1] for p in attn_parts], o_h, proj, an, hn, w_out_b, gp, x, *e_args)


def mix_bwd(dx1, mixed, gp, w_out_b, attn, an, o_h, proj, hn):
    s = dx1.shape[0]
    tm = TOKEN_TILE
    gate_col = 3
    hd = HGRN_HEAD_DIM
    aw = ATTN_WIDTH

    nd = len(DILATIONS)

    def body(*refs):
        dx1_ref, mixed_ref, gp_ref, w_ref, attn_ref, an_ref, oh_ref, gate_ref, hn_ref, dmix_ref = refs[:10]
        do_refs, delta_refs = refs[10:10 + nd], refs[10 + nd:10 + 2 * nd]
        doh_ref, dgate_ref, dgp_ref, dan_ref, dhn_ref, do_ref, delta_ref = refs[10 + 2 * nd:]
        dmixed, gp_c = _rms_bwd(dx1_ref[...], mixed_ref[...], gp_ref[...], D_MODEL)
        _acc_rows(dgp_ref, gp_c)
        dmixed_b = dmixed.astype(BF16)
        dmix_ref[...] = dmixed_b
        dcat = _dot_nt(dmixed_b, w_ref[...])
        attn = attn_ref[...]
        d_o, an_c = _rms_bwd(dcat[:, :aw], attn, an_ref[...], aw)
        _acc_rows(dan_ref, an_c)
        _lane_blocks(do_ref, d_o)
        prod = d_o * attn
        lane = lax.broadcasted_iota(jnp.int32, (tm, LANES), 1)
        delta = jnp.zeros((tm, LANES), F32)
        for pair in range(ATTN_HEADS // 2):
            pp = prod[:, pair * LANES:(pair + 1) * LANES]
            low = _lane_half((tm, LANES), 0)
            lo = jnp.sum(jnp.where(low, pp, 0.0), axis=-1, keepdims=True)
            hi = jnp.sum(jnp.where(low, 0.0, pp), axis=-1, keepdims=True)
            delta = jnp.where(lane == 2 * pair, lo, jnp.where(lane == 2 * pair + 1, hi, delta))
        delta_ref[0] = delta
        for d, o_ref, l_ref in zip(DILATIONS, do_refs, delta_refs):
            _to_dilated(do_ref, o_ref, d, tm, cast=BF16)
            _to_dilated(delta_ref, l_ref, d, tm)
        gate = gate_ref[...]
        sg = _sigmoid(gate)
        silu_g = gate * sg
        drec = dcat[:, aw:]
        hn_parts = []
        for h in range(HGRN_HEADS):
            hs = slice(h * hd, (h + 1) * hd)
            oh = oh_ref[:, hs]
            on = _rms_fwd(oh, hn_ref[:, hs], hd)
            dgate_ref[:, hs] = (drec[:, hs] * on * (sg[:, hs] * (1.0 + gate[:, hs] * (1.0 - sg[:, hs])))).astype(BF16)
            d_oh, hn_c = _rms_bwd(drec[:, hs] * silu_g[:, hs], oh, hn_ref[:, hs], hd)
            doh_ref[:, hs] = d_oh
            hn_parts.append(hn_c)
        _acc_rows(dhn_ref, jnp.concatenate(hn_parts, axis=1))

    return pl.pallas_call(
        body,
        name="mix_bwd",
        grid=(s // tm,),
        in_specs=[_row_spec(tm, D_MODEL), _row_spec(tm, D_MODEL), _const_spec(D_MODEL), _vmem_spec(), _row_spec(tm, aw),
                  _const_spec(aw), _row_spec(tm, aw), _row_spec(tm, aw, gate_col), _const_spec(aw)],
        out_specs=[_row_spec(tm, D_MODEL)] + [_dilated_spec(d, tm, aw) for d in DILATIONS] + [
            _dilated_spec(d, tm, LANES) for d in DILATIONS] + [_row_spec(tm, aw)] * 2 + [
            _const_spec(D_MODEL), _const_spec(aw), _const_spec(aw)],
        out_shape=[jax.ShapeDtypeStruct((s, D_MODEL), BF16)] + [
            jax.ShapeDtypeStruct((d, s // d, aw), BF16) for d in DILATIONS] + [
            jax.ShapeDtypeStruct((d, s // d, LANES), F32) for d in DILATIONS] + [
            jax.ShapeDtypeStruct((s, aw), F32), jax.ShapeDtypeStruct((s, aw), BF16),
            jax.ShapeDtypeStruct((1, D_MODEL), F32), jax.ShapeDtypeStruct((1, aw), F32),
            jax.ShapeDtypeStruct((1, aw), F32)],
        scratch_shapes=[pltpu.VMEM((aw // LANES, tm, LANES), F32), pltpu.VMEM((1, tm, LANES), F32)],
        compiler_params=_params(dimension_semantics=("arbitrary",)),
    )(dx1, mixed, gp, w_out_b, attn, an, o_h, proj, hn)


def mlp_fwd_bwd(x1, g_pre, w1_blocks, w2_b, g_post, target):
    s = x1.shape[0]
    tm = MLP_TILE
    nblk, _, fb = w1_blocks.shape

    def body(x1_ref, gpre_ref, w1_ref, w2_ref, gpost_ref, t_ref,
             dx1_ref, h2_ref, a_ref, du_ref, dff_ref, loss_ref, dgpre_ref, dgpost_ref, u_ref):
        x1v = x1_ref[...]
        h2 = _rms_fwd(x1v, gpre_ref[...], D_MODEL).astype(BF16)
        h2_ref[...] = h2
        ff = jnp.zeros((tm, D_MODEL), F32)
        for j in range(nblk):
            cols = slice(j * fb, (j + 1) * fb)
            ru = jnp.maximum(_dot(h2, w1_ref[j]), 0.0)
            u_ref[:, cols] = ru.astype(BF16)
            a = (ru * ru).astype(BF16)
            a_ref[:, cols] = a
            ff = ff + _dot(a, w2_ref[cols, :])
        diff = x1v + _rms_fwd(ff, gpost_ref[...], D_MODEL) - t_ref[...]
        _acc_rows(loss_ref, diff * diff)
        dy = diff * (1.0 / D_MODEL)
        dff, gpost_c = _rms_bwd(dy, ff, gpost_ref[...], D_MODEL)
        _acc_rows(dgpost_ref, gpost_c)
        dff_b = dff.astype(BF16)
        dff_ref[...] = dff_b
        dh2 = jnp.zeros((tm, D_MODEL), F32)
        for j in range(nblk):
            cols = slice(j * fb, (j + 1) * fb)
            du = (_dot_nt(dff_b, w2_ref[cols, :]) * (2.0 * u_ref[:, cols])).astype(BF16)
            du_ref[:, cols] = du
            dh2 = dh2 + _dot_nt(du, w1_ref[j])
        dxa, gpre_c = _rms_bwd(dh2, x1v, gpre_ref[...], D_MODEL)
        _acc_rows(dgpre_ref, gpre_c)
        dx1_ref[...] = dy + dxa

    dm = D_MODEL
    return pl.pallas_call(
        body,
        name="mlp_fwd_bwd",
        grid=(s // tm,),
        in_specs=[_row_spec(tm, dm), _const_spec(dm), _vmem_spec(), _vmem_spec(), _const_spec(dm), _row_spec(tm, dm)],
        out_specs=[_row_spec(tm, dm), _row_spec(tm, dm), _row_spec(tm, D_FF), _row_spec(tm, D_FF), _row_spec(tm, dm),
                   _const_spec(dm), _const_spec(dm), _const_spec(dm)],
        out_shape=[
            jax.ShapeDtypeStruct((s, dm), F32),
            jax.ShapeDtypeStruct((s, dm), BF16),
            jax.ShapeDtypeStruct((s, D_FF), BF16),
            jax.ShapeDtypeStruct((s, D_FF), BF16),
            jax.ShapeDtypeStruct((s, dm), BF16),
            jax.ShapeDtypeStruct((1, dm), F32),
            jax.ShapeDtypeStruct((1, dm), F32),
            jax.ShapeDtypeStruct((1, dm), F32),
        ],
        scratch_shapes=[pltpu.VMEM((tm, D_FF), BF16)],
        compiler_params=_params(dimension_semantics=("arbitrary",)),
    )(x1, g_pre, w1_blocks, w2_b, g_post, target)


def in_proj_bwd(attn_grads, hgrn_grads, dgate, w_in_b, x, g1, dx1):
    s = x.shape[0]
    tm = PROJ_TILE
    aw = ATTN_WIDTH
    n_attn = len(attn_grads)
    flat = [g[k] for k in range(3) for g in attn_grads] + list(hgrn_grads) + [dgate]

    def body(*refs):
        parts = refs[:len(flat)]
        w_ref, x_ref, g_ref, dx1_ref, dx_ref, dproj_ref, dg_ref, scr = refs[len(flat):]
        groups = []
        for k in range(3):
            acc = None
            for p, d in zip(parts[k * n_attn:(k + 1) * n_attn], DILATIONS):
                v = _from_dilated(p, scr, d, tm)
                acc = v if acc is None else acc + v
            groups.append(acc)
        groups += [p[...] for p in parts[3 * n_attn:]]
        dh = jnp.zeros((tm, D_MODEL), F32)
        for gi, grp in enumerate(groups):
            cols = slice(gi * aw, (gi + 1) * aw)
            gb = grp.astype(BF16)
            dproj_ref[:, cols] = gb
            dh = dh + _dot_nt(gb, w_ref[:, cols])
        dxa, g_c = _rms_bwd(dh, x_ref[...], g_ref[...], D_MODEL)
        _acc_rows(dg_ref, g_c)
        dx_ref[...] = dx1_ref[...] + dxa

    dm = D_MODEL
    return pl.pallas_call(
        body,
        name="in_proj_bwd",
        grid=(s // tm,),
        in_specs=[_dilated_spec(d, tm, aw) for d in DILATIONS] * 3 + [_row_spec(tm, aw)] * 4 + [
            _vmem_spec(), _row_spec(tm, dm), _const_spec(dm), _row_spec(tm, dm)],
        out_specs=[_row_spec(tm, dm), _row_spec(tm, IN_PROJ_WIDTH), _const_spec(dm)],
        out_shape=[jax.ShapeDtypeStruct((s, dm), F32), jax.ShapeDtypeStruct((s, IN_PROJ_WIDTH), BF16),
                   jax.ShapeDtypeStruct((1, dm), F32)],
        scratch_shapes=[pltpu.VMEM((aw // LANES, tm, LANES), F32)],
        compiler_params=_params(dimension_semantics=("arbitrary",)),
    )(*flat, w_in_b, x, g1, dx1)


def wgrad(a_b, b_b, tn, name, ts=2048, per_step=1, ride=None):
    s, k = a_b.shape
    n = b_b.shape[1]

    def body(a_ref, b_ref, o_ref):
        @pl.when(pl.program_id(1) == 0)
        def _():
            o_ref[...] = jnp.zeros_like(o_ref)

        a = a_ref[...]
        for jj in range(per_step):
            o_ref[jj] += _dot_tn(a, b_ref[:, jj * tn:(jj + 1) * tn])

    wide = tn * per_step
    gn, gs = n // wide, s // ts
    step = lambda j, i: (lambda: (pl.program_id(0) == j) & (pl.program_id(1) == i))
    e_in, e_out, e_shape, e_scr, e_args = _ride_specs(ride)
    out = pl.pallas_call(
        _riding(body, 2, 1, 0, ride, step(0, 0), step(gn // 2, 0), step(gn - 1, gs - 1)),
        name=name,
        grid=(gn, gs),
        in_specs=[pl.BlockSpec((ts, k), lambda j, i: (i, 0)), pl.BlockSpec((ts, wide), lambda j, i: (i, j))] + e_in,
        out_specs=[pl.BlockSpec((per_step, k, tn), lambda j, i: (j, 0, 0))] + e_out,
        out_shape=[jax.ShapeDtypeStruct((n // tn, k, tn), F32)] + e_shape,
        scratch_shapes=e_scr,
        compiler_params=_params(ride, dimension_semantics=("arbitrary", "arbitrary")),
    )(a_b, b_b, *e_args)
    return out[0] if ride is None else out


def train_step(x, target, g1, an, logits, hn, gp, g_pre, g_post, w, m, v):
    nd = len(DILATIONS)
    shard_b = {k: w[k].astype(BF16) for k in BIG}
    (w_in_g,) = run_exchange(gather_exchange([shard_b["w_in"]]), "gather_w_in")
    w_in_b = w_in_g.transpose(1, 0, 2).reshape(D_MODEL, IN_PROJ_WIDTH)

    proj, h_b, *qkvs, w2_g = in_proj_fwd(x, g1, w_in_b, ride=gather_exchange([shard_b["w_ff2"]]))
    w2_b = w2_g.reshape(D_FF, D_MODEL)
    attn_parts = [attn_fwd(qkv, d) for qkv, d in zip(qkvs, DILATIONS)]
    o_h, states, a_mat, w_out_g, w1_blocks = hgrn_fwd(
        proj, logits, ride=gather_exchange([shard_b["w_out"], shard_b["w_ff1"]]))
    w_out_b = w_out_g.reshape(D_MODEL, D_MODEL)
    x1, cat_b, mixed, attn, *lses = mix_fwd(attn_parts, o_h, proj, an, hn, w_out_b, gp, x)
    dx1, h2_b, a_b, du_b, dff_b, loss_vec, dg_pre, dg_post = mlp_fwd_bwd(x1, g_pre, w1_blocks, w2_b, g_post, target)
    dw2 = wgrad(a_b, dff_b, D_MODEL, "wgrad_ff2", ts=1024)
    dw1 = wgrad(h2_b, du_b, D_FF // N_DEV, "wgrad_ff1", per_step=2)
    dmix_b, *rest = mix_bwd(dx1, mixed, gp, w_out_b, attn, an, o_h, proj, hn)
    d_os, deltas = rest[:nd], rest[nd:2 * nd]
    d_oh, dgate, dgp, dan, dhn = rest[2 * nd:]
    dwout = wgrad(cat_b, dmix_b, D_MODEL, "wgrad_out")

    early = ("w_out", "w_ff1", "w_ff2")
    early_grads = [dwout.reshape(N_DEV, D_MODEL // N_DEV, D_MODEL), dw1, dw2.reshape(N_DEV, D_FF // N_DEV, D_MODEL)]
    res = attn_bwd(qkvs[0], d_os[0], lses[0], deltas[0], DILATIONS[0], ride=to_core_exchange(early_grads))
    pairs = [pair_sum(g, s, f"pair_sum_{name}") for g, s, name in zip(early_grads, res[3:], early)]
    attn_grads = [res[:3]]
    *res, others_ff2 = attn_bwd(qkvs[1], d_os[1], lses[1], deltas[1], DILATIONS[1],
                                ride=to_chip_exchange([pairs[2][1]]))
    attn_grads.append(res)
    attn_grads.append(attn_bwd(qkvs[2], d_os[2], lses[2], deltas[2], DILATIONS[2]))
    dq_h, df_h, di_h, dlb, *others = hgrn_bwd(proj, logits, d_oh, states, a_mat,
                                              ride=to_chip_exchange([pairs[0][1], pairs[1][1]]))
    others.append(others_ff2)
    dx, dproj_b, dg1 = in_proj_bwd(attn_grads, (dq_h, df_h, di_h), dgate, w_in_b, x, g1, dx1)
    packed = _pack_small(dg1, dgp, dg_pre, dg_post, dan, dhn, dlb, loss_vec)
    dwin, small_slots = wgrad(h_b, dproj_b, 2 * IN_PROJ_WIDTH // N_DEV, "wgrad_in",
                              ride=small_exchange(packed))
    big = {name: sum_adamw(p[0], o, w[name], m[name], v[name], f"sum_adamw_{name}")
           for name, p, o in zip(early, pairs, others)}

    shard_w = IN_PROJ_WIDTH // N_DEV
    dwin_blocks = dwin.reshape(N_DEV // 2, D_MODEL, 2, shard_w).transpose(0, 2, 1, 3).reshape(N_DEV, D_MODEL, shard_w)
    pair_in, others_in = reduce_last(dwin_blocks)
    big["w_in"] = sum_adamw(pair_in, others_in, w["w_in"], m["w_in"], v["w_in"], "sum_adamw_w_in")
    return dx, big, small_slots


def _position():
    x, y, c = lax.axis_index("x"), lax.axis_index("y"), lax.axis_index("c")
    other_chips = [(1 - x, y), (x, 1 - y), (1 - x, 1 - y)]
    return x, y, c, other_chips


def _any_spec():
    return pl.BlockSpec(memory_space=pl.ANY)


class Exchange:
    def __init__(self, arrays, out_shape, sems, stages, collective_id, peers):
        self.arrays, self.out_shape, self.sems, self.stages = list(arrays), list(out_shape), list(sems), stages
        self.collective_id, self.peers = collective_id, peers

    def open(self):
        barrier = pltpu.get_barrier_semaphore()
        peers = self.peers()
        for peer in peers:
            pl.semaphore_signal(barrier, inc=1, device_id=peer, device_id_type=MESH)
        pl.semaphore_wait(barrier, len(peers))


def _siblings():
    x, y, c, _ = _position()
    return [(x, y, 1 - c)]


def _same_core_of_other_chips():
    x, y, c, chips = _position()
    return [(px, py, c) for px, py in chips]


def _gather_peers():
    x, y, c, _ = _position()
    return [(x, y, 1 - c), (1 - x, y, c), (x, 1 - y, c)]


def _all_others():
    x, y, c, _ = _position()
    return [(1 - x if rel & 4 else x, 1 - y if rel & 2 else y, 1 - c if rel & 1 else c) for rel in range(1, N_DEV)]


def gather_exchange(shards):
    n = len(shards)
    halves = [sh.shape[0] // 2 for sh in shards]

    def stages(ins, outs, sems):
        send_sems, recv_sems, local_sems = sems

        def parts():
            x, y, c, _ = _position()
            me, sibling = (x, y, c), (x, y, 1 - c)
            nbr_x, nbr_y, diag = (1 - x, y, c), (x, 1 - y, c), (1 - x, 1 - y, c)

            def slot(a, dev, rows=None):
                ref = outs[a].at[4 * dev[0] + 2 * dev[1] + dev[2]]
                return ref if rows is None else ref.at[rows]

            def copy(a, k, block, to, rows=None, src=None):
                return pltpu.make_async_remote_copy(
                    src_ref=slot(a, block, rows) if src is None else src, dst_ref=slot(a, block, rows),
                    send_sem=send_sems.at[a, k], recv_sem=recv_sems.at[a, k], device_id=to, device_id_type=MESH)

            upper = lambda a: pl.ds(0, halves[a])
            lower = lambda a: pl.ds(halves[a], halves[a])
            return me, sibling, nbr_x, nbr_y, diag, slot, copy, upper, lower

        def begin():
            me, sibling, nbr_x, nbr_y, _, slot, copy, _, _ = parts()
            for a in range(n):
                pltpu.make_async_copy(ins[a], slot(a, me), local_sems.at[a]).start()
                for k, to in enumerate((sibling, nbr_x, nbr_y)):
                    copy(a, k, me, to, src=ins[a]).start()

        def middle():
            me, sibling, nbr_x, nbr_y, _, _, copy, upper, lower = parts()
            for a in range(n):
                copy(a, 1, nbr_x, me).wait_recv()
                copy(a, 3, nbr_x, sibling).start()
                copy(a, 5, nbr_x, nbr_y, rows=lower(a)).start()
                copy(a, 2, nbr_y, me).wait_recv()
                copy(a, 4, nbr_y, sibling).start()
                copy(a, 6, nbr_y, nbr_x, rows=upper(a)).start()

        def late():
            me, sibling, _, _, diag, _, copy, upper, lower = parts()
            for a in range(n):
                copy(a, 6, diag, me, rows=upper(a)).wait_recv()
                copy(a, 5, diag, me, rows=lower(a)).wait_recv()
                copy(a, 7, diag, sibling).start()

        def end():
            me, sibling, nbr_x, nbr_y, diag, slot, copy, upper, lower = parts()
            sib = lambda dev: (dev[0], dev[1], sibling[2])
            for a in range(n):
                for k, block in ((0, sibling), (3, sib(nbr_x)), (4, sib(nbr_y)), (7, sib(diag))):
                    copy(a, k, block, me).wait_recv()
                copy(a, 0, me, sibling, src=ins[a]).wait_send()
                copy(a, 1, me, nbr_x, src=ins[a]).wait_send()
                copy(a, 2, me, nbr_y, src=ins[a]).wait_send()
                copy(a, 3, nbr_x, sibling).wait_send()
                copy(a, 4, nbr_y, sibling).wait_send()
                copy(a, 5, nbr_x, nbr_y, rows=lower(a)).wait_send()
                copy(a, 6, nbr_y, nbr_x, rows=upper(a)).wait_send()
                copy(a, 7, diag, sibling).wait_send()
                pltpu.make_async_copy(ins[a], slot(a, me), local_sems.at[a]).wait()

        return begin, (middle, late), end

    return Exchange(
        shards, [jax.ShapeDtypeStruct((N_DEV,) + sh.shape, sh.dtype) for sh in shards],
        [pltpu.SemaphoreType.DMA((n, 8)), pltpu.SemaphoreType.DMA((n, 8)), pltpu.SemaphoreType.DMA((n,))], stages,
        collective_id=0, peers=_gather_peers)


def to_core_exchange(grads):
    n = len(grads)

    def stages(ins, outs, sems):
        send_sems, recv_sems = sems

        def copies():
            x, y, c, _ = _position()
            return [pltpu.make_async_remote_copy(
                src_ref=ins[a].at[2 * q + (1 - c)], dst_ref=outs[a].at[q], send_sem=send_sems.at[a, q],
                recv_sem=recv_sems.at[a, q], device_id=(x, y, 1 - c), device_id_type=MESH)
                for a in range(n) for q in range(4)]

        def begin():
            for cp in copies():
                cp.start()

        def end():
            for cp in copies():
                cp.wait()

        return begin, None, end

    return Exchange(grads, [jax.ShapeDtypeStruct((4,) + g.shape[1:], g.dtype) for g in grads],
                    [pltpu.SemaphoreType.DMA((n, 4)), pltpu.SemaphoreType.DMA((n, 4))], stages,
                    collective_id=1, peers=_siblings)


def pair_sum(grad, from_sibling, name):
    _, r, cdim = grad.shape
    tr = min(r, ELEMENTWISE_ROWS)
    c_idx = lax.axis_index("c").astype(jnp.int32).reshape(1)

    def body(c_ref, g_ref, s_ref, o_ref, ob_ref):
        total = g_ref[...] + s_ref[...]
        o_ref[...] = total
        ob_ref[...] = total.astype(BF16)

    blk = lambda: pl.BlockSpec((1, tr, cdim), lambda q, i, cr: (q, i, 0))
    return pl.pallas_call(
        body,
        name=name,
        grid_spec=pltpu.PrefetchScalarGridSpec(
            num_scalar_prefetch=1,
            grid=(4, r // tr),
            in_specs=[pl.BlockSpec((1, tr, cdim), lambda q, i, cr: (2 * q + cr[0], i, 0)), blk()],
            out_specs=[blk(), blk()],
        ),
        out_shape=[jax.ShapeDtypeStruct((4, r, cdim), F32), jax.ShapeDtypeStruct((4, r, cdim), BF16)],
        compiler_params=_params(dimension_semantics=("arbitrary", "arbitrary")),
    )(c_idx, grad, from_sibling)


def to_chip_exchange(pairs):
    n = len(pairs)

    def stages(ins, outs, sems):
        send_sems, recv_sems = sems

        def copies():
            x, y, c, chips = _position()
            return [pltpu.make_async_remote_copy(
                src_ref=ins[a].at[2 * px + py], dst_ref=outs[a].at[j], send_sem=send_sems.at[a, j],
                recv_sem=recv_sems.at[a, j], device_id=(px, py, c), device_id_type=MESH)
                for a in range(n) for j, (px, py) in enumerate(chips)]

        def begin():
            for cp in copies():
                cp.start()

        def end():
            for cp in copies():
                cp.wait()

        return begin, None, end

    return Exchange(pairs, [jax.ShapeDtypeStruct((3,) + p.shape[1:], p.dtype) for p in pairs],
                    [pltpu.SemaphoreType.DMA((n, 3)), pltpu.SemaphoreType.DMA((n, 3))], stages,
                    collective_id=2, peers=_same_core_of_other_chips)


def run_exchange(ex, name):
    n_in, n_out = len(ex.arrays), len(ex.out_shape)

    def body(*refs):
        begin, middle, end = ex.stages(refs[:n_in], refs[n_in:n_in + n_out], refs[n_in + n_out:])
        ex.open()
        begin()
        for stage in _as_tuple(middle):
            stage()
        end()

    return pl.pallas_call(
        body,
        name=name,
        in_specs=[_any_spec()] * n_in,
        out_specs=[_any_spec()] * n_out,
        out_shape=ex.out_shape,
        scratch_shapes=ex.sems,
        compiler_params=pltpu.CompilerParams(collective_id=ex.collective_id),
    )(*ex.arrays)


def _as_tuple(stages):
    return () if stages is None else stages if isinstance(stages, tuple) else (stages,)


def _riding(body, n_in, n_out, n_scratch, ex, first, middle, last, late=None):
    if ex is None:
        return body
    r_in, r_out = len(ex.arrays), len(ex.out_shape)

    def wrapped(*refs):
        k_in, refs = refs[:n_in], refs[n_in:]
        e_in, refs = refs[:r_in], refs[r_in:]
        k_out, refs = refs[:n_out], refs[n_out:]
        e_out, refs = refs[:r_out], refs[r_out:]
        k_scr, e_sems = refs[:n_scratch], refs[n_scratch:]
        begin, mid, end = ex.stages(e_in, e_out, e_sems)

        @pl.when(first())
        def _():
            ex.open()
            begin()

        body(*k_in, *k_out, *k_scr)
        for stage, at in zip(_as_tuple(mid), (middle, late or last)):
            pl.when(at())(stage)
        pl.when(last())(end)

    return wrapped


def _ride_specs(ex):
    if ex is None:
        return [], [], [], [], []
    return [_any_spec()] * len(ex.arrays), [_any_spec()] * len(ex.out_shape), ex.out_shape, ex.sems, ex.arrays


def reduce_last(grad):
    _, r, cdim = grad.shape

    def body(g_hbm, own_ref, others_hbm, g_buf, to_sib, from_sib, send_buf, load_sem, send_sems, recv_sems):
        x, y, c, chips = _position()
        order = chips + [(x, y)]
        n_other = len(chips)
        sibling = (x, y, 1 - c)
        barrier = pltpu.get_barrier_semaphore()
        peers = [sibling] + [(px, py, c) for px, py in chips]
        for peer in peers:
            pl.semaphore_signal(barrier, inc=1, device_id=peer, device_id_type=MESH)
        pl.semaphore_wait(barrier, len(peers))

        def load(block):
            cp = pltpu.make_async_copy(g_hbm.at[block], g_buf, load_sem)
            cp.start()
            cp.wait()
            return g_buf[...]

        def to_sibling(j):
            return pltpu.make_async_remote_copy(
                src_ref=to_sib.at[j], dst_ref=from_sib.at[j], send_sem=send_sems.at[n_other + j],
                recv_sem=recv_sems.at[n_other + j], device_id=sibling, device_id_type=MESH)

        def to_chip(j):
            px, py = chips[j]
            return pltpu.make_async_remote_copy(
                src_ref=send_buf.at[j], dst_ref=others_hbm.at[j], send_sem=send_sems.at[j], recv_sem=recv_sems.at[j],
                device_id=(px, py, c), device_id_type=MESH)

        def hand_over(j):
            px, py = order[j]
            to_sib[j] = load(2 * (2 * px + py) + (1 - c)).astype(BF16)
            to_sibling(j).start()

        def pair_sum_of(j):
            px, py = order[j]
            to_sibling(j).wait_recv()
            total = load(2 * (2 * px + py) + c) + from_sib[j].astype(F32)
            if j < n_other:
                send_buf[j] = total.astype(BF16)
                to_chip(j).start()
            else:
                own_ref[0] = total

        turn = [n_other - 1] + list(range(n_other - 1)) + [n_other]
        hand_over(turn[0])
        for before, j in zip(turn, turn[1:]):
            hand_over(j)
            pair_sum_of(before)
        pair_sum_of(turn[-1])
        for j in range(len(order)):
            to_sibling(j).wait_send()
        for j in range(n_other):
            to_chip(j).wait()

    n_blocks = N_DEV // 2
    return pl.pallas_call(
        body,
        name="reduce_w_in",
        in_specs=[_any_spec()],
        out_specs=[_vmem_spec(), _any_spec()],
        out_shape=[jax.ShapeDtypeStruct((1, r, cdim), F32), jax.ShapeDtypeStruct((n_blocks - 1, r, cdim), BF16)],
        scratch_shapes=[pltpu.VMEM((r, cdim), F32), pltpu.VMEM((n_blocks, r, cdim), BF16),
                        pltpu.VMEM((n_blocks, r, cdim), BF16), pltpu.VMEM((n_blocks - 1, r, cdim), BF16),
                        pltpu.SemaphoreType.DMA(()), pltpu.SemaphoreType.DMA((2 * n_blocks - 1,)),
                        pltpu.SemaphoreType.DMA((2 * n_blocks - 1,))],
        compiler_params=pltpu.CompilerParams(collective_id=4, vmem_limit_bytes=VMEM_LIMIT),
    )(grad)


def _adamw(w, g, m, v):
    m = ADAM_B1 * m + (1.0 - ADAM_B1) * g
    v = ADAM_B2 * v + (1.0 - ADAM_B2) * (g * g)
    m_hat = m / (1.0 - ADAM_B1 ** ADAM_STEP)
    v_hat = v / (1.0 - ADAM_B2 ** ADAM_STEP)
    delta = -ADAM_LR * (m_hat / (jnp.sqrt(v_hat) + ADAM_EPS) + ADAM_WD * w)
    return delta, m, v


def sum_adamw(pairs, others, w, m, v, name):
    r, cdim = w.shape
    tr = min(r, ELEMENTWISE_ROWS // 2)
    if pairs.shape[0] == 1:
        chip_idx = jnp.zeros((1,), jnp.int32)
    else:
        chip_idx = (2 * lax.axis_index("x") + lax.axis_index("y")).astype(jnp.int32).reshape(1)

    def body(q_ref, p_ref, o_ref, w_ref, m_ref, v_ref, g_out, d_out, m_out, v_out):
        g = p_ref[0] + o_ref[0].astype(F32) + o_ref[1].astype(F32) + o_ref[2].astype(F32)
        g_out[...] = g
        d_out[...], m_out[...], v_out[...] = _adamw(w_ref[...], g, m_ref[...], v_ref[...])

    tile = lambda: pl.BlockSpec((tr, cdim), lambda i, qr: (i, 0))
    return pl.pallas_call(
        body,
        name=name,
        grid_spec=pltpu.PrefetchScalarGridSpec(
            num_scalar_prefetch=1,
            grid=(r // tr,),
            in_specs=[pl.BlockSpec((1, tr, cdim), lambda i, qr: (qr[0], i, 0)),
                      pl.BlockSpec((3, tr, cdim), lambda i, qr: (0, i, 0)), tile(), tile(), tile()],
            out_specs=[tile(), tile(), tile(), tile()],
        ),
        out_shape=[jax.ShapeDtypeStruct((r, cdim), F32)] * 4,
        compiler_params=_params(dimension_semantics=("arbitrary",)),
    )(chip_idx, pairs, others, w, m, v)


def small_exchange(packed):
    def stages(ins, outs, sems):
        send_sems, recv_sems, local_sem = sems
        (src,), (slots,) = ins, outs

        def copies():
            x, y, c, _ = _position()
            my_id = 4 * x + 2 * y + c
            sends, landings = [], []
            for rel in range(1, N_DEV):
                px = 1 - x if (rel >> 2) & 1 else x
                py = 1 - y if (rel >> 1) & 1 else y
                pc = 1 - c if rel & 1 else c
                peer = dict(send_sem=send_sems.at[rel - 1], recv_sem=recv_sems.at[rel - 1], device_id=(px, py, pc),
                            device_id_type=MESH)
                sends.append(pltpu.make_async_remote_copy(src_ref=src, dst_ref=slots.at[my_id], **peer))
                landings.append(pltpu.make_async_remote_copy(src_ref=src, dst_ref=slots.at[4 * px + 2 * py + pc], **peer))
            return pltpu.make_async_copy(src, slots.at[my_id], local_sem), sends, landings

        def begin():
            local, sends, _ = copies()
            local.start()
            for cp in sends:
                cp.start()

        def end():
            local, sends, landings = copies()
            for cp in landings:
                cp.wait_recv()
            for cp in sends:
                cp.wait_send()
            local.wait()

        return begin, None, end

    return Exchange([packed], [jax.ShapeDtypeStruct((N_DEV,) + packed.shape, packed.dtype)],
                    [pltpu.SemaphoreType.DMA((N_DEV - 1,)), pltpu.SemaphoreType.DMA((N_DEV - 1,)),
                     pltpu.SemaphoreType.DMA(())], stages, collective_id=3, peers=_all_others)


def small_adamw(slots, w, m, v):
    def body(r_ref, w_ref, m_ref, v_ref, g_out, d_out, m_out, v_out, loss_out):
        red = r_ref[0]
        for k in range(1, N_DEV):
            red = red + r_ref[k]
        wv = w_ref[...]
        lb = _lower_bound(jnp.concatenate([wv[5:6, :HGRN_WIDTH], wv[5:6, HGRN_WIDTH:]], axis=0))
        t = red[5:6, :HGRN_WIDTH] * lb * (1.0 - lb)
        row = lax.broadcasted_iota(jnp.int32, red.shape, 0)
        g = jnp.where(row == 5, jnp.concatenate([t, -t], axis=1), jnp.where(row >= 6, 0.0, red))
        g_out[...] = g
        d_out[...], m_out[...], v_out[...] = _adamw(wv, g, m_ref[...], v_ref[...])
        loss = jnp.sum(red[6:7, :], axis=-1, keepdims=True) * (0.5 / D_MODEL)
        loss_out[...] = jnp.broadcast_to(loss, loss_out.shape)

    return pl.pallas_call(
        body,
        name="small_adamw",
        in_specs=[_vmem_spec()] * 4,
        out_specs=[_vmem_spec()] * 5,
        out_shape=[jax.ShapeDtypeStruct(w.shape, F32)] * 4 + [jax.ShapeDtypeStruct((SUBLANES, LANES), F32)],
    )(slots, w, m, v)


def _pack_small(g1, gp, g_pre, g_post, an, hn, logits_or_dlb, extra=None):
    row5 = logits_or_dlb.reshape(1, -1)
    row5 = jnp.pad(row5, ((0, 0), (0, D_MODEL - row5.shape[1])))
    row6 = jnp.zeros((1, D_MODEL), F32) if extra is None else extra
    return jnp.concatenate([g1, gp, g_pre, g_post, jnp.concatenate([an, hn], axis=1), row5, row6,
                            jnp.zeros((1, D_MODEL), F32)], axis=0)


def _unpack_small(p):
    return dict(mix_pre_norm=p[0:1], mix_post_norm=p[1:2], mlp_pre_norm=p[2:3], mlp_post_norm=p[3:4],
                attn_out_norm=p[4:5, :ATTN_WIDTH], hgrn_out_norm=p[4:5, ATTN_WIDTH:],
                hgrn_lb_logits=p[5].reshape(2, HGRN_WIDTH))


BIG = ("w_in", "w_out", "w_ff1", "w_ff2")
ORDER = ("mix_pre_norm", "w_in", "attn_out_norm", "hgrn_lb_logits", "hgrn_out_norm", "w_out", "mix_post_norm",
         "mlp_pre_norm", "w_ff1", "w_ff2", "mlp_post_norm")


def kernel(x, mix_pre_norm, w_in, attn_out_norm, hgrn_lb_logits, hgrn_out_norm, w_out, mix_post_norm, mlp_pre_norm, w_ff1, w_ff2, mlp_post_norm, loss_target, m_mix_pre_norm, m_w_in, m_attn_out_norm, m_hgrn_lb_logits, m_hgrn_out_norm, m_w_out, m_mix_post_norm, m_mlp_pre_norm, m_w_ff1, m_w_ff2, m_mlp_post_norm, v_mix_pre_norm, v_w_in, v_attn_out_norm, v_hgrn_lb_logits, v_hgrn_out_norm, v_w_out, v_mix_post_norm, v_mlp_pre_norm, v_w_ff1, v_w_ff2, v_mlp_post_norm):
    w = dict(w_in=w_in[0], w_out=w_out[0], w_ff1=w_ff1[0], w_ff2=w_ff2[0])
    m = dict(w_in=m_w_in[0], w_out=m_w_out[0], w_ff1=m_w_ff1[0], w_ff2=m_w_ff2[0])
    v = dict(w_in=v_w_in[0], w_out=v_w_out[0], w_ff1=v_w_ff1[0], w_ff2=v_w_ff2[0])

    dx, big, small_slots = train_step(x[0], loss_target[0], mix_pre_norm, attn_out_norm, hgrn_lb_logits, hgrn_out_norm,
                                      mix_post_norm, mlp_pre_norm, mlp_post_norm, w, m, v)

    pack = lambda a, b, c2, d, e, f, g: _pack_small(a, b, c2, d, e, f, g)
    w_s = pack(mix_pre_norm, mix_post_norm, mlp_pre_norm, mlp_post_norm, attn_out_norm, hgrn_out_norm, hgrn_lb_logits)
    m_s = pack(m_mix_pre_norm, m_mix_post_norm, m_mlp_pre_norm, m_mlp_post_norm, m_attn_out_norm, m_hgrn_out_norm,
               m_hgrn_lb_logits)
    v_s = pack(v_mix_pre_norm, v_mix_post_norm, v_mlp_pre_norm, v_mlp_post_norm, v_attn_out_norm, v_hgrn_out_norm,
               v_hgrn_lb_logits)
    g_s, d_s, nm_s, nv_s, loss = small_adamw(small_slots, w_s, m_s, v_s)
    small_out = [_unpack_small(t) for t in (g_s, d_s, nm_s, nv_s)]

    outs = [loss[0, 0], dx[None]]
    for kind in range(4):
        for name in ORDER:
            outs.append(big[name][kind][None] if name in BIG else small_out[kind][name])
    return tuple(outs)
```

```python
import jax
import jax.numpy as jnp
from jax import lax
from jax.experimental import pallas as pl
from jax.experimental.pallas import tpu as pltpu

F32 = jnp.float32
BF16 = jnp.bfloat16

D_MODEL = 1024
ATTN_WIDTH = 512
ATTN_HEAD_DIM = 64
ATTN_HEADS = 8
ATTN_BLOCK = 128
DILATIONS = (1, 4, 16)
HGRN_WIDTH = 512
HGRN_HEADS = 4
HGRN_HEAD_DIM = 128
HGRN_CHUNK = 64
IN_PROJ_WIDTH = 3584
D_FF = 4096
RMS_EPS = 1e-6
N_DEV = 8
ADAM_LR = 0.001
ADAM_B1 = 0.9
ADAM_B2 = 0.999
ADAM_EPS = 1e-08
ADAM_WD = 0.01
ADAM_STEP = 10

SUBLANES = 8
LANES = 128
COLUMN_UNROLL = 8
HGRN_CHUNKS_PER_STEP = 2
SUB_BLOCK = 16
TOKEN_TILE = 512
ELEMENTWISE_ROWS = 1024
MLP_TILE = 256
PROJ_TILE = 512
VMEM_BYTES_V7X = 64 * 1024 * 1024
VMEM_LIMIT = VMEM_BYTES_V7X // 8 * 7
NEG_BIG = -1e30
MESH = pl.DeviceIdType.MESH


def _params(ride=None, **kw):
    if ride is not None:
        kw["collective_id"] = ride.collective_id
    return pltpu.CompilerParams(vmem_limit_bytes=VMEM_LIMIT, **kw)


def _vmem_spec():
    return pl.BlockSpec(memory_space=pltpu.VMEM)


def _dot(a, b):
    return jnp.dot(a, b, preferred_element_type=F32)


def _dot_nt(a, b):
    return lax.dot_general(a, b, (((1,), (1,)), ((), ())), preferred_element_type=F32)


def _dot_tn(a, b):
    return lax.dot_general(a, b, (((0,), (0,)), ((), ())), preferred_element_type=F32)


def _sigmoid(x):
    return 1.0 / (1.0 + jnp.exp(-x))


def _rms_fwd(x, gain, width):
    r = lax.rsqrt(jnp.sum(x * x, axis=-1, keepdims=True) * (1.0 / width) + RMS_EPS)
    return x * r * gain


def _rms_bwd(dy, x, gain, width):
    r = lax.rsqrt(jnp.sum(x * x, axis=-1, keepdims=True) * (1.0 / width) + RMS_EPS)
    xhat = x * r
    dxhat = dy * gain
    dx = r * (dxhat - xhat * (jnp.sum(dxhat * xhat, axis=-1, keepdims=True) * (1.0 / width)))
    return dx, dy * xhat


def _split3(x):
    hi = x.astype(BF16)
    r1 = x - hi.astype(F32)
    mid = r1.astype(BF16)
    lo = (r1 - mid.astype(F32)).astype(BF16)
    return hi, mid, lo


def _tri_sum(tri_bf16, x):
    hi, mid, lo = _split3(x)
    return _dot(tri_bf16, hi) + _dot(tri_bf16, mid) + _dot(tri_bf16, lo)


def _dilated_spec(d, tm, width):
    return pl.BlockSpec((d, tm // d, width), lambda i: (0, i, 0))


def _lane_blocks(ref, value):
    for c in range(ref.shape[0]):
        ref[c] = value[:, c * LANES:(c + 1) * LANES]


def _to_dilated(src_ref, dst_ref, d, tm, cast=None):
    for r in range(d):
        for c in range(src_ref.shape[0]):
            v = src_ref[c] if d == 1 else src_ref[c, pl.ds(r, tm // d, stride=d), :]
            dst_ref[r, :, c * LANES:(c + 1) * LANES] = v if cast is None else v.astype(cast)


def _from_dilated(src_ref, scratch_ref, d, tm):
    if d == 1:
        return src_ref[0].astype(F32)
    nblk = scratch_ref.shape[0]
    for r in range(d):
        for c in range(nblk):
            scratch_ref[c, pl.ds(r, tm // d, stride=d), :] = src_ref[r, :, c * LANES:(c + 1) * LANES].astype(F32)
    return jnp.concatenate([scratch_ref[c] for c in range(nblk)], axis=1)


def in_proj_fwd(x, g1, w_in_b, ride=None):
    s = x.shape[0]
    tm = PROJ_TILE
    qkv_w = 3 * ATTN_WIDTH
    hg_w = IN_PROJ_WIDTH - qkv_w

    def body(x_ref, g_ref, w_ref, hg_ref, h_ref, *rest):
        qkv_refs, qkv_scr = rest[:len(DILATIONS)], rest[len(DILATIONS)]
        h = _rms_fwd(x_ref[...], g_ref[...], D_MODEL).astype(BF16)
        h_ref[...] = h
        proj = _dot(h, w_ref[...])
        hg_ref[...] = proj[:, qkv_w:]
        _lane_blocks(qkv_scr, proj[:, :qkv_w])
        for d, ref in zip(DILATIONS, qkv_refs):
            _to_dilated(qkv_scr, ref, d, tm, cast=BF16)

    n_steps = s // tm
    step = lambda k: (lambda: pl.program_id(0) == k)
    e_in, e_out, e_shape, e_scr, e_args = _ride_specs(ride)
    return pl.pallas_call(
        _riding(body, 3, 2 + len(DILATIONS), 1, ride, step(0), step(n_steps // 2), step(n_steps - 1),
                late=step(n_steps - 2)),
        name="in_proj_fwd",
        grid=(n_steps,),
        in_specs=[
            pl.BlockSpec((tm, D_MODEL), lambda i: (i, 0)),
            pl.BlockSpec((1, D_MODEL), lambda i: (0, 0)),
            _vmem_spec(),
        ] + e_in,
        out_specs=[
            pl.BlockSpec((tm, hg_w), lambda i: (i, 0)),
            pl.BlockSpec((tm, D_MODEL), lambda i: (i, 0)),
        ] + [_dilated_spec(d, tm, qkv_w) for d in DILATIONS] + e_out,
        out_shape=[jax.ShapeDtypeStruct((s, hg_w), F32), jax.ShapeDtypeStruct((s, D_MODEL), BF16)] + [
            jax.ShapeDtypeStruct((d, s // d, qkv_w), BF16) for d in DILATIONS] + e_shape,
        scratch_shapes=[pltpu.VMEM((qkv_w // LANES, tm, LANES), F32)] + e_scr,
        compiler_params=_params(ride, dimension_semantics=("arbitrary",)),
    )(x, g1, w_in_b, *e_args)


ATTN_SCALE = ATTN_HEAD_DIM ** -0.5


def _fill_attn_bias(bias_ref, dilation):
    qi = lax.broadcasted_iota(jnp.int32, (ATTN_BLOCK, 2 * ATTN_BLOCK), 0)
    kj = lax.broadcasted_iota(jnp.int32, (ATTN_BLOCK, 2 * ATTN_BLOCK), 1)
    dist = qi + ATTN_BLOCK - kj
    valid = (dist >= 0) & (dist <= ATTN_BLOCK)
    for head in range(ATTN_HEADS):
        slope = 2.0 ** (-8.0 * (head + 1) / ATTN_HEADS)
        bias = jnp.where(valid, dist.astype(F32) * (-slope * dilation), NEG_BIG)
        bias_ref[0, head] = bias
        bias_ref[1, head] = jnp.where(kj >= ATTN_BLOCK, bias, NEG_BIG)


def _stack_heads(x):
    low = _lane_half(x.shape, 0)
    zero = jnp.zeros_like(x)
    return jnp.concatenate([jnp.where(low, x, zero), jnp.where(low, zero, x)], axis=0)


def _unstack_heads(y):
    half = y.shape[0] // 2
    return jnp.where(_lane_half((half, y.shape[1]), 0), y[:half], y[half:])


def _attn_scores(q_stack, kcat, bias_ref, pair, first_block):
    f = first_block.astype(jnp.int32)
    bias = jnp.concatenate([bias_ref[f, 2 * pair], bias_ref[f, 2 * pair + 1]], axis=0)
    return _dot_nt(q_stack, kcat) + bias


def _lane_half(shape, sub):
    lane = lax.broadcasted_iota(jnp.int32, shape, 1)
    return (lane < ATTN_HEAD_DIM) if sub == 0 else (lane >= ATTN_HEAD_DIM)


def _sub_block(col, row):
    return pl.BlockSpec((None, ATTN_BLOCK, ATTN_WIDTH), lambda r, n: (r, row(n), col))


def attn_fwd(qkv, dilation):
    d, length, _ = qkv.shape
    assert d == dilation
    nb = length // ATTN_BLOCK

    def body(q_ref, kc_ref, kp_ref, vc_ref, vp_ref, o_ref, lse_ref, bias_ref):
        @pl.when((pl.program_id(0) == 0) & (pl.program_id(1) == 0))
        def _():
            _fill_attn_bias(bias_ref, d)

        first = pl.program_id(1) == 0
        for pair in range(ATTN_HEADS // 2):
            lanes = slice(pair * LANES, (pair + 1) * LANES)
            q_stack = _stack_heads(q_ref[:, lanes] * ATTN_SCALE)
            kcat = jnp.concatenate([kp_ref[:, lanes], kc_ref[:, lanes]], axis=0)
            vcat = jnp.concatenate([vp_ref[:, lanes], vc_ref[:, lanes]], axis=0)
            sc = _attn_scores(q_stack, kcat, bias_ref, pair, first)
            m = jnp.max(sc, axis=-1, keepdims=True)
            p = jnp.exp(sc - m)
            den = jnp.sum(p, axis=-1, keepdims=True)
            o_ref[:, lanes] = _unstack_heads(_dot(p.astype(BF16), vcat) / den).astype(BF16)
            lse_ref[:, lanes] = _unstack_heads(jnp.broadcast_to(m + jnp.log(den), (2 * ATTN_BLOCK, LANES)))

    cur = lambda n: n
    prev = lambda n: jnp.maximum(n - 1, 0)
    return pl.pallas_call(
        body,
        name=f"attn_fwd_d{d}",
        grid=(d, nb),
        in_specs=[_sub_block(0, cur), _sub_block(1, cur), _sub_block(1, prev), _sub_block(2, cur), _sub_block(2, prev)],
        out_specs=[_sub_block(0, cur), _sub_block(0, cur)],
        out_shape=[jax.ShapeDtypeStruct((d, length, ATTN_WIDTH), BF16), jax.ShapeDtypeStruct((d, length, ATTN_WIDTH), F32)],
        scratch_shapes=[pltpu.VMEM((2, ATTN_HEADS, ATTN_BLOCK, 2 * ATTN_BLOCK), F32)],
        compiler_params=_params(dimension_semantics=("arbitrary", "arbitrary")),
    )(qkv, qkv, qkv, qkv, qkv)


def attn_bwd(qkv, d_out, lse, delta, dilation, ride=None):
    d, length, _ = qkv.shape
    assert d == dilation
    nb = length // ATTN_BLOCK

    steps = d * nb + 1

    def body(q_ref, kc_ref, kp_ref, vc_ref, vp_ref, do_ref, lse_ref, dl_ref, dq_ref, dk_ref, dv_ref, ck_ref, cv_ref,
             bias_ref):
        t = pl.program_id(0)

        @pl.when(t == 0)
        def _():
            ck_ref[...] = jnp.zeros_like(ck_ref)
            cv_ref[...] = jnp.zeros_like(cv_ref)
            _fill_attn_bias(bias_ref, d)

        @pl.when(t < steps - 1)
        def _():
            first = t % nb == 0
            for pair in range(ATTN_HEADS // 2):
                lanes = slice(pair * LANES, (pair + 1) * LANES)
                q_stack = _stack_heads(q_ref[:, lanes] * ATTN_SCALE)
                do_stack = _stack_heads(do_ref[:, lanes])
                kcat = jnp.concatenate([kp_ref[:, lanes], kc_ref[:, lanes]], axis=0)
                vcat = jnp.concatenate([vp_ref[:, lanes], vc_ref[:, lanes]], axis=0)
                col_a, col_b = 2 * pair, 2 * pair + 1
                lse_col = jnp.concatenate([lse_ref[:, col_a:col_a + 1], lse_ref[:, col_b:col_b + 1]], axis=0)
                dl_col = jnp.concatenate([dl_ref[:, col_a:col_a + 1], dl_ref[:, col_b:col_b + 1]], axis=0)
                p = jnp.exp(_attn_scores(q_stack, kcat, bias_ref, pair, first) - lse_col)
                ds = (p * (_dot_nt(do_stack, vcat) - dl_col)).astype(BF16)
                dq_ref[:, lanes] = (_unstack_heads(_dot(ds, kcat)) * ATTN_SCALE).astype(BF16)
                dk_cat = _dot_tn(ds, q_stack)
                dv_cat = _dot_tn(p.astype(BF16), do_stack)
                dk_ref[:, lanes] = (ck_ref[:, lanes] + dk_cat[:ATTN_BLOCK]).astype(BF16)
                dv_ref[:, lanes] = (cv_ref[:, lanes] + dv_cat[:ATTN_BLOCK]).astype(BF16)
                ck_ref[:, lanes] = dk_cat[ATTN_BLOCK:]
                cv_ref[:, lanes] = dv_cat[ATTN_BLOCK:]

        @pl.when(t == steps - 1)
        def _():
            dk_ref[...] = ck_ref[...].astype(BF16)
            dv_ref[...] = cv_ref[...].astype(BF16)

    blk = (ATTN_BLOCK, ATTN_WIDTH)

    def spec(col, shift, width=ATTN_WIDTH):
        def index(t):
            f = jnp.minimum(t, steps - 2) if shift > -2 else jnp.maximum(t - 1, 0)
            r, n = f // nb, f % nb
            return (r, jnp.maximum(n - 1, 0) if shift == -1 else n, col)
        return pl.BlockSpec((None, ATTN_BLOCK, width), index)

    step = lambda k: (lambda: pl.program_id(0) == k)
    e_in, e_out, e_shape, e_scr, e_args = _ride_specs(ride)
    return pl.pallas_call(
        _riding(body, 8, 3, 3, ride, step(0), step(steps // 2), step(steps - 1)),
        name=f"attn_bwd_d{d}",
        grid=(steps,),
        in_specs=[spec(0, 0), spec(1, 0), spec(1, -1), spec(2, 0), spec(2, -1), spec(0, 0), spec(0, 0, LANES),
                  spec(0, 0, LANES)] + e_in,
        out_specs=[spec(0, 0), spec(0, -2), spec(0, -2)] + e_out,
        out_shape=[jax.ShapeDtypeStruct((d, length, ATTN_WIDTH), BF16)] * 3 + e_shape,
        scratch_shapes=[pltpu.VMEM(blk, F32), pltpu.VMEM(blk, F32),
                        pltpu.VMEM((2, ATTN_HEADS, ATTN_BLOCK, 2 * ATTN_BLOCK), F32)] + e_scr,
        compiler_params=_params(ride, dimension_semantics=("arbitrary",)),
    )(qkv, qkv, qkv, qkv, qkv, d_out, lse, delta, *e_args)


def _lower_bound(logits):
    return _sigmoid(logits[0:1, :] - logits[1:2, :])


def _hgrn_gates(q, fp, lb):
    sq = _sigmoid(q)
    qf = q * sq
    sig = _sigmoid(fp)
    sig_neg = _sigmoid(-fp)
    kf = (1.0 - lb) * sig_neg
    log_sig = jnp.minimum(fp, 0.0) - jnp.log(1.0 + jnp.exp(-jnp.abs(fp)))
    a = jnp.log(lb)
    c = jnp.log(1.0 - lb) + log_sig
    log_f = jnp.maximum(a, c) + jnp.log(1.0 + jnp.exp(-jnp.abs(a - c)))
    return sq, qf, (sig, sig_neg, c), log_f, kf


def _tril_bf16(n, upper=False):
    r = lax.broadcasted_iota(jnp.int32, (n, n), 0)
    c = lax.broadcasted_iota(jnp.int32, (n, n), 1)
    keep = (c >= r) if upper else (c <= r)
    return jnp.where(keep, 1.0, 0.0).astype(BF16)


def _hgrn_diagonal_loops(c_len, diagonal):
    for half in range(SUB_BLOCK // SUBLANES):
        def step(jj, carry, half=half):
            j = half * SUBLANES + jj
            for i in range(c_len // SUB_BLOCK):
                diagonal(slice(i * SUB_BLOCK + half * SUBLANES, (i + 1) * SUB_BLOCK), j, i * SUB_BLOCK + j)
            return carry

        lax.fori_loop(0, SUBLANES, step, 0, unroll=COLUMN_UNROLL)


def _hgrn_off_diagonal(b, qf, kf):
    c_len, width = b.shape
    edges = [b[0:1, :]] + [b[i * SUB_BLOCK - 1:i * SUB_BLOCK, :] for i in range(1, c_len // SUB_BLOCK)]
    eq = jnp.exp(b - jnp.concatenate([jnp.broadcast_to(e, (SUB_BLOCK, width)) for e in edges], axis=0))
    q_til = qf * eq
    k_til, ek = [], []
    for i in range(1, c_len // SUB_BLOCK):
        n = i * SUB_BLOCK
        e = jnp.exp(edges[i] - b[:n, :])
        ek.append(e)
        k_til.append(jnp.concatenate([kf[:n, :] * e, jnp.zeros((2 * c_len - n, width), F32)], axis=0))
    return q_til, k_til, eq, ek


def _split2(x):
    hi = x.astype(BF16)
    return hi, (x - hi.astype(F32)).astype(BF16)


def hgrn_fwd(proj, lb, ride=None):
    s = proj.shape[0]
    c_len, nh, hd = HGRN_CHUNK, HGRN_HEADS, HGRN_HEAD_DIM
    n_chunks = s // c_len
    col0 = 0

    cps = 2 * HGRN_CHUNKS_PER_STEP
    n_steps = n_chunks // cps

    def body(q_ref, f_ref, i_ref, lb_ref, o_ref, st_out_ref, a_out_ref, st_ref, b_ref, qf_ref, kf_ref, a_ref):
        @pl.when(pl.program_id(0) == 0)
        def _():
            st_ref[...] = jnp.zeros_like(st_ref)

        lbv = _lower_bound(lb_ref[...])
        for u in range(cps):
            rs = slice(u * c_len, (u + 1) * c_len)
            b_u, qf_u, kf_u, a_u = b_ref.at[u], qf_ref.at[u], kf_ref.at[u], a_ref.at[u]
            _, qf, _, log_f, kf = _hgrn_gates(q_ref[rs, :], f_ref[rs, :], lbv)
            b = _tri_sum(_tril_bf16(c_len), log_f)
            b_u[...] = b
            qf_u[...] = qf
            kf_u[...] = kf
            a_u[...] = jnp.zeros_like(a_u)

            def diagonal(rows, j, key, b_u=b_u, qf_u=qf_u, kf_u=kf_u, a_u=a_u):
                bj = b_u[pl.ds(key, 1), :]
                kj = kf_u[pl.ds(key, 1), :]
                nrow = rows.stop - rows.start
                t_loc = lax.broadcasted_iota(jnp.int32, (nrow, nh * hd), 0) + (rows.start % SUB_BLOCK)
                e = jnp.exp(jnp.where(t_loc >= j, b_u[rows, :] - bj, NEG_BIG))
                prod = qf_u[rows, :] * kj * e
                lane = lax.broadcasted_iota(jnp.int32, (nrow, hd), 1)
                for h in range(nh):
                    col = jnp.sum(prod[:, h * hd:(h + 1) * hd], axis=-1, keepdims=True)
                    a_u[h, rows, :] = jnp.where(lane == key, col, a_u[h, rows, :])

            _hgrn_diagonal_loops(c_len, diagonal)
            q_til, k_til, _, _ = _hgrn_off_diagonal(b, qf, kf)
            q_til = q_til.astype(BF16)
            k_til = [k.astype(BF16) for k in k_til]

            b_last = b[c_len - 1:c_len, :]
            qb = (qf * jnp.exp(b)).astype(BF16)
            kb2 = (kf * jnp.exp(b_last - b)).astype(BF16)
            vf = i_ref[rs, :].astype(BF16)
            for h in range(nh):
                hs = slice(h * hd, (h + 1) * hd)
                st = st_ref[h]
                st_out_ref[u, h] = st
                off = [jnp.zeros((SUB_BLOCK, hd), F32)]
                for i in range(1, c_len // SUB_BLOCK):
                    off.append(_dot_nt(q_til[i * SUB_BLOCK:(i + 1) * SUB_BLOCK, hs], k_til[i - 1][:, hs]))
                a_h = a_u[h] + jnp.concatenate(off, axis=0)
                a_out_ref[rs, hs] = a_h
                o_ref[rs, hs] = _dot_nt(qb[:, hs], st.astype(BF16)) + _dot(a_h[:, :c_len].astype(BF16), vf[:, hs])
                st_ref[h] = st * jnp.exp(b_last[:, hs]) + _dot_tn(vf[:, hs], kb2[:, hs])

    blk = (cps * c_len, HGRN_WIDTH)
    sblk = (cps, c_len, HGRN_WIDTH)
    step = lambda k: (lambda: pl.program_id(0) == k)
    e_in, e_out, e_shape, e_scr, e_args = _ride_specs(ride)
    return pl.pallas_call(
        _riding(body, 4, 3, 5, ride, step(0), step(n_steps // 2), step(n_steps - 1), late=step((3 * n_steps) // 4)),
        name="hgrn_fwd",
        grid=(n_steps,),
        in_specs=[
            pl.BlockSpec(blk, lambda c: (c, col0)),
            pl.BlockSpec(blk, lambda c: (c, col0 + 1)),
            pl.BlockSpec(blk, lambda c: (c, col0 + 2)),
            pl.BlockSpec((2, HGRN_WIDTH), lambda c: (0, 0)),
        ] + e_in,
        out_specs=[
            pl.BlockSpec(blk, lambda c: (c, 0)),
            pl.BlockSpec((cps, nh, hd, hd), lambda c: (c, 0, 0, 0)),
            pl.BlockSpec(blk, lambda c: (c, 0)),
        ] + e_out,
        out_shape=[
            jax.ShapeDtypeStruct((s, HGRN_WIDTH), F32),
            jax.ShapeDtypeStruct((n_chunks, nh, hd, hd), F32),
            jax.ShapeDtypeStruct((s, nh * hd), F32),
        ] + e_shape,
        scratch_shapes=[
            pltpu.VMEM((nh, hd, hd), F32),
            pltpu.VMEM(sblk, F32),
            pltpu.VMEM(sblk, F32),
            pltpu.VMEM(sblk, F32),
            pltpu.VMEM((cps, nh, c_len, hd), F32),
        ] + e_scr,
        compiler_params=_params(ride, dimension_semantics=("arbitrary",)),
    )(proj, proj, proj, lb, *e_args)


def hgrn_bwd(proj, lb, d_o, states, a_mat, ride=None):
    s = proj.shape[0]
    c_len, nh, hd = HGRN_CHUNK, HGRN_HEADS, HGRN_HEAD_DIM
    n_chunks = s // c_len
    col0 = 0
    cps = HGRN_CHUNKS_PER_STEP
    n_steps = n_chunks // cps
    last = n_steps - 1

    def body(q_ref, f_ref, i_ref, lb_ref, do_ref, st_in_ref, a_in_ref, dq_ref, df_ref, di_ref, dlb_ref,
             dst_ref, b_ref, qf_ref, kf_ref, da_ref, dqi_ref, dki_ref):
        @pl.when(pl.program_id(0) == 0)
        def _():
            dst_ref[...] = jnp.zeros_like(dst_ref)
            dlb_ref[...] = jnp.zeros_like(dlb_ref)

        lbv = _lower_bound(lb_ref[...])
        for u in reversed(range(cps)):
            rs = slice(u * c_len, (u + 1) * c_len)
            b_u, qf_u, kf_u, da_u, dqi_u, dki_u = (b_ref.at[u], qf_ref.at[u], kf_ref.at[u], da_ref.at[u], dqi_ref.at[u],
                                                   dki_ref.at[u])
            q = q_ref[rs, :]
            sq, qf, (sig, sig_neg, log_c), log_f, kf = _hgrn_gates(q, f_ref[rs, :], lbv)
            b = _tri_sum(_tril_bf16(c_len), log_f)
            b_u[...] = b
            qf_u[...] = qf
            kf_u[...] = kf
            b_last = b[c_len - 1:c_len, :]
            eb = jnp.exp(b)
            ebl = jnp.exp(b_last - b)
            qb = qf * eb
            kb2 = kf * ebl
            vf = i_ref[rs, :]
            d_o = do_ref[rs, :]
            qb_b, kb2_b, vf_b, do_b = qb.astype(BF16), kb2.astype(BF16), vf.astype(BF16), d_o.astype(BF16)
            tq = lax.broadcasted_iota(jnp.int32, (c_len, hd), 0)
            lane = lax.broadcasted_iota(jnp.int32, (c_len, hd), 1)

            dqb_parts, dvf_parts, dkb2_parts, dbl_parts = [], [], [], []
            for h in range(nh):
                hs = slice(h * hd, (h + 1) * hd)
                st = st_in_ref[u, h]
                dst = dst_ref[h]
                st_b, dst_b = st.astype(BF16), dst.astype(BF16)
                a_h = a_in_ref[rs, hs][:, :c_len].astype(BF16)
                dqb_parts.append(_dot(do_b[:, hs], st_b))
                dvf_parts.append(_dot_tn(a_h, do_b[:, hs]) + _dot_nt(kb2_b[:, hs], dst_b))
                dkb2_parts.append(_dot(vf_b[:, hs], dst_b))
                da = _dot_nt(do_b[:, hs], vf_b[:, hs])
                da = jnp.concatenate([da, jnp.zeros((c_len, hd - c_len), F32)], axis=1)
                da_u[h] = jnp.where(tq >= lane, da, 0.0)
                dbl_parts.append(jnp.sum(dst * st, axis=0, keepdims=True) * jnp.exp(b_last[:, hs]))
                dst_ref[h] = dst * jnp.exp(b_last[:, hs]) + _dot_tn(do_b[:, hs], qb_b[:, hs])
            dqb = jnp.concatenate(dqb_parts, axis=1)
            dvf = jnp.concatenate(dvf_parts, axis=1)
            dkb2 = jnp.concatenate(dkb2_parts, axis=1)
            dbl = jnp.concatenate(dbl_parts, axis=1) + jnp.sum(dkb2 * kb2, axis=0, keepdims=True)

            dqi_u[...] = jnp.zeros_like(dqi_u)
            t_idx = lax.broadcasted_iota(jnp.int32, (c_len, nh * hd), 0)

            def diagonal(rows, j, key, b_u=b_u, qf_u=qf_u, kf_u=kf_u, da_u=da_u, dqi_u=dqi_u, dki_u=dki_u):
                bj = b_u[pl.ds(key, 1), :]
                kj = kf_u[pl.ds(key, 1), :]
                nrow = rows.stop - rows.start
                t_loc = lax.broadcasted_iota(jnp.int32, (nrow, nh * hd), 0) + (rows.start % SUB_BLOCK)
                e = jnp.exp(jnp.where(t_loc >= j, b_u[rows, :] - bj, NEG_BIG))
                lane_r = lax.broadcasted_iota(jnp.int32, (nrow, hd), 1)
                cols = [jnp.sum(jnp.where(lane_r == key, da_u[h, rows, :], 0.0), axis=-1, keepdims=True)
                        for h in range(nh)]
                w = e * jnp.concatenate([jnp.broadcast_to(cc, (nrow, hd)) for cc in cols], axis=1)
                dqi_u[rows, :] += w * kj
                dki_u[pl.ds(key, 1), :] = jnp.sum(w * qf_u[rows, :], axis=0, keepdims=True)

            _hgrn_diagonal_loops(c_len, diagonal)

            q_til, k_til, eq, ek = _hgrn_off_diagonal(b, qf, kf)
            q_hi, q_lo = _split2(q_til)
            k_pairs = [_split2(k) for k in k_til]
            n_sub = c_len // SUB_BLOCK
            dq_heads, dk_heads = [], []
            for h in range(nh):
                hs = slice(h * hd, (h + 1) * hd)
                dq_rows = [jnp.zeros((SUB_BLOCK, hd), F32)]
                dk_h = jnp.zeros((c_len, hd), F32)
                for i in range(1, n_sub):
                    rows = slice(i * SUB_BLOCK, (i + 1) * SUB_BLOCK)
                    n = i * SUB_BLOCK
                    da_i = da_u[h, rows, :].astype(BF16)
                    k_hi, k_lo = k_pairs[i - 1]
                    dq_rows.append((_dot(da_i, k_hi[:, hs]) + _dot(da_i, k_lo[:, hs])) * eq[rows, hs])
                    dk_t = (_dot_tn(da_i, q_hi[rows, hs]) + _dot_tn(da_i, q_lo[rows, hs]))[:n, :] * ek[i - 1][:, hs]
                    dk_h = dk_h + jnp.concatenate([dk_t, jnp.zeros((c_len - n, hd), F32)], axis=0)
                dq_heads.append(jnp.concatenate(dq_rows, axis=0))
                dk_heads.append(dk_h)
            dq_intra = dqi_u[...] + jnp.concatenate(dq_heads, axis=1)
            dk_intra = dki_u[...] + jnp.concatenate(dk_heads, axis=1)

            db = dqb * qb + qf * dq_intra - kf * dk_intra - dkb2 * kb2
            db = db + jnp.where(t_idx == c_len - 1, dbl, 0.0)
            dg = _tri_sum(_tril_bf16(c_len, upper=True), db)
            dqf = dqb * eb + dq_intra
            dkf = dkb2 * ebl + dk_intra
            dq_ref[rs, :] = (dqf * (sq * (1.0 + q * (1.0 - sq)))).astype(BF16)
            df_ref[rs, :] = (sig_neg * (dg * jnp.exp(log_c - log_f) - dkf * (1.0 - lbv) * sig)).astype(BF16)
            di_ref[rs, :] = dvf.astype(BF16)
            dlb_ref[...] += jnp.sum(sig_neg * (dg * jnp.exp(-log_f) - dkf), axis=0, keepdims=True)

    blk = (cps * c_len, HGRN_WIDTH)
    sblk = (cps, c_len, HGRN_WIDTH)
    rev = lambda c: last - c
    step = lambda k: (lambda: pl.program_id(0) == k)
    e_in, e_out, e_shape, e_scr, e_args = _ride_specs(ride)
    return pl.pallas_call(
        _riding(body, 7, 4, 7, ride, step(0), step(n_steps // 2), step(last)),
        name="hgrn_bwd",
        grid=(n_steps,),
        in_specs=[
            pl.BlockSpec(blk, lambda c: (rev(c), col0)),
            pl.BlockSpec(blk, lambda c: (rev(c), col0 + 1)),
            pl.BlockSpec(blk, lambda c: (rev(c), col0 + 2)),
            pl.BlockSpec((2, HGRN_WIDTH), lambda c: (0, 0)),
            pl.BlockSpec(blk, lambda c: (rev(c), 0)),
            pl.BlockSpec((cps, nh, hd, hd), lambda c: (rev(c), 0, 0, 0)),
            pl.BlockSpec(blk, lambda c: (rev(c), 0)),
        ] + e_in,
        out_specs=[
            pl.BlockSpec(blk, lambda c: (rev(c), 0)),
            pl.BlockSpec(blk, lambda c: (rev(c), 0)),
            pl.BlockSpec(blk, lambda c: (rev(c), 0)),
            pl.BlockSpec((1, HGRN_WIDTH), lambda c: (0, 0)),
        ] + e_out,
        out_shape=[jax.ShapeDtypeStruct((s, HGRN_WIDTH), BF16)] * 3 + [jax.ShapeDtypeStruct((1, HGRN_WIDTH), F32)] + e_shape,
        scratch_shapes=[
            pltpu.VMEM((nh, hd, hd), F32),
            pltpu.VMEM(sblk, F32),
            pltpu.VMEM(sblk, F32),
            pltpu.VMEM(sblk, F32),
            pltpu.VMEM((cps, nh, c_len, hd), F32),
            pltpu.VMEM(sblk, F32),
            pltpu.VMEM(sblk, F32),
        ] + e_scr,
        compiler_params=_params(ride, dimension_semantics=("arbitrary",)),
    )(proj, proj, proj, lb, d_o, states, a_mat, *e_args)


def _per_head_lanes(x):
    lane = lax.broadcasted_iota(jnp.int32, (x.shape[0], LANES), 1)
    out = jnp.zeros((x.shape[0], LANES), F32)
    for h in range(ATTN_HEADS):
        out = jnp.where(lane == h, x[:, h * ATTN_HEAD_DIM:h * ATTN_HEAD_DIM + 1], out)
    return out


def _row_spec(tm, width, col=0):
    return pl.BlockSpec((tm, width), lambda i: (i, col))


def _const_spec(width):
    return pl.BlockSpec((1, width), lambda i: (0, 0))


def _acc_rows(ref, value):
    @pl.when(pl.program_id(0) == 0)
    def _():
        ref[...] = jnp.zeros_like(ref)

    ref[...] += jnp.sum(value, axis=0, keepdims=True)


def mix_fwd(attn_parts, o_h, proj, an, hn, w_out_b, gp, x, ride=None):
    s = x.shape[0]
    tm = TOKEN_TILE
    gate_col = 3
    hd = HGRN_HEAD_DIM
    nd = len(DILATIONS)

    def body(*refs):
        o_refs, l_refs = refs[:nd], refs[nd:2 * nd]
        oh_ref, gate_ref, an_ref, hn_ref, w_ref, gp_ref, x_ref = refs[2 * nd:2 * nd + 7]
        x1_ref, cat_ref, mixed_ref, attn_ref = refs[2 * nd + 7:2 * nd + 11]
        lse_refs = refs[2 * nd + 11:3 * nd + 11]
        o_scr, l_scr, lse_scr = refs[3 * nd + 11:]
        os_ = [_from_dilated(r, o_scr.at[k], d, tm) for k, (r, d) in enumerate(zip(o_refs, DILATIONS))]
        ls = [_from_dilated(r, l_scr.at[k], d, tm) for k, (r, d) in enumerate(zip(l_refs, DILATIONS))]
        m = jnp.maximum(jnp.maximum(ls[0], ls[1]), ls[2])
        es = [jnp.exp(l - m) for l in ls]
        den = es[0] + es[1] + es[2]
        attn = (es[0] * os_[0] + es[1] * os_[1] + es[2] * os_[2]) / den
        attn_ref[...] = attn
        lse_scr[0] = _per_head_lanes(m + jnp.log(den))
        for d, ref in zip(DILATIONS, lse_refs):
            _to_dilated(lse_scr, ref, d, tm)
        cat_ref[:, :ATTN_WIDTH] = _rms_fwd(attn, an_ref[...], ATTN_WIDTH).astype(BF16)
        gate = gate_ref[...]
        silu_g = gate * _sigmoid(gate)
        for h in range(HGRN_HEADS):
            hs = slice(h * hd, (h + 1) * hd)
            rec = _rms_fwd(oh_ref[:, hs], hn_ref[:, hs], hd) * silu_g[:, hs]
            cat_ref[:, ATTN_WIDTH + h * hd:ATTN_WIDTH + (h + 1) * hd] = rec.astype(BF16)
        mixed = _dot(cat_ref[...], w_ref[...])
        mixed_ref[...] = mixed
        x1_ref[...] = x_ref[...] + _rms_fwd(mixed, gp_ref[...], D_MODEL)

    aw = ATTN_WIDTH
    n_steps = s // tm
    step = lambda k: (lambda: pl.program_id(0) == k)
    e_in, e_out, e_shape, e_scr, e_args = _ride_specs(ride)
    return pl.pallas_call(
        _riding(body, 2 * nd + 7, 4 + nd, 3, ride, step(0), step((13 * n_steps) // 16), step(n_steps - 1)),
        name="mix_fwd",
        grid=(n_steps,),
        in_specs=[_dilated_spec(d, tm, aw) for d in DILATIONS] * 2 + [
            _row_spec(tm, aw), _row_spec(tm, aw, gate_col), _const_spec(aw), _const_spec(aw), _vmem_spec(),
            _const_spec(D_MODEL), _row_spec(tm, D_MODEL)] + e_in,
        out_specs=[_row_spec(tm, D_MODEL), _row_spec(tm, D_MODEL), _row_spec(tm, D_MODEL), _row_spec(tm, aw)] + [
            _dilated_spec(d, tm, LANES) for d in DILATIONS] + e_out,
        out_shape=[
            jax.ShapeDtypeStruct((s, D_MODEL), F32),
            jax.ShapeDtypeStruct((s, D_MODEL), BF16),
            jax.ShapeDtypeStruct((s, D_MODEL), F32),
            jax.ShapeDtypeStruct((s, aw), F32),
        ] + [jax.ShapeDtypeStruct((d, s // d, LANES), F32) for d in DILATIONS] + e_shape,
        scratch_shapes=[pltpu.VMEM((nd, aw // LANES, tm, LANES), F32), pltpu.VMEM((nd, aw // LANES, tm, LANES), F32),
                        pltpu.VMEM((1, tm, LANES), F32)] + e_scr,
        compiler_params=_params(ride, dimension_semantics=("arbitrary",)),
    )(*[p[0] for p in attn_parts], *[p[1] for p in attn_parts], o_h, proj, an, hn, w_out_b, gp, x, *e_args)


def mix_bwd(dx1, mixed, gp, w_out_b, attn, an, o_h, proj, hn):
    s = dx1.shape[0]
    tm = TOKEN_TILE
    gate_col = 3
    hd = HGRN_HEAD_DIM
    aw = ATTN_WIDTH

    nd = len(DILATIONS)

    def body(*refs):
        dx1_ref, mixed_ref, gp_ref, w_ref, attn_ref, an_ref, oh_ref, gate_ref, hn_ref, dmix_ref = refs[:10]
        do_refs, delta_refs = refs[10:10 + nd], refs[10 + nd:10 + 2 * nd]
        doh_ref, dgate_ref, dgp_ref, dan_ref, dhn_ref, do_ref, delta_ref = refs[10 + 2 * nd:]
        dmixed, gp_c = _rms_bwd(dx1_ref[...], mixed_ref[...], gp_ref[...], D_MODEL)
        _acc_rows(dgp_ref, gp_c)
        dmixed_b = dmixed.astype(BF16)
        dmix_ref[...] = dmixed_b
        dcat = _dot_nt(dmixed_b, w_ref[...])
        attn = attn_ref[...]
        d_o, an_c = _rms_bwd(dcat[:, :aw], attn, an_ref[...], aw)
        _acc_rows(dan_ref, an_c)
        _lane_blocks(do_ref, d_o)
        prod = d_o * attn
        lane = lax.broadcasted_iota(jnp.int32, (tm, LANES), 1)
        delta = jnp.zeros((tm, LANES), F32)
        for pair in range(ATTN_HEADS // 2):
            pp = prod[:, pair * LANES:(pair + 1) * LANES]
            low = _lane_half((tm, LANES), 0)
            lo = jnp.sum(jnp.where(low, pp, 0.0), axis=-1, keepdims=True)
            hi = jnp.sum(jnp.where(low, 0.0, pp), axis=-1, keepdims=True)
            delta = jnp.where(lane == 2 * pair, lo, jnp.where(lane == 2 * pair + 1, hi, delta))
        delta_ref[0] = delta
        for d, o_ref, l_ref in zip(DILATIONS, do_refs, delta_refs):
            _to_dilated(do_ref, o_ref, d, tm, cast=BF16)
            _to_dilated(delta_ref, l_ref, d, tm)
        gate = gate_ref[...]
        sg = _sigmoid(gate)
        silu_g = gate * sg
        drec = dcat[:, aw:]
        hn_parts = []
        for h in range(HGRN_HEADS):
            hs = slice(h * hd, (h + 1) * hd)
            oh = oh_ref[:, hs]
            on = _rms_fwd(oh, hn_ref[:, hs], hd)
            dgate_ref[:, hs] = (drec[:, hs] * on * (sg[:, hs] * (1.0 + gate[:, hs] * (1.0 - sg[:, hs])))).astype(BF16)
            d_oh, hn_c = _rms_bwd(drec[:, hs] * silu_g[:, hs], oh, hn_ref[:, hs], hd)
            doh_ref[:, hs] = d_oh
            hn_parts.append(hn_c)
        _acc_rows(dhn_ref, jnp.concatenate(hn_parts, axis=1))

    return pl.pallas_call(
        body,
        name="mix_bwd",
        grid=(s // tm,),
        in_specs=[_row_spec(tm, D_MODEL), _row_spec(tm, D_MODEL), _const_spec(D_MODEL), _vmem_spec(), _row_spec(tm, aw),
                  _const_spec(aw), _row_spec(tm, aw), _row_spec(tm, aw, gate_col), _const_spec(aw)],
        out_specs=[_row_spec(tm, D_MODEL)] + [_dilated_spec(d, tm, aw) for d in DILATIONS] + [
            _dilated_spec(d, tm, LANES) for d in DILATIONS] + [_row_spec(tm, aw)] * 2 + [
            _const_spec(D_MODEL), _const_spec(aw), _const_spec(aw)],
        out_shape=[jax.ShapeDtypeStruct((s, D_MODEL), BF16)] + [
            jax.ShapeDtypeStruct((d, s // d, aw), BF16) for d in DILATIONS] + [
            jax.ShapeDtypeStruct((d, s // d, LANES), F32) for d in DILATIONS] + [
            jax.ShapeDtypeStruct((s, aw), F32), jax.ShapeDtypeStruct((s, aw), BF16),
            jax.ShapeDtypeStruct((1, D_MODEL), F32), jax.ShapeDtypeStruct((1, aw), F32),
            jax.ShapeDtypeStruct((1, aw), F32)],
        scratch_shapes=[pltpu.VMEM((aw // LANES, tm, LANES), F32), pltpu.VMEM((1, tm, LANES), F32)],
        compiler_params=_params(dimension_semantics=("arbitrary",)),
    )(dx1, mixed, gp, w_out_b, attn, an, o_h, proj, hn)


def mlp_fwd_bwd(x1, g_pre, w1_blocks, w2_b, g_post, target):
    s = x1.shape[0]
    tm = MLP_TILE
    nblk, _, fb = w1_blocks.shape

    def body(x1_ref, gpre_ref, w1_ref, w2_ref, gpost_ref, t_ref,
             dx1_ref, h2_ref, a_ref, du_ref, dff_ref, loss_ref, dgpre_ref, dgpost_ref, u_ref):
        x1v = x1_ref[...]
        h2 = _rms_fwd(x1v, gpre_ref[...], D_MODEL).astype(BF16)
        h2_ref[...] = h2
        ff = jnp.zeros((tm, D_MODEL), F32)
        for j in range(nblk):
            cols = slice(j * fb, (j + 1) * fb)
            ru = jnp.maximum(_dot(h2, w1_ref[j]), 0.0)
            u_ref[:, cols] = ru.astype(BF16)
            a = (ru * ru).astype(BF16)
            a_ref[:, cols] = a
            ff = ff + _dot(a, w2_ref[cols, :])
        diff = x1v + _rms_fwd(ff, gpost_ref[...], D_MODEL) - t_ref[...]
        _acc_rows(loss_ref, diff * diff)
        dy = diff * (1.0 / D_MODEL)
        dff, gpost_c = _rms_bwd(dy, ff, gpost_ref[...], D_MODEL)
        _acc_rows(dgpost_ref, gpost_c)
        dff_b = dff.astype(BF16)
        dff_ref[...] = dff_b
        dh2 = jnp.zeros((tm, D_MODEL), F32)
        for j in range(nblk):
            cols = slice(j * fb, (j + 1) * fb)
            du = (_dot_nt(dff_b, w2_ref[cols, :]) * (2.0 * u_ref[:, cols])).astype(BF16)
            du_ref[:, cols] = du
            dh2 = dh2 + _dot_nt(du, w1_ref[j])
        dxa, gpre_c = _rms_bwd(dh2, x1v, gpre_ref[...], D_MODEL)
        _acc_rows(dgpre_ref, gpre_c)
        dx1_ref[...] = dy + dxa

    dm = D_MODEL
    return pl.pallas_call(
        body,
        name="mlp_fwd_bwd",
        grid=(s // tm,),
        in_specs=[_row_spec(tm, dm), _const_spec(dm), _vmem_spec(), _vmem_spec(), _const_spec(dm), _row_spec(tm, dm)],
        out_specs=[_row_spec(tm, dm), _row_spec(tm, dm), _row_spec(tm, D_FF), _row_spec(tm, D_FF), _row_spec(tm, dm),
                   _const_spec(dm), _const_spec(dm), _const_spec(dm)],
        out_shape=[
            jax.ShapeDtypeStruct((s, dm), F32),
            jax.ShapeDtypeStruct((s, dm), BF16),
            jax.ShapeDtypeStruct((s, D_FF), BF16),
            jax.ShapeDtypeStruct((s, D_FF), BF16),
            jax.ShapeDtypeStruct((s, dm), BF16),
            jax.ShapeDtypeStruct((1, dm), F32),
            jax.ShapeDtypeStruct((1, dm), F32),
            jax.ShapeDtypeStruct((1, dm), F32),
        ],
        scratch_shapes=[pltpu.VMEM((tm, D_FF), BF16)],
        compiler_params=_params(dimension_semantics=("arbitrary",)),
    )(x1, g_pre, w1_blocks, w2_b, g_post, target)


def in_proj_bwd(attn_grads, hgrn_grads, dgate, w_in_b, x, g1, dx1):
    s = x.shape[0]
    tm = PROJ_TILE
    aw = ATTN_WIDTH
    n_attn = len(attn_grads)
    flat = [g[k] for k in range(3) for g in attn_grads] + list(hgrn_grads) + [dgate]

    def body(*refs):
        parts = refs[:len(flat)]
        w_ref, x_ref, g_ref, dx1_ref, dx_ref, dproj_ref, dg_ref, scr = refs[len(flat):]
        groups = []
        for k in range(3):
            acc = None
            for p, d in zip(parts[k * n_attn:(k + 1) * n_attn], DILATIONS):
                v = _from_dilated(p, scr, d, tm)
                acc = v if acc is None else acc + v
            groups.append(acc)
        groups += [p[...] for p in parts[3 * n_attn:]]
        dh = jnp.zeros((tm, D_MODEL), F32)
        for gi, grp in enumerate(groups):
            cols = slice(gi * aw, (gi + 1) * aw)
            gb = grp.astype(BF16)
            dproj_ref[:, cols] = gb
            dh = dh + _dot_nt(gb, w_ref[:, cols])
        dxa, g_c = _rms_bwd(dh, x_ref[...], g_ref[...], D_MODEL)
        _acc_rows(dg_ref, g_c)
        dx_ref[...] = dx1_ref[...] + dxa

    dm = D_MODEL
    return pl.pallas_call(
        body,
        name="in_proj_bwd",
        grid=(s // tm,),
        in_specs=[_dilated_spec(d, tm, aw) for d in DILATIONS] * 3 + [_row_spec(tm, aw)] * 4 + [
            _vmem_spec(), _row_spec(tm, dm), _const_spec(dm), _row_spec(tm, dm)],
        out_specs=[_row_spec(tm, dm), _row_spec(tm, IN_PROJ_WIDTH), _const_spec(dm)],
        out_shape=[jax.ShapeDtypeStruct((s, dm), F32), jax.ShapeDtypeStruct((s, IN_PROJ_WIDTH), BF16),
                   jax.ShapeDtypeStruct((1, dm), F32)],
        scratch_shapes=[pltpu.VMEM((aw // LANES, tm, LANES), F32)],
        compiler_params=_params(dimension_semantics=("arbitrary",)),
    )(*flat, w_in_b, x, g1, dx1)


def wgrad(a_b, b_b, tn, name, ts=2048, per_step=1, ride=None):
    s, k = a_b.shape
    n = b_b.shape[1]

    def body(a_ref, b_ref, o_ref):
        @pl.when(pl.program_id(1) == 0)
        def _():
            o_ref[...] = jnp.zeros_like(o_ref)

        a = a_ref[...]
        for jj in range(per_step):
            o_ref[jj] += _dot_tn(a, b_ref[:, jj * tn:(jj + 1) * tn])

    wide = tn * per_step
    gn, gs = n // wide, s // ts
    step = lambda j, i: (lambda: (pl.program_id(0) == j) & (pl.program_id(1) == i))
    e_in, e_out, e_shape, e_scr, e_args = _ride_specs(ride)
    out = pl.pallas_call(
        _riding(body, 2, 1, 0, ride, step(0, 0), step(gn // 2, 0), step(gn - 1, gs - 1)),
        name=name,
        grid=(gn, gs),
        in_specs=[pl.BlockSpec((ts, k), lambda j, i: (i, 0)), pl.BlockSpec((ts, wide), lambda j, i: (i, j))] + e_in,
        out_specs=[pl.BlockSpec((per_step, k, tn), lambda j, i: (j, 0, 0))] + e_out,
        out_shape=[jax.ShapeDtypeStruct((n // tn, k, tn), F32)] + e_shape,
        scratch_shapes=e_scr,
        compiler_params=_params(ride, dimension_semantics=("arbitrary", "arbitrary")),
    )(a_b, b_b, *e_args)
    return out[0] if ride is None else out


def train_step(x, target, g1, an, logits, hn, gp, g_pre, g_post, w, m, v):
    nd = len(DILATIONS)
    shard_b = {k: w[k].astype(BF16) for k in BIG}
    (w_in_g,) = run_exchange(gather_exchange([shard_b["w_in"]]), "gather_w_in")
    w_in_b = w_in_g.transpose(1, 0, 2).reshape(D_MODEL, IN_PROJ_WIDTH)

    proj, h_b, *qkvs, w2_g, w_out_g = in_proj_fwd(
        x, g1, w_in_b, ride=gather_exchange([shard_b["w_ff2"], shard_b["w_out"]]))
    w2_b = w2_g.reshape(D_FF, D_MODEL)
    attn_parts = [attn_fwd(qkv, d) for qkv, d in zip(qkvs, DILATIONS)]
    o_h, states, a_mat, w1_blocks = hgrn_fwd(proj, logits, ride=gather_exchange([shard_b["w_ff1"]]))
    w_out_b = w_out_g.reshape(D_MODEL, D_MODEL)
    x1, cat_b, mixed, attn, *lses = mix_fwd(attn_parts, o_h, proj, an, hn, w_out_b, gp, x)
    dx1, h2_b, a_b, du_b, dff_b, loss_vec, dg_pre, dg_post = mlp_fwd_bwd(x1, g_pre, w1_blocks, w2_b, g_post, target)
    dw2 = wgrad(a_b, dff_b, D_MODEL, "wgrad_ff2", ts=512)
    dw1 = wgrad(h2_b, du_b, D_FF // N_DEV, "wgrad_ff1", per_step=2)
    dmix_b, *rest = mix_bwd(dx1, mixed, gp, w_out_b, attn, an, o_h, proj, hn)
    d_os, deltas = rest[:nd], rest[nd:2 * nd]
    d_oh, dgate, dgp, dan, dhn = rest[2 * nd:]
    dwout = wgrad(cat_b, dmix_b, D_MODEL, "wgrad_out")

    early = ("w_out", "w_ff1", "w_ff2")
    early_grads = [dwout.reshape(N_DEV, D_MODEL // N_DEV, D_MODEL), dw1, dw2.reshape(N_DEV, D_FF // N_DEV, D_MODEL)]
    res = attn_bwd(qkvs[0], d_os[0], lses[0], deltas[0], DILATIONS[0], ride=to_core_exchange(early_grads))
    pairs = [pair_sum(g, s, f"pair_sum_{name}") for g, s, name in zip(early_grads, res[3:], early)]
    attn_grads = [res[:3]]
    *res, others_ff2 = attn_bwd(qkvs[1], d_os[1], lses[1], deltas[1], DILATIONS[1],
                                ride=to_chip_exchange([pairs[2][1]]))
    attn_grads.append(res)
    attn_grads.append(attn_bwd(qkvs[2], d_os[2], lses[2], deltas[2], DILATIONS[2]))
    dq_h, df_h, di_h, dlb, *others = hgrn_bwd(proj, logits, d_oh, states, a_mat,
                                              ride=to_chip_exchange([pairs[0][1], pairs[1][1]]))
    others.append(others_ff2)
    dx, dproj_b, dg1 = in_proj_bwd(attn_grads, (dq_h, df_h, di_h), dgate, w_in_b, x, g1, dx1)
    packed = _pack_small(dg1, dgp, dg_pre, dg_post, dan, dhn, dlb, loss_vec)
    dwin, small_slots = wgrad(h_b, dproj_b, 2 * IN_PROJ_WIDTH // N_DEV, "wgrad_in",
                              ride=small_exchange(packed))
    big = {name: sum_adamw(p[0], o, w[name], m[name], v[name], f"sum_adamw_{name}")
           for name, p, o in zip(early, pairs, others)}

    shard_w = IN_PROJ_WIDTH // N_DEV
    dwin_blocks = dwin.reshape(N_DEV // 2, D_MODEL, 2, shard_w).transpose(0, 2, 1, 3).reshape(N_DEV, D_MODEL, shard_w)
    pair_in, others_in = reduce_last(dwin_blocks)
    big["w_in"] = sum_adamw(pair_in, others_in, w["w_in"], m["w_in"], v["w_in"], "sum_adamw_w_in")
    return dx, big, small_slots


def _position():
    x, y, c = lax.axis_index("x"), lax.axis_index("y"), lax.axis_index("c")
    other_chips = [(1 - x, y), (x, 1 - y), (1 - x, 1 - y)]
    return x, y, c, other_chips


def _any_spec():
    return pl.BlockSpec(memory_space=pl.ANY)


class Exchange:
    def __init__(self, arrays, out_shape, sems, stages, collective_id, peers):
        self.arrays, self.out_shape, self.sems, self.stages = list(arrays), list(out_shape), list(sems), stages
        self.collective_id, self.peers = collective_id, peers

    def open(self):
        barrier = pltpu.get_barrier_semaphore()
        peers = self.peers()
        for peer in peers:
            pl.semaphore_signal(barrier, inc=1, device_id=peer, device_id_type=MESH)
        pl.semaphore_wait(barrier, len(peers))


def _siblings():
    x, y, c, _ = _position()
    return [(x, y, 1 - c)]


def _same_core_of_other_chips():
    x, y, c, chips = _position()
    return [(px, py, c) for px, py in chips]


def _gather_peers():
    x, y, c, _ = _position()
    return [(x, y, 1 - c), (1 - x, y, c), (x, 1 - y, c)]


def _all_others():
    x, y, c, _ = _position()
    return [(1 - x if rel & 4 else x, 1 - y if rel & 2 else y, 1 - c if rel & 1 else c) for rel in range(1, N_DEV)]


def gather_exchange(shards):
    n = len(shards)
    halves = [sh.shape[0] // 2 for sh in shards]

    def stages(ins, outs, sems):
        send_sems, recv_sems, local_sems = sems

        def parts():
            x, y, c, _ = _position()
            me, sibling = (x, y, c), (x, y, 1 - c)
            nbr_x, nbr_y, diag = (1 - x, y, c), (x, 1 - y, c), (1 - x, 1 - y, c)

            def slot(a, dev, rows=None):
                ref = outs[a].at[4 * dev[0] + 2 * dev[1] + dev[2]]
                return ref if rows is None else ref.at[rows]

            def copy(a, k, block, to, rows=None, src=None):
                return pltpu.make_async_remote_copy(
                    src_ref=slot(a, block, rows) if src is None else src, dst_ref=slot(a, block, rows),
                    send_sem=send_sems.at[a, k], recv_sem=recv_sems.at[a, k], device_id=to, device_id_type=MESH)

            upper = lambda a: pl.ds(0, halves[a])
            lower = lambda a: pl.ds(halves[a], halves[a])
            return me, sibling, nbr_x, nbr_y, diag, slot, copy, upper, lower

        def begin():
            me, sibling, nbr_x, nbr_y, _, slot, copy, _, _ = parts()
            for a in range(n):
                pltpu.make_async_copy(ins[a], slot(a, me), local_sems.at[a]).start()
                for k, to in enumerate((sibling, nbr_x, nbr_y)):
                    copy(a, k, me, to, src=ins[a]).start()

        def middle():
            me, sibling, nbr_x, nbr_y, _, _, copy, upper, lower = parts()
            for a in range(n):
                copy(a, 1, nbr_x, me).wait_recv()
                copy(a, 3, nbr_x, sibling).start()
                copy(a, 5, nbr_x, nbr_y, rows=lower(a)).start()
                copy(a, 2, nbr_y, me).wait_recv()
                copy(a, 4, nbr_y, sibling).start()
                copy(a, 6, nbr_y, nbr_x, rows=upper(a)).start()

        def late():
            me, sibling, _, _, diag, _, copy, upper, lower = parts()
            for a in range(n):
                copy(a, 6, diag, me, rows=upper(a)).wait_recv()
                copy(a, 5, diag, me, rows=lower(a)).wait_recv()
                copy(a, 7, diag, sibling).start()

        def end():
            me, sibling, nbr_x, nbr_y, diag, slot, copy, upper, lower = parts()
            sib = lambda dev: (dev[0], dev[1], sibling[2])
            for a in range(n):
                for k, block in ((0, sibling), (3, sib(nbr_x)), (4, sib(nbr_y)), (7, sib(diag))):
                    copy(a, k, block, me).wait_recv()
                copy(a, 0, me, sibling, src=ins[a]).wait_send()
                copy(a, 1, me, nbr_x, src=ins[a]).wait_send()
                copy(a, 2, me, nbr_y, src=ins[a]).wait_send()
                copy(a, 3, nbr_x, sibling).wait_send()
                copy(a, 4, nbr_y, sibling).wait_send()
                copy(a, 5, nbr_x, nbr_y, rows=lower(a)).wait_send()
                copy(a, 6, nbr_y, nbr_x, rows=upper(a)).wait_send()
                copy(a, 7, diag, sibling).wait_send()
                pltpu.make_async_copy(ins[a], slot(a, me), local_sems.at[a]).wait()

        return begin, (middle, late), end

    return Exchange(
        shards, [jax.ShapeDtypeStruct((N_DEV,) + sh.shape, sh.dtype) for sh in shards],
        [pltpu.SemaphoreType.DMA((n, 8)), pltpu.SemaphoreType.DMA((n, 8)), pltpu.SemaphoreType.DMA((n,))], stages,
        collective_id=0, peers=_gather_peers)


def to_core_exchange(grads):
    n = len(grads)

    def stages(ins, outs, sems):
        send_sems, recv_sems = sems

        def copies():
            x, y, c, _ = _position()
            return [pltpu.make_async_remote_copy(
                src_ref=ins[a].at[2 * q + (1 - c)], dst_ref=outs[a].at[q], send_sem=send_sems.at[a, q],
                recv_sem=recv_sems.at[a, q], device_id=(x, y, 1 - c), device_id_type=MESH)
                for a in range(n) for q in range(4)]

        def begin():
            for cp in copies():
                cp.start()

        def end():
            for cp in copies():
                cp.wait()

        return begin, None, end

    return Exchange(grads, [jax.ShapeDtypeStruct((4,) + g.shape[1:], g.dtype) for g in grads],
                    [pltpu.SemaphoreType.DMA((n, 4)), pltpu.SemaphoreType.DMA((n, 4))], stages,
                    collective_id=1, peers=_siblings)


def pair_sum(grad, from_sibling, name):
    _, r, cdim = grad.shape
    tr = min(r, ELEMENTWISE_ROWS)
    c_idx = lax.axis_index("c").astype(jnp.int32).reshape(1)

    def body(c_ref, g_ref, s_ref, o_ref, ob_ref):
        total = g_ref[...] + s_ref[...]
        o_ref[...] = total
        ob_ref[...] = total.astype(BF16)

    blk = lambda: pl.BlockSpec((1, tr, cdim), lambda q, i, cr: (q, i, 0))
    return pl.pallas_call(
        body,
        name=name,
        grid_spec=pltpu.PrefetchScalarGridSpec(
            num_scalar_prefetch=1,
            grid=(4, r // tr),
            in_specs=[pl.BlockSpec((1, tr, cdim), lambda q, i, cr: (2 * q + cr[0], i, 0)), blk()],
            out_specs=[blk(), blk()],
        ),
        out_shape=[jax.ShapeDtypeStruct((4, r, cdim), F32), jax.ShapeDtypeStruct((4, r, cdim), BF16)],
        compiler_params=_params(dimension_semantics=("arbitrary", "arbitrary")),
    )(c_idx, grad, from_sibling)


def to_chip_exchange(pairs):
    n = len(pairs)

    def stages(ins, outs, sems):
        send_sems, recv_sems = sems

        def copies():
            x, y, c, chips = _position()
            return [pltpu.make_async_remote_copy(
                src_ref=ins[a].at[2 * px + py], dst_ref=outs[a].at[j], send_sem=send_sems.at[a, j],
                recv_sem=recv_sems.at[a, j], device_id=(px, py, c), device_id_type=MESH)
                for a in range(n) for j, (px, py) in enumerate(chips)]

        def begin():
            for cp in copies():
                cp.start()

        def end():
            for cp in copies():
                cp.wait()

        return begin, None, end

    return Exchange(pairs, [jax.ShapeDtypeStruct((3,) + p.shape[1:], p.dtype) for p in pairs],
                    [pltpu.SemaphoreType.DMA((n, 3)), pltpu.SemaphoreType.DMA((n, 3))], stages,
                    collective_id=2, peers=_same_core_of_other_chips)


def run_exchange(ex, name):
    n_in, n_out = len(ex.arrays), len(ex.out_shape)

    def body(*refs):
        begin, middle, end = ex.stages(refs[:n_in], refs[n_in:n_in + n_out], refs[n_in + n_out:])
        ex.open()
        begin()
        for stage in _as_tuple(middle):
            stage()
        end()

    return pl.pallas_call(
        body,
        name=name,
        in_specs=[_any_spec()] * n_in,
        out_specs=[_any_spec()] * n_out,
        out_shape=ex.out_shape,
        scratch_shapes=ex.sems,
        compiler_params=pltpu.CompilerParams(collective_id=ex.collective_id),
    )(*ex.arrays)


def _as_tuple(stages):
    return () if stages is None else stages if isinstance(stages, tuple) else (stages,)


def _riding(body, n_in, n_out, n_scratch, ex, first, middle, last, late=None):
    if ex is None:
        return body
    r_in, r_out = len(ex.arrays), len(ex.out_shape)

    def wrapped(*refs):
        k_in, refs = refs[:n_in], refs[n_in:]
        e_in, refs = refs[:r_in], refs[r_in:]
        k_out, refs = refs[:n_out], refs[n_out:]
        e_out, refs = refs[:r_out], refs[r_out:]
        k_scr, e_sems = refs[:n_scratch], refs[n_scratch:]
        begin, mid, end = ex.stages(e_in, e_out, e_sems)

        @pl.when(first())
        def _():
            ex.open()
            begin()

        body(*k_in, *k_out, *k_scr)
        for stage, at in zip(_as_tuple(mid), (middle, late or last)):
            pl.when(at())(stage)
        pl.when(last())(end)

    return wrapped


def _ride_specs(ex):
    if ex is None:
        return [], [], [], [], []
    return [_any_spec()] * len(ex.arrays), [_any_spec()] * len(ex.out_shape), ex.out_shape, ex.sems, ex.arrays


def reduce_last(grad):
    _, r, cdim = grad.shape

    def body(g_hbm, own_ref, others_hbm, g_buf, to_sib, from_sib, send_buf, load_sem, send_sems, recv_sems):
        x, y, c, chips = _position()
        order = chips + [(x, y)]
        n_other = len(chips)
        sibling = (x, y, 1 - c)
        barrier = pltpu.get_barrier_semaphore()
        peers = [sibling] + [(px, py, c) for px, py in chips]
        for peer in peers:
            pl.semaphore_signal(barrier, inc=1, device_id=peer, device_id_type=MESH)
        pl.semaphore_wait(barrier, len(peers))

        def load(block):
            cp = pltpu.make_async_copy(g_hbm.at[block], g_buf, load_sem)
            cp.start()
            cp.wait()
            return g_buf[...]

        def to_sibling(j):
            return pltpu.make_async_remote_copy(
                src_ref=to_sib.at[j], dst_ref=from_sib.at[j], send_sem=send_sems.at[n_other + j],
                recv_sem=recv_sems.at[n_other + j], device_id=sibling, device_id_type=MESH)

        def to_chip(j):
            px, py = chips[j]
            return pltpu.make_async_remote_copy(
                src_ref=send_buf.at[j], dst_ref=others_hbm.at[j], send_sem=send_sems.at[j], recv_sem=recv_sems.at[j],
                device_id=(px, py, c), device_id_type=MESH)

        def hand_over(j):
            px, py = order[j]
            to_sib[j] = load(2 * (2 * px + py) + (1 - c)).astype(BF16)
            to_sibling(j).start()

        def pair_sum_of(j):
            px, py = order[j]
            to_sibling(j).wait_recv()
            total = load(2 * (2 * px + py) + c) + from_sib[j].astype(F32)
            if j < n_other:
                send_buf[j] = total.astype(BF16)
                to_chip(j).start()
            else:
                own_ref[0] = total

        turn = [n_other - 1] + list(range(n_other - 1)) + [n_other]
        hand_over(turn[0])
        for before, j in zip(turn, turn[1:]):
            hand_over(j)
            pair_sum_of(before)
        pair_sum_of(turn[-1])
        for j in range(len(order)):
            to_sibling(j).wait_send()
        for j in range(n_other):
            to_chip(j).wait()

    n_blocks = N_DEV // 2
    return pl.pallas_call(
        body,
        name="reduce_w_in",
        in_specs=[_any_spec()],
        out_specs=[_vmem_spec(), _any_spec()],
        out_shape=[jax.ShapeDtypeStruct((1, r, cdim), F32), jax.ShapeDtypeStruct((n_blocks - 1, r, cdim), BF16)],
        scratch_shapes=[pltpu.VMEM((r, cdim), F32), pltpu.VMEM((n_blocks, r, cdim), BF16),
                        pltpu.VMEM((n_blocks, r, cdim), BF16), pltpu.VMEM((n_blocks - 1, r, cdim), BF16),
                        pltpu.SemaphoreType.DMA(()), pltpu.SemaphoreType.DMA((2 * n_blocks - 1,)),
                        pltpu.SemaphoreType.DMA((2 * n_blocks - 1,))],
        compiler_params=pltpu.CompilerParams(collective_id=4, vmem_limit_bytes=VMEM_LIMIT),
    )(grad)


def _adamw(w, g, m, v):
    m = ADAM_B1 * m + (1.0 - ADAM_B1) * g
    v = ADAM_B2 * v + (1.0 - ADAM_B2) * (g * g)
    m_hat = m / (1.0 - ADAM_B1 ** ADAM_STEP)
    v_hat = v / (1.0 - ADAM_B2 ** ADAM_STEP)
    delta = -ADAM_LR * (m_hat / (jnp.sqrt(v_hat) + ADAM_EPS) + ADAM_WD * w)
    return delta, m, v


def sum_adamw(pairs, others, w, m, v, name):
    r, cdim = w.shape
    tr = min(r, ELEMENTWISE_ROWS // 2)
    if pairs.shape[0] == 1:
        chip_idx = jnp.zeros((1,), jnp.int32)
    else:
        chip_idx = (2 * lax.axis_index("x") + lax.axis_index("y")).astype(jnp.int32).reshape(1)

    def body(q_ref, p_ref, o_ref, w_ref, m_ref, v_ref, g_out, d_out, m_out, v_out):
        g = p_ref[0] + o_ref[0].astype(F32) + o_ref[1].astype(F32) + o_ref[2].astype(F32)
        g_out[...] = g
        d_out[...], m_out[...], v_out[...] = _adamw(w_ref[...], g, m_ref[...], v_ref[...])

    tile = lambda: pl.BlockSpec((tr, cdim), lambda i, qr: (i, 0))
    return pl.pallas_call(
        body,
        name=name,
        grid_spec=pltpu.PrefetchScalarGridSpec(
            num_scalar_prefetch=1,
            grid=(r // tr,),
            in_specs=[pl.BlockSpec((1, tr, cdim), lambda i, qr: (qr[0], i, 0)),
                      pl.BlockSpec((3, tr, cdim), lambda i, qr: (0, i, 0)), tile(), tile(), tile()],
            out_specs=[tile(), tile(), tile(), tile()],
        ),
        out_shape=[jax.ShapeDtypeStruct((r, cdim), F32)] * 4,
        compiler_params=_params(dimension_semantics=("arbitrary",)),
    )(chip_idx, pairs, others, w, m, v)


def small_exchange(packed):
    def stages(ins, outs, sems):
        send_sems, recv_sems, local_sem = sems
        (src,), (slots,) = ins, outs

        def copies():
            x, y, c, _ = _position()
            my_id = 4 * x + 2 * y + c
            sends, landings = [], []
            for rel in range(1, N_DEV):
                px = 1 - x if (rel >> 2) & 1 else x
                py = 1 - y if (rel >> 1) & 1 else y
                pc = 1 - c if rel & 1 else c
                peer = dict(send_sem=send_sems.at[rel - 1], recv_sem=recv_sems.at[rel - 1], device_id=(px, py, pc),
                            device_id_type=MESH)
                sends.append(pltpu.make_async_remote_copy(src_ref=src, dst_ref=slots.at[my_id], **peer))
                landings.append(pltpu.make_async_remote_copy(src_ref=src, dst_ref=slots.at[4 * px + 2 * py + pc], **peer))
            return pltpu.make_async_copy(src, slots.at[my_id], local_sem), sends, landings

        def begin():
            local, sends, _ = copies()
            local.start()
            for cp in sends:
                cp.start()

        def end():
            local, sends, landings = copies()
            for cp in landings:
                cp.wait_recv()
            for cp in sends:
                cp.wait_send()
            local.wait()

        return begin, None, end

    return Exchange([packed], [jax.ShapeDtypeStruct((N_DEV,) + packed.shape, packed.dtype)],
                    [pltpu.SemaphoreType.DMA((N_DEV - 1,)), pltpu.SemaphoreType.DMA((N_DEV - 1,)),
                     pltpu.SemaphoreType.DMA(())], stages, collective_id=3, peers=_all_others)


def small_adamw(slots, w, m, v):
    def body(r_ref, w_ref, m_ref, v_ref, g_out, d_out, m_out, v_out, loss_out):
        red = r_ref[0]
        for k in range(1, N_DEV):
            red = red + r_ref[k]
        wv = w_ref[...]
        lb = _lower_bound(jnp.concatenate([wv[5:6, :HGRN_WIDTH], wv[5:6, HGRN_WIDTH:]], axis=0))
        t = red[5:6, :HGRN_WIDTH] * lb * (1.0 - lb)
        row = lax.broadcasted_iota(jnp.int32, red.shape, 0)
        g = jnp.where(row == 5, jnp.concatenate([t, -t], axis=1), jnp.where(row >= 6, 0.0, red))
        g_out[...] = g
        d_out[...], m_out[...], v_out[...] = _adamw(wv, g, m_ref[...], v_ref[...])
        loss = jnp.sum(red[6:7, :], axis=-1, keepdims=True) * (0.5 / D_MODEL)
        loss_out[...] = jnp.broadcast_to(loss, loss_out.shape)

    return pl.pallas_call(
        body,
        name="small_adamw",
        in_specs=[_vmem_spec()] * 4,
        out_specs=[_vmem_spec()] * 5,
        out_shape=[jax.ShapeDtypeStruct(w.shape, F32)] * 4 + [jax.ShapeDtypeStruct((SUBLANES, LANES), F32)],
    )(slots, w, m, v)


def _pack_small(g1, gp, g_pre, g_post, an, hn, logits_or_dlb, extra=None):
    row5 = logits_or_dlb.reshape(1, -1)
    row5 = jnp.pad(row5, ((0, 0), (0, D_MODEL - row5.shape[1])))
    row6 = jnp.zeros((1, D_MODEL), F32) if extra is None else extra
    return jnp.concatenate([g1, gp, g_pre, g_post, jnp.concatenate([an, hn], axis=1), row5, row6,
                            jnp.zeros((1, D_MODEL), F32)], axis=0)


def _unpack_small(p):
    return dict(mix_pre_norm=p[0:1], mix_post_norm=p[1:2], mlp_pre_norm=p[2:3], mlp_post_norm=p[3:4],
                attn_out_norm=p[4:5, :ATTN_WIDTH], hgrn_out_norm=p[4:5, ATTN_WIDTH:],
                hgrn_lb_logits=p[5].reshape(2, HGRN_WIDTH))


BIG = ("w_in", "w_out", "w_ff1", "w_ff2")
ORDER = ("mix_pre_norm", "w_in", "attn_out_norm", "hgrn_lb_logits", "hgrn_out_norm", "w_out", "mix_post_norm",
         "mlp_pre_norm", "w_ff1", "w_ff2", "mlp_post_norm")


def kernel(x, mix_pre_norm, w_in, attn_out_norm, hgrn_lb_logits, hgrn_out_norm, w_out, mix_post_norm, mlp_pre_norm, w_ff1, w_ff2, mlp_post_norm, loss_target, m_mix_pre_norm, m_w_in, m_attn_out_norm, m_hgrn_lb_logits, m_hgrn_out_norm, m_w_out, m_mix_post_norm, m_mlp_pre_norm, m_w_ff1, m_w_ff2, m_mlp_post_norm, v_mix_pre_norm, v_w_in, v_attn_out_norm, v_hgrn_lb_logits, v_hgrn_out_norm, v_w_out, v_mix_post_norm, v_mlp_pre_norm, v_w_ff1, v_w_ff2, v_mlp_post_norm):
    w = dict(w_in=w_in[0], w_out=w_out[0], w_ff1=w_ff1[0], w_ff2=w_ff2[0])
    m = dict(w_in=m_w_in[0], w_out=m_w_out[0], w_ff1=m_w_ff1[0], w_ff2=m_w_ff2[0])
    v = dict(w_in=v_w_in[0], w_out=v_w_out[0], w_ff1=v_w_ff1[0], w_ff2=v_w_ff2[0])

    dx, big, small_slots = train_step(x[0], loss_target[0], mix_pre_norm, attn_out_norm, hgrn_lb_logits, hgrn_out_norm,
                                      mix_post_norm, mlp_pre_norm, mlp_post_norm, w, m, v)

    pack = lambda a, b, c2, d, e, f, g: _pack_small(a, b, c2, d, e, f, g)
    w_s = pack(mix_pre_norm, mix_post_norm, mlp_pre_norm, mlp_post_norm, attn_out_norm, hgrn_out_norm, hgrn_lb_logits)
    m_s = pack(m_mix_pre_norm, m_mix_post_norm, m_mlp_pre_norm, m_mlp_post_norm, m_attn_out_norm, m_hgrn_out_norm,
               m_hgrn_lb_logits)
    v_s = pack(v_mix_pre_norm, v_mix_post_norm, v_mlp_pre_norm, v_mlp_post_norm, v_attn_out_norm, v_hgrn_out_norm,
               v_hgrn_lb_logits)
    g_s, d_s, nm_s, nv_s, loss = small_adamw(small_slots, w_s, m_s, v_s)
    small_out = [_unpack_small(t) for t in (g_s, d_s, nm_s, nv_s)]

    outs = [loss[0, 0], dx[None]]
    for kind in range(4):
        for name in ORDER:
            outs.append(big[name][kind][None] if name in BIG else small_out[kind][name])
    return tuple(outs)
```

```python
import jax
import jax.numpy as jnp
from jax import lax
from jax.experimental import pallas as pl
from jax.experimental.pallas import tpu as pltpu

F32 = jnp.float32
BF16 = jnp.bfloat16

D_MODEL = 1024
ATTN_WIDTH = 512
ATTN_HEAD_DIM = 64
ATTN_HEADS = 8
ATTN_BLOCK = 128
DILATIONS = (1, 4, 16)
HGRN_WIDTH = 512
HGRN_HEADS = 4
HGRN_HEAD_DIM = 128
HGRN_CHUNK = 64
IN_PROJ_WIDTH = 3584
D_FF = 4096
RMS_EPS = 1e-6
N_DEV = 8
ADAM_LR = 0.001
ADAM_B1 = 0.9
ADAM_B2 = 0.999
ADAM_EPS = 1e-08
ADAM_WD = 0.01
ADAM_STEP = 10

SUBLANES = 8
LANES = 128
COLUMN_UNROLL = 8
HGRN_CHUNKS_PER_STEP = 2
SUB_BLOCK = 16
TOKEN_TILE = 512
ELEMENTWISE_ROWS = 1024
MLP_TILE = 256
PROJ_TILE = 512
VMEM_BYTES_V7X = 64 * 1024 * 1024
VMEM_LIMIT = VMEM_BYTES_V7X // 8 * 7
NEG_BIG = -1e30
MESH = pl.DeviceIdType.MESH


def _params(ride=None, **kw):
    if ride is not None:
        kw["collective_id"] = ride.collective_id
    return pltpu.CompilerParams(vmem_limit_bytes=VMEM_LIMIT, **kw)


def _vmem_spec():
    return pl.BlockSpec(memory_space=pltpu.VMEM)


def _dot(a, b):
    return jnp.dot(a, b, preferred_element_type=F32)


def _dot_nt(a, b):
    return lax.dot_general(a, b, (((1,), (1,)), ((), ())), preferred_element_type=F32)


def _dot_tn(a, b):
    return lax.dot_general(a, b, (((0,), (0,)), ((), ())), preferred_element_type=F32)


def _sigmoid(x):
    return 1.0 / (1.0 + jnp.exp(-x))


def _rms_fwd(x, gain, width):
    r = lax.rsqrt(jnp.sum(x * x, axis=-1, keepdims=True) * (1.0 / width) + RMS_EPS)
    return x * r * gain


def _rms_bwd(dy, x, gain, width):
    r = lax.rsqrt(jnp.sum(x * x, axis=-1, keepdims=True) * (1.0 / width) + RMS_EPS)
    xhat = x * r
    dxhat = dy * gain
    dx = r * (dxhat - xhat * (jnp.sum(dxhat * xhat, axis=-1, keepdims=True) * (1.0 / width)))
    return dx, dy * xhat


def _split3(x):
    hi = x.astype(BF16)
    r1 = x - hi.astype(F32)
    mid = r1.astype(BF16)
    lo = (r1 - mid.astype(F32)).astype(BF16)
    return hi, mid, lo


def _tri_sum(tri_bf16, x):
    hi, mid, lo = _split3(x)
    return _dot(tri_bf16, hi) + _dot(tri_bf16, mid) + _dot(tri_bf16, lo)


def _dilated_spec(d, tm, width):
    return pl.BlockSpec((d, tm // d, width), lambda i: (0, i, 0))


def _lane_blocks(ref, value):
    for c in range(ref.shape[0]):
        ref[c] = value[:, c * LANES:(c + 1) * LANES]


def _to_dilated(src_ref, dst_ref, d, tm, cast=None):
    for r in range(d):
        for c in range(src_ref.shape[0]):
            v = src_ref[c] if d == 1 else src_ref[c, pl.ds(r, tm // d, stride=d), :]
            dst_ref[r, :, c * LANES:(c + 1) * LANES] = v if cast is None else v.astype(cast)


def _from_dilated(src_ref, scratch_ref, d, tm):
    if d == 1:
        return src_ref[0].astype(F32)
    nblk = scratch_ref.shape[0]
    for r in range(d):
        for c in range(nblk):
            scratch_ref[c, pl.ds(r, tm // d, stride=d), :] = src_ref[r, :, c * LANES:(c + 1) * LANES].astype(F32)
    return jnp.concatenate([scratch_ref[c] for c in range(nblk)], axis=1)


def in_proj_fwd(x, g1, w_in_b, ride=None):
    s = x.shape[0]
    tm = PROJ_TILE
    qkv_w = 3 * ATTN_WIDTH
    hg_w = IN_PROJ_WIDTH - qkv_w

    def body(x_ref, g_ref, w_ref, hg_ref, h_ref, *rest):
        qkv_refs, qkv_scr = rest[:len(DILATIONS)], rest[len(DILATIONS)]
        h = _rms_fwd(x_ref[...], g_ref[...], D_MODEL).astype(BF16)
        h_ref[...] = h
        proj = _dot(h, w_ref[...])
        hg_ref[...] = proj[:, qkv_w:]
        _lane_blocks(qkv_scr, proj[:, :qkv_w])
        for d, ref in zip(DILATIONS, qkv_refs):
            _to_dilated(qkv_scr, ref, d, tm, cast=BF16)

    n_steps = s // tm
    step = lambda k: (lambda: pl.program_id(0) == k)
    e_in, e_out, e_shape, e_scr, e_args = _ride_specs(ride)
    return pl.pallas_call(
        _riding(body, 3, 2 + len(DILATIONS), 1, ride, step(0), step(n_steps // 2), step(n_steps - 1),
                late=step(n_steps - 2)),
        name="in_proj_fwd",
        grid=(n_steps,),
        in_specs=[
            pl.BlockSpec((tm, D_MODEL), lambda i: (i, 0)),
            pl.BlockSpec((1, D_MODEL), lambda i: (0, 0)),
            _vmem_spec(),
        ] + e_in,
        out_specs=[
            pl.BlockSpec((tm, hg_w), lambda i: (i, 0)),
            pl.BlockSpec((tm, D_MODEL), lambda i: (i, 0)),
        ] + [_dilated_spec(d, tm, qkv_w) for d in DILATIONS] + e_out,
        out_shape=[jax.ShapeDtypeStruct((s, hg_w), F32), jax.ShapeDtypeStruct((s, D_MODEL), BF16)] + [
            jax.ShapeDtypeStruct((d, s // d, qkv_w), BF16) for d in DILATIONS] + e_shape,
        scratch_shapes=[pltpu.VMEM((qkv_w // LANES, tm, LANES), F32)] + e_scr,
        compiler_params=_params(ride, dimension_semantics=("arbitrary",)),
    )(x, g1, w_in_b, *e_args)


ATTN_SCALE = ATTN_HEAD_DIM ** -0.5


def _fill_attn_bias(bias_ref, dilation):
    qi = lax.broadcasted_iota(jnp.int32, (ATTN_BLOCK, 2 * ATTN_BLOCK), 0)
    kj = lax.broadcasted_iota(jnp.int32, (ATTN_BLOCK, 2 * ATTN_BLOCK), 1)
    dist = qi + ATTN_BLOCK - kj
    valid = (dist >= 0) & (dist <= ATTN_BLOCK)
    for head in range(ATTN_HEADS):
        slope = 2.0 ** (-8.0 * (head + 1) / ATTN_HEADS)
        bias = jnp.where(valid, dist.astype(F32) * (-slope * dilation), NEG_BIG)
        bias_ref[0, head] = bias
        bias_ref[1, head] = jnp.where(kj >= ATTN_BLOCK, bias, NEG_BIG)


def _stack_heads(x):
    low = _lane_half(x.shape, 0)
    zero = jnp.zeros_like(x)
    return jnp.concatenate([jnp.where(low, x, zero), jnp.where(low, zero, x)], axis=0)


def _unstack_heads(y):
    half = y.shape[0] // 2
    return jnp.where(_lane_half((half, y.shape[1]), 0), y[:half], y[half:])


def _attn_scores(q_stack, kcat, bias_ref, pair, first_block):
    f = first_block.astype(jnp.int32)
    bias = jnp.concatenate([bias_ref[f, 2 * pair], bias_ref[f, 2 * pair + 1]], axis=0)
    return _dot_nt(q_stack, kcat) + bias


def _lane_half(shape, sub):
    lane = lax.broadcasted_iota(jnp.int32, shape, 1)
    return (lane < ATTN_HEAD_DIM) if sub == 0 else (lane >= ATTN_HEAD_DIM)


def _sub_block(col, row):
    return pl.BlockSpec((None, ATTN_BLOCK, ATTN_WIDTH), lambda r, n: (r, row(n), col))


def attn_fwd(qkv, dilation):
    d, length, _ = qkv.shape
    assert d == dilation
    nb = length // ATTN_BLOCK

    def body(q_ref, kc_ref, kp_ref, vc_ref, vp_ref, o_ref, lse_ref, bias_ref):
        @pl.when((pl.program_id(0) == 0) & (pl.program_id(1) == 0))
        def _():
            _fill_attn_bias(bias_ref, d)

        first = pl.program_id(1) == 0
        for pair in range(ATTN_HEADS // 2):
            lanes = slice(pair * LANES, (pair + 1) * LANES)
            q_stack = _stack_heads(q_ref[:, lanes] * ATTN_SCALE)
            kcat = jnp.concatenate([kp_ref[:, lanes], kc_ref[:, lanes]], axis=0)
            vcat = jnp.concatenate([vp_ref[:, lanes], vc_ref[:, lanes]], axis=0)
            sc = _attn_scores(q_stack, kcat, bias_ref, pair, first)
            m = jnp.max(sc, axis=-1, keepdims=True)
            p = jnp.exp(sc - m)
            den = jnp.sum(p, axis=-1, keepdims=True)
            o_ref[:, lanes] = _unstack_heads(_dot(p.astype(BF16), vcat) / den).astype(BF16)
            lse_ref[:, lanes] = _unstack_heads(jnp.broadcast_to(m + jnp.log(den), (2 * ATTN_BLOCK, LANES)))

    cur = lambda n: n
    prev = lambda n: jnp.maximum(n - 1, 0)
    return pl.pallas_call(
        body,
        name=f"attn_fwd_d{d}",
        grid=(d, nb),
        in_specs=[_sub_block(0, cur), _sub_block(1, cur), _sub_block(1, prev), _sub_block(2, cur), _sub_block(2, prev)],
        out_specs=[_sub_block(0, cur), _sub_block(0, cur)],
        out_shape=[jax.ShapeDtypeStruct((d, length, ATTN_WIDTH), BF16), jax.ShapeDtypeStruct((d, length, ATTN_WIDTH), F32)],
        scratch_shapes=[pltpu.VMEM((2, ATTN_HEADS, ATTN_BLOCK, 2 * ATTN_BLOCK), F32)],
        compiler_params=_params(dimension_semantics=("arbitrary", "arbitrary")),
    )(qkv, qkv, qkv, qkv, qkv)


def attn_bwd(qkv, d_out, lse, delta, dilation, ride=None):
    d, length, _ = qkv.shape
    assert d == dilation
    nb = length // ATTN_BLOCK

    steps = d * nb + 1

    def body(q_ref, kc_ref, kp_ref, vc_ref, vp_ref, do_ref, lse_ref, dl_ref, dq_ref, dk_ref, dv_ref, ck_ref, cv_ref,
             bias_ref):
        t = pl.program_id(0)

        @pl.when(t == 0)
        def _():
            ck_ref[...] = jnp.zeros_like(ck_ref)
            cv_ref[...] = jnp.zeros_like(cv_ref)
            _fill_attn_bias(bias_ref, d)

        @pl.when(t < steps - 1)
        def _():
            first = t % nb == 0
            for pair in range(ATTN_HEADS // 2):
                lanes = slice(pair * LANES, (pair + 1) * LANES)
                q_stack = _stack_heads(q_ref[:, lanes] * ATTN_SCALE)
                do_stack = _stack_heads(do_ref[:, lanes])
                kcat = jnp.concatenate([kp_ref[:, lanes], kc_ref[:, lanes]], axis=0)
                vcat = jnp.concatenate([vp_ref[:, lanes], vc_ref[:, lanes]], axis=0)
                col_a, col_b = 2 * pair, 2 * pair + 1
                lse_col = jnp.concatenate([lse_ref[:, col_a:col_a + 1], lse_ref[:, col_b:col_b + 1]], axis=0)
                dl_col = jnp.concatenate([dl_ref[:, col_a:col_a + 1], dl_ref[:, col_b:col_b + 1]], axis=0)
                p = jnp.exp(_attn_scores(q_stack, kcat, bias_ref, pair, first) - lse_col)
                ds = (p * (_dot_nt(do_stack, vcat) - dl_col)).astype(BF16)
                dq_ref[:, lanes] = (_unstack_heads(_dot(ds, kcat)) * ATTN_SCALE).astype(BF16)
                dk_cat = _dot_tn(ds, q_stack)
                dv_cat = _dot_tn(p.astype(BF16), do_stack)
                dk_ref[:, lanes] = (ck_ref[:, lanes] + dk_cat[:ATTN_BLOCK]).astype(BF16)
                dv_ref[:, lanes] = (cv_ref[:, lanes] + dv_cat[:ATTN_BLOCK]).astype(BF16)
                ck_ref[:, lanes] = dk_cat[ATTN_BLOCK:]
                cv_ref[:, lanes] = dv_cat[ATTN_BLOCK:]

        @pl.when(t == steps - 1)
        def _():
            dk_ref[...] = ck_ref[...].astype(BF16)
            dv_ref[...] = cv_ref[...].astype(BF16)

    blk = (ATTN_BLOCK, ATTN_WIDTH)

    def spec(col, shift, width=ATTN_WIDTH):
        def index(t):
            f = jnp.minimum(t, steps - 2) if shift > -2 else jnp.maximum(t - 1, 0)
            r, n = f // nb, f % nb
            return (r, jnp.maximum(n - 1, 0) if shift == -1 else n, col)
        return pl.BlockSpec((None, ATTN_BLOCK, width), index)

    step = lambda k: (lambda: pl.program_id(0) == k)
    e_in, e_out, e_shape, e_scr, e_args = _ride_specs(ride)
    return pl.pallas_call(
        _riding(body, 8, 3, 3, ride, step(0), step(steps // 2), step(steps - 1)),
        name=f"attn_bwd_d{d}",
        grid=(steps,),
        in_specs=[spec(0, 0), spec(1, 0), spec(1, -1), spec(2, 0), spec(2, -1), spec(0, 0), spec(0, 0, LANES),
                  spec(0, 0, LANES)] + e_in,
        out_specs=[spec(0, 0), spec(0, -2), spec(0, -2)] + e_out,
        out_shape=[jax.ShapeDtypeStruct((d, length, ATTN_WIDTH), BF16)] * 3 + e_shape,
        scratch_shapes=[pltpu.VMEM(blk, F32), pltpu.VMEM(blk, F32),
                        pltpu.VMEM((2, ATTN_HEADS, ATTN_BLOCK, 2 * ATTN_BLOCK), F32)] + e_scr,
        compiler_params=_params(ride, dimension_semantics=("arbitrary",)),
    )(qkv, qkv, qkv, qkv, qkv, d_out, lse, delta, *e_args)


def _lower_bound(logits):
    return _sigmoid(logits[0:1, :] - logits[1:2, :])


def _hgrn_gates(q, fp, lb):
    sq = _sigmoid(q)
    qf = q * sq
    sig = _sigmoid(fp)
    sig_neg = _sigmoid(-fp)
    kf = (1.0 - lb) * sig_neg
    log_sig = jnp.minimum(fp, 0.0) - jnp.log(1.0 + jnp.exp(-jnp.abs(fp)))
    a = jnp.log(lb)
    c = jnp.log(1.0 - lb) + log_sig
    log_f = jnp.maximum(a, c) + jnp.log(1.0 + jnp.exp(-jnp.abs(a - c)))
    return sq, qf, (sig, sig_neg, c), log_f, kf


def _tril_bf16(n, upper=False):
    r = lax.broadcasted_iota(jnp.int32, (n, n), 0)
    c = lax.broadcasted_iota(jnp.int32, (n, n), 1)
    keep = (c >= r) if upper else (c <= r)
    return jnp.where(keep, 1.0, 0.0).astype(BF16)


def _hgrn_diagonal_loops(c_len, diagonal):
    for half in range(SUB_BLOCK // SUBLANES):
        def step(jj, carry, half=half):
            j = half * SUBLANES + jj
            for i in range(c_len // SUB_BLOCK):
                diagonal(slice(i * SUB_BLOCK + half * SUBLANES, (i + 1) * SUB_BLOCK), j, i * SUB_BLOCK + j)
            return carry

        lax.fori_loop(0, SUBLANES, step, 0, unroll=COLUMN_UNROLL)


def _hgrn_off_diagonal(b, qf, kf):
    c_len, width = b.shape
    edges = [b[0:1, :]] + [b[i * SUB_BLOCK - 1:i * SUB_BLOCK, :] for i in range(1, c_len // SUB_BLOCK)]
    eq = jnp.exp(b - jnp.concatenate([jnp.broadcast_to(e, (SUB_BLOCK, width)) for e in edges], axis=0))
    q_til = qf * eq
    k_til, ek = [], []
    for i in range(1, c_len // SUB_BLOCK):
        n = i * SUB_BLOCK
        e = jnp.exp(edges[i] - b[:n, :])
        ek.append(e)
        k_til.append(jnp.concatenate([kf[:n, :] * e, jnp.zeros((2 * c_len - n, width), F32)], axis=0))
    return q_til, k_til, eq, ek


def _split2(x):
    hi = x.astype(BF16)
    return hi, (x - hi.astype(F32)).astype(BF16)


def hgrn_fwd(proj, lb, ride=None):
    s = proj.shape[0]
    c_len, nh, hd = HGRN_CHUNK, HGRN_HEADS, HGRN_HEAD_DIM
    n_chunks = s // c_len
    col0 = 0

    cps = 2 * HGRN_CHUNKS_PER_STEP
    n_steps = n_chunks // cps

    def body(q_ref, f_ref, i_ref, lb_ref, o_ref, st_out_ref, a_out_ref, st_ref, b_ref, qf_ref, kf_ref, a_ref):
        @pl.when(pl.program_id(0) == 0)
        def _():
            st_ref[...] = jnp.zeros_like(st_ref)

        lbv = _lower_bound(lb_ref[...])
        for u in range(cps):
            rs = slice(u * c_len, (u + 1) * c_len)
            b_u, qf_u, kf_u, a_u = b_ref.at[u], qf_ref.at[u], kf_ref.at[u], a_ref.at[u]
            _, qf, _, log_f, kf = _hgrn_gates(q_ref[rs, :], f_ref[rs, :], lbv)
            b = _tri_sum(_tril_bf16(c_len), log_f)
            b_u[...] = b
            qf_u[...] = qf
            kf_u[...] = kf
            a_u[...] = jnp.zeros_like(a_u)

            def diagonal(rows, j, key, b_u=b_u, qf_u=qf_u, kf_u=kf_u, a_u=a_u):
                bj = b_u[pl.ds(key, 1), :]
                kj = kf_u[pl.ds(key, 1), :]
                nrow = rows.stop - rows.start
                t_loc = lax.broadcasted_iota(jnp.int32, (nrow, nh * hd), 0) + (rows.start % SUB_BLOCK)
                e = jnp.exp(jnp.where(t_loc >= j, b_u[rows, :] - bj, NEG_BIG))
                prod = qf_u[rows, :] * kj * e
                lane = lax.broadcasted_iota(jnp.int32, (nrow, hd), 1)
                for h in range(nh):
                    col = jnp.sum(prod[:, h * hd:(h + 1) * hd], axis=-1, keepdims=True)
                    a_u[h, rows, :] = jnp.where(lane == key, col, a_u[h, rows, :])

            _hgrn_diagonal_loops(c_len, diagonal)
            q_til, k_til, _, _ = _hgrn_off_diagonal(b, qf, kf)
            q_til = q_til.astype(BF16)
            k_til = [k.astype(BF16) for k in k_til]

            b_last = b[c_len - 1:c_len, :]
            qb = (qf * jnp.exp(b)).astype(BF16)
            kb2 = (kf * jnp.exp(b_last - b)).astype(BF16)
            vf = i_ref[rs, :].astype(BF16)
            for h in range(nh):
                hs = slice(h * hd, (h + 1) * hd)
                st = st_ref[h]
                st_out_ref[u, h] = st
                off = [jnp.zeros((SUB_BLOCK, hd), F32)]
                for i in range(1, c_len // SUB_BLOCK):
                    off.append(_dot_nt(q_til[i * SUB_BLOCK:(i + 1) * SUB_BLOCK, hs], k_til[i - 1][:, hs]))
                a_h = a_u[h] + jnp.concatenate(off, axis=0)
                a_out_ref[rs, hs] = a_h
                o_ref[rs, hs] = _dot_nt(qb[:, hs], st.astype(BF16)) + _dot(a_h[:, :c_len].astype(BF16), vf[:, hs])
                st_ref[h] = st * jnp.exp(b_last[:, hs]) + _dot_tn(vf[:, hs], kb2[:, hs])

    blk = (cps * c_len, HGRN_WIDTH)
    sblk = (cps, c_len, HGRN_WIDTH)
    step = lambda k: (lambda: pl.program_id(0) == k)
    e_in, e_out, e_shape, e_scr, e_args = _ride_specs(ride)
    return pl.pallas_call(
        _riding(body, 4, 3, 5, ride, step(0), step((7 * n_steps) // 16), step(n_steps - 1),
                late=step((11 * n_steps) // 16)),
        name="hgrn_fwd",
        grid=(n_steps,),
        in_specs=[
            pl.BlockSpec(blk, lambda c: (c, col0)),
            pl.BlockSpec(blk, lambda c: (c, col0 + 1)),
            pl.BlockSpec(blk, lambda c: (c, col0 + 2)),
            pl.BlockSpec((2, HGRN_WIDTH), lambda c: (0, 0)),
        ] + e_in,
        out_specs=[
            pl.BlockSpec(blk, lambda c: (c, 0)),
            pl.BlockSpec((cps, nh, hd, hd), lambda c: (c, 0, 0, 0)),
            pl.BlockSpec(blk, lambda c: (c, 0)),
        ] + e_out,
        out_shape=[
            jax.ShapeDtypeStruct((s, HGRN_WIDTH), F32),
            jax.ShapeDtypeStruct((n_chunks, nh, hd, hd), F32),
            jax.ShapeDtypeStruct((s, nh * hd), F32),
        ] + e_shape,
        scratch_shapes=[
            pltpu.VMEM((nh, hd, hd), F32),
            pltpu.VMEM(sblk, F32),
            pltpu.VMEM(sblk, F32),
            pltpu.VMEM(sblk, F32),
            pltpu.VMEM((cps, nh, c_len, hd), F32),
        ] + e_scr,
        compiler_params=_params(ride, dimension_semantics=("arbitrary",)),
    )(proj, proj, proj, lb, *e_args)


def hgrn_bwd(proj, lb, d_o, states, a_mat, ride=None):
    s = proj.shape[0]
    c_len, nh, hd = HGRN_CHUNK, HGRN_HEADS, HGRN_HEAD_DIM
    n_chunks = s // c_len
    col0 = 0
    cps = HGRN_CHUNKS_PER_STEP
    n_steps = n_chunks // cps
    last = n_steps - 1

    def body(q_ref, f_ref, i_ref, lb_ref, do_ref, st_in_ref, a_in_ref, dq_ref, df_ref, di_ref, dlb_ref,
             dst_ref, b_ref, qf_ref, kf_ref, da_ref, dqi_ref, dki_ref):
        @pl.when(pl.program_id(0) == 0)
        def _():
            dst_ref[...] = jnp.zeros_like(dst_ref)
            dlb_ref[...] = jnp.zeros_like(dlb_ref)

        lbv = _lower_bound(lb_ref[...])
        for u in reversed(range(cps)):
            rs = slice(u * c_len, (u + 1) * c_len)
            b_u, qf_u, kf_u, da_u, dqi_u, dki_u = (b_ref.at[u], qf_ref.at[u], kf_ref.at[u], da_ref.at[u], dqi_ref.at[u],
                                                   dki_ref.at[u])
            q = q_ref[rs, :]
            sq, qf, (sig, sig_neg, log_c), log_f, kf = _hgrn_gates(q, f_ref[rs, :], lbv)
            b = _tri_sum(_tril_bf16(c_len), log_f)
            b_u[...] = b
            qf_u[...] = qf
            kf_u[...] = kf
            b_last = b[c_len - 1:c_len, :]
            eb = jnp.exp(b)
            ebl = jnp.exp(b_last - b)
            qb = qf * eb
            kb2 = kf * ebl
            vf = i_ref[rs, :]
            d_o = do_ref[rs, :]
            qb_b, kb2_b, vf_b, do_b = qb.astype(BF16), kb2.astype(BF16), vf.astype(BF16), d_o.astype(BF16)
            tq = lax.broadcasted_iota(jnp.int32, (c_len, hd), 0)
            lane = lax.broadcasted_iota(jnp.int32, (c_len, hd), 1)

            dqb_parts, dvf_parts, dkb2_parts, dbl_parts = [], [], [], []
            for h in range(nh):
                hs = slice(h * hd, (h + 1) * hd)
                st = st_in_ref[u, h]
                dst = dst_ref[h]
                st_b, dst_b = st.astype(BF16), dst.astype(BF16)
                a_h = a_in_ref[rs, hs][:, :c_len].astype(BF16)
                dqb_parts.append(_dot(do_b[:, hs], st_b))
                dvf_parts.append(_dot_tn(a_h, do_b[:, hs]) + _dot_nt(kb2_b[:, hs], dst_b))
                dkb2_parts.append(_dot(vf_b[:, hs], dst_b))
                da = _dot_nt(do_b[:, hs], vf_b[:, hs])
                da = jnp.concatenate([da, jnp.zeros((c_len, hd - c_len), F32)], axis=1)
                da_u[h] = jnp.where(tq >= lane, da, 0.0)
                dbl_parts.append(jnp.sum(dst * st, axis=0, keepdims=True) * jnp.exp(b_last[:, hs]))
                dst_ref[h] = dst * jnp.exp(b_last[:, hs]) + _dot_tn(do_b[:, hs], qb_b[:, hs])
            dqb = jnp.concatenate(dqb_parts, axis=1)
            dvf = jnp.concatenate(dvf_parts, axis=1)
            dkb2 = jnp.concatenate(dkb2_parts, axis=1)
            dbl = jnp.concatenate(dbl_parts, axis=1) + jnp.sum(dkb2 * kb2, axis=0, keepdims=True)

            dqi_u[...] = jnp.zeros_like(dqi_u)
            t_idx = lax.broadcasted_iota(jnp.int32, (c_len, nh * hd), 0)

            def diagonal(rows, j, key, b_u=b_u, qf_u=qf_u, kf_u=kf_u, da_u=da_u, dqi_u=dqi_u, dki_u=dki_u):
                bj = b_u[pl.ds(key, 1), :]
                kj = kf_u[pl.ds(key, 1), :]
                nrow = rows.stop - rows.start
                t_loc = lax.broadcasted_iota(jnp.int32, (nrow, nh * hd), 0) + (rows.start % SUB_BLOCK)
                e = jnp.exp(jnp.where(t_loc >= j, b_u[rows, :] - bj, NEG_BIG))
                lane_r = lax.broadcasted_iota(jnp.int32, (nrow, hd), 1)
                cols = [jnp.sum(jnp.where(lane_r == key, da_u[h, rows, :], 0.0), axis=-1, keepdims=True)
                        for h in range(nh)]
                w = e * jnp.concatenate([jnp.broadcast_to(cc, (nrow, hd)) for cc in cols], axis=1)
                dqi_u[rows, :] += w * kj
                dki_u[pl.ds(key, 1), :] = jnp.sum(w * qf_u[rows, :], axis=0, keepdims=True)

            _hgrn_diagonal_loops(c_len, diagonal)

            q_til, k_til, eq, ek = _hgrn_off_diagonal(b, qf, kf)
            q_hi, q_lo = _split2(q_til)
            k_pairs = [_split2(k) for k in k_til]
            n_sub = c_len // SUB_BLOCK
            dq_heads, dk_heads = [], []
            for h in range(nh):
                hs = slice(h * hd, (h + 1) * hd)
                dq_rows = [jnp.zeros((SUB_BLOCK, hd), F32)]
                dk_h = jnp.zeros((c_len, hd), F32)
                for i in range(1, n_sub):
                    rows = slice(i * SUB_BLOCK, (i + 1) * SUB_BLOCK)
                    n = i * SUB_BLOCK
                    da_i = da_u[h, rows, :].astype(BF16)
                    k_hi, k_lo = k_pairs[i - 1]
                    dq_rows.append((_dot(da_i, k_hi[:, hs]) + _dot(da_i, k_lo[:, hs])) * eq[rows, hs])
                    dk_t = (_dot_tn(da_i, q_hi[rows, hs]) + _dot_tn(da_i, q_lo[rows, hs]))[:n, :] * ek[i - 1][:, hs]
                    dk_h = dk_h + jnp.concatenate([dk_t, jnp.zeros((c_len - n, hd), F32)], axis=0)
                dq_heads.append(jnp.concatenate(dq_rows, axis=0))
                dk_heads.append(dk_h)
            dq_intra = dqi_u[...] + jnp.concatenate(dq_heads, axis=1)
            dk_intra = dki_u[...] + jnp.concatenate(dk_heads, axis=1)

            db = dqb * qb + qf * dq_intra - kf * dk_intra - dkb2 * kb2
            db = db + jnp.where(t_idx == c_len - 1, dbl, 0.0)
            dg = _tri_sum(_tril_bf16(c_len, upper=True), db)
            dqf = dqb * eb + dq_intra
            dkf = dkb2 * ebl + dk_intra
            dq_ref[rs, :] = (dqf * (sq * (1.0 + q * (1.0 - sq)))).astype(BF16)
            df_ref[rs, :] = (sig_neg * (dg * jnp.exp(log_c - log_f) - dkf * (1.0 - lbv) * sig)).astype(BF16)
            di_ref[rs, :] = dvf.astype(BF16)
            dlb_ref[...] += jnp.sum(sig_neg * (dg * jnp.exp(-log_f) - dkf), axis=0, keepdims=True)

    blk = (cps * c_len, HGRN_WIDTH)
    sblk = (cps, c_len, HGRN_WIDTH)
    rev = lambda c: last - c
    step = lambda k: (lambda: pl.program_id(0) == k)
    e_in, e_out, e_shape, e_scr, e_args = _ride_specs(ride)
    return pl.pallas_call(
        _riding(body, 7, 4, 7, ride, step(0), step(n_steps // 2), step(last)),
        name="hgrn_bwd",
        grid=(n_steps,),
        in_specs=[
            pl.BlockSpec(blk, lambda c: (rev(c), col0)),
            pl.BlockSpec(blk, lambda c: (rev(c), col0 + 1)),
            pl.BlockSpec(blk, lambda c: (rev(c), col0 + 2)),
            pl.BlockSpec((2, HGRN_WIDTH), lambda c: (0, 0)),
            pl.BlockSpec(blk, lambda c: (rev(c), 0)),
            pl.BlockSpec((cps, nh, hd, hd), lambda c: (rev(c), 0, 0, 0)),
            pl.BlockSpec(blk, lambda c: (rev(c), 0)),
        ] + e_in,
        out_specs=[
            pl.BlockSpec(blk, lambda c: (rev(c), 0)),
            pl.BlockSpec(blk, lambda c: (rev(c), 0)),
            pl.BlockSpec(blk, lambda c: (rev(c), 0)),
            pl.BlockSpec((1, HGRN_WIDTH), lambda c: (0, 0)),
        ] + e_out,
        out_shape=[jax.ShapeDtypeStruct((s, HGRN_WIDTH), BF16)] * 3 + [jax.ShapeDtypeStruct((1, HGRN_WIDTH), F32)] + e_shape,
        scratch_shapes=[
            pltpu.VMEM((nh, hd, hd), F32),
            pltpu.VMEM(sblk, F32),
            pltpu.VMEM(sblk, F32),
            pltpu.VMEM(sblk, F32),
            pltpu.VMEM((cps, nh, c_len, hd), F32),
            pltpu.VMEM(sblk, F32),
            pltpu.VMEM(sblk, F32),
        ] + e_scr,
        compiler_params=_params(ride, dimension_semantics=("arbitrary",)),
    )(proj, proj, proj, lb, d_o, states, a_mat, *e_args)


def _per_head_lanes(x):
    lane = lax.broadcasted_iota(jnp.int32, (x.shape[0], LANES), 1)
    out = jnp.zeros((x.shape[0], LANES), F32)
    for h in range(ATTN_HEADS):
        out = jnp.where(lane == h, x[:, h * ATTN_HEAD_DIM:h * ATTN_HEAD_DIM + 1], out)
    return out


def _row_spec(tm, width, col=0):
    return pl.BlockSpec((tm, width), lambda i: (i, col))


def _const_spec(width):
    return pl.BlockSpec((1, width), lambda i: (0, 0))


def _acc_rows(ref, value):
    @pl.when(pl.program_id(0) == 0)
    def _():
        ref[...] = jnp.zeros_like(ref)

    ref[...] += jnp.sum(value, axis=0, keepdims=True)


def mix_fwd(attn_parts, o_h, proj, an, hn, w_out_b, gp, x, ride=None):
    s = x.shape[0]
    tm = TOKEN_TILE
    gate_col = 3
    hd = HGRN_HEAD_DIM
    nd = len(DILATIONS)

    def body(*refs):
        o_refs, l_refs = refs[:nd], refs[nd:2 * nd]
        oh_ref, gate_ref, an_ref, hn_ref, w_ref, gp_ref, x_ref = refs[2 * nd:2 * nd + 7]
        x1_ref, cat_ref, mixed_ref, attn_ref = refs[2 * nd + 7:2 * nd + 11]
        lse_refs = refs[2 * nd + 11:3 * nd + 11]
        o_scr, l_scr, lse_scr = refs[3 * nd + 11:]
        os_ = [_from_dilated(r, o_scr.at[k], d, tm) for k, (r, d) in enumerate(zip(o_refs, DILATIONS))]
        ls = [_from_dilated(r, l_scr.at[k], d, tm) for k, (r, d) in enumerate(zip(l_refs, DILATIONS))]
        m = jnp.maximum(jnp.maximum(ls[0], ls[1]), ls[2])
        es = [jnp.exp(l - m) for l in ls]
        den = es[0] + es[1] + es[2]
        attn = (es[0] * os_[0] + es[1] * os_[1] + es[2] * os_[2]) / den
        attn_ref[...] = attn
        lse_scr[0] = _per_head_lanes(m + jnp.log(den))
        for d, ref in zip(DILATIONS, lse_refs):
            _to_dilated(lse_scr, ref, d, tm)
        cat_ref[:, :ATTN_WIDTH] = _rms_fwd(attn, an_ref[...], ATTN_WIDTH).astype(BF16)
        gate = gate_ref[...]
        silu_g = gate * _sigmoid(gate)
        for h in range(HGRN_HEADS):
            hs = slice(h * hd, (h + 1) * hd)
            rec = _rms_fwd(oh_ref[:, hs], hn_ref[:, hs], hd) * silu_g[:, hs]
            cat_ref[:, ATTN_WIDTH + h * hd:ATTN_WIDTH + (h + 1) * hd] = rec.astype(BF16)
        mixed = _dot(cat_ref[...], w_ref[...])
        mixed_ref[...] = mixed
        x1_ref[...] = x_ref[...] + _rms_fwd(mixed, gp_ref[...], D_MODEL)

    aw = ATTN_WIDTH
    n_steps = s // tm
    step = lambda k: (lambda: pl.program_id(0) == k)
    e_in, e_out, e_shape, e_scr, e_args = _ride_specs(ride)
    return pl.pallas_call(
        _riding(body, 2 * nd + 7, 4 + nd, 3, ride, step(0), step((13 * n_steps) // 16), step(n_steps - 1)),
        name="mix_fwd",
        grid=(n_steps,),
        in_specs=[_dilated_spec(d, tm, aw) for d in DILATIONS] * 2 + [
            _row_spec(tm, aw), _row_spec(tm, aw, gate_col), _const_spec(aw), _const_spec(aw), _vmem_spec(),
            _const_spec(D_MODEL), _row_spec(tm, D_MODEL)] + e_in,
        out_specs=[_row_spec(tm, D_MODEL), _row_spec(tm, D_MODEL), _row_spec(tm, D_MODEL), _row_spec(tm, aw)] + [
            _dilated_spec(d, tm, LANES) for d in DILATIONS] + e_out,
        out_shape=[
            jax.ShapeDtypeStruct((s, D_MODEL), F32),
            jax.ShapeDtypeStruct((s, D_MODEL), BF16),
            jax.ShapeDtypeStruct((s, D_MODEL), F32),
            jax.ShapeDtypeStruct((s, aw), F32),
        ] + [jax.ShapeDtypeStruct((d, s // d, LANES), F32) for d in DILATIONS] + e_shape,
        scratch_shapes=[pltpu.VMEM((nd, aw // LANES, tm, LANES), F32), pltpu.VMEM((nd, aw // LANES, tm, LANES), F32),
                        pltpu.VMEM((1, tm, LANES), F32)] + e_scr,
        compiler_params=_params(ride, dimension_semantics=("arbitrary",)),
    )(*[p[0] for p in attn_parts], *[p[1] for p in attn_parts], o_h, proj, an, hn, w_out_b, gp, x, *e_args)


def mix_bwd(dx1, mixed, gp, w_out_b, attn, an, o_h, proj, hn):
    s = dx1.shape[0]
    tm = TOKEN_TILE
    gate_col = 3
    hd = HGRN_HEAD_DIM
    aw = ATTN_WIDTH

    nd = len(DILATIONS)

    def body(*refs):
        dx1_ref, mixed_ref, gp_ref, w_ref, attn_ref, an_ref, oh_ref, gate_ref, hn_ref, dmix_ref = refs[:10]
        do_refs, delta_refs = refs[10:10 + nd], refs[10 + nd:10 + 2 * nd]
        doh_ref, dgate_ref, dgp_ref, dan_ref, dhn_ref, do_ref, delta_ref = refs[10 + 2 * nd:]
        dmixed, gp_c = _rms_bwd(dx1_ref[...], mixed_ref[...], gp_ref[...], D_MODEL)
        _acc_rows(dgp_ref, gp_c)
        dmixed_b = dmixed.astype(BF16)
        dmix_ref[...] = dmixed_b
        dcat = _dot_nt(dmixed_b, w_ref[...])
        attn = attn_ref[...]
        d_o, an_c = _rms_bwd(dcat[:, :aw], attn, an_ref[...], aw)
        _acc_rows(dan_ref, an_c)
        _lane_blocks(do_ref, d_o)
        prod = d_o * attn
        lane = lax.broadcasted_iota(jnp.int32, (tm, LANES), 1)
        delta = jnp.zeros((tm, LANES), F32)
        for pair in range(ATTN_HEADS // 2):
            pp = prod[:, pair * LANES:(pair + 1) * LANES]
            low = _lane_half((tm, LANES), 0)
            lo = jnp.sum(jnp.where(low, pp, 0.0), axis=-1, keepdims=True)
            hi = jnp.sum(jnp.where(low, 0.0, pp), axis=-1, keepdims=True)
            delta = jnp.where(lane == 2 * pair, lo, jnp.where(lane == 2 * pair + 1, hi, delta))
        delta_ref[0] = delta
        for d, o_ref, l_ref in zip(DILATIONS, do_refs, delta_refs):
            _to_dilated(do_ref, o_ref, d, tm, cast=BF16)
            _to_dilated(delta_ref, l_ref, d, tm)
        gate = gate_ref[...]
        sg = _sigmoid(gate)
        silu_g = gate * sg
        drec = dcat[:, aw:]
        hn_parts = []
        for h in range(HGRN_HEADS):
            hs = slice(h * hd, (h + 1) * hd)
            oh = oh_ref[:, hs]
            on = _rms_fwd(oh, hn_ref[:, hs], hd)
            dgate_ref[:, hs] = (drec[:, hs] * on * (sg[:, hs] * (1.0 + gate[:, hs] * (1.0 - sg[:, hs])))).astype(BF16)
            d_oh, hn_c = _rms_bwd(drec[:, hs] * silu_g[:, hs], oh, hn_ref[:, hs], hd)
            doh_ref[:, hs] = d_oh
            hn_parts.append(hn_c)
        _acc_rows(dhn_ref, jnp.concatenate(hn_parts, axis=1))

    return pl.pallas_call(
        body,
        name="mix_bwd",
        grid=(s // tm,),
        in_specs=[_row_spec(tm, D_MODEL), _row_spec(tm, D_MODEL), _const_spec(D_MODEL), _vmem_spec(), _row_spec(tm, aw),
                  _const_spec(aw), _row_spec(tm, aw), _row_spec(tm, aw, gate_col), _const_spec(aw)],
        out_specs=[_row_spec(tm, D_MODEL)] + [_dilated_spec(d, tm, aw) for d in DILATIONS] + [
            _dilated_spec(d, tm, LANES) for d in DILATIONS] + [_row_spec(tm, aw)] * 2 + [
            _const_spec(D_MODEL), _const_spec(aw), _const_spec(aw)],
        out_shape=[jax.ShapeDtypeStruct((s, D_MODEL), BF16)] + [
            jax.ShapeDtypeStruct((d, s // d, aw), BF16) for d in DILATIONS] + [
            jax.ShapeDtypeStruct((d, s // d, LANES), F32) for d in DILATIONS] + [
            jax.ShapeDtypeStruct((s, aw), F32), jax.ShapeDtypeStruct((s, aw), BF16),
            jax.ShapeDtypeStruct((1, D_MODEL), F32), jax.ShapeDtypeStruct((1, aw), F32),
            jax.ShapeDtypeStruct((1, aw), F32)],
        scratch_shapes=[pltpu.VMEM((aw // LANES, tm, LANES), F32), pltpu.VMEM((1, tm, LANES), F32)],
        compiler_params=_params(dimension_semantics=("arbitrary",)),
    )(dx1, mixed, gp, w_out_b, attn, an, o_h, proj, hn)


def mlp_fwd_bwd(x1, g_pre, w1_blocks, w2_b, g_post, target):
    s = x1.shape[0]
    tm = MLP_TILE
    nblk, _, fb = w1_blocks.shape

    def body(x1_ref, gpre_ref, w1_ref, w2_ref, gpost_ref, t_ref,
             dx1_ref, h2_ref, a_ref, du_ref, dff_ref, loss_ref, dgpre_ref, dgpost_ref, u_ref):
        x1v = x1_ref[...]
        h2 = _rms_fwd(x1v, gpre_ref[...], D_MODEL).astype(BF16)
        h2_ref[...] = h2
        ff = jnp.zeros((tm, D_MODEL), F32)
        for j in range(nblk):
            cols = slice(j * fb, (j + 1) * fb)
            ru = jnp.maximum(_dot(h2, w1_ref[j]), 0.0)
            u_ref[:, cols] = ru.astype(BF16)
            a = (ru * ru).astype(BF16)
            a_ref[:, cols] = a
            ff = ff + _dot(a, w2_ref[cols, :])
        diff = x1v + _rms_fwd(ff, gpost_ref[...], D_MODEL) - t_ref[...]
        _acc_rows(loss_ref, diff * diff)
        dy = diff * (1.0 / D_MODEL)
        dff, gpost_c = _rms_bwd(dy, ff, gpost_ref[...], D_MODEL)
        _acc_rows(dgpost_ref, gpost_c)
        dff_b = dff.astype(BF16)
        dff_ref[...] = dff_b
        dh2 = jnp.zeros((tm, D_MODEL), F32)
        for j in range(nblk):
            cols = slice(j * fb, (j + 1) * fb)
            du = (_dot_nt(dff_b, w2_ref[cols, :]) * (2.0 * u_ref[:, cols])).astype(BF16)
            du_ref[:, cols] = du
            dh2 = dh2 + _dot_nt(du, w1_ref[j])
        dxa, gpre_c = _rms_bwd(dh2, x1v, gpre_ref[...], D_MODEL)
        _acc_rows(dgpre_ref, gpre_c)
        dx1_ref[...] = dy + dxa

    dm = D_MODEL
    return pl.pallas_call(
        body,
        name="mlp_fwd_bwd",
        grid=(s // tm,),
        in_specs=[_row_spec(tm, dm), _const_spec(dm), _vmem_spec(), _vmem_spec(), _const_spec(dm), _row_spec(tm, dm)],
        out_specs=[_row_spec(tm, dm), _row_spec(tm, dm), _row_spec(tm, D_FF), _row_spec(tm, D_FF), _row_spec(tm, dm),
                   _const_spec(dm), _const_spec(dm), _const_spec(dm)],
        out_shape=[
            jax.ShapeDtypeStruct((s, dm), F32),
            jax.ShapeDtypeStruct((s, dm), BF16),
            jax.ShapeDtypeStruct((s, D_FF), BF16),
            jax.ShapeDtypeStruct((s, D_FF), BF16),
            jax.ShapeDtypeStruct((s, dm), BF16),
            jax.ShapeDtypeStruct((1, dm), F32),
            jax.ShapeDtypeStruct((1, dm), F32),
            jax.ShapeDtypeStruct((1, dm), F32),
        ],
        scratch_shapes=[pltpu.VMEM((tm, D_FF), BF16)],
        compiler_params=_params(dimension_semantics=("arbitrary",)),
    )(x1, g_pre, w1_blocks, w2_b, g_post, target)


def in_proj_bwd(attn_grads, hgrn_grads, dgate, w_in_b, x, g1, dx1):
    s = x.shape[0]
    tm = PROJ_TILE
    aw = ATTN_WIDTH
    n_attn = len(attn_grads)
    flat = [g[k] for k in range(3) for g in attn_grads] + list(hgrn_grads) + [dgate]

    def body(*refs):
        parts = refs[:len(flat)]
        w_ref, x_ref, g_ref, dx1_ref, dx_ref, dproj_ref, dg_ref, scr = refs[len(flat):]
        groups = []
        for k in range(3):
            acc = None
            for p, d in zip(parts[k * n_attn:(k + 1) * n_attn], DILATIONS):
                v = _from_dilated(p, scr, d, tm)
                acc = v if acc is None else acc + v
            groups.append(acc)
        groups += [p[...] for p in parts[3 * n_attn:]]
        dh = jnp.zeros((tm, D_MODEL), F32)
        for gi, grp in enumerate(groups):
            cols = slice(gi * aw, (gi + 1) * aw)
            gb = grp.astype(BF16)
            dproj_ref[:, cols] = gb
            dh = dh + _dot_nt(gb, w_ref[:, cols])
        dxa, g_c = _rms_bwd(dh, x_ref[...], g_ref[...], D_MODEL)
        _acc_rows(dg_ref, g_c)
        dx_ref[...] = dx1_ref[...] + dxa

    dm = D_MODEL
    return pl.pallas_call(
        body,
        name="in_proj_bwd",
        grid=(s // tm,),
        in_specs=[_dilated_spec(d, tm, aw) for d in DILATIONS] * 3 + [_row_spec(tm, aw)] * 4 + [
            _vmem_spec(), _row_spec(tm, dm), _const_spec(dm), _row_spec(tm, dm)],
        out_specs=[_row_spec(tm, dm), _row_spec(tm, IN_PROJ_WIDTH), _const_spec(dm)],
        out_shape=[jax.ShapeDtypeStruct((s, dm), F32), jax.ShapeDtypeStruct((s, IN_PROJ_WIDTH), BF16),
                   jax.ShapeDtypeStruct((1, dm), F32)],
        scratch_shapes=[pltpu.VMEM((aw // LANES, tm, LANES), F32)],
        compiler_params=_params(dimension_semantics=("arbitrary",)),
    )(*flat, w_in_b, x, g1, dx1)


def wgrad(a_b, b_b, tn, name, ts=2048, per_step=1, ride=None):
    s, k = a_b.shape
    n = b_b.shape[1]

    def body(a_ref, b_ref, o_ref):
        @pl.when(pl.program_id(1) == 0)
        def _():
            o_ref[...] = jnp.zeros_like(o_ref)

        a = a_ref[...]
        for jj in range(per_step):
            o_ref[jj] += _dot_tn(a, b_ref[:, jj * tn:(jj + 1) * tn])

    wide = tn * per_step
    gn, gs = n // wide, s // ts
    step = lambda j, i: (lambda: (pl.program_id(0) == j) & (pl.program_id(1) == i))
    e_in, e_out, e_shape, e_scr, e_args = _ride_specs(ride)
    out = pl.pallas_call(
        _riding(body, 2, 1, 0, ride, step(0, 0), step(gn // 2, 0), step(gn - 1, gs - 1)),
        name=name,
        grid=(gn, gs),
        in_specs=[pl.BlockSpec((ts, k), lambda j, i: (i, 0)), pl.BlockSpec((ts, wide), lambda j, i: (i, j))] + e_in,
        out_specs=[pl.BlockSpec((per_step, k, tn), lambda j, i: (j, 0, 0))] + e_out,
        out_shape=[jax.ShapeDtypeStruct((n // tn, k, tn), F32)] + e_shape,
        scratch_shapes=e_scr,
        compiler_params=_params(ride, dimension_semantics=("arbitrary", "arbitrary")),
    )(a_b, b_b, *e_args)
    return out[0] if ride is None else out


def train_step(x, target, g1, an, logits, hn, gp, g_pre, g_post, w, m, v):
    nd = len(DILATIONS)
    shard_b = {k: w[k].astype(BF16) for k in BIG}
    (w_in_g,) = run_exchange(gather_exchange([shard_b["w_in"]]), "gather_w_in")
    w_in_b = w_in_g.transpose(1, 0, 2).reshape(D_MODEL, IN_PROJ_WIDTH)

    proj, h_b, *qkvs, w2_g = in_proj_fwd(x, g1, w_in_b, ride=gather_exchange([shard_b["w_ff2"]]))
    w2_b = w2_g.reshape(D_FF, D_MODEL)
    attn_parts = [attn_fwd(qkv, d) for qkv, d in zip(qkvs, DILATIONS)]
    o_h, states, a_mat, w_out_g, w1_blocks = hgrn_fwd(
        proj, logits, ride=gather_exchange([shard_b["w_out"], shard_b["w_ff1"]]))
    w_out_b = w_out_g.reshape(D_MODEL, D_MODEL)
    x1, cat_b, mixed, attn, *lses = mix_fwd(attn_parts, o_h, proj, an, hn, w_out_b, gp, x)
    dx1, h2_b, a_b, du_b, dff_b, loss_vec, dg_pre, dg_post = mlp_fwd_bwd(x1, g_pre, w1_blocks, w2_b, g_post, target)
    dw2 = wgrad(a_b, dff_b, D_MODEL, "wgrad_ff2", ts=512)
    dw1 = wgrad(h2_b, du_b, D_FF // N_DEV, "wgrad_ff1", per_step=2)
    dmix_b, *rest = mix_bwd(dx1, mixed, gp, w_out_b, attn, an, o_h, proj, hn)
    d_os, deltas = rest[:nd], rest[nd:2 * nd]
    d_oh, dgate, dgp, dan, dhn = rest[2 * nd:]
    dwout = wgrad(cat_b, dmix_b, D_MODEL, "wgrad_out")

    early = ("w_out", "w_ff1", "w_ff2")
    early_grads = [dwout.reshape(N_DEV, D_MODEL // N_DEV, D_MODEL), dw1, dw2.reshape(N_DEV, D_FF // N_DEV, D_MODEL)]
    res = attn_bwd(qkvs[0], d_os[0], lses[0], deltas[0], DILATIONS[0], ride=to_core_exchange(early_grads))
    pairs = [pair_sum(g, s, f"pair_sum_{name}") for g, s, name in zip(early_grads, res[3:], early)]
    attn_grads = [res[:3]]
    *res, others_ff2 = attn_bwd(qkvs[1], d_os[1], lses[1], deltas[1], DILATIONS[1],
                                ride=to_chip_exchange([pairs[2][1]]))
    attn_grads.append(res)
    attn_grads.append(attn_bwd(qkvs[2], d_os[2], lses[2], deltas[2], DILATIONS[2]))
    dq_h, df_h, di_h, dlb, *others = hgrn_bwd(proj, logits, d_oh, states, a_mat,
                                              ride=to_chip_exchange([pairs[0][1], pairs[1][1]]))
    others.append(others_ff2)
    dx, dproj_b, dg1 = in_proj_bwd(attn_grads, (dq_h, df_h, di_h), dgate, w_in_b, x, g1, dx1)
    packed = _pack_small(dg1, dgp, dg_pre, dg_post, dan, dhn, dlb, loss_vec)
    dwin, small_slots = wgrad(h_b, dproj_b, 2 * IN_PROJ_WIDTH // N_DEV, "wgrad_in",
                              ride=small_exchange(packed))
    big = {name: sum_adamw(p[0], o, w[name], m[name], v[name], f"sum_adamw_{name}")
           for name, p, o in zip(early, pairs, others)}

    shard_w = IN_PROJ_WIDTH // N_DEV
    dwin_blocks = dwin.reshape(N_DEV // 2, D_MODEL, 2, shard_w).transpose(0, 2, 1, 3).reshape(N_DEV, D_MODEL, shard_w)
    pair_in, others_in = reduce_last(dwin_blocks)
    big["w_in"] = sum_adamw(pair_in, others_in, w["w_in"], m["w_in"], v["w_in"], "sum_adamw_w_in")
    return dx, big, small_slots


def _position():
    x, y, c = lax.axis_index("x"), lax.axis_index("y"), lax.axis_index("c")
    other_chips = [(1 - x, y), (x, 1 - y), (1 - x, 1 - y)]
    return x, y, c, other_chips


def _any_spec():
    return pl.BlockSpec(memory_space=pl.ANY)


class Exchange:
    def __init__(self, arrays, out_shape, sems, stages, collective_id, peers):
        self.arrays, self.out_shape, self.sems, self.stages = list(arrays), list(out_shape), list(sems), stages
        self.collective_id, self.peers = collective_id, peers

    def open(self):
        barrier = pltpu.get_barrier_semaphore()
        peers = self.peers()
        for peer in peers:
            pl.semaphore_signal(barrier, inc=1, device_id=peer, device_id_type=MESH)
        pl.semaphore_wait(barrier, len(peers))


def _siblings():
    x, y, c, _ = _position()
    return [(x, y, 1 - c)]


def _same_core_of_other_chips():
    x, y, c, chips = _position()
    return [(px, py, c) for px, py in chips]


def _gather_peers():
    x, y, c, _ = _position()
    return [(x, y, 1 - c), (1 - x, y, c), (x, 1 - y, c)]


def _all_others():
    x, y, c, _ = _position()
    return [(1 - x if rel & 4 else x, 1 - y if rel & 2 else y, 1 - c if rel & 1 else c) for rel in range(1, N_DEV)]


def gather_exchange(shards):
    n = len(shards)
    halves = [sh.shape[0] // 2 for sh in shards]

    def stages(ins, outs, sems):
        send_sems, recv_sems, local_sems = sems

        def parts():
            x, y, c, _ = _position()
            me, sibling = (x, y, c), (x, y, 1 - c)
            nbr_x, nbr_y, diag = (1 - x, y, c), (x, 1 - y, c), (1 - x, 1 - y, c)

            def slot(a, dev, rows=None):
                ref = outs[a].at[4 * dev[0] + 2 * dev[1] + dev[2]]
                return ref if rows is None else ref.at[rows]

            def copy(a, k, block, to, rows=None, src=None):
                return pltpu.make_async_remote_copy(
                    src_ref=slot(a, block, rows) if src is None else src, dst_ref=slot(a, block, rows),
                    send_sem=send_sems.at[a, k], recv_sem=recv_sems.at[a, k], device_id=to, device_id_type=MESH)

            upper = lambda a: pl.ds(0, halves[a])
            lower = lambda a: pl.ds(halves[a], halves[a])
            return me, sibling, nbr_x, nbr_y, diag, slot, copy, upper, lower

        def begin():
            me, sibling, nbr_x, nbr_y, _, slot, copy, _, _ = parts()
            for a in range(n):
                pltpu.make_async_copy(ins[a], slot(a, me), local_sems.at[a]).start()
                for k, to in enumerate((sibling, nbr_x, nbr_y)):
                    copy(a, k, me, to, src=ins[a]).start()

        def middle():
            me, sibling, nbr_x, nbr_y, _, _, copy, upper, lower = parts()
            for a in range(n):
                copy(a, 1, nbr_x, me).wait_recv()
                copy(a, 3, nbr_x, sibling).start()
                copy(a, 5, nbr_x, nbr_y, rows=lower(a)).start()
                copy(a, 2, nbr_y, me).wait_recv()
                copy(a, 4, nbr_y, sibling).start()
                copy(a, 6, nbr_y, nbr_x, rows=upper(a)).start()

        def late():
            me, sibling, _, _, diag, _, copy, upper, lower = parts()
            for a in range(n):
                copy(a, 6, diag, me, rows=upper(a)).wait_recv()
                copy(a, 5, diag, me, rows=lower(a)).wait_recv()
                copy(a, 7, diag, sibling).start()

        def end():
            me, sibling, nbr_x, nbr_y, diag, slot, copy, upper, lower = parts()
            sib = lambda dev: (dev[0], dev[1], sibling[2])
            for a in range(n):
                for k, block in ((0, sibling), (3, sib(nbr_x)), (4, sib(nbr_y)), (7, sib(diag))):
                    copy(a, k, block, me).wait_recv()
                copy(a, 0, me, sibling, src=ins[a]).wait_send()
                copy(a, 1, me, nbr_x, src=ins[a]).wait_send()
                copy(a, 2, me, nbr_y, src=ins[a]).wait_send()
                copy(a, 3, nbr_x, sibling).wait_send()
                copy(a, 4, nbr_y, sibling).wait_send()
                copy(a, 5, nbr_x, nbr_y, rows=lower(a)).wait_send()
                copy(a, 6, nbr_y, nbr_x, rows=upper(a)).wait_send()
                copy(a, 7, diag, sibling).wait_send()
                pltpu.make_async_copy(ins[a], slot(a, me), local_sems.at[a]).wait()

        return begin, (middle, late), end

    return Exchange(
        shards, [jax.ShapeDtypeStruct((N_DEV,) + sh.shape, sh.dtype) for sh in shards],
        [pltpu.SemaphoreType.DMA((n, 8)), pltpu.SemaphoreType.DMA((n, 8)), pltpu.SemaphoreType.DMA((n,))], stages,
        collective_id=0, peers=_gather_peers)


def to_core_exchange(grads):
    n = len(grads)

    def stages(ins, outs, sems):
        send_sems, recv_sems = sems

        def copies():
            x, y, c, _ = _position()
            return [pltpu.make_async_remote_copy(
                src_ref=ins[a].at[2 * q + (1 - c)], dst_ref=outs[a].at[q], send_sem=send_sems.at[a, q],
                recv_sem=recv_sems.at[a, q], device_id=(x, y, 1 - c), device_id_type=MESH)
                for a in range(n) for q in range(4)]

        def begin():
            for cp in copies():
                cp.start()

        def end():
            for cp in copies():
                cp.wait()

        return begin, None, end

    return Exchange(grads, [jax.ShapeDtypeStruct((4,) + g.shape[1:], g.dtype) for g in grads],
                    [pltpu.SemaphoreType.DMA((n, 4)), pltpu.SemaphoreType.DMA((n, 4))], stages,
                    collective_id=1, peers=_siblings)


def pair_sum(grad, from_sibling, name):
    _, r, cdim = grad.shape
    tr = min(r, ELEMENTWISE_ROWS)
    c_idx = lax.axis_index("c").astype(jnp.int32).reshape(1)

    def body(c_ref, g_ref, s_ref, o_ref, ob_ref):
        total = g_ref[...] + s_ref[...]
        o_ref[...] = total
        ob_ref[...] = total.astype(BF16)

    blk = lambda: pl.BlockSpec((1, tr, cdim), lambda q, i, cr: (q, i, 0))
    return pl.pallas_call(
        body,
        name=name,
        grid_spec=pltpu.PrefetchScalarGridSpec(
            num_scalar_prefetch=1,
            grid=(4, r // tr),
            in_specs=[pl.BlockSpec((1, tr, cdim), lambda q, i, cr: (2 * q + cr[0], i, 0)), blk()],
            out_specs=[blk(), blk()],
        ),
        out_shape=[jax.ShapeDtypeStruct((4, r, cdim), F32), jax.ShapeDtypeStruct((4, r, cdim), BF16)],
        compiler_params=_params(dimension_semantics=("arbitrary", "arbitrary")),
    )(c_idx, grad, from_sibling)


def to_chip_exchange(pairs):
    n = len(pairs)

    def stages(ins, outs, sems):
        send_sems, recv_sems = sems

        def copies():
            x, y, c, chips = _position()
            return [pltpu.make_async_remote_copy(
                src_ref=ins[a].at[2 * px + py], dst_ref=outs[a].at[j], send_sem=send_sems.at[a, j],
                recv_sem=recv_sems.at[a, j], device_id=(px, py, c), device_id_type=MESH)
                for a in range(n) for j, (px, py) in enumerate(chips)]

        def begin():
            for cp in copies():
                cp.start()

        def end():
            for cp in copies():
                cp.wait()

        return begin, None, end

    return Exchange(pairs, [jax.ShapeDtypeStruct((3,) + p.shape[1:], p.dtype) for p in pairs],
                    [pltpu.SemaphoreType.DMA((n, 3)), pltpu.SemaphoreType.DMA((n, 3))], stages,
                    collective_id=2, peers=_same_core_of_other_chips)


def run_exchange(ex, name):
    n_in, n_out = len(ex.arrays), len(ex.out_shape)

    def body(*refs):
        begin, middle, end = ex.stages(refs[:n_in], refs[n_in:n_in + n_out], refs[n_in + n_out:])
        ex.open()
        begin()
        for stage in _as_tuple(middle):
            stage()
        end()

    return pl.pallas_call(
        body,
        name=name,
        in_specs=[_any_spec()] * n_in,
        out_specs=[_any_spec()] * n_out,
        out_shape=ex.out_shape,
        scratch_shapes=ex.sems,
        compiler_params=pltpu.CompilerParams(collective_id=ex.collective_id),
    )(*ex.arrays)


def _as_tuple(stages):
    return () if stages is None else stages if isinstance(stages, tuple) else (stages,)


def _riding(body, n_in, n_out, n_scratch, ex, first, middle, last, late=None):
    if ex is None:
        return body
    r_in, r_out = len(ex.arrays), len(ex.out_shape)

    def wrapped(*refs):
        k_in, refs = refs[:n_in], refs[n_in:]
        e_in, refs = refs[:r_in], refs[r_in:]
        k_out, refs = refs[:n_out], refs[n_out:]
        e_out, refs = refs[:r_out], refs[r_out:]
        k_scr, e_sems = refs[:n_scratch], refs[n_scratch:]
        begin, mid, end = ex.stages(e_in, e_out, e_sems)

        @pl.when(first())
        def _():
            ex.open()
            begin()

        body(*k_in, *k_out, *k_scr)
        for stage, at in zip(_as_tuple(mid), (middle, late or last)):
            pl.when(at())(stage)
        pl.when(last())(end)

    return wrapped


def _ride_specs(ex):
    if ex is None:
        return [], [], [], [], []
    return [_any_spec()] * len(ex.arrays), [_any_spec()] * len(ex.out_shape), ex.out_shape, ex.sems, ex.arrays


def reduce_last(grad):
    _, r, cdim = grad.shape

    def body(g_hbm, own_ref, others_hbm, g_buf, to_sib, from_sib, send_buf, load_sem, send_sems, recv_sems):
        x, y, c, chips = _position()
        order = chips + [(x, y)]
        n_other = len(chips)
        sibling = (x, y, 1 - c)
        barrier = pltpu.get_barrier_semaphore()
        peers = [sibling] + [(px, py, c) for px, py in chips]
        for peer in peers:
            pl.semaphore_signal(barrier, inc=1, device_id=peer, device_id_type=MESH)
        pl.semaphore_wait(barrier, len(peers))

        def load(block):
            cp = pltpu.make_async_copy(g_hbm.at[block], g_buf, load_sem)
            cp.start()
            cp.wait()
            return g_buf[...]

        def to_sibling(j):
            return pltpu.make_async_remote_copy(
                src_ref=to_sib.at[j], dst_ref=from_sib.at[j], send_sem=send_sems.at[n_other + j],
                recv_sem=recv_sems.at[n_other + j], device_id=sibling, device_id_type=MESH)

        def to_chip(j):
            px, py = chips[j]
            return pltpu.make_async_remote_copy(
                src_ref=send_buf.at[j], dst_ref=others_hbm.at[j], send_sem=send_sems.at[j], recv_sem=recv_sems.at[j],
                device_id=(px, py, c), device_id_type=MESH)

        def hand_over(j):
            px, py = order[j]
            to_sib[j] = load(2 * (2 * px + py) + (1 - c)).astype(BF16)
            to_sibling(j).start()

        def pair_sum_of(j):
            px, py = order[j]
            to_sibling(j).wait_recv()
            total = load(2 * (2 * px + py) + c) + from_sib[j].astype(F32)
            if j < n_other:
                send_buf[j] = total.astype(BF16)
                to_chip(j).start()
            else:
                own_ref[0] = total

        turn = [n_other - 1] + list(range(n_other - 1)) + [n_other]
        hand_over(turn[0])
        for before, j in zip(turn, turn[1:]):
            hand_over(j)
            pair_sum_of(before)
        pair_sum_of(turn[-1])
        for j in range(len(order)):
            to_sibling(j).wait_send()
        for j in range(n_other):
            to_chip(j).wait()

    n_blocks = N_DEV // 2
    return pl.pallas_call(
        body,
        name="reduce_w_in",
        in_specs=[_any_spec()],
        out_specs=[_vmem_spec(), _any_spec()],
        out_shape=[jax.ShapeDtypeStruct((1, r, cdim), F32), jax.ShapeDtypeStruct((n_blocks - 1, r, cdim), BF16)],
        scratch_shapes=[pltpu.VMEM((r, cdim), F32), pltpu.VMEM((n_blocks, r, cdim), BF16),
                        pltpu.VMEM((n_blocks, r, cdim), BF16), pltpu.VMEM((n_blocks - 1, r, cdim), BF16),
                        pltpu.SemaphoreType.DMA(()), pltpu.SemaphoreType.DMA((2 * n_blocks - 1,)),
                        pltpu.SemaphoreType.DMA((2 * n_blocks - 1,))],
        compiler_params=pltpu.CompilerParams(collective_id=4, vmem_limit_bytes=VMEM_LIMIT),
    )(grad)


def _adamw(w, g, m, v):
    m = ADAM_B1 * m + (1.0 - ADAM_B1) * g
    v = ADAM_B2 * v + (1.0 - ADAM_B2) * (g * g)
    m_hat = m / (1.0 - ADAM_B1 ** ADAM_STEP)
    v_hat = v / (1.0 - ADAM_B2 ** ADAM_STEP)
    delta = -ADAM_LR * (m_hat / (jnp.sqrt(v_hat) + ADAM_EPS) + ADAM_WD * w)
    return delta, m, v


def sum_adamw(pairs, others, w, m, v, name):
    r, cdim = w.shape
    tr = min(r, ELEMENTWISE_ROWS // 2)
    if pairs.shape[0] == 1:
        chip_idx = jnp.zeros((1,), jnp.int32)
    else:
        chip_idx = (2 * lax.axis_index("x") + lax.axis_index("y")).astype(jnp.int32).reshape(1)

    def body(q_ref, p_ref, o_ref, w_ref, m_ref, v_ref, g_out, d_out, m_out, v_out):
        g = p_ref[0] + o_ref[0].astype(F32) + o_ref[1].astype(F32) + o_ref[2].astype(F32)
        g_out[...] = g
        d_out[...], m_out[...], v_out[...] = _adamw(w_ref[...], g, m_ref[...], v_ref[...])

    tile = lambda: pl.BlockSpec((tr, cdim), lambda i, qr: (i, 0))
    return pl.pallas_call(
        body,
        name=name,
        grid_spec=pltpu.PrefetchScalarGridSpec(
            num_scalar_prefetch=1,
            grid=(r // tr,),
            in_specs=[pl.BlockSpec((1, tr, cdim), lambda i, qr: (qr[0], i, 0)),
                      pl.BlockSpec((3, tr, cdim), lambda i, qr: (0, i, 0)), tile(), tile(), tile()],
            out_specs=[tile(), tile(), tile(), tile()],
        ),
        out_shape=[jax.ShapeDtypeStruct((r, cdim), F32)] * 4,
        compiler_params=_params(dimension_semantics=("arbitrary",)),
    )(chip_idx, pairs, others, w, m, v)


def small_exchange(packed):
    def stages(ins, outs, sems):
        send_sems, recv_sems, local_sem = sems
        (src,), (slots,) = ins, outs

        def copies():
            x, y, c, _ = _position()
            my_id = 4 * x + 2 * y + c
            sends, landings = [], []
            for rel in range(1, N_DEV):
                px = 1 - x if (rel >> 2) & 1 else x
                py = 1 - y if (rel >> 1) & 1 else y
                pc = 1 - c if rel & 1 else c
                peer = dict(send_sem=send_sems.at[rel - 1], recv_sem=recv_sems.at[rel - 1], device_id=(px, py, pc),
                            device_id_type=MESH)
                sends.append(pltpu.make_async_remote_copy(src_ref=src, dst_ref=slots.at[my_id], **peer))
                landings.append(pltpu.make_async_remote_copy(src_ref=src, dst_ref=slots.at[4 * px + 2 * py + pc], **peer))
            return pltpu.make_async_copy(src, slots.at[my_id], local_sem), sends, landings

        def begin():
            local, sends, _ = copies()
            local.start()
            for cp in sends:
                cp.start()

        def end():
            local, sends, landings = copies()
            for cp in landings:
                cp.wait_recv()
            for cp in sends:
                cp.wait_send()
            local.wait()

        return begin, None, end

    return Exchange([packed], [jax.ShapeDtypeStruct((N_DEV,) + packed.shape, packed.dtype)],
                    [pltpu.SemaphoreType.DMA((N_DEV - 1,)), pltpu.SemaphoreType.DMA((N_DEV - 1,)),
                     pltpu.SemaphoreType.DMA(())], stages, collective_id=3, peers=_all_others)


def small_adamw(slots, w, m, v):
    def body(r_ref, w_ref, m_ref, v_ref, g_out, d_out, m_out, v_out, loss_out):
        red = r_ref[0]
        for k in range(1, N_DEV):
            red = red + r_ref[k]
        wv = w_ref[...]
        lb = _lower_bound(jnp.concatenate([wv[5:6, :HGRN_WIDTH], wv[5:6, HGRN_WIDTH:]], axis=0))
        t = red[5:6, :HGRN_WIDTH] * lb * (1.0 - lb)
        row = lax.broadcasted_iota(jnp.int32, red.shape, 0)
        g = jnp.where(row == 5, jnp.concatenate([t, -t], axis=1), jnp.where(row >= 6, 0.0, red))
        g_out[...] = g
        d_out[...], m_out[...], v_out[...] = _adamw(wv, g, m_ref[...], v_ref[...])
        loss = jnp.sum(red[6:7, :], axis=-1, keepdims=True) * (0.5 / D_MODEL)
        loss_out[...] = jnp.broadcast_to(loss, loss_out.shape)

    return pl.pallas_call(
        body,
        name="small_adamw",
        in_specs=[_vmem_spec()] * 4,
        out_specs=[_vmem_spec()] * 5,
        out_shape=[jax.ShapeDtypeStruct(w.shape, F32)] * 4 + [jax.ShapeDtypeStruct((SUBLANES, LANES), F32)],
    )(slots, w, m, v)


def _pack_small(g1, gp, g_pre, g_post, an, hn, logits_or_dlb, extra=None):
    row5 = logits_or_dlb.reshape(1, -1)
    row5 = jnp.pad(row5, ((0, 0), (0, D_MODEL - row5.shape[1])))
    row6 = jnp.zeros((1, D_MODEL), F32) if extra is None else extra
    return jnp.concatenate([g1, gp, g_pre, g_post, jnp.concatenate([an, hn], axis=1), row5, row6,
                            jnp.zeros((1, D_MODEL), F32)], axis=0)


def _unpack_small(p):
    return dict(mix_pre_norm=p[0:1], mix_post_norm=p[1:2], mlp_pre_norm=p[2:3], mlp_post_norm=p[3:4],
                attn_out_norm=p[4:5, :ATTN_WIDTH], hgrn_out_norm=p[4:5, ATTN_WIDTH:],
                hgrn_lb_logits=p[5].reshape(2, HGRN_WIDTH))


BIG = ("w_in", "w_out", "w_ff1", "w_ff2")
ORDER = ("mix_pre_norm", "w_in", "attn_out_norm", "hgrn_lb_logits", "hgrn_out_norm", "w_out", "mix_post_norm",
         "mlp_pre_norm", "w_ff1", "w_ff2", "mlp_post_norm")


def kernel(x, mix_pre_norm, w_in, attn_out_norm, hgrn_lb_logits, hgrn_out_norm, w_out, mix_post_norm, mlp_pre_norm, w_ff1, w_ff2, mlp_post_norm, loss_target, m_mix_pre_norm, m_w_in, m_attn_out_norm, m_hgrn_lb_logits, m_hgrn_out_norm, m_w_out, m_mix_post_norm, m_mlp_pre_norm, m_w_ff1, m_w_ff2, m_mlp_post_norm, v_mix_pre_norm, v_w_in, v_attn_out_norm, v_hgrn_lb_logits, v_hgrn_out_norm, v_w_out, v_mix_post_norm, v_mlp_pre_norm, v_w_ff1, v_w_ff2, v_mlp_post_norm):
    w = dict(w_in=w_in[0], w_out=w_out[0], w_ff1=w_ff1[0], w_ff2=w_ff2[0])
    m = dict(w_in=m_w_in[0], w_out=m_w_out[0], w_ff1=m_w_ff1[0], w_ff2=m_w_ff2[0])
    v = dict(w_in=v_w_in[0], w_out=v_w_out[0], w_ff1=v_w_ff1[0], w_ff2=v_w_ff2[0])

    dx, big, small_slots = train_step(x[0], loss_target[0], mix_pre_norm, attn_out_norm, hgrn_lb_logits, hgrn_out_norm,
                                      mix_post_norm, mlp_pre_norm, mlp_post_norm, w, m, v)

    pack = lambda a, b, c2, d, e, f, g: _pack_small(a, b, c2, d, e, f, g)
    w_s = pack(mix_pre_norm, mix_post_norm, mlp_pre_norm, mlp_post_norm, attn_out_norm, hgrn_out_norm, hgrn_lb_logits)
    m_s = pack(m_mix_pre_norm, m_mix_post_norm, m_mlp_pre_norm, m_mlp_post_norm, m_attn_out_norm, m_hgrn_out_norm,
               m_hgrn_lb_logits)
    v_s = pack(v_mix_pre_norm, v_mix_post_norm, v_mlp_pre_norm, v_mlp_post_norm, v_attn_out_norm, v_hgrn_out_norm,
               v_hgrn_lb_logits)
    g_s, d_s, nm_s, nv_s, loss = small_adamw(small_slots, w_s, m_s, v_s)
    small_out = [_unpack_small(t) for t in (g_s, d_s, nm_s, nv_s)]

    outs = [loss[0, 0], dx[None]]
    for kind in range(4):
        for name in ORDER:
            outs.append(big[name][kind][None] if name in BIG else small_out[kind][name])
    return tuple(outs)
```
